```python
import jax, jax.numpy as jnp
from jax import lax
import numpy as np

D_MODEL = 1024
BATCH = 8
SEQ = 8192
DEPTH = 1

LRU_WIDTH = D_MODEL
LRU_BLOCKS = 8
LRU_BLOCK_W = LRU_WIDTH // LRU_BLOCKS
LRU_CONV = 4
LRU_C = 8.0
HG_EXPAND = 128
HG_HEADS = D_MODEL // HG_EXPAND
HG_KDIM = HG_HEADS * HG_EXPAND
HG_VHEAD = D_MODEL // HG_HEADS
HG_VDIM = HG_HEADS * HG_VHEAD
HG_CHUNK = 32
D_FF = 2816
FFN_CONV = 3
EPS = 1e-6
IN_SPLITS = (LRU_WIDTH, LRU_WIDTH, HG_KDIM, HG_KDIM, HG_VDIM, HG_VDIM, D_MODEL, D_MODEL)
D_IN = LRU_WIDTH * 2 + HG_KDIM * 2 + HG_VDIM * 2 + D_MODEL * 2

kernel_name = "hybrid_rglru_hgrn2_convffn"


def rmsnorm(x, g):
    xf = x.astype(jnp.float32)
    y = xf * lax.rsqrt(jnp.mean(xf * xf, axis=-1, keepdims=True) + EPS)
    return (y * g.astype(jnp.float32)).astype(x.dtype)


def causal_dwconv(x, w, b):
    k = w.shape[0]
    y = lax.conv_general_dilated(
        x, w[:, None, :].astype(x.dtype), window_strides=(1,), padding=[(k - 1, 0)],
        dimension_numbers=("NWC", "WIO", "NWC"), feature_group_count=x.shape[-1])
    return y + b.astype(x.dtype)


def rg_lru(x, w_a, b_a, w_x, b_x, lam):
    bsz, t, _ = x.shape
    xb = x.astype(jnp.float32).reshape(bsz, t, LRU_BLOCKS, LRU_BLOCK_W)
    r = jax.nn.sigmoid(jnp.einsum("btnd,nde->btne", xb, w_a.astype(jnp.float32))
                       + b_a.astype(jnp.float32).reshape(LRU_BLOCKS, LRU_BLOCK_W))
    i = jax.nn.sigmoid(jnp.einsum("btnd,nde->btne", xb, w_x.astype(jnp.float32))
                       + b_x.astype(jnp.float32).reshape(LRU_BLOCKS, LRU_BLOCK_W))
    log_a = -LRU_C * r * jax.nn.softplus(-lam.astype(jnp.float32)).reshape(LRU_BLOCKS, LRU_BLOCK_W)
    a = jnp.exp(log_a).reshape(bsz, t, LRU_WIDTH)
    u = (jnp.sqrt(-jnp.expm1(2.0 * log_a)) * i * xb).reshape(bsz, t, LRU_WIDTH)

    def combine(left, right):
        a_l, h_l = left
        a_r, h_r = right
        return a_l * a_r, a_r * h_l + h_r

    _, h = lax.associative_scan(combine, (a, u), axis=1)
    return h


def hgrn2(q, fz, v, lb):
    bsz, t, _ = q.shape
    n = t // HG_CHUNK
    q = q.astype(jnp.float32)
    fz = fz.astype(jnp.float32)
    v = v.astype(jnp.float32)
    f = lb + (1.0 - lb) * jax.nn.sigmoid(fz)
    log_f = jnp.log(f)
    k = (1.0 - lb) * jax.nn.sigmoid(-fz)

    def chunks(z, d):
        return z.reshape(bsz, n, HG_CHUNK, HG_HEADS, d).transpose(0, 1, 3, 2, 4)

    qc, kc, gc = chunks(q, HG_EXPAND), chunks(k, HG_EXPAND), chunks(log_f, HG_EXPAND)
    vc = chunks(v, HG_VHEAD)
    g_cum = jnp.cumsum(gc, axis=3)
    g_last = g_cum[:, :, :, -1:, :]
    q_t = qc * jnp.exp(g_cum)
    k_t = kc * jnp.exp(-g_cum)
    k_dec = kc * jnp.exp(g_last - g_cum)
    dec = jnp.exp(g_last[:, :, :, 0, :])

    causal = jnp.tril(jnp.ones((HG_CHUNK, HG_CHUNK), dtype=bool))
    att = jnp.where(causal, jnp.einsum("bnhcd,bnhsd->bnhcs", q_t, k_t), 0.0)
    o_intra = jnp.einsum("bnhcs,bnhse->bnhce", att, vc)

    def step(s, xs):
        q_c, kd_c, v_c, d_c = xs
        o = jnp.einsum("bhcd,bhde->bhce", q_c, s)
        s = s * d_c[..., None] + jnp.einsum("bhcd,bhce->bhde", kd_c, v_c)
        return s, o

    s0 = jnp.zeros((bsz, HG_HEADS, HG_EXPAND, HG_VHEAD), jnp.float32)
    _, o_inter = lax.scan(step, s0, (jnp.moveaxis(q_t, 1, 0), jnp.moveaxis(k_dec, 1, 0),
                                     jnp.moveaxis(vc, 1, 0), jnp.moveaxis(dec, 1, 0)))
    o = o_intra + jnp.moveaxis(o_inter, 0, 1)
    return o.transpose(0, 1, 3, 2, 4).reshape(bsz, t, HG_HEADS, HG_VHEAD)


def _fwd_setup_inputs(seed: int = 0) -> dict:
    key = jax.random.key(seed)
    ks = jax.random.split(key, 24)
    f32 = jnp.float32

    def nrm(k, shape, fan_in):
        return jax.random.normal(k, shape, f32) * (fan_in ** -0.5)

    def gain(k, shape):
        return 1.0 + 0.02 * jax.random.normal(k, shape, f32)

    def bias(k, shape):
        return 0.02 * jax.random.normal(k, shape, f32)

    u = jax.random.uniform(ks[9], (DEPTH, LRU_WIDTH), f32, 0.9, 0.999)
    s = u ** (1.0 / LRU_C)
    lru_lambda = jnp.log(s) - jnp.log1p(-s)
    return {
        "x": jax.random.normal(ks[0], (BATCH, SEQ, D_MODEL), f32),
        "norm_pre_mix": gain(ks[1], (DEPTH, D_MODEL)),
        "w_in": nrm(ks[2], (DEPTH, D_MODEL, D_IN), D_MODEL),
        "conv_a_w": nrm(ks[3], (DEPTH, LRU_CONV, LRU_WIDTH), LRU_CONV),
        "conv_a_b": bias(ks[4], (DEPTH, LRU_WIDTH)),
        "lru_wa": nrm(ks[5], (DEPTH, LRU_BLOCKS, LRU_BLOCK_W, LRU_BLOCK_W), LRU_BLOCK_W),
        "lru_ba": bias(ks[6], (DEPTH, LRU_WIDTH)),
        "lru_wx": nrm(ks[7], (DEPTH, LRU_BLOCKS, LRU_BLOCK_W, LRU_BLOCK_W), LRU_BLOCK_W),
        "lru_bx": bias(ks[8], (DEPTH, LRU_WIDTH)),
        "lru_lambda": lru_lambda,
        "hg_lb_logits": 0.1 * jax.random.normal(ks[10], (DEPTH + 1, HG_KDIM), f32),
        "hg_norm_g": gain(ks[11], (DEPTH, HG_VDIM)),
        "w_branch_a": nrm(ks[12], (DEPTH, LRU_WIDTH, D_MODEL), LRU_WIDTH),
        "w_branch_b": nrm(ks[13], (DEPTH, HG_VDIM, D_MODEL), HG_VDIM),
        "w_out": nrm(ks[14], (DEPTH, D_MODEL, D_MODEL), D_MODEL),
        "norm_post_mix": gain(ks[15], (DEPTH, D_MODEL)),
        "norm_pre_ffn": gain(ks[16], (DEPTH, D_MODEL)),
        "w_up": nrm(ks[17], (DEPTH, D_MODEL, 2 * D_FF), D_MODEL),
        "conv_f_w": nrm(ks[18], (DEPTH, FFN_CONV, 2 * D_FF), FFN_CONV),
        "conv_f_b": bias(ks[19], (DEPTH, 2 * D_FF)),
        "w_down": nrm(ks[20], (DEPTH, D_FF, D_MODEL), D_FF),
        "norm_post_ffn": gain(ks[21], (DEPTH, D_MODEL)),
    }


def _fwd_reference(x, norm_pre_mix, w_in, conv_a_w, conv_a_b, lru_wa, lru_ba, lru_wx, lru_bx,
              lru_lambda, hg_lb_logits, hg_norm_g, w_branch_a, w_branch_b, w_out,
              norm_post_mix, norm_pre_ffn, w_up, conv_f_w, conv_f_b, w_down, norm_post_ffn):
    bsz, t, _ = x.shape
    lb_all = jnp.cumsum(jax.nn.softmax(hg_lb_logits.astype(jnp.float32), axis=0), axis=0)
    split_idx = [int(v) for v in np.cumsum(IN_SPLITS)[:-1]]
    for l in range(DEPTH):
        h = rmsnorm(x, norm_pre_mix[l])
        p = h @ w_in[l].astype(h.dtype)
        xa, ga, q, fz, vi, og, gate_a, gate_b = jnp.split(p, split_idx, axis=-1)

        xa = causal_dwconv(xa, conv_a_w[l], conv_a_b[l])
        ya = rg_lru(xa, lru_wa[l], lru_ba[l], lru_wx[l], lru_bx[l], lru_lambda[l])
        ya = (ya * jax.nn.gelu(ga.astype(jnp.float32), approximate=True)).astype(x.dtype)

        ob = hgrn2(q, fz, vi, lb_all[l])
        ob = ob * lax.rsqrt(jnp.mean(ob * ob, axis=-1, keepdims=True) + EPS)
        ob = ob * hg_norm_g[l].astype(jnp.float32).reshape(HG_HEADS, HG_VHEAD)
        yb = (ob.reshape(bsz, t, HG_VDIM) * jax.nn.silu(og.astype(jnp.float32))).astype(x.dtype)

        za = ya @ w_branch_a[l].astype(ya.dtype)
        zb = yb @ w_branch_b[l].astype(yb.dtype)
        mix = jax.nn.sigmoid(gate_a) * za + jax.nn.sigmoid(gate_b) * zb
        x = x + rmsnorm(mix @ w_out[l].astype(mix.dtype), norm_post_mix[l])

        h2 = rmsnorm(x, norm_pre_ffn[l])
        up = causal_dwconv(h2 @ w_up[l].astype(h2.dtype), conv_f_w[l], conv_f_b[l])
        u_gate, u_val = jnp.split(up, 2, axis=-1)
        y = jax.nn.gelu(u_gate, approximate=True) * u_val
        x = x + rmsnorm(y @ w_down[l].astype(y.dtype), norm_post_ffn[l])
    return x


import jax as _jax
import jax.numpy as _jnp

TWIN_FORMAT = 'train_step'
FWD_PARAMS = ['x', 'norm_pre_mix', 'w_in', 'conv_a_w', 'conv_a_b', 'lru_wa', 'lru_ba', 'lru_wx', 'lru_bx', 'lru_lambda', 'hg_lb_logits', 'hg_norm_g', 'w_branch_a', 'w_branch_b', 'w_out', 'norm_post_mix', 'norm_pre_ffn', 'w_up', 'conv_f_w', 'conv_f_b', 'w_down', 'norm_post_ffn']
TWIN_WEIGHTS = ['norm_pre_mix', 'w_in', 'conv_a_w', 'conv_a_b', 'lru_wa', 'lru_ba', 'lru_wx', 'lru_bx', 'lru_lambda', 'hg_lb_logits', 'hg_norm_g', 'w_branch_a', 'w_branch_b', 'w_out', 'norm_post_mix', 'norm_pre_ffn', 'w_up', 'conv_f_w', 'conv_f_b', 'w_down', 'norm_post_ffn']
TWIN_DIFF_INPUT = 'x'
TWIN_INPUTS = ['x', 'norm_pre_mix', 'w_in', 'conv_a_w', 'conv_a_b', 'lru_wa', 'lru_ba', 'lru_wx', 'lru_bx', 'lru_lambda', 'hg_lb_logits', 'hg_norm_g', 'w_branch_a', 'w_branch_b', 'w_out', 'norm_post_mix', 'norm_pre_ffn', 'w_up', 'conv_f_w', 'conv_f_b', 'w_down', 'norm_post_ffn', 'loss_target', 'm_norm_pre_mix', 'm_w_in', 'm_conv_a_w', 'm_conv_a_b', 'm_lru_wa', 'm_lru_ba', 'm_lru_wx', 'm_lru_bx', 'm_lru_lambda', 'm_hg_lb_logits', 'm_hg_norm_g', 'm_w_branch_a', 'm_w_branch_b', 'm_w_out', 'm_norm_post_mix', 'm_norm_pre_ffn', 'm_w_up', 'm_conv_f_w', 'm_conv_f_b', 'm_w_down', 'm_norm_post_ffn', 'v_norm_pre_mix', 'v_w_in', 'v_conv_a_w', 'v_conv_a_b', 'v_lru_wa', 'v_lru_ba', 'v_lru_wx', 'v_lru_bx', 'v_lru_lambda', 'v_hg_lb_logits', 'v_hg_norm_g', 'v_w_branch_a', 'v_w_branch_b', 'v_w_out', 'v_norm_post_mix', 'v_norm_pre_ffn', 'v_w_up', 'v_conv_f_w', 'v_conv_f_b', 'v_w_down', 'v_norm_post_ffn']
TWIN_OUTPUTS = ['loss', 'grad_x', 'grad_norm_pre_mix', 'grad_w_in', 'grad_conv_a_w', 'grad_conv_a_b', 'grad_lru_wa', 'grad_lru_ba', 'grad_lru_wx', 'grad_lru_bx', 'grad_lru_lambda', 'grad_hg_lb_logits', 'grad_hg_norm_g', 'grad_w_branch_a', 'grad_w_branch_b', 'grad_w_out', 'grad_norm_post_mix', 'grad_norm_pre_ffn', 'grad_w_up', 'grad_conv_f_w', 'grad_conv_f_b', 'grad_w_down', 'grad_norm_post_ffn', 'delta_norm_pre_mix', 'delta_w_in', 'delta_conv_a_w', 'delta_conv_a_b', 'delta_lru_wa', 'delta_lru_ba', 'delta_lru_wx', 'delta_lru_bx', 'delta_lru_lambda', 'delta_hg_lb_logits', 'delta_hg_norm_g', 'delta_w_branch_a', 'delta_w_branch_b', 'delta_w_out', 'delta_norm_post_mix', 'delta_norm_pre_ffn', 'delta_w_up', 'delta_conv_f_w', 'delta_conv_f_b', 'delta_w_down', 'delta_norm_post_ffn', 'new_m_norm_pre_mix', 'new_m_w_in', 'new_m_conv_a_w', 'new_m_conv_a_b', 'new_m_lru_wa', 'new_m_lru_ba', 'new_m_lru_wx', 'new_m_lru_bx', 'new_m_lru_lambda', 'new_m_hg_lb_logits', 'new_m_hg_norm_g', 'new_m_w_branch_a', 'new_m_w_branch_b', 'new_m_w_out', 'new_m_norm_post_mix', 'new_m_norm_pre_ffn', 'new_m_w_up', 'new_m_conv_f_w', 'new_m_conv_f_b', 'new_m_w_down', 'new_m_norm_post_ffn', 'new_v_norm_pre_mix', 'new_v_w_in', 'new_v_conv_a_w', 'new_v_conv_a_b', 'new_v_lru_wa', 'new_v_lru_ba', 'new_v_lru_wx', 'new_v_lru_bx', 'new_v_lru_lambda', 'new_v_hg_lb_logits', 'new_v_hg_norm_g', 'new_v_w_branch_a', 'new_v_w_branch_b', 'new_v_w_out', 'new_v_norm_post_mix', 'new_v_norm_pre_ffn', 'new_v_w_up', 'new_v_conv_f_w', 'new_v_conv_f_b', 'new_v_w_down', 'new_v_norm_post_ffn']
TWIN_LEAF_KINDS = {'loss': 'loss', 'grad_x': 'grad_x', 'grad_norm_pre_mix': 'grad_w', 'grad_w_in': 'grad_w', 'grad_conv_a_w': 'grad_w', 'grad_conv_a_b': 'grad_w', 'grad_lru_wa': 'grad_w', 'grad_lru_ba': 'grad_w', 'grad_lru_wx': 'grad_w', 'grad_lru_bx': 'grad_w', 'grad_lru_lambda': 'grad_w', 'grad_hg_lb_logits': 'grad_w', 'grad_hg_norm_g': 'grad_w', 'grad_w_branch_a': 'grad_w', 'grad_w_branch_b': 'grad_w', 'grad_w_out': 'grad_w', 'grad_norm_post_mix': 'grad_w', 'grad_norm_pre_ffn': 'grad_w', 'grad_w_up': 'grad_w', 'grad_conv_f_w': 'grad_w', 'grad_conv_f_b': 'grad_w', 'grad_w_down': 'grad_w', 'grad_norm_post_ffn': 'grad_w', 'delta_norm_pre_mix': 'delta_w', 'delta_w_in': 'delta_w', 'delta_conv_a_w': 'delta_w', 'delta_conv_a_b': 'delta_w', 'delta_lru_wa': 'delta_w', 'delta_lru_ba': 'delta_w', 'delta_lru_wx': 'delta_w', 'delta_lru_bx': 'delta_w', 'delta_lru_lambda': 'delta_w', 'delta_hg_lb_logits': 'delta_w', 'delta_hg_norm_g': 'delta_w', 'delta_w_branch_a': 'delta_w', 'delta_w_branch_b': 'delta_w', 'delta_w_out': 'delta_w', 'delta_norm_post_mix': 'delta_w', 'delta_norm_pre_ffn': 'delta_w', 'delta_w_up': 'delta_w', 'delta_conv_f_w': 'delta_w', 'delta_conv_f_b': 'delta_w', 'delta_w_down': 'delta_w', 'delta_norm_post_ffn': 'delta_w', 'new_m_norm_pre_mix': 'new_m', 'new_m_w_in': 'new_m', 'new_m_conv_a_w': 'new_m', 'new_m_conv_a_b': 'new_m', 'new_m_lru_wa': 'new_m', 'new_m_lru_ba': 'new_m', 'new_m_lru_wx': 'new_m', 'new_m_lru_bx': 'new_m', 'new_m_lru_lambda': 'new_m', 'new_m_hg_lb_logits': 'new_m', 'new_m_hg_norm_g': 'new_m', 'new_m_w_branch_a': 'new_m', 'new_m_w_branch_b': 'new_m', 'new_m_w_out': 'new_m', 'new_m_norm_post_mix': 'new_m', 'new_m_norm_pre_ffn': 'new_m', 'new_m_w_up': 'new_m', 'new_m_conv_f_w': 'new_m', 'new_m_conv_f_b': 'new_m', 'new_m_w_down': 'new_m', 'new_m_norm_post_ffn': 'new_m', 'new_v_norm_pre_mix': 'new_v', 'new_v_w_in': 'new_v', 'new_v_conv_a_w': 'new_v', 'new_v_conv_a_b': 'new_v', 'new_v_lru_wa': 'new_v', 'new_v_lru_ba': 'new_v', 'new_v_lru_wx': 'new_v', 'new_v_lru_bx': 'new_v', 'new_v_lru_lambda': 'new_v', 'new_v_hg_lb_logits': 'new_v', 'new_v_hg_norm_g': 'new_v', 'new_v_w_branch_a': 'new_v', 'new_v_w_branch_b': 'new_v', 'new_v_w_out': 'new_v', 'new_v_norm_post_mix': 'new_v', 'new_v_norm_pre_ffn': 'new_v', 'new_v_w_up': 'new_v', 'new_v_conv_f_w': 'new_v', 'new_v_conv_f_b': 'new_v', 'new_v_w_down': 'new_v', 'new_v_norm_post_ffn': 'new_v'}


def _forward(args):
    return _fwd_reference(*[args[k] for k in FWD_PARAMS])


def _output_shape():
    def fwd():
        inp = _fwd_setup_inputs(0)
        return _fwd_reference(*[inp[k] for k in FWD_PARAMS])
    out = _jax.eval_shape(fwd)
    return out.shape, out.dtype

N_MICROBATCH = 1
ADAM_LR = 0.001
ADAM_B1 = 0.9
ADAM_B2 = 0.999
ADAM_EPS = 1e-08
ADAM_WD = 0.01
ADAM_STEP = 10
PER_EXAMPLE_BATCH_AXIS = {'x': 0, 'loss_target': 0}
SHARED_INPUTS = []
_WEIGHT_DTYPES = {'norm_pre_mix': _jnp.float32, 'w_in': _jnp.float32, 'conv_a_w': _jnp.float32, 'conv_a_b': _jnp.float32, 'lru_wa': _jnp.float32, 'lru_ba': _jnp.float32, 'lru_wx': _jnp.float32, 'lru_bx': _jnp.float32, 'lru_lambda': _jnp.float32, 'hg_lb_logits': _jnp.float32, 'hg_norm_g': _jnp.float32, 'w_branch_a': _jnp.float32, 'w_branch_b': _jnp.float32, 'w_out': _jnp.float32, 'norm_post_mix': _jnp.float32, 'norm_pre_ffn': _jnp.float32, 'w_up': _jnp.float32, 'conv_f_w': _jnp.float32, 'conv_f_b': _jnp.float32, 'w_down': _jnp.float32, 'norm_post_ffn': _jnp.float32}
MOMENT_SCALE = {'norm_pre_mix': 1.358638e+00, 'w_in': 4.536254e-01, 'conv_a_w': 4.416979e-01, 'conv_a_b': 9.283893e+00, 'lru_wa': 2.127927e-01, 'lru_ba': 1.439561e-01, 'lru_wx': 3.877484e-01, 'lru_bx': 1.457813e-01, 'lru_lambda': 2.452219e-01, 'hg_lb_logits': 3.534929e-01, 'hg_norm_g': 5.663968e-01, 'w_branch_a': 5.081093e-01, 'w_branch_b': 5.381900e-01, 'w_out': 7.715119e-01, 'norm_post_mix': 6.401728e+01, 'norm_pre_ffn': 7.468866e-01, 'w_up': 3.463595e-01, 'conv_f_w': 4.073798e-01, 'conv_f_b': 1.327800e+00, 'w_down': 6.938483e-01, 'norm_post_ffn': 6.417953e+01}


def _to_microbatches(a, axis):
    t = _jnp.moveaxis(a, axis, 0)
    t = t.reshape((N_MICROBATCH, t.shape[0] // N_MICROBATCH) + t.shape[1:])
    return _jnp.moveaxis(t, 1, axis + 1)


def setup_inputs(seed: int = 0) -> dict:
    inp = _fwd_setup_inputs(seed)
    key = _jax.random.fold_in(_jax.random.key(seed), 7919)
    shape, _ = _output_shape()
    out = dict(inp)
    out["loss_target"] = _jax.random.normal(_jax.random.fold_in(key, 0), shape, _jnp.float32)
    for i, name in enumerate(TWIN_WEIGHTS):
        w = inp[name].astype(_jnp.float32)
        if MOMENT_SCALE is None:
            s = _jnp.sqrt(_jnp.mean(_jnp.square(w)) + 1e-30)
        else:
            s = MOMENT_SCALE[name]
        km, kv = _jax.random.split(_jax.random.fold_in(key, i + 1))
        out[name] = w
        out["m_" + name] = s * _jax.random.normal(km, w.shape, _jnp.float32)
        out["v_" + name] = (s * s) * _jax.random.uniform(kv, w.shape, _jnp.float32, 0.5, 1.5)
    if N_MICROBATCH > 1:
        for name, axis in PER_EXAMPLE_BATCH_AXIS.items():
            out[name] = _to_microbatches(out[name], axis)
    return {'x': out['x'], 'norm_pre_mix': out['norm_pre_mix'], 'w_in': out['w_in'], 'conv_a_w': out['conv_a_w'], 'conv_a_b': out['conv_a_b'], 'lru_wa': out['lru_wa'], 'lru_ba': out['lru_ba'], 'lru_wx': out['lru_wx'], 'lru_bx': out['lru_bx'], 'lru_lambda': out['lru_lambda'], 'hg_lb_logits': out['hg_lb_logits'], 'hg_norm_g': out['hg_norm_g'], 'w_branch_a': out['w_branch_a'], 'w_branch_b': out['w_branch_b'], 'w_out': out['w_out'], 'norm_post_mix': out['norm_post_mix'], 'norm_pre_ffn': out['norm_pre_ffn'], 'w_up': out['w_up'], 'conv_f_w': out['conv_f_w'], 'conv_f_b': out['conv_f_b'], 'w_down': out['w_down'], 'norm_post_ffn': out['norm_post_ffn'], 'loss_target': out['loss_target'], 'm_norm_pre_mix': out['m_norm_pre_mix'], 'm_w_in': out['m_w_in'], 'm_conv_a_w': out['m_conv_a_w'], 'm_conv_a_b': out['m_conv_a_b'], 'm_lru_wa': out['m_lru_wa'], 'm_lru_ba': out['m_lru_ba'], 'm_lru_wx': out['m_lru_wx'], 'm_lru_bx': out['m_lru_bx'], 'm_lru_lambda': out['m_lru_lambda'], 'm_hg_lb_logits': out['m_hg_lb_logits'], 'm_hg_norm_g': out['m_hg_norm_g'], 'm_w_branch_a': out['m_w_branch_a'], 'm_w_branch_b': out['m_w_branch_b'], 'm_w_out': out['m_w_out'], 'm_norm_post_mix': out['m_norm_post_mix'], 'm_norm_pre_ffn': out['m_norm_pre_ffn'], 'm_w_up': out['m_w_up'], 'm_conv_f_w': out['m_conv_f_w'], 'm_conv_f_b': out['m_conv_f_b'], 'm_w_down': out['m_w_down'], 'm_norm_post_ffn': out['m_norm_post_ffn'], 'v_norm_pre_mix': out['v_norm_pre_mix'], 'v_w_in': out['v_w_in'], 'v_conv_a_w': out['v_conv_a_w'], 'v_conv_a_b': out['v_conv_a_b'], 'v_lru_wa': out['v_lru_wa'], 'v_lru_ba': out['v_lru_ba'], 'v_lru_wx': out['v_lru_wx'], 'v_lru_bx': out['v_lru_bx'], 'v_lru_lambda': out['v_lru_lambda'], 'v_hg_lb_logits': out['v_hg_lb_logits'], 'v_hg_norm_g': out['v_hg_norm_g'], 'v_w_branch_a': out['v_w_branch_a'], 'v_w_branch_b': out['v_w_branch_b'], 'v_w_out': out['v_w_out'], 'v_norm_post_mix': out['v_norm_post_mix'], 'v_norm_pre_ffn': out['v_norm_pre_ffn'], 'v_w_up': out['v_w_up'], 'v_conv_f_w': out['v_conv_f_w'], 'v_conv_f_b': out['v_conv_f_b'], 'v_w_down': out['v_w_down'], 'v_norm_post_ffn': out['v_norm_post_ffn']}


def _loss(weights, diff, rest, loss_target):
    with _jax.named_scope("forward"):
        args = {**rest, TWIN_DIFF_INPUT: diff, **{k: w.astype(_WEIGHT_DTYPES[k]) for k, w in weights.items()}}
        y = _forward(args)
    with _jax.named_scope("loss_head"):
        err = _jnp.square(y.astype(_jnp.float32) - loss_target)
        return 0.5 * _jnp.sum(_jnp.mean(err, axis=-1)) if err.ndim else 0.5 * err


def _adamw(w, g, m, v):
    m = ADAM_B1 * m + (1.0 - ADAM_B1) * g
    v = ADAM_B2 * v + (1.0 - ADAM_B2) * _jnp.square(g)
    m_hat = m / (1.0 - ADAM_B1 ** ADAM_STEP)
    v_hat = v / (1.0 - ADAM_B2 ** ADAM_STEP)
    delta = -ADAM_LR * (m_hat / (_jnp.sqrt(v_hat) + ADAM_EPS) + ADAM_WD * w)
    return delta, m, v


def reference(x, norm_pre_mix, w_in, conv_a_w, conv_a_b, lru_wa, lru_ba, lru_wx, lru_bx, lru_lambda, hg_lb_logits, hg_norm_g, w_branch_a, w_branch_b, w_out, norm_post_mix, norm_pre_ffn, w_up, conv_f_w, conv_f_b, w_down, norm_post_ffn, loss_target, m_norm_pre_mix, m_w_in, m_conv_a_w, m_conv_a_b, m_lru_wa, m_lru_ba, m_lru_wx, m_lru_bx, m_lru_lambda, m_hg_lb_logits, m_hg_norm_g, m_w_branch_a, m_w_branch_b, m_w_out, m_norm_post_mix, m_norm_pre_ffn, m_w_up, m_conv_f_w, m_conv_f_b, m_w_down, m_norm_post_ffn, v_norm_pre_mix, v_w_in, v_conv_a_w, v_conv_a_b, v_lru_wa, v_lru_ba, v_lru_wx, v_lru_bx, v_lru_lambda, v_hg_lb_logits, v_hg_norm_g, v_w_branch_a, v_w_branch_b, v_w_out, v_norm_post_mix, v_norm_pre_ffn, v_w_up, v_conv_f_w, v_conv_f_b, v_w_down, v_norm_post_ffn):
    given = dict(x=x, norm_pre_mix=norm_pre_mix, w_in=w_in, conv_a_w=conv_a_w, conv_a_b=conv_a_b, lru_wa=lru_wa, lru_ba=lru_ba, lru_wx=lru_wx, lru_bx=lru_bx, lru_lambda=lru_lambda, hg_lb_logits=hg_lb_logits, hg_norm_g=hg_norm_g, w_branch_a=w_branch_a, w_branch_b=w_branch_b, w_out=w_out, norm_post_mix=norm_post_mix, norm_pre_ffn=norm_pre_ffn, w_up=w_up, conv_f_w=conv_f_w, conv_f_b=conv_f_b, w_down=w_down, norm_post_ffn=norm_post_ffn, loss_target=loss_target, m_norm_pre_mix=m_norm_pre_mix, m_w_in=m_w_in, m_conv_a_w=m_conv_a_w, m_conv_a_b=m_conv_a_b, m_lru_wa=m_lru_wa, m_lru_ba=m_lru_ba, m_lru_wx=m_lru_wx, m_lru_bx=m_lru_bx, m_lru_lambda=m_lru_lambda, m_hg_lb_logits=m_hg_lb_logits, m_hg_norm_g=m_hg_norm_g, m_w_branch_a=m_w_branch_a, m_w_branch_b=m_w_branch_b, m_w_out=m_w_out, m_norm_post_mix=m_norm_post_mix, m_norm_pre_ffn=m_norm_pre_ffn, m_w_up=m_w_up, m_conv_f_w=m_conv_f_w, m_conv_f_b=m_conv_f_b, m_w_down=m_w_down, m_norm_post_ffn=m_norm_post_ffn, v_norm_pre_mix=v_norm_pre_mix, v_w_in=v_w_in, v_conv_a_w=v_conv_a_w, v_conv_a_b=v_conv_a_b, v_lru_wa=v_lru_wa, v_lru_ba=v_lru_ba, v_lru_wx=v_lru_wx, v_lru_bx=v_lru_bx, v_lru_lambda=v_lru_lambda, v_hg_lb_logits=v_hg_lb_logits, v_hg_norm_g=v_hg_norm_g, v_w_branch_a=v_w_branch_a, v_w_branch_b=v_w_branch_b, v_w_out=v_w_out, v_norm_post_mix=v_norm_post_mix, v_norm_pre_ffn=v_norm_pre_ffn, v_w_up=v_w_up, v_conv_f_w=v_conv_f_w, v_conv_f_b=v_conv_f_b, v_w_down=v_w_down, v_norm_post_ffn=v_norm_post_ffn)
    weights = {n: given[n] for n in TWIN_WEIGHTS}
    shared = {n: given[n] for n in SHARED_INPUTS}
    per_example = {n: given[n] for n in ['x']}
    grad_fn = _jax.value_and_grad(_loss, argnums=(0, 1))

    def one_microbatch(ex, loss_target):
        ex = dict(ex)
        diff = ex.pop(TWIN_DIFF_INPUT)
        return grad_fn(weights, diff, {**shared, **ex}, loss_target)

    if N_MICROBATCH == 1:
        loss, (grad_w, grad_x) = one_microbatch(per_example, given["loss_target"])
    else:
        def body(carry, xs):
            loss_sum, grad_sum = carry
            l_k, (gw_k, gx_k) = one_microbatch(xs[0], xs[1])
            with _jax.named_scope("update"):
                return (loss_sum + l_k, _jax.tree.map(_jnp.add, grad_sum, gw_k)), gx_k

        init = (_jnp.zeros((), _jnp.float32), _jax.tree.map(_jnp.zeros_like, weights))
        (loss, grad_w), grad_x = _jax.lax.scan(body, init, (per_example, given["loss_target"]))
    with _jax.named_scope("update"):
        delta_w, new_m, new_v = {}, {}, {}
        for n in TWIN_WEIGHTS:
            delta_w[n], new_m[n], new_v[n] = _adamw(weights[n], grad_w[n], given["m_" + n], given["v_" + n])
    return (loss, grad_x, *[grad_w[n] for n in TWIN_WEIGHTS], *[delta_w[n] for n in TWIN_WEIGHTS],
            *[new_m[n] for n in TWIN_WEIGHTS], *[new_v[n] for n in TWIN_WEIGHTS])
```

```python
import functools

import jax
import jax.numpy as jnp
from jax import lax
from jax.experimental import pallas as pl
from jax.experimental.pallas import tpu as pltpu

F32 = jnp.float32
BF16 = jnp.bfloat16

D = 1024
NH = 8
HD = 128
CH = 32
DFF = 2816
DUP = 2 * DFF
NCHIP = 4
SH_IN = 2 * D
SH_UP = DUP // NCHIP
SH_DN = DFF // NCHIP
SH_BR = D // NCHIP
EPS = 1e-6
LRU_C = 8.0
ADAM_LR = 0.001
ADAM_B1 = 0.9
ADAM_B2 = 0.999
ADAM_EPS = 1e-08
ADAM_WD = 0.01
ADAM_STEP = 10
VMEM_BIG = 56 * 1024 * 1024
MESH = pl.DeviceIdType.MESH

SLOT_A, SLOT_B, SLOT_C, SLOT_G = 2, 0, 1, 3


def _slot_of_chip(s):
    return jnp.where(s == 3, 3, (s + 2) % 3)


def _params(sem, vmem=None):
    return pltpu.CompilerParams(dimension_semantics=sem, vmem_limit_bytes=vmem)


_GC = 0.7978845608028654
_GA = 0.044715


def _gelu(x):
    return 0.5 * x * (1.0 + jnp.tanh(_GC * (x + _GA * x * x * x)))


def _gelu_and_grad(x):
    x2 = x * x
    th = jnp.tanh(_GC * x * (1.0 + _GA * x2))
    g = 0.5 * x * (1.0 + th)
    dg = 0.5 * (1.0 + th) + 0.5 * x * (1.0 - th * th) * _GC * (1.0 + 3.0 * _GA * x2)
    return g, dg


def _sig(x):
    return jax.nn.sigmoid(x)


def _dot(a, b):
    return jnp.dot(a, b, preferred_element_type=F32)


def _dot_nt(a, b):
    return lax.dot_general(a, b, (((1,), (1,)), ((), ())), preferred_element_type=F32)


def _dot_tn(a, b):
    return lax.dot_general(a, b, (((0,), (0,)), ((), ())), preferred_element_type=F32)


def _split_dot(mat_bf, x, passes, dot=_dot):
    acc = None
    rem = x
    for _ in range(passes):
        part = rem.astype(BF16)
        t = dot(mat_bf, part)
        acc = t if acc is None else acc + t
        rem = rem - part.astype(F32)
    return acc


def _rms_stats(x):
    r = lax.rsqrt(jnp.mean(x * x, axis=-1, keepdims=True) + EPS)
    return r, x * r


def _rms_bwd(gd, n, r):
    return r * (gd - n * jnp.mean(gd * n, axis=-1, keepdims=True))


def _shift_rows(x, d, fill):
    rows = lax.broadcasted_iota(jnp.int32, (x.shape[0], 1), 0)
    return jnp.where(rows >= d, pltpu.roll(x, d, 0), fill)


def _shift_rows_up(x, d, fill):
    n = x.shape[0]
    rows = lax.broadcasted_iota(jnp.int32, (n, 1), 0)
    return jnp.where(rows < n - d, pltpu.roll(x, n - d, 0), fill)


def _mm_nn_sharded(a, b3, out_dtype, tm, name, slot_fn=None):
    m, k = a.shape
    s, _, ns = b3.shape

    def body(a_ref, b_ref, o_ref):
        o_ref[...] = _dot(a_ref[...], b_ref[...]).astype(out_dtype)

    if slot_fn is None:
        out_shape = jax.ShapeDtypeStruct((m, s * ns), out_dtype)
        out_spec = pl.BlockSpec((tm, ns), lambda j, i: (i, j))
    else:
        out_shape = jax.ShapeDtypeStruct((s, m, ns), out_dtype)
        out_spec = pl.BlockSpec((None, tm, ns), lambda j, i: (slot_fn(j), i, 0))
    return pl.pallas_call(
        body, name=name, out_shape=out_shape, grid=(s, m // tm),
        in_specs=[pl.BlockSpec((tm, k), lambda j, i: (i, 0)),
                  pl.BlockSpec((None, k, ns), lambda j, i: (j, 0, 0))],
        out_specs=out_spec,
        compiler_params=_params(("parallel", "parallel"), VMEM_BIG),
    )(a, b3)


def _mm_nt_sharded(a, b3, tm, name, stacked_slot_fn=None):
    s, k, ns = b3.shape
    m = a.shape[1] if stacked_slot_fn is not None else a.shape[0]

    def body(a_ref, b_ref, o_ref, acc_ref):
        j = pl.program_id(1)
        t = _dot_nt(a_ref[...], b_ref[...])

        @pl.when(j == 0)
        def _():
            acc_ref[...] = t

        @pl.when(j > 0)
        def _():
            acc_ref[...] += t

        @pl.when(j == s - 1)
        def _():
            o_ref[...] = acc_ref[...]

    if stacked_slot_fn is None:
        a_spec = pl.BlockSpec((tm, ns), lambda i, j: (i, j))
    else:
        a_spec = pl.BlockSpec((None, tm, ns), lambda i, j: (stacked_slot_fn(j), i, 0))
    return pl.pallas_call(
        body, name=name, out_shape=jax.ShapeDtypeStruct((m, k), F32), grid=(m // tm, s),
        in_specs=[a_spec, pl.BlockSpec((None, k, ns), lambda i, j: (j, 0, 0))],
        out_specs=pl.BlockSpec((tm, k), lambda i, j: (i, 0)),
        scratch_shapes=[pltpu.VMEM((tm, k), F32)],
        compiler_params=_params(("parallel", "arbitrary"), VMEM_BIG),
    )(a, b3)


def _mm_tn(a, g, tkk, tn, tk, name, stacked_slot_fn=None, stacked_out=False):
    m, k = a.shape
    if stacked_slot_fn is not None:
        n = g.shape[0] * g.shape[2]
        g_spec = pl.BlockSpec((None, tk, tn), lambda kk, j, mm: (stacked_slot_fn(j), mm, 0))
    else:
        n = g.shape[1]
        g_spec = pl.BlockSpec((tk, tn), lambda kk, j, mm: (mm, j))

    def body(a_ref, g_ref, o_ref):
        mm = pl.program_id(2)
        t = _dot_tn(a_ref[...], g_ref[...])

        @pl.when(mm == 0)
        def _():
            o_ref[...] = t

        @pl.when(mm > 0)
        def _():
            o_ref[...] += t

    if stacked_out:
        out_shape = jax.ShapeDtypeStruct((n // tn, k, tn), F32)
        out_spec = pl.BlockSpec((None, tkk, tn), lambda kk, j, mm: (j, kk, 0))
    else:
        out_shape = jax.ShapeDtypeStruct((k, n), F32)
        out_spec = pl.BlockSpec((tkk, tn), lambda kk, j, mm: (kk, j))
    return pl.pallas_call(
        body, name=name, out_shape=out_shape, grid=(k // tkk, n // tn, m // tk),
        in_specs=[pl.BlockSpec((tk, tkk), lambda kk, j, mm: (mm, kk)), g_spec],
        out_specs=out_spec,
        compiler_params=_params(("parallel", "parallel", "arbitrary"), VMEM_BIG),
    )(a, g)


def _norm_fwd(x, gain, tt):
    t = x.shape[0]

    def body(x_ref, g_ref, h_ref):
        _, n = _rms_stats(x_ref[...])
        h_ref[...] = (n * g_ref[...]).astype(BF16)

    return pl.pallas_call(
        body, name="norm_fwd", out_shape=jax.ShapeDtypeStruct((t, D), BF16), grid=(t // tt,),
        in_specs=[pl.BlockSpec((tt, D), lambda i: (i, 0)), pl.BlockSpec((1, D), lambda i: (0, 0))],
        out_specs=pl.BlockSpec((tt, D), lambda i: (i, 0)),
        compiler_params=_params(("parallel",)),
    )(x, gain)


def _lru_gates(xc, wa_ref, wx_ref, ba, bx, lam):
    xcb = xc.astype(BF16)
    ra = jnp.concatenate([_dot(xcb[:, n * HD:(n + 1) * HD], wa_ref[n]) for n in range(NH)], axis=1) + ba
    ix = jnp.concatenate([_dot(xcb[:, n * HD:(n + 1) * HD], wx_ref[n]) for n in range(NH)], axis=1) + bx
    r = _sig(ra)
    ig = _sig(ix)
    z = -lam
    sp = jnp.maximum(z, 0.0) + jnp.log1p(jnp.exp(-jnp.abs(z)))
    log_a = -LRU_C * r * sp
    a = jnp.exp(log_a)
    z2 = 2.0 * log_a
    series = -z2 * (1.0 + z2 * (0.5 + z2 * (1.0 / 6.0 + z2 * (1.0 / 24.0))))
    om = jnp.where(z2 > -0.02, series, 1.0 - jnp.exp(z2))
    mult = jnp.sqrt(om)
    return xcb, r, ig, sp, a, mult


def _mixer_a_fwd(p4, cw, cb, wa, wx, ba, bx, lam, tt):
    t = p4.shape[1]
    steps = tt.bit_length() - 1

    def body(p_ref, cw_ref, cb_ref, wa_ref, wx_ref, ba_ref, bx_ref, lam_ref, ya_ref, h_ref, ext, hc):
        i = pl.program_id(0)

        @pl.when(i == 0)
        def _():
            ext[0:8, :] = jnp.zeros((8, D), F32)
            hc[...] = jnp.zeros((1, D), F32)

        xa = p_ref[:, 0:D]
        ga = p_ref[:, D:2 * D]
        ext[8:, :] = xa
        xc = cb_ref[...] + sum(cw_ref[k:k + 1, :] * ext[pl.ds(5 + k, tt), :] for k in range(4))
        ext[0:8, :] = xa[tt - 8:, :]
        _, _, ig, _, a, mult = _lru_gates(xc, wa_ref, wx_ref, ba_ref[...], bx_ref[...], lam_ref[...])
        u = mult * ig * xc
        acc_a, acc_u = a, u
        for lvl in range(steps):
            d = 1 << lvl
            acc_u = acc_a * _shift_rows(acc_u, d, 0.0) + acc_u
            acc_a = acc_a * _shift_rows(acc_a, d, 1.0)
        h = acc_a * hc[...] + acc_u
        hc[...] = h[tt - 1:tt, :]
        h_ref[...] = h
        ya_ref[...] = (h * _gelu(ga)).astype(BF16)

    full = lambda shape: pl.BlockSpec(shape, lambda i: (0,) * len(shape))
    return pl.pallas_call(
        body, name="mixer_a_fwd",
        out_shape=(jax.ShapeDtypeStruct((t, D), BF16), jax.ShapeDtypeStruct((t, D), F32)),
        grid=(t // tt,),
        in_specs=[pl.BlockSpec((None, tt, 2 * D), lambda i: (SLOT_A, i, 0)),
                  full((4, D)), full((1, D)), full((NH, HD, HD)), full((NH, HD, HD)),
                  full((1, D)), full((1, D)), full((1, D))],
        out_specs=(pl.BlockSpec((tt, D), lambda i: (i, 0)), pl.BlockSpec((tt, D), lambda i: (i, 0))),
        scratch_shapes=[pltpu.VMEM((tt + 8, D), F32), pltpu.VMEM((1, D), F32)],
        compiler_params=_params(("arbitrary",), VMEM_BIG),
    )(p4, cw, cb, wa, wx, ba, bx, lam)


def _chunk_masks(tt):
    row = lax.broadcasted_iota(jnp.int32, (tt, tt), 0)
    col = lax.broadcasted_iota(jnp.int32, (tt, tt), 1)
    same = jnp.right_shift(row, 5) == jnp.right_shift(col, 5)
    low = same & (col <= row)
    upp = same & (col > row)
    return low, upp


def _hg_head_fwd(q, fz, lbh, low_bf, upp_bf):
    sg = _sig(fz)
    sgn = _sig(-fz)
    f = lbh + (1.0 - lbh) * sg
    logf = jnp.log(f)
    k = (1.0 - lbh) * sgn
    g = _split_dot(low_bf, logf, 3)
    gu = _split_dot(upp_bf, logf, 3)
    eg = jnp.exp(g)
    eng = jnp.exp(-g)
    egu = jnp.exp(gu)
    qt = q * eg
    kt = k * eng
    kd = k * egu
    return sg, sgn, f, k, g, eg, eng, egu, qt, kt, kd


def _lb_of(logits_ref):
    return _sig(logits_ref[0:1, :] - logits_ref[1:2, :])


def _hgrn2_fwd(p4, logits, gnorm, tt):
    t = p4.shape[1]
    nc = tt // CH

    def body(p_ref, lg_ref, gn_ref, yb_ref, o_ref, ss_ref, st):
        i = pl.program_id(0)

        @pl.when(i == 0)
        def _():
            st[...] = jnp.zeros((NH, HD, HD), F32)

        low, upp = _chunk_masks(tt)
        low_bf = low.astype(BF16)
        upp_bf = upp.astype(BF16)
        lb = _lb_of(lg_ref)
        for h in range(NH):
            c0 = h * HD
            q = p_ref[0, :, c0:c0 + HD]
            fz = p_ref[0, :, D + c0:D + c0 + HD]
            v = p_ref[1, :, c0:c0 + HD]
            og = p_ref[1, :, D + c0:D + c0 + HD]
            lbh = lb[:, c0:c0 + HD]
            _, _, _, _, g, _, _, _, qt, kt, kd = _hg_head_fwd(q, fz, lbh, low_bf, upp_bf)
            qtb, ktb, kdb, vb = qt.astype(BF16), kt.astype(BF16), kd.astype(BF16), v.astype(BF16)
            att = jnp.where(low, _dot_nt(qtb, ktb), 0.0)
            o_in = _dot(att.astype(BF16), vb)
            s_t = st[h]
            pieces = []
            for c in range(nc):
                sl = slice(c * CH, (c + 1) * CH)
                s_bf = s_t.astype(BF16)
                ss_ref[c, h] = s_bf
                pieces.append(o_in[sl] + _dot_nt(qtb[sl], s_bf))
                dec = jnp.exp(g[c * CH + CH - 1:c * CH + CH, :])
                s_t = s_t * dec + _dot_tn(vb[sl], kdb[sl])
            st[h] = s_t
            o = jnp.concatenate(pieces, axis=0)
            r, n = _rms_stats(o)
            ob = n * gn_ref[:, c0:c0 + HD]
            o_ref[:, c0:c0 + HD] = o
            yb_ref[:, c0:c0 + HD] = (ob * (og * _sig(og))).astype(BF16)

    return pl.pallas_call(
        body, name="hgrn2_fwd",
        out_shape=(jax.ShapeDtypeStruct((t, D), BF16), jax.ShapeDtypeStruct((t, D), F32),
                   jax.ShapeDtypeStruct((t // CH, NH, HD, HD), BF16)),
        grid=(t // tt,),
        in_specs=[pl.BlockSpec((2, tt, 2 * D), lambda i: (0, i, 0)),
                  pl.BlockSpec((2, D), lambda i: (0, 0)), pl.BlockSpec((1, D), lambda i: (0, 0))],
        out_specs=(pl.BlockSpec((tt, D), lambda i: (i, 0)), pl.BlockSpec((tt, D), lambda i: (i, 0)),
                   pl.BlockSpec((nc, NH, HD, HD), lambda i: (i, 0, 0, 0))),
        scratch_shapes=[pltpu.VMEM((NH, HD, HD), F32)],
        compiler_params=_params(("arbitrary",), VMEM_BIG),
    )(p4, logits, gnorm)


def _mid_fwd(ya, yb, p4, x, wa, wb, wo, g_pm, g_pf, tt):
    t = x.shape[0]

    def body(ya_ref, yb_ref, gt_ref, x_ref, wa_ref, wb_ref, wo_ref, gpm_ref, gpf_ref,
             za_ref, zb_ref, mix_ref, m2_ref, x1_ref, h2_ref):
        za = _dot(ya_ref[...], wa_ref[...])
        zb = _dot(yb_ref[...], wb_ref[...])
        mix = _sig(gt_ref[:, 0:D]) * za + _sig(gt_ref[:, D:2 * D]) * zb
        mixb = mix.astype(BF16)
        m2 = _dot(mixb, wo_ref[...])
        _, n2 = _rms_stats(m2)
        x1 = x_ref[...] + n2 * gpm_ref[...]
        _, n1 = _rms_stats(x1)
        za_ref[...] = za.astype(BF16)
        zb_ref[...] = zb.astype(BF16)
        mix_ref[...] = mixb
        m2_ref[...] = m2
        x1_ref[...] = x1
        h2_ref[...] = (n1 * gpf_ref[...]).astype(BF16)

    row = lambda dt: jax.ShapeDtypeStruct((t, D), dt)
    tile = pl.BlockSpec((tt, D), lambda i: (i, 0))
    wsp = pl.BlockSpec((D, D), lambda i: (0, 0))
    vec = pl.BlockSpec((1, D), lambda i: (0, 0))
    return pl.pallas_call(
        body, name="mid_fwd",
        out_shape=(row(BF16), row(BF16), row(BF16), row(F32), row(F32), row(BF16)),
        grid=(t // tt,),
        in_specs=[tile, tile, pl.BlockSpec((None, tt, 2 * D), lambda i: (SLOT_G, i, 0)), tile,
                  wsp, wsp, wsp, vec, vec],
        out_specs=(tile,) * 6,
        compiler_params=_params(("parallel",), VMEM_BIG),
    )(ya, yb, p4, x, wa, wb, wo, g_pm, g_pf)


def _ffn_act_fwd(up_pre, cfw, cfb, tt):
    t = up_pre.shape[0]

    def body(u_ref, halo_ref, w_ref, b_ref, y_ref, ext):
        i = pl.program_id(0)
        ext[0:8, :] = jnp.where(i > 0, halo_ref[...], 0.0)
        ext[8:, :] = u_ref[...]
        up = b_ref[...] + sum(w_ref[k:k + 1, :] * ext[pl.ds(6 + k, tt), :] for k in range(3))
        y_ref[...] = (_gelu(up[:, 0:DFF]) * up[:, DFF:DUP]).astype(BF16)

    hb = tt // 8
    return pl.pallas_call(
        body, name="ffn_act_fwd", out_shape=jax.ShapeDtypeStruct((t, DFF), BF16), grid=(t // tt,),
        in_specs=[pl.BlockSpec((tt, DUP), lambda i: (i, 0)),
                  pl.BlockSpec((8, DUP), lambda i: (jnp.maximum(i * hb - 1, 0), 0)),
                  pl.BlockSpec((3, DUP), lambda i: (0, 0)), pl.BlockSpec((1, DUP), lambda i: (0, 0))],
        out_specs=pl.BlockSpec((tt, DFF), lambda i: (i, 0)),
        scratch_shapes=[pltpu.VMEM((tt + 8, DUP), F32)],
        compiler_params=_params(("parallel",), VMEM_BIG),
    )(up_pre, up_pre, cfw, cfb)


def _down_loss(y, wdn, x1, tgt, g_post, tt):
    t = x1.shape[0]

    def body(y_ref, w_ref, x1_ref, t_ref, g_ref, dx2_ref, dm3_ref, lossv_ref, dg_ref):
        i = pl.program_id(0)
        m3 = _dot(y_ref[...], w_ref[...])
        r, n3 = _rms_stats(m3)
        g = g_ref[...]
        e = x1_ref[...] + n3 * g - t_ref[...]
        dx2 = e * (1.0 / D)
        dx2_ref[...] = dx2
        dm3_ref[...] = _rms_bwd(dx2 * g, n3, r).astype(BF16)
        lv = jnp.sum(e * e, axis=0, keepdims=True)
        dgv = jnp.sum(dx2 * n3, axis=0, keepdims=True)

        @pl.when(i == 0)
        def _():
            lossv_ref[...] = lv
            dg_ref[...] = dgv

        @pl.when(i > 0)
        def _():
            lossv_ref[...] += lv
            dg_ref[...] += dgv

    tile = pl.BlockSpec((tt, D), lambda i: (i, 0))
    vec = pl.BlockSpec((1, D), lambda i: (0, 0))
    return pl.pallas_call(
        body, name="down_loss",
        out_shape=(jax.ShapeDtypeStruct((t, D), F32), jax.ShapeDtypeStruct((t, D), BF16),
                   jax.ShapeDtypeStruct((1, D), F32), jax.ShapeDtypeStruct((1, D), F32)),
        grid=(t // tt,),
        in_specs=[pl.BlockSpec((tt, DFF), lambda i: (i, 0)), pl.BlockSpec((DFF, D), lambda i: (0, 0)),
                  tile, tile, vec],
        out_specs=(tile, tile, vec, vec),
        compiler_params=_params(("arbitrary",), VMEM_BIG),
    )(y, wdn, x1, tgt, g_post)


def _ffn_act_bwd(dm3, wdn, up_pre, cfw, cfb, tt):
    t = up_pre.shape[0]
    nt = t // tt

    def body(dm_ref, dmn_ref, w_ref, u_ref, up_ref, un_ref, cw_ref, cb_ref, du_ref, dcw_ref, dcb_ref, ext, dext):
        i = pl.program_id(0)
        ext[0:8, :] = jnp.where(i > 0, up_ref[...], 0.0)
        ext[8:tt + 8, :] = u_ref[...]
        ext[tt + 8:, :] = un_ref[...]
        dy = jnp.concatenate([_dot_nt(dm_ref[...], w_ref[...]),
                              _dot_nt(dmn_ref[...], w_ref[...])[0:8, :]], axis=0)
        up = cb_ref[...] + sum(cw_ref[k:k + 1, :] * ext[pl.ds(6 + k, tt + 8), :] for k in range(3))
        gl, dgl = _gelu_and_grad(up[:, 0:DFF])
        rows = lax.broadcasted_iota(jnp.int32, (tt + 8, 1), 0)
        live = (rows < tt) | (i < nt - 1)
        dext[:, 0:DFF] = jnp.where(live, dy * up[:, DFF:DUP] * dgl, 0.0)
        dext[:, DFF:DUP] = jnp.where(live, dy * gl, 0.0)
        du_ref[...] = sum(cw_ref[k:k + 1, :] * dext[pl.ds(2 - k, tt), :] for k in range(3)).astype(BF16)
        dup = dext[0:tt, :]
        dcw = jnp.concatenate(
            [jnp.sum(dup * ext[pl.ds(6 + k, tt), :], axis=0, keepdims=True) for k in range(3)], axis=0)
        dcb = jnp.sum(dup, axis=0, keepdims=True)

        @pl.when(i == 0)
        def _():
            dcw_ref[...] = dcw
            dcb_ref[...] = dcb

        @pl.when(i > 0)
        def _():
            dcw_ref[...] += dcw
            dcb_ref[...] += dcb

    hb = tt // 8
    last8 = t // 8 - 1
    return pl.pallas_call(
        body, name="ffn_act_bwd",
        out_shape=(jax.ShapeDtypeStruct((t, DUP), BF16), jax.ShapeDtypeStruct((3, DUP), F32),
                   jax.ShapeDtypeStruct((1, DUP), F32)),
        grid=(nt,),
        in_specs=[pl.BlockSpec((tt, D), lambda i: (i, 0)),
                  pl.BlockSpec((16, D), lambda i: (jnp.minimum((i + 1) * (tt // 16), t // 16 - 1), 0)),
                  pl.BlockSpec((DFF, D), lambda i: (0, 0)),
                  pl.BlockSpec((tt, DUP), lambda i: (i, 0)),
                  pl.BlockSpec((8, DUP), lambda i: (jnp.maximum(i * hb - 1, 0), 0)),
                  pl.BlockSpec((8, DUP), lambda i: (jnp.minimum((i + 1) * hb, last8), 0)),
                  pl.BlockSpec((3, DUP), lambda i: (0, 0)), pl.BlockSpec((1, DUP), lambda i: (0, 0))],
        out_specs=(pl.BlockSpec((tt, DUP), lambda i: (i, 0)), pl.BlockSpec((3, DUP), lambda i: (0, 0)),
                   pl.BlockSpec((1, DUP), lambda i: (0, 0))),
        scratch_shapes=[pltpu.VMEM((tt + 16, DUP), F32), pltpu.VMEM((tt + 8, DUP), F32)],
        compiler_params=_params(("arbitrary",), VMEM_BIG),
    )(dm3, dm3, wdn, up_pre, up_pre, up_pre, cfw, cfb)


def _mid_bwd(dh2, dx2, x1, m2, za, zb, p4, wa, wb, wo, g_pm, g_pf, tt):
    t = x1.shape[0]

    def body(dh2_ref, dx2_ref, x1_ref, m2_ref, za_ref, zb_ref, gt_ref, wa_ref, wb_ref, wo_ref, gpm_ref, gpf_ref,
             dx1_ref, dm2_ref, dza_ref, dzb_ref, dya_ref, dyb_ref, dp_ref, dgpm_ref, dgpf_ref):
        i = pl.program_id(0)
        r1, n1 = _rms_stats(x1_ref[...])
        dh2 = dh2_ref[...]
        dx1 = dx2_ref[...] + _rms_bwd(dh2 * gpf_ref[...], n1, r1)
        r2, n2 = _rms_stats(m2_ref[...])
        dm2 = _rms_bwd(dx1 * gpm_ref[...], n2, r2).astype(BF16)
        dmix = _dot_nt(dm2, wo_ref[...])
        sa = _sig(gt_ref[:, 0:D])
        sb = _sig(gt_ref[:, D:2 * D])
        dza = (dmix * sa).astype(BF16)
        dzb = (dmix * sb).astype(BF16)
        dp_ref[:, 0:D] = (dmix * za_ref[...].astype(F32) * sa * (1.0 - sa)).astype(BF16)
        dp_ref[:, D:2 * D] = (dmix * zb_ref[...].astype(F32) * sb * (1.0 - sb)).astype(BF16)
        dx1_ref[...] = dx1
        dm2_ref[...] = dm2
        dza_ref[...] = dza
        dzb_ref[...] = dzb
        dya_ref[...] = _dot_nt(dza, wa_ref[...])
        dyb_ref[...] = _dot_nt(dzb, wb_ref[...])
        dgpf = jnp.sum(dh2 * n1, axis=0, keepdims=True)
        dgpm = jnp.sum(dx1 * n2, axis=0, keepdims=True)

        @pl.when(i == 0)
        def _():
            dgpf_ref[...] = dgpf
            dgpm_ref[...] = dgpm

        @pl.when(i > 0)
        def _():
            dgpf_ref[...] += dgpf
            dgpm_ref[...] += dgpm

    row = lambda dt: jax.ShapeDtypeStruct((t, D), dt)
    tile = pl.BlockSpec((tt, D), lambda i: (i, 0))
    wsp = pl.BlockSpec((D, D), lambda i: (0, 0))
    vec = pl.BlockSpec((1, D), lambda i: (0, 0))
    gates = pl.BlockSpec((None, tt, 2 * D), lambda i: (SLOT_G, i, 0))
    return pl.pallas_call(
        body, name="mid_bwd",
        out_shape=(row(F32), row(BF16), row(BF16), row(BF16), row(F32), row(F32),
                   jax.ShapeDtypeStruct((NCHIP, t, 2 * D), BF16),
                   jax.ShapeDtypeStruct((1, D), F32), jax.ShapeDtypeStruct((1, D), F32)),
        grid=(t // tt,),
        in_specs=[tile, tile, tile, tile, tile, tile, gates, wsp, wsp, wsp, vec, vec],
        out_specs=(tile, tile, tile, tile, tile, tile, gates, vec, vec),
        compiler_params=_params(("arbitrary",), VMEM_BIG),
    )(dh2, dx2, x1, m2, za, zb, p4, wa, wb, wo, g_pm, g_pf)


def _hgrn2_bwd(p4, o_all, ss, dyb, dp4, logits, gnorm, tt):
    t = p4.shape[1]
    nt = t // tt
    nc = tt // CH

    def body(p_ref, o_ref, ss_ref, dyb_ref, dp_in, lg_ref, gn_ref, dp_ref, dlb_ref, dgn_ref, dst):
        del dp_in
        i = pl.program_id(0)

        @pl.when(i == 0)
        def _():
            dst[...] = jnp.zeros((NH, HD, HD), F32)

        low, upp = _chunk_masks(tt)
        low_bf = low.astype(BF16)
        upp_bf = upp.astype(BF16)
        lb = _lb_of(lg_ref)
        dlb_parts = []
        dgn_parts = []
        for h in range(NH):
            c0 = h * HD
            q = p_ref[0, :, c0:c0 + HD]
            fz = p_ref[0, :, D + c0:D + c0 + HD]
            v = p_ref[1, :, c0:c0 + HD]
            og = p_ref[1, :, D + c0:D + c0 + HD]
            lbh = lb[:, c0:c0 + HD]
            gh = gn_ref[:, c0:c0 + HD]
            sg, sgn, f, k, g, eg, eng, egu, qt, kt, kd = _hg_head_fwd(q, fz, lbh, low_bf, upp_bf)
            qtb, ktb, kdb, vb = qt.astype(BF16), kt.astype(BF16), kd.astype(BF16), v.astype(BF16)
            att = jnp.where(low, _dot_nt(qtb, ktb), 0.0).astype(BF16)
            o = o_ref[:, c0:c0 + HD]
            dyb_h = dyb_ref[:, c0:c0 + HD]
            r, n = _rms_stats(o)
            so = _sig(og)
            dob = dyb_h * (og * so)
            dog = dyb_h * (n * gh) * (so * (1.0 + og * (1.0 - so)))
            dgn_parts.append(jnp.sum(dob * n, axis=0, keepdims=True))
            do = _rms_bwd(dob * gh, n, r)
            dob_ = do.astype(BF16)
            d_att = jnp.where(low, _dot_nt(dob_, vb), 0.0).astype(BF16)
            dv_in = _dot_tn(att, dob_)
            dqt_in = _dot(d_att, ktb)
            dkt = _dot_tn(d_att, qtb)
            ds_t = dst[h]
            dv_p, dqt_p, dkd_p, dgl_p = [None] * nc, [None] * nc, [None] * nc, [None] * nc
            for c in reversed(range(nc)):
                sl = slice(c * CH, (c + 1) * CH)
                s_prev = ss_ref[c, h]
                ds_bf = ds_t.astype(BF16)
                dec = jnp.exp(g[c * CH + CH - 1:c * CH + CH, :])
                dv_p[c] = dv_in[sl] + _dot_nt(kdb[sl], ds_bf)
                dqt_p[c] = dqt_in[sl] + _dot(dob_[sl], s_prev)
                dkd_p[c] = _dot(vb[sl], ds_bf)
                ddec = jnp.sum(s_prev.astype(F32) * ds_t, axis=0, keepdims=True)
                dgl_p[c] = jnp.broadcast_to(ddec * dec, (CH, HD))
                ds_t = ds_t * dec + _dot_tn(dob_[sl], qtb[sl])
            dst[h] = ds_t
            dv = jnp.concatenate(dv_p, axis=0)
            dqt = jnp.concatenate(dqt_p, axis=0)
            dkd = jnp.concatenate(dkd_p, axis=0)
            dgl = jnp.concatenate(dgl_p, axis=0)
            dq = dqt * eg
            dk = dkt * eng + dkd * egu
            dg = dqt * qt - dkt * kt
            dgu = dkd * kd
            dlogf = _split_dot(low_bf, dg, 2, _dot_tn) + _split_dot(upp_bf, dgu, 2, _dot_tn) + dgl
            one_m_lb = 1.0 - lbh
            dfz = one_m_lb * sg * sgn * (dlogf / f - dk)
            dlb_parts.append(jnp.sum(sgn * (dlogf / f - dk), axis=0, keepdims=True))
            dp_ref[0, :, c0:c0 + HD] = dq.astype(BF16)
            dp_ref[0, :, D + c0:D + c0 + HD] = dfz.astype(BF16)
            dp_ref[1, :, c0:c0 + HD] = dv.astype(BF16)
            dp_ref[1, :, D + c0:D + c0 + HD] = dog.astype(BF16)
        dlb = jnp.concatenate(dlb_parts, axis=1)
        dgn = jnp.concatenate(dgn_parts, axis=1)

        @pl.when(i == 0)
        def _():
            dlb_ref[0:1, :] = dlb
            dgn_ref[...] = dgn

        @pl.when(i > 0)
        def _():
            dlb_ref[0:1, :] += dlb
            dgn_ref[...] += dgn

        @pl.when(i == nt - 1)
        def _():
            d0 = dlb_ref[0:1, :] * lb * (1.0 - lb)
            dlb_ref[0:1, :] = d0
            dlb_ref[1:2, :] = -d0

    rev = lambda i: nt - 1 - i
    vec = pl.BlockSpec((1, D), lambda i: (0, 0))
    return pl.pallas_call(
        body, name="hgrn2_bwd",
        out_shape=(jax.ShapeDtypeStruct(dp4.shape, BF16), jax.ShapeDtypeStruct((2, D), F32),
                   jax.ShapeDtypeStruct((1, D), F32)),
        grid=(nt,),
        in_specs=[pl.BlockSpec((2, tt, 2 * D), lambda i: (0, rev(i), 0)),
                  pl.BlockSpec((tt, D), lambda i: (rev(i), 0)),
                  pl.BlockSpec((nc, NH, HD, HD), lambda i: (rev(i), 0, 0, 0)),
                  pl.BlockSpec((tt, D), lambda i: (rev(i), 0)),
                  pl.BlockSpec(memory_space=pl.ANY),
                  pl.BlockSpec((2, D), lambda i: (0, 0)), vec],
        out_specs=(pl.BlockSpec((2, tt, 2 * D), lambda i: (0, rev(i), 0)),
                   pl.BlockSpec((2, D), lambda i: (0, 0)), vec),
        scratch_shapes=[pltpu.VMEM((NH, HD, HD), F32)],
        input_output_aliases={4: 0},
        compiler_params=_params(("arbitrary",), VMEM_BIG),
    )(p4, o_all, ss, dyb, dp4, logits, gnorm)


def _mixer_a_bwd(p4, hseq, dya, dp4, cw, cb, wa, wx, ba, bx, lam, tt):
    t = p4.shape[1]
    nt = t // tt
    steps = tt.bit_length() - 1

    def body(p_ref, ph_ref, h_ref, hh_ref, dya_ref, dp_in, cw_ref, cb_ref, wa_ref, wx_ref, ba_ref, bx_ref, lam_ref,
             dp_ref, dcw_ref, dcb_ref, dwa_ref, dwx_ref, dba_ref, dbx_ref, dlam_ref,
             ext, dext, dhc, afc):
        del dp_in
        i = pl.program_id(0)
        first_tile = i == nt - 1

        @pl.when(i == 0)
        def _():
            dext[tt:, :] = jnp.zeros((8, D), F32)
            dhc[...] = jnp.zeros((1, D), F32)
            afc[...] = jnp.zeros((1, D), F32)

        xa = p_ref[:, 0:D]
        ga = p_ref[:, D:2 * D]
        ext[0:8, :] = jnp.where(first_tile, 0.0, ph_ref[:, 0:D])
        ext[8:, :] = xa
        xc = cb_ref[...] + sum(cw_ref[k:k + 1, :] * ext[pl.ds(5 + k, tt), :] for k in range(4))
        lam = lam_ref[...]
        xcb, r, ig, sp, a, mult = _lru_gates(xc, wa_ref, wx_ref, ba_ref[...], bx_ref[...], lam)
        h = h_ref[...]
        gl, dgl = _gelu_and_grad(ga)
        dya = dya_ref[...]
        dga = dya * h * dgl
        acc_g = dya * gl
        rows = lax.broadcasted_iota(jnp.int32, (tt, 1), 0)
        acc_b = jnp.where(rows == tt - 1, afc[...], pltpu.roll(a, tt - 1, 0))
        for lvl in range(steps):
            d = 1 << lvl
            acc_g = acc_g + acc_b * _shift_rows_up(acc_g, d, 0.0)
            acc_b = acc_b * _shift_rows_up(acc_b, d, 1.0)
        dh = acc_g + acc_b * dhc[...]
        dhc[...] = dh[0:1, :]
        afc[...] = a[0:1, :]
        h_prev = jnp.where(rows == 0, jnp.where(first_tile, 0.0, hh_ref[7:8, :]), pltpu.roll(h, 1, 0))
        da = dh * h_prev
        dmult = dh * ig * xc
        di = dh * mult * xc
        dlog_a = da * a - dmult * a * a / mult
        dr = dlog_a * (-LRU_C * sp)
        dsp = jnp.sum(dlog_a * (-LRU_C * r), axis=0, keepdims=True)
        dra = dr * r * (1.0 - r)
        dix = di * ig * (1.0 - ig)
        drab = dra.astype(BF16)
        dixb = dix.astype(BF16)
        dxc_lin = []
        dwa_new = []
        dwx_new = []
        for n in range(NH):
            cs = slice(n * HD, (n + 1) * HD)
            dxc_lin.append(_dot_nt(drab[:, cs], wa_ref[n]) + _dot_nt(dixb[:, cs], wx_ref[n]))
            dwa_new.append(_dot_tn(xcb[:, cs], drab[:, cs]))
            dwx_new.append(_dot_tn(xcb[:, cs], dixb[:, cs]))
        dxc = dh * mult * ig + jnp.concatenate(dxc_lin, axis=1)
        dext[0:tt, :] = dxc
        dxa = sum(cw_ref[k:k + 1, :] * dext[pl.ds(3 - k, tt), :] for k in range(4))
        dext[tt:, :] = dxc[0:8, :]
        dp_ref[:, 0:D] = dxa.astype(BF16)
        dp_ref[:, D:2 * D] = dga.astype(BF16)
        dcw = jnp.concatenate(
            [jnp.sum(dxc * ext[pl.ds(5 + k, tt), :], axis=0, keepdims=True) for k in range(4)], axis=0)
        dcb = jnp.sum(dxc, axis=0, keepdims=True)
        dba = jnp.sum(dra, axis=0, keepdims=True)
        dbx = jnp.sum(dix, axis=0, keepdims=True)
        dlam = dsp * (-_sig(-lam))

        @pl.when(i == 0)
        def _():
            dcw_ref[...] = dcw
            dcb_ref[...] = dcb
            dba_ref[...] = dba
            dbx_ref[...] = dbx
            dlam_ref[...] = dlam
            for n in range(NH):
                dwa_ref[n] = dwa_new[n]
                dwx_ref[n] = dwx_new[n]

        @pl.when(i > 0)
        def _():
            dcw_ref[...] += dcw
            dcb_ref[...] += dcb
            dba_ref[...] += dba
            dbx_ref[...] += dbx
            dlam_ref[...] += dlam
            for n in range(NH):
                dwa_ref[n] += dwa_new[n]
                dwx_ref[n] += dwx_new[n]

    rev = lambda i: nt - 1 - i
    hb = tt // 8
    full = lambda shape: pl.BlockSpec(shape, lambda i: (0,) * len(shape))
    vecs = jax.ShapeDtypeStruct((1, D), F32)
    blk = jax.ShapeDtypeStruct((NH, HD, HD), F32)
    return pl.pallas_call(
        body, name="mixer_a_bwd",
        out_shape=(jax.ShapeDtypeStruct(dp4.shape, BF16), jax.ShapeDtypeStruct((4, D), F32), vecs, blk, blk,
                   vecs, vecs, vecs),
        grid=(nt,),
        in_specs=[pl.BlockSpec((None, tt, 2 * D), lambda i: (SLOT_A, rev(i), 0)),
                  pl.BlockSpec((None, 8, 2 * D), lambda i: (SLOT_A, jnp.maximum(rev(i) * hb - 1, 0), 0)),
                  pl.BlockSpec((tt, D), lambda i: (rev(i), 0)),
                  pl.BlockSpec((8, D), lambda i: (jnp.maximum(rev(i) * hb - 1, 0), 0)),
                  pl.BlockSpec((tt, D), lambda i: (rev(i), 0)),
                  pl.BlockSpec(memory_space=pl.ANY),
                  full((4, D)), full((1, D)), full((NH, HD, HD)), full((NH, HD, HD)),
                  full((1, D)), full((1, D)), full((1, D))],
        out_specs=(pl.BlockSpec((None, tt, 2 * D), lambda i: (SLOT_A, rev(i), 0)),
                   full((4, D)), full((1, D)), full((NH, HD, HD)), full((NH, HD, HD)),
                   full((1, D)), full((1, D)), full((1, D))),
        scratch_shapes=[pltpu.VMEM((tt + 8, D), F32), pltpu.VMEM((tt + 8, D), F32),
                        pltpu.VMEM((1, D), F32), pltpu.VMEM((1, D), F32)],
        input_output_aliases={5: 0},
        compiler_params=_params(("arbitrary",), VMEM_BIG),
    )(p4, p4, hseq, hseq, dya, dp4, cw, cb, wa, wx, ba, bx, lam)


def _norm_bwd(dh1, dx1, x, gain, tt):
    t = x.shape[0]

    def body(dh_ref, dx1_ref, x_ref, g_ref, dx_ref, dg_ref):
        i = pl.program_id(0)
        r, n = _rms_stats(x_ref[...])
        dh = dh_ref[...]
        dx_ref[...] = dx1_ref[...] + _rms_bwd(dh * g_ref[...], n, r)
        dgv = jnp.sum(dh * n, axis=0, keepdims=True)

        @pl.when(i == 0)
        def _():
            dg_ref[...] = dgv

        @pl.when(i > 0)
        def _():
            dg_ref[...] += dgv

    tile = pl.BlockSpec((tt, D), lambda i: (i, 0))
    vec = pl.BlockSpec((1, D), lambda i: (0, 0))
    return pl.pallas_call(
        body, name="norm_bwd",
        out_shape=(jax.ShapeDtypeStruct((t, D), F32), jax.ShapeDtypeStruct((1, D), F32)),
        grid=(t // tt,), in_specs=[tile, tile, tile, vec], out_specs=(tile, vec),
        compiler_params=_params(("arbitrary",)),
    )(dh1, dx1, x, gain)


def _local_step(x, tgt, w, small):
    t = x.shape[0]
    tt = min(256, t)
    tm = min(512, t)
    wa_bf = small["lru_wa"].astype(BF16)
    wx_bf = small["lru_wx"].astype(BF16)
    w_br_a = w["w_branch_a"].reshape(D, D)
    w_br_b = w["w_branch_b"].reshape(D, D)
    w_out = w["w_out"].reshape(D, D)
    w_down = w["w_down"].reshape(DFF, D)
    conv_a_w = w["conv_a_w"]
    conv_f_w = w["conv_f_w"]

    h1 = _norm_fwd(x, small["norm_pre_mix"], tt)
    p4 = _mm_nn_sharded(h1, w["w_in"], F32, tm, "mm_in", _slot_of_chip)
    ya, hseq = _mixer_a_fwd(p4, conv_a_w, small["conv_a_b"], wa_bf, wx_bf, small["lru_ba"], small["lru_bx"],
                            small["lru_lambda"], tt)
    yb, o_all, ss = _hgrn2_fwd(p4, small["hg_lb_logits"], small["hg_norm_g"], tt)
    za, zb, mixb, m2, x1, h2 = _mid_fwd(ya, yb, p4, x, w_br_a, w_br_b, w_out, small["norm_post_mix"],
                                        small["norm_pre_ffn"], tt)
    up_pre = _mm_nn_sharded(h2, w["w_up"], F32, tm, "mm_up")
    tf = min(128, t)
    y = _ffn_act_fwd(up_pre, conv_f_w, small["conv_f_b"], tf)
    dx2, dm3, lossv, d_norm_post_ffn = _down_loss(y, w_down, x1, tgt, small["norm_post_ffn"], tt)

    d_w_down = _mm_tn(y, dm3, DFF // 2, D, tm, "mm_dw_down")
    dup_pre, d_conv_f_w, d_conv_f_b = _ffn_act_bwd(dm3, w_down, up_pre, conv_f_w, small["conv_f_b"], tf)
    d_w_up = _mm_tn(h2, dup_pre, D, SH_UP, tm, "mm_dw_up", stacked_out=True)
    dh2 = _mm_nt_sharded(dup_pre, w["w_up"], tm, "mm_dh2")
    dx1, dm2, dza, dzb, dya, dyb, dp4, d_norm_post_mix, d_norm_pre_ffn = _mid_bwd(
        dh2, dx2, x1, m2, za, zb, p4, w_br_a, w_br_b, w_out, small["norm_post_mix"], small["norm_pre_ffn"], tt)
    d_w_out = _mm_tn(mixb, dm2, D, D, tm, "mm_dw_out")
    d_w_br_a = _mm_tn(ya, dza, D, D, tm, "mm_dw_bra")
    d_w_br_b = _mm_tn(yb, dzb, D, D, tm, "mm_dw_brb")
    dp4, d_lb, d_hg_norm_g = _hgrn2_bwd(p4, o_all, ss, dyb, dp4, small["hg_lb_logits"], small["hg_norm_g"], tt)
    dp4, d_conv_a_w, d_conv_a_b, d_lru_wa, d_lru_wx, d_lru_ba, d_lru_bx, d_lru_lambda = _mixer_a_bwd(
        p4, hseq, dya, dp4, conv_a_w, small["conv_a_b"], wa_bf, wx_bf, small["lru_ba"], small["lru_bx"],
        small["lru_lambda"], tt)
    d_w_in = _mm_tn(h1, dp4, D, SH_IN, tm, "mm_dw_in", stacked_slot_fn=_slot_of_chip, stacked_out=True)
    dh1 = _mm_nt_sharded(dp4, w["w_in"], tm, "mm_dh1", stacked_slot_fn=_slot_of_chip)
    grad_x, d_norm_pre_mix = _norm_bwd(dh1, dx1, x, small["norm_pre_mix"], tt)

    big = {
        "w_in": d_w_in,
        "w_branch_a": d_w_br_a.reshape(NCHIP, SH_BR, D),
        "w_branch_b": d_w_br_b.reshape(NCHIP, SH_BR, D),
        "w_out": d_w_out.reshape(NCHIP, SH_BR, D),
        "w_up": d_w_up,
        "w_down": d_w_down.reshape(NCHIP, SH_DN, D),
    }
    smalls = {
        "norm_pre_mix": d_norm_pre_mix, "conv_a_b": d_conv_a_b, "lru_ba": d_lru_ba, "lru_bx": d_lru_bx,
        "lru_lambda": d_lru_lambda, "hg_lb_logits": d_lb, "hg_norm_g": d_hg_norm_g, "norm_post_mix": d_norm_post_mix,
        "norm_pre_ffn": d_norm_pre_ffn, "norm_post_ffn": d_norm_post_ffn, "lossv": lossv,
        "conv_a_w": d_conv_a_w, "lru_wa": d_lru_wa, "lru_wx": d_lru_wx,
        "conv_f_b": d_conv_f_b, "conv_f_w": d_conv_f_w,
    }
    return grad_x, big, smalls


BIG = ("w_in", "w_branch_a", "w_branch_b", "w_out", "w_up", "w_down")
BIG_SHAPE = {"w_in": (D, SH_IN), "w_branch_a": (SH_BR, D), "w_branch_b": (SH_BR, D), "w_out": (SH_BR, D),
             "w_up": (D, SH_UP), "w_down": (SH_DN, D)}
NBIG = len(BIG)
VEC_ROWS = (("norm_pre_mix", 0, 1), ("conv_a_b", 1, 1), ("lru_ba", 2, 1), ("lru_bx", 3, 1), ("lru_lambda", 4, 1),
            ("hg_lb_logits", 5, 2), ("hg_norm_g", 7, 1), ("norm_post_mix", 8, 1), ("norm_pre_ffn", 9, 1),
            ("norm_post_ffn", 10, 1))
ROW_LOSS = 11
ROW_CONV_A = 12
S1_ROWS = 16
S2_ROWS = 8


def _place():
    x, y, c = lax.axis_index("x"), lax.axis_index("y"), lax.axis_index("c")
    chips = [(1 - x, y), (x, 1 - y), (1 - x, 1 - y)]
    return x, y, c, 2 * x + y, chips


def _remote(src, dst, ssem, rsem, dev):
    return pltpu.make_async_remote_copy(src_ref=src, dst_ref=dst, send_sem=ssem, recv_sem=rsem,
                                        device_id=dev, device_id_type=MESH)


def _hbm_call(body, name, ins, out_shapes, n_sems):
    any_spec = pl.BlockSpec(memory_space=pl.ANY)
    return pl.pallas_call(
        body, name=name, out_shape=tuple(out_shapes),
        in_specs=[any_spec] * len(ins), out_specs=tuple([any_spec] * len(out_shapes)),
        scratch_shapes=[pltpu.SemaphoreType.DMA((n,)) for n in n_sems],
        compiler_params=pltpu.CompilerParams(has_side_effects=True),
    )(*ins)


def _gather_weights(shards, conv_a_s, conv_f_s):
    ins = [shards[n] for n in BIG] + [conv_a_s, conv_f_s]
    n_in = len(ins)
    halves = [BIG_SHAPE[n][0] // 2 for n in BIG]
    out_shapes = [jax.ShapeDtypeStruct((NCHIP,) + a.shape, a.dtype) for a in ins]

    def body(*refs):
        src, dst = refs[:n_in], refs[n_in:2 * n_in]
        ssem, rsem, fssem, frsem, lsem = refs[2 * n_in:]
        x, y, c, j, chips = _place()

        def half(ref, w, which):
            return ref.at[pl.ds(which * halves[w], halves[w]), :]

        locs = [pltpu.make_async_copy(src[w], dst[w].at[j], lsem.at[w]) for w in range(n_in)]
        for cp in locs:
            cp.start()
        sends = []
        for w in range(n_in):
            for k, (cx, cy) in enumerate(chips):
                if w < NBIG:
                    cp = _remote(half(src[w], w, c), half(dst[w].at[j], w, c),
                                 ssem.at[3 * w + k], rsem.at[3 * w + k], (cx, cy, c))
                else:
                    cp = _remote(src[w], dst[w].at[j], ssem.at[3 * w + k], rsem.at[3 * w + k], (cx, cy, c))
                cp.start()
                sends.append(cp)
        fwds = []
        for w in range(n_in):
            for k, (cx, cy) in enumerate(chips):
                jk = 2 * cx + cy
                if w < NBIG:
                    got = half(dst[w].at[jk], w, c)
                    _remote(got, got, ssem.at[3 * w + k], rsem.at[3 * w + k], (cx, cy, c)).wait_recv()
                    cp = _remote(got, got, fssem.at[3 * w + k], frsem.at[3 * w + k], (x, y, 1 - c))
                    cp.start()
                    fwds.append(cp)
                else:
                    got = dst[w].at[jk]
                    _remote(got, got, ssem.at[3 * w + k], rsem.at[3 * w + k], (cx, cy, c)).wait_recv()
        for w in range(NBIG):
            for k, (cx, cy) in enumerate(chips):
                other = half(dst[w].at[2 * cx + cy], w, 1 - c)
                _remote(other, other, fssem.at[3 * w + k], frsem.at[3 * w + k], (x, y, 1 - c)).wait_recv()
        for cp in sends + fwds:
            cp.wait_send()
        for cp in locs:
            cp.wait()

    outs = _hbm_call(body, "gather_weights", ins, out_shapes, (3 * n_in, 3 * n_in, 3 * NBIG, 3 * NBIG, n_in))
    return dict(zip(BIG, outs[:NBIG])), outs[NBIG], outs[NBIG + 1]


def _reduce_stage1(big_g, s1, s2, s3):
    ins = [big_g[n] for n in BIG] + [s1, s2, s3]
    n_in = len(ins)
    halves = [BIG_SHAPE[n][0] // 2 for n in BIG]
    out_shapes = [jax.ShapeDtypeStruct((NCHIP, halves[w], BIG_SHAPE[n][1]), F32) for w, n in enumerate(BIG)]
    out_shapes += [jax.ShapeDtypeStruct(a.shape, F32) for a in (s1, s2, s3)]

    def body(*refs):
        src, dst = refs[:n_in], refs[n_in:2 * n_in]
        ssem, rsem = refs[2 * n_in:]
        x, y, c, _, _ = _place()
        cps = []
        for w in range(n_in):
            s_ = src[w].at[:, pl.ds((1 - c) * halves[w], halves[w]), :] if w < NBIG else src[w]
            cp = _remote(s_, dst[w], ssem.at[w], rsem.at[w], (x, y, 1 - c))
            cp.start()
            cps.append(cp)
        for cp in cps:
            cp.wait()

    outs = _hbm_call(body, "reduce_d2d_in", ins, out_shapes, (n_in, n_in))
    return dict(zip(BIG, outs[:NBIG])), outs[NBIG], outs[NBIG + 1], outs[NBIG + 2]


def _reduce_stage2(p_big, ps1, ps2, ps3):
    ins = [p_big[n] for n in BIG] + [ps1, ps2, ps3]
    n_in = len(ins)
    h1, h2, h3 = S1_ROWS // 2, DUP // 2, D
    out_shapes = [jax.ShapeDtypeStruct(p_big[n].shape, BF16) for n in BIG]
    out_shapes += [jax.ShapeDtypeStruct((NCHIP, h1, D), F32), jax.ShapeDtypeStruct((NCHIP, S2_ROWS, h2), F32),
                   jax.ShapeDtypeStruct((NCHIP, h3, HD), F32)]

    def body(*refs):
        src, dst = refs[:n_in], refs[n_in:2 * n_in]
        ssem, rsem, lsem = refs[2 * n_in:]
        x, y, c, j, chips = _place()

        def piece(w, to_chip):
            if w < NBIG:
                return src[w].at[to_chip]
            if w == NBIG:
                return src[w].at[pl.ds(c * h1, h1), :]
            if w == NBIG + 1:
                return src[w].at[:, pl.ds(c * h2, h2)]
            return src[w].at[pl.ds(c * h3, h3), :]

        locs = [pltpu.make_async_copy(piece(w, j), dst[w].at[j], lsem.at[w]) for w in range(n_in)]
        for cp in locs:
            cp.start()
        sends = []
        for w in range(n_in):
            for k, (cx, cy) in enumerate(chips):
                cp = _remote(piece(w, 2 * cx + cy), dst[w].at[j], ssem.at[3 * w + k], rsem.at[3 * w + k], (cx, cy, c))
                cp.start()
                sends.append(cp)
        for w in range(n_in):
            for k, (cx, cy) in enumerate(chips):
                got = dst[w].at[2 * cx + cy]
                _remote(got, got, ssem.at[3 * w + k], rsem.at[3 * w + k], (cx, cy, c)).wait_recv()
        for cp in sends:
            cp.wait_send()
        for cp in locs:
            cp.wait()

    outs = _hbm_call(body, "reduce_ici", ins, out_shapes, (3 * n_in, 3 * n_in, n_in))
    return dict(zip(BIG, outs[:NBIG])), outs[NBIG], outs[NBIG + 1], outs[NBIG + 2]


def _reduce_stage3(f_big, fs1, fs2, fs3):
    ins = [f_big[n] for n in BIG] + [fs1, fs2, fs3]
    n_in = len(ins)
    halves = [BIG_SHAPE[n][0] // 2 for n in BIG]
    h1, h2, h3 = S1_ROWS // 2, DUP // 2, D
    out_shapes = [jax.ShapeDtypeStruct(BIG_SHAPE[n], F32) for n in BIG]
    out_shapes += [jax.ShapeDtypeStruct((S1_ROWS, D), F32), jax.ShapeDtypeStruct((S2_ROWS, DUP), F32),
                   jax.ShapeDtypeStruct((2 * D, HD), F32)]

    def body(*refs):
        src, dst = refs[:n_in], refs[n_in:2 * n_in]
        ssem, rsem, lsem = refs[2 * n_in:]
        x, y, c, _, _ = _place()

        def place(w, which):
            if w < NBIG:
                return dst[w].at[pl.ds(which * halves[w], halves[w]), :]
            if w == NBIG:
                return dst[w].at[pl.ds(which * h1, h1), :]
            if w == NBIG + 1:
                return dst[w].at[:, pl.ds(which * h2, h2)]
            return dst[w].at[pl.ds(which * h3, h3), :]

        locs = [pltpu.make_async_copy(src[w], place(w, c), lsem.at[w]) for w in range(n_in)]
        cps = [_remote(src[w], place(w, c), ssem.at[w], rsem.at[w], (x, y, 1 - c)) for w in range(n_in)]
        for cp in locs + cps:
            cp.start()
        for w in range(n_in):
            got = place(w, 1 - c)
            _remote(got, got, ssem.at[w], rsem.at[w], (x, y, 1 - c)).wait_recv()
        for cp in cps:
            cp.wait_send()
        for cp in locs:
            cp.wait()

    outs = _hbm_call(body, "reduce_d2d_out", ins, out_shapes, (n_in, n_in, n_in))
    return dict(zip(BIG, outs[:NBIG])), outs[NBIG], outs[NBIG + 1], outs[NBIG + 2]


def _row_tile(rows):
    for tr in (128, 176, 64, 16, 8):
        if rows % tr == 0:
            return tr
    return rows


def _sum_own_half(g, rb, cidx, name):
    s, rows, cols = g.shape
    half = rows // 2
    tr = _row_tile(half)
    nb = half // tr

    def body(c_ref, g_ref, r_ref, o_ref):
        del c_ref
        o_ref[...] = (g_ref[...] + r_ref[...]).astype(BF16)

    grid_spec = pltpu.PrefetchScalarGridSpec(
        num_scalar_prefetch=1, grid=(s, nb),
        in_specs=[pl.BlockSpec((None, tr, cols), lambda k, i, c: (k, c[0] * nb + i, 0)),
                  pl.BlockSpec((None, tr, cols), lambda k, i, c: (k, i, 0))],
        out_specs=pl.BlockSpec((None, tr, cols), lambda k, i, c: (k, i, 0)))
    return pl.pallas_call(
        body, name=name, grid_spec=grid_spec, out_shape=jax.ShapeDtypeStruct((s, half, cols), BF16),
        compiler_params=_params(("parallel", "parallel")),
    )(cidx, g, rb)


def _sum_chips(q, name):
    s, rows, cols = q.shape
    tr = _row_tile(rows)

    def body(q_ref, o_ref):
        o_ref[...] = ((q_ref[0].astype(F32) + q_ref[1].astype(F32)) + q_ref[2].astype(F32)) + q_ref[3].astype(F32)

    return pl.pallas_call(
        body, name=name, out_shape=jax.ShapeDtypeStruct((rows, cols), F32), grid=(rows // tr,),
        in_specs=[pl.BlockSpec((s, tr, cols), lambda i: (0, i, 0))],
        out_specs=pl.BlockSpec((tr, cols), lambda i: (i, 0)),
        compiler_params=_params(("parallel",)),
    )(q)


def _add(a, b, name):
    def body(a_ref, b_ref, o_ref):
        o_ref[...] = a_ref[...] + b_ref[...]

    return pl.pallas_call(body, name=name, out_shape=jax.ShapeDtypeStruct(a.shape, F32))(a, b)


def _pack_small(sm):
    vec_in = [sm[n] for n, _, _ in VEC_ROWS]
    nv = len(vec_in)

    def body(*refs):
        ins, lossv, dcw, dcfb, dcfw, s1, s2 = refs[:nv], refs[nv], refs[nv + 1], refs[nv + 2], refs[nv + 3], \
            refs[nv + 4], refs[nv + 5]
        for ref, (_, r0, nr) in zip(ins, VEC_ROWS):
            s1[r0:r0 + nr, :] = ref[...]
        s1[ROW_LOSS:ROW_LOSS + 1, :] = lossv[...]
        s1[ROW_CONV_A:ROW_CONV_A + 4, :] = dcw[...]
        s2[0:1, :] = dcfb[...]
        s2[1:4, :] = dcfw[...]
        s2[4:8, :] = jnp.zeros((4, DUP), F32)

    return pl.pallas_call(
        body, name="pack_small",
        out_shape=(jax.ShapeDtypeStruct((S1_ROWS, D), F32), jax.ShapeDtypeStruct((S2_ROWS, DUP), F32)),
    )(*vec_in, sm["lossv"], sm["conv_a_w"], sm["conv_f_b"], sm["conv_f_w"])


def _adam_math(w, g, m, v):
    m = ADAM_B1 * m + (1.0 - ADAM_B1) * g
    v = ADAM_B2 * v + (1.0 - ADAM_B2) * (g * g)
    m_hat = m / (1.0 - ADAM_B1 ** ADAM_STEP)
    v_hat = v / (1.0 - ADAM_B2 ** ADAM_STEP)
    delta = -ADAM_LR * (m_hat / (jnp.sqrt(v_hat) + ADAM_EPS) + ADAM_WD * w)
    return delta, m, v


def _adam(w, g, m, v, name):
    rows, cols = w.shape
    tr = _row_tile(rows)

    def body(w_ref, g_ref, m_ref, v_ref, d_ref, mo_ref, vo_ref):
        d_ref[...], mo_ref[...], vo_ref[...] = _adam_math(w_ref[...], g_ref[...], m_ref[...], v_ref[...])

    spec = pl.BlockSpec((tr, cols), lambda i: (i, 0))
    return pl.pallas_call(
        body, name=name, out_shape=(jax.ShapeDtypeStruct(w.shape, F32),) * 3, grid=(rows // tr,),
        in_specs=[spec] * 4, out_specs=(spec,) * 3,
        compiler_params=_params(("parallel",)),
    )(w, g, m, v)


def _adam_small(gs1, gs2, gs3, w, m, v):
    names = [n for n, _, _ in VEC_ROWS] + ["conv_f_b", "lru_wa", "lru_wx"]
    nn = len(names)

    def grad_of(i, g1, g2, g3):
        if i < len(VEC_ROWS):
            _, r0, nr = VEC_ROWS[i]
            return g1[r0:r0 + nr, :]
        if names[i] == "conv_f_b":
            return g2[0:1, :]
        return g3[0] if names[i] == "lru_wa" else g3[1]

    def body(*refs):
        g1, g2, g3 = refs[0], refs[1], refs[2]
        ws, ms, vs = refs[3:3 + nn], refs[3 + nn:3 + 2 * nn], refs[3 + 2 * nn:3 + 3 * nn]
        outs = refs[3 + 3 * nn:]
        for i in range(nn):
            d, mn, vn = _adam_math(ws[i][...], grad_of(i, g1, g2, g3), ms[i][...], vs[i][...])
            outs[i][...] = d
            outs[nn + i][...] = mn
            outs[2 * nn + i][...] = vn

    shapes = [jax.ShapeDtypeStruct(w[n].shape, F32) for n in names]
    outs = pl.pallas_call(body, name="adam_small", out_shape=tuple(shapes * 3))(
        gs1, gs2, gs3, *[w[n] for n in names], *[m[n] for n in names], *[v[n] for n in names])
    return {n: (outs[i], outs[nn + i], outs[2 * nn + i]) for i, n in enumerate(names)}


WEIGHTS = ("norm_pre_mix", "w_in", "conv_a_w", "conv_a_b", "lru_wa", "lru_ba", "lru_wx", "lru_bx", "lru_lambda",
           "hg_lb_logits", "hg_norm_g", "w_branch_a", "w_branch_b", "w_out", "norm_post_mix", "norm_pre_ffn",
           "w_up", "conv_f_w", "conv_f_b", "w_down", "norm_post_ffn")
NW = len(WEIGHTS)


def kernel(x, norm_pre_mix, w_in, conv_a_w, conv_a_b, lru_wa, lru_ba, lru_wx, lru_bx, lru_lambda, hg_lb_logits, hg_norm_g, w_branch_a, w_branch_b, w_out, norm_post_mix, norm_pre_ffn, w_up, conv_f_w, conv_f_b, w_down, norm_post_ffn, loss_target, m_norm_pre_mix, m_w_in, m_conv_a_w, m_conv_a_b, m_lru_wa, m_lru_ba, m_lru_wx, m_lru_bx, m_lru_lambda, m_hg_lb_logits, m_hg_norm_g, m_w_branch_a, m_w_branch_b, m_w_out, m_norm_post_mix, m_norm_pre_ffn, m_w_up, m_conv_f_w, m_conv_f_b, m_w_down, m_norm_post_ffn, v_norm_pre_mix, v_w_in, v_conv_a_w, v_conv_a_b, v_lru_wa, v_lru_ba, v_lru_wx, v_lru_bx, v_lru_lambda, v_hg_lb_logits, v_hg_norm_g, v_w_branch_a, v_w_branch_b, v_w_out, v_norm_post_mix, v_norm_pre_ffn, v_w_up, v_conv_f_w, v_conv_f_b, v_w_down, v_norm_post_ffn):
    rest = (norm_pre_mix, w_in, conv_a_w, conv_a_b, lru_wa, lru_ba, lru_wx, lru_bx, lru_lambda, hg_lb_logits, hg_norm_g, w_branch_a, w_branch_b, w_out, norm_post_mix, norm_pre_ffn, w_up, conv_f_w, conv_f_b, w_down, norm_post_ffn, loss_target, m_norm_pre_mix, m_w_in, m_conv_a_w, m_conv_a_b, m_lru_wa, m_lru_ba, m_lru_wx, m_lru_bx, m_lru_lambda, m_hg_lb_logits, m_hg_norm_g, m_w_branch_a, m_w_branch_b, m_w_out, m_norm_post_mix, m_norm_pre_ffn, m_w_up, m_conv_f_w, m_conv_f_b, m_w_down, m_norm_post_ffn, v_norm_pre_mix, v_w_in, v_conv_a_w, v_conv_a_b, v_lru_wa, v_lru_ba, v_lru_wx, v_lru_bx, v_lru_lambda, v_hg_lb_logits, v_hg_norm_g, v_w_branch_a, v_w_branch_b, v_w_out, v_norm_post_mix, v_norm_pre_ffn, v_w_up, v_conv_f_w, v_conv_f_b, v_w_down, v_norm_post_ffn)
    w_in_args = dict(zip(WEIGHTS, rest[:NW]))
    loss_target = rest[NW]
    m_args = dict(zip(WEIGHTS, rest[NW + 1:2 * NW + 1]))
    v_args = dict(zip(WEIGHTS, rest[2 * NW + 1:3 * NW + 1]))
    shape_of = {n: w_in_args[n].shape for n in WEIGHTS}

    def two_d(n, a):
        if n in BIG:
            return a.reshape(BIG_SHAPE[n])
        if n in ("lru_wa", "lru_wx"):
            return a.reshape(NH, HD, HD)
        return a.reshape(a.shape[-2:])

    w2 = {n: two_d(n, w_in_args[n]) for n in WEIGHTS}
    m2 = {n: two_d(n, m_args[n]) for n in WEIGHTS}
    v2 = {n: two_d(n, v_args[n]) for n in WEIGHTS}

    cidx = lax.axis_index("c").astype(jnp.int32).reshape(1)
    jchip = 2 * lax.axis_index("x") + lax.axis_index("y")

    shards = {n: w2[n].astype(BF16) for n in BIG}
    conv_a_s = jnp.pad(w2["conv_a_w"], ((0, 4), (0, 0)))
    conv_f_s = jnp.pad(w2["conv_f_w"], ((0, 5), (0, 0)))
    w_full, conv_a_g, conv_f_g = _gather_weights(shards, conv_a_s, conv_f_s)
    w_full["conv_a_w"] = jnp.transpose(conv_a_g, (1, 0, 2)).reshape(8, D)[0:4]
    w_full["conv_f_w"] = jnp.transpose(conv_f_g, (1, 0, 2)).reshape(8, DUP)[0:3]
    small = {n: w2[n] for n in WEIGHTS if n not in BIG and n not in ("conv_a_w", "conv_f_w")}

    grad_x, big_g, sm_g = _local_step(x[0], loss_target[0], w_full, small)

    s1, s2 = _pack_small(sm_g)
    s3 = jnp.concatenate([sm_g["lru_wa"].reshape(D, HD), sm_g["lru_wx"].reshape(D, HD)], axis=0)
    rb, rs1, rs2, rs3 = _reduce_stage1(big_g, s1, s2, s3)
    p_big = {n: _sum_own_half(big_g[n], rb[n], cidx, "sum_half_" + n) for n in BIG}
    ps1, ps2, ps3 = _add(s1, rs1, "add_s1"), _add(s2, rs2, "add_s2"), _add(s3, rs3, "add_s3")
    q_big, qs1, qs2, qs3 = _reduce_stage2(p_big, ps1, ps2, ps3)
    f_big = {n: _sum_chips(q_big[n], "sum_chips_" + n) for n in BIG}
    fs1, fs2, fs3 = _sum_chips(qs1, "sum_chips_s1"), _sum_chips(qs2, "sum_chips_s2"), _sum_chips(qs3, "sum_chips_s3")
    g_big, gs1, gs2, gs3 = _reduce_stage3(f_big, fs1, fs2, fs3)

    res = {}
    for n in BIG:
        d, mn, vn = _adam(w2[n], g_big[n], m2[n], v2[n], "adam_" + n)
        res[n] = (g_big[n], d, mn, vn)
    small_res = _adam_small(gs1, gs2, gs3.reshape(2, NH, HD, HD), w2, m2, v2)
    for n, r0, nr in VEC_ROWS:
        res[n] = (gs1[r0:r0 + nr],) + small_res[n]
    res["conv_f_b"] = (gs2[0:1],) + small_res["conv_f_b"]
    res["lru_wa"] = (gs3[0:D].reshape(NH, HD, HD),) + small_res["lru_wa"]
    res["lru_wx"] = (gs3[D:2 * D].reshape(NH, HD, HD),) + small_res["lru_wx"]
    g_ca = lax.dynamic_slice_in_dim(gs1[ROW_CONV_A:ROW_CONV_A + 4], jchip * (D // NCHIP), D // NCHIP, axis=1)
    g_cf = lax.dynamic_slice_in_dim(gs2[1:4], jchip * SH_UP, SH_UP, axis=1)
    res["conv_a_w"] = (g_ca,) + _adam(w2["conv_a_w"], g_ca, m2["conv_a_w"], v2["conv_a_w"], "adam_conv_a_w")
    res["conv_f_w"] = (g_cf,) + _adam(w2["conv_f_w"], g_cf, m2["conv_f_w"], v2["conv_f_w"], "adam_conv_f_w")

    loss = (0.5 / D) * jnp.sum(gs1[ROW_LOSS])
    out = [loss, grad_x.reshape(x.shape)]
    for part in range(4):
        out += [res[n][part].reshape(shape_of[n]) for n in WEIGHTS]
    return tuple(out)
```

```python
import functools

import jax
import jax.numpy as jnp
from jax import lax
from jax.experimental import pallas as pl
from jax.experimental.pallas import tpu as pltpu

F32 = jnp.float32
BF16 = jnp.bfloat16

D = 1024
NH = 8
HD = 128
CH = 32
DFF = 2816
DUP = 2 * DFF
NCHIP = 4
SH_IN = 2 * D
SH_UP = DUP // NCHIP
SH_DN = DFF // NCHIP
SH_BR = D // NCHIP
EPS = 1e-6
LRU_C = 8.0
ADAM_LR = 0.001
ADAM_B1 = 0.9
ADAM_B2 = 0.999
ADAM_EPS = 1e-08
ADAM_WD = 0.01
ADAM_STEP = 10
VMEM_BIG = 56 * 1024 * 1024
MESH = pl.DeviceIdType.MESH

SLOT_A, SLOT_B, SLOT_C, SLOT_G = 2, 0, 1, 3


def _slot_of_chip(s):
    return jnp.where(s == 3, 3, (s + 2) % 3)


def _params(sem, vmem=None):
    return pltpu.CompilerParams(dimension_semantics=sem, vmem_limit_bytes=vmem)


_GC = 0.7978845608028654
_GA = 0.044715


def _gelu(x):
    return 0.5 * x * (1.0 + jnp.tanh(_GC * (x + _GA * x * x * x)))


def _gelu_and_grad(x):
    x2 = x * x
    th = jnp.tanh(_GC * x * (1.0 + _GA * x2))
    g = 0.5 * x * (1.0 + th)
    dg = 0.5 * (1.0 + th) + 0.5 * x * (1.0 - th * th) * _GC * (1.0 + 3.0 * _GA * x2)
    return g, dg


def _sig(x):
    return jax.nn.sigmoid(x)


def _dot(a, b):
    return jnp.dot(a, b, preferred_element_type=F32)


def _dot_nt(a, b):
    return lax.dot_general(a, b, (((1,), (1,)), ((), ())), preferred_element_type=F32)


def _dot_tn(a, b):
    return lax.dot_general(a, b, (((0,), (0,)), ((), ())), preferred_element_type=F32)


def _chunk_cumsum(x):
    pos = lax.broadcasted_iota(jnp.int32, (x.shape[0], 1), 0) & (CH - 1)
    d = 1
    while d < CH:
        x = x + jnp.where(pos >= d, pltpu.roll(x, d, 0), 0.0)
        d *= 2
    return x


def _chunk_revcumsum(x):
    n = x.shape[0]
    pos = lax.broadcasted_iota(jnp.int32, (n, 1), 0) & (CH - 1)
    d = 1
    while d < CH:
        x = x + jnp.where(pos < CH - d, pltpu.roll(x, n - d, 0), 0.0)
        d *= 2
    return x


def _chunk_last(x):
    n = x.shape[0]
    return jnp.concatenate(
        [jnp.broadcast_to(x[c * CH + CH - 1:c * CH + CH, :], (CH, x.shape[1])) for c in range(n // CH)], axis=0)


def _chunk_total(x):
    n = x.shape[0]
    return jnp.concatenate(
        [jnp.broadcast_to(jnp.sum(x[c * CH:(c + 1) * CH, :], axis=0, keepdims=True), (CH, x.shape[1]))
         for c in range(n // CH)], axis=0)


def _rms_stats(x):
    r = lax.rsqrt(jnp.mean(x * x, axis=-1, keepdims=True) + EPS)
    return r, x * r


def _rms_bwd(gd, n, r):
    return r * (gd - n * jnp.mean(gd * n, axis=-1, keepdims=True))


def _shift_rows(x, d, fill):
    rows = lax.broadcasted_iota(jnp.int32, (x.shape[0], 1), 0)
    return jnp.where(rows >= d, pltpu.roll(x, d, 0), fill)


def _shift_rows_up(x, d, fill):
    n = x.shape[0]
    rows = lax.broadcasted_iota(jnp.int32, (n, 1), 0)
    return jnp.where(rows < n - d, pltpu.roll(x, n - d, 0), fill)


def _mm_nn_sharded(a, b3, out_dtype, tm, name, slot_fn=None):
    m, k = a.shape
    s, _, ns = b3.shape

    def body(a_ref, b_ref, o_ref):
        o_ref[...] = _dot(a_ref[...], b_ref[...]).astype(out_dtype)

    if slot_fn is None:
        out_shape = jax.ShapeDtypeStruct((m, s * ns), out_dtype)
        out_spec = pl.BlockSpec((tm, ns), lambda j, i: (i, j))
    else:
        out_shape = jax.ShapeDtypeStruct((s, m, ns), out_dtype)
        out_spec = pl.BlockSpec((None, tm, ns), lambda j, i: (slot_fn(j), i, 0))
    return pl.pallas_call(
        body, name=name, out_shape=out_shape, grid=(s, m // tm),
        in_specs=[pl.BlockSpec((tm, k), lambda j, i: (i, 0)),
                  pl.BlockSpec((None, k, ns), lambda j, i: (j, 0, 0))],
        out_specs=out_spec,
        compiler_params=_params(("parallel", "parallel"), VMEM_BIG),
    )(a, b3)


def _mm_nt_sharded(a, b3, tm, name, stacked_slot_fn=None):
    s, k, ns = b3.shape
    m = a.shape[1] if stacked_slot_fn is not None else a.shape[0]

    def body(a_ref, b_ref, o_ref, acc_ref):
        j = pl.program_id(1)
        t = _dot_nt(a_ref[...], b_ref[...])

        @pl.when(j == 0)
        def _():
            acc_ref[...] = t

        @pl.when(j > 0)
        def _():
            acc_ref[...] += t

        @pl.when(j == s - 1)
        def _():
            o_ref[...] = acc_ref[...]

    if stacked_slot_fn is None:
        a_spec = pl.BlockSpec((tm, ns), lambda i, j: (i, j))
    else:
        a_spec = pl.BlockSpec((None, tm, ns), lambda i, j: (stacked_slot_fn(j), i, 0))
    return pl.pallas_call(
        body, name=name, out_shape=jax.ShapeDtypeStruct((m, k), F32), grid=(m // tm, s),
        in_specs=[a_spec, pl.BlockSpec((None, k, ns), lambda i, j: (j, 0, 0))],
        out_specs=pl.BlockSpec((tm, k), lambda i, j: (i, 0)),
        scratch_shapes=[pltpu.VMEM((tm, k), F32)],
        compiler_params=_params(("parallel", "arbitrary"), VMEM_BIG),
    )(a, b3)


def _mm_tn(a, g, tkk, tn, tk, name, stacked_slot_fn=None, stacked_out=False):
    m, k = a.shape
    if stacked_slot_fn is not None:
        n = g.shape[0] * g.shape[2]
        g_spec = pl.BlockSpec((None, tk, tn), lambda kk, j, mm: (stacked_slot_fn(j), mm, 0))
    else:
        n = g.shape[1]
        g_spec = pl.BlockSpec((tk, tn), lambda kk, j, mm: (mm, j))

    def body(a_ref, g_ref, o_ref):
        mm = pl.program_id(2)
        t = _dot_tn(a_ref[...], g_ref[...])

        @pl.when(mm == 0)
        def _():
            o_ref[...] = t

        @pl.when(mm > 0)
        def _():
            o_ref[...] += t

    if stacked_out:
        out_shape = jax.ShapeDtypeStruct((n // tn, k, tn), F32)
        out_spec = pl.BlockSpec((None, tkk, tn), lambda kk, j, mm: (j, kk, 0))
    else:
        out_shape = jax.ShapeDtypeStruct((k, n), F32)
        out_spec = pl.BlockSpec((tkk, tn), lambda kk, j, mm: (kk, j))
    return pl.pallas_call(
        body, name=name, out_shape=out_shape, grid=(k // tkk, n // tn, m // tk),
        in_specs=[pl.BlockSpec((tk, tkk), lambda kk, j, mm: (mm, kk)), g_spec],
        out_specs=out_spec,
        compiler_params=_params(("parallel", "parallel", "arbitrary"), VMEM_BIG),
    )(a, g)


def _norm_fwd(x, gain, tt):
    t = x.shape[0]

    def body(x_ref, g_ref, h_ref):
        _, n = _rms_stats(x_ref[...])
        h_ref[...] = (n * g_ref[...]).astype(BF16)

    return pl.pallas_call(
        body, name="norm_fwd", out_shape=jax.ShapeDtypeStruct((t, D), BF16), grid=(t // tt,),
        in_specs=[pl.BlockSpec((tt, D), lambda i: (i, 0)), pl.BlockSpec((1, D), lambda i: (0, 0))],
        out_specs=pl.BlockSpec((tt, D), lambda i: (i, 0)),
        compiler_params=_params(("parallel",)),
    )(x, gain)


def _lru_gates(xc, wa_ref, wx_ref, ba, bx, lam):
    xcb = xc.astype(BF16)
    ra = jnp.concatenate([_dot(xcb[:, n * HD:(n + 1) * HD], wa_ref[n]) for n in range(NH)], axis=1) + ba
    ix = jnp.concatenate([_dot(xcb[:, n * HD:(n + 1) * HD], wx_ref[n]) for n in range(NH)], axis=1) + bx
    r = _sig(ra)
    ig = _sig(ix)
    z = -lam
    sp = jnp.maximum(z, 0.0) + jnp.log1p(jnp.exp(-jnp.abs(z)))
    log_a = -LRU_C * r * sp
    a = jnp.exp(log_a)
    z2 = 2.0 * log_a
    series = -z2 * (1.0 + z2 * (0.5 + z2 * (1.0 / 6.0 + z2 * (1.0 / 24.0))))
    om = jnp.where(z2 > -0.02, series, 1.0 - jnp.exp(z2))
    mult = jnp.sqrt(om)
    return xcb, r, ig, sp, a, mult


def _mixer_a_fwd(p4, cw, cb, wa, wx, ba, bx, lam, tt):
    t = p4.shape[1]
    steps = tt.bit_length() - 1

    def body(p_ref, cw_ref, cb_ref, wa_ref, wx_ref, ba_ref, bx_ref, lam_ref, ya_ref, h_ref, ext, hc):
        i = pl.program_id(0)

        @pl.when(i == 0)
        def _():
            ext[0:8, :] = jnp.zeros((8, D), F32)
            hc[...] = jnp.zeros((1, D), F32)

        xa = p_ref[:, 0:D]
        ga = p_ref[:, D:2 * D]
        ext[8:, :] = xa
        xc = cb_ref[...] + sum(cw_ref[k:k + 1, :] * ext[pl.ds(5 + k, tt), :] for k in range(4))
        ext[0:8, :] = xa[tt - 8:, :]
        _, _, ig, _, a, mult = _lru_gates(xc, wa_ref, wx_ref, ba_ref[...], bx_ref[...], lam_ref[...])
        u = mult * ig * xc
        acc_a, acc_u = a, u
        for lvl in range(steps):
            d = 1 << lvl
            acc_u = acc_a * _shift_rows(acc_u, d, 0.0) + acc_u
            acc_a = acc_a * _shift_rows(acc_a, d, 1.0)
        h = acc_a * hc[...] + acc_u
        hc[...] = h[tt - 1:tt, :]
        h_ref[...] = h
        ya_ref[...] = (h * _gelu(ga)).astype(BF16)

    full = lambda shape: pl.BlockSpec(shape, lambda i: (0,) * len(shape))
    return pl.pallas_call(
        body, name="mixer_a_fwd",
        out_shape=(jax.ShapeDtypeStruct((t, D), BF16), jax.ShapeDtypeStruct((t, D), F32)),
        grid=(t // tt,),
        in_specs=[pl.BlockSpec((None, tt, 2 * D), lambda i: (SLOT_A, i, 0)),
                  full((4, D)), full((1, D)), full((NH, HD, HD)), full((NH, HD, HD)),
                  full((1, D)), full((1, D)), full((1, D))],
        out_specs=(pl.BlockSpec((tt, D), lambda i: (i, 0)), pl.BlockSpec((tt, D), lambda i: (i, 0))),
        scratch_shapes=[pltpu.VMEM((tt + 8, D), F32), pltpu.VMEM((1, D), F32)],
        compiler_params=_params(("arbitrary",), VMEM_BIG),
    )(p4, cw, cb, wa, wx, ba, bx, lam)


def _chunk_masks(tt):
    row = lax.broadcasted_iota(jnp.int32, (tt, tt), 0)
    col = lax.broadcasted_iota(jnp.int32, (tt, tt), 1)
    same = jnp.right_shift(row, 5) == jnp.right_shift(col, 5)
    return same & (col <= row)


def _hg_head_fwd(q, fz, lbh):
    sg = _sig(fz)
    sgn = _sig(-fz)
    f = lbh + (1.0 - lbh) * sg
    logf = jnp.log(f)
    k = (1.0 - lbh) * sgn
    g = _chunk_cumsum(logf)
    gu = _chunk_last(g) - g
    eg = jnp.exp(g)
    eng = jnp.exp(-g)
    egu = jnp.exp(gu)
    qt = q * eg
    kt = k * eng
    kd = k * egu
    return sg, sgn, f, k, g, eg, eng, egu, qt, kt, kd


def _lb_of(logits_ref):
    return _sig(logits_ref[0:1, :] - logits_ref[1:2, :])


def _hgrn2_fwd(p4, logits, gnorm, tt):
    t = p4.shape[1]
    nc = tt // CH

    def body(p_ref, lg_ref, gn_ref, yb_ref, o_ref, ss_ref, st):
        i = pl.program_id(0)

        @pl.when(i == 0)
        def _():
            st[...] = jnp.zeros((NH, HD, HD), F32)

        low = _chunk_masks(tt)
        lb = _lb_of(lg_ref)
        for h in range(NH):
            c0 = h * HD
            q = p_ref[0, :, c0:c0 + HD]
            fz = p_ref[0, :, D + c0:D + c0 + HD]
            v = p_ref[1, :, c0:c0 + HD]
            og = p_ref[1, :, D + c0:D + c0 + HD]
            lbh = lb[:, c0:c0 + HD]
            _, _, _, _, g, _, _, _, qt, kt, kd = _hg_head_fwd(q, fz, lbh)
            qtb, ktb, kdb, vb = qt.astype(BF16), kt.astype(BF16), kd.astype(BF16), v.astype(BF16)
            att = jnp.where(low, _dot_nt(qtb, ktb), 0.0)
            o_in = _dot(att.astype(BF16), vb)
            s_t = st[h]
            pieces = []
            for c in range(nc):
                sl = slice(c * CH, (c + 1) * CH)
                s_bf = s_t.astype(BF16)
                ss_ref[c, h] = s_bf
                pieces.append(o_in[sl] + _dot_nt(qtb[sl], s_bf))
                dec = jnp.exp(g[c * CH + CH - 1:c * CH + CH, :])
                s_t = s_t * dec + _dot_tn(vb[sl], kdb[sl])
            st[h] = s_t
            o = jnp.concatenate(pieces, axis=0)
            r, n = _rms_stats(o)
            ob = n * gn_ref[:, c0:c0 + HD]
            o_ref[:, c0:c0 + HD] = o
            yb_ref[:, c0:c0 + HD] = (ob * (og * _sig(og))).astype(BF16)

    return pl.pallas_call(
        body, name="hgrn2_fwd",
        out_shape=(jax.ShapeDtypeStruct((t, D), BF16), jax.ShapeDtypeStruct((t, D), F32),
                   jax.ShapeDtypeStruct((t // CH, NH, HD, HD), BF16)),
        grid=(t // tt,),
        in_specs=[pl.BlockSpec((2, tt, 2 * D), lambda i: (0, i, 0)),
                  pl.BlockSpec((2, D), lambda i: (0, 0)), pl.BlockSpec((1, D), lambda i: (0, 0))],
        out_specs=(pl.BlockSpec((tt, D), lambda i: (i, 0)), pl.BlockSpec((tt, D), lambda i: (i, 0)),
                   pl.BlockSpec((nc, NH, HD, HD), lambda i: (i, 0, 0, 0))),
        scratch_shapes=[pltpu.VMEM((NH, HD, HD), F32)],
        compiler_params=_params(("arbitrary",), VMEM_BIG),
    )(p4, logits, gnorm)


def _mid_fwd(ya, yb, p4, x, wa, wb, wo, g_pm, g_pf, tt):
    t = x.shape[0]

    def body(ya_ref, yb_ref, gt_ref, x_ref, wa_ref, wb_ref, wo_ref, gpm_ref, gpf_ref,
             za_ref, zb_ref, mix_ref, m2_ref, x1_ref, h2_ref):
        za = _dot(ya_ref[...], wa_ref[...])
        zb = _dot(yb_ref[...], wb_ref[...])
        mix = _sig(gt_ref[:, 0:D]) * za + _sig(gt_ref[:, D:2 * D]) * zb
        mixb = mix.astype(BF16)
        m2 = _dot(mixb, wo_ref[...])
        _, n2 = _rms_stats(m2)
        x1 = x_ref[...] + n2 * gpm_ref[...]
        _, n1 = _rms_stats(x1)
        za_ref[...] = za.astype(BF16)
        zb_ref[...] = zb.astype(BF16)
        mix_ref[...] = mixb
        m2_ref[...] = m2
        x1_ref[...] = x1
        h2_ref[...] = (n1 * gpf_ref[...]).astype(BF16)

    row = lambda dt: jax.ShapeDtypeStruct((t, D), dt)
    tile = pl.BlockSpec((tt, D), lambda i: (i, 0))
    wsp = pl.BlockSpec((D, D), lambda i: (0, 0))
    vec = pl.BlockSpec((1, D), lambda i: (0, 0))
    return pl.pallas_call(
        body, name="mid_fwd",
        out_shape=(row(BF16), row(BF16), row(BF16), row(F32), row(F32), row(BF16)),
        grid=(t // tt,),
        in_specs=[tile, tile, pl.BlockSpec((None, tt, 2 * D), lambda i: (SLOT_G, i, 0)), tile,
                  wsp, wsp, wsp, vec, vec],
        out_specs=(tile,) * 6,
        compiler_params=_params(("parallel",), VMEM_BIG),
    )(ya, yb, p4, x, wa, wb, wo, g_pm, g_pf)


def _ffn_act_fwd(up_pre, cfw, cfb, tt):
    t = up_pre.shape[0]

    def body(u_ref, halo_ref, w_ref, b_ref, y_ref):
        i = pl.program_id(0)
        halves = []
        for c0 in (0, DFF):
            cs = slice(c0, c0 + DFF)
            xe = jnp.concatenate([jnp.where(i > 0, halo_ref[:, cs], 0.0), u_ref[:, cs]], axis=0)
            up = (b_ref[:, cs] + w_ref[2:3, cs] * xe + w_ref[1:2, cs] * pltpu.roll(xe, 1, 0)
                  + w_ref[0:1, cs] * pltpu.roll(xe, 2, 0))
            halves.append(up[8:, :])
        y_ref[...] = (_gelu(halves[0]) * halves[1]).astype(BF16)

    hb = tt // 8
    return pl.pallas_call(
        body, name="ffn_act_fwd", out_shape=jax.ShapeDtypeStruct((t, DFF), BF16), grid=(t // tt,),
        in_specs=[pl.BlockSpec((tt, DUP), lambda i: (i, 0)),
                  pl.BlockSpec((8, DUP), lambda i: (jnp.maximum(i * hb - 1, 0), 0)),
                  pl.BlockSpec((3, DUP), lambda i: (0, 0)), pl.BlockSpec((1, DUP), lambda i: (0, 0))],
        out_specs=pl.BlockSpec((tt, DFF), lambda i: (i, 0)),
        compiler_params=_params(("parallel",), VMEM_BIG),
    )(up_pre, up_pre, cfw, cfb)


def _down_loss(y, wdn, x1, tgt, g_post, tt):
    t = x1.shape[0]

    def body(y_ref, w_ref, x1_ref, t_ref, g_ref, dx2_ref, dm3_ref, lossv_ref, dg_ref):
        i = pl.program_id(0)
        m3 = _dot(y_ref[...], w_ref[...])
        r, n3 = _rms_stats(m3)
        g = g_ref[...]
        e = x1_ref[...] + n3 * g - t_ref[...]
        dx2 = e * (1.0 / D)
        dx2_ref[...] = dx2
        dm3_ref[...] = _rms_bwd(dx2 * g, n3, r).astype(BF16)
        lv = jnp.sum(e * e, axis=0, keepdims=True)
        dgv = jnp.sum(dx2 * n3, axis=0, keepdims=True)

        @pl.when(i == 0)
        def _():
            lossv_ref[...] = lv
            dg_ref[...] = dgv

        @pl.when(i > 0)
        def _():
            lossv_ref[...] += lv
            dg_ref[...] += dgv

    tile = pl.BlockSpec((tt, D), lambda i: (i, 0))
    vec = pl.BlockSpec((1, D), lambda i: (0, 0))
    return pl.pallas_call(
        body, name="down_loss",
        out_shape=(jax.ShapeDtypeStruct((t, D), F32), jax.ShapeDtypeStruct((t, D), BF16),
                   jax.ShapeDtypeStruct((1, D), F32), jax.ShapeDtypeStruct((1, D), F32)),
        grid=(t // tt,),
        in_specs=[pl.BlockSpec((tt, DFF), lambda i: (i, 0)), pl.BlockSpec((DFF, D), lambda i: (0, 0)),
                  tile, tile, vec],
        out_specs=(tile, tile, vec, vec),
        compiler_params=_params(("arbitrary",), VMEM_BIG),
    )(y, wdn, x1, tgt, g_post)


def _ffn_act_bwd(dm3, wdn, up_pre, cfw, cfb, tt):
    t = up_pre.shape[0]
    nt = t // tt

    def body(dm_ref, dmn_ref, w_ref, u_ref, up_ref, un_ref, cw_ref, cb_ref, du_ref, dcw_ref, dcb_ref):
        i = pl.program_id(0)
        n = tt + 8
        next_live = jnp.where(i < nt - 1, 1.0, 0.0)
        dy = jnp.concatenate([_dot_nt(dm_ref[...], w_ref[...]),
                              _dot_nt(dmn_ref[...], w_ref[...])[0:8, :] * next_live], axis=0)
        ups, xs = [], []
        for c0 in (0, DFF):
            cs = slice(c0, c0 + DFF)
            xe = jnp.concatenate([jnp.where(i > 0, up_ref[:, cs], 0.0), u_ref[:, cs], un_ref[:, cs]], axis=0)
            x1 = pltpu.roll(xe, 1, 0)
            x2 = pltpu.roll(xe, 2, 0)
            up = cb_ref[:, cs] + cw_ref[2:3, cs] * xe + cw_ref[1:2, cs] * x1 + cw_ref[0:1, cs] * x2
            ups.append(up[8:, :])
            xs.append((x2[8:tt + 8, :], x1[8:tt + 8, :], xe[8:tt + 8, :]))
        gl, dgl = _gelu_and_grad(ups[0])
        ds = (dy * ups[1] * dgl, dy * gl)
        dcw_parts, dcb_parts = [], []
        for hh, c0 in enumerate((0, DFF)):
            cs = slice(c0, c0 + DFF)
            dd = ds[hh]
            du = cw_ref[2:3, cs] * dd + cw_ref[1:2, cs] * pltpu.roll(dd, n - 1, 0) \
                + cw_ref[0:1, cs] * pltpu.roll(dd, n - 2, 0)
            du_ref[:, cs] = du[0:tt, :].astype(BF16)
            dup = dd[0:tt, :]
            dcw_parts.append(jnp.concatenate(
                [jnp.sum(dup * xs[hh][k], axis=0, keepdims=True) for k in range(3)], axis=0))
            dcb_parts.append(jnp.sum(dup, axis=0, keepdims=True))
        dcw = jnp.concatenate(dcw_parts, axis=1)
        dcb = jnp.concatenate(dcb_parts, axis=1)

        @pl.when(i == 0)
        def _():
            dcw_ref[...] = dcw
            dcb_ref[...] = dcb

        @pl.when(i > 0)
        def _():
            dcw_ref[...] += dcw
            dcb_ref[...] += dcb

    hb = tt // 8
    last8 = t // 8 - 1
    return pl.pallas_call(
        body, name="ffn_act_bwd",
        out_shape=(jax.ShapeDtypeStruct((t, DUP), BF16), jax.ShapeDtypeStruct((3, DUP), F32),
                   jax.ShapeDtypeStruct((1, DUP), F32)),
        grid=(nt,),
        in_specs=[pl.BlockSpec((tt, D), lambda i: (i, 0)),
                  pl.BlockSpec((16, D), lambda i: (jnp.minimum((i + 1) * (tt // 16), t // 16 - 1), 0)),
                  pl.BlockSpec((DFF, D), lambda i: (0, 0)),
                  pl.BlockSpec((tt, DUP), lambda i: (i, 0)),
                  pl.BlockSpec((8, DUP), lambda i: (jnp.maximum(i * hb - 1, 0), 0)),
                  pl.BlockSpec((8, DUP), lambda i: (jnp.minimum((i + 1) * hb, last8), 0)),
                  pl.BlockSpec((3, DUP), lambda i: (0, 0)), pl.BlockSpec((1, DUP), lambda i: (0, 0))],
        out_specs=(pl.BlockSpec((tt, DUP), lambda i: (i, 0)), pl.BlockSpec((3, DUP), lambda i: (0, 0)),
                   pl.BlockSpec((1, DUP), lambda i: (0, 0))),
        compiler_params=_params(("arbitrary",), VMEM_BIG),
    )(dm3, dm3, wdn, up_pre, up_pre, up_pre, cfw, cfb)


def _mid_bwd(dh2, dx2, x1, m2, za, zb, p4, wa, wb, wo, g_pm, g_pf, tt):
    t = x1.shape[0]

    def body(dh2_ref, dx2_ref, x1_ref, m2_ref, za_ref, zb_ref, gt_ref, wa_ref, wb_ref, wo_ref, gpm_ref, gpf_ref,
             dx1_ref, dm2_ref, dza_ref, dzb_ref, dya_ref, dyb_ref, dp_ref, dgpm_ref, dgpf_ref):
        i = pl.program_id(0)
        r1, n1 = _rms_stats(x1_ref[...])
        dh2 = dh2_ref[...]
        dx1 = dx2_ref[...] + _rms_bwd(dh2 * gpf_ref[...], n1, r1)
        r2, n2 = _rms_stats(m2_ref[...])
        dm2 = _rms_bwd(dx1 * gpm_ref[...], n2, r2).astype(BF16)
        dmix = _dot_nt(dm2, wo_ref[...])
        sa = _sig(gt_ref[:, 0:D])
        sb = _sig(gt_ref[:, D:2 * D])
        dza = (dmix * sa).astype(BF16)
        dzb = (dmix * sb).astype(BF16)
        dp_ref[:, 0:D] = (dmix * za_ref[...].astype(F32) * sa * (1.0 - sa)).astype(BF16)
        dp_ref[:, D:2 * D] = (dmix * zb_ref[...].astype(F32) * sb * (1.0 - sb)).astype(BF16)
        dx1_ref[...] = dx1
        dm2_ref[...] = dm2
        dza_ref[...] = dza
        dzb_ref[...] = dzb
        dya_ref[...] = _dot_nt(dza, wa_ref[...])
        dyb_ref[...] = _dot_nt(dzb, wb_ref[...])
        dgpf = jnp.sum(dh2 * n1, axis=0, keepdims=True)
        dgpm = jnp.sum(dx1 * n2, axis=0, keepdims=True)

        @pl.when(i == 0)
        def _():
            dgpf_ref[...] = dgpf
            dgpm_ref[...] = dgpm

        @pl.when(i > 0)
        def _():
            dgpf_ref[...] += dgpf
            dgpm_ref[...] += dgpm

    row = lambda dt: jax.ShapeDtypeStruct((t, D), dt)
    tile = pl.BlockSpec((tt, D), lambda i: (i, 0))
    wsp = pl.BlockSpec((D, D), lambda i: (0, 0))
    vec = pl.BlockSpec((1, D), lambda i: (0, 0))
    gates = pl.BlockSpec((None, tt, 2 * D), lambda i: (SLOT_G, i, 0))
    return pl.pallas_call(
        body, name="mid_bwd",
        out_shape=(row(F32), row(BF16), row(BF16), row(BF16), row(F32), row(F32),
                   jax.ShapeDtypeStruct((NCHIP, t, 2 * D), BF16),
                   jax.ShapeDtypeStruct((1, D), F32), jax.ShapeDtypeStruct((1, D), F32)),
        grid=(t // tt,),
        in_specs=[tile, tile, tile, tile, tile, tile, gates, wsp, wsp, wsp, vec, vec],
        out_specs=(tile, tile, tile, tile, tile, tile, gates, vec, vec),
        compiler_params=_params(("arbitrary",), VMEM_BIG),
    )(dh2, dx2, x1, m2, za, zb, p4, wa, wb, wo, g_pm, g_pf)


def _hgrn2_bwd(p4, o_all, ss, dyb, dp4, logits, gnorm, tt):
    t = p4.shape[1]
    nt = t // tt
    nc = tt // CH

    def body(p_ref, o_ref, ss_ref, dyb_ref, dp_in, lg_ref, gn_ref, dp_ref, dlb_ref, dgn_ref, dst):
        del dp_in
        i = pl.program_id(0)

        @pl.when(i == 0)
        def _():
            dst[...] = jnp.zeros((NH, HD, HD), F32)

        low = _chunk_masks(tt)
        lb = _lb_of(lg_ref)
        dlb_parts = []
        dgn_parts = []
        for h in range(NH):
            c0 = h * HD
            q = p_ref[0, :, c0:c0 + HD]
            fz = p_ref[0, :, D + c0:D + c0 + HD]
            v = p_ref[1, :, c0:c0 + HD]
            og = p_ref[1, :, D + c0:D + c0 + HD]
            lbh = lb[:, c0:c0 + HD]
            gh = gn_ref[:, c0:c0 + HD]
            sg, sgn, f, k, g, eg, eng, egu, qt, kt, kd = _hg_head_fwd(q, fz, lbh)
            qtb, ktb, kdb, vb = qt.astype(BF16), kt.astype(BF16), kd.astype(BF16), v.astype(BF16)
            att = jnp.where(low, _dot_nt(qtb, ktb), 0.0).astype(BF16)
            o = o_ref[:, c0:c0 + HD]
            dyb_h = dyb_ref[:, c0:c0 + HD]
            r, n = _rms_stats(o)
            so = _sig(og)
            dob = dyb_h * (og * so)
            dog = dyb_h * (n * gh) * (so * (1.0 + og * (1.0 - so)))
            dgn_parts.append(jnp.sum(dob * n, axis=0, keepdims=True))
            do = _rms_bwd(dob * gh, n, r)
            dob_ = do.astype(BF16)
            d_att = jnp.where(low, _dot_nt(dob_, vb), 0.0).astype(BF16)
            dv_in = _dot_tn(att, dob_)
            dqt_in = _dot(d_att, ktb)
            dkt = _dot_tn(d_att, qtb)
            ds_t = dst[h]
            dv_p, dqt_p, dkd_p, dgl_p = [None] * nc, [None] * nc, [None] * nc, [None] * nc
            for c in reversed(range(nc)):
                sl = slice(c * CH, (c + 1) * CH)
                s_prev = ss_ref[c, h]
                ds_bf = ds_t.astype(BF16)
                dec = jnp.exp(g[c * CH + CH - 1:c * CH + CH, :])
                dv_p[c] = dv_in[sl] + _dot_nt(kdb[sl], ds_bf)
                dqt_p[c] = dqt_in[sl] + _dot(dob_[sl], s_prev)
                dkd_p[c] = _dot(vb[sl], ds_bf)
                ddec = jnp.sum(s_prev.astype(F32) * ds_t, axis=0, keepdims=True)
                dgl_p[c] = jnp.broadcast_to(ddec * dec, (CH, HD))
                ds_t = ds_t * dec + _dot_tn(dob_[sl], qtb[sl])
            dst[h] = ds_t
            dv = jnp.concatenate(dv_p, axis=0)
            dqt = jnp.concatenate(dqt_p, axis=0)
            dkd = jnp.concatenate(dkd_p, axis=0)
            dgl = jnp.concatenate(dgl_p, axis=0)
            dq = dqt * eg
            dk = dkt * eng + dkd * egu
            dg = dqt * qt - dkt * kt
            dgu = dkd * kd
            dlogf = _chunk_revcumsum(dg - dgu) + _chunk_total(dgu) + dgl
            one_m_lb = 1.0 - lbh
            dfz = one_m_lb * sg * sgn * (dlogf / f - dk)
            dlb_parts.append(jnp.sum(sgn * (dlogf / f - dk), axis=0, keepdims=True))
            dp_ref[0, :, c0:c0 + HD] = dq.astype(BF16)
            dp_ref[0, :, D + c0:D + c0 + HD] = dfz.astype(BF16)
            dp_ref[1, :, c0:c0 + HD] = dv.astype(BF16)
            dp_ref[1, :, D + c0:D + c0 + HD] = dog.astype(BF16)
        dlb = jnp.concatenate(dlb_parts, axis=1)
        dgn = jnp.concatenate(dgn_parts, axis=1)

        @pl.when(i == 0)
        def _():
            dlb_ref[0:1, :] = dlb
            dgn_ref[...] = dgn

        @pl.when(i > 0)
        def _():
            dlb_ref[0:1, :] += dlb
            dgn_ref[...] += dgn

        @pl.when(i == nt - 1)
        def _():
            d0 = dlb_ref[0:1, :] * lb * (1.0 - lb)
            dlb_ref[0:1, :] = d0
            dlb_ref[1:2, :] = -d0

    rev = lambda i: nt - 1 - i
    vec = pl.BlockSpec((1, D), lambda i: (0, 0))
    return pl.pallas_call(
        body, name="hgrn2_bwd",
        out_shape=(jax.ShapeDtypeStruct(dp4.shape, BF16), jax.ShapeDtypeStruct((2, D), F32),
                   jax.ShapeDtypeStruct((1, D), F32)),
        grid=(nt,),
        in_specs=[pl.BlockSpec((2, tt, 2 * D), lambda i: (0, rev(i), 0)),
                  pl.BlockSpec((tt, D), lambda i: (rev(i), 0)),
                  pl.BlockSpec((nc, NH, HD, HD), lambda i: (rev(i), 0, 0, 0)),
                  pl.BlockSpec((tt, D), lambda i: (rev(i), 0)),
                  pl.BlockSpec(memory_space=pl.ANY),
                  pl.BlockSpec((2, D), lambda i: (0, 0)), vec],
        out_specs=(pl.BlockSpec((2, tt, 2 * D), lambda i: (0, rev(i), 0)),
                   pl.BlockSpec((2, D), lambda i: (0, 0)), vec),
        scratch_shapes=[pltpu.VMEM((NH, HD, HD), F32)],
        input_output_aliases={4: 0},
        compiler_params=_params(("arbitrary",), VMEM_BIG),
    )(p4, o_all, ss, dyb, dp4, logits, gnorm)


def _mixer_a_bwd(p4, hseq, dya, dp4, cw, cb, wa, wx, ba, bx, lam, tt):
    t = p4.shape[1]
    nt = t // tt
    steps = tt.bit_length() - 1

    def body(p_ref, ph_ref, h_ref, hh_ref, dya_ref, dp_in, cw_ref, cb_ref, wa_ref, wx_ref, ba_ref, bx_ref, lam_ref,
             dp_ref, dcw_ref, dcb_ref, dwa_ref, dwx_ref, dba_ref, dbx_ref, dlam_ref,
             ext, dext, dhc, afc):
        del dp_in
        i = pl.program_id(0)
        first_tile = i == nt - 1

        @pl.when(i == 0)
        def _():
            dext[tt:, :] = jnp.zeros((8, D), F32)
            dhc[...] = jnp.zeros((1, D), F32)
            afc[...] = jnp.zeros((1, D), F32)

        xa = p_ref[:, 0:D]
        ga = p_ref[:, D:2 * D]
        ext[0:8, :] = jnp.where(first_tile, 0.0, ph_ref[:, 0:D])
        ext[8:, :] = xa
        xc = cb_ref[...] + sum(cw_ref[k:k + 1, :] * ext[pl.ds(5 + k, tt), :] for k in range(4))
        lam = lam_ref[...]
        xcb, r, ig, sp, a, mult = _lru_gates(xc, wa_ref, wx_ref, ba_ref[...], bx_ref[...], lam)
        h = h_ref[...]
        gl, dgl = _gelu_and_grad(ga)
        dya = dya_ref[...]
        dga = dya * h * dgl
        acc_g = dya * gl
        rows = lax.broadcasted_iota(jnp.int32, (tt, 1), 0)
        acc_b = jnp.where(rows == tt - 1, afc[...], pltpu.roll(a, tt - 1, 0))
        for lvl in range(steps):
            d = 1 << lvl
            acc_g = acc_g + acc_b * _shift_rows_up(acc_g, d, 0.0)
            acc_b = acc_b * _shift_rows_up(acc_b, d, 1.0)
        dh = acc_g + acc_b * dhc[...]
        dhc[...] = dh[0:1, :]
        afc[...] = a[0:1, :]
        h_prev = jnp.where(rows == 0, jnp.where(first_tile, 0.0, hh_ref[7:8, :]), pltpu.roll(h, 1, 0))
        da = dh * h_prev
        dmult = dh * ig * xc
        di = dh * mult * xc
        dlog_a = da * a - dmult * a * a / mult
        dr = dlog_a * (-LRU_C * sp)
        dsp = jnp.sum(dlog_a * (-LRU_C * r), axis=0, keepdims=True)
        dra = dr * r * (1.0 - r)
        dix = di * ig * (1.0 - ig)
        drab = dra.astype(BF16)
        dixb = dix.astype(BF16)
        dxc_lin = []
        dwa_new = []
        dwx_new = []
        for n in range(NH):
            cs = slice(n * HD, (n + 1) * HD)
            dxc_lin.append(_dot_nt(drab[:, cs], wa_ref[n]) + _dot_nt(dixb[:, cs], wx_ref[n]))
            dwa_new.append(_dot_tn(xcb[:, cs], drab[:, cs]))
            dwx_new.append(_dot_tn(xcb[:, cs], dixb[:, cs]))
        dxc = dh * mult * ig + jnp.concatenate(dxc_lin, axis=1)
        dext[0:tt, :] = dxc
        dxa = sum(cw_ref[k:k + 1, :] * dext[pl.ds(3 - k, tt), :] for k in range(4))
        dext[tt:, :] = dxc[0:8, :]
        dp_ref[:, 0:D] = dxa.astype(BF16)
        dp_ref[:, D:2 * D] = dga.astype(BF16)
        dcw = jnp.concatenate(
            [jnp.sum(dxc * ext[pl.ds(5 + k, tt), :], axis=0, keepdims=True) for k in range(4)], axis=0)
        dcb = jnp.sum(dxc, axis=0, keepdims=True)
        dba = jnp.sum(dra, axis=0, keepdims=True)
        dbx = jnp.sum(dix, axis=0, keepdims=True)
        dlam = dsp * (-_sig(-lam))

        @pl.when(i == 0)
        def _():
            dcw_ref[...] = dcw
            dcb_ref[...] = dcb
            dba_ref[...] = dba
            dbx_ref[...] = dbx
            dlam_ref[...] = dlam
            for n in range(NH):
                dwa_ref[n] = dwa_new[n]
                dwx_ref[n] = dwx_new[n]

        @pl.when(i > 0)
        def _():
            dcw_ref[...] += dcw
            dcb_ref[...] += dcb
            dba_ref[...] += dba
            dbx_ref[...] += dbx
            dlam_ref[...] += dlam
            for n in range(NH):
                dwa_ref[n] += dwa_new[n]
                dwx_ref[n] += dwx_new[n]

    rev = lambda i: nt - 1 - i
    hb = tt // 8
    full = lambda shape: pl.BlockSpec(shape, lambda i: (0,) * len(shape))
    vecs = jax.ShapeDtypeStruct((1, D), F32)
    blk = jax.ShapeDtypeStruct((NH, HD, HD), F32)
    return pl.pallas_call(
        body, name="mixer_a_bwd",
        out_shape=(jax.ShapeDtypeStruct(dp4.shape, BF16), jax.ShapeDtypeStruct((4, D), F32), vecs, blk, blk,
                   vecs, vecs, vecs),
        grid=(nt,),
        in_specs=[pl.BlockSpec((None, tt, 2 * D), lambda i: (SLOT_A, rev(i), 0)),
                  pl.BlockSpec((None, 8, 2 * D), lambda i: (SLOT_A, jnp.maximum(rev(i) * hb - 1, 0), 0)),
                  pl.BlockSpec((tt, D), lambda i: (rev(i), 0)),
                  pl.BlockSpec((8, D), lambda i: (jnp.maximum(rev(i) * hb - 1, 0), 0)),
                  pl.BlockSpec((tt, D), lambda i: (rev(i), 0)),
                  pl.BlockSpec(memory_space=pl.ANY),
                  full((4, D)), full((1, D)), full((NH, HD, HD)), full((NH, HD, HD)),
                  full((1, D)), full((1, D)), full((1, D))],
        out_specs=(pl.BlockSpec((None, tt, 2 * D), lambda i: (SLOT_A, rev(i), 0)),
                   full((4, D)), full((1, D)), full((NH, HD, HD)), full((NH, HD, HD)),
                   full((1, D)), full((1, D)), full((1, D))),
        scratch_shapes=[pltpu.VMEM((tt + 8, D), F32), pltpu.VMEM((tt + 8, D), F32),
                        pltpu.VMEM((1, D), F32), pltpu.VMEM((1, D), F32)],
        input_output_aliases={5: 0},
        compiler_params=_params(("arbitrary",), VMEM_BIG),
    )(p4, p4, hseq, hseq, dya, dp4, cw, cb, wa, wx, ba, bx, lam)


def _norm_bwd(dh1, dx1, x, gain, tt):
    t = x.shape[0]

    def body(dh_ref, dx1_ref, x_ref, g_ref, dx_ref, dg_ref):
        i = pl.program_id(0)
        r, n = _rms_stats(x_ref[...])
        dh = dh_ref[...]
        dx_ref[...] = dx1_ref[...] + _rms_bwd(dh * g_ref[...], n, r)
        dgv = jnp.sum(dh * n, axis=0, keepdims=True)

        @pl.when(i == 0)
        def _():
            dg_ref[...] = dgv

        @pl.when(i > 0)
        def _():
            dg_ref[...] += dgv

    tile = pl.BlockSpec((tt, D), lambda i: (i, 0))
    vec = pl.BlockSpec((1, D), lambda i: (0, 0))
    return pl.pallas_call(
        body, name="norm_bwd",
        out_shape=(jax.ShapeDtypeStruct((t, D), F32), jax.ShapeDtypeStruct((1, D), F32)),
        grid=(t // tt,), in_specs=[tile, tile, tile, vec], out_specs=(tile, vec),
        compiler_params=_params(("arbitrary",)),
    )(dh1, dx1, x, gain)


def _local_step(x, tgt, w, small):
    t = x.shape[0]
    tt = min(256, t)
    tm = min(512, t)
    wa_bf = small["lru_wa"].astype(BF16)
    wx_bf = small["lru_wx"].astype(BF16)
    w_br_a = w["w_branch_a"].reshape(D, D)
    w_br_b = w["w_branch_b"].reshape(D, D)
    w_out = w["w_out"].reshape(D, D)
    w_down = w["w_down"].reshape(DFF, D)
    conv_a_w = w["conv_a_w"]
    conv_f_w = w["conv_f_w"]

    h1 = _norm_fwd(x, small["norm_pre_mix"], tt)
    p4 = _mm_nn_sharded(h1, w["w_in"], F32, tm, "mm_in", _slot_of_chip)
    ya, hseq = _mixer_a_fwd(p4, conv_a_w, small["conv_a_b"], wa_bf, wx_bf, small["lru_ba"], small["lru_bx"],
                            small["lru_lambda"], tt)
    yb, o_all, ss = _hgrn2_fwd(p4, small["hg_lb_logits"], small["hg_norm_g"], tt)
    za, zb, mixb, m2, x1, h2 = _mid_fwd(ya, yb, p4, x, w_br_a, w_br_b, w_out, small["norm_post_mix"],
                                        small["norm_pre_ffn"], tt)
    up_pre = _mm_nn_sharded(h2, w["w_up"], F32, tm, "mm_up")
    tf = min(128, t)
    y = _ffn_act_fwd(up_pre, conv_f_w, small["conv_f_b"], tf)
    dx2, dm3, lossv, d_norm_post_ffn = _down_loss(y, w_down, x1, tgt, small["norm_post_ffn"], tt)

    d_w_down = _mm_tn(y, dm3, DFF // 2, D, tm, "mm_dw_down")
    dup_pre, d_conv_f_w, d_conv_f_b = _ffn_act_bwd(dm3, w_down, up_pre, conv_f_w, small["conv_f_b"], tf)
    d_w_up = _mm_tn(h2, dup_pre, D, SH_UP, tm, "mm_dw_up", stacked_out=True)
    dh2 = _mm_nt_sharded(dup_pre, w["w_up"], tm, "mm_dh2")
    dx1, dm2, dza, dzb, dya, dyb, dp4, d_norm_post_mix, d_norm_pre_ffn = _mid_bwd(
        dh2, dx2, x1, m2, za, zb, p4, w_br_a, w_br_b, w_out, small["norm_post_mix"], small["norm_pre_ffn"], tt)
    d_w_out = _mm_tn(mixb, dm2, D, D, tm, "mm_dw_out")
    d_w_br_a = _mm_tn(ya, dza, D, D, tm, "mm_dw_bra")
    d_w_br_b = _mm_tn(yb, dzb, D, D, tm, "mm_dw_brb")
    dp4, d_lb, d_hg_norm_g = _hgrn2_bwd(p4, o_all, ss, dyb, dp4, small["hg_lb_logits"], small["hg_norm_g"], tt)
    dp4, d_conv_a_w, d_conv_a_b, d_lru_wa, d_lru_wx, d_lru_ba, d_lru_bx, d_lru_lambda = _mixer_a_bwd(
        p4, hseq, dya, dp4, conv_a_w, small["conv_a_b"], wa_bf, wx_bf, small["lru_ba"], small["lru_bx"],
        small["lru_lambda"], tt)
    d_w_in = _mm_tn(h1, dp4, D, SH_IN, tm, "mm_dw_in", stacked_slot_fn=_slot_of_chip, stacked_out=True)
    dh1 = _mm_nt_sharded(dp4, w["w_in"], tm, "mm_dh1", stacked_slot_fn=_slot_of_chip)
    grad_x, d_norm_pre_mix = _norm_bwd(dh1, dx1, x, small["norm_pre_mix"], tt)

    big = {
        "w_in": d_w_in,
        "w_branch_a": d_w_br_a.reshape(NCHIP, SH_BR, D),
        "w_branch_b": d_w_br_b.reshape(NCHIP, SH_BR, D),
        "w_out": d_w_out.reshape(NCHIP, SH_BR, D),
        "w_up": d_w_up,
        "w_down": d_w_down.reshape(NCHIP, SH_DN, D),
    }
    smalls = {
        "norm_pre_mix": d_norm_pre_mix, "conv_a_b": d_conv_a_b, "lru_ba": d_lru_ba, "lru_bx": d_lru_bx,
        "lru_lambda": d_lru_lambda, "hg_lb_logits": d_lb, "hg_norm_g": d_hg_norm_g, "norm_post_mix": d_norm_post_mix,
        "norm_pre_ffn": d_norm_pre_ffn, "norm_post_ffn": d_norm_post_ffn, "lossv": lossv,
        "conv_a_w": d_conv_a_w, "lru_wa": d_lru_wa, "lru_wx": d_lru_wx,
        "conv_f_b": d_conv_f_b, "conv_f_w": d_conv_f_w,
    }
    return grad_x, big, smalls


BIG = ("w_in", "w_branch_a", "w_branch_b", "w_out", "w_up", "w_down")
BIG_SHAPE = {"w_in": (D, SH_IN), "w_branch_a": (SH_BR, D), "w_branch_b": (SH_BR, D), "w_out": (SH_BR, D),
             "w_up": (D, SH_UP), "w_down": (SH_DN, D)}
NBIG = len(BIG)
VEC_ROWS = (("norm_pre_mix", 0, 1), ("conv_a_b", 1, 1), ("lru_ba", 2, 1), ("lru_bx", 3, 1), ("lru_lambda", 4, 1),
            ("hg_lb_logits", 5, 2), ("hg_norm_g", 7, 1), ("norm_post_mix", 8, 1), ("norm_pre_ffn", 9, 1),
            ("norm_post_ffn", 10, 1))
ROW_LOSS = 11
ROW_CONV_A = 12
S1_ROWS = 16
S2_ROWS = 8


def _place():
    x, y, c = lax.axis_index("x"), lax.axis_index("y"), lax.axis_index("c")
    chips = [(1 - x, y), (x, 1 - y), (1 - x, 1 - y)]
    return x, y, c, 2 * x + y, chips


def _remote(src, dst, ssem, rsem, dev):
    return pltpu.make_async_remote_copy(src_ref=src, dst_ref=dst, send_sem=ssem, recv_sem=rsem,
                                        device_id=dev, device_id_type=MESH)


def _hbm_call(body, name, ins, out_shapes, n_sems, aliases=None):
    any_spec = pl.BlockSpec(memory_space=pl.ANY)
    return pl.pallas_call(
        body, name=name, out_shape=tuple(out_shapes),
        in_specs=[any_spec] * len(ins), out_specs=tuple([any_spec] * len(out_shapes)),
        scratch_shapes=[pltpu.SemaphoreType.DMA((n,)) for n in n_sems],
        input_output_aliases=aliases or {},
        compiler_params=pltpu.CompilerParams(has_side_effects=True),
    )(*ins)


def _gather_weights(stacked, conv_a_s, conv_f_s):
    ins = [stacked[n] for n in BIG] + [conv_a_s, conv_f_s]
    n_in = len(ins)
    halves = [BIG_SHAPE[n][0] // 2 for n in BIG]
    out_shapes = [jax.ShapeDtypeStruct(stacked[n].shape, stacked[n].dtype) for n in BIG]
    out_shapes += [jax.ShapeDtypeStruct((NCHIP,) + a.shape, a.dtype) for a in (conv_a_s, conv_f_s)]

    def body(*refs):
        src, dst = refs[:n_in], refs[n_in:2 * n_in]
        ssem, rsem, fssem, frsem, lsem = refs[2 * n_in:]
        x, y, c, j, chips = _place()

        def half(ref, w, which):
            return ref.at[pl.ds(which * halves[w], halves[w]), :]

        locs = [pltpu.make_async_copy(src[w], dst[w].at[j], lsem.at[w - NBIG]) for w in range(NBIG, n_in)]
        for cp in locs:
            cp.start()
        sends = []
        for w in range(n_in):
            for k, (cx, cy) in enumerate(chips):
                if w < NBIG:
                    mine = half(dst[w].at[j], w, c)
                    cp = _remote(mine, mine, ssem.at[3 * w + k], rsem.at[3 * w + k], (cx, cy, c))
                else:
                    cp = _remote(src[w], dst[w].at[j], ssem.at[3 * w + k], rsem.at[3 * w + k], (cx, cy, c))
                cp.start()
                sends.append(cp)
        fwds = []
        for w in range(n_in):
            for k, (cx, cy) in enumerate(chips):
                jk = 2 * cx + cy
                if w < NBIG:
                    got = half(dst[w].at[jk], w, c)
                    _remote(got, got, ssem.at[3 * w + k], rsem.at[3 * w + k], (cx, cy, c)).wait_recv()
                    cp = _remote(got, got, fssem.at[3 * w + k], frsem.at[3 * w + k], (x, y, 1 - c))
                    cp.start()
                    fwds.append(cp)
                else:
                    got = dst[w].at[jk]
                    _remote(got, got, ssem.at[3 * w + k], rsem.at[3 * w + k], (cx, cy, c)).wait_recv()
        for w in range(NBIG):
            for k, (cx, cy) in enumerate(chips):
                other = half(dst[w].at[2 * cx + cy], w, 1 - c)
                _remote(other, other, fssem.at[3 * w + k], frsem.at[3 * w + k], (x, y, 1 - c)).wait_recv()
        for cp in sends + fwds:
            cp.wait_send()
        for cp in locs:
            cp.wait()

    outs = _hbm_call(body, "gather_weights", ins, out_shapes, (3 * n_in, 3 * n_in, 3 * NBIG, 3 * NBIG, n_in - NBIG),
                     aliases={w: w for w in range(NBIG)})
    return dict(zip(BIG, outs[:NBIG])), outs[NBIG], outs[NBIG + 1]


def _reduce_stage1(big_g, s1, s2, s3):
    ins = [big_g[n] for n in BIG] + [s1, s2, s3]
    n_in = len(ins)
    halves = [BIG_SHAPE[n][0] // 2 for n in BIG]
    out_shapes = [jax.ShapeDtypeStruct((NCHIP, halves[w], BIG_SHAPE[n][1]), F32) for w, n in enumerate(BIG)]
    out_shapes += [jax.ShapeDtypeStruct(a.shape, F32) for a in (s1, s2, s3)]

    def body(*refs):
        src, dst = refs[:n_in], refs[n_in:2 * n_in]
        ssem, rsem = refs[2 * n_in:]
        x, y, c, _, _ = _place()
        cps = []
        for w in range(n_in):
            s_ = src[w].at[:, pl.ds((1 - c) * halves[w], halves[w]), :] if w < NBIG else src[w]
            cp = _remote(s_, dst[w], ssem.at[w], rsem.at[w], (x, y, 1 - c))
            cp.start()
            cps.append(cp)
        for cp in cps:
            cp.wait()

    outs = _hbm_call(body, "reduce_d2d_in", ins, out_shapes, (n_in, n_in))
    return dict(zip(BIG, outs[:NBIG])), outs[NBIG], outs[NBIG + 1], outs[NBIG + 2]


def _reduce_stage2(p_big, ps1, ps2, ps3):
    ins = [p_big[n] for n in BIG] + [ps1, ps2, ps3]
    n_in = len(ins)
    h1, h2, h3 = S1_ROWS // 2, DUP // 2, D
    out_shapes = [jax.ShapeDtypeStruct(p_big[n].shape, BF16) for n in BIG]
    out_shapes += [jax.ShapeDtypeStruct((NCHIP, h1, D), F32), jax.ShapeDtypeStruct((NCHIP, S2_ROWS, h2), F32),
                   jax.ShapeDtypeStruct((NCHIP, h3, HD), F32)]

    def body(*refs):
        src, dst = refs[:n_in], refs[n_in:2 * n_in]
        ssem, rsem = refs[2 * n_in:]
        x, y, c, j, chips = _place()

        def piece(w, to_chip):
            if w < NBIG:
                return src[w].at[to_chip]
            if w == NBIG:
                return src[w].at[pl.ds(c * h1, h1), :]
            if w == NBIG + 1:
                return src[w].at[:, pl.ds(c * h2, h2)]
            return src[w].at[pl.ds(c * h3, h3), :]

        sends = []
        for w in range(n_in):
            for k, (cx, cy) in enumerate(chips):
                cp = _remote(piece(w, 2 * cx + cy), dst[w].at[j], ssem.at[3 * w + k], rsem.at[3 * w + k], (cx, cy, c))
                cp.start()
                sends.append(cp)
        for w in range(n_in):
            for k, (cx, cy) in enumerate(chips):
                got = dst[w].at[2 * cx + cy]
                _remote(got, got, ssem.at[3 * w + k], rsem.at[3 * w + k], (cx, cy, c)).wait_recv()
        for cp in sends:
            cp.wait_send()

    outs = _hbm_call(body, "reduce_ici", ins, out_shapes, (3 * n_in, 3 * n_in))
    return dict(zip(BIG, outs[:NBIG])), outs[NBIG], outs[NBIG + 1], outs[NBIG + 2]


def _reduce_stage3(f_big, fs1, fs2, fs3):
    ins = [f_big[n] for n in BIG] + [fs1, fs2, fs3]
    n_in = len(ins)
    halves = [BIG_SHAPE[n][0] // 2 for n in BIG]
    h1, h2, h3 = S1_ROWS // 2, DUP // 2, D
    out_shapes = [jax.ShapeDtypeStruct(BIG_SHAPE[n], F32) for n in BIG]
    out_shapes += [jax.ShapeDtypeStruct((S1_ROWS, D), F32), jax.ShapeDtypeStruct((S2_ROWS, DUP), F32),
                   jax.ShapeDtypeStruct((2 * D, HD), F32)]

    def body(*refs):
        dst = refs[n_in:2 * n_in]
        ssem, rsem = refs[2 * n_in:]
        x, y, c, _, _ = _place()

        def place(w, which):
            if w < NBIG:
                return dst[w].at[pl.ds(which * halves[w], halves[w]), :]
            if w == NBIG:
                return dst[w].at[pl.ds(which * h1, h1), :]
            if w == NBIG + 1:
                return dst[w].at[:, pl.ds(which * h2, h2)]
            return dst[w].at[pl.ds(which * h3, h3), :]

        cps = [_remote(place(w, c), place(w, c), ssem.at[w], rsem.at[w], (x, y, 1 - c)) for w in range(n_in)]
        for cp in cps:
            cp.start()
        for w in range(n_in):
            got = place(w, 1 - c)
            _remote(got, got, ssem.at[w], rsem.at[w], (x, y, 1 - c)).wait_recv()
        for cp in cps:
            cp.wait_send()

    outs = _hbm_call(body, "reduce_d2d_out", ins, out_shapes, (n_in, n_in), aliases={w: w for w in range(n_in)})
    return dict(zip(BIG, outs[:NBIG])), outs[NBIG], outs[NBIG + 1], outs[NBIG + 2]


def _row_tile(rows):
    for tr in (128, 176, 64, 16, 8):
        if rows % tr == 0:
            return tr
    return rows


def _sum_own_half(g, rb, cidx, name):
    s, rows, cols = g.shape
    half = rows // 2
    tr = _row_tile(half)
    nb = half // tr

    def body(c_ref, g_ref, r_ref, o_ref):
        del c_ref
        o_ref[...] = (g_ref[...] + r_ref[...]).astype(BF16)

    grid_spec = pltpu.PrefetchScalarGridSpec(
        num_scalar_prefetch=1, grid=(s, nb),
        in_specs=[pl.BlockSpec((None, tr, cols), lambda k, i, c: (k, c[0] * nb + i, 0)),
                  pl.BlockSpec((None, tr, cols), lambda k, i, c: (k, i, 0))],
        out_specs=pl.BlockSpec((None, tr, cols), lambda k, i, c: (k, i, 0)))
    return pl.pallas_call(
        body, name=name, grid_spec=grid_spec, out_shape=jax.ShapeDtypeStruct((s, half, cols), BF16),
        compiler_params=_params(("parallel", "parallel")),
    )(cidx, g, rb)


def _sum_chips(q, p, jc, name, by_cols=False):
    s, rows, cols = q.shape
    tr = _row_tile(rows)
    nb = rows // tr
    stacked = p.ndim == 3

    def body(jc_ref, q_ref, p_ref, o_ref):
        j = jc_ref[0]
        own = p_ref[...].astype(F32)
        acc = None
        for k in range(NCHIP):
            term = jnp.where(j == k, own, q_ref[k].astype(F32))
            acc = term if acc is None else acc + term
        o_ref[...] = acc

    if by_cols:
        half_spec = pl.BlockSpec((tr, cols), lambda i, jc_ref: (i, jc_ref[1]))
        out_shape = jax.ShapeDtypeStruct((rows, 2 * cols), F32)
    else:
        half_spec = pl.BlockSpec((tr, cols), lambda i, jc_ref: (jc_ref[1] * nb + i, 0))
        out_shape = jax.ShapeDtypeStruct((2 * rows, cols), F32)
    p_spec = pl.BlockSpec((None, tr, cols), lambda i, jc_ref: (jc_ref[0], i, 0)) if stacked else half_spec
    grid_spec = pltpu.PrefetchScalarGridSpec(
        num_scalar_prefetch=1, grid=(nb,),
        in_specs=[pl.BlockSpec((s, tr, cols), lambda i, jc_ref: (0, i, 0)), p_spec],
        out_specs=half_spec)
    return pl.pallas_call(
        body, name=name, grid_spec=grid_spec, out_shape=out_shape,
        compiler_params=_params(("parallel",)),
    )(jc, q, p)


def _place_shard(w, jc, name):
    rows, cols = w.shape
    tr = _row_tile(rows)

    def body(jc_ref, w_ref, o_ref):
        del jc_ref
        o_ref[...] = w_ref[...].astype(BF16)

    grid_spec = pltpu.PrefetchScalarGridSpec(
        num_scalar_prefetch=1, grid=(rows // tr,),
        in_specs=[pl.BlockSpec((tr, cols), lambda i, jc_ref: (i, 0))],
        out_specs=pl.BlockSpec((None, tr, cols), lambda i, jc_ref: (jc_ref[0], i, 0)))
    return pl.pallas_call(
        body, name=name, grid_spec=grid_spec, out_shape=jax.ShapeDtypeStruct((NCHIP, rows, cols), BF16),
        compiler_params=_params(("parallel",)),
    )(jc, w)


def _add(a, b, name):
    def body(a_ref, b_ref, o_ref):
        o_ref[...] = a_ref[...] + b_ref[...]

    return pl.pallas_call(body, name=name, out_shape=jax.ShapeDtypeStruct(a.shape, F32))(a, b)


def _pack_small(sm):
    vec_in = [sm[n] for n, _, _ in VEC_ROWS]
    nv = len(vec_in)

    def body(*refs):
        ins, lossv, dcw, dcfb, dcfw, s1, s2 = refs[:nv], refs[nv], refs[nv + 1], refs[nv + 2], refs[nv + 3], \
            refs[nv + 4], refs[nv + 5]
        for ref, (_, r0, nr) in zip(ins, VEC_ROWS):
            s1[r0:r0 + nr, :] = ref[...]
        s1[ROW_LOSS:ROW_LOSS + 1, :] = lossv[...]
        s1[ROW_CONV_A:ROW_CONV_A + 4, :] = dcw[...]
        s2[0:1, :] = dcfb[...]
        s2[1:4, :] = dcfw[...]
        s2[4:8, :] = jnp.zeros((4, DUP), F32)

    return pl.pallas_call(
        body, name="pack_small",
        out_shape=(jax.ShapeDtypeStruct((S1_ROWS, D), F32), jax.ShapeDtypeStruct((S2_ROWS, DUP), F32)),
    )(*vec_in, sm["lossv"], sm["conv_a_w"], sm["conv_f_b"], sm["conv_f_w"])


def _adam_math(w, g, m, v):
    m = ADAM_B1 * m + (1.0 - ADAM_B1) * g
    v = ADAM_B2 * v + (1.0 - ADAM_B2) * (g * g)
    m_hat = m / (1.0 - ADAM_B1 ** ADAM_STEP)
    v_hat = v / (1.0 - ADAM_B2 ** ADAM_STEP)
    delta = -ADAM_LR * (m_hat / (jnp.sqrt(v_hat) + ADAM_EPS) + ADAM_WD * w)
    return delta, m, v


def _adam(w, g, m, v, name):
    rows, cols = w.shape
    tr = _row_tile(rows)

    def body(w_ref, g_ref, m_ref, v_ref, d_ref, mo_ref, vo_ref):
        d_ref[...], mo_ref[...], vo_ref[...] = _adam_math(w_ref[...], g_ref[...], m_ref[...], v_ref[...])

    spec = pl.BlockSpec((tr, cols), lambda i: (i, 0))
    return pl.pallas_call(
        body, name=name, out_shape=(jax.ShapeDtypeStruct(w.shape, F32),) * 3, grid=(rows // tr,),
        in_specs=[spec] * 4, out_specs=(spec,) * 3,
        compiler_params=_params(("parallel",)),
    )(w, g, m, v)


def _adam_small(gs1, gs2, gs3, w, m, v):
    names = [n for n, _, _ in VEC_ROWS] + ["conv_f_b", "lru_wa", "lru_wx"]
    nn = len(names)

    def grad_of(i, g1, g2, g3):
        if i < len(VEC_ROWS):
            _, r0, nr = VEC_ROWS[i]
            return g1[r0:r0 + nr, :]
        if names[i] == "conv_f_b":
            return g2[0:1, :]
        return g3[0] if names[i] == "lru_wa" else g3[1]

    def body(*refs):
        g1, g2, g3 = refs[0], refs[1], refs[2]
        ws, ms, vs = refs[3:3 + nn], refs[3 + nn:3 + 2 * nn], refs[3 + 2 * nn:3 + 3 * nn]
        outs = refs[3 + 3 * nn:]
        for i in range(nn):
            d, mn, vn = _adam_math(ws[i][...], grad_of(i, g1, g2, g3), ms[i][...], vs[i][...])
            outs[i][...] = d
            outs[nn + i][...] = mn
            outs[2 * nn + i][...] = vn

    shapes = [jax.ShapeDtypeStruct(w[n].shape, F32) for n in names]
    outs = pl.pallas_call(body, name="adam_small", out_shape=tuple(shapes * 3))(
        gs1, gs2, gs3, *[w[n] for n in names], *[m[n] for n in names], *[v[n] for n in names])
    return {n: (outs[i], outs[nn + i], outs[2 * nn + i]) for i, n in enumerate(names)}


WEIGHTS = ("norm_pre_mix", "w_in", "conv_a_w", "conv_a_b", "lru_wa", "lru_ba", "lru_wx", "lru_bx", "lru_lambda",
           "hg_lb_logits", "hg_norm_g", "w_branch_a", "w_branch_b", "w_out", "norm_post_mix", "norm_pre_ffn",
           "w_up", "conv_f_w", "conv_f_b", "w_down", "norm_post_ffn")
NW = len(WEIGHTS)


def kernel(x, norm_pre_mix, w_in, conv_a_w, conv_a_b, lru_wa, lru_ba, lru_wx, lru_bx, lru_lambda, hg_lb_logits, hg_norm_g, w_branch_a, w_branch_b, w_out, norm_post_mix, norm_pre_ffn, w_up, conv_f_w, conv_f_b, w_down, norm_post_ffn, loss_target, m_norm_pre_mix, m_w_in, m_conv_a_w, m_conv_a_b, m_lru_wa, m_lru_ba, m_lru_wx, m_lru_bx, m_lru_lambda, m_hg_lb_logits, m_hg_norm_g, m_w_branch_a, m_w_branch_b, m_w_out, m_norm_post_mix, m_norm_pre_ffn, m_w_up, m_conv_f_w, m_conv_f_b, m_w_down, m_norm_post_ffn, v_norm_pre_mix, v_w_in, v_conv_a_w, v_conv_a_b, v_lru_wa, v_lru_ba, v_lru_wx, v_lru_bx, v_lru_lambda, v_hg_lb_logits, v_hg_norm_g, v_w_branch_a, v_w_branch_b, v_w_out, v_norm_post_mix, v_norm_pre_ffn, v_w_up, v_conv_f_w, v_conv_f_b, v_w_down, v_norm_post_ffn):
    rest = (norm_pre_mix, w_in, conv_a_w, conv_a_b, lru_wa, lru_ba, lru_wx, lru_bx, lru_lambda, hg_lb_logits, hg_norm_g, w_branch_a, w_branch_b, w_out, norm_post_mix, norm_pre_ffn, w_up, conv_f_w, conv_f_b, w_down, norm_post_ffn, loss_target, m_norm_pre_mix, m_w_in, m_conv_a_w, m_conv_a_b, m_lru_wa, m_lru_ba, m_lru_wx, m_lru_bx, m_lru_lambda, m_hg_lb_logits, m_hg_norm_g, m_w_branch_a, m_w_branch_b, m_w_out, m_norm_post_mix, m_norm_pre_ffn, m_w_up, m_conv_f_w, m_conv_f_b, m_w_down, m_norm_post_ffn, v_norm_pre_mix, v_w_in, v_conv_a_w, v_conv_a_b, v_lru_wa, v_lru_ba, v_lru_wx, v_lru_bx, v_lru_lambda, v_hg_lb_logits, v_hg_norm_g, v_w_branch_a, v_w_branch_b, v_w_out, v_norm_post_mix, v_norm_pre_ffn, v_w_up, v_conv_f_w, v_conv_f_b, v_w_down, v_norm_post_ffn)
    w_in_args = dict(zip(WEIGHTS, rest[:NW]))
    loss_target = rest[NW]
    m_args = dict(zip(WEIGHTS, rest[NW + 1:2 * NW + 1]))
    v_args = dict(zip(WEIGHTS, rest[2 * NW + 1:3 * NW + 1]))
    shape_of = {n: w_in_args[n].shape for n in WEIGHTS}

    def two_d(n, a):
        if n in BIG:
            return a.reshape(BIG_SHAPE[n])
        if n in ("lru_wa", "lru_wx"):
            return a.reshape(NH, HD, HD)
        return a.reshape(a.shape[-2:])

    w2 = {n: two_d(n, w_in_args[n]) for n in WEIGHTS}
    m2 = {n: two_d(n, m_args[n]) for n in WEIGHTS}
    v2 = {n: two_d(n, v_args[n]) for n in WEIGHTS}

    cidx = lax.axis_index("c").astype(jnp.int32).reshape(1)
    jchip = 2 * lax.axis_index("x") + lax.axis_index("y")

    jc = jnp.stack([jchip, lax.axis_index("c")]).astype(jnp.int32)

    shards = {n: _place_shard(w2[n], jc, "place_" + n) for n in BIG}
    conv_a_s = jnp.pad(w2["conv_a_w"], ((0, 4), (0, 0)))
    conv_f_s = jnp.pad(w2["conv_f_w"], ((0, 5), (0, 0)))
    w_full, conv_a_g, conv_f_g = _gather_weights(shards, conv_a_s, conv_f_s)
    w_full["conv_a_w"] = jnp.transpose(conv_a_g, (1, 0, 2)).reshape(8, D)[0:4]
    w_full["conv_f_w"] = jnp.transpose(conv_f_g, (1, 0, 2)).reshape(8, DUP)[0:3]
    small = {n: w2[n] for n in WEIGHTS if n not in BIG and n not in ("conv_a_w", "conv_f_w")}

    grad_x, big_g, sm_g = _local_step(x[0], loss_target[0], w_full, small)

    s1, s2 = _pack_small(sm_g)
    s3 = jnp.concatenate([sm_g["lru_wa"].reshape(D, HD), sm_g["lru_wx"].reshape(D, HD)], axis=0)
    rb, rs1, rs2, rs3 = _reduce_stage1(big_g, s1, s2, s3)
    p_big = {n: _sum_own_half(big_g[n], rb[n], cidx, "sum_half_" + n) for n in BIG}
    ps1, ps2, ps3 = _add(s1, rs1, "add_s1"), _add(s2, rs2, "add_s2"), _add(s3, rs3, "add_s3")
    q_big, qs1, qs2, qs3 = _reduce_stage2(p_big, ps1, ps2, ps3)
    f_big = {n: _sum_chips(q_big[n], p_big[n], jc, "sum_chips_" + n) for n in BIG}
    fs1 = _sum_chips(qs1, ps1, jc, "sum_chips_s1")
    fs2 = _sum_chips(qs2, ps2, jc, "sum_chips_s2", by_cols=True)
    fs3 = _sum_chips(qs3, ps3, jc, "sum_chips_s3")
    g_big, gs1, gs2, gs3 = _reduce_stage3(f_big, fs1, fs2, fs3)

    res = {}
    for n in BIG:
        d, mn, vn = _adam(w2[n], g_big[n], m2[n], v2[n], "adam_" + n)
        res[n] = (g_big[n], d, mn, vn)
    small_res = _adam_small(gs1, gs2, gs3.reshape(2, NH, HD, HD), w2, m2, v2)
    for n, r0, nr in VEC_ROWS:
        res[n] = (gs1[r0:r0 + nr],) + small_res[n]
    res["conv_f_b"] = (gs2[0:1],) + small_res["conv_f_b"]
    res["lru_wa"] = (gs3[0:D].reshape(NH, HD, HD),) + small_res["lru_wa"]
    res["lru_wx"] = (gs3[D:2 * D].reshape(NH, HD, HD),) + small_res["lru_wx"]
    g_ca = lax.dynamic_slice_in_dim(gs1[ROW_CONV_A:ROW_CONV_A + 4], jchip * (D // NCHIP), D // NCHIP, axis=1)
    g_cf = lax.dynamic_slice_in_dim(gs2[1:4], jchip * SH_UP, SH_UP, axis=1)
    res["conv_a_w"] = (g_ca,) + _adam(w2["conv_a_w"], g_ca, m2["conv_a_w"], v2["conv_a_w"], "adam_conv_a_w")
    res["conv_f_w"] = (g_cf,) + _adam(w2["conv_f_w"], g_cf, m2["conv_f_w"], v2["conv_f_w"], "adam_conv_f_w")

    loss = (0.5 / D) * jnp.sum(gs1[ROW_LOSS])
    out = [loss, grad_x.reshape(x.shape)]
    for part in range(4):
        out += [res[n][part].reshape(shape_of[n]) for n in WEIGHTS]
    return tuple(out)
```

```python
import functools

import jax
import jax.numpy as jnp
from jax import lax
from jax.experimental import pallas as pl
from jax.experimental.pallas import tpu as pltpu

F32 = jnp.float32
BF16 = jnp.bfloat16

D = 1024
NH = 8
HD = 128
CH = 32
DFF = 2816
DUP = 2 * DFF
NCHIP = 4
SH_IN = 2 * D
SH_UP = DUP // NCHIP
SH_DN = DFF // NCHIP
SH_BR = D // NCHIP
EPS = 1e-6
LRU_C = 8.0
ADAM_LR = 0.001
ADAM_B1 = 0.9
ADAM_B2 = 0.999
ADAM_EPS = 1e-08
ADAM_WD = 0.01
ADAM_STEP = 10
VMEM_BIG = 56 * 1024 * 1024
MESH = pl.DeviceIdType.MESH

SLOT_A, SLOT_B, SLOT_C, SLOT_G = 2, 0, 1, 3


def _slot_of_chip(s):
    return jnp.where(s == 3, 3, (s + 2) % 3)


def _params(sem, vmem=None):
    return pltpu.CompilerParams(dimension_semantics=sem, vmem_limit_bytes=vmem)


_GC = 0.7978845608028654
_GA = 0.044715


def _gelu(x):
    return 0.5 * x * (1.0 + jnp.tanh(_GC * (x + _GA * x * x * x)))


def _gelu_and_grad(x):
    x2 = x * x
    th = jnp.tanh(_GC * x * (1.0 + _GA * x2))
    g = 0.5 * x * (1.0 + th)
    dg = 0.5 * (1.0 + th) + 0.5 * x * (1.0 - th * th) * _GC * (1.0 + 3.0 * _GA * x2)
    return g, dg


def _sig(x):
    return jax.nn.sigmoid(x)


def _dot(a, b):
    return jnp.dot(a, b, preferred_element_type=F32)


def _dot_nt(a, b):
    return lax.dot_general(a, b, (((1,), (1,)), ((), ())), preferred_element_type=F32)


def _dot_tn(a, b):
    return lax.dot_general(a, b, (((0,), (0,)), ((), ())), preferred_element_type=F32)


def _chunk_cumsum(x):
    pos = lax.broadcasted_iota(jnp.int32, (x.shape[0], 1), 0) & (CH - 1)
    d = 1
    while d < CH:
        x = x + jnp.where(pos >= d, pltpu.roll(x, d, 0), 0.0)
        d *= 2
    return x


def _chunk_revcumsum(x):
    n = x.shape[0]
    pos = lax.broadcasted_iota(jnp.int32, (n, 1), 0) & (CH - 1)
    d = 1
    while d < CH:
        x = x + jnp.where(pos < CH - d, pltpu.roll(x, n - d, 0), 0.0)
        d *= 2
    return x


def _chunk_last(x):
    n = x.shape[0]
    return jnp.concatenate(
        [jnp.broadcast_to(x[c * CH + CH - 1:c * CH + CH, :], (CH, x.shape[1])) for c in range(n // CH)], axis=0)


def _chunk_total(x):
    n = x.shape[0]
    return jnp.concatenate(
        [jnp.broadcast_to(jnp.sum(x[c * CH:(c + 1) * CH, :], axis=0, keepdims=True), (CH, x.shape[1]))
         for c in range(n // CH)], axis=0)


def _rms_stats(x):
    r = lax.rsqrt(jnp.mean(x * x, axis=-1, keepdims=True) + EPS)
    return r, x * r


def _rms_bwd(gd, n, r):
    return r * (gd - n * jnp.mean(gd * n, axis=-1, keepdims=True))


def _shift_rows(x, d, fill):
    rows = lax.broadcasted_iota(jnp.int32, (x.shape[0], 1), 0)
    return jnp.where(rows >= d, pltpu.roll(x, d, 0), fill)


def _scan_down(a, u, carry):
    n = a.shape[0]
    pos = lax.broadcasted_iota(jnp.int32, (n, 1), 0) & 7
    for d in (1, 2, 4):
        u = a * jnp.where(pos >= d, pltpu.roll(u, d, 0), 0.0) + u
        a = a * jnp.where(pos >= d, pltpu.roll(a, d, 0), 1.0)
    out = []
    for v in range(n // 8):
        h = a[v * 8:v * 8 + 8, :] * carry + u[v * 8:v * 8 + 8, :]
        carry = h[7:8, :]
        out.append(h)
    return jnp.concatenate(out, axis=0)


def _scan_up(b, g, carry):
    n = b.shape[0]
    pos = lax.broadcasted_iota(jnp.int32, (n, 1), 0) & 7
    for d in (1, 2, 4):
        g = g + b * jnp.where(pos < 8 - d, pltpu.roll(g, n - d, 0), 0.0)
        b = b * jnp.where(pos < 8 - d, pltpu.roll(b, n - d, 0), 1.0)
    out = [None] * (n // 8)
    for v in reversed(range(n // 8)):
        h = g[v * 8:v * 8 + 8, :] + b[v * 8:v * 8 + 8, :] * carry
        carry = h[0:1, :]
        out[v] = h
    return jnp.concatenate(out, axis=0)


def _shift_rows_up(x, d, fill):
    n = x.shape[0]
    rows = lax.broadcasted_iota(jnp.int32, (n, 1), 0)
    return jnp.where(rows < n - d, pltpu.roll(x, n - d, 0), fill)


def _mm_nn_sharded(a, b3, out_dtype, tm, name, slot_fn=None):
    m, k = a.shape
    s, _, ns = b3.shape

    def body(a_ref, b_ref, o_ref):
        o_ref[...] = _dot(a_ref[...], b_ref[...]).astype(out_dtype)

    if slot_fn is None:
        out_shape = jax.ShapeDtypeStruct((m, s * ns), out_dtype)
        out_spec = pl.BlockSpec((tm, ns), lambda j, i: (i, j))
    else:
        out_shape = jax.ShapeDtypeStruct((s, m, ns), out_dtype)
        out_spec = pl.BlockSpec((None, tm, ns), lambda j, i: (slot_fn(j), i, 0))
    return pl.pallas_call(
        body, name=name, out_shape=out_shape, grid=(s, m // tm),
        in_specs=[pl.BlockSpec((tm, k), lambda j, i: (i, 0)),
                  pl.BlockSpec((None, k, ns), lambda j, i: (j, 0, 0))],
        out_specs=out_spec,
        compiler_params=_params(("parallel", "parallel"), VMEM_BIG),
    )(a, b3)


def _mm_nt_sharded(a, b3, tm, name, stacked_slot_fn=None):
    s, k, ns = b3.shape
    m = a.shape[1] if stacked_slot_fn is not None else a.shape[0]

    def body(a_ref, b_ref, o_ref, acc_ref):
        j = pl.program_id(1)
        t = _dot_nt(a_ref[...], b_ref[...])

        @pl.when(j == 0)
        def _():
            acc_ref[...] = t

        @pl.when(j > 0)
        def _():
            acc_ref[...] += t

        @pl.when(j == s - 1)
        def _():
            o_ref[...] = acc_ref[...]

    if stacked_slot_fn is None:
        a_spec = pl.BlockSpec((tm, ns), lambda i, j: (i, j))
    else:
        a_spec = pl.BlockSpec((None, tm, ns), lambda i, j: (stacked_slot_fn(j), i, 0))
    return pl.pallas_call(
        body, name=name, out_shape=jax.ShapeDtypeStruct((m, k), F32), grid=(m // tm, s),
        in_specs=[a_spec, pl.BlockSpec((None, k, ns), lambda i, j: (j, 0, 0))],
        out_specs=pl.BlockSpec((tm, k), lambda i, j: (i, 0)),
        scratch_shapes=[pltpu.VMEM((tm, k), F32)],
        compiler_params=_params(("parallel", "arbitrary"), VMEM_BIG),
    )(a, b3)


def _mm_tn(a, g, tkk, tn, tk, name, stacked_slot_fn=None, stacked_out=False):
    m, k = a.shape
    if stacked_slot_fn is not None:
        n = g.shape[0] * g.shape[2]
        g_spec = pl.BlockSpec((None, tk, tn), lambda kk, j, mm: (stacked_slot_fn(j), mm, 0))
    else:
        n = g.shape[1]
        g_spec = pl.BlockSpec((tk, tn), lambda kk, j, mm: (mm, j))

    def body(a_ref, g_ref, o_ref):
        mm = pl.program_id(2)
        t = _dot_tn(a_ref[...], g_ref[...])

        @pl.when(mm == 0)
        def _():
            o_ref[...] = t

        @pl.when(mm > 0)
        def _():
            o_ref[...] += t

    if stacked_out:
        out_shape = jax.ShapeDtypeStruct((n // tn, k, tn), F32)
        out_spec = pl.BlockSpec((None, tkk, tn), lambda kk, j, mm: (j, kk, 0))
    else:
        out_shape = jax.ShapeDtypeStruct((k, n), F32)
        out_spec = pl.BlockSpec((tkk, tn), lambda kk, j, mm: (kk, j))
    return pl.pallas_call(
        body, name=name, out_shape=out_shape, grid=(k // tkk, n // tn, m // tk),
        in_specs=[pl.BlockSpec((tk, tkk), lambda kk, j, mm: (mm, kk)), g_spec],
        out_specs=out_spec,
        compiler_params=_params(("parallel", "parallel", "arbitrary"), VMEM_BIG),
    )(a, g)


def _norm_fwd(x, gain, tt):
    t = x.shape[0]

    def body(x_ref, g_ref, h_ref):
        _, n = _rms_stats(x_ref[...])
        h_ref[...] = (n * g_ref[...]).astype(BF16)

    return pl.pallas_call(
        body, name="norm_fwd", out_shape=jax.ShapeDtypeStruct((t, D), BF16), grid=(t // tt,),
        in_specs=[pl.BlockSpec((tt, D), lambda i: (i, 0)), pl.BlockSpec((1, D), lambda i: (0, 0))],
        out_specs=pl.BlockSpec((tt, D), lambda i: (i, 0)),
        compiler_params=_params(("parallel",)),
    )(x, gain)


def _lru_gates(xc, wa_ref, wx_ref, ba, bx, lam):
    xcb = xc.astype(BF16)
    ra = jnp.concatenate([_dot(xcb[:, n * HD:(n + 1) * HD], wa_ref[n]) for n in range(NH)], axis=1) + ba
    ix = jnp.concatenate([_dot(xcb[:, n * HD:(n + 1) * HD], wx_ref[n]) for n in range(NH)], axis=1) + bx
    r = _sig(ra)
    ig = _sig(ix)
    z = -lam
    sp = jnp.maximum(z, 0.0) + jnp.log1p(jnp.exp(-jnp.abs(z)))
    log_a = -LRU_C * r * sp
    a = jnp.exp(log_a)
    z2 = 2.0 * log_a
    series = -z2 * (1.0 + z2 * (0.5 + z2 * (1.0 / 6.0 + z2 * (1.0 / 24.0))))
    om = jnp.where(z2 > -0.02, series, 1.0 - jnp.exp(z2))
    mult = jnp.sqrt(om)
    return xcb, r, ig, sp, a, mult


def _mixer_a_fwd(p4, cw, cb, wa, wx, ba, bx, lam, tt):
    t = p4.shape[1]

    def body(p_ref, cw_ref, cb_ref, wa_ref, wx_ref, ba_ref, bx_ref, lam_ref, ya_ref, h_ref, halo, hc):
        i = pl.program_id(0)

        @pl.when(i == 0)
        def _():
            halo[...] = jnp.zeros((8, D), F32)
            hc[...] = jnp.zeros((1, D), F32)

        xa = p_ref[:, 0:D]
        ga = p_ref[:, D:2 * D]
        xe = jnp.concatenate([halo[...], xa], axis=0)
        xc = (cb_ref[...] + cw_ref[3:4, :] * xe
              + sum(cw_ref[3 - s:4 - s, :] * pltpu.roll(xe, s, 0) for s in (1, 2, 3)))[8:, :]
        halo[...] = xa[tt - 8:, :]
        _, _, ig, _, a, mult = _lru_gates(xc, wa_ref, wx_ref, ba_ref[...], bx_ref[...], lam_ref[...])
        u = mult * ig * xc
        h = _scan_down(a, u, hc[...])
        hc[...] = h[tt - 1:tt, :]
        h_ref[...] = h
        ya_ref[...] = (h * _gelu(ga)).astype(BF16)

    full = lambda shape: pl.BlockSpec(shape, lambda i: (0,) * len(shape))
    return pl.pallas_call(
        body, name="mixer_a_fwd",
        out_shape=(jax.ShapeDtypeStruct((t, D), BF16), jax.ShapeDtypeStruct((t, D), F32)),
        grid=(t // tt,),
        in_specs=[pl.BlockSpec((None, tt, 2 * D), lambda i: (SLOT_A, i, 0)),
                  full((4, D)), full((1, D)), full((NH, HD, HD)), full((NH, HD, HD)),
                  full((1, D)), full((1, D)), full((1, D))],
        out_specs=(pl.BlockSpec((tt, D), lambda i: (i, 0)), pl.BlockSpec((tt, D), lambda i: (i, 0))),
        scratch_shapes=[pltpu.VMEM((8, D), F32), pltpu.VMEM((1, D), F32)],
        compiler_params=_params(("arbitrary",), VMEM_BIG),
    )(p4, cw, cb, wa, wx, ba, bx, lam)


def _chunk_masks(tt):
    row = lax.broadcasted_iota(jnp.int32, (tt, tt), 0)
    col = lax.broadcasted_iota(jnp.int32, (tt, tt), 1)
    same = jnp.right_shift(row, 5) == jnp.right_shift(col, 5)
    return same & (col <= row)


def _hg_head_fwd(q, fz, lbh):
    sg = _sig(fz)
    sgn = _sig(-fz)
    f = lbh + (1.0 - lbh) * sg
    logf = jnp.log(f)
    k = (1.0 - lbh) * sgn
    g = _chunk_cumsum(logf)
    gu = _chunk_last(g) - g
    eg = jnp.exp(g)
    eng = jnp.exp(-g)
    egu = jnp.exp(gu)
    qt = q * eg
    kt = k * eng
    kd = k * egu
    return sg, sgn, f, k, g, eg, eng, egu, qt, kt, kd


def _lb_of(logits_ref):
    return _sig(logits_ref[0:1, :] - logits_ref[1:2, :])


def _hgrn2_fwd(p4, logits, gnorm, tt):
    t = p4.shape[1]
    nc = tt // CH

    def body(p_ref, lg_ref, gn_ref, yb_ref, o_ref, ss_ref, st):
        i = pl.program_id(0)

        @pl.when(i == 0)
        def _():
            st[...] = jnp.zeros((NH, HD, HD), F32)

        low = _chunk_masks(tt)
        lb = _lb_of(lg_ref)
        for h in range(NH):
            c0 = h * HD
            q = p_ref[0, :, c0:c0 + HD]
            fz = p_ref[0, :, D + c0:D + c0 + HD]
            v = p_ref[1, :, c0:c0 + HD]
            og = p_ref[1, :, D + c0:D + c0 + HD]
            lbh = lb[:, c0:c0 + HD]
            _, _, _, _, g, _, _, _, qt, kt, kd = _hg_head_fwd(q, fz, lbh)
            qtb, ktb, kdb, vb = qt.astype(BF16), kt.astype(BF16), kd.astype(BF16), v.astype(BF16)
            att = jnp.where(low, _dot_nt(qtb, ktb), 0.0)
            o_in = _dot(att.astype(BF16), vb)
            s_t = st[h]
            pieces = []
            for c in range(nc):
                sl = slice(c * CH, (c + 1) * CH)
                s_bf = s_t.astype(BF16)
                ss_ref[c, h] = s_bf
                pieces.append(o_in[sl] + _dot_nt(qtb[sl], s_bf))
                dec = jnp.exp(g[c * CH + CH - 1:c * CH + CH, :])
                s_t = s_t * dec + _dot_tn(vb[sl], kdb[sl])
            st[h] = s_t
            o = jnp.concatenate(pieces, axis=0)
            r, n = _rms_stats(o)
            ob = n * gn_ref[:, c0:c0 + HD]
            o_ref[:, c0:c0 + HD] = o
            yb_ref[:, c0:c0 + HD] = (ob * (og * _sig(og))).astype(BF16)

    return pl.pallas_call(
        body, name="hgrn2_fwd",
        out_shape=(jax.ShapeDtypeStruct((t, D), BF16), jax.ShapeDtypeStruct((t, D), F32),
                   jax.ShapeDtypeStruct((t // CH, NH, HD, HD), BF16)),
        grid=(t // tt,),
        in_specs=[pl.BlockSpec((2, tt, 2 * D), lambda i: (0, i, 0)),
                  pl.BlockSpec((2, D), lambda i: (0, 0)), pl.BlockSpec((1, D), lambda i: (0, 0))],
        out_specs=(pl.BlockSpec((tt, D), lambda i: (i, 0)), pl.BlockSpec((tt, D), lambda i: (i, 0)),
                   pl.BlockSpec((nc, NH, HD, HD), lambda i: (i, 0, 0, 0))),
        scratch_shapes=[pltpu.VMEM((NH, HD, HD), F32)],
        compiler_params=_params(("arbitrary",), VMEM_BIG),
    )(p4, logits, gnorm)


def _mid_fwd(ya, yb, p4, x, wa, wb, wo, g_pm, g_pf, tt):
    t = x.shape[0]

    def body(ya_ref, yb_ref, gt_ref, x_ref, wa_ref, wb_ref, wo_ref, gpm_ref, gpf_ref,
             za_ref, zb_ref, mix_ref, m2_ref, x1_ref, h2_ref):
        za = _dot(ya_ref[...], wa_ref[...])
        zb = _dot(yb_ref[...], wb_ref[...])
        mix = _sig(gt_ref[:, 0:D]) * za + _sig(gt_ref[:, D:2 * D]) * zb
        mixb = mix.astype(BF16)
        m2 = _dot(mixb, wo_ref[...])
        _, n2 = _rms_stats(m2)
        x1 = x_ref[...] + n2 * gpm_ref[...]
        _, n1 = _rms_stats(x1)
        za_ref[...] = za.astype(BF16)
        zb_ref[...] = zb.astype(BF16)
        mix_ref[...] = mixb
        m2_ref[...] = m2
        x1_ref[...] = x1
        h2_ref[...] = (n1 * gpf_ref[...]).astype(BF16)

    row = lambda dt: jax.ShapeDtypeStruct((t, D), dt)
    tile = pl.BlockSpec((tt, D), lambda i: (i, 0))
    wsp = pl.BlockSpec((D, D), lambda i: (0, 0))
    vec = pl.BlockSpec((1, D), lambda i: (0, 0))
    return pl.pallas_call(
        body, name="mid_fwd",
        out_shape=(row(BF16), row(BF16), row(BF16), row(F32), row(F32), row(BF16)),
        grid=(t // tt,),
        in_specs=[tile, tile, pl.BlockSpec((None, tt, 2 * D), lambda i: (SLOT_G, i, 0)), tile,
                  wsp, wsp, wsp, vec, vec],
        out_specs=(tile,) * 6,
        compiler_params=_params(("parallel",), VMEM_BIG),
    )(ya, yb, p4, x, wa, wb, wo, g_pm, g_pf)


def _ffn_act_fwd(up_pre, cfw, cfb, tt):
    t = up_pre.shape[0]

    def body(u_ref, halo_ref, w_ref, b_ref, y_ref):
        i = pl.program_id(0)
        halves = []
        for c0 in (0, DFF):
            cs = slice(c0, c0 + DFF)
            xe = jnp.concatenate([jnp.where(i > 0, halo_ref[:, cs], 0.0), u_ref[:, cs]], axis=0)
            up = (b_ref[:, cs] + w_ref[2:3, cs] * xe + w_ref[1:2, cs] * pltpu.roll(xe, 1, 0)
                  + w_ref[0:1, cs] * pltpu.roll(xe, 2, 0))
            halves.append(up[8:, :])
        y_ref[...] = (_gelu(halves[0]) * halves[1]).astype(BF16)

    hb = tt // 8
    return pl.pallas_call(
        body, name="ffn_act_fwd", out_shape=jax.ShapeDtypeStruct((t, DFF), BF16), grid=(t // tt,),
        in_specs=[pl.BlockSpec((tt, DUP), lambda i: (i, 0)),
                  pl.BlockSpec((8, DUP), lambda i: (jnp.maximum(i * hb - 1, 0), 0)),
                  pl.BlockSpec((3, DUP), lambda i: (0, 0)), pl.BlockSpec((1, DUP), lambda i: (0, 0))],
        out_specs=pl.BlockSpec((tt, DFF), lambda i: (i, 0)),
        compiler_params=_params(("parallel",), VMEM_BIG),
    )(up_pre, up_pre, cfw, cfb)


def _down_loss(y, wdn, x1, tgt, g_post, tt):
    t = x1.shape[0]

    def body(y_ref, w_ref, x1_ref, t_ref, g_ref, dx2_ref, dm3_ref, lossv_ref, dg_ref):
        i = pl.program_id(0)
        m3 = _dot(y_ref[...], w_ref[...])
        r, n3 = _rms_stats(m3)
        g = g_ref[...]
        e = x1_ref[...] + n3 * g - t_ref[...]
        dx2 = e * (1.0 / D)
        dx2_ref[...] = dx2
        dm3_ref[...] = _rms_bwd(dx2 * g, n3, r).astype(BF16)
        lv = jnp.sum(e * e, axis=0, keepdims=True)
        dgv = jnp.sum(dx2 * n3, axis=0, keepdims=True)

        @pl.when(i == 0)
        def _():
            lossv_ref[...] = lv
            dg_ref[...] = dgv

        @pl.when(i > 0)
        def _():
            lossv_ref[...] += lv
            dg_ref[...] += dgv

    tile = pl.BlockSpec((tt, D), lambda i: (i, 0))
    vec = pl.BlockSpec((1, D), lambda i: (0, 0))
    return pl.pallas_call(
        body, name="down_loss",
        out_shape=(jax.ShapeDtypeStruct((t, D), F32), jax.ShapeDtypeStruct((t, D), BF16),
                   jax.ShapeDtypeStruct((1, D), F32), jax.ShapeDtypeStruct((1, D), F32)),
        grid=(t // tt,),
        in_specs=[pl.BlockSpec((tt, DFF), lambda i: (i, 0)), pl.BlockSpec((DFF, D), lambda i: (0, 0)),
                  tile, tile, vec],
        out_specs=(tile, tile, vec, vec),
        compiler_params=_params(("arbitrary",), VMEM_BIG),
    )(y, wdn, x1, tgt, g_post)


def _ffn_act_bwd(dm3, wdn, up_pre, cfw, cfb, tt):
    t = up_pre.shape[0]
    nt = t // tt

    def body(dm_ref, dmn_ref, w_ref, u_ref, up_ref, un_ref, cw_ref, cb_ref, du_ref, dcw_ref, dcb_ref):
        i = pl.program_id(0)
        n = tt + 8
        next_live = jnp.where(i < nt - 1, 1.0, 0.0)
        dy = jnp.concatenate([_dot_nt(dm_ref[...], w_ref[...]),
                              _dot_nt(dmn_ref[...], w_ref[...])[0:8, :] * next_live], axis=0)
        ups, xs = [], []
        for c0 in (0, DFF):
            cs = slice(c0, c0 + DFF)
            xe = jnp.concatenate([jnp.where(i > 0, up_ref[:, cs], 0.0), u_ref[:, cs], un_ref[:, cs]], axis=0)
            x1 = pltpu.roll(xe, 1, 0)
            x2 = pltpu.roll(xe, 2, 0)
            up = cb_ref[:, cs] + cw_ref[2:3, cs] * xe + cw_ref[1:2, cs] * x1 + cw_ref[0:1, cs] * x2
            ups.append(up[8:, :])
            xs.append((x2[8:tt + 8, :], x1[8:tt + 8, :], xe[8:tt + 8, :]))
        gl, dgl = _gelu_and_grad(ups[0])
        ds = (dy * ups[1] * dgl, dy * gl)
        dcw_parts, dcb_parts = [], []
        for hh, c0 in enumerate((0, DFF)):
            cs = slice(c0, c0 + DFF)
            dd = ds[hh]
            du = cw_ref[2:3, cs] * dd + cw_ref[1:2, cs] * pltpu.roll(dd, n - 1, 0) \
                + cw_ref[0:1, cs] * pltpu.roll(dd, n - 2, 0)
            du_ref[:, cs] = du[0:tt, :].astype(BF16)
            dup = dd[0:tt, :]
            dcw_parts.append(jnp.concatenate(
                [jnp.sum(dup * xs[hh][k], axis=0, keepdims=True) for k in range(3)], axis=0))
            dcb_parts.append(jnp.sum(dup, axis=0, keepdims=True))
        dcw = jnp.concatenate(dcw_parts, axis=1)
        dcb = jnp.concatenate(dcb_parts, axis=1)

        @pl.when(i == 0)
        def _():
            dcw_ref[...] = dcw
            dcb_ref[...] = dcb

        @pl.when(i > 0)
        def _():
            dcw_ref[...] += dcw
            dcb_ref[...] += dcb

    hb = tt // 8
    last8 = t // 8 - 1
    return pl.pallas_call(
        body, name="ffn_act_bwd",
        out_shape=(jax.ShapeDtypeStruct((t, DUP), BF16), jax.ShapeDtypeStruct((3, DUP), F32),
                   jax.ShapeDtypeStruct((1, DUP), F32)),
        grid=(nt,),
        in_specs=[pl.BlockSpec((tt, D), lambda i: (i, 0)),
                  pl.BlockSpec((16, D), lambda i: (jnp.minimum((i + 1) * (tt // 16), t // 16 - 1), 0)),
                  pl.BlockSpec((DFF, D), lambda i: (0, 0)),
                  pl.BlockSpec((tt, DUP), lambda i: (i, 0)),
                  pl.BlockSpec((8, DUP), lambda i: (jnp.maximum(i * hb - 1, 0), 0)),
                  pl.BlockSpec((8, DUP), lambda i: (jnp.minimum((i + 1) * hb, last8), 0)),
                  pl.BlockSpec((3, DUP), lambda i: (0, 0)), pl.BlockSpec((1, DUP), lambda i: (0, 0))],
        out_specs=(pl.BlockSpec((tt, DUP), lambda i: (i, 0)), pl.BlockSpec((3, DUP), lambda i: (0, 0)),
                   pl.BlockSpec((1, DUP), lambda i: (0, 0))),
        compiler_params=_params(("arbitrary",), VMEM_BIG),
    )(dm3, dm3, wdn, up_pre, up_pre, up_pre, cfw, cfb)


def _mid_bwd(dh2, dx2, x1, m2, za, zb, p4, wa, wb, wo, g_pm, g_pf, tt):
    t = x1.shape[0]

    def body(dh2_ref, dx2_ref, x1_ref, m2_ref, za_ref, zb_ref, gt_ref, wa_ref, wb_ref, wo_ref, gpm_ref, gpf_ref,
             dx1_ref, dm2_ref, dza_ref, dzb_ref, dya_ref, dyb_ref, dp_ref, dgpm_ref, dgpf_ref):
        i = pl.program_id(0)
        r1, n1 = _rms_stats(x1_ref[...])
        dh2 = dh2_ref[...]
        dx1 = dx2_ref[...] + _rms_bwd(dh2 * gpf_ref[...], n1, r1)
        r2, n2 = _rms_stats(m2_ref[...])
        dm2 = _rms_bwd(dx1 * gpm_ref[...], n2, r2).astype(BF16)
        dmix = _dot_nt(dm2, wo_ref[...])
        sa = _sig(gt_ref[:, 0:D])
        sb = _sig(gt_ref[:, D:2 * D])
        dza = (dmix * sa).astype(BF16)
        dzb = (dmix * sb).astype(BF16)
        dp_ref[:, 0:D] = (dmix * za_ref[...].astype(F32) * sa * (1.0 - sa)).astype(BF16)
        dp_ref[:, D:2 * D] = (dmix * zb_ref[...].astype(F32) * sb * (1.0 - sb)).astype(BF16)
        dx1_ref[...] = dx1
        dm2_ref[...] = dm2
        dza_ref[...] = dza
        dzb_ref[...] = dzb
        dya_ref[...] = _dot_nt(dza, wa_ref[...])
        dyb_ref[...] = _dot_nt(dzb, wb_ref[...])
        dgpf = jnp.sum(dh2 * n1, axis=0, keepdims=True)
        dgpm = jnp.sum(dx1 * n2, axis=0, keepdims=True)

        @pl.when(i == 0)
        def _():
            dgpf_ref[...] = dgpf
            dgpm_ref[...] = dgpm

        @pl.when(i > 0)
        def _():
            dgpf_ref[...] += dgpf
            dgpm_ref[...] += dgpm

    row = lambda dt: jax.ShapeDtypeStruct((t, D), dt)
    tile = pl.BlockSpec((tt, D), lambda i: (i, 0))
    wsp = pl.BlockSpec((D, D), lambda i: (0, 0))
    vec = pl.BlockSpec((1, D), lambda i: (0, 0))
    gates = pl.BlockSpec((None, tt, 2 * D), lambda i: (SLOT_G, i, 0))
    return pl.pallas_call(
        body, name="mid_bwd",
        out_shape=(row(F32), row(BF16), row(BF16), row(BF16), row(F32), row(F32),
                   jax.ShapeDtypeStruct((NCHIP, t, 2 * D), BF16),
                   jax.ShapeDtypeStruct((1, D), F32), jax.ShapeDtypeStruct((1, D), F32)),
        grid=(t // tt,),
        in_specs=[tile, tile, tile, tile, tile, tile, gates, wsp, wsp, wsp, vec, vec],
        out_specs=(tile, tile, tile, tile, tile, tile, gates, vec, vec),
        compiler_params=_params(("arbitrary",), VMEM_BIG),
    )(dh2, dx2, x1, m2, za, zb, p4, wa, wb, wo, g_pm, g_pf)


def _hgrn2_bwd(p4, o_all, ss, dyb, dp4, logits, gnorm, tt):
    t = p4.shape[1]
    nt = t // tt
    nc = tt // CH

    def body(p_ref, o_ref, ss_ref, dyb_ref, dp_in, lg_ref, gn_ref, dp_ref, dlb_ref, dgn_ref, dst):
        del dp_in
        i = pl.program_id(0)

        @pl.when(i == 0)
        def _():
            dst[...] = jnp.zeros((NH, HD, HD), F32)

        low = _chunk_masks(tt)
        lb = _lb_of(lg_ref)
        dlb_parts = []
        dgn_parts = []
        for h in range(NH):
            c0 = h * HD
            q = p_ref[0, :, c0:c0 + HD]
            fz = p_ref[0, :, D + c0:D + c0 + HD]
            v = p_ref[1, :, c0:c0 + HD]
            og = p_ref[1, :, D + c0:D + c0 + HD]
            lbh = lb[:, c0:c0 + HD]
            gh = gn_ref[:, c0:c0 + HD]
            sg, sgn, f, k, g, eg, eng, egu, qt, kt, kd = _hg_head_fwd(q, fz, lbh)
            qtb, ktb, kdb, vb = qt.astype(BF16), kt.astype(BF16), kd.astype(BF16), v.astype(BF16)
            att = jnp.where(low, _dot_nt(qtb, ktb), 0.0).astype(BF16)
            o = o_ref[:, c0:c0 + HD]
            dyb_h = dyb_ref[:, c0:c0 + HD]
            r, n = _rms_stats(o)
            so = _sig(og)
            dob = dyb_h * (og * so)
            dog = dyb_h * (n * gh) * (so * (1.0 + og * (1.0 - so)))
            dgn_parts.append(jnp.sum(dob * n, axis=0, keepdims=True))
            do = _rms_bwd(dob * gh, n, r)
            dob_ = do.astype(BF16)
            d_att = jnp.where(low, _dot_nt(dob_, vb), 0.0).astype(BF16)
            dv_in = _dot_tn(att, dob_)
            dqt_in = _dot(d_att, ktb)
            dkt = _dot_tn(d_att, qtb)
            ds_t = dst[h]
            dv_p, dqt_p, dkd_p, dgl_p = [None] * nc, [None] * nc, [None] * nc, [None] * nc
            for c in reversed(range(nc)):
                sl = slice(c * CH, (c + 1) * CH)
                s_prev = ss_ref[c, h]
                ds_bf = ds_t.astype(BF16)
                dec = jnp.exp(g[c * CH + CH - 1:c * CH + CH, :])
                dv_p[c] = dv_in[sl] + _dot_nt(kdb[sl], ds_bf)
                dqt_p[c] = dqt_in[sl] + _dot(dob_[sl], s_prev)
                dkd_p[c] = _dot(vb[sl], ds_bf)
                ddec = jnp.sum(s_prev.astype(F32) * ds_t, axis=0, keepdims=True)
                dgl_p[c] = jnp.broadcast_to(ddec * dec, (CH, HD))
                ds_t = ds_t * dec + _dot_tn(dob_[sl], qtb[sl])
            dst[h] = ds_t
            dv = jnp.concatenate(dv_p, axis=0)
            dqt = jnp.concatenate(dqt_p, axis=0)
            dkd = jnp.concatenate(dkd_p, axis=0)
            dgl = jnp.concatenate(dgl_p, axis=0)
            dq = dqt * eg
            dk = dkt * eng + dkd * egu
            dg = dqt * qt - dkt * kt
            dgu = dkd * kd
            dlogf = _chunk_revcumsum(dg - dgu) + _chunk_total(dgu) + dgl
            one_m_lb = 1.0 - lbh
            dfz = one_m_lb * sg * sgn * (dlogf / f - dk)
            dlb_parts.append(jnp.sum(sgn * (dlogf / f - dk), axis=0, keepdims=True))
            dp_ref[0, :, c0:c0 + HD] = dq.astype(BF16)
            dp_ref[0, :, D + c0:D + c0 + HD] = dfz.astype(BF16)
            dp_ref[1, :, c0:c0 + HD] = dv.astype(BF16)
            dp_ref[1, :, D + c0:D + c0 + HD] = dog.astype(BF16)
        dlb = jnp.concatenate(dlb_parts, axis=1)
        dgn = jnp.concatenate(dgn_parts, axis=1)

        @pl.when(i == 0)
        def _():
            dlb_ref[0:1, :] = dlb
            dgn_ref[...] = dgn

        @pl.when(i > 0)
        def _():
            dlb_ref[0:1, :] += dlb
            dgn_ref[...] += dgn

        @pl.when(i == nt - 1)
        def _():
            d0 = dlb_ref[0:1, :] * lb * (1.0 - lb)
            dlb_ref[0:1, :] = d0
            dlb_ref[1:2, :] = -d0

    rev = lambda i: nt - 1 - i
    vec = pl.BlockSpec((1, D), lambda i: (0, 0))
    return pl.pallas_call(
        body, name="hgrn2_bwd",
        out_shape=(jax.ShapeDtypeStruct(dp4.shape, BF16), jax.ShapeDtypeStruct((2, D), F32),
                   jax.ShapeDtypeStruct((1, D), F32)),
        grid=(nt,),
        in_specs=[pl.BlockSpec((2, tt, 2 * D), lambda i: (0, rev(i), 0)),
                  pl.BlockSpec((tt, D), lambda i: (rev(i), 0)),
                  pl.BlockSpec((nc, NH, HD, HD), lambda i: (rev(i), 0, 0, 0)),
                  pl.BlockSpec((tt, D), lambda i: (rev(i), 0)),
                  pl.BlockSpec(memory_space=pl.ANY),
                  pl.BlockSpec((2, D), lambda i: (0, 0)), vec],
        out_specs=(pl.BlockSpec((2, tt, 2 * D), lambda i: (0, rev(i), 0)),
                   pl.BlockSpec((2, D), lambda i: (0, 0)), vec),
        scratch_shapes=[pltpu.VMEM((NH, HD, HD), F32)],
        input_output_aliases={4: 0},
        compiler_params=_params(("arbitrary",), VMEM_BIG),
    )(p4, o_all, ss, dyb, dp4, logits, gnorm)


def _mixer_a_bwd(p4, hseq, dya, dp4, cw, cb, wa, wx, ba, bx, lam, tt):
    t = p4.shape[1]
    nt = t // tt
    steps = tt.bit_length() - 1

    def body(p_ref, ph_ref, h_ref, hh_ref, dya_ref, dp_in, cw_ref, cb_ref, wa_ref, wx_ref, ba_ref, bx_ref, lam_ref,
             dp_ref, dcw_ref, dcb_ref, dwa_ref, dwx_ref, dba_ref, dbx_ref, dlam_ref,
             dnext, dhc, afc):
        del dp_in
        i = pl.program_id(0)
        first_tile = i == nt - 1

        @pl.when(i == 0)
        def _():
            dnext[...] = jnp.zeros((8, D), F32)
            dhc[...] = jnp.zeros((1, D), F32)
            afc[...] = jnp.zeros((1, D), F32)

        xa = p_ref[:, 0:D]
        ga = p_ref[:, D:2 * D]
        xe = jnp.concatenate([jnp.where(first_tile, 0.0, ph_ref[:, 0:D]), xa], axis=0)
        xs = [xe[8:, :]] + [pltpu.roll(xe, s, 0)[8:, :] for s in (1, 2, 3)]
        xc = cb_ref[...] + sum(cw_ref[3 - s:4 - s, :] * xs[s] for s in range(4))
        lam = lam_ref[...]
        xcb, r, ig, sp, a, mult = _lru_gates(xc, wa_ref, wx_ref, ba_ref[...], bx_ref[...], lam)
        h = h_ref[...]
        gl, dgl = _gelu_and_grad(ga)
        dya = dya_ref[...]
        dga = dya * h * dgl
        rows = lax.broadcasted_iota(jnp.int32, (tt, 1), 0)
        a_next = jnp.where(rows == tt - 1, afc[...], pltpu.roll(a, tt - 1, 0))
        dh = _scan_up(a_next, dya * gl, dhc[...])
        dhc[...] = dh[0:1, :]
        afc[...] = a[0:1, :]
        h_prev = jnp.where(rows == 0, jnp.where(first_tile, 0.0, hh_ref[7:8, :]), pltpu.roll(h, 1, 0))
        da = dh * h_prev
        dmult = dh * ig * xc
        di = dh * mult * xc
        dlog_a = da * a - dmult * a * a / mult
        dr = dlog_a * (-LRU_C * sp)
        dsp = jnp.sum(dlog_a * (-LRU_C * r), axis=0, keepdims=True)
        dra = dr * r * (1.0 - r)
        dix = di * ig * (1.0 - ig)
        drab = dra.astype(BF16)
        dixb = dix.astype(BF16)
        dxc_lin = []
        dwa_new = []
        dwx_new = []
        for n in range(NH):
            cs = slice(n * HD, (n + 1) * HD)
            dxc_lin.append(_dot_nt(drab[:, cs], wa_ref[n]) + _dot_nt(dixb[:, cs], wx_ref[n]))
            dwa_new.append(_dot_tn(xcb[:, cs], drab[:, cs]))
            dwx_new.append(_dot_tn(xcb[:, cs], dixb[:, cs]))
        dxc = dh * mult * ig + jnp.concatenate(dxc_lin, axis=1)
        de = jnp.concatenate([dxc, dnext[...]], axis=0)
        dxa = (cw_ref[3:4, :] * de
               + sum(cw_ref[3 - s:4 - s, :] * pltpu.roll(de, tt + 8 - s, 0) for s in (1, 2, 3)))[0:tt, :]
        dnext[...] = dxc[0:8, :]
        dp_ref[:, 0:D] = dxa.astype(BF16)
        dp_ref[:, D:2 * D] = dga.astype(BF16)
        dcw = jnp.concatenate(
            [jnp.sum(dxc * xs[3 - k], axis=0, keepdims=True) for k in range(4)], axis=0)
        dcb = jnp.sum(dxc, axis=0, keepdims=True)
        dba = jnp.sum(dra, axis=0, keepdims=True)
        dbx = jnp.sum(dix, axis=0, keepdims=True)
        dlam = dsp * (-_sig(-lam))

        @pl.when(i == 0)
        def _():
            dcw_ref[...] = dcw
            dcb_ref[...] = dcb
            dba_ref[...] = dba
            dbx_ref[...] = dbx
            dlam_ref[...] = dlam
            for n in range(NH):
                dwa_ref[n] = dwa_new[n]
                dwx_ref[n] = dwx_new[n]

        @pl.when(i > 0)
        def _():
            dcw_ref[...] += dcw
            dcb_ref[...] += dcb
            dba_ref[...] += dba
            dbx_ref[...] += dbx
            dlam_ref[...] += dlam
            for n in range(NH):
                dwa_ref[n] += dwa_new[n]
                dwx_ref[n] += dwx_new[n]

    rev = lambda i: nt - 1 - i
    hb = tt // 8
    full = lambda shape: pl.BlockSpec(shape, lambda i: (0,) * len(shape))
    vecs = jax.ShapeDtypeStruct((1, D), F32)
    blk = jax.ShapeDtypeStruct((NH, HD, HD), F32)
    return pl.pallas_call(
        body, name="mixer_a_bwd",
        out_shape=(jax.ShapeDtypeStruct(dp4.shape, BF16), jax.ShapeDtypeStruct((4, D), F32), vecs, blk, blk,
                   vecs, vecs, vecs),
        grid=(nt,),
        in_specs=[pl.BlockSpec((None, tt, 2 * D), lambda i: (SLOT_A, rev(i), 0)),
                  pl.BlockSpec((None, 8, 2 * D), lambda i: (SLOT_A, jnp.maximum(rev(i) * hb - 1, 0), 0)),
                  pl.BlockSpec((tt, D), lambda i: (rev(i), 0)),
                  pl.BlockSpec((8, D), lambda i: (jnp.maximum(rev(i) * hb - 1, 0), 0)),
                  pl.BlockSpec((tt, D), lambda i: (rev(i), 0)),
                  pl.BlockSpec(memory_space=pl.ANY),
                  full((4, D)), full((1, D)), full((NH, HD, HD)), full((NH, HD, HD)),
                  full((1, D)), full((1, D)), full((1, D))],
        out_specs=(pl.BlockSpec((None, tt, 2 * D), lambda i: (SLOT_A, rev(i), 0)),
                   full((4, D)), full((1, D)), full((NH, HD, HD)), full((NH, HD, HD)),
                   full((1, D)), full((1, D)), full((1, D))),
        scratch_shapes=[pltpu.VMEM((8, D), F32), pltpu.VMEM((1, D), F32), pltpu.VMEM((1, D), F32)],
        input_output_aliases={5: 0},
        compiler_params=_params(("arbitrary",), VMEM_BIG),
    )(p4, p4, hseq, hseq, dya, dp4, cw, cb, wa, wx, ba, bx, lam)


def _norm_bwd(dh1, dx1, x, gain, tt):
    t = x.shape[0]

    def body(dh_ref, dx1_ref, x_ref, g_ref, dx_ref, dg_ref):
        i = pl.program_id(0)
        r, n = _rms_stats(x_ref[...])
        dh = dh_ref[...]
        dx_ref[...] = dx1_ref[...] + _rms_bwd(dh * g_ref[...], n, r)
        dgv = jnp.sum(dh * n, axis=0, keepdims=True)

        @pl.when(i == 0)
        def _():
            dg_ref[...] = dgv

        @pl.when(i > 0)
        def _():
            dg_ref[...] += dgv

    tile = pl.BlockSpec((tt, D), lambda i: (i, 0))
    vec = pl.BlockSpec((1, D), lambda i: (0, 0))
    return pl.pallas_call(
        body, name="norm_bwd",
        out_shape=(jax.ShapeDtypeStruct((t, D), F32), jax.ShapeDtypeStruct((1, D), F32)),
        grid=(t // tt,), in_specs=[tile, tile, tile, vec], out_specs=(tile, vec),
        compiler_params=_params(("arbitrary",)),
    )(dh1, dx1, x, gain)


def _local_step(x, tgt, w, small):
    t = x.shape[0]
    tt = min(256, t)
    tm = min(1024, t)
    wa_bf = small["lru_wa"].astype(BF16)
    wx_bf = small["lru_wx"].astype(BF16)
    w_br_a = w["w_branch_a"].reshape(D, D)
    w_br_b = w["w_branch_b"].reshape(D, D)
    w_out = w["w_out"].reshape(D, D)
    w_down = w["w_down"].reshape(DFF, D)
    conv_a_w = w["conv_a_w"]
    conv_f_w = w["conv_f_w"]

    h1 = _norm_fwd(x, small["norm_pre_mix"], tt)
    p4 = _mm_nn_sharded(h1, w["w_in"], F32, tm, "mm_in", _slot_of_chip)
    ya, hseq = _mixer_a_fwd(p4, conv_a_w, small["conv_a_b"], wa_bf, wx_bf, small["lru_ba"], small["lru_bx"],
                            small["lru_lambda"], tt)
    yb, o_all, ss = _hgrn2_fwd(p4, small["hg_lb_logits"], small["hg_norm_g"], tt)
    za, zb, mixb, m2, x1, h2 = _mid_fwd(ya, yb, p4, x, w_br_a, w_br_b, w_out, small["norm_post_mix"],
                                        small["norm_pre_ffn"], tt)
    up_pre = _mm_nn_sharded(h2, w["w_up"], F32, tm, "mm_up")
    tf = min(128, t)
    y = _ffn_act_fwd(up_pre, conv_f_w, small["conv_f_b"], tf)
    dx2, dm3, lossv, d_norm_post_ffn = _down_loss(y, w_down, x1, tgt, small["norm_post_ffn"], tt)

    d_w_down = _mm_tn(y, dm3, DFF // 2, D, tm, "mm_dw_down")
    dup_pre, d_conv_f_w, d_conv_f_b = _ffn_act_bwd(dm3, w_down, up_pre, conv_f_w, small["conv_f_b"], tf)
    d_w_up = _mm_tn(h2, dup_pre, D, SH_UP, tm, "mm_dw_up", stacked_out=True)
    dh2 = _mm_nt_sharded(dup_pre, w["w_up"], tm, "mm_dh2")
    dx1, dm2, dza, dzb, dya, dyb, dp4, d_norm_post_mix, d_norm_pre_ffn = _mid_bwd(
        dh2, dx2, x1, m2, za, zb, p4, w_br_a, w_br_b, w_out, small["norm_post_mix"], small["norm_pre_ffn"], tt)
    d_w_out = _mm_tn(mixb, dm2, D, D, tm, "mm_dw_out")
    d_w_br_a = _mm_tn(ya, dza, D, D, tm, "mm_dw_bra")
    d_w_br_b = _mm_tn(yb, dzb, D, D, tm, "mm_dw_brb")
    dp4, d_lb, d_hg_norm_g = _hgrn2_bwd(p4, o_all, ss, dyb, dp4, small["hg_lb_logits"], small["hg_norm_g"], tt)
    dp4, d_conv_a_w, d_conv_a_b, d_lru_wa, d_lru_wx, d_lru_ba, d_lru_bx, d_lru_lambda = _mixer_a_bwd(
        p4, hseq, dya, dp4, conv_a_w, small["conv_a_b"], wa_bf, wx_bf, small["lru_ba"], small["lru_bx"],
        small["lru_lambda"], tt)
    d_w_in = _mm_tn(h1, dp4, D, SH_IN, tm, "mm_dw_in", stacked_slot_fn=_slot_of_chip, stacked_out=True)
    dh1 = _mm_nt_sharded(dp4, w["w_in"], tm, "mm_dh1", stacked_slot_fn=_slot_of_chip)
    grad_x, d_norm_pre_mix = _norm_bwd(dh1, dx1, x, small["norm_pre_mix"], tt)

    big = {
        "w_in": d_w_in,
        "w_branch_a": d_w_br_a.reshape(NCHIP, SH_BR, D),
        "w_branch_b": d_w_br_b.reshape(NCHIP, SH_BR, D),
        "w_out": d_w_out.reshape(NCHIP, SH_BR, D),
        "w_up": d_w_up,
        "w_down": d_w_down.reshape(NCHIP, SH_DN, D),
    }
    smalls = {
        "norm_pre_mix": d_norm_pre_mix, "conv_a_b": d_conv_a_b, "lru_ba": d_lru_ba, "lru_bx": d_lru_bx,
        "lru_lambda": d_lru_lambda, "hg_lb_logits": d_lb, "hg_norm_g": d_hg_norm_g, "norm_post_mix": d_norm_post_mix,
        "norm_pre_ffn": d_norm_pre_ffn, "norm_post_ffn": d_norm_post_ffn, "lossv": lossv,
        "conv_a_w": d_conv_a_w, "lru_wa": d_lru_wa, "lru_wx": d_lru_wx,
        "conv_f_b": d_conv_f_b, "conv_f_w": d_conv_f_w,
    }
    return grad_x, big, smalls


BIG = ("w_in", "w_branch_a", "w_branch_b", "w_out", "w_up", "w_down")
BIG_SHAPE = {"w_in": (D, SH_IN), "w_branch_a": (SH_BR, D), "w_branch_b": (SH_BR, D), "w_out": (SH_BR, D),
             "w_up": (D, SH_UP), "w_down": (SH_DN, D)}
NBIG = len(BIG)
VEC_ROWS = (("norm_pre_mix", 0, 1), ("conv_a_b", 1, 1), ("lru_ba", 2, 1), ("lru_bx", 3, 1), ("lru_lambda", 4, 1),
            ("hg_lb_logits", 5, 2), ("hg_norm_g", 7, 1), ("norm_post_mix", 8, 1), ("norm_pre_ffn", 9, 1),
            ("norm_post_ffn", 10, 1))
ROW_LOSS = 11
ROW_CONV_A = 12
S1_ROWS = 16
S2_ROWS = 8


def _place():
    x, y, c = lax.axis_index("x"), lax.axis_index("y"), lax.axis_index("c")
    chips = [(1 - x, y), (x, 1 - y), (1 - x, 1 - y)]
    return x, y, c, 2 * x + y, chips


def _remote(src, dst, ssem, rsem, dev):
    return pltpu.make_async_remote_copy(src_ref=src, dst_ref=dst, send_sem=ssem, recv_sem=rsem,
                                        device_id=dev, device_id_type=MESH)


def _hbm_call(body, name, ins, out_shapes, n_sems, aliases=None):
    any_spec = pl.BlockSpec(memory_space=pl.ANY)
    return pl.pallas_call(
        body, name=name, out_shape=tuple(out_shapes),
        in_specs=[any_spec] * len(ins), out_specs=tuple([any_spec] * len(out_shapes)),
        scratch_shapes=[pltpu.SemaphoreType.DMA((n,)) for n in n_sems],
        input_output_aliases=aliases or {},
        compiler_params=pltpu.CompilerParams(has_side_effects=True),
    )(*ins)


def _gather_weights(stacked, conv_a_s, conv_f_s):
    ins = [stacked[n] for n in BIG] + [conv_a_s, conv_f_s]
    n_in = len(ins)
    halves = [BIG_SHAPE[n][0] // 2 for n in BIG]
    out_shapes = [jax.ShapeDtypeStruct(stacked[n].shape, stacked[n].dtype) for n in BIG]
    out_shapes += [jax.ShapeDtypeStruct((NCHIP,) + a.shape, a.dtype) for a in (conv_a_s, conv_f_s)]

    def body(*refs):
        src, dst = refs[:n_in], refs[n_in:2 * n_in]
        ssem, rsem, fssem, frsem, lsem = refs[2 * n_in:]
        x, y, c, j, chips = _place()

        def half(ref, w, which):
            return ref.at[pl.ds(which * halves[w], halves[w]), :]

        locs = [pltpu.make_async_copy(src[w], dst[w].at[j], lsem.at[w - NBIG]) for w in range(NBIG, n_in)]
        for cp in locs:
            cp.start()
        sends = []
        for w in range(n_in):
            for k, (cx, cy) in enumerate(chips):
                if w < NBIG:
                    mine = half(dst[w].at[j], w, c)
                    cp = _remote(mine, mine, ssem.at[3 * w + k], rsem.at[3 * w + k], (cx, cy, c))
                else:
                    cp = _remote(src[w], dst[w].at[j], ssem.at[3 * w + k], rsem.at[3 * w + k], (cx, cy, c))
                cp.start()
                sends.append(cp)
        fwds = []
        for w in range(n_in):
            for k, (cx, cy) in enumerate(chips):
                jk = 2 * cx + cy
                if w < NBIG:
                    got = half(dst[w].at[jk], w, c)
                    _remote(got, got, ssem.at[3 * w + k], rsem.at[3 * w + k], (cx, cy, c)).wait_recv()
                    cp = _remote(got, got, fssem.at[3 * w + k], frsem.at[3 * w + k], (x, y, 1 - c))
                    cp.start()
                    fwds.append(cp)
                else:
                    got = dst[w].at[jk]
                    _remote(got, got, ssem.at[3 * w + k], rsem.at[3 * w + k], (cx, cy, c)).wait_recv()
        for w in range(NBIG):
            for k, (cx, cy) in enumerate(chips):
                other = half(dst[w].at[2 * cx + cy], w, 1 - c)
                _remote(other, other, fssem.at[3 * w + k], frsem.at[3 * w + k], (x, y, 1 - c)).wait_recv()
        for cp in sends + fwds:
            cp.wait_send()
        for cp in locs:
            cp.wait()

    outs = _hbm_call(body, "gather_weights", ins, out_shapes, (3 * n_in, 3 * n_in, 3 * NBIG, 3 * NBIG, n_in - NBIG),
                     aliases={w: w for w in range(NBIG)})
    return dict(zip(BIG, outs[:NBIG])), outs[NBIG], outs[NBIG + 1]


def _reduce_stage1(big_g, s1, s2, s3):
    ins = [big_g[n] for n in BIG] + [s1, s2, s3]
    n_in = len(ins)
    halves = [BIG_SHAPE[n][0] // 2 for n in BIG]
    out_shapes = [jax.ShapeDtypeStruct((NCHIP, halves[w], BIG_SHAPE[n][1]), F32) for w, n in enumerate(BIG)]
    out_shapes += [jax.ShapeDtypeStruct(a.shape, F32) for a in (s1, s2, s3)]

    def body(*refs):
        src, dst = refs[:n_in], refs[n_in:2 * n_in]
        ssem, rsem = refs[2 * n_in:]
        x, y, c, _, _ = _place()
        cps = []
        for w in range(n_in):
            s_ = src[w].at[:, pl.ds((1 - c) * halves[w], halves[w]), :] if w < NBIG else src[w]
            cp = _remote(s_, dst[w], ssem.at[w], rsem.at[w], (x, y, 1 - c))
            cp.start()
            cps.append(cp)
        for cp in cps:
            cp.wait()

    outs = _hbm_call(body, "reduce_d2d_in", ins, out_shapes, (n_in, n_in))
    return dict(zip(BIG, outs[:NBIG])), outs[NBIG], outs[NBIG + 1], outs[NBIG + 2]


def _reduce_stage2(p_big, ps1, ps2, ps3):
    ins = [p_big[n] for n in BIG] + [ps1, ps2, ps3]
    n_in = len(ins)
    h1, h2, h3 = S1_ROWS // 2, DUP // 2, D
    out_shapes = [jax.ShapeDtypeStruct(p_big[n].shape, BF16) for n in BIG]
    out_shapes += [jax.ShapeDtypeStruct((NCHIP, h1, D), F32), jax.ShapeDtypeStruct((NCHIP, S2_ROWS, h2), F32),
                   jax.ShapeDtypeStruct((NCHIP, h3, HD), F32)]

    def body(*refs):
        src, dst = refs[:n_in], refs[n_in:2 * n_in]
        ssem, rsem = refs[2 * n_in:]
        x, y, c, j, chips = _place()

        def piece(w, to_chip):
            if w < NBIG:
                return src[w].at[to_chip]
            if w == NBIG:
                return src[w].at[pl.ds(c * h1, h1), :]
            if w == NBIG + 1:
                return src[w].at[:, pl.ds(c * h2, h2)]
            return src[w].at[pl.ds(c * h3, h3), :]

        sends = []
        for w in range(n_in):
            for k, (cx, cy) in enumerate(chips):
                cp = _remote(piece(w, 2 * cx + cy), dst[w].at[j], ssem.at[3 * w + k], rsem.at[3 * w + k], (cx, cy, c))
                cp.start()
                sends.append(cp)
        for w in range(n_in):
            for k, (cx, cy) in enumerate(chips):
                got = dst[w].at[2 * cx + cy]
                _remote(got, got, ssem.at[3 * w + k], rsem.at[3 * w + k], (cx, cy, c)).wait_recv()
        for cp in sends:
            cp.wait_send()

    outs = _hbm_call(body, "reduce_ici", ins, out_shapes, (3 * n_in, 3 * n_in))
    return dict(zip(BIG, outs[:NBIG])), outs[NBIG], outs[NBIG + 1], outs[NBIG + 2]


def _reduce_stage3(f_big, fs1, fs2, fs3):
    ins = [f_big[n] for n in BIG] + [fs1, fs2, fs3]
    n_in = len(ins)
    halves = [BIG_SHAPE[n][0] // 2 for n in BIG]
    h1, h2, h3 = S1_ROWS // 2, DUP // 2, D
    out_shapes = [jax.ShapeDtypeStruct(BIG_SHAPE[n], F32) for n in BIG]
    out_shapes += [jax.ShapeDtypeStruct((S1_ROWS, D), F32), jax.ShapeDtypeStruct((S2_ROWS, DUP), F32),
                   jax.ShapeDtypeStruct((2 * D, HD), F32)]

    def body(*refs):
        dst = refs[n_in:2 * n_in]
        ssem, rsem = refs[2 * n_in:]
        x, y, c, _, _ = _place()

        def place(w, which):
            if w < NBIG:
                return dst[w].at[pl.ds(which * halves[w], halves[w]), :]
            if w == NBIG:
                return dst[w].at[pl.ds(which * h1, h1), :]
            if w == NBIG + 1:
                return dst[w].at[:, pl.ds(which * h2, h2)]
            return dst[w].at[pl.ds(which * h3, h3), :]

        cps = [_remote(place(w, c), place(w, c), ssem.at[w], rsem.at[w], (x, y, 1 - c)) for w in range(n_in)]
        for cp in cps:
            cp.start()
        for w in range(n_in):
            got = place(w, 1 - c)
            _remote(got, got, ssem.at[w], rsem.at[w], (x, y, 1 - c)).wait_recv()
        for cp in cps:
            cp.wait_send()

    outs = _hbm_call(body, "reduce_d2d_out", ins, out_shapes, (n_in, n_in), aliases={w: w for w in range(n_in)})
    return dict(zip(BIG, outs[:NBIG])), outs[NBIG], outs[NBIG + 1], outs[NBIG + 2]


def _row_tile(rows):
    for tr in (128, 176, 64, 16, 8):
        if rows % tr == 0:
            return tr
    return rows


def _sum_own_half(g, rb, cidx, name):
    s, rows, cols = g.shape
    half = rows // 2
    tr = _row_tile(half)
    nb = half // tr

    def body(c_ref, g_ref, r_ref, o_ref):
        del c_ref
        o_ref[...] = (g_ref[...] + r_ref[...]).astype(BF16)

    grid_spec = pltpu.PrefetchScalarGridSpec(
        num_scalar_prefetch=1, grid=(s, nb),
        in_specs=[pl.BlockSpec((None, tr, cols), lambda k, i, c: (k, c[0] * nb + i, 0)),
                  pl.BlockSpec((None, tr, cols), lambda k, i, c: (k, i, 0))],
        out_specs=pl.BlockSpec((None, tr, cols), lambda k, i, c: (k, i, 0)))
    return pl.pallas_call(
        body, name=name, grid_spec=grid_spec, out_shape=jax.ShapeDtypeStruct((s, half, cols), BF16),
        compiler_params=_params(("parallel", "parallel")),
    )(cidx, g, rb)


def _sum_chips(q, p, jc, name, by_cols=False):
    s, rows, cols = q.shape
    tr = _row_tile(rows)
    nb = rows // tr
    stacked = p.ndim == 3

    def body(jc_ref, q_ref, p_ref, o_ref):
        j = jc_ref[0]
        own = p_ref[...].astype(F32)
        acc = None
        for k in range(NCHIP):
            term = jnp.where(j == k, own, q_ref[k].astype(F32))
            acc = term if acc is None else acc + term
        o_ref[...] = acc

    if by_cols:
        half_spec = pl.BlockSpec((tr, cols), lambda i, jc_ref: (i, jc_ref[1]))
        out_shape = jax.ShapeDtypeStruct((rows, 2 * cols), F32)
    else:
        half_spec = pl.BlockSpec((tr, cols), lambda i, jc_ref: (jc_ref[1] * nb + i, 0))
        out_shape = jax.ShapeDtypeStruct((2 * rows, cols), F32)
    p_spec = pl.BlockSpec((None, tr, cols), lambda i, jc_ref: (jc_ref[0], i, 0)) if stacked else half_spec
    grid_spec = pltpu.PrefetchScalarGridSpec(
        num_scalar_prefetch=1, grid=(nb,),
        in_specs=[pl.BlockSpec((s, tr, cols), lambda i, jc_ref: (0, i, 0)), p_spec],
        out_specs=half_spec)
    return pl.pallas_call(
        body, name=name, grid_spec=grid_spec, out_shape=out_shape,
        compiler_params=_params(("parallel",)),
    )(jc, q, p)


def _place_shard(w, jc, name):
    rows, cols = w.shape
    tr = _row_tile(rows)

    def body(jc_ref, w_ref, o_ref):
        del jc_ref
        o_ref[...] = w_ref[...].astype(BF16)

    grid_spec = pltpu.PrefetchScalarGridSpec(
        num_scalar_prefetch=1, grid=(rows // tr,),
        in_specs=[pl.BlockSpec((tr, cols), lambda i, jc_ref: (i, 0))],
        out_specs=pl.BlockSpec((None, tr, cols), lambda i, jc_ref: (jc_ref[0], i, 0)))
    return pl.pallas_call(
        body, name=name, grid_spec=grid_spec, out_shape=jax.ShapeDtypeStruct((NCHIP, rows, cols), BF16),
        compiler_params=_params(("parallel",)),
    )(jc, w)


def _add(a, b, name):
    def body(a_ref, b_ref, o_ref):
        o_ref[...] = a_ref[...] + b_ref[...]

    return pl.pallas_call(body, name=name, out_shape=jax.ShapeDtypeStruct(a.shape, F32))(a, b)


def _pack_small(sm):
    vec_in = [sm[n] for n, _, _ in VEC_ROWS]
    nv = len(vec_in)

    def body(*refs):
        ins, lossv, dcw, dcfb, dcfw, s1, s2 = refs[:nv], refs[nv], refs[nv + 1], refs[nv + 2], refs[nv + 3], \
            refs[nv + 4], refs[nv + 5]
        for ref, (_, r0, nr) in zip(ins, VEC_ROWS):
            s1[r0:r0 + nr, :] = ref[...]
        s1[ROW_LOSS:ROW_LOSS + 1, :] = lossv[...]
        s1[ROW_CONV_A:ROW_CONV_A + 4, :] = dcw[...]
        s2[0:1, :] = dcfb[...]
        s2[1:4, :] = dcfw[...]
        s2[4:8, :] = jnp.zeros((4, DUP), F32)

    return pl.pallas_call(
        body, name="pack_small",
        out_shape=(jax.ShapeDtypeStruct((S1_ROWS, D), F32), jax.ShapeDtypeStruct((S2_ROWS, DUP), F32)),
    )(*vec_in, sm["lossv"], sm["conv_a_w"], sm["conv_f_b"], sm["conv_f_w"])


def _adam_math(w, g, m, v):
    m = ADAM_B1 * m + (1.0 - ADAM_B1) * g
    v = ADAM_B2 * v + (1.0 - ADAM_B2) * (g * g)
    m_hat = m / (1.0 - ADAM_B1 ** ADAM_STEP)
    v_hat = v / (1.0 - ADAM_B2 ** ADAM_STEP)
    delta = -ADAM_LR * (m_hat / (jnp.sqrt(v_hat) + ADAM_EPS) + ADAM_WD * w)
    return delta, m, v


def _adam(w, g, m, v, name):
    rows, cols = w.shape
    tr = _row_tile(rows)

    def body(w_ref, g_ref, m_ref, v_ref, d_ref, mo_ref, vo_ref):
        d_ref[...], mo_ref[...], vo_ref[...] = _adam_math(w_ref[...], g_ref[...], m_ref[...], v_ref[...])

    spec = pl.BlockSpec((tr, cols), lambda i: (i, 0))
    return pl.pallas_call(
        body, name=name, out_shape=(jax.ShapeDtypeStruct(w.shape, F32),) * 3, grid=(rows // tr,),
        in_specs=[spec] * 4, out_specs=(spec,) * 3,
        compiler_params=_params(("parallel",)),
    )(w, g, m, v)


def _adam_small(gs1, gs2, gs3, w, m, v):
    names = [n for n, _, _ in VEC_ROWS] + ["conv_f_b", "lru_wa", "lru_wx"]
    nn = len(names)

    def grad_of(i, g1, g2, g3):
        if i < len(VEC_ROWS):
            _, r0, nr = VEC_ROWS[i]
            return g1[r0:r0 + nr, :]
        if names[i] == "conv_f_b":
            return g2[0:1, :]
        return g3[0] if names[i] == "lru_wa" else g3[1]

    def body(*refs):
        g1, g2, g3 = refs[0], refs[1], refs[2]
        ws, ms, vs = refs[3:3 + nn], refs[3 + nn:3 + 2 * nn], refs[3 + 2 * nn:3 + 3 * nn]
        outs = refs[3 + 3 * nn:]
        for i in range(nn):
            d, mn, vn = _adam_math(ws[i][...], grad_of(i, g1, g2, g3), ms[i][...], vs[i][...])
            outs[i][...] = d
            outs[nn + i][...] = mn
            outs[2 * nn + i][...] = vn

    shapes = [jax.ShapeDtypeStruct(w[n].shape, F32) for n in names]
    outs = pl.pallas_call(body, name="adam_small", out_shape=tuple(shapes * 3))(
        gs1, gs2, gs3, *[w[n] for n in names], *[m[n] for n in names], *[v[n] for n in names])
    return {n: (outs[i], outs[nn + i], outs[2 * nn + i]) for i, n in enumerate(names)}


WEIGHTS = ("norm_pre_mix", "w_in", "conv_a_w", "conv_a_b", "lru_wa", "lru_ba", "lru_wx", "lru_bx", "lru_lambda",
           "hg_lb_logits", "hg_norm_g", "w_branch_a", "w_branch_b", "w_out", "norm_post_mix", "norm_pre_ffn",
           "w_up", "conv_f_w", "conv_f_b", "w_down", "norm_post_ffn")
NW = len(WEIGHTS)


def kernel(x, norm_pre_mix, w_in, conv_a_w, conv_a_b, lru_wa, lru_ba, lru_wx, lru_bx, lru_lambda, hg_lb_logits, hg_norm_g, w_branch_a, w_branch_b, w_out, norm_post_mix, norm_pre_ffn, w_up, conv_f_w, conv_f_b, w_down, norm_post_ffn, loss_target, m_norm_pre_mix, m_w_in, m_conv_a_w, m_conv_a_b, m_lru_wa, m_lru_ba, m_lru_wx, m_lru_bx, m_lru_lambda, m_hg_lb_logits, m_hg_norm_g, m_w_branch_a, m_w_branch_b, m_w_out, m_norm_post_mix, m_norm_pre_ffn, m_w_up, m_conv_f_w, m_conv_f_b, m_w_down, m_norm_post_ffn, v_norm_pre_mix, v_w_in, v_conv_a_w, v_conv_a_b, v_lru_wa, v_lru_ba, v_lru_wx, v_lru_bx, v_lru_lambda, v_hg_lb_logits, v_hg_norm_g, v_w_branch_a, v_w_branch_b, v_w_out, v_norm_post_mix, v_norm_pre_ffn, v_w_up, v_conv_f_w, v_conv_f_b, v_w_down, v_norm_post_ffn):
    rest = (norm_pre_mix, w_in, conv_a_w, conv_a_b, lru_wa, lru_ba, lru_wx, lru_bx, lru_lambda, hg_lb_logits, hg_norm_g, w_branch_a, w_branch_b, w_out, norm_post_mix, norm_pre_ffn, w_up, conv_f_w, conv_f_b, w_down, norm_post_ffn, loss_target, m_norm_pre_mix, m_w_in, m_conv_a_w, m_conv_a_b, m_lru_wa, m_lru_ba, m_lru_wx, m_lru_bx, m_lru_lambda, m_hg_lb_logits, m_hg_norm_g, m_w_branch_a, m_w_branch_b, m_w_out, m_norm_post_mix, m_norm_pre_ffn, m_w_up, m_conv_f_w, m_conv_f_b, m_w_down, m_norm_post_ffn, v_norm_pre_mix, v_w_in, v_conv_a_w, v_conv_a_b, v_lru_wa, v_lru_ba, v_lru_wx, v_lru_bx, v_lru_lambda, v_hg_lb_logits, v_hg_norm_g, v_w_branch_a, v_w_branch_b, v_w_out, v_norm_post_mix, v_norm_pre_ffn, v_w_up, v_conv_f_w, v_conv_f_b, v_w_down, v_norm_post_ffn)
    w_in_args = dict(zip(WEIGHTS, rest[:NW]))
    loss_target = rest[NW]
    m_args = dict(zip(WEIGHTS, rest[NW + 1:2 * NW + 1]))
    v_args = dict(zip(WEIGHTS, rest[2 * NW + 1:3 * NW + 1]))
    shape_of = {n: w_in_args[n].shape for n in WEIGHTS}

    def two_d(n, a):
        if n in BIG:
            return a.reshape(BIG_SHAPE[n])
        if n in ("lru_wa", "lru_wx"):
            return a.reshape(NH, HD, HD)
        return a.reshape(a.shape[-2:])

    w2 = {n: two_d(n, w_in_args[n]) for n in WEIGHTS}
    m2 = {n: two_d(n, m_args[n]) for n in WEIGHTS}
    v2 = {n: two_d(n, v_args[n]) for n in WEIGHTS}

    cidx = lax.axis_index("c").astype(jnp.int32).reshape(1)
    jchip = 2 * lax.axis_index("x") + lax.axis_index("y")

    jc = jnp.stack([jchip, lax.axis_index("c")]).astype(jnp.int32)

    shards = {n: _place_shard(w2[n], jc, "place_" + n) for n in BIG}
    conv_a_s = jnp.pad(w2["conv_a_w"], ((0, 4), (0, 0)))
    conv_f_s = jnp.pad(w2["conv_f_w"], ((0, 5), (0, 0)))
    w_full, conv_a_g, conv_f_g = _gather_weights(shards, conv_a_s, conv_f_s)
    w_full["conv_a_w"] = jnp.transpose(conv_a_g, (1, 0, 2)).reshape(8, D)[0:4]
    w_full["conv_f_w"] = jnp.transpose(conv_f_g, (1, 0, 2)).reshape(8, DUP)[0:3]
    small = {n: w2[n] for n in WEIGHTS if n not in BIG and n not in ("conv_a_w", "conv_f_w")}

    grad_x, big_g, sm_g = _local_step(x[0], loss_target[0], w_full, small)

    s1, s2 = _pack_small(sm_g)
    s3 = jnp.concatenate([sm_g["lru_wa"].reshape(D, HD), sm_g["lru_wx"].reshape(D, HD)], axis=0)
    rb, rs1, rs2, rs3 = _reduce_stage1(big_g, s1, s2, s3)
    p_big = {n: _sum_own_half(big_g[n], rb[n], cidx, "sum_half_" + n) for n in BIG}
    ps1, ps2, ps3 = _add(s1, rs1, "add_s1"), _add(s2, rs2, "add_s2"), _add(s3, rs3, "add_s3")
    q_big, qs1, qs2, qs3 = _reduce_stage2(p_big, ps1, ps2, ps3)
    f_big = {n: _sum_chips(q_big[n], p_big[n], jc, "sum_chips_" + n) for n in BIG}
    fs1 = _sum_chips(qs1, ps1, jc, "sum_chips_s1")
    fs2 = _sum_chips(qs2, ps2, jc, "sum_chips_s2", by_cols=True)
    fs3 = _sum_chips(qs3, ps3, jc, "sum_chips_s3")
    g_big, gs1, gs2, gs3 = _reduce_stage3(f_big, fs1, fs2, fs3)

    res = {}
    for n in BIG:
        d, mn, vn = _adam(w2[n], g_big[n], m2[n], v2[n], "adam_" + n)
        res[n] = (g_big[n], d, mn, vn)
    small_res = _adam_small(gs1, gs2, gs3.reshape(2, NH, HD, HD), w2, m2, v2)
    for n, r0, nr in VEC_ROWS:
        res[n] = (gs1[r0:r0 + nr],) + small_res[n]
    res["conv_f_b"] = (gs2[0:1],) + small_res["conv_f_b"]
    res["lru_wa"] = (gs3[0:D].reshape(NH, HD, HD),) + small_res["lru_wa"]
    res["lru_wx"] = (gs3[D:2 * D].reshape(NH, HD, HD),) + small_res["lru_wx"]
    g_ca = lax.dynamic_slice_in_dim(gs1[ROW_CONV_A:ROW_CONV_A + 4], jchip * (D // NCHIP), D // NCHIP, axis=1)
    g_cf = lax.dynamic_slice_in_dim(gs2[1:4], jchip * SH_UP, SH_UP, axis=1)
    res["conv_a_w"] = (g_ca,) + _adam(w2["conv_a_w"], g_ca, m2["conv_a_w"], v2["conv_a_w"], "adam_conv_a_w")
    res["conv_f_w"] = (g_cf,) + _adam(w2["conv_f_w"], g_cf, m2["conv_f_w"], v2["conv_f_w"], "adam_conv_f_w")

    loss = (0.5 / D) * jnp.sum(gs1[ROW_LOSS])
    out = [loss, grad_x.reshape(x.shape)]
    for part in range(4):
        out += [res[n][part].reshape(shape_of[n]) for n in WEIGHTS]
    return tuple(out)
```

```python
import functools

import jax
import jax.numpy as jnp
from jax import lax
from jax.experimental import pallas as pl
from jax.experimental.pallas import tpu as pltpu

F32 = jnp.float32
BF16 = jnp.bfloat16

D = 1024
NH = 8
HD = 128
CH = 32
DFF = 2816
DUP = 2 * DFF
NCHIP = 4
SH_IN = 2 * D
SH_UP = DUP // NCHIP
SH_DN = DFF // NCHIP
SH_BR = D // NCHIP
EPS = 1e-6
LRU_C = 8.0
ADAM_LR = 0.001
ADAM_B1 = 0.9
ADAM_B2 = 0.999
ADAM_EPS = 1e-08
ADAM_WD = 0.01
ADAM_STEP = 10
VMEM_BIG = 56 * 1024 * 1024
MESH = pl.DeviceIdType.MESH

SLOT_A, SLOT_B, SLOT_C, SLOT_G = 2, 0, 1, 3


def _slot_of_chip(s):
    return jnp.where(s == 3, 3, (s + 2) % 3)


def _params(sem, vmem=None):
    return pltpu.CompilerParams(dimension_semantics=sem, vmem_limit_bytes=vmem)


_GC = 0.7978845608028654
_GA = 0.044715


def _gelu(x):
    return 0.5 * x * (1.0 + jnp.tanh(_GC * (x + _GA * x * x * x)))


def _gelu_and_grad(x):
    x2 = x * x
    th = jnp.tanh(_GC * x * (1.0 + _GA * x2))
    g = 0.5 * x * (1.0 + th)
    dg = 0.5 * (1.0 + th) + 0.5 * x * (1.0 - th * th) * _GC * (1.0 + 3.0 * _GA * x2)
    return g, dg


def _sig(x):
    return jax.nn.sigmoid(x)


def _dot(a, b):
    return jnp.dot(a, b, preferred_element_type=F32)


def _dot_nt(a, b):
    return lax.dot_general(a, b, (((1,), (1,)), ((), ())), preferred_element_type=F32)


def _dot_tn(a, b):
    return lax.dot_general(a, b, (((0,), (0,)), ((), ())), preferred_element_type=F32)


def _chunk_cumsum(x):
    pos = lax.broadcasted_iota(jnp.int32, (x.shape[0], 1), 0) & (CH - 1)
    d = 1
    while d < CH:
        x = x + jnp.where(pos >= d, pltpu.roll(x, d, 0), 0.0)
        d *= 2
    return x


def _chunk_revcumsum(x):
    n = x.shape[0]
    pos = lax.broadcasted_iota(jnp.int32, (n, 1), 0) & (CH - 1)
    d = 1
    while d < CH:
        x = x + jnp.where(pos < CH - d, pltpu.roll(x, n - d, 0), 0.0)
        d *= 2
    return x


def _chunk_last(x):
    n = x.shape[0]
    return jnp.concatenate(
        [jnp.broadcast_to(x[c * CH + CH - 1:c * CH + CH, :], (CH, x.shape[1])) for c in range(n // CH)], axis=0)


def _chunk_total(x):
    n = x.shape[0]
    return jnp.concatenate(
        [jnp.broadcast_to(jnp.sum(x[c * CH:(c + 1) * CH, :], axis=0, keepdims=True), (CH, x.shape[1]))
         for c in range(n // CH)], axis=0)


def _rms_stats(x):
    r = lax.rsqrt(jnp.mean(x * x, axis=-1, keepdims=True) + EPS)
    return r, x * r


def _rms_bwd(gd, n, r):
    return r * (gd - n * jnp.mean(gd * n, axis=-1, keepdims=True))


def _shift_rows(x, d, fill):
    rows = lax.broadcasted_iota(jnp.int32, (x.shape[0], 1), 0)
    return jnp.where(rows >= d, pltpu.roll(x, d, 0), fill)


def _scan_down(a, u, carry):
    n = a.shape[0]
    pos = lax.broadcasted_iota(jnp.int32, (n, 1), 0) & 7
    for d in (1, 2, 4):
        u = a * jnp.where(pos >= d, pltpu.roll(u, d, 0), 0.0) + u
        a = a * jnp.where(pos >= d, pltpu.roll(a, d, 0), 1.0)
    out = []
    for v in range(n // 8):
        h = a[v * 8:v * 8 + 8, :] * carry + u[v * 8:v * 8 + 8, :]
        carry = h[7:8, :]
        out.append(h)
    return jnp.concatenate(out, axis=0)


def _scan_up(b, g, carry):
    n = b.shape[0]
    pos = lax.broadcasted_iota(jnp.int32, (n, 1), 0) & 7
    for d in (1, 2, 4):
        g = g + b * jnp.where(pos < 8 - d, pltpu.roll(g, n - d, 0), 0.0)
        b = b * jnp.where(pos < 8 - d, pltpu.roll(b, n - d, 0), 1.0)
    out = [None] * (n // 8)
    for v in reversed(range(n // 8)):
        h = g[v * 8:v * 8 + 8, :] + b[v * 8:v * 8 + 8, :] * carry
        carry = h[0:1, :]
        out[v] = h
    return jnp.concatenate(out, axis=0)


def _shift_rows_up(x, d, fill):
    n = x.shape[0]
    rows = lax.broadcasted_iota(jnp.int32, (n, 1), 0)
    return jnp.where(rows < n - d, pltpu.roll(x, n - d, 0), fill)


def _mm_nn_sharded(a, b3, out_dtype, tm, name, slot_fn=None):
    m, k = a.shape
    s, _, ns = b3.shape

    def body(a_ref, b_ref, o_ref):
        o_ref[...] = _dot(a_ref[...], b_ref[...]).astype(out_dtype)

    if slot_fn is None:
        out_shape = jax.ShapeDtypeStruct((m, s * ns), out_dtype)
        out_spec = pl.BlockSpec((tm, ns), lambda j, i: (i, j))
    else:
        out_shape = jax.ShapeDtypeStruct((s, m, ns), out_dtype)
        out_spec = pl.BlockSpec((None, tm, ns), lambda j, i: (slot_fn(j), i, 0))
    return pl.pallas_call(
        body, name=name, out_shape=out_shape, grid=(s, m // tm),
        in_specs=[pl.BlockSpec((tm, k), lambda j, i: (i, 0)),
                  pl.BlockSpec((None, k, ns), lambda j, i: (j, 0, 0))],
        out_specs=out_spec,
        compiler_params=_params(("parallel", "parallel"), VMEM_BIG),
    )(a, b3)


def _mm_nt_sharded(a, b3, tm, name, stacked_slot_fn=None):
    s, k, ns = b3.shape
    m = a.shape[1] if stacked_slot_fn is not None else a.shape[0]

    def body(a_ref, b_ref, o_ref, acc_ref):
        j = pl.program_id(1)
        t = _dot_nt(a_ref[...], b_ref[...])

        @pl.when(j == 0)
        def _():
            acc_ref[...] = t

        @pl.when(j > 0)
        def _():
            acc_ref[...] += t

        @pl.when(j == s - 1)
        def _():
            o_ref[...] = acc_ref[...]

    if stacked_slot_fn is None:
        a_spec = pl.BlockSpec((tm, ns), lambda i, j: (i, j))
    else:
        a_spec = pl.BlockSpec((None, tm, ns), lambda i, j: (stacked_slot_fn(j), i, 0))
    return pl.pallas_call(
        body, name=name, out_shape=jax.ShapeDtypeStruct((m, k), F32), grid=(m // tm, s),
        in_specs=[a_spec, pl.BlockSpec((None, k, ns), lambda i, j: (j, 0, 0))],
        out_specs=pl.BlockSpec((tm, k), lambda i, j: (i, 0)),
        scratch_shapes=[pltpu.VMEM((tm, k), F32)],
        compiler_params=_params(("parallel", "arbitrary"), VMEM_BIG),
    )(a, b3)


def _mm_tn(a, g, tkk, tn, tk, name, stacked_slot_fn=None, stacked_out=False):
    m, k = a.shape
    if stacked_slot_fn is not None:
        n = g.shape[0] * g.shape[2]
        g_spec = pl.BlockSpec((None, tk, tn), lambda kk, j, mm: (stacked_slot_fn(j), mm, 0))
    else:
        n = g.shape[1]
        g_spec = pl.BlockSpec((tk, tn), lambda kk, j, mm: (mm, j))

    def body(a_ref, g_ref, o_ref):
        mm = pl.program_id(2)
        t = _dot_tn(a_ref[...], g_ref[...])

        @pl.when(mm == 0)
        def _():
            o_ref[...] = t

        @pl.when(mm > 0)
        def _():
            o_ref[...] += t

    if stacked_out:
        out_shape = jax.ShapeDtypeStruct((n // tn, k, tn), F32)
        out_spec = pl.BlockSpec((None, tkk, tn), lambda kk, j, mm: (j, kk, 0))
    else:
        out_shape = jax.ShapeDtypeStruct((k, n), F32)
        out_spec = pl.BlockSpec((tkk, tn), lambda kk, j, mm: (kk, j))
    return pl.pallas_call(
        body, name=name, out_shape=out_shape, grid=(k // tkk, n // tn, m // tk),
        in_specs=[pl.BlockSpec((tk, tkk), lambda kk, j, mm: (mm, kk)), g_spec],
        out_specs=out_spec,
        compiler_params=_params(("parallel", "parallel", "arbitrary"), VMEM_BIG),
    )(a, g)


def _norm_fwd(x, gain, tt):
    t = x.shape[0]

    def body(x_ref, g_ref, h_ref):
        _, n = _rms_stats(x_ref[...])
        h_ref[...] = (n * g_ref[...]).astype(BF16)

    return pl.pallas_call(
        body, name="norm_fwd", out_shape=jax.ShapeDtypeStruct((t, D), BF16), grid=(t // tt,),
        in_specs=[pl.BlockSpec((tt, D), lambda i: (i, 0)), pl.BlockSpec((1, D), lambda i: (0, 0))],
        out_specs=pl.BlockSpec((tt, D), lambda i: (i, 0)),
        compiler_params=_params(("parallel",)),
    )(x, gain)


def _lru_gates(xc, wa_ref, wx_ref, ba, bx, lam):
    xcb = xc.astype(BF16)
    ra = jnp.concatenate([_dot(xcb[:, n * HD:(n + 1) * HD], wa_ref[n]) for n in range(NH)], axis=1) + ba
    ix = jnp.concatenate([_dot(xcb[:, n * HD:(n + 1) * HD], wx_ref[n]) for n in range(NH)], axis=1) + bx
    r = _sig(ra)
    ig = _sig(ix)
    z = -lam
    sp = jnp.maximum(z, 0.0) + jnp.log1p(jnp.exp(-jnp.abs(z)))
    log_a = -LRU_C * r * sp
    a = jnp.exp(log_a)
    z2 = 2.0 * log_a
    series = -z2 * (1.0 + z2 * (0.5 + z2 * (1.0 / 6.0 + z2 * (1.0 / 24.0))))
    om = jnp.where(z2 > -0.02, series, 1.0 - jnp.exp(z2))
    mult = jnp.sqrt(om)
    return xcb, r, ig, sp, a, mult


def _mixer_a_fwd(p4, cw, cb, wa, wx, ba, bx, lam, tt):
    t = p4.shape[1]

    def body(p_ref, cw_ref, cb_ref, wa_ref, wx_ref, ba_ref, bx_ref, lam_ref, ya_ref, h_ref, halo, hc):
        i = pl.program_id(0)

        @pl.when(i == 0)
        def _():
            halo[...] = jnp.zeros((8, D), F32)
            hc[...] = jnp.zeros((1, D), F32)

        xa = p_ref[:, 0:D]
        ga = p_ref[:, D:2 * D]
        xe = jnp.concatenate([halo[...], xa], axis=0)
        xc = (cb_ref[...] + cw_ref[3:4, :] * xe
              + sum(cw_ref[3 - s:4 - s, :] * pltpu.roll(xe, s, 0) for s in (1, 2, 3)))[8:, :]
        halo[...] = xa[tt - 8:, :]
        _, _, ig, _, a, mult = _lru_gates(xc, wa_ref, wx_ref, ba_ref[...], bx_ref[...], lam_ref[...])
        u = mult * ig * xc
        h = _scan_down(a, u, hc[...])
        hc[...] = h[tt - 1:tt, :]
        h_ref[...] = h
        ya_ref[...] = (h * _gelu(ga)).astype(BF16)

    full = lambda shape: pl.BlockSpec(shape, lambda i: (0,) * len(shape))
    return pl.pallas_call(
        body, name="mixer_a_fwd",
        out_shape=(jax.ShapeDtypeStruct((t, D), BF16), jax.ShapeDtypeStruct((t, D), F32)),
        grid=(t // tt,),
        in_specs=[pl.BlockSpec((None, tt, 2 * D), lambda i: (SLOT_A, i, 0)),
                  full((4, D)), full((1, D)), full((NH, HD, HD)), full((NH, HD, HD)),
                  full((1, D)), full((1, D)), full((1, D))],
        out_specs=(pl.BlockSpec((tt, D), lambda i: (i, 0)), pl.BlockSpec((tt, D), lambda i: (i, 0))),
        scratch_shapes=[pltpu.VMEM((8, D), F32), pltpu.VMEM((1, D), F32)],
        compiler_params=_params(("arbitrary",), VMEM_BIG),
    )(p4, cw, cb, wa, wx, ba, bx, lam)


def _chunk_masks(tt):
    row = lax.broadcasted_iota(jnp.int32, (tt, tt), 0)
    col = lax.broadcasted_iota(jnp.int32, (tt, tt), 1)
    same = jnp.right_shift(row, 5) == jnp.right_shift(col, 5)
    return same & (col <= row)


def _hg_head_fwd(q, fz, lbh):
    sg = _sig(fz)
    sgn = _sig(-fz)
    f = lbh + (1.0 - lbh) * sg
    logf = jnp.log(f)
    k = (1.0 - lbh) * sgn
    g = _chunk_cumsum(logf)
    gu = _chunk_last(g) - g
    eg = jnp.exp(g)
    eng = jnp.exp(-g)
    egu = jnp.exp(gu)
    qt = q * eg
    kt = k * eng
    kd = k * egu
    return sg, sgn, f, k, g, eg, eng, egu, qt, kt, kd


def _lb_of(logits_ref):
    return _sig(logits_ref[0:1, :] - logits_ref[1:2, :])


def _hgrn2_fwd(p4, logits, gnorm, tt):
    t = p4.shape[1]
    nc = tt // CH

    def body(p_ref, lg_ref, gn_ref, yb_ref, o_ref, ss_ref, st):
        i = pl.program_id(0)

        @pl.when(i == 0)
        def _():
            st[...] = jnp.zeros((NH, HD, HD), F32)

        low = _chunk_masks(tt)
        lb = _lb_of(lg_ref)
        heads = [slice(h * HD, (h + 1) * HD) for h in range(NH)]
        _, _, _, _, g, _, _, _, qt, kt, kd = _hg_head_fwd(p_ref[0, :, 0:D], p_ref[0, :, D:2 * D], lb)
        qtb, ktb, kdb, vb = qt.astype(BF16), kt.astype(BF16), kd.astype(BF16), p_ref[1, :, 0:D].astype(BF16)
        decs = [jnp.exp(g[c * CH + CH - 1:c * CH + CH, :]) for c in range(nc)]
        o_in = []
        for hs in heads:
            att = jnp.where(low, _dot_nt(qtb[:, hs], ktb[:, hs]), 0.0)
            o_in.append(_dot(att.astype(BF16), vb[:, hs]))
        s_t = [st[h] for h in range(NH)]
        pieces = [[None] * nc for _ in range(NH)]
        for c in range(nc):
            sl = slice(c * CH, (c + 1) * CH)
            for h, hs in enumerate(heads):
                s_bf = s_t[h].astype(BF16)
                ss_ref[c, h] = s_bf
                pieces[h][c] = o_in[h][sl] + _dot_nt(qtb[sl, hs], s_bf)
                s_t[h] = s_t[h] * decs[c][:, hs] + _dot_tn(vb[sl, hs], kdb[sl, hs])
        for h, hs in enumerate(heads):
            st[h] = s_t[h]
            o = jnp.concatenate(pieces[h], axis=0)
            _, n = _rms_stats(o)
            og = p_ref[1, :, D + h * HD:D + (h + 1) * HD]
            o_ref[:, hs] = o
            yb_ref[:, hs] = (n * gn_ref[:, hs] * (og * _sig(og))).astype(BF16)

    return pl.pallas_call(
        body, name="hgrn2_fwd",
        out_shape=(jax.ShapeDtypeStruct((t, D), BF16), jax.ShapeDtypeStruct((t, D), F32),
                   jax.ShapeDtypeStruct((t // CH, NH, HD, HD), BF16)),
        grid=(t // tt,),
        in_specs=[pl.BlockSpec((2, tt, 2 * D), lambda i: (0, i, 0)),
                  pl.BlockSpec((2, D), lambda i: (0, 0)), pl.BlockSpec((1, D), lambda i: (0, 0))],
        out_specs=(pl.BlockSpec((tt, D), lambda i: (i, 0)), pl.BlockSpec((tt, D), lambda i: (i, 0)),
                   pl.BlockSpec((nc, NH, HD, HD), lambda i: (i, 0, 0, 0))),
        scratch_shapes=[pltpu.VMEM((NH, HD, HD), F32)],
        compiler_params=_params(("arbitrary",), VMEM_BIG),
    )(p4, logits, gnorm)


def _mid_fwd(ya, yb, p4, x, wa, wb, wo, g_pm, g_pf, tt):
    t = x.shape[0]

    def body(ya_ref, yb_ref, gt_ref, x_ref, wa_ref, wb_ref, wo_ref, gpm_ref, gpf_ref,
             za_ref, zb_ref, mix_ref, m2_ref, x1_ref, h2_ref):
        za = _dot(ya_ref[...], wa_ref[...])
        zb = _dot(yb_ref[...], wb_ref[...])
        mix = _sig(gt_ref[:, 0:D]) * za + _sig(gt_ref[:, D:2 * D]) * zb
        mixb = mix.astype(BF16)
        m2 = _dot(mixb, wo_ref[...])
        _, n2 = _rms_stats(m2)
        x1 = x_ref[...] + n2 * gpm_ref[...]
        _, n1 = _rms_stats(x1)
        za_ref[...] = za.astype(BF16)
        zb_ref[...] = zb.astype(BF16)
        mix_ref[...] = mixb
        m2_ref[...] = m2
        x1_ref[...] = x1
        h2_ref[...] = (n1 * gpf_ref[...]).astype(BF16)

    row = lambda dt: jax.ShapeDtypeStruct((t, D), dt)
    tile = pl.BlockSpec((tt, D), lambda i: (i, 0))
    wsp = pl.BlockSpec((D, D), lambda i: (0, 0))
    vec = pl.BlockSpec((1, D), lambda i: (0, 0))
    return pl.pallas_call(
        body, name="mid_fwd",
        out_shape=(row(BF16), row(BF16), row(BF16), row(F32), row(F32), row(BF16)),
        grid=(t // tt,),
        in_specs=[tile, tile, pl.BlockSpec((None, tt, 2 * D), lambda i: (SLOT_G, i, 0)), tile,
                  wsp, wsp, wsp, vec, vec],
        out_specs=(tile,) * 6,
        compiler_params=_params(("parallel",), VMEM_BIG),
    )(ya, yb, p4, x, wa, wb, wo, g_pm, g_pf)


def _ffn_act_fwd(up_pre, cfw, cfb, tt):
    t = up_pre.shape[0]

    def body(u_ref, halo_ref, w_ref, b_ref, y_ref):
        i = pl.program_id(0)
        halves = []
        for c0 in (0, DFF):
            cs = slice(c0, c0 + DFF)
            xe = jnp.concatenate([jnp.where(i > 0, halo_ref[:, cs], 0.0), u_ref[:, cs]], axis=0)
            up = (b_ref[:, cs] + w_ref[2:3, cs] * xe + w_ref[1:2, cs] * pltpu.roll(xe, 1, 0)
                  + w_ref[0:1, cs] * pltpu.roll(xe, 2, 0))
            halves.append(up[8:, :])
        y_ref[...] = (_gelu(halves[0]) * halves[1]).astype(BF16)

    hb = tt // 8
    return pl.pallas_call(
        body, name="ffn_act_fwd", out_shape=jax.ShapeDtypeStruct((t, DFF), BF16), grid=(t // tt,),
        in_specs=[pl.BlockSpec((tt, DUP), lambda i: (i, 0)),
                  pl.BlockSpec((8, DUP), lambda i: (jnp.maximum(i * hb - 1, 0), 0)),
                  pl.BlockSpec((3, DUP), lambda i: (0, 0)), pl.BlockSpec((1, DUP), lambda i: (0, 0))],
        out_specs=pl.BlockSpec((tt, DFF), lambda i: (i, 0)),
        compiler_params=_params(("parallel",), VMEM_BIG),
    )(up_pre, up_pre, cfw, cfb)


def _down_loss(y, wdn, x1, tgt, g_post, tt):
    t = x1.shape[0]

    def body(y_ref, w_ref, x1_ref, t_ref, g_ref, dx2_ref, dm3_ref, lossv_ref, dg_ref):
        i = pl.program_id(0)
        m3 = _dot(y_ref[...], w_ref[...])
        r, n3 = _rms_stats(m3)
        g = g_ref[...]
        e = x1_ref[...] + n3 * g - t_ref[...]
        dx2 = e * (1.0 / D)
        dx2_ref[...] = dx2
        dm3_ref[...] = _rms_bwd(dx2 * g, n3, r).astype(BF16)
        lv = jnp.sum(e * e, axis=0, keepdims=True)
        dgv = jnp.sum(dx2 * n3, axis=0, keepdims=True)

        @pl.when(i == 0)
        def _():
            lossv_ref[...] = lv
            dg_ref[...] = dgv

        @pl.when(i > 0)
        def _():
            lossv_ref[...] += lv
            dg_ref[...] += dgv

    tile = pl.BlockSpec((tt, D), lambda i: (i, 0))
    vec = pl.BlockSpec((1, D), lambda i: (0, 0))
    return pl.pallas_call(
        body, name="down_loss",
        out_shape=(jax.ShapeDtypeStruct((t, D), F32), jax.ShapeDtypeStruct((t, D), BF16),
                   jax.ShapeDtypeStruct((1, D), F32), jax.ShapeDtypeStruct((1, D), F32)),
        grid=(t // tt,),
        in_specs=[pl.BlockSpec((tt, DFF), lambda i: (i, 0)), pl.BlockSpec((DFF, D), lambda i: (0, 0)),
                  tile, tile, vec],
        out_specs=(tile, tile, vec, vec),
        compiler_params=_params(("arbitrary",), VMEM_BIG),
    )(y, wdn, x1, tgt, g_post)


def _ffn_act_bwd(dm3, wdn, up_pre, cfw, cfb, tt):
    t = up_pre.shape[0]
    nt = t // tt

    def body(dm_ref, dmn_ref, w_ref, u_ref, up_ref, un_ref, cw_ref, cb_ref, du_ref, dcw_ref, dcb_ref):
        i = pl.program_id(0)
        n = tt + 8
        next_live = jnp.where(i < nt - 1, 1.0, 0.0)
        dy = jnp.concatenate([_dot_nt(dm_ref[...], w_ref[...]),
                              _dot_nt(dmn_ref[...], w_ref[...])[0:8, :] * next_live], axis=0)
        ups, xs = [], []
        for c0 in (0, DFF):
            cs = slice(c0, c0 + DFF)
            xe = jnp.concatenate([jnp.where(i > 0, up_ref[:, cs], 0.0), u_ref[:, cs], un_ref[:, cs]], axis=0)
            x1 = pltpu.roll(xe, 1, 0)
            x2 = pltpu.roll(xe, 2, 0)
            up = cb_ref[:, cs] + cw_ref[2:3, cs] * xe + cw_ref[1:2, cs] * x1 + cw_ref[0:1, cs] * x2
            ups.append(up[8:, :])
            xs.append((x2[8:tt + 8, :], x1[8:tt + 8, :], xe[8:tt + 8, :]))
        gl, dgl = _gelu_and_grad(ups[0])
        ds = (dy * ups[1] * dgl, dy * gl)
        dcw_parts, dcb_parts = [], []
        for hh, c0 in enumerate((0, DFF)):
            cs = slice(c0, c0 + DFF)
            dd = ds[hh]
            du = cw_ref[2:3, cs] * dd + cw_ref[1:2, cs] * pltpu.roll(dd, n - 1, 0) \
                + cw_ref[0:1, cs] * pltpu.roll(dd, n - 2, 0)
            du_ref[:, cs] = du[0:tt, :].astype(BF16)
            dup = dd[0:tt, :]
            dcw_parts.append(jnp.concatenate(
                [jnp.sum(dup * xs[hh][k], axis=0, keepdims=True) for k in range(3)], axis=0))
            dcb_parts.append(jnp.sum(dup, axis=0, keepdims=True))
        dcw = jnp.concatenate(dcw_parts, axis=1)
        dcb = jnp.concatenate(dcb_parts, axis=1)

        @pl.when(i == 0)
        def _():
            dcw_ref[...] = dcw
            dcb_ref[...] = dcb

        @pl.when(i > 0)
        def _():
            dcw_ref[...] += dcw
            dcb_ref[...] += dcb

    hb = tt // 8
    last8 = t // 8 - 1
    return pl.pallas_call(
        body, name="ffn_act_bwd",
        out_shape=(jax.ShapeDtypeStruct((t, DUP), BF16), jax.ShapeDtypeStruct((3, DUP), F32),
                   jax.ShapeDtypeStruct((1, DUP), F32)),
        grid=(nt,),
        in_specs=[pl.BlockSpec((tt, D), lambda i: (i, 0)),
                  pl.BlockSpec((16, D), lambda i: (jnp.minimum((i + 1) * (tt // 16), t // 16 - 1), 0)),
                  pl.BlockSpec((DFF, D), lambda i: (0, 0)),
                  pl.BlockSpec((tt, DUP), lambda i: (i, 0)),
                  pl.BlockSpec((8, DUP), lambda i: (jnp.maximum(i * hb - 1, 0), 0)),
                  pl.BlockSpec((8, DUP), lambda i: (jnp.minimum((i + 1) * hb, last8), 0)),
                  pl.BlockSpec((3, DUP), lambda i: (0, 0)), pl.BlockSpec((1, DUP), lambda i: (0, 0))],
        out_specs=(pl.BlockSpec((tt, DUP), lambda i: (i, 0)), pl.BlockSpec((3, DUP), lambda i: (0, 0)),
                   pl.BlockSpec((1, DUP), lambda i: (0, 0))),
        compiler_params=_params(("arbitrary",), VMEM_BIG),
    )(dm3, dm3, wdn, up_pre, up_pre, up_pre, cfw, cfb)


def _mid_bwd(dh2, dx2, x1, m2, za, zb, p4, wa, wb, wo, g_pm, g_pf, tt):
    t = x1.shape[0]

    def body(dh2_ref, dx2_ref, x1_ref, m2_ref, za_ref, zb_ref, gt_ref, wa_ref, wb_ref, wo_ref, gpm_ref, gpf_ref,
             dx1_ref, dm2_ref, dza_ref, dzb_ref, dya_ref, dyb_ref, dp_ref, dgpm_ref, dgpf_ref):
        i = pl.program_id(0)
        r1, n1 = _rms_stats(x1_ref[...])
        dh2 = dh2_ref[...]
        dx1 = dx2_ref[...] + _rms_bwd(dh2 * gpf_ref[...], n1, r1)
        r2, n2 = _rms_stats(m2_ref[...])
        dm2 = _rms_bwd(dx1 * gpm_ref[...], n2, r2).astype(BF16)
        dmix = _dot_nt(dm2, wo_ref[...])
        sa = _sig(gt_ref[:, 0:D])
        sb = _sig(gt_ref[:, D:2 * D])
        dza = (dmix * sa).astype(BF16)
        dzb = (dmix * sb).astype(BF16)
        dp_ref[:, 0:D] = (dmix * za_ref[...].astype(F32) * sa * (1.0 - sa)).astype(BF16)
        dp_ref[:, D:2 * D] = (dmix * zb_ref[...].astype(F32) * sb * (1.0 - sb)).astype(BF16)
        dx1_ref[...] = dx1
        dm2_ref[...] = dm2
        dza_ref[...] = dza
        dzb_ref[...] = dzb
        dya_ref[...] = _dot_nt(dza, wa_ref[...])
        dyb_ref[...] = _dot_nt(dzb, wb_ref[...])
        dgpf = jnp.sum(dh2 * n1, axis=0, keepdims=True)
        dgpm = jnp.sum(dx1 * n2, axis=0, keepdims=True)

        @pl.when(i == 0)
        def _():
            dgpf_ref[...] = dgpf
            dgpm_ref[...] = dgpm

        @pl.when(i > 0)
        def _():
            dgpf_ref[...] += dgpf
            dgpm_ref[...] += dgpm

    row = lambda dt: jax.ShapeDtypeStruct((t, D), dt)
    tile = pl.BlockSpec((tt, D), lambda i: (i, 0))
    wsp = pl.BlockSpec((D, D), lambda i: (0, 0))
    vec = pl.BlockSpec((1, D), lambda i: (0, 0))
    gates = pl.BlockSpec((None, tt, 2 * D), lambda i: (SLOT_G, i, 0))
    return pl.pallas_call(
        body, name="mid_bwd",
        out_shape=(row(F32), row(BF16), row(BF16), row(BF16), row(F32), row(F32),
                   jax.ShapeDtypeStruct((NCHIP, t, 2 * D), BF16),
                   jax.ShapeDtypeStruct((1, D), F32), jax.ShapeDtypeStruct((1, D), F32)),
        grid=(t // tt,),
        in_specs=[tile, tile, tile, tile, tile, tile, gates, wsp, wsp, wsp, vec, vec],
        out_specs=(tile, tile, tile, tile, tile, tile, gates, vec, vec),
        compiler_params=_params(("arbitrary",), VMEM_BIG),
    )(dh2, dx2, x1, m2, za, zb, p4, wa, wb, wo, g_pm, g_pf)


def _hgrn2_bwd(p4, o_all, ss, dyb, dp4, logits, gnorm, tt):
    t = p4.shape[1]
    nt = t // tt
    nc = tt // CH

    def body(p_ref, o_ref, ss_ref, dyb_ref, dp_in, lg_ref, gn_ref, dp_ref, dlb_ref, dgn_ref, dst):
        del dp_in
        i = pl.program_id(0)

        @pl.when(i == 0)
        def _():
            dst[...] = jnp.zeros((NH, HD, HD), F32)

        low = _chunk_masks(tt)
        lb = _lb_of(lg_ref)
        heads = [slice(h * HD, (h + 1) * HD) for h in range(NH)]
        sg, sgn, f, k, g, eg, eng, egu, qt, kt, kd = _hg_head_fwd(p_ref[0, :, 0:D], p_ref[0, :, D:2 * D], lb)
        qtb, ktb, kdb, vb = qt.astype(BF16), kt.astype(BF16), kd.astype(BF16), p_ref[1, :, 0:D].astype(BF16)
        decs = [jnp.exp(g[c * CH + CH - 1:c * CH + CH, :]) for c in range(nc)]
        og = p_ref[1, :, D:2 * D]
        so = _sig(og)
        dyb = dyb_ref[...]
        dob = dyb * (og * so)
        rn = [_rms_stats(o_ref[:, hs]) for hs in heads]
        r_all = jnp.concatenate([jnp.broadcast_to(r, (tt, HD)) for r, _ in rn], axis=1)
        n_all = jnp.concatenate([n for _, n in rn], axis=1)
        gd = dob * gn_ref[...]
        proj = jnp.concatenate(
            [jnp.broadcast_to(jnp.mean(gd[:, hs] * n_all[:, hs], axis=-1, keepdims=True), (tt, HD)) for hs in heads],
            axis=1)
        dob_ = (r_all * (gd - n_all * proj)).astype(BF16)
        dog = dyb * (n_all * gn_ref[...]) * (so * (1.0 + og * (1.0 - so)))
        dgn = jnp.sum(dob * n_all, axis=0, keepdims=True)
        dv_in, dqt_in, dkt_h = [], [], []
        for hs in heads:
            att = jnp.where(low, _dot_nt(qtb[:, hs], ktb[:, hs]), 0.0).astype(BF16)
            d_att = jnp.where(low, _dot_nt(dob_[:, hs], vb[:, hs]), 0.0).astype(BF16)
            dv_in.append(_dot_tn(att, dob_[:, hs]))
            dqt_in.append(_dot(d_att, ktb[:, hs]))
            dkt_h.append(_dot_tn(d_att, qtb[:, hs]))
        ds_t = [dst[h] for h in range(NH)]
        dv_p = [[None] * NH for _ in range(nc)]
        dqt_p = [[None] * NH for _ in range(nc)]
        dkd_p = [[None] * NH for _ in range(nc)]
        dgl_p = [[None] * NH for _ in range(nc)]
        for c in reversed(range(nc)):
            sl = slice(c * CH, (c + 1) * CH)
            for h, hs in enumerate(heads):
                s_prev = ss_ref[c, h]
                ds_bf = ds_t[h].astype(BF16)
                dec = decs[c][:, hs]
                dv_p[c][h] = dv_in[h][sl] + _dot_nt(kdb[sl, hs], ds_bf)
                dqt_p[c][h] = dqt_in[h][sl] + _dot(dob_[sl, hs], s_prev)
                dkd_p[c][h] = _dot(vb[sl, hs], ds_bf)
                ddec = jnp.sum(s_prev.astype(F32) * ds_t[h], axis=0, keepdims=True)
                dgl_p[c][h] = jnp.broadcast_to(ddec * dec, (CH, HD))
                ds_t[h] = ds_t[h] * dec + _dot_tn(dob_[sl, hs], qtb[sl, hs])
        for h in range(NH):
            dst[h] = ds_t[h]
        whole = lambda parts: jnp.concatenate([jnp.concatenate(row, axis=1) for row in parts], axis=0)
        dv, dqt, dkd, dgl = whole(dv_p), whole(dqt_p), whole(dkd_p), whole(dgl_p)
        dkt = jnp.concatenate(dkt_h, axis=1)
        dq = dqt * eg
        dk = dkt * eng + dkd * egu
        dg = dqt * qt - dkt * kt
        dgu = dkd * kd
        dlogf = _chunk_revcumsum(dg - dgu) + _chunk_total(dgu) + dgl
        common = sgn * (dlogf / f - dk)
        dfz = (1.0 - lb) * sg * common
        dlb = jnp.sum(common, axis=0, keepdims=True)
        dp_ref[0, :, 0:D] = dq.astype(BF16)
        dp_ref[0, :, D:2 * D] = dfz.astype(BF16)
        dp_ref[1, :, 0:D] = dv.astype(BF16)
        dp_ref[1, :, D:2 * D] = dog.astype(BF16)

        @pl.when(i == 0)
        def _():
            dlb_ref[0:1, :] = dlb
            dgn_ref[...] = dgn

        @pl.when(i > 0)
        def _():
            dlb_ref[0:1, :] += dlb
            dgn_ref[...] += dgn

        @pl.when(i == nt - 1)
        def _():
            d0 = dlb_ref[0:1, :] * lb * (1.0 - lb)
            dlb_ref[0:1, :] = d0
            dlb_ref[1:2, :] = -d0

    rev = lambda i: nt - 1 - i
    vec = pl.BlockSpec((1, D), lambda i: (0, 0))
    return pl.pallas_call(
        body, name="hgrn2_bwd",
        out_shape=(jax.ShapeDtypeStruct(dp4.shape, BF16), jax.ShapeDtypeStruct((2, D), F32),
                   jax.ShapeDtypeStruct((1, D), F32)),
        grid=(nt,),
        in_specs=[pl.BlockSpec((2, tt, 2 * D), lambda i: (0, rev(i), 0)),
                  pl.BlockSpec((tt, D), lambda i: (rev(i), 0)),
                  pl.BlockSpec((nc, NH, HD, HD), lambda i: (rev(i), 0, 0, 0)),
                  pl.BlockSpec((tt, D), lambda i: (rev(i), 0)),
                  pl.BlockSpec(memory_space=pl.ANY),
                  pl.BlockSpec((2, D), lambda i: (0, 0)), vec],
        out_specs=(pl.BlockSpec((2, tt, 2 * D), lambda i: (0, rev(i), 0)),
                   pl.BlockSpec((2, D), lambda i: (0, 0)), vec),
        scratch_shapes=[pltpu.VMEM((NH, HD, HD), F32)],
        input_output_aliases={4: 0},
        compiler_params=_params(("arbitrary",), VMEM_BIG),
    )(p4, o_all, ss, dyb, dp4, logits, gnorm)


def _mixer_a_bwd(p4, hseq, dya, dp4, cw, cb, wa, wx, ba, bx, lam, tt):
    t = p4.shape[1]
    nt = t // tt
    steps = tt.bit_length() - 1

    def body(p_ref, ph_ref, h_ref, hh_ref, dya_ref, dp_in, cw_ref, cb_ref, wa_ref, wx_ref, ba_ref, bx_ref, lam_ref,
             dp_ref, dcw_ref, dcb_ref, dwa_ref, dwx_ref, dba_ref, dbx_ref, dlam_ref,
             dnext, dhc, afc):
        del dp_in
        i = pl.program_id(0)
        first_tile = i == nt - 1

        @pl.when(i == 0)
        def _():
            dnext[...] = jnp.zeros((8, D), F32)
            dhc[...] = jnp.zeros((1, D), F32)
            afc[...] = jnp.zeros((1, D), F32)

        xa = p_ref[:, 0:D]
        ga = p_ref[:, D:2 * D]
        xe = jnp.concatenate([jnp.where(first_tile, 0.0, ph_ref[:, 0:D]), xa], axis=0)
        xs = [xe[8:, :]] + [pltpu.roll(xe, s, 0)[8:, :] for s in (1, 2, 3)]
        xc = cb_ref[...] + sum(cw_ref[3 - s:4 - s, :] * xs[s] for s in range(4))
        lam = lam_ref[...]
        xcb, r, ig, sp, a, mult = _lru_gates(xc, wa_ref, wx_ref, ba_ref[...], bx_ref[...], lam)
        h = h_ref[...]
        gl, dgl = _gelu_and_grad(ga)
        dya = dya_ref[...]
        dga = dya * h * dgl
        rows = lax.broadcasted_iota(jnp.int32, (tt, 1), 0)
        a_next = jnp.where(rows == tt - 1, afc[...], pltpu.roll(a, tt - 1, 0))
        dh = _scan_up(a_next, dya * gl, dhc[...])
        dhc[...] = dh[0:1, :]
        afc[...] = a[0:1, :]
        h_prev = jnp.where(rows == 0, jnp.where(first_tile, 0.0, hh_ref[7:8, :]), pltpu.roll(h, 1, 0))
        da = dh * h_prev
        dmult = dh * ig * xc
        di = dh * mult * xc
        dlog_a = da * a - dmult * a * a / mult
        dr = dlog_a * (-LRU_C * sp)
        dsp = jnp.sum(dlog_a * (-LRU_C * r), axis=0, keepdims=True)
        dra = dr * r * (1.0 - r)
        dix = di * ig * (1.0 - ig)
        drab = dra.astype(BF16)
        dixb = dix.astype(BF16)
        dxc_lin = []
        dwa_new = []
        dwx_new = []
        for n in range(NH):
            cs = slice(n * HD, (n + 1) * HD)
            dxc_lin.append(_dot_nt(drab[:, cs], wa_ref[n]) + _dot_nt(dixb[:, cs], wx_ref[n]))
            dwa_new.append(_dot_tn(xcb[:, cs], drab[:, cs]))
            dwx_new.append(_dot_tn(xcb[:, cs], dixb[:, cs]))
        dxc = dh * mult * ig + jnp.concatenate(dxc_lin, axis=1)
        de = jnp.concatenate([dxc, dnext[...]], axis=0)
        dxa = (cw_ref[3:4, :] * de
               + sum(cw_ref[3 - s:4 - s, :] * pltpu.roll(de, tt + 8 - s, 0) for s in (1, 2, 3)))[0:tt, :]
        dnext[...] = dxc[0:8, :]
        dp_ref[:, 0:D] = dxa.astype(BF16)
        dp_ref[:, D:2 * D] = dga.astype(BF16)
        dcw = jnp.concatenate(
            [jnp.sum(dxc * xs[3 - k], axis=0, keepdims=True) for k in range(4)], axis=0)
        dcb = jnp.sum(dxc, axis=0, keepdims=True)
        dba = jnp.sum(dra, axis=0, keepdims=True)
        dbx = jnp.sum(dix, axis=0, keepdims=True)
        dlam = dsp * (-_sig(-lam))

        @pl.when(i == 0)
        def _():
            dcw_ref[...] = dcw
            dcb_ref[...] = dcb
            dba_ref[...] = dba
            dbx_ref[...] = dbx
            dlam_ref[...] = dlam
            for n in range(NH):
                dwa_ref[n] = dwa_new[n]
                dwx_ref[n] = dwx_new[n]

        @pl.when(i > 0)
        def _():
            dcw_ref[...] += dcw
            dcb_ref[...] += dcb
            dba_ref[...] += dba
            dbx_ref[...] += dbx
            dlam_ref[...] += dlam
            for n in range(NH):
                dwa_ref[n] += dwa_new[n]
                dwx_ref[n] += dwx_new[n]

    rev = lambda i: nt - 1 - i
    hb = tt // 8
    full = lambda shape: pl.BlockSpec(shape, lambda i: (0,) * len(shape))
    vecs = jax.ShapeDtypeStruct((1, D), F32)
    blk = jax.ShapeDtypeStruct((NH, HD, HD), F32)
    return pl.pallas_call(
        body, name="mixer_a_bwd",
        out_shape=(jax.ShapeDtypeStruct(dp4.shape, BF16), jax.ShapeDtypeStruct((4, D), F32), vecs, blk, blk,
                   vecs, vecs, vecs),
        grid=(nt,),
        in_specs=[pl.BlockSpec((None, tt, 2 * D), lambda i: (SLOT_A, rev(i), 0)),
                  pl.BlockSpec((None, 8, 2 * D), lambda i: (SLOT_A, jnp.maximum(rev(i) * hb - 1, 0), 0)),
                  pl.BlockSpec((tt, D), lambda i: (rev(i), 0)),
                  pl.BlockSpec((8, D), lambda i: (jnp.maximum(rev(i) * hb - 1, 0), 0)),
                  pl.BlockSpec((tt, D), lambda i: (rev(i), 0)),
                  pl.BlockSpec(memory_space=pl.ANY),
                  full((4, D)), full((1, D)), full((NH, HD, HD)), full((NH, HD, HD)),
                  full((1, D)), full((1, D)), full((1, D))],
        out_specs=(pl.BlockSpec((None, tt, 2 * D), lambda i: (SLOT_A, rev(i), 0)),
                   full((4, D)), full((1, D)), full((NH, HD, HD)), full((NH, HD, HD)),
                   full((1, D)), full((1, D)), full((1, D))),
        scratch_shapes=[pltpu.VMEM((8, D), F32), pltpu.VMEM((1, D), F32), pltpu.VMEM((1, D), F32)],
        input_output_aliases={5: 0},
        compiler_params=_params(("arbitrary",), VMEM_BIG),
    )(p4, p4, hseq, hseq, dya, dp4, cw, cb, wa, wx, ba, bx, lam)


def _norm_bwd(dh1, dx1, x, gain, tt):
    t = x.shape[0]

    def body(dh_ref, dx1_ref, x_ref, g_ref, dx_ref, dg_ref):
        i = pl.program_id(0)
        r, n = _rms_stats(x_ref[...])
        dh = dh_ref[...]
        dx_ref[...] = dx1_ref[...] + _rms_bwd(dh * g_ref[...], n, r)
        dgv = jnp.sum(dh * n, axis=0, keepdims=True)

        @pl.when(i == 0)
        def _():
            dg_ref[...] = dgv

        @pl.when(i > 0)
        def _():
            dg_ref[...] += dgv

    tile = pl.BlockSpec((tt, D), lambda i: (i, 0))
    vec = pl.BlockSpec((1, D), lambda i: (0, 0))
    return pl.pallas_call(
        body, name="norm_bwd",
        out_shape=(jax.ShapeDtypeStruct((t, D), F32), jax.ShapeDtypeStruct((1, D), F32)),
        grid=(t // tt,), in_specs=[tile, tile, tile, vec], out_specs=(tile, vec),
        compiler_params=_params(("arbitrary",)),
    )(dh1, dx1, x, gain)


def _local_step(x, tgt, w, small):
    t = x.shape[0]
    tt = min(256, t)
    tm = min(1024, t)
    wa_bf = small["lru_wa"].astype(BF16)
    wx_bf = small["lru_wx"].astype(BF16)
    w_br_a = w["w_branch_a"].reshape(D, D)
    w_br_b = w["w_branch_b"].reshape(D, D)
    w_out = w["w_out"].reshape(D, D)
    w_down = w["w_down"].reshape(DFF, D)
    conv_a_w = w["conv_a_w"]
    conv_f_w = w["conv_f_w"]

    h1 = _norm_fwd(x, small["norm_pre_mix"], tt)
    p4 = _mm_nn_sharded(h1, w["w_in"], F32, tm, "mm_in", _slot_of_chip)
    ya, hseq = _mixer_a_fwd(p4, conv_a_w, small["conv_a_b"], wa_bf, wx_bf, small["lru_ba"], small["lru_bx"],
                            small["lru_lambda"], tt)
    yb, o_all, ss = _hgrn2_fwd(p4, small["hg_lb_logits"], small["hg_norm_g"], tt)
    za, zb, mixb, m2, x1, h2 = _mid_fwd(ya, yb, p4, x, w_br_a, w_br_b, w_out, small["norm_post_mix"],
                                        small["norm_pre_ffn"], tt)
    up_pre = _mm_nn_sharded(h2, w["w_up"], F32, tm, "mm_up")
    tf = min(128, t)
    y = _ffn_act_fwd(up_pre, conv_f_w, small["conv_f_b"], tf)
    dx2, dm3, lossv, d_norm_post_ffn = _down_loss(y, w_down, x1, tgt, small["norm_post_ffn"], tt)

    d_w_down = _mm_tn(y, dm3, DFF // 2, D, tm, "mm_dw_down")
    dup_pre, d_conv_f_w, d_conv_f_b = _ffn_act_bwd(dm3, w_down, up_pre, conv_f_w, small["conv_f_b"], tf)
    d_w_up = _mm_tn(h2, dup_pre, D, SH_UP, tm, "mm_dw_up", stacked_out=True)
    dh2 = _mm_nt_sharded(dup_pre, w["w_up"], tm, "mm_dh2")
    dx1, dm2, dza, dzb, dya, dyb, dp4, d_norm_post_mix, d_norm_pre_ffn = _mid_bwd(
        dh2, dx2, x1, m2, za, zb, p4, w_br_a, w_br_b, w_out, small["norm_post_mix"], small["norm_pre_ffn"], tt)
    d_w_out = _mm_tn(mixb, dm2, D, D, tm, "mm_dw_out")
    d_w_br_a = _mm_tn(ya, dza, D, D, tm, "mm_dw_bra")
    d_w_br_b = _mm_tn(yb, dzb, D, D, tm, "mm_dw_brb")
    dp4, d_lb, d_hg_norm_g = _hgrn2_bwd(p4, o_all, ss, dyb, dp4, small["hg_lb_logits"], small["hg_norm_g"], tt)
    dp4, d_conv_a_w, d_conv_a_b, d_lru_wa, d_lru_wx, d_lru_ba, d_lru_bx, d_lru_lambda = _mixer_a_bwd(
        p4, hseq, dya, dp4, conv_a_w, small["conv_a_b"], wa_bf, wx_bf, small["lru_ba"], small["lru_bx"],
        small["lru_lambda"], tt)
    d_w_in = _mm_tn(h1, dp4, D, SH_IN, tm, "mm_dw_in", stacked_slot_fn=_slot_of_chip, stacked_out=True)
    dh1 = _mm_nt_sharded(dp4, w["w_in"], tm, "mm_dh1", stacked_slot_fn=_slot_of_chip)
    grad_x, d_norm_pre_mix = _norm_bwd(dh1, dx1, x, small["norm_pre_mix"], tt)

    big = {
        "w_in": d_w_in,
        "w_branch_a": d_w_br_a.reshape(NCHIP, SH_BR, D),
        "w_branch_b": d_w_br_b.reshape(NCHIP, SH_BR, D),
        "w_out": d_w_out.reshape(NCHIP, SH_BR, D),
        "w_up": d_w_up,
        "w_down": d_w_down.reshape(NCHIP, SH_DN, D),
    }
    smalls = {
        "norm_pre_mix": d_norm_pre_mix, "conv_a_b": d_conv_a_b, "lru_ba": d_lru_ba, "lru_bx": d_lru_bx,
        "lru_lambda": d_lru_lambda, "hg_lb_logits": d_lb, "hg_norm_g": d_hg_norm_g, "norm_post_mix": d_norm_post_mix,
        "norm_pre_ffn": d_norm_pre_ffn, "norm_post_ffn": d_norm_post_ffn, "lossv": lossv,
        "conv_a_w": d_conv_a_w, "lru_wa": d_lru_wa, "lru_wx": d_lru_wx,
        "conv_f_b": d_conv_f_b, "conv_f_w": d_conv_f_w,
    }
    return grad_x, big, smalls


BIG = ("w_in", "w_branch_a", "w_branch_b", "w_out", "w_up", "w_down")
BIG_SHAPE = {"w_in": (D, SH_IN), "w_branch_a": (SH_BR, D), "w_branch_b": (SH_BR, D), "w_out": (SH_BR, D),
             "w_up": (D, SH_UP), "w_down": (SH_DN, D)}
NBIG = len(BIG)
VEC_ROWS = (("norm_pre_mix", 0, 1), ("conv_a_b", 1, 1), ("lru_ba", 2, 1), ("lru_bx", 3, 1), ("lru_lambda", 4, 1),
            ("hg_lb_logits", 5, 2), ("hg_norm_g", 7, 1), ("norm_post_mix", 8, 1), ("norm_pre_ffn", 9, 1),
            ("norm_post_ffn", 10, 1))
ROW_LOSS = 11
ROW_CONV_A = 12
S1_ROWS = 16
S2_ROWS = 8


def _place():
    x, y, c = lax.axis_index("x"), lax.axis_index("y"), lax.axis_index("c")
    chips = [(1 - x, y), (x, 1 - y), (1 - x, 1 - y)]
    return x, y, c, 2 * x + y, chips


def _remote(src, dst, ssem, rsem, dev):
    return pltpu.make_async_remote_copy(src_ref=src, dst_ref=dst, send_sem=ssem, recv_sem=rsem,
                                        device_id=dev, device_id_type=MESH)


def _hbm_call(body, name, ins, out_shapes, n_sems, aliases=None):
    any_spec = pl.BlockSpec(memory_space=pl.ANY)
    return pl.pallas_call(
        body, name=name, out_shape=tuple(out_shapes),
        in_specs=[any_spec] * len(ins), out_specs=tuple([any_spec] * len(out_shapes)),
        scratch_shapes=[pltpu.SemaphoreType.DMA((n,)) for n in n_sems],
        input_output_aliases=aliases or {},
        compiler_params=pltpu.CompilerParams(has_side_effects=True),
    )(*ins)


def _gather_weights(stacked, conv_a_s, conv_f_s):
    ins = [stacked[n] for n in BIG] + [conv_a_s, conv_f_s]
    n_in = len(ins)
    halves = [BIG_SHAPE[n][0] // 2 for n in BIG]
    out_shapes = [jax.ShapeDtypeStruct(stacked[n].shape, stacked[n].dtype) for n in BIG]
    out_shapes += [jax.ShapeDtypeStruct((NCHIP,) + a.shape, a.dtype) for a in (conv_a_s, conv_f_s)]

    def body(*refs):
        src, dst = refs[:n_in], refs[n_in:2 * n_in]
        ssem, rsem, fssem, frsem, lsem = refs[2 * n_in:]
        x, y, c, j, chips = _place()

        def half(ref, w, which):
            return ref.at[pl.ds(which * halves[w], halves[w]), :]

        locs = [pltpu.make_async_copy(src[w], dst[w].at[j], lsem.at[w - NBIG]) for w in range(NBIG, n_in)]
        for cp in locs:
            cp.start()
        sends = []
        for w in range(n_in):
            for k, (cx, cy) in enumerate(chips):
                if w < NBIG:
                    mine = half(dst[w].at[j], w, c)
                    cp = _remote(mine, mine, ssem.at[3 * w + k], rsem.at[3 * w + k], (cx, cy, c))
                else:
                    cp = _remote(src[w], dst[w].at[j], ssem.at[3 * w + k], rsem.at[3 * w + k], (cx, cy, c))
                cp.start()
                sends.append(cp)
        fwds = []
        for w in range(n_in):
            for k, (cx, cy) in enumerate(chips):
                jk = 2 * cx + cy
                if w < NBIG:
                    got = half(dst[w].at[jk], w, c)
                    _remote(got, got, ssem.at[3 * w + k], rsem.at[3 * w + k], (cx, cy, c)).wait_recv()
                    cp = _remote(got, got, fssem.at[3 * w + k], frsem.at[3 * w + k], (x, y, 1 - c))
                    cp.start()
                    fwds.append(cp)
                else:
                    got = dst[w].at[jk]
                    _remote(got, got, ssem.at[3 * w + k], rsem.at[3 * w + k], (cx, cy, c)).wait_recv()
        for w in range(NBIG):
            for k, (cx, cy) in enumerate(chips):
                other = half(dst[w].at[2 * cx + cy], w, 1 - c)
                _remote(other, other, fssem.at[3 * w + k], frsem.at[3 * w + k], (x, y, 1 - c)).wait_recv()
        for cp in sends + fwds:
            cp.wait_send()
        for cp in locs:
            cp.wait()

    outs = _hbm_call(body, "gather_weights", ins, out_shapes, (3 * n_in, 3 * n_in, 3 * NBIG, 3 * NBIG, n_in - NBIG),
                     aliases={w: w for w in range(NBIG)})
    return dict(zip(BIG, outs[:NBIG])), outs[NBIG], outs[NBIG + 1]


def _reduce_stage1(big_g, s1, s2, s3):
    ins = [big_g[n] for n in BIG] + [s1, s2, s3]
    n_in = len(ins)
    halves = [BIG_SHAPE[n][0] // 2 for n in BIG]
    out_shapes = [jax.ShapeDtypeStruct((NCHIP, halves[w], BIG_SHAPE[n][1]), F32) for w, n in enumerate(BIG)]
    out_shapes += [jax.ShapeDtypeStruct(a.shape, F32) for a in (s1, s2, s3)]

    def body(*refs):
        src, dst = refs[:n_in], refs[n_in:2 * n_in]
        ssem, rsem = refs[2 * n_in:]
        x, y, c, _, _ = _place()
        cps = []
        for w in range(n_in):
            s_ = src[w].at[:, pl.ds((1 - c) * halves[w], halves[w]), :] if w < NBIG else src[w]
            cp = _remote(s_, dst[w], ssem.at[w], rsem.at[w], (x, y, 1 - c))
            cp.start()
            cps.append(cp)
        for cp in cps:
            cp.wait()

    outs = _hbm_call(body, "reduce_d2d_in", ins, out_shapes, (n_in, n_in))
    return dict(zip(BIG, outs[:NBIG])), outs[NBIG], outs[NBIG + 1], outs[NBIG + 2]


def _reduce_stage2(p_big, ps1, ps2, ps3):
    ins = [p_big[n] for n in BIG] + [ps1, ps2, ps3]
    n_in = len(ins)
    h1, h2, h3 = S1_ROWS // 2, DUP // 2, D
    out_shapes = [jax.ShapeDtypeStruct(p_big[n].shape, BF16) for n in BIG]
    out_shapes += [jax.ShapeDtypeStruct((NCHIP, h1, D), F32), jax.ShapeDtypeStruct((NCHIP, S2_ROWS, h2), F32),
                   jax.ShapeDtypeStruct((NCHIP, h3, HD), F32)]

    def body(*refs):
        src, dst = refs[:n_in], refs[n_in:2 * n_in]
        ssem, rsem = refs[2 * n_in:]
        x, y, c, j, chips = _place()

        def piece(w, to_chip):
            if w < NBIG:
                return src[w].at[to_chip]
            if w == NBIG:
                return src[w].at[pl.ds(c * h1, h1), :]
            if w == NBIG + 1:
                return src[w].at[:, pl.ds(c * h2, h2)]
            return src[w].at[pl.ds(c * h3, h3), :]

        sends = []
        for w in range(n_in):
            for k, (cx, cy) in enumerate(chips):
                cp = _remote(piece(w, 2 * cx + cy), dst[w].at[j], ssem.at[3 * w + k], rsem.at[3 * w + k], (cx, cy, c))
                cp.start()
                sends.append(cp)
        for w in range(n_in):
            for k, (cx, cy) in enumerate(chips):
                got = dst[w].at[2 * cx + cy]
                _remote(got, got, ssem.at[3 * w + k], rsem.at[3 * w + k], (cx, cy, c)).wait_recv()
        for cp in sends:
            cp.wait_send()

    outs = _hbm_call(body, "reduce_ici", ins, out_shapes, (3 * n_in, 3 * n_in))
    return dict(zip(BIG, outs[:NBIG])), outs[NBIG], outs[NBIG + 1], outs[NBIG + 2]


def _reduce_stage3(f_big, fs1, fs2, fs3):
    ins = [f_big[n] for n in BIG] + [fs1, fs2, fs3]
    n_in = len(ins)
    halves = [BIG_SHAPE[n][0] // 2 for n in BIG]
    h1, h2, h3 = S1_ROWS // 2, DUP // 2, D
    out_shapes = [jax.ShapeDtypeStruct(BIG_SHAPE[n], F32) for n in BIG]
    out_shapes += [jax.ShapeDtypeStruct((S1_ROWS, D), F32), jax.ShapeDtypeStruct((S2_ROWS, DUP), F32),
                   jax.ShapeDtypeStruct((2 * D, HD), F32)]

    def body(*refs):
        dst = refs[n_in:2 * n_in]
        ssem, rsem = refs[2 * n_in:]
        x, y, c, _, _ = _place()

        def place(w, which):
            if w < NBIG:
                return dst[w].at[pl.ds(which * halves[w], halves[w]), :]
            if w == NBIG:
                return dst[w].at[pl.ds(which * h1, h1), :]
            if w == NBIG + 1:
                return dst[w].at[:, pl.ds(which * h2, h2)]
            return dst[w].at[pl.ds(which * h3, h3), :]

        cps = [_remote(place(w, c), place(w, c), ssem.at[w], rsem.at[w], (x, y, 1 - c)) for w in range(n_in)]
        for cp in cps:
            cp.start()
        for w in range(n_in):
            got = place(w, 1 - c)
            _remote(got, got, ssem.at[w], rsem.at[w], (x, y, 1 - c)).wait_recv()
        for cp in cps:
            cp.wait_send()

    outs = _hbm_call(body, "reduce_d2d_out", ins, out_shapes, (n_in, n_in), aliases={w: w for w in range(n_in)})
    return dict(zip(BIG, outs[:NBIG])), outs[NBIG], outs[NBIG + 1], outs[NBIG + 2]


def _row_tile(rows):
    for tr in (128, 176, 64, 16, 8):
        if rows % tr == 0:
            return tr
    return rows


def _sum_own_half(g, rb, cidx, name):
    s, rows, cols = g.shape
    half = rows // 2
    tr = _row_tile(half)
    nb = half // tr

    def body(c_ref, g_ref, r_ref, o_ref):
        del c_ref
        o_ref[...] = (g_ref[...] + r_ref[...]).astype(BF16)

    grid_spec = pltpu.PrefetchScalarGridSpec(
        num_scalar_prefetch=1, grid=(s, nb),
        in_specs=[pl.BlockSpec((None, tr, cols), lambda k, i, c: (k, c[0] * nb + i, 0)),
                  pl.BlockSpec((None, tr, cols), lambda k, i, c: (k, i, 0))],
        out_specs=pl.BlockSpec((None, tr, cols), lambda k, i, c: (k, i, 0)))
    return pl.pallas_call(
        body, name=name, grid_spec=grid_spec, out_shape=jax.ShapeDtypeStruct((s, half, cols), BF16),
        compiler_params=_params(("parallel", "parallel")),
    )(cidx, g, rb)


def _sum_chips(q, p, jc, name, by_cols=False):
    s, rows, cols = q.shape
    tr = _row_tile(rows)
    nb = rows // tr
    stacked = p.ndim == 3

    def body(jc_ref, q_ref, p_ref, o_ref):
        j = jc_ref[0]
        own = p_ref[...].astype(F32)
        acc = None
        for k in range(NCHIP):
            term = jnp.where(j == k, own, q_ref[k].astype(F32))
            acc = term if acc is None else acc + term
        o_ref[...] = acc

    if by_cols:
        half_spec = pl.BlockSpec((tr, cols), lambda i, jc_ref: (i, jc_ref[1]))
        out_shape = jax.ShapeDtypeStruct((rows, 2 * cols), F32)
    else:
        half_spec = pl.BlockSpec((tr, cols), lambda i, jc_ref: (jc_ref[1] * nb + i, 0))
        out_shape = jax.ShapeDtypeStruct((2 * rows, cols), F32)
    p_spec = pl.BlockSpec((None, tr, cols), lambda i, jc_ref: (jc_ref[0], i, 0)) if stacked else half_spec
    grid_spec = pltpu.PrefetchScalarGridSpec(
        num_scalar_prefetch=1, grid=(nb,),
        in_specs=[pl.BlockSpec((s, tr, cols), lambda i, jc_ref: (0, i, 0)), p_spec],
        out_specs=half_spec)
    return pl.pallas_call(
        body, name=name, grid_spec=grid_spec, out_shape=out_shape,
        compiler_params=_params(("parallel",)),
    )(jc, q, p)


def _place_shard(w, jc, name):
    rows, cols = w.shape
    tr = _row_tile(rows)

    def body(jc_ref, w_ref, o_ref):
        del jc_ref
        o_ref[...] = w_ref[...].astype(BF16)

    grid_spec = pltpu.PrefetchScalarGridSpec(
        num_scalar_prefetch=1, grid=(rows // tr,),
        in_specs=[pl.BlockSpec((tr, cols), lambda i, jc_ref: (i, 0))],
        out_specs=pl.BlockSpec((None, tr, cols), lambda i, jc_ref: (jc_ref[0], i, 0)))
    return pl.pallas_call(
        body, name=name, grid_spec=grid_spec, out_shape=jax.ShapeDtypeStruct((NCHIP, rows, cols), BF16),
        compiler_params=_params(("parallel",)),
    )(jc, w)


def _add(a, b, name):
    def body(a_ref, b_ref, o_ref):
        o_ref[...] = a_ref[...] + b_ref[...]

    return pl.pallas_call(body, name=name, out_shape=jax.ShapeDtypeStruct(a.shape, F32))(a, b)


def _pack_small(sm):
    vec_in = [sm[n] for n, _, _ in VEC_ROWS]
    nv = len(vec_in)

    def body(*refs):
        ins, lossv, dcw, dcfb, dcfw, s1, s2 = refs[:nv], refs[nv], refs[nv + 1], refs[nv + 2], refs[nv + 3], \
            refs[nv + 4], refs[nv + 5]
        for ref, (_, r0, nr) in zip(ins, VEC_ROWS):
            s1[r0:r0 + nr, :] = ref[...]
        s1[ROW_LOSS:ROW_LOSS + 1, :] = lossv[...]
        s1[ROW_CONV_A:ROW_CONV_A + 4, :] = dcw[...]
        s2[0:1, :] = dcfb[...]
        s2[1:4, :] = dcfw[...]
        s2[4:8, :] = jnp.zeros((4, DUP), F32)

    return pl.pallas_call(
        body, name="pack_small",
        out_shape=(jax.ShapeDtypeStruct((S1_ROWS, D), F32), jax.ShapeDtypeStruct((S2_ROWS, DUP), F32)),
    )(*vec_in, sm["lossv"], sm["conv_a_w"], sm["conv_f_b"], sm["conv_f_w"])


def _adam_math(w, g, m, v):
    m = ADAM_B1 * m + (1.0 - ADAM_B1) * g
    v = ADAM_B2 * v + (1.0 - ADAM_B2) * (g * g)
    m_hat = m / (1.0 - ADAM_B1 ** ADAM_STEP)
    v_hat = v / (1.0 - ADAM_B2 ** ADAM_STEP)
    delta = -ADAM_LR * (m_hat / (jnp.sqrt(v_hat) + ADAM_EPS) + ADAM_WD * w)
    return delta, m, v


def _adam(w, g, m, v, name):
    rows, cols = w.shape
    tr = _row_tile(rows)

    def body(w_ref, g_ref, m_ref, v_ref, d_ref, mo_ref, vo_ref):
        d_ref[...], mo_ref[...], vo_ref[...] = _adam_math(w_ref[...], g_ref[...], m_ref[...], v_ref[...])

    spec = pl.BlockSpec((tr, cols), lambda i: (i, 0))
    return pl.pallas_call(
        body, name=name, out_shape=(jax.ShapeDtypeStruct(w.shape, F32),) * 3, grid=(rows // tr,),
        in_specs=[spec] * 4, out_specs=(spec,) * 3,
        compiler_params=_params(("parallel",)),
    )(w, g, m, v)


def _adam_small(gs1, gs2, gs3, w, m, v):
    names = [n for n, _, _ in VEC_ROWS] + ["conv_f_b", "lru_wa", "lru_wx"]
    nn = len(names)

    def grad_of(i, g1, g2, g3):
        if i < len(VEC_ROWS):
            _, r0, nr = VEC_ROWS[i]
            return g1[r0:r0 + nr, :]
        if names[i] == "conv_f_b":
            return g2[0:1, :]
        return g3[0] if names[i] == "lru_wa" else g3[1]

    def body(*refs):
        g1, g2, g3 = refs[0], refs[1], refs[2]
        ws, ms, vs = refs[3:3 + nn], refs[3 + nn:3 + 2 * nn], refs[3 + 2 * nn:3 + 3 * nn]
        outs = refs[3 + 3 * nn:]
        for i in range(nn):
            d, mn, vn = _adam_math(ws[i][...], grad_of(i, g1, g2, g3), ms[i][...], vs[i][...])
            outs[i][...] = d
            outs[nn + i][...] = mn
            outs[2 * nn + i][...] = vn

    shapes = [jax.ShapeDtypeStruct(w[n].shape, F32) for n in names]
    outs = pl.pallas_call(body, name="adam_small", out_shape=tuple(shapes * 3))(
        gs1, gs2, gs3, *[w[n] for n in names], *[m[n] for n in names], *[v[n] for n in names])
    return {n: (outs[i], outs[nn + i], outs[2 * nn + i]) for i, n in enumerate(names)}


WEIGHTS = ("norm_pre_mix", "w_in", "conv_a_w", "conv_a_b", "lru_wa", "lru_ba", "lru_wx", "lru_bx", "lru_lambda",
           "hg_lb_logits", "hg_norm_g", "w_branch_a", "w_branch_b", "w_out", "norm_post_mix", "norm_pre_ffn",
           "w_up", "conv_f_w", "conv_f_b", "w_down", "norm_post_ffn")
NW = len(WEIGHTS)


def kernel(x, norm_pre_mix, w_in, conv_a_w, conv_a_b, lru_wa, lru_ba, lru_wx, lru_bx, lru_lambda, hg_lb_logits, hg_norm_g, w_branch_a, w_branch_b, w_out, norm_post_mix, norm_pre_ffn, w_up, conv_f_w, conv_f_b, w_down, norm_post_ffn, loss_target, m_norm_pre_mix, m_w_in, m_conv_a_w, m_conv_a_b, m_lru_wa, m_lru_ba, m_lru_wx, m_lru_bx, m_lru_lambda, m_hg_lb_logits, m_hg_norm_g, m_w_branch_a, m_w_branch_b, m_w_out, m_norm_post_mix, m_norm_pre_ffn, m_w_up, m_conv_f_w, m_conv_f_b, m_w_down, m_norm_post_ffn, v_norm_pre_mix, v_w_in, v_conv_a_w, v_conv_a_b, v_lru_wa, v_lru_ba, v_lru_wx, v_lru_bx, v_lru_lambda, v_hg_lb_logits, v_hg_norm_g, v_w_branch_a, v_w_branch_b, v_w_out, v_norm_post_mix, v_norm_pre_ffn, v_w_up, v_conv_f_w, v_conv_f_b, v_w_down, v_norm_post_ffn):
    rest = (norm_pre_mix, w_in, conv_a_w, conv_a_b, lru_wa, lru_ba, lru_wx, lru_bx, lru_lambda, hg_lb_logits, hg_norm_g, w_branch_a, w_branch_b, w_out, norm_post_mix, norm_pre_ffn, w_up, conv_f_w, conv_f_b, w_down, norm_post_ffn, loss_target, m_norm_pre_mix, m_w_in, m_conv_a_w, m_conv_a_b, m_lru_wa, m_lru_ba, m_lru_wx, m_lru_bx, m_lru_lambda, m_hg_lb_logits, m_hg_norm_g, m_w_branch_a, m_w_branch_b, m_w_out, m_norm_post_mix, m_norm_pre_ffn, m_w_up, m_conv_f_w, m_conv_f_b, m_w_down, m_norm_post_ffn, v_norm_pre_mix, v_w_in, v_conv_a_w, v_conv_a_b, v_lru_wa, v_lru_ba, v_lru_wx, v_lru_bx, v_lru_lambda, v_hg_lb_logits, v_hg_norm_g, v_w_branch_a, v_w_branch_b, v_w_out, v_norm_post_mix, v_norm_pre_ffn, v_w_up, v_conv_f_w, v_conv_f_b, v_w_down, v_norm_post_ffn)
    w_in_args = dict(zip(WEIGHTS, rest[:NW]))
    loss_target = rest[NW]
    m_args = dict(zip(WEIGHTS, rest[NW + 1:2 * NW + 1]))
    v_args = dict(zip(WEIGHTS, rest[2 * NW + 1:3 * NW + 1]))
    shape_of = {n: w_in_args[n].shape for n in WEIGHTS}

    def two_d(n, a):
        if n in BIG:
            return a.reshape(BIG_SHAPE[n])
        if n in ("lru_wa", "lru_wx"):
            return a.reshape(NH, HD, HD)
        return a.reshape(a.shape[-2:])

    w2 = {n: two_d(n, w_in_args[n]) for n in WEIGHTS}
    m2 = {n: two_d(n, m_args[n]) for n in WEIGHTS}
    v2 = {n: two_d(n, v_args[n]) for n in WEIGHTS}

    cidx = lax.axis_index("c").astype(jnp.int32).reshape(1)
    jchip = 2 * lax.axis_index("x") + lax.axis_index("y")

    jc = jnp.stack([jchip, lax.axis_index("c")]).astype(jnp.int32)

    shards = {n: _place_shard(w2[n], jc, "place_" + n) for n in BIG}
    conv_a_s = jnp.pad(w2["conv_a_w"], ((0, 4), (0, 0)))
    conv_f_s = jnp.pad(w2["conv_f_w"], ((0, 5), (0, 0)))
    w_full, conv_a_g, conv_f_g = _gather_weights(shards, conv_a_s, conv_f_s)
    w_full["conv_a_w"] = jnp.transpose(conv_a_g, (1, 0, 2)).reshape(8, D)[0:4]
    w_full["conv_f_w"] = jnp.transpose(conv_f_g, (1, 0, 2)).reshape(8, DUP)[0:3]
    small = {n: w2[n] for n in WEIGHTS if n not in BIG and n not in ("conv_a_w", "conv_f_w")}

    grad_x, big_g, sm_g = _local_step(x[0], loss_target[0], w_full, small)

    s1, s2 = _pack_small(sm_g)
    s3 = jnp.concatenate([sm_g["lru_wa"].reshape(D, HD), sm_g["lru_wx"].reshape(D, HD)], axis=0)
    rb, rs1, rs2, rs3 = _reduce_stage1(big_g, s1, s2, s3)
    p_big = {n: _sum_own_half(big_g[n], rb[n], cidx, "sum_half_" + n) for n in BIG}
    ps1, ps2, ps3 = _add(s1, rs1, "add_s1"), _add(s2, rs2, "add_s2"), _add(s3, rs3, "add_s3")
    q_big, qs1, qs2, qs3 = _reduce_stage2(p_big, ps1, ps2, ps3)
    f_big = {n: _sum_chips(q_big[n], p_big[n], jc, "sum_chips_" + n) for n in BIG}
    fs1 = _sum_chips(qs1, ps1, jc, "sum_chips_s1")
    fs2 = _sum_chips(qs2, ps2, jc, "sum_chips_s2", by_cols=True)
    fs3 = _sum_chips(qs3, ps3, jc, "sum_chips_s3")
    g_big, gs1, gs2, gs3 = _reduce_stage3(f_big, fs1, fs2, fs3)

    res = {}
    for n in BIG:
        d, mn, vn = _adam(w2[n], g_big[n], m2[n], v2[n], "adam_" + n)
        res[n] = (g_big[n], d, mn, vn)
    small_res = _adam_small(gs1, gs2, gs3.reshape(2, NH, HD, HD), w2, m2, v2)
    for n, r0, nr in VEC_ROWS:
        res[n] = (gs1[r0:r0 + nr],) + small_res[n]
    res["conv_f_b"] = (gs2[0:1],) + small_res["conv_f_b"]
    res["lru_wa"] = (gs3[0:D].reshape(NH, HD, HD),) + small_res["lru_wa"]
    res["lru_wx"] = (gs3[D:2 * D].reshape(NH, HD, HD),) + small_res["lru_wx"]
    g_ca = lax.dynamic_slice_in_dim(gs1[ROW_CONV_A:ROW_CONV_A + 4], jchip * (D // NCHIP), D // NCHIP, axis=1)
    g_cf = lax.dynamic_slice_in_dim(gs2[1:4], jchip * SH_UP, SH_UP, axis=1)
    res["conv_a_w"] = (g_ca,) + _adam(w2["conv_a_w"], g_ca, m2["conv_a_w"], v2["conv_a_w"], "adam_conv_a_w")
    res["conv_f_w"] = (g_cf,) + _adam(w2["conv_f_w"], g_cf, m2["conv_f_w"], v2["conv_f_w"], "adam_conv_f_w")

    loss = (0.5 / D) * jnp.sum(gs1[ROW_LOSS])
    out = [loss, grad_x.reshape(x.shape)]
    for part in range(4):
        out += [res[n][part].reshape(shape_of[n]) for n in WEIGHTS]
    return tuple(out)
```

```python
import functools

import jax
import jax.numpy as jnp
from jax import lax
from jax.experimental import pallas as pl
from jax.experimental.pallas import tpu as pltpu

F32 = jnp.float32
BF16 = jnp.bfloat16

D = 1024
NH = 8
HD = 128
CH = 32
DFF = 2816
DUP = 2 * DFF
NCHIP = 4
SH_IN = 2 * D
SH_UP = DUP // NCHIP
SH_DN = DFF // NCHIP
SH_BR = D // NCHIP
EPS = 1e-6
LRU_C = 8.0
ADAM_LR = 0.001
ADAM_B1 = 0.9
ADAM_B2 = 0.999
ADAM_EPS = 1e-08
ADAM_WD = 0.01
ADAM_STEP = 10
VMEM_BIG = 56 * 1024 * 1024
MESH = pl.DeviceIdType.MESH

SLOT_A, SLOT_B, SLOT_C, SLOT_G = 2, 0, 1, 3


def _slot_of_chip(s):
    return jnp.where(s == 3, 3, (s + 2) % 3)


def _params(sem, vmem=None):
    return pltpu.CompilerParams(dimension_semantics=sem, vmem_limit_bytes=vmem)


_GC = 0.7978845608028654
_GA = 0.044715


def _gelu(x):
    return 0.5 * x * (1.0 + jnp.tanh(_GC * (x + _GA * x * x * x)))


def _gelu_and_grad(x):
    x2 = x * x
    th = jnp.tanh(_GC * x * (1.0 + _GA * x2))
    g = 0.5 * x * (1.0 + th)
    dg = 0.5 * (1.0 + th) + 0.5 * x * (1.0 - th * th) * _GC * (1.0 + 3.0 * _GA * x2)
    return g, dg


def _sig(x):
    return jax.nn.sigmoid(x)


def _dot(a, b):
    return jnp.dot(a, b, preferred_element_type=F32)


def _dot_nt(a, b):
    return lax.dot_general(a, b, (((1,), (1,)), ((), ())), preferred_element_type=F32)


def _dot_tn(a, b):
    return lax.dot_general(a, b, (((0,), (0,)), ((), ())), preferred_element_type=F32)


def _chunk_cumsum(x):
    pos = lax.broadcasted_iota(jnp.int32, (x.shape[0], 1), 0) & (CH - 1)
    d = 1
    while d < CH:
        x = x + jnp.where(pos >= d, pltpu.roll(x, d, 0), 0.0)
        d *= 2
    return x


def _chunk_revcumsum(x):
    n = x.shape[0]
    pos = lax.broadcasted_iota(jnp.int32, (n, 1), 0) & (CH - 1)
    d = 1
    while d < CH:
        x = x + jnp.where(pos < CH - d, pltpu.roll(x, n - d, 0), 0.0)
        d *= 2
    return x


def _chunk_last(x):
    n = x.shape[0]
    return jnp.concatenate(
        [jnp.broadcast_to(x[c * CH + CH - 1:c * CH + CH, :], (CH, x.shape[1])) for c in range(n // CH)], axis=0)


def _chunk_total(x):
    n = x.shape[0]
    return jnp.concatenate(
        [jnp.broadcast_to(jnp.sum(x[c * CH:(c + 1) * CH, :], axis=0, keepdims=True), (CH, x.shape[1]))
         for c in range(n // CH)], axis=0)


def _rms_stats(x):
    r = lax.rsqrt(jnp.mean(x * x, axis=-1, keepdims=True) + EPS)
    return r, x * r


def _rms_bwd(gd, n, r):
    return r * (gd - n * jnp.mean(gd * n, axis=-1, keepdims=True))


def _shift_rows(x, d, fill):
    rows = lax.broadcasted_iota(jnp.int32, (x.shape[0], 1), 0)
    return jnp.where(rows >= d, pltpu.roll(x, d, 0), fill)


def _scan_down(a, u, carry):
    n = a.shape[0]
    pos = lax.broadcasted_iota(jnp.int32, (n, 1), 0) & 7
    for d in (1, 2, 4):
        u = a * jnp.where(pos >= d, pltpu.roll(u, d, 0), 0.0) + u
        a = a * jnp.where(pos >= d, pltpu.roll(a, d, 0), 1.0)
    out = []
    for v in range(n // 8):
        h = a[v * 8:v * 8 + 8, :] * carry + u[v * 8:v * 8 + 8, :]
        carry = h[7:8, :]
        out.append(h)
    return jnp.concatenate(out, axis=0)


def _scan_up(b, g, carry):
    n = b.shape[0]
    pos = lax.broadcasted_iota(jnp.int32, (n, 1), 0) & 7
    for d in (1, 2, 4):
        g = g + b * jnp.where(pos < 8 - d, pltpu.roll(g, n - d, 0), 0.0)
        b = b * jnp.where(pos < 8 - d, pltpu.roll(b, n - d, 0), 1.0)
    out = [None] * (n // 8)
    for v in reversed(range(n // 8)):
        h = g[v * 8:v * 8 + 8, :] + b[v * 8:v * 8 + 8, :] * carry
        carry = h[0:1, :]
        out[v] = h
    return jnp.concatenate(out, axis=0)


def _shift_rows_up(x, d, fill):
    n = x.shape[0]
    rows = lax.broadcasted_iota(jnp.int32, (n, 1), 0)
    return jnp.where(rows < n - d, pltpu.roll(x, n - d, 0), fill)


def _mm_nn_sharded(a, b3, out_dtype, tm, name, slot_fn=None):
    m, k = a.shape
    s, _, ns = b3.shape

    def body(a_ref, b_ref, o_ref):
        o_ref[...] = _dot(a_ref[...], b_ref[...]).astype(out_dtype)

    if slot_fn is None:
        out_shape = jax.ShapeDtypeStruct((m, s * ns), out_dtype)
        out_spec = pl.BlockSpec((tm, ns), lambda j, i: (i, j))
    else:
        out_shape = jax.ShapeDtypeStruct((s, m, ns), out_dtype)
        out_spec = pl.BlockSpec((None, tm, ns), lambda j, i: (slot_fn(j), i, 0))
    return pl.pallas_call(
        body, name=name, out_shape=out_shape, grid=(s, m // tm),
        in_specs=[pl.BlockSpec((tm, k), lambda j, i: (i, 0)),
                  pl.BlockSpec((None, k, ns), lambda j, i: (j, 0, 0))],
        out_specs=out_spec,
        compiler_params=_params(("parallel", "parallel"), VMEM_BIG),
    )(a, b3)


def _mm_in_gather(a, w_in3, stk_rest, names_rest, tm):
    m, k = a.shape
    s, _, ns = w_in3.shape
    nr = len(stk_rest)
    mt = m // tm

    def body(a_ref, b_ref, *rest):
        o_ref = rest[nr]
        stk = rest[nr + 1:2 * nr + 1]
        ssem, rsem = rest[2 * nr + 1:]
        step = pl.program_id(0) * mt + pl.program_id(1)
        sends, arrive, _, _ = _gather_copies(stk, names_rest, ssem, rsem)

        @pl.when(step == 0)
        def _():
            for cp in sends:
                cp.start()

        o_ref[...] = _dot(a_ref[...], b_ref[...])

        @pl.when(step == s * mt - 1)
        def _():
            for cp in arrive:
                cp.wait_recv()
            for cp in sends:
                cp.wait_send()

    any_spec = pl.BlockSpec(memory_space=pl.ANY)
    outs = pl.pallas_call(
        body, name="mm_in",
        out_shape=(jax.ShapeDtypeStruct((s, m, ns), F32),) + tuple(jax.ShapeDtypeStruct(x.shape, x.dtype) for x in stk_rest),
        grid=(s, mt),
        in_specs=[pl.BlockSpec((tm, k), lambda j, i: (i, 0)), pl.BlockSpec((None, k, ns), lambda j, i: (j, 0, 0))]
        + [any_spec] * nr,
        out_specs=(pl.BlockSpec((None, tm, ns), lambda j, i: (_slot_of_chip(j), i, 0)),) + (any_spec,) * nr,
        scratch_shapes=[pltpu.SemaphoreType.DMA((3 * nr,)), pltpu.SemaphoreType.DMA((3 * nr,))],
        input_output_aliases={2 + w: 1 + w for w in range(nr)},
        compiler_params=pltpu.CompilerParams(dimension_semantics=("arbitrary", "arbitrary"),
                                             vmem_limit_bytes=VMEM_BIG, has_side_effects=True),
    )(a, w_in3, *stk_rest)
    return outs[0], list(outs[1:])


def _mm_nt_sharded(a, b3, tm, name, stacked_slot_fn=None):
    s, k, ns = b3.shape
    m = a.shape[1] if stacked_slot_fn is not None else a.shape[0]

    def body(a_ref, b_ref, o_ref, acc_ref):
        j = pl.program_id(1)
        t = _dot_nt(a_ref[...], b_ref[...])

        @pl.when(j == 0)
        def _():
            acc_ref[...] = t

        @pl.when(j > 0)
        def _():
            acc_ref[...] += t

        @pl.when(j == s - 1)
        def _():
            o_ref[...] = acc_ref[...]

    if stacked_slot_fn is None:
        a_spec = pl.BlockSpec((tm, ns), lambda i, j: (i, j))
    else:
        a_spec = pl.BlockSpec((None, tm, ns), lambda i, j: (stacked_slot_fn(j), i, 0))
    return pl.pallas_call(
        body, name=name, out_shape=jax.ShapeDtypeStruct((m, k), F32), grid=(m // tm, s),
        in_specs=[a_spec, pl.BlockSpec((None, k, ns), lambda i, j: (j, 0, 0))],
        out_specs=pl.BlockSpec((tm, k), lambda i, j: (i, 0)),
        scratch_shapes=[pltpu.VMEM((tm, k), F32)],
        compiler_params=_params(("parallel", "arbitrary"), VMEM_BIG),
    )(a, b3)


def _mm_tn(a, g, tkk, tn, tk, name, stacked_slot_fn=None, stacked_out=False):
    m, k = a.shape
    if stacked_slot_fn is not None:
        n = g.shape[0] * g.shape[2]
        g_spec = pl.BlockSpec((None, tk, tn), lambda kk, j, mm: (stacked_slot_fn(j), mm, 0))
    else:
        n = g.shape[1]
        g_spec = pl.BlockSpec((tk, tn), lambda kk, j, mm: (mm, j))

    def body(a_ref, g_ref, o_ref):
        mm = pl.program_id(2)
        t = _dot_tn(a_ref[...], g_ref[...])

        @pl.when(mm == 0)
        def _():
            o_ref[...] = t

        @pl.when(mm > 0)
        def _():
            o_ref[...] += t

    if stacked_out:
        out_shape = jax.ShapeDtypeStruct((n // tn, k, tn), F32)
        out_spec = pl.BlockSpec((None, tkk, tn), lambda kk, j, mm: (j, kk, 0))
    else:
        out_shape = jax.ShapeDtypeStruct((k, n), F32)
        out_spec = pl.BlockSpec((tkk, tn), lambda kk, j, mm: (kk, j))
    return pl.pallas_call(
        body, name=name, out_shape=out_shape, grid=(k // tkk, n // tn, m // tk),
        in_specs=[pl.BlockSpec((tk, tkk), lambda kk, j, mm: (mm, kk)), g_spec],
        out_specs=out_spec,
        compiler_params=_params(("parallel", "parallel", "arbitrary"), VMEM_BIG),
    )(a, g)


def _norm_fwd(x, gain, tt):
    t = x.shape[0]

    def body(x_ref, g_ref, h_ref):
        _, n = _rms_stats(x_ref[...])
        h_ref[...] = (n * g_ref[...]).astype(BF16)

    return pl.pallas_call(
        body, name="norm_fwd", out_shape=jax.ShapeDtypeStruct((t, D), BF16), grid=(t // tt,),
        in_specs=[pl.BlockSpec((tt, D), lambda i: (i, 0)), pl.BlockSpec((1, D), lambda i: (0, 0))],
        out_specs=pl.BlockSpec((tt, D), lambda i: (i, 0)),
        compiler_params=_params(("parallel",)),
    )(x, gain)


def _lru_gates(xc, wa_ref, wx_ref, ba, bx, lam):
    xcb = xc.astype(BF16)
    ra = jnp.concatenate([_dot(xcb[:, n * HD:(n + 1) * HD], wa_ref[n]) for n in range(NH)], axis=1) + ba
    ix = jnp.concatenate([_dot(xcb[:, n * HD:(n + 1) * HD], wx_ref[n]) for n in range(NH)], axis=1) + bx
    r = _sig(ra)
    ig = _sig(ix)
    z = -lam
    sp = jnp.maximum(z, 0.0) + jnp.log1p(jnp.exp(-jnp.abs(z)))
    log_a = -LRU_C * r * sp
    a = jnp.exp(log_a)
    z2 = 2.0 * log_a
    series = -z2 * (1.0 + z2 * (0.5 + z2 * (1.0 / 6.0 + z2 * (1.0 / 24.0))))
    om = jnp.where(z2 > -0.02, series, 1.0 - jnp.exp(z2))
    mult = jnp.sqrt(om)
    return xcb, r, ig, sp, a, mult


def _mixer_a_fwd(p4, cw, cb, wa, wx, ba, bx, lam, tt):
    t = p4.shape[1]

    def body(p_ref, cw_ref, cb_ref, wa_ref, wx_ref, ba_ref, bx_ref, lam_ref, ya_ref, h_ref, halo, hc):
        i = pl.program_id(0)

        @pl.when(i == 0)
        def _():
            halo[...] = jnp.zeros((8, D), F32)
            hc[...] = jnp.zeros((1, D), F32)

        xa = p_ref[:, 0:D]
        ga = p_ref[:, D:2 * D]
        xe = jnp.concatenate([halo[...], xa], axis=0)
        xc = (cb_ref[...] + cw_ref[3:4, :] * xe
              + sum(cw_ref[3 - s:4 - s, :] * pltpu.roll(xe, s, 0) for s in (1, 2, 3)))[8:, :]
        halo[...] = xa[tt - 8:, :]
        _, _, ig, _, a, mult = _lru_gates(xc, wa_ref, wx_ref, ba_ref[...], bx_ref[...], lam_ref[...])
        u = mult * ig * xc
        h = _scan_down(a, u, hc[...])
        hc[...] = h[tt - 1:tt, :]
        h_ref[...] = h
        ya_ref[...] = (h * _gelu(ga)).astype(BF16)

    full = lambda shape: pl.BlockSpec(shape, lambda i: (0,) * len(shape))
    return pl.pallas_call(
        body, name="mixer_a_fwd",
        out_shape=(jax.ShapeDtypeStruct((t, D), BF16), jax.ShapeDtypeStruct((t, D), F32)),
        grid=(t // tt,),
        in_specs=[pl.BlockSpec((None, tt, 2 * D), lambda i: (SLOT_A, i, 0)),
                  full((4, D)), full((1, D)), full((NH, HD, HD)), full((NH, HD, HD)),
                  full((1, D)), full((1, D)), full((1, D))],
        out_specs=(pl.BlockSpec((tt, D), lambda i: (i, 0)), pl.BlockSpec((tt, D), lambda i: (i, 0))),
        scratch_shapes=[pltpu.VMEM((8, D), F32), pltpu.VMEM((1, D), F32)],
        compiler_params=_params(("arbitrary",), VMEM_BIG),
    )(p4, cw, cb, wa, wx, ba, bx, lam)


def _chunk_masks(tt):
    row = lax.broadcasted_iota(jnp.int32, (tt, tt), 0)
    col = lax.broadcasted_iota(jnp.int32, (tt, tt), 1)
    same = jnp.right_shift(row, 5) == jnp.right_shift(col, 5)
    return same & (col <= row)


def _hg_head_fwd(q, fz, lbh):
    sg = _sig(fz)
    sgn = _sig(-fz)
    f = lbh + (1.0 - lbh) * sg
    logf = jnp.log(f)
    k = (1.0 - lbh) * sgn
    g = _chunk_cumsum(logf)
    gu = _chunk_last(g) - g
    eg = jnp.exp(g)
    eng = jnp.exp(-g)
    egu = jnp.exp(gu)
    qt = q * eg
    kt = k * eng
    kd = k * egu
    return sg, sgn, f, k, g, eg, eng, egu, qt, kt, kd


def _lb_of(logits_ref):
    return _sig(logits_ref[0:1, :] - logits_ref[1:2, :])


def _hgrn2_fwd(p4, logits, gnorm, tt):
    t = p4.shape[1]
    nc = tt // CH

    def body(p_ref, lg_ref, gn_ref, yb_ref, o_ref, ss_ref, st):
        i = pl.program_id(0)

        @pl.when(i == 0)
        def _():
            st[...] = jnp.zeros((NH, HD, HD), F32)

        low = _chunk_masks(tt)
        lb = _lb_of(lg_ref)
        heads = [slice(h * HD, (h + 1) * HD) for h in range(NH)]
        _, _, _, _, g, _, _, _, qt, kt, kd = _hg_head_fwd(p_ref[0, :, 0:D], p_ref[0, :, D:2 * D], lb)
        qtb, ktb, kdb, vb = qt.astype(BF16), kt.astype(BF16), kd.astype(BF16), p_ref[1, :, 0:D].astype(BF16)
        decs = [jnp.exp(g[c * CH + CH - 1:c * CH + CH, :]) for c in range(nc)]
        o_in = []
        for hs in heads:
            att = jnp.where(low, _dot_nt(qtb[:, hs], ktb[:, hs]), 0.0)
            o_in.append(_dot(att.astype(BF16), vb[:, hs]))
        s_t = [st[h] for h in range(NH)]
        pieces = [[None] * nc for _ in range(NH)]
        for c in range(nc):
            sl = slice(c * CH, (c + 1) * CH)
            for h, hs in enumerate(heads):
                s_bf = s_t[h].astype(BF16)
                ss_ref[c, h] = s_bf
                pieces[h][c] = o_in[h][sl] + _dot_nt(qtb[sl, hs], s_bf)
                s_t[h] = s_t[h] * decs[c][:, hs] + _dot_tn(vb[sl, hs], kdb[sl, hs])
        for h, hs in enumerate(heads):
            st[h] = s_t[h]
            o = jnp.concatenate(pieces[h], axis=0)
            _, n = _rms_stats(o)
            og = p_ref[1, :, D + h * HD:D + (h + 1) * HD]
            o_ref[:, hs] = o
            yb_ref[:, hs] = (n * gn_ref[:, hs] * (og * _sig(og))).astype(BF16)

    return pl.pallas_call(
        body, name="hgrn2_fwd",
        out_shape=(jax.ShapeDtypeStruct((t, D), BF16), jax.ShapeDtypeStruct((t, D), F32),
                   jax.ShapeDtypeStruct((t // CH, NH, HD, HD), BF16)),
        grid=(t // tt,),
        in_specs=[pl.BlockSpec((2, tt, 2 * D), lambda i: (0, i, 0)),
                  pl.BlockSpec((2, D), lambda i: (0, 0)), pl.BlockSpec((1, D), lambda i: (0, 0))],
        out_specs=(pl.BlockSpec((tt, D), lambda i: (i, 0)), pl.BlockSpec((tt, D), lambda i: (i, 0)),
                   pl.BlockSpec((nc, NH, HD, HD), lambda i: (i, 0, 0, 0))),
        scratch_shapes=[pltpu.VMEM((NH, HD, HD), F32)],
        compiler_params=_params(("arbitrary",), VMEM_BIG),
    )(p4, logits, gnorm)


def _mid_fwd(ya, yb, p4, x, wa, wb, wo, g_pm, g_pf, tt):
    t = x.shape[0]

    def body(ya_ref, yb_ref, gt_ref, x_ref, wa_ref, wb_ref, wo_ref, gpm_ref, gpf_ref,
             za_ref, zb_ref, mix_ref, m2_ref, x1_ref, h2_ref):
        za = _dot(ya_ref[...], wa_ref[...])
        zb = _dot(yb_ref[...], wb_ref[...])
        mix = _sig(gt_ref[:, 0:D]) * za + _sig(gt_ref[:, D:2 * D]) * zb
        mixb = mix.astype(BF16)
        m2 = _dot(mixb, wo_ref[...])
        _, n2 = _rms_stats(m2)
        x1 = x_ref[...] + n2 * gpm_ref[...]
        _, n1 = _rms_stats(x1)
        za_ref[...] = za.astype(BF16)
        zb_ref[...] = zb.astype(BF16)
        mix_ref[...] = mixb
        m2_ref[...] = m2
        x1_ref[...] = x1
        h2_ref[...] = (n1 * gpf_ref[...]).astype(BF16)

    row = lambda dt: jax.ShapeDtypeStruct((t, D), dt)
    tile = pl.BlockSpec((tt, D), lambda i: (i, 0))
    wsp = pl.BlockSpec((D, D), lambda i: (0, 0))
    vec = pl.BlockSpec((1, D), lambda i: (0, 0))
    return pl.pallas_call(
        body, name="mid_fwd",
        out_shape=(row(BF16), row(BF16), row(BF16), row(F32), row(F32), row(BF16)),
        grid=(t // tt,),
        in_specs=[tile, tile, pl.BlockSpec((None, tt, 2 * D), lambda i: (SLOT_G, i, 0)), tile,
                  wsp, wsp, wsp, vec, vec],
        out_specs=(tile,) * 6,
        compiler_params=_params(("parallel",), VMEM_BIG),
    )(ya, yb, p4, x, wa, wb, wo, g_pm, g_pf)


def _ffn_act_fwd(up_pre, cfw, cfb, tt):
    t = up_pre.shape[0]

    def body(u_ref, halo_ref, w_ref, b_ref, y_ref, uv_ref, gl_ref, dgl_ref):
        i = pl.program_id(0)
        halves = []
        for c0 in (0, DFF):
            cs = slice(c0, c0 + DFF)
            xe = jnp.concatenate([jnp.where(i > 0, halo_ref[:, cs], 0.0), u_ref[:, cs]], axis=0)
            up = (b_ref[:, cs] + w_ref[2:3, cs] * xe + w_ref[1:2, cs] * pltpu.roll(xe, 1, 0)
                  + w_ref[0:1, cs] * pltpu.roll(xe, 2, 0))
            halves.append(up[8:, :])
        gl, dgl = _gelu_and_grad(halves[0])
        y_ref[...] = (gl * halves[1]).astype(BF16)
        uv_ref[...] = halves[1]
        gl_ref[...] = gl
        dgl_ref[...] = dgl

    hb = tt // 8
    half = pl.BlockSpec((tt, DFF), lambda i: (i, 0))
    return pl.pallas_call(
        body, name="ffn_act_fwd",
        out_shape=(jax.ShapeDtypeStruct((t, DFF), BF16),) + (jax.ShapeDtypeStruct((t, DFF), F32),) * 3,
        grid=(t // tt,),
        in_specs=[pl.BlockSpec((tt, DUP), lambda i: (i, 0)),
                  pl.BlockSpec((8, DUP), lambda i: (jnp.maximum(i * hb - 1, 0), 0)),
                  pl.BlockSpec((3, DUP), lambda i: (0, 0)), pl.BlockSpec((1, DUP), lambda i: (0, 0))],
        out_specs=(half,) * 4,
        compiler_params=_params(("parallel",), VMEM_BIG),
    )(up_pre, up_pre, cfw, cfb)


def _down_loss(y, wdn, x1, tgt, g_post, tt):
    t = x1.shape[0]

    def body(y_ref, w_ref, x1_ref, t_ref, g_ref, dx2_ref, dm3_ref, lossv_ref, dg_ref):
        i = pl.program_id(0)
        m3 = _dot(y_ref[...], w_ref[...])
        r, n3 = _rms_stats(m3)
        g = g_ref[...]
        e = x1_ref[...] + n3 * g - t_ref[...]
        dx2 = e * (1.0 / D)
        dx2_ref[...] = dx2
        dm3_ref[...] = _rms_bwd(dx2 * g, n3, r).astype(BF16)
        lv = jnp.sum(e * e, axis=0, keepdims=True)
        dgv = jnp.sum(dx2 * n3, axis=0, keepdims=True)

        @pl.when(i == 0)
        def _():
            lossv_ref[...] = lv
            dg_ref[...] = dgv

        @pl.when(i > 0)
        def _():
            lossv_ref[...] += lv
            dg_ref[...] += dgv

    tile = pl.BlockSpec((tt, D), lambda i: (i, 0))
    vec = pl.BlockSpec((1, D), lambda i: (0, 0))
    return pl.pallas_call(
        body, name="down_loss",
        out_shape=(jax.ShapeDtypeStruct((t, D), F32), jax.ShapeDtypeStruct((t, D), BF16),
                   jax.ShapeDtypeStruct((1, D), F32), jax.ShapeDtypeStruct((1, D), F32)),
        grid=(t // tt,),
        in_specs=[pl.BlockSpec((tt, DFF), lambda i: (i, 0)), pl.BlockSpec((DFF, D), lambda i: (0, 0)),
                  tile, tile, vec],
        out_specs=(tile, tile, vec, vec),
        compiler_params=_params(("arbitrary",), VMEM_BIG),
    )(y, wdn, x1, tgt, g_post)


def _ffn_act_bwd(dm3, wdn, up_pre, uv, gl, dgl, cfw, tt):
    t = up_pre.shape[0]
    nt = t // tt

    def body(dm_ref, dmn_ref, w_ref, u_ref, uv_ref, uvn_ref, gl_ref, gln_ref, dgl_ref, dgln_ref, cw_ref,
             du_ref, dcw_ref, dcb_ref):
        i = pl.program_id(0)
        n = tt + 8
        next_live = jnp.where(i < nt - 1, 1.0, 0.0)
        dy = jnp.concatenate([_dot_nt(dm_ref[...], w_ref[...]),
                              _dot_nt(dmn_ref[...], w_ref[...])[0:8, :] * next_live], axis=0)
        ext = lambda ref, nref: jnp.concatenate([ref[...], nref[...]], axis=0)
        ds = (dy * ext(uv_ref, uvn_ref) * ext(dgl_ref, dgln_ref), dy * ext(gl_ref, gln_ref))
        dcw_parts, dcb_parts = [], []
        for hh, c0 in enumerate((0, DFF)):
            cs = slice(c0, c0 + DFF)
            dd = ds[hh]
            d1 = pltpu.roll(dd, n - 1, 0)
            d2 = pltpu.roll(dd, n - 2, 0)
            du_ref[:, cs] = (cw_ref[2:3, cs] * dd + cw_ref[1:2, cs] * d1 + cw_ref[0:1, cs] * d2)[0:tt, :].astype(BF16)
            x = u_ref[:, cs]
            dcw_parts.append(jnp.concatenate(
                [jnp.sum(dk[0:tt, :] * x, axis=0, keepdims=True) for dk in (d2, d1, dd)], axis=0))
            dcb_parts.append(jnp.sum(dd[0:tt, :], axis=0, keepdims=True))
        dcw = jnp.concatenate(dcw_parts, axis=1)
        dcb = jnp.concatenate(dcb_parts, axis=1)

        @pl.when(i == 0)
        def _():
            dcw_ref[...] = dcw
            dcb_ref[...] = dcb

        @pl.when(i > 0)
        def _():
            dcw_ref[...] += dcw
            dcb_ref[...] += dcb

    hb = tt // 8
    last8 = t // 8 - 1
    half = pl.BlockSpec((tt, DFF), lambda i: (i, 0))
    half_next = pl.BlockSpec((8, DFF), lambda i: (jnp.minimum((i + 1) * hb, last8), 0))
    return pl.pallas_call(
        body, name="ffn_act_bwd",
        out_shape=(jax.ShapeDtypeStruct((t, DUP), BF16), jax.ShapeDtypeStruct((3, DUP), F32),
                   jax.ShapeDtypeStruct((1, DUP), F32)),
        grid=(nt,),
        in_specs=[pl.BlockSpec((tt, D), lambda i: (i, 0)),
                  pl.BlockSpec((16, D), lambda i: (jnp.minimum((i + 1) * (tt // 16), t // 16 - 1), 0)),
                  pl.BlockSpec((DFF, D), lambda i: (0, 0)),
                  pl.BlockSpec((tt, DUP), lambda i: (i, 0)),
                  half, half_next, half, half_next, half, half_next,
                  pl.BlockSpec((3, DUP), lambda i: (0, 0))],
        out_specs=(pl.BlockSpec((tt, DUP), lambda i: (i, 0)), pl.BlockSpec((3, DUP), lambda i: (0, 0)),
                   pl.BlockSpec((1, DUP), lambda i: (0, 0))),
        compiler_params=_params(("arbitrary",), VMEM_BIG),
    )(dm3, dm3, wdn, up_pre, uv, uv, gl, gl, dgl, dgl, cfw)


def _mid_bwd(dh2, dx2, x1, m2, za, zb, p4, wa, wb, wo, g_pm, g_pf, tt):
    t = x1.shape[0]

    def body(dh2_ref, dx2_ref, x1_ref, m2_ref, za_ref, zb_ref, gt_ref, wa_ref, wb_ref, wo_ref, gpm_ref, gpf_ref,
             dx1_ref, dm2_ref, dza_ref, dzb_ref, dya_ref, dyb_ref, dp_ref, dgpm_ref, dgpf_ref):
        i = pl.program_id(0)
        r1, n1 = _rms_stats(x1_ref[...])
        dh2 = dh2_ref[...]
        dx1 = dx2_ref[...] + _rms_bwd(dh2 * gpf_ref[...], n1, r1)
        r2, n2 = _rms_stats(m2_ref[...])
        dm2 = _rms_bwd(dx1 * gpm_ref[...], n2, r2).astype(BF16)
        dmix = _dot_nt(dm2, wo_ref[...])
        sa = _sig(gt_ref[:, 0:D])
        sb = _sig(gt_ref[:, D:2 * D])
        dza = (dmix * sa).astype(BF16)
        dzb = (dmix * sb).astype(BF16)
        dp_ref[:, 0:D] = (dmix * za_ref[...].astype(F32) * sa * (1.0 - sa)).astype(BF16)
        dp_ref[:, D:2 * D] = (dmix * zb_ref[...].astype(F32) * sb * (1.0 - sb)).astype(BF16)
        dx1_ref[...] = dx1
        dm2_ref[...] = dm2
        dza_ref[...] = dza
        dzb_ref[...] = dzb
        dya_ref[...] = _dot_nt(dza, wa_ref[...])
        dyb_ref[...] = _dot_nt(dzb, wb_ref[...])
        dgpf = jnp.sum(dh2 * n1, axis=0, keepdims=True)
        dgpm = jnp.sum(dx1 * n2, axis=0, keepdims=True)

        @pl.when(i == 0)
        def _():
            dgpf_ref[...] = dgpf
            dgpm_ref[...] = dgpm

        @pl.when(i > 0)
        def _():
            dgpf_ref[...] += dgpf
            dgpm_ref[...] += dgpm

    row = lambda dt: jax.ShapeDtypeStruct((t, D), dt)
    tile = pl.BlockSpec((tt, D), lambda i: (i, 0))
    wsp = pl.BlockSpec((D, D), lambda i: (0, 0))
    vec = pl.BlockSpec((1, D), lambda i: (0, 0))
    gates = pl.BlockSpec((None, tt, 2 * D), lambda i: (SLOT_G, i, 0))
    return pl.pallas_call(
        body, name="mid_bwd",
        out_shape=(row(F32), row(BF16), row(BF16), row(BF16), row(F32), row(F32),
                   jax.ShapeDtypeStruct((NCHIP, t, 2 * D), BF16),
                   jax.ShapeDtypeStruct((1, D), F32), jax.ShapeDtypeStruct((1, D), F32)),
        grid=(t // tt,),
        in_specs=[tile, tile, tile, tile, tile, tile, gates, wsp, wsp, wsp, vec, vec],
        out_specs=(tile, tile, tile, tile, tile, tile, gates, vec, vec),
        compiler_params=_params(("arbitrary",), VMEM_BIG),
    )(dh2, dx2, x1, m2, za, zb, p4, wa, wb, wo, g_pm, g_pf)


def _hgrn2_bwd(p4, o_all, ss, dyb, dp4, logits, gnorm, p_early, tt):
    t = p4.shape[1]
    nt = t // tt
    nc = tt // CH
    ne = len(p_early)

    def body(p_ref, o_ref, ss_ref, dyb_ref, dp_in, lg_ref, gn_ref, *rest):
        del dp_in
        pe = rest[:ne]
        dp_ref, dlb_ref, dgn_ref = rest[ne:ne + 3]
        qe = rest[ne + 3:2 * ne + 3]
        dst, ssem, rsem = rest[2 * ne + 3:]
        i = pl.program_id(0)
        sends, arrive = _exchange_copies(qe, [(lambda chip, r=r: r.at[chip]) for r in pe], ssem, rsem)

        @pl.when(i == 0)
        def _():
            dst[...] = jnp.zeros((NH, HD, HD), F32)
            for cp in sends:
                cp.start()

        low = _chunk_masks(tt)
        lb = _lb_of(lg_ref)
        heads = [slice(h * HD, (h + 1) * HD) for h in range(NH)]
        sg, sgn, f, k, g, eg, eng, egu, qt, kt, kd = _hg_head_fwd(p_ref[0, :, 0:D], p_ref[0, :, D:2 * D], lb)
        qtb, ktb, kdb, vb = qt.astype(BF16), kt.astype(BF16), kd.astype(BF16), p_ref[1, :, 0:D].astype(BF16)
        decs = [jnp.exp(g[c * CH + CH - 1:c * CH + CH, :]) for c in range(nc)]
        og = p_ref[1, :, D:2 * D]
        so = _sig(og)
        dyb = dyb_ref[...]
        dob = dyb * (og * so)
        rn = [_rms_stats(o_ref[:, hs]) for hs in heads]
        r_all = jnp.concatenate([jnp.broadcast_to(r, (tt, HD)) for r, _ in rn], axis=1)
        n_all = jnp.concatenate([n for _, n in rn], axis=1)
        gd = dob * gn_ref[...]
        proj = jnp.concatenate(
            [jnp.broadcast_to(jnp.mean(gd[:, hs] * n_all[:, hs], axis=-1, keepdims=True), (tt, HD)) for hs in heads],
            axis=1)
        dob_ = (r_all * (gd - n_all * proj)).astype(BF16)
        dog = dyb * (n_all * gn_ref[...]) * (so * (1.0 + og * (1.0 - so)))
        dgn = jnp.sum(dob * n_all, axis=0, keepdims=True)
        dv_in, dqt_in, dkt_h = [], [], []
        for hs in heads:
            att = jnp.where(low, _dot_nt(qtb[:, hs], ktb[:, hs]), 0.0).astype(BF16)
            d_att = jnp.where(low, _dot_nt(dob_[:, hs], vb[:, hs]), 0.0).astype(BF16)
            dv_in.append(_dot_tn(att, dob_[:, hs]))
            dqt_in.append(_dot(d_att, ktb[:, hs]))
            dkt_h.append(_dot_tn(d_att, qtb[:, hs]))
        ds_t = [dst[h] for h in range(NH)]
        dv_p = [[None] * NH for _ in range(nc)]
        dqt_p = [[None] * NH for _ in range(nc)]
        dkd_p = [[None] * NH for _ in range(nc)]
        dgl_p = [[None] * NH for _ in range(nc)]
        for c in reversed(range(nc)):
            sl = slice(c * CH, (c + 1) * CH)
            for h, hs in enumerate(heads):
                s_prev = ss_ref[c, h]
                ds_bf = ds_t[h].astype(BF16)
                dec = decs[c][:, hs]
                dv_p[c][h] = dv_in[h][sl] + _dot_nt(kdb[sl, hs], ds_bf)
                dqt_p[c][h] = dqt_in[h][sl] + _dot(dob_[sl, hs], s_prev)
                dkd_p[c][h] = _dot(vb[sl, hs], ds_bf)
                ddec = jnp.sum(s_prev.astype(F32) * ds_t[h], axis=0, keepdims=True)
                dgl_p[c][h] = jnp.broadcast_to(ddec * dec, (CH, HD))
                ds_t[h] = ds_t[h] * dec + _dot_tn(dob_[sl, hs], qtb[sl, hs])
        for h in range(NH):
            dst[h] = ds_t[h]
        whole = lambda parts: jnp.concatenate([jnp.concatenate(row, axis=1) for row in parts], axis=0)
        dv, dqt, dkd, dgl = whole(dv_p), whole(dqt_p), whole(dkd_p), whole(dgl_p)
        dkt = jnp.concatenate(dkt_h, axis=1)
        dq = dqt * eg
        dk = dkt * eng + dkd * egu
        dg = dqt * qt - dkt * kt
        dgu = dkd * kd
        dlogf = _chunk_revcumsum(dg - dgu) + _chunk_total(dgu) + dgl
        common = sgn * (dlogf / f - dk)
        dfz = (1.0 - lb) * sg * common
        dlb = jnp.sum(common, axis=0, keepdims=True)
        dp_ref[0, :, 0:D] = dq.astype(BF16)
        dp_ref[0, :, D:2 * D] = dfz.astype(BF16)
        dp_ref[1, :, 0:D] = dv.astype(BF16)
        dp_ref[1, :, D:2 * D] = dog.astype(BF16)

        @pl.when(i == 0)
        def _():
            dlb_ref[0:1, :] = dlb
            dgn_ref[...] = dgn

        @pl.when(i > 0)
        def _():
            dlb_ref[0:1, :] += dlb
            dgn_ref[...] += dgn

        @pl.when(i == nt - 1)
        def _():
            d0 = dlb_ref[0:1, :] * lb * (1.0 - lb)
            dlb_ref[0:1, :] = d0
            dlb_ref[1:2, :] = -d0
            for cp in arrive:
                cp.wait_recv()
            for cp in sends:
                cp.wait_send()

    rev = lambda i: nt - 1 - i
    vec = pl.BlockSpec((1, D), lambda i: (0, 0))
    any_spec = pl.BlockSpec(memory_space=pl.ANY)
    outs = pl.pallas_call(
        body, name="hgrn2_bwd",
        out_shape=(jax.ShapeDtypeStruct(dp4.shape, BF16), jax.ShapeDtypeStruct((2, D), F32),
                   jax.ShapeDtypeStruct((1, D), F32)) + tuple(jax.ShapeDtypeStruct(a.shape, BF16) for a in p_early),
        grid=(nt,),
        in_specs=[pl.BlockSpec((2, tt, 2 * D), lambda i: (0, rev(i), 0)),
                  pl.BlockSpec((tt, D), lambda i: (rev(i), 0)),
                  pl.BlockSpec((nc, NH, HD, HD), lambda i: (rev(i), 0, 0, 0)),
                  pl.BlockSpec((tt, D), lambda i: (rev(i), 0)),
                  any_spec,
                  pl.BlockSpec((2, D), lambda i: (0, 0)), vec] + [any_spec] * ne,
        out_specs=(pl.BlockSpec((2, tt, 2 * D), lambda i: (0, rev(i), 0)),
                   pl.BlockSpec((2, D), lambda i: (0, 0)), vec) + (any_spec,) * ne,
        scratch_shapes=[pltpu.VMEM((NH, HD, HD), F32), pltpu.SemaphoreType.DMA((3 * ne,)),
                        pltpu.SemaphoreType.DMA((3 * ne,))],
        input_output_aliases={4: 0},
        compiler_params=pltpu.CompilerParams(dimension_semantics=("arbitrary",), vmem_limit_bytes=VMEM_BIG,
                                             has_side_effects=True),
    )(p4, o_all, ss, dyb, dp4, logits, gnorm, *p_early)
    return outs[0], outs[1], outs[2], list(outs[3:])


def _mixer_a_bwd(p4, hseq, dya, dp4, cw, cb, wa, wx, ba, bx, lam, tt):
    t = p4.shape[1]
    nt = t // tt
    steps = tt.bit_length() - 1

    def body(p_ref, ph_ref, h_ref, hh_ref, dya_ref, dp_in, cw_ref, cb_ref, wa_ref, wx_ref, ba_ref, bx_ref, lam_ref,
             dp_ref, dcw_ref, dcb_ref, dwa_ref, dwx_ref, dba_ref, dbx_ref, dlam_ref,
             dnext, dhc, afc):
        del dp_in
        i = pl.program_id(0)
        first_tile = i == nt - 1

        @pl.when(i == 0)
        def _():
            dnext[...] = jnp.zeros((8, D), F32)
            dhc[...] = jnp.zeros((1, D), F32)
            afc[...] = jnp.zeros((1, D), F32)

        xa = p_ref[:, 0:D]
        ga = p_ref[:, D:2 * D]
        xe = jnp.concatenate([jnp.where(first_tile, 0.0, ph_ref[:, 0:D]), xa], axis=0)
        xs = [xe[8:, :]] + [pltpu.roll(xe, s, 0)[8:, :] for s in (1, 2, 3)]
        xc = cb_ref[...] + sum(cw_ref[3 - s:4 - s, :] * xs[s] for s in range(4))
        lam = lam_ref[...]
        xcb, r, ig, sp, a, mult = _lru_gates(xc, wa_ref, wx_ref, ba_ref[...], bx_ref[...], lam)
        h = h_ref[...]
        gl, dgl = _gelu_and_grad(ga)
        dya = dya_ref[...]
        dga = dya * h * dgl
        rows = lax.broadcasted_iota(jnp.int32, (tt, 1), 0)
        a_next = jnp.where(rows == tt - 1, afc[...], pltpu.roll(a, tt - 1, 0))
        dh = _scan_up(a_next, dya * gl, dhc[...])
        dhc[...] = dh[0:1, :]
        afc[...] = a[0:1, :]
        h_prev = jnp.where(rows == 0, jnp.where(first_tile, 0.0, hh_ref[7:8, :]), pltpu.roll(h, 1, 0))
        da = dh * h_prev
        dmult = dh * ig * xc
        di = dh * mult * xc
        dlog_a = da * a - dmult * a * a / mult
        dr = dlog_a * (-LRU_C * sp)
        dsp = jnp.sum(dlog_a * (-LRU_C * r), axis=0, keepdims=True)
        dra = dr * r * (1.0 - r)
        dix = di * ig * (1.0 - ig)
        drab = dra.astype(BF16)
        dixb = dix.astype(BF16)
        dxc_lin = []
        dwa_new = []
        dwx_new = []
        for n in range(NH):
            cs = slice(n * HD, (n + 1) * HD)
            dxc_lin.append(_dot_nt(drab[:, cs], wa_ref[n]) + _dot_nt(dixb[:, cs], wx_ref[n]))
            dwa_new.append(_dot_tn(xcb[:, cs], drab[:, cs]))
            dwx_new.append(_dot_tn(xcb[:, cs], dixb[:, cs]))
        dxc = dh * mult * ig + jnp.concatenate(dxc_lin, axis=1)
        de = jnp.concatenate([dxc, dnext[...]], axis=0)
        dxa = (cw_ref[3:4, :] * de
               + sum(cw_ref[3 - s:4 - s, :] * pltpu.roll(de, tt + 8 - s, 0) for s in (1, 2, 3)))[0:tt, :]
        dnext[...] = dxc[0:8, :]
        dp_ref[:, 0:D] = dxa.astype(BF16)
        dp_ref[:, D:2 * D] = dga.astype(BF16)
        dcw = jnp.concatenate(
            [jnp.sum(dxc * xs[3 - k], axis=0, keepdims=True) for k in range(4)], axis=0)
        dcb = jnp.sum(dxc, axis=0, keepdims=True)
        dba = jnp.sum(dra, axis=0, keepdims=True)
        dbx = jnp.sum(dix, axis=0, keepdims=True)
        dlam = dsp * (-_sig(-lam))

        @pl.when(i == 0)
        def _():
            dcw_ref[...] = dcw
            dcb_ref[...] = dcb
            dba_ref[...] = dba
            dbx_ref[...] = dbx
            dlam_ref[...] = dlam
            for n in range(NH):
                dwa_ref[n] = dwa_new[n]
                dwx_ref[n] = dwx_new[n]

        @pl.when(i > 0)
        def _():
            dcw_ref[...] += dcw
            dcb_ref[...] += dcb
            dba_ref[...] += dba
            dbx_ref[...] += dbx
            dlam_ref[...] += dlam
            for n in range(NH):
                dwa_ref[n] += dwa_new[n]
                dwx_ref[n] += dwx_new[n]

    rev = lambda i: nt - 1 - i
    hb = tt // 8
    full = lambda shape: pl.BlockSpec(shape, lambda i: (0,) * len(shape))
    vecs = jax.ShapeDtypeStruct((1, D), F32)
    blk = jax.ShapeDtypeStruct((NH, HD, HD), F32)
    return pl.pallas_call(
        body, name="mixer_a_bwd",
        out_shape=(jax.ShapeDtypeStruct(dp4.shape, BF16), jax.ShapeDtypeStruct((4, D), F32), vecs, blk, blk,
                   vecs, vecs, vecs),
        grid=(nt,),
        in_specs=[pl.BlockSpec((None, tt, 2 * D), lambda i: (SLOT_A, rev(i), 0)),
                  pl.BlockSpec((None, 8, 2 * D), lambda i: (SLOT_A, jnp.maximum(rev(i) * hb - 1, 0), 0)),
                  pl.BlockSpec((tt, D), lambda i: (rev(i), 0)),
                  pl.BlockSpec((8, D), lambda i: (jnp.maximum(rev(i) * hb - 1, 0), 0)),
                  pl.BlockSpec((tt, D), lambda i: (rev(i), 0)),
                  pl.BlockSpec(memory_space=pl.ANY),
                  full((4, D)), full((1, D)), full((NH, HD, HD)), full((NH, HD, HD)),
                  full((1, D)), full((1, D)), full((1, D))],
        out_specs=(pl.BlockSpec((None, tt, 2 * D), lambda i: (SLOT_A, rev(i), 0)),
                   full((4, D)), full((1, D)), full((NH, HD, HD)), full((NH, HD, HD)),
                   full((1, D)), full((1, D)), full((1, D))),
        scratch_shapes=[pltpu.VMEM((8, D), F32), pltpu.VMEM((1, D), F32), pltpu.VMEM((1, D), F32)],
        input_output_aliases={5: 0},
        compiler_params=_params(("arbitrary",), VMEM_BIG),
    )(p4, p4, hseq, hseq, dya, dp4, cw, cb, wa, wx, ba, bx, lam)


def _norm_bwd(dh1, dx1, x, gain, tt):
    t = x.shape[0]

    def body(dh_ref, dx1_ref, x_ref, g_ref, dx_ref, dg_ref):
        i = pl.program_id(0)
        r, n = _rms_stats(x_ref[...])
        dh = dh_ref[...]
        dx_ref[...] = dx1_ref[...] + _rms_bwd(dh * g_ref[...], n, r)
        dgv = jnp.sum(dh * n, axis=0, keepdims=True)

        @pl.when(i == 0)
        def _():
            dg_ref[...] = dgv

        @pl.when(i > 0)
        def _():
            dg_ref[...] += dgv

    tile = pl.BlockSpec((tt, D), lambda i: (i, 0))
    vec = pl.BlockSpec((1, D), lambda i: (0, 0))
    return pl.pallas_call(
        body, name="norm_bwd",
        out_shape=(jax.ShapeDtypeStruct((t, D), F32), jax.ShapeDtypeStruct((1, D), F32)),
        grid=(t // tt,), in_specs=[tile, tile, tile, vec], out_specs=(tile, vec),
        compiler_params=_params(("arbitrary",)),
    )(dh1, dx1, x, gain)


def _local_step(x, tgt, w_in, stk_rest, conv_a_w, conv_f_w, small, cidx):
    t = x.shape[0]
    tt = min(256, t)
    tm = min(1024, t)
    wa_bf = small["lru_wa"].astype(BF16)
    wx_bf = small["lru_wx"].astype(BF16)

    h1 = _norm_fwd(x, small["norm_pre_mix"], tt)
    p4, stk_rest = _mm_in_gather(h1, w_in, stk_rest, REST, tm)
    w = dict(zip(REST, _gather_forward(stk_rest, REST)))
    w["w_in"] = w_in
    w_br_a = w["w_branch_a"].reshape(D, D)
    w_br_b = w["w_branch_b"].reshape(D, D)
    w_out = w["w_out"].reshape(D, D)
    w_down = w["w_down"].reshape(DFF, D)
    ya, hseq = _mixer_a_fwd(p4, conv_a_w, small["conv_a_b"], wa_bf, wx_bf, small["lru_ba"], small["lru_bx"],
                            small["lru_lambda"], tt)
    yb, o_all, ss = _hgrn2_fwd(p4, small["hg_lb_logits"], small["hg_norm_g"], tt)
    za, zb, mixb, m2, x1, h2 = _mid_fwd(ya, yb, p4, x, w_br_a, w_br_b, w_out, small["norm_post_mix"],
                                        small["norm_pre_ffn"], tt)
    up_pre = _mm_nn_sharded(h2, w["w_up"], F32, tm, "mm_up")
    tf = min(128, t)
    y, uv, gl, dgl = _ffn_act_fwd(up_pre, conv_f_w, small["conv_f_b"], tf)
    dx2, dm3, lossv, d_norm_post_ffn = _down_loss(y, w_down, x1, tgt, small["norm_post_ffn"], tt)

    d_w_down = _mm_tn(y, dm3, DFF // 2, D, tm, "mm_dw_down")
    dup_pre, d_conv_f_w, d_conv_f_b = _ffn_act_bwd(dm3, w_down, up_pre, uv, gl, dgl, conv_f_w, tf)
    d_w_up = _mm_tn(h2, dup_pre, D, SH_UP, tm, "mm_dw_up", stacked_out=True)
    dh2 = _mm_nt_sharded(dup_pre, w["w_up"], tm, "mm_dh2")
    dx1, dm2, dza, dzb, dya, dyb, dp4, d_norm_post_mix, d_norm_pre_ffn = _mid_bwd(
        dh2, dx2, x1, m2, za, zb, p4, w_br_a, w_br_b, w_out, small["norm_post_mix"], small["norm_pre_ffn"], tt)
    d_w_out = _mm_tn(mixb, dm2, D, D, tm, "mm_dw_out")
    d_w_br_a = _mm_tn(ya, dza, D, D, tm, "mm_dw_bra")
    d_w_br_b = _mm_tn(yb, dzb, D, D, tm, "mm_dw_brb")
    early = {"w_branch_a": d_w_br_a.reshape(NCHIP, SH_BR, D), "w_branch_b": d_w_br_b.reshape(NCHIP, SH_BR, D),
             "w_out": d_w_out.reshape(NCHIP, SH_BR, D), "w_up": d_w_up, "w_down": d_w_down.reshape(NCHIP, SH_DN, D)}
    rb, _ = _reduce_stage1(early, REST, (), "reduce_d2d_in_early")
    p_rest = [_sum_own_half(early[n], rb[n], cidx, "sum_half_" + n) for n in REST]
    dp4, d_lb, d_hg_norm_g, q_rest = _hgrn2_bwd(p4, o_all, ss, dyb, dp4, small["hg_lb_logits"], small["hg_norm_g"],
                                                p_rest, tt)
    dp4, d_conv_a_w, d_conv_a_b, d_lru_wa, d_lru_wx, d_lru_ba, d_lru_bx, d_lru_lambda = _mixer_a_bwd(
        p4, hseq, dya, dp4, conv_a_w, small["conv_a_b"], wa_bf, wx_bf, small["lru_ba"], small["lru_bx"],
        small["lru_lambda"], tt)
    d_w_in = _mm_tn(h1, dp4, D, SH_IN, tm, "mm_dw_in", stacked_slot_fn=_slot_of_chip, stacked_out=True)
    dh1 = _mm_nt_sharded(dp4, w["w_in"], tm, "mm_dh1", stacked_slot_fn=_slot_of_chip)
    grad_x, d_norm_pre_mix = _norm_bwd(dh1, dx1, x, small["norm_pre_mix"], tt)

    smalls = {
        "norm_pre_mix": d_norm_pre_mix, "conv_a_b": d_conv_a_b, "lru_ba": d_lru_ba, "lru_bx": d_lru_bx,
        "lru_lambda": d_lru_lambda, "hg_lb_logits": d_lb, "hg_norm_g": d_hg_norm_g, "norm_post_mix": d_norm_post_mix,
        "norm_pre_ffn": d_norm_pre_ffn, "norm_post_ffn": d_norm_post_ffn, "lossv": lossv,
        "conv_a_w": d_conv_a_w, "lru_wa": d_lru_wa, "lru_wx": d_lru_wx,
        "conv_f_b": d_conv_f_b, "conv_f_w": d_conv_f_w,
    }
    return grad_x, d_w_in, dict(zip(REST, p_rest)), dict(zip(REST, q_rest)), smalls


BIG = ("w_in", "w_branch_a", "w_branch_b", "w_out", "w_up", "w_down")
BIG_SHAPE = {"w_in": (D, SH_IN), "w_branch_a": (SH_BR, D), "w_branch_b": (SH_BR, D), "w_out": (SH_BR, D),
             "w_up": (D, SH_UP), "w_down": (SH_DN, D)}
NBIG = len(BIG)
REST = BIG[1:]
VEC_ROWS = (("norm_pre_mix", 0, 1), ("conv_a_b", 1, 1), ("lru_ba", 2, 1), ("lru_bx", 3, 1), ("lru_lambda", 4, 1),
            ("hg_lb_logits", 5, 2), ("hg_norm_g", 7, 1), ("norm_post_mix", 8, 1), ("norm_pre_ffn", 9, 1),
            ("norm_post_ffn", 10, 1))
ROW_LOSS = 11
ROW_CONV_A = 12
S1_ROWS = 16
S2_ROWS = 8


def _place():
    x, y, c = lax.axis_index("x"), lax.axis_index("y"), lax.axis_index("c")
    chips = [(1 - x, y), (x, 1 - y), (1 - x, 1 - y)]
    return x, y, c, 2 * x + y, chips


def _remote(src, dst, ssem, rsem, dev):
    return pltpu.make_async_remote_copy(src_ref=src, dst_ref=dst, send_sem=ssem, recv_sem=rsem,
                                        device_id=dev, device_id_type=MESH)


def _hbm_call(body, name, ins, out_shapes, n_sems, aliases=None):
    any_spec = pl.BlockSpec(memory_space=pl.ANY)
    return pl.pallas_call(
        body, name=name, out_shape=tuple(out_shapes),
        in_specs=[any_spec] * len(ins), out_specs=tuple([any_spec] * len(out_shapes)),
        scratch_shapes=[pltpu.SemaphoreType.DMA((n,)) for n in n_sems],
        input_output_aliases=aliases or {},
        compiler_params=pltpu.CompilerParams(has_side_effects=True),
    )(*ins)


def _gather_copies(stk, names, ssem, rsem, fssem=None, frsem=None):
    x, y, c, j, chips = _place()
    sends, arrive, fwds, farrive = [], [], [], []
    for w, n in enumerate(names):
        hw = BIG_SHAPE[n][0] // 2
        mine = stk[w].at[j, pl.ds(c * hw, hw), :]
        for k, (cx, cy) in enumerate(chips):
            i = 3 * w + k
            got = stk[w].at[2 * cx + cy, pl.ds(c * hw, hw), :]
            other = stk[w].at[2 * cx + cy, pl.ds((1 - c) * hw, hw), :]
            sends.append(_remote(mine, mine, ssem.at[i], rsem.at[i], (cx, cy, c)))
            arrive.append(_remote(got, got, ssem.at[i], rsem.at[i], (cx, cy, c)))
            if fssem is not None:
                fwds.append(_remote(got, got, fssem.at[i], frsem.at[i], (x, y, 1 - c)))
                farrive.append(_remote(other, other, fssem.at[i], frsem.at[i], (x, y, 1 - c)))
    return sends, arrive, fwds, farrive


def _gather_first(stacked_w_in, conv_a_s, conv_f_s):
    ins = [stacked_w_in, conv_a_s, conv_f_s]
    out_shapes = [jax.ShapeDtypeStruct(stacked_w_in.shape, stacked_w_in.dtype)]
    out_shapes += [jax.ShapeDtypeStruct((NCHIP,) + a.shape, a.dtype) for a in (conv_a_s, conv_f_s)]

    def body(w_in, ca_src, cf_src, w_out, ca_dst, cf_dst, ssem, rsem, fssem, frsem, csend, crecv, lsem):
        del w_in
        x, y, c, j, chips = _place()
        conv = ((ca_src, ca_dst), (cf_src, cf_dst))
        locs = [pltpu.make_async_copy(src, dst.at[j], lsem.at[i]) for i, (src, dst) in enumerate(conv)]
        csends = [_remote(src, dst.at[j], csend.at[3 * i + k], crecv.at[3 * i + k], (cx, cy, c))
                  for i, (src, dst) in enumerate(conv) for k, (cx, cy) in enumerate(chips)]
        sends, arrive, fwds, farrive = _gather_copies([w_out], ("w_in",), ssem, rsem, fssem, frsem)
        for cp in locs + sends + csends:
            cp.start()
        for got, fwd in zip(arrive, fwds):
            got.wait_recv()
            fwd.start()
        for i, (_, dst) in enumerate(conv):
            for k, (cx, cy) in enumerate(chips):
                got = dst.at[2 * cx + cy]
                _remote(got, got, csend.at[3 * i + k], crecv.at[3 * i + k], (cx, cy, c)).wait_recv()
        for cp in farrive:
            cp.wait_recv()
        for cp in sends + fwds + csends:
            cp.wait_send()
        for cp in locs:
            cp.wait()

    return _hbm_call(body, "gather_first", ins, out_shapes, (3, 3, 3, 3, 6, 6, 2), aliases={0: 0})


def _gather_forward(stk, names):
    nw = len(names)

    def body(*refs):
        dst = refs[nw:2 * nw]
        ssem, rsem, fssem, frsem = refs[2 * nw:]
        _, _, fwds, farrive = _gather_copies(dst, names, ssem, rsem, fssem, frsem)
        for cp in fwds:
            cp.start()
        for cp in farrive:
            cp.wait_recv()
        for cp in fwds:
            cp.wait_send()

    out_shapes = [jax.ShapeDtypeStruct(a.shape, a.dtype) for a in stk]
    return _hbm_call(body, "gather_forward", stk, out_shapes, (3 * nw,) * 4, aliases={w: w for w in range(nw)})


def _exchange_copies(dst, pieces, ssem, rsem):
    x, y, c, j, chips = _place()
    sends, arrive = [], []
    for w in range(len(dst)):
        for k, (cx, cy) in enumerate(chips):
            i = 3 * w + k
            sends.append(_remote(pieces[w](2 * cx + cy), dst[w].at[j], ssem.at[i], rsem.at[i], (cx, cy, c)))
            got = dst[w].at[2 * cx + cy]
            arrive.append(_remote(got, got, ssem.at[i], rsem.at[i], (cx, cy, c)))
    return sends, arrive


def _reduce_stage1(big_g, names, smalls, name):
    nb = len(names)
    ins = [big_g[n] for n in names] + list(smalls)
    n_in = len(ins)
    halves = [BIG_SHAPE[n][0] // 2 for n in names]
    out_shapes = [jax.ShapeDtypeStruct((NCHIP, halves[w], BIG_SHAPE[n][1]), F32) for w, n in enumerate(names)]
    out_shapes += [jax.ShapeDtypeStruct(a.shape, F32) for a in smalls]

    def body(*refs):
        src, dst = refs[:n_in], refs[n_in:2 * n_in]
        ssem, rsem = refs[2 * n_in:]
        x, y, c, _, _ = _place()
        cps = []
        for w in range(n_in):
            s_ = src[w].at[:, pl.ds((1 - c) * halves[w], halves[w]), :] if w < nb else src[w]
            cp = _remote(s_, dst[w], ssem.at[w], rsem.at[w], (x, y, 1 - c))
            cp.start()
            cps.append(cp)
        for cp in cps:
            cp.wait()

    outs = _hbm_call(body, name, ins, out_shapes, (n_in, n_in))
    return dict(zip(names, outs[:nb])), outs[nb:]


def _reduce_stage2(p_w_in, ps1, ps2, ps3):
    ins = [p_w_in, ps1, ps2, ps3]
    h1, h2, h3 = S1_ROWS // 2, DUP // 2, D
    out_shapes = [jax.ShapeDtypeStruct(p_w_in.shape, BF16), jax.ShapeDtypeStruct((NCHIP, h1, D), F32),
                  jax.ShapeDtypeStruct((NCHIP, S2_ROWS, h2), F32), jax.ShapeDtypeStruct((NCHIP, h3, HD), F32)]

    def body(*refs):
        src, dst = refs[:4], refs[4:8]
        ssem, rsem = refs[8:]
        c = lax.axis_index("c")
        pieces = [lambda chip: src[0].at[chip],
                  lambda chip: src[1].at[pl.ds(c * h1, h1), :],
                  lambda chip: src[2].at[:, pl.ds(c * h2, h2)],
                  lambda chip: src[3].at[pl.ds(c * h3, h3), :]]
        sends, arrive = _exchange_copies(dst, pieces, ssem, rsem)
        for cp in sends:
            cp.start()
        for cp in arrive:
            cp.wait_recv()
        for cp in sends:
            cp.wait_send()

    return _hbm_call(body, "reduce_ici_late", ins, out_shapes, (12, 12))


def _reduce_stage3(f_big, fs1, fs2, fs3):
    ins = [f_big[n] for n in BIG] + [fs1, fs2, fs3]
    n_in = len(ins)
    halves = [BIG_SHAPE[n][0] // 2 for n in BIG]
    h1, h2, h3 = S1_ROWS // 2, DUP // 2, D
    out_shapes = [jax.ShapeDtypeStruct(BIG_SHAPE[n], F32) for n in BIG]
    out_shapes += [jax.ShapeDtypeStruct((S1_ROWS, D), F32), jax.ShapeDtypeStruct((S2_ROWS, DUP), F32),
                   jax.ShapeDtypeStruct((2 * D, HD), F32)]

    def body(*refs):
        dst = refs[n_in:2 * n_in]
        ssem, rsem = refs[2 * n_in:]
        x, y, c, _, _ = _place()

        def place(w, which):
            if w < NBIG:
                return dst[w].at[pl.ds(which * halves[w], halves[w]), :]
            if w == NBIG:
                return dst[w].at[pl.ds(which * h1, h1), :]
            if w == NBIG + 1:
                return dst[w].at[:, pl.ds(which * h2, h2)]
            return dst[w].at[pl.ds(which * h3, h3), :]

        cps = [_remote(place(w, c), place(w, c), ssem.at[w], rsem.at[w], (x, y, 1 - c)) for w in range(n_in)]
        for cp in cps:
            cp.start()
        for w in range(n_in):
            got = place(w, 1 - c)
            _remote(got, got, ssem.at[w], rsem.at[w], (x, y, 1 - c)).wait_recv()
        for cp in cps:
            cp.wait_send()

    outs = _hbm_call(body, "reduce_d2d_out", ins, out_shapes, (n_in, n_in), aliases={w: w for w in range(n_in)})
    return dict(zip(BIG, outs[:NBIG])), outs[NBIG], outs[NBIG + 1], outs[NBIG + 2]


def _row_tile(rows):
    for tr in (128, 176, 64, 16, 8):
        if rows % tr == 0:
            return tr
    return rows


def _sum_own_half(g, rb, cidx, name):
    s, rows, cols = g.shape
    half = rows // 2
    tr = _row_tile(half)
    nb = half // tr

    def body(c_ref, g_ref, r_ref, o_ref):
        del c_ref
        o_ref[...] = (g_ref[...] + r_ref[...]).astype(BF16)

    grid_spec = pltpu.PrefetchScalarGridSpec(
        num_scalar_prefetch=1, grid=(s, nb),
        in_specs=[pl.BlockSpec((None, tr, cols), lambda k, i, c: (k, c[0] * nb + i, 0)),
                  pl.BlockSpec((None, tr, cols), lambda k, i, c: (k, i, 0))],
        out_specs=pl.BlockSpec((None, tr, cols), lambda k, i, c: (k, i, 0)))
    return pl.pallas_call(
        body, name=name, grid_spec=grid_spec, out_shape=jax.ShapeDtypeStruct((s, half, cols), BF16),
        compiler_params=_params(("parallel", "parallel")),
    )(cidx, g, rb)


def _sum_chips(q, p, jc, name, by_cols=False):
    s, rows, cols = q.shape
    tr = _row_tile(rows)
    nb = rows // tr
    stacked = p.ndim == 3

    def body(jc_ref, q_ref, p_ref, o_ref):
        j = jc_ref[0]
        own = p_ref[...].astype(F32)
        acc = None
        for k in range(NCHIP):
            term = jnp.where(j == k, own, q_ref[k].astype(F32))
            acc = term if acc is None else acc + term
        o_ref[...] = acc

    if by_cols:
        half_spec = pl.BlockSpec((tr, cols), lambda i, jc_ref: (i, jc_ref[1]))
        out_shape = jax.ShapeDtypeStruct((rows, 2 * cols), F32)
    else:
        half_spec = pl.BlockSpec((tr, cols), lambda i, jc_ref: (jc_ref[1] * nb + i, 0))
        out_shape = jax.ShapeDtypeStruct((2 * rows, cols), F32)
    p_spec = pl.BlockSpec((None, tr, cols), lambda i, jc_ref: (jc_ref[0], i, 0)) if stacked else half_spec
    grid_spec = pltpu.PrefetchScalarGridSpec(
        num_scalar_prefetch=1, grid=(nb,),
        in_specs=[pl.BlockSpec((s, tr, cols), lambda i, jc_ref: (0, i, 0)), p_spec],
        out_specs=half_spec)
    return pl.pallas_call(
        body, name=name, grid_spec=grid_spec, out_shape=out_shape,
        compiler_params=_params(("parallel",)),
    )(jc, q, p)


def _place_shard(w, jc, name):
    rows, cols = w.shape
    tr = _row_tile(rows)

    def body(jc_ref, w_ref, o_ref):
        del jc_ref
        o_ref[...] = w_ref[...].astype(BF16)

    grid_spec = pltpu.PrefetchScalarGridSpec(
        num_scalar_prefetch=1, grid=(rows // tr,),
        in_specs=[pl.BlockSpec((tr, cols), lambda i, jc_ref: (i, 0))],
        out_specs=pl.BlockSpec((None, tr, cols), lambda i, jc_ref: (jc_ref[0], i, 0)))
    return pl.pallas_call(
        body, name=name, grid_spec=grid_spec, out_shape=jax.ShapeDtypeStruct((NCHIP, rows, cols), BF16),
        compiler_params=_params(("parallel",)),
    )(jc, w)


def _add(a, b, name):
    def body(a_ref, b_ref, o_ref):
        o_ref[...] = a_ref[...] + b_ref[...]

    return pl.pallas_call(body, name=name, out_shape=jax.ShapeDtypeStruct(a.shape, F32))(a, b)


def _pack_small(sm):
    vec_in = [sm[n] for n, _, _ in VEC_ROWS]
    nv = len(vec_in)

    def body(*refs):
        ins, lossv, dcw, dcfb, dcfw, s1, s2 = refs[:nv], refs[nv], refs[nv + 1], refs[nv + 2], refs[nv + 3], \
            refs[nv + 4], refs[nv + 5]
        for ref, (_, r0, nr) in zip(ins, VEC_ROWS):
            s1[r0:r0 + nr, :] = ref[...]
        s1[ROW_LOSS:ROW_LOSS + 1, :] = lossv[...]
        s1[ROW_CONV_A:ROW_CONV_A + 4, :] = dcw[...]
        s2[0:1, :] = dcfb[...]
        s2[1:4, :] = dcfw[...]
        s2[4:8, :] = jnp.zeros((4, DUP), F32)

    return pl.pallas_call(
        body, name="pack_small",
        out_shape=(jax.ShapeDtypeStruct((S1_ROWS, D), F32), jax.ShapeDtypeStruct((S2_ROWS, DUP), F32)),
    )(*vec_in, sm["lossv"], sm["conv_a_w"], sm["conv_f_b"], sm["conv_f_w"])


def _adam_math(w, g, m, v):
    m = ADAM_B1 * m + (1.0 - ADAM_B1) * g
    v = ADAM_B2 * v + (1.0 - ADAM_B2) * (g * g)
    m_hat = m / (1.0 - ADAM_B1 ** ADAM_STEP)
    v_hat = v / (1.0 - ADAM_B2 ** ADAM_STEP)
    delta = -ADAM_LR * (m_hat / (jnp.sqrt(v_hat) + ADAM_EPS) + ADAM_WD * w)
    return delta, m, v


def _adam(w, g, m, v, name):
    rows, cols = w.shape
    tr = _row_tile(rows)

    def body(w_ref, g_ref, m_ref, v_ref, d_ref, mo_ref, vo_ref):
        d_ref[...], mo_ref[...], vo_ref[...] = _adam_math(w_ref[...], g_ref[...], m_ref[...], v_ref[...])

    spec = pl.BlockSpec((tr, cols), lambda i: (i, 0))
    return pl.pallas_call(
        body, name=name, out_shape=(jax.ShapeDtypeStruct(w.shape, F32),) * 3, grid=(rows // tr,),
        in_specs=[spec] * 4, out_specs=(spec,) * 3,
        compiler_params=_params(("parallel",)),
    )(w, g, m, v)


def _adam_small(gs1, gs2, gs3, w, m, v):
    names = [n for n, _, _ in VEC_ROWS] + ["conv_f_b", "lru_wa", "lru_wx"]
    nn = len(names)

    def grad_of(i, g1, g2, g3):
        if i < len(VEC_ROWS):
            _, r0, nr = VEC_ROWS[i]
            return g1[r0:r0 + nr, :]
        if names[i] == "conv_f_b":
            return g2[0:1, :]
        return g3[0] if names[i] == "lru_wa" else g3[1]

    def body(*refs):
        g1, g2, g3 = refs[0], refs[1], refs[2]
        ws, ms, vs = refs[3:3 + nn], refs[3 + nn:3 + 2 * nn], refs[3 + 2 * nn:3 + 3 * nn]
        outs = refs[3 + 3 * nn:]
        for i in range(nn):
            d, mn, vn = _adam_math(ws[i][...], grad_of(i, g1, g2, g3), ms[i][...], vs[i][...])
            outs[i][...] = d
            outs[nn + i][...] = mn
            outs[2 * nn + i][...] = vn

    shapes = [jax.ShapeDtypeStruct(w[n].shape, F32) for n in names]
    outs = pl.pallas_call(body, name="adam_small", out_shape=tuple(shapes * 3))(
        gs1, gs2, gs3, *[w[n] for n in names], *[m[n] for n in names], *[v[n] for n in names])
    return {n: (outs[i], outs[nn + i], outs[2 * nn + i]) for i, n in enumerate(names)}


WEIGHTS = ("norm_pre_mix", "w_in", "conv_a_w", "conv_a_b", "lru_wa", "lru_ba", "lru_wx", "lru_bx", "lru_lambda",
           "hg_lb_logits", "hg_norm_g", "w_branch_a", "w_branch_b", "w_out", "norm_post_mix", "norm_pre_ffn",
           "w_up", "conv_f_w", "conv_f_b", "w_down", "norm_post_ffn")
NW = len(WEIGHTS)


def kernel(x, norm_pre_mix, w_in, conv_a_w, conv_a_b, lru_wa, lru_ba, lru_wx, lru_bx, lru_lambda, hg_lb_logits, hg_norm_g, w_branch_a, w_branch_b, w_out, norm_post_mix, norm_pre_ffn, w_up, conv_f_w, conv_f_b, w_down, norm_post_ffn, loss_target, m_norm_pre_mix, m_w_in, m_conv_a_w, m_conv_a_b, m_lru_wa, m_lru_ba, m_lru_wx, m_lru_bx, m_lru_lambda, m_hg_lb_logits, m_hg_norm_g, m_w_branch_a, m_w_branch_b, m_w_out, m_norm_post_mix, m_norm_pre_ffn, m_w_up, m_conv_f_w, m_conv_f_b, m_w_down, m_norm_post_ffn, v_norm_pre_mix, v_w_in, v_conv_a_w, v_conv_a_b, v_lru_wa, v_lru_ba, v_lru_wx, v_lru_bx, v_lru_lambda, v_hg_lb_logits, v_hg_norm_g, v_w_branch_a, v_w_branch_b, v_w_out, v_norm_post_mix, v_norm_pre_ffn, v_w_up, v_conv_f_w, v_conv_f_b, v_w_down, v_norm_post_ffn):
    rest = (norm_pre_mix, w_in, conv_a_w, conv_a_b, lru_wa, lru_ba, lru_wx, lru_bx, lru_lambda, hg_lb_logits, hg_norm_g, w_branch_a, w_branch_b, w_out, norm_post_mix, norm_pre_ffn, w_up, conv_f_w, conv_f_b, w_down, norm_post_ffn, loss_target, m_norm_pre_mix, m_w_in, m_conv_a_w, m_conv_a_b, m_lru_wa, m_lru_ba, m_lru_wx, m_lru_bx, m_lru_lambda, m_hg_lb_logits, m_hg_norm_g, m_w_branch_a, m_w_branch_b, m_w_out, m_norm_post_mix, m_norm_pre_ffn, m_w_up, m_conv_f_w, m_conv_f_b, m_w_down, m_norm_post_ffn, v_norm_pre_mix, v_w_in, v_conv_a_w, v_conv_a_b, v_lru_wa, v_lru_ba, v_lru_wx, v_lru_bx, v_lru_lambda, v_hg_lb_logits, v_hg_norm_g, v_w_branch_a, v_w_branch_b, v_w_out, v_norm_post_mix, v_norm_pre_ffn, v_w_up, v_conv_f_w, v_conv_f_b, v_w_down, v_norm_post_ffn)
    w_in_args = dict(zip(WEIGHTS, rest[:NW]))
    loss_target = rest[NW]
    m_args = dict(zip(WEIGHTS, rest[NW + 1:2 * NW + 1]))
    v_args = dict(zip(WEIGHTS, rest[2 * NW + 1:3 * NW + 1]))
    shape_of = {n: w_in_args[n].shape for n in WEIGHTS}

    def two_d(n, a):
        if n in BIG:
            return a.reshape(BIG_SHAPE[n])
        if n in ("lru_wa", "lru_wx"):
            return a.reshape(NH, HD, HD)
        return a.reshape(a.shape[-2:])

    w2 = {n: two_d(n, w_in_args[n]) for n in WEIGHTS}
    m2 = {n: two_d(n, m_args[n]) for n in WEIGHTS}
    v2 = {n: two_d(n, v_args[n]) for n in WEIGHTS}

    cidx = lax.axis_index("c").astype(jnp.int32).reshape(1)
    jchip = 2 * lax.axis_index("x") + lax.axis_index("y")

    jc = jnp.stack([jchip, lax.axis_index("c")]).astype(jnp.int32)

    shards = {n: _place_shard(w2[n], jc, "place_" + n) for n in BIG}
    conv_a_s = jnp.pad(w2["conv_a_w"], ((0, 4), (0, 0)))
    conv_f_s = jnp.pad(w2["conv_f_w"], ((0, 5), (0, 0)))
    w_in_full, conv_a_g, conv_f_g = _gather_first(shards["w_in"], conv_a_s, conv_f_s)
    conv_a_full = jnp.transpose(conv_a_g, (1, 0, 2)).reshape(8, D)[0:4]
    conv_f_full = jnp.transpose(conv_f_g, (1, 0, 2)).reshape(8, DUP)[0:3]
    small = {n: w2[n] for n in WEIGHTS if n not in BIG and n not in ("conv_a_w", "conv_f_w")}

    grad_x, d_w_in, p_big, q_big, sm_g = _local_step(
        x[0], loss_target[0], w_in_full, [shards[n] for n in REST], conv_a_full, conv_f_full, small, cidx)

    s1, s2 = _pack_small(sm_g)
    s3 = jnp.concatenate([sm_g["lru_wa"].reshape(D, HD), sm_g["lru_wx"].reshape(D, HD)], axis=0)
    rb, (rs1, rs2, rs3) = _reduce_stage1({"w_in": d_w_in}, ("w_in",), (s1, s2, s3), "reduce_d2d_in_late")
    p_big["w_in"] = _sum_own_half(d_w_in, rb["w_in"], cidx, "sum_half_w_in")
    ps1, ps2, ps3 = _add(s1, rs1, "add_s1"), _add(s2, rs2, "add_s2"), _add(s3, rs3, "add_s3")
    q_big["w_in"], qs1, qs2, qs3 = _reduce_stage2(p_big["w_in"], ps1, ps2, ps3)
    f_big = {n: _sum_chips(q_big[n], p_big[n], jc, "sum_chips_" + n) for n in BIG}
    fs1 = _sum_chips(qs1, ps1, jc, "sum_chips_s1")
    fs2 = _sum_chips(qs2, ps2, jc, "sum_chips_s2", by_cols=True)
    fs3 = _sum_chips(qs3, ps3, jc, "sum_chips_s3")
    g_big, gs1, gs2, gs3 = _reduce_stage3(f_big, fs1, fs2, fs3)

    res = {}
    for n in BIG:
        d, mn, vn = _adam(w2[n], g_big[n], m2[n], v2[n], "adam_" + n)
        res[n] = (g_big[n], d, mn, vn)
    small_res = _adam_small(gs1, gs2, gs3.reshape(2, NH, HD, HD), w2, m2, v2)
    for n, r0, nr in VEC_ROWS:
        res[n] = (gs1[r0:r0 + nr],) + small_res[n]
    res["conv_f_b"] = (gs2[0:1],) + small_res["conv_f_b"]
    res["lru_wa"] = (gs3[0:D].reshape(NH, HD, HD),) + small_res["lru_wa"]
    res["lru_wx"] = (gs3[D:2 * D].reshape(NH, HD, HD),) + small_res["lru_wx"]
    g_ca = lax.dynamic_slice_in_dim(gs1[ROW_CONV_A:ROW_CONV_A + 4], jchip * (D // NCHIP), D // NCHIP, axis=1)
    g_cf = lax.dynamic_slice_in_dim(gs2[1:4], jchip * SH_UP, SH_UP, axis=1)
    res["conv_a_w"] = (g_ca,) + _adam(w2["conv_a_w"], g_ca, m2["conv_a_w"], v2["conv_a_w"], "adam_conv_a_w")
    res["conv_f_w"] = (g_cf,) + _adam(w2["conv_f_w"], g_cf, m2["conv_f_w"], v2["conv_f_w"], "adam_conv_f_w")

    loss = (0.5 / D) * jnp.sum(gs1[ROW_LOSS])
    out = [loss, grad_x.reshape(x.shape)]
    for part in range(4):
        out += [res[n][part].reshape(shape_of[n]) for n in WEIGHTS]
    return tuple(out)
```

```python
import functools

import jax
import jax.numpy as jnp
from jax import lax
from jax.experimental import pallas as pl
from jax.experimental.pallas import tpu as pltpu

F32 = jnp.float32
BF16 = jnp.bfloat16

D = 1024
NH = 8
HD = 128
CH = 32
DFF = 2816
DUP = 2 * DFF
NCHIP = 4
SH_IN = 2 * D
SH_UP = DUP // NCHIP
SH_DN = DFF // NCHIP
SH_BR = D // NCHIP
EPS = 1e-6
LRU_C = 8.0
ADAM_LR = 0.001
ADAM_B1 = 0.9
ADAM_B2 = 0.999
ADAM_EPS = 1e-08
ADAM_WD = 0.01
ADAM_STEP = 10
VMEM_BIG = 56 * 1024 * 1024
MESH = pl.DeviceIdType.MESH

SLOT_A, SLOT_B, SLOT_C, SLOT_G = 2, 0, 1, 3


def _slot_of_chip(s):
    return jnp.where(s == 3, 3, (s + 2) % 3)


def _params(sem, vmem=None):
    return pltpu.CompilerParams(dimension_semantics=sem, vmem_limit_bytes=vmem)


_GC = 0.7978845608028654
_GA = 0.044715


def _gelu(x):
    return 0.5 * x * (1.0 + jnp.tanh(_GC * (x + _GA * x * x * x)))


def _gelu_and_grad(x):
    x2 = x * x
    th = jnp.tanh(_GC * x * (1.0 + _GA * x2))
    g = 0.5 * x * (1.0 + th)
    dg = 0.5 * (1.0 + th) + 0.5 * x * (1.0 - th * th) * _GC * (1.0 + 3.0 * _GA * x2)
    return g, dg


def _sig(x):
    return jax.nn.sigmoid(x)


def _dot(a, b):
    return jnp.dot(a, b, preferred_element_type=F32)


def _dot_nt(a, b):
    return lax.dot_general(a, b, (((1,), (1,)), ((), ())), preferred_element_type=F32)


def _dot_tn(a, b):
    return lax.dot_general(a, b, (((0,), (0,)), ((), ())), preferred_element_type=F32)


def _chunk_cumsum(x):
    pos = lax.broadcasted_iota(jnp.int32, (x.shape[0], 1), 0) & (CH - 1)
    d = 1
    while d < CH:
        x = x + jnp.where(pos >= d, pltpu.roll(x, d, 0), 0.0)
        d *= 2
    return x


def _chunk_revcumsum(x):
    n = x.shape[0]
    pos = lax.broadcasted_iota(jnp.int32, (n, 1), 0) & (CH - 1)
    d = 1
    while d < CH:
        x = x + jnp.where(pos < CH - d, pltpu.roll(x, n - d, 0), 0.0)
        d *= 2
    return x


def _chunk_last(x):
    n = x.shape[0]
    return jnp.concatenate(
        [jnp.broadcast_to(x[c * CH + CH - 1:c * CH + CH, :], (CH, x.shape[1])) for c in range(n // CH)], axis=0)


def _chunk_total(x):
    n = x.shape[0]
    return jnp.concatenate(
        [jnp.broadcast_to(jnp.sum(x[c * CH:(c + 1) * CH, :], axis=0, keepdims=True), (CH, x.shape[1]))
         for c in range(n // CH)], axis=0)


def _rms_stats(x):
    r = lax.rsqrt(jnp.mean(x * x, axis=-1, keepdims=True) + EPS)
    return r, x * r


def _rms_bwd(gd, n, r):
    return r * (gd - n * jnp.mean(gd * n, axis=-1, keepdims=True))


def _shift_rows(x, d, fill):
    rows = lax.broadcasted_iota(jnp.int32, (x.shape[0], 1), 0)
    return jnp.where(rows >= d, pltpu.roll(x, d, 0), fill)


def _scan_down(a, u, carry):
    n = a.shape[0]
    pos = lax.broadcasted_iota(jnp.int32, (n, 1), 0) & 7
    for d in (1, 2, 4):
        u = a * jnp.where(pos >= d, pltpu.roll(u, d, 0), 0.0) + u
        a = a * jnp.where(pos >= d, pltpu.roll(a, d, 0), 1.0)
    out = []
    for v in range(n // 8):
        h = a[v * 8:v * 8 + 8, :] * carry + u[v * 8:v * 8 + 8, :]
        carry = h[7:8, :]
        out.append(h)
    return jnp.concatenate(out, axis=0)


def _scan_up(b, g, carry):
    n = b.shape[0]
    pos = lax.broadcasted_iota(jnp.int32, (n, 1), 0) & 7
    for d in (1, 2, 4):
        g = g + b * jnp.where(pos < 8 - d, pltpu.roll(g, n - d, 0), 0.0)
        b = b * jnp.where(pos < 8 - d, pltpu.roll(b, n - d, 0), 1.0)
    out = [None] * (n // 8)
    for v in reversed(range(n // 8)):
        h = g[v * 8:v * 8 + 8, :] + b[v * 8:v * 8 + 8, :] * carry
        carry = h[0:1, :]
        out[v] = h
    return jnp.concatenate(out, axis=0)


def _shift_rows_up(x, d, fill):
    n = x.shape[0]
    rows = lax.broadcasted_iota(jnp.int32, (n, 1), 0)
    return jnp.where(rows < n - d, pltpu.roll(x, n - d, 0), fill)


def _mm_nn_sharded(a, b3, out_dtype, tm, name, slot_fn=None):
    m, k = a.shape
    s, _, ns = b3.shape

    def body(a_ref, b_ref, o_ref):
        o_ref[...] = _dot(a_ref[...], b_ref[...]).astype(out_dtype)

    if slot_fn is None:
        out_shape = jax.ShapeDtypeStruct((m, s * ns), out_dtype)
        out_spec = pl.BlockSpec((tm, ns), lambda j, i: (i, j))
    else:
        out_shape = jax.ShapeDtypeStruct((s, m, ns), out_dtype)
        out_spec = pl.BlockSpec((None, tm, ns), lambda j, i: (slot_fn(j), i, 0))
    return pl.pallas_call(
        body, name=name, out_shape=out_shape, grid=(s, m // tm),
        in_specs=[pl.BlockSpec((tm, k), lambda j, i: (i, 0)),
                  pl.BlockSpec((None, k, ns), lambda j, i: (j, 0, 0))],
        out_specs=out_spec,
        compiler_params=_params(("parallel", "parallel"), VMEM_BIG),
    )(a, b3)


def _mm_in_gather(a, w_in3, stk_rest, names_rest, tm):
    m, k = a.shape
    s, _, ns = w_in3.shape
    nr = len(stk_rest)
    mt = m // tm

    def body(a_ref, b_ref, *rest):
        o_ref = rest[nr]
        stk = rest[nr + 1:2 * nr + 1]
        ssem, rsem = rest[2 * nr + 1:]
        step = pl.program_id(0) * mt + pl.program_id(1)
        sends, arrive, _, _ = _gather_copies(stk, names_rest, ssem, rsem)

        @pl.when(step == 0)
        def _():
            for cp in sends:
                cp.start()

        o_ref[...] = _dot(a_ref[...], b_ref[...])

        @pl.when(step == s * mt - 1)
        def _():
            for cp in arrive:
                cp.wait_recv()
            for cp in sends:
                cp.wait_send()

    any_spec = pl.BlockSpec(memory_space=pl.ANY)
    outs = pl.pallas_call(
        body, name="mm_in",
        out_shape=(jax.ShapeDtypeStruct((s, m, ns), F32),) + tuple(jax.ShapeDtypeStruct(x.shape, x.dtype) for x in stk_rest),
        grid=(s, mt),
        in_specs=[pl.BlockSpec((tm, k), lambda j, i: (i, 0)), pl.BlockSpec((None, k, ns), lambda j, i: (j, 0, 0))]
        + [any_spec] * nr,
        out_specs=(pl.BlockSpec((None, tm, ns), lambda j, i: (_slot_of_chip(j), i, 0)),) + (any_spec,) * nr,
        scratch_shapes=[pltpu.SemaphoreType.DMA((3 * nr,)), pltpu.SemaphoreType.DMA((3 * nr,))],
        input_output_aliases={2 + w: 1 + w for w in range(nr)},
        compiler_params=pltpu.CompilerParams(dimension_semantics=("arbitrary", "arbitrary"),
                                             vmem_limit_bytes=VMEM_BIG, has_side_effects=True),
    )(a, w_in3, *stk_rest)
    return outs[0], list(outs[1:])


def _mm_dh1_exchange(dp4, w_in3, p_w_in, tm):
    s, k, ns = w_in3.shape
    m = dp4.shape[1]
    mt = m // tm

    def body(a_ref, b_ref, p_ref, o_ref, q_ref, acc_ref, ssem, rsem):
        i, j = pl.program_id(0), pl.program_id(1)
        sends, arrive = _exchange_copies([q_ref], [lambda chip: p_ref.at[chip]], ssem, rsem)

        @pl.when((i == 0) & (j == 0))
        def _():
            for cp in sends:
                cp.start()

        t = _dot_nt(a_ref[...], b_ref[...])

        @pl.when(j == 0)
        def _():
            acc_ref[...] = t

        @pl.when(j > 0)
        def _():
            acc_ref[...] += t

        @pl.when(j == s - 1)
        def _():
            o_ref[...] = acc_ref[...]

        @pl.when((i == mt - 1) & (j == s - 1))
        def _():
            for cp in arrive:
                cp.wait_recv()
            for cp in sends:
                cp.wait_send()

    any_spec = pl.BlockSpec(memory_space=pl.ANY)
    return pl.pallas_call(
        body, name="mm_dh1",
        out_shape=(jax.ShapeDtypeStruct((m, k), F32), jax.ShapeDtypeStruct(p_w_in.shape, BF16)),
        grid=(mt, s),
        in_specs=[pl.BlockSpec((None, tm, ns), lambda i, j: (_slot_of_chip(j), i, 0)),
                  pl.BlockSpec((None, k, ns), lambda i, j: (j, 0, 0)), any_spec],
        out_specs=(pl.BlockSpec((tm, k), lambda i, j: (i, 0)), any_spec),
        scratch_shapes=[pltpu.VMEM((tm, k), F32), pltpu.SemaphoreType.DMA((3,)), pltpu.SemaphoreType.DMA((3,))],
        compiler_params=pltpu.CompilerParams(dimension_semantics=("arbitrary", "arbitrary"),
                                             vmem_limit_bytes=VMEM_BIG, has_side_effects=True),
    )(dp4, w_in3, p_w_in)


def _mm_nt_sharded(a, b3, tm, name, stacked_slot_fn=None):
    s, k, ns = b3.shape
    m = a.shape[1] if stacked_slot_fn is not None else a.shape[0]

    def body(a_ref, b_ref, o_ref, acc_ref):
        j = pl.program_id(1)
        t = _dot_nt(a_ref[...], b_ref[...])

        @pl.when(j == 0)
        def _():
            acc_ref[...] = t

        @pl.when(j > 0)
        def _():
            acc_ref[...] += t

        @pl.when(j == s - 1)
        def _():
            o_ref[...] = acc_ref[...]

    if stacked_slot_fn is None:
        a_spec = pl.BlockSpec((tm, ns), lambda i, j: (i, j))
    else:
        a_spec = pl.BlockSpec((None, tm, ns), lambda i, j: (stacked_slot_fn(j), i, 0))
    return pl.pallas_call(
        body, name=name, out_shape=jax.ShapeDtypeStruct((m, k), F32), grid=(m // tm, s),
        in_specs=[a_spec, pl.BlockSpec((None, k, ns), lambda i, j: (j, 0, 0))],
        out_specs=pl.BlockSpec((tm, k), lambda i, j: (i, 0)),
        scratch_shapes=[pltpu.VMEM((tm, k), F32)],
        compiler_params=_params(("parallel", "arbitrary"), VMEM_BIG),
    )(a, b3)


def _mm_tn(a, g, tkk, tn, tk, name, stacked_slot_fn=None, stacked_out=False):
    m, k = a.shape
    if stacked_slot_fn is not None:
        n = g.shape[0] * g.shape[2]
        g_spec = pl.BlockSpec((None, tk, tn), lambda kk, j, mm: (stacked_slot_fn(j), mm, 0))
    else:
        n = g.shape[1]
        g_spec = pl.BlockSpec((tk, tn), lambda kk, j, mm: (mm, j))

    def body(a_ref, g_ref, o_ref):
        mm = pl.program_id(2)
        t = _dot_tn(a_ref[...], g_ref[...])

        @pl.when(mm == 0)
        def _():
            o_ref[...] = t

        @pl.when(mm > 0)
        def _():
            o_ref[...] += t

    if stacked_out:
        out_shape = jax.ShapeDtypeStruct((n // tn, k, tn), F32)
        out_spec = pl.BlockSpec((None, tkk, tn), lambda kk, j, mm: (j, kk, 0))
    else:
        out_shape = jax.ShapeDtypeStruct((k, n), F32)
        out_spec = pl.BlockSpec((tkk, tn), lambda kk, j, mm: (kk, j))
    return pl.pallas_call(
        body, name=name, out_shape=out_shape, grid=(k // tkk, n // tn, m // tk),
        in_specs=[pl.BlockSpec((tk, tkk), lambda kk, j, mm: (mm, kk)), g_spec],
        out_specs=out_spec,
        compiler_params=_params(("parallel", "parallel", "arbitrary"), VMEM_BIG),
    )(a, g)


def _norm_fwd(x, gain, tt):
    t = x.shape[0]

    def body(x_ref, g_ref, h_ref):
        _, n = _rms_stats(x_ref[...])
        h_ref[...] = (n * g_ref[...]).astype(BF16)

    return pl.pallas_call(
        body, name="norm_fwd", out_shape=jax.ShapeDtypeStruct((t, D), BF16), grid=(t // tt,),
        in_specs=[pl.BlockSpec((tt, D), lambda i: (i, 0)), pl.BlockSpec((1, D), lambda i: (0, 0))],
        out_specs=pl.BlockSpec((tt, D), lambda i: (i, 0)),
        compiler_params=_params(("parallel",)),
    )(x, gain)


def _lru_gates(xc, wa_ref, wx_ref, ba, bx, lam):
    xcb = xc.astype(BF16)
    ra = jnp.concatenate([_dot(xcb[:, n * HD:(n + 1) * HD], wa_ref[n]) for n in range(NH)], axis=1) + ba
    ix = jnp.concatenate([_dot(xcb[:, n * HD:(n + 1) * HD], wx_ref[n]) for n in range(NH)], axis=1) + bx
    r = _sig(ra)
    ig = _sig(ix)
    z = -lam
    sp = jnp.maximum(z, 0.0) + jnp.log1p(jnp.exp(-jnp.abs(z)))
    log_a = -LRU_C * r * sp
    a = jnp.exp(log_a)
    z2 = 2.0 * log_a
    series = -z2 * (1.0 + z2 * (0.5 + z2 * (1.0 / 6.0 + z2 * (1.0 / 24.0))))
    om = jnp.where(z2 > -0.02, series, 1.0 - jnp.exp(z2))
    mult = jnp.sqrt(om)
    return xcb, r, ig, sp, a, mult


def _mixer_a_fwd(p4, cw, cb, wa, wx, ba, bx, lam, tt):
    t = p4.shape[1]

    def body(p_ref, cw_ref, cb_ref, wa_ref, wx_ref, ba_ref, bx_ref, lam_ref, ya_ref, h_ref, halo, hc):
        i = pl.program_id(0)

        @pl.when(i == 0)
        def _():
            halo[...] = jnp.zeros((8, D), F32)
            hc[...] = jnp.zeros((1, D), F32)

        xa = p_ref[:, 0:D]
        ga = p_ref[:, D:2 * D]
        xe = jnp.concatenate([halo[...], xa], axis=0)
        xc = (cb_ref[...] + cw_ref[3:4, :] * xe
              + sum(cw_ref[3 - s:4 - s, :] * pltpu.roll(xe, s, 0) for s in (1, 2, 3)))[8:, :]
        halo[...] = xa[tt - 8:, :]
        _, _, ig, _, a, mult = _lru_gates(xc, wa_ref, wx_ref, ba_ref[...], bx_ref[...], lam_ref[...])
        u = mult * ig * xc
        h = _scan_down(a, u, hc[...])
        hc[...] = h[tt - 1:tt, :]
        h_ref[...] = h
        ya_ref[...] = (h * _gelu(ga)).astype(BF16)

    full = lambda shape: pl.BlockSpec(shape, lambda i: (0,) * len(shape))
    return pl.pallas_call(
        body, name="mixer_a_fwd",
        out_shape=(jax.ShapeDtypeStruct((t, D), BF16), jax.ShapeDtypeStruct((t, D), F32)),
        grid=(t // tt,),
        in_specs=[pl.BlockSpec((None, tt, 2 * D), lambda i: (SLOT_A, i, 0)),
                  full((4, D)), full((1, D)), full((NH, HD, HD)), full((NH, HD, HD)),
                  full((1, D)), full((1, D)), full((1, D))],
        out_specs=(pl.BlockSpec((tt, D), lambda i: (i, 0)), pl.BlockSpec((tt, D), lambda i: (i, 0))),
        scratch_shapes=[pltpu.VMEM((8, D), F32), pltpu.VMEM((1, D), F32)],
        compiler_params=_params(("arbitrary",), VMEM_BIG),
    )(p4, cw, cb, wa, wx, ba, bx, lam)


def _chunk_masks(tt):
    row = lax.broadcasted_iota(jnp.int32, (tt, tt), 0)
    col = lax.broadcasted_iota(jnp.int32, (tt, tt), 1)
    same = jnp.right_shift(row, 5) == jnp.right_shift(col, 5)
    return same & (col <= row)


def _hg_head_fwd(q, fz, lbh):
    sg = _sig(fz)
    sgn = _sig(-fz)
    f = lbh + (1.0 - lbh) * sg
    logf = jnp.log(f)
    k = (1.0 - lbh) * sgn
    g = _chunk_cumsum(logf)
    gu = _chunk_last(g) - g
    eg = jnp.exp(g)
    eng = jnp.exp(-g)
    egu = jnp.exp(gu)
    qt = q * eg
    kt = k * eng
    kd = k * egu
    return sg, sgn, f, k, g, eg, eng, egu, qt, kt, kd


def _lb_of(logits_ref):
    return _sig(logits_ref[0:1, :] - logits_ref[1:2, :])


def _hgrn2_fwd(p4, logits, gnorm, tt):
    t = p4.shape[1]
    nc = tt // CH

    def body(p_ref, lg_ref, gn_ref, yb_ref, o_ref, ss_ref, st):
        i = pl.program_id(0)

        @pl.when(i == 0)
        def _():
            st[...] = jnp.zeros((NH, HD, HD), F32)

        low = _chunk_masks(tt)
        lb = _lb_of(lg_ref)
        heads = [slice(h * HD, (h + 1) * HD) for h in range(NH)]
        _, _, _, _, g, _, _, _, qt, kt, kd = _hg_head_fwd(p_ref[0, :, 0:D], p_ref[0, :, D:2 * D], lb)
        qtb, ktb, kdb, vb = qt.astype(BF16), kt.astype(BF16), kd.astype(BF16), p_ref[1, :, 0:D].astype(BF16)
        decs = [jnp.exp(g[c * CH + CH - 1:c * CH + CH, :]) for c in range(nc)]
        o_in = []
        for hs in heads:
            att = jnp.where(low, _dot_nt(qtb[:, hs], ktb[:, hs]), 0.0)
            o_in.append(_dot(att.astype(BF16), vb[:, hs]))
        s_t = [st[h] for h in range(NH)]
        pieces = [[None] * nc for _ in range(NH)]
        for c in range(nc):
            sl = slice(c * CH, (c + 1) * CH)
            for h, hs in enumerate(heads):
                s_bf = s_t[h].astype(BF16)
                ss_ref[c, h] = s_bf
                pieces[h][c] = o_in[h][sl] + _dot_nt(qtb[sl, hs], s_bf)
                s_t[h] = s_t[h] * decs[c][:, hs] + _dot_tn(vb[sl, hs], kdb[sl, hs])
        for h, hs in enumerate(heads):
            st[h] = s_t[h]
            o = jnp.concatenate(pieces[h], axis=0)
            _, n = _rms_stats(o)
            og = p_ref[1, :, D + h * HD:D + (h + 1) * HD]
            o_ref[:, hs] = o
            yb_ref[:, hs] = (n * gn_ref[:, hs] * (og * _sig(og))).astype(BF16)

    return pl.pallas_call(
        body, name="hgrn2_fwd",
        out_shape=(jax.ShapeDtypeStruct((t, D), BF16), jax.ShapeDtypeStruct((t, D), F32),
                   jax.ShapeDtypeStruct((t // CH, NH, HD, HD), BF16)),
        grid=(t // tt,),
        in_specs=[pl.BlockSpec((2, tt, 2 * D), lambda i: (0, i, 0)),
                  pl.BlockSpec((2, D), lambda i: (0, 0)), pl.BlockSpec((1, D), lambda i: (0, 0))],
        out_specs=(pl.BlockSpec((tt, D), lambda i: (i, 0)), pl.BlockSpec((tt, D), lambda i: (i, 0)),
                   pl.BlockSpec((nc, NH, HD, HD), lambda i: (i, 0, 0, 0))),
        scratch_shapes=[pltpu.VMEM((NH, HD, HD), F32)],
        compiler_params=_params(("arbitrary",), VMEM_BIG),
    )(p4, logits, gnorm)


def _mid_fwd(ya, yb, p4, x, wa, wb, wo, g_pm, g_pf, tt):
    t = x.shape[0]

    def body(ya_ref, yb_ref, gt_ref, x_ref, wa_ref, wb_ref, wo_ref, gpm_ref, gpf_ref,
             za_ref, zb_ref, mix_ref, m2_ref, x1_ref, h2_ref):
        za = _dot(ya_ref[...], wa_ref[...])
        zb = _dot(yb_ref[...], wb_ref[...])
        mix = _sig(gt_ref[:, 0:D]) * za + _sig(gt_ref[:, D:2 * D]) * zb
        mixb = mix.astype(BF16)
        m2 = _dot(mixb, wo_ref[...])
        _, n2 = _rms_stats(m2)
        x1 = x_ref[...] + n2 * gpm_ref[...]
        _, n1 = _rms_stats(x1)
        za_ref[...] = za.astype(BF16)
        zb_ref[...] = zb.astype(BF16)
        mix_ref[...] = mixb
        m2_ref[...] = m2
        x1_ref[...] = x1
        h2_ref[...] = (n1 * gpf_ref[...]).astype(BF16)

    row = lambda dt: jax.ShapeDtypeStruct((t, D), dt)
    tile = pl.BlockSpec((tt, D), lambda i: (i, 0))
    wsp = pl.BlockSpec((D, D), lambda i: (0, 0))
    vec = pl.BlockSpec((1, D), lambda i: (0, 0))
    return pl.pallas_call(
        body, name="mid_fwd",
        out_shape=(row(BF16), row(BF16), row(BF16), row(F32), row(F32), row(BF16)),
        grid=(t // tt,),
        in_specs=[tile, tile, pl.BlockSpec((None, tt, 2 * D), lambda i: (SLOT_G, i, 0)), tile,
                  wsp, wsp, wsp, vec, vec],
        out_specs=(tile,) * 6,
        compiler_params=_params(("parallel",), VMEM_BIG),
    )(ya, yb, p4, x, wa, wb, wo, g_pm, g_pf)


def _ffn_act_fwd(up_pre, cfw, cfb, tt):
    t = up_pre.shape[0]

    def body(u_ref, halo_ref, w_ref, b_ref, y_ref, uv_ref, gl_ref, dgl_ref):
        i = pl.program_id(0)
        halves = []
        for c0 in (0, DFF):
            cs = slice(c0, c0 + DFF)
            xe = jnp.concatenate([jnp.where(i > 0, halo_ref[:, cs], 0.0), u_ref[:, cs]], axis=0)
            up = (b_ref[:, cs] + w_ref[2:3, cs] * xe + w_ref[1:2, cs] * pltpu.roll(xe, 1, 0)
                  + w_ref[0:1, cs] * pltpu.roll(xe, 2, 0))
            halves.append(up[8:, :])
        gl, dgl = _gelu_and_grad(halves[0])
        y_ref[...] = (gl * halves[1]).astype(BF16)
        uv_ref[...] = halves[1].astype(BF16)
        gl_ref[...] = gl.astype(BF16)
        dgl_ref[...] = dgl.astype(BF16)

    hb = tt // 8
    half = pl.BlockSpec((tt, DFF), lambda i: (i, 0))
    return pl.pallas_call(
        body, name="ffn_act_fwd",
        out_shape=(jax.ShapeDtypeStruct((t, DFF), BF16),) * 4,
        grid=(t // tt,),
        in_specs=[pl.BlockSpec((tt, DUP), lambda i: (i, 0)),
                  pl.BlockSpec((8, DUP), lambda i: (jnp.maximum(i * hb - 1, 0), 0)),
                  pl.BlockSpec((3, DUP), lambda i: (0, 0)), pl.BlockSpec((1, DUP), lambda i: (0, 0))],
        out_specs=(half,) * 4,
        compiler_params=_params(("parallel",), VMEM_BIG),
    )(up_pre, up_pre, cfw, cfb)


def _down_loss(y, wdn, x1, tgt, g_post, tt):
    t = x1.shape[0]

    def body(y_ref, w_ref, x1_ref, t_ref, g_ref, dx2_ref, dm3_ref, lossv_ref, dg_ref):
        i = pl.program_id(0)
        m3 = _dot(y_ref[...], w_ref[...])
        r, n3 = _rms_stats(m3)
        g = g_ref[...]
        e = x1_ref[...] + n3 * g - t_ref[...]
        dx2 = e * (1.0 / D)
        dx2_ref[...] = dx2
        dm3_ref[...] = _rms_bwd(dx2 * g, n3, r).astype(BF16)
        lv = jnp.sum(e * e, axis=0, keepdims=True)
        dgv = jnp.sum(dx2 * n3, axis=0, keepdims=True)

        @pl.when(i == 0)
        def _():
            lossv_ref[...] = lv
            dg_ref[...] = dgv

        @pl.when(i > 0)
        def _():
            lossv_ref[...] += lv
            dg_ref[...] += dgv

    tile = pl.BlockSpec((tt, D), lambda i: (i, 0))
    vec = pl.BlockSpec((1, D), lambda i: (0, 0))
    return pl.pallas_call(
        body, name="down_loss",
        out_shape=(jax.ShapeDtypeStruct((t, D), F32), jax.ShapeDtypeStruct((t, D), BF16),
                   jax.ShapeDtypeStruct((1, D), F32), jax.ShapeDtypeStruct((1, D), F32)),
        grid=(t // tt,),
        in_specs=[pl.BlockSpec((tt, DFF), lambda i: (i, 0)), pl.BlockSpec((DFF, D), lambda i: (0, 0)),
                  tile, tile, vec],
        out_specs=(tile, tile, vec, vec),
        compiler_params=_params(("arbitrary",), VMEM_BIG),
    )(y, wdn, x1, tgt, g_post)


def _ffn_act_bwd(dm3, wdn, up_pre, uv, gl, dgl, cfw, tt):
    t = up_pre.shape[0]
    nt = t // tt

    def body(dm_ref, dmn_ref, w_ref, u_ref, uv_ref, uvn_ref, gl_ref, gln_ref, dgl_ref, dgln_ref, cw_ref,
             du_ref, dcw_ref, dcb_ref):
        i = pl.program_id(0)
        n = tt + 8
        next_live = jnp.where(i < nt - 1, 1.0, 0.0)
        dy = jnp.concatenate([_dot_nt(dm_ref[...], w_ref[...]),
                              _dot_nt(dmn_ref[...], w_ref[...])[0:8, :] * next_live], axis=0)
        ext = lambda ref, nref: jnp.concatenate([ref[...].astype(F32), nref[...].astype(F32)[0:8, :]], axis=0)
        ds = (dy * ext(uv_ref, uvn_ref) * ext(dgl_ref, dgln_ref), dy * ext(gl_ref, gln_ref))
        dcw_parts, dcb_parts = [], []
        for hh, c0 in enumerate((0, DFF)):
            cs = slice(c0, c0 + DFF)
            dd = ds[hh]
            d1 = pltpu.roll(dd, n - 1, 0)
            d2 = pltpu.roll(dd, n - 2, 0)
            du_ref[:, cs] = (cw_ref[2:3, cs] * dd + cw_ref[1:2, cs] * d1 + cw_ref[0:1, cs] * d2)[0:tt, :].astype(BF16)
            x = u_ref[:, cs]
            dcw_parts.append(jnp.concatenate(
                [jnp.sum(dk[0:tt, :] * x, axis=0, keepdims=True) for dk in (d2, d1, dd)], axis=0))
            dcb_parts.append(jnp.sum(dd[0:tt, :], axis=0, keepdims=True))
        dcw = jnp.concatenate(dcw_parts, axis=1)
        dcb = jnp.concatenate(dcb_parts, axis=1)

        @pl.when(i == 0)
        def _():
            dcw_ref[...] = dcw
            dcb_ref[...] = dcb

        @pl.when(i > 0)
        def _():
            dcw_ref[...] += dcw
            dcb_ref[...] += dcb

    half = pl.BlockSpec((tt, DFF), lambda i: (i, 0))
    half_next = pl.BlockSpec((16, DFF), lambda i: (jnp.minimum((i + 1) * (tt // 16), t // 16 - 1), 0))
    return pl.pallas_call(
        body, name="ffn_act_bwd",
        out_shape=(jax.ShapeDtypeStruct((t, DUP), BF16), jax.ShapeDtypeStruct((3, DUP), F32),
                   jax.ShapeDtypeStruct((1, DUP), F32)),
        grid=(nt,),
        in_specs=[pl.BlockSpec((tt, D), lambda i: (i, 0)),
                  pl.BlockSpec((16, D), lambda i: (jnp.minimum((i + 1) * (tt // 16), t // 16 - 1), 0)),
                  pl.BlockSpec((DFF, D), lambda i: (0, 0)),
                  pl.BlockSpec((tt, DUP), lambda i: (i, 0)),
                  half, half_next, half, half_next, half, half_next,
                  pl.BlockSpec((3, DUP), lambda i: (0, 0))],
        out_specs=(pl.BlockSpec((tt, DUP), lambda i: (i, 0)), pl.BlockSpec((3, DUP), lambda i: (0, 0)),
                   pl.BlockSpec((1, DUP), lambda i: (0, 0))),
        compiler_params=_params(("arbitrary",), VMEM_BIG),
    )(dm3, dm3, wdn, up_pre, uv, uv, gl, gl, dgl, dgl, cfw)


def _mid_bwd(dh2, dx2, x1, m2, za, zb, p4, wa, wb, wo, g_pm, g_pf, tt):
    t = x1.shape[0]

    def body(dh2_ref, dx2_ref, x1_ref, m2_ref, za_ref, zb_ref, gt_ref, wa_ref, wb_ref, wo_ref, gpm_ref, gpf_ref,
             dx1_ref, dm2_ref, dza_ref, dzb_ref, dya_ref, dyb_ref, dp_ref, dgpm_ref, dgpf_ref):
        i = pl.program_id(0)
        r1, n1 = _rms_stats(x1_ref[...])
        dh2 = dh2_ref[...]
        dx1 = dx2_ref[...] + _rms_bwd(dh2 * gpf_ref[...], n1, r1)
        r2, n2 = _rms_stats(m2_ref[...])
        dm2 = _rms_bwd(dx1 * gpm_ref[...], n2, r2).astype(BF16)
        dmix = _dot_nt(dm2, wo_ref[...])
        sa = _sig(gt_ref[:, 0:D])
        sb = _sig(gt_ref[:, D:2 * D])
        dza = (dmix * sa).astype(BF16)
        dzb = (dmix * sb).astype(BF16)
        dp_ref[:, 0:D] = (dmix * za_ref[...].astype(F32) * sa * (1.0 - sa)).astype(BF16)
        dp_ref[:, D:2 * D] = (dmix * zb_ref[...].astype(F32) * sb * (1.0 - sb)).astype(BF16)
        dx1_ref[...] = dx1
        dm2_ref[...] = dm2
        dza_ref[...] = dza
        dzb_ref[...] = dzb
        dya_ref[...] = _dot_nt(dza, wa_ref[...])
        dyb_ref[...] = _dot_nt(dzb, wb_ref[...])
        dgpf = jnp.sum(dh2 * n1, axis=0, keepdims=True)
        dgpm = jnp.sum(dx1 * n2, axis=0, keepdims=True)

        @pl.when(i == 0)
        def _():
            dgpf_ref[...] = dgpf
            dgpm_ref[...] = dgpm

        @pl.when(i > 0)
        def _():
            dgpf_ref[...] += dgpf
            dgpm_ref[...] += dgpm

    row = lambda dt: jax.ShapeDtypeStruct((t, D), dt)
    tile = pl.BlockSpec((tt, D), lambda i: (i, 0))
    wsp = pl.BlockSpec((D, D), lambda i: (0, 0))
    vec = pl.BlockSpec((1, D), lambda i: (0, 0))
    gates = pl.BlockSpec((None, tt, 2 * D), lambda i: (SLOT_G, i, 0))
    return pl.pallas_call(
        body, name="mid_bwd",
        out_shape=(row(F32), row(BF16), row(BF16), row(BF16), row(F32), row(F32),
                   jax.ShapeDtypeStruct((NCHIP, t, 2 * D), BF16),
                   jax.ShapeDtypeStruct((1, D), F32), jax.ShapeDtypeStruct((1, D), F32)),
        grid=(t // tt,),
        in_specs=[tile, tile, tile, tile, tile, tile, gates, wsp, wsp, wsp, vec, vec],
        out_specs=(tile, tile, tile, tile, tile, tile, gates, vec, vec),
        compiler_params=_params(("arbitrary",), VMEM_BIG),
    )(dh2, dx2, x1, m2, za, zb, p4, wa, wb, wo, g_pm, g_pf)


def _hgrn2_bwd(p4, o_all, ss, dyb, dp4, logits, gnorm, p_early, tt):
    t = p4.shape[1]
    nt = t // tt
    nc = tt // CH
    ne = len(p_early)

    def body(p_ref, o_ref, ss_ref, dyb_ref, dp_in, lg_ref, gn_ref, *rest):
        del dp_in
        pe = rest[:ne]
        dp_ref, dlb_ref, dgn_ref = rest[ne:ne + 3]
        qe = rest[ne + 3:2 * ne + 3]
        dst, ssem, rsem = rest[2 * ne + 3:]
        i = pl.program_id(0)
        sends, arrive = _exchange_copies(qe, [(lambda chip, r=r: r.at[chip]) for r in pe], ssem, rsem)

        @pl.when(i == 0)
        def _():
            dst[...] = jnp.zeros((NH, HD, HD), F32)
            for cp in sends:
                cp.start()

        low = _chunk_masks(tt)
        lb = _lb_of(lg_ref)
        heads = [slice(h * HD, (h + 1) * HD) for h in range(NH)]
        sg, sgn, f, k, g, eg, eng, egu, qt, kt, kd = _hg_head_fwd(p_ref[0, :, 0:D], p_ref[0, :, D:2 * D], lb)
        qtb, ktb, kdb, vb = qt.astype(BF16), kt.astype(BF16), kd.astype(BF16), p_ref[1, :, 0:D].astype(BF16)
        decs = [jnp.exp(g[c * CH + CH - 1:c * CH + CH, :]) for c in range(nc)]
        og = p_ref[1, :, D:2 * D]
        so = _sig(og)
        dyb = dyb_ref[...]
        dob = dyb * (og * so)
        rn = [_rms_stats(o_ref[:, hs]) for hs in heads]
        r_all = jnp.concatenate([jnp.broadcast_to(r, (tt, HD)) for r, _ in rn], axis=1)
        n_all = jnp.concatenate([n for _, n in rn], axis=1)
        gd = dob * gn_ref[...]
        proj = jnp.concatenate(
            [jnp.broadcast_to(jnp.mean(gd[:, hs] * n_all[:, hs], axis=-1, keepdims=True), (tt, HD)) for hs in heads],
            axis=1)
        dob_ = (r_all * (gd - n_all * proj)).astype(BF16)
        dog = dyb * (n_all * gn_ref[...]) * (so * (1.0 + og * (1.0 - so)))
        dgn = jnp.sum(dob * n_all, axis=0, keepdims=True)
        dv_in, dqt_in, dkt_h = [], [], []
        for hs in heads:
            att = jnp.where(low, _dot_nt(qtb[:, hs], ktb[:, hs]), 0.0).astype(BF16)
            d_att = jnp.where(low, _dot_nt(dob_[:, hs], vb[:, hs]), 0.0).astype(BF16)
            dv_in.append(_dot_tn(att, dob_[:, hs]))
            dqt_in.append(_dot(d_att, ktb[:, hs]))
            dkt_h.append(_dot_tn(d_att, qtb[:, hs]))
        ds_t = [dst[h] for h in range(NH)]
        dv_p = [[None] * NH for _ in range(nc)]
        dqt_p = [[None] * NH for _ in range(nc)]
        dkd_p = [[None] * NH for _ in range(nc)]
        dgl_p = [[None] * NH for _ in range(nc)]
        for c in reversed(range(nc)):
            sl = slice(c * CH, (c + 1) * CH)
            for h, hs in enumerate(heads):
                s_prev = ss_ref[c, h]
                ds_bf = ds_t[h].astype(BF16)
                dec = decs[c][:, hs]
                dv_p[c][h] = dv_in[h][sl] + _dot_nt(kdb[sl, hs], ds_bf)
                dqt_p[c][h] = dqt_in[h][sl] + _dot(dob_[sl, hs], s_prev)
                dkd_p[c][h] = _dot(vb[sl, hs], ds_bf)
                ddec = jnp.sum(s_prev.astype(F32) * ds_t[h], axis=0, keepdims=True)
                dgl_p[c][h] = jnp.broadcast_to(ddec * dec, (CH, HD))
                ds_t[h] = ds_t[h] * dec + _dot_tn(dob_[sl, hs], qtb[sl, hs])
        for h in range(NH):
            dst[h] = ds_t[h]
        whole = lambda parts: jnp.concatenate([jnp.concatenate(row, axis=1) for row in parts], axis=0)
        dv, dqt, dkd, dgl = whole(dv_p), whole(dqt_p), whole(dkd_p), whole(dgl_p)
        dkt = jnp.concatenate(dkt_h, axis=1)
        dq = dqt * eg
        dk = dkt * eng + dkd * egu
        dg = dqt * qt - dkt * kt
        dgu = dkd * kd
        dlogf = _chunk_revcumsum(dg - dgu) + _chunk_total(dgu) + dgl
        common = sgn * (dlogf / f - dk)
        dfz = (1.0 - lb) * sg * common
        dlb = jnp.sum(common, axis=0, keepdims=True)
        dp_ref[0, :, 0:D] = dq.astype(BF16)
        dp_ref[0, :, D:2 * D] = dfz.astype(BF16)
        dp_ref[1, :, 0:D] = dv.astype(BF16)
        dp_ref[1, :, D:2 * D] = dog.astype(BF16)

        @pl.when(i == 0)
        def _():
            dlb_ref[0:1, :] = dlb
            dgn_ref[...] = dgn

        @pl.when(i > 0)
        def _():
            dlb_ref[0:1, :] += dlb
            dgn_ref[...] += dgn

        @pl.when(i == nt - 1)
        def _():
            d0 = dlb_ref[0:1, :] * lb * (1.0 - lb)
            dlb_ref[0:1, :] = d0
            dlb_ref[1:2, :] = -d0
            for cp in arrive:
                cp.wait_recv()
            for cp in sends:
                cp.wait_send()

    rev = lambda i: nt - 1 - i
    vec = pl.BlockSpec((1, D), lambda i: (0, 0))
    any_spec = pl.BlockSpec(memory_space=pl.ANY)
    outs = pl.pallas_call(
        body, name="hgrn2_bwd",
        out_shape=(jax.ShapeDtypeStruct(dp4.shape, BF16), jax.ShapeDtypeStruct((2, D), F32),
                   jax.ShapeDtypeStruct((1, D), F32)) + tuple(jax.ShapeDtypeStruct(a.shape, BF16) for a in p_early),
        grid=(nt,),
        in_specs=[pl.BlockSpec((2, tt, 2 * D), lambda i: (0, rev(i), 0)),
                  pl.BlockSpec((tt, D), lambda i: (rev(i), 0)),
                  pl.BlockSpec((nc, NH, HD, HD), lambda i: (rev(i), 0, 0, 0)),
                  pl.BlockSpec((tt, D), lambda i: (rev(i), 0)),
                  any_spec,
                  pl.BlockSpec((2, D), lambda i: (0, 0)), vec] + [any_spec] * ne,
        out_specs=(pl.BlockSpec((2, tt, 2 * D), lambda i: (0, rev(i), 0)),
                   pl.BlockSpec((2, D), lambda i: (0, 0)), vec) + (any_spec,) * ne,
        scratch_shapes=[pltpu.VMEM((NH, HD, HD), F32), pltpu.SemaphoreType.DMA((3 * ne,)),
                        pltpu.SemaphoreType.DMA((3 * ne,))],
        input_output_aliases={4: 0},
        compiler_params=pltpu.CompilerParams(dimension_semantics=("arbitrary",), vmem_limit_bytes=VMEM_BIG,
                                             has_side_effects=True),
    )(p4, o_all, ss, dyb, dp4, logits, gnorm, *p_early)
    return outs[0], outs[1], outs[2], list(outs[3:])


def _mixer_a_bwd(p4, hseq, dya, dp4, cw, cb, wa, wx, ba, bx, lam, tt):
    t = p4.shape[1]
    nt = t // tt
    steps = tt.bit_length() - 1

    def body(p_ref, ph_ref, h_ref, hh_ref, dya_ref, dp_in, cw_ref, cb_ref, wa_ref, wx_ref, ba_ref, bx_ref, lam_ref,
             dp_ref, dcw_ref, dcb_ref, dwa_ref, dwx_ref, dba_ref, dbx_ref, dlam_ref,
             dnext, dhc, afc):
        del dp_in
        i = pl.program_id(0)
        first_tile = i == nt - 1

        @pl.when(i == 0)
        def _():
            dnext[...] = jnp.zeros((8, D), F32)
            dhc[...] = jnp.zeros((1, D), F32)
            afc[...] = jnp.zeros((1, D), F32)

        xa = p_ref[:, 0:D]
        ga = p_ref[:, D:2 * D]
        xe = jnp.concatenate([jnp.where(first_tile, 0.0, ph_ref[:, 0:D]), xa], axis=0)
        xs = [xe[8:, :]] + [pltpu.roll(xe, s, 0)[8:, :] for s in (1, 2, 3)]
        xc = cb_ref[...] + sum(cw_ref[3 - s:4 - s, :] * xs[s] for s in range(4))
        lam = lam_ref[...]
        xcb, r, ig, sp, a, mult = _lru_gates(xc, wa_ref, wx_ref, ba_ref[...], bx_ref[...], lam)
        h = h_ref[...]
        gl, dgl = _gelu_and_grad(ga)
        dya = dya_ref[...]
        dga = dya * h * dgl
        rows = lax.broadcasted_iota(jnp.int32, (tt, 1), 0)
        a_next = jnp.where(rows == tt - 1, afc[...], pltpu.roll(a, tt - 1, 0))
        dh = _scan_up(a_next, dya * gl, dhc[...])
        dhc[...] = dh[0:1, :]
        afc[...] = a[0:1, :]
        h_prev = jnp.where(rows == 0, jnp.where(first_tile, 0.0, hh_ref[7:8, :]), pltpu.roll(h, 1, 0))
        da = dh * h_prev
        dmult = dh * ig * xc
        di = dh * mult * xc
        dlog_a = da * a - dmult * a * a / mult
        dr = dlog_a * (-LRU_C * sp)
        dsp = jnp.sum(dlog_a * (-LRU_C * r), axis=0, keepdims=True)
        dra = dr * r * (1.0 - r)
        dix = di * ig * (1.0 - ig)
        drab = dra.astype(BF16)
        dixb = dix.astype(BF16)
        dxc_lin = []
        dwa_new = []
        dwx_new = []
        for n in range(NH):
            cs = slice(n * HD, (n + 1) * HD)
            dxc_lin.append(_dot_nt(drab[:, cs], wa_ref[n]) + _dot_nt(dixb[:, cs], wx_ref[n]))
            dwa_new.append(_dot_tn(xcb[:, cs], drab[:, cs]))
            dwx_new.append(_dot_tn(xcb[:, cs], dixb[:, cs]))
        dxc = dh * mult * ig + jnp.concatenate(dxc_lin, axis=1)
        de = jnp.concatenate([dxc, dnext[...]], axis=0)
        dxa = (cw_ref[3:4, :] * de
               + sum(cw_ref[3 - s:4 - s, :] * pltpu.roll(de, tt + 8 - s, 0) for s in (1, 2, 3)))[0:tt, :]
        dnext[...] = dxc[0:8, :]
        dp_ref[:, 0:D] = dxa.astype(BF16)
        dp_ref[:, D:2 * D] = dga.astype(BF16)
        dcw = jnp.concatenate(
            [jnp.sum(dxc * xs[3 - k], axis=0, keepdims=True) for k in range(4)], axis=0)
        dcb = jnp.sum(dxc, axis=0, keepdims=True)
        dba = jnp.sum(dra, axis=0, keepdims=True)
        dbx = jnp.sum(dix, axis=0, keepdims=True)
        dlam = dsp * (-_sig(-lam))

        @pl.when(i == 0)
        def _():
            dcw_ref[...] = dcw
            dcb_ref[...] = dcb
            dba_ref[...] = dba
            dbx_ref[...] = dbx
            dlam_ref[...] = dlam
            for n in range(NH):
                dwa_ref[n] = dwa_new[n]
                dwx_ref[n] = dwx_new[n]

        @pl.when(i > 0)
        def _():
            dcw_ref[...] += dcw
            dcb_ref[...] += dcb
            dba_ref[...] += dba
            dbx_ref[...] += dbx
            dlam_ref[...] += dlam
            for n in range(NH):
                dwa_ref[n] += dwa_new[n]
                dwx_ref[n] += dwx_new[n]

    rev = lambda i: nt - 1 - i
    hb = tt // 8
    full = lambda shape: pl.BlockSpec(shape, lambda i: (0,) * len(shape))
    vecs = jax.ShapeDtypeStruct((1, D), F32)
    blk = jax.ShapeDtypeStruct((NH, HD, HD), F32)
    return pl.pallas_call(
        body, name="mixer_a_bwd",
        out_shape=(jax.ShapeDtypeStruct(dp4.shape, BF16), jax.ShapeDtypeStruct((4, D), F32), vecs, blk, blk,
                   vecs, vecs, vecs),
        grid=(nt,),
        in_specs=[pl.BlockSpec((None, tt, 2 * D), lambda i: (SLOT_A, rev(i), 0)),
                  pl.BlockSpec((None, 8, 2 * D), lambda i: (SLOT_A, jnp.maximum(rev(i) * hb - 1, 0), 0)),
                  pl.BlockSpec((tt, D), lambda i: (rev(i), 0)),
                  pl.BlockSpec((8, D), lambda i: (jnp.maximum(rev(i) * hb - 1, 0), 0)),
                  pl.BlockSpec((tt, D), lambda i: (rev(i), 0)),
                  pl.BlockSpec(memory_space=pl.ANY),
                  full((4, D)), full((1, D)), full((NH, HD, HD)), full((NH, HD, HD)),
                  full((1, D)), full((1, D)), full((1, D))],
        out_specs=(pl.BlockSpec((None, tt, 2 * D), lambda i: (SLOT_A, rev(i), 0)),
                   full((4, D)), full((1, D)), full((NH, HD, HD)), full((NH, HD, HD)),
                   full((1, D)), full((1, D)), full((1, D))),
        scratch_shapes=[pltpu.VMEM((8, D), F32), pltpu.VMEM((1, D), F32), pltpu.VMEM((1, D), F32)],
        input_output_aliases={5: 0},
        compiler_params=_params(("arbitrary",), VMEM_BIG),
    )(p4, p4, hseq, hseq, dya, dp4, cw, cb, wa, wx, ba, bx, lam)


def _norm_bwd(dh1, dx1, x, gain, tt):
    t = x.shape[0]

    def body(dh_ref, dx1_ref, x_ref, g_ref, dx_ref, dg_ref):
        i = pl.program_id(0)
        r, n = _rms_stats(x_ref[...])
        dh = dh_ref[...]
        dx_ref[...] = dx1_ref[...] + _rms_bwd(dh * g_ref[...], n, r)
        dgv = jnp.sum(dh * n, axis=0, keepdims=True)

        @pl.when(i == 0)
        def _():
            dg_ref[...] = dgv

        @pl.when(i > 0)
        def _():
            dg_ref[...] += dgv

    tile = pl.BlockSpec((tt, D), lambda i: (i, 0))
    vec = pl.BlockSpec((1, D), lambda i: (0, 0))
    return pl.pallas_call(
        body, name="norm_bwd",
        out_shape=(jax.ShapeDtypeStruct((t, D), F32), jax.ShapeDtypeStruct((1, D), F32)),
        grid=(t // tt,), in_specs=[tile, tile, tile, vec], out_specs=(tile, vec),
        compiler_params=_params(("arbitrary",)),
    )(dh1, dx1, x, gain)


def _local_step(x, tgt, w_in, stk_rest, conv_a_w, conv_f_w, small, cidx):
    t = x.shape[0]
    tt = min(256, t)
    tm = min(1024, t)
    wa_bf = small["lru_wa"].astype(BF16)
    wx_bf = small["lru_wx"].astype(BF16)

    h1 = _norm_fwd(x, small["norm_pre_mix"], tt)
    p4, stk_rest = _mm_in_gather(h1, w_in, stk_rest, REST, tm)
    w = dict(zip(REST, _gather_forward(stk_rest, REST)))
    w["w_in"] = w_in
    w_br_a = w["w_branch_a"].reshape(D, D)
    w_br_b = w["w_branch_b"].reshape(D, D)
    w_out = w["w_out"].reshape(D, D)
    w_down = w["w_down"].reshape(DFF, D)
    ya, hseq = _mixer_a_fwd(p4, conv_a_w, small["conv_a_b"], wa_bf, wx_bf, small["lru_ba"], small["lru_bx"],
                            small["lru_lambda"], tt)
    yb, o_all, ss = _hgrn2_fwd(p4, small["hg_lb_logits"], small["hg_norm_g"], tt)
    za, zb, mixb, m2, x1, h2 = _mid_fwd(ya, yb, p4, x, w_br_a, w_br_b, w_out, small["norm_post_mix"],
                                        small["norm_pre_ffn"], min(512, t))
    up_pre = _mm_nn_sharded(h2, w["w_up"], F32, tm, "mm_up")
    tf = min(128, t)
    y, uv, gl, dgl = _ffn_act_fwd(up_pre, conv_f_w, small["conv_f_b"], tf)
    dx2, dm3, lossv, d_norm_post_ffn = _down_loss(y, w_down, x1, tgt, small["norm_post_ffn"], min(512, t))

    d_w_down = _mm_tn(y, dm3, DFF // 2, D, tm, "mm_dw_down")
    dup_pre, d_conv_f_w, d_conv_f_b = _ffn_act_bwd(dm3, w_down, up_pre, uv, gl, dgl, conv_f_w, tf)
    d_w_up = _mm_tn(h2, dup_pre, D, SH_UP, tm, "mm_dw_up", stacked_out=True)
    dh2 = _mm_nt_sharded(dup_pre, w["w_up"], tm, "mm_dh2")
    dx1, dm2, dza, dzb, dya, dyb, dp4, d_norm_post_mix, d_norm_pre_ffn = _mid_bwd(
        dh2, dx2, x1, m2, za, zb, p4, w_br_a, w_br_b, w_out, small["norm_post_mix"], small["norm_pre_ffn"], tt)
    d_w_out = _mm_tn(mixb, dm2, D, D, tm, "mm_dw_out")
    d_w_br_a = _mm_tn(ya, dza, D, D, tm, "mm_dw_bra")
    d_w_br_b = _mm_tn(yb, dzb, D, D, tm, "mm_dw_brb")
    early = {"w_branch_a": d_w_br_a.reshape(NCHIP, SH_BR, D), "w_branch_b": d_w_br_b.reshape(NCHIP, SH_BR, D),
             "w_out": d_w_out.reshape(NCHIP, SH_BR, D), "w_up": d_w_up, "w_down": d_w_down.reshape(NCHIP, SH_DN, D)}
    rb, _ = _reduce_stage1(early, REST, (), "reduce_d2d_in_early")
    p_rest = [_sum_own_half(early[n], rb[n], cidx, "sum_half_" + n) for n in REST]
    dp4, d_lb, d_hg_norm_g, q_rest = _hgrn2_bwd(p4, o_all, ss, dyb, dp4, small["hg_lb_logits"], small["hg_norm_g"],
                                                p_rest, tt)
    dp4, d_conv_a_w, d_conv_a_b, d_lru_wa, d_lru_wx, d_lru_ba, d_lru_bx, d_lru_lambda = _mixer_a_bwd(
        p4, hseq, dya, dp4, conv_a_w, small["conv_a_b"], wa_bf, wx_bf, small["lru_ba"], small["lru_bx"],
        small["lru_lambda"], tt)
    d_w_in = _mm_tn(h1, dp4, D, SH_IN, tm, "mm_dw_in", stacked_slot_fn=_slot_of_chip, stacked_out=True)
    rb, _ = _reduce_stage1({"w_in": d_w_in}, ("w_in",), (), "reduce_d2d_in_w_in")
    p_w_in = _sum_own_half(d_w_in, rb["w_in"], cidx, "sum_half_w_in")
    dh1, q_w_in = _mm_dh1_exchange(dp4, w_in, p_w_in, tm)
    grad_x, d_norm_pre_mix = _norm_bwd(dh1, dx1, x, small["norm_pre_mix"], tt)

    smalls = {
        "norm_pre_mix": d_norm_pre_mix, "conv_a_b": d_conv_a_b, "lru_ba": d_lru_ba, "lru_bx": d_lru_bx,
        "lru_lambda": d_lru_lambda, "hg_lb_logits": d_lb, "hg_norm_g": d_hg_norm_g, "norm_post_mix": d_norm_post_mix,
        "norm_pre_ffn": d_norm_pre_ffn, "norm_post_ffn": d_norm_post_ffn, "lossv": lossv,
        "conv_a_w": d_conv_a_w, "lru_wa": d_lru_wa, "lru_wx": d_lru_wx,
        "conv_f_b": d_conv_f_b, "conv_f_w": d_conv_f_w,
    }
    p_big = dict(zip(REST, p_rest), w_in=p_w_in)
    q_big = dict(zip(REST, q_rest), w_in=q_w_in)
    return grad_x, p_big, q_big, smalls


BIG = ("w_in", "w_branch_a", "w_branch_b", "w_out", "w_up", "w_down")
BIG_SHAPE = {"w_in": (D, SH_IN), "w_branch_a": (SH_BR, D), "w_branch_b": (SH_BR, D), "w_out": (SH_BR, D),
             "w_up": (D, SH_UP), "w_down": (SH_DN, D)}
NBIG = len(BIG)
REST = BIG[1:]
VEC_ROWS = (("norm_pre_mix", 0, 1), ("conv_a_b", 1, 1), ("lru_ba", 2, 1), ("lru_bx", 3, 1), ("lru_lambda", 4, 1),
            ("hg_lb_logits", 5, 2), ("hg_norm_g", 7, 1), ("norm_post_mix", 8, 1), ("norm_pre_ffn", 9, 1),
            ("norm_post_ffn", 10, 1))
ROW_LOSS = 11
ROW_CONV_A = 12
S1_ROWS = 16
S2_ROWS = 8


def _place():
    x, y, c = lax.axis_index("x"), lax.axis_index("y"), lax.axis_index("c")
    chips = [(1 - x, y), (x, 1 - y), (1 - x, 1 - y)]
    return x, y, c, 2 * x + y, chips


def _remote(src, dst, ssem, rsem, dev):
    return pltpu.make_async_remote_copy(src_ref=src, dst_ref=dst, send_sem=ssem, recv_sem=rsem,
                                        device_id=dev, device_id_type=MESH)


def _hbm_call(body, name, ins, out_shapes, n_sems, aliases=None):
    any_spec = pl.BlockSpec(memory_space=pl.ANY)
    return pl.pallas_call(
        body, name=name, out_shape=tuple(out_shapes),
        in_specs=[any_spec] * len(ins), out_specs=tuple([any_spec] * len(out_shapes)),
        scratch_shapes=[pltpu.SemaphoreType.DMA((n,)) for n in n_sems],
        input_output_aliases=aliases or {},
        compiler_params=pltpu.CompilerParams(has_side_effects=True),
    )(*ins)


def _gather_copies(stk, names, ssem, rsem, fssem=None, frsem=None):
    x, y, c, j, chips = _place()
    sends, arrive, fwds, farrive = [], [], [], []
    for w, n in enumerate(names):
        hw = BIG_SHAPE[n][0] // 2
        mine = stk[w].at[j, pl.ds(c * hw, hw), :]
        for k, (cx, cy) in enumerate(chips):
            i = 3 * w + k
            got = stk[w].at[2 * cx + cy, pl.ds(c * hw, hw), :]
            other = stk[w].at[2 * cx + cy, pl.ds((1 - c) * hw, hw), :]
            sends.append(_remote(mine, mine, ssem.at[i], rsem.at[i], (cx, cy, c)))
            arrive.append(_remote(got, got, ssem.at[i], rsem.at[i], (cx, cy, c)))
            if fssem is not None:
                fwds.append(_remote(got, got, fssem.at[i], frsem.at[i], (x, y, 1 - c)))
                farrive.append(_remote(other, other, fssem.at[i], frsem.at[i], (x, y, 1 - c)))
    return sends, arrive, fwds, farrive


def _gather_first(stacked_w_in, conv_a_s, conv_f_s):
    ins = [stacked_w_in, conv_a_s, conv_f_s]
    out_shapes = [jax.ShapeDtypeStruct(stacked_w_in.shape, stacked_w_in.dtype)]
    out_shapes += [jax.ShapeDtypeStruct((NCHIP,) + a.shape, a.dtype) for a in (conv_a_s, conv_f_s)]

    def body(w_in, ca_src, cf_src, w_out, ca_dst, cf_dst, ssem, rsem, fssem, frsem, csend, crecv, lsem):
        del w_in
        x, y, c, j, chips = _place()
        conv = ((ca_src, ca_dst), (cf_src, cf_dst))
        locs = [pltpu.make_async_copy(src, dst.at[j], lsem.at[i]) for i, (src, dst) in enumerate(conv)]
        csends = [_remote(src, dst.at[j], csend.at[3 * i + k], crecv.at[3 * i + k], (cx, cy, c))
                  for i, (src, dst) in enumerate(conv) for k, (cx, cy) in enumerate(chips)]
        sends, arrive, fwds, farrive = _gather_copies([w_out], ("w_in",), ssem, rsem, fssem, frsem)
        for cp in locs + sends + csends:
            cp.start()
        for got, fwd in zip(arrive, fwds):
            got.wait_recv()
            fwd.start()
        for i, (_, dst) in enumerate(conv):
            for k, (cx, cy) in enumerate(chips):
                got = dst.at[2 * cx + cy]
                _remote(got, got, csend.at[3 * i + k], crecv.at[3 * i + k], (cx, cy, c)).wait_recv()
        for cp in farrive:
            cp.wait_recv()
        for cp in sends + fwds + csends:
            cp.wait_send()
        for cp in locs:
            cp.wait()

    return _hbm_call(body, "gather_first", ins, out_shapes, (3, 3, 3, 3, 6, 6, 2), aliases={0: 0})


def _gather_forward(stk, names):
    nw = len(names)

    def body(*refs):
        dst = refs[nw:2 * nw]
        ssem, rsem, fssem, frsem = refs[2 * nw:]
        _, _, fwds, farrive = _gather_copies(dst, names, ssem, rsem, fssem, frsem)
        for cp in fwds:
            cp.start()
        for cp in farrive:
            cp.wait_recv()
        for cp in fwds:
            cp.wait_send()

    out_shapes = [jax.ShapeDtypeStruct(a.shape, a.dtype) for a in stk]
    return _hbm_call(body, "gather_forward", stk, out_shapes, (3 * nw,) * 4, aliases={w: w for w in range(nw)})


def _exchange_copies(dst, pieces, ssem, rsem):
    x, y, c, j, chips = _place()
    sends, arrive = [], []
    for w in range(len(dst)):
        for k, (cx, cy) in enumerate(chips):
            i = 3 * w + k
            sends.append(_remote(pieces[w](2 * cx + cy), dst[w].at[j], ssem.at[i], rsem.at[i], (cx, cy, c)))
            got = dst[w].at[2 * cx + cy]
            arrive.append(_remote(got, got, ssem.at[i], rsem.at[i], (cx, cy, c)))
    return sends, arrive


def _reduce_stage1(big_g, names, smalls, name):
    nb = len(names)
    ins = [big_g[n] for n in names] + list(smalls)
    n_in = len(ins)
    halves = [BIG_SHAPE[n][0] // 2 for n in names]
    out_shapes = [jax.ShapeDtypeStruct((NCHIP, halves[w], BIG_SHAPE[n][1]), F32) for w, n in enumerate(names)]
    out_shapes += [jax.ShapeDtypeStruct(a.shape, F32) for a in smalls]

    def body(*refs):
        src, dst = refs[:n_in], refs[n_in:2 * n_in]
        ssem, rsem = refs[2 * n_in:]
        x, y, c, _, _ = _place()
        cps = []
        for w in range(n_in):
            s_ = src[w].at[:, pl.ds((1 - c) * halves[w], halves[w]), :] if w < nb else src[w]
            cp = _remote(s_, dst[w], ssem.at[w], rsem.at[w], (x, y, 1 - c))
            cp.start()
            cps.append(cp)
        for cp in cps:
            cp.wait()

    outs = _hbm_call(body, name, ins, out_shapes, (n_in, n_in))
    return dict(zip(names, outs[:nb])), outs[nb:]


def _reduce_stage2(ps1, ps2, ps3):
    ins = [ps1, ps2, ps3]
    h1, h2, h3 = S1_ROWS // 2, DUP // 2, D
    out_shapes = [jax.ShapeDtypeStruct((NCHIP, h1, D), F32), jax.ShapeDtypeStruct((NCHIP, S2_ROWS, h2), F32),
                  jax.ShapeDtypeStruct((NCHIP, h3, HD), F32)]

    def body(*refs):
        src, dst = refs[:3], refs[3:6]
        ssem, rsem = refs[6:]
        c = lax.axis_index("c")
        pieces = [lambda chip: src[0].at[pl.ds(c * h1, h1), :],
                  lambda chip: src[1].at[:, pl.ds(c * h2, h2)],
                  lambda chip: src[2].at[pl.ds(c * h3, h3), :]]
        sends, arrive = _exchange_copies(dst, pieces, ssem, rsem)
        for cp in sends:
            cp.start()
        for cp in arrive:
            cp.wait_recv()
        for cp in sends:
            cp.wait_send()

    return _hbm_call(body, "reduce_ici_small", ins, out_shapes, (9, 9))


def _reduce_stage3(f_big, fs1, fs2, fs3):
    ins = [f_big[n] for n in BIG] + [fs1, fs2, fs3]
    n_in = len(ins)
    halves = [BIG_SHAPE[n][0] // 2 for n in BIG]
    h1, h2, h3 = S1_ROWS // 2, DUP // 2, D
    out_shapes = [jax.ShapeDtypeStruct(BIG_SHAPE[n], F32) for n in BIG]
    out_shapes += [jax.ShapeDtypeStruct((S1_ROWS, D), F32), jax.ShapeDtypeStruct((S2_ROWS, DUP), F32),
                   jax.ShapeDtypeStruct((2 * D, HD), F32)]

    def body(*refs):
        dst = refs[n_in:2 * n_in]
        ssem, rsem = refs[2 * n_in:]
        x, y, c, _, _ = _place()

        def place(w, which):
            if w < NBIG:
                return dst[w].at[pl.ds(which * halves[w], halves[w]), :]
            if w == NBIG:
                return dst[w].at[pl.ds(which * h1, h1), :]
            if w == NBIG + 1:
                return dst[w].at[:, pl.ds(which * h2, h2)]
            return dst[w].at[pl.ds(which * h3, h3), :]

        cps = [_remote(place(w, c), place(w, c), ssem.at[w], rsem.at[w], (x, y, 1 - c)) for w in range(n_in)]
        for cp in cps:
            cp.start()
        for w in range(n_in):
            got = place(w, 1 - c)
            _remote(got, got, ssem.at[w], rsem.at[w], (x, y, 1 - c)).wait_recv()
        for cp in cps:
            cp.wait_send()

    outs = _hbm_call(body, "reduce_d2d_out", ins, out_shapes, (n_in, n_in), aliases={w: w for w in range(n_in)})
    return dict(zip(BIG, outs[:NBIG])), outs[NBIG], outs[NBIG + 1], outs[NBIG + 2]


def _row_tile(rows):
    for tr in (128, 176, 64, 16, 8):
        if rows % tr == 0:
            return tr
    return rows


def _sum_own_half(g, rb, cidx, name):
    s, rows, cols = g.shape
    half = rows // 2
    tr = _row_tile(half)
    nb = half // tr

    def body(c_ref, g_ref, r_ref, o_ref):
        del c_ref
        o_ref[...] = (g_ref[...] + r_ref[...]).astype(BF16)

    grid_spec = pltpu.PrefetchScalarGridSpec(
        num_scalar_prefetch=1, grid=(s, nb),
        in_specs=[pl.BlockSpec((None, tr, cols), lambda k, i, c: (k, c[0] * nb + i, 0)),
                  pl.BlockSpec((None, tr, cols), lambda k, i, c: (k, i, 0))],
        out_specs=pl.BlockSpec((None, tr, cols), lambda k, i, c: (k, i, 0)))
    return pl.pallas_call(
        body, name=name, grid_spec=grid_spec, out_shape=jax.ShapeDtypeStruct((s, half, cols), BF16),
        compiler_params=_params(("parallel", "parallel")),
    )(cidx, g, rb)


def _sum_chips(q, p, jc, name, by_cols=False):
    s, rows, cols = q.shape
    tr = _row_tile(rows)
    nb = rows // tr
    stacked = p.ndim == 3

    def body(jc_ref, q_ref, p_ref, o_ref):
        j = jc_ref[0]
        own = p_ref[...].astype(F32)
        acc = None
        for k in range(NCHIP):
            term = jnp.where(j == k, own, q_ref[k].astype(F32))
            acc = term if acc is None else acc + term
        o_ref[...] = acc

    if by_cols:
        half_spec = pl.BlockSpec((tr, cols), lambda i, jc_ref: (i, jc_ref[1]))
        out_shape = jax.ShapeDtypeStruct((rows, 2 * cols), F32)
    else:
        half_spec = pl.BlockSpec((tr, cols), lambda i, jc_ref: (jc_ref[1] * nb + i, 0))
        out_shape = jax.ShapeDtypeStruct((2 * rows, cols), F32)
    p_spec = pl.BlockSpec((None, tr, cols), lambda i, jc_ref: (jc_ref[0], i, 0)) if stacked else half_spec
    grid_spec = pltpu.PrefetchScalarGridSpec(
        num_scalar_prefetch=1, grid=(nb,),
        in_specs=[pl.BlockSpec((s, tr, cols), lambda i, jc_ref: (0, i, 0)), p_spec],
        out_specs=half_spec)
    return pl.pallas_call(
        body, name=name, grid_spec=grid_spec, out_shape=out_shape,
        compiler_params=_params(("parallel",)),
    )(jc, q, p)


def _place_shard(w, jc, name):
    rows, cols = w.shape
    tr = _row_tile(rows)

    def body(jc_ref, w_ref, o_ref):
        del jc_ref
        o_ref[...] = w_ref[...].astype(BF16)

    grid_spec = pltpu.PrefetchScalarGridSpec(
        num_scalar_prefetch=1, grid=(rows // tr,),
        in_specs=[pl.BlockSpec((tr, cols), lambda i, jc_ref: (i, 0))],
        out_specs=pl.BlockSpec((None, tr, cols), lambda i, jc_ref: (jc_ref[0], i, 0)))
    return pl.pallas_call(
        body, name=name, grid_spec=grid_spec, out_shape=jax.ShapeDtypeStruct((NCHIP, rows, cols), BF16),
        compiler_params=_params(("parallel",)),
    )(jc, w)


def _add(a, b, name):
    def body(a_ref, b_ref, o_ref):
        o_ref[...] = a_ref[...] + b_ref[...]

    return pl.pallas_call(body, name=name, out_shape=jax.ShapeDtypeStruct(a.shape, F32))(a, b)


def _pack_small(sm):
    vec_in = [sm[n] for n, _, _ in VEC_ROWS]
    nv = len(vec_in)

    def body(*refs):
        ins, lossv, dcw, dcfb, dcfw, s1, s2 = refs[:nv], refs[nv], refs[nv + 1], refs[nv + 2], refs[nv + 3], \
            refs[nv + 4], refs[nv + 5]
        for ref, (_, r0, nr) in zip(ins, VEC_ROWS):
            s1[r0:r0 + nr, :] = ref[...]
        s1[ROW_LOSS:ROW_LOSS + 1, :] = lossv[...]
        s1[ROW_CONV_A:ROW_CONV_A + 4, :] = dcw[...]
        s2[0:1, :] = dcfb[...]
        s2[1:4, :] = dcfw[...]
        s2[4:8, :] = jnp.zeros((4, DUP), F32)

    return pl.pallas_call(
        body, name="pack_small",
        out_shape=(jax.ShapeDtypeStruct((S1_ROWS, D), F32), jax.ShapeDtypeStruct((S2_ROWS, DUP), F32)),
    )(*vec_in, sm["lossv"], sm["conv_a_w"], sm["conv_f_b"], sm["conv_f_w"])


def _adam_math(w, g, m, v):
    m = ADAM_B1 * m + (1.0 - ADAM_B1) * g
    v = ADAM_B2 * v + (1.0 - ADAM_B2) * (g * g)
    m_hat = m / (1.0 - ADAM_B1 ** ADAM_STEP)
    v_hat = v / (1.0 - ADAM_B2 ** ADAM_STEP)
    delta = -ADAM_LR * (m_hat / (jnp.sqrt(v_hat) + ADAM_EPS) + ADAM_WD * w)
    return delta, m, v


def _adam(w, g, m, v, name):
    rows, cols = w.shape
    tr = _row_tile(rows)

    def body(w_ref, g_ref, m_ref, v_ref, d_ref, mo_ref, vo_ref):
        d_ref[...], mo_ref[...], vo_ref[...] = _adam_math(w_ref[...], g_ref[...], m_ref[...], v_ref[...])

    spec = pl.BlockSpec((tr, cols), lambda i: (i, 0))
    return pl.pallas_call(
        body, name=name, out_shape=(jax.ShapeDtypeStruct(w.shape, F32),) * 3, grid=(rows // tr,),
        in_specs=[spec] * 4, out_specs=(spec,) * 3,
        compiler_params=_params(("parallel",)),
    )(w, g, m, v)


def _adam_small(gs1, gs2, gs3, w, m, v):
    names = [n for n, _, _ in VEC_ROWS] + ["conv_f_b", "lru_wa", "lru_wx"]
    nn = len(names)

    def grad_of(i, g1, g2, g3):
        if i < len(VEC_ROWS):
            _, r0, nr = VEC_ROWS[i]
            return g1[r0:r0 + nr, :]
        if names[i] == "conv_f_b":
            return g2[0:1, :]
        return g3[0] if names[i] == "lru_wa" else g3[1]

    def body(*refs):
        g1, g2, g3 = refs[0], refs[1], refs[2]
        ws, ms, vs = refs[3:3 + nn], refs[3 + nn:3 + 2 * nn], refs[3 + 2 * nn:3 + 3 * nn]
        outs = refs[3 + 3 * nn:]
        for i in range(nn):
            d, mn, vn = _adam_math(ws[i][...], grad_of(i, g1, g2, g3), ms[i][...], vs[i][...])
            outs[i][...] = d
            outs[nn + i][...] = mn
            outs[2 * nn + i][...] = vn

    shapes = [jax.ShapeDtypeStruct(w[n].shape, F32) for n in names]
    outs = pl.pallas_call(body, name="adam_small", out_shape=tuple(shapes * 3))(
        gs1, gs2, gs3, *[w[n] for n in names], *[m[n] for n in names], *[v[n] for n in names])
    return {n: (outs[i], outs[nn + i], outs[2 * nn + i]) for i, n in enumerate(names)}


WEIGHTS = ("norm_pre_mix", "w_in", "conv_a_w", "conv_a_b", "lru_wa", "lru_ba", "lru_wx", "lru_bx", "lru_lambda",
           "hg_lb_logits", "hg_norm_g", "w_branch_a", "w_branch_b", "w_out", "norm_post_mix", "norm_pre_ffn",
           "w_up", "conv_f_w", "conv_f_b", "w_down", "norm_post_ffn")
NW = len(WEIGHTS)


def kernel(x, norm_pre_mix, w_in, conv_a_w, conv_a_b, lru_wa, lru_ba, lru_wx, lru_bx, lru_lambda, hg_lb_logits, hg_norm_g, w_branch_a, w_branch_b, w_out, norm_post_mix, norm_pre_ffn, w_up, conv_f_w, conv_f_b, w_down, norm_post_ffn, loss_target, m_norm_pre_mix, m_w_in, m_conv_a_w, m_conv_a_b, m_lru_wa, m_lru_ba, m_lru_wx, m_lru_bx, m_lru_lambda, m_hg_lb_logits, m_hg_norm_g, m_w_branch_a, m_w_branch_b, m_w_out, m_norm_post_mix, m_norm_pre_ffn, m_w_up, m_conv_f_w, m_conv_f_b, m_w_down, m_norm_post_ffn, v_norm_pre_mix, v_w_in, v_conv_a_w, v_conv_a_b, v_lru_wa, v_lru_ba, v_lru_wx, v_lru_bx, v_lru_lambda, v_hg_lb_logits, v_hg_norm_g, v_w_branch_a, v_w_branch_b, v_w_out, v_norm_post_mix, v_norm_pre_ffn, v_w_up, v_conv_f_w, v_conv_f_b, v_w_down, v_norm_post_ffn):
    rest = (norm_pre_mix, w_in, conv_a_w, conv_a_b, lru_wa, lru_ba, lru_wx, lru_bx, lru_lambda, hg_lb_logits, hg_norm_g, w_branch_a, w_branch_b, w_out, norm_post_mix, norm_pre_ffn, w_up, conv_f_w, conv_f_b, w_down, norm_post_ffn, loss_target, m_norm_pre_mix, m_w_in, m_conv_a_w, m_conv_a_b, m_lru_wa, m_lru_ba, m_lru_wx, m_lru_bx, m_lru_lambda, m_hg_lb_logits, m_hg_norm_g, m_w_branch_a, m_w_branch_b, m_w_out, m_norm_post_mix, m_norm_pre_ffn, m_w_up, m_conv_f_w, m_conv_f_b, m_w_down, m_norm_post_ffn, v_norm_pre_mix, v_w_in, v_conv_a_w, v_conv_a_b, v_lru_wa, v_lru_ba, v_lru_wx, v_lru_bx, v_lru_lambda, v_hg_lb_logits, v_hg_norm_g, v_w_branch_a, v_w_branch_b, v_w_out, v_norm_post_mix, v_norm_pre_ffn, v_w_up, v_conv_f_w, v_conv_f_b, v_w_down, v_norm_post_ffn)
    w_in_args = dict(zip(WEIGHTS, rest[:NW]))
    loss_target = rest[NW]
    m_args = dict(zip(WEIGHTS, rest[NW + 1:2 * NW + 1]))
    v_args = dict(zip(WEIGHTS, rest[2 * NW + 1:3 * NW + 1]))
    shape_of = {n: w_in_args[n].shape for n in WEIGHTS}

    def two_d(n, a):
        if n in BIG:
            return a.reshape(BIG_SHAPE[n])
        if n in ("lru_wa", "lru_wx"):
            return a.reshape(NH, HD, HD)
        return a.reshape(a.shape[-2:])

    w2 = {n: two_d(n, w_in_args[n]) for n in WEIGHTS}
    m2 = {n: two_d(n, m_args[n]) for n in WEIGHTS}
    v2 = {n: two_d(n, v_args[n]) for n in WEIGHTS}

    cidx = lax.axis_index("c").astype(jnp.int32).reshape(1)
    jchip = 2 * lax.axis_index("x") + lax.axis_index("y")

    jc = jnp.stack([jchip, lax.axis_index("c")]).astype(jnp.int32)

    shards = {n: _place_shard(w2[n], jc, "place_" + n) for n in BIG}
    conv_a_s = jnp.pad(w2["conv_a_w"], ((0, 4), (0, 0)))
    conv_f_s = jnp.pad(w2["conv_f_w"], ((0, 5), (0, 0)))
    w_in_full, conv_a_g, conv_f_g = _gather_first(shards["w_in"], conv_a_s, conv_f_s)
    conv_a_full = jnp.transpose(conv_a_g, (1, 0, 2)).reshape(8, D)[0:4]
    conv_f_full = jnp.transpose(conv_f_g, (1, 0, 2)).reshape(8, DUP)[0:3]
    small = {n: w2[n] for n in WEIGHTS if n not in BIG and n not in ("conv_a_w", "conv_f_w")}

    grad_x, p_big, q_big, sm_g = _local_step(
        x[0], loss_target[0], w_in_full, [shards[n] for n in REST], conv_a_full, conv_f_full, small, cidx)

    s1, s2 = _pack_small(sm_g)
    s3 = jnp.concatenate([sm_g["lru_wa"].reshape(D, HD), sm_g["lru_wx"].reshape(D, HD)], axis=0)
    _, (rs1, rs2, rs3) = _reduce_stage1({}, (), (s1, s2, s3), "reduce_d2d_in_small")
    ps1, ps2, ps3 = _add(s1, rs1, "add_s1"), _add(s2, rs2, "add_s2"), _add(s3, rs3, "add_s3")
    qs1, qs2, qs3 = _reduce_stage2(ps1, ps2, ps3)
    f_big = {n: _sum_chips(q_big[n], p_big[n], jc, "sum_chips_" + n) for n in BIG}
    fs1 = _sum_chips(qs1, ps1, jc, "sum_chips_s1")
    fs2 = _sum_chips(qs2, ps2, jc, "sum_chips_s2", by_cols=True)
    fs3 = _sum_chips(qs3, ps3, jc, "sum_chips_s3")
    g_big, gs1, gs2, gs3 = _reduce_stage3(f_big, fs1, fs2, fs3)

    res = {}
    for n in BIG:
        d, mn, vn = _adam(w2[n], g_big[n], m2[n], v2[n], "adam_" + n)
        res[n] = (g_big[n], d, mn, vn)
    small_res = _adam_small(gs1, gs2, gs3.reshape(2, NH, HD, HD), w2, m2, v2)
    for n, r0, nr in VEC_ROWS:
        res[n] = (gs1[r0:r0 + nr],) + small_res[n]
    res["conv_f_b"] = (gs2[0:1],) + small_res["conv_f_b"]
    res["lru_wa"] = (gs3[0:D].reshape(NH, HD, HD),) + small_res["lru_wa"]
    res["lru_wx"] = (gs3[D:2 * D].reshape(NH, HD, HD),) + small_res["lru_wx"]
    g_ca = lax.dynamic_slice_in_dim(gs1[ROW_CONV_A:ROW_CONV_A + 4], jchip * (D // NCHIP), D // NCHIP, axis=1)
    g_cf = lax.dynamic_slice_in_dim(gs2[1:4], jchip * SH_UP, SH_UP, axis=1)
    res["conv_a_w"] = (g_ca,) + _adam(w2["conv_a_w"], g_ca, m2["conv_a_w"], v2["conv_a_w"], "adam_conv_a_w")
    res["conv_f_w"] = (g_cf,) + _adam(w2["conv_f_w"], g_cf, m2["conv_f_w"], v2["conv_f_w"], "adam_conv_f_w")

    loss = (0.5 / D) * jnp.sum(gs1[ROW_LOSS])
    out = [loss, grad_x.reshape(x.shape)]
    for part in range(4):
        out += [res[n][part].reshape(shape_of[n]) for n in WEIGHTS]
    return tuple(out)
```

```python
import functools

import jax
import jax.numpy as jnp
from jax import lax
from jax.experimental import pallas as pl
from jax.experimental.pallas import tpu as pltpu

F32 = jnp.float32
BF16 = jnp.bfloat16

D = 1024
NH = 8
HD = 128
CH = 32
DFF = 2816
DUP = 2 * DFF
NCHIP = 4
SH_IN = 2 * D
SH_UP = DUP // NCHIP
SH_DN = DFF // NCHIP
SH_BR = D // NCHIP
EPS = 1e-6
LRU_C = 8.0
ADAM_LR = 0.001
ADAM_B1 = 0.9
ADAM_B2 = 0.999
ADAM_EPS = 1e-08
ADAM_WD = 0.01
ADAM_STEP = 10
VMEM_BIG = 56 * 1024 * 1024
MESH = pl.DeviceIdType.MESH

SLOT_A, SLOT_B, SLOT_C, SLOT_G = 2, 0, 1, 3


def _slot_of_chip(s):
    return jnp.where(s == 3, 3, (s + 2) % 3)


def _params(sem, vmem=None):
    return pltpu.CompilerParams(dimension_semantics=sem, vmem_limit_bytes=vmem)


_GC = 0.7978845608028654
_GA = 0.044715


def _gelu(x):
    return 0.5 * x * (1.0 + jnp.tanh(_GC * (x + _GA * x * x * x)))


def _gelu_and_grad(x):
    x2 = x * x
    th = jnp.tanh(_GC * x * (1.0 + _GA * x2))
    g = 0.5 * x * (1.0 + th)
    dg = 0.5 * (1.0 + th) + 0.5 * x * (1.0 - th * th) * _GC * (1.0 + 3.0 * _GA * x2)
    return g, dg


def _sig(x):
    return jax.nn.sigmoid(x)


def _dot(a, b):
    return jnp.dot(a, b, preferred_element_type=F32)


def _dot_nt(a, b):
    return lax.dot_general(a, b, (((1,), (1,)), ((), ())), preferred_element_type=F32)


def _dot_tn(a, b):
    return lax.dot_general(a, b, (((0,), (0,)), ((), ())), preferred_element_type=F32)


def _chunk_cumsum(x):
    pos = lax.broadcasted_iota(jnp.int32, (x.shape[0], 1), 0) & (CH - 1)
    d = 1
    while d < CH:
        x = x + jnp.where(pos >= d, pltpu.roll(x, d, 0), 0.0)
        d *= 2
    return x


def _chunk_revcumsum(x):
    n = x.shape[0]
    pos = lax.broadcasted_iota(jnp.int32, (n, 1), 0) & (CH - 1)
    d = 1
    while d < CH:
        x = x + jnp.where(pos < CH - d, pltpu.roll(x, n - d, 0), 0.0)
        d *= 2
    return x


def _chunk_last(x):
    n = x.shape[0]
    return jnp.concatenate(
        [jnp.broadcast_to(x[c * CH + CH - 1:c * CH + CH, :], (CH, x.shape[1])) for c in range(n // CH)], axis=0)


def _chunk_total(x):
    n = x.shape[0]
    return jnp.concatenate(
        [jnp.broadcast_to(jnp.sum(x[c * CH:(c + 1) * CH, :], axis=0, keepdims=True), (CH, x.shape[1]))
         for c in range(n // CH)], axis=0)


def _rms_stats(x):
    r = lax.rsqrt(jnp.mean(x * x, axis=-1, keepdims=True) + EPS)
    return r, x * r


def _rms_bwd(gd, n, r):
    return r * (gd - n * jnp.mean(gd * n, axis=-1, keepdims=True))


def _shift_rows(x, d, fill):
    rows = lax.broadcasted_iota(jnp.int32, (x.shape[0], 1), 0)
    return jnp.where(rows >= d, pltpu.roll(x, d, 0), fill)


def _scan_down(a, u, carry):
    n = a.shape[0]
    pos = lax.broadcasted_iota(jnp.int32, (n, 1), 0) & 7
    for d in (1, 2, 4):
        u = a * jnp.where(pos >= d, pltpu.roll(u, d, 0), 0.0) + u
        a = a * jnp.where(pos >= d, pltpu.roll(a, d, 0), 1.0)
    out = []
    for v in range(n // 8):
        h = a[v * 8:v * 8 + 8, :] * carry + u[v * 8:v * 8 + 8, :]
        carry = h[7:8, :]
        out.append(h)
    return jnp.concatenate(out, axis=0)


def _scan_up(b, g, carry):
    n = b.shape[0]
    pos = lax.broadcasted_iota(jnp.int32, (n, 1), 0) & 7
    for d in (1, 2, 4):
        g = g + b * jnp.where(pos < 8 - d, pltpu.roll(g, n - d, 0), 0.0)
        b = b * jnp.where(pos < 8 - d, pltpu.roll(b, n - d, 0), 1.0)
    out = [None] * (n // 8)
    for v in reversed(range(n // 8)):
        h = g[v * 8:v * 8 + 8, :] + b[v * 8:v * 8 + 8, :] * carry
        carry = h[0:1, :]
        out[v] = h
    return jnp.concatenate(out, axis=0)


def _shift_rows_up(x, d, fill):
    n = x.shape[0]
    rows = lax.broadcasted_iota(jnp.int32, (n, 1), 0)
    return jnp.where(rows < n - d, pltpu.roll(x, n - d, 0), fill)


def _mm_nn_sharded(a, b3, out_dtype, tm, name, slot_fn=None):
    m, k = a.shape
    s, _, ns = b3.shape

    def body(a_ref, b_ref, o_ref):
        o_ref[...] = _dot(a_ref[...], b_ref[...]).astype(out_dtype)

    if slot_fn is None:
        out_shape = jax.ShapeDtypeStruct((m, s * ns), out_dtype)
        out_spec = pl.BlockSpec((tm, ns), lambda j, i: (i, j))
    else:
        out_shape = jax.ShapeDtypeStruct((s, m, ns), out_dtype)
        out_spec = pl.BlockSpec((None, tm, ns), lambda j, i: (slot_fn(j), i, 0))
    return pl.pallas_call(
        body, name=name, out_shape=out_shape, grid=(s, m // tm),
        in_specs=[pl.BlockSpec((tm, k), lambda j, i: (i, 0)),
                  pl.BlockSpec((None, k, ns), lambda j, i: (j, 0, 0))],
        out_specs=out_spec,
        compiler_params=_params(("parallel", "parallel"), VMEM_BIG),
    )(a, b3)


def _mm_in_gather(a, w_in3, stk_rest, names_rest, tm):
    m, k = a.shape
    s, _, ns = w_in3.shape
    nr = len(stk_rest)
    mt = m // tm

    def body(a_ref, b_ref, *rest):
        o_ref = rest[nr]
        stk = rest[nr + 1:2 * nr + 1]
        ssem, rsem = rest[2 * nr + 1:]
        step = pl.program_id(0) * mt + pl.program_id(1)
        sends, arrive, _, _ = _gather_copies(stk, names_rest, ssem, rsem)

        @pl.when(step == 0)
        def _():
            for cp in sends:
                cp.start()

        o_ref[...] = _dot(a_ref[...], b_ref[...])

        @pl.when(step == s * mt - 1)
        def _():
            for cp in arrive:
                cp.wait_recv()
            for cp in sends:
                cp.wait_send()

    any_spec = pl.BlockSpec(memory_space=pl.ANY)
    outs = pl.pallas_call(
        body, name="mm_in",
        out_shape=(jax.ShapeDtypeStruct((s, m, ns), F32),) + tuple(jax.ShapeDtypeStruct(x.shape, x.dtype) for x in stk_rest),
        grid=(s, mt),
        in_specs=[pl.BlockSpec((tm, k), lambda j, i: (i, 0)), pl.BlockSpec((None, k, ns), lambda j, i: (j, 0, 0))]
        + [any_spec] * nr,
        out_specs=(pl.BlockSpec((None, tm, ns), lambda j, i: (_slot_of_chip(j), i, 0)),) + (any_spec,) * nr,
        scratch_shapes=[pltpu.SemaphoreType.DMA((3 * nr,)), pltpu.SemaphoreType.DMA((3 * nr,))],
        input_output_aliases={2 + w: 1 + w for w in range(nr)},
        compiler_params=pltpu.CompilerParams(dimension_semantics=("arbitrary", "arbitrary"),
                                             vmem_limit_bytes=VMEM_BIG, has_side_effects=True),
    )(a, w_in3, *stk_rest)
    return outs[0], list(outs[1:])


def _mm_dh1_exchange(dp4, w_in3, p_w_in, tm):
    s, k, ns = w_in3.shape
    m = dp4.shape[1]
    mt = m // tm

    def body(a_ref, b_ref, p_ref, o_ref, q_ref, ssem, rsem):
        i, j = pl.program_id(0), pl.program_id(1)
        sends, arrive = _exchange_copies([q_ref], [lambda chip: p_ref.at[chip]], ssem, rsem)

        @pl.when((i == 0) & (j == 0))
        def _():
            for cp in sends:
                cp.start()

        @pl.when(j == 0)
        def _():
            o_ref[...] = _dot_nt(a_ref[...], b_ref[...])

        @pl.when(j > 0)
        def _():
            o_ref[...] += _dot_nt(a_ref[...], b_ref[...])

        @pl.when((i == mt - 1) & (j == s - 1))
        def _():
            for cp in arrive:
                cp.wait_recv()
            for cp in sends:
                cp.wait_send()

    any_spec = pl.BlockSpec(memory_space=pl.ANY)
    return pl.pallas_call(
        body, name="mm_dh1",
        out_shape=(jax.ShapeDtypeStruct((m, k), F32), jax.ShapeDtypeStruct(p_w_in.shape, BF16)),
        grid=(mt, s),
        in_specs=[pl.BlockSpec((None, tm, ns), lambda i, j: (_slot_of_chip(j), i, 0)),
                  pl.BlockSpec((None, k, ns), lambda i, j: (j, 0, 0)), any_spec],
        out_specs=(pl.BlockSpec((tm, k), lambda i, j: (i, 0)), any_spec),
        scratch_shapes=[pltpu.SemaphoreType.DMA((3,)), pltpu.SemaphoreType.DMA((3,))],
        compiler_params=pltpu.CompilerParams(dimension_semantics=("arbitrary", "arbitrary"),
                                             vmem_limit_bytes=VMEM_BIG, has_side_effects=True),
    )(dp4, w_in3, p_w_in)


def _mm_nt_sharded(a, b3, tm, name, stacked_slot_fn=None):
    s, k, ns = b3.shape
    m = a.shape[1] if stacked_slot_fn is not None else a.shape[0]

    def body(a_ref, b_ref, o_ref):
        j = pl.program_id(1)
        @pl.when(j == 0)
        def _():
            o_ref[...] = _dot_nt(a_ref[...], b_ref[...])

        @pl.when(j > 0)
        def _():
            o_ref[...] += _dot_nt(a_ref[...], b_ref[...])

    if stacked_slot_fn is None:
        a_spec = pl.BlockSpec((tm, ns), lambda i, j: (i, j))
    else:
        a_spec = pl.BlockSpec((None, tm, ns), lambda i, j: (stacked_slot_fn(j), i, 0))
    return pl.pallas_call(
        body, name=name, out_shape=jax.ShapeDtypeStruct((m, k), F32), grid=(m // tm, s),
        in_specs=[a_spec, pl.BlockSpec((None, k, ns), lambda i, j: (j, 0, 0))],
        out_specs=pl.BlockSpec((tm, k), lambda i, j: (i, 0)),
        compiler_params=_params(("parallel", "arbitrary"), VMEM_BIG),
    )(a, b3)


def _mm_tn(a, g, tkk, tn, tk, name, stacked_slot_fn=None, stacked_out=False):
    m, k = a.shape
    if stacked_slot_fn is not None:
        n = g.shape[0] * g.shape[2]
        g_spec = pl.BlockSpec((None, tk, tn), lambda kk, j, mm: (stacked_slot_fn(j), mm, 0))
    else:
        n = g.shape[1]
        g_spec = pl.BlockSpec((tk, tn), lambda kk, j, mm: (mm, j))

    def body(a_ref, g_ref, o_ref):
        mm = pl.program_id(2)

        @pl.when(mm == 0)
        def _():
            o_ref[...] = _dot_tn(a_ref[...], g_ref[...])

        @pl.when(mm > 0)
        def _():
            o_ref[...] += _dot_tn(a_ref[...], g_ref[...])

    if stacked_out:
        out_shape = jax.ShapeDtypeStruct((n // tn, k, tn), F32)
        out_spec = pl.BlockSpec((None, tkk, tn), lambda kk, j, mm: (j, kk, 0))
    else:
        out_shape = jax.ShapeDtypeStruct((k, n), F32)
        out_spec = pl.BlockSpec((tkk, tn), lambda kk, j, mm: (kk, j))
    return pl.pallas_call(
        body, name=name, out_shape=out_shape, grid=(k // tkk, n // tn, m // tk),
        in_specs=[pl.BlockSpec((tk, tkk), lambda kk, j, mm: (mm, kk)), g_spec],
        out_specs=out_spec,
        compiler_params=_params(("parallel", "parallel", "arbitrary"), VMEM_BIG),
    )(a, g)


def _norm_fwd(x, gain, tt):
    t = x.shape[0]

    def body(x_ref, g_ref, h_ref):
        _, n = _rms_stats(x_ref[...])
        h_ref[...] = (n * g_ref[...]).astype(BF16)

    return pl.pallas_call(
        body, name="norm_fwd", out_shape=jax.ShapeDtypeStruct((t, D), BF16), grid=(t // tt,),
        in_specs=[pl.BlockSpec((tt, D), lambda i: (i, 0)), pl.BlockSpec((1, D), lambda i: (0, 0))],
        out_specs=pl.BlockSpec((tt, D), lambda i: (i, 0)),
        compiler_params=_params(("parallel",)),
    )(x, gain)


def _lru_gates(xc, wa_ref, wx_ref, ba, bx, lam):
    xcb = xc.astype(BF16)
    ra = jnp.concatenate([_dot(xcb[:, n * HD:(n + 1) * HD], wa_ref[n]) for n in range(NH)], axis=1) + ba
    ix = jnp.concatenate([_dot(xcb[:, n * HD:(n + 1) * HD], wx_ref[n]) for n in range(NH)], axis=1) + bx
    r = _sig(ra)
    ig = _sig(ix)
    z = -lam
    sp = jnp.maximum(z, 0.0) + jnp.log1p(jnp.exp(-jnp.abs(z)))
    log_a = -LRU_C * r * sp
    a = jnp.exp(log_a)
    z2 = 2.0 * log_a
    series = -z2 * (1.0 + z2 * (0.5 + z2 * (1.0 / 6.0 + z2 * (1.0 / 24.0))))
    om = jnp.where(z2 > -0.02, series, 1.0 - jnp.exp(z2))
    mult = jnp.sqrt(om)
    return xcb, r, ig, sp, a, mult


def _mixer_a_fwd(p4, cw, cb, wa, wx, ba, bx, lam, tt):
    t = p4.shape[1]

    def body(p_ref, cw_ref, cb_ref, wa_ref, wx_ref, ba_ref, bx_ref, lam_ref, ya_ref, h_ref, halo, hc):
        i = pl.program_id(0)

        @pl.when(i == 0)
        def _():
            halo[...] = jnp.zeros((8, D), F32)
            hc[...] = jnp.zeros((1, D), F32)

        xa = p_ref[:, 0:D]
        ga = p_ref[:, D:2 * D]
        xe = jnp.concatenate([halo[...], xa], axis=0)
        xc = (cb_ref[...] + cw_ref[3:4, :] * xe
              + sum(cw_ref[3 - s:4 - s, :] * pltpu.roll(xe, s, 0) for s in (1, 2, 3)))[8:, :]
        halo[...] = xa[tt - 8:, :]
        _, _, ig, _, a, mult = _lru_gates(xc, wa_ref, wx_ref, ba_ref[...], bx_ref[...], lam_ref[...])
        u = mult * ig * xc
        h = _scan_down(a, u, hc[...])
        hc[...] = h[tt - 1:tt, :]
        h_ref[...] = h
        ya_ref[...] = (h * _gelu(ga)).astype(BF16)

    full = lambda shape: pl.BlockSpec(shape, lambda i: (0,) * len(shape))
    return pl.pallas_call(
        body, name="mixer_a_fwd",
        out_shape=(jax.ShapeDtypeStruct((t, D), BF16), jax.ShapeDtypeStruct((t, D), F32)),
        grid=(t // tt,),
        in_specs=[pl.BlockSpec((None, tt, 2 * D), lambda i: (SLOT_A, i, 0)),
                  full((4, D)), full((1, D)), full((NH, HD, HD)), full((NH, HD, HD)),
                  full((1, D)), full((1, D)), full((1, D))],
        out_specs=(pl.BlockSpec((tt, D), lambda i: (i, 0)), pl.BlockSpec((tt, D), lambda i: (i, 0))),
        scratch_shapes=[pltpu.VMEM((8, D), F32), pltpu.VMEM((1, D), F32)],
        compiler_params=_params(("arbitrary",), VMEM_BIG),
    )(p4, cw, cb, wa, wx, ba, bx, lam)


def _chunk_masks(tt):
    row = lax.broadcasted_iota(jnp.int32, (tt, tt), 0)
    col = lax.broadcasted_iota(jnp.int32, (tt, tt), 1)
    same = jnp.right_shift(row, 5) == jnp.right_shift(col, 5)
    return same & (col <= row)


def _hg_head_fwd(q, fz, lbh):
    sg = _sig(fz)
    sgn = _sig(-fz)
    f = lbh + (1.0 - lbh) * sg
    logf = jnp.log(f)
    k = (1.0 - lbh) * sgn
    g = _chunk_cumsum(logf)
    gu = _chunk_last(g) - g
    eg = jnp.exp(g)
    eng = jnp.exp(-g)
    egu = jnp.exp(gu)
    qt = q * eg
    kt = k * eng
    kd = k * egu
    return sg, sgn, f, k, g, eg, eng, egu, qt, kt, kd


def _lb_of(logits_ref):
    return _sig(logits_ref[0:1, :] - logits_ref[1:2, :])


def _hgrn2_fwd(p4, logits, gnorm, tt):
    t = p4.shape[1]
    nc = tt // CH

    def body(p_ref, lg_ref, gn_ref, yb_ref, o_ref, ss_ref, st):
        i = pl.program_id(0)

        @pl.when(i == 0)
        def _():
            st[...] = jnp.zeros((NH, HD, HD), F32)

        low = _chunk_masks(tt)
        lb = _lb_of(lg_ref)
        heads = [slice(h * HD, (h + 1) * HD) for h in range(NH)]
        _, _, _, _, g, _, _, _, qt, kt, kd = _hg_head_fwd(p_ref[0, :, 0:D], p_ref[0, :, D:2 * D], lb)
        qtb, ktb, kdb, vb = qt.astype(BF16), kt.astype(BF16), kd.astype(BF16), p_ref[1, :, 0:D].astype(BF16)
        decs = [jnp.exp(g[c * CH + CH - 1:c * CH + CH, :]) for c in range(nc)]
        o_in = []
        for hs in heads:
            att = jnp.where(low, _dot_nt(qtb[:, hs], ktb[:, hs]), 0.0)
            o_in.append(_dot(att.astype(BF16), vb[:, hs]))
        s_t = [st[h] for h in range(NH)]
        pieces = [[None] * nc for _ in range(NH)]
        for c in range(nc):
            sl = slice(c * CH, (c + 1) * CH)
            for h, hs in enumerate(heads):
                s_bf = s_t[h].astype(BF16)
                ss_ref[c, h] = s_bf
                pieces[h][c] = o_in[h][sl] + _dot_nt(qtb[sl, hs], s_bf)
                s_t[h] = s_t[h] * decs[c][:, hs] + _dot_tn(vb[sl, hs], kdb[sl, hs])
        for h, hs in enumerate(heads):
            st[h] = s_t[h]
            o = jnp.concatenate(pieces[h], axis=0)
            _, n = _rms_stats(o)
            og = p_ref[1, :, D + h * HD:D + (h + 1) * HD]
            o_ref[:, hs] = o
            yb_ref[:, hs] = (n * gn_ref[:, hs] * (og * _sig(og))).astype(BF16)

    return pl.pallas_call(
        body, name="hgrn2_fwd",
        out_shape=(jax.ShapeDtypeStruct((t, D), BF16), jax.ShapeDtypeStruct((t, D), F32),
                   jax.ShapeDtypeStruct((t // CH, NH, HD, HD), BF16)),
        grid=(t // tt,),
        in_specs=[pl.BlockSpec((2, tt, 2 * D), lambda i: (0, i, 0)),
                  pl.BlockSpec((2, D), lambda i: (0, 0)), pl.BlockSpec((1, D), lambda i: (0, 0))],
        out_specs=(pl.BlockSpec((tt, D), lambda i: (i, 0)), pl.BlockSpec((tt, D), lambda i: (i, 0)),
                   pl.BlockSpec((nc, NH, HD, HD), lambda i: (i, 0, 0, 0))),
        scratch_shapes=[pltpu.VMEM((NH, HD, HD), F32)],
        compiler_params=_params(("arbitrary",), VMEM_BIG),
    )(p4, logits, gnorm)


def _mid_fwd(ya, yb, p4, x, wa, wb, wo, g_pm, g_pf, tt):
    t = x.shape[0]

    def body(ya_ref, yb_ref, gt_ref, x_ref, wa_ref, wb_ref, wo_ref, gpm_ref, gpf_ref,
             za_ref, zb_ref, mix_ref, m2_ref, x1_ref, h2_ref):
        za = _dot(ya_ref[...], wa_ref[...])
        zb = _dot(yb_ref[...], wb_ref[...])
        mix = _sig(gt_ref[:, 0:D]) * za + _sig(gt_ref[:, D:2 * D]) * zb
        mixb = mix.astype(BF16)
        m2 = _dot(mixb, wo_ref[...])
        _, n2 = _rms_stats(m2)
        x1 = x_ref[...] + n2 * gpm_ref[...]
        _, n1 = _rms_stats(x1)
        za_ref[...] = za.astype(BF16)
        zb_ref[...] = zb.astype(BF16)
        mix_ref[...] = mixb
        m2_ref[...] = m2
        x1_ref[...] = x1
        h2_ref[...] = (n1 * gpf_ref[...]).astype(BF16)

    row = lambda dt: jax.ShapeDtypeStruct((t, D), dt)
    tile = pl.BlockSpec((tt, D), lambda i: (i, 0))
    wsp = pl.BlockSpec((D, D), lambda i: (0, 0))
    vec = pl.BlockSpec((1, D), lambda i: (0, 0))
    return pl.pallas_call(
        body, name="mid_fwd",
        out_shape=(row(BF16), row(BF16), row(BF16), row(F32), row(F32), row(BF16)),
        grid=(t // tt,),
        in_specs=[tile, tile, pl.BlockSpec((None, tt, 2 * D), lambda i: (SLOT_G, i, 0)), tile,
                  wsp, wsp, wsp, vec, vec],
        out_specs=(tile,) * 6,
        compiler_params=_params(("parallel",), VMEM_BIG),
    )(ya, yb, p4, x, wa, wb, wo, g_pm, g_pf)


def _ffn_act_fwd(up_pre, cfw, cfb, tt):
    t = up_pre.shape[0]

    def body(u_ref, halo_ref, w_ref, b_ref, y_ref, uv_ref, gl_ref, dgl_ref):
        i = pl.program_id(0)
        halves = []
        for c0 in (0, DFF):
            cs = slice(c0, c0 + DFF)
            xe = jnp.concatenate([jnp.where(i > 0, halo_ref[:, cs], 0.0), u_ref[:, cs]], axis=0)
            up = (b_ref[:, cs] + w_ref[2:3, cs] * xe + w_ref[1:2, cs] * pltpu.roll(xe, 1, 0)
                  + w_ref[0:1, cs] * pltpu.roll(xe, 2, 0))
            halves.append(up[8:, :])
        gl, dgl = _gelu_and_grad(halves[0])
        y_ref[...] = (gl * halves[1]).astype(BF16)
        uv_ref[...] = halves[1].astype(BF16)
        gl_ref[...] = gl.astype(BF16)
        dgl_ref[...] = dgl.astype(BF16)

    hb = tt // 8
    half = pl.BlockSpec((tt, DFF), lambda i: (i, 0))
    return pl.pallas_call(
        body, name="ffn_act_fwd",
        out_shape=(jax.ShapeDtypeStruct((t, DFF), BF16),) * 4,
        grid=(t // tt,),
        in_specs=[pl.BlockSpec((tt, DUP), lambda i: (i, 0)),
                  pl.BlockSpec((8, DUP), lambda i: (jnp.maximum(i * hb - 1, 0), 0)),
                  pl.BlockSpec((3, DUP), lambda i: (0, 0)), pl.BlockSpec((1, DUP), lambda i: (0, 0))],
        out_specs=(half,) * 4,
        compiler_params=_params(("parallel",), VMEM_BIG),
    )(up_pre, up_pre, cfw, cfb)


def _down_loss(y, wdn, x1, tgt, g_post, tt):
    t = x1.shape[0]

    def body(y_ref, w_ref, x1_ref, t_ref, g_ref, dx2_ref, dm3_ref, lossv_ref, dg_ref):
        i = pl.program_id(0)
        m3 = _dot(y_ref[...], w_ref[...])
        r, n3 = _rms_stats(m3)
        g = g_ref[...]
        e = x1_ref[...] + n3 * g - t_ref[...]
        dx2 = e * (1.0 / D)
        dx2_ref[...] = dx2
        dm3_ref[...] = _rms_bwd(dx2 * g, n3, r).astype(BF16)
        lv = jnp.sum(e * e, axis=0, keepdims=True)
        dgv = jnp.sum(dx2 * n3, axis=0, keepdims=True)

        @pl.when(i == 0)
        def _():
            lossv_ref[...] = lv
            dg_ref[...] = dgv

        @pl.when(i > 0)
        def _():
            lossv_ref[...] += lv
            dg_ref[...] += dgv

    tile = pl.BlockSpec((tt, D), lambda i: (i, 0))
    vec = pl.BlockSpec((1, D), lambda i: (0, 0))
    return pl.pallas_call(
        body, name="down_loss",
        out_shape=(jax.ShapeDtypeStruct((t, D), F32), jax.ShapeDtypeStruct((t, D), BF16),
                   jax.ShapeDtypeStruct((1, D), F32), jax.ShapeDtypeStruct((1, D), F32)),
        grid=(t // tt,),
        in_specs=[pl.BlockSpec((tt, DFF), lambda i: (i, 0)), pl.BlockSpec((DFF, D), lambda i: (0, 0)),
                  tile, tile, vec],
        out_specs=(tile, tile, vec, vec),
        compiler_params=_params(("arbitrary",), VMEM_BIG),
    )(y, wdn, x1, tgt, g_post)


def _ffn_act_bwd(dm3, wdn, up_pre, uv, gl, dgl, cfw, tt):
    t = up_pre.shape[0]
    nt = t // tt

    def body(dm_ref, dmn_ref, w_ref, u_ref, uv_ref, uvn_ref, gl_ref, gln_ref, dgl_ref, dgln_ref, cw_ref,
             du_ref, dcw_ref, dcb_ref):
        i = pl.program_id(0)
        n = tt + 8
        next_live = jnp.where(i < nt - 1, 1.0, 0.0)
        dy = jnp.concatenate([_dot_nt(dm_ref[...], w_ref[...]),
                              _dot_nt(dmn_ref[...], w_ref[...])[0:8, :] * next_live], axis=0)
        ext = lambda ref, nref: jnp.concatenate([ref[...].astype(F32), nref[...].astype(F32)[0:8, :]], axis=0)
        ds = (dy * ext(uv_ref, uvn_ref) * ext(dgl_ref, dgln_ref), dy * ext(gl_ref, gln_ref))
        dcw_parts, dcb_parts = [], []
        for hh, c0 in enumerate((0, DFF)):
            cs = slice(c0, c0 + DFF)
            dd = ds[hh]
            d1 = pltpu.roll(dd, n - 1, 0)
            d2 = pltpu.roll(dd, n - 2, 0)
            du_ref[:, cs] = (cw_ref[2:3, cs] * dd + cw_ref[1:2, cs] * d1 + cw_ref[0:1, cs] * d2)[0:tt, :].astype(BF16)
            x = u_ref[:, cs]
            dcw_parts.append(jnp.concatenate(
                [jnp.sum(dk[0:tt, :] * x, axis=0, keepdims=True) for dk in (d2, d1, dd)], axis=0))
            dcb_parts.append(jnp.sum(dd[0:tt, :], axis=0, keepdims=True))
        dcw = jnp.concatenate(dcw_parts, axis=1)
        dcb = jnp.concatenate(dcb_parts, axis=1)

        @pl.when(i == 0)
        def _():
            dcw_ref[...] = dcw
            dcb_ref[...] = dcb

        @pl.when(i > 0)
        def _():
            dcw_ref[...] += dcw
            dcb_ref[...] += dcb

    half = pl.BlockSpec((tt, DFF), lambda i: (i, 0))
    half_next = pl.BlockSpec((16, DFF), lambda i: (jnp.minimum((i + 1) * (tt // 16), t // 16 - 1), 0))
    return pl.pallas_call(
        body, name="ffn_act_bwd",
        out_shape=(jax.ShapeDtypeStruct((t, DUP), BF16), jax.ShapeDtypeStruct((3, DUP), F32),
                   jax.ShapeDtypeStruct((1, DUP), F32)),
        grid=(nt,),
        in_specs=[pl.BlockSpec((tt, D), lambda i: (i, 0)),
                  pl.BlockSpec((16, D), lambda i: (jnp.minimum((i + 1) * (tt // 16), t // 16 - 1), 0)),
                  pl.BlockSpec((DFF, D), lambda i: (0, 0)),
                  pl.BlockSpec((tt, DUP), lambda i: (i, 0)),
                  half, half_next, half, half_next, half, half_next,
                  pl.BlockSpec((3, DUP), lambda i: (0, 0))],
        out_specs=(pl.BlockSpec((tt, DUP), lambda i: (i, 0)), pl.BlockSpec((3, DUP), lambda i: (0, 0)),
                   pl.BlockSpec((1, DUP), lambda i: (0, 0))),
        compiler_params=_params(("arbitrary",), VMEM_BIG),
    )(dm3, dm3, wdn, up_pre, uv, uv, gl, gl, dgl, dgl, cfw)


def _mid_bwd(dh2, dx2, x1, m2, za, zb, p4, wa, wb, wo, g_pm, g_pf, tt):
    t = x1.shape[0]

    def body(dh2_ref, dx2_ref, x1_ref, m2_ref, za_ref, zb_ref, gt_ref, wa_ref, wb_ref, wo_ref, gpm_ref, gpf_ref,
             dx1_ref, dm2_ref, dza_ref, dzb_ref, dya_ref, dyb_ref, dp_ref, dgpm_ref, dgpf_ref):
        i = pl.program_id(0)
        r1, n1 = _rms_stats(x1_ref[...])
        dh2 = dh2_ref[...]
        dx1 = dx2_ref[...] + _rms_bwd(dh2 * gpf_ref[...], n1, r1)
        r2, n2 = _rms_stats(m2_ref[...])
        dm2 = _rms_bwd(dx1 * gpm_ref[...], n2, r2).astype(BF16)
        dmix = _dot_nt(dm2, wo_ref[...])
        sa = _sig(gt_ref[:, 0:D])
        sb = _sig(gt_ref[:, D:2 * D])
        dza = (dmix * sa).astype(BF16)
        dzb = (dmix * sb).astype(BF16)
        dp_ref[:, 0:D] = (dmix * za_ref[...].astype(F32) * sa * (1.0 - sa)).astype(BF16)
        dp_ref[:, D:2 * D] = (dmix * zb_ref[...].astype(F32) * sb * (1.0 - sb)).astype(BF16)
        dx1_ref[...] = dx1
        dm2_ref[...] = dm2
        dza_ref[...] = dza
        dzb_ref[...] = dzb
        dya_ref[...] = _dot_nt(dza, wa_ref[...])
        dyb_ref[...] = _dot_nt(dzb, wb_ref[...])
        dgpf = jnp.sum(dh2 * n1, axis=0, keepdims=True)
        dgpm = jnp.sum(dx1 * n2, axis=0, keepdims=True)

        @pl.when(i == 0)
        def _():
            dgpf_ref[...] = dgpf
            dgpm_ref[...] = dgpm

        @pl.when(i > 0)
        def _():
            dgpf_ref[...] += dgpf
            dgpm_ref[...] += dgpm

    row = lambda dt: jax.ShapeDtypeStruct((t, D), dt)
    tile = pl.BlockSpec((tt, D), lambda i: (i, 0))
    wsp = pl.BlockSpec((D, D), lambda i: (0, 0))
    vec = pl.BlockSpec((1, D), lambda i: (0, 0))
    gates = pl.BlockSpec((None, tt, 2 * D), lambda i: (SLOT_G, i, 0))
    return pl.pallas_call(
        body, name="mid_bwd",
        out_shape=(row(F32), row(BF16), row(BF16), row(BF16), row(F32), row(F32),
                   jax.ShapeDtypeStruct((NCHIP, t, 2 * D), BF16),
                   jax.ShapeDtypeStruct((1, D), F32), jax.ShapeDtypeStruct((1, D), F32)),
        grid=(t // tt,),
        in_specs=[tile, tile, tile, tile, tile, tile, gates, wsp, wsp, wsp, vec, vec],
        out_specs=(tile, tile, tile, tile, tile, tile, gates, vec, vec),
        compiler_params=_params(("arbitrary",), VMEM_BIG),
    )(dh2, dx2, x1, m2, za, zb, p4, wa, wb, wo, g_pm, g_pf)


def _hgrn2_bwd(p4, o_all, ss, dyb, dp4, logits, gnorm, p_early, tt):
    t = p4.shape[1]
    nt = t // tt
    nc = tt // CH
    ne = len(p_early)

    def body(p_ref, o_ref, ss_ref, dyb_ref, dp_in, lg_ref, gn_ref, *rest):
        del dp_in
        pe = rest[:ne]
        dp_ref, dlb_ref, dgn_ref = rest[ne:ne + 3]
        qe = rest[ne + 3:2 * ne + 3]
        dst, ssem, rsem = rest[2 * ne + 3:]
        i = pl.program_id(0)
        sends, arrive = _exchange_copies(qe, [(lambda chip, r=r: r.at[chip]) for r in pe], ssem, rsem)

        @pl.when(i == 0)
        def _():
            dst[...] = jnp.zeros((NH, HD, HD), F32)
            for cp in sends:
                cp.start()

        low = _chunk_masks(tt)
        lb = _lb_of(lg_ref)
        heads = [slice(h * HD, (h + 1) * HD) for h in range(NH)]
        sg, sgn, f, k, g, eg, eng, egu, qt, kt, kd = _hg_head_fwd(p_ref[0, :, 0:D], p_ref[0, :, D:2 * D], lb)
        qtb, ktb, kdb, vb = qt.astype(BF16), kt.astype(BF16), kd.astype(BF16), p_ref[1, :, 0:D].astype(BF16)
        decs = [jnp.exp(g[c * CH + CH - 1:c * CH + CH, :]) for c in range(nc)]
        og = p_ref[1, :, D:2 * D]
        so = _sig(og)
        dyb = dyb_ref[...]
        dob = dyb * (og * so)
        rn = [_rms_stats(o_ref[:, hs]) for hs in heads]
        r_all = jnp.concatenate([jnp.broadcast_to(r, (tt, HD)) for r, _ in rn], axis=1)
        n_all = jnp.concatenate([n for _, n in rn], axis=1)
        gd = dob * gn_ref[...]
        proj = jnp.concatenate(
            [jnp.broadcast_to(jnp.mean(gd[:, hs] * n_all[:, hs], axis=-1, keepdims=True), (tt, HD)) for hs in heads],
            axis=1)
        dob_ = (r_all * (gd - n_all * proj)).astype(BF16)
        dog = dyb * (n_all * gn_ref[...]) * (so * (1.0 + og * (1.0 - so)))
        dgn = jnp.sum(dob * n_all, axis=0, keepdims=True)
        dv_in, dqt_in, dkt_h = [], [], []
        for hs in heads:
            att = jnp.where(low, _dot_nt(qtb[:, hs], ktb[:, hs]), 0.0).astype(BF16)
            d_att = jnp.where(low, _dot_nt(dob_[:, hs], vb[:, hs]), 0.0).astype(BF16)
            dv_in.append(_dot_tn(att, dob_[:, hs]))
            dqt_in.append(_dot(d_att, ktb[:, hs]))
            dkt_h.append(_dot_tn(d_att, qtb[:, hs]))
        ds_t = [dst[h] for h in range(NH)]
        dv_p = [[None] * NH for _ in range(nc)]
        dqt_p = [[None] * NH for _ in range(nc)]
        dkd_p = [[None] * NH for _ in range(nc)]
        dgl_p = [[None] * NH for _ in range(nc)]
        for c in reversed(range(nc)):
            sl = slice(c * CH, (c + 1) * CH)
            for h, hs in enumerate(heads):
                s_prev = ss_ref[c, h]
                ds_bf = ds_t[h].astype(BF16)
                dec = decs[c][:, hs]
                dv_p[c][h] = dv_in[h][sl] + _dot_nt(kdb[sl, hs], ds_bf)
                dqt_p[c][h] = dqt_in[h][sl] + _dot(dob_[sl, hs], s_prev)
                dkd_p[c][h] = _dot(vb[sl, hs], ds_bf)
                ddec = jnp.sum(s_prev.astype(F32) * ds_t[h], axis=0, keepdims=True)
                dgl_p[c][h] = jnp.broadcast_to(ddec * dec, (CH, HD))
                ds_t[h] = ds_t[h] * dec + _dot_tn(dob_[sl, hs], qtb[sl, hs])
        for h in range(NH):
            dst[h] = ds_t[h]
        whole = lambda parts: jnp.concatenate([jnp.concatenate(row, axis=1) for row in parts], axis=0)
        dv, dqt, dkd, dgl = whole(dv_p), whole(dqt_p), whole(dkd_p), whole(dgl_p)
        dkt = jnp.concatenate(dkt_h, axis=1)
        dq = dqt * eg
        dk = dkt * eng + dkd * egu
        dg = dqt * qt - dkt * kt
        dgu = dkd * kd
        dlogf = _chunk_revcumsum(dg - dgu) + _chunk_total(dgu) + dgl
        common = sgn * (dlogf / f - dk)
        dfz = (1.0 - lb) * sg * common
        dlb = jnp.sum(common, axis=0, keepdims=True)
        dp_ref[0, :, 0:D] = dq.astype(BF16)
        dp_ref[0, :, D:2 * D] = dfz.astype(BF16)
        dp_ref[1, :, 0:D] = dv.astype(BF16)
        dp_ref[1, :, D:2 * D] = dog.astype(BF16)

        @pl.when(i == 0)
        def _():
            dlb_ref[0:1, :] = dlb
            dgn_ref[...] = dgn

        @pl.when(i > 0)
        def _():
            dlb_ref[0:1, :] += dlb
            dgn_ref[...] += dgn

        @pl.when(i == nt - 1)
        def _():
            d0 = dlb_ref[0:1, :] * lb * (1.0 - lb)
            dlb_ref[0:1, :] = d0
            dlb_ref[1:2, :] = -d0
            for cp in arrive:
                cp.wait_recv()
            for cp in sends:
                cp.wait_send()

    rev = lambda i: nt - 1 - i
    vec = pl.BlockSpec((1, D), lambda i: (0, 0))
    any_spec = pl.BlockSpec(memory_space=pl.ANY)
    outs = pl.pallas_call(
        body, name="hgrn2_bwd",
        out_shape=(jax.ShapeDtypeStruct(dp4.shape, BF16), jax.ShapeDtypeStruct((2, D), F32),
                   jax.ShapeDtypeStruct((1, D), F32)) + tuple(jax.ShapeDtypeStruct(a.shape, BF16) for a in p_early),
        grid=(nt,),
        in_specs=[pl.BlockSpec((2, tt, 2 * D), lambda i: (0, rev(i), 0)),
                  pl.BlockSpec((tt, D), lambda i: (rev(i), 0)),
                  pl.BlockSpec((nc, NH, HD, HD), lambda i: (rev(i), 0, 0, 0)),
                  pl.BlockSpec((tt, D), lambda i: (rev(i), 0)),
                  any_spec,
                  pl.BlockSpec((2, D), lambda i: (0, 0)), vec] + [any_spec] * ne,
        out_specs=(pl.BlockSpec((2, tt, 2 * D), lambda i: (0, rev(i), 0)),
                   pl.BlockSpec((2, D), lambda i: (0, 0)), vec) + (any_spec,) * ne,
        scratch_shapes=[pltpu.VMEM((NH, HD, HD), F32), pltpu.SemaphoreType.DMA((3 * ne,)),
                        pltpu.SemaphoreType.DMA((3 * ne,))],
        input_output_aliases={4: 0},
        compiler_params=pltpu.CompilerParams(dimension_semantics=("arbitrary",), vmem_limit_bytes=VMEM_BIG,
                                             has_side_effects=True),
    )(p4, o_all, ss, dyb, dp4, logits, gnorm, *p_early)
    return outs[0], outs[1], outs[2], list(outs[3:])


def _mixer_a_bwd(p4, hseq, dya, dp4, cw, cb, wa, wx, ba, bx, lam, tt):
    t = p4.shape[1]
    nt = t // tt
    steps = tt.bit_length() - 1

    def body(p_ref, ph_ref, h_ref, hh_ref, dya_ref, dp_in, cw_ref, cb_ref, wa_ref, wx_ref, ba_ref, bx_ref, lam_ref,
             dp_ref, dcw_ref, dcb_ref, dwa_ref, dwx_ref, dba_ref, dbx_ref, dlam_ref,
             dnext, dhc, afc):
        del dp_in
        i = pl.program_id(0)
        first_tile = i == nt - 1

        @pl.when(i == 0)
        def _():
            dnext[...] = jnp.zeros((8, D), F32)
            dhc[...] = jnp.zeros((1, D), F32)
            afc[...] = jnp.zeros((1, D), F32)

        xa = p_ref[:, 0:D]
        ga = p_ref[:, D:2 * D]
        xe = jnp.concatenate([jnp.where(first_tile, 0.0, ph_ref[:, 0:D]), xa], axis=0)
        xs = [xe[8:, :]] + [pltpu.roll(xe, s, 0)[8:, :] for s in (1, 2, 3)]
        xc = cb_ref[...] + sum(cw_ref[3 - s:4 - s, :] * xs[s] for s in range(4))
        lam = lam_ref[...]
        xcb, r, ig, sp, a, mult = _lru_gates(xc, wa_ref, wx_ref, ba_ref[...], bx_ref[...], lam)
        h = h_ref[...]
        gl, dgl = _gelu_and_grad(ga)
        dya = dya_ref[...]
        dga = dya * h * dgl
        rows = lax.broadcasted_iota(jnp.int32, (tt, 1), 0)
        a_next = jnp.where(rows == tt - 1, afc[...], pltpu.roll(a, tt - 1, 0))
        dh = _scan_up(a_next, dya * gl, dhc[...])
        dhc[...] = dh[0:1, :]
        afc[...] = a[0:1, :]
        h_prev = jnp.where(rows == 0, jnp.where(first_tile, 0.0, hh_ref[7:8, :]), pltpu.roll(h, 1, 0))
        da = dh * h_prev
        dmult = dh * ig * xc
        di = dh * mult * xc
        dlog_a = da * a - dmult * a * a / mult
        dr = dlog_a * (-LRU_C * sp)
        dsp = jnp.sum(dlog_a * (-LRU_C * r), axis=0, keepdims=True)
        dra = dr * r * (1.0 - r)
        dix = di * ig * (1.0 - ig)
        drab = dra.astype(BF16)
        dixb = dix.astype(BF16)
        dxc_lin = []
        dwa_new = []
        dwx_new = []
        for n in range(NH):
            cs = slice(n * HD, (n + 1) * HD)
            dxc_lin.append(_dot_nt(drab[:, cs], wa_ref[n]) + _dot_nt(dixb[:, cs], wx_ref[n]))
            dwa_new.append(_dot_tn(xcb[:, cs], drab[:, cs]))
            dwx_new.append(_dot_tn(xcb[:, cs], dixb[:, cs]))
        dxc = dh * mult * ig + jnp.concatenate(dxc_lin, axis=1)
        de = jnp.concatenate([dxc, dnext[...]], axis=0)
        dxa = (cw_ref[3:4, :] * de
               + sum(cw_ref[3 - s:4 - s, :] * pltpu.roll(de, tt + 8 - s, 0) for s in (1, 2, 3)))[0:tt, :]
        dnext[...] = dxc[0:8, :]
        dp_ref[:, 0:D] = dxa.astype(BF16)
        dp_ref[:, D:2 * D] = dga.astype(BF16)
        dcw = jnp.concatenate(
            [jnp.sum(dxc * xs[3 - k], axis=0, keepdims=True) for k in range(4)], axis=0)
        dcb = jnp.sum(dxc, axis=0, keepdims=True)
        dba = jnp.sum(dra, axis=0, keepdims=True)
        dbx = jnp.sum(dix, axis=0, keepdims=True)
        dlam = dsp * (-_sig(-lam))

        @pl.when(i == 0)
        def _():
            dcw_ref[...] = dcw
            dcb_ref[...] = dcb
            dba_ref[...] = dba
            dbx_ref[...] = dbx
            dlam_ref[...] = dlam
            for n in range(NH):
                dwa_ref[n] = dwa_new[n]
                dwx_ref[n] = dwx_new[n]

        @pl.when(i > 0)
        def _():
            dcw_ref[...] += dcw
            dcb_ref[...] += dcb
            dba_ref[...] += dba
            dbx_ref[...] += dbx
            dlam_ref[...] += dlam
            for n in range(NH):
                dwa_ref[n] += dwa_new[n]
                dwx_ref[n] += dwx_new[n]

    rev = lambda i: nt - 1 - i
    hb = tt // 8
    full = lambda shape: pl.BlockSpec(shape, lambda i: (0,) * len(shape))
    vecs = jax.ShapeDtypeStruct((1, D), F32)
    blk = jax.ShapeDtypeStruct((NH, HD, HD), F32)
    return pl.pallas_call(
        body, name="mixer_a_bwd",
        out_shape=(jax.ShapeDtypeStruct(dp4.shape, BF16), jax.ShapeDtypeStruct((4, D), F32), vecs, blk, blk,
                   vecs, vecs, vecs),
        grid=(nt,),
        in_specs=[pl.BlockSpec((None, tt, 2 * D), lambda i: (SLOT_A, rev(i), 0)),
                  pl.BlockSpec((None, 8, 2 * D), lambda i: (SLOT_A, jnp.maximum(rev(i) * hb - 1, 0), 0)),
                  pl.BlockSpec((tt, D), lambda i: (rev(i), 0)),
                  pl.BlockSpec((8, D), lambda i: (jnp.maximum(rev(i) * hb - 1, 0), 0)),
                  pl.BlockSpec((tt, D), lambda i: (rev(i), 0)),
                  pl.BlockSpec(memory_space=pl.ANY),
                  full((4, D)), full((1, D)), full((NH, HD, HD)), full((NH, HD, HD)),
                  full((1, D)), full((1, D)), full((1, D))],
        out_specs=(pl.BlockSpec((None, tt, 2 * D), lambda i: (SLOT_A, rev(i), 0)),
                   full((4, D)), full((1, D)), full((NH, HD, HD)), full((NH, HD, HD)),
                   full((1, D)), full((1, D)), full((1, D))),
        scratch_shapes=[pltpu.VMEM((8, D), F32), pltpu.VMEM((1, D), F32), pltpu.VMEM((1, D), F32)],
        input_output_aliases={5: 0},
        compiler_params=_params(("arbitrary",), VMEM_BIG),
    )(p4, p4, hseq, hseq, dya, dp4, cw, cb, wa, wx, ba, bx, lam)


def _norm_bwd(dh1, dx1, x, gain, tt):
    t = x.shape[0]

    def body(dh_ref, dx1_ref, x_ref, g_ref, dx_ref, dg_ref):
        i = pl.program_id(0)
        r, n = _rms_stats(x_ref[...])
        dh = dh_ref[...]
        dx_ref[...] = dx1_ref[...] + _rms_bwd(dh * g_ref[...], n, r)
        dgv = jnp.sum(dh * n, axis=0, keepdims=True)

        @pl.when(i == 0)
        def _():
            dg_ref[...] = dgv

        @pl.when(i > 0)
        def _():
            dg_ref[...] += dgv

    tile = pl.BlockSpec((tt, D), lambda i: (i, 0))
    vec = pl.BlockSpec((1, D), lambda i: (0, 0))
    return pl.pallas_call(
        body, name="norm_bwd",
        out_shape=(jax.ShapeDtypeStruct((t, D), F32), jax.ShapeDtypeStruct((1, D), F32)),
        grid=(t // tt,), in_specs=[tile, tile, tile, vec], out_specs=(tile, vec),
        compiler_params=_params(("arbitrary",)),
    )(dh1, dx1, x, gain)


def _local_step(x, tgt, w_in, stk_rest, conv_a_w, conv_f_w, small, cidx):
    t = x.shape[0]
    tt = min(256, t)
    tm = min(1024, t)
    tk = min(2048, t)
    wa_bf = small["lru_wa"].astype(BF16)
    wx_bf = small["lru_wx"].astype(BF16)

    h1 = _norm_fwd(x, small["norm_pre_mix"], tt)
    p4, stk_rest = _mm_in_gather(h1, w_in, stk_rest, REST, tm)
    w = dict(zip(REST, _gather_forward(stk_rest, REST)))
    w["w_in"] = w_in
    w_br_a = w["w_branch_a"].reshape(D, D)
    w_br_b = w["w_branch_b"].reshape(D, D)
    w_out = w["w_out"].reshape(D, D)
    w_down = w["w_down"].reshape(DFF, D)
    ya, hseq = _mixer_a_fwd(p4, conv_a_w, small["conv_a_b"], wa_bf, wx_bf, small["lru_ba"], small["lru_bx"],
                            small["lru_lambda"], tt)
    yb, o_all, ss = _hgrn2_fwd(p4, small["hg_lb_logits"], small["hg_norm_g"], tt)
    za, zb, mixb, m2, x1, h2 = _mid_fwd(ya, yb, p4, x, w_br_a, w_br_b, w_out, small["norm_post_mix"],
                                        small["norm_pre_ffn"], min(512, t))
    up_pre = _mm_nn_sharded(h2, w["w_up"], F32, tm, "mm_up")
    tf = min(128, t)
    y, uv, gl, dgl = _ffn_act_fwd(up_pre, conv_f_w, small["conv_f_b"], tf)
    dx2, dm3, lossv, d_norm_post_ffn = _down_loss(y, w_down, x1, tgt, small["norm_post_ffn"], min(512, t))

    d_w_down = _mm_tn(y, dm3, DFF // 2, D, tk, "mm_dw_down")
    dup_pre, d_conv_f_w, d_conv_f_b = _ffn_act_bwd(dm3, w_down, up_pre, uv, gl, dgl, conv_f_w, tf)
    d_w_up = _mm_tn(h2, dup_pre, D, SH_UP, tk, "mm_dw_up", stacked_out=True)
    dh2 = _mm_nt_sharded(dup_pre, w["w_up"], tm, "mm_dh2")
    dx1, dm2, dza, dzb, dya, dyb, dp4, d_norm_post_mix, d_norm_pre_ffn = _mid_bwd(
        dh2, dx2, x1, m2, za, zb, p4, w_br_a, w_br_b, w_out, small["norm_post_mix"], small["norm_pre_ffn"], tt)
    d_w_out = _mm_tn(mixb, dm2, D, D, tm, "mm_dw_out")
    d_w_br_a = _mm_tn(ya, dza, D, D, tm, "mm_dw_bra")
    d_w_br_b = _mm_tn(yb, dzb, D, D, tm, "mm_dw_brb")
    early = {"w_branch_a": d_w_br_a.reshape(NCHIP, SH_BR, D), "w_branch_b": d_w_br_b.reshape(NCHIP, SH_BR, D),
             "w_out": d_w_out.reshape(NCHIP, SH_BR, D), "w_up": d_w_up, "w_down": d_w_down.reshape(NCHIP, SH_DN, D)}
    rb, _ = _reduce_stage1(early, REST, (), "reduce_d2d_in_early")
    p_rest = [_sum_own_half(early[n], rb[n], cidx, "sum_half_" + n) for n in REST]
    dp4, d_lb, d_hg_norm_g, q_rest = _hgrn2_bwd(p4, o_all, ss, dyb, dp4, small["hg_lb_logits"], small["hg_norm_g"],
                                                p_rest, tt)
    dp4, d_conv_a_w, d_conv_a_b, d_lru_wa, d_lru_wx, d_lru_ba, d_lru_bx, d_lru_lambda = _mixer_a_bwd(
        p4, hseq, dya, dp4, conv_a_w, small["conv_a_b"], wa_bf, wx_bf, small["lru_ba"], small["lru_bx"],
        small["lru_lambda"], tt)
    d_w_in = _mm_tn(h1, dp4, D, SH_IN, tk, "mm_dw_in", stacked_slot_fn=_slot_of_chip, stacked_out=True)
    rb, _ = _reduce_stage1({"w_in": d_w_in}, ("w_in",), (), "reduce_d2d_in_w_in")
    p_w_in = _sum_own_half(d_w_in, rb["w_in"], cidx, "sum_half_w_in")
    dh1, q_w_in = _mm_dh1_exchange(dp4, w_in, p_w_in, tm)
    grad_x, d_norm_pre_mix = _norm_bwd(dh1, dx1, x, small["norm_pre_mix"], tt)

    smalls = {
        "norm_pre_mix": d_norm_pre_mix, "conv_a_b": d_conv_a_b, "lru_ba": d_lru_ba, "lru_bx": d_lru_bx,
        "lru_lambda": d_lru_lambda, "hg_lb_logits": d_lb, "hg_norm_g": d_hg_norm_g, "norm_post_mix": d_norm_post_mix,
        "norm_pre_ffn": d_norm_pre_ffn, "norm_post_ffn": d_norm_post_ffn, "lossv": lossv,
        "conv_a_w": d_conv_a_w, "lru_wa": d_lru_wa, "lru_wx": d_lru_wx,
        "conv_f_b": d_conv_f_b, "conv_f_w": d_conv_f_w,
    }
    p_big = dict(zip(REST, p_rest), w_in=p_w_in)
    q_big = dict(zip(REST, q_rest), w_in=q_w_in)
    return grad_x, p_big, q_big, smalls


BIG = ("w_in", "w_branch_a", "w_branch_b", "w_out", "w_up", "w_down")
BIG_SHAPE = {"w_in": (D, SH_IN), "w_branch_a": (SH_BR, D), "w_branch_b": (SH_BR, D), "w_out": (SH_BR, D),
             "w_up": (D, SH_UP), "w_down": (SH_DN, D)}
NBIG = len(BIG)
REST = BIG[1:]
VEC_ROWS = (("norm_pre_mix", 0, 1), ("conv_a_b", 1, 1), ("lru_ba", 2, 1), ("lru_bx", 3, 1), ("lru_lambda", 4, 1),
            ("hg_lb_logits", 5, 2), ("hg_norm_g", 7, 1), ("norm_post_mix", 8, 1), ("norm_pre_ffn", 9, 1),
            ("norm_post_ffn", 10, 1))
ROW_LOSS = 11
ROW_CONV_A = 12
S1_ROWS = 16
S2_ROWS = 8


def _place():
    x, y, c = lax.axis_index("x"), lax.axis_index("y"), lax.axis_index("c")
    chips = [(1 - x, y), (x, 1 - y), (1 - x, 1 - y)]
    return x, y, c, 2 * x + y, chips


def _remote(src, dst, ssem, rsem, dev):
    return pltpu.make_async_remote_copy(src_ref=src, dst_ref=dst, send_sem=ssem, recv_sem=rsem,
                                        device_id=dev, device_id_type=MESH)


def _hbm_call(body, name, ins, out_shapes, n_sems, aliases=None):
    any_spec = pl.BlockSpec(memory_space=pl.ANY)
    return pl.pallas_call(
        body, name=name, out_shape=tuple(out_shapes),
        in_specs=[any_spec] * len(ins), out_specs=tuple([any_spec] * len(out_shapes)),
        scratch_shapes=[pltpu.SemaphoreType.DMA((n,)) for n in n_sems],
        input_output_aliases=aliases or {},
        compiler_params=pltpu.CompilerParams(has_side_effects=True),
    )(*ins)


def _gather_copies(stk, names, ssem, rsem, fssem=None, frsem=None):
    x, y, c, j, chips = _place()
    sends, arrive, fwds, farrive = [], [], [], []
    for w, n in enumerate(names):
        hw = BIG_SHAPE[n][0] // 2
        mine = stk[w].at[j, pl.ds(c * hw, hw), :]
        for k, (cx, cy) in enumerate(chips):
            i = 3 * w + k
            got = stk[w].at[2 * cx + cy, pl.ds(c * hw, hw), :]
            other = stk[w].at[2 * cx + cy, pl.ds((1 - c) * hw, hw), :]
            sends.append(_remote(mine, mine, ssem.at[i], rsem.at[i], (cx, cy, c)))
            arrive.append(_remote(got, got, ssem.at[i], rsem.at[i], (cx, cy, c)))
            if fssem is not None:
                fwds.append(_remote(got, got, fssem.at[i], frsem.at[i], (x, y, 1 - c)))
                farrive.append(_remote(other, other, fssem.at[i], frsem.at[i], (x, y, 1 - c)))
    return sends, arrive, fwds, farrive


def _gather_first(stacked_w_in, conv_a_s, conv_f_s):
    ins = [stacked_w_in, conv_a_s, conv_f_s]
    out_shapes = [jax.ShapeDtypeStruct(stacked_w_in.shape, stacked_w_in.dtype)]
    out_shapes += [jax.ShapeDtypeStruct((NCHIP,) + a.shape, a.dtype) for a in (conv_a_s, conv_f_s)]

    def body(w_in, ca_src, cf_src, w_out, ca_dst, cf_dst, ssem, rsem, fssem, frsem, csend, crecv, lsem):
        del w_in
        x, y, c, j, chips = _place()
        conv = ((ca_src, ca_dst), (cf_src, cf_dst))
        locs = [pltpu.make_async_copy(src, dst.at[j], lsem.at[i]) for i, (src, dst) in enumerate(conv)]
        csends = [_remote(src, dst.at[j], csend.at[3 * i + k], crecv.at[3 * i + k], (cx, cy, c))
                  for i, (src, dst) in enumerate(conv) for k, (cx, cy) in enumerate(chips)]
        sends, arrive, fwds, farrive = _gather_copies([w_out], ("w_in",), ssem, rsem, fssem, frsem)
        for cp in locs + sends + csends:
            cp.start()
        for got, fwd in zip(arrive, fwds):
            got.wait_recv()
            fwd.start()
        for i, (_, dst) in enumerate(conv):
            for k, (cx, cy) in enumerate(chips):
                got = dst.at[2 * cx + cy]
                _remote(got, got, csend.at[3 * i + k], crecv.at[3 * i + k], (cx, cy, c)).wait_recv()
        for cp in farrive:
            cp.wait_recv()
        for cp in sends + fwds + csends:
            cp.wait_send()
        for cp in locs:
            cp.wait()

    return _hbm_call(body, "gather_first", ins, out_shapes, (3, 3, 3, 3, 6, 6, 2), aliases={0: 0})


def _gather_forward(stk, names):
    nw = len(names)

    def body(*refs):
        dst = refs[nw:2 * nw]
        ssem, rsem, fssem, frsem = refs[2 * nw:]
        _, _, fwds, farrive = _gather_copies(dst, names, ssem, rsem, fssem, frsem)
        for cp in fwds:
            cp.start()
        for cp in farrive:
            cp.wait_recv()
        for cp in fwds:
            cp.wait_send()

    out_shapes = [jax.ShapeDtypeStruct(a.shape, a.dtype) for a in stk]
    return _hbm_call(body, "gather_forward", stk, out_shapes, (3 * nw,) * 4, aliases={w: w for w in range(nw)})


def _exchange_copies(dst, pieces, ssem, rsem):
    x, y, c, j, chips = _place()
    sends, arrive = [], []
    for w in range(len(dst)):
        for k, (cx, cy) in enumerate(chips):
            i = 3 * w + k
            sends.append(_remote(pieces[w](2 * cx + cy), dst[w].at[j], ssem.at[i], rsem.at[i], (cx, cy, c)))
            got = dst[w].at[2 * cx + cy]
            arrive.append(_remote(got, got, ssem.at[i], rsem.at[i], (cx, cy, c)))
    return sends, arrive


def _reduce_stage1(big_g, names, smalls, name):
    nb = len(names)
    ins = [big_g[n] for n in names] + list(smalls)
    n_in = len(ins)
    halves = [BIG_SHAPE[n][0] // 2 for n in names]
    out_shapes = [jax.ShapeDtypeStruct((NCHIP, halves[w], BIG_SHAPE[n][1]), F32) for w, n in enumerate(names)]
    out_shapes += [jax.ShapeDtypeStruct(a.shape, F32) for a in smalls]

    def body(*refs):
        src, dst = refs[:n_in], refs[n_in:2 * n_in]
        ssem, rsem = refs[2 * n_in:]
        x, y, c, _, _ = _place()
        cps = []
        for w in range(n_in):
            s_ = src[w].at[:, pl.ds((1 - c) * halves[w], halves[w]), :] if w < nb else src[w]
            cp = _remote(s_, dst[w], ssem.at[w], rsem.at[w], (x, y, 1 - c))
            cp.start()
            cps.append(cp)
        for cp in cps:
            cp.wait()

    outs = _hbm_call(body, name, ins, out_shapes, (n_in, n_in))
    return dict(zip(names, outs[:nb])), outs[nb:]


def _reduce_stage2(ps1, ps2, ps3):
    ins = [ps1, ps2, ps3]
    h1, h2, h3 = S1_ROWS // 2, DUP // 2, D
    out_shapes = [jax.ShapeDtypeStruct((NCHIP, h1, D), F32), jax.ShapeDtypeStruct((NCHIP, S2_ROWS, h2), F32),
                  jax.ShapeDtypeStruct((NCHIP, h3, HD), F32)]

    def body(*refs):
        src, dst = refs[:3], refs[3:6]
        ssem, rsem = refs[6:]
        c = lax.axis_index("c")
        pieces = [lambda chip: src[0].at[pl.ds(c * h1, h1), :],
                  lambda chip: src[1].at[:, pl.ds(c * h2, h2)],
                  lambda chip: src[2].at[pl.ds(c * h3, h3), :]]
        sends, arrive = _exchange_copies(dst, pieces, ssem, rsem)
        for cp in sends:
            cp.start()
        for cp in arrive:
            cp.wait_recv()
        for cp in sends:
            cp.wait_send()

    return _hbm_call(body, "reduce_ici_small", ins, out_shapes, (9, 9))


def _reduce_stage3(f_big, fs1, fs2, fs3):
    ins = [f_big[n] for n in BIG] + [fs1, fs2, fs3]
    n_in = len(ins)
    halves = [BIG_SHAPE[n][0] // 2 for n in BIG]
    h1, h2, h3 = S1_ROWS // 2, DUP // 2, D
    out_shapes = [jax.ShapeDtypeStruct(BIG_SHAPE[n], F32) for n in BIG]
    out_shapes += [jax.ShapeDtypeStruct((S1_ROWS, D), F32), jax.ShapeDtypeStruct((S2_ROWS, DUP), F32),
                   jax.ShapeDtypeStruct((2 * D, HD), F32)]

    def body(*refs):
        dst = refs[n_in:2 * n_in]
        ssem, rsem = refs[2 * n_in:]
        x, y, c, _, _ = _place()

        def place(w, which):
            if w < NBIG:
                return dst[w].at[pl.ds(which * halves[w], halves[w]), :]
            if w == NBIG:
                return dst[w].at[pl.ds(which * h1, h1), :]
            if w == NBIG + 1:
                return dst[w].at[:, pl.ds(which * h2, h2)]
            return dst[w].at[pl.ds(which * h3, h3), :]

        cps = [_remote(place(w, c), place(w, c), ssem.at[w], rsem.at[w], (x, y, 1 - c)) for w in range(n_in)]
        for cp in cps:
            cp.start()
        for w in range(n_in):
            got = place(w, 1 - c)
            _remote(got, got, ssem.at[w], rsem.at[w], (x, y, 1 - c)).wait_recv()
        for cp in cps:
            cp.wait_send()

    outs = _hbm_call(body, "reduce_d2d_out", ins, out_shapes, (n_in, n_in), aliases={w: w for w in range(n_in)})
    return dict(zip(BIG, outs[:NBIG])), outs[NBIG], outs[NBIG + 1], outs[NBIG + 2]


def _row_tile(rows):
    for tr in (128, 176, 64, 16, 8):
        if rows % tr == 0:
            return tr
    return rows


def _sum_own_half(g, rb, cidx, name):
    s, rows, cols = g.shape
    half = rows // 2
    tr = _row_tile(half)
    nb = half // tr

    def body(c_ref, g_ref, r_ref, o_ref):
        del c_ref
        o_ref[...] = (g_ref[...] + r_ref[...]).astype(BF16)

    grid_spec = pltpu.PrefetchScalarGridSpec(
        num_scalar_prefetch=1, grid=(s, nb),
        in_specs=[pl.BlockSpec((None, tr, cols), lambda k, i, c: (k, c[0] * nb + i, 0)),
                  pl.BlockSpec((None, tr, cols), lambda k, i, c: (k, i, 0))],
        out_specs=pl.BlockSpec((None, tr, cols), lambda k, i, c: (k, i, 0)))
    return pl.pallas_call(
        body, name=name, grid_spec=grid_spec, out_shape=jax.ShapeDtypeStruct((s, half, cols), BF16),
        compiler_params=_params(("parallel", "parallel")),
    )(cidx, g, rb)


def _sum_chips(q, p, jc, name, by_cols=False):
    s, rows, cols = q.shape
    tr = _row_tile(rows)
    nb = rows // tr
    stacked = p.ndim == 3

    def body(jc_ref, q_ref, p_ref, o_ref):
        j = jc_ref[0]
        own = p_ref[...].astype(F32)
        acc = None
        for k in range(NCHIP):
            term = jnp.where(j == k, own, q_ref[k].astype(F32))
            acc = term if acc is None else acc + term
        o_ref[...] = acc

    if by_cols:
        half_spec = pl.BlockSpec((tr, cols), lambda i, jc_ref: (i, jc_ref[1]))
        out_shape = jax.ShapeDtypeStruct((rows, 2 * cols), F32)
    else:
        half_spec = pl.BlockSpec((tr, cols), lambda i, jc_ref: (jc_ref[1] * nb + i, 0))
        out_shape = jax.ShapeDtypeStruct((2 * rows, cols), F32)
    p_spec = pl.BlockSpec((None, tr, cols), lambda i, jc_ref: (jc_ref[0], i, 0)) if stacked else half_spec
    grid_spec = pltpu.PrefetchScalarGridSpec(
        num_scalar_prefetch=1, grid=(nb,),
        in_specs=[pl.BlockSpec((s, tr, cols), lambda i, jc_ref: (0, i, 0)), p_spec],
        out_specs=half_spec)
    return pl.pallas_call(
        body, name=name, grid_spec=grid_spec, out_shape=out_shape,
        compiler_params=_params(("parallel",)),
    )(jc, q, p)


def _place_shard(w, jc, name):
    rows, cols = w.shape
    tr = _row_tile(rows)

    def body(jc_ref, w_ref, o_ref):
        del jc_ref
        o_ref[...] = w_ref[...].astype(BF16)

    grid_spec = pltpu.PrefetchScalarGridSpec(
        num_scalar_prefetch=1, grid=(rows // tr,),
        in_specs=[pl.BlockSpec((tr, cols), lambda i, jc_ref: (i, 0))],
        out_specs=pl.BlockSpec((None, tr, cols), lambda i, jc_ref: (jc_ref[0], i, 0)))
    return pl.pallas_call(
        body, name=name, grid_spec=grid_spec, out_shape=jax.ShapeDtypeStruct((NCHIP, rows, cols), BF16),
        compiler_params=_params(("parallel",)),
    )(jc, w)


def _add(a, b, name):
    def body(a_ref, b_ref, o_ref):
        o_ref[...] = a_ref[...] + b_ref[...]

    return pl.pallas_call(body, name=name, out_shape=jax.ShapeDtypeStruct(a.shape, F32))(a, b)


def _pack_small(sm):
    vec_in = [sm[n] for n, _, _ in VEC_ROWS]
    nv = len(vec_in)

    def body(*refs):
        ins, lossv, dcw, dcfb, dcfw, s1, s2 = refs[:nv], refs[nv], refs[nv + 1], refs[nv + 2], refs[nv + 3], \
            refs[nv + 4], refs[nv + 5]
        for ref, (_, r0, nr) in zip(ins, VEC_ROWS):
            s1[r0:r0 + nr, :] = ref[...]
        s1[ROW_LOSS:ROW_LOSS + 1, :] = lossv[...]
        s1[ROW_CONV_A:ROW_CONV_A + 4, :] = dcw[...]
        s2[0:1, :] = dcfb[...]
        s2[1:4, :] = dcfw[...]
        s2[4:8, :] = jnp.zeros((4, DUP), F32)

    return pl.pallas_call(
        body, name="pack_small",
        out_shape=(jax.ShapeDtypeStruct((S1_ROWS, D), F32), jax.ShapeDtypeStruct((S2_ROWS, DUP), F32)),
    )(*vec_in, sm["lossv"], sm["conv_a_w"], sm["conv_f_b"], sm["conv_f_w"])


def _adam_math(w, g, m, v):
    m = ADAM_B1 * m + (1.0 - ADAM_B1) * g
    v = ADAM_B2 * v + (1.0 - ADAM_B2) * (g * g)
    m_hat = m / (1.0 - ADAM_B1 ** ADAM_STEP)
    v_hat = v / (1.0 - ADAM_B2 ** ADAM_STEP)
    delta = -ADAM_LR * (m_hat / (jnp.sqrt(v_hat) + ADAM_EPS) + ADAM_WD * w)
    return delta, m, v


def _adam(w, g, m, v, name):
    rows, cols = w.shape
    tr = _row_tile(rows)

    def body(w_ref, g_ref, m_ref, v_ref, d_ref, mo_ref, vo_ref):
        d_ref[...], mo_ref[...], vo_ref[...] = _adam_math(w_ref[...], g_ref[...], m_ref[...], v_ref[...])

    spec = pl.BlockSpec((tr, cols), lambda i: (i, 0))
    return pl.pallas_call(
        body, name=name, out_shape=(jax.ShapeDtypeStruct(w.shape, F32),) * 3, grid=(rows // tr,),
        in_specs=[spec] * 4, out_specs=(spec,) * 3,
        compiler_params=_params(("parallel",)),
    )(w, g, m, v)


def _adam_small(gs1, gs2, gs3, w, m, v):
    names = [n for n, _, _ in VEC_ROWS] + ["conv_f_b", "lru_wa", "lru_wx"]
    nn = len(names)

    def grad_of(i, g1, g2, g3):
        if i < len(VEC_ROWS):
            _, r0, nr = VEC_ROWS[i]
            return g1[r0:r0 + nr, :]
        if names[i] == "conv_f_b":
            return g2[0:1, :]
        return g3[0] if names[i] == "lru_wa" else g3[1]

    def body(*refs):
        g1, g2, g3 = refs[0], refs[1], refs[2]
        ws, ms, vs = refs[3:3 + nn], refs[3 + nn:3 + 2 * nn], refs[3 + 2 * nn:3 + 3 * nn]
        outs = refs[3 + 3 * nn:]
        for i in range(nn):
            d, mn, vn = _adam_math(ws[i][...], grad_of(i, g1, g2, g3), ms[i][...], vs[i][...])
            outs[i][...] = d
            outs[nn + i][...] = mn
            outs[2 * nn + i][...] = vn

    shapes = [jax.ShapeDtypeStruct(w[n].shape, F32) for n in names]
    outs = pl.pallas_call(body, name="adam_small", out_shape=tuple(shapes * 3))(
        gs1, gs2, gs3, *[w[n] for n in names], *[m[n] for n in names], *[v[n] for n in names])
    return {n: (outs[i], outs[nn + i], outs[2 * nn + i]) for i, n in enumerate(names)}


WEIGHTS = ("norm_pre_mix", "w_in", "conv_a_w", "conv_a_b", "lru_wa", "lru_ba", "lru_wx", "lru_bx", "lru_lambda",
           "hg_lb_logits", "hg_norm_g", "w_branch_a", "w_branch_b", "w_out", "norm_post_mix", "norm_pre_ffn",
           "w_up", "conv_f_w", "conv_f_b", "w_down", "norm_post_ffn")
NW = len(WEIGHTS)


def kernel(x, norm_pre_mix, w_in, conv_a_w, conv_a_b, lru_wa, lru_ba, lru_wx, lru_bx, lru_lambda, hg_lb_logits, hg_norm_g, w_branch_a, w_branch_b, w_out, norm_post_mix, norm_pre_ffn, w_up, conv_f_w, conv_f_b, w_down, norm_post_ffn, loss_target, m_norm_pre_mix, m_w_in, m_conv_a_w, m_conv_a_b, m_lru_wa, m_lru_ba, m_lru_wx, m_lru_bx, m_lru_lambda, m_hg_lb_logits, m_hg_norm_g, m_w_branch_a, m_w_branch_b, m_w_out, m_norm_post_mix, m_norm_pre_ffn, m_w_up, m_conv_f_w, m_conv_f_b, m_w_down, m_norm_post_ffn, v_norm_pre_mix, v_w_in, v_conv_a_w, v_conv_a_b, v_lru_wa, v_lru_ba, v_lru_wx, v_lru_bx, v_lru_lambda, v_hg_lb_logits, v_hg_norm_g, v_w_branch_a, v_w_branch_b, v_w_out, v_norm_post_mix, v_norm_pre_ffn, v_w_up, v_conv_f_w, v_conv_f_b, v_w_down, v_norm_post_ffn):
    rest = (norm_pre_mix, w_in, conv_a_w, conv_a_b, lru_wa, lru_ba, lru_wx, lru_bx, lru_lambda, hg_lb_logits, hg_norm_g, w_branch_a, w_branch_b, w_out, norm_post_mix, norm_pre_ffn, w_up, conv_f_w, conv_f_b, w_down, norm_post_ffn, loss_target, m_norm_pre_mix, m_w_in, m_conv_a_w, m_conv_a_b, m_lru_wa, m_lru_ba, m_lru_wx, m_lru_bx, m_lru_lambda, m_hg_lb_logits, m_hg_norm_g, m_w_branch_a, m_w_branch_b, m_w_out, m_norm_post_mix, m_norm_pre_ffn, m_w_up, m_conv_f_w, m_conv_f_b, m_w_down, m_norm_post_ffn, v_norm_pre_mix, v_w_in, v_conv_a_w, v_conv_a_b, v_lru_wa, v_lru_ba, v_lru_wx, v_lru_bx, v_lru_lambda, v_hg_lb_logits, v_hg_norm_g, v_w_branch_a, v_w_branch_b, v_w_out, v_norm_post_mix, v_norm_pre_ffn, v_w_up, v_conv_f_w, v_conv_f_b, v_w_down, v_norm_post_ffn)
    w_in_args = dict(zip(WEIGHTS, rest[:NW]))
    loss_target = rest[NW]
    m_args = dict(zip(WEIGHTS, rest[NW + 1:2 * NW + 1]))
    v_args = dict(zip(WEIGHTS, rest[2 * NW + 1:3 * NW + 1]))
    shape_of = {n: w_in_args[n].shape for n in WEIGHTS}

    def two_d(n, a):
        if n in BIG:
            return a.reshape(BIG_SHAPE[n])
        if n in ("lru_wa", "lru_wx"):
            return a.reshape(NH, HD, HD)
        return a.reshape(a.shape[-2:])

    w2 = {n: two_d(n, w_in_args[n]) for n in WEIGHTS}
    m2 = {n: two_d(n, m_args[n]) for n in WEIGHTS}
    v2 = {n: two_d(n, v_args[n]) for n in WEIGHTS}

    cidx = lax.axis_index("c").astype(jnp.int32).reshape(1)
    jchip = 2 * lax.axis_index("x") + lax.axis_index("y")

    jc = jnp.stack([jchip, lax.axis_index("c")]).astype(jnp.int32)

    shards = {n: _place_shard(w2[n], jc, "place_" + n) for n in BIG}
    conv_a_s = jnp.pad(w2["conv_a_w"], ((0, 4), (0, 0)))
    conv_f_s = jnp.pad(w2["conv_f_w"], ((0, 5), (0, 0)))
    w_in_full, conv_a_g, conv_f_g = _gather_first(shards["w_in"], conv_a_s, conv_f_s)
    conv_a_full = jnp.transpose(conv_a_g, (1, 0, 2)).reshape(8, D)[0:4]
    conv_f_full = jnp.transpose(conv_f_g, (1, 0, 2)).reshape(8, DUP)[0:3]
    small = {n: w2[n] for n in WEIGHTS if n not in BIG and n not in ("conv_a_w", "conv_f_w")}

    grad_x, p_big, q_big, sm_g = _local_step(
        x[0], loss_target[0], w_in_full, [shards[n] for n in REST], conv_a_full, conv_f_full, small, cidx)

    s1, s2 = _pack_small(sm_g)
    s3 = jnp.concatenate([sm_g["lru_wa"].reshape(D, HD), sm_g["lru_wx"].reshape(D, HD)], axis=0)
    _, (rs1, rs2, rs3) = _reduce_stage1({}, (), (s1, s2, s3), "reduce_d2d_in_small")
    ps1, ps2, ps3 = _add(s1, rs1, "add_s1"), _add(s2, rs2, "add_s2"), _add(s3, rs3, "add_s3")
    qs1, qs2, qs3 = _reduce_stage2(ps1, ps2, ps3)
    f_big = {n: _sum_chips(q_big[n], p_big[n], jc, "sum_chips_" + n) for n in BIG}
    fs1 = _sum_chips(qs1, ps1, jc, "sum_chips_s1")
    fs2 = _sum_chips(qs2, ps2, jc, "sum_chips_s2", by_cols=True)
    fs3 = _sum_chips(qs3, ps3, jc, "sum_chips_s3")
    g_big, gs1, gs2, gs3 = _reduce_stage3(f_big, fs1, fs2, fs3)

    res = {}
    for n in BIG:
        d, mn, vn = _adam(w2[n], g_big[n], m2[n], v2[n], "adam_" + n)
        res[n] = (g_big[n], d, mn, vn)
    small_res = _adam_small(gs1, gs2, gs3.reshape(2, NH, HD, HD), w2, m2, v2)
    for n, r0, nr in VEC_ROWS:
        res[n] = (gs1[r0:r0 + nr],) + small_res[n]
    res["conv_f_b"] = (gs2[0:1],) + small_res["conv_f_b"]
    res["lru_wa"] = (gs3[0:D].reshape(NH, HD, HD),) + small_res["lru_wa"]
    res["lru_wx"] = (gs3[D:2 * D].reshape(NH, HD, HD),) + small_res["lru_wx"]
    g_ca = lax.dynamic_slice_in_dim(gs1[ROW_CONV_A:ROW_CONV_A + 4], jchip * (D // NCHIP), D // NCHIP, axis=1)
    g_cf = lax.dynamic_slice_in_dim(gs2[1:4], jchip * SH_UP, SH_UP, axis=1)
    res["conv_a_w"] = (g_ca,) + _adam(w2["conv_a_w"], g_ca, m2["conv_a_w"], v2["conv_a_w"], "adam_conv_a_w")
    res["conv_f_w"] = (g_cf,) + _adam(w2["conv_f_w"], g_cf, m2["conv_f_w"], v2["conv_f_w"], "adam_conv_f_w")

    loss = (0.5 / D) * jnp.sum(gs1[ROW_LOSS])
    out = [loss, grad_x.reshape(x.shape)]
    for part in range(4):
        out += [res[n][part].reshape(shape_of[n]) for n in WEIGHTS]
    return tuple(out)
```

```python
import functools

import jax
import jax.numpy as jnp
from jax import lax
from jax.experimental import pallas as pl
from jax.experimental.pallas import tpu as pltpu

F32 = jnp.float32
BF16 = jnp.bfloat16

D = 1024
NH = 8
HD = 128
CH = 32
DFF = 2816
DUP = 2 * DFF
NCHIP = 4
SH_IN = 2 * D
SH_UP = DUP // NCHIP
SH_DN = DFF // NCHIP
SH_BR = D // NCHIP
EPS = 1e-6
LRU_C = 8.0
ADAM_LR = 0.001
ADAM_B1 = 0.9
ADAM_B2 = 0.999
ADAM_EPS = 1e-08
ADAM_WD = 0.01
ADAM_STEP = 10
VMEM_BIG = 56 * 1024 * 1024
MESH = pl.DeviceIdType.MESH

SLOT_A, SLOT_B, SLOT_C, SLOT_G = 2, 0, 1, 3


def _slot_of_chip(s):
    return jnp.where(s == 3, 3, (s + 2) % 3)


def _params(sem, vmem=None):
    return pltpu.CompilerParams(dimension_semantics=sem, vmem_limit_bytes=vmem)


_GC = 0.7978845608028654
_GA = 0.044715


def _gelu(x):
    return 0.5 * x * (1.0 + jnp.tanh(_GC * (x + _GA * x * x * x)))


def _gelu_and_grad(x):
    x2 = x * x
    th = jnp.tanh(_GC * x * (1.0 + _GA * x2))
    g = 0.5 * x * (1.0 + th)
    dg = 0.5 * (1.0 + th) + 0.5 * x * (1.0 - th * th) * _GC * (1.0 + 3.0 * _GA * x2)
    return g, dg


def _sig(x):
    return jax.nn.sigmoid(x)


def _dot(a, b):
    return jnp.dot(a, b, preferred_element_type=F32)


def _dot_nt(a, b):
    return lax.dot_general(a, b, (((1,), (1,)), ((), ())), preferred_element_type=F32)


def _dot_tn(a, b):
    return lax.dot_general(a, b, (((0,), (0,)), ((), ())), preferred_element_type=F32)


def _chunk_cumsum(x):
    pos = lax.broadcasted_iota(jnp.int32, (x.shape[0], 1), 0) & (CH - 1)
    d = 1
    while d < CH:
        x = x + jnp.where(pos >= d, pltpu.roll(x, d, 0), 0.0)
        d *= 2
    return x


def _chunk_revcumsum(x):
    n = x.shape[0]
    pos = lax.broadcasted_iota(jnp.int32, (n, 1), 0) & (CH - 1)
    d = 1
    while d < CH:
        x = x + jnp.where(pos < CH - d, pltpu.roll(x, n - d, 0), 0.0)
        d *= 2
    return x


def _chunk_last(x):
    n = x.shape[0]
    return jnp.concatenate(
        [jnp.broadcast_to(x[c * CH + CH - 1:c * CH + CH, :], (CH, x.shape[1])) for c in range(n // CH)], axis=0)


def _chunk_total(x):
    n = x.shape[0]
    return jnp.concatenate(
        [jnp.broadcast_to(jnp.sum(x[c * CH:(c + 1) * CH, :], axis=0, keepdims=True), (CH, x.shape[1]))
         for c in range(n // CH)], axis=0)


def _rms_stats(x):
    r = lax.rsqrt(jnp.mean(x * x, axis=-1, keepdims=True) + EPS)
    return r, x * r


def _rms_bwd(gd, n, r):
    return r * (gd - n * jnp.mean(gd * n, axis=-1, keepdims=True))


def _shift_rows(x, d, fill):
    rows = lax.broadcasted_iota(jnp.int32, (x.shape[0], 1), 0)
    return jnp.where(rows >= d, pltpu.roll(x, d, 0), fill)


def _scan_down(a, u, carry):
    n = a.shape[0]
    pos = lax.broadcasted_iota(jnp.int32, (n, 1), 0) & 7
    for d in (1, 2, 4):
        u = a * jnp.where(pos >= d, pltpu.roll(u, d, 0), 0.0) + u
        a = a * jnp.where(pos >= d, pltpu.roll(a, d, 0), 1.0)
    out = []
    for v in range(n // 8):
        h = a[v * 8:v * 8 + 8, :] * carry + u[v * 8:v * 8 + 8, :]
        carry = h[7:8, :]
        out.append(h)
    return jnp.concatenate(out, axis=0)


def _scan_up(b, g, carry):
    n = b.shape[0]
    pos = lax.broadcasted_iota(jnp.int32, (n, 1), 0) & 7
    for d in (1, 2, 4):
        g = g + b * jnp.where(pos < 8 - d, pltpu.roll(g, n - d, 0), 0.0)
        b = b * jnp.where(pos < 8 - d, pltpu.roll(b, n - d, 0), 1.0)
    out = [None] * (n // 8)
    for v in reversed(range(n // 8)):
        h = g[v * 8:v * 8 + 8, :] + b[v * 8:v * 8 + 8, :] * carry
        carry = h[0:1, :]
        out[v] = h
    return jnp.concatenate(out, axis=0)


def _shift_rows_up(x, d, fill):
    n = x.shape[0]
    rows = lax.broadcasted_iota(jnp.int32, (n, 1), 0)
    return jnp.where(rows < n - d, pltpu.roll(x, n - d, 0), fill)


def _mm_nn_sharded(a, b3, out_dtype, tm, name, slot_fn=None):
    m, k = a.shape
    s, _, ns = b3.shape

    def body(a_ref, b_ref, o_ref):
        o_ref[...] = _dot(a_ref[...], b_ref[...]).astype(out_dtype)

    if slot_fn is None:
        out_shape = jax.ShapeDtypeStruct((m, s * ns), out_dtype)
        out_spec = pl.BlockSpec((tm, ns), lambda j, i: (i, j))
    else:
        out_shape = jax.ShapeDtypeStruct((s, m, ns), out_dtype)
        out_spec = pl.BlockSpec((None, tm, ns), lambda j, i: (slot_fn(j), i, 0))
    return pl.pallas_call(
        body, name=name, out_shape=out_shape, grid=(s, m // tm),
        in_specs=[pl.BlockSpec((tm, k), lambda j, i: (i, 0)),
                  pl.BlockSpec((None, k, ns), lambda j, i: (j, 0, 0))],
        out_specs=out_spec,
        compiler_params=_params(("parallel", "parallel"), VMEM_BIG),
    )(a, b3)


def _mm_in_gather(a, w_in3, stk_rest, names_rest, tm):
    m, k = a.shape
    s, _, ns = w_in3.shape
    nr = len(stk_rest)
    mt = m // tm

    def body(a_ref, b_ref, *rest):
        o_ref = rest[nr]
        stk = rest[nr + 1:2 * nr + 1]
        ssem, rsem = rest[2 * nr + 1:]
        step = pl.program_id(0) * mt + pl.program_id(1)
        sends, arrive, _, _ = _gather_copies(stk, names_rest, ssem, rsem)

        @pl.when(step == 0)
        def _():
            for cp in sends:
                cp.start()

        o_ref[...] = _dot(a_ref[...], b_ref[...])

        @pl.when(step == s * mt - 1)
        def _():
            for cp in arrive:
                cp.wait_recv()
            for cp in sends:
                cp.wait_send()

    any_spec = pl.BlockSpec(memory_space=pl.ANY)
    outs = pl.pallas_call(
        body, name="mm_in",
        out_shape=(jax.ShapeDtypeStruct((s, m, ns), F32),) + tuple(jax.ShapeDtypeStruct(x.shape, x.dtype) for x in stk_rest),
        grid=(s, mt),
        in_specs=[pl.BlockSpec((tm, k), lambda j, i: (i, 0)), pl.BlockSpec((None, k, ns), lambda j, i: (j, 0, 0))]
        + [any_spec] * nr,
        out_specs=(pl.BlockSpec((None, tm, ns), lambda j, i: (_slot_of_chip(j), i, 0)),) + (any_spec,) * nr,
        scratch_shapes=[pltpu.SemaphoreType.DMA((3 * nr,)), pltpu.SemaphoreType.DMA((3 * nr,))],
        input_output_aliases={2 + w: 1 + w for w in range(nr)},
        compiler_params=pltpu.CompilerParams(dimension_semantics=("arbitrary", "arbitrary"),
                                             vmem_limit_bytes=VMEM_BIG, has_side_effects=True),
    )(a, w_in3, *stk_rest)
    return outs[0], list(outs[1:])


def _mm_dh1_exchange(dp4, w_in3, p_w_in, tm):
    s, k, ns = w_in3.shape
    m = dp4.shape[1]
    mt = m // tm

    def body(a_ref, b_ref, p_ref, o_ref, q_ref, ssem, rsem):
        i, j = pl.program_id(0), pl.program_id(1)
        sends, arrive = _exchange_copies([q_ref], [lambda chip: p_ref.at[chip]], ssem, rsem)

        @pl.when((i == 0) & (j == 0))
        def _():
            for cp in sends:
                cp.start()

        @pl.when(j == 0)
        def _():
            o_ref[...] = _dot_nt(a_ref[...], b_ref[...])

        @pl.when(j > 0)
        def _():
            o_ref[...] += _dot_nt(a_ref[...], b_ref[...])

        @pl.when((i == mt - 1) & (j == s - 1))
        def _():
            for cp in arrive:
                cp.wait_recv()
            for cp in sends:
                cp.wait_send()

    any_spec = pl.BlockSpec(memory_space=pl.ANY)
    return pl.pallas_call(
        body, name="mm_dh1",
        out_shape=(jax.ShapeDtypeStruct((m, k), F32), jax.ShapeDtypeStruct(p_w_in.shape, BF16)),
        grid=(mt, s),
        in_specs=[pl.BlockSpec((None, tm, ns), lambda i, j: (_slot_of_chip(j), i, 0)),
                  pl.BlockSpec((None, k, ns), lambda i, j: (j, 0, 0)), any_spec],
        out_specs=(pl.BlockSpec((tm, k), lambda i, j: (i, 0)), any_spec),
        scratch_shapes=[pltpu.SemaphoreType.DMA((3,)), pltpu.SemaphoreType.DMA((3,))],
        compiler_params=pltpu.CompilerParams(dimension_semantics=("arbitrary", "arbitrary"),
                                             vmem_limit_bytes=VMEM_BIG, has_side_effects=True),
    )(dp4, w_in3, p_w_in)


def _mm_nt_sharded(a, b3, tm, name, stacked_slot_fn=None):
    s, k, ns = b3.shape
    m = a.shape[1] if stacked_slot_fn is not None else a.shape[0]

    def body(a_ref, b_ref, o_ref):
        j = pl.program_id(1)
        @pl.when(j == 0)
        def _():
            o_ref[...] = _dot_nt(a_ref[...], b_ref[...])

        @pl.when(j > 0)
        def _():
            o_ref[...] += _dot_nt(a_ref[...], b_ref[...])

    if stacked_slot_fn is None:
        a_spec = pl.BlockSpec((tm, ns), lambda i, j: (i, j))
    else:
        a_spec = pl.BlockSpec((None, tm, ns), lambda i, j: (stacked_slot_fn(j), i, 0))
    return pl.pallas_call(
        body, name=name, out_shape=jax.ShapeDtypeStruct((m, k), F32), grid=(m // tm, s),
        in_specs=[a_spec, pl.BlockSpec((None, k, ns), lambda i, j: (j, 0, 0))],
        out_specs=pl.BlockSpec((tm, k), lambda i, j: (i, 0)),
        compiler_params=_params(("parallel", "arbitrary"), VMEM_BIG),
    )(a, b3)


def _mm_nt(a, b, out_dtype, tm, name):
    m, k = a.shape
    n = b.shape[0]

    def body(a_ref, b_ref, o_ref):
        o_ref[...] = _dot_nt(a_ref[...], b_ref[...]).astype(out_dtype)

    return pl.pallas_call(
        body, name=name, out_shape=jax.ShapeDtypeStruct((m, n), out_dtype), grid=(m // tm,),
        in_specs=[pl.BlockSpec((tm, k), lambda i: (i, 0)), pl.BlockSpec((n, k), lambda i: (0, 0))],
        out_specs=pl.BlockSpec((tm, n), lambda i: (i, 0)),
        compiler_params=_params(("parallel",), VMEM_BIG),
    )(a, b)


def _mm_tn(a, g, tkk, tn, tk, name, stacked_slot_fn=None, stacked_out=False):
    m, k = a.shape
    if stacked_slot_fn is not None:
        n = g.shape[0] * g.shape[2]
        g_spec = pl.BlockSpec((None, tk, tn), lambda kk, j, mm: (stacked_slot_fn(j), mm, 0))
    else:
        n = g.shape[1]
        g_spec = pl.BlockSpec((tk, tn), lambda kk, j, mm: (mm, j))

    def body(a_ref, g_ref, o_ref):
        mm = pl.program_id(2)

        @pl.when(mm == 0)
        def _():
            o_ref[...] = _dot_tn(a_ref[...], g_ref[...])

        @pl.when(mm > 0)
        def _():
            o_ref[...] += _dot_tn(a_ref[...], g_ref[...])

    if stacked_out:
        out_shape = jax.ShapeDtypeStruct((n // tn, k, tn), F32)
        out_spec = pl.BlockSpec((None, tkk, tn), lambda kk, j, mm: (j, kk, 0))
    else:
        out_shape = jax.ShapeDtypeStruct((k, n), F32)
        out_spec = pl.BlockSpec((tkk, tn), lambda kk, j, mm: (kk, j))
    return pl.pallas_call(
        body, name=name, out_shape=out_shape, grid=(k // tkk, n // tn, m // tk),
        in_specs=[pl.BlockSpec((tk, tkk), lambda kk, j, mm: (mm, kk)), g_spec],
        out_specs=out_spec,
        compiler_params=_params(("parallel", "parallel", "arbitrary"), VMEM_BIG),
    )(a, g)


def _norm_fwd(x, gain, tt):
    t = x.shape[0]

    def body(x_ref, g_ref, h_ref):
        _, n = _rms_stats(x_ref[...])
        h_ref[...] = (n * g_ref[...]).astype(BF16)

    return pl.pallas_call(
        body, name="norm_fwd", out_shape=jax.ShapeDtypeStruct((t, D), BF16), grid=(t // tt,),
        in_specs=[pl.BlockSpec((tt, D), lambda i: (i, 0)), pl.BlockSpec((1, D), lambda i: (0, 0))],
        out_specs=pl.BlockSpec((tt, D), lambda i: (i, 0)),
        compiler_params=_params(("parallel",)),
    )(x, gain)


def _lru_gates(xc, wa_ref, wx_ref, ba, bx, lam):
    xcb = xc.astype(BF16)
    ra = jnp.concatenate([_dot(xcb[:, n * HD:(n + 1) * HD], wa_ref[n]) for n in range(NH)], axis=1) + ba
    ix = jnp.concatenate([_dot(xcb[:, n * HD:(n + 1) * HD], wx_ref[n]) for n in range(NH)], axis=1) + bx
    r = _sig(ra)
    ig = _sig(ix)
    z = -lam
    sp = jnp.maximum(z, 0.0) + jnp.log1p(jnp.exp(-jnp.abs(z)))
    log_a = -LRU_C * r * sp
    a = jnp.exp(log_a)
    z2 = 2.0 * log_a
    series = -z2 * (1.0 + z2 * (0.5 + z2 * (1.0 / 6.0 + z2 * (1.0 / 24.0))))
    om = jnp.where(z2 > -0.02, series, 1.0 - jnp.exp(z2))
    mult = jnp.sqrt(om)
    return xcb, r, ig, sp, a, mult


def _mixer_a_fwd(p4, cw, cb, wa, wx, ba, bx, lam, tt):
    t = p4.shape[1]

    def body(p_ref, cw_ref, cb_ref, wa_ref, wx_ref, ba_ref, bx_ref, lam_ref, ya_ref, h_ref, halo, hc):
        i = pl.program_id(0)

        @pl.when(i == 0)
        def _():
            halo[...] = jnp.zeros((8, D), F32)
            hc[...] = jnp.zeros((1, D), F32)

        xa = p_ref[:, 0:D]
        ga = p_ref[:, D:2 * D]
        xe = jnp.concatenate([halo[...], xa], axis=0)
        xc = (cb_ref[...] + cw_ref[3:4, :] * xe
              + sum(cw_ref[3 - s:4 - s, :] * pltpu.roll(xe, s, 0) for s in (1, 2, 3)))[8:, :]
        halo[...] = xa[tt - 8:, :]
        _, _, ig, _, a, mult = _lru_gates(xc, wa_ref, wx_ref, ba_ref[...], bx_ref[...], lam_ref[...])
        u = mult * ig * xc
        h = _scan_down(a, u, hc[...])
        hc[...] = h[tt - 1:tt, :]
        h_ref[...] = h
        ya_ref[...] = (h * _gelu(ga)).astype(BF16)

    full = lambda shape: pl.BlockSpec(shape, lambda i: (0,) * len(shape))
    return pl.pallas_call(
        body, name="mixer_a_fwd",
        out_shape=(jax.ShapeDtypeStruct((t, D), BF16), jax.ShapeDtypeStruct((t, D), F32)),
        grid=(t // tt,),
        in_specs=[pl.BlockSpec((None, tt, 2 * D), lambda i: (SLOT_A, i, 0)),
                  full((4, D)), full((1, D)), full((NH, HD, HD)), full((NH, HD, HD)),
                  full((1, D)), full((1, D)), full((1, D))],
        out_specs=(pl.BlockSpec((tt, D), lambda i: (i, 0)), pl.BlockSpec((tt, D), lambda i: (i, 0))),
        scratch_shapes=[pltpu.VMEM((8, D), F32), pltpu.VMEM((1, D), F32)],
        compiler_params=_params(("arbitrary",), VMEM_BIG),
    )(p4, cw, cb, wa, wx, ba, bx, lam)


def _chunk_masks(tt):
    row = lax.broadcasted_iota(jnp.int32, (tt, tt), 0)
    col = lax.broadcasted_iota(jnp.int32, (tt, tt), 1)
    same = jnp.right_shift(row, 5) == jnp.right_shift(col, 5)
    return same & (col <= row)


def _hg_head_fwd(q, fz, lbh):
    sg = _sig(fz)
    sgn = _sig(-fz)
    f = lbh + (1.0 - lbh) * sg
    logf = jnp.log(f)
    k = (1.0 - lbh) * sgn
    g = _chunk_cumsum(logf)
    gu = _chunk_last(g) - g
    eg = jnp.exp(g)
    eng = jnp.exp(-g)
    egu = jnp.exp(gu)
    qt = q * eg
    kt = k * eng
    kd = k * egu
    return sg, sgn, f, k, g, eg, eng, egu, qt, kt, kd


def _lb_of(logits_ref):
    return _sig(logits_ref[0:1, :] - logits_ref[1:2, :])


def _hgrn2_fwd(p4, logits, gnorm, tt):
    t = p4.shape[1]
    nc = tt // CH

    def body(p_ref, lg_ref, gn_ref, yb_ref, o_ref, ss_ref, st):
        i = pl.program_id(0)

        @pl.when(i == 0)
        def _():
            st[...] = jnp.zeros((NH, HD, HD), F32)

        low = _chunk_masks(tt)
        lb = _lb_of(lg_ref)
        heads = [slice(h * HD, (h + 1) * HD) for h in range(NH)]
        _, _, _, _, g, _, _, _, qt, kt, kd = _hg_head_fwd(p_ref[0, :, 0:D], p_ref[0, :, D:2 * D], lb)
        qtb, ktb, kdb, vb = qt.astype(BF16), kt.astype(BF16), kd.astype(BF16), p_ref[1, :, 0:D].astype(BF16)
        decs = [jnp.exp(g[c * CH + CH - 1:c * CH + CH, :]) for c in range(nc)]
        o_in = []
        for hs in heads:
            att = jnp.where(low, _dot_nt(qtb[:, hs], ktb[:, hs]), 0.0)
            o_in.append(_dot(att.astype(BF16), vb[:, hs]))
        s_t = [st[h] for h in range(NH)]
        pieces = [[None] * nc for _ in range(NH)]
        for c in range(nc):
            sl = slice(c * CH, (c + 1) * CH)
            for h, hs in enumerate(heads):
                s_bf = s_t[h].astype(BF16)
                ss_ref[c, h] = s_bf
                pieces[h][c] = o_in[h][sl] + _dot_nt(qtb[sl, hs], s_bf)
                s_t[h] = s_t[h] * decs[c][:, hs] + _dot_tn(vb[sl, hs], kdb[sl, hs])
        for h, hs in enumerate(heads):
            st[h] = s_t[h]
            o = jnp.concatenate(pieces[h], axis=0)
            _, n = _rms_stats(o)
            og = p_ref[1, :, D + h * HD:D + (h + 1) * HD]
            o_ref[:, hs] = o
            yb_ref[:, hs] = (n * gn_ref[:, hs] * (og * _sig(og))).astype(BF16)

    return pl.pallas_call(
        body, name="hgrn2_fwd",
        out_shape=(jax.ShapeDtypeStruct((t, D), BF16), jax.ShapeDtypeStruct((t, D), F32),
                   jax.ShapeDtypeStruct((t // CH, NH, HD, HD), BF16)),
        grid=(t // tt,),
        in_specs=[pl.BlockSpec((2, tt, 2 * D), lambda i: (0, i, 0)),
                  pl.BlockSpec((2, D), lambda i: (0, 0)), pl.BlockSpec((1, D), lambda i: (0, 0))],
        out_specs=(pl.BlockSpec((tt, D), lambda i: (i, 0)), pl.BlockSpec((tt, D), lambda i: (i, 0)),
                   pl.BlockSpec((nc, NH, HD, HD), lambda i: (i, 0, 0, 0))),
        scratch_shapes=[pltpu.VMEM((NH, HD, HD), F32)],
        compiler_params=_params(("arbitrary",), VMEM_BIG),
    )(p4, logits, gnorm)


def _mid_fwd(ya, yb, p4, x, wa, wb, wo, g_pm, g_pf, tt):
    t = x.shape[0]

    def body(ya_ref, yb_ref, gt_ref, x_ref, wa_ref, wb_ref, wo_ref, gpm_ref, gpf_ref,
             za_ref, zb_ref, mix_ref, m2_ref, x1_ref, h2_ref):
        za = _dot(ya_ref[...], wa_ref[...])
        zb = _dot(yb_ref[...], wb_ref[...])
        mix = _sig(gt_ref[:, 0:D]) * za + _sig(gt_ref[:, D:2 * D]) * zb
        mixb = mix.astype(BF16)
        m2 = _dot(mixb, wo_ref[...])
        _, n2 = _rms_stats(m2)
        x1 = x_ref[...] + n2 * gpm_ref[...]
        _, n1 = _rms_stats(x1)
        za_ref[...] = za.astype(BF16)
        zb_ref[...] = zb.astype(BF16)
        mix_ref[...] = mixb
        m2_ref[...] = m2
        x1_ref[...] = x1
        h2_ref[...] = (n1 * gpf_ref[...]).astype(BF16)

    row = lambda dt: jax.ShapeDtypeStruct((t, D), dt)
    tile = pl.BlockSpec((tt, D), lambda i: (i, 0))
    wsp = pl.BlockSpec((D, D), lambda i: (0, 0))
    vec = pl.BlockSpec((1, D), lambda i: (0, 0))
    return pl.pallas_call(
        body, name="mid_fwd",
        out_shape=(row(BF16), row(BF16), row(BF16), row(F32), row(F32), row(BF16)),
        grid=(t // tt,),
        in_specs=[tile, tile, pl.BlockSpec((None, tt, 2 * D), lambda i: (SLOT_G, i, 0)), tile,
                  wsp, wsp, wsp, vec, vec],
        out_specs=(tile,) * 6,
        compiler_params=_params(("parallel",), VMEM_BIG),
    )(ya, yb, p4, x, wa, wb, wo, g_pm, g_pf)


def _ffn_act_fwd(up_pre, cfw, cfb, tt):
    t = up_pre.shape[0]

    def body(u_ref, halo_ref, w_ref, b_ref, y_ref, uv_ref, gl_ref, dgl_ref):
        i = pl.program_id(0)
        halves = []
        for c0 in (0, DFF):
            cs = slice(c0, c0 + DFF)
            xe = jnp.concatenate([jnp.where(i > 0, halo_ref[:, cs], 0.0), u_ref[:, cs]], axis=0)
            up = (b_ref[:, cs] + w_ref[2:3, cs] * xe + w_ref[1:2, cs] * pltpu.roll(xe, 1, 0)
                  + w_ref[0:1, cs] * pltpu.roll(xe, 2, 0))
            halves.append(up[8:, :])
        gl, dgl = _gelu_and_grad(halves[0])
        y_ref[...] = (gl * halves[1]).astype(BF16)
        uv_ref[...] = halves[1].astype(BF16)
        gl_ref[...] = gl.astype(BF16)
        dgl_ref[...] = dgl.astype(BF16)

    hb = tt // 8
    half = pl.BlockSpec((tt, DFF), lambda i: (i, 0))
    return pl.pallas_call(
        body, name="ffn_act_fwd",
        out_shape=(jax.ShapeDtypeStruct((t, DFF), BF16),) * 4,
        grid=(t // tt,),
        in_specs=[pl.BlockSpec((tt, DUP), lambda i: (i, 0)),
                  pl.BlockSpec((8, DUP), lambda i: (jnp.maximum(i * hb - 1, 0), 0)),
                  pl.BlockSpec((3, DUP), lambda i: (0, 0)), pl.BlockSpec((1, DUP), lambda i: (0, 0))],
        out_specs=(half,) * 4,
        compiler_params=_params(("parallel",), VMEM_BIG),
    )(up_pre, up_pre, cfw, cfb)


def _down_loss(y, wdn, x1, tgt, g_post, tt):
    t = x1.shape[0]

    def body(y_ref, w_ref, x1_ref, t_ref, g_ref, dx2_ref, dm3_ref, lossv_ref, dg_ref):
        i = pl.program_id(0)
        m3 = _dot(y_ref[...], w_ref[...])
        r, n3 = _rms_stats(m3)
        g = g_ref[...]
        e = x1_ref[...] + n3 * g - t_ref[...]
        dx2 = e * (1.0 / D)
        dx2_ref[...] = dx2
        dm3_ref[...] = _rms_bwd(dx2 * g, n3, r).astype(BF16)
        lv = jnp.sum(e * e, axis=0, keepdims=True)
        dgv = jnp.sum(dx2 * n3, axis=0, keepdims=True)

        @pl.when(i == 0)
        def _():
            lossv_ref[...] = lv
            dg_ref[...] = dgv

        @pl.when(i > 0)
        def _():
            lossv_ref[...] += lv
            dg_ref[...] += dgv

    tile = pl.BlockSpec((tt, D), lambda i: (i, 0))
    vec = pl.BlockSpec((1, D), lambda i: (0, 0))
    return pl.pallas_call(
        body, name="down_loss",
        out_shape=(jax.ShapeDtypeStruct((t, D), F32), jax.ShapeDtypeStruct((t, D), BF16),
                   jax.ShapeDtypeStruct((1, D), F32), jax.ShapeDtypeStruct((1, D), F32)),
        grid=(t // tt,),
        in_specs=[pl.BlockSpec((tt, DFF), lambda i: (i, 0)), pl.BlockSpec((DFF, D), lambda i: (0, 0)),
                  tile, tile, vec],
        out_specs=(tile, tile, vec, vec),
        compiler_params=_params(("arbitrary",), VMEM_BIG),
    )(y, wdn, x1, tgt, g_post)


def _ffn_act_bwd(dy, up_pre, uv, gl, dgl, cfw, tt):
    t = up_pre.shape[0]
    nt = t // tt

    def body(dy_ref, dyn_ref, u_ref, uv_ref, uvn_ref, gl_ref, gln_ref, dgl_ref, dgln_ref, cw_ref,
             du_ref, dcw_ref, dcb_ref):
        i = pl.program_id(0)
        n = tt + 8
        next_live = jnp.where(i < nt - 1, 1.0, 0.0)
        ext = lambda ref, nref: jnp.concatenate([ref[...].astype(F32), nref[...].astype(F32)[0:8, :]], axis=0)
        dy = jnp.concatenate([dy_ref[...].astype(F32), dyn_ref[...].astype(F32)[0:8, :] * next_live], axis=0)
        ds = (dy * ext(uv_ref, uvn_ref) * ext(dgl_ref, dgln_ref), dy * ext(gl_ref, gln_ref))
        dcw_parts, dcb_parts = [], []
        for hh, c0 in enumerate((0, DFF)):
            cs = slice(c0, c0 + DFF)
            dd = ds[hh]
            d1 = pltpu.roll(dd, n - 1, 0)
            d2 = pltpu.roll(dd, n - 2, 0)
            du_ref[:, cs] = (cw_ref[2:3, cs] * dd + cw_ref[1:2, cs] * d1 + cw_ref[0:1, cs] * d2)[0:tt, :].astype(BF16)
            x = u_ref[:, cs]
            dcw_parts.append(jnp.concatenate(
                [jnp.sum(dk[0:tt, :] * x, axis=0, keepdims=True) for dk in (d2, d1, dd)], axis=0))
            dcb_parts.append(jnp.sum(dd[0:tt, :], axis=0, keepdims=True))
        dcw = jnp.concatenate(dcw_parts, axis=1)
        dcb = jnp.concatenate(dcb_parts, axis=1)

        @pl.when(i == 0)
        def _():
            dcw_ref[...] = dcw
            dcb_ref[...] = dcb

        @pl.when(i > 0)
        def _():
            dcw_ref[...] += dcw
            dcb_ref[...] += dcb

    half = pl.BlockSpec((tt, DFF), lambda i: (i, 0))
    half_next = pl.BlockSpec((16, DFF), lambda i: (jnp.minimum((i + 1) * (tt // 16), t // 16 - 1), 0))
    return pl.pallas_call(
        body, name="ffn_act_bwd",
        out_shape=(jax.ShapeDtypeStruct((t, DUP), BF16), jax.ShapeDtypeStruct((3, DUP), F32),
                   jax.ShapeDtypeStruct((1, DUP), F32)),
        grid=(nt,),
        in_specs=[half, half_next, pl.BlockSpec((tt, DUP), lambda i: (i, 0)),
                  half, half_next, half, half_next, half, half_next,
                  pl.BlockSpec((3, DUP), lambda i: (0, 0))],
        out_specs=(pl.BlockSpec((tt, DUP), lambda i: (i, 0)), pl.BlockSpec((3, DUP), lambda i: (0, 0)),
                   pl.BlockSpec((1, DUP), lambda i: (0, 0))),
        compiler_params=_params(("arbitrary",), VMEM_BIG),
    )(dy, dy, up_pre, uv, uv, gl, gl, dgl, dgl, cfw)


def _mid_bwd(dh2, dx2, x1, m2, za, zb, p4, wa, wb, wo, g_pm, g_pf, tt):
    t = x1.shape[0]

    def body(dh2_ref, dx2_ref, x1_ref, m2_ref, za_ref, zb_ref, gt_ref, wa_ref, wb_ref, wo_ref, gpm_ref, gpf_ref,
             dx1_ref, dm2_ref, dza_ref, dzb_ref, dya_ref, dyb_ref, dp_ref, dgpm_ref, dgpf_ref):
        i = pl.program_id(0)
        r1, n1 = _rms_stats(x1_ref[...])
        dh2 = dh2_ref[...]
        dx1 = dx2_ref[...] + _rms_bwd(dh2 * gpf_ref[...], n1, r1)
        r2, n2 = _rms_stats(m2_ref[...])
        dm2 = _rms_bwd(dx1 * gpm_ref[...], n2, r2).astype(BF16)
        dmix = _dot_nt(dm2, wo_ref[...])
        sa = _sig(gt_ref[:, 0:D])
        sb = _sig(gt_ref[:, D:2 * D])
        dza = (dmix * sa).astype(BF16)
        dzb = (dmix * sb).astype(BF16)
        dp_ref[:, 0:D] = (dmix * za_ref[...].astype(F32) * sa * (1.0 - sa)).astype(BF16)
        dp_ref[:, D:2 * D] = (dmix * zb_ref[...].astype(F32) * sb * (1.0 - sb)).astype(BF16)
        dx1_ref[...] = dx1
        dm2_ref[...] = dm2
        dza_ref[...] = dza
        dzb_ref[...] = dzb
        dya_ref[...] = _dot_nt(dza, wa_ref[...])
        dyb_ref[...] = _dot_nt(dzb, wb_ref[...])
        dgpf = jnp.sum(dh2 * n1, axis=0, keepdims=True)
        dgpm = jnp.sum(dx1 * n2, axis=0, keepdims=True)

        @pl.when(i == 0)
        def _():
            dgpf_ref[...] = dgpf
            dgpm_ref[...] = dgpm

        @pl.when(i > 0)
        def _():
            dgpf_ref[...] += dgpf
            dgpm_ref[...] += dgpm

    row = lambda dt: jax.ShapeDtypeStruct((t, D), dt)
    tile = pl.BlockSpec((tt, D), lambda i: (i, 0))
    wsp = pl.BlockSpec((D, D), lambda i: (0, 0))
    vec = pl.BlockSpec((1, D), lambda i: (0, 0))
    gates = pl.BlockSpec((None, tt, 2 * D), lambda i: (SLOT_G, i, 0))
    return pl.pallas_call(
        body, name="mid_bwd",
        out_shape=(row(F32), row(BF16), row(BF16), row(BF16), row(F32), row(F32),
                   jax.ShapeDtypeStruct((NCHIP, t, 2 * D), BF16),
                   jax.ShapeDtypeStruct((1, D), F32), jax.ShapeDtypeStruct((1, D), F32)),
        grid=(t // tt,),
        in_specs=[tile, tile, tile, tile, tile, tile, gates, wsp, wsp, wsp, vec, vec],
        out_specs=(tile, tile, tile, tile, tile, tile, gates, vec, vec),
        compiler_params=_params(("arbitrary",), VMEM_BIG),
    )(dh2, dx2, x1, m2, za, zb, p4, wa, wb, wo, g_pm, g_pf)


def _hgrn2_bwd(p4, o_all, ss, dyb, dp4, logits, gnorm, p_early, tt):
    t = p4.shape[1]
    nt = t // tt
    nc = tt // CH
    ne = len(p_early)

    def body(p_ref, o_ref, ss_ref, dyb_ref, dp_in, lg_ref, gn_ref, *rest):
        del dp_in
        pe = rest[:ne]
        dp_ref, dlb_ref, dgn_ref = rest[ne:ne + 3]
        qe = rest[ne + 3:2 * ne + 3]
        dst, ssem, rsem = rest[2 * ne + 3:]
        i = pl.program_id(0)
        sends, arrive = _exchange_copies(qe, [(lambda chip, r=r: r.at[chip]) for r in pe], ssem, rsem)

        @pl.when(i == 0)
        def _():
            dst[...] = jnp.zeros((NH, HD, HD), F32)
            for cp in sends:
                cp.start()

        low = _chunk_masks(tt)
        lb = _lb_of(lg_ref)
        heads = [slice(h * HD, (h + 1) * HD) for h in range(NH)]
        sg, sgn, f, k, g, eg, eng, egu, qt, kt, kd = _hg_head_fwd(p_ref[0, :, 0:D], p_ref[0, :, D:2 * D], lb)
        qtb, ktb, kdb, vb = qt.astype(BF16), kt.astype(BF16), kd.astype(BF16), p_ref[1, :, 0:D].astype(BF16)
        decs = [jnp.exp(g[c * CH + CH - 1:c * CH + CH, :]) for c in range(nc)]
        og = p_ref[1, :, D:2 * D]
        so = _sig(og)
        dyb = dyb_ref[...]
        dob = dyb * (og * so)
        rn = [_rms_stats(o_ref[:, hs]) for hs in heads]
        r_all = jnp.concatenate([jnp.broadcast_to(r, (tt, HD)) for r, _ in rn], axis=1)
        n_all = jnp.concatenate([n for _, n in rn], axis=1)
        gd = dob * gn_ref[...]
        proj = jnp.concatenate(
            [jnp.broadcast_to(jnp.mean(gd[:, hs] * n_all[:, hs], axis=-1, keepdims=True), (tt, HD)) for hs in heads],
            axis=1)
        dob_ = (r_all * (gd - n_all * proj)).astype(BF16)
        dog = dyb * (n_all * gn_ref[...]) * (so * (1.0 + og * (1.0 - so)))
        dgn = jnp.sum(dob * n_all, axis=0, keepdims=True)
        dv_in, dqt_in, dkt_h = [], [], []
        for hs in heads:
            att = jnp.where(low, _dot_nt(qtb[:, hs], ktb[:, hs]), 0.0).astype(BF16)
            d_att = jnp.where(low, _dot_nt(dob_[:, hs], vb[:, hs]), 0.0).astype(BF16)
            dv_in.append(_dot_tn(att, dob_[:, hs]))
            dqt_in.append(_dot(d_att, ktb[:, hs]))
            dkt_h.append(_dot_tn(d_att, qtb[:, hs]))
        ds_t = [dst[h] for h in range(NH)]
        dv_p = [[None] * NH for _ in range(nc)]
        dqt_p = [[None] * NH for _ in range(nc)]
        dkd_p = [[None] * NH for _ in range(nc)]
        dgl_p = [[None] * NH for _ in range(nc)]
        for c in reversed(range(nc)):
            sl = slice(c * CH, (c + 1) * CH)
            for h, hs in enumerate(heads):
                s_prev = ss_ref[c, h]
                ds_bf = ds_t[h].astype(BF16)
                dec = decs[c][:, hs]
                dv_p[c][h] = dv_in[h][sl] + _dot_nt(kdb[sl, hs], ds_bf)
                dqt_p[c][h] = dqt_in[h][sl] + _dot(dob_[sl, hs], s_prev)
                dkd_p[c][h] = _dot(vb[sl, hs], ds_bf)
                ddec = jnp.sum(s_prev.astype(F32) * ds_t[h], axis=0, keepdims=True)
                dgl_p[c][h] = jnp.broadcast_to(ddec * dec, (CH, HD))
                ds_t[h] = ds_t[h] * dec + _dot_tn(dob_[sl, hs], qtb[sl, hs])
        for h in range(NH):
            dst[h] = ds_t[h]
        whole = lambda parts: jnp.concatenate([jnp.concatenate(row, axis=1) for row in parts], axis=0)
        dv, dqt, dkd, dgl = whole(dv_p), whole(dqt_p), whole(dkd_p), whole(dgl_p)
        dkt = jnp.concatenate(dkt_h, axis=1)
        dq = dqt * eg
        dk = dkt * eng + dkd * egu
        dg = dqt * qt - dkt * kt
        dgu = dkd * kd
        dlogf = _chunk_revcumsum(dg - dgu) + _chunk_total(dgu) + dgl
        common = sgn * (dlogf / f - dk)
        dfz = (1.0 - lb) * sg * common
        dlb = jnp.sum(common, axis=0, keepdims=True)
        dp_ref[0, :, 0:D] = dq.astype(BF16)
        dp_ref[0, :, D:2 * D] = dfz.astype(BF16)
        dp_ref[1, :, 0:D] = dv.astype(BF16)
        dp_ref[1, :, D:2 * D] = dog.astype(BF16)

        @pl.when(i == 0)
        def _():
            dlb_ref[0:1, :] = dlb
            dgn_ref[...] = dgn

        @pl.when(i > 0)
        def _():
            dlb_ref[0:1, :] += dlb
            dgn_ref[...] += dgn

        @pl.when(i == nt - 1)
        def _():
            d0 = dlb_ref[0:1, :] * lb * (1.0 - lb)
            dlb_ref[0:1, :] = d0
            dlb_ref[1:2, :] = -d0
            for cp in arrive:
                cp.wait_recv()
            for cp in sends:
                cp.wait_send()

    rev = lambda i: nt - 1 - i
    vec = pl.BlockSpec((1, D), lambda i: (0, 0))
    any_spec = pl.BlockSpec(memory_space=pl.ANY)
    outs = pl.pallas_call(
        body, name="hgrn2_bwd",
        out_shape=(jax.ShapeDtypeStruct(dp4.shape, BF16), jax.ShapeDtypeStruct((2, D), F32),
                   jax.ShapeDtypeStruct((1, D), F32)) + tuple(jax.ShapeDtypeStruct(a.shape, BF16) for a in p_early),
        grid=(nt,),
        in_specs=[pl.BlockSpec((2, tt, 2 * D), lambda i: (0, rev(i), 0)),
                  pl.BlockSpec((tt, D), lambda i: (rev(i), 0)),
                  pl.BlockSpec((nc, NH, HD, HD), lambda i: (rev(i), 0, 0, 0)),
                  pl.BlockSpec((tt, D), lambda i: (rev(i), 0)),
                  any_spec,
                  pl.BlockSpec((2, D), lambda i: (0, 0)), vec] + [any_spec] * ne,
        out_specs=(pl.BlockSpec((2, tt, 2 * D), lambda i: (0, rev(i), 0)),
                   pl.BlockSpec((2, D), lambda i: (0, 0)), vec) + (any_spec,) * ne,
        scratch_shapes=[pltpu.VMEM((NH, HD, HD), F32), pltpu.SemaphoreType.DMA((3 * ne,)),
                        pltpu.SemaphoreType.DMA((3 * ne,))],
        input_output_aliases={4: 0},
        compiler_params=pltpu.CompilerParams(dimension_semantics=("arbitrary",), vmem_limit_bytes=VMEM_BIG,
                                             has_side_effects=True),
    )(p4, o_all, ss, dyb, dp4, logits, gnorm, *p_early)
    return outs[0], outs[1], outs[2], list(outs[3:])


def _mixer_a_bwd(p4, hseq, dya, dp4, cw, cb, wa, wx, ba, bx, lam, tt):
    t = p4.shape[1]
    nt = t // tt
    steps = tt.bit_length() - 1

    def body(p_ref, ph_ref, h_ref, hh_ref, dya_ref, dp_in, cw_ref, cb_ref, wa_ref, wx_ref, ba_ref, bx_ref, lam_ref,
             dp_ref, dcw_ref, dcb_ref, dwa_ref, dwx_ref, dba_ref, dbx_ref, dlam_ref,
             dnext, dhc, afc):
        del dp_in
        i = pl.program_id(0)
        first_tile = i == nt - 1

        @pl.when(i == 0)
        def _():
            dnext[...] = jnp.zeros((8, D), F32)
            dhc[...] = jnp.zeros((1, D), F32)
            afc[...] = jnp.zeros((1, D), F32)

        xa = p_ref[:, 0:D]
        ga = p_ref[:, D:2 * D]
        xe = jnp.concatenate([jnp.where(first_tile, 0.0, ph_ref[:, 0:D]), xa], axis=0)
        xs = [xe[8:, :]] + [pltpu.roll(xe, s, 0)[8:, :] for s in (1, 2, 3)]
        xc = cb_ref[...] + sum(cw_ref[3 - s:4 - s, :] * xs[s] for s in range(4))
        lam = lam_ref[...]
        xcb, r, ig, sp, a, mult = _lru_gates(xc, wa_ref, wx_ref, ba_ref[...], bx_ref[...], lam)
        h = h_ref[...]
        gl, dgl = _gelu_and_grad(ga)
        dya = dya_ref[...]
        dga = dya * h * dgl
        rows = lax.broadcasted_iota(jnp.int32, (tt, 1), 0)
        a_next = jnp.where(rows == tt - 1, afc[...], pltpu.roll(a, tt - 1, 0))
        dh = _scan_up(a_next, dya * gl, dhc[...])
        dhc[...] = dh[0:1, :]
        afc[...] = a[0:1, :]
        h_prev = jnp.where(rows == 0, jnp.where(first_tile, 0.0, hh_ref[7:8, :]), pltpu.roll(h, 1, 0))
        da = dh * h_prev
        dmult = dh * ig * xc
        di = dh * mult * xc
        dlog_a = da * a - dmult * a * a / mult
        dr = dlog_a * (-LRU_C * sp)
        dsp = jnp.sum(dlog_a * (-LRU_C * r), axis=0, keepdims=True)
        dra = dr * r * (1.0 - r)
        dix = di * ig * (1.0 - ig)
        drab = dra.astype(BF16)
        dixb = dix.astype(BF16)
        dxc_lin = []
        dwa_new = []
        dwx_new = []
        for n in range(NH):
            cs = slice(n * HD, (n + 1) * HD)
            dxc_lin.append(_dot_nt(drab[:, cs], wa_ref[n]) + _dot_nt(dixb[:, cs], wx_ref[n]))
            dwa_new.append(_dot_tn(xcb[:, cs], drab[:, cs]))
            dwx_new.append(_dot_tn(xcb[:, cs], dixb[:, cs]))
        dxc = dh * mult * ig + jnp.concatenate(dxc_lin, axis=1)
        de = jnp.concatenate([dxc, dnext[...]], axis=0)
        dxa = (cw_ref[3:4, :] * de
               + sum(cw_ref[3 - s:4 - s, :] * pltpu.roll(de, tt + 8 - s, 0) for s in (1, 2, 3)))[0:tt, :]
        dnext[...] = dxc[0:8, :]
        dp_ref[:, 0:D] = dxa.astype(BF16)
        dp_ref[:, D:2 * D] = dga.astype(BF16)
        dcw = jnp.concatenate(
            [jnp.sum(dxc * xs[3 - k], axis=0, keepdims=True) for k in range(4)], axis=0)
        dcb = jnp.sum(dxc, axis=0, keepdims=True)
        dba = jnp.sum(dra, axis=0, keepdims=True)
        dbx = jnp.sum(dix, axis=0, keepdims=True)
        dlam = dsp * (-_sig(-lam))

        @pl.when(i == 0)
        def _():
            dcw_ref[...] = dcw
            dcb_ref[...] = dcb
            dba_ref[...] = dba
            dbx_ref[...] = dbx
            dlam_ref[...] = dlam
            for n in range(NH):
                dwa_ref[n] = dwa_new[n]
                dwx_ref[n] = dwx_new[n]

        @pl.when(i > 0)
        def _():
            dcw_ref[...] += dcw
            dcb_ref[...] += dcb
            dba_ref[...] += dba
            dbx_ref[...] += dbx
            dlam_ref[...] += dlam
            for n in range(NH):
                dwa_ref[n] += dwa_new[n]
                dwx_ref[n] += dwx_new[n]

    rev = lambda i: nt - 1 - i
    hb = tt // 8
    full = lambda shape: pl.BlockSpec(shape, lambda i: (0,) * len(shape))
    vecs = jax.ShapeDtypeStruct((1, D), F32)
    blk = jax.ShapeDtypeStruct((NH, HD, HD), F32)
    return pl.pallas_call(
        body, name="mixer_a_bwd",
        out_shape=(jax.ShapeDtypeStruct(dp4.shape, BF16), jax.ShapeDtypeStruct((4, D), F32), vecs, blk, blk,
                   vecs, vecs, vecs),
        grid=(nt,),
        in_specs=[pl.BlockSpec((None, tt, 2 * D), lambda i: (SLOT_A, rev(i), 0)),
                  pl.BlockSpec((None, 8, 2 * D), lambda i: (SLOT_A, jnp.maximum(rev(i) * hb - 1, 0), 0)),
                  pl.BlockSpec((tt, D), lambda i: (rev(i), 0)),
                  pl.BlockSpec((8, D), lambda i: (jnp.maximum(rev(i) * hb - 1, 0), 0)),
                  pl.BlockSpec((tt, D), lambda i: (rev(i), 0)),
                  pl.BlockSpec(memory_space=pl.ANY),
                  full((4, D)), full((1, D)), full((NH, HD, HD)), full((NH, HD, HD)),
                  full((1, D)), full((1, D)), full((1, D))],
        out_specs=(pl.BlockSpec((None, tt, 2 * D), lambda i: (SLOT_A, rev(i), 0)),
                   full((4, D)), full((1, D)), full((NH, HD, HD)), full((NH, HD, HD)),
                   full((1, D)), full((1, D)), full((1, D))),
        scratch_shapes=[pltpu.VMEM((8, D), F32), pltpu.VMEM((1, D), F32), pltpu.VMEM((1, D), F32)],
        input_output_aliases={5: 0},
        compiler_params=_params(("arbitrary",), VMEM_BIG),
    )(p4, p4, hseq, hseq, dya, dp4, cw, cb, wa, wx, ba, bx, lam)


def _norm_bwd(dh1, dx1, x, gain, tt):
    t = x.shape[0]

    def body(dh_ref, dx1_ref, x_ref, g_ref, dx_ref, dg_ref):
        i = pl.program_id(0)
        r, n = _rms_stats(x_ref[...])
        dh = dh_ref[...]
        dx_ref[...] = dx1_ref[...] + _rms_bwd(dh * g_ref[...], n, r)
        dgv = jnp.sum(dh * n, axis=0, keepdims=True)

        @pl.when(i == 0)
        def _():
            dg_ref[...] = dgv

        @pl.when(i > 0)
        def _():
            dg_ref[...] += dgv

    tile = pl.BlockSpec((tt, D), lambda i: (i, 0))
    vec = pl.BlockSpec((1, D), lambda i: (0, 0))
    return pl.pallas_call(
        body, name="norm_bwd",
        out_shape=(jax.ShapeDtypeStruct((t, D), F32), jax.ShapeDtypeStruct((1, D), F32)),
        grid=(t // tt,), in_specs=[tile, tile, tile, vec], out_specs=(tile, vec),
        compiler_params=_params(("arbitrary",)),
    )(dh1, dx1, x, gain)


def _local_step(x, tgt, w_in, stk_rest, conv_a_w, conv_f_w, small, cidx):
    t = x.shape[0]
    tt = min(256, t)
    tm = min(1024, t)
    tk = min(2048, t)
    wa_bf = small["lru_wa"].astype(BF16)
    wx_bf = small["lru_wx"].astype(BF16)

    h1 = _norm_fwd(x, small["norm_pre_mix"], tt)
    p4, stk_rest = _mm_in_gather(h1, w_in, stk_rest, REST, tm)
    w = dict(zip(REST, _gather_forward(stk_rest, REST)))
    w["w_in"] = w_in
    w_br_a = w["w_branch_a"].reshape(D, D)
    w_br_b = w["w_branch_b"].reshape(D, D)
    w_out = w["w_out"].reshape(D, D)
    w_down = w["w_down"].reshape(DFF, D)
    ya, hseq = _mixer_a_fwd(p4, conv_a_w, small["conv_a_b"], wa_bf, wx_bf, small["lru_ba"], small["lru_bx"],
                            small["lru_lambda"], tt)
    yb, o_all, ss = _hgrn2_fwd(p4, small["hg_lb_logits"], small["hg_norm_g"], tt)
    za, zb, mixb, m2, x1, h2 = _mid_fwd(ya, yb, p4, x, w_br_a, w_br_b, w_out, small["norm_post_mix"],
                                        small["norm_pre_ffn"], min(512, t))
    up_pre = _mm_nn_sharded(h2, w["w_up"], F32, tm, "mm_up")
    tf = min(128, t)
    y, uv, gl, dgl = _ffn_act_fwd(up_pre, conv_f_w, small["conv_f_b"], tf)
    dx2, dm3, lossv, d_norm_post_ffn = _down_loss(y, w_down, x1, tgt, small["norm_post_ffn"], min(512, t))

    d_w_down = _mm_tn(y, dm3, DFF // 2, D, tk, "mm_dw_down")
    dy = _mm_nt(dm3, w_down, BF16, tm, "mm_dy")
    dup_pre, d_conv_f_w, d_conv_f_b = _ffn_act_bwd(dy, up_pre, uv, gl, dgl, conv_f_w, tt)
    d_w_up = _mm_tn(h2, dup_pre, D, SH_UP, tk, "mm_dw_up", stacked_out=True)
    dh2 = _mm_nt_sharded(dup_pre, w["w_up"], tm, "mm_dh2")
    dx1, dm2, dza, dzb, dya, dyb, dp4, d_norm_post_mix, d_norm_pre_ffn = _mid_bwd(
        dh2, dx2, x1, m2, za, zb, p4, w_br_a, w_br_b, w_out, small["norm_post_mix"], small["norm_pre_ffn"], tt)
    d_w_out = _mm_tn(mixb, dm2, D, D, tm, "mm_dw_out")
    d_w_br_a = _mm_tn(ya, dza, D, D, tm, "mm_dw_bra")
    d_w_br_b = _mm_tn(yb, dzb, D, D, tm, "mm_dw_brb")
    early = {"w_branch_a": d_w_br_a.reshape(NCHIP, SH_BR, D), "w_branch_b": d_w_br_b.reshape(NCHIP, SH_BR, D),
             "w_out": d_w_out.reshape(NCHIP, SH_BR, D), "w_up": d_w_up, "w_down": d_w_down.reshape(NCHIP, SH_DN, D)}
    rb, _ = _reduce_stage1(early, REST, (), "reduce_d2d_in_early")
    p_rest = [_sum_own_half(early[n], rb[n], cidx, "sum_half_" + n) for n in REST]
    dp4, d_lb, d_hg_norm_g, q_rest = _hgrn2_bwd(p4, o_all, ss, dyb, dp4, small["hg_lb_logits"], small["hg_norm_g"],
                                                p_rest, tt)
    dp4, d_conv_a_w, d_conv_a_b, d_lru_wa, d_lru_wx, d_lru_ba, d_lru_bx, d_lru_lambda = _mixer_a_bwd(
        p4, hseq, dya, dp4, conv_a_w, small["conv_a_b"], wa_bf, wx_bf, small["lru_ba"], small["lru_bx"],
        small["lru_lambda"], tt)
    d_w_in = _mm_tn(h1, dp4, D, SH_IN, tk, "mm_dw_in", stacked_slot_fn=_slot_of_chip, stacked_out=True)
    rb, _ = _reduce_stage1({"w_in": d_w_in}, ("w_in",), (), "reduce_d2d_in_w_in")
    p_w_in = _sum_own_half(d_w_in, rb["w_in"], cidx, "sum_half_w_in")
    dh1, q_w_in = _mm_dh1_exchange(dp4, w_in, p_w_in, tm)
    grad_x, d_norm_pre_mix = _norm_bwd(dh1, dx1, x, small["norm_pre_mix"], tt)

    smalls = {
        "norm_pre_mix": d_norm_pre_mix, "conv_a_b": d_conv_a_b, "lru_ba": d_lru_ba, "lru_bx": d_lru_bx,
        "lru_lambda": d_lru_lambda, "hg_lb_logits": d_lb, "hg_norm_g": d_hg_norm_g, "norm_post_mix": d_norm_post_mix,
        "norm_pre_ffn": d_norm_pre_ffn, "norm_post_ffn": d_norm_post_ffn, "lossv": lossv,
        "conv_a_w": d_conv_a_w, "lru_wa": d_lru_wa, "lru_wx": d_lru_wx,
        "conv_f_b": d_conv_f_b, "conv_f_w": d_conv_f_w,
    }
    p_big = dict(zip(REST, p_rest), w_in=p_w_in)
    q_big = dict(zip(REST, q_rest), w_in=q_w_in)
    return grad_x, p_big, q_big, smalls


BIG = ("w_in", "w_branch_a", "w_branch_b", "w_out", "w_up", "w_down")
BIG_SHAPE = {"w_in": (D, SH_IN), "w_branch_a": (SH_BR, D), "w_branch_b": (SH_BR, D), "w_out": (SH_BR, D),
             "w_up": (D, SH_UP), "w_down": (SH_DN, D)}
NBIG = len(BIG)
REST = BIG[1:]
VEC_ROWS = (("norm_pre_mix", 0, 1), ("conv_a_b", 1, 1), ("lru_ba", 2, 1), ("lru_bx", 3, 1), ("lru_lambda", 4, 1),
            ("hg_lb_logits", 5, 2), ("hg_norm_g", 7, 1), ("norm_post_mix", 8, 1), ("norm_pre_ffn", 9, 1),
            ("norm_post_ffn", 10, 1))
ROW_LOSS = 11
ROW_CONV_A = 12
S1_ROWS = 16
S2_ROWS = 8


def _place():
    x, y, c = lax.axis_index("x"), lax.axis_index("y"), lax.axis_index("c")
    chips = [(1 - x, y), (x, 1 - y), (1 - x, 1 - y)]
    return x, y, c, 2 * x + y, chips


def _remote(src, dst, ssem, rsem, dev):
    return pltpu.make_async_remote_copy(src_ref=src, dst_ref=dst, send_sem=ssem, recv_sem=rsem,
                                        device_id=dev, device_id_type=MESH)


def _hbm_call(body, name, ins, out_shapes, n_sems, aliases=None):
    any_spec = pl.BlockSpec(memory_space=pl.ANY)
    return pl.pallas_call(
        body, name=name, out_shape=tuple(out_shapes),
        in_specs=[any_spec] * len(ins), out_specs=tuple([any_spec] * len(out_shapes)),
        scratch_shapes=[pltpu.SemaphoreType.DMA((n,)) for n in n_sems],
        input_output_aliases=aliases or {},
        compiler_params=pltpu.CompilerParams(has_side_effects=True),
    )(*ins)


def _gather_copies(stk, names, ssem, rsem, fssem=None, frsem=None):
    x, y, c, j, chips = _place()
    sends, arrive, fwds, farrive = [], [], [], []
    for w, n in enumerate(names):
        hw = BIG_SHAPE[n][0] // 2
        mine = stk[w].at[j, pl.ds(c * hw, hw), :]
        for k, (cx, cy) in enumerate(chips):
            i = 3 * w + k
            got = stk[w].at[2 * cx + cy, pl.ds(c * hw, hw), :]
            other = stk[w].at[2 * cx + cy, pl.ds((1 - c) * hw, hw), :]
            sends.append(_remote(mine, mine, ssem.at[i], rsem.at[i], (cx, cy, c)))
            arrive.append(_remote(got, got, ssem.at[i], rsem.at[i], (cx, cy, c)))
            if fssem is not None:
                fwds.append(_remote(got, got, fssem.at[i], frsem.at[i], (x, y, 1 - c)))
                farrive.append(_remote(other, other, fssem.at[i], frsem.at[i], (x, y, 1 - c)))
    return sends, arrive, fwds, farrive


def _gather_first(stacked_w_in, conv_a_s, conv_f_s):
    ins = [stacked_w_in, conv_a_s, conv_f_s]
    out_shapes = [jax.ShapeDtypeStruct(stacked_w_in.shape, stacked_w_in.dtype)]
    out_shapes += [jax.ShapeDtypeStruct((NCHIP,) + a.shape, a.dtype) for a in (conv_a_s, conv_f_s)]

    def body(w_in, ca_src, cf_src, w_out, ca_dst, cf_dst, ssem, rsem, fssem, frsem, csend, crecv, lsem):
        del w_in
        x, y, c, j, chips = _place()
        conv = ((ca_src, ca_dst), (cf_src, cf_dst))
        locs = [pltpu.make_async_copy(src, dst.at[j], lsem.at[i]) for i, (src, dst) in enumerate(conv)]
        csends = [_remote(src, dst.at[j], csend.at[3 * i + k], crecv.at[3 * i + k], (cx, cy, c))
                  for i, (src, dst) in enumerate(conv) for k, (cx, cy) in enumerate(chips)]
        sends, arrive, fwds, farrive = _gather_copies([w_out], ("w_in",), ssem, rsem, fssem, frsem)
        for cp in locs + sends + csends:
            cp.start()
        for got, fwd in zip(arrive, fwds):
            got.wait_recv()
            fwd.start()
        for i, (_, dst) in enumerate(conv):
            for k, (cx, cy) in enumerate(chips):
                got = dst.at[2 * cx + cy]
                _remote(got, got, csend.at[3 * i + k], crecv.at[3 * i + k], (cx, cy, c)).wait_recv()
        for cp in farrive:
            cp.wait_recv()
        for cp in sends + fwds + csends:
            cp.wait_send()
        for cp in locs:
            cp.wait()

    return _hbm_call(body, "gather_first", ins, out_shapes, (3, 3, 3, 3, 6, 6, 2), aliases={0: 0})


def _gather_forward(stk, names):
    nw = len(names)

    def body(*refs):
        dst = refs[nw:2 * nw]
        ssem, rsem, fssem, frsem = refs[2 * nw:]
        _, _, fwds, farrive = _gather_copies(dst, names, ssem, rsem, fssem, frsem)
        for cp in fwds:
            cp.start()
        for cp in farrive:
            cp.wait_recv()
        for cp in fwds:
            cp.wait_send()

    out_shapes = [jax.ShapeDtypeStruct(a.shape, a.dtype) for a in stk]
    return _hbm_call(body, "gather_forward", stk, out_shapes, (3 * nw,) * 4, aliases={w: w for w in range(nw)})


def _exchange_copies(dst, pieces, ssem, rsem):
    x, y, c, j, chips = _place()
    sends, arrive = [], []
    for w in range(len(dst)):
        for k, (cx, cy) in enumerate(chips):
            i = 3 * w + k
            sends.append(_remote(pieces[w](2 * cx + cy), dst[w].at[j], ssem.at[i], rsem.at[i], (cx, cy, c)))
            got = dst[w].at[2 * cx + cy]
            arrive.append(_remote(got, got, ssem.at[i], rsem.at[i], (cx, cy, c)))
    return sends, arrive


def _reduce_stage1(big_g, names, smalls, name):
    nb = len(names)
    ins = [big_g[n] for n in names] + list(smalls)
    n_in = len(ins)
    halves = [BIG_SHAPE[n][0] // 2 for n in names]
    out_shapes = [jax.ShapeDtypeStruct((NCHIP, halves[w], BIG_SHAPE[n][1]), F32) for w, n in enumerate(names)]
    out_shapes += [jax.ShapeDtypeStruct(a.shape, F32) for a in smalls]

    def body(*refs):
        src, dst = refs[:n_in], refs[n_in:2 * n_in]
        ssem, rsem = refs[2 * n_in:]
        x, y, c, _, _ = _place()
        cps = []
        for w in range(n_in):
            s_ = src[w].at[:, pl.ds((1 - c) * halves[w], halves[w]), :] if w < nb else src[w]
            cp = _remote(s_, dst[w], ssem.at[w], rsem.at[w], (x, y, 1 - c))
            cp.start()
            cps.append(cp)
        for cp in cps:
            cp.wait()

    outs = _hbm_call(body, name, ins, out_shapes, (n_in, n_in))
    return dict(zip(names, outs[:nb])), outs[nb:]


def _reduce_stage2(ps1, ps2, ps3):
    ins = [ps1, ps2, ps3]
    h1, h2, h3 = S1_ROWS // 2, DUP // 2, D
    out_shapes = [jax.ShapeDtypeStruct((NCHIP, h1, D), F32), jax.ShapeDtypeStruct((NCHIP, S2_ROWS, h2), F32),
                  jax.ShapeDtypeStruct((NCHIP, h3, HD), F32)]

    def body(*refs):
        src, dst = refs[:3], refs[3:6]
        ssem, rsem = refs[6:]
        c = lax.axis_index("c")
        pieces = [lambda chip: src[0].at[pl.ds(c * h1, h1), :],
                  lambda chip: src[1].at[:, pl.ds(c * h2, h2)],
                  lambda chip: src[2].at[pl.ds(c * h3, h3), :]]
        sends, arrive = _exchange_copies(dst, pieces, ssem, rsem)
        for cp in sends:
            cp.start()
        for cp in arrive:
            cp.wait_recv()
        for cp in sends:
            cp.wait_send()

    return _hbm_call(body, "reduce_ici_small", ins, out_shapes, (9, 9))


def _reduce_stage3(f_big, fs1, fs2, fs3):
    ins = [f_big[n] for n in BIG] + [fs1, fs2, fs3]
    n_in = len(ins)
    halves = [BIG_SHAPE[n][0] // 2 for n in BIG]
    h1, h2, h3 = S1_ROWS // 2, DUP // 2, D
    out_shapes = [jax.ShapeDtypeStruct(BIG_SHAPE[n], F32) for n in BIG]
    out_shapes += [jax.ShapeDtypeStruct((S1_ROWS, D), F32), jax.ShapeDtypeStruct((S2_ROWS, DUP), F32),
                   jax.ShapeDtypeStruct((2 * D, HD), F32)]

    def body(*refs):
        dst = refs[n_in:2 * n_in]
        ssem, rsem = refs[2 * n_in:]
        x, y, c, _, _ = _place()

        def place(w, which):
            if w < NBIG:
                return dst[w].at[pl.ds(which * halves[w], halves[w]), :]
            if w == NBIG:
                return dst[w].at[pl.ds(which * h1, h1), :]
            if w == NBIG + 1:
                return dst[w].at[:, pl.ds(which * h2, h2)]
            return dst[w].at[pl.ds(which * h3, h3), :]

        cps = [_remote(place(w, c), place(w, c), ssem.at[w], rsem.at[w], (x, y, 1 - c)) for w in range(n_in)]
        for cp in cps:
            cp.start()
        for w in range(n_in):
            got = place(w, 1 - c)
            _remote(got, got, ssem.at[w], rsem.at[w], (x, y, 1 - c)).wait_recv()
        for cp in cps:
            cp.wait_send()

    outs = _hbm_call(body, "reduce_d2d_out", ins, out_shapes, (n_in, n_in), aliases={w: w for w in range(n_in)})
    return dict(zip(BIG, outs[:NBIG])), outs[NBIG], outs[NBIG + 1], outs[NBIG + 2]


def _row_tile(rows):
    for tr in (128, 176, 64, 16, 8):
        if rows % tr == 0:
            return tr
    return rows


def _sum_own_half(g, rb, cidx, name):
    s, rows, cols = g.shape
    half = rows // 2
    tr = _row_tile(half)
    nb = half // tr

    def body(c_ref, g_ref, r_ref, o_ref):
        del c_ref
        o_ref[...] = (g_ref[...] + r_ref[...]).astype(BF16)

    grid_spec = pltpu.PrefetchScalarGridSpec(
        num_scalar_prefetch=1, grid=(s, nb),
        in_specs=[pl.BlockSpec((None, tr, cols), lambda k, i, c: (k, c[0] * nb + i, 0)),
                  pl.BlockSpec((None, tr, cols), lambda k, i, c: (k, i, 0))],
        out_specs=pl.BlockSpec((None, tr, cols), lambda k, i, c: (k, i, 0)))
    return pl.pallas_call(
        body, name=name, grid_spec=grid_spec, out_shape=jax.ShapeDtypeStruct((s, half, cols), BF16),
        compiler_params=_params(("parallel", "parallel")),
    )(cidx, g, rb)


def _sum_chips(q, p, jc, name, by_cols=False):
    s, rows, cols = q.shape
    tr = _row_tile(rows)
    nb = rows // tr
    stacked = p.ndim == 3

    def body(jc_ref, q_ref, p_ref, o_ref):
        j = jc_ref[0]
        own = p_ref[...].astype(F32)
        acc = None
        for k in range(NCHIP):
            term = jnp.where(j == k, own, q_ref[k].astype(F32))
            acc = term if acc is None else acc + term
        o_ref[...] = acc

    if by_cols:
        half_spec = pl.BlockSpec((tr, cols), lambda i, jc_ref: (i, jc_ref[1]))
        out_shape = jax.ShapeDtypeStruct((rows, 2 * cols), F32)
    else:
        half_spec = pl.BlockSpec((tr, cols), lambda i, jc_ref: (jc_ref[1] * nb + i, 0))
        out_shape = jax.ShapeDtypeStruct((2 * rows, cols), F32)
    p_spec = pl.BlockSpec((None, tr, cols), lambda i, jc_ref: (jc_ref[0], i, 0)) if stacked else half_spec
    grid_spec = pltpu.PrefetchScalarGridSpec(
        num_scalar_prefetch=1, grid=(nb,),
        in_specs=[pl.BlockSpec((s, tr, cols), lambda i, jc_ref: (0, i, 0)), p_spec],
        out_specs=half_spec)
    return pl.pallas_call(
        body, name=name, grid_spec=grid_spec, out_shape=out_shape,
        compiler_params=_params(("parallel",)),
    )(jc, q, p)


def _place_shard(w, jc, name):
    rows, cols = w.shape
    tr = _row_tile(rows)

    def body(jc_ref, w_ref, o_ref):
        del jc_ref
        o_ref[...] = w_ref[...].astype(BF16)

    grid_spec = pltpu.PrefetchScalarGridSpec(
        num_scalar_prefetch=1, grid=(rows // tr,),
        in_specs=[pl.BlockSpec((tr, cols), lambda i, jc_ref: (i, 0))],
        out_specs=pl.BlockSpec((None, tr, cols), lambda i, jc_ref: (jc_ref[0], i, 0)))
    return pl.pallas_call(
        body, name=name, grid_spec=grid_spec, out_shape=jax.ShapeDtypeStruct((NCHIP, rows, cols), BF16),
        compiler_params=_params(("parallel",)),
    )(jc, w)


def _add(a, b, name):
    def body(a_ref, b_ref, o_ref):
        o_ref[...] = a_ref[...] + b_ref[...]

    return pl.pallas_call(body, name=name, out_shape=jax.ShapeDtypeStruct(a.shape, F32))(a, b)


def _pack_small(sm):
    vec_in = [sm[n] for n, _, _ in VEC_ROWS]
    nv = len(vec_in)

    def body(*refs):
        ins, lossv, dcw, dcfb, dcfw, s1, s2 = refs[:nv], refs[nv], refs[nv + 1], refs[nv + 2], refs[nv + 3], \
            refs[nv + 4], refs[nv + 5]
        for ref, (_, r0, nr) in zip(ins, VEC_ROWS):
            s1[r0:r0 + nr, :] = ref[...]
        s1[ROW_LOSS:ROW_LOSS + 1, :] = lossv[...]
        s1[ROW_CONV_A:ROW_CONV_A + 4, :] = dcw[...]
        s2[0:1, :] = dcfb[...]
        s2[1:4, :] = dcfw[...]
        s2[4:8, :] = jnp.zeros((4, DUP), F32)

    return pl.pallas_call(
        body, name="pack_small",
        out_shape=(jax.ShapeDtypeStruct((S1_ROWS, D), F32), jax.ShapeDtypeStruct((S2_ROWS, DUP), F32)),
    )(*vec_in, sm["lossv"], sm["conv_a_w"], sm["conv_f_b"], sm["conv_f_w"])


def _adam_math(w, g, m, v):
    m = ADAM_B1 * m + (1.0 - ADAM_B1) * g
    v = ADAM_B2 * v + (1.0 - ADAM_B2) * (g * g)
    m_hat = m / (1.0 - ADAM_B1 ** ADAM_STEP)
    v_hat = v / (1.0 - ADAM_B2 ** ADAM_STEP)
    delta = -ADAM_LR * (m_hat / (jnp.sqrt(v_hat) + ADAM_EPS) + ADAM_WD * w)
    return delta, m, v


def _adam(w, g, m, v, name):
    rows, cols = w.shape
    tr = _row_tile(rows)

    def body(w_ref, g_ref, m_ref, v_ref, d_ref, mo_ref, vo_ref):
        d_ref[...], mo_ref[...], vo_ref[...] = _adam_math(w_ref[...], g_ref[...], m_ref[...], v_ref[...])

    spec = pl.BlockSpec((tr, cols), lambda i: (i, 0))
    return pl.pallas_call(
        body, name=name, out_shape=(jax.ShapeDtypeStruct(w.shape, F32),) * 3, grid=(rows // tr,),
        in_specs=[spec] * 4, out_specs=(spec,) * 3,
        compiler_params=_params(("parallel",)),
    )(w, g, m, v)


def _adam_small(gs1, gs2, gs3, w, m, v):
    names = [n for n, _, _ in VEC_ROWS] + ["conv_f_b", "lru_wa", "lru_wx"]
    nn = len(names)

    def grad_of(i, g1, g2, g3):
        if i < len(VEC_ROWS):
            _, r0, nr = VEC_ROWS[i]
            return g1[r0:r0 + nr, :]
        if names[i] == "conv_f_b":
            return g2[0:1, :]
        return g3[0] if names[i] == "lru_wa" else g3[1]

    def body(*refs):
        g1, g2, g3 = refs[0], refs[1], refs[2]
        ws, ms, vs = refs[3:3 + nn], refs[3 + nn:3 + 2 * nn], refs[3 + 2 * nn:3 + 3 * nn]
        outs = refs[3 + 3 * nn:]
        for i in range(nn):
            d, mn, vn = _adam_math(ws[i][...], grad_of(i, g1, g2, g3), ms[i][...], vs[i][...])
            outs[i][...] = d
            outs[nn + i][...] = mn
            outs[2 * nn + i][...] = vn

    shapes = [jax.ShapeDtypeStruct(w[n].shape, F32) for n in names]
    outs = pl.pallas_call(body, name="adam_small", out_shape=tuple(shapes * 3))(
        gs1, gs2, gs3, *[w[n] for n in names], *[m[n] for n in names], *[v[n] for n in names])
    return {n: (outs[i], outs[nn + i], outs[2 * nn + i]) for i, n in enumerate(names)}


WEIGHTS = ("norm_pre_mix", "w_in", "conv_a_w", "conv_a_b", "lru_wa", "lru_ba", "lru_wx", "lru_bx", "lru_lambda",
           "hg_lb_logits", "hg_norm_g", "w_branch_a", "w_branch_b", "w_out", "norm_post_mix", "norm_pre_ffn",
           "w_up", "conv_f_w", "conv_f_b", "w_down", "norm_post_ffn")
NW = len(WEIGHTS)


def kernel(x, norm_pre_mix, w_in, conv_a_w, conv_a_b, lru_wa, lru_ba, lru_wx, lru_bx, lru_lambda, hg_lb_logits, hg_norm_g, w_branch_a, w_branch_b, w_out, norm_post_mix, norm_pre_ffn, w_up, conv_f_w, conv_f_b, w_down, norm_post_ffn, loss_target, m_norm_pre_mix, m_w_in, m_conv_a_w, m_conv_a_b, m_lru_wa, m_lru_ba, m_lru_wx, m_lru_bx, m_lru_lambda, m_hg_lb_logits, m_hg_norm_g, m_w_branch_a, m_w_branch_b, m_w_out, m_norm_post_mix, m_norm_pre_ffn, m_w_up, m_conv_f_w, m_conv_f_b, m_w_down, m_norm_post_ffn, v_norm_pre_mix, v_w_in, v_conv_a_w, v_conv_a_b, v_lru_wa, v_lru_ba, v_lru_wx, v_lru_bx, v_lru_lambda, v_hg_lb_logits, v_hg_norm_g, v_w_branch_a, v_w_branch_b, v_w_out, v_norm_post_mix, v_norm_pre_ffn, v_w_up, v_conv_f_w, v_conv_f_b, v_w_down, v_norm_post_ffn):
    rest = (norm_pre_mix, w_in, conv_a_w, conv_a_b, lru_wa, lru_ba, lru_wx, lru_bx, lru_lambda, hg_lb_logits, hg_norm_g, w_branch_a, w_branch_b, w_out, norm_post_mix, norm_pre_ffn, w_up, conv_f_w, conv_f_b, w_down, norm_post_ffn, loss_target, m_norm_pre_mix, m_w_in, m_conv_a_w, m_conv_a_b, m_lru_wa, m_lru_ba, m_lru_wx, m_lru_bx, m_lru_lambda, m_hg_lb_logits, m_hg_norm_g, m_w_branch_a, m_w_branch_b, m_w_out, m_norm_post_mix, m_norm_pre_ffn, m_w_up, m_conv_f_w, m_conv_f_b, m_w_down, m_norm_post_ffn, v_norm_pre_mix, v_w_in, v_conv_a_w, v_conv_a_b, v_lru_wa, v_lru_ba, v_lru_wx, v_lru_bx, v_lru_lambda, v_hg_lb_logits, v_hg_norm_g, v_w_branch_a, v_w_branch_b, v_w_out, v_norm_post_mix, v_norm_pre_ffn, v_w_up, v_conv_f_w, v_conv_f_b, v_w_down, v_norm_post_ffn)
    w_in_args = dict(zip(WEIGHTS, rest[:NW]))
    loss_target = rest[NW]
    m_args = dict(zip(WEIGHTS, rest[NW + 1:2 * NW + 1]))
    v_args = dict(zip(WEIGHTS, rest[2 * NW + 1:3 * NW + 1]))
    shape_of = {n: w_in_args[n].shape for n in WEIGHTS}

    def two_d(n, a):
        if n in BIG:
            return a.reshape(BIG_SHAPE[n])
        if n in ("lru_wa", "lru_wx"):
            return a.reshape(NH, HD, HD)
        return a.reshape(a.shape[-2:])

    w2 = {n: two_d(n, w_in_args[n]) for n in WEIGHTS}
    m2 = {n: two_d(n, m_args[n]) for n in WEIGHTS}
    v2 = {n: two_d(n, v_args[n]) for n in WEIGHTS}

    cidx = lax.axis_index("c").astype(jnp.int32).reshape(1)
    jchip = 2 * lax.axis_index("x") + lax.axis_index("y")

    jc = jnp.stack([jchip, lax.axis_index("c")]).astype(jnp.int32)

    shards = {n: _place_shard(w2[n], jc, "place_" + n) for n in BIG}
    conv_a_s = jnp.pad(w2["conv_a_w"], ((0, 4), (0, 0)))
    conv_f_s = jnp.pad(w2["conv_f_w"], ((0, 5), (0, 0)))
    w_in_full, conv_a_g, conv_f_g = _gather_first(shards["w_in"], conv_a_s, conv_f_s)
    conv_a_full = jnp.transpose(conv_a_g, (1, 0, 2)).reshape(8, D)[0:4]
    conv_f_full = jnp.transpose(conv_f_g, (1, 0, 2)).reshape(8, DUP)[0:3]
    small = {n: w2[n] for n in WEIGHTS if n not in BIG and n not in ("conv_a_w", "conv_f_w")}

    grad_x, p_big, q_big, sm_g = _local_step(
        x[0], loss_target[0], w_in_full, [shards[n] for n in REST], conv_a_full, conv_f_full, small, cidx)

    s1, s2 = _pack_small(sm_g)
    s3 = jnp.concatenate([sm_g["lru_wa"].reshape(D, HD), sm_g["lru_wx"].reshape(D, HD)], axis=0)
    _, (rs1, rs2, rs3) = _reduce_stage1({}, (), (s1, s2, s3), "reduce_d2d_in_small")
    ps1, ps2, ps3 = _add(s1, rs1, "add_s1"), _add(s2, rs2, "add_s2"), _add(s3, rs3, "add_s3")
    qs1, qs2, qs3 = _reduce_stage2(ps1, ps2, ps3)
    f_big = {n: _sum_chips(q_big[n], p_big[n], jc, "sum_chips_" + n) for n in BIG}
    fs1 = _sum_chips(qs1, ps1, jc, "sum_chips_s1")
    fs2 = _sum_chips(qs2, ps2, jc, "sum_chips_s2", by_cols=True)
    fs3 = _sum_chips(qs3, ps3, jc, "sum_chips_s3")
    g_big, gs1, gs2, gs3 = _reduce_stage3(f_big, fs1, fs2, fs3)

    res = {}
    for n in BIG:
        d, mn, vn = _adam(w2[n], g_big[n], m2[n], v2[n], "adam_" + n)
        res[n] = (g_big[n], d, mn, vn)
    small_res = _adam_small(gs1, gs2, gs3.reshape(2, NH, HD, HD), w2, m2, v2)
    for n, r0, nr in VEC_ROWS:
        res[n] = (gs1[r0:r0 + nr],) + small_res[n]
    res["conv_f_b"] = (gs2[0:1],) + small_res["conv_f_b"]
    res["lru_wa"] = (gs3[0:D].reshape(NH, HD, HD),) + small_res["lru_wa"]
    res["lru_wx"] = (gs3[D:2 * D].reshape(NH, HD, HD),) + small_res["lru_wx"]
    g_ca = lax.dynamic_slice_in_dim(gs1[ROW_CONV_A:ROW_CONV_A + 4], jchip * (D // NCHIP), D // NCHIP, axis=1)
    g_cf = lax.dynamic_slice_in_dim(gs2[1:4], jchip * SH_UP, SH_UP, axis=1)
    res["conv_a_w"] = (g_ca,) + _adam(w2["conv_a_w"], g_ca, m2["conv_a_w"], v2["conv_a_w"], "adam_conv_a_w")
    res["conv_f_w"] = (g_cf,) + _adam(w2["conv_f_w"], g_cf, m2["conv_f_w"], v2["conv_f_w"], "adam_conv_f_w")

    loss = (0.5 / D) * jnp.sum(gs1[ROW_LOSS])
    out = [loss, grad_x.reshape(x.shape)]
    for part in range(4):
        out += [res[n][part].reshape(shape_of[n]) for n in WEIGHTS]
    return tuple(out)
```

```python
import functools

import jax
import jax.numpy as jnp
from jax import lax
from jax.experimental import pallas as pl
from jax.experimental.pallas import tpu as pltpu

F32 = jnp.float32
BF16 = jnp.bfloat16

D = 1024
NH = 8
HD = 128
CH = 32
DFF = 2816
DUP = 2 * DFF
NCHIP = 4
SH_IN = 2 * D
SH_UP = DUP // NCHIP
SH_DN = DFF // NCHIP
SH_BR = D // NCHIP
EPS = 1e-6
LRU_C = 8.0
ADAM_LR = 0.001
ADAM_B1 = 0.9
ADAM_B2 = 0.999
ADAM_EPS = 1e-08
ADAM_WD = 0.01
ADAM_STEP = 10
VMEM_BIG = 56 * 1024 * 1024
MESH = pl.DeviceIdType.MESH

SLOT_A, SLOT_B, SLOT_C, SLOT_G = 2, 0, 1, 3


def _slot_of_chip(s):
    return jnp.where(s == 3, 3, (s + 2) % 3)


def _params(sem, vmem=None):
    return pltpu.CompilerParams(dimension_semantics=sem, vmem_limit_bytes=vmem)


_GC = 0.7978845608028654
_GA = 0.044715


def _gelu(x):
    return 0.5 * x * (1.0 + jnp.tanh(_GC * (x + _GA * x * x * x)))


def _gelu_and_grad(x):
    x2 = x * x
    th = jnp.tanh(_GC * x * (1.0 + _GA * x2))
    g = 0.5 * x * (1.0 + th)
    dg = 0.5 * (1.0 + th) + 0.5 * x * (1.0 - th * th) * _GC * (1.0 + 3.0 * _GA * x2)
    return g, dg


def _sig(x):
    return jax.nn.sigmoid(x)


def _dot(a, b):
    return jnp.dot(a, b, preferred_element_type=F32)


def _dot_nt(a, b):
    return lax.dot_general(a, b, (((1,), (1,)), ((), ())), preferred_element_type=F32)


def _dot_tn(a, b):
    return lax.dot_general(a, b, (((0,), (0,)), ((), ())), preferred_element_type=F32)


def _chunk_cumsum(x):
    pos = lax.broadcasted_iota(jnp.int32, (x.shape[0], 1), 0) & (CH - 1)
    d = 1
    while d < CH:
        x = x + jnp.where(pos >= d, pltpu.roll(x, d, 0), 0.0)
        d *= 2
    return x


def _chunk_revcumsum(x):
    n = x.shape[0]
    pos = lax.broadcasted_iota(jnp.int32, (n, 1), 0) & (CH - 1)
    d = 1
    while d < CH:
        x = x + jnp.where(pos < CH - d, pltpu.roll(x, n - d, 0), 0.0)
        d *= 2
    return x


def _chunk_last(x):
    n = x.shape[0]
    return jnp.concatenate(
        [jnp.broadcast_to(x[c * CH + CH - 1:c * CH + CH, :], (CH, x.shape[1])) for c in range(n // CH)], axis=0)


def _chunk_total(x):
    n = x.shape[0]
    return jnp.concatenate(
        [jnp.broadcast_to(jnp.sum(x[c * CH:(c + 1) * CH, :], axis=0, keepdims=True), (CH, x.shape[1]))
         for c in range(n // CH)], axis=0)


def _rms_stats(x):
    r = lax.rsqrt(jnp.mean(x * x, axis=-1, keepdims=True) + EPS)
    return r, x * r


def _rms_bwd(gd, n, r):
    return r * (gd - n * jnp.mean(gd * n, axis=-1, keepdims=True))


def _shift_rows(x, d, fill):
    rows = lax.broadcasted_iota(jnp.int32, (x.shape[0], 1), 0)
    return jnp.where(rows >= d, pltpu.roll(x, d, 0), fill)


def _scan_down(a, u, carry):
    n = a.shape[0]
    pos = lax.broadcasted_iota(jnp.int32, (n, 1), 0) & 7
    for d in (1, 2, 4):
        u = a * jnp.where(pos >= d, pltpu.roll(u, d, 0), 0.0) + u
        a = a * jnp.where(pos >= d, pltpu.roll(a, d, 0), 1.0)
    out = []
    for v in range(n // 8):
        h = a[v * 8:v * 8 + 8, :] * carry + u[v * 8:v * 8 + 8, :]
        carry = h[7:8, :]
        out.append(h)
    return jnp.concatenate(out, axis=0)


def _scan_up(b, g, carry):
    n = b.shape[0]
    pos = lax.broadcasted_iota(jnp.int32, (n, 1), 0) & 7
    for d in (1, 2, 4):
        g = g + b * jnp.where(pos < 8 - d, pltpu.roll(g, n - d, 0), 0.0)
        b = b * jnp.where(pos < 8 - d, pltpu.roll(b, n - d, 0), 1.0)
    out = [None] * (n // 8)
    for v in reversed(range(n // 8)):
        h = g[v * 8:v * 8 + 8, :] + b[v * 8:v * 8 + 8, :] * carry
        carry = h[0:1, :]
        out[v] = h
    return jnp.concatenate(out, axis=0)


def _shift_rows_up(x, d, fill):
    n = x.shape[0]
    rows = lax.broadcasted_iota(jnp.int32, (n, 1), 0)
    return jnp.where(rows < n - d, pltpu.roll(x, n - d, 0), fill)


def _mm_nn_sharded(a, b3, out_dtype, tm, name, slot_fn=None):
    m, k = a.shape
    s, _, ns = b3.shape

    def body(a_ref, b_ref, o_ref):
        o_ref[...] = _dot(a_ref[...], b_ref[...]).astype(out_dtype)

    if slot_fn is None:
        out_shape = jax.ShapeDtypeStruct((m, s * ns), out_dtype)
        out_spec = pl.BlockSpec((tm, ns), lambda j, i: (i, j))
    else:
        out_shape = jax.ShapeDtypeStruct((s, m, ns), out_dtype)
        out_spec = pl.BlockSpec((None, tm, ns), lambda j, i: (slot_fn(j), i, 0))
    return pl.pallas_call(
        body, name=name, out_shape=out_shape, grid=(s, m // tm),
        in_specs=[pl.BlockSpec((tm, k), lambda j, i: (i, 0)),
                  pl.BlockSpec((None, k, ns), lambda j, i: (j, 0, 0))],
        out_specs=out_spec,
        compiler_params=_params(("parallel", "parallel"), VMEM_BIG),
    )(a, b3)


def _mm_in_gather(a, w_in3, stk_rest, names_rest, tm):
    m, k = a.shape
    s, _, ns = w_in3.shape
    nr = len(stk_rest)
    mt = m // tm

    def body(a_ref, b_ref, *rest):
        o_ref = rest[nr]
        stk = rest[nr + 1:2 * nr + 1]
        ssem, rsem = rest[2 * nr + 1:]
        step = pl.program_id(0) * mt + pl.program_id(1)
        sends, arrive, _, _ = _gather_copies(stk, names_rest, ssem, rsem)

        @pl.when(step == 0)
        def _():
            for cp in sends:
                cp.start()

        o_ref[...] = _dot(a_ref[...], b_ref[...])

        @pl.when(step == s * mt - 1)
        def _():
            for cp in arrive:
                cp.wait_recv()
            for cp in sends:
                cp.wait_send()

    any_spec = pl.BlockSpec(memory_space=pl.ANY)
    outs = pl.pallas_call(
        body, name="mm_in",
        out_shape=(jax.ShapeDtypeStruct((s, m, ns), F32),) + tuple(jax.ShapeDtypeStruct(x.shape, x.dtype) for x in stk_rest),
        grid=(s, mt),
        in_specs=[pl.BlockSpec((tm, k), lambda j, i: (i, 0)), pl.BlockSpec((None, k, ns), lambda j, i: (j, 0, 0))]
        + [any_spec] * nr,
        out_specs=(pl.BlockSpec((None, tm, ns), lambda j, i: (_slot_of_chip(j), i, 0)),) + (any_spec,) * nr,
        scratch_shapes=[pltpu.SemaphoreType.DMA((3 * nr,)), pltpu.SemaphoreType.DMA((3 * nr,))],
        input_output_aliases={2 + w: 1 + w for w in range(nr)},
        compiler_params=pltpu.CompilerParams(dimension_semantics=("arbitrary", "arbitrary"),
                                             vmem_limit_bytes=VMEM_BIG, has_side_effects=True),
    )(a, w_in3, *stk_rest)
    return outs[0], list(outs[1:])


def _mm_dh1_exchange(dp4, w_in3, p_w_in, tm):
    s, k, ns = w_in3.shape
    m = dp4.shape[1]
    mt = m // tm

    def body(a_ref, b_ref, p_ref, o_ref, q_ref, ssem, rsem):
        i, j = pl.program_id(0), pl.program_id(1)
        sends, arrive = _exchange_copies([q_ref], [lambda chip: p_ref.at[chip]], ssem, rsem)

        @pl.when((i == 0) & (j == 0))
        def _():
            for cp in sends:
                cp.start()

        @pl.when(j == 0)
        def _():
            o_ref[...] = _dot_nt(a_ref[...], b_ref[...])

        @pl.when(j > 0)
        def _():
            o_ref[...] += _dot_nt(a_ref[...], b_ref[...])

        @pl.when((i == mt - 1) & (j == s - 1))
        def _():
            for cp in arrive:
                cp.wait_recv()
            for cp in sends:
                cp.wait_send()

    any_spec = pl.BlockSpec(memory_space=pl.ANY)
    return pl.pallas_call(
        body, name="mm_dh1",
        out_shape=(jax.ShapeDtypeStruct((m, k), F32), jax.ShapeDtypeStruct(p_w_in.shape, BF16)),
        grid=(mt, s),
        in_specs=[pl.BlockSpec((None, tm, ns), lambda i, j: (_slot_of_chip(j), i, 0)),
                  pl.BlockSpec((None, k, ns), lambda i, j: (j, 0, 0)), any_spec],
        out_specs=(pl.BlockSpec((tm, k), lambda i, j: (i, 0)), any_spec),
        scratch_shapes=[pltpu.SemaphoreType.DMA((3,)), pltpu.SemaphoreType.DMA((3,))],
        compiler_params=pltpu.CompilerParams(dimension_semantics=("arbitrary", "arbitrary"),
                                             vmem_limit_bytes=VMEM_BIG, has_side_effects=True),
    )(dp4, w_in3, p_w_in)


def _mm_nt_sharded(a, b3, tm, name, stacked_slot_fn=None):
    s, k, ns = b3.shape
    m = a.shape[1] if stacked_slot_fn is not None else a.shape[0]

    def body(a_ref, b_ref, o_ref):
        j = pl.program_id(1)
        @pl.when(j == 0)
        def _():
            o_ref[...] = _dot_nt(a_ref[...], b_ref[...])

        @pl.when(j > 0)
        def _():
            o_ref[...] += _dot_nt(a_ref[...], b_ref[...])

    if stacked_slot_fn is None:
        a_spec = pl.BlockSpec((tm, ns), lambda i, j: (i, j))
    else:
        a_spec = pl.BlockSpec((None, tm, ns), lambda i, j: (stacked_slot_fn(j), i, 0))
    return pl.pallas_call(
        body, name=name, out_shape=jax.ShapeDtypeStruct((m, k), F32), grid=(m // tm, s),
        in_specs=[a_spec, pl.BlockSpec((None, k, ns), lambda i, j: (j, 0, 0))],
        out_specs=pl.BlockSpec((tm, k), lambda i, j: (i, 0)),
        compiler_params=_params(("parallel", "arbitrary"), VMEM_BIG),
    )(a, b3)


def _mm_nt(a, b, out_dtype, tm, name):
    m, k = a.shape
    n = b.shape[0]

    def body(a_ref, b_ref, o_ref):
        o_ref[...] = _dot_nt(a_ref[...], b_ref[...]).astype(out_dtype)

    return pl.pallas_call(
        body, name=name, out_shape=jax.ShapeDtypeStruct((m, n), out_dtype), grid=(m // tm,),
        in_specs=[pl.BlockSpec((tm, k), lambda i: (i, 0)), pl.BlockSpec((n, k), lambda i: (0, 0))],
        out_specs=pl.BlockSpec((tm, n), lambda i: (i, 0)),
        compiler_params=_params(("parallel",), VMEM_BIG),
    )(a, b)


def _mm_tn(a, g, tkk, tn, tk, name, stacked_slot_fn=None, stacked_out=False):
    m, k = a.shape
    if stacked_slot_fn is not None:
        n = g.shape[0] * g.shape[2]
        g_spec = pl.BlockSpec((None, tk, tn), lambda kk, j, mm: (stacked_slot_fn(j), mm, 0))
    else:
        n = g.shape[1]
        g_spec = pl.BlockSpec((tk, tn), lambda kk, j, mm: (mm, j))

    def body(a_ref, g_ref, o_ref):
        mm = pl.program_id(2)

        @pl.when(mm == 0)
        def _():
            o_ref[...] = _dot_tn(a_ref[...], g_ref[...])

        @pl.when(mm > 0)
        def _():
            o_ref[...] += _dot_tn(a_ref[...], g_ref[...])

    if stacked_out:
        out_shape = jax.ShapeDtypeStruct((n // tn, k, tn), F32)
        out_spec = pl.BlockSpec((None, tkk, tn), lambda kk, j, mm: (j, kk, 0))
    else:
        out_shape = jax.ShapeDtypeStruct((k, n), F32)
        out_spec = pl.BlockSpec((tkk, tn), lambda kk, j, mm: (kk, j))
    return pl.pallas_call(
        body, name=name, out_shape=out_shape, grid=(k // tkk, n // tn, m // tk),
        in_specs=[pl.BlockSpec((tk, tkk), lambda kk, j, mm: (mm, kk)), g_spec],
        out_specs=out_spec,
        compiler_params=_params(("parallel", "parallel", "arbitrary"), VMEM_BIG),
    )(a, g)


def _norm_fwd(x, gain, tt):
    t = x.shape[0]

    def body(x_ref, g_ref, h_ref):
        _, n = _rms_stats(x_ref[...])
        h_ref[...] = (n * g_ref[...]).astype(BF16)

    return pl.pallas_call(
        body, name="norm_fwd", out_shape=jax.ShapeDtypeStruct((t, D), BF16), grid=(t // tt,),
        in_specs=[pl.BlockSpec((tt, D), lambda i: (i, 0)), pl.BlockSpec((1, D), lambda i: (0, 0))],
        out_specs=pl.BlockSpec((tt, D), lambda i: (i, 0)),
        compiler_params=_params(("parallel",)),
    )(x, gain)


def _lru_gates(xc, wa_ref, wx_ref, ba, bx, lam):
    xcb = xc.astype(BF16)
    ra = jnp.concatenate([_dot(xcb[:, n * HD:(n + 1) * HD], wa_ref[n]) for n in range(NH)], axis=1) + ba
    ix = jnp.concatenate([_dot(xcb[:, n * HD:(n + 1) * HD], wx_ref[n]) for n in range(NH)], axis=1) + bx
    r = _sig(ra)
    ig = _sig(ix)
    z = -lam
    sp = jnp.maximum(z, 0.0) + jnp.log1p(jnp.exp(-jnp.abs(z)))
    log_a = -LRU_C * r * sp
    a = jnp.exp(log_a)
    z2 = 2.0 * log_a
    series = -z2 * (1.0 + z2 * (0.5 + z2 * (1.0 / 6.0 + z2 * (1.0 / 24.0))))
    om = jnp.where(z2 > -0.02, series, 1.0 - jnp.exp(z2))
    mult = jnp.sqrt(om)
    return xcb, r, ig, sp, a, mult


def _mixer_a_fwd(p4, cw, cb, wa, wx, ba, bx, lam, tt):
    t = p4.shape[1]

    def body(p_ref, cw_ref, cb_ref, wa_ref, wx_ref, ba_ref, bx_ref, lam_ref, ya_ref, h_ref, halo, hc):
        i = pl.program_id(0)

        @pl.when(i == 0)
        def _():
            halo[...] = jnp.zeros((8, D), F32)
            hc[...] = jnp.zeros((1, D), F32)

        xa = p_ref[:, 0:D]
        ga = p_ref[:, D:2 * D]
        xe = jnp.concatenate([halo[...], xa], axis=0)
        xc = (cb_ref[...] + cw_ref[3:4, :] * xe
              + sum(cw_ref[3 - s:4 - s, :] * pltpu.roll(xe, s, 0) for s in (1, 2, 3)))[8:, :]
        halo[...] = xa[tt - 8:, :]
        _, _, ig, _, a, mult = _lru_gates(xc, wa_ref, wx_ref, ba_ref[...], bx_ref[...], lam_ref[...])
        u = mult * ig * xc
        h = _scan_down(a, u, hc[...])
        hc[...] = h[tt - 1:tt, :]
        h_ref[...] = h
        ya_ref[...] = (h * _gelu(ga)).astype(BF16)

    full = lambda shape: pl.BlockSpec(shape, lambda i: (0,) * len(shape))
    return pl.pallas_call(
        body, name="mixer_a_fwd",
        out_shape=(jax.ShapeDtypeStruct((t, D), BF16), jax.ShapeDtypeStruct((t, D), F32)),
        grid=(t // tt,),
        in_specs=[pl.BlockSpec((None, tt, 2 * D), lambda i: (SLOT_A, i, 0)),
                  full((4, D)), full((1, D)), full((NH, HD, HD)), full((NH, HD, HD)),
                  full((1, D)), full((1, D)), full((1, D))],
        out_specs=(pl.BlockSpec((tt, D), lambda i: (i, 0)), pl.BlockSpec((tt, D), lambda i: (i, 0))),
        scratch_shapes=[pltpu.VMEM((8, D), F32), pltpu.VMEM((1, D), F32)],
        compiler_params=_params(("arbitrary",), VMEM_BIG),
    )(p4, cw, cb, wa, wx, ba, bx, lam)


def _chunk_masks(tt):
    row = lax.broadcasted_iota(jnp.int32, (tt, tt), 0)
    col = lax.broadcasted_iota(jnp.int32, (tt, tt), 1)
    same = jnp.right_shift(row, 5) == jnp.right_shift(col, 5)
    return same & (col <= row)


def _hg_head_fwd(q, fz, lbh):
    sg = _sig(fz)
    sgn = _sig(-fz)
    f = lbh + (1.0 - lbh) * sg
    logf = jnp.log(f)
    k = (1.0 - lbh) * sgn
    g = _chunk_cumsum(logf)
    gu = _chunk_last(g) - g
    eg = jnp.exp(g)
    eng = jnp.exp(-g)
    egu = jnp.exp(gu)
    qt = q * eg
    kt = k * eng
    kd = k * egu
    return sg, sgn, f, k, g, eg, eng, egu, qt, kt, kd


def _lb_of(logits_ref):
    return _sig(logits_ref[0:1, :] - logits_ref[1:2, :])


def _hgrn2_fwd(p4, logits, gnorm, tt):
    t = p4.shape[1]
    nc = tt // CH

    def body(p_ref, lg_ref, gn_ref, yb_ref, o_ref, ss_ref, st):
        i = pl.program_id(0)

        @pl.when(i == 0)
        def _():
            st[...] = jnp.zeros((NH, HD, HD), F32)

        low = _chunk_masks(tt)
        lb = _lb_of(lg_ref)
        heads = [slice(h * HD, (h + 1) * HD) for h in range(NH)]
        _, _, _, _, g, _, _, _, qt, kt, kd = _hg_head_fwd(p_ref[0, :, 0:D], p_ref[0, :, D:2 * D], lb)
        qtb, ktb, kdb, vb = qt.astype(BF16), kt.astype(BF16), kd.astype(BF16), p_ref[1, :, 0:D].astype(BF16)
        decs = [jnp.exp(g[c * CH + CH - 1:c * CH + CH, :]) for c in range(nc)]
        o_in = []
        for hs in heads:
            att = jnp.where(low, _dot_nt(qtb[:, hs], ktb[:, hs]), 0.0)
            o_in.append(_dot(att.astype(BF16), vb[:, hs]))
        s_t = [st[h] for h in range(NH)]
        pieces = [[None] * nc for _ in range(NH)]
        for c in range(nc):
            sl = slice(c * CH, (c + 1) * CH)
            for h, hs in enumerate(heads):
                s_bf = s_t[h].astype(BF16)
                ss_ref[c, h] = s_bf
                pieces[h][c] = o_in[h][sl] + _dot_nt(qtb[sl, hs], s_bf)
                s_t[h] = s_t[h] * decs[c][:, hs] + _dot_tn(vb[sl, hs], kdb[sl, hs])
        for h, hs in enumerate(heads):
            st[h] = s_t[h]
            o = jnp.concatenate(pieces[h], axis=0)
            _, n = _rms_stats(o)
            og = p_ref[1, :, D + h * HD:D + (h + 1) * HD]
            o_ref[:, hs] = o
            yb_ref[:, hs] = (n * gn_ref[:, hs] * (og * _sig(og))).astype(BF16)

    return pl.pallas_call(
        body, name="hgrn2_fwd",
        out_shape=(jax.ShapeDtypeStruct((t, D), BF16), jax.ShapeDtypeStruct((t, D), F32),
                   jax.ShapeDtypeStruct((t // CH, NH, HD, HD), BF16)),
        grid=(t // tt,),
        in_specs=[pl.BlockSpec((2, tt, 2 * D), lambda i: (0, i, 0)),
                  pl.BlockSpec((2, D), lambda i: (0, 0)), pl.BlockSpec((1, D), lambda i: (0, 0))],
        out_specs=(pl.BlockSpec((tt, D), lambda i: (i, 0)), pl.BlockSpec((tt, D), lambda i: (i, 0)),
                   pl.BlockSpec((nc, NH, HD, HD), lambda i: (i, 0, 0, 0))),
        scratch_shapes=[pltpu.VMEM((NH, HD, HD), F32)],
        compiler_params=_params(("arbitrary",), VMEM_BIG),
    )(p4, logits, gnorm)


def _mid_fwd(ya, yb, p4, x, wa, wb, wo, g_pm, g_pf, tt):
    t = x.shape[0]

    def body(ya_ref, yb_ref, gt_ref, x_ref, wa_ref, wb_ref, wo_ref, gpm_ref, gpf_ref,
             za_ref, zb_ref, mix_ref, m2_ref, x1_ref, h2_ref):
        za = _dot(ya_ref[...], wa_ref[...])
        zb = _dot(yb_ref[...], wb_ref[...])
        mix = _sig(gt_ref[:, 0:D]) * za + _sig(gt_ref[:, D:2 * D]) * zb
        mixb = mix.astype(BF16)
        m2 = _dot(mixb, wo_ref[...])
        _, n2 = _rms_stats(m2)
        x1 = x_ref[...] + n2 * gpm_ref[...]
        _, n1 = _rms_stats(x1)
        za_ref[...] = za.astype(BF16)
        zb_ref[...] = zb.astype(BF16)
        mix_ref[...] = mixb
        m2_ref[...] = m2
        x1_ref[...] = x1
        h2_ref[...] = (n1 * gpf_ref[...]).astype(BF16)

    row = lambda dt: jax.ShapeDtypeStruct((t, D), dt)
    tile = pl.BlockSpec((tt, D), lambda i: (i, 0))
    wsp = pl.BlockSpec((D, D), lambda i: (0, 0))
    vec = pl.BlockSpec((1, D), lambda i: (0, 0))
    return pl.pallas_call(
        body, name="mid_fwd",
        out_shape=(row(BF16), row(BF16), row(BF16), row(F32), row(F32), row(BF16)),
        grid=(t // tt,),
        in_specs=[tile, tile, pl.BlockSpec((None, tt, 2 * D), lambda i: (SLOT_G, i, 0)), tile,
                  wsp, wsp, wsp, vec, vec],
        out_specs=(tile,) * 6,
        compiler_params=_params(("parallel",), VMEM_BIG),
    )(ya, yb, p4, x, wa, wb, wo, g_pm, g_pf)


def _up_act_fwd(h2, w_up4, cfw, cfb, tm):
    t = h2.shape[0]
    ns = SH_UP

    def body(a_ref, ah_ref, wg_ref, wv_ref, cwg_ref, cwv_ref, cbg_ref, cbv_ref,
             pg_ref, pv_ref, y_ref, uv_ref, gl_ref, dgl_ref):
        i = pl.program_id(1)
        rows = jnp.concatenate([ah_ref[...], a_ref[...]], axis=0)
        ups = []
        for w_ref, cw_ref, cb_ref, pre_ref in ((wg_ref, cwg_ref, cbg_ref, pg_ref), (wv_ref, cwv_ref, cbv_ref, pv_ref)):
            pre = _dot(rows, w_ref[...])
            pre_ref[...] = pre[16:, :]
            xe = jnp.concatenate([jnp.where(i > 0, pre[8:16, :], 0.0), pre[16:, :]], axis=0)
            up = (cb_ref[...] + cw_ref[2:3, :] * xe + cw_ref[1:2, :] * pltpu.roll(xe, 1, 0)
                  + cw_ref[0:1, :] * pltpu.roll(xe, 2, 0))
            ups.append(up[8:, :])
        gl, dgl = _gelu_and_grad(ups[0])
        y_ref[...] = (gl * ups[1]).astype(BF16)
        uv_ref[...] = ups[1].astype(BF16)
        gl_ref[...] = gl.astype(BF16)
        dgl_ref[...] = dgl.astype(BF16)

    hb = tm // 16
    tile = pl.BlockSpec((tm, ns), lambda p, i: (i, p))
    return pl.pallas_call(
        body, name="up_act_fwd",
        out_shape=(jax.ShapeDtypeStruct((t, DFF), F32),) * 2 + (jax.ShapeDtypeStruct((t, DFF), BF16),) * 4,
        grid=(2, t // tm),
        in_specs=[pl.BlockSpec((tm, D), lambda p, i: (i, 0)),
                  pl.BlockSpec((16, D), lambda p, i: (jnp.maximum(i * hb - 1, 0), 0)),
                  pl.BlockSpec((None, D, ns), lambda p, i: (p, 0, 0)),
                  pl.BlockSpec((None, D, ns), lambda p, i: (p + 2, 0, 0)),
                  pl.BlockSpec((3, ns), lambda p, i: (0, p)), pl.BlockSpec((3, ns), lambda p, i: (0, p + 2)),
                  pl.BlockSpec((1, ns), lambda p, i: (0, p)), pl.BlockSpec((1, ns), lambda p, i: (0, p + 2))],
        out_specs=(tile,) * 6,
        compiler_params=_params(("parallel", "parallel"), VMEM_BIG),
    )(h2, h2, w_up4, w_up4, cfw, cfw, cfb, cfb)


def _down_loss(y, wdn, x1, tgt, g_post, tt):
    t = x1.shape[0]

    def body(y_ref, w_ref, x1_ref, t_ref, g_ref, dx2_ref, dm3_ref, lossv_ref, dg_ref):
        i = pl.program_id(0)
        m3 = _dot(y_ref[...], w_ref[...])
        r, n3 = _rms_stats(m3)
        g = g_ref[...]
        e = x1_ref[...] + n3 * g - t_ref[...]
        dx2 = e * (1.0 / D)
        dx2_ref[...] = dx2
        dm3_ref[...] = _rms_bwd(dx2 * g, n3, r).astype(BF16)
        lv = jnp.sum(e * e, axis=0, keepdims=True)
        dgv = jnp.sum(dx2 * n3, axis=0, keepdims=True)

        @pl.when(i == 0)
        def _():
            lossv_ref[...] = lv
            dg_ref[...] = dgv

        @pl.when(i > 0)
        def _():
            lossv_ref[...] += lv
            dg_ref[...] += dgv

    tile = pl.BlockSpec((tt, D), lambda i: (i, 0))
    vec = pl.BlockSpec((1, D), lambda i: (0, 0))
    return pl.pallas_call(
        body, name="down_loss",
        out_shape=(jax.ShapeDtypeStruct((t, D), F32), jax.ShapeDtypeStruct((t, D), BF16),
                   jax.ShapeDtypeStruct((1, D), F32), jax.ShapeDtypeStruct((1, D), F32)),
        grid=(t // tt,),
        in_specs=[pl.BlockSpec((tt, DFF), lambda i: (i, 0)), pl.BlockSpec((DFF, D), lambda i: (0, 0)),
                  tile, tile, vec],
        out_specs=(tile, tile, vec, vec),
        compiler_params=_params(("arbitrary",), VMEM_BIG),
    )(y, wdn, x1, tgt, g_post)


def _ffn_act_bwd(dy, pre_g, pre_v, uv, gl, dgl, cfw, tt):
    t = dy.shape[0]
    nt = t // tt

    def body(dy_ref, dyn_ref, pg_ref, pv_ref, uv_ref, uvn_ref, gl_ref, gln_ref, dgl_ref, dgln_ref, cw_ref,
             du_ref, dcw_ref, dcb_ref):
        i = pl.program_id(0)
        n = tt + 8
        next_live = jnp.where(i < nt - 1, 1.0, 0.0)
        ext = lambda ref, nref: jnp.concatenate([ref[...].astype(F32), nref[...].astype(F32)[0:8, :]], axis=0)
        dy = jnp.concatenate([dy_ref[...].astype(F32), dyn_ref[...].astype(F32)[0:8, :] * next_live], axis=0)
        ds = (dy * ext(uv_ref, uvn_ref) * ext(dgl_ref, dgln_ref), dy * ext(gl_ref, gln_ref))
        dcw_parts, dcb_parts = [], []
        for hh, c0 in enumerate((0, DFF)):
            cs = slice(c0, c0 + DFF)
            dd = ds[hh]
            d1 = pltpu.roll(dd, n - 1, 0)
            d2 = pltpu.roll(dd, n - 2, 0)
            du_ref[:, cs] = (cw_ref[2:3, cs] * dd + cw_ref[1:2, cs] * d1 + cw_ref[0:1, cs] * d2)[0:tt, :].astype(BF16)
            x = (pg_ref, pv_ref)[hh][...]
            dcw_parts.append(jnp.concatenate(
                [jnp.sum(dk[0:tt, :] * x, axis=0, keepdims=True) for dk in (d2, d1, dd)], axis=0))
            dcb_parts.append(jnp.sum(dd[0:tt, :], axis=0, keepdims=True))
        dcw = jnp.concatenate(dcw_parts, axis=1)
        dcb = jnp.concatenate(dcb_parts, axis=1)

        @pl.when(i == 0)
        def _():
            dcw_ref[...] = dcw
            dcb_ref[...] = dcb

        @pl.when(i > 0)
        def _():
            dcw_ref[...] += dcw
            dcb_ref[...] += dcb

    half = pl.BlockSpec((tt, DFF), lambda i: (i, 0))
    half_next = pl.BlockSpec((16, DFF), lambda i: (jnp.minimum((i + 1) * (tt // 16), t // 16 - 1), 0))
    return pl.pallas_call(
        body, name="ffn_act_bwd",
        out_shape=(jax.ShapeDtypeStruct((t, DUP), BF16), jax.ShapeDtypeStruct((3, DUP), F32),
                   jax.ShapeDtypeStruct((1, DUP), F32)),
        grid=(nt,),
        in_specs=[half, half_next, half, half,
                  half, half_next, half, half_next, half, half_next,
                  pl.BlockSpec((3, DUP), lambda i: (0, 0))],
        out_specs=(pl.BlockSpec((tt, DUP), lambda i: (i, 0)), pl.BlockSpec((3, DUP), lambda i: (0, 0)),
                   pl.BlockSpec((1, DUP), lambda i: (0, 0))),
        compiler_params=_params(("arbitrary",), VMEM_BIG),
    )(dy, dy, pre_g, pre_v, uv, uv, gl, gl, dgl, dgl, cfw)


def _mid_bwd(dh2, dx2, x1, m2, za, zb, p4, wa, wb, wo, g_pm, g_pf, tt):
    t = x1.shape[0]

    def body(dh2_ref, dx2_ref, x1_ref, m2_ref, za_ref, zb_ref, gt_ref, wa_ref, wb_ref, wo_ref, gpm_ref, gpf_ref,
             dx1_ref, dm2_ref, dza_ref, dzb_ref, dya_ref, dyb_ref, dp_ref, dgpm_ref, dgpf_ref):
        i = pl.program_id(0)
        r1, n1 = _rms_stats(x1_ref[...])
        dh2 = dh2_ref[...]
        dx1 = dx2_ref[...] + _rms_bwd(dh2 * gpf_ref[...], n1, r1)
        r2, n2 = _rms_stats(m2_ref[...])
        dm2 = _rms_bwd(dx1 * gpm_ref[...], n2, r2).astype(BF16)
        dmix = _dot_nt(dm2, wo_ref[...])
        sa = _sig(gt_ref[:, 0:D])
        sb = _sig(gt_ref[:, D:2 * D])
        dza = (dmix * sa).astype(BF16)
        dzb = (dmix * sb).astype(BF16)
        dp_ref[:, 0:D] = (dmix * za_ref[...].astype(F32) * sa * (1.0 - sa)).astype(BF16)
        dp_ref[:, D:2 * D] = (dmix * zb_ref[...].astype(F32) * sb * (1.0 - sb)).astype(BF16)
        dx1_ref[...] = dx1
        dm2_ref[...] = dm2
        dza_ref[...] = dza
        dzb_ref[...] = dzb
        dya_ref[...] = _dot_nt(dza, wa_ref[...])
        dyb_ref[...] = _dot_nt(dzb, wb_ref[...])
        dgpf = jnp.sum(dh2 * n1, axis=0, keepdims=True)
        dgpm = jnp.sum(dx1 * n2, axis=0, keepdims=True)

        @pl.when(i == 0)
        def _():
            dgpf_ref[...] = dgpf
            dgpm_ref[...] = dgpm

        @pl.when(i > 0)
        def _():
            dgpf_ref[...] += dgpf
            dgpm_ref[...] += dgpm

    row = lambda dt: jax.ShapeDtypeStruct((t, D), dt)
    tile = pl.BlockSpec((tt, D), lambda i: (i, 0))
    wsp = pl.BlockSpec((D, D), lambda i: (0, 0))
    vec = pl.BlockSpec((1, D), lambda i: (0, 0))
    gates = pl.BlockSpec((None, tt, 2 * D), lambda i: (SLOT_G, i, 0))
    return pl.pallas_call(
        body, name="mid_bwd",
        out_shape=(row(F32), row(BF16), row(BF16), row(BF16), row(F32), row(F32),
                   jax.ShapeDtypeStruct((NCHIP, t, 2 * D), BF16),
                   jax.ShapeDtypeStruct((1, D), F32), jax.ShapeDtypeStruct((1, D), F32)),
        grid=(t // tt,),
        in_specs=[tile, tile, tile, tile, tile, tile, gates, wsp, wsp, wsp, vec, vec],
        out_specs=(tile, tile, tile, tile, tile, tile, gates, vec, vec),
        compiler_params=_params(("arbitrary",), VMEM_BIG),
    )(dh2, dx2, x1, m2, za, zb, p4, wa, wb, wo, g_pm, g_pf)


def _hgrn2_bwd(p4, o_all, ss, dyb, dp4, logits, gnorm, p_early, tt):
    t = p4.shape[1]
    nt = t // tt
    nc = tt // CH
    ne = len(p_early)

    def body(p_ref, o_ref, ss_ref, dyb_ref, dp_in, lg_ref, gn_ref, *rest):
        del dp_in
        pe = rest[:ne]
        dp_ref, dlb_ref, dgn_ref = rest[ne:ne + 3]
        qe = rest[ne + 3:2 * ne + 3]
        dst, ssem, rsem = rest[2 * ne + 3:]
        i = pl.program_id(0)
        sends, arrive = _exchange_copies(qe, [(lambda chip, r=r: r.at[chip]) for r in pe], ssem, rsem)

        @pl.when(i == 0)
        def _():
            dst[...] = jnp.zeros((NH, HD, HD), F32)
            for cp in sends:
                cp.start()

        low = _chunk_masks(tt)
        lb = _lb_of(lg_ref)
        heads = [slice(h * HD, (h + 1) * HD) for h in range(NH)]
        sg, sgn, f, k, g, eg, eng, egu, qt, kt, kd = _hg_head_fwd(p_ref[0, :, 0:D], p_ref[0, :, D:2 * D], lb)
        qtb, ktb, kdb, vb = qt.astype(BF16), kt.astype(BF16), kd.astype(BF16), p_ref[1, :, 0:D].astype(BF16)
        decs = [jnp.exp(g[c * CH + CH - 1:c * CH + CH, :]) for c in range(nc)]
        og = p_ref[1, :, D:2 * D]
        so = _sig(og)
        dyb = dyb_ref[...]
        dob = dyb * (og * so)
        rn = [_rms_stats(o_ref[:, hs]) for hs in heads]
        r_all = jnp.concatenate([jnp.broadcast_to(r, (tt, HD)) for r, _ in rn], axis=1)
        n_all = jnp.concatenate([n for _, n in rn], axis=1)
        gd = dob * gn_ref[...]
        proj = jnp.concatenate(
            [jnp.broadcast_to(jnp.mean(gd[:, hs] * n_all[:, hs], axis=-1, keepdims=True), (tt, HD)) for hs in heads],
            axis=1)
        dob_ = (r_all * (gd - n_all * proj)).astype(BF16)
        dog = dyb * (n_all * gn_ref[...]) * (so * (1.0 + og * (1.0 - so)))
        dgn = jnp.sum(dob * n_all, axis=0, keepdims=True)
        dv_in, dqt_in, dkt_h = [], [], []
        for hs in heads:
            att = jnp.where(low, _dot_nt(qtb[:, hs], ktb[:, hs]), 0.0).astype(BF16)
            d_att = jnp.where(low, _dot_nt(dob_[:, hs], vb[:, hs]), 0.0).astype(BF16)
            dv_in.append(_dot_tn(att, dob_[:, hs]))
            dqt_in.append(_dot(d_att, ktb[:, hs]))
            dkt_h.append(_dot_tn(d_att, qtb[:, hs]))
        ds_t = [dst[h] for h in range(NH)]
        dv_p = [[None] * NH for _ in range(nc)]
        dqt_p = [[None] * NH for _ in range(nc)]
        dkd_p = [[None] * NH for _ in range(nc)]
        dgl_p = [[None] * NH for _ in range(nc)]
        for c in reversed(range(nc)):
            sl = slice(c * CH, (c + 1) * CH)
            for h, hs in enumerate(heads):
                s_prev = ss_ref[c, h]
                ds_bf = ds_t[h].astype(BF16)
                dec = decs[c][:, hs]
                dv_p[c][h] = dv_in[h][sl] + _dot_nt(kdb[sl, hs], ds_bf)
                dqt_p[c][h] = dqt_in[h][sl] + _dot(dob_[sl, hs], s_prev)
                dkd_p[c][h] = _dot(vb[sl, hs], ds_bf)
                ddec = jnp.sum(s_prev.astype(F32) * ds_t[h], axis=0, keepdims=True)
                dgl_p[c][h] = jnp.broadcast_to(ddec * dec, (CH, HD))
                ds_t[h] = ds_t[h] * dec + _dot_tn(dob_[sl, hs], qtb[sl, hs])
        for h in range(NH):
            dst[h] = ds_t[h]
        whole = lambda parts: jnp.concatenate([jnp.concatenate(row, axis=1) for row in parts], axis=0)
        dv, dqt, dkd, dgl = whole(dv_p), whole(dqt_p), whole(dkd_p), whole(dgl_p)
        dkt = jnp.concatenate(dkt_h, axis=1)
        dq = dqt * eg
        dk = dkt * eng + dkd * egu
        dg = dqt * qt - dkt * kt
        dgu = dkd * kd
        dlogf = _chunk_revcumsum(dg - dgu) + _chunk_total(dgu) + dgl
        common = sgn * (dlogf / f - dk)
        dfz = (1.0 - lb) * sg * common
        dlb = jnp.sum(common, axis=0, keepdims=True)
        dp_ref[0, :, 0:D] = dq.astype(BF16)
        dp_ref[0, :, D:2 * D] = dfz.astype(BF16)
        dp_ref[1, :, 0:D] = dv.astype(BF16)
        dp_ref[1, :, D:2 * D] = dog.astype(BF16)

        @pl.when(i == 0)
        def _():
            dlb_ref[0:1, :] = dlb
            dgn_ref[...] = dgn

        @pl.when(i > 0)
        def _():
            dlb_ref[0:1, :] += dlb
            dgn_ref[...] += dgn

        @pl.when(i == nt - 1)
        def _():
            d0 = dlb_ref[0:1, :] * lb * (1.0 - lb)
            dlb_ref[0:1, :] = d0
            dlb_ref[1:2, :] = -d0
            for cp in arrive:
                cp.wait_recv()
            for cp in sends:
                cp.wait_send()

    rev = lambda i: nt - 1 - i
    vec = pl.BlockSpec((1, D), lambda i: (0, 0))
    any_spec = pl.BlockSpec(memory_space=pl.ANY)
    outs = pl.pallas_call(
        body, name="hgrn2_bwd",
        out_shape=(jax.ShapeDtypeStruct(dp4.shape, BF16), jax.ShapeDtypeStruct((2, D), F32),
                   jax.ShapeDtypeStruct((1, D), F32)) + tuple(jax.ShapeDtypeStruct(a.shape, BF16) for a in p_early),
        grid=(nt,),
        in_specs=[pl.BlockSpec((2, tt, 2 * D), lambda i: (0, rev(i), 0)),
                  pl.BlockSpec((tt, D), lambda i: (rev(i), 0)),
                  pl.BlockSpec((nc, NH, HD, HD), lambda i: (rev(i), 0, 0, 0)),
                  pl.BlockSpec((tt, D), lambda i: (rev(i), 0)),
                  any_spec,
                  pl.BlockSpec((2, D), lambda i: (0, 0)), vec] + [any_spec] * ne,
        out_specs=(pl.BlockSpec((2, tt, 2 * D), lambda i: (0, rev(i), 0)),
                   pl.BlockSpec((2, D), lambda i: (0, 0)), vec) + (any_spec,) * ne,
        scratch_shapes=[pltpu.VMEM((NH, HD, HD), F32), pltpu.SemaphoreType.DMA((3 * ne,)),
                        pltpu.SemaphoreType.DMA((3 * ne,))],
        input_output_aliases={4: 0},
        compiler_params=pltpu.CompilerParams(dimension_semantics=("arbitrary",), vmem_limit_bytes=VMEM_BIG,
                                             has_side_effects=True),
    )(p4, o_all, ss, dyb, dp4, logits, gnorm, *p_early)
    return outs[0], outs[1], outs[2], list(outs[3:])


def _mixer_a_bwd(p4, hseq, dya, dp4, cw, cb, wa, wx, ba, bx, lam, tt):
    t = p4.shape[1]
    nt = t // tt
    steps = tt.bit_length() - 1

    def body(p_ref, ph_ref, h_ref, hh_ref, dya_ref, dp_in, cw_ref, cb_ref, wa_ref, wx_ref, ba_ref, bx_ref, lam_ref,
             dp_ref, dcw_ref, dcb_ref, dwa_ref, dwx_ref, dba_ref, dbx_ref, dlam_ref,
             dnext, dhc, afc):
        del dp_in
        i = pl.program_id(0)
        first_tile = i == nt - 1

        @pl.when(i == 0)
        def _():
            dnext[...] = jnp.zeros((8, D), F32)
            dhc[...] = jnp.zeros((1, D), F32)
            afc[...] = jnp.zeros((1, D), F32)

        xa = p_ref[:, 0:D]
        ga = p_ref[:, D:2 * D]
        xe = jnp.concatenate([jnp.where(first_tile, 0.0, ph_ref[:, 0:D]), xa], axis=0)
        xs = [xe[8:, :]] + [pltpu.roll(xe, s, 0)[8:, :] for s in (1, 2, 3)]
        xc = cb_ref[...] + sum(cw_ref[3 - s:4 - s, :] * xs[s] for s in range(4))
        lam = lam_ref[...]
        xcb, r, ig, sp, a, mult = _lru_gates(xc, wa_ref, wx_ref, ba_ref[...], bx_ref[...], lam)
        h = h_ref[...]
        gl, dgl = _gelu_and_grad(ga)
        dya = dya_ref[...]
        dga = dya * h * dgl
        rows = lax.broadcasted_iota(jnp.int32, (tt, 1), 0)
        a_next = jnp.where(rows == tt - 1, afc[...], pltpu.roll(a, tt - 1, 0))
        dh = _scan_up(a_next, dya * gl, dhc[...])
        dhc[...] = dh[0:1, :]
        afc[...] = a[0:1, :]
        h_prev = jnp.where(rows == 0, jnp.where(first_tile, 0.0, hh_ref[7:8, :]), pltpu.roll(h, 1, 0))
        da = dh * h_prev
        dmult = dh * ig * xc
        di = dh * mult * xc
        dlog_a = da * a - dmult * a * a / mult
        dr = dlog_a * (-LRU_C * sp)
        dsp = jnp.sum(dlog_a * (-LRU_C * r), axis=0, keepdims=True)
        dra = dr * r * (1.0 - r)
        dix = di * ig * (1.0 - ig)
        drab = dra.astype(BF16)
        dixb = dix.astype(BF16)
        dxc_lin = []
        dwa_new = []
        dwx_new = []
        for n in range(NH):
            cs = slice(n * HD, (n + 1) * HD)
            dxc_lin.append(_dot_nt(drab[:, cs], wa_ref[n]) + _dot_nt(dixb[:, cs], wx_ref[n]))
            dwa_new.append(_dot_tn(xcb[:, cs], drab[:, cs]))
            dwx_new.append(_dot_tn(xcb[:, cs], dixb[:, cs]))
        dxc = dh * mult * ig + jnp.concatenate(dxc_lin, axis=1)
        de = jnp.concatenate([dxc, dnext[...]], axis=0)
        dxa = (cw_ref[3:4, :] * de
               + sum(cw_ref[3 - s:4 - s, :] * pltpu.roll(de, tt + 8 - s, 0) for s in (1, 2, 3)))[0:tt, :]
        dnext[...] = dxc[0:8, :]
        dp_ref[:, 0:D] = dxa.astype(BF16)
        dp_ref[:, D:2 * D] = dga.astype(BF16)
        dcw = jnp.concatenate(
            [jnp.sum(dxc * xs[3 - k], axis=0, keepdims=True) for k in range(4)], axis=0)
        dcb = jnp.sum(dxc, axis=0, keepdims=True)
        dba = jnp.sum(dra, axis=0, keepdims=True)
        dbx = jnp.sum(dix, axis=0, keepdims=True)
        dlam = dsp * (-_sig(-lam))

        @pl.when(i == 0)
        def _():
            dcw_ref[...] = dcw
            dcb_ref[...] = dcb
            dba_ref[...] = dba
            dbx_ref[...] = dbx
            dlam_ref[...] = dlam
            for n in range(NH):
                dwa_ref[n] = dwa_new[n]
                dwx_ref[n] = dwx_new[n]

        @pl.when(i > 0)
        def _():
            dcw_ref[...] += dcw
            dcb_ref[...] += dcb
            dba_ref[...] += dba
            dbx_ref[...] += dbx
            dlam_ref[...] += dlam
            for n in range(NH):
                dwa_ref[n] += dwa_new[n]
                dwx_ref[n] += dwx_new[n]

    rev = lambda i: nt - 1 - i
    hb = tt // 8
    full = lambda shape: pl.BlockSpec(shape, lambda i: (0,) * len(shape))
    vecs = jax.ShapeDtypeStruct((1, D), F32)
    blk = jax.ShapeDtypeStruct((NH, HD, HD), F32)
    return pl.pallas_call(
        body, name="mixer_a_bwd",
        out_shape=(jax.ShapeDtypeStruct(dp4.shape, BF16), jax.ShapeDtypeStruct((4, D), F32), vecs, blk, blk,
                   vecs, vecs, vecs),
        grid=(nt,),
        in_specs=[pl.BlockSpec((None, tt, 2 * D), lambda i: (SLOT_A, rev(i), 0)),
                  pl.BlockSpec((None, 8, 2 * D), lambda i: (SLOT_A, jnp.maximum(rev(i) * hb - 1, 0), 0)),
                  pl.BlockSpec((tt, D), lambda i: (rev(i), 0)),
                  pl.BlockSpec((8, D), lambda i: (jnp.maximum(rev(i) * hb - 1, 0), 0)),
                  pl.BlockSpec((tt, D), lambda i: (rev(i), 0)),
                  pl.BlockSpec(memory_space=pl.ANY),
                  full((4, D)), full((1, D)), full((NH, HD, HD)), full((NH, HD, HD)),
                  full((1, D)), full((1, D)), full((1, D))],
        out_specs=(pl.BlockSpec((None, tt, 2 * D), lambda i: (SLOT_A, rev(i), 0)),
                   full((4, D)), full((1, D)), full((NH, HD, HD)), full((NH, HD, HD)),
                   full((1, D)), full((1, D)), full((1, D))),
        scratch_shapes=[pltpu.VMEM((8, D), F32), pltpu.VMEM((1, D), F32), pltpu.VMEM((1, D), F32)],
        input_output_aliases={5: 0},
        compiler_params=_params(("arbitrary",), VMEM_BIG),
    )(p4, p4, hseq, hseq, dya, dp4, cw, cb, wa, wx, ba, bx, lam)


def _norm_bwd(dh1, dx1, x, gain, tt):
    t = x.shape[0]

    def body(dh_ref, dx1_ref, x_ref, g_ref, dx_ref, dg_ref):
        i = pl.program_id(0)
        r, n = _rms_stats(x_ref[...])
        dh = dh_ref[...]
        dx_ref[...] = dx1_ref[...] + _rms_bwd(dh * g_ref[...], n, r)
        dgv = jnp.sum(dh * n, axis=0, keepdims=True)

        @pl.when(i == 0)
        def _():
            dg_ref[...] = dgv

        @pl.when(i > 0)
        def _():
            dg_ref[...] += dgv

    tile = pl.BlockSpec((tt, D), lambda i: (i, 0))
    vec = pl.BlockSpec((1, D), lambda i: (0, 0))
    return pl.pallas_call(
        body, name="norm_bwd",
        out_shape=(jax.ShapeDtypeStruct((t, D), F32), jax.ShapeDtypeStruct((1, D), F32)),
        grid=(t // tt,), in_specs=[tile, tile, tile, vec], out_specs=(tile, vec),
        compiler_params=_params(("arbitrary",)),
    )(dh1, dx1, x, gain)


def _local_step(x, tgt, w_in, stk_rest, conv_a_w, conv_f_w, small, cidx):
    t = x.shape[0]
    tt = min(256, t)
    tm = min(1024, t)
    tk = min(2048, t)
    wa_bf = small["lru_wa"].astype(BF16)
    wx_bf = small["lru_wx"].astype(BF16)

    h1 = _norm_fwd(x, small["norm_pre_mix"], tt)
    p4, stk_rest = _mm_in_gather(h1, w_in, stk_rest, REST, tm)
    w = dict(zip(REST, _gather_forward(stk_rest, REST)))
    w["w_in"] = w_in
    w_br_a = w["w_branch_a"].reshape(D, D)
    w_br_b = w["w_branch_b"].reshape(D, D)
    w_out = w["w_out"].reshape(D, D)
    w_down = w["w_down"].reshape(DFF, D)
    ya, hseq = _mixer_a_fwd(p4, conv_a_w, small["conv_a_b"], wa_bf, wx_bf, small["lru_ba"], small["lru_bx"],
                            small["lru_lambda"], tt)
    yb, o_all, ss = _hgrn2_fwd(p4, small["hg_lb_logits"], small["hg_norm_g"], tt)
    za, zb, mixb, m2, x1, h2 = _mid_fwd(ya, yb, p4, x, w_br_a, w_br_b, w_out, small["norm_post_mix"],
                                        small["norm_pre_ffn"], min(512, t))
    pre_g, pre_v, y, uv, gl, dgl = _up_act_fwd(h2, w["w_up"], conv_f_w, small["conv_f_b"], tt)
    dx2, dm3, lossv, d_norm_post_ffn = _down_loss(y, w_down, x1, tgt, small["norm_post_ffn"], min(512, t))

    d_w_down = _mm_tn(y, dm3, DFF // 2, D, tk, "mm_dw_down")
    dy = _mm_nt(dm3, w_down, BF16, tm, "mm_dy")
    dup_pre, d_conv_f_w, d_conv_f_b = _ffn_act_bwd(dy, pre_g, pre_v, uv, gl, dgl, conv_f_w, tt)
    d_w_up = _mm_tn(h2, dup_pre, D, SH_UP, tk, "mm_dw_up", stacked_out=True)
    dh2 = _mm_nt_sharded(dup_pre, w["w_up"], tm, "mm_dh2")
    dx1, dm2, dza, dzb, dya, dyb, dp4, d_norm_post_mix, d_norm_pre_ffn = _mid_bwd(
        dh2, dx2, x1, m2, za, zb, p4, w_br_a, w_br_b, w_out, small["norm_post_mix"], small["norm_pre_ffn"], tt)
    d_w_out = _mm_tn(mixb, dm2, D, D, tm, "mm_dw_out")
    d_w_br_a = _mm_tn(ya, dza, D, D, tm, "mm_dw_bra")
    d_w_br_b = _mm_tn(yb, dzb, D, D, tm, "mm_dw_brb")
    early = {"w_branch_a": d_w_br_a.reshape(NCHIP, SH_BR, D), "w_branch_b": d_w_br_b.reshape(NCHIP, SH_BR, D),
             "w_out": d_w_out.reshape(NCHIP, SH_BR, D), "w_up": d_w_up, "w_down": d_w_down.reshape(NCHIP, SH_DN, D)}
    rb, _ = _reduce_stage1(early, REST, (), "reduce_d2d_in_early")
    p_rest = [_sum_own_half(early[n], rb[n], cidx, "sum_half_" + n) for n in REST]
    dp4, d_lb, d_hg_norm_g, q_rest = _hgrn2_bwd(p4, o_all, ss, dyb, dp4, small["hg_lb_logits"], small["hg_norm_g"],
                                                p_rest, tt)
    dp4, d_conv_a_w, d_conv_a_b, d_lru_wa, d_lru_wx, d_lru_ba, d_lru_bx, d_lru_lambda = _mixer_a_bwd(
        p4, hseq, dya, dp4, conv_a_w, small["conv_a_b"], wa_bf, wx_bf, small["lru_ba"], small["lru_bx"],
        small["lru_lambda"], tt)
    d_w_in = _mm_tn(h1, dp4, D, SH_IN, tk, "mm_dw_in", stacked_slot_fn=_slot_of_chip, stacked_out=True)
    rb, _ = _reduce_stage1({"w_in": d_w_in}, ("w_in",), (), "reduce_d2d_in_w_in")
    p_w_in = _sum_own_half(d_w_in, rb["w_in"], cidx, "sum_half_w_in")
    dh1, q_w_in = _mm_dh1_exchange(dp4, w_in, p_w_in, tm)
    grad_x, d_norm_pre_mix = _norm_bwd(dh1, dx1, x, small["norm_pre_mix"], tt)

    smalls = {
        "norm_pre_mix": d_norm_pre_mix, "conv_a_b": d_conv_a_b, "lru_ba": d_lru_ba, "lru_bx": d_lru_bx,
        "lru_lambda": d_lru_lambda, "hg_lb_logits": d_lb, "hg_norm_g": d_hg_norm_g, "norm_post_mix": d_norm_post_mix,
        "norm_pre_ffn": d_norm_pre_ffn, "norm_post_ffn": d_norm_post_ffn, "lossv": lossv,
        "conv_a_w": d_conv_a_w, "lru_wa": d_lru_wa, "lru_wx": d_lru_wx,
        "conv_f_b": d_conv_f_b, "conv_f_w": d_conv_f_w,
    }
    p_big = dict(zip(REST, p_rest), w_in=p_w_in)
    q_big = dict(zip(REST, q_rest), w_in=q_w_in)
    return grad_x, p_big, q_big, smalls


BIG = ("w_in", "w_branch_a", "w_branch_b", "w_out", "w_up", "w_down")
BIG_SHAPE = {"w_in": (D, SH_IN), "w_branch_a": (SH_BR, D), "w_branch_b": (SH_BR, D), "w_out": (SH_BR, D),
             "w_up": (D, SH_UP), "w_down": (SH_DN, D)}
NBIG = len(BIG)
REST = BIG[1:]
VEC_ROWS = (("norm_pre_mix", 0, 1), ("conv_a_b", 1, 1), ("lru_ba", 2, 1), ("lru_bx", 3, 1), ("lru_lambda", 4, 1),
            ("hg_lb_logits", 5, 2), ("hg_norm_g", 7, 1), ("norm_post_mix", 8, 1), ("norm_pre_ffn", 9, 1),
            ("norm_post_ffn", 10, 1))
ROW_LOSS = 11
ROW_CONV_A = 12
S1_ROWS = 16
S2_ROWS = 8


def _place():
    x, y, c = lax.axis_index("x"), lax.axis_index("y"), lax.axis_index("c")
    chips = [(1 - x, y), (x, 1 - y), (1 - x, 1 - y)]
    return x, y, c, 2 * x + y, chips


def _remote(src, dst, ssem, rsem, dev):
    return pltpu.make_async_remote_copy(src_ref=src, dst_ref=dst, send_sem=ssem, recv_sem=rsem,
                                        device_id=dev, device_id_type=MESH)


def _hbm_call(body, name, ins, out_shapes, n_sems, aliases=None):
    any_spec = pl.BlockSpec(memory_space=pl.ANY)
    return pl.pallas_call(
        body, name=name, out_shape=tuple(out_shapes),
        in_specs=[any_spec] * len(ins), out_specs=tuple([any_spec] * len(out_shapes)),
        scratch_shapes=[pltpu.SemaphoreType.DMA((n,)) for n in n_sems],
        input_output_aliases=aliases or {},
        compiler_params=pltpu.CompilerParams(has_side_effects=True),
    )(*ins)


def _gather_copies(stk, names, ssem, rsem, fssem=None, frsem=None):
    x, y, c, j, chips = _place()
    sends, arrive, fwds, farrive = [], [], [], []
    for w, n in enumerate(names):
        hw = BIG_SHAPE[n][0] // 2
        mine = stk[w].at[j, pl.ds(c * hw, hw), :]
        for k, (cx, cy) in enumerate(chips):
            i = 3 * w + k
            got = stk[w].at[2 * cx + cy, pl.ds(c * hw, hw), :]
            other = stk[w].at[2 * cx + cy, pl.ds((1 - c) * hw, hw), :]
            sends.append(_remote(mine, mine, ssem.at[i], rsem.at[i], (cx, cy, c)))
            arrive.append(_remote(got, got, ssem.at[i], rsem.at[i], (cx, cy, c)))
            if fssem is not None:
                fwds.append(_remote(got, got, fssem.at[i], frsem.at[i], (x, y, 1 - c)))
                farrive.append(_remote(other, other, fssem.at[i], frsem.at[i], (x, y, 1 - c)))
    return sends, arrive, fwds, farrive


def _gather_first(stacked_w_in, conv_a_s, conv_f_s):
    ins = [stacked_w_in, conv_a_s, conv_f_s]
    out_shapes = [jax.ShapeDtypeStruct(stacked_w_in.shape, stacked_w_in.dtype)]
    out_shapes += [jax.ShapeDtypeStruct((NCHIP,) + a.shape, a.dtype) for a in (conv_a_s, conv_f_s)]

    def body(w_in, ca_src, cf_src, w_out, ca_dst, cf_dst, ssem, rsem, fssem, frsem, csend, crecv, lsem):
        del w_in
        x, y, c, j, chips = _place()
        conv = ((ca_src, ca_dst), (cf_src, cf_dst))
        locs = [pltpu.make_async_copy(src, dst.at[j], lsem.at[i]) for i, (src, dst) in enumerate(conv)]
        csends = [_remote(src, dst.at[j], csend.at[3 * i + k], crecv.at[3 * i + k], (cx, cy, c))
                  for i, (src, dst) in enumerate(conv) for k, (cx, cy) in enumerate(chips)]
        sends, arrive, fwds, farrive = _gather_copies([w_out], ("w_in",), ssem, rsem, fssem, frsem)
        for cp in locs + sends + csends:
            cp.start()
        for got, fwd in zip(arrive, fwds):
            got.wait_recv()
            fwd.start()
        for i, (_, dst) in enumerate(conv):
            for k, (cx, cy) in enumerate(chips):
                got = dst.at[2 * cx + cy]
                _remote(got, got, csend.at[3 * i + k], crecv.at[3 * i + k], (cx, cy, c)).wait_recv()
        for cp in farrive:
            cp.wait_recv()
        for cp in sends + fwds + csends:
            cp.wait_send()
        for cp in locs:
            cp.wait()

    return _hbm_call(body, "gather_first", ins, out_shapes, (3, 3, 3, 3, 6, 6, 2), aliases={0: 0})


def _gather_forward(stk, names):
    nw = len(names)

    def body(*refs):
        dst = refs[nw:2 * nw]
        ssem, rsem, fssem, frsem = refs[2 * nw:]
        _, _, fwds, farrive = _gather_copies(dst, names, ssem, rsem, fssem, frsem)
        for cp in fwds:
            cp.start()
        for cp in farrive:
            cp.wait_recv()
        for cp in fwds:
            cp.wait_send()

    out_shapes = [jax.ShapeDtypeStruct(a.shape, a.dtype) for a in stk]
    return _hbm_call(body, "gather_forward", stk, out_shapes, (3 * nw,) * 4, aliases={w: w for w in range(nw)})


def _exchange_copies(dst, pieces, ssem, rsem):
    x, y, c, j, chips = _place()
    sends, arrive = [], []
    for w in range(len(dst)):
        for k, (cx, cy) in enumerate(chips):
            i = 3 * w + k
            sends.append(_remote(pieces[w](2 * cx + cy), dst[w].at[j], ssem.at[i], rsem.at[i], (cx, cy, c)))
            got = dst[w].at[2 * cx + cy]
            arrive.append(_remote(got, got, ssem.at[i], rsem.at[i], (cx, cy, c)))
    return sends, arrive


def _reduce_stage1(big_g, names, smalls, name):
    nb = len(names)
    ins = [big_g[n] for n in names] + list(smalls)
    n_in = len(ins)
    halves = [BIG_SHAPE[n][0] // 2 for n in names]
    out_shapes = [jax.ShapeDtypeStruct((NCHIP, halves[w], BIG_SHAPE[n][1]), F32) for w, n in enumerate(names)]
    out_shapes += [jax.ShapeDtypeStruct(a.shape, F32) for a in smalls]

    def body(*refs):
        src, dst = refs[:n_in], refs[n_in:2 * n_in]
        ssem, rsem = refs[2 * n_in:]
        x, y, c, _, _ = _place()
        cps = []
        for w in range(n_in):
            s_ = src[w].at[:, pl.ds((1 - c) * halves[w], halves[w]), :] if w < nb else src[w]
            cp = _remote(s_, dst[w], ssem.at[w], rsem.at[w], (x, y, 1 - c))
            cp.start()
            cps.append(cp)
        for cp in cps:
            cp.wait()

    outs = _hbm_call(body, name, ins, out_shapes, (n_in, n_in))
    return dict(zip(names, outs[:nb])), outs[nb:]


def _reduce_stage2(ps1, ps2, ps3):
    ins = [ps1, ps2, ps3]
    h1, h2, h3 = S1_ROWS // 2, DUP // 2, D
    out_shapes = [jax.ShapeDtypeStruct((NCHIP, h1, D), F32), jax.ShapeDtypeStruct((NCHIP, S2_ROWS, h2), F32),
                  jax.ShapeDtypeStruct((NCHIP, h3, HD), F32)]

    def body(*refs):
        src, dst = refs[:3], refs[3:6]
        ssem, rsem = refs[6:]
        c = lax.axis_index("c")
        pieces = [lambda chip: src[0].at[pl.ds(c * h1, h1), :],
                  lambda chip: src[1].at[:, pl.ds(c * h2, h2)],
                  lambda chip: src[2].at[pl.ds(c * h3, h3), :]]
        sends, arrive = _exchange_copies(dst, pieces, ssem, rsem)
        for cp in sends:
            cp.start()
        for cp in arrive:
            cp.wait_recv()
        for cp in sends:
            cp.wait_send()

    return _hbm_call(body, "reduce_ici_small", ins, out_shapes, (9, 9))


def _reduce_stage3(f_big, fs1, fs2, fs3):
    ins = [f_big[n] for n in BIG] + [fs1, fs2, fs3]
    n_in = len(ins)
    halves = [BIG_SHAPE[n][0] // 2 for n in BIG]
    h1, h2, h3 = S1_ROWS // 2, DUP // 2, D
    out_shapes = [jax.ShapeDtypeStruct(BIG_SHAPE[n], F32) for n in BIG]
    out_shapes += [jax.ShapeDtypeStruct((S1_ROWS, D), F32), jax.ShapeDtypeStruct((S2_ROWS, DUP), F32),
                   jax.ShapeDtypeStruct((2 * D, HD), F32)]

    def body(*refs):
        dst = refs[n_in:2 * n_in]
        ssem, rsem = refs[2 * n_in:]
        x, y, c, _, _ = _place()

        def place(w, which):
            if w < NBIG:
                return dst[w].at[pl.ds(which * halves[w], halves[w]), :]
            if w == NBIG:
                return dst[w].at[pl.ds(which * h1, h1), :]
            if w == NBIG + 1:
                return dst[w].at[:, pl.ds(which * h2, h2)]
            return dst[w].at[pl.ds(which * h3, h3), :]

        cps = [_remote(place(w, c), place(w, c), ssem.at[w], rsem.at[w], (x, y, 1 - c)) for w in range(n_in)]
        for cp in cps:
            cp.start()
        for w in range(n_in):
            got = place(w, 1 - c)
            _remote(got, got, ssem.at[w], rsem.at[w], (x, y, 1 - c)).wait_recv()
        for cp in cps:
            cp.wait_send()

    outs = _hbm_call(body, "reduce_d2d_out", ins, out_shapes, (n_in, n_in), aliases={w: w for w in range(n_in)})
    return dict(zip(BIG, outs[:NBIG])), outs[NBIG], outs[NBIG + 1], outs[NBIG + 2]


def _row_tile(rows):
    for tr in (128, 176, 64, 16, 8):
        if rows % tr == 0:
            return tr
    return rows


def _sum_own_half(g, rb, cidx, name):
    s, rows, cols = g.shape
    half = rows // 2
    tr = _row_tile(half)
    nb = half // tr

    def body(c_ref, g_ref, r_ref, o_ref):
        del c_ref
        o_ref[...] = (g_ref[...] + r_ref[...]).astype(BF16)

    grid_spec = pltpu.PrefetchScalarGridSpec(
        num_scalar_prefetch=1, grid=(s, nb),
        in_specs=[pl.BlockSpec((None, tr, cols), lambda k, i, c: (k, c[0] * nb + i, 0)),
                  pl.BlockSpec((None, tr, cols), lambda k, i, c: (k, i, 0))],
        out_specs=pl.BlockSpec((None, tr, cols), lambda k, i, c: (k, i, 0)))
    return pl.pallas_call(
        body, name=name, grid_spec=grid_spec, out_shape=jax.ShapeDtypeStruct((s, half, cols), BF16),
        compiler_params=_params(("parallel", "parallel")),
    )(cidx, g, rb)


def _sum_chips(q, p, jc, name, by_cols=False):
    s, rows, cols = q.shape
    tr = _row_tile(rows)
    nb = rows // tr
    stacked = p.ndim == 3

    def body(jc_ref, q_ref, p_ref, o_ref):
        j = jc_ref[0]
        own = p_ref[...].astype(F32)
        acc = None
        for k in range(NCHIP):
            term = jnp.where(j == k, own, q_ref[k].astype(F32))
            acc = term if acc is None else acc + term
        o_ref[...] = acc

    if by_cols:
        half_spec = pl.BlockSpec((tr, cols), lambda i, jc_ref: (i, jc_ref[1]))
        out_shape = jax.ShapeDtypeStruct((rows, 2 * cols), F32)
    else:
        half_spec = pl.BlockSpec((tr, cols), lambda i, jc_ref: (jc_ref[1] * nb + i, 0))
        out_shape = jax.ShapeDtypeStruct((2 * rows, cols), F32)
    p_spec = pl.BlockSpec((None, tr, cols), lambda i, jc_ref: (jc_ref[0], i, 0)) if stacked else half_spec
    grid_spec = pltpu.PrefetchScalarGridSpec(
        num_scalar_prefetch=1, grid=(nb,),
        in_specs=[pl.BlockSpec((s, tr, cols), lambda i, jc_ref: (0, i, 0)), p_spec],
        out_specs=half_spec)
    return pl.pallas_call(
        body, name=name, grid_spec=grid_spec, out_shape=out_shape,
        compiler_params=_params(("parallel",)),
    )(jc, q, p)


def _place_shard(w, jc, name):
    rows, cols = w.shape
    tr = _row_tile(rows)

    def body(jc_ref, w_ref, o_ref):
        del jc_ref
        o_ref[...] = w_ref[...].astype(BF16)

    grid_spec = pltpu.PrefetchScalarGridSpec(
        num_scalar_prefetch=1, grid=(rows // tr,),
        in_specs=[pl.BlockSpec((tr, cols), lambda i, jc_ref: (i, 0))],
        out_specs=pl.BlockSpec((None, tr, cols), lambda i, jc_ref: (jc_ref[0], i, 0)))
    return pl.pallas_call(
        body, name=name, grid_spec=grid_spec, out_shape=jax.ShapeDtypeStruct((NCHIP, rows, cols), BF16),
        compiler_params=_params(("parallel",)),
    )(jc, w)


def _add(a, b, name):
    def body(a_ref, b_ref, o_ref):
        o_ref[...] = a_ref[...] + b_ref[...]

    return pl.pallas_call(body, name=name, out_shape=jax.ShapeDtypeStruct(a.shape, F32))(a, b)


def _pack_small(sm):
    vec_in = [sm[n] for n, _, _ in VEC_ROWS]
    nv = len(vec_in)

    def body(*refs):
        ins, lossv, dcw, dcfb, dcfw, s1, s2 = refs[:nv], refs[nv], refs[nv + 1], refs[nv + 2], refs[nv + 3], \
            refs[nv + 4], refs[nv + 5]
        for ref, (_, r0, nr) in zip(ins, VEC_ROWS):
            s1[r0:r0 + nr, :] = ref[...]
        s1[ROW_LOSS:ROW_LOSS + 1, :] = lossv[...]
        s1[ROW_CONV_A:ROW_CONV_A + 4, :] = dcw[...]
        s2[0:1, :] = dcfb[...]
        s2[1:4, :] = dcfw[...]
        s2[4:8, :] = jnp.zeros((4, DUP), F32)

    return pl.pallas_call(
        body, name="pack_small",
        out_shape=(jax.ShapeDtypeStruct((S1_ROWS, D), F32), jax.ShapeDtypeStruct((S2_ROWS, DUP), F32)),
    )(*vec_in, sm["lossv"], sm["conv_a_w"], sm["conv_f_b"], sm["conv_f_w"])


def _adam_math(w, g, m, v):
    m = ADAM_B1 * m + (1.0 - ADAM_B1) * g
    v = ADAM_B2 * v + (1.0 - ADAM_B2) * (g * g)
    m_hat = m / (1.0 - ADAM_B1 ** ADAM_STEP)
    v_hat = v / (1.0 - ADAM_B2 ** ADAM_STEP)
    delta = -ADAM_LR * (m_hat / (jnp.sqrt(v_hat) + ADAM_EPS) + ADAM_WD * w)
    return delta, m, v


def _adam(w, g, m, v, name):
    rows, cols = w.shape
    tr = _row_tile(rows)

    def body(w_ref, g_ref, m_ref, v_ref, d_ref, mo_ref, vo_ref):
        d_ref[...], mo_ref[...], vo_ref[...] = _adam_math(w_ref[...], g_ref[...], m_ref[...], v_ref[...])

    spec = pl.BlockSpec((tr, cols), lambda i: (i, 0))
    return pl.pallas_call(
        body, name=name, out_shape=(jax.ShapeDtypeStruct(w.shape, F32),) * 3, grid=(rows // tr,),
        in_specs=[spec] * 4, out_specs=(spec,) * 3,
        compiler_params=_params(("parallel",)),
    )(w, g, m, v)


def _adam_small(gs1, gs2, gs3, w, m, v):
    names = [n for n, _, _ in VEC_ROWS] + ["conv_f_b", "lru_wa", "lru_wx"]
    nn = len(names)

    def grad_of(i, g1, g2, g3):
        if i < len(VEC_ROWS):
            _, r0, nr = VEC_ROWS[i]
            return g1[r0:r0 + nr, :]
        if names[i] == "conv_f_b":
            return g2[0:1, :]
        return g3[0] if names[i] == "lru_wa" else g3[1]

    def body(*refs):
        g1, g2, g3 = refs[0], refs[1], refs[2]
        ws, ms, vs = refs[3:3 + nn], refs[3 + nn:3 + 2 * nn], refs[3 + 2 * nn:3 + 3 * nn]
        outs = refs[3 + 3 * nn:]
        for i in range(nn):
            d, mn, vn = _adam_math(ws[i][...], grad_of(i, g1, g2, g3), ms[i][...], vs[i][...])
            outs[i][...] = d
            outs[nn + i][...] = mn
            outs[2 * nn + i][...] = vn

    shapes = [jax.ShapeDtypeStruct(w[n].shape, F32) for n in names]
    outs = pl.pallas_call(body, name="adam_small", out_shape=tuple(shapes * 3))(
        gs1, gs2, gs3, *[w[n] for n in names], *[m[n] for n in names], *[v[n] for n in names])
    return {n: (outs[i], outs[nn + i], outs[2 * nn + i]) for i, n in enumerate(names)}


WEIGHTS = ("norm_pre_mix", "w_in", "conv_a_w", "conv_a_b", "lru_wa", "lru_ba", "lru_wx", "lru_bx", "lru_lambda",
           "hg_lb_logits", "hg_norm_g", "w_branch_a", "w_branch_b", "w_out", "norm_post_mix", "norm_pre_ffn",
           "w_up", "conv_f_w", "conv_f_b", "w_down", "norm_post_ffn")
NW = len(WEIGHTS)


def kernel(x, norm_pre_mix, w_in, conv_a_w, conv_a_b, lru_wa, lru_ba, lru_wx, lru_bx, lru_lambda, hg_lb_logits, hg_norm_g, w_branch_a, w_branch_b, w_out, norm_post_mix, norm_pre_ffn, w_up, conv_f_w, conv_f_b, w_down, norm_post_ffn, loss_target, m_norm_pre_mix, m_w_in, m_conv_a_w, m_conv_a_b, m_lru_wa, m_lru_ba, m_lru_wx, m_lru_bx, m_lru_lambda, m_hg_lb_logits, m_hg_norm_g, m_w_branch_a, m_w_branch_b, m_w_out, m_norm_post_mix, m_norm_pre_ffn, m_w_up, m_conv_f_w, m_conv_f_b, m_w_down, m_norm_post_ffn, v_norm_pre_mix, v_w_in, v_conv_a_w, v_conv_a_b, v_lru_wa, v_lru_ba, v_lru_wx, v_lru_bx, v_lru_lambda, v_hg_lb_logits, v_hg_norm_g, v_w_branch_a, v_w_branch_b, v_w_out, v_norm_post_mix, v_norm_pre_ffn, v_w_up, v_conv_f_w, v_conv_f_b, v_w_down, v_norm_post_ffn):
    rest = (norm_pre_mix, w_in, conv_a_w, conv_a_b, lru_wa, lru_ba, lru_wx, lru_bx, lru_lambda, hg_lb_logits, hg_norm_g, w_branch_a, w_branch_b, w_out, norm_post_mix, norm_pre_ffn, w_up, conv_f_w, conv_f_b, w_down, norm_post_ffn, loss_target, m_norm_pre_mix, m_w_in, m_conv_a_w, m_conv_a_b, m_lru_wa, m_lru_ba, m_lru_wx, m_lru_bx, m_lru_lambda, m_hg_lb_logits, m_hg_norm_g, m_w_branch_a, m_w_branch_b, m_w_out, m_norm_post_mix, m_norm_pre_ffn, m_w_up, m_conv_f_w, m_conv_f_b, m_w_down, m_norm_post_ffn, v_norm_pre_mix, v_w_in, v_conv_a_w, v_conv_a_b, v_lru_wa, v_lru_ba, v_lru_wx, v_lru_bx, v_lru_lambda, v_hg_lb_logits, v_hg_norm_g, v_w_branch_a, v_w_branch_b, v_w_out, v_norm_post_mix, v_norm_pre_ffn, v_w_up, v_conv_f_w, v_conv_f_b, v_w_down, v_norm_post_ffn)
    w_in_args = dict(zip(WEIGHTS, rest[:NW]))
    loss_target = rest[NW]
    m_args = dict(zip(WEIGHTS, rest[NW + 1:2 * NW + 1]))
    v_args = dict(zip(WEIGHTS, rest[2 * NW + 1:3 * NW + 1]))
    shape_of = {n: w_in_args[n].shape for n in WEIGHTS}

    def two_d(n, a):
        if n in BIG:
            return a.reshape(BIG_SHAPE[n])
        if n in ("lru_wa", "lru_wx"):
            return a.reshape(NH, HD, HD)
        return a.reshape(a.shape[-2:])

    w2 = {n: two_d(n, w_in_args[n]) for n in WEIGHTS}
    m2 = {n: two_d(n, m_args[n]) for n in WEIGHTS}
    v2 = {n: two_d(n, v_args[n]) for n in WEIGHTS}

    cidx = lax.axis_index("c").astype(jnp.int32).reshape(1)
    jchip = 2 * lax.axis_index("x") + lax.axis_index("y")

    jc = jnp.stack([jchip, lax.axis_index("c")]).astype(jnp.int32)

    shards = {n: _place_shard(w2[n], jc, "place_" + n) for n in BIG}
    conv_a_s = jnp.pad(w2["conv_a_w"], ((0, 4), (0, 0)))
    conv_f_s = jnp.pad(w2["conv_f_w"], ((0, 5), (0, 0)))
    w_in_full, conv_a_g, conv_f_g = _gather_first(shards["w_in"], conv_a_s, conv_f_s)
    conv_a_full = jnp.transpose(conv_a_g, (1, 0, 2)).reshape(8, D)[0:4]
    conv_f_full = jnp.transpose(conv_f_g, (1, 0, 2)).reshape(8, DUP)[0:3]
    small = {n: w2[n] for n in WEIGHTS if n not in BIG and n not in ("conv_a_w", "conv_f_w")}

    grad_x, p_big, q_big, sm_g = _local_step(
        x[0], loss_target[0], w_in_full, [shards[n] for n in REST], conv_a_full, conv_f_full, small, cidx)

    s1, s2 = _pack_small(sm_g)
    s3 = jnp.concatenate([sm_g["lru_wa"].reshape(D, HD), sm_g["lru_wx"].reshape(D, HD)], axis=0)
    _, (rs1, rs2, rs3) = _reduce_stage1({}, (), (s1, s2, s3), "reduce_d2d_in_small")
    ps1, ps2, ps3 = _add(s1, rs1, "add_s1"), _add(s2, rs2, "add_s2"), _add(s3, rs3, "add_s3")
    qs1, qs2, qs3 = _reduce_stage2(ps1, ps2, ps3)
    f_big = {n: _sum_chips(q_big[n], p_big[n], jc, "sum_chips_" + n) for n in BIG}
    fs1 = _sum_chips(qs1, ps1, jc, "sum_chips_s1")
    fs2 = _sum_chips(qs2, ps2, jc, "sum_chips_s2", by_cols=True)
    fs3 = _sum_chips(qs3, ps3, jc, "sum_chips_s3")
    g_big, gs1, gs2, gs3 = _reduce_stage3(f_big, fs1, fs2, fs3)

    res = {}
    for n in BIG:
        d, mn, vn = _adam(w2[n], g_big[n], m2[n], v2[n], "adam_" + n)
        res[n] = (g_big[n], d, mn, vn)
    small_res = _adam_small(gs1, gs2, gs3.reshape(2, NH, HD, HD), w2, m2, v2)
    for n, r0, nr in VEC_ROWS:
        res[n] = (gs1[r0:r0 + nr],) + small_res[n]
    res["conv_f_b"] = (gs2[0:1],) + small_res["conv_f_b"]
    res["lru_wa"] = (gs3[0:D].reshape(NH, HD, HD),) + small_res["lru_wa"]
    res["lru_wx"] = (gs3[D:2 * D].reshape(NH, HD, HD),) + small_res["lru_wx"]
    g_ca = lax.dynamic_slice_in_dim(gs1[ROW_CONV_A:ROW_CONV_A + 4], jchip * (D // NCHIP), D // NCHIP, axis=1)
    g_cf = lax.dynamic_slice_in_dim(gs2[1:4], jchip * SH_UP, SH_UP, axis=1)
    res["conv_a_w"] = (g_ca,) + _adam(w2["conv_a_w"], g_ca, m2["conv_a_w"], v2["conv_a_w"], "adam_conv_a_w")
    res["conv_f_w"] = (g_cf,) + _adam(w2["conv_f_w"], g_cf, m2["conv_f_w"], v2["conv_f_w"], "adam_conv_f_w")

    loss = (0.5 / D) * jnp.sum(gs1[ROW_LOSS])
    out = [loss, grad_x.reshape(x.shape)]
    for part in range(4):
        out += [res[n][part].reshape(shape_of[n]) for n in WEIGHTS]
    return tuple(out)
```

```python
import functools

import jax
import jax.numpy as jnp
from jax import lax
from jax.experimental import pallas as pl
from jax.experimental.pallas import tpu as pltpu

F32 = jnp.float32
BF16 = jnp.bfloat16

D = 1024
NH = 8
HD = 128
CH = 32
DFF = 2816
DUP = 2 * DFF
NCHIP = 4
SH_IN = 2 * D
SH_UP = DUP // NCHIP
SH_DN = DFF // NCHIP
SH_BR = D // NCHIP
EPS = 1e-6
LRU_C = 8.0
ADAM_LR = 0.001
ADAM_B1 = 0.9
ADAM_B2 = 0.999
ADAM_EPS = 1e-08
ADAM_WD = 0.01
ADAM_STEP = 10
VMEM_BIG = 56 * 1024 * 1024
MESH = pl.DeviceIdType.MESH

SLOT_A, SLOT_B, SLOT_C, SLOT_G = 2, 0, 1, 3


def _slot_of_chip(s):
    return jnp.where(s == 3, 3, (s + 2) % 3)


def _params(sem, vmem=None):
    return pltpu.CompilerParams(dimension_semantics=sem, vmem_limit_bytes=vmem)


_GC = 0.7978845608028654
_GA = 0.044715


def _gelu(x):
    return 0.5 * x * (1.0 + jnp.tanh(_GC * (x + _GA * x * x * x)))


def _gelu_and_grad(x):
    x2 = x * x
    th = jnp.tanh(_GC * x * (1.0 + _GA * x2))
    g = 0.5 * x * (1.0 + th)
    dg = 0.5 * (1.0 + th) + 0.5 * x * (1.0 - th * th) * _GC * (1.0 + 3.0 * _GA * x2)
    return g, dg


def _sig(x):
    return jax.nn.sigmoid(x)


def _dot(a, b):
    return jnp.dot(a, b, preferred_element_type=F32)


def _dot_nt(a, b):
    return lax.dot_general(a, b, (((1,), (1,)), ((), ())), preferred_element_type=F32)


def _dot_tn(a, b):
    return lax.dot_general(a, b, (((0,), (0,)), ((), ())), preferred_element_type=F32)


def _chunk_cumsum(x):
    pos = lax.broadcasted_iota(jnp.int32, (x.shape[0], 1), 0) & (CH - 1)
    d = 1
    while d < CH:
        x = x + jnp.where(pos >= d, pltpu.roll(x, d, 0), 0.0)
        d *= 2
    return x


def _chunk_revcumsum(x):
    n = x.shape[0]
    pos = lax.broadcasted_iota(jnp.int32, (n, 1), 0) & (CH - 1)
    d = 1
    while d < CH:
        x = x + jnp.where(pos < CH - d, pltpu.roll(x, n - d, 0), 0.0)
        d *= 2
    return x


def _chunk_last(x):
    n = x.shape[0]
    return jnp.concatenate(
        [jnp.broadcast_to(x[c * CH + CH - 1:c * CH + CH, :], (CH, x.shape[1])) for c in range(n // CH)], axis=0)


def _chunk_total(x):
    n = x.shape[0]
    return jnp.concatenate(
        [jnp.broadcast_to(jnp.sum(x[c * CH:(c + 1) * CH, :], axis=0, keepdims=True), (CH, x.shape[1]))
         for c in range(n // CH)], axis=0)


def _rms_stats(x):
    r = lax.rsqrt(jnp.mean(x * x, axis=-1, keepdims=True) + EPS)
    return r, x * r


def _rms_bwd(gd, n, r):
    return r * (gd - n * jnp.mean(gd * n, axis=-1, keepdims=True))


def _shift_rows(x, d, fill):
    rows = lax.broadcasted_iota(jnp.int32, (x.shape[0], 1), 0)
    return jnp.where(rows >= d, pltpu.roll(x, d, 0), fill)


def _scan_down(a, u, carry):
    n = a.shape[0]
    pos = lax.broadcasted_iota(jnp.int32, (n, 1), 0) & 7
    for d in (1, 2, 4):
        u = a * jnp.where(pos >= d, pltpu.roll(u, d, 0), 0.0) + u
        a = a * jnp.where(pos >= d, pltpu.roll(a, d, 0), 1.0)
    out = []
    for v in range(n // 8):
        h = a[v * 8:v * 8 + 8, :] * carry + u[v * 8:v * 8 + 8, :]
        carry = h[7:8, :]
        out.append(h)
    return jnp.concatenate(out, axis=0)


def _scan_up(b, g, carry):
    n = b.shape[0]
    pos = lax.broadcasted_iota(jnp.int32, (n, 1), 0) & 7
    for d in (1, 2, 4):
        g = g + b * jnp.where(pos < 8 - d, pltpu.roll(g, n - d, 0), 0.0)
        b = b * jnp.where(pos < 8 - d, pltpu.roll(b, n - d, 0), 1.0)
    out = [None] * (n // 8)
    for v in reversed(range(n // 8)):
        h = g[v * 8:v * 8 + 8, :] + b[v * 8:v * 8 + 8, :] * carry
        carry = h[0:1, :]
        out[v] = h
    return jnp.concatenate(out, axis=0)


def _shift_rows_up(x, d, fill):
    n = x.shape[0]
    rows = lax.broadcasted_iota(jnp.int32, (n, 1), 0)
    return jnp.where(rows < n - d, pltpu.roll(x, n - d, 0), fill)


def _mm_nn_sharded(a, b3, out_dtype, tm, name, slot_fn=None):
    m, k = a.shape
    s, _, ns = b3.shape

    def body(a_ref, b_ref, o_ref):
        o_ref[...] = _dot(a_ref[...], b_ref[...]).astype(out_dtype)

    if slot_fn is None:
        out_shape = jax.ShapeDtypeStruct((m, s * ns), out_dtype)
        out_spec = pl.BlockSpec((tm, ns), lambda j, i: (i, j))
    else:
        out_shape = jax.ShapeDtypeStruct((s, m, ns), out_dtype)
        out_spec = pl.BlockSpec((None, tm, ns), lambda j, i: (slot_fn(j), i, 0))
    return pl.pallas_call(
        body, name=name, out_shape=out_shape, grid=(s, m // tm),
        in_specs=[pl.BlockSpec((tm, k), lambda j, i: (i, 0)),
                  pl.BlockSpec((None, k, ns), lambda j, i: (j, 0, 0))],
        out_specs=out_spec,
        compiler_params=_params(("parallel", "parallel"), VMEM_BIG),
    )(a, b3)


def _mm_in_gather(a, w_in3, stk_rest, names_rest, tm):
    m, k = a.shape
    s, _, ns = w_in3.shape
    nr = len(stk_rest)
    mt = m // tm

    def body(a_ref, b_ref, *rest):
        o_ref = rest[nr]
        stk = rest[nr + 1:2 * nr + 1]
        ssem, rsem = rest[2 * nr + 1:]
        step = pl.program_id(0) * mt + pl.program_id(1)
        sends, arrive, _, _ = _gather_copies(stk, names_rest, ssem, rsem)

        @pl.when(step == 0)
        def _():
            for cp in sends:
                cp.start()

        o_ref[...] = _dot(a_ref[...], b_ref[...])

        @pl.when(step == s * mt - 1)
        def _():
            for cp in arrive:
                cp.wait_recv()
            for cp in sends:
                cp.wait_send()

    any_spec = pl.BlockSpec(memory_space=pl.ANY)
    outs = pl.pallas_call(
        body, name="mm_in",
        out_shape=(jax.ShapeDtypeStruct((s, m, ns), F32),) + tuple(jax.ShapeDtypeStruct(x.shape, x.dtype) for x in stk_rest),
        grid=(s, mt),
        in_specs=[pl.BlockSpec((tm, k), lambda j, i: (i, 0)), pl.BlockSpec((None, k, ns), lambda j, i: (j, 0, 0))]
        + [any_spec] * nr,
        out_specs=(pl.BlockSpec((None, tm, ns), lambda j, i: (_slot_of_chip(j), i, 0)),) + (any_spec,) * nr,
        scratch_shapes=[pltpu.SemaphoreType.DMA((3 * nr,)), pltpu.SemaphoreType.DMA((3 * nr,))],
        input_output_aliases={2 + w: 1 + w for w in range(nr)},
        compiler_params=pltpu.CompilerParams(dimension_semantics=("arbitrary", "arbitrary"),
                                             vmem_limit_bytes=VMEM_BIG, has_side_effects=True),
    )(a, w_in3, *stk_rest)
    return outs[0], list(outs[1:])


def _mm_dh1_exchange(dp4, w_in3, p_w_in, x, dx1, gain, tm):
    s, k, ns = w_in3.shape
    m = dp4.shape[1]
    mt = m // tm

    def body(a_ref, b_ref, p_ref, x_ref, dx1_ref, g_ref, o_ref, q_ref, dg_ref, ssem, rsem):
        i, j = pl.program_id(0), pl.program_id(1)
        sends, arrive = _exchange_copies([q_ref], [lambda chip: p_ref.at[chip]], ssem, rsem)

        @pl.when((i == 0) & (j == 0))
        def _():
            for cp in sends:
                cp.start()

        @pl.when(j == 0)
        def _():
            o_ref[...] = _dot_nt(a_ref[...], b_ref[...])

        @pl.when(j > 0)
        def _():
            o_ref[...] += _dot_nt(a_ref[...], b_ref[...])

        @pl.when(j == s - 1)
        def _():
            dh = o_ref[...]
            r, n = _rms_stats(x_ref[...])
            o_ref[...] = dx1_ref[...] + _rms_bwd(dh * g_ref[...], n, r)
            dgv = jnp.sum(dh * n, axis=0, keepdims=True)

            @pl.when(i == 0)
            def _():
                dg_ref[...] = dgv

            @pl.when(i > 0)
            def _():
                dg_ref[...] += dgv

        @pl.when((i == mt - 1) & (j == s - 1))
        def _():
            for cp in arrive:
                cp.wait_recv()
            for cp in sends:
                cp.wait_send()

    any_spec = pl.BlockSpec(memory_space=pl.ANY)
    row_tile = pl.BlockSpec((tm, k), lambda i, j: (i, 0))
    vec = pl.BlockSpec((1, k), lambda i, j: (0, 0))
    return pl.pallas_call(
        body, name="mm_dh1",
        out_shape=(jax.ShapeDtypeStruct((m, k), F32), jax.ShapeDtypeStruct(p_w_in.shape, BF16),
                   jax.ShapeDtypeStruct((1, k), F32)),
        grid=(mt, s),
        in_specs=[pl.BlockSpec((None, tm, ns), lambda i, j: (_slot_of_chip(j), i, 0)),
                  pl.BlockSpec((None, k, ns), lambda i, j: (j, 0, 0)), any_spec, row_tile, row_tile, vec],
        out_specs=(row_tile, any_spec, vec),
        scratch_shapes=[pltpu.SemaphoreType.DMA((3,)), pltpu.SemaphoreType.DMA((3,))],
        compiler_params=pltpu.CompilerParams(dimension_semantics=("arbitrary", "arbitrary"),
                                             vmem_limit_bytes=VMEM_BIG, has_side_effects=True),
    )(dp4, w_in3, p_w_in, x, dx1, gain)


def _mm_nt_sharded(a, b3, tm, name, stacked_slot_fn=None):
    s, k, ns = b3.shape
    m = a.shape[1] if stacked_slot_fn is not None else a.shape[0]

    def body(a_ref, b_ref, o_ref):
        j = pl.program_id(1)
        @pl.when(j == 0)
        def _():
            o_ref[...] = _dot_nt(a_ref[...], b_ref[...])

        @pl.when(j > 0)
        def _():
            o_ref[...] += _dot_nt(a_ref[...], b_ref[...])

    if stacked_slot_fn is None:
        a_spec = pl.BlockSpec((tm, ns), lambda i, j: (i, j))
    else:
        a_spec = pl.BlockSpec((None, tm, ns), lambda i, j: (stacked_slot_fn(j), i, 0))
    return pl.pallas_call(
        body, name=name, out_shape=jax.ShapeDtypeStruct((m, k), F32), grid=(m // tm, s),
        in_specs=[a_spec, pl.BlockSpec((None, k, ns), lambda i, j: (j, 0, 0))],
        out_specs=pl.BlockSpec((tm, k), lambda i, j: (i, 0)),
        compiler_params=_params(("parallel", "arbitrary"), VMEM_BIG),
    )(a, b3)


def _mm_nt(a, b, out_dtype, tm, name):
    m, k = a.shape
    n = b.shape[0]

    def body(a_ref, b_ref, o_ref):
        o_ref[...] = _dot_nt(a_ref[...], b_ref[...]).astype(out_dtype)

    return pl.pallas_call(
        body, name=name, out_shape=jax.ShapeDtypeStruct((m, n), out_dtype), grid=(m // tm,),
        in_specs=[pl.BlockSpec((tm, k), lambda i: (i, 0)), pl.BlockSpec((n, k), lambda i: (0, 0))],
        out_specs=pl.BlockSpec((tm, n), lambda i: (i, 0)),
        compiler_params=_params(("parallel",), VMEM_BIG),
    )(a, b)


def _mm_tn(a, g, tkk, tn, tk, name, stacked_slot_fn=None, stacked_out=False):
    m, k = a.shape
    if stacked_slot_fn is not None:
        n = g.shape[0] * g.shape[2]
        g_spec = pl.BlockSpec((None, tk, tn), lambda kk, j, mm: (stacked_slot_fn(j), mm, 0))
    else:
        n = g.shape[1]
        g_spec = pl.BlockSpec((tk, tn), lambda kk, j, mm: (mm, j))
    steps = m // tk

    def body(a_ref, g_ref, o_ref, acc_ref):
        mm = pl.program_id(2)

        @pl.when(mm == 0)
        def _():
            acc_ref[...] = _dot_tn(a_ref[...], g_ref[...])

        @pl.when(mm > 0)
        def _():
            acc_ref[...] += _dot_tn(a_ref[...], g_ref[...])

        @pl.when(mm == steps - 1)
        def _():
            o_ref[...] = acc_ref[...].astype(BF16)

    if stacked_out:
        out_shape = jax.ShapeDtypeStruct((n // tn, k, tn), BF16)
        out_spec = pl.BlockSpec((None, tkk, tn), lambda kk, j, mm: (j, kk, 0))
    else:
        out_shape = jax.ShapeDtypeStruct((k, n), BF16)
        out_spec = pl.BlockSpec((tkk, tn), lambda kk, j, mm: (kk, j))
    return pl.pallas_call(
        body, name=name, out_shape=out_shape, grid=(k // tkk, n // tn, steps),
        in_specs=[pl.BlockSpec((tk, tkk), lambda kk, j, mm: (mm, kk)), g_spec],
        out_specs=out_spec,
        scratch_shapes=[pltpu.VMEM((tkk, tn), F32)],
        compiler_params=_params(("parallel", "parallel", "arbitrary"), VMEM_BIG),
    )(a, g)


def _norm_fwd(x, gain, tt):
    t = x.shape[0]

    def body(x_ref, g_ref, h_ref):
        _, n = _rms_stats(x_ref[...])
        h_ref[...] = (n * g_ref[...]).astype(BF16)

    return pl.pallas_call(
        body, name="norm_fwd", out_shape=jax.ShapeDtypeStruct((t, D), BF16), grid=(t // tt,),
        in_specs=[pl.BlockSpec((tt, D), lambda i: (i, 0)), pl.BlockSpec((1, D), lambda i: (0, 0))],
        out_specs=pl.BlockSpec((tt, D), lambda i: (i, 0)),
        compiler_params=_params(("parallel",)),
    )(x, gain)


def _lru_gates(xc, wa_ref, wx_ref, ba, bx, lam):
    xcb = xc.astype(BF16)
    ra = jnp.concatenate([_dot(xcb[:, n * HD:(n + 1) * HD], wa_ref[n]) for n in range(NH)], axis=1) + ba
    ix = jnp.concatenate([_dot(xcb[:, n * HD:(n + 1) * HD], wx_ref[n]) for n in range(NH)], axis=1) + bx
    r = _sig(ra)
    ig = _sig(ix)
    z = -lam
    sp = jnp.maximum(z, 0.0) + jnp.log1p(jnp.exp(-jnp.abs(z)))
    log_a = -LRU_C * r * sp
    a = jnp.exp(log_a)
    z2 = 2.0 * log_a
    series = -z2 * (1.0 + z2 * (0.5 + z2 * (1.0 / 6.0 + z2 * (1.0 / 24.0))))
    om = jnp.where(z2 > -0.02, series, 1.0 - jnp.exp(z2))
    mult = jnp.sqrt(om)
    return xcb, r, ig, sp, a, mult


def _mixer_a_fwd(p4, cw, cb, wa, wx, ba, bx, lam, tt):
    t = p4.shape[1]

    def body(p_ref, cw_ref, cb_ref, wa_ref, wx_ref, ba_ref, bx_ref, lam_ref, ya_ref, h_ref, halo, hc):
        i = pl.program_id(0)

        @pl.when(i == 0)
        def _():
            halo[...] = jnp.zeros((8, D), F32)
            hc[...] = jnp.zeros((1, D), F32)

        xa = p_ref[:, 0:D]
        ga = p_ref[:, D:2 * D]
        xe = jnp.concatenate([halo[...], xa], axis=0)
        xc = (cb_ref[...] + cw_ref[3:4, :] * xe
              + sum(cw_ref[3 - s:4 - s, :] * pltpu.roll(xe, s, 0) for s in (1, 2, 3)))[8:, :]
        halo[...] = xa[tt - 8:, :]
        _, _, ig, _, a, mult = _lru_gates(xc, wa_ref, wx_ref, ba_ref[...], bx_ref[...], lam_ref[...])
        u = mult * ig * xc
        h = _scan_down(a, u, hc[...])
        hc[...] = h[tt - 1:tt, :]
        h_ref[...] = h
        ya_ref[...] = (h * _gelu(ga)).astype(BF16)

    full = lambda shape: pl.BlockSpec(shape, lambda i: (0,) * len(shape))
    return pl.pallas_call(
        body, name="mixer_a_fwd",
        out_shape=(jax.ShapeDtypeStruct((t, D), BF16), jax.ShapeDtypeStruct((t, D), F32)),
        grid=(t // tt,),
        in_specs=[pl.BlockSpec((None, tt, 2 * D), lambda i: (SLOT_A, i, 0)),
                  full((4, D)), full((1, D)), full((NH, HD, HD)), full((NH, HD, HD)),
                  full((1, D)), full((1, D)), full((1, D))],
        out_specs=(pl.BlockSpec((tt, D), lambda i: (i, 0)), pl.BlockSpec((tt, D), lambda i: (i, 0))),
        scratch_shapes=[pltpu.VMEM((8, D), F32), pltpu.VMEM((1, D), F32)],
        compiler_params=_params(("arbitrary",), VMEM_BIG),
    )(p4, cw, cb, wa, wx, ba, bx, lam)


def _chunk_masks(tt):
    row = lax.broadcasted_iota(jnp.int32, (tt, tt), 0)
    col = lax.broadcasted_iota(jnp.int32, (tt, tt), 1)
    same = jnp.right_shift(row, 5) == jnp.right_shift(col, 5)
    return same & (col <= row)


def _hg_head_fwd(q, fz, lbh):
    sg = _sig(fz)
    sgn = _sig(-fz)
    f = lbh + (1.0 - lbh) * sg
    logf = jnp.log(f)
    k = (1.0 - lbh) * sgn
    g = _chunk_cumsum(logf)
    gu = _chunk_last(g) - g
    eg = jnp.exp(g)
    eng = jnp.exp(-g)
    egu = jnp.exp(gu)
    qt = q * eg
    kt = k * eng
    kd = k * egu
    return sg, sgn, f, k, g, eg, eng, egu, qt, kt, kd


def _lb_of(logits_ref):
    return _sig(logits_ref[0:1, :] - logits_ref[1:2, :])


def _hgrn2_fwd(p4, logits, gnorm, tt):
    t = p4.shape[1]
    nc = tt // CH

    def body(p_ref, lg_ref, gn_ref, yb_ref, o_ref, ss_ref, st):
        i = pl.program_id(0)

        @pl.when(i == 0)
        def _():
            st[...] = jnp.zeros((NH, HD, HD), F32)

        low = _chunk_masks(tt)
        lb = _lb_of(lg_ref)
        heads = [slice(h * HD, (h + 1) * HD) for h in range(NH)]
        _, _, _, _, g, _, _, _, qt, kt, kd = _hg_head_fwd(p_ref[0, :, 0:D], p_ref[0, :, D:2 * D], lb)
        qtb, ktb, kdb, vb = qt.astype(BF16), kt.astype(BF16), kd.astype(BF16), p_ref[1, :, 0:D].astype(BF16)
        decs = [jnp.exp(g[c * CH + CH - 1:c * CH + CH, :]) for c in range(nc)]
        o_in = []
        for hs in heads:
            att = jnp.where(low, _dot_nt(qtb[:, hs], ktb[:, hs]), 0.0)
            o_in.append(_dot(att.astype(BF16), vb[:, hs]))
        s_t = [st[h] for h in range(NH)]
        pieces = [[None] * nc for _ in range(NH)]
        for c in range(nc):
            sl = slice(c * CH, (c + 1) * CH)
            for h, hs in enumerate(heads):
                s_bf = s_t[h].astype(BF16)
                ss_ref[c, h] = s_bf
                pieces[h][c] = o_in[h][sl] + _dot_nt(qtb[sl, hs], s_bf)
                s_t[h] = s_t[h] * decs[c][:, hs] + _dot_tn(vb[sl, hs], kdb[sl, hs])
        for h, hs in enumerate(heads):
            st[h] = s_t[h]
            o = jnp.concatenate(pieces[h], axis=0)
            _, n = _rms_stats(o)
            og = p_ref[1, :, D + h * HD:D + (h + 1) * HD]
            o_ref[:, hs] = o
            yb_ref[:, hs] = (n * gn_ref[:, hs] * (og * _sig(og))).astype(BF16)

    return pl.pallas_call(
        body, name="hgrn2_fwd",
        out_shape=(jax.ShapeDtypeStruct((t, D), BF16), jax.ShapeDtypeStruct((t, D), F32),
                   jax.ShapeDtypeStruct((t // CH, NH, HD, HD), BF16)),
        grid=(t // tt,),
        in_specs=[pl.BlockSpec((2, tt, 2 * D), lambda i: (0, i, 0)),
                  pl.BlockSpec((2, D), lambda i: (0, 0)), pl.BlockSpec((1, D), lambda i: (0, 0))],
        out_specs=(pl.BlockSpec((tt, D), lambda i: (i, 0)), pl.BlockSpec((tt, D), lambda i: (i, 0)),
                   pl.BlockSpec((nc, NH, HD, HD), lambda i: (i, 0, 0, 0))),
        scratch_shapes=[pltpu.VMEM((NH, HD, HD), F32)],
        compiler_params=_params(("arbitrary",), VMEM_BIG),
    )(p4, logits, gnorm)


def _mid_fwd(ya, yb, p4, x, wa, wb, wo, g_pm, g_pf, tt):
    t = x.shape[0]

    def body(ya_ref, yb_ref, gt_ref, x_ref, wa_ref, wb_ref, wo_ref, gpm_ref, gpf_ref,
             za_ref, zb_ref, mix_ref, m2_ref, x1_ref, h2_ref):
        za = _dot(ya_ref[...], wa_ref[...])
        zb = _dot(yb_ref[...], wb_ref[...])
        mix = _sig(gt_ref[:, 0:D]) * za + _sig(gt_ref[:, D:2 * D]) * zb
        mixb = mix.astype(BF16)
        m2 = _dot(mixb, wo_ref[...])
        _, n2 = _rms_stats(m2)
        x1 = x_ref[...] + n2 * gpm_ref[...]
        _, n1 = _rms_stats(x1)
        za_ref[...] = za.astype(BF16)
        zb_ref[...] = zb.astype(BF16)
        mix_ref[...] = mixb
        m2_ref[...] = m2
        x1_ref[...] = x1
        h2_ref[...] = (n1 * gpf_ref[...]).astype(BF16)

    row = lambda dt: jax.ShapeDtypeStruct((t, D), dt)
    tile = pl.BlockSpec((tt, D), lambda i: (i, 0))
    wsp = pl.BlockSpec((D, D), lambda i: (0, 0))
    vec = pl.BlockSpec((1, D), lambda i: (0, 0))
    return pl.pallas_call(
        body, name="mid_fwd",
        out_shape=(row(BF16), row(BF16), row(BF16), row(F32), row(F32), row(BF16)),
        grid=(t // tt,),
        in_specs=[tile, tile, pl.BlockSpec((None, tt, 2 * D), lambda i: (SLOT_G, i, 0)), tile,
                  wsp, wsp, wsp, vec, vec],
        out_specs=(tile,) * 6,
        compiler_params=_params(("parallel",), VMEM_BIG),
    )(ya, yb, p4, x, wa, wb, wo, g_pm, g_pf)


def _up_act_fwd(h2, w_up4, cfw, cfb, tm):
    t = h2.shape[0]
    ns = SH_UP

    def body(a_ref, ah_ref, wg_ref, wv_ref, cwg_ref, cwv_ref, cbg_ref, cbv_ref,
             pg_ref, pv_ref, y_ref, uv_ref, gl_ref, dgl_ref):
        i = pl.program_id(1)
        rows = jnp.concatenate([ah_ref[...], a_ref[...]], axis=0)
        ups = []
        for w_ref, cw_ref, cb_ref, pre_ref in ((wg_ref, cwg_ref, cbg_ref, pg_ref), (wv_ref, cwv_ref, cbv_ref, pv_ref)):
            pre = _dot(rows, w_ref[...])
            pre_ref[...] = pre[16:, :]
            xe = jnp.concatenate([jnp.where(i > 0, pre[8:16, :], 0.0), pre[16:, :]], axis=0)
            up = (cb_ref[...] + cw_ref[2:3, :] * xe + cw_ref[1:2, :] * pltpu.roll(xe, 1, 0)
                  + cw_ref[0:1, :] * pltpu.roll(xe, 2, 0))
            ups.append(up[8:, :])
        gl, dgl = _gelu_and_grad(ups[0])
        y_ref[...] = (gl * ups[1]).astype(BF16)
        uv_ref[...] = ups[1].astype(BF16)
        gl_ref[...] = gl.astype(BF16)
        dgl_ref[...] = dgl.astype(BF16)

    hb = tm // 16
    tile = pl.BlockSpec((tm, ns), lambda p, i: (i, p))
    return pl.pallas_call(
        body, name="up_act_fwd",
        out_shape=(jax.ShapeDtypeStruct((t, DFF), F32),) * 2 + (jax.ShapeDtypeStruct((t, DFF), BF16),) * 4,
        grid=(2, t // tm),
        in_specs=[pl.BlockSpec((tm, D), lambda p, i: (i, 0)),
                  pl.BlockSpec((16, D), lambda p, i: (jnp.maximum(i * hb - 1, 0), 0)),
                  pl.BlockSpec((None, D, ns), lambda p, i: (p, 0, 0)),
                  pl.BlockSpec((None, D, ns), lambda p, i: (p + 2, 0, 0)),
                  pl.BlockSpec((3, ns), lambda p, i: (0, p)), pl.BlockSpec((3, ns), lambda p, i: (0, p + 2)),
                  pl.BlockSpec((1, ns), lambda p, i: (0, p)), pl.BlockSpec((1, ns), lambda p, i: (0, p + 2))],
        out_specs=(tile,) * 6,
        compiler_params=_params(("parallel", "parallel"), VMEM_BIG),
    )(h2, h2, w_up4, w_up4, cfw, cfw, cfb, cfb)


def _down_loss(y, wdn, x1, tgt, g_post, tt):
    t = x1.shape[0]

    def body(y_ref, w_ref, x1_ref, t_ref, g_ref, dx2_ref, dm3_ref, lossv_ref, dg_ref):
        i = pl.program_id(0)
        m3 = _dot(y_ref[...], w_ref[...])
        r, n3 = _rms_stats(m3)
        g = g_ref[...]
        e = x1_ref[...] + n3 * g - t_ref[...]
        dx2 = e * (1.0 / D)
        dx2_ref[...] = dx2
        dm3_ref[...] = _rms_bwd(dx2 * g, n3, r).astype(BF16)
        lv = jnp.sum(e * e, axis=0, keepdims=True)
        dgv = jnp.sum(dx2 * n3, axis=0, keepdims=True)

        @pl.when(i == 0)
        def _():
            lossv_ref[...] = lv
            dg_ref[...] = dgv

        @pl.when(i > 0)
        def _():
            lossv_ref[...] += lv
            dg_ref[...] += dgv

    tile = pl.BlockSpec((tt, D), lambda i: (i, 0))
    vec = pl.BlockSpec((1, D), lambda i: (0, 0))
    return pl.pallas_call(
        body, name="down_loss",
        out_shape=(jax.ShapeDtypeStruct((t, D), F32), jax.ShapeDtypeStruct((t, D), BF16),
                   jax.ShapeDtypeStruct((1, D), F32), jax.ShapeDtypeStruct((1, D), F32)),
        grid=(t // tt,),
        in_specs=[pl.BlockSpec((tt, DFF), lambda i: (i, 0)), pl.BlockSpec((DFF, D), lambda i: (0, 0)),
                  tile, tile, vec],
        out_specs=(tile, tile, vec, vec),
        compiler_params=_params(("arbitrary",), VMEM_BIG),
    )(y, wdn, x1, tgt, g_post)


def _ffn_act_bwd(dy, pre_g, pre_v, uv, gl, dgl, cfw, tt):
    t = dy.shape[0]
    nt = t // tt

    def body(dy_ref, dyn_ref, pg_ref, pv_ref, uv_ref, uvn_ref, gl_ref, gln_ref, dgl_ref, dgln_ref, cw_ref,
             du_ref, dcw_ref, dcb_ref):
        i = pl.program_id(0)
        n = tt + 8
        next_live = jnp.where(i < nt - 1, 1.0, 0.0)
        ext = lambda ref, nref: jnp.concatenate([ref[...].astype(F32), nref[...].astype(F32)[0:8, :]], axis=0)
        dy = jnp.concatenate([dy_ref[...].astype(F32), dyn_ref[...].astype(F32)[0:8, :] * next_live], axis=0)
        ds = (dy * ext(uv_ref, uvn_ref) * ext(dgl_ref, dgln_ref), dy * ext(gl_ref, gln_ref))
        dcw_parts, dcb_parts = [], []
        for hh, c0 in enumerate((0, DFF)):
            cs = slice(c0, c0 + DFF)
            dd = ds[hh]
            d1 = pltpu.roll(dd, n - 1, 0)
            d2 = pltpu.roll(dd, n - 2, 0)
            du_ref[:, cs] = (cw_ref[2:3, cs] * dd + cw_ref[1:2, cs] * d1 + cw_ref[0:1, cs] * d2)[0:tt, :].astype(BF16)
            x = (pg_ref, pv_ref)[hh][...]
            dcw_parts.append(jnp.concatenate(
                [jnp.sum(dk[0:tt, :] * x, axis=0, keepdims=True) for dk in (d2, d1, dd)], axis=0))
            dcb_parts.append(jnp.sum(dd[0:tt, :], axis=0, keepdims=True))
        dcw = jnp.concatenate(dcw_parts, axis=1)
        dcb = jnp.concatenate(dcb_parts, axis=1)

        @pl.when(i == 0)
        def _():
            dcw_ref[...] = dcw
            dcb_ref[...] = dcb

        @pl.when(i > 0)
        def _():
            dcw_ref[...] += dcw
            dcb_ref[...] += dcb

    half = pl.BlockSpec((tt, DFF), lambda i: (i, 0))
    half_next = pl.BlockSpec((16, DFF), lambda i: (jnp.minimum((i + 1) * (tt // 16), t // 16 - 1), 0))
    return pl.pallas_call(
        body, name="ffn_act_bwd",
        out_shape=(jax.ShapeDtypeStruct((t, DUP), BF16), jax.ShapeDtypeStruct((3, DUP), F32),
                   jax.ShapeDtypeStruct((1, DUP), F32)),
        grid=(nt,),
        in_specs=[half, half_next, half, half,
                  half, half_next, half, half_next, half, half_next,
                  pl.BlockSpec((3, DUP), lambda i: (0, 0))],
        out_specs=(pl.BlockSpec((tt, DUP), lambda i: (i, 0)), pl.BlockSpec((3, DUP), lambda i: (0, 0)),
                   pl.BlockSpec((1, DUP), lambda i: (0, 0))),
        compiler_params=_params(("arbitrary",), VMEM_BIG),
    )(dy, dy, pre_g, pre_v, uv, uv, gl, gl, dgl, dgl, cfw)


def _mid_bwd(dh2, dx2, x1, m2, za, zb, p4, wa, wb, wo, g_pm, g_pf, tt):
    t = x1.shape[0]

    def body(dh2_ref, dx2_ref, x1_ref, m2_ref, za_ref, zb_ref, gt_ref, wa_ref, wb_ref, wo_ref, gpm_ref, gpf_ref,
             dx1_ref, dm2_ref, dza_ref, dzb_ref, dya_ref, dyb_ref, dp_ref, dgpm_ref, dgpf_ref):
        i = pl.program_id(0)
        r1, n1 = _rms_stats(x1_ref[...])
        dh2 = dh2_ref[...]
        dx1 = dx2_ref[...] + _rms_bwd(dh2 * gpf_ref[...], n1, r1)
        r2, n2 = _rms_stats(m2_ref[...])
        dm2 = _rms_bwd(dx1 * gpm_ref[...], n2, r2).astype(BF16)
        dmix = _dot_nt(dm2, wo_ref[...])
        sa = _sig(gt_ref[:, 0:D])
        sb = _sig(gt_ref[:, D:2 * D])
        dza = (dmix * sa).astype(BF16)
        dzb = (dmix * sb).astype(BF16)
        dp_ref[:, 0:D] = (dmix * za_ref[...].astype(F32) * sa * (1.0 - sa)).astype(BF16)
        dp_ref[:, D:2 * D] = (dmix * zb_ref[...].astype(F32) * sb * (1.0 - sb)).astype(BF16)
        dx1_ref[...] = dx1
        dm2_ref[...] = dm2
        dza_ref[...] = dza
        dzb_ref[...] = dzb
        dya_ref[...] = _dot_nt(dza, wa_ref[...])
        dyb_ref[...] = _dot_nt(dzb, wb_ref[...])
        dgpf = jnp.sum(dh2 * n1, axis=0, keepdims=True)
        dgpm = jnp.sum(dx1 * n2, axis=0, keepdims=True)

        @pl.when(i == 0)
        def _():
            dgpf_ref[...] = dgpf
            dgpm_ref[...] = dgpm

        @pl.when(i > 0)
        def _():
            dgpf_ref[...] += dgpf
            dgpm_ref[...] += dgpm

    row = lambda dt: jax.ShapeDtypeStruct((t, D), dt)
    tile = pl.BlockSpec((tt, D), lambda i: (i, 0))
    wsp = pl.BlockSpec((D, D), lambda i: (0, 0))
    vec = pl.BlockSpec((1, D), lambda i: (0, 0))
    gates = pl.BlockSpec((None, tt, 2 * D), lambda i: (SLOT_G, i, 0))
    return pl.pallas_call(
        body, name="mid_bwd",
        out_shape=(row(F32), row(BF16), row(BF16), row(BF16), row(F32), row(F32),
                   jax.ShapeDtypeStruct((NCHIP, t, 2 * D), BF16),
                   jax.ShapeDtypeStruct((1, D), F32), jax.ShapeDtypeStruct((1, D), F32)),
        grid=(t // tt,),
        in_specs=[tile, tile, tile, tile, tile, tile, gates, wsp, wsp, wsp, vec, vec],
        out_specs=(tile, tile, tile, tile, tile, tile, gates, vec, vec),
        compiler_params=_params(("arbitrary",), VMEM_BIG),
    )(dh2, dx2, x1, m2, za, zb, p4, wa, wb, wo, g_pm, g_pf)


def _hgrn2_bwd(p4, o_all, ss, dyb, dp4, logits, gnorm, p_early, tt):
    t = p4.shape[1]
    nt = t // tt
    nc = tt // CH
    ne = len(p_early)

    def body(p_ref, o_ref, ss_ref, dyb_ref, dp_in, lg_ref, gn_ref, *rest):
        del dp_in
        pe = rest[:ne]
        dp_ref, dlb_ref, dgn_ref = rest[ne:ne + 3]
        qe = rest[ne + 3:2 * ne + 3]
        dst, ssem, rsem = rest[2 * ne + 3:]
        i = pl.program_id(0)
        sends, arrive = _exchange_copies(qe, [(lambda chip, r=r: r.at[chip]) for r in pe], ssem, rsem)

        @pl.when(i == 0)
        def _():
            dst[...] = jnp.zeros((NH, HD, HD), F32)
            for cp in sends:
                cp.start()

        low = _chunk_masks(tt)
        lb = _lb_of(lg_ref)
        heads = [slice(h * HD, (h + 1) * HD) for h in range(NH)]
        sg, sgn, f, k, g, eg, eng, egu, qt, kt, kd = _hg_head_fwd(p_ref[0, :, 0:D], p_ref[0, :, D:2 * D], lb)
        qtb, ktb, kdb, vb = qt.astype(BF16), kt.astype(BF16), kd.astype(BF16), p_ref[1, :, 0:D].astype(BF16)
        decs = [jnp.exp(g[c * CH + CH - 1:c * CH + CH, :]) for c in range(nc)]
        og = p_ref[1, :, D:2 * D]
        so = _sig(og)
        dyb = dyb_ref[...]
        dob = dyb * (og * so)
        rn = [_rms_stats(o_ref[:, hs]) for hs in heads]
        r_all = jnp.concatenate([jnp.broadcast_to(r, (tt, HD)) for r, _ in rn], axis=1)
        n_all = jnp.concatenate([n for _, n in rn], axis=1)
        gd = dob * gn_ref[...]
        proj = jnp.concatenate(
            [jnp.broadcast_to(jnp.mean(gd[:, hs] * n_all[:, hs], axis=-1, keepdims=True), (tt, HD)) for hs in heads],
            axis=1)
        dob_ = (r_all * (gd - n_all * proj)).astype(BF16)
        dog = dyb * (n_all * gn_ref[...]) * (so * (1.0 + og * (1.0 - so)))
        dgn = jnp.sum(dob * n_all, axis=0, keepdims=True)
        dv_in, dqt_in, dkt_h = [], [], []
        for hs in heads:
            att = jnp.where(low, _dot_nt(qtb[:, hs], ktb[:, hs]), 0.0).astype(BF16)
            d_att = jnp.where(low, _dot_nt(dob_[:, hs], vb[:, hs]), 0.0).astype(BF16)
            dv_in.append(_dot_tn(att, dob_[:, hs]))
            dqt_in.append(_dot(d_att, ktb[:, hs]))
            dkt_h.append(_dot_tn(d_att, qtb[:, hs]))
        ds_t = [dst[h] for h in range(NH)]
        dv_p = [[None] * NH for _ in range(nc)]
        dqt_p = [[None] * NH for _ in range(nc)]
        dkd_p = [[None] * NH for _ in range(nc)]
        dgl_p = [[None] * NH for _ in range(nc)]
        for c in reversed(range(nc)):
            sl = slice(c * CH, (c + 1) * CH)
            for h, hs in enumerate(heads):
                s_prev = ss_ref[c, h]
                ds_bf = ds_t[h].astype(BF16)
                dec = decs[c][:, hs]
                dv_p[c][h] = dv_in[h][sl] + _dot_nt(kdb[sl, hs], ds_bf)
                dqt_p[c][h] = dqt_in[h][sl] + _dot(dob_[sl, hs], s_prev)
                dkd_p[c][h] = _dot(vb[sl, hs], ds_bf)
                ddec = jnp.sum(s_prev.astype(F32) * ds_t[h], axis=0, keepdims=True)
                dgl_p[c][h] = jnp.broadcast_to(ddec * dec, (CH, HD))
                ds_t[h] = ds_t[h] * dec + _dot_tn(dob_[sl, hs], qtb[sl, hs])
        for h in range(NH):
            dst[h] = ds_t[h]
        whole = lambda parts: jnp.concatenate([jnp.concatenate(row, axis=1) for row in parts], axis=0)
        dv, dqt, dkd, dgl = whole(dv_p), whole(dqt_p), whole(dkd_p), whole(dgl_p)
        dkt = jnp.concatenate(dkt_h, axis=1)
        dq = dqt * eg
        dk = dkt * eng + dkd * egu
        dg = dqt * qt - dkt * kt
        dgu = dkd * kd
        dlogf = _chunk_revcumsum(dg - dgu) + _chunk_total(dgu) + dgl
        common = sgn * (dlogf / f - dk)
        dfz = (1.0 - lb) * sg * common
        dlb = jnp.sum(common, axis=0, keepdims=True)
        dp_ref[0, :, 0:D] = dq.astype(BF16)
        dp_ref[0, :, D:2 * D] = dfz.astype(BF16)
        dp_ref[1, :, 0:D] = dv.astype(BF16)
        dp_ref[1, :, D:2 * D] = dog.astype(BF16)

        @pl.when(i == 0)
        def _():
            dlb_ref[0:1, :] = dlb
            dgn_ref[...] = dgn

        @pl.when(i > 0)
        def _():
            dlb_ref[0:1, :] += dlb
            dgn_ref[...] += dgn

        @pl.when(i == nt - 1)
        def _():
            d0 = dlb_ref[0:1, :] * lb * (1.0 - lb)
            dlb_ref[0:1, :] = d0
            dlb_ref[1:2, :] = -d0
            for cp in arrive:
                cp.wait_recv()
            for cp in sends:
                cp.wait_send()

    rev = lambda i: nt - 1 - i
    vec = pl.BlockSpec((1, D), lambda i: (0, 0))
    any_spec = pl.BlockSpec(memory_space=pl.ANY)
    outs = pl.pallas_call(
        body, name="hgrn2_bwd",
        out_shape=(jax.ShapeDtypeStruct(dp4.shape, BF16), jax.ShapeDtypeStruct((2, D), F32),
                   jax.ShapeDtypeStruct((1, D), F32)) + tuple(jax.ShapeDtypeStruct(a.shape, BF16) for a in p_early),
        grid=(nt,),
        in_specs=[pl.BlockSpec((2, tt, 2 * D), lambda i: (0, rev(i), 0)),
                  pl.BlockSpec((tt, D), lambda i: (rev(i), 0)),
                  pl.BlockSpec((nc, NH, HD, HD), lambda i: (rev(i), 0, 0, 0)),
                  pl.BlockSpec((tt, D), lambda i: (rev(i), 0)),
                  any_spec,
                  pl.BlockSpec((2, D), lambda i: (0, 0)), vec] + [any_spec] * ne,
        out_specs=(pl.BlockSpec((2, tt, 2 * D), lambda i: (0, rev(i), 0)),
                   pl.BlockSpec((2, D), lambda i: (0, 0)), vec) + (any_spec,) * ne,
        scratch_shapes=[pltpu.VMEM((NH, HD, HD), F32), pltpu.SemaphoreType.DMA((3 * ne,)),
                        pltpu.SemaphoreType.DMA((3 * ne,))],
        input_output_aliases={4: 0},
        compiler_params=pltpu.CompilerParams(dimension_semantics=("arbitrary",), vmem_limit_bytes=VMEM_BIG,
                                             has_side_effects=True),
    )(p4, o_all, ss, dyb, dp4, logits, gnorm, *p_early)
    return outs[0], outs[1], outs[2], list(outs[3:])


def _mixer_a_bwd(p4, hseq, dya, dp4, cw, cb, wa, wx, ba, bx, lam, tt):
    t = p4.shape[1]
    nt = t // tt
    steps = tt.bit_length() - 1

    def body(p_ref, ph_ref, h_ref, hh_ref, dya_ref, dp_in, cw_ref, cb_ref, wa_ref, wx_ref, ba_ref, bx_ref, lam_ref,
             dp_ref, dcw_ref, dcb_ref, dwa_ref, dwx_ref, dba_ref, dbx_ref, dlam_ref,
             dnext, dhc, afc):
        del dp_in
        i = pl.program_id(0)
        first_tile = i == nt - 1

        @pl.when(i == 0)
        def _():
            dnext[...] = jnp.zeros((8, D), F32)
            dhc[...] = jnp.zeros((1, D), F32)
            afc[...] = jnp.zeros((1, D), F32)

        xa = p_ref[:, 0:D]
        ga = p_ref[:, D:2 * D]
        xe = jnp.concatenate([jnp.where(first_tile, 0.0, ph_ref[:, 0:D]), xa], axis=0)
        xs = [xe[8:, :]] + [pltpu.roll(xe, s, 0)[8:, :] for s in (1, 2, 3)]
        xc = cb_ref[...] + sum(cw_ref[3 - s:4 - s, :] * xs[s] for s in range(4))
        lam = lam_ref[...]
        xcb, r, ig, sp, a, mult = _lru_gates(xc, wa_ref, wx_ref, ba_ref[...], bx_ref[...], lam)
        h = h_ref[...]
        gl, dgl = _gelu_and_grad(ga)
        dya = dya_ref[...]
        dga = dya * h * dgl
        rows = lax.broadcasted_iota(jnp.int32, (tt, 1), 0)
        a_next = jnp.where(rows == tt - 1, afc[...], pltpu.roll(a, tt - 1, 0))
        dh = _scan_up(a_next, dya * gl, dhc[...])
        dhc[...] = dh[0:1, :]
        afc[...] = a[0:1, :]
        h_prev = jnp.where(rows == 0, jnp.where(first_tile, 0.0, hh_ref[7:8, :]), pltpu.roll(h, 1, 0))
        da = dh * h_prev
        dmult = dh * ig * xc
        di = dh * mult * xc
        dlog_a = da * a - dmult * a * a / mult
        dr = dlog_a * (-LRU_C * sp)
        dsp = jnp.sum(dlog_a * (-LRU_C * r), axis=0, keepdims=True)
        dra = dr * r * (1.0 - r)
        dix = di * ig * (1.0 - ig)
        drab = dra.astype(BF16)
        dixb = dix.astype(BF16)
        dxc_lin = []
        dwa_new = []
        dwx_new = []
        for n in range(NH):
            cs = slice(n * HD, (n + 1) * HD)
            dxc_lin.append(_dot_nt(drab[:, cs], wa_ref[n]) + _dot_nt(dixb[:, cs], wx_ref[n]))
            dwa_new.append(_dot_tn(xcb[:, cs], drab[:, cs]))
            dwx_new.append(_dot_tn(xcb[:, cs], dixb[:, cs]))
        dxc = dh * mult * ig + jnp.concatenate(dxc_lin, axis=1)
        de = jnp.concatenate([dxc, dnext[...]], axis=0)
        dxa = (cw_ref[3:4, :] * de
               + sum(cw_ref[3 - s:4 - s, :] * pltpu.roll(de, tt + 8 - s, 0) for s in (1, 2, 3)))[0:tt, :]
        dnext[...] = dxc[0:8, :]
        dp_ref[:, 0:D] = dxa.astype(BF16)
        dp_ref[:, D:2 * D] = dga.astype(BF16)
        dcw = jnp.concatenate(
            [jnp.sum(dxc * xs[3 - k], axis=0, keepdims=True) for k in range(4)], axis=0)
        dcb = jnp.sum(dxc, axis=0, keepdims=True)
        dba = jnp.sum(dra, axis=0, keepdims=True)
        dbx = jnp.sum(dix, axis=0, keepdims=True)
        dlam = dsp * (-_sig(-lam))

        @pl.when(i == 0)
        def _():
            dcw_ref[...] = dcw
            dcb_ref[...] = dcb
            dba_ref[...] = dba
            dbx_ref[...] = dbx
            dlam_ref[...] = dlam
            for n in range(NH):
                dwa_ref[n] = dwa_new[n]
                dwx_ref[n] = dwx_new[n]

        @pl.when(i > 0)
        def _():
            dcw_ref[...] += dcw
            dcb_ref[...] += dcb
            dba_ref[...] += dba
            dbx_ref[...] += dbx
            dlam_ref[...] += dlam
            for n in range(NH):
                dwa_ref[n] += dwa_new[n]
                dwx_ref[n] += dwx_new[n]

    rev = lambda i: nt - 1 - i
    hb = tt // 8
    full = lambda shape: pl.BlockSpec(shape, lambda i: (0,) * len(shape))
    vecs = jax.ShapeDtypeStruct((1, D), F32)
    blk = jax.ShapeDtypeStruct((NH, HD, HD), F32)
    return pl.pallas_call(
        body, name="mixer_a_bwd",
        out_shape=(jax.ShapeDtypeStruct(dp4.shape, BF16), jax.ShapeDtypeStruct((4, D), F32), vecs, blk, blk,
                   vecs, vecs, vecs),
        grid=(nt,),
        in_specs=[pl.BlockSpec((None, tt, 2 * D), lambda i: (SLOT_A, rev(i), 0)),
                  pl.BlockSpec((None, 8, 2 * D), lambda i: (SLOT_A, jnp.maximum(rev(i) * hb - 1, 0), 0)),
                  pl.BlockSpec((tt, D), lambda i: (rev(i), 0)),
                  pl.BlockSpec((8, D), lambda i: (jnp.maximum(rev(i) * hb - 1, 0), 0)),
                  pl.BlockSpec((tt, D), lambda i: (rev(i), 0)),
                  pl.BlockSpec(memory_space=pl.ANY),
                  full((4, D)), full((1, D)), full((NH, HD, HD)), full((NH, HD, HD)),
                  full((1, D)), full((1, D)), full((1, D))],
        out_specs=(pl.BlockSpec((None, tt, 2 * D), lambda i: (SLOT_A, rev(i), 0)),
                   full((4, D)), full((1, D)), full((NH, HD, HD)), full((NH, HD, HD)),
                   full((1, D)), full((1, D)), full((1, D))),
        scratch_shapes=[pltpu.VMEM((8, D), F32), pltpu.VMEM((1, D), F32), pltpu.VMEM((1, D), F32)],
        input_output_aliases={5: 0},
        compiler_params=_params(("arbitrary",), VMEM_BIG),
    )(p4, p4, hseq, hseq, dya, dp4, cw, cb, wa, wx, ba, bx, lam)


def _local_step(x, tgt, w_in, stk_rest, conv_a_w, conv_f_w, small, cidx):
    t = x.shape[0]
    tt = min(256, t)
    tm = min(1024, t)
    tk = min(2048, t)
    wa_bf = small["lru_wa"].astype(BF16)
    wx_bf = small["lru_wx"].astype(BF16)

    h1 = _norm_fwd(x, small["norm_pre_mix"], tt)
    p4, stk_rest = _mm_in_gather(h1, w_in, stk_rest, REST, tm)
    w = dict(zip(REST, _gather_forward(stk_rest, REST)))
    w["w_in"] = w_in
    w_br_a = w["w_branch_a"].reshape(D, D)
    w_br_b = w["w_branch_b"].reshape(D, D)
    w_out = w["w_out"].reshape(D, D)
    w_down = w["w_down"].reshape(DFF, D)
    ya, hseq = _mixer_a_fwd(p4, conv_a_w, small["conv_a_b"], wa_bf, wx_bf, small["lru_ba"], small["lru_bx"],
                            small["lru_lambda"], tt)
    yb, o_all, ss = _hgrn2_fwd(p4, small["hg_lb_logits"], small["hg_norm_g"], tt)
    za, zb, mixb, m2, x1, h2 = _mid_fwd(ya, yb, p4, x, w_br_a, w_br_b, w_out, small["norm_post_mix"],
                                        small["norm_pre_ffn"], min(512, t))
    pre_g, pre_v, y, uv, gl, dgl = _up_act_fwd(h2, w["w_up"], conv_f_w, small["conv_f_b"], tt)
    dx2, dm3, lossv, d_norm_post_ffn = _down_loss(y, w_down, x1, tgt, small["norm_post_ffn"], min(512, t))

    d_w_down = _mm_tn(y, dm3, DFF // 2, D, tk, "mm_dw_down")
    dy = _mm_nt(dm3, w_down, BF16, tm, "mm_dy")
    dup_pre, d_conv_f_w, d_conv_f_b = _ffn_act_bwd(dy, pre_g, pre_v, uv, gl, dgl, conv_f_w, tt)
    d_w_up = _mm_tn(h2, dup_pre, D, SH_UP, tk, "mm_dw_up", stacked_out=True)
    dh2 = _mm_nt_sharded(dup_pre, w["w_up"], tm, "mm_dh2")
    dx1, dm2, dza, dzb, dya, dyb, dp4, d_norm_post_mix, d_norm_pre_ffn = _mid_bwd(
        dh2, dx2, x1, m2, za, zb, p4, w_br_a, w_br_b, w_out, small["norm_post_mix"], small["norm_pre_ffn"], tt)
    d_w_out = _mm_tn(mixb, dm2, D, D, tm, "mm_dw_out")
    d_w_br_a = _mm_tn(ya, dza, D, D, tm, "mm_dw_bra")
    d_w_br_b = _mm_tn(yb, dzb, D, D, tm, "mm_dw_brb")
    early = {"w_branch_a": d_w_br_a.reshape(NCHIP, SH_BR, D), "w_branch_b": d_w_br_b.reshape(NCHIP, SH_BR, D),
             "w_out": d_w_out.reshape(NCHIP, SH_BR, D), "w_up": d_w_up, "w_down": d_w_down.reshape(NCHIP, SH_DN, D)}
    rb, _ = _reduce_stage1(early, REST, (), "reduce_d2d_in_early")
    p_rest = [_sum_own_half(early[n], rb[n], cidx, "sum_half_" + n) for n in REST]
    dp4, d_lb, d_hg_norm_g, q_rest = _hgrn2_bwd(p4, o_all, ss, dyb, dp4, small["hg_lb_logits"], small["hg_norm_g"],
                                                p_rest, tt)
    dp4, d_conv_a_w, d_conv_a_b, d_lru_wa, d_lru_wx, d_lru_ba, d_lru_bx, d_lru_lambda = _mixer_a_bwd(
        p4, hseq, dya, dp4, conv_a_w, small["conv_a_b"], wa_bf, wx_bf, small["lru_ba"], small["lru_bx"],
        small["lru_lambda"], tt)
    d_w_in = _mm_tn(h1, dp4, D, SH_IN, tk, "mm_dw_in", stacked_slot_fn=_slot_of_chip, stacked_out=True)
    rb, _ = _reduce_stage1({"w_in": d_w_in}, ("w_in",), (), "reduce_d2d_in_w_in")
    p_w_in = _sum_own_half(d_w_in, rb["w_in"], cidx, "sum_half_w_in")
    grad_x, q_w_in, d_norm_pre_mix = _mm_dh1_exchange(dp4, w_in, p_w_in, x, dx1, small["norm_pre_mix"], tm)

    smalls = {
        "norm_pre_mix": d_norm_pre_mix, "conv_a_b": d_conv_a_b, "lru_ba": d_lru_ba, "lru_bx": d_lru_bx,
        "lru_lambda": d_lru_lambda, "hg_lb_logits": d_lb, "hg_norm_g": d_hg_norm_g, "norm_post_mix": d_norm_post_mix,
        "norm_pre_ffn": d_norm_pre_ffn, "norm_post_ffn": d_norm_post_ffn, "lossv": lossv,
        "conv_a_w": d_conv_a_w, "lru_wa": d_lru_wa, "lru_wx": d_lru_wx,
        "conv_f_b": d_conv_f_b, "conv_f_w": d_conv_f_w,
    }
    p_big = dict(zip(REST, p_rest), w_in=p_w_in)
    q_big = dict(zip(REST, q_rest), w_in=q_w_in)
    return grad_x, p_big, q_big, smalls


BIG = ("w_in", "w_branch_a", "w_branch_b", "w_out", "w_up", "w_down")
BIG_SHAPE = {"w_in": (D, SH_IN), "w_branch_a": (SH_BR, D), "w_branch_b": (SH_BR, D), "w_out": (SH_BR, D),
             "w_up": (D, SH_UP), "w_down": (SH_DN, D)}
NBIG = len(BIG)
REST = BIG[1:]
VEC_ROWS = (("norm_pre_mix", 0, 1), ("conv_a_b", 1, 1), ("lru_ba", 2, 1), ("lru_bx", 3, 1), ("lru_lambda", 4, 1),
            ("hg_lb_logits", 5, 2), ("hg_norm_g", 7, 1), ("norm_post_mix", 8, 1), ("norm_pre_ffn", 9, 1),
            ("norm_post_ffn", 10, 1))
ROW_LOSS = 11
ROW_CONV_A = 12
S1_ROWS = 16
S2_ROWS = 8


def _place():
    x, y, c = lax.axis_index("x"), lax.axis_index("y"), lax.axis_index("c")
    chips = [(1 - x, y), (x, 1 - y), (1 - x, 1 - y)]
    return x, y, c, 2 * x + y, chips


def _remote(src, dst, ssem, rsem, dev):
    return pltpu.make_async_remote_copy(src_ref=src, dst_ref=dst, send_sem=ssem, recv_sem=rsem,
                                        device_id=dev, device_id_type=MESH)


def _hbm_call(body, name, ins, out_shapes, n_sems, aliases=None):
    any_spec = pl.BlockSpec(memory_space=pl.ANY)
    return pl.pallas_call(
        body, name=name, out_shape=tuple(out_shapes),
        in_specs=[any_spec] * len(ins), out_specs=tuple([any_spec] * len(out_shapes)),
        scratch_shapes=[pltpu.SemaphoreType.DMA((n,)) for n in n_sems],
        input_output_aliases=aliases or {},
        compiler_params=pltpu.CompilerParams(has_side_effects=True),
    )(*ins)


def _gather_copies(stk, names, ssem, rsem, fssem=None, frsem=None):
    x, y, c, j, chips = _place()
    sends, arrive, fwds, farrive = [], [], [], []
    for w, n in enumerate(names):
        hw = BIG_SHAPE[n][0] // 2
        mine = stk[w].at[j, pl.ds(c * hw, hw), :]
        for k, (cx, cy) in enumerate(chips):
            i = 3 * w + k
            got = stk[w].at[2 * cx + cy, pl.ds(c * hw, hw), :]
            other = stk[w].at[2 * cx + cy, pl.ds((1 - c) * hw, hw), :]
            sends.append(_remote(mine, mine, ssem.at[i], rsem.at[i], (cx, cy, c)))
            arrive.append(_remote(got, got, ssem.at[i], rsem.at[i], (cx, cy, c)))
            if fssem is not None:
                fwds.append(_remote(got, got, fssem.at[i], frsem.at[i], (x, y, 1 - c)))
                farrive.append(_remote(other, other, fssem.at[i], frsem.at[i], (x, y, 1 - c)))
    return sends, arrive, fwds, farrive


def _gather_first(stacked_w_in, conv_a_s, conv_f_s):
    ins = [stacked_w_in, conv_a_s, conv_f_s]
    out_shapes = [jax.ShapeDtypeStruct(stacked_w_in.shape, stacked_w_in.dtype)]
    out_shapes += [jax.ShapeDtypeStruct((NCHIP,) + a.shape, a.dtype) for a in (conv_a_s, conv_f_s)]

    def body(w_in, ca_src, cf_src, w_out, ca_dst, cf_dst, ssem, rsem, fssem, frsem, csend, crecv, lsem):
        del w_in
        x, y, c, j, chips = _place()
        conv = ((ca_src, ca_dst), (cf_src, cf_dst))
        locs = [pltpu.make_async_copy(src, dst.at[j], lsem.at[i]) for i, (src, dst) in enumerate(conv)]
        csends = [_remote(src, dst.at[j], csend.at[3 * i + k], crecv.at[3 * i + k], (cx, cy, c))
                  for i, (src, dst) in enumerate(conv) for k, (cx, cy) in enumerate(chips)]
        sends, arrive, fwds, farrive = _gather_copies([w_out], ("w_in",), ssem, rsem, fssem, frsem)
        for cp in locs + sends + csends:
            cp.start()
        for got, fwd in zip(arrive, fwds):
            got.wait_recv()
            fwd.start()
        for i, (_, dst) in enumerate(conv):
            for k, (cx, cy) in enumerate(chips):
                got = dst.at[2 * cx + cy]
                _remote(got, got, csend.at[3 * i + k], crecv.at[3 * i + k], (cx, cy, c)).wait_recv()
        for cp in farrive:
            cp.wait_recv()
        for cp in sends + fwds + csends:
            cp.wait_send()
        for cp in locs:
            cp.wait()

    return _hbm_call(body, "gather_first", ins, out_shapes, (3, 3, 3, 3, 6, 6, 2), aliases={0: 0})


def _gather_forward(stk, names):
    nw = len(names)

    def body(*refs):
        dst = refs[nw:2 * nw]
        ssem, rsem, fssem, frsem = refs[2 * nw:]
        _, _, fwds, farrive = _gather_copies(dst, names, ssem, rsem, fssem, frsem)
        for cp in fwds:
            cp.start()
        for cp in farrive:
            cp.wait_recv()
        for cp in fwds:
            cp.wait_send()

    out_shapes = [jax.ShapeDtypeStruct(a.shape, a.dtype) for a in stk]
    return _hbm_call(body, "gather_forward", stk, out_shapes, (3 * nw,) * 4, aliases={w: w for w in range(nw)})


def _exchange_copies(dst, pieces, ssem, rsem):
    x, y, c, j, chips = _place()
    sends, arrive = [], []
    for w in range(len(dst)):
        for k, (cx, cy) in enumerate(chips):
            i = 3 * w + k
            sends.append(_remote(pieces[w](2 * cx + cy), dst[w].at[j], ssem.at[i], rsem.at[i], (cx, cy, c)))
            got = dst[w].at[2 * cx + cy]
            arrive.append(_remote(got, got, ssem.at[i], rsem.at[i], (cx, cy, c)))
    return sends, arrive


def _reduce_stage1(big_g, names, smalls, name):
    nb = len(names)
    ins = [big_g[n] for n in names] + list(smalls)
    n_in = len(ins)
    halves = [BIG_SHAPE[n][0] // 2 for n in names]
    out_shapes = [jax.ShapeDtypeStruct((NCHIP, halves[w], BIG_SHAPE[n][1]), big_g[n].dtype)
                  for w, n in enumerate(names)]
    out_shapes += [jax.ShapeDtypeStruct(a.shape, F32) for a in smalls]

    def body(*refs):
        src, dst = refs[:n_in], refs[n_in:2 * n_in]
        ssem, rsem = refs[2 * n_in:]
        x, y, c, _, _ = _place()
        cps = []
        for w in range(n_in):
            s_ = src[w].at[:, pl.ds((1 - c) * halves[w], halves[w]), :] if w < nb else src[w]
            cp = _remote(s_, dst[w], ssem.at[w], rsem.at[w], (x, y, 1 - c))
            cp.start()
            cps.append(cp)
        for cp in cps:
            cp.wait()

    outs = _hbm_call(body, name, ins, out_shapes, (n_in, n_in))
    return dict(zip(names, outs[:nb])), outs[nb:]


def _reduce_stage2(ps1, ps2, ps3):
    ins = [ps1, ps2, ps3]
    h1, h2, h3 = S1_ROWS // 2, DUP // 2, D
    out_shapes = [jax.ShapeDtypeStruct((NCHIP, h1, D), F32), jax.ShapeDtypeStruct((NCHIP, S2_ROWS, h2), F32),
                  jax.ShapeDtypeStruct((NCHIP, h3, HD), F32)]

    def body(*refs):
        src, dst = refs[:3], refs[3:6]
        ssem, rsem = refs[6:]
        c = lax.axis_index("c")
        pieces = [lambda chip: src[0].at[pl.ds(c * h1, h1), :],
                  lambda chip: src[1].at[:, pl.ds(c * h2, h2)],
                  lambda chip: src[2].at[pl.ds(c * h3, h3), :]]
        sends, arrive = _exchange_copies(dst, pieces, ssem, rsem)
        for cp in sends:
            cp.start()
        for cp in arrive:
            cp.wait_recv()
        for cp in sends:
            cp.wait_send()

    return _hbm_call(body, "reduce_ici_small", ins, out_shapes, (9, 9))


def _reduce_stage3(f_big, fs1, fs2, fs3):
    ins = [f_big[n] for n in BIG] + [fs1, fs2, fs3]
    n_in = len(ins)
    halves = [BIG_SHAPE[n][0] // 2 for n in BIG]
    h1, h2, h3 = S1_ROWS // 2, DUP // 2, D
    out_shapes = [jax.ShapeDtypeStruct(BIG_SHAPE[n], F32) for n in BIG]
    out_shapes += [jax.ShapeDtypeStruct((S1_ROWS, D), F32), jax.ShapeDtypeStruct((S2_ROWS, DUP), F32),
                   jax.ShapeDtypeStruct((2 * D, HD), F32)]

    def body(*refs):
        dst = refs[n_in:2 * n_in]
        ssem, rsem = refs[2 * n_in:]
        x, y, c, _, _ = _place()

        def place(w, which):
            if w < NBIG:
                return dst[w].at[pl.ds(which * halves[w], halves[w]), :]
            if w == NBIG:
                return dst[w].at[pl.ds(which * h1, h1), :]
            if w == NBIG + 1:
                return dst[w].at[:, pl.ds(which * h2, h2)]
            return dst[w].at[pl.ds(which * h3, h3), :]

        cps = [_remote(place(w, c), place(w, c), ssem.at[w], rsem.at[w], (x, y, 1 - c)) for w in range(n_in)]
        for cp in cps:
            cp.start()
        for w in range(n_in):
            got = place(w, 1 - c)
            _remote(got, got, ssem.at[w], rsem.at[w], (x, y, 1 - c)).wait_recv()
        for cp in cps:
            cp.wait_send()

    outs = _hbm_call(body, "reduce_d2d_out", ins, out_shapes, (n_in, n_in), aliases={w: w for w in range(n_in)})
    return dict(zip(BIG, outs[:NBIG])), outs[NBIG], outs[NBIG + 1], outs[NBIG + 2]


def _row_tile(rows):
    for tr in (128, 176, 64, 16, 8):
        if rows % tr == 0:
            return tr
    return rows


def _sum_own_half(g, rb, cidx, name):
    s, rows, cols = g.shape
    half = rows // 2
    tr = _row_tile(half)
    nb = half // tr

    def body(c_ref, g_ref, r_ref, o_ref):
        del c_ref
        o_ref[...] = (g_ref[...].astype(F32) + r_ref[...].astype(F32)).astype(BF16)

    grid_spec = pltpu.PrefetchScalarGridSpec(
        num_scalar_prefetch=1, grid=(s, nb),
        in_specs=[pl.BlockSpec((None, tr, cols), lambda k, i, c: (k, c[0] * nb + i, 0)),
                  pl.BlockSpec((None, tr, cols), lambda k, i, c: (k, i, 0))],
        out_specs=pl.BlockSpec((None, tr, cols), lambda k, i, c: (k, i, 0)))
    return pl.pallas_call(
        body, name=name, grid_spec=grid_spec, out_shape=jax.ShapeDtypeStruct((s, half, cols), BF16),
        compiler_params=_params(("parallel", "parallel")),
    )(cidx, g, rb)


def _sum_chips(q, p, jc, name, by_cols=False):
    s, rows, cols = q.shape
    tr = _row_tile(rows)
    nb = rows // tr
    stacked = p.ndim == 3

    def body(jc_ref, q_ref, p_ref, o_ref):
        j = jc_ref[0]
        own = p_ref[...].astype(F32)
        acc = None
        for k in range(NCHIP):
            term = jnp.where(j == k, own, q_ref[k].astype(F32))
            acc = term if acc is None else acc + term
        o_ref[...] = acc

    if by_cols:
        half_spec = pl.BlockSpec((tr, cols), lambda i, jc_ref: (i, jc_ref[1]))
        out_shape = jax.ShapeDtypeStruct((rows, 2 * cols), F32)
    else:
        half_spec = pl.BlockSpec((tr, cols), lambda i, jc_ref: (jc_ref[1] * nb + i, 0))
        out_shape = jax.ShapeDtypeStruct((2 * rows, cols), F32)
    p_spec = pl.BlockSpec((None, tr, cols), lambda i, jc_ref: (jc_ref[0], i, 0)) if stacked else half_spec
    grid_spec = pltpu.PrefetchScalarGridSpec(
        num_scalar_prefetch=1, grid=(nb,),
        in_specs=[pl.BlockSpec((s, tr, cols), lambda i, jc_ref: (0, i, 0)), p_spec],
        out_specs=half_spec)
    return pl.pallas_call(
        body, name=name, grid_spec=grid_spec, out_shape=out_shape,
        compiler_params=_params(("parallel",)),
    )(jc, q, p)


def _place_shard(w, jc, name):
    rows, cols = w.shape
    tr = _row_tile(rows)

    def body(jc_ref, w_ref, o_ref):
        del jc_ref
        o_ref[...] = w_ref[...].astype(BF16)

    grid_spec = pltpu.PrefetchScalarGridSpec(
        num_scalar_prefetch=1, grid=(rows // tr,),
        in_specs=[pl.BlockSpec((tr, cols), lambda i, jc_ref: (i, 0))],
        out_specs=pl.BlockSpec((None, tr, cols), lambda i, jc_ref: (jc_ref[0], i, 0)))
    return pl.pallas_call(
        body, name=name, grid_spec=grid_spec, out_shape=jax.ShapeDtypeStruct((NCHIP, rows, cols), BF16),
        compiler_params=_params(("parallel",)),
    )(jc, w)


def _add(a, b, name):
    def body(a_ref, b_ref, o_ref):
        o_ref[...] = a_ref[...] + b_ref[...]

    return pl.pallas_call(body, name=name, out_shape=jax.ShapeDtypeStruct(a.shape, F32))(a, b)


def _pack_small(sm):
    vec_in = [sm[n] for n, _, _ in VEC_ROWS]
    nv = len(vec_in)

    def body(*refs):
        ins, lossv, dcw, dcfb, dcfw, s1, s2 = refs[:nv], refs[nv], refs[nv + 1], refs[nv + 2], refs[nv + 3], \
            refs[nv + 4], refs[nv + 5]
        for ref, (_, r0, nr) in zip(ins, VEC_ROWS):
            s1[r0:r0 + nr, :] = ref[...]
        s1[ROW_LOSS:ROW_LOSS + 1, :] = lossv[...]
        s1[ROW_CONV_A:ROW_CONV_A + 4, :] = dcw[...]
        s2[0:1, :] = dcfb[...]
        s2[1:4, :] = dcfw[...]
        s2[4:8, :] = jnp.zeros((4, DUP), F32)

    return pl.pallas_call(
        body, name="pack_small",
        out_shape=(jax.ShapeDtypeStruct((S1_ROWS, D), F32), jax.ShapeDtypeStruct((S2_ROWS, DUP), F32)),
    )(*vec_in, sm["lossv"], sm["conv_a_w"], sm["conv_f_b"], sm["conv_f_w"])


def _adam_math(w, g, m, v):
    m = ADAM_B1 * m + (1.0 - ADAM_B1) * g
    v = ADAM_B2 * v + (1.0 - ADAM_B2) * (g * g)
    m_hat = m / (1.0 - ADAM_B1 ** ADAM_STEP)
    v_hat = v / (1.0 - ADAM_B2 ** ADAM_STEP)
    delta = -ADAM_LR * (m_hat / (jnp.sqrt(v_hat) + ADAM_EPS) + ADAM_WD * w)
    return delta, m, v


def _adam(w, g, m, v, name):
    rows, cols = w.shape
    tr = _row_tile(rows)

    def body(w_ref, g_ref, m_ref, v_ref, d_ref, mo_ref, vo_ref):
        d_ref[...], mo_ref[...], vo_ref[...] = _adam_math(w_ref[...], g_ref[...], m_ref[...], v_ref[...])

    spec = pl.BlockSpec((tr, cols), lambda i: (i, 0))
    return pl.pallas_call(
        body, name=name, out_shape=(jax.ShapeDtypeStruct(w.shape, F32),) * 3, grid=(rows // tr,),
        in_specs=[spec] * 4, out_specs=(spec,) * 3,
        compiler_params=_params(("parallel",)),
    )(w, g, m, v)


def _adam_small(gs1, gs2, gs3, w, m, v):
    names = [n for n, _, _ in VEC_ROWS] + ["conv_f_b", "lru_wa", "lru_wx"]
    nn = len(names)

    def grad_of(i, g1, g2, g3):
        if i < len(VEC_ROWS):
            _, r0, nr = VEC_ROWS[i]
            return g1[r0:r0 + nr, :]
        if names[i] == "conv_f_b":
            return g2[0:1, :]
        return g3[0] if names[i] == "lru_wa" else g3[1]

    def body(*refs):
        g1, g2, g3 = refs[0], refs[1], refs[2]
        ws, ms, vs = refs[3:3 + nn], refs[3 + nn:3 + 2 * nn], refs[3 + 2 * nn:3 + 3 * nn]
        outs = refs[3 + 3 * nn:]
        for i in range(nn):
            d, mn, vn = _adam_math(ws[i][...], grad_of(i, g1, g2, g3), ms[i][...], vs[i][...])
            outs[i][...] = d
            outs[nn + i][...] = mn
            outs[2 * nn + i][...] = vn

    shapes = [jax.ShapeDtypeStruct(w[n].shape, F32) for n in names]
    outs = pl.pallas_call(body, name="adam_small", out_shape=tuple(shapes * 3))(
        gs1, gs2, gs3, *[w[n] for n in names], *[m[n] for n in names], *[v[n] for n in names])
    return {n: (outs[i], outs[nn + i], outs[2 * nn + i]) for i, n in enumerate(names)}


WEIGHTS = ("norm_pre_mix", "w_in", "conv_a_w", "conv_a_b", "lru_wa", "lru_ba", "lru_wx", "lru_bx", "lru_lambda",
           "hg_lb_logits", "hg_norm_g", "w_branch_a", "w_branch_b", "w_out", "norm_post_mix", "norm_pre_ffn",
           "w_up", "conv_f_w", "conv_f_b", "w_down", "norm_post_ffn")
NW = len(WEIGHTS)


def kernel(x, norm_pre_mix, w_in, conv_a_w, conv_a_b, lru_wa, lru_ba, lru_wx, lru_bx, lru_lambda, hg_lb_logits, hg_norm_g, w_branch_a, w_branch_b, w_out, norm_post_mix, norm_pre_ffn, w_up, conv_f_w, conv_f_b, w_down, norm_post_ffn, loss_target, m_norm_pre_mix, m_w_in, m_conv_a_w, m_conv_a_b, m_lru_wa, m_lru_ba, m_lru_wx, m_lru_bx, m_lru_lambda, m_hg_lb_logits, m_hg_norm_g, m_w_branch_a, m_w_branch_b, m_w_out, m_norm_post_mix, m_norm_pre_ffn, m_w_up, m_conv_f_w, m_conv_f_b, m_w_down, m_norm_post_ffn, v_norm_pre_mix, v_w_in, v_conv_a_w, v_conv_a_b, v_lru_wa, v_lru_ba, v_lru_wx, v_lru_bx, v_lru_lambda, v_hg_lb_logits, v_hg_norm_g, v_w_branch_a, v_w_branch_b, v_w_out, v_norm_post_mix, v_norm_pre_ffn, v_w_up, v_conv_f_w, v_conv_f_b, v_w_down, v_norm_post_ffn):
    rest = (norm_pre_mix, w_in, conv_a_w, conv_a_b, lru_wa, lru_ba, lru_wx, lru_bx, lru_lambda, hg_lb_logits, hg_norm_g, w_branch_a, w_branch_b, w_out, norm_post_mix, norm_pre_ffn, w_up, conv_f_w, conv_f_b, w_down, norm_post_ffn, loss_target, m_norm_pre_mix, m_w_in, m_conv_a_w, m_conv_a_b, m_lru_wa, m_lru_ba, m_lru_wx, m_lru_bx, m_lru_lambda, m_hg_lb_logits, m_hg_norm_g, m_w_branch_a, m_w_branch_b, m_w_out, m_norm_post_mix, m_norm_pre_ffn, m_w_up, m_conv_f_w, m_conv_f_b, m_w_down, m_norm_post_ffn, v_norm_pre_mix, v_w_in, v_conv_a_w, v_conv_a_b, v_lru_wa, v_lru_ba, v_lru_wx, v_lru_bx, v_lru_lambda, v_hg_lb_logits, v_hg_norm_g, v_w_branch_a, v_w_branch_b, v_w_out, v_norm_post_mix, v_norm_pre_ffn, v_w_up, v_conv_f_w, v_conv_f_b, v_w_down, v_norm_post_ffn)
    w_in_args = dict(zip(WEIGHTS, rest[:NW]))
    loss_target = rest[NW]
    m_args = dict(zip(WEIGHTS, rest[NW + 1:2 * NW + 1]))
    v_args = dict(zip(WEIGHTS, rest[2 * NW + 1:3 * NW + 1]))
    shape_of = {n: w_in_args[n].shape for n in WEIGHTS}

    def two_d(n, a):
        if n in BIG:
            return a.reshape(BIG_SHAPE[n])
        if n in ("lru_wa", "lru_wx"):
            return a.reshape(NH, HD, HD)
        return a.reshape(a.shape[-2:])

    w2 = {n: two_d(n, w_in_args[n]) for n in WEIGHTS}
    m2 = {n: two_d(n, m_args[n]) for n in WEIGHTS}
    v2 = {n: two_d(n, v_args[n]) for n in WEIGHTS}

    cidx = lax.axis_index("c").astype(jnp.int32).reshape(1)
    jchip = 2 * lax.axis_index("x") + lax.axis_index("y")

    jc = jnp.stack([jchip, lax.axis_index("c")]).astype(jnp.int32)

    shards = {n: _place_shard(w2[n], jc, "place_" + n) for n in BIG}
    conv_a_s = jnp.pad(w2["conv_a_w"], ((0, 4), (0, 0)))
    conv_f_s = jnp.pad(w2["conv_f_w"], ((0, 5), (0, 0)))
    w_in_full, conv_a_g, conv_f_g = _gather_first(shards["w_in"], conv_a_s, conv_f_s)
    conv_a_full = jnp.transpose(conv_a_g, (1, 0, 2)).reshape(8, D)[0:4]
    conv_f_full = jnp.transpose(conv_f_g, (1, 0, 2)).reshape(8, DUP)[0:3]
    small = {n: w2[n] for n in WEIGHTS if n not in BIG and n not in ("conv_a_w", "conv_f_w")}

    grad_x, p_big, q_big, sm_g = _local_step(
        x[0], loss_target[0], w_in_full, [shards[n] for n in REST], conv_a_full, conv_f_full, small, cidx)

    s1, s2 = _pack_small(sm_g)
    s3 = jnp.concatenate([sm_g["lru_wa"].reshape(D, HD), sm_g["lru_wx"].reshape(D, HD)], axis=0)
    _, (rs1, rs2, rs3) = _reduce_stage1({}, (), (s1, s2, s3), "reduce_d2d_in_small")
    ps1, ps2, ps3 = _add(s1, rs1, "add_s1"), _add(s2, rs2, "add_s2"), _add(s3, rs3, "add_s3")
    qs1, qs2, qs3 = _reduce_stage2(ps1, ps2, ps3)
    f_big = {n: _sum_chips(q_big[n], p_big[n], jc, "sum_chips_" + n) for n in BIG}
    fs1 = _sum_chips(qs1, ps1, jc, "sum_chips_s1")
    fs2 = _sum_chips(qs2, ps2, jc, "sum_chips_s2", by_cols=True)
    fs3 = _sum_chips(qs3, ps3, jc, "sum_chips_s3")
    g_big, gs1, gs2, gs3 = _reduce_stage3(f_big, fs1, fs2, fs3)

    res = {}
    for n in BIG:
        d, mn, vn = _adam(w2[n], g_big[n], m2[n], v2[n], "adam_" + n)
        res[n] = (g_big[n], d, mn, vn)
    small_res = _adam_small(gs1, gs2, gs3.reshape(2, NH, HD, HD), w2, m2, v2)
    for n, r0, nr in VEC_ROWS:
        res[n] = (gs1[r0:r0 + nr],) + small_res[n]
    res["conv_f_b"] = (gs2[0:1],) + small_res["conv_f_b"]
    res["lru_wa"] = (gs3[0:D].reshape(NH, HD, HD),) + small_res["lru_wa"]
    res["lru_wx"] = (gs3[D:2 * D].reshape(NH, HD, HD),) + small_res["lru_wx"]
    g_ca = lax.dynamic_slice_in_dim(gs1[ROW_CONV_A:ROW_CONV_A + 4], jchip * (D // NCHIP), D // NCHIP, axis=1)
    g_cf = lax.dynamic_slice_in_dim(gs2[1:4], jchip * SH_UP, SH_UP, axis=1)
    res["conv_a_w"] = (g_ca,) + _adam(w2["conv_a_w"], g_ca, m2["conv_a_w"], v2["conv_a_w"], "adam_conv_a_w")
    res["conv_f_w"] = (g_cf,) + _adam(w2["conv_f_w"], g_cf, m2["conv_f_w"], v2["conv_f_w"], "adam_conv_f_w")

    loss = (0.5 / D) * jnp.sum(gs1[ROW_LOSS])
    out = [loss, grad_x.reshape(x.shape)]
    for part in range(4):
        out += [res[n][part].reshape(shape_of[n]) for n in WEIGHTS]
    return tuple(out)
```

```python
import functools

import jax
import jax.numpy as jnp
from jax import lax
from jax.experimental import pallas as pl
from jax.experimental.pallas import tpu as pltpu

F32 = jnp.float32
BF16 = jnp.bfloat16

D = 1024
NH = 8
HD = 128
CH = 32
DFF = 2816
DUP = 2 * DFF
NCHIP = 4
SH_IN = 2 * D
SH_UP = DUP // NCHIP
SH_DN = DFF // NCHIP
SH_BR = D // NCHIP
EPS = 1e-6
LRU_C = 8.0
ADAM_LR = 0.001
ADAM_B1 = 0.9
ADAM_B2 = 0.999
ADAM_EPS = 1e-08
ADAM_WD = 0.01
ADAM_STEP = 10
VMEM_BIG = 56 * 1024 * 1024
MESH = pl.DeviceIdType.MESH

SLOT_A, SLOT_B, SLOT_C, SLOT_G = 2, 0, 1, 3


def _slot_of_chip(s):
    return jnp.where(s == 3, 3, (s + 2) % 3)


def _params(sem, vmem=None):
    return pltpu.CompilerParams(dimension_semantics=sem, vmem_limit_bytes=vmem)


_GC = 0.7978845608028654
_GA = 0.044715


def _gelu(x):
    return 0.5 * x * (1.0 + jnp.tanh(_GC * (x + _GA * x * x * x)))


def _gelu_and_grad(x):
    x2 = x * x
    th = jnp.tanh(_GC * x * (1.0 + _GA * x2))
    g = 0.5 * x * (1.0 + th)
    dg = 0.5 * (1.0 + th) + 0.5 * x * (1.0 - th * th) * _GC * (1.0 + 3.0 * _GA * x2)
    return g, dg


def _sig(x):
    return jax.nn.sigmoid(x)


def _dot(a, b):
    return jnp.dot(a, b, preferred_element_type=F32)


def _dot_nt(a, b):
    return lax.dot_general(a, b, (((1,), (1,)), ((), ())), preferred_element_type=F32)


def _dot_tn(a, b):
    return lax.dot_general(a, b, (((0,), (0,)), ((), ())), preferred_element_type=F32)


def _chunk_cumsum(x):
    pos = lax.broadcasted_iota(jnp.int32, (x.shape[0], 1), 0) & (CH - 1)
    d = 1
    while d < CH:
        x = x + jnp.where(pos >= d, pltpu.roll(x, d, 0), 0.0)
        d *= 2
    return x


def _chunk_revcumsum(x):
    n = x.shape[0]
    pos = lax.broadcasted_iota(jnp.int32, (n, 1), 0) & (CH - 1)
    d = 1
    while d < CH:
        x = x + jnp.where(pos < CH - d, pltpu.roll(x, n - d, 0), 0.0)
        d *= 2
    return x


def _chunk_last(x):
    n = x.shape[0]
    return jnp.concatenate(
        [jnp.broadcast_to(x[c * CH + CH - 1:c * CH + CH, :], (CH, x.shape[1])) for c in range(n // CH)], axis=0)


def _chunk_total(x):
    n = x.shape[0]
    return jnp.concatenate(
        [jnp.broadcast_to(jnp.sum(x[c * CH:(c + 1) * CH, :], axis=0, keepdims=True), (CH, x.shape[1]))
         for c in range(n // CH)], axis=0)


def _rms_stats(x):
    r = lax.rsqrt(jnp.mean(x * x, axis=-1, keepdims=True) + EPS)
    return r, x * r


def _rms_bwd(gd, n, r):
    return r * (gd - n * jnp.mean(gd * n, axis=-1, keepdims=True))


def _shift_rows(x, d, fill):
    rows = lax.broadcasted_iota(jnp.int32, (x.shape[0], 1), 0)
    return jnp.where(rows >= d, pltpu.roll(x, d, 0), fill)


def _scan_down(a, u, carry):
    n = a.shape[0]
    pos = lax.broadcasted_iota(jnp.int32, (n, 1), 0) & 7
    for d in (1, 2, 4):
        u = a * jnp.where(pos >= d, pltpu.roll(u, d, 0), 0.0) + u
        a = a * jnp.where(pos >= d, pltpu.roll(a, d, 0), 1.0)
    out = []
    for v in range(n // 8):
        h = a[v * 8:v * 8 + 8, :] * carry + u[v * 8:v * 8 + 8, :]
        carry = h[7:8, :]
        out.append(h)
    return jnp.concatenate(out, axis=0)


def _scan_up(b, g, carry):
    n = b.shape[0]
    pos = lax.broadcasted_iota(jnp.int32, (n, 1), 0) & 7
    for d in (1, 2, 4):
        g = g + b * jnp.where(pos < 8 - d, pltpu.roll(g, n - d, 0), 0.0)
        b = b * jnp.where(pos < 8 - d, pltpu.roll(b, n - d, 0), 1.0)
    out = [None] * (n // 8)
    for v in reversed(range(n // 8)):
        h = g[v * 8:v * 8 + 8, :] + b[v * 8:v * 8 + 8, :] * carry
        carry = h[0:1, :]
        out[v] = h
    return jnp.concatenate(out, axis=0)


def _shift_rows_up(x, d, fill):
    n = x.shape[0]
    rows = lax.broadcasted_iota(jnp.int32, (n, 1), 0)
    return jnp.where(rows < n - d, pltpu.roll(x, n - d, 0), fill)


def _mm_nn_sharded(a, b3, out_dtype, tm, name, slot_fn=None):
    m, k = a.shape
    s, _, ns = b3.shape

    def body(a_ref, b_ref, o_ref):
        o_ref[...] = _dot(a_ref[...], b_ref[...]).astype(out_dtype)

    if slot_fn is None:
        out_shape = jax.ShapeDtypeStruct((m, s * ns), out_dtype)
        out_spec = pl.BlockSpec((tm, ns), lambda j, i: (i, j))
    else:
        out_shape = jax.ShapeDtypeStruct((s, m, ns), out_dtype)
        out_spec = pl.BlockSpec((None, tm, ns), lambda j, i: (slot_fn(j), i, 0))
    return pl.pallas_call(
        body, name=name, out_shape=out_shape, grid=(s, m // tm),
        in_specs=[pl.BlockSpec((tm, k), lambda j, i: (i, 0)),
                  pl.BlockSpec((None, k, ns), lambda j, i: (j, 0, 0))],
        out_specs=out_spec,
        compiler_params=_params(("parallel", "parallel"), VMEM_BIG),
    )(a, b3)


def _mm_in_gather(a, w_in3, stk_rest, names_rest, tm):
    m, k = a.shape
    s, _, ns = w_in3.shape
    nr = len(stk_rest)
    mt = m // tm

    def body(a_ref, b_ref, *rest):
        o_ref = rest[nr]
        stk = rest[nr + 1:2 * nr + 1]
        ssem, rsem = rest[2 * nr + 1:]
        step = pl.program_id(0) * mt + pl.program_id(1)
        sends, arrive, _, _ = _gather_copies(stk, names_rest, ssem, rsem)

        @pl.when(step == 0)
        def _():
            for cp in sends:
                cp.start()

        o_ref[...] = _dot(a_ref[...], b_ref[...])

        @pl.when(step == s * mt - 1)
        def _():
            for cp in arrive:
                cp.wait_recv()
            for cp in sends:
                cp.wait_send()

    any_spec = pl.BlockSpec(memory_space=pl.ANY)
    outs = pl.pallas_call(
        body, name="mm_in",
        out_shape=(jax.ShapeDtypeStruct((s, m, ns), F32),) + tuple(jax.ShapeDtypeStruct(x.shape, x.dtype) for x in stk_rest),
        grid=(s, mt),
        in_specs=[pl.BlockSpec((tm, k), lambda j, i: (i, 0)), pl.BlockSpec((None, k, ns), lambda j, i: (j, 0, 0))]
        + [any_spec] * nr,
        out_specs=(pl.BlockSpec((None, tm, ns), lambda j, i: (_slot_of_chip(j), i, 0)),) + (any_spec,) * nr,
        scratch_shapes=[pltpu.SemaphoreType.DMA((3 * nr,)), pltpu.SemaphoreType.DMA((3 * nr,))],
        input_output_aliases={2 + w: 1 + w for w in range(nr)},
        compiler_params=pltpu.CompilerParams(dimension_semantics=("arbitrary", "arbitrary"),
                                             vmem_limit_bytes=VMEM_BIG, has_side_effects=True),
    )(a, w_in3, *stk_rest)
    return outs[0], list(outs[1:])


def _mm_dh1_exchange(dp4, w_in3, p_w_in, x, dx1, gain, tm):
    s, k, ns = w_in3.shape
    m = dp4.shape[1]
    mt = m // tm

    def body(a_ref, b_ref, p_ref, x_ref, dx1_ref, g_ref, o_ref, q_ref, dg_ref, ssem, rsem):
        i, j = pl.program_id(0), pl.program_id(1)
        sends, arrive = _exchange_copies([q_ref], [lambda chip: p_ref.at[chip]], ssem, rsem)

        @pl.when((i == 0) & (j == 0))
        def _():
            for cp in sends:
                cp.start()

        @pl.when(j == 0)
        def _():
            o_ref[...] = _dot_nt(a_ref[...], b_ref[...])

        @pl.when(j > 0)
        def _():
            o_ref[...] += _dot_nt(a_ref[...], b_ref[...])

        @pl.when(j == s - 1)
        def _():
            dh = o_ref[...]
            r, n = _rms_stats(x_ref[...])
            o_ref[...] = dx1_ref[...] + _rms_bwd(dh * g_ref[...], n, r)
            dgv = jnp.sum(dh * n, axis=0, keepdims=True)

            @pl.when(i == 0)
            def _():
                dg_ref[...] = dgv

            @pl.when(i > 0)
            def _():
                dg_ref[...] += dgv

        @pl.when((i == mt - 1) & (j == s - 1))
        def _():
            for cp in arrive:
                cp.wait_recv()
            for cp in sends:
                cp.wait_send()

    any_spec = pl.BlockSpec(memory_space=pl.ANY)
    row_tile = pl.BlockSpec((tm, k), lambda i, j: (i, 0))
    vec = pl.BlockSpec((1, k), lambda i, j: (0, 0))
    return pl.pallas_call(
        body, name="mm_dh1",
        out_shape=(jax.ShapeDtypeStruct((m, k), F32), jax.ShapeDtypeStruct(p_w_in.shape, BF16),
                   jax.ShapeDtypeStruct((1, k), F32)),
        grid=(mt, s),
        in_specs=[pl.BlockSpec((None, tm, ns), lambda i, j: (_slot_of_chip(j), i, 0)),
                  pl.BlockSpec((None, k, ns), lambda i, j: (j, 0, 0)), any_spec, row_tile, row_tile, vec],
        out_specs=(row_tile, any_spec, vec),
        scratch_shapes=[pltpu.SemaphoreType.DMA((3,)), pltpu.SemaphoreType.DMA((3,))],
        compiler_params=pltpu.CompilerParams(dimension_semantics=("arbitrary", "arbitrary"),
                                             vmem_limit_bytes=VMEM_BIG, has_side_effects=True),
    )(dp4, w_in3, p_w_in, x, dx1, gain)


def _mm_nt_sharded(a, b3, tm, name, stacked_slot_fn=None):
    s, k, ns = b3.shape
    m = a.shape[1] if stacked_slot_fn is not None else a.shape[0]

    def body(a_ref, b_ref, o_ref):
        j = pl.program_id(1)
        @pl.when(j == 0)
        def _():
            o_ref[...] = _dot_nt(a_ref[...], b_ref[...])

        @pl.when(j > 0)
        def _():
            o_ref[...] += _dot_nt(a_ref[...], b_ref[...])

    if stacked_slot_fn is None:
        a_spec = pl.BlockSpec((tm, ns), lambda i, j: (i, j))
    else:
        a_spec = pl.BlockSpec((None, tm, ns), lambda i, j: (stacked_slot_fn(j), i, 0))
    return pl.pallas_call(
        body, name=name, out_shape=jax.ShapeDtypeStruct((m, k), F32), grid=(m // tm, s),
        in_specs=[a_spec, pl.BlockSpec((None, k, ns), lambda i, j: (j, 0, 0))],
        out_specs=pl.BlockSpec((tm, k), lambda i, j: (i, 0)),
        compiler_params=_params(("parallel", "arbitrary"), VMEM_BIG),
    )(a, b3)


def _mm_nt(a, b, out_dtype, tm, name):
    m, k = a.shape
    n = b.shape[0]

    def body(a_ref, b_ref, o_ref):
        o_ref[...] = _dot_nt(a_ref[...], b_ref[...]).astype(out_dtype)

    return pl.pallas_call(
        body, name=name, out_shape=jax.ShapeDtypeStruct((m, n), out_dtype), grid=(m // tm,),
        in_specs=[pl.BlockSpec((tm, k), lambda i: (i, 0)), pl.BlockSpec((n, k), lambda i: (0, 0))],
        out_specs=pl.BlockSpec((tm, n), lambda i: (i, 0)),
        compiler_params=_params(("parallel",), VMEM_BIG),
    )(a, b)


def _mm_tn(a, g, tkk, tn, tk, name, stacked_slot_fn=None, stacked_out=False):
    m, k = a.shape
    if stacked_slot_fn is not None:
        n = g.shape[0] * g.shape[2]
        g_spec = pl.BlockSpec((None, tk, tn), lambda kk, j, mm: (stacked_slot_fn(j), mm, 0))
    else:
        n = g.shape[1]
        g_spec = pl.BlockSpec((tk, tn), lambda kk, j, mm: (mm, j))
    steps = m // tk

    def body(a_ref, g_ref, o_ref, acc_ref):
        mm = pl.program_id(2)

        @pl.when(mm == 0)
        def _():
            acc_ref[...] = _dot_tn(a_ref[...], g_ref[...])

        @pl.when(mm > 0)
        def _():
            acc_ref[...] += _dot_tn(a_ref[...], g_ref[...])

        @pl.when(mm == steps - 1)
        def _():
            o_ref[...] = acc_ref[...].astype(BF16)

    if stacked_out:
        out_shape = jax.ShapeDtypeStruct((n // tn, k, tn), BF16)
        out_spec = pl.BlockSpec((None, tkk, tn), lambda kk, j, mm: (j, kk, 0))
    else:
        out_shape = jax.ShapeDtypeStruct((k, n), BF16)
        out_spec = pl.BlockSpec((tkk, tn), lambda kk, j, mm: (kk, j))
    return pl.pallas_call(
        body, name=name, out_shape=out_shape, grid=(k // tkk, n // tn, steps),
        in_specs=[pl.BlockSpec((tk, tkk), lambda kk, j, mm: (mm, kk)), g_spec],
        out_specs=out_spec,
        scratch_shapes=[pltpu.VMEM((tkk, tn), F32)],
        compiler_params=_params(("parallel", "parallel", "arbitrary"), VMEM_BIG),
    )(a, g)


def _norm_fwd(x, gain, tt):
    t = x.shape[0]

    def body(x_ref, g_ref, h_ref):
        _, n = _rms_stats(x_ref[...])
        h_ref[...] = (n * g_ref[...]).astype(BF16)

    return pl.pallas_call(
        body, name="norm_fwd", out_shape=jax.ShapeDtypeStruct((t, D), BF16), grid=(t // tt,),
        in_specs=[pl.BlockSpec((tt, D), lambda i: (i, 0)), pl.BlockSpec((1, D), lambda i: (0, 0))],
        out_specs=pl.BlockSpec((tt, D), lambda i: (i, 0)),
        compiler_params=_params(("parallel",)),
    )(x, gain)


def _lru_gates(xc, wa_ref, wx_ref, ba, bx, lam):
    xcb = xc.astype(BF16)
    ra = jnp.concatenate([_dot(xcb[:, n * HD:(n + 1) * HD], wa_ref[n]) for n in range(NH)], axis=1) + ba
    ix = jnp.concatenate([_dot(xcb[:, n * HD:(n + 1) * HD], wx_ref[n]) for n in range(NH)], axis=1) + bx
    r = _sig(ra)
    ig = _sig(ix)
    z = -lam
    sp = jnp.maximum(z, 0.0) + jnp.log1p(jnp.exp(-jnp.abs(z)))
    log_a = -LRU_C * r * sp
    a = jnp.exp(log_a)
    z2 = 2.0 * log_a
    series = -z2 * (1.0 + z2 * (0.5 + z2 * (1.0 / 6.0 + z2 * (1.0 / 24.0))))
    om = jnp.where(z2 > -0.02, series, 1.0 - jnp.exp(z2))
    mult = jnp.sqrt(om)
    return xcb, r, ig, sp, a, mult


def _mixer_a_fwd(p4, cw, cb, wa, wx, ba, bx, lam, tt):
    t = p4.shape[1]

    def body(p_ref, cw_ref, cb_ref, wa_ref, wx_ref, ba_ref, bx_ref, lam_ref, ya_ref, h_ref, sv_ref, halo, hc):
        i = pl.program_id(0)

        @pl.when(i == 0)
        def _():
            halo[...] = jnp.zeros((8, D), F32)
            hc[...] = jnp.zeros((1, D), F32)

        xa = p_ref[:, 0:D]
        ga = p_ref[:, D:2 * D]
        xe = jnp.concatenate([halo[...], xa], axis=0)
        xc = (cb_ref[...] + cw_ref[3:4, :] * xe
              + sum(cw_ref[3 - s:4 - s, :] * pltpu.roll(xe, s, 0) for s in (1, 2, 3)))[8:, :]
        halo[...] = xa[tt - 8:, :]
        _, r, ig, _, a, mult = _lru_gates(xc, wa_ref, wx_ref, ba_ref[...], bx_ref[...], lam_ref[...])
        u = mult * ig * xc
        h = _scan_down(a, u, hc[...])
        hc[...] = h[tt - 1:tt, :]
        h_ref[...] = h
        ya_ref[...] = (h * _gelu(ga)).astype(BF16)
        for idx, val in enumerate((xc, r, ig, a, mult)):
            sv_ref[idx] = val

    full = lambda shape: pl.BlockSpec(shape, lambda i: (0,) * len(shape))
    return pl.pallas_call(
        body, name="mixer_a_fwd",
        out_shape=(jax.ShapeDtypeStruct((t, D), BF16), jax.ShapeDtypeStruct((t, D), F32),
                   jax.ShapeDtypeStruct((5, t, D), F32)),
        grid=(t // tt,),
        in_specs=[pl.BlockSpec((None, tt, 2 * D), lambda i: (SLOT_A, i, 0)),
                  full((4, D)), full((1, D)), full((NH, HD, HD)), full((NH, HD, HD)),
                  full((1, D)), full((1, D)), full((1, D))],
        out_specs=(pl.BlockSpec((tt, D), lambda i: (i, 0)), pl.BlockSpec((tt, D), lambda i: (i, 0)),
                   pl.BlockSpec((5, tt, D), lambda i: (0, i, 0))),
        scratch_shapes=[pltpu.VMEM((8, D), F32), pltpu.VMEM((1, D), F32)],
        compiler_params=_params(("arbitrary",), VMEM_BIG),
    )(p4, cw, cb, wa, wx, ba, bx, lam)


def _chunk_masks(tt):
    row = lax.broadcasted_iota(jnp.int32, (tt, tt), 0)
    col = lax.broadcasted_iota(jnp.int32, (tt, tt), 1)
    same = jnp.right_shift(row, 5) == jnp.right_shift(col, 5)
    return same & (col <= row)


def _hg_head_fwd(q, fz, lbh, saved=None):
    sgn = _sig(-fz)
    k = (1.0 - lbh) * sgn
    if saved is None:
        sg = _sig(fz)
        f = lbh + (1.0 - lbh) * sg
        g = _chunk_cumsum(jnp.log(f))
    else:
        sg, g = saved
        f = lbh + (1.0 - lbh) * sg
    gu = _chunk_last(g) - g
    eg = jnp.exp(g)
    eng = jnp.exp(-g)
    egu = jnp.exp(gu)
    qt = q * eg
    kt = k * eng
    kd = k * egu
    return sg, sgn, f, k, g, eg, eng, egu, qt, kt, kd


def _lb_of(logits_ref):
    return _sig(logits_ref[0:1, :] - logits_ref[1:2, :])


def _hgrn2_fwd(p4, logits, gnorm, tt):
    t = p4.shape[1]
    nc = tt // CH

    def body(p_ref, lg_ref, gn_ref, yb_ref, o_ref, ss_ref, sv_ref, st):
        i = pl.program_id(0)

        @pl.when(i == 0)
        def _():
            st[...] = jnp.zeros((NH, HD, HD), F32)

        low = _chunk_masks(tt)
        lb = _lb_of(lg_ref)
        heads = [slice(h * HD, (h + 1) * HD) for h in range(NH)]
        sg, _, _, _, g, _, _, _, qt, kt, kd = _hg_head_fwd(p_ref[0, :, 0:D], p_ref[0, :, D:2 * D], lb)
        sv_ref[0] = sg
        sv_ref[1] = g
        qtb, ktb, kdb, vb = qt.astype(BF16), kt.astype(BF16), kd.astype(BF16), p_ref[1, :, 0:D].astype(BF16)
        decs = [jnp.exp(g[c * CH + CH - 1:c * CH + CH, :]) for c in range(nc)]
        o_in = []
        for hs in heads:
            att = jnp.where(low, _dot_nt(qtb[:, hs], ktb[:, hs]), 0.0)
            o_in.append(_dot(att.astype(BF16), vb[:, hs]))
        s_t = [st[h] for h in range(NH)]
        pieces = [[None] * nc for _ in range(NH)]
        for c in range(nc):
            sl = slice(c * CH, (c + 1) * CH)
            for h, hs in enumerate(heads):
                s_bf = s_t[h].astype(BF16)
                ss_ref[c, h] = s_bf
                pieces[h][c] = o_in[h][sl] + _dot_nt(qtb[sl, hs], s_bf)
                s_t[h] = s_t[h] * decs[c][:, hs] + _dot_tn(vb[sl, hs], kdb[sl, hs])
        for h, hs in enumerate(heads):
            st[h] = s_t[h]
            o = jnp.concatenate(pieces[h], axis=0)
            _, n = _rms_stats(o)
            og = p_ref[1, :, D + h * HD:D + (h + 1) * HD]
            o_ref[:, hs] = o
            yb_ref[:, hs] = (n * gn_ref[:, hs] * (og * _sig(og))).astype(BF16)

    return pl.pallas_call(
        body, name="hgrn2_fwd",
        out_shape=(jax.ShapeDtypeStruct((t, D), BF16), jax.ShapeDtypeStruct((t, D), F32),
                   jax.ShapeDtypeStruct((t // CH, NH, HD, HD), BF16), jax.ShapeDtypeStruct((2, t, D), F32)),
        grid=(t // tt,),
        in_specs=[pl.BlockSpec((2, tt, 2 * D), lambda i: (0, i, 0)),
                  pl.BlockSpec((2, D), lambda i: (0, 0)), pl.BlockSpec((1, D), lambda i: (0, 0))],
        out_specs=(pl.BlockSpec((tt, D), lambda i: (i, 0)), pl.BlockSpec((tt, D), lambda i: (i, 0)),
                   pl.BlockSpec((nc, NH, HD, HD), lambda i: (i, 0, 0, 0)),
                   pl.BlockSpec((2, tt, D), lambda i: (0, i, 0))),
        scratch_shapes=[pltpu.VMEM((NH, HD, HD), F32)],
        compiler_params=_params(("arbitrary",), VMEM_BIG),
    )(p4, logits, gnorm)


def _mid_fwd(ya, yb, p4, x, wa, wb, wo, g_pm, g_pf, tt):
    t = x.shape[0]

    def body(ya_ref, yb_ref, gt_ref, x_ref, wa_ref, wb_ref, wo_ref, gpm_ref, gpf_ref,
             za_ref, zb_ref, mix_ref, m2_ref, x1_ref, h2_ref):
        za = _dot(ya_ref[...], wa_ref[...])
        zb = _dot(yb_ref[...], wb_ref[...])
        mix = _sig(gt_ref[:, 0:D]) * za + _sig(gt_ref[:, D:2 * D]) * zb
        mixb = mix.astype(BF16)
        m2 = _dot(mixb, wo_ref[...])
        _, n2 = _rms_stats(m2)
        x1 = x_ref[...] + n2 * gpm_ref[...]
        _, n1 = _rms_stats(x1)
        za_ref[...] = za.astype(BF16)
        zb_ref[...] = zb.astype(BF16)
        mix_ref[...] = mixb
        m2_ref[...] = m2
        x1_ref[...] = x1
        h2_ref[...] = (n1 * gpf_ref[...]).astype(BF16)

    row = lambda dt: jax.ShapeDtypeStruct((t, D), dt)
    tile = pl.BlockSpec((tt, D), lambda i: (i, 0))
    wsp = pl.BlockSpec((D, D), lambda i: (0, 0))
    vec = pl.BlockSpec((1, D), lambda i: (0, 0))
    return pl.pallas_call(
        body, name="mid_fwd",
        out_shape=(row(BF16), row(BF16), row(BF16), row(F32), row(F32), row(BF16)),
        grid=(t // tt,),
        in_specs=[tile, tile, pl.BlockSpec((None, tt, 2 * D), lambda i: (SLOT_G, i, 0)), tile,
                  wsp, wsp, wsp, vec, vec],
        out_specs=(tile,) * 6,
        compiler_params=_params(("parallel",), VMEM_BIG),
    )(ya, yb, p4, x, wa, wb, wo, g_pm, g_pf)


def _up_act_fwd(h2, w_up4, cfw, cfb, tm):
    t = h2.shape[0]
    ns = SH_UP

    def body(a_ref, ah_ref, wg_ref, wv_ref, cwg_ref, cwv_ref, cbg_ref, cbv_ref,
             pg_ref, pv_ref, y_ref, uv_ref, gl_ref, dgl_ref):
        i = pl.program_id(1)
        rows = jnp.concatenate([ah_ref[...], a_ref[...]], axis=0)
        ups = []
        for w_ref, cw_ref, cb_ref, pre_ref in ((wg_ref, cwg_ref, cbg_ref, pg_ref), (wv_ref, cwv_ref, cbv_ref, pv_ref)):
            pre = _dot(rows, w_ref[...])
            pre_ref[...] = pre[16:, :]
            xe = jnp.concatenate([jnp.where(i > 0, pre[8:16, :], 0.0), pre[16:, :]], axis=0)
            up = (cb_ref[...] + cw_ref[2:3, :] * xe + cw_ref[1:2, :] * pltpu.roll(xe, 1, 0)
                  + cw_ref[0:1, :] * pltpu.roll(xe, 2, 0))
            ups.append(up[8:, :])
        gl, dgl = _gelu_and_grad(ups[0])
        y_ref[...] = (gl * ups[1]).astype(BF16)
        uv_ref[...] = ups[1].astype(BF16)
        gl_ref[...] = gl.astype(BF16)
        dgl_ref[...] = dgl.astype(BF16)

    hb = tm // 16
    tile = pl.BlockSpec((tm, ns), lambda p, i: (i, p))
    return pl.pallas_call(
        body, name="up_act_fwd",
        out_shape=(jax.ShapeDtypeStruct((t, DFF), F32),) * 2 + (jax.ShapeDtypeStruct((t, DFF), BF16),) * 4,
        grid=(2, t // tm),
        in_specs=[pl.BlockSpec((tm, D), lambda p, i: (i, 0)),
                  pl.BlockSpec((16, D), lambda p, i: (jnp.maximum(i * hb - 1, 0), 0)),
                  pl.BlockSpec((None, D, ns), lambda p, i: (p, 0, 0)),
                  pl.BlockSpec((None, D, ns), lambda p, i: (p + 2, 0, 0)),
                  pl.BlockSpec((3, ns), lambda p, i: (0, p)), pl.BlockSpec((3, ns), lambda p, i: (0, p + 2)),
                  pl.BlockSpec((1, ns), lambda p, i: (0, p)), pl.BlockSpec((1, ns), lambda p, i: (0, p + 2))],
        out_specs=(tile,) * 6,
        compiler_params=_params(("parallel", "parallel"), VMEM_BIG),
    )(h2, h2, w_up4, w_up4, cfw, cfw, cfb, cfb)


def _down_loss(y, wdn, x1, tgt, g_post, tt):
    t = x1.shape[0]

    def body(y_ref, w_ref, x1_ref, t_ref, g_ref, dx2_ref, dm3_ref, lossv_ref, dg_ref):
        i = pl.program_id(0)
        m3 = _dot(y_ref[...], w_ref[...])
        r, n3 = _rms_stats(m3)
        g = g_ref[...]
        e = x1_ref[...] + n3 * g - t_ref[...]
        dx2 = e * (1.0 / D)
        dx2_ref[...] = dx2
        dm3_ref[...] = _rms_bwd(dx2 * g, n3, r).astype(BF16)
        lv = jnp.sum(e * e, axis=0, keepdims=True)
        dgv = jnp.sum(dx2 * n3, axis=0, keepdims=True)

        @pl.when(i == 0)
        def _():
            lossv_ref[...] = lv
            dg_ref[...] = dgv

        @pl.when(i > 0)
        def _():
            lossv_ref[...] += lv
            dg_ref[...] += dgv

    tile = pl.BlockSpec((tt, D), lambda i: (i, 0))
    vec = pl.BlockSpec((1, D), lambda i: (0, 0))
    return pl.pallas_call(
        body, name="down_loss",
        out_shape=(jax.ShapeDtypeStruct((t, D), F32), jax.ShapeDtypeStruct((t, D), BF16),
                   jax.ShapeDtypeStruct((1, D), F32), jax.ShapeDtypeStruct((1, D), F32)),
        grid=(t // tt,),
        in_specs=[pl.BlockSpec((tt, DFF), lambda i: (i, 0)), pl.BlockSpec((DFF, D), lambda i: (0, 0)),
                  tile, tile, vec],
        out_specs=(tile, tile, vec, vec),
        compiler_params=_params(("arbitrary",), VMEM_BIG),
    )(y, wdn, x1, tgt, g_post)


def _ffn_act_bwd(dy, pre_g, pre_v, uv, gl, dgl, cfw, tt):
    t = dy.shape[0]
    nt = t // tt

    def body(dy_ref, dyn_ref, pg_ref, pv_ref, uv_ref, uvn_ref, gl_ref, gln_ref, dgl_ref, dgln_ref, cw_ref,
             du_ref, dcw_ref, dcb_ref):
        i = pl.program_id(0)
        n = tt + 8
        next_live = jnp.where(i < nt - 1, 1.0, 0.0)
        ext = lambda ref, nref: jnp.concatenate([ref[...].astype(F32), nref[...].astype(F32)[0:8, :]], axis=0)
        dy = jnp.concatenate([dy_ref[...].astype(F32), dyn_ref[...].astype(F32)[0:8, :] * next_live], axis=0)
        ds = (dy * ext(uv_ref, uvn_ref) * ext(dgl_ref, dgln_ref), dy * ext(gl_ref, gln_ref))
        dcw_parts, dcb_parts = [], []
        for hh, c0 in enumerate((0, DFF)):
            cs = slice(c0, c0 + DFF)
            dd = ds[hh]
            d1 = pltpu.roll(dd, n - 1, 0)
            d2 = pltpu.roll(dd, n - 2, 0)
            du_ref[:, cs] = (cw_ref[2:3, cs] * dd + cw_ref[1:2, cs] * d1 + cw_ref[0:1, cs] * d2)[0:tt, :].astype(BF16)
            x = (pg_ref, pv_ref)[hh][...]
            dcw_parts.append(jnp.concatenate(
                [jnp.sum(dk[0:tt, :] * x, axis=0, keepdims=True) for dk in (d2, d1, dd)], axis=0))
            dcb_parts.append(jnp.sum(dd[0:tt, :], axis=0, keepdims=True))
        dcw = jnp.concatenate(dcw_parts, axis=1)
        dcb = jnp.concatenate(dcb_parts, axis=1)

        @pl.when(i == 0)
        def _():
            dcw_ref[...] = dcw
            dcb_ref[...] = dcb

        @pl.when(i > 0)
        def _():
            dcw_ref[...] += dcw
            dcb_ref[...] += dcb

    half = pl.BlockSpec((tt, DFF), lambda i: (i, 0))
    half_next = pl.BlockSpec((16, DFF), lambda i: (jnp.minimum((i + 1) * (tt // 16), t // 16 - 1), 0))
    return pl.pallas_call(
        body, name="ffn_act_bwd",
        out_shape=(jax.ShapeDtypeStruct((t, DUP), BF16), jax.ShapeDtypeStruct((3, DUP), F32),
                   jax.ShapeDtypeStruct((1, DUP), F32)),
        grid=(nt,),
        in_specs=[half, half_next, half, half,
                  half, half_next, half, half_next, half, half_next,
                  pl.BlockSpec((3, DUP), lambda i: (0, 0))],
        out_specs=(pl.BlockSpec((tt, DUP), lambda i: (i, 0)), pl.BlockSpec((3, DUP), lambda i: (0, 0)),
                   pl.BlockSpec((1, DUP), lambda i: (0, 0))),
        compiler_params=_params(("arbitrary",), VMEM_BIG),
    )(dy, dy, pre_g, pre_v, uv, uv, gl, gl, dgl, dgl, cfw)


def _mid_bwd(dh2, dx2, x1, m2, za, zb, p4, wa, wb, wo, g_pm, g_pf, tt):
    t = x1.shape[0]

    def body(dh2_ref, dx2_ref, x1_ref, m2_ref, za_ref, zb_ref, gt_ref, wa_ref, wb_ref, wo_ref, gpm_ref, gpf_ref,
             dx1_ref, dm2_ref, dza_ref, dzb_ref, dya_ref, dyb_ref, dp_ref, dgpm_ref, dgpf_ref):
        i = pl.program_id(0)
        r1, n1 = _rms_stats(x1_ref[...])
        dh2 = dh2_ref[...]
        dx1 = dx2_ref[...] + _rms_bwd(dh2 * gpf_ref[...], n1, r1)
        r2, n2 = _rms_stats(m2_ref[...])
        dm2 = _rms_bwd(dx1 * gpm_ref[...], n2, r2).astype(BF16)
        dmix = _dot_nt(dm2, wo_ref[...])
        sa = _sig(gt_ref[:, 0:D])
        sb = _sig(gt_ref[:, D:2 * D])
        dza = (dmix * sa).astype(BF16)
        dzb = (dmix * sb).astype(BF16)
        dp_ref[:, 0:D] = (dmix * za_ref[...].astype(F32) * sa * (1.0 - sa)).astype(BF16)
        dp_ref[:, D:2 * D] = (dmix * zb_ref[...].astype(F32) * sb * (1.0 - sb)).astype(BF16)
        dx1_ref[...] = dx1
        dm2_ref[...] = dm2
        dza_ref[...] = dza
        dzb_ref[...] = dzb
        dya_ref[...] = _dot_nt(dza, wa_ref[...])
        dyb_ref[...] = _dot_nt(dzb, wb_ref[...])
        dgpf = jnp.sum(dh2 * n1, axis=0, keepdims=True)
        dgpm = jnp.sum(dx1 * n2, axis=0, keepdims=True)

        @pl.when(i == 0)
        def _():
            dgpf_ref[...] = dgpf
            dgpm_ref[...] = dgpm

        @pl.when(i > 0)
        def _():
            dgpf_ref[...] += dgpf
            dgpm_ref[...] += dgpm

    row = lambda dt: jax.ShapeDtypeStruct((t, D), dt)
    tile = pl.BlockSpec((tt, D), lambda i: (i, 0))
    wsp = pl.BlockSpec((D, D), lambda i: (0, 0))
    vec = pl.BlockSpec((1, D), lambda i: (0, 0))
    gates = pl.BlockSpec((None, tt, 2 * D), lambda i: (SLOT_G, i, 0))
    return pl.pallas_call(
        body, name="mid_bwd",
        out_shape=(row(F32), row(BF16), row(BF16), row(BF16), row(F32), row(F32),
                   jax.ShapeDtypeStruct((NCHIP, t, 2 * D), BF16),
                   jax.ShapeDtypeStruct((1, D), F32), jax.ShapeDtypeStruct((1, D), F32)),
        grid=(t // tt,),
        in_specs=[tile, tile, tile, tile, tile, tile, gates, wsp, wsp, wsp, vec, vec],
        out_specs=(tile, tile, tile, tile, tile, tile, gates, vec, vec),
        compiler_params=_params(("arbitrary",), VMEM_BIG),
    )(dh2, dx2, x1, m2, za, zb, p4, wa, wb, wo, g_pm, g_pf)


def _hgrn2_bwd(p4, o_all, ss, saved, dyb, dp4, logits, gnorm, p_early, tt):
    t = p4.shape[1]
    nt = t // tt
    nc = tt // CH
    ne = len(p_early)

    def body(p_ref, o_ref, ss_ref, sv_ref, dyb_ref, dp_in, lg_ref, gn_ref, *rest):
        del dp_in
        pe = rest[:ne]
        dp_ref, dlb_ref, dgn_ref = rest[ne:ne + 3]
        qe = rest[ne + 3:2 * ne + 3]
        dst, ssem, rsem = rest[2 * ne + 3:]
        i = pl.program_id(0)
        sends, arrive = _exchange_copies(qe, [(lambda chip, r=r: r.at[chip]) for r in pe], ssem, rsem)

        @pl.when(i == 0)
        def _():
            dst[...] = jnp.zeros((NH, HD, HD), F32)
            for cp in sends:
                cp.start()

        low = _chunk_masks(tt)
        lb = _lb_of(lg_ref)
        heads = [slice(h * HD, (h + 1) * HD) for h in range(NH)]
        sg, sgn, f, k, g, eg, eng, egu, qt, kt, kd = _hg_head_fwd(p_ref[0, :, 0:D], p_ref[0, :, D:2 * D], lb,
                                                                  (sv_ref[0], sv_ref[1]))
        qtb, ktb, kdb, vb = qt.astype(BF16), kt.astype(BF16), kd.astype(BF16), p_ref[1, :, 0:D].astype(BF16)
        decs = [jnp.exp(g[c * CH + CH - 1:c * CH + CH, :]) for c in range(nc)]
        og = p_ref[1, :, D:2 * D]
        so = _sig(og)
        dyb = dyb_ref[...]
        dob = dyb * (og * so)
        rn = [_rms_stats(o_ref[:, hs]) for hs in heads]
        r_all = jnp.concatenate([jnp.broadcast_to(r, (tt, HD)) for r, _ in rn], axis=1)
        n_all = jnp.concatenate([n for _, n in rn], axis=1)
        gd = dob * gn_ref[...]
        proj = jnp.concatenate(
            [jnp.broadcast_to(jnp.mean(gd[:, hs] * n_all[:, hs], axis=-1, keepdims=True), (tt, HD)) for hs in heads],
            axis=1)
        dob_ = (r_all * (gd - n_all * proj)).astype(BF16)
        dog = dyb * (n_all * gn_ref[...]) * (so * (1.0 + og * (1.0 - so)))
        dgn = jnp.sum(dob * n_all, axis=0, keepdims=True)
        dv_in, dqt_in, dkt_h = [], [], []
        for hs in heads:
            att = jnp.where(low, _dot_nt(qtb[:, hs], ktb[:, hs]), 0.0).astype(BF16)
            d_att = jnp.where(low, _dot_nt(dob_[:, hs], vb[:, hs]), 0.0).astype(BF16)
            dv_in.append(_dot_tn(att, dob_[:, hs]))
            dqt_in.append(_dot(d_att, ktb[:, hs]))
            dkt_h.append(_dot_tn(d_att, qtb[:, hs]))
        ds_t = [dst[h] for h in range(NH)]
        dv_p = [[None] * NH for _ in range(nc)]
        dqt_p = [[None] * NH for _ in range(nc)]
        dkd_p = [[None] * NH for _ in range(nc)]
        dgl_p = [[None] * NH for _ in range(nc)]
        for c in reversed(range(nc)):
            sl = slice(c * CH, (c + 1) * CH)
            for h, hs in enumerate(heads):
                s_prev = ss_ref[c, h]
                ds_bf = ds_t[h].astype(BF16)
                dec = decs[c][:, hs]
                dv_p[c][h] = dv_in[h][sl] + _dot_nt(kdb[sl, hs], ds_bf)
                dqt_p[c][h] = dqt_in[h][sl] + _dot(dob_[sl, hs], s_prev)
                dkd_p[c][h] = _dot(vb[sl, hs], ds_bf)
                ddec = jnp.sum(s_prev.astype(F32) * ds_t[h], axis=0, keepdims=True)
                dgl_p[c][h] = jnp.broadcast_to(ddec * dec, (CH, HD))
                ds_t[h] = ds_t[h] * dec + _dot_tn(dob_[sl, hs], qtb[sl, hs])
        for h in range(NH):
            dst[h] = ds_t[h]
        whole = lambda parts: jnp.concatenate([jnp.concatenate(row, axis=1) for row in parts], axis=0)
        dv, dqt, dkd, dgl = whole(dv_p), whole(dqt_p), whole(dkd_p), whole(dgl_p)
        dkt = jnp.concatenate(dkt_h, axis=1)
        dq = dqt * eg
        dk = dkt * eng + dkd * egu
        dg = dqt * qt - dkt * kt
        dgu = dkd * kd
        dlogf = _chunk_revcumsum(dg - dgu) + _chunk_total(dgu) + dgl
        common = sgn * (dlogf / f - dk)
        dfz = (1.0 - lb) * sg * common
        dlb = jnp.sum(common, axis=0, keepdims=True)
        dp_ref[0, :, 0:D] = dq.astype(BF16)
        dp_ref[0, :, D:2 * D] = dfz.astype(BF16)
        dp_ref[1, :, 0:D] = dv.astype(BF16)
        dp_ref[1, :, D:2 * D] = dog.astype(BF16)

        @pl.when(i == 0)
        def _():
            dlb_ref[0:1, :] = dlb
            dgn_ref[...] = dgn

        @pl.when(i > 0)
        def _():
            dlb_ref[0:1, :] += dlb
            dgn_ref[...] += dgn

        @pl.when(i == nt - 1)
        def _():
            d0 = dlb_ref[0:1, :] * lb * (1.0 - lb)
            dlb_ref[0:1, :] = d0
            dlb_ref[1:2, :] = -d0
            for cp in arrive:
                cp.wait_recv()
            for cp in sends:
                cp.wait_send()

    rev = lambda i: nt - 1 - i
    vec = pl.BlockSpec((1, D), lambda i: (0, 0))
    any_spec = pl.BlockSpec(memory_space=pl.ANY)
    outs = pl.pallas_call(
        body, name="hgrn2_bwd",
        out_shape=(jax.ShapeDtypeStruct(dp4.shape, BF16), jax.ShapeDtypeStruct((2, D), F32),
                   jax.ShapeDtypeStruct((1, D), F32)) + tuple(jax.ShapeDtypeStruct(a.shape, BF16) for a in p_early),
        grid=(nt,),
        in_specs=[pl.BlockSpec((2, tt, 2 * D), lambda i: (0, rev(i), 0)),
                  pl.BlockSpec((tt, D), lambda i: (rev(i), 0)),
                  pl.BlockSpec((nc, NH, HD, HD), lambda i: (rev(i), 0, 0, 0)),
                  pl.BlockSpec((2, tt, D), lambda i: (0, rev(i), 0)),
                  pl.BlockSpec((tt, D), lambda i: (rev(i), 0)),
                  any_spec,
                  pl.BlockSpec((2, D), lambda i: (0, 0)), vec] + [any_spec] * ne,
        out_specs=(pl.BlockSpec((2, tt, 2 * D), lambda i: (0, rev(i), 0)),
                   pl.BlockSpec((2, D), lambda i: (0, 0)), vec) + (any_spec,) * ne,
        scratch_shapes=[pltpu.VMEM((NH, HD, HD), F32), pltpu.SemaphoreType.DMA((3 * ne,)),
                        pltpu.SemaphoreType.DMA((3 * ne,))],
        input_output_aliases={5: 0},
        compiler_params=pltpu.CompilerParams(dimension_semantics=("arbitrary",), vmem_limit_bytes=VMEM_BIG,
                                             has_side_effects=True),
    )(p4, o_all, ss, saved, dyb, dp4, logits, gnorm, *p_early)
    return outs[0], outs[1], outs[2], list(outs[3:])


def _mixer_a_bwd(p4, hseq, saved, dya, dp4, cw, wa, wx, lam, tt):
    t = p4.shape[1]
    nt = t // tt

    def body(p_ref, sv_ref, h_ref, hh_ref, dya_ref, dp_in, cw_ref, wa_ref, wx_ref, lam_ref,
             dp_ref, dcw_ref, dcb_ref, dwa_ref, dwx_ref, dba_ref, dbx_ref, dlam_ref,
             dnext, dhc, afc):
        del dp_in
        i = pl.program_id(0)
        first_tile = i == nt - 1

        @pl.when(i == 0)
        def _():
            dnext[...] = jnp.zeros((8, D), F32)
            dhc[...] = jnp.zeros((1, D), F32)
            afc[...] = jnp.zeros((1, D), F32)

        xa = p_ref[:, 0:D]
        ga = p_ref[:, D:2 * D]
        xc, r, ig, a, mult = (sv_ref[idx] for idx in range(5))
        xcb = xc.astype(BF16)
        lam = lam_ref[...]
        sp = jnp.maximum(-lam, 0.0) + jnp.log1p(jnp.exp(-jnp.abs(lam)))
        h = h_ref[...]
        gl, dgl = _gelu_and_grad(ga)
        dya = dya_ref[...]
        dga = dya * h * dgl
        rows = lax.broadcasted_iota(jnp.int32, (tt, 1), 0)
        a_next = jnp.where(rows == tt - 1, afc[...], pltpu.roll(a, tt - 1, 0))
        dh = _scan_up(a_next, dya * gl, dhc[...])
        dhc[...] = dh[0:1, :]
        afc[...] = a[0:1, :]
        h_prev = jnp.where(rows == 0, jnp.where(first_tile, 0.0, hh_ref[7:8, :]), pltpu.roll(h, 1, 0))
        da = dh * h_prev
        dmult = dh * ig * xc
        di = dh * mult * xc
        dlog_a = da * a - dmult * a * a / mult
        dr = dlog_a * (-LRU_C * sp)
        dsp = jnp.sum(dlog_a * (-LRU_C * r), axis=0, keepdims=True)
        dra = dr * r * (1.0 - r)
        dix = di * ig * (1.0 - ig)
        drab = dra.astype(BF16)
        dixb = dix.astype(BF16)
        dxc_lin = []
        dwa_new = []
        dwx_new = []
        for n in range(NH):
            cs = slice(n * HD, (n + 1) * HD)
            dxc_lin.append(_dot_nt(drab[:, cs], wa_ref[n]) + _dot_nt(dixb[:, cs], wx_ref[n]))
            dwa_new.append(_dot_tn(xcb[:, cs], drab[:, cs]))
            dwx_new.append(_dot_tn(xcb[:, cs], dixb[:, cs]))
        dxc = dh * mult * ig + jnp.concatenate(dxc_lin, axis=1)
        de = jnp.concatenate([dxc, dnext[...]], axis=0)
        ups = [de[0:tt, :]] + [pltpu.roll(de, tt + 8 - s, 0)[0:tt, :] for s in (1, 2, 3)]
        dxa = sum(cw_ref[3 - s:4 - s, :] * ups[s] for s in range(4))
        dnext[...] = dxc[0:8, :]
        dp_ref[:, 0:D] = dxa.astype(BF16)
        dp_ref[:, D:2 * D] = dga.astype(BF16)
        dcw = jnp.concatenate(
            [jnp.sum(ups[3 - k] * xa, axis=0, keepdims=True) for k in range(4)], axis=0)
        dcb = jnp.sum(dxc, axis=0, keepdims=True)
        dba = jnp.sum(dra, axis=0, keepdims=True)
        dbx = jnp.sum(dix, axis=0, keepdims=True)
        dlam = dsp * (-_sig(-lam))

        @pl.when(i == 0)
        def _():
            dcw_ref[...] = dcw
            dcb_ref[...] = dcb
            dba_ref[...] = dba
            dbx_ref[...] = dbx
            dlam_ref[...] = dlam
            for n in range(NH):
                dwa_ref[n] = dwa_new[n]
                dwx_ref[n] = dwx_new[n]

        @pl.when(i > 0)
        def _():
            dcw_ref[...] += dcw
            dcb_ref[...] += dcb
            dba_ref[...] += dba
            dbx_ref[...] += dbx
            dlam_ref[...] += dlam
            for n in range(NH):
                dwa_ref[n] += dwa_new[n]
                dwx_ref[n] += dwx_new[n]

    rev = lambda i: nt - 1 - i
    hb = tt // 8
    full = lambda shape: pl.BlockSpec(shape, lambda i: (0,) * len(shape))
    vecs = jax.ShapeDtypeStruct((1, D), F32)
    blk = jax.ShapeDtypeStruct((NH, HD, HD), F32)
    return pl.pallas_call(
        body, name="mixer_a_bwd",
        out_shape=(jax.ShapeDtypeStruct(dp4.shape, BF16), jax.ShapeDtypeStruct((4, D), F32), vecs, blk, blk,
                   vecs, vecs, vecs),
        grid=(nt,),
        in_specs=[pl.BlockSpec((None, tt, 2 * D), lambda i: (SLOT_A, rev(i), 0)),
                  pl.BlockSpec((5, tt, D), lambda i: (0, rev(i), 0)),
                  pl.BlockSpec((tt, D), lambda i: (rev(i), 0)),
                  pl.BlockSpec((8, D), lambda i: (jnp.maximum(rev(i) * hb - 1, 0), 0)),
                  pl.BlockSpec((tt, D), lambda i: (rev(i), 0)),
                  pl.BlockSpec(memory_space=pl.ANY),
                  full((4, D)), full((NH, HD, HD)), full((NH, HD, HD)), full((1, D))],
        out_specs=(pl.BlockSpec((None, tt, 2 * D), lambda i: (SLOT_A, rev(i), 0)),
                   full((4, D)), full((1, D)), full((NH, HD, HD)), full((NH, HD, HD)),
                   full((1, D)), full((1, D)), full((1, D))),
        scratch_shapes=[pltpu.VMEM((8, D), F32), pltpu.VMEM((1, D), F32), pltpu.VMEM((1, D), F32)],
        input_output_aliases={5: 0},
        compiler_params=_params(("arbitrary",), VMEM_BIG),
    )(p4, saved, hseq, hseq, dya, dp4, cw, wa, wx, lam)


def _local_step(x, tgt, w_in, stk_rest, conv_a_w, conv_f_w, small, cidx):
    t = x.shape[0]
    tt = min(256, t)
    tm = min(1024, t)
    tk = min(2048, t)
    wa_bf = small["lru_wa"].astype(BF16)
    wx_bf = small["lru_wx"].astype(BF16)

    h1 = _norm_fwd(x, small["norm_pre_mix"], tt)
    p4, stk_rest = _mm_in_gather(h1, w_in, stk_rest, REST, tm)
    w = dict(zip(REST, _gather_forward(stk_rest, REST)))
    w["w_in"] = w_in
    w_br_a = w["w_branch_a"].reshape(D, D)
    w_br_b = w["w_branch_b"].reshape(D, D)
    w_out = w["w_out"].reshape(D, D)
    w_down = w["w_down"].reshape(DFF, D)
    ya, hseq, saved_a = _mixer_a_fwd(p4, conv_a_w, small["conv_a_b"], wa_bf, wx_bf, small["lru_ba"], small["lru_bx"],
                            small["lru_lambda"], tt)
    yb, o_all, ss, saved_b = _hgrn2_fwd(p4, small["hg_lb_logits"], small["hg_norm_g"], tt)
    za, zb, mixb, m2, x1, h2 = _mid_fwd(ya, yb, p4, x, w_br_a, w_br_b, w_out, small["norm_post_mix"],
                                        small["norm_pre_ffn"], min(512, t))
    pre_g, pre_v, y, uv, gl, dgl = _up_act_fwd(h2, w["w_up"], conv_f_w, small["conv_f_b"], tt)
    dx2, dm3, lossv, d_norm_post_ffn = _down_loss(y, w_down, x1, tgt, small["norm_post_ffn"], min(512, t))

    d_w_down = _mm_tn(y, dm3, DFF // 2, D, tk, "mm_dw_down")
    dy = _mm_nt(dm3, w_down, BF16, tm, "mm_dy")
    dup_pre, d_conv_f_w, d_conv_f_b = _ffn_act_bwd(dy, pre_g, pre_v, uv, gl, dgl, conv_f_w, tt)
    d_w_up = _mm_tn(h2, dup_pre, D, SH_UP, tk, "mm_dw_up", stacked_out=True)
    dh2 = _mm_nt_sharded(dup_pre, w["w_up"], tm, "mm_dh2")
    dx1, dm2, dza, dzb, dya, dyb, dp4, d_norm_post_mix, d_norm_pre_ffn = _mid_bwd(
        dh2, dx2, x1, m2, za, zb, p4, w_br_a, w_br_b, w_out, small["norm_post_mix"], small["norm_pre_ffn"], tt)
    d_w_out = _mm_tn(mixb, dm2, D, D, tm, "mm_dw_out")
    d_w_br_a = _mm_tn(ya, dza, D, D, tm, "mm_dw_bra")
    d_w_br_b = _mm_tn(yb, dzb, D, D, tm, "mm_dw_brb")
    early = {"w_branch_a": d_w_br_a.reshape(NCHIP, SH_BR, D), "w_branch_b": d_w_br_b.reshape(NCHIP, SH_BR, D),
             "w_out": d_w_out.reshape(NCHIP, SH_BR, D), "w_up": d_w_up, "w_down": d_w_down.reshape(NCHIP, SH_DN, D)}
    rb, _ = _reduce_stage1(early, REST, (), "reduce_d2d_in_early")
    p_rest = [_sum_own_half(early[n], rb[n], cidx, "sum_half_" + n) for n in REST]
    dp4, d_lb, d_hg_norm_g, q_rest = _hgrn2_bwd(p4, o_all, ss, saved_b, dyb, dp4, small["hg_lb_logits"], small["hg_norm_g"],
                                                p_rest, tt)
    dp4, d_conv_a_w, d_conv_a_b, d_lru_wa, d_lru_wx, d_lru_ba, d_lru_bx, d_lru_lambda = _mixer_a_bwd(
        p4, hseq, saved_a, dya, dp4, conv_a_w, wa_bf, wx_bf, small["lru_lambda"], tt)
    d_w_in = _mm_tn(h1, dp4, D, SH_IN, tk, "mm_dw_in", stacked_slot_fn=_slot_of_chip, stacked_out=True)
    rb, _ = _reduce_stage1({"w_in": d_w_in}, ("w_in",), (), "reduce_d2d_in_w_in")
    p_w_in = _sum_own_half(d_w_in, rb["w_in"], cidx, "sum_half_w_in")
    grad_x, q_w_in, d_norm_pre_mix = _mm_dh1_exchange(dp4, w_in, p_w_in, x, dx1, small["norm_pre_mix"], tm)

    smalls = {
        "norm_pre_mix": d_norm_pre_mix, "conv_a_b": d_conv_a_b, "lru_ba": d_lru_ba, "lru_bx": d_lru_bx,
        "lru_lambda": d_lru_lambda, "hg_lb_logits": d_lb, "hg_norm_g": d_hg_norm_g, "norm_post_mix": d_norm_post_mix,
        "norm_pre_ffn": d_norm_pre_ffn, "norm_post_ffn": d_norm_post_ffn, "lossv": lossv,
        "conv_a_w": d_conv_a_w, "lru_wa": d_lru_wa, "lru_wx": d_lru_wx,
        "conv_f_b": d_conv_f_b, "conv_f_w": d_conv_f_w,
    }
    p_big = dict(zip(REST, p_rest), w_in=p_w_in)
    q_big = dict(zip(REST, q_rest), w_in=q_w_in)
    return grad_x, p_big, q_big, smalls


BIG = ("w_in", "w_branch_a", "w_branch_b", "w_out", "w_up", "w_down")
BIG_SHAPE = {"w_in": (D, SH_IN), "w_branch_a": (SH_BR, D), "w_branch_b": (SH_BR, D), "w_out": (SH_BR, D),
             "w_up": (D, SH_UP), "w_down": (SH_DN, D)}
NBIG = len(BIG)
REST = BIG[1:]
VEC_ROWS = (("norm_pre_mix", 0, 1), ("conv_a_b", 1, 1), ("lru_ba", 2, 1), ("lru_bx", 3, 1), ("lru_lambda", 4, 1),
            ("hg_lb_logits", 5, 2), ("hg_norm_g", 7, 1), ("norm_post_mix", 8, 1), ("norm_pre_ffn", 9, 1),
            ("norm_post_ffn", 10, 1))
ROW_LOSS = 11
ROW_CONV_A = 12
S1_ROWS = 16
S2_ROWS = 8


def _place():
    x, y, c = lax.axis_index("x"), lax.axis_index("y"), lax.axis_index("c")
    chips = [(1 - x, y), (x, 1 - y), (1 - x, 1 - y)]
    return x, y, c, 2 * x + y, chips


def _remote(src, dst, ssem, rsem, dev):
    return pltpu.make_async_remote_copy(src_ref=src, dst_ref=dst, send_sem=ssem, recv_sem=rsem,
                                        device_id=dev, device_id_type=MESH)


def _hbm_call(body, name, ins, out_shapes, n_sems, aliases=None):
    any_spec = pl.BlockSpec(memory_space=pl.ANY)
    return pl.pallas_call(
        body, name=name, out_shape=tuple(out_shapes),
        in_specs=[any_spec] * len(ins), out_specs=tuple([any_spec] * len(out_shapes)),
        scratch_shapes=[pltpu.SemaphoreType.DMA((n,)) for n in n_sems],
        input_output_aliases=aliases or {},
        compiler_params=pltpu.CompilerParams(has_side_effects=True),
    )(*ins)


def _gather_copies(stk, names, ssem, rsem, fssem=None, frsem=None):
    x, y, c, j, chips = _place()
    sends, arrive, fwds, farrive = [], [], [], []
    for w, n in enumerate(names):
        hw = BIG_SHAPE[n][0] // 2
        mine = stk[w].at[j, pl.ds(c * hw, hw), :]
        for k, (cx, cy) in enumerate(chips):
            i = 3 * w + k
            got = stk[w].at[2 * cx + cy, pl.ds(c * hw, hw), :]
            other = stk[w].at[2 * cx + cy, pl.ds((1 - c) * hw, hw), :]
            sends.append(_remote(mine, mine, ssem.at[i], rsem.at[i], (cx, cy, c)))
            arrive.append(_remote(got, got, ssem.at[i], rsem.at[i], (cx, cy, c)))
            if fssem is not None:
                fwds.append(_remote(got, got, fssem.at[i], frsem.at[i], (x, y, 1 - c)))
                farrive.append(_remote(other, other, fssem.at[i], frsem.at[i], (x, y, 1 - c)))
    return sends, arrive, fwds, farrive


def _gather_first(stacked_w_in, conv_a_s, conv_f_s):
    ins = [stacked_w_in, conv_a_s, conv_f_s]
    out_shapes = [jax.ShapeDtypeStruct(stacked_w_in.shape, stacked_w_in.dtype)]
    out_shapes += [jax.ShapeDtypeStruct((NCHIP,) + a.shape, a.dtype) for a in (conv_a_s, conv_f_s)]

    def body(w_in, ca_src, cf_src, w_out, ca_dst, cf_dst, ssem, rsem, fssem, frsem, csend, crecv, lsem):
        del w_in
        x, y, c, j, chips = _place()
        conv = ((ca_src, ca_dst), (cf_src, cf_dst))
        locs = [pltpu.make_async_copy(src, dst.at[j], lsem.at[i]) for i, (src, dst) in enumerate(conv)]
        csends = [_remote(src, dst.at[j], csend.at[3 * i + k], crecv.at[3 * i + k], (cx, cy, c))
                  for i, (src, dst) in enumerate(conv) for k, (cx, cy) in enumerate(chips)]
        sends, arrive, fwds, farrive = _gather_copies([w_out], ("w_in",), ssem, rsem, fssem, frsem)
        for cp in locs + sends + csends:
            cp.start()
        for got, fwd in zip(arrive, fwds):
            got.wait_recv()
            fwd.start()
        for i, (_, dst) in enumerate(conv):
            for k, (cx, cy) in enumerate(chips):
                got = dst.at[2 * cx + cy]
                _remote(got, got, csend.at[3 * i + k], crecv.at[3 * i + k], (cx, cy, c)).wait_recv()
        for cp in farrive:
            cp.wait_recv()
        for cp in sends + fwds + csends:
            cp.wait_send()
        for cp in locs:
            cp.wait()

    return _hbm_call(body, "gather_first", ins, out_shapes, (3, 3, 3, 3, 6, 6, 2), aliases={0: 0})


def _gather_forward(stk, names):
    nw = len(names)

    def body(*refs):
        dst = refs[nw:2 * nw]
        ssem, rsem, fssem, frsem = refs[2 * nw:]
        _, _, fwds, farrive = _gather_copies(dst, names, ssem, rsem, fssem, frsem)
        for cp in fwds:
            cp.start()
        for cp in farrive:
            cp.wait_recv()
        for cp in fwds:
            cp.wait_send()

    out_shapes = [jax.ShapeDtypeStruct(a.shape, a.dtype) for a in stk]
    return _hbm_call(body, "gather_forward", stk, out_shapes, (3 * nw,) * 4, aliases={w: w for w in range(nw)})


def _exchange_copies(dst, pieces, ssem, rsem):
    x, y, c, j, chips = _place()
    sends, arrive = [], []
    for w in range(len(dst)):
        for k, (cx, cy) in enumerate(chips):
            i = 3 * w + k
            sends.append(_remote(pieces[w](2 * cx + cy), dst[w].at[j], ssem.at[i], rsem.at[i], (cx, cy, c)))
            got = dst[w].at[2 * cx + cy]
            arrive.append(_remote(got, got, ssem.at[i], rsem.at[i], (cx, cy, c)))
    return sends, arrive


def _reduce_stage1(big_g, names, smalls, name):
    nb = len(names)
    ins = [big_g[n] for n in names] + list(smalls)
    n_in = len(ins)
    halves = [BIG_SHAPE[n][0] // 2 for n in names]
    out_shapes = [jax.ShapeDtypeStruct((NCHIP, halves[w], BIG_SHAPE[n][1]), big_g[n].dtype)
                  for w, n in enumerate(names)]
    out_shapes += [jax.ShapeDtypeStruct(a.shape, F32) for a in smalls]

    def body(*refs):
        src, dst = refs[:n_in], refs[n_in:2 * n_in]
        ssem, rsem = refs[2 * n_in:]
        x, y, c, _, _ = _place()
        cps = []
        for w in range(n_in):
            s_ = src[w].at[:, pl.ds((1 - c) * halves[w], halves[w]), :] if w < nb else src[w]
            cp = _remote(s_, dst[w], ssem.at[w], rsem.at[w], (x, y, 1 - c))
            cp.start()
            cps.append(cp)
        for cp in cps:
            cp.wait()

    outs = _hbm_call(body, name, ins, out_shapes, (n_in, n_in))
    return dict(zip(names, outs[:nb])), outs[nb:]


def _reduce_stage2(ps1, ps2, ps3):
    ins = [ps1, ps2, ps3]
    h1, h2, h3 = S1_ROWS // 2, DUP // 2, D
    out_shapes = [jax.ShapeDtypeStruct((NCHIP, h1, D), F32), jax.ShapeDtypeStruct((NCHIP, S2_ROWS, h2), F32),
                  jax.ShapeDtypeStruct((NCHIP, h3, HD), F32)]

    def body(*refs):
        src, dst = refs[:3], refs[3:6]
        ssem, rsem = refs[6:]
        c = lax.axis_index("c")
        pieces = [lambda chip: src[0].at[pl.ds(c * h1, h1), :],
                  lambda chip: src[1].at[:, pl.ds(c * h2, h2)],
                  lambda chip: src[2].at[pl.ds(c * h3, h3), :]]
        sends, arrive = _exchange_copies(dst, pieces, ssem, rsem)
        for cp in sends:
            cp.start()
        for cp in arrive:
            cp.wait_recv()
        for cp in sends:
            cp.wait_send()

    return _hbm_call(body, "reduce_ici_small", ins, out_shapes, (9, 9))


def _reduce_stage3(f_big, fs1, fs2, fs3):
    ins = [f_big[n] for n in BIG] + [fs1, fs2, fs3]
    n_in = len(ins)
    halves = [BIG_SHAPE[n][0] // 2 for n in BIG]
    h1, h2, h3 = S1_ROWS // 2, DUP // 2, D
    out_shapes = [jax.ShapeDtypeStruct(BIG_SHAPE[n], F32) for n in BIG]
    out_shapes += [jax.ShapeDtypeStruct((S1_ROWS, D), F32), jax.ShapeDtypeStruct((S2_ROWS, DUP), F32),
                   jax.ShapeDtypeStruct((2 * D, HD), F32)]

    def body(*refs):
        dst = refs[n_in:2 * n_in]
        ssem, rsem = refs[2 * n_in:]
        x, y, c, _, _ = _place()

        def place(w, which):
            if w < NBIG:
                return dst[w].at[pl.ds(which * halves[w], halves[w]), :]
            if w == NBIG:
                return dst[w].at[pl.ds(which * h1, h1), :]
            if w == NBIG + 1:
                return dst[w].at[:, pl.ds(which * h2, h2)]
            return dst[w].at[pl.ds(which * h3, h3), :]

        cps = [_remote(place(w, c), place(w, c), ssem.at[w], rsem.at[w], (x, y, 1 - c)) for w in range(n_in)]
        for cp in cps:
            cp.start()
        for w in range(n_in):
            got = place(w, 1 - c)
            _remote(got, got, ssem.at[w], rsem.at[w], (x, y, 1 - c)).wait_recv()
        for cp in cps:
            cp.wait_send()

    outs = _hbm_call(body, "reduce_d2d_out", ins, out_shapes, (n_in, n_in), aliases={w: w for w in range(n_in)})
    return dict(zip(BIG, outs[:NBIG])), outs[NBIG], outs[NBIG + 1], outs[NBIG + 2]


def _row_tile(rows):
    for tr in (128, 176, 64, 16, 8):
        if rows % tr == 0:
            return tr
    return rows


def _sum_own_half(g, rb, cidx, name):
    s, rows, cols = g.shape
    half = rows // 2
    tr = _row_tile(half)
    nb = half // tr

    def body(c_ref, g_ref, r_ref, o_ref):
        del c_ref
        o_ref[...] = (g_ref[...].astype(F32) + r_ref[...].astype(F32)).astype(BF16)

    grid_spec = pltpu.PrefetchScalarGridSpec(
        num_scalar_prefetch=1, grid=(s, nb),
        in_specs=[pl.BlockSpec((None, tr, cols), lambda k, i, c: (k, c[0] * nb + i, 0)),
                  pl.BlockSpec((None, tr, cols), lambda k, i, c: (k, i, 0))],
        out_specs=pl.BlockSpec((None, tr, cols), lambda k, i, c: (k, i, 0)))
    return pl.pallas_call(
        body, name=name, grid_spec=grid_spec, out_shape=jax.ShapeDtypeStruct((s, half, cols), BF16),
        compiler_params=_params(("parallel", "parallel")),
    )(cidx, g, rb)


def _sum_chips(q, p, jc, name, by_cols=False):
    s, rows, cols = q.shape
    tr = _row_tile(rows)
    nb = rows // tr
    stacked = p.ndim == 3

    def body(jc_ref, q_ref, p_ref, o_ref):
        j = jc_ref[0]
        own = p_ref[...].astype(F32)
        acc = None
        for k in range(NCHIP):
            term = jnp.where(j == k, own, q_ref[k].astype(F32))
            acc = term if acc is None else acc + term
        o_ref[...] = acc

    if by_cols:
        half_spec = pl.BlockSpec((tr, cols), lambda i, jc_ref: (i, jc_ref[1]))
        out_shape = jax.ShapeDtypeStruct((rows, 2 * cols), F32)
    else:
        half_spec = pl.BlockSpec((tr, cols), lambda i, jc_ref: (jc_ref[1] * nb + i, 0))
        out_shape = jax.ShapeDtypeStruct((2 * rows, cols), F32)
    p_spec = pl.BlockSpec((None, tr, cols), lambda i, jc_ref: (jc_ref[0], i, 0)) if stacked else half_spec
    grid_spec = pltpu.PrefetchScalarGridSpec(
        num_scalar_prefetch=1, grid=(nb,),
        in_specs=[pl.BlockSpec((s, tr, cols), lambda i, jc_ref: (0, i, 0)), p_spec],
        out_specs=half_spec)
    return pl.pallas_call(
        body, name=name, grid_spec=grid_spec, out_shape=out_shape,
        compiler_params=_params(("parallel",)),
    )(jc, q, p)


def _place_shard(w, jc, name):
    rows, cols = w.shape
    tr = _row_tile(rows)

    def body(jc_ref, w_ref, o_ref):
        del jc_ref
        o_ref[...] = w_ref[...].astype(BF16)

    grid_spec = pltpu.PrefetchScalarGridSpec(
        num_scalar_prefetch=1, grid=(rows // tr,),
        in_specs=[pl.BlockSpec((tr, cols), lambda i, jc_ref: (i, 0))],
        out_specs=pl.BlockSpec((None, tr, cols), lambda i, jc_ref: (jc_ref[0], i, 0)))
    return pl.pallas_call(
        body, name=name, grid_spec=grid_spec, out_shape=jax.ShapeDtypeStruct((NCHIP, rows, cols), BF16),
        compiler_params=_params(("parallel",)),
    )(jc, w)


def _add(a, b, name):
    def body(a_ref, b_ref, o_ref):
        o_ref[...] = a_ref[...] + b_ref[...]

    return pl.pallas_call(body, name=name, out_shape=jax.ShapeDtypeStruct(a.shape, F32))(a, b)


def _pack_small(sm):
    vec_in = [sm[n] for n, _, _ in VEC_ROWS]
    nv = len(vec_in)

    def body(*refs):
        ins, lossv, dcw, dcfb, dcfw, s1, s2 = refs[:nv], refs[nv], refs[nv + 1], refs[nv + 2], refs[nv + 3], \
            refs[nv + 4], refs[nv + 5]
        for ref, (_, r0, nr) in zip(ins, VEC_ROWS):
            s1[r0:r0 + nr, :] = ref[...]
        s1[ROW_LOSS:ROW_LOSS + 1, :] = lossv[...]
        s1[ROW_CONV_A:ROW_CONV_A + 4, :] = dcw[...]
        s2[0:1, :] = dcfb[...]
        s2[1:4, :] = dcfw[...]
        s2[4:8, :] = jnp.zeros((4, DUP), F32)

    return pl.pallas_call(
        body, name="pack_small",
        out_shape=(jax.ShapeDtypeStruct((S1_ROWS, D), F32), jax.ShapeDtypeStruct((S2_ROWS, DUP), F32)),
    )(*vec_in, sm["lossv"], sm["conv_a_w"], sm["conv_f_b"], sm["conv_f_w"])


def _adam_math(w, g, m, v):
    m = ADAM_B1 * m + (1.0 - ADAM_B1) * g
    v = ADAM_B2 * v + (1.0 - ADAM_B2) * (g * g)
    m_hat = m / (1.0 - ADAM_B1 ** ADAM_STEP)
    v_hat = v / (1.0 - ADAM_B2 ** ADAM_STEP)
    delta = -ADAM_LR * (m_hat / (jnp.sqrt(v_hat) + ADAM_EPS) + ADAM_WD * w)
    return delta, m, v


def _adam(w, g, m, v, name):
    rows, cols = w.shape
    tr = _row_tile(rows)

    def body(w_ref, g_ref, m_ref, v_ref, d_ref, mo_ref, vo_ref):
        d_ref[...], mo_ref[...], vo_ref[...] = _adam_math(w_ref[...], g_ref[...], m_ref[...], v_ref[...])

    spec = pl.BlockSpec((tr, cols), lambda i: (i, 0))
    return pl.pallas_call(
        body, name=name, out_shape=(jax.ShapeDtypeStruct(w.shape, F32),) * 3, grid=(rows // tr,),
        in_specs=[spec] * 4, out_specs=(spec,) * 3,
        compiler_params=_params(("parallel",)),
    )(w, g, m, v)


def _adam_small(gs1, gs2, gs3, w, m, v):
    names = [n for n, _, _ in VEC_ROWS] + ["conv_f_b", "lru_wa", "lru_wx"]
    nn = len(names)

    def grad_of(i, g1, g2, g3):
        if i < len(VEC_ROWS):
            _, r0, nr = VEC_ROWS[i]
            return g1[r0:r0 + nr, :]
        if names[i] == "conv_f_b":
            return g2[0:1, :]
        return g3[0] if names[i] == "lru_wa" else g3[1]

    def body(*refs):
        g1, g2, g3 = refs[0], refs[1], refs[2]
        ws, ms, vs = refs[3:3 + nn], refs[3 + nn:3 + 2 * nn], refs[3 + 2 * nn:3 + 3 * nn]
        outs = refs[3 + 3 * nn:]
        for i in range(nn):
            d, mn, vn = _adam_math(ws[i][...], grad_of(i, g1, g2, g3), ms[i][...], vs[i][...])
            outs[i][...] = d
            outs[nn + i][...] = mn
            outs[2 * nn + i][...] = vn

    shapes = [jax.ShapeDtypeStruct(w[n].shape, F32) for n in names]
    outs = pl.pallas_call(body, name="adam_small", out_shape=tuple(shapes * 3))(
        gs1, gs2, gs3, *[w[n] for n in names], *[m[n] for n in names], *[v[n] for n in names])
    return {n: (outs[i], outs[nn + i], outs[2 * nn + i]) for i, n in enumerate(names)}


WEIGHTS = ("norm_pre_mix", "w_in", "conv_a_w", "conv_a_b", "lru_wa", "lru_ba", "lru_wx", "lru_bx", "lru_lambda",
           "hg_lb_logits", "hg_norm_g", "w_branch_a", "w_branch_b", "w_out", "norm_post_mix", "norm_pre_ffn",
           "w_up", "conv_f_w", "conv_f_b", "w_down", "norm_post_ffn")
NW = len(WEIGHTS)


def kernel(x, norm_pre_mix, w_in, conv_a_w, conv_a_b, lru_wa, lru_ba, lru_wx, lru_bx, lru_lambda, hg_lb_logits, hg_norm_g, w_branch_a, w_branch_b, w_out, norm_post_mix, norm_pre_ffn, w_up, conv_f_w, conv_f_b, w_down, norm_post_ffn, loss_target, m_norm_pre_mix, m_w_in, m_conv_a_w, m_conv_a_b, m_lru_wa, m_lru_ba, m_lru_wx, m_lru_bx, m_lru_lambda, m_hg_lb_logits, m_hg_norm_g, m_w_branch_a, m_w_branch_b, m_w_out, m_norm_post_mix, m_norm_pre_ffn, m_w_up, m_conv_f_w, m_conv_f_b, m_w_down, m_norm_post_ffn, v_norm_pre_mix, v_w_in, v_conv_a_w, v_conv_a_b, v_lru_wa, v_lru_ba, v_lru_wx, v_lru_bx, v_lru_lambda, v_hg_lb_logits, v_hg_norm_g, v_w_branch_a, v_w_branch_b, v_w_out, v_norm_post_mix, v_norm_pre_ffn, v_w_up, v_conv_f_w, v_conv_f_b, v_w_down, v_norm_post_ffn):
    rest = (norm_pre_mix, w_in, conv_a_w, conv_a_b, lru_wa, lru_ba, lru_wx, lru_bx, lru_lambda, hg_lb_logits, hg_norm_g, w_branch_a, w_branch_b, w_out, norm_post_mix, norm_pre_ffn, w_up, conv_f_w, conv_f_b, w_down, norm_post_ffn, loss_target, m_norm_pre_mix, m_w_in, m_conv_a_w, m_conv_a_b, m_lru_wa, m_lru_ba, m_lru_wx, m_lru_bx, m_lru_lambda, m_hg_lb_logits, m_hg_norm_g, m_w_branch_a, m_w_branch_b, m_w_out, m_norm_post_mix, m_norm_pre_ffn, m_w_up, m_conv_f_w, m_conv_f_b, m_w_down, m_norm_post_ffn, v_norm_pre_mix, v_w_in, v_conv_a_w, v_conv_a_b, v_lru_wa, v_lru_ba, v_lru_wx, v_lru_bx, v_lru_lambda, v_hg_lb_logits, v_hg_norm_g, v_w_branch_a, v_w_branch_b, v_w_out, v_norm_post_mix, v_norm_pre_ffn, v_w_up, v_conv_f_w, v_conv_f_b, v_w_down, v_norm_post_ffn)
    w_in_args = dict(zip(WEIGHTS, rest[:NW]))
    loss_target = rest[NW]
    m_args = dict(zip(WEIGHTS, rest[NW + 1:2 * NW + 1]))
    v_args = dict(zip(WEIGHTS, rest[2 * NW + 1:3 * NW + 1]))
    shape_of = {n: w_in_args[n].shape for n in WEIGHTS}

    def two_d(n, a):
        if n in BIG:
            return a.reshape(BIG_SHAPE[n])
        if n in ("lru_wa", "lru_wx"):
            return a.reshape(NH, HD, HD)
        return a.reshape(a.shape[-2:])

    w2 = {n: two_d(n, w_in_args[n]) for n in WEIGHTS}
    m2 = {n: two_d(n, m_args[n]) for n in WEIGHTS}
    v2 = {n: two_d(n, v_args[n]) for n in WEIGHTS}

    cidx = lax.axis_index("c").astype(jnp.int32).reshape(1)
    jchip = 2 * lax.axis_index("x") + lax.axis_index("y")

    jc = jnp.stack([jchip, lax.axis_index("c")]).astype(jnp.int32)

    shards = {n: _place_shard(w2[n], jc, "place_" + n) for n in BIG}
    conv_a_s = jnp.pad(w2["conv_a_w"], ((0, 4), (0, 0)))
    conv_f_s = jnp.pad(w2["conv_f_w"], ((0, 5), (0, 0)))
    w_in_full, conv_a_g, conv_f_g = _gather_first(shards["w_in"], conv_a_s, conv_f_s)
    conv_a_full = jnp.transpose(conv_a_g, (1, 0, 2)).reshape(8, D)[0:4]
    conv_f_full = jnp.transpose(conv_f_g, (1, 0, 2)).reshape(8, DUP)[0:3]
    small = {n: w2[n] for n in WEIGHTS if n not in BIG and n not in ("conv_a_w", "conv_f_w")}

    grad_x, p_big, q_big, sm_g = _local_step(
        x[0], loss_target[0], w_in_full, [shards[n] for n in REST], conv_a_full, conv_f_full, small, cidx)

    s1, s2 = _pack_small(sm_g)
    s3 = jnp.concatenate([sm_g["lru_wa"].reshape(D, HD), sm_g["lru_wx"].reshape(D, HD)], axis=0)
    _, (rs1, rs2, rs3) = _reduce_stage1({}, (), (s1, s2, s3), "reduce_d2d_in_small")
    ps1, ps2, ps3 = _add(s1, rs1, "add_s1"), _add(s2, rs2, "add_s2"), _add(s3, rs3, "add_s3")
    qs1, qs2, qs3 = _reduce_stage2(ps1, ps2, ps3)
    f_big = {n: _sum_chips(q_big[n], p_big[n], jc, "sum_chips_" + n) for n in BIG}
    fs1 = _sum_chips(qs1, ps1, jc, "sum_chips_s1")
    fs2 = _sum_chips(qs2, ps2, jc, "sum_chips_s2", by_cols=True)
    fs3 = _sum_chips(qs3, ps3, jc, "sum_chips_s3")
    g_big, gs1, gs2, gs3 = _reduce_stage3(f_big, fs1, fs2, fs3)

    res = {}
    for n in BIG:
        d, mn, vn = _adam(w2[n], g_big[n], m2[n], v2[n], "adam_" + n)
        res[n] = (g_big[n], d, mn, vn)
    small_res = _adam_small(gs1, gs2, gs3.reshape(2, NH, HD, HD), w2, m2, v2)
    for n, r0, nr in VEC_ROWS:
        res[n] = (gs1[r0:r0 + nr],) + small_res[n]
    res["conv_f_b"] = (gs2[0:1],) + small_res["conv_f_b"]
    res["lru_wa"] = (gs3[0:D].reshape(NH, HD, HD),) + small_res["lru_wa"]
    res["lru_wx"] = (gs3[D:2 * D].reshape(NH, HD, HD),) + small_res["lru_wx"]
    g_ca = lax.dynamic_slice_in_dim(gs1[ROW_CONV_A:ROW_CONV_A + 4], jchip * (D // NCHIP), D // NCHIP, axis=1)
    g_cf = lax.dynamic_slice_in_dim(gs2[1:4], jchip * SH_UP, SH_UP, axis=1)
    res["conv_a_w"] = (g_ca,) + _adam(w2["conv_a_w"], g_ca, m2["conv_a_w"], v2["conv_a_w"], "adam_conv_a_w")
    res["conv_f_w"] = (g_cf,) + _adam(w2["conv_f_w"], g_cf, m2["conv_f_w"], v2["conv_f_w"], "adam_conv_f_w")

    loss = (0.5 / D) * jnp.sum(gs1[ROW_LOSS])
    out = [loss, grad_x.reshape(x.shape)]
    for part in range(4):
        out += [res[n][part].reshape(shape_of[n]) for n in WEIGHTS]
    return tuple(out)
```

```python
import functools

import jax
import jax.numpy as jnp
from jax import lax
from jax.experimental import pallas as pl
from jax.experimental.pallas import tpu as pltpu

F32 = jnp.float32
BF16 = jnp.bfloat16

D = 1024
NH = 8
HD = 128
CH = 32
DFF = 2816
DUP = 2 * DFF
NCHIP = 4
SH_IN = 2 * D
SH_UP = DUP // NCHIP
SH_DN = DFF // NCHIP
SH_BR = D // NCHIP
EPS = 1e-6
LRU_C = 8.0
ADAM_LR = 0.001
ADAM_B1 = 0.9
ADAM_B2 = 0.999
ADAM_EPS = 1e-08
ADAM_WD = 0.01
ADAM_STEP = 10
VMEM_BIG = 56 * 1024 * 1024
MESH = pl.DeviceIdType.MESH

SLOT_A, SLOT_B, SLOT_C, SLOT_G = 2, 0, 1, 3


def _slot_of_chip(s):
    return jnp.where(s == 3, 3, (s + 2) % 3)


def _params(sem, vmem=None):
    return pltpu.CompilerParams(dimension_semantics=sem, vmem_limit_bytes=vmem)


_GC = 0.7978845608028654
_GA = 0.044715


def _gelu(x):
    return 0.5 * x * (1.0 + jnp.tanh(_GC * (x + _GA * x * x * x)))


def _gelu_and_grad(x):
    x2 = x * x
    th = jnp.tanh(_GC * x * (1.0 + _GA * x2))
    g = 0.5 * x * (1.0 + th)
    dg = 0.5 * (1.0 + th) + 0.5 * x * (1.0 - th * th) * _GC * (1.0 + 3.0 * _GA * x2)
    return g, dg


def _sig(x):
    return jax.nn.sigmoid(x)


def _dot(a, b):
    return jnp.dot(a, b, preferred_element_type=F32)


def _dot_nt(a, b):
    return lax.dot_general(a, b, (((1,), (1,)), ((), ())), preferred_element_type=F32)


def _dot_tn(a, b):
    return lax.dot_general(a, b, (((0,), (0,)), ((), ())), preferred_element_type=F32)


def _chunk_cumsum(x):
    pos = lax.broadcasted_iota(jnp.int32, (x.shape[0], 1), 0) & (CH - 1)
    d = 1
    while d < CH:
        x = x + jnp.where(pos >= d, pltpu.roll(x, d, 0), 0.0)
        d *= 2
    return x


def _chunk_revcumsum(x):
    n = x.shape[0]
    pos = lax.broadcasted_iota(jnp.int32, (n, 1), 0) & (CH - 1)
    d = 1
    while d < CH:
        x = x + jnp.where(pos < CH - d, pltpu.roll(x, n - d, 0), 0.0)
        d *= 2
    return x


def _chunk_last(x):
    n = x.shape[0]
    return jnp.concatenate(
        [jnp.broadcast_to(x[c * CH + CH - 1:c * CH + CH, :], (CH, x.shape[1])) for c in range(n // CH)], axis=0)


def _chunk_total(x):
    n = x.shape[0]
    return jnp.concatenate(
        [jnp.broadcast_to(jnp.sum(x[c * CH:(c + 1) * CH, :], axis=0, keepdims=True), (CH, x.shape[1]))
         for c in range(n // CH)], axis=0)


def _rms_stats(x):
    r = lax.rsqrt(jnp.mean(x * x, axis=-1, keepdims=True) + EPS)
    return r, x * r


def _rms_bwd(gd, n, r):
    return r * (gd - n * jnp.mean(gd * n, axis=-1, keepdims=True))


def _shift_rows(x, d, fill):
    rows = lax.broadcasted_iota(jnp.int32, (x.shape[0], 1), 0)
    return jnp.where(rows >= d, pltpu.roll(x, d, 0), fill)


def _scan_down(a, u, carry):
    n = a.shape[0]
    pos = lax.broadcasted_iota(jnp.int32, (n, 1), 0) & 7
    for d in (1, 2, 4):
        u = a * jnp.where(pos >= d, pltpu.roll(u, d, 0), 0.0) + u
        a = a * jnp.where(pos >= d, pltpu.roll(a, d, 0), 1.0)
    out = []
    for v in range(n // 8):
        h = a[v * 8:v * 8 + 8, :] * carry + u[v * 8:v * 8 + 8, :]
        carry = h[7:8, :]
        out.append(h)
    return jnp.concatenate(out, axis=0)


def _scan_up(b, g, carry):
    n = b.shape[0]
    pos = lax.broadcasted_iota(jnp.int32, (n, 1), 0) & 7
    for d in (1, 2, 4):
        g = g + b * jnp.where(pos < 8 - d, pltpu.roll(g, n - d, 0), 0.0)
        b = b * jnp.where(pos < 8 - d, pltpu.roll(b, n - d, 0), 1.0)
    out = [None] * (n // 8)
    for v in reversed(range(n // 8)):
        h = g[v * 8:v * 8 + 8, :] + b[v * 8:v * 8 + 8, :] * carry
        carry = h[0:1, :]
        out[v] = h
    return jnp.concatenate(out, axis=0)


def _shift_rows_up(x, d, fill):
    n = x.shape[0]
    rows = lax.broadcasted_iota(jnp.int32, (n, 1), 0)
    return jnp.where(rows < n - d, pltpu.roll(x, n - d, 0), fill)


def _mm_nn_sharded(a, b3, out_dtype, tm, name, slot_fn=None):
    m, k = a.shape
    s, _, ns = b3.shape

    def body(a_ref, b_ref, o_ref):
        o_ref[...] = _dot(a_ref[...], b_ref[...]).astype(out_dtype)

    if slot_fn is None:
        out_shape = jax.ShapeDtypeStruct((m, s * ns), out_dtype)
        out_spec = pl.BlockSpec((tm, ns), lambda j, i: (i, j))
    else:
        out_shape = jax.ShapeDtypeStruct((s, m, ns), out_dtype)
        out_spec = pl.BlockSpec((None, tm, ns), lambda j, i: (slot_fn(j), i, 0))
    return pl.pallas_call(
        body, name=name, out_shape=out_shape, grid=(s, m // tm),
        in_specs=[pl.BlockSpec((tm, k), lambda j, i: (i, 0)),
                  pl.BlockSpec((None, k, ns), lambda j, i: (j, 0, 0))],
        out_specs=out_spec,
        compiler_params=_params(("parallel", "parallel"), VMEM_BIG),
    )(a, b3)


def _peer_of_step(s):
    return ((s & 1) << 1) | (s >> 1)


def _mm_in_gather(a, stk_w_in, stk_rest, names_rest, conv_a_s, conv_f_s, jc, tm):
    m, k = a.shape
    s_n, _, ns = stk_w_in.shape
    nr = len(stk_rest)
    mt = m // tm

    def body(jc_ref, a_ref, w_in_any, *rest):
        del w_in_any
        ca_src, cf_src = rest[nr], rest[nr + 1]
        o_ref, w_full = rest[nr + 2], rest[nr + 3]
        stk = rest[nr + 4:2 * nr + 4]
        ca_dst, cf_dst = rest[2 * nr + 4], rest[2 * nr + 5]
        wbuf, ssem_w, rsem_w, fssem_w, frsem_w, ssem_r, rsem_r, csend, crecv, lsem, wsem = rest[2 * nr + 6:]
        s, i = pl.program_id(0), pl.program_id(1)
        x, y, c, j, chips = _place()
        sends_w, arrive_w, fwds_w, farrive_w = _gather_copies([w_full], ("w_in",), ssem_w, rsem_w, fssem_w, frsem_w)
        sends_r, arrive_r, _, _ = _gather_copies(stk, names_rest, ssem_r, rsem_r)
        conv = ((ca_src, ca_dst), (cf_src, cf_dst))
        locs = [pltpu.make_async_copy(src, dst.at[j], lsem.at[n]) for n, (src, dst) in enumerate(conv)]
        csends = [_remote(src, dst.at[j], csend.at[3 * n + kk], crecv.at[3 * n + kk], (cx, cy, c))
                  for n, (src, dst) in enumerate(conv) for kk, (cx, cy) in enumerate(chips)]

        @pl.when((s == 0) & (i == 0))
        def _():
            for cp in sends_w + locs + csends + sends_r:
                cp.start()

        for kk in range(3):
            @pl.when((s == kk + 1) & (i == 0))
            def _(kk=kk):
                arrive_w[kk].wait_recv()
                fwds_w[kk].start()
                farrive_w[kk].wait_recv()

        @pl.when(i == 0)
        def _():
            fetch = pltpu.make_async_copy(w_full.at[j ^ _peer_of_step(s)], wbuf, wsem.at[0])
            fetch.start()
            fetch.wait()

        o_ref[...] = _dot(a_ref[...], wbuf[...])

        @pl.when((s == s_n - 1) & (i == mt - 1))
        def _():
            for cp in arrive_r:
                cp.wait_recv()
            for n, (_, dst) in enumerate(conv):
                for kk, (cx, cy) in enumerate(chips):
                    got = dst.at[2 * cx + cy]
                    _remote(got, got, csend.at[3 * n + kk], crecv.at[3 * n + kk], (cx, cy, c)).wait_recv()
            for cp in sends_w + fwds_w + sends_r + csends:
                cp.wait_send()
            for cp in locs:
                cp.wait()

    any_spec = pl.BlockSpec(memory_space=pl.ANY)
    sem = pltpu.SemaphoreType.DMA
    grid_spec = pltpu.PrefetchScalarGridSpec(
        num_scalar_prefetch=1, grid=(s_n, mt),
        in_specs=[pl.BlockSpec((tm, k), lambda s, i, jc_ref: (i, 0))] + [any_spec] * (nr + 3),
        out_specs=(pl.BlockSpec((None, tm, ns),
                                lambda s, i, jc_ref: (_slot_of_chip(jc_ref[0] ^ _peer_of_step(s)), i, 0)),)
        + (any_spec,) * (nr + 3),
        scratch_shapes=[pltpu.VMEM((k, ns), BF16), sem((3,)), sem((3,)), sem((3,)), sem((3,)),
                        sem((3 * nr,)), sem((3 * nr,)), sem((6,)), sem((6,)), sem((2,)), sem((1,))])
    outs = pl.pallas_call(
        body, name="mm_in", grid_spec=grid_spec,
        out_shape=(jax.ShapeDtypeStruct((s_n, m, ns), F32), jax.ShapeDtypeStruct(stk_w_in.shape, BF16))
        + tuple(jax.ShapeDtypeStruct(v.shape, v.dtype) for v in stk_rest)
        + tuple(jax.ShapeDtypeStruct((NCHIP,) + v.shape, v.dtype) for v in (conv_a_s, conv_f_s)),
        input_output_aliases={2 + w: 1 + w for w in range(nr + 1)},
        compiler_params=pltpu.CompilerParams(dimension_semantics=("arbitrary", "arbitrary"),
                                             vmem_limit_bytes=VMEM_BIG, has_side_effects=True),
    )(jc, a, stk_w_in, *stk_rest, conv_a_s, conv_f_s)
    return outs[0], outs[1], list(outs[2:2 + nr]), outs[2 + nr], outs[3 + nr]


def _mm_dh1_exchange(dp4, w_in3, p_w_in, x, dx1, gain, tm):
    s, k, ns = w_in3.shape
    m = dp4.shape[1]
    mt = m // tm

    def body(a_ref, b_ref, p_ref, x_ref, dx1_ref, g_ref, o_ref, q_ref, dg_ref, ssem, rsem):
        i, j = pl.program_id(0), pl.program_id(1)
        sends, arrive = _exchange_copies([q_ref], [lambda chip: p_ref.at[chip]], ssem, rsem)

        @pl.when((i == 0) & (j == 0))
        def _():
            for cp in sends:
                cp.start()

        @pl.when(j == 0)
        def _():
            o_ref[...] = _dot_nt(a_ref[...], b_ref[...])

        @pl.when(j > 0)
        def _():
            o_ref[...] += _dot_nt(a_ref[...], b_ref[...])

        @pl.when(j == s - 1)
        def _():
            dh = o_ref[...]
            r, n = _rms_stats(x_ref[...])
            o_ref[...] = dx1_ref[...] + _rms_bwd(dh * g_ref[...], n, r)
            dgv = jnp.sum(dh * n, axis=0, keepdims=True)

            @pl.when(i == 0)
            def _():
                dg_ref[...] = dgv

            @pl.when(i > 0)
            def _():
                dg_ref[...] += dgv

        @pl.when((i == mt - 1) & (j == s - 1))
        def _():
            for cp in arrive:
                cp.wait_recv()
            for cp in sends:
                cp.wait_send()

    any_spec = pl.BlockSpec(memory_space=pl.ANY)
    row_tile = pl.BlockSpec((tm, k), lambda i, j: (i, 0))
    vec = pl.BlockSpec((1, k), lambda i, j: (0, 0))
    return pl.pallas_call(
        body, name="mm_dh1",
        out_shape=(jax.ShapeDtypeStruct((m, k), F32), jax.ShapeDtypeStruct(p_w_in.shape, BF16),
                   jax.ShapeDtypeStruct((1, k), F32)),
        grid=(mt, s),
        in_specs=[pl.BlockSpec((None, tm, ns), lambda i, j: (_slot_of_chip(j), i, 0)),
                  pl.BlockSpec((None, k, ns), lambda i, j: (j, 0, 0)), any_spec, row_tile, row_tile, vec],
        out_specs=(row_tile, any_spec, vec),
        scratch_shapes=[pltpu.SemaphoreType.DMA((3,)), pltpu.SemaphoreType.DMA((3,))],
        compiler_params=pltpu.CompilerParams(dimension_semantics=("arbitrary", "arbitrary"),
                                             vmem_limit_bytes=VMEM_BIG, has_side_effects=True),
    )(dp4, w_in3, p_w_in, x, dx1, gain)


def _mm_nt_sharded(a, b3, tm, name, stacked_slot_fn=None):
    s, k, ns = b3.shape
    m = a.shape[1] if stacked_slot_fn is not None else a.shape[0]

    def body(a_ref, b_ref, o_ref):
        j = pl.program_id(1)
        @pl.when(j == 0)
        def _():
            o_ref[...] = _dot_nt(a_ref[...], b_ref[...])

        @pl.when(j > 0)
        def _():
            o_ref[...] += _dot_nt(a_ref[...], b_ref[...])

    if stacked_slot_fn is None:
        a_spec = pl.BlockSpec((tm, ns), lambda i, j: (i, j))
    else:
        a_spec = pl.BlockSpec((None, tm, ns), lambda i, j: (stacked_slot_fn(j), i, 0))
    return pl.pallas_call(
        body, name=name, out_shape=jax.ShapeDtypeStruct((m, k), F32), grid=(m // tm, s),
        in_specs=[a_spec, pl.BlockSpec((None, k, ns), lambda i, j: (j, 0, 0))],
        out_specs=pl.BlockSpec((tm, k), lambda i, j: (i, 0)),
        compiler_params=_params(("parallel", "arbitrary"), VMEM_BIG),
    )(a, b3)


def _mm_nt(a, b, out_dtype, tm, name):
    m, k = a.shape
    n = b.shape[0]

    def body(a_ref, b_ref, o_ref):
        o_ref[...] = _dot_nt(a_ref[...], b_ref[...]).astype(out_dtype)

    return pl.pallas_call(
        body, name=name, out_shape=jax.ShapeDtypeStruct((m, n), out_dtype), grid=(m // tm,),
        in_specs=[pl.BlockSpec((tm, k), lambda i: (i, 0)), pl.BlockSpec((n, k), lambda i: (0, 0))],
        out_specs=pl.BlockSpec((tm, n), lambda i: (i, 0)),
        compiler_params=_params(("parallel",), VMEM_BIG),
    )(a, b)


def _mm_tn(a, g, tkk, tn, tk, name, stacked_slot_fn=None, stacked_out=False):
    m, k = a.shape
    if stacked_slot_fn is not None:
        n = g.shape[0] * g.shape[2]
        g_spec = pl.BlockSpec((None, tk, tn), lambda kk, j, mm: (stacked_slot_fn(j), mm, 0))
    else:
        n = g.shape[1]
        g_spec = pl.BlockSpec((tk, tn), lambda kk, j, mm: (mm, j))
    steps = m // tk

    def body(a_ref, g_ref, o_ref, acc_ref):
        mm = pl.program_id(2)

        @pl.when(mm == 0)
        def _():
            acc_ref[...] = _dot_tn(a_ref[...], g_ref[...])

        @pl.when(mm > 0)
        def _():
            acc_ref[...] += _dot_tn(a_ref[...], g_ref[...])

        @pl.when(mm == steps - 1)
        def _():
            o_ref[...] = acc_ref[...].astype(BF16)

    if stacked_out:
        out_shape = jax.ShapeDtypeStruct((n // tn, k, tn), BF16)
        out_spec = pl.BlockSpec((None, tkk, tn), lambda kk, j, mm: (j, kk, 0))
    else:
        out_shape = jax.ShapeDtypeStruct((k, n), BF16)
        out_spec = pl.BlockSpec((tkk, tn), lambda kk, j, mm: (kk, j))
    return pl.pallas_call(
        body, name=name, out_shape=out_shape, grid=(k // tkk, n // tn, steps),
        in_specs=[pl.BlockSpec((tk, tkk), lambda kk, j, mm: (mm, kk)), g_spec],
        out_specs=out_spec,
        scratch_shapes=[pltpu.VMEM((tkk, tn), F32)],
        compiler_params=_params(("parallel", "parallel", "arbitrary"), VMEM_BIG),
    )(a, g)


def _norm_fwd(x, gain, tt):
    t = x.shape[0]

    def body(x_ref, g_ref, h_ref):
        _, n = _rms_stats(x_ref[...])
        h_ref[...] = (n * g_ref[...]).astype(BF16)

    return pl.pallas_call(
        body, name="norm_fwd", out_shape=jax.ShapeDtypeStruct((t, D), BF16), grid=(t // tt,),
        in_specs=[pl.BlockSpec((tt, D), lambda i: (i, 0)), pl.BlockSpec((1, D), lambda i: (0, 0))],
        out_specs=pl.BlockSpec((tt, D), lambda i: (i, 0)),
        compiler_params=_params(("parallel",)),
    )(x, gain)


def _lru_gates(xc, wa_ref, wx_ref, ba, bx, lam):
    xcb = xc.astype(BF16)
    ra = jnp.concatenate([_dot(xcb[:, n * HD:(n + 1) * HD], wa_ref[n]) for n in range(NH)], axis=1) + ba
    ix = jnp.concatenate([_dot(xcb[:, n * HD:(n + 1) * HD], wx_ref[n]) for n in range(NH)], axis=1) + bx
    r = _sig(ra)
    ig = _sig(ix)
    z = -lam
    sp = jnp.maximum(z, 0.0) + jnp.log1p(jnp.exp(-jnp.abs(z)))
    log_a = -LRU_C * r * sp
    a = jnp.exp(log_a)
    z2 = 2.0 * log_a
    series = -z2 * (1.0 + z2 * (0.5 + z2 * (1.0 / 6.0 + z2 * (1.0 / 24.0))))
    om = jnp.where(z2 > -0.02, series, 1.0 - jnp.exp(z2))
    mult = jnp.sqrt(om)
    return xcb, r, ig, sp, a, mult


def _mixer_a_fwd(p4, cw, cb, wa, wx, ba, bx, lam, tt):
    t = p4.shape[1]

    def body(p_ref, cw_ref, cb_ref, wa_ref, wx_ref, ba_ref, bx_ref, lam_ref, ya_ref, h_ref, sv_ref, halo, hc):
        i = pl.program_id(0)

        @pl.when(i == 0)
        def _():
            halo[...] = jnp.zeros((8, D), F32)
            hc[...] = jnp.zeros((1, D), F32)

        xa = p_ref[:, 0:D]
        ga = p_ref[:, D:2 * D]
        xe = jnp.concatenate([halo[...], xa], axis=0)
        xc = (cb_ref[...] + cw_ref[3:4, :] * xe
              + sum(cw_ref[3 - s:4 - s, :] * pltpu.roll(xe, s, 0) for s in (1, 2, 3)))[8:, :]
        halo[...] = xa[tt - 8:, :]
        _, r, ig, _, a, mult = _lru_gates(xc, wa_ref, wx_ref, ba_ref[...], bx_ref[...], lam_ref[...])
        u = mult * ig * xc
        h = _scan_down(a, u, hc[...])
        hc[...] = h[tt - 1:tt, :]
        h_ref[...] = h
        ya_ref[...] = (h * _gelu(ga)).astype(BF16)
        for idx, val in enumerate((xc, r, ig, a, mult)):
            sv_ref[idx] = val

    full = lambda shape: pl.BlockSpec(shape, lambda i: (0,) * len(shape))
    return pl.pallas_call(
        body, name="mixer_a_fwd",
        out_shape=(jax.ShapeDtypeStruct((t, D), BF16), jax.ShapeDtypeStruct((t, D), F32),
                   jax.ShapeDtypeStruct((5, t, D), F32)),
        grid=(t // tt,),
        in_specs=[pl.BlockSpec((None, tt, 2 * D), lambda i: (SLOT_A, i, 0)),
                  full((4, D)), full((1, D)), full((NH, HD, HD)), full((NH, HD, HD)),
                  full((1, D)), full((1, D)), full((1, D))],
        out_specs=(pl.BlockSpec((tt, D), lambda i: (i, 0)), pl.BlockSpec((tt, D), lambda i: (i, 0)),
                   pl.BlockSpec((5, tt, D), lambda i: (0, i, 0))),
        scratch_shapes=[pltpu.VMEM((8, D), F32), pltpu.VMEM((1, D), F32)],
        compiler_params=_params(("arbitrary",), VMEM_BIG),
    )(p4, cw, cb, wa, wx, ba, bx, lam)


def _chunk_masks(tt):
    row = lax.broadcasted_iota(jnp.int32, (tt, tt), 0)
    col = lax.broadcasted_iota(jnp.int32, (tt, tt), 1)
    same = jnp.right_shift(row, 5) == jnp.right_shift(col, 5)
    return same & (col <= row)


def _hg_head_fwd(q, fz, lbh, saved=None):
    sgn = _sig(-fz)
    k = (1.0 - lbh) * sgn
    if saved is None:
        sg = _sig(fz)
        f = lbh + (1.0 - lbh) * sg
        g = _chunk_cumsum(jnp.log(f))
    else:
        sg, g = saved
        f = lbh + (1.0 - lbh) * sg
    gu = _chunk_last(g) - g
    eg = jnp.exp(g)
    eng = jnp.exp(-g)
    egu = jnp.exp(gu)
    qt = q * eg
    kt = k * eng
    kd = k * egu
    return sg, sgn, f, k, g, eg, eng, egu, qt, kt, kd


def _lb_of(logits_ref):
    return _sig(logits_ref[0:1, :] - logits_ref[1:2, :])


def _hgrn2_fwd(p4, logits, gnorm, tt):
    t = p4.shape[1]
    nc = tt // CH

    def body(p_ref, lg_ref, gn_ref, yb_ref, o_ref, ss_ref, sv_ref, st):
        i = pl.program_id(0)

        @pl.when(i == 0)
        def _():
            st[...] = jnp.zeros((NH, HD, HD), F32)

        low = _chunk_masks(tt)
        lb = _lb_of(lg_ref)
        heads = [slice(h * HD, (h + 1) * HD) for h in range(NH)]
        sg, _, _, _, g, _, _, _, qt, kt, kd = _hg_head_fwd(p_ref[0, :, 0:D], p_ref[0, :, D:2 * D], lb)
        sv_ref[0] = sg
        sv_ref[1] = g
        qtb, ktb, kdb, vb = qt.astype(BF16), kt.astype(BF16), kd.astype(BF16), p_ref[1, :, 0:D].astype(BF16)
        decs = [jnp.exp(g[c * CH + CH - 1:c * CH + CH, :]) for c in range(nc)]
        o_in = []
        for hs in heads:
            att = jnp.where(low, _dot_nt(qtb[:, hs], ktb[:, hs]), 0.0)
            o_in.append(_dot(att.astype(BF16), vb[:, hs]))
        s_t = [st[h] for h in range(NH)]
        pieces = [[None] * nc for _ in range(NH)]
        for c in range(nc):
            sl = slice(c * CH, (c + 1) * CH)
            for h, hs in enumerate(heads):
                s_bf = s_t[h].astype(BF16)
                ss_ref[c, h] = s_bf
                pieces[h][c] = o_in[h][sl] + _dot_nt(qtb[sl, hs], s_bf)
                s_t[h] = s_t[h] * decs[c][:, hs] + _dot_tn(vb[sl, hs], kdb[sl, hs])
        for h, hs in enumerate(heads):
            st[h] = s_t[h]
            o = jnp.concatenate(pieces[h], axis=0)
            _, n = _rms_stats(o)
            og = p_ref[1, :, D + h * HD:D + (h + 1) * HD]
            o_ref[:, hs] = o
            yb_ref[:, hs] = (n * gn_ref[:, hs] * (og * _sig(og))).astype(BF16)

    return pl.pallas_call(
        body, name="hgrn2_fwd",
        out_shape=(jax.ShapeDtypeStruct((t, D), BF16), jax.ShapeDtypeStruct((t, D), F32),
                   jax.ShapeDtypeStruct((t // CH, NH, HD, HD), BF16), jax.ShapeDtypeStruct((2, t, D), F32)),
        grid=(t // tt,),
        in_specs=[pl.BlockSpec((2, tt, 2 * D), lambda i: (0, i, 0)),
                  pl.BlockSpec((2, D), lambda i: (0, 0)), pl.BlockSpec((1, D), lambda i: (0, 0))],
        out_specs=(pl.BlockSpec((tt, D), lambda i: (i, 0)), pl.BlockSpec((tt, D), lambda i: (i, 0)),
                   pl.BlockSpec((nc, NH, HD, HD), lambda i: (i, 0, 0, 0)),
                   pl.BlockSpec((2, tt, D), lambda i: (0, i, 0))),
        scratch_shapes=[pltpu.VMEM((NH, HD, HD), F32)],
        compiler_params=_params(("arbitrary",), VMEM_BIG),
    )(p4, logits, gnorm)


def _mid_fwd(ya, yb, p4, x, wa, wb, wo, g_pm, g_pf, tt):
    t = x.shape[0]

    def body(ya_ref, yb_ref, gt_ref, x_ref, wa_ref, wb_ref, wo_ref, gpm_ref, gpf_ref,
             za_ref, zb_ref, mix_ref, m2_ref, x1_ref, h2_ref):
        za = _dot(ya_ref[...], wa_ref[...])
        zb = _dot(yb_ref[...], wb_ref[...])
        mix = _sig(gt_ref[:, 0:D]) * za + _sig(gt_ref[:, D:2 * D]) * zb
        mixb = mix.astype(BF16)
        m2 = _dot(mixb, wo_ref[...])
        _, n2 = _rms_stats(m2)
        x1 = x_ref[...] + n2 * gpm_ref[...]
        _, n1 = _rms_stats(x1)
        za_ref[...] = za.astype(BF16)
        zb_ref[...] = zb.astype(BF16)
        mix_ref[...] = mixb
        m2_ref[...] = m2
        x1_ref[...] = x1
        h2_ref[...] = (n1 * gpf_ref[...]).astype(BF16)

    row = lambda dt: jax.ShapeDtypeStruct((t, D), dt)
    tile = pl.BlockSpec((tt, D), lambda i: (i, 0))
    wsp = pl.BlockSpec((D, D), lambda i: (0, 0))
    vec = pl.BlockSpec((1, D), lambda i: (0, 0))
    return pl.pallas_call(
        body, name="mid_fwd",
        out_shape=(row(BF16), row(BF16), row(BF16), row(F32), row(F32), row(BF16)),
        grid=(t // tt,),
        in_specs=[tile, tile, pl.BlockSpec((None, tt, 2 * D), lambda i: (SLOT_G, i, 0)), tile,
                  wsp, wsp, wsp, vec, vec],
        out_specs=(tile,) * 6,
        compiler_params=_params(("parallel",), VMEM_BIG),
    )(ya, yb, p4, x, wa, wb, wo, g_pm, g_pf)


def _up_act_fwd(h2, w_up4, cfw, cfb, tm):
    t = h2.shape[0]
    ns = SH_UP

    def body(a_ref, ah_ref, wg_ref, wv_ref, cwg_ref, cwv_ref, cbg_ref, cbv_ref,
             pg_ref, pv_ref, y_ref, uv_ref, gl_ref, dgl_ref):
        i = pl.program_id(1)
        rows = jnp.concatenate([ah_ref[...], a_ref[...]], axis=0)
        ups = []
        for w_ref, cw_ref, cb_ref, pre_ref in ((wg_ref, cwg_ref, cbg_ref, pg_ref), (wv_ref, cwv_ref, cbv_ref, pv_ref)):
            pre = _dot(rows, w_ref[...])
            pre_ref[...] = pre[16:, :]
            xe = jnp.concatenate([jnp.where(i > 0, pre[8:16, :], 0.0), pre[16:, :]], axis=0)
            up = (cb_ref[...] + cw_ref[2:3, :] * xe + cw_ref[1:2, :] * pltpu.roll(xe, 1, 0)
                  + cw_ref[0:1, :] * pltpu.roll(xe, 2, 0))
            ups.append(up[8:, :])
        gl, dgl = _gelu_and_grad(ups[0])
        y_ref[...] = (gl * ups[1]).astype(BF16)
        uv_ref[...] = ups[1].astype(BF16)
        gl_ref[...] = gl.astype(BF16)
        dgl_ref[...] = dgl.astype(BF16)

    hb = tm // 16
    tile = pl.BlockSpec((tm, ns), lambda p, i: (i, p))
    return pl.pallas_call(
        body, name="up_act_fwd",
        out_shape=(jax.ShapeDtypeStruct((t, DFF), F32),) * 2 + (jax.ShapeDtypeStruct((t, DFF), BF16),) * 4,
        grid=(2, t // tm),
        in_specs=[pl.BlockSpec((tm, D), lambda p, i: (i, 0)),
                  pl.BlockSpec((16, D), lambda p, i: (jnp.maximum(i * hb - 1, 0), 0)),
                  pl.BlockSpec((None, D, ns), lambda p, i: (p, 0, 0)),
                  pl.BlockSpec((None, D, ns), lambda p, i: (p + 2, 0, 0)),
                  pl.BlockSpec((3, ns), lambda p, i: (0, p)), pl.BlockSpec((3, ns), lambda p, i: (0, p + 2)),
                  pl.BlockSpec((1, ns), lambda p, i: (0, p)), pl.BlockSpec((1, ns), lambda p, i: (0, p + 2))],
        out_specs=(tile,) * 6,
        compiler_params=_params(("parallel", "parallel"), VMEM_BIG),
    )(h2, h2, w_up4, w_up4, cfw, cfw, cfb, cfb)


def _down_loss(y, wdn, x1, tgt, g_post, tt):
    t = x1.shape[0]

    def body(y_ref, w_ref, x1_ref, t_ref, g_ref, dx2_ref, dm3_ref, lossv_ref, dg_ref):
        i = pl.program_id(0)
        m3 = _dot(y_ref[...], w_ref[...])
        r, n3 = _rms_stats(m3)
        g = g_ref[...]
        e = x1_ref[...] + n3 * g - t_ref[...]
        dx2 = e * (1.0 / D)
        dx2_ref[...] = dx2
        dm3_ref[...] = _rms_bwd(dx2 * g, n3, r).astype(BF16)
        lv = jnp.sum(e * e, axis=0, keepdims=True)
        dgv = jnp.sum(dx2 * n3, axis=0, keepdims=True)

        @pl.when(i == 0)
        def _():
            lossv_ref[...] = lv
            dg_ref[...] = dgv

        @pl.when(i > 0)
        def _():
            lossv_ref[...] += lv
            dg_ref[...] += dgv

    tile = pl.BlockSpec((tt, D), lambda i: (i, 0))
    vec = pl.BlockSpec((1, D), lambda i: (0, 0))
    return pl.pallas_call(
        body, name="down_loss",
        out_shape=(jax.ShapeDtypeStruct((t, D), F32), jax.ShapeDtypeStruct((t, D), BF16),
                   jax.ShapeDtypeStruct((1, D), F32), jax.ShapeDtypeStruct((1, D), F32)),
        grid=(t // tt,),
        in_specs=[pl.BlockSpec((tt, DFF), lambda i: (i, 0)), pl.BlockSpec((DFF, D), lambda i: (0, 0)),
                  tile, tile, vec],
        out_specs=(tile, tile, vec, vec),
        compiler_params=_params(("arbitrary",), VMEM_BIG),
    )(y, wdn, x1, tgt, g_post)


def _ffn_act_bwd(dy, pre_g, pre_v, uv, gl, dgl, cfw, tt):
    t = dy.shape[0]
    nt = t // tt

    def body(dy_ref, dyn_ref, pg_ref, pv_ref, uv_ref, uvn_ref, gl_ref, gln_ref, dgl_ref, dgln_ref, cw_ref,
             du_ref, dcw_ref, dcb_ref):
        i = pl.program_id(0)
        n = tt + 8
        next_live = jnp.where(i < nt - 1, 1.0, 0.0)
        ext = lambda ref, nref: jnp.concatenate([ref[...].astype(F32), nref[...].astype(F32)[0:8, :]], axis=0)
        dy = jnp.concatenate([dy_ref[...].astype(F32), dyn_ref[...].astype(F32)[0:8, :] * next_live], axis=0)
        ds = (dy * ext(uv_ref, uvn_ref) * ext(dgl_ref, dgln_ref), dy * ext(gl_ref, gln_ref))
        dcw_parts, dcb_parts = [], []
        for hh, c0 in enumerate((0, DFF)):
            cs = slice(c0, c0 + DFF)
            dd = ds[hh]
            d1 = pltpu.roll(dd, n - 1, 0)
            d2 = pltpu.roll(dd, n - 2, 0)
            du_ref[:, cs] = (cw_ref[2:3, cs] * dd + cw_ref[1:2, cs] * d1 + cw_ref[0:1, cs] * d2)[0:tt, :].astype(BF16)
            x = (pg_ref, pv_ref)[hh][...]
            dcw_parts.append(jnp.concatenate(
                [jnp.sum(dk[0:tt, :] * x, axis=0, keepdims=True) for dk in (d2, d1, dd)], axis=0))
            dcb_parts.append(jnp.sum(dd[0:tt, :], axis=0, keepdims=True))
        dcw = jnp.concatenate(dcw_parts, axis=1)
        dcb = jnp.concatenate(dcb_parts, axis=1)

        @pl.when(i == 0)
        def _():
            dcw_ref[...] = dcw
            dcb_ref[...] = dcb

        @pl.when(i > 0)
        def _():
            dcw_ref[...] += dcw
            dcb_ref[...] += dcb

    half = pl.BlockSpec((tt, DFF), lambda i: (i, 0))
    half_next = pl.BlockSpec((16, DFF), lambda i: (jnp.minimum((i + 1) * (tt // 16), t // 16 - 1), 0))
    return pl.pallas_call(
        body, name="ffn_act_bwd",
        out_shape=(jax.ShapeDtypeStruct((t, DUP), BF16), jax.ShapeDtypeStruct((3, DUP), F32),
                   jax.ShapeDtypeStruct((1, DUP), F32)),
        grid=(nt,),
        in_specs=[half, half_next, half, half,
                  half, half_next, half, half_next, half, half_next,
                  pl.BlockSpec((3, DUP), lambda i: (0, 0))],
        out_specs=(pl.BlockSpec((tt, DUP), lambda i: (i, 0)), pl.BlockSpec((3, DUP), lambda i: (0, 0)),
                   pl.BlockSpec((1, DUP), lambda i: (0, 0))),
        compiler_params=_params(("arbitrary",), VMEM_BIG),
    )(dy, dy, pre_g, pre_v, uv, uv, gl, gl, dgl, dgl, cfw)


def _mid_bwd(dh2, dx2, x1, m2, za, zb, p4, wa, wb, wo, g_pm, g_pf, tt):
    t = x1.shape[0]

    def body(dh2_ref, dx2_ref, x1_ref, m2_ref, za_ref, zb_ref, gt_ref, wa_ref, wb_ref, wo_ref, gpm_ref, gpf_ref,
             dx1_ref, dm2_ref, dza_ref, dzb_ref, dya_ref, dyb_ref, dp_ref, dgpm_ref, dgpf_ref):
        i = pl.program_id(0)
        r1, n1 = _rms_stats(x1_ref[...])
        dh2 = dh2_ref[...]
        dx1 = dx2_ref[...] + _rms_bwd(dh2 * gpf_ref[...], n1, r1)
        r2, n2 = _rms_stats(m2_ref[...])
        dm2 = _rms_bwd(dx1 * gpm_ref[...], n2, r2).astype(BF16)
        dmix = _dot_nt(dm2, wo_ref[...])
        sa = _sig(gt_ref[:, 0:D])
        sb = _sig(gt_ref[:, D:2 * D])
        dza = (dmix * sa).astype(BF16)
        dzb = (dmix * sb).astype(BF16)
        dp_ref[:, 0:D] = (dmix * za_ref[...].astype(F32) * sa * (1.0 - sa)).astype(BF16)
        dp_ref[:, D:2 * D] = (dmix * zb_ref[...].astype(F32) * sb * (1.0 - sb)).astype(BF16)
        dx1_ref[...] = dx1
        dm2_ref[...] = dm2
        dza_ref[...] = dza
        dzb_ref[...] = dzb
        dya_ref[...] = _dot_nt(dza, wa_ref[...])
        dyb_ref[...] = _dot_nt(dzb, wb_ref[...])
        dgpf = jnp.sum(dh2 * n1, axis=0, keepdims=True)
        dgpm = jnp.sum(dx1 * n2, axis=0, keepdims=True)

        @pl.when(i == 0)
        def _():
            dgpf_ref[...] = dgpf
            dgpm_ref[...] = dgpm

        @pl.when(i > 0)
        def _():
            dgpf_ref[...] += dgpf
            dgpm_ref[...] += dgpm

    row = lambda dt: jax.ShapeDtypeStruct((t, D), dt)
    tile = pl.BlockSpec((tt, D), lambda i: (i, 0))
    wsp = pl.BlockSpec((D, D), lambda i: (0, 0))
    vec = pl.BlockSpec((1, D), lambda i: (0, 0))
    gates = pl.BlockSpec((None, tt, 2 * D), lambda i: (SLOT_G, i, 0))
    return pl.pallas_call(
        body, name="mid_bwd",
        out_shape=(row(F32), row(BF16), row(BF16), row(BF16), row(F32), row(F32),
                   jax.ShapeDtypeStruct((NCHIP, t, 2 * D), BF16),
                   jax.ShapeDtypeStruct((1, D), F32), jax.ShapeDtypeStruct((1, D), F32)),
        grid=(t // tt,),
        in_specs=[tile, tile, tile, tile, tile, tile, gates, wsp, wsp, wsp, vec, vec],
        out_specs=(tile, tile, tile, tile, tile, tile, gates, vec, vec),
        compiler_params=_params(("arbitrary",), VMEM_BIG),
    )(dh2, dx2, x1, m2, za, zb, p4, wa, wb, wo, g_pm, g_pf)


def _hgrn2_bwd(p4, o_all, ss, saved, dyb, dp4, logits, gnorm, p_early, tt):
    t = p4.shape[1]
    nt = t // tt
    nc = tt // CH
    ne = len(p_early)

    def body(p_ref, o_ref, ss_ref, sv_ref, dyb_ref, dp_in, lg_ref, gn_ref, *rest):
        del dp_in
        pe = rest[:ne]
        dp_ref, dlb_ref, dgn_ref = rest[ne:ne + 3]
        qe = rest[ne + 3:2 * ne + 3]
        dst, ssem, rsem = rest[2 * ne + 3:]
        i = pl.program_id(0)
        sends, arrive = _exchange_copies(qe, [(lambda chip, r=r: r.at[chip]) for r in pe], ssem, rsem)

        @pl.when(i == 0)
        def _():
            dst[...] = jnp.zeros((NH, HD, HD), F32)
            for cp in sends:
                cp.start()

        low = _chunk_masks(tt)
        lb = _lb_of(lg_ref)
        heads = [slice(h * HD, (h + 1) * HD) for h in range(NH)]
        sg, sgn, f, k, g, eg, eng, egu, qt, kt, kd = _hg_head_fwd(p_ref[0, :, 0:D], p_ref[0, :, D:2 * D], lb,
                                                                  (sv_ref[0], sv_ref[1]))
        qtb, ktb, kdb, vb = qt.astype(BF16), kt.astype(BF16), kd.astype(BF16), p_ref[1, :, 0:D].astype(BF16)
        decs = [jnp.exp(g[c * CH + CH - 1:c * CH + CH, :]) for c in range(nc)]
        og = p_ref[1, :, D:2 * D]
        so = _sig(og)
        dyb = dyb_ref[...]
        dob = dyb * (og * so)
        rn = [_rms_stats(o_ref[:, hs]) for hs in heads]
        r_all = jnp.concatenate([jnp.broadcast_to(r, (tt, HD)) for r, _ in rn], axis=1)
        n_all = jnp.concatenate([n for _, n in rn], axis=1)
        gd = dob * gn_ref[...]
        proj = jnp.concatenate(
            [jnp.broadcast_to(jnp.mean(gd[:, hs] * n_all[:, hs], axis=-1, keepdims=True), (tt, HD)) for hs in heads],
            axis=1)
        dob_ = (r_all * (gd - n_all * proj)).astype(BF16)
        dog = dyb * (n_all * gn_ref[...]) * (so * (1.0 + og * (1.0 - so)))
        dgn = jnp.sum(dob * n_all, axis=0, keepdims=True)
        dv_in, dqt_in, dkt_h = [], [], []
        for hs in heads:
            att = jnp.where(low, _dot_nt(qtb[:, hs], ktb[:, hs]), 0.0).astype(BF16)
            d_att = jnp.where(low, _dot_nt(dob_[:, hs], vb[:, hs]), 0.0).astype(BF16)
            dv_in.append(_dot_tn(att, dob_[:, hs]))
            dqt_in.append(_dot(d_att, ktb[:, hs]))
            dkt_h.append(_dot_tn(d_att, qtb[:, hs]))
        ds_t = [dst[h] for h in range(NH)]
        dv_p = [[None] * NH for _ in range(nc)]
        dqt_p = [[None] * NH for _ in range(nc)]
        dkd_p = [[None] * NH for _ in range(nc)]
        dgl_p = [[None] * NH for _ in range(nc)]
        for c in reversed(range(nc)):
            sl = slice(c * CH, (c + 1) * CH)
            for h, hs in enumerate(heads):
                s_prev = ss_ref[c, h]
                ds_bf = ds_t[h].astype(BF16)
                dec = decs[c][:, hs]
                dv_p[c][h] = dv_in[h][sl] + _dot_nt(kdb[sl, hs], ds_bf)
                dqt_p[c][h] = dqt_in[h][sl] + _dot(dob_[sl, hs], s_prev)
                dkd_p[c][h] = _dot(vb[sl, hs], ds_bf)
                ddec = jnp.sum(s_prev.astype(F32) * ds_t[h], axis=0, keepdims=True)
                dgl_p[c][h] = jnp.broadcast_to(ddec * dec, (CH, HD))
                ds_t[h] = ds_t[h] * dec + _dot_tn(dob_[sl, hs], qtb[sl, hs])
        for h in range(NH):
            dst[h] = ds_t[h]
        whole = lambda parts: jnp.concatenate([jnp.concatenate(row, axis=1) for row in parts], axis=0)
        dv, dqt, dkd, dgl = whole(dv_p), whole(dqt_p), whole(dkd_p), whole(dgl_p)
        dkt = jnp.concatenate(dkt_h, axis=1)
        dq = dqt * eg
        dk = dkt * eng + dkd * egu
        dg = dqt * qt - dkt * kt
        dgu = dkd * kd
        dlogf = _chunk_revcumsum(dg - dgu) + _chunk_total(dgu) + dgl
        common = sgn * (dlogf / f - dk)
        dfz = (1.0 - lb) * sg * common
        dlb = jnp.sum(common, axis=0, keepdims=True)
        dp_ref[0, :, 0:D] = dq.astype(BF16)
        dp_ref[0, :, D:2 * D] = dfz.astype(BF16)
        dp_ref[1, :, 0:D] = dv.astype(BF16)
        dp_ref[1, :, D:2 * D] = dog.astype(BF16)

        @pl.when(i == 0)
        def _():
            dlb_ref[0:1, :] = dlb
            dgn_ref[...] = dgn

        @pl.when(i > 0)
        def _():
            dlb_ref[0:1, :] += dlb
            dgn_ref[...] += dgn

        @pl.when(i == nt - 1)
        def _():
            d0 = dlb_ref[0:1, :] * lb * (1.0 - lb)
            dlb_ref[0:1, :] = d0
            dlb_ref[1:2, :] = -d0
            for cp in arrive:
                cp.wait_recv()
            for cp in sends:
                cp.wait_send()

    rev = lambda i: nt - 1 - i
    vec = pl.BlockSpec((1, D), lambda i: (0, 0))
    any_spec = pl.BlockSpec(memory_space=pl.ANY)
    outs = pl.pallas_call(
        body, name="hgrn2_bwd",
        out_shape=(jax.ShapeDtypeStruct(dp4.shape, BF16), jax.ShapeDtypeStruct((2, D), F32),
                   jax.ShapeDtypeStruct((1, D), F32)) + tuple(jax.ShapeDtypeStruct(a.shape, BF16) for a in p_early),
        grid=(nt,),
        in_specs=[pl.BlockSpec((2, tt, 2 * D), lambda i: (0, rev(i), 0)),
                  pl.BlockSpec((tt, D), lambda i: (rev(i), 0)),
                  pl.BlockSpec((nc, NH, HD, HD), lambda i: (rev(i), 0, 0, 0)),
                  pl.BlockSpec((2, tt, D), lambda i: (0, rev(i), 0)),
                  pl.BlockSpec((tt, D), lambda i: (rev(i), 0)),
                  any_spec,
                  pl.BlockSpec((2, D), lambda i: (0, 0)), vec] + [any_spec] * ne,
        out_specs=(pl.BlockSpec((2, tt, 2 * D), lambda i: (0, rev(i), 0)),
                   pl.BlockSpec((2, D), lambda i: (0, 0)), vec) + (any_spec,) * ne,
        scratch_shapes=[pltpu.VMEM((NH, HD, HD), F32), pltpu.SemaphoreType.DMA((3 * ne,)),
                        pltpu.SemaphoreType.DMA((3 * ne,))],
        input_output_aliases={5: 0},
        compiler_params=pltpu.CompilerParams(dimension_semantics=("arbitrary",), vmem_limit_bytes=VMEM_BIG,
                                             has_side_effects=True),
    )(p4, o_all, ss, saved, dyb, dp4, logits, gnorm, *p_early)
    return outs[0], outs[1], outs[2], list(outs[3:])


def _mixer_a_bwd(p4, hseq, saved, dya, dp4, cw, wa, wx, lam, tt):
    t = p4.shape[1]
    nt = t // tt

    def body(p_ref, sv_ref, h_ref, hh_ref, dya_ref, dp_in, cw_ref, wa_ref, wx_ref, lam_ref,
             dp_ref, dcw_ref, dcb_ref, dwa_ref, dwx_ref, dba_ref, dbx_ref, dlam_ref,
             dnext, dhc, afc):
        del dp_in
        i = pl.program_id(0)
        first_tile = i == nt - 1

        @pl.when(i == 0)
        def _():
            dnext[...] = jnp.zeros((8, D), F32)
            dhc[...] = jnp.zeros((1, D), F32)
            afc[...] = jnp.zeros((1, D), F32)

        xa = p_ref[:, 0:D]
        ga = p_ref[:, D:2 * D]
        xc, r, ig, a, mult = (sv_ref[idx] for idx in range(5))
        xcb = xc.astype(BF16)
        lam = lam_ref[...]
        sp = jnp.maximum(-lam, 0.0) + jnp.log1p(jnp.exp(-jnp.abs(lam)))
        h = h_ref[...]
        gl, dgl = _gelu_and_grad(ga)
        dya = dya_ref[...]
        dga = dya * h * dgl
        rows = lax.broadcasted_iota(jnp.int32, (tt, 1), 0)
        a_next = jnp.where(rows == tt - 1, afc[...], pltpu.roll(a, tt - 1, 0))
        dh = _scan_up(a_next, dya * gl, dhc[...])
        dhc[...] = dh[0:1, :]
        afc[...] = a[0:1, :]
        h_prev = jnp.where(rows == 0, jnp.where(first_tile, 0.0, hh_ref[7:8, :]), pltpu.roll(h, 1, 0))
        da = dh * h_prev
        dmult = dh * ig * xc
        di = dh * mult * xc
        dlog_a = da * a - dmult * a * a / mult
        dr = dlog_a * (-LRU_C * sp)
        dsp = jnp.sum(dlog_a * (-LRU_C * r), axis=0, keepdims=True)
        dra = dr * r * (1.0 - r)
        dix = di * ig * (1.0 - ig)
        drab = dra.astype(BF16)
        dixb = dix.astype(BF16)
        dxc_lin = []
        dwa_new = []
        dwx_new = []
        for n in range(NH):
            cs = slice(n * HD, (n + 1) * HD)
            dxc_lin.append(_dot_nt(drab[:, cs], wa_ref[n]) + _dot_nt(dixb[:, cs], wx_ref[n]))
            dwa_new.append(_dot_tn(xcb[:, cs], drab[:, cs]))
            dwx_new.append(_dot_tn(xcb[:, cs], dixb[:, cs]))
        dxc = dh * mult * ig + jnp.concatenate(dxc_lin, axis=1)
        de = jnp.concatenate([dxc, dnext[...]], axis=0)
        ups = [de[0:tt, :]] + [pltpu.roll(de, tt + 8 - s, 0)[0:tt, :] for s in (1, 2, 3)]
        dxa = sum(cw_ref[3 - s:4 - s, :] * ups[s] for s in range(4))
        dnext[...] = dxc[0:8, :]
        dp_ref[:, 0:D] = dxa.astype(BF16)
        dp_ref[:, D:2 * D] = dga.astype(BF16)
        dcw = jnp.concatenate(
            [jnp.sum(ups[3 - k] * xa, axis=0, keepdims=True) for k in range(4)], axis=0)
        dcb = jnp.sum(dxc, axis=0, keepdims=True)
        dba = jnp.sum(dra, axis=0, keepdims=True)
        dbx = jnp.sum(dix, axis=0, keepdims=True)
        dlam = dsp * (-_sig(-lam))

        @pl.when(i == 0)
        def _():
            dcw_ref[...] = dcw
            dcb_ref[...] = dcb
            dba_ref[...] = dba
            dbx_ref[...] = dbx
            dlam_ref[...] = dlam
            for n in range(NH):
                dwa_ref[n] = dwa_new[n]
                dwx_ref[n] = dwx_new[n]

        @pl.when(i > 0)
        def _():
            dcw_ref[...] += dcw
            dcb_ref[...] += dcb
            dba_ref[...] += dba
            dbx_ref[...] += dbx
            dlam_ref[...] += dlam
            for n in range(NH):
                dwa_ref[n] += dwa_new[n]
                dwx_ref[n] += dwx_new[n]

    rev = lambda i: nt - 1 - i
    hb = tt // 8
    full = lambda shape: pl.BlockSpec(shape, lambda i: (0,) * len(shape))
    vecs = jax.ShapeDtypeStruct((1, D), F32)
    blk = jax.ShapeDtypeStruct((NH, HD, HD), F32)
    return pl.pallas_call(
        body, name="mixer_a_bwd",
        out_shape=(jax.ShapeDtypeStruct(dp4.shape, BF16), jax.ShapeDtypeStruct((4, D), F32), vecs, blk, blk,
                   vecs, vecs, vecs),
        grid=(nt,),
        in_specs=[pl.BlockSpec((None, tt, 2 * D), lambda i: (SLOT_A, rev(i), 0)),
                  pl.BlockSpec((5, tt, D), lambda i: (0, rev(i), 0)),
                  pl.BlockSpec((tt, D), lambda i: (rev(i), 0)),
                  pl.BlockSpec((8, D), lambda i: (jnp.maximum(rev(i) * hb - 1, 0), 0)),
                  pl.BlockSpec((tt, D), lambda i: (rev(i), 0)),
                  pl.BlockSpec(memory_space=pl.ANY),
                  full((4, D)), full((NH, HD, HD)), full((NH, HD, HD)), full((1, D))],
        out_specs=(pl.BlockSpec((None, tt, 2 * D), lambda i: (SLOT_A, rev(i), 0)),
                   full((4, D)), full((1, D)), full((NH, HD, HD)), full((NH, HD, HD)),
                   full((1, D)), full((1, D)), full((1, D))),
        scratch_shapes=[pltpu.VMEM((8, D), F32), pltpu.VMEM((1, D), F32), pltpu.VMEM((1, D), F32)],
        input_output_aliases={5: 0},
        compiler_params=_params(("arbitrary",), VMEM_BIG),
    )(p4, saved, hseq, hseq, dya, dp4, cw, wa, wx, lam)


def _local_step(x, tgt, stk_w_in, stk_rest, conv_a_s, conv_f_s, small, jc, cidx):
    t = x.shape[0]
    tt = min(256, t)
    tm = min(1024, t)
    tk = min(2048, t)
    wa_bf = small["lru_wa"].astype(BF16)
    wx_bf = small["lru_wx"].astype(BF16)

    h1 = _norm_fwd(x, small["norm_pre_mix"], tt)
    p4, w_in, stk_rest, conv_a_g, conv_f_g = _mm_in_gather(h1, stk_w_in, stk_rest, REST, conv_a_s, conv_f_s, jc, tm)
    conv_a_w = jnp.transpose(conv_a_g, (1, 0, 2)).reshape(8, D)[0:4]
    conv_f_w = jnp.transpose(conv_f_g, (1, 0, 2)).reshape(8, DUP)[0:3]
    w = dict(zip(REST, _gather_forward(stk_rest, REST)))
    w["w_in"] = w_in
    w_br_a = w["w_branch_a"].reshape(D, D)
    w_br_b = w["w_branch_b"].reshape(D, D)
    w_out = w["w_out"].reshape(D, D)
    w_down = w["w_down"].reshape(DFF, D)
    ya, hseq, saved_a = _mixer_a_fwd(p4, conv_a_w, small["conv_a_b"], wa_bf, wx_bf, small["lru_ba"], small["lru_bx"],
                            small["lru_lambda"], tt)
    yb, o_all, ss, saved_b = _hgrn2_fwd(p4, small["hg_lb_logits"], small["hg_norm_g"], tt)
    za, zb, mixb, m2, x1, h2 = _mid_fwd(ya, yb, p4, x, w_br_a, w_br_b, w_out, small["norm_post_mix"],
                                        small["norm_pre_ffn"], min(512, t))
    pre_g, pre_v, y, uv, gl, dgl = _up_act_fwd(h2, w["w_up"], conv_f_w, small["conv_f_b"], tt)
    dx2, dm3, lossv, d_norm_post_ffn = _down_loss(y, w_down, x1, tgt, small["norm_post_ffn"], min(512, t))

    d_w_down = _mm_tn(y, dm3, DFF // 2, D, tk, "mm_dw_down")
    dy = _mm_nt(dm3, w_down, BF16, tm, "mm_dy")
    dup_pre, d_conv_f_w, d_conv_f_b = _ffn_act_bwd(dy, pre_g, pre_v, uv, gl, dgl, conv_f_w, tt)
    d_w_up = _mm_tn(h2, dup_pre, D, SH_UP, tk, "mm_dw_up", stacked_out=True)
    dh2 = _mm_nt_sharded(dup_pre, w["w_up"], tm, "mm_dh2")
    dx1, dm2, dza, dzb, dya, dyb, dp4, d_norm_post_mix, d_norm_pre_ffn = _mid_bwd(
        dh2, dx2, x1, m2, za, zb, p4, w_br_a, w_br_b, w_out, small["norm_post_mix"], small["norm_pre_ffn"], tt)
    d_w_out = _mm_tn(mixb, dm2, D, D, tm, "mm_dw_out")
    d_w_br_a = _mm_tn(ya, dza, D, D, tm, "mm_dw_bra")
    d_w_br_b = _mm_tn(yb, dzb, D, D, tm, "mm_dw_brb")
    early = {"w_branch_a": d_w_br_a.reshape(NCHIP, SH_BR, D), "w_branch_b": d_w_br_b.reshape(NCHIP, SH_BR, D),
             "w_out": d_w_out.reshape(NCHIP, SH_BR, D), "w_up": d_w_up, "w_down": d_w_down.reshape(NCHIP, SH_DN, D)}
    rb, _ = _reduce_stage1(early, REST, (), "reduce_d2d_in_early")
    p_rest = [_sum_own_half(early[n], rb[n], cidx, "sum_half_" + n) for n in REST]
    dp4, d_lb, d_hg_norm_g, q_rest = _hgrn2_bwd(p4, o_all, ss, saved_b, dyb, dp4, small["hg_lb_logits"], small["hg_norm_g"],
                                                p_rest, tt)
    dp4, d_conv_a_w, d_conv_a_b, d_lru_wa, d_lru_wx, d_lru_ba, d_lru_bx, d_lru_lambda = _mixer_a_bwd(
        p4, hseq, saved_a, dya, dp4, conv_a_w, wa_bf, wx_bf, small["lru_lambda"], tt)
    d_w_in = _mm_tn(h1, dp4, D, SH_IN, tk, "mm_dw_in", stacked_slot_fn=_slot_of_chip, stacked_out=True)
    rb, _ = _reduce_stage1({"w_in": d_w_in}, ("w_in",), (), "reduce_d2d_in_w_in")
    p_w_in = _sum_own_half(d_w_in, rb["w_in"], cidx, "sum_half_w_in")
    grad_x, q_w_in, d_norm_pre_mix = _mm_dh1_exchange(dp4, w_in, p_w_in, x, dx1, small["norm_pre_mix"], tm)

    smalls = {
        "norm_pre_mix": d_norm_pre_mix, "conv_a_b": d_conv_a_b, "lru_ba": d_lru_ba, "lru_bx": d_lru_bx,
        "lru_lambda": d_lru_lambda, "hg_lb_logits": d_lb, "hg_norm_g": d_hg_norm_g, "norm_post_mix": d_norm_post_mix,
        "norm_pre_ffn": d_norm_pre_ffn, "norm_post_ffn": d_norm_post_ffn, "lossv": lossv,
        "conv_a_w": d_conv_a_w, "lru_wa": d_lru_wa, "lru_wx": d_lru_wx,
        "conv_f_b": d_conv_f_b, "conv_f_w": d_conv_f_w,
    }
    p_big = dict(zip(REST, p_rest), w_in=p_w_in)
    q_big = dict(zip(REST, q_rest), w_in=q_w_in)
    return grad_x, p_big, q_big, smalls


BIG = ("w_in", "w_branch_a", "w_branch_b", "w_out", "w_up", "w_down")
BIG_SHAPE = {"w_in": (D, SH_IN), "w_branch_a": (SH_BR, D), "w_branch_b": (SH_BR, D), "w_out": (SH_BR, D),
             "w_up": (D, SH_UP), "w_down": (SH_DN, D)}
NBIG = len(BIG)
REST = BIG[1:]
VEC_ROWS = (("norm_pre_mix", 0, 1), ("conv_a_b", 1, 1), ("lru_ba", 2, 1), ("lru_bx", 3, 1), ("lru_lambda", 4, 1),
            ("hg_lb_logits", 5, 2), ("hg_norm_g", 7, 1), ("norm_post_mix", 8, 1), ("norm_pre_ffn", 9, 1),
            ("norm_post_ffn", 10, 1))
ROW_LOSS = 11
ROW_CONV_A = 12
S1_ROWS = 16
S2_ROWS = 8


def _place():
    x, y, c = lax.axis_index("x"), lax.axis_index("y"), lax.axis_index("c")
    chips = [(1 - x, y), (x, 1 - y), (1 - x, 1 - y)]
    return x, y, c, 2 * x + y, chips


def _remote(src, dst, ssem, rsem, dev):
    return pltpu.make_async_remote_copy(src_ref=src, dst_ref=dst, send_sem=ssem, recv_sem=rsem,
                                        device_id=dev, device_id_type=MESH)


def _hbm_call(body, name, ins, out_shapes, n_sems, aliases=None):
    any_spec = pl.BlockSpec(memory_space=pl.ANY)
    return pl.pallas_call(
        body, name=name, out_shape=tuple(out_shapes),
        in_specs=[any_spec] * len(ins), out_specs=tuple([any_spec] * len(out_shapes)),
        scratch_shapes=[pltpu.SemaphoreType.DMA((n,)) for n in n_sems],
        input_output_aliases=aliases or {},
        compiler_params=pltpu.CompilerParams(has_side_effects=True),
    )(*ins)


def _gather_copies(stk, names, ssem, rsem, fssem=None, frsem=None):
    x, y, c, j, chips = _place()
    sends, arrive, fwds, farrive = [], [], [], []
    for w, n in enumerate(names):
        hw = BIG_SHAPE[n][0] // 2
        mine = stk[w].at[j, pl.ds(c * hw, hw), :]
        for k, (cx, cy) in enumerate(chips):
            i = 3 * w + k
            got = stk[w].at[2 * cx + cy, pl.ds(c * hw, hw), :]
            other = stk[w].at[2 * cx + cy, pl.ds((1 - c) * hw, hw), :]
            sends.append(_remote(mine, mine, ssem.at[i], rsem.at[i], (cx, cy, c)))
            arrive.append(_remote(got, got, ssem.at[i], rsem.at[i], (cx, cy, c)))
            if fssem is not None:
                fwds.append(_remote(got, got, fssem.at[i], frsem.at[i], (x, y, 1 - c)))
                farrive.append(_remote(other, other, fssem.at[i], frsem.at[i], (x, y, 1 - c)))
    return sends, arrive, fwds, farrive


def _gather_forward(stk, names):
    nw = len(names)

    def body(*refs):
        dst = refs[nw:2 * nw]
        ssem, rsem, fssem, frsem = refs[2 * nw:]
        _, _, fwds, farrive = _gather_copies(dst, names, ssem, rsem, fssem, frsem)
        for cp in fwds:
            cp.start()
        for cp in farrive:
            cp.wait_recv()
        for cp in fwds:
            cp.wait_send()

    out_shapes = [jax.ShapeDtypeStruct(a.shape, a.dtype) for a in stk]
    return _hbm_call(body, "gather_forward", stk, out_shapes, (3 * nw,) * 4, aliases={w: w for w in range(nw)})


def _exchange_copies(dst, pieces, ssem, rsem):
    x, y, c, j, chips = _place()
    sends, arrive = [], []
    for w in range(len(dst)):
        for k, (cx, cy) in enumerate(chips):
            i = 3 * w + k
            sends.append(_remote(pieces[w](2 * cx + cy), dst[w].at[j], ssem.at[i], rsem.at[i], (cx, cy, c)))
            got = dst[w].at[2 * cx + cy]
            arrive.append(_remote(got, got, ssem.at[i], rsem.at[i], (cx, cy, c)))
    return sends, arrive


def _reduce_stage1(big_g, names, smalls, name):
    nb = len(names)
    ins = [big_g[n] for n in names] + list(smalls)
    n_in = len(ins)
    halves = [BIG_SHAPE[n][0] // 2 for n in names]
    out_shapes = [jax.ShapeDtypeStruct((NCHIP, halves[w], BIG_SHAPE[n][1]), big_g[n].dtype)
                  for w, n in enumerate(names)]
    out_shapes += [jax.ShapeDtypeStruct(a.shape, F32) for a in smalls]

    def body(*refs):
        src, dst = refs[:n_in], refs[n_in:2 * n_in]
        ssem, rsem = refs[2 * n_in:]
        x, y, c, _, _ = _place()
        cps = []
        for w in range(n_in):
            s_ = src[w].at[:, pl.ds((1 - c) * halves[w], halves[w]), :] if w < nb else src[w]
            cp = _remote(s_, dst[w], ssem.at[w], rsem.at[w], (x, y, 1 - c))
            cp.start()
            cps.append(cp)
        for cp in cps:
            cp.wait()

    outs = _hbm_call(body, name, ins, out_shapes, (n_in, n_in))
    return dict(zip(names, outs[:nb])), outs[nb:]


def _reduce_stage2(ps1, ps2, ps3):
    ins = [ps1, ps2, ps3]
    h1, h2, h3 = S1_ROWS // 2, DUP // 2, D
    out_shapes = [jax.ShapeDtypeStruct((NCHIP, h1, D), F32), jax.ShapeDtypeStruct((NCHIP, S2_ROWS, h2), F32),
                  jax.ShapeDtypeStruct((NCHIP, h3, HD), F32)]

    def body(*refs):
        src, dst = refs[:3], refs[3:6]
        ssem, rsem = refs[6:]
        c = lax.axis_index("c")
        pieces = [lambda chip: src[0].at[pl.ds(c * h1, h1), :],
                  lambda chip: src[1].at[:, pl.ds(c * h2, h2)],
                  lambda chip: src[2].at[pl.ds(c * h3, h3), :]]
        sends, arrive = _exchange_copies(dst, pieces, ssem, rsem)
        for cp in sends:
            cp.start()
        for cp in arrive:
            cp.wait_recv()
        for cp in sends:
            cp.wait_send()

    return _hbm_call(body, "reduce_ici_small", ins, out_shapes, (9, 9))


def _reduce_stage3(f_big, fs1, fs2, fs3):
    ins = [f_big[n] for n in BIG] + [fs1, fs2, fs3]
    n_in = len(ins)
    halves = [BIG_SHAPE[n][0] // 2 for n in BIG]
    h1, h2, h3 = S1_ROWS // 2, DUP // 2, D
    out_shapes = [jax.ShapeDtypeStruct(BIG_SHAPE[n], F32) for n in BIG]
    out_shapes += [jax.ShapeDtypeStruct((S1_ROWS, D), F32), jax.ShapeDtypeStruct((S2_ROWS, DUP), F32),
                   jax.ShapeDtypeStruct((2 * D, HD), F32)]

    def body(*refs):
        dst = refs[n_in:2 * n_in]
        ssem, rsem = refs[2 * n_in:]
        x, y, c, _, _ = _place()

        def place(w, which):
            if w < NBIG:
                return dst[w].at[pl.ds(which * halves[w], halves[w]), :]
            if w == NBIG:
                return dst[w].at[pl.ds(which * h1, h1), :]
            if w == NBIG + 1:
                return dst[w].at[:, pl.ds(which * h2, h2)]
            return dst[w].at[pl.ds(which * h3, h3), :]

        cps = [_remote(place(w, c), place(w, c), ssem.at[w], rsem.at[w], (x, y, 1 - c)) for w in range(n_in)]
        for cp in cps:
            cp.start()
        for w in range(n_in):
            got = place(w, 1 - c)
            _remote(got, got, ssem.at[w], rsem.at[w], (x, y, 1 - c)).wait_recv()
        for cp in cps:
            cp.wait_send()

    outs = _hbm_call(body, "reduce_d2d_out", ins, out_shapes, (n_in, n_in), aliases={w: w for w in range(n_in)})
    return dict(zip(BIG, outs[:NBIG])), outs[NBIG], outs[NBIG + 1], outs[NBIG + 2]


def _row_tile(rows):
    for tr in (128, 176, 64, 16, 8):
        if rows % tr == 0:
            return tr
    return rows


def _sum_own_half(g, rb, cidx, name):
    s, rows, cols = g.shape
    half = rows // 2
    tr = _row_tile(half)
    nb = half // tr

    def body(c_ref, g_ref, r_ref, o_ref):
        del c_ref
        o_ref[...] = (g_ref[...].astype(F32) + r_ref[...].astype(F32)).astype(BF16)

    grid_spec = pltpu.PrefetchScalarGridSpec(
        num_scalar_prefetch=1, grid=(s, nb),
        in_specs=[pl.BlockSpec((None, tr, cols), lambda k, i, c: (k, c[0] * nb + i, 0)),
                  pl.BlockSpec((None, tr, cols), lambda k, i, c: (k, i, 0))],
        out_specs=pl.BlockSpec((None, tr, cols), lambda k, i, c: (k, i, 0)))
    return pl.pallas_call(
        body, name=name, grid_spec=grid_spec, out_shape=jax.ShapeDtypeStruct((s, half, cols), BF16),
        compiler_params=_params(("parallel", "parallel")),
    )(cidx, g, rb)


def _sum_chips(q, p, jc, name, by_cols=False):
    s, rows, cols = q.shape
    tr = _row_tile(rows)
    nb = rows // tr
    stacked = p.ndim == 3

    def body(jc_ref, q_ref, p_ref, o_ref):
        j = jc_ref[0]
        own = p_ref[...].astype(F32)
        acc = None
        for k in range(NCHIP):
            term = jnp.where(j == k, own, q_ref[k].astype(F32))
            acc = term if acc is None else acc + term
        o_ref[...] = acc

    if by_cols:
        half_spec = pl.BlockSpec((tr, cols), lambda i, jc_ref: (i, jc_ref[1]))
        out_shape = jax.ShapeDtypeStruct((rows, 2 * cols), F32)
    else:
        half_spec = pl.BlockSpec((tr, cols), lambda i, jc_ref: (jc_ref[1] * nb + i, 0))
        out_shape = jax.ShapeDtypeStruct((2 * rows, cols), F32)
    p_spec = pl.BlockSpec((None, tr, cols), lambda i, jc_ref: (jc_ref[0], i, 0)) if stacked else half_spec
    grid_spec = pltpu.PrefetchScalarGridSpec(
        num_scalar_prefetch=1, grid=(nb,),
        in_specs=[pl.BlockSpec((s, tr, cols), lambda i, jc_ref: (0, i, 0)), p_spec],
        out_specs=half_spec)
    return pl.pallas_call(
        body, name=name, grid_spec=grid_spec, out_shape=out_shape,
        compiler_params=_params(("parallel",)),
    )(jc, q, p)


def _place_shard(w, jc, name):
    rows, cols = w.shape
    tr = _row_tile(rows)

    def body(jc_ref, w_ref, o_ref):
        del jc_ref
        o_ref[...] = w_ref[...].astype(BF16)

    grid_spec = pltpu.PrefetchScalarGridSpec(
        num_scalar_prefetch=1, grid=(rows // tr,),
        in_specs=[pl.BlockSpec((tr, cols), lambda i, jc_ref: (i, 0))],
        out_specs=pl.BlockSpec((None, tr, cols), lambda i, jc_ref: (jc_ref[0], i, 0)))
    return pl.pallas_call(
        body, name=name, grid_spec=grid_spec, out_shape=jax.ShapeDtypeStruct((NCHIP, rows, cols), BF16),
        compiler_params=_params(("parallel",)),
    )(jc, w)


def _add(a, b, name):
    def body(a_ref, b_ref, o_ref):
        o_ref[...] = a_ref[...] + b_ref[...]

    return pl.pallas_call(body, name=name, out_shape=jax.ShapeDtypeStruct(a.shape, F32))(a, b)


def _pack_small(sm):
    vec_in = [sm[n] for n, _, _ in VEC_ROWS]
    nv = len(vec_in)

    def body(*refs):
        ins, lossv, dcw, dcfb, dcfw, s1, s2 = refs[:nv], refs[nv], refs[nv + 1], refs[nv + 2], refs[nv + 3], \
            refs[nv + 4], refs[nv + 5]
        for ref, (_, r0, nr) in zip(ins, VEC_ROWS):
            s1[r0:r0 + nr, :] = ref[...]
        s1[ROW_LOSS:ROW_LOSS + 1, :] = lossv[...]
        s1[ROW_CONV_A:ROW_CONV_A + 4, :] = dcw[...]
        s2[0:1, :] = dcfb[...]
        s2[1:4, :] = dcfw[...]
        s2[4:8, :] = jnp.zeros((4, DUP), F32)

    return pl.pallas_call(
        body, name="pack_small",
        out_shape=(jax.ShapeDtypeStruct((S1_ROWS, D), F32), jax.ShapeDtypeStruct((S2_ROWS, DUP), F32)),
    )(*vec_in, sm["lossv"], sm["conv_a_w"], sm["conv_f_b"], sm["conv_f_w"])


def _adam_math(w, g, m, v):
    m = ADAM_B1 * m + (1.0 - ADAM_B1) * g
    v = ADAM_B2 * v + (1.0 - ADAM_B2) * (g * g)
    m_hat = m / (1.0 - ADAM_B1 ** ADAM_STEP)
    v_hat = v / (1.0 - ADAM_B2 ** ADAM_STEP)
    delta = -ADAM_LR * (m_hat / (jnp.sqrt(v_hat) + ADAM_EPS) + ADAM_WD * w)
    return delta, m, v


def _adam(w, g, m, v, name):
    rows, cols = w.shape
    tr = _row_tile(rows)

    def body(w_ref, g_ref, m_ref, v_ref, d_ref, mo_ref, vo_ref):
        d_ref[...], mo_ref[...], vo_ref[...] = _adam_math(w_ref[...], g_ref[...], m_ref[...], v_ref[...])

    spec = pl.BlockSpec((tr, cols), lambda i: (i, 0))
    return pl.pallas_call(
        body, name=name, out_shape=(jax.ShapeDtypeStruct(w.shape, F32),) * 3, grid=(rows // tr,),
        in_specs=[spec] * 4, out_specs=(spec,) * 3,
        compiler_params=_params(("parallel",)),
    )(w, g, m, v)


def _adam_small(gs1, gs2, gs3, w, m, v):
    names = [n for n, _, _ in VEC_ROWS] + ["conv_f_b", "lru_wa", "lru_wx"]
    nn = len(names)

    def grad_of(i, g1, g2, g3):
        if i < len(VEC_ROWS):
            _, r0, nr = VEC_ROWS[i]
            return g1[r0:r0 + nr, :]
        if names[i] == "conv_f_b":
            return g2[0:1, :]
        return g3[0] if names[i] == "lru_wa" else g3[1]

    def body(*refs):
        g1, g2, g3 = refs[0], refs[1], refs[2]
        ws, ms, vs = refs[3:3 + nn], refs[3 + nn:3 + 2 * nn], refs[3 + 2 * nn:3 + 3 * nn]
        outs = refs[3 + 3 * nn:]
        for i in range(nn):
            d, mn, vn = _adam_math(ws[i][...], grad_of(i, g1, g2, g3), ms[i][...], vs[i][...])
            outs[i][...] = d
            outs[nn + i][...] = mn
            outs[2 * nn + i][...] = vn

    shapes = [jax.ShapeDtypeStruct(w[n].shape, F32) for n in names]
    outs = pl.pallas_call(body, name="adam_small", out_shape=tuple(shapes * 3))(
        gs1, gs2, gs3, *[w[n] for n in names], *[m[n] for n in names], *[v[n] for n in names])
    return {n: (outs[i], outs[nn + i], outs[2 * nn + i]) for i, n in enumerate(names)}


WEIGHTS = ("norm_pre_mix", "w_in", "conv_a_w", "conv_a_b", "lru_wa", "lru_ba", "lru_wx", "lru_bx", "lru_lambda",
           "hg_lb_logits", "hg_norm_g", "w_branch_a", "w_branch_b", "w_out", "norm_post_mix", "norm_pre_ffn",
           "w_up", "conv_f_w", "conv_f_b", "w_down", "norm_post_ffn")
NW = len(WEIGHTS)


def kernel(x, norm_pre_mix, w_in, conv_a_w, conv_a_b, lru_wa, lru_ba, lru_wx, lru_bx, lru_lambda, hg_lb_logits, hg_norm_g, w_branch_a, w_branch_b, w_out, norm_post_mix, norm_pre_ffn, w_up, conv_f_w, conv_f_b, w_down, norm_post_ffn, loss_target, m_norm_pre_mix, m_w_in, m_conv_a_w, m_conv_a_b, m_lru_wa, m_lru_ba, m_lru_wx, m_lru_bx, m_lru_lambda, m_hg_lb_logits, m_hg_norm_g, m_w_branch_a, m_w_branch_b, m_w_out, m_norm_post_mix, m_norm_pre_ffn, m_w_up, m_conv_f_w, m_conv_f_b, m_w_down, m_norm_post_ffn, v_norm_pre_mix, v_w_in, v_conv_a_w, v_conv_a_b, v_lru_wa, v_lru_ba, v_lru_wx, v_lru_bx, v_lru_lambda, v_hg_lb_logits, v_hg_norm_g, v_w_branch_a, v_w_branch_b, v_w_out, v_norm_post_mix, v_norm_pre_ffn, v_w_up, v_conv_f_w, v_conv_f_b, v_w_down, v_norm_post_ffn):
    rest = (norm_pre_mix, w_in, conv_a_w, conv_a_b, lru_wa, lru_ba, lru_wx, lru_bx, lru_lambda, hg_lb_logits, hg_norm_g, w_branch_a, w_branch_b, w_out, norm_post_mix, norm_pre_ffn, w_up, conv_f_w, conv_f_b, w_down, norm_post_ffn, loss_target, m_norm_pre_mix, m_w_in, m_conv_a_w, m_conv_a_b, m_lru_wa, m_lru_ba, m_lru_wx, m_lru_bx, m_lru_lambda, m_hg_lb_logits, m_hg_norm_g, m_w_branch_a, m_w_branch_b, m_w_out, m_norm_post_mix, m_norm_pre_ffn, m_w_up, m_conv_f_w, m_conv_f_b, m_w_down, m_norm_post_ffn, v_norm_pre_mix, v_w_in, v_conv_a_w, v_conv_a_b, v_lru_wa, v_lru_ba, v_lru_wx, v_lru_bx, v_lru_lambda, v_hg_lb_logits, v_hg_norm_g, v_w_branch_a, v_w_branch_b, v_w_out, v_norm_post_mix, v_norm_pre_ffn, v_w_up, v_conv_f_w, v_conv_f_b, v_w_down, v_norm_post_ffn)
    w_in_args = dict(zip(WEIGHTS, rest[:NW]))
    loss_target = rest[NW]
    m_args = dict(zip(WEIGHTS, rest[NW + 1:2 * NW + 1]))
    v_args = dict(zip(WEIGHTS, rest[2 * NW + 1:3 * NW + 1]))
    shape_of = {n: w_in_args[n].shape for n in WEIGHTS}

    def two_d(n, a):
        if n in BIG:
            return a.reshape(BIG_SHAPE[n])
        if n in ("lru_wa", "lru_wx"):
            return a.reshape(NH, HD, HD)
        return a.reshape(a.shape[-2:])

    w2 = {n: two_d(n, w_in_args[n]) for n in WEIGHTS}
    m2 = {n: two_d(n, m_args[n]) for n in WEIGHTS}
    v2 = {n: two_d(n, v_args[n]) for n in WEIGHTS}

    cidx = lax.axis_index("c").astype(jnp.int32).reshape(1)
    jchip = 2 * lax.axis_index("x") + lax.axis_index("y")

    jc = jnp.stack([jchip, lax.axis_index("c")]).astype(jnp.int32)

    shards = {n: _place_shard(w2[n], jc, "place_" + n) for n in BIG}
    conv_a_s = jnp.pad(w2["conv_a_w"], ((0, 4), (0, 0)))
    conv_f_s = jnp.pad(w2["conv_f_w"], ((0, 5), (0, 0)))
    small = {n: w2[n] for n in WEIGHTS if n not in BIG and n not in ("conv_a_w", "conv_f_w")}

    grad_x, p_big, q_big, sm_g = _local_step(
        x[0], loss_target[0], shards["w_in"], [shards[n] for n in REST], conv_a_s, conv_f_s, small, jc, cidx)

    s1, s2 = _pack_small(sm_g)
    s3 = jnp.concatenate([sm_g["lru_wa"].reshape(D, HD), sm_g["lru_wx"].reshape(D, HD)], axis=0)
    _, (rs1, rs2, rs3) = _reduce_stage1({}, (), (s1, s2, s3), "reduce_d2d_in_small")
    ps1, ps2, ps3 = _add(s1, rs1, "add_s1"), _add(s2, rs2, "add_s2"), _add(s3, rs3, "add_s3")
    qs1, qs2, qs3 = _reduce_stage2(ps1, ps2, ps3)
    f_big = {n: _sum_chips(q_big[n], p_big[n], jc, "sum_chips_" + n) for n in BIG}
    fs1 = _sum_chips(qs1, ps1, jc, "sum_chips_s1")
    fs2 = _sum_chips(qs2, ps2, jc, "sum_chips_s2", by_cols=True)
    fs3 = _sum_chips(qs3, ps3, jc, "sum_chips_s3")
    g_big, gs1, gs2, gs3 = _reduce_stage3(f_big, fs1, fs2, fs3)

    res = {}
    for n in BIG:
        d, mn, vn = _adam(w2[n], g_big[n], m2[n], v2[n], "adam_" + n)
        res[n] = (g_big[n], d, mn, vn)
    small_res = _adam_small(gs1, gs2, gs3.reshape(2, NH, HD, HD), w2, m2, v2)
    for n, r0, nr in VEC_ROWS:
        res[n] = (gs1[r0:r0 + nr],) + small_res[n]
    res["conv_f_b"] = (gs2[0:1],) + small_res["conv_f_b"]
    res["lru_wa"] = (gs3[0:D].reshape(NH, HD, HD),) + small_res["lru_wa"]
    res["lru_wx"] = (gs3[D:2 * D].reshape(NH, HD, HD),) + small_res["lru_wx"]
    g_ca = lax.dynamic_slice_in_dim(gs1[ROW_CONV_A:ROW_CONV_A + 4], jchip * (D // NCHIP), D // NCHIP, axis=1)
    g_cf = lax.dynamic_slice_in_dim(gs2[1:4], jchip * SH_UP, SH_UP, axis=1)
    res["conv_a_w"] = (g_ca,) + _adam(w2["conv_a_w"], g_ca, m2["conv_a_w"], v2["conv_a_w"], "adam_conv_a_w")
    res["conv_f_w"] = (g_cf,) + _adam(w2["conv_f_w"], g_cf, m2["conv_f_w"], v2["conv_f_w"], "adam_conv_f_w")

    loss = (0.5 / D) * jnp.sum(gs1[ROW_LOSS])
    out = [loss, grad_x.reshape(x.shape)]
    for part in range(4):
        out += [res[n][part].reshape(shape_of[n]) for n in WEIGHTS]
    return tuple(out)
```

```python
import functools

import jax
import jax.numpy as jnp
from jax import lax
from jax.experimental import pallas as pl
from jax.experimental.pallas import tpu as pltpu

F32 = jnp.float32
BF16 = jnp.bfloat16

D = 1024
NH = 8
HD = 128
CH = 32
DFF = 2816
DUP = 2 * DFF
NCHIP = 4
SH_IN = 2 * D
SH_UP = DUP // NCHIP
SH_DN = DFF // NCHIP
SH_BR = D // NCHIP
EPS = 1e-6
LRU_C = 8.0
ADAM_LR = 0.001
ADAM_B1 = 0.9
ADAM_B2 = 0.999
ADAM_EPS = 1e-08
ADAM_WD = 0.01
ADAM_STEP = 10
VMEM_BIG = 56 * 1024 * 1024
MESH = pl.DeviceIdType.MESH

SLOT_A, SLOT_B, SLOT_C, SLOT_G = 2, 0, 1, 3


def _slot_of_chip(s):
    return jnp.where(s == 3, 3, (s + 2) % 3)


def _params(sem, vmem=None):
    return pltpu.CompilerParams(dimension_semantics=sem, vmem_limit_bytes=vmem)


_GC = 0.7978845608028654
_GA = 0.044715


def _gelu(x):
    return 0.5 * x * (1.0 + jnp.tanh(_GC * (x + _GA * x * x * x)))


def _gelu_and_grad(x):
    x2 = x * x
    th = jnp.tanh(_GC * x * (1.0 + _GA * x2))
    g = 0.5 * x * (1.0 + th)
    dg = 0.5 * (1.0 + th) + 0.5 * x * (1.0 - th * th) * _GC * (1.0 + 3.0 * _GA * x2)
    return g, dg


def _sig(x):
    return jax.nn.sigmoid(x)


def _dot(a, b):
    return jnp.dot(a, b, preferred_element_type=F32)


def _dot_nt(a, b):
    return lax.dot_general(a, b, (((1,), (1,)), ((), ())), preferred_element_type=F32)


def _dot_tn(a, b):
    return lax.dot_general(a, b, (((0,), (0,)), ((), ())), preferred_element_type=F32)


def _chunk_cumsum(x):
    pos = lax.broadcasted_iota(jnp.int32, (x.shape[0], 1), 0) & (CH - 1)
    d = 1
    while d < CH:
        x = x + jnp.where(pos >= d, pltpu.roll(x, d, 0), 0.0)
        d *= 2
    return x


def _chunk_revcumsum(x):
    n = x.shape[0]
    pos = lax.broadcasted_iota(jnp.int32, (n, 1), 0) & (CH - 1)
    d = 1
    while d < CH:
        x = x + jnp.where(pos < CH - d, pltpu.roll(x, n - d, 0), 0.0)
        d *= 2
    return x


def _chunk_last(x):
    n = x.shape[0]
    return jnp.concatenate(
        [jnp.broadcast_to(x[c * CH + CH - 1:c * CH + CH, :], (CH, x.shape[1])) for c in range(n // CH)], axis=0)


def _chunk_total(x):
    n = x.shape[0]
    return jnp.concatenate(
        [jnp.broadcast_to(jnp.sum(x[c * CH:(c + 1) * CH, :], axis=0, keepdims=True), (CH, x.shape[1]))
         for c in range(n // CH)], axis=0)


def _rms_stats(x):
    r = lax.rsqrt(jnp.mean(x * x, axis=-1, keepdims=True) + EPS)
    return r, x * r


def _rms_bwd(gd, n, r):
    return r * (gd - n * jnp.mean(gd * n, axis=-1, keepdims=True))


def _shift_rows(x, d, fill):
    rows = lax.broadcasted_iota(jnp.int32, (x.shape[0], 1), 0)
    return jnp.where(rows >= d, pltpu.roll(x, d, 0), fill)


def _scan_down(a, u, carry):
    n = a.shape[0]
    pos = lax.broadcasted_iota(jnp.int32, (n, 1), 0) & 7
    for d in (1, 2, 4):
        u = a * jnp.where(pos >= d, pltpu.roll(u, d, 0), 0.0) + u
        a = a * jnp.where(pos >= d, pltpu.roll(a, d, 0), 1.0)
    out = []
    for v in range(n // 8):
        h = a[v * 8:v * 8 + 8, :] * carry + u[v * 8:v * 8 + 8, :]
        carry = h[7:8, :]
        out.append(h)
    return jnp.concatenate(out, axis=0)


def _scan_up(b, g, carry):
    n = b.shape[0]
    pos = lax.broadcasted_iota(jnp.int32, (n, 1), 0) & 7
    for d in (1, 2, 4):
        g = g + b * jnp.where(pos < 8 - d, pltpu.roll(g, n - d, 0), 0.0)
        b = b * jnp.where(pos < 8 - d, pltpu.roll(b, n - d, 0), 1.0)
    out = [None] * (n // 8)
    for v in reversed(range(n // 8)):
        h = g[v * 8:v * 8 + 8, :] + b[v * 8:v * 8 + 8, :] * carry
        carry = h[0:1, :]
        out[v] = h
    return jnp.concatenate(out, axis=0)


def _shift_rows_up(x, d, fill):
    n = x.shape[0]
    rows = lax.broadcasted_iota(jnp.int32, (n, 1), 0)
    return jnp.where(rows < n - d, pltpu.roll(x, n - d, 0), fill)


def _mm_nn_sharded(a, b3, out_dtype, tm, name, slot_fn=None):
    m, k = a.shape
    s, _, ns = b3.shape

    def body(a_ref, b_ref, o_ref):
        o_ref[...] = _dot(a_ref[...], b_ref[...]).astype(out_dtype)

    if slot_fn is None:
        out_shape = jax.ShapeDtypeStruct((m, s * ns), out_dtype)
        out_spec = pl.BlockSpec((tm, ns), lambda j, i: (i, j))
    else:
        out_shape = jax.ShapeDtypeStruct((s, m, ns), out_dtype)
        out_spec = pl.BlockSpec((None, tm, ns), lambda j, i: (slot_fn(j), i, 0))
    return pl.pallas_call(
        body, name=name, out_shape=out_shape, grid=(s, m // tm),
        in_specs=[pl.BlockSpec((tm, k), lambda j, i: (i, 0)),
                  pl.BlockSpec((None, k, ns), lambda j, i: (j, 0, 0))],
        out_specs=out_spec,
        compiler_params=_params(("parallel", "parallel"), VMEM_BIG),
    )(a, b3)


def _peer_of_step(s):
    return ((s & 1) << 1) | (s >> 1)


def _mm_in_gather(a, stk_w_in, stk_rest, names_rest, conv_a_s, conv_f_s, jc, tm):
    m, k = a.shape
    s_n, _, ns = stk_w_in.shape
    nr = len(stk_rest)
    mt = m // tm

    def body(jc_ref, a_ref, w_in_any, *rest):
        del w_in_any
        ca_src, cf_src = rest[nr], rest[nr + 1]
        o_ref, w_full = rest[nr + 2], rest[nr + 3]
        stk = rest[nr + 4:2 * nr + 4]
        ca_dst, cf_dst = rest[2 * nr + 4], rest[2 * nr + 5]
        wbuf, ssem_w, rsem_w, fssem_w, frsem_w, ssem_r, rsem_r, csend, crecv, lsem, wsem = rest[2 * nr + 6:]
        s, i = pl.program_id(0), pl.program_id(1)
        x, y, c, j, chips = _place()
        sends_w, arrive_w, fwds_w, farrive_w = _gather_copies([w_full], ("w_in",), ssem_w, rsem_w, fssem_w, frsem_w)
        sends_r, arrive_r, _, _ = _gather_copies(stk, names_rest, ssem_r, rsem_r)
        conv = ((ca_src, ca_dst), (cf_src, cf_dst))
        locs = [pltpu.make_async_copy(src, dst.at[j], lsem.at[n]) for n, (src, dst) in enumerate(conv)]
        csends = [_remote(src, dst.at[j], csend.at[3 * n + kk], crecv.at[3 * n + kk], (cx, cy, c))
                  for n, (src, dst) in enumerate(conv) for kk, (cx, cy) in enumerate(chips)]

        def fetch(step):
            return pltpu.make_async_copy(w_full.at[j ^ _peer_of_step(step)], wbuf.at[step & 1], wsem.at[step & 1])

        @pl.when((s == 0) & (i == 0))
        def _():
            for cp in sends_w + locs + csends + sends_r:
                cp.start()
            fetch(s).start()

        @pl.when(i == 0)
        def _():
            fetch(s).wait()

        o_ref[...] = _dot(a_ref[...], wbuf[s & 1])

        @pl.when((s == 0) & (i == mt - 1))
        def _():
            for kk in (0, 1):
                arrive_w[kk].wait_recv()
                fwds_w[kk].start()

        @pl.when((s == 1) & (i == mt - 1))
        def _():
            arrive_w[2].wait_recv()
            fwds_w[2].start()

        for kk in range(3):
            @pl.when((s == kk) & (i == mt - 1))
            def _(kk=kk):
                farrive_w[kk].wait_recv()
                fetch(s + 1).start()

        @pl.when((s == s_n - 1) & (i == mt - 1))
        def _():
            for cp in arrive_r:
                cp.wait_recv()
            for n, (_, dst) in enumerate(conv):
                for kk, (cx, cy) in enumerate(chips):
                    got = dst.at[2 * cx + cy]
                    _remote(got, got, csend.at[3 * n + kk], crecv.at[3 * n + kk], (cx, cy, c)).wait_recv()
            for cp in sends_w + fwds_w + sends_r + csends:
                cp.wait_send()
            for cp in locs:
                cp.wait()

    any_spec = pl.BlockSpec(memory_space=pl.ANY)
    sem = pltpu.SemaphoreType.DMA
    grid_spec = pltpu.PrefetchScalarGridSpec(
        num_scalar_prefetch=1, grid=(s_n, mt),
        in_specs=[pl.BlockSpec((tm, k), lambda s, i, jc_ref: (i, 0))] + [any_spec] * (nr + 3),
        out_specs=(pl.BlockSpec((None, tm, ns),
                                lambda s, i, jc_ref: (_slot_of_chip(jc_ref[0] ^ _peer_of_step(s)), i, 0)),)
        + (any_spec,) * (nr + 3),
        scratch_shapes=[pltpu.VMEM((2, k, ns), BF16), sem((3,)), sem((3,)), sem((3,)), sem((3,)),
                        sem((3 * nr,)), sem((3 * nr,)), sem((6,)), sem((6,)), sem((2,)), sem((2,))])
    outs = pl.pallas_call(
        body, name="mm_in", grid_spec=grid_spec,
        out_shape=(jax.ShapeDtypeStruct((s_n, m, ns), F32), jax.ShapeDtypeStruct(stk_w_in.shape, BF16))
        + tuple(jax.ShapeDtypeStruct(v.shape, v.dtype) for v in stk_rest)
        + tuple(jax.ShapeDtypeStruct((NCHIP,) + v.shape, v.dtype) for v in (conv_a_s, conv_f_s)),
        input_output_aliases={2 + w: 1 + w for w in range(nr + 1)},
        compiler_params=pltpu.CompilerParams(dimension_semantics=("arbitrary", "arbitrary"),
                                             vmem_limit_bytes=VMEM_BIG, has_side_effects=True),
    )(jc, a, stk_w_in, *stk_rest, conv_a_s, conv_f_s)
    return outs[0], outs[1], list(outs[2:2 + nr]), outs[2 + nr], outs[3 + nr]


def _mm_dh1_exchange(dp4, w_in3, p_w_in, x, dx1, gain, tm):
    s, k, ns = w_in3.shape
    m = dp4.shape[1]
    mt = m // tm

    def body(a_ref, b_ref, p_ref, x_ref, dx1_ref, g_ref, o_ref, q_ref, dg_ref, ssem, rsem):
        i, j = pl.program_id(0), pl.program_id(1)
        sends, arrive = _exchange_copies([q_ref], [lambda chip: p_ref.at[chip]], ssem, rsem)

        @pl.when((i == 0) & (j == 0))
        def _():
            for cp in sends:
                cp.start()

        @pl.when(j == 0)
        def _():
            o_ref[...] = _dot_nt(a_ref[...], b_ref[...])

        @pl.when(j > 0)
        def _():
            o_ref[...] += _dot_nt(a_ref[...], b_ref[...])

        @pl.when(j == s - 1)
        def _():
            dh = o_ref[...]
            r, n = _rms_stats(x_ref[...])
            o_ref[...] = dx1_ref[...] + _rms_bwd(dh * g_ref[...], n, r)
            dgv = jnp.sum(dh * n, axis=0, keepdims=True)

            @pl.when(i == 0)
            def _():
                dg_ref[...] = dgv

            @pl.when(i > 0)
            def _():
                dg_ref[...] += dgv

        @pl.when((i == mt - 1) & (j == s - 1))
        def _():
            for cp in arrive:
                cp.wait_recv()
            for cp in sends:
                cp.wait_send()

    any_spec = pl.BlockSpec(memory_space=pl.ANY)
    row_tile = pl.BlockSpec((tm, k), lambda i, j: (i, 0))
    vec = pl.BlockSpec((1, k), lambda i, j: (0, 0))
    return pl.pallas_call(
        body, name="mm_dh1",
        out_shape=(jax.ShapeDtypeStruct((m, k), F32), jax.ShapeDtypeStruct(p_w_in.shape, BF16),
                   jax.ShapeDtypeStruct((1, k), F32)),
        grid=(mt, s),
        in_specs=[pl.BlockSpec((None, tm, ns), lambda i, j: (_slot_of_chip(j), i, 0)),
                  pl.BlockSpec((None, k, ns), lambda i, j: (j, 0, 0)), any_spec, row_tile, row_tile, vec],
        out_specs=(row_tile, any_spec, vec),
        scratch_shapes=[pltpu.SemaphoreType.DMA((3,)), pltpu.SemaphoreType.DMA((3,))],
        compiler_params=pltpu.CompilerParams(dimension_semantics=("arbitrary", "arbitrary"),
                                             vmem_limit_bytes=VMEM_BIG, has_side_effects=True),
    )(dp4, w_in3, p_w_in, x, dx1, gain)


def _mm_nt_sharded(a, b3, tm, name, stacked_slot_fn=None):
    s, k, ns = b3.shape
    m = a.shape[1] if stacked_slot_fn is not None else a.shape[0]

    def body(a_ref, b_ref, o_ref):
        j = pl.program_id(1)
        @pl.when(j == 0)
        def _():
            o_ref[...] = _dot_nt(a_ref[...], b_ref[...])

        @pl.when(j > 0)
        def _():
            o_ref[...] += _dot_nt(a_ref[...], b_ref[...])

    if stacked_slot_fn is None:
        a_spec = pl.BlockSpec((tm, ns), lambda i, j: (i, j))
    else:
        a_spec = pl.BlockSpec((None, tm, ns), lambda i, j: (stacked_slot_fn(j), i, 0))
    return pl.pallas_call(
        body, name=name, out_shape=jax.ShapeDtypeStruct((m, k), F32), grid=(m // tm, s),
        in_specs=[a_spec, pl.BlockSpec((None, k, ns), lambda i, j: (j, 0, 0))],
        out_specs=pl.BlockSpec((tm, k), lambda i, j: (i, 0)),
        compiler_params=_params(("parallel", "arbitrary"), VMEM_BIG),
    )(a, b3)


def _mm_nt(a, b, out_dtype, tm, name):
    m, k = a.shape
    n = b.shape[0]

    def body(a_ref, b_ref, o_ref):
        o_ref[...] = _dot_nt(a_ref[...], b_ref[...]).astype(out_dtype)

    return pl.pallas_call(
        body, name=name, out_shape=jax.ShapeDtypeStruct((m, n), out_dtype), grid=(m // tm,),
        in_specs=[pl.BlockSpec((tm, k), lambda i: (i, 0)), pl.BlockSpec((n, k), lambda i: (0, 0))],
        out_specs=pl.BlockSpec((tm, n), lambda i: (i, 0)),
        compiler_params=_params(("parallel",), VMEM_BIG),
    )(a, b)


def _mm_tn(a, g, tkk, tn, tk, name, stacked_slot_fn=None, stacked_out=False):
    m, k = a.shape
    if stacked_slot_fn is not None:
        n = g.shape[0] * g.shape[2]
        g_spec = pl.BlockSpec((None, tk, tn), lambda kk, j, mm: (stacked_slot_fn(j), mm, 0))
    else:
        n = g.shape[1]
        g_spec = pl.BlockSpec((tk, tn), lambda kk, j, mm: (mm, j))
    steps = m // tk

    def body(a_ref, g_ref, o_ref, acc_ref):
        mm = pl.program_id(2)

        @pl.when(mm == 0)
        def _():
            acc_ref[...] = _dot_tn(a_ref[...], g_ref[...])

        @pl.when(mm > 0)
        def _():
            acc_ref[...] += _dot_tn(a_ref[...], g_ref[...])

        @pl.when(mm == steps - 1)
        def _():
            o_ref[...] = acc_ref[...].astype(BF16)

    if stacked_out:
        out_shape = jax.ShapeDtypeStruct((n // tn, k, tn), BF16)
        out_spec = pl.BlockSpec((None, tkk, tn), lambda kk, j, mm: (j, kk, 0))
    else:
        out_shape = jax.ShapeDtypeStruct((k, n), BF16)
        out_spec = pl.BlockSpec((tkk, tn), lambda kk, j, mm: (kk, j))
    return pl.pallas_call(
        body, name=name, out_shape=out_shape, grid=(k // tkk, n // tn, steps),
        in_specs=[pl.BlockSpec((tk, tkk), lambda kk, j, mm: (mm, kk)), g_spec],
        out_specs=out_spec,
        scratch_shapes=[pltpu.VMEM((tkk, tn), F32)],
        compiler_params=_params(("parallel", "parallel", "arbitrary"), VMEM_BIG),
    )(a, g)


def _norm_fwd(x, gain, tt):
    t = x.shape[0]

    def body(x_ref, g_ref, h_ref):
        _, n = _rms_stats(x_ref[...])
        h_ref[...] = (n * g_ref[...]).astype(BF16)

    return pl.pallas_call(
        body, name="norm_fwd", out_shape=jax.ShapeDtypeStruct((t, D), BF16), grid=(t // tt,),
        in_specs=[pl.BlockSpec((tt, D), lambda i: (i, 0)), pl.BlockSpec((1, D), lambda i: (0, 0))],
        out_specs=pl.BlockSpec((tt, D), lambda i: (i, 0)),
        compiler_params=_params(("parallel",)),
    )(x, gain)


def _lru_gates(xc, wa_ref, wx_ref, ba, bx, lam):
    xcb = xc.astype(BF16)
    ra = jnp.concatenate([_dot(xcb[:, n * HD:(n + 1) * HD], wa_ref[n]) for n in range(NH)], axis=1) + ba
    ix = jnp.concatenate([_dot(xcb[:, n * HD:(n + 1) * HD], wx_ref[n]) for n in range(NH)], axis=1) + bx
    r = _sig(ra)
    ig = _sig(ix)
    z = -lam
    sp = jnp.maximum(z, 0.0) + jnp.log1p(jnp.exp(-jnp.abs(z)))
    log_a = -LRU_C * r * sp
    a = jnp.exp(log_a)
    z2 = 2.0 * log_a
    series = -z2 * (1.0 + z2 * (0.5 + z2 * (1.0 / 6.0 + z2 * (1.0 / 24.0))))
    om = jnp.where(z2 > -0.02, series, 1.0 - jnp.exp(z2))
    mult = jnp.sqrt(om)
    return xcb, r, ig, sp, a, mult


def _mixer_a_fwd(p4, cw, cb, wa, wx, ba, bx, lam, tt):
    t = p4.shape[1]

    def body(p_ref, cw_ref, cb_ref, wa_ref, wx_ref, ba_ref, bx_ref, lam_ref, ya_ref, h_ref, sv_ref, halo, hc):
        i = pl.program_id(0)

        @pl.when(i == 0)
        def _():
            halo[...] = jnp.zeros((8, D), F32)
            hc[...] = jnp.zeros((1, D), F32)

        xa = p_ref[:, 0:D]
        ga = p_ref[:, D:2 * D]
        xe = jnp.concatenate([halo[...], xa], axis=0)
        xc = (cb_ref[...] + cw_ref[3:4, :] * xe
              + sum(cw_ref[3 - s:4 - s, :] * pltpu.roll(xe, s, 0) for s in (1, 2, 3)))[8:, :]
        halo[...] = xa[tt - 8:, :]
        _, r, ig, _, a, mult = _lru_gates(xc, wa_ref, wx_ref, ba_ref[...], bx_ref[...], lam_ref[...])
        u = mult * ig * xc
        h = _scan_down(a, u, hc[...])
        hc[...] = h[tt - 1:tt, :]
        h_ref[...] = h
        ya_ref[...] = (h * _gelu(ga)).astype(BF16)
        for idx, val in enumerate((xc, r, ig, a, mult)):
            sv_ref[idx] = val

    full = lambda shape: pl.BlockSpec(shape, lambda i: (0,) * len(shape))
    return pl.pallas_call(
        body, name="mixer_a_fwd",
        out_shape=(jax.ShapeDtypeStruct((t, D), BF16), jax.ShapeDtypeStruct((t, D), F32),
                   jax.ShapeDtypeStruct((5, t, D), F32)),
        grid=(t // tt,),
        in_specs=[pl.BlockSpec((None, tt, 2 * D), lambda i: (SLOT_A, i, 0)),
                  full((4, D)), full((1, D)), full((NH, HD, HD)), full((NH, HD, HD)),
                  full((1, D)), full((1, D)), full((1, D))],
        out_specs=(pl.BlockSpec((tt, D), lambda i: (i, 0)), pl.BlockSpec((tt, D), lambda i: (i, 0)),
                   pl.BlockSpec((5, tt, D), lambda i: (0, i, 0))),
        scratch_shapes=[pltpu.VMEM((8, D), F32), pltpu.VMEM((1, D), F32)],
        compiler_params=_params(("arbitrary",), VMEM_BIG),
    )(p4, cw, cb, wa, wx, ba, bx, lam)


def _chunk_masks(tt):
    row = lax.broadcasted_iota(jnp.int32, (tt, tt), 0)
    col = lax.broadcasted_iota(jnp.int32, (tt, tt), 1)
    same = jnp.right_shift(row, 5) == jnp.right_shift(col, 5)
    return same & (col <= row)


def _hg_head_fwd(q, fz, lbh, saved=None):
    sgn = _sig(-fz)
    k = (1.0 - lbh) * sgn
    if saved is None:
        sg = _sig(fz)
        f = lbh + (1.0 - lbh) * sg
        g = _chunk_cumsum(jnp.log(f))
    else:
        sg, g = saved
        f = lbh + (1.0 - lbh) * sg
    gu = _chunk_last(g) - g
    eg = jnp.exp(g)
    eng = jnp.exp(-g)
    egu = jnp.exp(gu)
    qt = q * eg
    kt = k * eng
    kd = k * egu
    return sg, sgn, f, k, g, eg, eng, egu, qt, kt, kd


def _lb_of(logits_ref):
    return _sig(logits_ref[0:1, :] - logits_ref[1:2, :])


def _hgrn2_fwd(p4, logits, gnorm, tt):
    t = p4.shape[1]
    nc = tt // CH

    def body(p_ref, lg_ref, gn_ref, yb_ref, o_ref, ss_ref, sv_ref, st):
        i = pl.program_id(0)

        @pl.when(i == 0)
        def _():
            st[...] = jnp.zeros((NH, HD, HD), F32)

        low = _chunk_masks(tt)
        lb = _lb_of(lg_ref)
        heads = [slice(h * HD, (h + 1) * HD) for h in range(NH)]
        sg, _, _, _, g, _, _, _, qt, kt, kd = _hg_head_fwd(p_ref[0, :, 0:D], p_ref[0, :, D:2 * D], lb)
        sv_ref[0] = sg
        sv_ref[1] = g
        qtb, ktb, kdb, vb = qt.astype(BF16), kt.astype(BF16), kd.astype(BF16), p_ref[1, :, 0:D].astype(BF16)
        decs = [jnp.exp(g[c * CH + CH - 1:c * CH + CH, :]) for c in range(nc)]
        o_in = []
        for hs in heads:
            att = jnp.where(low, _dot_nt(qtb[:, hs], ktb[:, hs]), 0.0)
            o_in.append(_dot(att.astype(BF16), vb[:, hs]))
        s_t = [st[h] for h in range(NH)]
        pieces = [[None] * nc for _ in range(NH)]
        for c in range(nc):
            sl = slice(c * CH, (c + 1) * CH)
            for h, hs in enumerate(heads):
                s_bf = s_t[h].astype(BF16)
                ss_ref[c, h] = s_bf
                pieces[h][c] = o_in[h][sl] + _dot_nt(qtb[sl, hs], s_bf)
                s_t[h] = s_t[h] * decs[c][:, hs] + _dot_tn(vb[sl, hs], kdb[sl, hs])
        for h, hs in enumerate(heads):
            st[h] = s_t[h]
            o = jnp.concatenate(pieces[h], axis=0)
            _, n = _rms_stats(o)
            og = p_ref[1, :, D + h * HD:D + (h + 1) * HD]
            o_ref[:, hs] = o
            yb_ref[:, hs] = (n * gn_ref[:, hs] * (og * _sig(og))).astype(BF16)

    return pl.pallas_call(
        body, name="hgrn2_fwd",
        out_shape=(jax.ShapeDtypeStruct((t, D), BF16), jax.ShapeDtypeStruct((t, D), F32),
                   jax.ShapeDtypeStruct((t // CH, NH, HD, HD), BF16), jax.ShapeDtypeStruct((2, t, D), F32)),
        grid=(t // tt,),
        in_specs=[pl.BlockSpec((2, tt, 2 * D), lambda i: (0, i, 0)),
                  pl.BlockSpec((2, D), lambda i: (0, 0)), pl.BlockSpec((1, D), lambda i: (0, 0))],
        out_specs=(pl.BlockSpec((tt, D), lambda i: (i, 0)), pl.BlockSpec((tt, D), lambda i: (i, 0)),
                   pl.BlockSpec((nc, NH, HD, HD), lambda i: (i, 0, 0, 0)),
                   pl.BlockSpec((2, tt, D), lambda i: (0, i, 0))),
        scratch_shapes=[pltpu.VMEM((NH, HD, HD), F32)],
        compiler_params=_params(("arbitrary",), VMEM_BIG),
    )(p4, logits, gnorm)


def _mid_fwd(ya, yb, p4, x, wa, wb, wo, g_pm, g_pf, tt):
    t = x.shape[0]

    def body(ya_ref, yb_ref, gt_ref, x_ref, wa_ref, wb_ref, wo_ref, gpm_ref, gpf_ref,
             za_ref, zb_ref, mix_ref, m2_ref, x1_ref, h2_ref):
        za = _dot(ya_ref[...], wa_ref[...])
        zb = _dot(yb_ref[...], wb_ref[...])
        mix = _sig(gt_ref[:, 0:D]) * za + _sig(gt_ref[:, D:2 * D]) * zb
        mixb = mix.astype(BF16)
        m2 = _dot(mixb, wo_ref[...])
        _, n2 = _rms_stats(m2)
        x1 = x_ref[...] + n2 * gpm_ref[...]
        _, n1 = _rms_stats(x1)
        za_ref[...] = za.astype(BF16)
        zb_ref[...] = zb.astype(BF16)
        mix_ref[...] = mixb
        m2_ref[...] = m2
        x1_ref[...] = x1
        h2_ref[...] = (n1 * gpf_ref[...]).astype(BF16)

    row = lambda dt: jax.ShapeDtypeStruct((t, D), dt)
    tile = pl.BlockSpec((tt, D), lambda i: (i, 0))
    wsp = pl.BlockSpec((D, D), lambda i: (0, 0))
    vec = pl.BlockSpec((1, D), lambda i: (0, 0))
    return pl.pallas_call(
        body, name="mid_fwd",
        out_shape=(row(BF16), row(BF16), row(BF16), row(F32), row(F32), row(BF16)),
        grid=(t // tt,),
        in_specs=[tile, tile, pl.BlockSpec((None, tt, 2 * D), lambda i: (SLOT_G, i, 0)), tile,
                  wsp, wsp, wsp, vec, vec],
        out_specs=(tile,) * 6,
        compiler_params=_params(("parallel",), VMEM_BIG),
    )(ya, yb, p4, x, wa, wb, wo, g_pm, g_pf)


def _up_act_fwd(h2, w_up4, cfw, cfb, tm):
    t = h2.shape[0]
    ns = SH_UP

    def body(a_ref, ah_ref, wg_ref, wv_ref, cwg_ref, cwv_ref, cbg_ref, cbv_ref,
             pg_ref, pv_ref, y_ref, uv_ref, gl_ref, dgl_ref):
        i = pl.program_id(1)
        rows = jnp.concatenate([ah_ref[...], a_ref[...]], axis=0)
        ups = []
        for w_ref, cw_ref, cb_ref, pre_ref in ((wg_ref, cwg_ref, cbg_ref, pg_ref), (wv_ref, cwv_ref, cbv_ref, pv_ref)):
            pre = _dot(rows, w_ref[...])
            pre_ref[...] = pre[16:, :]
            xe = jnp.concatenate([jnp.where(i > 0, pre[8:16, :], 0.0), pre[16:, :]], axis=0)
            up = (cb_ref[...] + cw_ref[2:3, :] * xe + cw_ref[1:2, :] * pltpu.roll(xe, 1, 0)
                  + cw_ref[0:1, :] * pltpu.roll(xe, 2, 0))
            ups.append(up[8:, :])
        gl, dgl = _gelu_and_grad(ups[0])
        y_ref[...] = (gl * ups[1]).astype(BF16)
        uv_ref[...] = ups[1].astype(BF16)
        gl_ref[...] = gl.astype(BF16)
        dgl_ref[...] = dgl.astype(BF16)

    hb = tm // 16
    tile = pl.BlockSpec((tm, ns), lambda p, i: (i, p))
    return pl.pallas_call(
        body, name="up_act_fwd",
        out_shape=(jax.ShapeDtypeStruct((t, DFF), F32),) * 2 + (jax.ShapeDtypeStruct((t, DFF), BF16),) * 4,
        grid=(2, t // tm),
        in_specs=[pl.BlockSpec((tm, D), lambda p, i: (i, 0)),
                  pl.BlockSpec((16, D), lambda p, i: (jnp.maximum(i * hb - 1, 0), 0)),
                  pl.BlockSpec((None, D, ns), lambda p, i: (p, 0, 0)),
                  pl.BlockSpec((None, D, ns), lambda p, i: (p + 2, 0, 0)),
                  pl.BlockSpec((3, ns), lambda p, i: (0, p)), pl.BlockSpec((3, ns), lambda p, i: (0, p + 2)),
                  pl.BlockSpec((1, ns), lambda p, i: (0, p)), pl.BlockSpec((1, ns), lambda p, i: (0, p + 2))],
        out_specs=(tile,) * 6,
        compiler_params=_params(("parallel", "parallel"), VMEM_BIG),
    )(h2, h2, w_up4, w_up4, cfw, cfw, cfb, cfb)


def _down_loss(y, wdn, x1, tgt, g_post, tt):
    t = x1.shape[0]

    def body(y_ref, w_ref, x1_ref, t_ref, g_ref, dx2_ref, dm3_ref, lossv_ref, dg_ref):
        i = pl.program_id(0)
        m3 = _dot(y_ref[...], w_ref[...])
        r, n3 = _rms_stats(m3)
        g = g_ref[...]
        e = x1_ref[...] + n3 * g - t_ref[...]
        dx2 = e * (1.0 / D)
        dx2_ref[...] = dx2
        dm3_ref[...] = _rms_bwd(dx2 * g, n3, r).astype(BF16)
        lv = jnp.sum(e * e, axis=0, keepdims=True)
        dgv = jnp.sum(dx2 * n3, axis=0, keepdims=True)

        @pl.when(i == 0)
        def _():
            lossv_ref[...] = lv
            dg_ref[...] = dgv

        @pl.when(i > 0)
        def _():
            lossv_ref[...] += lv
            dg_ref[...] += dgv

    tile = pl.BlockSpec((tt, D), lambda i: (i, 0))
    vec = pl.BlockSpec((1, D), lambda i: (0, 0))
    return pl.pallas_call(
        body, name="down_loss",
        out_shape=(jax.ShapeDtypeStruct((t, D), F32), jax.ShapeDtypeStruct((t, D), BF16),
                   jax.ShapeDtypeStruct((1, D), F32), jax.ShapeDtypeStruct((1, D), F32)),
        grid=(t // tt,),
        in_specs=[pl.BlockSpec((tt, DFF), lambda i: (i, 0)), pl.BlockSpec((DFF, D), lambda i: (0, 0)),
                  tile, tile, vec],
        out_specs=(tile, tile, vec, vec),
        compiler_params=_params(("arbitrary",), VMEM_BIG),
    )(y, wdn, x1, tgt, g_post)


def _ffn_act_bwd(dy, pre_g, pre_v, uv, gl, dgl, cfw, tt):
    t = dy.shape[0]
    nt = t // tt

    def body(dy_ref, dyn_ref, pg_ref, pv_ref, uv_ref, uvn_ref, gl_ref, gln_ref, dgl_ref, dgln_ref, cw_ref,
             du_ref, dcw_ref, dcb_ref):
        i = pl.program_id(0)
        n = tt + 8
        next_live = jnp.where(i < nt - 1, 1.0, 0.0)
        ext = lambda ref, nref: jnp.concatenate([ref[...].astype(F32), nref[...].astype(F32)[0:8, :]], axis=0)
        dy = jnp.concatenate([dy_ref[...].astype(F32), dyn_ref[...].astype(F32)[0:8, :] * next_live], axis=0)
        ds = (dy * ext(uv_ref, uvn_ref) * ext(dgl_ref, dgln_ref), dy * ext(gl_ref, gln_ref))
        dcw_parts, dcb_parts = [], []
        for hh, c0 in enumerate((0, DFF)):
            cs = slice(c0, c0 + DFF)
            dd = ds[hh]
            d1 = pltpu.roll(dd, n - 1, 0)
            d2 = pltpu.roll(dd, n - 2, 0)
            du_ref[:, cs] = (cw_ref[2:3, cs] * dd + cw_ref[1:2, cs] * d1 + cw_ref[0:1, cs] * d2)[0:tt, :].astype(BF16)
            x = (pg_ref, pv_ref)[hh][...]
            dcw_parts.append(jnp.concatenate(
                [jnp.sum(dk[0:tt, :] * x, axis=0, keepdims=True) for dk in (d2, d1, dd)], axis=0))
            dcb_parts.append(jnp.sum(dd[0:tt, :], axis=0, keepdims=True))
        dcw = jnp.concatenate(dcw_parts, axis=1)
        dcb = jnp.concatenate(dcb_parts, axis=1)

        @pl.when(i == 0)
        def _():
            dcw_ref[...] = dcw
            dcb_ref[...] = dcb

        @pl.when(i > 0)
        def _():
            dcw_ref[...] += dcw
            dcb_ref[...] += dcb

    half = pl.BlockSpec((tt, DFF), lambda i: (i, 0))
    half_next = pl.BlockSpec((16, DFF), lambda i: (jnp.minimum((i + 1) * (tt // 16), t // 16 - 1), 0))
    return pl.pallas_call(
        body, name="ffn_act_bwd",
        out_shape=(jax.ShapeDtypeStruct((t, DUP), BF16), jax.ShapeDtypeStruct((3, DUP), F32),
                   jax.ShapeDtypeStruct((1, DUP), F32)),
        grid=(nt,),
        in_specs=[half, half_next, half, half,
                  half, half_next, half, half_next, half, half_next,
                  pl.BlockSpec((3, DUP), lambda i: (0, 0))],
        out_specs=(pl.BlockSpec((tt, DUP), lambda i: (i, 0)), pl.BlockSpec((3, DUP), lambda i: (0, 0)),
                   pl.BlockSpec((1, DUP), lambda i: (0, 0))),
        compiler_params=_params(("arbitrary",), VMEM_BIG),
    )(dy, dy, pre_g, pre_v, uv, uv, gl, gl, dgl, dgl, cfw)


def _mid_bwd(dh2, dx2, x1, m2, za, zb, p4, wa, wb, wo, g_pm, g_pf, tt):
    t = x1.shape[0]

    def body(dh2_ref, dx2_ref, x1_ref, m2_ref, za_ref, zb_ref, gt_ref, wa_ref, wb_ref, wo_ref, gpm_ref, gpf_ref,
             dx1_ref, dm2_ref, dza_ref, dzb_ref, dya_ref, dyb_ref, dp_ref, dgpm_ref, dgpf_ref):
        i = pl.program_id(0)
        r1, n1 = _rms_stats(x1_ref[...])
        dh2 = dh2_ref[...]
        dx1 = dx2_ref[...] + _rms_bwd(dh2 * gpf_ref[...], n1, r1)
        r2, n2 = _rms_stats(m2_ref[...])
        dm2 = _rms_bwd(dx1 * gpm_ref[...], n2, r2).astype(BF16)
        dmix = _dot_nt(dm2, wo_ref[...])
        sa = _sig(gt_ref[:, 0:D])
        sb = _sig(gt_ref[:, D:2 * D])
        dza = (dmix * sa).astype(BF16)
        dzb = (dmix * sb).astype(BF16)
        dp_ref[:, 0:D] = (dmix * za_ref[...].astype(F32) * sa * (1.0 - sa)).astype(BF16)
        dp_ref[:, D:2 * D] = (dmix * zb_ref[...].astype(F32) * sb * (1.0 - sb)).astype(BF16)
        dx1_ref[...] = dx1
        dm2_ref[...] = dm2
        dza_ref[...] = dza
        dzb_ref[...] = dzb
        dya_ref[...] = _dot_nt(dza, wa_ref[...]).astype(BF16)
        dyb_ref[...] = _dot_nt(dzb, wb_ref[...]).astype(BF16)
        dgpf = jnp.sum(dh2 * n1, axis=0, keepdims=True)
        dgpm = jnp.sum(dx1 * n2, axis=0, keepdims=True)

        @pl.when(i == 0)
        def _():
            dgpf_ref[...] = dgpf
            dgpm_ref[...] = dgpm

        @pl.when(i > 0)
        def _():
            dgpf_ref[...] += dgpf
            dgpm_ref[...] += dgpm

    row = lambda dt: jax.ShapeDtypeStruct((t, D), dt)
    tile = pl.BlockSpec((tt, D), lambda i: (i, 0))
    wsp = pl.BlockSpec((D, D), lambda i: (0, 0))
    vec = pl.BlockSpec((1, D), lambda i: (0, 0))
    gates = pl.BlockSpec((None, tt, 2 * D), lambda i: (SLOT_G, i, 0))
    return pl.pallas_call(
        body, name="mid_bwd",
        out_shape=(row(F32), row(BF16), row(BF16), row(BF16), row(BF16), row(BF16),
                   jax.ShapeDtypeStruct((NCHIP, t, 2 * D), BF16),
                   jax.ShapeDtypeStruct((1, D), F32), jax.ShapeDtypeStruct((1, D), F32)),
        grid=(t // tt,),
        in_specs=[tile, tile, tile, tile, tile, tile, gates, wsp, wsp, wsp, vec, vec],
        out_specs=(tile, tile, tile, tile, tile, tile, gates, vec, vec),
        compiler_params=_params(("arbitrary",), VMEM_BIG),
    )(dh2, dx2, x1, m2, za, zb, p4, wa, wb, wo, g_pm, g_pf)


def _hgrn2_bwd(p4, o_all, ss, saved, dyb, dp4, logits, gnorm, p_early, tt):
    t = p4.shape[1]
    nt = t // tt
    nc = tt // CH
    ne = len(p_early)

    def body(p_ref, o_ref, ss_ref, sv_ref, dyb_ref, dp_in, lg_ref, gn_ref, *rest):
        del dp_in
        pe = rest[:ne]
        dp_ref, dlb_ref, dgn_ref = rest[ne:ne + 3]
        qe = rest[ne + 3:2 * ne + 3]
        dst, ssem, rsem = rest[2 * ne + 3:]
        i = pl.program_id(0)
        sends, arrive = _exchange_copies(qe, [(lambda chip, r=r: r.at[chip]) for r in pe], ssem, rsem)

        @pl.when(i == 0)
        def _():
            dst[...] = jnp.zeros((NH, HD, HD), F32)
            for cp in sends:
                cp.start()

        low = _chunk_masks(tt)
        lb = _lb_of(lg_ref)
        heads = [slice(h * HD, (h + 1) * HD) for h in range(NH)]
        sg, sgn, f, k, g, eg, eng, egu, qt, kt, kd = _hg_head_fwd(p_ref[0, :, 0:D], p_ref[0, :, D:2 * D], lb,
                                                                  (sv_ref[0], sv_ref[1]))
        qtb, ktb, kdb, vb = qt.astype(BF16), kt.astype(BF16), kd.astype(BF16), p_ref[1, :, 0:D].astype(BF16)
        decs = [jnp.exp(g[c * CH + CH - 1:c * CH + CH, :]) for c in range(nc)]
        og = p_ref[1, :, D:2 * D]
        so = _sig(og)
        dyb = dyb_ref[...].astype(F32)
        dob = dyb * (og * so)
        rn = [_rms_stats(o_ref[:, hs]) for hs in heads]
        r_all = jnp.concatenate([jnp.broadcast_to(r, (tt, HD)) for r, _ in rn], axis=1)
        n_all = jnp.concatenate([n for _, n in rn], axis=1)
        gd = dob * gn_ref[...]
        proj = jnp.concatenate(
            [jnp.broadcast_to(jnp.mean(gd[:, hs] * n_all[:, hs], axis=-1, keepdims=True), (tt, HD)) for hs in heads],
            axis=1)
        dob_ = (r_all * (gd - n_all * proj)).astype(BF16)
        dog = dyb * (n_all * gn_ref[...]) * (so * (1.0 + og * (1.0 - so)))
        dgn = jnp.sum(dob * n_all, axis=0, keepdims=True)
        dv_in, dqt_in, dkt_h = [], [], []
        for hs in heads:
            att = jnp.where(low, _dot_nt(qtb[:, hs], ktb[:, hs]), 0.0).astype(BF16)
            d_att = jnp.where(low, _dot_nt(dob_[:, hs], vb[:, hs]), 0.0).astype(BF16)
            dv_in.append(_dot_tn(att, dob_[:, hs]))
            dqt_in.append(_dot(d_att, ktb[:, hs]))
            dkt_h.append(_dot_tn(d_att, qtb[:, hs]))
        ds_t = [dst[h] for h in range(NH)]
        dv_p = [[None] * NH for _ in range(nc)]
        dqt_p = [[None] * NH for _ in range(nc)]
        dkd_p = [[None] * NH for _ in range(nc)]
        dgl_p = [[None] * NH for _ in range(nc)]
        for c in reversed(range(nc)):
            sl = slice(c * CH, (c + 1) * CH)
            for h, hs in enumerate(heads):
                s_prev = ss_ref[c, h]
                ds_bf = ds_t[h].astype(BF16)
                dec = decs[c][:, hs]
                dv_p[c][h] = dv_in[h][sl] + _dot_nt(kdb[sl, hs], ds_bf)
                dqt_p[c][h] = dqt_in[h][sl] + _dot(dob_[sl, hs], s_prev)
                dkd_p[c][h] = _dot(vb[sl, hs], ds_bf)
                ddec = jnp.sum(s_prev.astype(F32) * ds_t[h], axis=0, keepdims=True)
                dgl_p[c][h] = jnp.broadcast_to(ddec * dec, (CH, HD))
                ds_t[h] = ds_t[h] * dec + _dot_tn(dob_[sl, hs], qtb[sl, hs])
        for h in range(NH):
            dst[h] = ds_t[h]
        whole = lambda parts: jnp.concatenate([jnp.concatenate(row, axis=1) for row in parts], axis=0)
        dv, dqt, dkd, dgl = whole(dv_p), whole(dqt_p), whole(dkd_p), whole(dgl_p)
        dkt = jnp.concatenate(dkt_h, axis=1)
        dq = dqt * eg
        dk = dkt * eng + dkd * egu
        dg = dqt * qt - dkt * kt
        dgu = dkd * kd
        dlogf = _chunk_revcumsum(dg - dgu) + _chunk_total(dgu) + dgl
        common = sgn * (dlogf / f - dk)
        dfz = (1.0 - lb) * sg * common
        dlb = jnp.sum(common, axis=0, keepdims=True)
        dp_ref[0, :, 0:D] = dq.astype(BF16)
        dp_ref[0, :, D:2 * D] = dfz.astype(BF16)
        dp_ref[1, :, 0:D] = dv.astype(BF16)
        dp_ref[1, :, D:2 * D] = dog.astype(BF16)

        @pl.when(i == 0)
        def _():
            dlb_ref[0:1, :] = dlb
            dgn_ref[...] = dgn

        @pl.when(i > 0)
        def _():
            dlb_ref[0:1, :] += dlb
            dgn_ref[...] += dgn

        @pl.when(i == nt - 1)
        def _():
            d0 = dlb_ref[0:1, :] * lb * (1.0 - lb)
            dlb_ref[0:1, :] = d0
            dlb_ref[1:2, :] = -d0
            for cp in arrive:
                cp.wait_recv()
            for cp in sends:
                cp.wait_send()

    rev = lambda i: nt - 1 - i
    vec = pl.BlockSpec((1, D), lambda i: (0, 0))
    any_spec = pl.BlockSpec(memory_space=pl.ANY)
    outs = pl.pallas_call(
        body, name="hgrn2_bwd",
        out_shape=(jax.ShapeDtypeStruct(dp4.shape, BF16), jax.ShapeDtypeStruct((2, D), F32),
                   jax.ShapeDtypeStruct((1, D), F32)) + tuple(jax.ShapeDtypeStruct(a.shape, BF16) for a in p_early),
        grid=(nt,),
        in_specs=[pl.BlockSpec((2, tt, 2 * D), lambda i: (0, rev(i), 0)),
                  pl.BlockSpec((tt, D), lambda i: (rev(i), 0)),
                  pl.BlockSpec((nc, NH, HD, HD), lambda i: (rev(i), 0, 0, 0)),
                  pl.BlockSpec((2, tt, D), lambda i: (0, rev(i), 0)),
                  pl.BlockSpec((tt, D), lambda i: (rev(i), 0)),
                  any_spec,
                  pl.BlockSpec((2, D), lambda i: (0, 0)), vec] + [any_spec] * ne,
        out_specs=(pl.BlockSpec((2, tt, 2 * D), lambda i: (0, rev(i), 0)),
                   pl.BlockSpec((2, D), lambda i: (0, 0)), vec) + (any_spec,) * ne,
        scratch_shapes=[pltpu.VMEM((NH, HD, HD), F32), pltpu.SemaphoreType.DMA((3 * ne,)),
                        pltpu.SemaphoreType.DMA((3 * ne,))],
        input_output_aliases={5: 0},
        compiler_params=pltpu.CompilerParams(dimension_semantics=("arbitrary",), vmem_limit_bytes=VMEM_BIG,
                                             has_side_effects=True),
    )(p4, o_all, ss, saved, dyb, dp4, logits, gnorm, *p_early)
    return outs[0], outs[1], outs[2], list(outs[3:])


def _mixer_a_bwd(p4, hseq, saved, dya, dp4, cw, wa, wx, lam, tt):
    t = p4.shape[1]
    nt = t // tt

    def body(p_ref, sv_ref, h_ref, hh_ref, dya_ref, dp_in, cw_ref, wa_ref, wx_ref, lam_ref,
             dp_ref, dcw_ref, dcb_ref, dwa_ref, dwx_ref, dba_ref, dbx_ref, dlam_ref,
             dnext, dhc, afc):
        del dp_in
        i = pl.program_id(0)
        first_tile = i == nt - 1

        @pl.when(i == 0)
        def _():
            dnext[...] = jnp.zeros((8, D), F32)
            dhc[...] = jnp.zeros((1, D), F32)
            afc[...] = jnp.zeros((1, D), F32)

        xa = p_ref[:, 0:D]
        ga = p_ref[:, D:2 * D]
        xc, r, ig, a, mult = (sv_ref[idx] for idx in range(5))
        xcb = xc.astype(BF16)
        lam = lam_ref[...]
        sp = jnp.maximum(-lam, 0.0) + jnp.log1p(jnp.exp(-jnp.abs(lam)))
        h = h_ref[...]
        gl, dgl = _gelu_and_grad(ga)
        dya = dya_ref[...].astype(F32)
        dga = dya * h * dgl
        rows = lax.broadcasted_iota(jnp.int32, (tt, 1), 0)
        a_next = jnp.where(rows == tt - 1, afc[...], pltpu.roll(a, tt - 1, 0))
        dh = _scan_up(a_next, dya * gl, dhc[...])
        dhc[...] = dh[0:1, :]
        afc[...] = a[0:1, :]
        h_prev = jnp.where(rows == 0, jnp.where(first_tile, 0.0, hh_ref[7:8, :]), pltpu.roll(h, 1, 0))
        da = dh * h_prev
        dmult = dh * ig * xc
        di = dh * mult * xc
        dlog_a = da * a - dmult * a * a / mult
        dr = dlog_a * (-LRU_C * sp)
        dsp = jnp.sum(dlog_a * (-LRU_C * r), axis=0, keepdims=True)
        dra = dr * r * (1.0 - r)
        dix = di * ig * (1.0 - ig)
        drab = dra.astype(BF16)
        dixb = dix.astype(BF16)
        dxc_lin = []
        dwa_new = []
        dwx_new = []
        for n in range(NH):
            cs = slice(n * HD, (n + 1) * HD)
            dxc_lin.append(_dot_nt(drab[:, cs], wa_ref[n]) + _dot_nt(dixb[:, cs], wx_ref[n]))
            dwa_new.append(_dot_tn(xcb[:, cs], drab[:, cs]))
            dwx_new.append(_dot_tn(xcb[:, cs], dixb[:, cs]))
        dxc = dh * mult * ig + jnp.concatenate(dxc_lin, axis=1)
        de = jnp.concatenate([dxc, dnext[...]], axis=0)
        ups = [de[0:tt, :]] + [pltpu.roll(de, tt + 8 - s, 0)[0:tt, :] for s in (1, 2, 3)]
        dxa = sum(cw_ref[3 - s:4 - s, :] * ups[s] for s in range(4))
        dnext[...] = dxc[0:8, :]
        dp_ref[:, 0:D] = dxa.astype(BF16)
        dp_ref[:, D:2 * D] = dga.astype(BF16)
        dcw = jnp.concatenate(
            [jnp.sum(ups[3 - k] * xa, axis=0, keepdims=True) for k in range(4)], axis=0)
        dcb = jnp.sum(dxc, axis=0, keepdims=True)
        dba = jnp.sum(dra, axis=0, keepdims=True)
        dbx = jnp.sum(dix, axis=0, keepdims=True)
        dlam = dsp * (-_sig(-lam))

        @pl.when(i == 0)
        def _():
            dcw_ref[...] = dcw
            dcb_ref[...] = dcb
            dba_ref[...] = dba
            dbx_ref[...] = dbx
            dlam_ref[...] = dlam
            for n in range(NH):
                dwa_ref[n] = dwa_new[n]
                dwx_ref[n] = dwx_new[n]

        @pl.when(i > 0)
        def _():
            dcw_ref[...] += dcw
            dcb_ref[...] += dcb
            dba_ref[...] += dba
            dbx_ref[...] += dbx
            dlam_ref[...] += dlam
            for n in range(NH):
                dwa_ref[n] += dwa_new[n]
                dwx_ref[n] += dwx_new[n]

    rev = lambda i: nt - 1 - i
    hb = tt // 8
    full = lambda shape: pl.BlockSpec(shape, lambda i: (0,) * len(shape))
    vecs = jax.ShapeDtypeStruct((1, D), F32)
    blk = jax.ShapeDtypeStruct((NH, HD, HD), F32)
    return pl.pallas_call(
        body, name="mixer_a_bwd",
        out_shape=(jax.ShapeDtypeStruct(dp4.shape, BF16), jax.ShapeDtypeStruct((4, D), F32), vecs, blk, blk,
                   vecs, vecs, vecs),
        grid=(nt,),
        in_specs=[pl.BlockSpec((None, tt, 2 * D), lambda i: (SLOT_A, rev(i), 0)),
                  pl.BlockSpec((5, tt, D), lambda i: (0, rev(i), 0)),
                  pl.BlockSpec((tt, D), lambda i: (rev(i), 0)),
                  pl.BlockSpec((8, D), lambda i: (jnp.maximum(rev(i) * hb - 1, 0), 0)),
                  pl.BlockSpec((tt, D), lambda i: (rev(i), 0)),
                  pl.BlockSpec(memory_space=pl.ANY),
                  full((4, D)), full((NH, HD, HD)), full((NH, HD, HD)), full((1, D))],
        out_specs=(pl.BlockSpec((None, tt, 2 * D), lambda i: (SLOT_A, rev(i), 0)),
                   full((4, D)), full((1, D)), full((NH, HD, HD)), full((NH, HD, HD)),
                   full((1, D)), full((1, D)), full((1, D))),
        scratch_shapes=[pltpu.VMEM((8, D), F32), pltpu.VMEM((1, D), F32), pltpu.VMEM((1, D), F32)],
        input_output_aliases={5: 0},
        compiler_params=_params(("arbitrary",), VMEM_BIG),
    )(p4, saved, hseq, hseq, dya, dp4, cw, wa, wx, lam)


def _local_step(x, tgt, stk_w_in, stk_rest, conv_a_s, conv_f_s, small, jc, cidx):
    t = x.shape[0]
    tt = min(256, t)
    tm = min(1024, t)
    tk = min(2048, t)
    wa_bf = small["lru_wa"].astype(BF16)
    wx_bf = small["lru_wx"].astype(BF16)

    h1 = _norm_fwd(x, small["norm_pre_mix"], tt)
    p4, w_in, stk_rest, conv_a_g, conv_f_g = _mm_in_gather(h1, stk_w_in, stk_rest, REST, conv_a_s, conv_f_s, jc, tm)
    conv_a_w = jnp.transpose(conv_a_g, (1, 0, 2)).reshape(8, D)[0:4]
    conv_f_w = jnp.transpose(conv_f_g, (1, 0, 2)).reshape(8, DUP)[0:3]
    w = dict(zip(REST, _gather_forward(stk_rest, REST)))
    w["w_in"] = w_in
    w_br_a = w["w_branch_a"].reshape(D, D)
    w_br_b = w["w_branch_b"].reshape(D, D)
    w_out = w["w_out"].reshape(D, D)
    w_down = w["w_down"].reshape(DFF, D)
    ya, hseq, saved_a = _mixer_a_fwd(p4, conv_a_w, small["conv_a_b"], wa_bf, wx_bf, small["lru_ba"], small["lru_bx"],
                            small["lru_lambda"], tt)
    yb, o_all, ss, saved_b = _hgrn2_fwd(p4, small["hg_lb_logits"], small["hg_norm_g"], tt)
    za, zb, mixb, m2, x1, h2 = _mid_fwd(ya, yb, p4, x, w_br_a, w_br_b, w_out, small["norm_post_mix"],
                                        small["norm_pre_ffn"], min(512, t))
    pre_g, pre_v, y, uv, gl, dgl = _up_act_fwd(h2, w["w_up"], conv_f_w, small["conv_f_b"], tt)
    dx2, dm3, lossv, d_norm_post_ffn = _down_loss(y, w_down, x1, tgt, small["norm_post_ffn"], min(512, t))

    d_w_down = _mm_tn(y, dm3, DFF // 2, D, tk, "mm_dw_down")
    dy = _mm_nt(dm3, w_down, BF16, tm, "mm_dy")
    dup_pre, d_conv_f_w, d_conv_f_b = _ffn_act_bwd(dy, pre_g, pre_v, uv, gl, dgl, conv_f_w, tt)
    d_w_up = _mm_tn(h2, dup_pre, D, SH_UP, tk, "mm_dw_up", stacked_out=True)
    dh2 = _mm_nt_sharded(dup_pre, w["w_up"], tm, "mm_dh2")
    dx1, dm2, dza, dzb, dya, dyb, dp4, d_norm_post_mix, d_norm_pre_ffn = _mid_bwd(
        dh2, dx2, x1, m2, za, zb, p4, w_br_a, w_br_b, w_out, small["norm_post_mix"], small["norm_pre_ffn"], tt)
    d_w_out = _mm_tn(mixb, dm2, D, D, tm, "mm_dw_out")
    d_w_br_a = _mm_tn(ya, dza, D, D, tm, "mm_dw_bra")
    d_w_br_b = _mm_tn(yb, dzb, D, D, tm, "mm_dw_brb")
    early = {"w_branch_a": d_w_br_a.reshape(NCHIP, SH_BR, D), "w_branch_b": d_w_br_b.reshape(NCHIP, SH_BR, D),
             "w_out": d_w_out.reshape(NCHIP, SH_BR, D), "w_up": d_w_up, "w_down": d_w_down.reshape(NCHIP, SH_DN, D)}
    rb, _ = _reduce_stage1(early, REST, (), "reduce_d2d_in_early")
    p_rest = [_sum_own_half(early[n], rb[n], cidx, "sum_half_" + n) for n in REST]
    dp4, d_lb, d_hg_norm_g, q_rest = _hgrn2_bwd(p4, o_all, ss, saved_b, dyb, dp4, small["hg_lb_logits"], small["hg_norm_g"],
                                                p_rest, tt)
    dp4, d_conv_a_w, d_conv_a_b, d_lru_wa, d_lru_wx, d_lru_ba, d_lru_bx, d_lru_lambda = _mixer_a_bwd(
        p4, hseq, saved_a, dya, dp4, conv_a_w, wa_bf, wx_bf, small["lru_lambda"], tt)
    d_w_in = _mm_tn(h1, dp4, D, SH_IN, tk, "mm_dw_in", stacked_slot_fn=_slot_of_chip, stacked_out=True)
    rb, _ = _reduce_stage1({"w_in": d_w_in}, ("w_in",), (), "reduce_d2d_in_w_in")
    p_w_in = _sum_own_half(d_w_in, rb["w_in"], cidx, "sum_half_w_in")
    grad_x, q_w_in, d_norm_pre_mix = _mm_dh1_exchange(dp4, w_in, p_w_in, x, dx1, small["norm_pre_mix"], tm)

    smalls = {
        "norm_pre_mix": d_norm_pre_mix, "conv_a_b": d_conv_a_b, "lru_ba": d_lru_ba, "lru_bx": d_lru_bx,
        "lru_lambda": d_lru_lambda, "hg_lb_logits": d_lb, "hg_norm_g": d_hg_norm_g, "norm_post_mix": d_norm_post_mix,
        "norm_pre_ffn": d_norm_pre_ffn, "norm_post_ffn": d_norm_post_ffn, "lossv": lossv,
        "conv_a_w": d_conv_a_w, "lru_wa": d_lru_wa, "lru_wx": d_lru_wx,
        "conv_f_b": d_conv_f_b, "conv_f_w": d_conv_f_w,
    }
    p_big = dict(zip(REST, p_rest), w_in=p_w_in)
    q_big = dict(zip(REST, q_rest), w_in=q_w_in)
    return grad_x, p_big, q_big, smalls


BIG = ("w_in", "w_branch_a", "w_branch_b", "w_out", "w_up", "w_down")
BIG_SHAPE = {"w_in": (D, SH_IN), "w_branch_a": (SH_BR, D), "w_branch_b": (SH_BR, D), "w_out": (SH_BR, D),
             "w_up": (D, SH_UP), "w_down": (SH_DN, D)}
NBIG = len(BIG)
REST = BIG[1:]
VEC_ROWS = (("norm_pre_mix", 0, 1), ("conv_a_b", 1, 1), ("lru_ba", 2, 1), ("lru_bx", 3, 1), ("lru_lambda", 4, 1),
            ("hg_lb_logits", 5, 2), ("hg_norm_g", 7, 1), ("norm_post_mix", 8, 1), ("norm_pre_ffn", 9, 1),
            ("norm_post_ffn", 10, 1))
ROW_LOSS = 11
ROW_CONV_A = 12
S1_ROWS = 16
S2_ROWS = 8


def _place():
    x, y, c = lax.axis_index("x"), lax.axis_index("y"), lax.axis_index("c")
    chips = [(1 - x, y), (x, 1 - y), (1 - x, 1 - y)]
    return x, y, c, 2 * x + y, chips


def _remote(src, dst, ssem, rsem, dev):
    return pltpu.make_async_remote_copy(src_ref=src, dst_ref=dst, send_sem=ssem, recv_sem=rsem,
                                        device_id=dev, device_id_type=MESH)


def _hbm_call(body, name, ins, out_shapes, n_sems, aliases=None):
    any_spec = pl.BlockSpec(memory_space=pl.ANY)
    return pl.pallas_call(
        body, name=name, out_shape=tuple(out_shapes),
        in_specs=[any_spec] * len(ins), out_specs=tuple([any_spec] * len(out_shapes)),
        scratch_shapes=[pltpu.SemaphoreType.DMA((n,)) for n in n_sems],
        input_output_aliases=aliases or {},
        compiler_params=pltpu.CompilerParams(has_side_effects=True),
    )(*ins)


def _gather_copies(stk, names, ssem, rsem, fssem=None, frsem=None):
    x, y, c, j, chips = _place()
    sends, arrive, fwds, farrive = [], [], [], []
    for w, n in enumerate(names):
        hw = BIG_SHAPE[n][0] // 2
        mine = stk[w].at[j, pl.ds(c * hw, hw), :]
        for k, (cx, cy) in enumerate(chips):
            i = 3 * w + k
            got = stk[w].at[2 * cx + cy, pl.ds(c * hw, hw), :]
            other = stk[w].at[2 * cx + cy, pl.ds((1 - c) * hw, hw), :]
            sends.append(_remote(mine, mine, ssem.at[i], rsem.at[i], (cx, cy, c)))
            arrive.append(_remote(got, got, ssem.at[i], rsem.at[i], (cx, cy, c)))
            if fssem is not None:
                fwds.append(_remote(got, got, fssem.at[i], frsem.at[i], (x, y, 1 - c)))
                farrive.append(_remote(other, other, fssem.at[i], frsem.at[i], (x, y, 1 - c)))
    return sends, arrive, fwds, farrive


def _gather_forward(stk, names):
    nw = len(names)

    def body(*refs):
        dst = refs[nw:2 * nw]
        ssem, rsem, fssem, frsem = refs[2 * nw:]
        _, _, fwds, farrive = _gather_copies(dst, names, ssem, rsem, fssem, frsem)
        for cp in fwds:
            cp.start()
        for cp in farrive:
            cp.wait_recv()
        for cp in fwds:
            cp.wait_send()

    out_shapes = [jax.ShapeDtypeStruct(a.shape, a.dtype) for a in stk]
    return _hbm_call(body, "gather_forward", stk, out_shapes, (3 * nw,) * 4, aliases={w: w for w in range(nw)})


def _exchange_copies(dst, pieces, ssem, rsem):
    x, y, c, j, chips = _place()
    sends, arrive = [], []
    for w in range(len(dst)):
        for k, (cx, cy) in enumerate(chips):
            i = 3 * w + k
            sends.append(_remote(pieces[w](2 * cx + cy), dst[w].at[j], ssem.at[i], rsem.at[i], (cx, cy, c)))
            got = dst[w].at[2 * cx + cy]
            arrive.append(_remote(got, got, ssem.at[i], rsem.at[i], (cx, cy, c)))
    return sends, arrive


def _reduce_stage1(big_g, names, smalls, name):
    nb = len(names)
    ins = [big_g[n] for n in names] + list(smalls)
    n_in = len(ins)
    halves = [BIG_SHAPE[n][0] // 2 for n in names]
    out_shapes = [jax.ShapeDtypeStruct((NCHIP, halves[w], BIG_SHAPE[n][1]), big_g[n].dtype)
                  for w, n in enumerate(names)]
    out_shapes += [jax.ShapeDtypeStruct(a.shape, F32) for a in smalls]

    def body(*refs):
        src, dst = refs[:n_in], refs[n_in:2 * n_in]
        ssem, rsem = refs[2 * n_in:]
        x, y, c, _, _ = _place()
        cps = []
        for w in range(n_in):
            s_ = src[w].at[:, pl.ds((1 - c) * halves[w], halves[w]), :] if w < nb else src[w]
            cp = _remote(s_, dst[w], ssem.at[w], rsem.at[w], (x, y, 1 - c))
            cp.start()
            cps.append(cp)
        for cp in cps:
            cp.wait()

    outs = _hbm_call(body, name, ins, out_shapes, (n_in, n_in))
    return dict(zip(names, outs[:nb])), outs[nb:]


def _reduce_stage2(ps1, ps2, ps3):
    ins = [ps1, ps2, ps3]
    h1, h2, h3 = S1_ROWS // 2, DUP // 2, D
    out_shapes = [jax.ShapeDtypeStruct((NCHIP, h1, D), F32), jax.ShapeDtypeStruct((NCHIP, S2_ROWS, h2), F32),
                  jax.ShapeDtypeStruct((NCHIP, h3, HD), F32)]

    def body(*refs):
        src, dst = refs[:3], refs[3:6]
        ssem, rsem = refs[6:]
        c = lax.axis_index("c")
        pieces = [lambda chip: src[0].at[pl.ds(c * h1, h1), :],
                  lambda chip: src[1].at[:, pl.ds(c * h2, h2)],
                  lambda chip: src[2].at[pl.ds(c * h3, h3), :]]
        sends, arrive = _exchange_copies(dst, pieces, ssem, rsem)
        for cp in sends:
            cp.start()
        for cp in arrive:
            cp.wait_recv()
        for cp in sends:
            cp.wait_send()

    return _hbm_call(body, "reduce_ici_small", ins, out_shapes, (9, 9))


def _reduce_stage3(f_big, fs1, fs2, fs3):
    ins = [f_big[n] for n in BIG] + [fs1, fs2, fs3]
    n_in = len(ins)
    halves = [BIG_SHAPE[n][0] // 2 for n in BIG]
    h1, h2, h3 = S1_ROWS // 2, DUP // 2, D
    out_shapes = [jax.ShapeDtypeStruct(BIG_SHAPE[n], F32) for n in BIG]
    out_shapes += [jax.ShapeDtypeStruct((S1_ROWS, D), F32), jax.ShapeDtypeStruct((S2_ROWS, DUP), F32),
                   jax.ShapeDtypeStruct((2 * D, HD), F32)]

    def body(*refs):
        dst = refs[n_in:2 * n_in]
        ssem, rsem = refs[2 * n_in:]
        x, y, c, _, _ = _place()

        def place(w, which):
            if w < NBIG:
                return dst[w].at[pl.ds(which * halves[w], halves[w]), :]
            if w == NBIG:
                return dst[w].at[pl.ds(which * h1, h1), :]
            if w == NBIG + 1:
                return dst[w].at[:, pl.ds(which * h2, h2)]
            return dst[w].at[pl.ds(which * h3, h3), :]

        cps = [_remote(place(w, c), place(w, c), ssem.at[w], rsem.at[w], (x, y, 1 - c)) for w in range(n_in)]
        for cp in cps:
            cp.start()
        for w in range(n_in):
            got = place(w, 1 - c)
            _remote(got, got, ssem.at[w], rsem.at[w], (x, y, 1 - c)).wait_recv()
        for cp in cps:
            cp.wait_send()

    outs = _hbm_call(body, "reduce_d2d_out", ins, out_shapes, (n_in, n_in), aliases={w: w for w in range(n_in)})
    return dict(zip(BIG, outs[:NBIG])), outs[NBIG], outs[NBIG + 1], outs[NBIG + 2]


def _row_tile(rows):
    for tr in (128, 176, 64, 16, 8):
        if rows % tr == 0:
            return tr
    return rows


def _sum_own_half(g, rb, cidx, name):
    s, rows, cols = g.shape
    half = rows // 2
    tr = _row_tile(half)
    nb = half // tr

    def body(c_ref, g_ref, r_ref, o_ref):
        del c_ref
        o_ref[...] = (g_ref[...].astype(F32) + r_ref[...].astype(F32)).astype(BF16)

    grid_spec = pltpu.PrefetchScalarGridSpec(
        num_scalar_prefetch=1, grid=(s, nb),
        in_specs=[pl.BlockSpec((None, tr, cols), lambda k, i, c: (k, c[0] * nb + i, 0)),
                  pl.BlockSpec((None, tr, cols), lambda k, i, c: (k, i, 0))],
        out_specs=pl.BlockSpec((None, tr, cols), lambda k, i, c: (k, i, 0)))
    return pl.pallas_call(
        body, name=name, grid_spec=grid_spec, out_shape=jax.ShapeDtypeStruct((s, half, cols), BF16),
        compiler_params=_params(("parallel", "parallel")),
    )(cidx, g, rb)


def _sum_chips(q, p, jc, name, by_cols=False):
    s, rows, cols = q.shape
    tr = _row_tile(rows)
    nb = rows // tr
    stacked = p.ndim == 3

    def body(jc_ref, q_ref, p_ref, o_ref):
        j = jc_ref[0]
        own = p_ref[...].astype(F32)
        acc = None
        for k in range(NCHIP):
            term = jnp.where(j == k, own, q_ref[k].astype(F32))
            acc = term if acc is None else acc + term
        o_ref[...] = acc

    if by_cols:
        half_spec = pl.BlockSpec((tr, cols), lambda i, jc_ref: (i, jc_ref[1]))
        out_shape = jax.ShapeDtypeStruct((rows, 2 * cols), F32)
    else:
        half_spec = pl.BlockSpec((tr, cols), lambda i, jc_ref: (jc_ref[1] * nb + i, 0))
        out_shape = jax.ShapeDtypeStruct((2 * rows, cols), F32)
    p_spec = pl.BlockSpec((None, tr, cols), lambda i, jc_ref: (jc_ref[0], i, 0)) if stacked else half_spec
    grid_spec = pltpu.PrefetchScalarGridSpec(
        num_scalar_prefetch=1, grid=(nb,),
        in_specs=[pl.BlockSpec((s, tr, cols), lambda i, jc_ref: (0, i, 0)), p_spec],
        out_specs=half_spec)
    return pl.pallas_call(
        body, name=name, grid_spec=grid_spec, out_shape=out_shape,
        compiler_params=_params(("parallel",)),
    )(jc, q, p)


def _place_shard(w, jc, name):
    rows, cols = w.shape
    tr = _row_tile(rows)

    def body(jc_ref, w_ref, o_ref):
        del jc_ref
        o_ref[...] = w_ref[...].astype(BF16)

    grid_spec = pltpu.PrefetchScalarGridSpec(
        num_scalar_prefetch=1, grid=(rows // tr,),
        in_specs=[pl.BlockSpec((tr, cols), lambda i, jc_ref: (i, 0))],
        out_specs=pl.BlockSpec((None, tr, cols), lambda i, jc_ref: (jc_ref[0], i, 0)))
    return pl.pallas_call(
        body, name=name, grid_spec=grid_spec, out_shape=jax.ShapeDtypeStruct((NCHIP, rows, cols), BF16),
        compiler_params=_params(("parallel",)),
    )(jc, w)


def _add(a, b, name):
    def body(a_ref, b_ref, o_ref):
        o_ref[...] = a_ref[...] + b_ref[...]

    return pl.pallas_call(body, name=name, out_shape=jax.ShapeDtypeStruct(a.shape, F32))(a, b)


def _pack_small(sm):
    vec_in = [sm[n] for n, _, _ in VEC_ROWS]
    nv = len(vec_in)

    def body(*refs):
        ins, lossv, dcw, dcfb, dcfw, s1, s2 = refs[:nv], refs[nv], refs[nv + 1], refs[nv + 2], refs[nv + 3], \
            refs[nv + 4], refs[nv + 5]
        for ref, (_, r0, nr) in zip(ins, VEC_ROWS):
            s1[r0:r0 + nr, :] = ref[...]
        s1[ROW_LOSS:ROW_LOSS + 1, :] = lossv[...]
        s1[ROW_CONV_A:ROW_CONV_A + 4, :] = dcw[...]
        s2[0:1, :] = dcfb[...]
        s2[1:4, :] = dcfw[...]
        s2[4:8, :] = jnp.zeros((4, DUP), F32)

    return pl.pallas_call(
        body, name="pack_small",
        out_shape=(jax.ShapeDtypeStruct((S1_ROWS, D), F32), jax.ShapeDtypeStruct((S2_ROWS, DUP), F32)),
    )(*vec_in, sm["lossv"], sm["conv_a_w"], sm["conv_f_b"], sm["conv_f_w"])


def _adam_math(w, g, m, v):
    m = ADAM_B1 * m + (1.0 - ADAM_B1) * g
    v = ADAM_B2 * v + (1.0 - ADAM_B2) * (g * g)
    m_hat = m / (1.0 - ADAM_B1 ** ADAM_STEP)
    v_hat = v / (1.0 - ADAM_B2 ** ADAM_STEP)
    delta = -ADAM_LR * (m_hat / (jnp.sqrt(v_hat) + ADAM_EPS) + ADAM_WD * w)
    return delta, m, v


def _adam(w, g, m, v, name):
    rows, cols = w.shape
    tr = _row_tile(rows)

    def body(w_ref, g_ref, m_ref, v_ref, d_ref, mo_ref, vo_ref):
        d_ref[...], mo_ref[...], vo_ref[...] = _adam_math(w_ref[...], g_ref[...], m_ref[...], v_ref[...])

    spec = pl.BlockSpec((tr, cols), lambda i: (i, 0))
    return pl.pallas_call(
        body, name=name, out_shape=(jax.ShapeDtypeStruct(w.shape, F32),) * 3, grid=(rows // tr,),
        in_specs=[spec] * 4, out_specs=(spec,) * 3,
        compiler_params=_params(("parallel",)),
    )(w, g, m, v)


def _adam_small(gs1, gs2, gs3, w, m, v):
    names = [n for n, _, _ in VEC_ROWS] + ["conv_f_b", "lru_wa", "lru_wx"]
    nn = len(names)

    def grad_of(i, g1, g2, g3):
        if i < len(VEC_ROWS):
            _, r0, nr = VEC_ROWS[i]
            return g1[r0:r0 + nr, :]
        if names[i] == "conv_f_b":
            return g2[0:1, :]
        return g3[0] if names[i] == "lru_wa" else g3[1]

    def body(*refs):
        g1, g2, g3 = refs[0], refs[1], refs[2]
        ws, ms, vs = refs[3:3 + nn], refs[3 + nn:3 + 2 * nn], refs[3 + 2 * nn:3 + 3 * nn]
        outs = refs[3 + 3 * nn:]
        for i in range(nn):
            d, mn, vn = _adam_math(ws[i][...], grad_of(i, g1, g2, g3), ms[i][...], vs[i][...])
            outs[i][...] = d
            outs[nn + i][...] = mn
            outs[2 * nn + i][...] = vn

    shapes = [jax.ShapeDtypeStruct(w[n].shape, F32) for n in names]
    outs = pl.pallas_call(body, name="adam_small", out_shape=tuple(shapes * 3))(
        gs1, gs2, gs3, *[w[n] for n in names], *[m[n] for n in names], *[v[n] for n in names])
    return {n: (outs[i], outs[nn + i], outs[2 * nn + i]) for i, n in enumerate(names)}


WEIGHTS = ("norm_pre_mix", "w_in", "conv_a_w", "conv_a_b", "lru_wa", "lru_ba", "lru_wx", "lru_bx", "lru_lambda",
           "hg_lb_logits", "hg_norm_g", "w_branch_a", "w_branch_b", "w_out", "norm_post_mix", "norm_pre_ffn",
           "w_up", "conv_f_w", "conv_f_b", "w_down", "norm_post_ffn")
NW = len(WEIGHTS)


def kernel(x, norm_pre_mix, w_in, conv_a_w, conv_a_b, lru_wa, lru_ba, lru_wx, lru_bx, lru_lambda, hg_lb_logits, hg_norm_g, w_branch_a, w_branch_b, w_out, norm_post_mix, norm_pre_ffn, w_up, conv_f_w, conv_f_b, w_down, norm_post_ffn, loss_target, m_norm_pre_mix, m_w_in, m_conv_a_w, m_conv_a_b, m_lru_wa, m_lru_ba, m_lru_wx, m_lru_bx, m_lru_lambda, m_hg_lb_logits, m_hg_norm_g, m_w_branch_a, m_w_branch_b, m_w_out, m_norm_post_mix, m_norm_pre_ffn, m_w_up, m_conv_f_w, m_conv_f_b, m_w_down, m_norm_post_ffn, v_norm_pre_mix, v_w_in, v_conv_a_w, v_conv_a_b, v_lru_wa, v_lru_ba, v_lru_wx, v_lru_bx, v_lru_lambda, v_hg_lb_logits, v_hg_norm_g, v_w_branch_a, v_w_branch_b, v_w_out, v_norm_post_mix, v_norm_pre_ffn, v_w_up, v_conv_f_w, v_conv_f_b, v_w_down, v_norm_post_ffn):
    rest = (norm_pre_mix, w_in, conv_a_w, conv_a_b, lru_wa, lru_ba, lru_wx, lru_bx, lru_lambda, hg_lb_logits, hg_norm_g, w_branch_a, w_branch_b, w_out, norm_post_mix, norm_pre_ffn, w_up, conv_f_w, conv_f_b, w_down, norm_post_ffn, loss_target, m_norm_pre_mix, m_w_in, m_conv_a_w, m_conv_a_b, m_lru_wa, m_lru_ba, m_lru_wx, m_lru_bx, m_lru_lambda, m_hg_lb_logits, m_hg_norm_g, m_w_branch_a, m_w_branch_b, m_w_out, m_norm_post_mix, m_norm_pre_ffn, m_w_up, m_conv_f_w, m_conv_f_b, m_w_down, m_norm_post_ffn, v_norm_pre_mix, v_w_in, v_conv_a_w, v_conv_a_b, v_lru_wa, v_lru_ba, v_lru_wx, v_lru_bx, v_lru_lambda, v_hg_lb_logits, v_hg_norm_g, v_w_branch_a, v_w_branch_b, v_w_out, v_norm_post_mix, v_norm_pre_ffn, v_w_up, v_conv_f_w, v_conv_f_b, v_w_down, v_norm_post_ffn)
    w_in_args = dict(zip(WEIGHTS, rest[:NW]))
    loss_target = rest[NW]
    m_args = dict(zip(WEIGHTS, rest[NW + 1:2 * NW + 1]))
    v_args = dict(zip(WEIGHTS, rest[2 * NW + 1:3 * NW + 1]))
    shape_of = {n: w_in_args[n].shape for n in WEIGHTS}

    def two_d(n, a):
        if n in BIG:
            return a.reshape(BIG_SHAPE[n])
        if n in ("lru_wa", "lru_wx"):
            return a.reshape(NH, HD, HD)
        return a.reshape(a.shape[-2:])

    w2 = {n: two_d(n, w_in_args[n]) for n in WEIGHTS}
    m2 = {n: two_d(n, m_args[n]) for n in WEIGHTS}
    v2 = {n: two_d(n, v_args[n]) for n in WEIGHTS}

    cidx = lax.axis_index("c").astype(jnp.int32).reshape(1)
    jchip = 2 * lax.axis_index("x") + lax.axis_index("y")

    jc = jnp.stack([jchip, lax.axis_index("c")]).astype(jnp.int32)

    shards = {n: _place_shard(w2[n], jc, "place_" + n) for n in BIG}
    conv_a_s = jnp.pad(w2["conv_a_w"], ((0, 4), (0, 0)))
    conv_f_s = jnp.pad(w2["conv_f_w"], ((0, 5), (0, 0)))
    small = {n: w2[n] for n in WEIGHTS if n not in BIG and n not in ("conv_a_w", "conv_f_w")}

    grad_x, p_big, q_big, sm_g = _local_step(
        x[0], loss_target[0], shards["w_in"], [shards[n] for n in REST], conv_a_s, conv_f_s, small, jc, cidx)

    s1, s2 = _pack_small(sm_g)
    s3 = jnp.concatenate([sm_g["lru_wa"].reshape(D, HD), sm_g["lru_wx"].reshape(D, HD)], axis=0)
    _, (rs1, rs2, rs3) = _reduce_stage1({}, (), (s1, s2, s3), "reduce_d2d_in_small")
    ps1, ps2, ps3 = _add(s1, rs1, "add_s1"), _add(s2, rs2, "add_s2"), _add(s3, rs3, "add_s3")
    qs1, qs2, qs3 = _reduce_stage2(ps1, ps2, ps3)
    f_big = {n: _sum_chips(q_big[n], p_big[n], jc, "sum_chips_" + n) for n in BIG}
    fs1 = _sum_chips(qs1, ps1, jc, "sum_chips_s1")
    fs2 = _sum_chips(qs2, ps2, jc, "sum_chips_s2", by_cols=True)
    fs3 = _sum_chips(qs3, ps3, jc, "sum_chips_s3")
    g_big, gs1, gs2, gs3 = _reduce_stage3(f_big, fs1, fs2, fs3)

    res = {}
    for n in BIG:
        d, mn, vn = _adam(w2[n], g_big[n], m2[n], v2[n], "adam_" + n)
        res[n] = (g_big[n], d, mn, vn)
    small_res = _adam_small(gs1, gs2, gs3.reshape(2, NH, HD, HD), w2, m2, v2)
    for n, r0, nr in VEC_ROWS:
        res[n] = (gs1[r0:r0 + nr],) + small_res[n]
    res["conv_f_b"] = (gs2[0:1],) + small_res["conv_f_b"]
    res["lru_wa"] = (gs3[0:D].reshape(NH, HD, HD),) + small_res["lru_wa"]
    res["lru_wx"] = (gs3[D:2 * D].reshape(NH, HD, HD),) + small_res["lru_wx"]
    g_ca = lax.dynamic_slice_in_dim(gs1[ROW_CONV_A:ROW_CONV_A + 4], jchip * (D // NCHIP), D // NCHIP, axis=1)
    g_cf = lax.dynamic_slice_in_dim(gs2[1:4], jchip * SH_UP, SH_UP, axis=1)
    res["conv_a_w"] = (g_ca,) + _adam(w2["conv_a_w"], g_ca, m2["conv_a_w"], v2["conv_a_w"], "adam_conv_a_w")
    res["conv_f_w"] = (g_cf,) + _adam(w2["conv_f_w"], g_cf, m2["conv_f_w"], v2["conv_f_w"], "adam_conv_f_w")

    loss = (0.5 / D) * jnp.sum(gs1[ROW_LOSS])
    out = [loss, grad_x.reshape(x.shape)]
    for part in range(4):
        out += [res[n][part].reshape(shape_of[n]) for n in WEIGHTS]
    return tuple(out)
```

```python
import functools

import jax
import jax.numpy as jnp
from jax import lax
from jax.experimental import pallas as pl
from jax.experimental.pallas import tpu as pltpu

F32 = jnp.float32
BF16 = jnp.bfloat16

D = 1024
NH = 8
HD = 128
CH = 32
DFF = 2816
DUP = 2 * DFF
NCHIP = 4
SH_IN = 2 * D
SH_UP = DUP // NCHIP
SH_DN = DFF // NCHIP
SH_BR = D // NCHIP
EPS = 1e-6
LRU_C = 8.0
ADAM_LR = 0.001
ADAM_B1 = 0.9
ADAM_B2 = 0.999
ADAM_EPS = 1e-08
ADAM_WD = 0.01
ADAM_STEP = 10
VMEM_BIG = 56 * 1024 * 1024
MESH = pl.DeviceIdType.MESH

SLOT_A, SLOT_B, SLOT_C, SLOT_G = 2, 0, 1, 3


def _slot_of_chip(s):
    return jnp.where(s == 3, 3, (s + 2) % 3)


def _params(sem, vmem=None):
    return pltpu.CompilerParams(dimension_semantics=sem, vmem_limit_bytes=vmem)


_GC = 0.7978845608028654
_GA = 0.044715


def _gelu(x):
    return 0.5 * x * (1.0 + jnp.tanh(_GC * (x + _GA * x * x * x)))


def _gelu_and_grad(x):
    x2 = x * x
    th = jnp.tanh(_GC * x * (1.0 + _GA * x2))
    g = 0.5 * x * (1.0 + th)
    dg = 0.5 * (1.0 + th) + 0.5 * x * (1.0 - th * th) * _GC * (1.0 + 3.0 * _GA * x2)
    return g, dg


def _sig(x):
    return jax.nn.sigmoid(x)


def _dot(a, b):
    return jnp.dot(a, b, preferred_element_type=F32)


def _dot_nt(a, b):
    return lax.dot_general(a, b, (((1,), (1,)), ((), ())), preferred_element_type=F32)


def _dot_tn(a, b):
    return lax.dot_general(a, b, (((0,), (0,)), ((), ())), preferred_element_type=F32)


def _chunk_cumsum(x):
    pos = lax.broadcasted_iota(jnp.int32, (x.shape[0], 1), 0) & (CH - 1)
    d = 1
    while d < CH:
        x = x + jnp.where(pos >= d, pltpu.roll(x, d, 0), 0.0)
        d *= 2
    return x


def _chunk_revcumsum(x):
    n = x.shape[0]
    pos = lax.broadcasted_iota(jnp.int32, (n, 1), 0) & (CH - 1)
    d = 1
    while d < CH:
        x = x + jnp.where(pos < CH - d, pltpu.roll(x, n - d, 0), 0.0)
        d *= 2
    return x


def _chunk_last(x):
    n = x.shape[0]
    return jnp.concatenate(
        [jnp.broadcast_to(x[c * CH + CH - 1:c * CH + CH, :], (CH, x.shape[1])) for c in range(n // CH)], axis=0)


def _chunk_total(x):
    n = x.shape[0]
    return jnp.concatenate(
        [jnp.broadcast_to(jnp.sum(x[c * CH:(c + 1) * CH, :], axis=0, keepdims=True), (CH, x.shape[1]))
         for c in range(n // CH)], axis=0)


def _rms_stats(x):
    r = lax.rsqrt(jnp.mean(x * x, axis=-1, keepdims=True) + EPS)
    return r, x * r


def _rms_bwd(gd, n, r):
    return r * (gd - n * jnp.mean(gd * n, axis=-1, keepdims=True))


def _shift_rows(x, d, fill):
    rows = lax.broadcasted_iota(jnp.int32, (x.shape[0], 1), 0)
    return jnp.where(rows >= d, pltpu.roll(x, d, 0), fill)


def _scan_down(a, u, carry):
    n = a.shape[0]
    pos = lax.broadcasted_iota(jnp.int32, (n, 1), 0) & 7
    for d in (1, 2, 4):
        u = a * jnp.where(pos >= d, pltpu.roll(u, d, 0), 0.0) + u
        a = a * jnp.where(pos >= d, pltpu.roll(a, d, 0), 1.0)
    out = []
    for v in range(n // 8):
        h = a[v * 8:v * 8 + 8, :] * carry + u[v * 8:v * 8 + 8, :]
        carry = h[7:8, :]
        out.append(h)
    return jnp.concatenate(out, axis=0)


def _scan_up(b, g, carry):
    n = b.shape[0]
    pos = lax.broadcasted_iota(jnp.int32, (n, 1), 0) & 7
    for d in (1, 2, 4):
        g = g + b * jnp.where(pos < 8 - d, pltpu.roll(g, n - d, 0), 0.0)
        b = b * jnp.where(pos < 8 - d, pltpu.roll(b, n - d, 0), 1.0)
    out = [None] * (n // 8)
    for v in reversed(range(n // 8)):
        h = g[v * 8:v * 8 + 8, :] + b[v * 8:v * 8 + 8, :] * carry
        carry = h[0:1, :]
        out[v] = h
    return jnp.concatenate(out, axis=0)


def _shift_rows_up(x, d, fill):
    n = x.shape[0]
    rows = lax.broadcasted_iota(jnp.int32, (n, 1), 0)
    return jnp.where(rows < n - d, pltpu.roll(x, n - d, 0), fill)


def _mm_nn_sharded(a, b3, out_dtype, tm, name, slot_fn=None):
    m, k = a.shape
    s, _, ns = b3.shape

    def body(a_ref, b_ref, o_ref):
        o_ref[...] = _dot(a_ref[...], b_ref[...]).astype(out_dtype)

    if slot_fn is None:
        out_shape = jax.ShapeDtypeStruct((m, s * ns), out_dtype)
        out_spec = pl.BlockSpec((tm, ns), lambda j, i: (i, j))
    else:
        out_shape = jax.ShapeDtypeStruct((s, m, ns), out_dtype)
        out_spec = pl.BlockSpec((None, tm, ns), lambda j, i: (slot_fn(j), i, 0))
    return pl.pallas_call(
        body, name=name, out_shape=out_shape, grid=(s, m // tm),
        in_specs=[pl.BlockSpec((tm, k), lambda j, i: (i, 0)),
                  pl.BlockSpec((None, k, ns), lambda j, i: (j, 0, 0))],
        out_specs=out_spec,
        compiler_params=_params(("parallel", "parallel"), VMEM_BIG),
    )(a, b3)


def _peer_of_step(s):
    return ((s & 1) << 1) | (s >> 1)


def _mm_in_gather(a, stk_w_in, stk_rest, names_rest, conv_a_s, conv_f_s, jc, tm):
    m, k = a.shape
    s_n, _, ns = stk_w_in.shape
    nr = len(stk_rest)
    mt = m // tm

    def body(jc_ref, a_ref, w_in_any, *rest):
        del w_in_any
        ca_src, cf_src = rest[nr], rest[nr + 1]
        o_ref, w_full = rest[nr + 2], rest[nr + 3]
        stk = rest[nr + 4:2 * nr + 4]
        ca_dst, cf_dst = rest[2 * nr + 4], rest[2 * nr + 5]
        wbuf, ssem_w, rsem_w, fssem_w, frsem_w, ssem_r, rsem_r, csend, crecv, lsem, wsem = rest[2 * nr + 6:]
        s, i = pl.program_id(0), pl.program_id(1)
        x, y, c, j, chips = _place()
        sends_w, arrive_w, fwds_w, farrive_w = _gather_copies([w_full], ("w_in",), ssem_w, rsem_w, fssem_w, frsem_w)
        sends_r, arrive_r, _, _ = _gather_copies(stk, names_rest, ssem_r, rsem_r)
        conv = ((ca_src, ca_dst), (cf_src, cf_dst))
        locs = [pltpu.make_async_copy(src, dst.at[j], lsem.at[n]) for n, (src, dst) in enumerate(conv)]
        csends = [_remote(src, dst.at[j], csend.at[3 * n + kk], crecv.at[3 * n + kk], (cx, cy, c))
                  for n, (src, dst) in enumerate(conv) for kk, (cx, cy) in enumerate(chips)]

        def fetch(step):
            return pltpu.make_async_copy(w_full.at[j ^ _peer_of_step(step)], wbuf.at[step & 1], wsem.at[step & 1])

        @pl.when((s == 0) & (i == 0))
        def _():
            for cp in sends_w + locs + csends + sends_r:
                cp.start()
            fetch(s).start()

        @pl.when(i == 0)
        def _():
            fetch(s).wait()

        o_ref[...] = _dot(a_ref[...], wbuf[s & 1])

        @pl.when((s == 0) & (i == mt - 1))
        def _():
            for kk in (0, 1):
                arrive_w[kk].wait_recv()
                fwds_w[kk].start()

        @pl.when((s == 1) & (i == mt - 1))
        def _():
            arrive_w[2].wait_recv()
            fwds_w[2].start()

        for kk in range(3):
            @pl.when((s == kk) & (i == mt - 1))
            def _(kk=kk):
                farrive_w[kk].wait_recv()
                fetch(s + 1).start()

        @pl.when((s == s_n - 1) & (i == mt - 1))
        def _():
            for cp in arrive_r:
                cp.wait_recv()
            for n, (_, dst) in enumerate(conv):
                for kk, (cx, cy) in enumerate(chips):
                    got = dst.at[2 * cx + cy]
                    _remote(got, got, csend.at[3 * n + kk], crecv.at[3 * n + kk], (cx, cy, c)).wait_recv()
            for cp in sends_w + fwds_w + sends_r + csends:
                cp.wait_send()
            for cp in locs:
                cp.wait()

    any_spec = pl.BlockSpec(memory_space=pl.ANY)
    sem = pltpu.SemaphoreType.DMA
    grid_spec = pltpu.PrefetchScalarGridSpec(
        num_scalar_prefetch=1, grid=(s_n, mt),
        in_specs=[pl.BlockSpec((tm, k), lambda s, i, jc_ref: (i, 0))] + [any_spec] * (nr + 3),
        out_specs=(pl.BlockSpec((None, tm, ns),
                                lambda s, i, jc_ref: (_slot_of_chip(jc_ref[0] ^ _peer_of_step(s)), i, 0)),)
        + (any_spec,) * (nr + 3),
        scratch_shapes=[pltpu.VMEM((2, k, ns), BF16), sem((3,)), sem((3,)), sem((3,)), sem((3,)),
                        sem((max(3 * nr, 1),)), sem((max(3 * nr, 1),)), sem((6,)), sem((6,)), sem((2,)), sem((2,))])
    outs = pl.pallas_call(
        body, name="mm_in", grid_spec=grid_spec,
        out_shape=(jax.ShapeDtypeStruct((s_n, m, ns), F32), jax.ShapeDtypeStruct(stk_w_in.shape, BF16))
        + tuple(jax.ShapeDtypeStruct(v.shape, v.dtype) for v in stk_rest)
        + tuple(jax.ShapeDtypeStruct((NCHIP,) + v.shape, v.dtype) for v in (conv_a_s, conv_f_s)),
        input_output_aliases={2 + w: 1 + w for w in range(nr + 1)},
        compiler_params=pltpu.CompilerParams(dimension_semantics=("arbitrary", "arbitrary"),
                                             vmem_limit_bytes=VMEM_BIG, has_side_effects=True),
    )(jc, a, stk_w_in, *stk_rest, conv_a_s, conv_f_s)
    return outs[0], outs[1], list(outs[2:2 + nr]), outs[2 + nr], outs[3 + nr]


def _mm_dh1_exchange(dp4, w_in3, p_w_in, x, dx1, gain, tm):
    s, k, ns = w_in3.shape
    m = dp4.shape[1]
    mt = m // tm

    def body(a_ref, b_ref, p_ref, x_ref, dx1_ref, g_ref, o_ref, q_ref, dg_ref, ssem, rsem):
        i, j = pl.program_id(0), pl.program_id(1)
        sends, arrive = _exchange_copies([q_ref], [lambda chip: p_ref.at[chip]], ssem, rsem)

        @pl.when((i == 0) & (j == 0))
        def _():
            for cp in sends:
                cp.start()

        @pl.when(j == 0)
        def _():
            o_ref[...] = _dot_nt(a_ref[...], b_ref[...])

        @pl.when(j > 0)
        def _():
            o_ref[...] += _dot_nt(a_ref[...], b_ref[...])

        @pl.when(j == s - 1)
        def _():
            dh = o_ref[...]
            r, n = _rms_stats(x_ref[...])
            o_ref[...] = dx1_ref[...] + _rms_bwd(dh * g_ref[...], n, r)
            dgv = jnp.sum(dh * n, axis=0, keepdims=True)

            @pl.when(i == 0)
            def _():
                dg_ref[...] = dgv

            @pl.when(i > 0)
            def _():
                dg_ref[...] += dgv

        @pl.when((i == mt - 1) & (j == s - 1))
        def _():
            for cp in arrive:
                cp.wait_recv()
            for cp in sends:
                cp.wait_send()

    any_spec = pl.BlockSpec(memory_space=pl.ANY)
    row_tile = pl.BlockSpec((tm, k), lambda i, j: (i, 0))
    vec = pl.BlockSpec((1, k), lambda i, j: (0, 0))
    return pl.pallas_call(
        body, name="mm_dh1",
        out_shape=(jax.ShapeDtypeStruct((m, k), F32), jax.ShapeDtypeStruct(p_w_in.shape, BF16),
                   jax.ShapeDtypeStruct((1, k), F32)),
        grid=(mt, s),
        in_specs=[pl.BlockSpec((None, tm, ns), lambda i, j: (_slot_of_chip(j), i, 0)),
                  pl.BlockSpec((None, k, ns), lambda i, j: (j, 0, 0)), any_spec, row_tile, row_tile, vec],
        out_specs=(row_tile, any_spec, vec),
        scratch_shapes=[pltpu.SemaphoreType.DMA((3,)), pltpu.SemaphoreType.DMA((3,))],
        compiler_params=pltpu.CompilerParams(dimension_semantics=("arbitrary", "arbitrary"),
                                             vmem_limit_bytes=VMEM_BIG, has_side_effects=True),
    )(dp4, w_in3, p_w_in, x, dx1, gain)


def _mm_nt_sharded(a, b3, tm, name, stacked_slot_fn=None):
    s, k, ns = b3.shape
    m = a.shape[1] if stacked_slot_fn is not None else a.shape[0]

    def body(a_ref, b_ref, o_ref):
        j = pl.program_id(1)
        @pl.when(j == 0)
        def _():
            o_ref[...] = _dot_nt(a_ref[...], b_ref[...])

        @pl.when(j > 0)
        def _():
            o_ref[...] += _dot_nt(a_ref[...], b_ref[...])

    if stacked_slot_fn is None:
        a_spec = pl.BlockSpec((tm, ns), lambda i, j: (i, j))
    else:
        a_spec = pl.BlockSpec((None, tm, ns), lambda i, j: (stacked_slot_fn(j), i, 0))
    return pl.pallas_call(
        body, name=name, out_shape=jax.ShapeDtypeStruct((m, k), F32), grid=(m // tm, s),
        in_specs=[a_spec, pl.BlockSpec((None, k, ns), lambda i, j: (j, 0, 0))],
        out_specs=pl.BlockSpec((tm, k), lambda i, j: (i, 0)),
        compiler_params=_params(("parallel", "arbitrary"), VMEM_BIG),
    )(a, b3)


def _mm_nt(a, b, out_dtype, tm, name):
    m, k = a.shape
    n = b.shape[0]

    def body(a_ref, b_ref, o_ref):
        o_ref[...] = _dot_nt(a_ref[...], b_ref[...]).astype(out_dtype)

    return pl.pallas_call(
        body, name=name, out_shape=jax.ShapeDtypeStruct((m, n), out_dtype), grid=(m // tm,),
        in_specs=[pl.BlockSpec((tm, k), lambda i: (i, 0)), pl.BlockSpec((n, k), lambda i: (0, 0))],
        out_specs=pl.BlockSpec((tm, n), lambda i: (i, 0)),
        compiler_params=_params(("parallel",), VMEM_BIG),
    )(a, b)


def _mm_tn(a, g, tkk, tn, tk, name, stacked_slot_fn=None, stacked_out=False):
    m, k = a.shape
    if stacked_slot_fn is not None:
        n = g.shape[0] * g.shape[2]
        g_spec = pl.BlockSpec((None, tk, tn), lambda kk, j, mm: (stacked_slot_fn(j), mm, 0))
    else:
        n = g.shape[1]
        g_spec = pl.BlockSpec((tk, tn), lambda kk, j, mm: (mm, j))
    steps = m // tk

    def body(a_ref, g_ref, o_ref, acc_ref):
        mm = pl.program_id(2)

        @pl.when(mm == 0)
        def _():
            acc_ref[...] = _dot_tn(a_ref[...], g_ref[...])

        @pl.when(mm > 0)
        def _():
            acc_ref[...] += _dot_tn(a_ref[...], g_ref[...])

        @pl.when(mm == steps - 1)
        def _():
            o_ref[...] = acc_ref[...].astype(BF16)

    if stacked_out:
        out_shape = jax.ShapeDtypeStruct((n // tn, k, tn), BF16)
        out_spec = pl.BlockSpec((None, tkk, tn), lambda kk, j, mm: (j, kk, 0))
    else:
        out_shape = jax.ShapeDtypeStruct((k, n), BF16)
        out_spec = pl.BlockSpec((tkk, tn), lambda kk, j, mm: (kk, j))
    return pl.pallas_call(
        body, name=name, out_shape=out_shape, grid=(k // tkk, n // tn, steps),
        in_specs=[pl.BlockSpec((tk, tkk), lambda kk, j, mm: (mm, kk)), g_spec],
        out_specs=out_spec,
        scratch_shapes=[pltpu.VMEM((tkk, tn), F32)],
        compiler_params=_params(("parallel", "parallel", "arbitrary"), VMEM_BIG),
    )(a, g)


def _norm_fwd(x, gain, tt):
    t = x.shape[0]

    def body(x_ref, g_ref, h_ref):
        _, n = _rms_stats(x_ref[...])
        h_ref[...] = (n * g_ref[...]).astype(BF16)

    return pl.pallas_call(
        body, name="norm_fwd", out_shape=jax.ShapeDtypeStruct((t, D), BF16), grid=(t // tt,),
        in_specs=[pl.BlockSpec((tt, D), lambda i: (i, 0)), pl.BlockSpec((1, D), lambda i: (0, 0))],
        out_specs=pl.BlockSpec((tt, D), lambda i: (i, 0)),
        compiler_params=_params(("parallel",)),
    )(x, gain)


def _lru_gates(xc, wa_ref, wx_ref, ba, bx, lam):
    xcb = xc.astype(BF16)
    ra = jnp.concatenate([_dot(xcb[:, n * HD:(n + 1) * HD], wa_ref[n]) for n in range(NH)], axis=1) + ba
    ix = jnp.concatenate([_dot(xcb[:, n * HD:(n + 1) * HD], wx_ref[n]) for n in range(NH)], axis=1) + bx
    r = _sig(ra)
    ig = _sig(ix)
    z = -lam
    sp = jnp.maximum(z, 0.0) + jnp.log1p(jnp.exp(-jnp.abs(z)))
    log_a = -LRU_C * r * sp
    a = jnp.exp(log_a)
    z2 = 2.0 * log_a
    series = -z2 * (1.0 + z2 * (0.5 + z2 * (1.0 / 6.0 + z2 * (1.0 / 24.0))))
    om = jnp.where(z2 > -0.02, series, 1.0 - jnp.exp(z2))
    mult = jnp.sqrt(om)
    return xcb, r, ig, sp, a, mult


def _mixer_a_fwd(p4, cw, cb, wa, wx, ba, bx, lam, stk, names, tt):
    t = p4.shape[1]
    ng = len(stk)

    def body(p_ref, cw_ref, cb_ref, wa_ref, wx_ref, ba_ref, bx_ref, lam_ref, *rest):
        ya_ref, h_ref, sv_ref = rest[ng:ng + 3]
        halo, hc, ssem, rsem = rest[2 * ng + 3:]
        i = pl.program_id(0)
        finish = _ici_leg_behind(rest[ng + 3:2 * ng + 3], names, ssem, rsem, i == 0, lambda: i == t // tt - 1)

        @pl.when(i == 0)
        def _():
            halo[...] = jnp.zeros((8, D), F32)
            hc[...] = jnp.zeros((1, D), F32)

        xa = p_ref[:, 0:D]
        ga = p_ref[:, D:2 * D]
        xe = jnp.concatenate([halo[...], xa], axis=0)
        xc = (cb_ref[...] + cw_ref[3:4, :] * xe
              + sum(cw_ref[3 - s:4 - s, :] * pltpu.roll(xe, s, 0) for s in (1, 2, 3)))[8:, :]
        halo[...] = xa[tt - 8:, :]
        _, r, ig, _, a, mult = _lru_gates(xc, wa_ref, wx_ref, ba_ref[...], bx_ref[...], lam_ref[...])
        u = mult * ig * xc
        h = _scan_down(a, u, hc[...])
        hc[...] = h[tt - 1:tt, :]
        h_ref[...] = h
        ya_ref[...] = (h * _gelu(ga)).astype(BF16)
        for idx, val in enumerate((xc, r, ig, a, mult)):
            sv_ref[idx] = val
        finish()

    full = lambda shape: pl.BlockSpec(shape, lambda i: (0,) * len(shape))
    any_spec = pl.BlockSpec(memory_space=pl.ANY)
    outs = pl.pallas_call(
        body, name="mixer_a_fwd",
        out_shape=(jax.ShapeDtypeStruct((t, D), BF16), jax.ShapeDtypeStruct((t, D), F32),
                   jax.ShapeDtypeStruct((5, t, D), F32)) + tuple(jax.ShapeDtypeStruct(v.shape, v.dtype) for v in stk),
        grid=(t // tt,),
        in_specs=[pl.BlockSpec((None, tt, 2 * D), lambda i: (SLOT_A, i, 0)),
                  full((4, D)), full((1, D)), full((NH, HD, HD)), full((NH, HD, HD)),
                  full((1, D)), full((1, D)), full((1, D))] + [any_spec] * ng,
        out_specs=(pl.BlockSpec((tt, D), lambda i: (i, 0)), pl.BlockSpec((tt, D), lambda i: (i, 0)),
                   pl.BlockSpec((5, tt, D), lambda i: (0, i, 0))) + (any_spec,) * ng,
        scratch_shapes=[pltpu.VMEM((8, D), F32), pltpu.VMEM((1, D), F32),
                        pltpu.SemaphoreType.DMA((3 * ng,)), pltpu.SemaphoreType.DMA((3 * ng,))],
        input_output_aliases={8 + w: 3 + w for w in range(ng)},
        compiler_params=pltpu.CompilerParams(dimension_semantics=("arbitrary",), vmem_limit_bytes=VMEM_BIG,
                                             has_side_effects=True),
    )(p4, cw, cb, wa, wx, ba, bx, lam, *stk)
    return outs[0], outs[1], outs[2], list(outs[3:])


def _chunk_masks(tt):
    row = lax.broadcasted_iota(jnp.int32, (tt, tt), 0)
    col = lax.broadcasted_iota(jnp.int32, (tt, tt), 1)
    same = jnp.right_shift(row, 5) == jnp.right_shift(col, 5)
    return same & (col <= row)


def _hg_head_fwd(q, fz, lbh, saved=None):
    sgn = _sig(-fz)
    k = (1.0 - lbh) * sgn
    if saved is None:
        sg = _sig(fz)
        f = lbh + (1.0 - lbh) * sg
        g = _chunk_cumsum(jnp.log(f))
    else:
        sg, g = saved
        f = lbh + (1.0 - lbh) * sg
    gu = _chunk_last(g) - g
    eg = jnp.exp(g)
    eng = jnp.exp(-g)
    egu = jnp.exp(gu)
    qt = q * eg
    kt = k * eng
    kd = k * egu
    return sg, sgn, f, k, g, eg, eng, egu, qt, kt, kd


def _lb_of(logits_ref):
    return _sig(logits_ref[0:1, :] - logits_ref[1:2, :])


def _hgrn2_fwd(p4, logits, gnorm, stk, names, tt):
    t = p4.shape[1]
    nc = tt // CH
    ng = len(stk)

    def body(p_ref, lg_ref, gn_ref, *rest):
        yb_ref, o_ref, ss_ref, sv_ref = rest[ng:ng + 4]
        st, ssem, rsem = rest[2 * ng + 4:]
        i = pl.program_id(0)
        finish = _ici_leg_behind(rest[ng + 4:2 * ng + 4], names, ssem, rsem, i == 0, lambda: i == t // tt - 1)

        @pl.when(i == 0)
        def _():
            st[...] = jnp.zeros((NH, HD, HD), F32)

        low = _chunk_masks(tt)
        lb = _lb_of(lg_ref)
        heads = [slice(h * HD, (h + 1) * HD) for h in range(NH)]
        sg, _, _, _, g, _, _, _, qt, kt, kd = _hg_head_fwd(p_ref[0, :, 0:D], p_ref[0, :, D:2 * D], lb)
        sv_ref[0] = sg
        sv_ref[1] = g
        qtb, ktb, kdb, vb = qt.astype(BF16), kt.astype(BF16), kd.astype(BF16), p_ref[1, :, 0:D].astype(BF16)
        decs = [jnp.exp(g[c * CH + CH - 1:c * CH + CH, :]) for c in range(nc)]
        o_in = []
        for hs in heads:
            att = jnp.where(low, _dot_nt(qtb[:, hs], ktb[:, hs]), 0.0)
            o_in.append(_dot(att.astype(BF16), vb[:, hs]))
        s_t = [st[h] for h in range(NH)]
        pieces = [[None] * nc for _ in range(NH)]
        for c in range(nc):
            sl = slice(c * CH, (c + 1) * CH)
            for h, hs in enumerate(heads):
                s_bf = s_t[h].astype(BF16)
                ss_ref[c, h] = s_bf
                pieces[h][c] = o_in[h][sl] + _dot_nt(qtb[sl, hs], s_bf)
                s_t[h] = s_t[h] * decs[c][:, hs] + _dot_tn(vb[sl, hs], kdb[sl, hs])
        for h, hs in enumerate(heads):
            st[h] = s_t[h]
            o = jnp.concatenate(pieces[h], axis=0)
            _, n = _rms_stats(o)
            og = p_ref[1, :, D + h * HD:D + (h + 1) * HD]
            o_ref[:, hs] = o
            yb_ref[:, hs] = (n * gn_ref[:, hs] * (og * _sig(og))).astype(BF16)
        finish()

    any_spec = pl.BlockSpec(memory_space=pl.ANY)
    outs = pl.pallas_call(
        body, name="hgrn2_fwd",
        out_shape=(jax.ShapeDtypeStruct((t, D), BF16), jax.ShapeDtypeStruct((t, D), F32),
                   jax.ShapeDtypeStruct((t // CH, NH, HD, HD), BF16), jax.ShapeDtypeStruct((2, t, D), F32))
        + tuple(jax.ShapeDtypeStruct(v.shape, v.dtype) for v in stk),
        grid=(t // tt,),
        in_specs=[pl.BlockSpec((2, tt, 2 * D), lambda i: (0, i, 0)),
                  pl.BlockSpec((2, D), lambda i: (0, 0)), pl.BlockSpec((1, D), lambda i: (0, 0))] + [any_spec] * ng,
        out_specs=(pl.BlockSpec((tt, D), lambda i: (i, 0)), pl.BlockSpec((tt, D), lambda i: (i, 0)),
                   pl.BlockSpec((nc, NH, HD, HD), lambda i: (i, 0, 0, 0)),
                   pl.BlockSpec((2, tt, D), lambda i: (0, i, 0))) + (any_spec,) * ng,
        scratch_shapes=[pltpu.VMEM((NH, HD, HD), F32), pltpu.SemaphoreType.DMA((3 * ng,)),
                        pltpu.SemaphoreType.DMA((3 * ng,))],
        input_output_aliases={3 + w: 4 + w for w in range(ng)},
        compiler_params=pltpu.CompilerParams(dimension_semantics=("arbitrary",), vmem_limit_bytes=VMEM_BIG,
                                             has_side_effects=True),
    )(p4, logits, gnorm, *stk)
    return outs[0], outs[1], outs[2], outs[3], list(outs[4:])


def _mid_fwd(ya, yb, p4, x, wa, wb, wo, g_pm, g_pf, tt):
    t = x.shape[0]

    def body(ya_ref, yb_ref, gt_ref, x_ref, wa_ref, wb_ref, wo_ref, gpm_ref, gpf_ref,
             za_ref, zb_ref, mix_ref, m2_ref, x1_ref, h2_ref):
        za = _dot(ya_ref[...], wa_ref[...])
        zb = _dot(yb_ref[...], wb_ref[...])
        mix = _sig(gt_ref[:, 0:D]) * za + _sig(gt_ref[:, D:2 * D]) * zb
        mixb = mix.astype(BF16)
        m2 = _dot(mixb, wo_ref[...])
        _, n2 = _rms_stats(m2)
        x1 = x_ref[...] + n2 * gpm_ref[...]
        _, n1 = _rms_stats(x1)
        za_ref[...] = za.astype(BF16)
        zb_ref[...] = zb.astype(BF16)
        mix_ref[...] = mixb
        m2_ref[...] = m2
        x1_ref[...] = x1
        h2_ref[...] = (n1 * gpf_ref[...]).astype(BF16)

    row = lambda dt: jax.ShapeDtypeStruct((t, D), dt)
    tile = pl.BlockSpec((tt, D), lambda i: (i, 0))
    wsp = pl.BlockSpec((D, D), lambda i: (0, 0))
    vec = pl.BlockSpec((1, D), lambda i: (0, 0))
    return pl.pallas_call(
        body, name="mid_fwd",
        out_shape=(row(BF16), row(BF16), row(BF16), row(F32), row(F32), row(BF16)),
        grid=(t // tt,),
        in_specs=[tile, tile, pl.BlockSpec((None, tt, 2 * D), lambda i: (SLOT_G, i, 0)), tile,
                  wsp, wsp, wsp, vec, vec],
        out_specs=(tile,) * 6,
        compiler_params=_params(("parallel",), VMEM_BIG),
    )(ya, yb, p4, x, wa, wb, wo, g_pm, g_pf)


def _up_act_fwd(h2, w_up4, cfw, cfb, tm):
    t = h2.shape[0]
    ns = SH_UP

    def body(a_ref, ah_ref, wg_ref, wv_ref, cwg_ref, cwv_ref, cbg_ref, cbv_ref,
             pg_ref, pv_ref, y_ref, uv_ref, gl_ref, dgl_ref):
        i = pl.program_id(1)
        rows = jnp.concatenate([ah_ref[...], a_ref[...]], axis=0)
        ups = []
        for w_ref, cw_ref, cb_ref, pre_ref in ((wg_ref, cwg_ref, cbg_ref, pg_ref), (wv_ref, cwv_ref, cbv_ref, pv_ref)):
            pre = _dot(rows, w_ref[...])
            pre_ref[...] = pre[16:, :]
            xe = jnp.concatenate([jnp.where(i > 0, pre[8:16, :], 0.0), pre[16:, :]], axis=0)
            up = (cb_ref[...] + cw_ref[2:3, :] * xe + cw_ref[1:2, :] * pltpu.roll(xe, 1, 0)
                  + cw_ref[0:1, :] * pltpu.roll(xe, 2, 0))
            ups.append(up[8:, :])
        gl, dgl = _gelu_and_grad(ups[0])
        y_ref[...] = (gl * ups[1]).astype(BF16)
        uv_ref[...] = ups[1].astype(BF16)
        gl_ref[...] = gl.astype(BF16)
        dgl_ref[...] = dgl.astype(BF16)

    hb = tm // 16
    tile = pl.BlockSpec((tm, ns), lambda p, i: (i, p))
    return pl.pallas_call(
        body, name="up_act_fwd",
        out_shape=(jax.ShapeDtypeStruct((t, DFF), F32),) * 2 + (jax.ShapeDtypeStruct((t, DFF), BF16),) * 4,
        grid=(2, t // tm),
        in_specs=[pl.BlockSpec((tm, D), lambda p, i: (i, 0)),
                  pl.BlockSpec((16, D), lambda p, i: (jnp.maximum(i * hb - 1, 0), 0)),
                  pl.BlockSpec((None, D, ns), lambda p, i: (p, 0, 0)),
                  pl.BlockSpec((None, D, ns), lambda p, i: (p + 2, 0, 0)),
                  pl.BlockSpec((3, ns), lambda p, i: (0, p)), pl.BlockSpec((3, ns), lambda p, i: (0, p + 2)),
                  pl.BlockSpec((1, ns), lambda p, i: (0, p)), pl.BlockSpec((1, ns), lambda p, i: (0, p + 2))],
        out_specs=(tile,) * 6,
        compiler_params=_params(("parallel", "parallel"), VMEM_BIG),
    )(h2, h2, w_up4, w_up4, cfw, cfw, cfb, cfb)


def _down_loss(y, wdn, x1, tgt, g_post, tt):
    t = x1.shape[0]

    def body(y_ref, w_ref, x1_ref, t_ref, g_ref, dx2_ref, dm3_ref, lossv_ref, dg_ref):
        i = pl.program_id(0)
        m3 = _dot(y_ref[...], w_ref[...])
        r, n3 = _rms_stats(m3)
        g = g_ref[...]
        e = x1_ref[...] + n3 * g - t_ref[...]
        dx2 = e * (1.0 / D)
        dx2_ref[...] = dx2
        dm3_ref[...] = _rms_bwd(dx2 * g, n3, r).astype(BF16)
        lv = jnp.sum(e * e, axis=0, keepdims=True)
        dgv = jnp.sum(dx2 * n3, axis=0, keepdims=True)

        @pl.when(i == 0)
        def _():
            lossv_ref[...] = lv
            dg_ref[...] = dgv

        @pl.when(i > 0)
        def _():
            lossv_ref[...] += lv
            dg_ref[...] += dgv

    tile = pl.BlockSpec((tt, D), lambda i: (i, 0))
    vec = pl.BlockSpec((1, D), lambda i: (0, 0))
    return pl.pallas_call(
        body, name="down_loss",
        out_shape=(jax.ShapeDtypeStruct((t, D), F32), jax.ShapeDtypeStruct((t, D), BF16),
                   jax.ShapeDtypeStruct((1, D), F32), jax.ShapeDtypeStruct((1, D), F32)),
        grid=(t // tt,),
        in_specs=[pl.BlockSpec((tt, DFF), lambda i: (i, 0)), pl.BlockSpec((DFF, D), lambda i: (0, 0)),
                  tile, tile, vec],
        out_specs=(tile, tile, vec, vec),
        compiler_params=_params(("arbitrary",), VMEM_BIG),
    )(y, wdn, x1, tgt, g_post)


def _ffn_act_bwd(dy, pre_g, pre_v, uv, gl, dgl, cfw, tt):
    t = dy.shape[0]
    nt = t // tt

    def body(dy_ref, dyn_ref, pg_ref, pv_ref, uv_ref, uvn_ref, gl_ref, gln_ref, dgl_ref, dgln_ref, cw_ref,
             du_ref, dcw_ref, dcb_ref):
        i = pl.program_id(0)
        n = tt + 8
        next_live = jnp.where(i < nt - 1, 1.0, 0.0)
        ext = lambda ref, nref: jnp.concatenate([ref[...].astype(F32), nref[...].astype(F32)[0:8, :]], axis=0)
        dy = jnp.concatenate([dy_ref[...].astype(F32), dyn_ref[...].astype(F32)[0:8, :] * next_live], axis=0)
        ds = (dy * ext(uv_ref, uvn_ref) * ext(dgl_ref, dgln_ref), dy * ext(gl_ref, gln_ref))
        dcw_parts, dcb_parts = [], []
        for hh, c0 in enumerate((0, DFF)):
            cs = slice(c0, c0 + DFF)
            dd = ds[hh]
            d1 = pltpu.roll(dd, n - 1, 0)
            d2 = pltpu.roll(dd, n - 2, 0)
            du_ref[:, cs] = (cw_ref[2:3, cs] * dd + cw_ref[1:2, cs] * d1 + cw_ref[0:1, cs] * d2)[0:tt, :].astype(BF16)
            x = (pg_ref, pv_ref)[hh][...]
            dcw_parts.append(jnp.concatenate(
                [jnp.sum(dk[0:tt, :] * x, axis=0, keepdims=True) for dk in (d2, d1, dd)], axis=0))
            dcb_parts.append(jnp.sum(dd[0:tt, :], axis=0, keepdims=True))
        dcw = jnp.concatenate(dcw_parts, axis=1)
        dcb = jnp.concatenate(dcb_parts, axis=1)

        @pl.when(i == 0)
        def _():
            dcw_ref[...] = dcw
            dcb_ref[...] = dcb

        @pl.when(i > 0)
        def _():
            dcw_ref[...] += dcw
            dcb_ref[...] += dcb

    half = pl.BlockSpec((tt, DFF), lambda i: (i, 0))
    half_next = pl.BlockSpec((16, DFF), lambda i: (jnp.minimum((i + 1) * (tt // 16), t // 16 - 1), 0))
    return pl.pallas_call(
        body, name="ffn_act_bwd",
        out_shape=(jax.ShapeDtypeStruct((t, DUP), BF16), jax.ShapeDtypeStruct((3, DUP), F32),
                   jax.ShapeDtypeStruct((1, DUP), F32)),
        grid=(nt,),
        in_specs=[half, half_next, half, half,
                  half, half_next, half, half_next, half, half_next,
                  pl.BlockSpec((3, DUP), lambda i: (0, 0))],
        out_specs=(pl.BlockSpec((tt, DUP), lambda i: (i, 0)), pl.BlockSpec((3, DUP), lambda i: (0, 0)),
                   pl.BlockSpec((1, DUP), lambda i: (0, 0))),
        compiler_params=_params(("arbitrary",), VMEM_BIG),
    )(dy, dy, pre_g, pre_v, uv, uv, gl, gl, dgl, dgl, cfw)


def _mid_bwd(dh2, dx2, x1, m2, za, zb, p4, wa, wb, wo, g_pm, g_pf, tt):
    t = x1.shape[0]

    def body(dh2_ref, dx2_ref, x1_ref, m2_ref, za_ref, zb_ref, gt_ref, wa_ref, wb_ref, wo_ref, gpm_ref, gpf_ref,
             dx1_ref, dm2_ref, dza_ref, dzb_ref, dya_ref, dyb_ref, dp_ref, dgpm_ref, dgpf_ref):
        i = pl.program_id(0)
        r1, n1 = _rms_stats(x1_ref[...])
        dh2 = dh2_ref[...]
        dx1 = dx2_ref[...] + _rms_bwd(dh2 * gpf_ref[...], n1, r1)
        r2, n2 = _rms_stats(m2_ref[...])
        dm2 = _rms_bwd(dx1 * gpm_ref[...], n2, r2).astype(BF16)
        dmix = _dot_nt(dm2, wo_ref[...])
        sa = _sig(gt_ref[:, 0:D])
        sb = _sig(gt_ref[:, D:2 * D])
        dza = (dmix * sa).astype(BF16)
        dzb = (dmix * sb).astype(BF16)
        dp_ref[:, 0:D] = (dmix * za_ref[...].astype(F32) * sa * (1.0 - sa)).astype(BF16)
        dp_ref[:, D:2 * D] = (dmix * zb_ref[...].astype(F32) * sb * (1.0 - sb)).astype(BF16)
        dx1_ref[...] = dx1
        dm2_ref[...] = dm2
        dza_ref[...] = dza
        dzb_ref[...] = dzb
        dya_ref[...] = _dot_nt(dza, wa_ref[...]).astype(BF16)
        dyb_ref[...] = _dot_nt(dzb, wb_ref[...]).astype(BF16)
        dgpf = jnp.sum(dh2 * n1, axis=0, keepdims=True)
        dgpm = jnp.sum(dx1 * n2, axis=0, keepdims=True)

        @pl.when(i == 0)
        def _():
            dgpf_ref[...] = dgpf
            dgpm_ref[...] = dgpm

        @pl.when(i > 0)
        def _():
            dgpf_ref[...] += dgpf
            dgpm_ref[...] += dgpm

    row = lambda dt: jax.ShapeDtypeStruct((t, D), dt)
    tile = pl.BlockSpec((tt, D), lambda i: (i, 0))
    wsp = pl.BlockSpec((D, D), lambda i: (0, 0))
    vec = pl.BlockSpec((1, D), lambda i: (0, 0))
    gates = pl.BlockSpec((None, tt, 2 * D), lambda i: (SLOT_G, i, 0))
    return pl.pallas_call(
        body, name="mid_bwd",
        out_shape=(row(F32), row(BF16), row(BF16), row(BF16), row(BF16), row(BF16),
                   jax.ShapeDtypeStruct((NCHIP, t, 2 * D), BF16),
                   jax.ShapeDtypeStruct((1, D), F32), jax.ShapeDtypeStruct((1, D), F32)),
        grid=(t // tt,),
        in_specs=[tile, tile, tile, tile, tile, tile, gates, wsp, wsp, wsp, vec, vec],
        out_specs=(tile, tile, tile, tile, tile, tile, gates, vec, vec),
        compiler_params=_params(("arbitrary",), VMEM_BIG),
    )(dh2, dx2, x1, m2, za, zb, p4, wa, wb, wo, g_pm, g_pf)


def _hgrn2_bwd(p4, o_all, ss, saved, dyb, dp4, logits, gnorm, p_early, tt):
    t = p4.shape[1]
    nt = t // tt
    nc = tt // CH
    ne = len(p_early)

    def body(p_ref, o_ref, ss_ref, sv_ref, dyb_ref, dp_in, lg_ref, gn_ref, *rest):
        del dp_in
        pe = rest[:ne]
        dp_ref, dlb_ref, dgn_ref = rest[ne:ne + 3]
        qe = rest[ne + 3:2 * ne + 3]
        dst, ssem, rsem = rest[2 * ne + 3:]
        i = pl.program_id(0)
        sends, arrive = _exchange_copies(qe, [(lambda chip, r=r: r.at[chip]) for r in pe], ssem, rsem)

        @pl.when(i == 0)
        def _():
            dst[...] = jnp.zeros((NH, HD, HD), F32)
            for cp in sends:
                cp.start()

        low = _chunk_masks(tt)
        lb = _lb_of(lg_ref)
        heads = [slice(h * HD, (h + 1) * HD) for h in range(NH)]
        sg, sgn, f, k, g, eg, eng, egu, qt, kt, kd = _hg_head_fwd(p_ref[0, :, 0:D], p_ref[0, :, D:2 * D], lb,
                                                                  (sv_ref[0], sv_ref[1]))
        qtb, ktb, kdb, vb = qt.astype(BF16), kt.astype(BF16), kd.astype(BF16), p_ref[1, :, 0:D].astype(BF16)
        decs = [jnp.exp(g[c * CH + CH - 1:c * CH + CH, :]) for c in range(nc)]
        og = p_ref[1, :, D:2 * D]
        so = _sig(og)
        dyb = dyb_ref[...].astype(F32)
        dob = dyb * (og * so)
        rn = [_rms_stats(o_ref[:, hs]) for hs in heads]
        r_all = jnp.concatenate([jnp.broadcast_to(r, (tt, HD)) for r, _ in rn], axis=1)
        n_all = jnp.concatenate([n for _, n in rn], axis=1)
        gd = dob * gn_ref[...]
        proj = jnp.concatenate(
            [jnp.broadcast_to(jnp.mean(gd[:, hs] * n_all[:, hs], axis=-1, keepdims=True), (tt, HD)) for hs in heads],
            axis=1)
        dob_ = (r_all * (gd - n_all * proj)).astype(BF16)
        dog = dyb * (n_all * gn_ref[...]) * (so * (1.0 + og * (1.0 - so)))
        dgn = jnp.sum(dob * n_all, axis=0, keepdims=True)
        dv_in, dqt_in, dkt_h = [], [], []
        for hs in heads:
            att = jnp.where(low, _dot_nt(qtb[:, hs], ktb[:, hs]), 0.0).astype(BF16)
            d_att = jnp.where(low, _dot_nt(dob_[:, hs], vb[:, hs]), 0.0).astype(BF16)
            dv_in.append(_dot_tn(att, dob_[:, hs]))
            dqt_in.append(_dot(d_att, ktb[:, hs]))
            dkt_h.append(_dot_tn(d_att, qtb[:, hs]))
        ds_t = [dst[h] for h in range(NH)]
        dv_p = [[None] * NH for _ in range(nc)]
        dqt_p = [[None] * NH for _ in range(nc)]
        dkd_p = [[None] * NH for _ in range(nc)]
        dgl_p = [[None] * NH for _ in range(nc)]
        for c in reversed(range(nc)):
            sl = slice(c * CH, (c + 1) * CH)
            for h, hs in enumerate(heads):
                s_prev = ss_ref[c, h]
                ds_bf = ds_t[h].astype(BF16)
                dec = decs[c][:, hs]
                dv_p[c][h] = dv_in[h][sl] + _dot_nt(kdb[sl, hs], ds_bf)
                dqt_p[c][h] = dqt_in[h][sl] + _dot(dob_[sl, hs], s_prev)
                dkd_p[c][h] = _dot(vb[sl, hs], ds_bf)
                ddec = jnp.sum(s_prev.astype(F32) * ds_t[h], axis=0, keepdims=True)
                dgl_p[c][h] = jnp.broadcast_to(ddec * dec, (CH, HD))
                ds_t[h] = ds_t[h] * dec + _dot_tn(dob_[sl, hs], qtb[sl, hs])
        for h in range(NH):
            dst[h] = ds_t[h]
        whole = lambda parts: jnp.concatenate([jnp.concatenate(row, axis=1) for row in parts], axis=0)
        dv, dqt, dkd, dgl = whole(dv_p), whole(dqt_p), whole(dkd_p), whole(dgl_p)
        dkt = jnp.concatenate(dkt_h, axis=1)
        dq = dqt * eg
        dk = dkt * eng + dkd * egu
        dg = dqt * qt - dkt * kt
        dgu = dkd * kd
        dlogf = _chunk_revcumsum(dg - dgu) + _chunk_total(dgu) + dgl
        common = sgn * (dlogf / f - dk)
        dfz = (1.0 - lb) * sg * common
        dlb = jnp.sum(common, axis=0, keepdims=True)
        dp_ref[0, :, 0:D] = dq.astype(BF16)
        dp_ref[0, :, D:2 * D] = dfz.astype(BF16)
        dp_ref[1, :, 0:D] = dv.astype(BF16)
        dp_ref[1, :, D:2 * D] = dog.astype(BF16)

        @pl.when(i == 0)
        def _():
            dlb_ref[0:1, :] = dlb
            dgn_ref[...] = dgn

        @pl.when(i > 0)
        def _():
            dlb_ref[0:1, :] += dlb
            dgn_ref[...] += dgn

        @pl.when(i == nt - 1)
        def _():
            d0 = dlb_ref[0:1, :] * lb * (1.0 - lb)
            dlb_ref[0:1, :] = d0
            dlb_ref[1:2, :] = -d0
            for cp in arrive:
                cp.wait_recv()
            for cp in sends:
                cp.wait_send()

    rev = lambda i: nt - 1 - i
    vec = pl.BlockSpec((1, D), lambda i: (0, 0))
    any_spec = pl.BlockSpec(memory_space=pl.ANY)
    outs = pl.pallas_call(
        body, name="hgrn2_bwd",
        out_shape=(jax.ShapeDtypeStruct(dp4.shape, BF16), jax.ShapeDtypeStruct((2, D), F32),
                   jax.ShapeDtypeStruct((1, D), F32)) + tuple(jax.ShapeDtypeStruct(a.shape, BF16) for a in p_early),
        grid=(nt,),
        in_specs=[pl.BlockSpec((2, tt, 2 * D), lambda i: (0, rev(i), 0)),
                  pl.BlockSpec((tt, D), lambda i: (rev(i), 0)),
                  pl.BlockSpec((nc, NH, HD, HD), lambda i: (rev(i), 0, 0, 0)),
                  pl.BlockSpec((2, tt, D), lambda i: (0, rev(i), 0)),
                  pl.BlockSpec((tt, D), lambda i: (rev(i), 0)),
                  any_spec,
                  pl.BlockSpec((2, D), lambda i: (0, 0)), vec] + [any_spec] * ne,
        out_specs=(pl.BlockSpec((2, tt, 2 * D), lambda i: (0, rev(i), 0)),
                   pl.BlockSpec((2, D), lambda i: (0, 0)), vec) + (any_spec,) * ne,
        scratch_shapes=[pltpu.VMEM((NH, HD, HD), F32), pltpu.SemaphoreType.DMA((3 * ne,)),
                        pltpu.SemaphoreType.DMA((3 * ne,))],
        input_output_aliases={5: 0},
        compiler_params=pltpu.CompilerParams(dimension_semantics=("arbitrary",), vmem_limit_bytes=VMEM_BIG,
                                             has_side_effects=True),
    )(p4, o_all, ss, saved, dyb, dp4, logits, gnorm, *p_early)
    return outs[0], outs[1], outs[2], list(outs[3:])


def _mixer_a_bwd(p4, hseq, saved, dya, dp4, cw, wa, wx, lam, tt):
    t = p4.shape[1]
    nt = t // tt

    def body(p_ref, sv_ref, h_ref, hh_ref, dya_ref, dp_in, cw_ref, wa_ref, wx_ref, lam_ref,
             dp_ref, dcw_ref, dcb_ref, dwa_ref, dwx_ref, dba_ref, dbx_ref, dlam_ref,
             dnext, dhc, afc):
        del dp_in
        i = pl.program_id(0)
        first_tile = i == nt - 1

        @pl.when(i == 0)
        def _():
            dnext[...] = jnp.zeros((8, D), F32)
            dhc[...] = jnp.zeros((1, D), F32)
            afc[...] = jnp.zeros((1, D), F32)

        xa = p_ref[:, 0:D]
        ga = p_ref[:, D:2 * D]
        xc, r, ig, a, mult = (sv_ref[idx] for idx in range(5))
        xcb = xc.astype(BF16)
        lam = lam_ref[...]
        sp = jnp.maximum(-lam, 0.0) + jnp.log1p(jnp.exp(-jnp.abs(lam)))
        h = h_ref[...]
        gl, dgl = _gelu_and_grad(ga)
        dya = dya_ref[...].astype(F32)
        dga = dya * h * dgl
        rows = lax.broadcasted_iota(jnp.int32, (tt, 1), 0)
        a_next = jnp.where(rows == tt - 1, afc[...], pltpu.roll(a, tt - 1, 0))
        dh = _scan_up(a_next, dya * gl, dhc[...])
        dhc[...] = dh[0:1, :]
        afc[...] = a[0:1, :]
        h_prev = jnp.where(rows == 0, jnp.where(first_tile, 0.0, hh_ref[7:8, :]), pltpu.roll(h, 1, 0))
        da = dh * h_prev
        dmult = dh * ig * xc
        di = dh * mult * xc
        dlog_a = da * a - dmult * a * a / mult
        dr = dlog_a * (-LRU_C * sp)
        dsp = jnp.sum(dlog_a * (-LRU_C * r), axis=0, keepdims=True)
        dra = dr * r * (1.0 - r)
        dix = di * ig * (1.0 - ig)
        drab = dra.astype(BF16)
        dixb = dix.astype(BF16)
        dxc_lin = []
        dwa_new = []
        dwx_new = []
        for n in range(NH):
            cs = slice(n * HD, (n + 1) * HD)
            dxc_lin.append(_dot_nt(drab[:, cs], wa_ref[n]) + _dot_nt(dixb[:, cs], wx_ref[n]))
            dwa_new.append(_dot_tn(xcb[:, cs], drab[:, cs]))
            dwx_new.append(_dot_tn(xcb[:, cs], dixb[:, cs]))
        dxc = dh * mult * ig + jnp.concatenate(dxc_lin, axis=1)
        de = jnp.concatenate([dxc, dnext[...]], axis=0)
        ups = [de[0:tt, :]] + [pltpu.roll(de, tt + 8 - s, 0)[0:tt, :] for s in (1, 2, 3)]
        dxa = sum(cw_ref[3 - s:4 - s, :] * ups[s] for s in range(4))
        dnext[...] = dxc[0:8, :]
        dp_ref[:, 0:D] = dxa.astype(BF16)
        dp_ref[:, D:2 * D] = dga.astype(BF16)
        dcw = jnp.concatenate(
            [jnp.sum(ups[3 - k] * xa, axis=0, keepdims=True) for k in range(4)], axis=0)
        dcb = jnp.sum(dxc, axis=0, keepdims=True)
        dba = jnp.sum(dra, axis=0, keepdims=True)
        dbx = jnp.sum(dix, axis=0, keepdims=True)
        dlam = dsp * (-_sig(-lam))

        @pl.when(i == 0)
        def _():
            dcw_ref[...] = dcw
            dcb_ref[...] = dcb
            dba_ref[...] = dba
            dbx_ref[...] = dbx
            dlam_ref[...] = dlam
            for n in range(NH):
                dwa_ref[n] = dwa_new[n]
                dwx_ref[n] = dwx_new[n]

        @pl.when(i > 0)
        def _():
            dcw_ref[...] += dcw
            dcb_ref[...] += dcb
            dba_ref[...] += dba
            dbx_ref[...] += dbx
            dlam_ref[...] += dlam
            for n in range(NH):
                dwa_ref[n] += dwa_new[n]
                dwx_ref[n] += dwx_new[n]

    rev = lambda i: nt - 1 - i
    hb = tt // 8
    full = lambda shape: pl.BlockSpec(shape, lambda i: (0,) * len(shape))
    vecs = jax.ShapeDtypeStruct((1, D), F32)
    blk = jax.ShapeDtypeStruct((NH, HD, HD), F32)
    return pl.pallas_call(
        body, name="mixer_a_bwd",
        out_shape=(jax.ShapeDtypeStruct(dp4.shape, BF16), jax.ShapeDtypeStruct((4, D), F32), vecs, blk, blk,
                   vecs, vecs, vecs),
        grid=(nt,),
        in_specs=[pl.BlockSpec((None, tt, 2 * D), lambda i: (SLOT_A, rev(i), 0)),
                  pl.BlockSpec((5, tt, D), lambda i: (0, rev(i), 0)),
                  pl.BlockSpec((tt, D), lambda i: (rev(i), 0)),
                  pl.BlockSpec((8, D), lambda i: (jnp.maximum(rev(i) * hb - 1, 0), 0)),
                  pl.BlockSpec((tt, D), lambda i: (rev(i), 0)),
                  pl.BlockSpec(memory_space=pl.ANY),
                  full((4, D)), full((NH, HD, HD)), full((NH, HD, HD)), full((1, D))],
        out_specs=(pl.BlockSpec((None, tt, 2 * D), lambda i: (SLOT_A, rev(i), 0)),
                   full((4, D)), full((1, D)), full((NH, HD, HD)), full((NH, HD, HD)),
                   full((1, D)), full((1, D)), full((1, D))),
        scratch_shapes=[pltpu.VMEM((8, D), F32), pltpu.VMEM((1, D), F32), pltpu.VMEM((1, D), F32)],
        input_output_aliases={5: 0},
        compiler_params=_params(("arbitrary",), VMEM_BIG),
    )(p4, saved, hseq, hseq, dya, dp4, cw, wa, wx, lam)


def _local_step(x, tgt, stk_w_in, stk_rest, conv_a_s, conv_f_s, small, jc, cidx):
    t = x.shape[0]
    tt = min(256, t)
    tm = min(1024, t)
    tk = min(2048, t)
    wa_bf = small["lru_wa"].astype(BF16)
    wx_bf = small["lru_wx"].astype(BF16)

    h1 = _norm_fwd(x, small["norm_pre_mix"], tt)
    p4, w_in, _, conv_a_g, conv_f_g = _mm_in_gather(h1, stk_w_in, [], (), conv_a_s, conv_f_s, jc, tm)
    conv_a_w = jnp.transpose(conv_a_g, (1, 0, 2)).reshape(8, D)[0:4]
    conv_f_w = jnp.transpose(conv_f_g, (1, 0, 2)).reshape(8, DUP)[0:3]
    stk = dict(zip(REST, stk_rest))
    ya, hseq, saved_a, got_a = _mixer_a_fwd(p4, conv_a_w, small["conv_a_b"], wa_bf, wx_bf, small["lru_ba"],
                                            small["lru_bx"], small["lru_lambda"],
                                            [stk[n] for n in REST_A], REST_A, tt)
    yb, o_all, ss, saved_b, got_b = _hgrn2_fwd(p4, small["hg_lb_logits"], small["hg_norm_g"],
                                               [stk[n] for n in REST_B], REST_B, tt)
    w = dict(zip(REST_A + REST_B, _gather_forward(got_a + got_b, REST_A + REST_B)))
    w["w_in"] = w_in
    w_br_a = w["w_branch_a"].reshape(D, D)
    w_br_b = w["w_branch_b"].reshape(D, D)
    w_out = w["w_out"].reshape(D, D)
    w_down = w["w_down"].reshape(DFF, D)
    za, zb, mixb, m2, x1, h2 = _mid_fwd(ya, yb, p4, x, w_br_a, w_br_b, w_out, small["norm_post_mix"],
                                        small["norm_pre_ffn"], min(512, t))
    pre_g, pre_v, y, uv, gl, dgl = _up_act_fwd(h2, w["w_up"], conv_f_w, small["conv_f_b"], tt)
    dx2, dm3, lossv, d_norm_post_ffn = _down_loss(y, w_down, x1, tgt, small["norm_post_ffn"], min(512, t))

    d_w_down = _mm_tn(y, dm3, DFF // 2, D, tk, "mm_dw_down")
    dy = _mm_nt(dm3, w_down, BF16, tm, "mm_dy")
    dup_pre, d_conv_f_w, d_conv_f_b = _ffn_act_bwd(dy, pre_g, pre_v, uv, gl, dgl, conv_f_w, tt)
    d_w_up = _mm_tn(h2, dup_pre, D, SH_UP, tk, "mm_dw_up", stacked_out=True)
    dh2 = _mm_nt_sharded(dup_pre, w["w_up"], tm, "mm_dh2")
    dx1, dm2, dza, dzb, dya, dyb, dp4, d_norm_post_mix, d_norm_pre_ffn = _mid_bwd(
        dh2, dx2, x1, m2, za, zb, p4, w_br_a, w_br_b, w_out, small["norm_post_mix"], small["norm_pre_ffn"], tt)
    d_w_out = _mm_tn(mixb, dm2, D, D, tm, "mm_dw_out")
    d_w_br_a = _mm_tn(ya, dza, D, D, tm, "mm_dw_bra")
    d_w_br_b = _mm_tn(yb, dzb, D, D, tm, "mm_dw_brb")
    early = {"w_branch_a": d_w_br_a.reshape(NCHIP, SH_BR, D), "w_branch_b": d_w_br_b.reshape(NCHIP, SH_BR, D),
             "w_out": d_w_out.reshape(NCHIP, SH_BR, D), "w_up": d_w_up, "w_down": d_w_down.reshape(NCHIP, SH_DN, D)}
    rb, _ = _reduce_stage1(early, REST, (), "reduce_d2d_in_early")
    p_rest = [_sum_own_half(early[n], rb[n], cidx, "sum_half_" + n) for n in REST]
    dp4, d_lb, d_hg_norm_g, q_rest = _hgrn2_bwd(p4, o_all, ss, saved_b, dyb, dp4, small["hg_lb_logits"], small["hg_norm_g"],
                                                p_rest, tt)
    dp4, d_conv_a_w, d_conv_a_b, d_lru_wa, d_lru_wx, d_lru_ba, d_lru_bx, d_lru_lambda = _mixer_a_bwd(
        p4, hseq, saved_a, dya, dp4, conv_a_w, wa_bf, wx_bf, small["lru_lambda"], tt)
    d_w_in = _mm_tn(h1, dp4, D, SH_IN, tk, "mm_dw_in", stacked_slot_fn=_slot_of_chip, stacked_out=True)
    rb, _ = _reduce_stage1({"w_in": d_w_in}, ("w_in",), (), "reduce_d2d_in_w_in")
    p_w_in = _sum_own_half(d_w_in, rb["w_in"], cidx, "sum_half_w_in")
    grad_x, q_w_in, d_norm_pre_mix = _mm_dh1_exchange(dp4, w_in, p_w_in, x, dx1, small["norm_pre_mix"], tm)

    smalls = {
        "norm_pre_mix": d_norm_pre_mix, "conv_a_b": d_conv_a_b, "lru_ba": d_lru_ba, "lru_bx": d_lru_bx,
        "lru_lambda": d_lru_lambda, "hg_lb_logits": d_lb, "hg_norm_g": d_hg_norm_g, "norm_post_mix": d_norm_post_mix,
        "norm_pre_ffn": d_norm_pre_ffn, "norm_post_ffn": d_norm_post_ffn, "lossv": lossv,
        "conv_a_w": d_conv_a_w, "lru_wa": d_lru_wa, "lru_wx": d_lru_wx,
        "conv_f_b": d_conv_f_b, "conv_f_w": d_conv_f_w,
    }
    p_big = dict(zip(REST, p_rest), w_in=p_w_in)
    q_big = dict(zip(REST, q_rest), w_in=q_w_in)
    return grad_x, p_big, q_big, smalls


BIG = ("w_in", "w_branch_a", "w_branch_b", "w_out", "w_up", "w_down")
BIG_SHAPE = {"w_in": (D, SH_IN), "w_branch_a": (SH_BR, D), "w_branch_b": (SH_BR, D), "w_out": (SH_BR, D),
             "w_up": (D, SH_UP), "w_down": (SH_DN, D)}
NBIG = len(BIG)
REST = BIG[1:]
REST_A = ("w_up", "w_down")
REST_B = ("w_branch_a", "w_branch_b", "w_out")
VEC_ROWS = (("norm_pre_mix", 0, 1), ("conv_a_b", 1, 1), ("lru_ba", 2, 1), ("lru_bx", 3, 1), ("lru_lambda", 4, 1),
            ("hg_lb_logits", 5, 2), ("hg_norm_g", 7, 1), ("norm_post_mix", 8, 1), ("norm_pre_ffn", 9, 1),
            ("norm_post_ffn", 10, 1))
ROW_LOSS = 11
ROW_CONV_A = 12
S1_ROWS = 16
S2_ROWS = 8


def _place():
    x, y, c = lax.axis_index("x"), lax.axis_index("y"), lax.axis_index("c")
    chips = [(1 - x, y), (x, 1 - y), (1 - x, 1 - y)]
    return x, y, c, 2 * x + y, chips


def _remote(src, dst, ssem, rsem, dev):
    return pltpu.make_async_remote_copy(src_ref=src, dst_ref=dst, send_sem=ssem, recv_sem=rsem,
                                        device_id=dev, device_id_type=MESH)


def _hbm_call(body, name, ins, out_shapes, n_sems, aliases=None):
    any_spec = pl.BlockSpec(memory_space=pl.ANY)
    return pl.pallas_call(
        body, name=name, out_shape=tuple(out_shapes),
        in_specs=[any_spec] * len(ins), out_specs=tuple([any_spec] * len(out_shapes)),
        scratch_shapes=[pltpu.SemaphoreType.DMA((n,)) for n in n_sems],
        input_output_aliases=aliases or {},
        compiler_params=pltpu.CompilerParams(has_side_effects=True),
    )(*ins)


def _gather_copies(stk, names, ssem, rsem, fssem=None, frsem=None):
    x, y, c, j, chips = _place()
    sends, arrive, fwds, farrive = [], [], [], []
    for w, n in enumerate(names):
        hw = BIG_SHAPE[n][0] // 2
        mine = stk[w].at[j, pl.ds(c * hw, hw), :]
        for k, (cx, cy) in enumerate(chips):
            i = 3 * w + k
            got = stk[w].at[2 * cx + cy, pl.ds(c * hw, hw), :]
            other = stk[w].at[2 * cx + cy, pl.ds((1 - c) * hw, hw), :]
            sends.append(_remote(mine, mine, ssem.at[i], rsem.at[i], (cx, cy, c)))
            arrive.append(_remote(got, got, ssem.at[i], rsem.at[i], (cx, cy, c)))
            if fssem is not None:
                fwds.append(_remote(got, got, fssem.at[i], frsem.at[i], (x, y, 1 - c)))
                farrive.append(_remote(other, other, fssem.at[i], frsem.at[i], (x, y, 1 - c)))
    return sends, arrive, fwds, farrive


def _ici_leg_behind(stk, names, ssem, rsem, first, last_fn):
    sends, arrive, _, _ = _gather_copies(stk, names, ssem, rsem)

    @pl.when(first)
    def _():
        for cp in sends:
            cp.start()

    def finish():
        @pl.when(last_fn())
        def _():
            for cp in arrive:
                cp.wait_recv()
            for cp in sends:
                cp.wait_send()

    return finish


def _gather_forward(stk, names):
    nw = len(names)

    def body(*refs):
        dst = refs[nw:2 * nw]
        ssem, rsem, fssem, frsem = refs[2 * nw:]
        _, _, fwds, farrive = _gather_copies(dst, names, ssem, rsem, fssem, frsem)
        for cp in fwds:
            cp.start()
        for cp in farrive:
            cp.wait_recv()
        for cp in fwds:
            cp.wait_send()

    out_shapes = [jax.ShapeDtypeStruct(a.shape, a.dtype) for a in stk]
    return _hbm_call(body, "gather_forward", stk, out_shapes, (3 * nw,) * 4, aliases={w: w for w in range(nw)})


def _exchange_copies(dst, pieces, ssem, rsem):
    x, y, c, j, chips = _place()
    sends, arrive = [], []
    for w in range(len(dst)):
        for k, (cx, cy) in enumerate(chips):
            i = 3 * w + k
            sends.append(_remote(pieces[w](2 * cx + cy), dst[w].at[j], ssem.at[i], rsem.at[i], (cx, cy, c)))
            got = dst[w].at[2 * cx + cy]
            arrive.append(_remote(got, got, ssem.at[i], rsem.at[i], (cx, cy, c)))
    return sends, arrive


def _reduce_stage1(big_g, names, smalls, name):
    nb = len(names)
    ins = [big_g[n] for n in names] + list(smalls)
    n_in = len(ins)
    halves = [BIG_SHAPE[n][0] // 2 for n in names]
    out_shapes = [jax.ShapeDtypeStruct((NCHIP, halves[w], BIG_SHAPE[n][1]), big_g[n].dtype)
                  for w, n in enumerate(names)]
    out_shapes += [jax.ShapeDtypeStruct(a.shape, F32) for a in smalls]

    def body(*refs):
        src, dst = refs[:n_in], refs[n_in:2 * n_in]
        ssem, rsem = refs[2 * n_in:]
        x, y, c, _, _ = _place()
        cps = []
        for w in range(n_in):
            s_ = src[w].at[:, pl.ds((1 - c) * halves[w], halves[w]), :] if w < nb else src[w]
            cp = _remote(s_, dst[w], ssem.at[w], rsem.at[w], (x, y, 1 - c))
            cp.start()
            cps.append(cp)
        for cp in cps:
            cp.wait()

    outs = _hbm_call(body, name, ins, out_shapes, (n_in, n_in))
    return dict(zip(names, outs[:nb])), outs[nb:]


def _reduce_stage2(ps1, ps2, ps3):
    ins = [ps1, ps2, ps3]
    h1, h2, h3 = S1_ROWS // 2, DUP // 2, D
    out_shapes = [jax.ShapeDtypeStruct((NCHIP, h1, D), F32), jax.ShapeDtypeStruct((NCHIP, S2_ROWS, h2), F32),
                  jax.ShapeDtypeStruct((NCHIP, h3, HD), F32)]

    def body(*refs):
        src, dst = refs[:3], refs[3:6]
        ssem, rsem = refs[6:]
        c = lax.axis_index("c")
        pieces = [lambda chip: src[0].at[pl.ds(c * h1, h1), :],
                  lambda chip: src[1].at[:, pl.ds(c * h2, h2)],
                  lambda chip: src[2].at[pl.ds(c * h3, h3), :]]
        sends, arrive = _exchange_copies(dst, pieces, ssem, rsem)
        for cp in sends:
            cp.start()
        for cp in arrive:
            cp.wait_recv()
        for cp in sends:
            cp.wait_send()

    return _hbm_call(body, "reduce_ici_small", ins, out_shapes, (9, 9))


def _reduce_stage3(f_big, fs1, fs2, fs3):
    ins = [f_big[n] for n in BIG] + [fs1, fs2, fs3]
    n_in = len(ins)
    halves = [BIG_SHAPE[n][0] // 2 for n in BIG]
    h1, h2, h3 = S1_ROWS // 2, DUP // 2, D
    out_shapes = [jax.ShapeDtypeStruct(BIG_SHAPE[n], F32) for n in BIG]
    out_shapes += [jax.ShapeDtypeStruct((S1_ROWS, D), F32), jax.ShapeDtypeStruct((S2_ROWS, DUP), F32),
                   jax.ShapeDtypeStruct((2 * D, HD), F32)]

    def body(*refs):
        dst = refs[n_in:2 * n_in]
        ssem, rsem = refs[2 * n_in:]
        x, y, c, _, _ = _place()

        def place(w, which):
            if w < NBIG:
                return dst[w].at[pl.ds(which * halves[w], halves[w]), :]
            if w == NBIG:
                return dst[w].at[pl.ds(which * h1, h1), :]
            if w == NBIG + 1:
                return dst[w].at[:, pl.ds(which * h2, h2)]
            return dst[w].at[pl.ds(which * h3, h3), :]

        cps = [_remote(place(w, c), place(w, c), ssem.at[w], rsem.at[w], (x, y, 1 - c)) for w in range(n_in)]
        for cp in cps:
            cp.start()
        for w in range(n_in):
            got = place(w, 1 - c)
            _remote(got, got, ssem.at[w], rsem.at[w], (x, y, 1 - c)).wait_recv()
        for cp in cps:
            cp.wait_send()

    outs = _hbm_call(body, "reduce_d2d_out", ins, out_shapes, (n_in, n_in), aliases={w: w for w in range(n_in)})
    return dict(zip(BIG, outs[:NBIG])), outs[NBIG], outs[NBIG + 1], outs[NBIG + 2]


def _row_tile(rows):
    for tr in (128, 176, 64, 16, 8):
        if rows % tr == 0:
            return tr
    return rows


def _sum_own_half(g, rb, cidx, name):
    s, rows, cols = g.shape
    half = rows // 2
    tr = _row_tile(half)
    nb = half // tr

    def body(c_ref, g_ref, r_ref, o_ref):
        del c_ref
        o_ref[...] = (g_ref[...].astype(F32) + r_ref[...].astype(F32)).astype(BF16)

    grid_spec = pltpu.PrefetchScalarGridSpec(
        num_scalar_prefetch=1, grid=(s, nb),
        in_specs=[pl.BlockSpec((None, tr, cols), lambda k, i, c: (k, c[0] * nb + i, 0)),
                  pl.BlockSpec((None, tr, cols), lambda k, i, c: (k, i, 0))],
        out_specs=pl.BlockSpec((None, tr, cols), lambda k, i, c: (k, i, 0)))
    return pl.pallas_call(
        body, name=name, grid_spec=grid_spec, out_shape=jax.ShapeDtypeStruct((s, half, cols), BF16),
        compiler_params=_params(("parallel", "parallel")),
    )(cidx, g, rb)


def _sum_chips(q, p, jc, name, by_cols=False):
    s, rows, cols = q.shape
    tr = _row_tile(rows)
    nb = rows // tr
    stacked = p.ndim == 3

    def body(jc_ref, q_ref, p_ref, o_ref):
        j = jc_ref[0]
        own = p_ref[...].astype(F32)
        acc = None
        for k in range(NCHIP):
            term = jnp.where(j == k, own, q_ref[k].astype(F32))
            acc = term if acc is None else acc + term
        o_ref[...] = acc

    if by_cols:
        half_spec = pl.BlockSpec((tr, cols), lambda i, jc_ref: (i, jc_ref[1]))
        out_shape = jax.ShapeDtypeStruct((rows, 2 * cols), F32)
    else:
        half_spec = pl.BlockSpec((tr, cols), lambda i, jc_ref: (jc_ref[1] * nb + i, 0))
        out_shape = jax.ShapeDtypeStruct((2 * rows, cols), F32)
    p_spec = pl.BlockSpec((None, tr, cols), lambda i, jc_ref: (jc_ref[0], i, 0)) if stacked else half_spec
    grid_spec = pltpu.PrefetchScalarGridSpec(
        num_scalar_prefetch=1, grid=(nb,),
        in_specs=[pl.BlockSpec((s, tr, cols), lambda i, jc_ref: (0, i, 0)), p_spec],
        out_specs=half_spec)
    return pl.pallas_call(
        body, name=name, grid_spec=grid_spec, out_shape=out_shape,
        compiler_params=_params(("parallel",)),
    )(jc, q, p)


def _place_shard(w, jc, name):
    rows, cols = w.shape
    tr = _row_tile(rows)

    def body(jc_ref, w_ref, o_ref):
        del jc_ref
        o_ref[...] = w_ref[...].astype(BF16)

    grid_spec = pltpu.PrefetchScalarGridSpec(
        num_scalar_prefetch=1, grid=(rows // tr,),
        in_specs=[pl.BlockSpec((tr, cols), lambda i, jc_ref: (i, 0))],
        out_specs=pl.BlockSpec((None, tr, cols), lambda i, jc_ref: (jc_ref[0], i, 0)))
    return pl.pallas_call(
        body, name=name, grid_spec=grid_spec, out_shape=jax.ShapeDtypeStruct((NCHIP, rows, cols), BF16),
        compiler_params=_params(("parallel",)),
    )(jc, w)


def _add(a, b, name):
    def body(a_ref, b_ref, o_ref):
        o_ref[...] = a_ref[...] + b_ref[...]

    return pl.pallas_call(body, name=name, out_shape=jax.ShapeDtypeStruct(a.shape, F32))(a, b)


def _pack_small(sm):
    vec_in = [sm[n] for n, _, _ in VEC_ROWS]
    nv = len(vec_in)

    def body(*refs):
        ins, lossv, dcw, dcfb, dcfw, s1, s2 = refs[:nv], refs[nv], refs[nv + 1], refs[nv + 2], refs[nv + 3], \
            refs[nv + 4], refs[nv + 5]
        for ref, (_, r0, nr) in zip(ins, VEC_ROWS):
            s1[r0:r0 + nr, :] = ref[...]
        s1[ROW_LOSS:ROW_LOSS + 1, :] = lossv[...]
        s1[ROW_CONV_A:ROW_CONV_A + 4, :] = dcw[...]
        s2[0:1, :] = dcfb[...]
        s2[1:4, :] = dcfw[...]
        s2[4:8, :] = jnp.zeros((4, DUP), F32)

    return pl.pallas_call(
        body, name="pack_small",
        out_shape=(jax.ShapeDtypeStruct((S1_ROWS, D), F32), jax.ShapeDtypeStruct((S2_ROWS, DUP), F32)),
    )(*vec_in, sm["lossv"], sm["conv_a_w"], sm["conv_f_b"], sm["conv_f_w"])


def _adam_math(w, g, m, v):
    m = ADAM_B1 * m + (1.0 - ADAM_B1) * g
    v = ADAM_B2 * v + (1.0 - ADAM_B2) * (g * g)
    m_hat = m / (1.0 - ADAM_B1 ** ADAM_STEP)
    v_hat = v / (1.0 - ADAM_B2 ** ADAM_STEP)
    delta = -ADAM_LR * (m_hat / (jnp.sqrt(v_hat) + ADAM_EPS) + ADAM_WD * w)
    return delta, m, v


def _adam(w, g, m, v, name):
    rows, cols = w.shape
    tr = _row_tile(rows)

    def body(w_ref, g_ref, m_ref, v_ref, d_ref, mo_ref, vo_ref):
        d_ref[...], mo_ref[...], vo_ref[...] = _adam_math(w_ref[...], g_ref[...], m_ref[...], v_ref[...])

    spec = pl.BlockSpec((tr, cols), lambda i: (i, 0))
    return pl.pallas_call(
        body, name=name, out_shape=(jax.ShapeDtypeStruct(w.shape, F32),) * 3, grid=(rows // tr,),
        in_specs=[spec] * 4, out_specs=(spec,) * 3,
        compiler_params=_params(("parallel",)),
    )(w, g, m, v)


def _adam_small(gs1, gs2, gs3, w, m, v):
    names = [n for n, _, _ in VEC_ROWS] + ["conv_f_b", "lru_wa", "lru_wx"]
    nn = len(names)

    def grad_of(i, g1, g2, g3):
        if i < len(VEC_ROWS):
            _, r0, nr = VEC_ROWS[i]
            return g1[r0:r0 + nr, :]
        if names[i] == "conv_f_b":
            return g2[0:1, :]
        return g3[0] if names[i] == "lru_wa" else g3[1]

    def body(*refs):
        g1, g2, g3 = refs[0], refs[1], refs[2]
        ws, ms, vs = refs[3:3 + nn], refs[3 + nn:3 + 2 * nn], refs[3 + 2 * nn:3 + 3 * nn]
        outs = refs[3 + 3 * nn:]
        for i in range(nn):
            d, mn, vn = _adam_math(ws[i][...], grad_of(i, g1, g2, g3), ms[i][...], vs[i][...])
            outs[i][...] = d
            outs[nn + i][...] = mn
            outs[2 * nn + i][...] = vn

    shapes = [jax.ShapeDtypeStruct(w[n].shape, F32) for n in names]
    outs = pl.pallas_call(body, name="adam_small", out_shape=tuple(shapes * 3))(
        gs1, gs2, gs3, *[w[n] for n in names], *[m[n] for n in names], *[v[n] for n in names])
    return {n: (outs[i], outs[nn + i], outs[2 * nn + i]) for i, n in enumerate(names)}


WEIGHTS = ("norm_pre_mix", "w_in", "conv_a_w", "conv_a_b", "lru_wa", "lru_ba", "lru_wx", "lru_bx", "lru_lambda",
           "hg_lb_logits", "hg_norm_g", "w_branch_a", "w_branch_b", "w_out", "norm_post_mix", "norm_pre_ffn",
           "w_up", "conv_f_w", "conv_f_b", "w_down", "norm_post_ffn")
NW = len(WEIGHTS)


def kernel(x, norm_pre_mix, w_in, conv_a_w, conv_a_b, lru_wa, lru_ba, lru_wx, lru_bx, lru_lambda, hg_lb_logits, hg_norm_g, w_branch_a, w_branch_b, w_out, norm_post_mix, norm_pre_ffn, w_up, conv_f_w, conv_f_b, w_down, norm_post_ffn, loss_target, m_norm_pre_mix, m_w_in, m_conv_a_w, m_conv_a_b, m_lru_wa, m_lru_ba, m_lru_wx, m_lru_bx, m_lru_lambda, m_hg_lb_logits, m_hg_norm_g, m_w_branch_a, m_w_branch_b, m_w_out, m_norm_post_mix, m_norm_pre_ffn, m_w_up, m_conv_f_w, m_conv_f_b, m_w_down, m_norm_post_ffn, v_norm_pre_mix, v_w_in, v_conv_a_w, v_conv_a_b, v_lru_wa, v_lru_ba, v_lru_wx, v_lru_bx, v_lru_lambda, v_hg_lb_logits, v_hg_norm_g, v_w_branch_a, v_w_branch_b, v_w_out, v_norm_post_mix, v_norm_pre_ffn, v_w_up, v_conv_f_w, v_conv_f_b, v_w_down, v_norm_post_ffn):
    rest = (norm_pre_mix, w_in, conv_a_w, conv_a_b, lru_wa, lru_ba, lru_wx, lru_bx, lru_lambda, hg_lb_logits, hg_norm_g, w_branch_a, w_branch_b, w_out, norm_post_mix, norm_pre_ffn, w_up, conv_f_w, conv_f_b, w_down, norm_post_ffn, loss_target, m_norm_pre_mix, m_w_in, m_conv_a_w, m_conv_a_b, m_lru_wa, m_lru_ba, m_lru_wx, m_lru_bx, m_lru_lambda, m_hg_lb_logits, m_hg_norm_g, m_w_branch_a, m_w_branch_b, m_w_out, m_norm_post_mix, m_norm_pre_ffn, m_w_up, m_conv_f_w, m_conv_f_b, m_w_down, m_norm_post_ffn, v_norm_pre_mix, v_w_in, v_conv_a_w, v_conv_a_b, v_lru_wa, v_lru_ba, v_lru_wx, v_lru_bx, v_lru_lambda, v_hg_lb_logits, v_hg_norm_g, v_w_branch_a, v_w_branch_b, v_w_out, v_norm_post_mix, v_norm_pre_ffn, v_w_up, v_conv_f_w, v_conv_f_b, v_w_down, v_norm_post_ffn)
    w_in_args = dict(zip(WEIGHTS, rest[:NW]))
    loss_target = rest[NW]
    m_args = dict(zip(WEIGHTS, rest[NW + 1:2 * NW + 1]))
    v_args = dict(zip(WEIGHTS, rest[2 * NW + 1:3 * NW + 1]))
    shape_of = {n: w_in_args[n].shape for n in WEIGHTS}

    def two_d(n, a):
        if n in BIG:
            return a.reshape(BIG_SHAPE[n])
        if n in ("lru_wa", "lru_wx"):
            return a.reshape(NH, HD, HD)
        return a.reshape(a.shape[-2:])

    w2 = {n: two_d(n, w_in_args[n]) for n in WEIGHTS}
    m2 = {n: two_d(n, m_args[n]) for n in WEIGHTS}
    v2 = {n: two_d(n, v_args[n]) for n in WEIGHTS}

    cidx = lax.axis_index("c").astype(jnp.int32).reshape(1)
    jchip = 2 * lax.axis_index("x") + lax.axis_index("y")

    jc = jnp.stack([jchip, lax.axis_index("c")]).astype(jnp.int32)

    shards = {n: _place_shard(w2[n], jc, "place_" + n) for n in BIG}
    conv_a_s = jnp.pad(w2["conv_a_w"], ((0, 4), (0, 0)))
    conv_f_s = jnp.pad(w2["conv_f_w"], ((0, 5), (0, 0)))
    small = {n: w2[n] for n in WEIGHTS if n not in BIG and n not in ("conv_a_w", "conv_f_w")}

    grad_x, p_big, q_big, sm_g = _local_step(
        x[0], loss_target[0], shards["w_in"], [shards[n] for n in REST], conv_a_s, conv_f_s, small, jc, cidx)

    s1, s2 = _pack_small(sm_g)
    s3 = jnp.concatenate([sm_g["lru_wa"].reshape(D, HD), sm_g["lru_wx"].reshape(D, HD)], axis=0)
    _, (rs1, rs2, rs3) = _reduce_stage1({}, (), (s1, s2, s3), "reduce_d2d_in_small")
    ps1, ps2, ps3 = _add(s1, rs1, "add_s1"), _add(s2, rs2, "add_s2"), _add(s3, rs3, "add_s3")
    qs1, qs2, qs3 = _reduce_stage2(ps1, ps2, ps3)
    f_big = {n: _sum_chips(q_big[n], p_big[n], jc, "sum_chips_" + n) for n in BIG}
    fs1 = _sum_chips(qs1, ps1, jc, "sum_chips_s1")
    fs2 = _sum_chips(qs2, ps2, jc, "sum_chips_s2", by_cols=True)
    fs3 = _sum_chips(qs3, ps3, jc, "sum_chips_s3")
    g_big, gs1, gs2, gs3 = _reduce_stage3(f_big, fs1, fs2, fs3)

    res = {}
    for n in BIG:
        d, mn, vn = _adam(w2[n], g_big[n], m2[n], v2[n], "adam_" + n)
        res[n] = (g_big[n], d, mn, vn)
    small_res = _adam_small(gs1, gs2, gs3.reshape(2, NH, HD, HD), w2, m2, v2)
    for n, r0, nr in VEC_ROWS:
        res[n] = (gs1[r0:r0 + nr],) + small_res[n]
    res["conv_f_b"] = (gs2[0:1],) + small_res["conv_f_b"]
    res["lru_wa"] = (gs3[0:D].reshape(NH, HD, HD),) + small_res["lru_wa"]
    res["lru_wx"] = (gs3[D:2 * D].reshape(NH, HD, HD),) + small_res["lru_wx"]
    g_ca = lax.dynamic_slice_in_dim(gs1[ROW_CONV_A:ROW_CONV_A + 4], jchip * (D // NCHIP), D // NCHIP, axis=1)
    g_cf = lax.dynamic_slice_in_dim(gs2[1:4], jchip * SH_UP, SH_UP, axis=1)
    res["conv_a_w"] = (g_ca,) + _adam(w2["conv_a_w"], g_ca, m2["conv_a_w"], v2["conv_a_w"], "adam_conv_a_w")
    res["conv_f_w"] = (g_cf,) + _adam(w2["conv_f_w"], g_cf, m2["conv_f_w"], v2["conv_f_w"], "adam_conv_f_w")

    loss = (0.5 / D) * jnp.sum(gs1[ROW_LOSS])
    out = [loss, grad_x.reshape(x.shape)]
    for part in range(4):
        out += [res[n][part].reshape(shape_of[n]) for n in WEIGHTS]
    return tuple(out)
```

```python
import functools

import jax
import jax.numpy as jnp
from jax import lax
from jax.experimental import pallas as pl
from jax.experimental.pallas import tpu as pltpu

F32 = jnp.float32
BF16 = jnp.bfloat16

D = 1024
NH = 8
HD = 128
CH = 32
DFF = 2816
DUP = 2 * DFF
NCHIP = 4
SH_IN = 2 * D
SH_UP = DUP // NCHIP
SH_DN = DFF // NCHIP
SH_BR = D // NCHIP
EPS = 1e-6
LRU_C = 8.0
ADAM_LR = 0.001
ADAM_B1 = 0.9
ADAM_B2 = 0.999
ADAM_EPS = 1e-08
ADAM_WD = 0.01
ADAM_STEP = 10
VMEM_BIG = 56 * 1024 * 1024
MESH = pl.DeviceIdType.MESH

SLOT_A, SLOT_B, SLOT_C, SLOT_G = 2, 0, 1, 3


def _slot_of_chip(s):
    return jnp.where(s == 3, 3, (s + 2) % 3)


def _params(sem, vmem=None):
    return pltpu.CompilerParams(dimension_semantics=sem, vmem_limit_bytes=vmem)


_GC = 0.7978845608028654
_GA = 0.044715


def _gelu(x):
    return 0.5 * x * (1.0 + jnp.tanh(_GC * (x + _GA * x * x * x)))


def _gelu_and_grad(x):
    x2 = x * x
    th = jnp.tanh(_GC * x * (1.0 + _GA * x2))
    g = 0.5 * x * (1.0 + th)
    dg = 0.5 * (1.0 + th) + 0.5 * x * (1.0 - th * th) * _GC * (1.0 + 3.0 * _GA * x2)
    return g, dg


def _sig(x):
    return jax.nn.sigmoid(x)


def _dot(a, b):
    return jnp.dot(a, b, preferred_element_type=F32)


def _dot_nt(a, b):
    return lax.dot_general(a, b, (((1,), (1,)), ((), ())), preferred_element_type=F32)


def _dot_tn(a, b):
    return lax.dot_general(a, b, (((0,), (0,)), ((), ())), preferred_element_type=F32)


def _chunk_cumsum(x):
    pos = lax.broadcasted_iota(jnp.int32, (x.shape[0], 1), 0) & (CH - 1)
    d = 1
    while d < CH:
        x = x + jnp.where(pos >= d, pltpu.roll(x, d, 0), 0.0)
        d *= 2
    return x


def _chunk_revcumsum(x):
    n = x.shape[0]
    pos = lax.broadcasted_iota(jnp.int32, (n, 1), 0) & (CH - 1)
    d = 1
    while d < CH:
        x = x + jnp.where(pos < CH - d, pltpu.roll(x, n - d, 0), 0.0)
        d *= 2
    return x


def _chunk_last(x):
    n = x.shape[0]
    return jnp.concatenate(
        [jnp.broadcast_to(x[c * CH + CH - 1:c * CH + CH, :], (CH, x.shape[1])) for c in range(n // CH)], axis=0)


def _chunk_total(x):
    n = x.shape[0]
    return jnp.concatenate(
        [jnp.broadcast_to(jnp.sum(x[c * CH:(c + 1) * CH, :], axis=0, keepdims=True), (CH, x.shape[1]))
         for c in range(n // CH)], axis=0)


def _rms_stats(x):
    r = lax.rsqrt(jnp.mean(x * x, axis=-1, keepdims=True) + EPS)
    return r, x * r


def _rms_bwd(gd, n, r):
    return r * (gd - n * jnp.mean(gd * n, axis=-1, keepdims=True))


def _shift_rows(x, d, fill):
    rows = lax.broadcasted_iota(jnp.int32, (x.shape[0], 1), 0)
    return jnp.where(rows >= d, pltpu.roll(x, d, 0), fill)


def _scan_down(a, u, carry):
    n = a.shape[0]
    pos = lax.broadcasted_iota(jnp.int32, (n, 1), 0) & 7
    for d in (1, 2, 4):
        u = a * jnp.where(pos >= d, pltpu.roll(u, d, 0), 0.0) + u
        a = a * jnp.where(pos >= d, pltpu.roll(a, d, 0), 1.0)
    out = []
    for v in range(n // 8):
        h = a[v * 8:v * 8 + 8, :] * carry + u[v * 8:v * 8 + 8, :]
        carry = h[7:8, :]
        out.append(h)
    return jnp.concatenate(out, axis=0)


def _scan_up(b, g, carry):
    n = b.shape[0]
    pos = lax.broadcasted_iota(jnp.int32, (n, 1), 0) & 7
    for d in (1, 2, 4):
        g = g + b * jnp.where(pos < 8 - d, pltpu.roll(g, n - d, 0), 0.0)
        b = b * jnp.where(pos < 8 - d, pltpu.roll(b, n - d, 0), 1.0)
    out = [None] * (n // 8)
    for v in reversed(range(n // 8)):
        h = g[v * 8:v * 8 + 8, :] + b[v * 8:v * 8 + 8, :] * carry
        carry = h[0:1, :]
        out[v] = h
    return jnp.concatenate(out, axis=0)


def _shift_rows_up(x, d, fill):
    n = x.shape[0]
    rows = lax.broadcasted_iota(jnp.int32, (n, 1), 0)
    return jnp.where(rows < n - d, pltpu.roll(x, n - d, 0), fill)


def _mm_nn_sharded(a, b3, out_dtype, tm, name, slot_fn=None):
    m, k = a.shape
    s, _, ns = b3.shape

    def body(a_ref, b_ref, o_ref):
        o_ref[...] = _dot(a_ref[...], b_ref[...]).astype(out_dtype)

    if slot_fn is None:
        out_shape = jax.ShapeDtypeStruct((m, s * ns), out_dtype)
        out_spec = pl.BlockSpec((tm, ns), lambda j, i: (i, j))
    else:
        out_shape = jax.ShapeDtypeStruct((s, m, ns), out_dtype)
        out_spec = pl.BlockSpec((None, tm, ns), lambda j, i: (slot_fn(j), i, 0))
    return pl.pallas_call(
        body, name=name, out_shape=out_shape, grid=(s, m // tm),
        in_specs=[pl.BlockSpec((tm, k), lambda j, i: (i, 0)),
                  pl.BlockSpec((None, k, ns), lambda j, i: (j, 0, 0))],
        out_specs=out_spec,
        compiler_params=_params(("parallel", "parallel"), VMEM_BIG),
    )(a, b3)


def _peer_of_step(s):
    return ((s & 1) << 1) | (s >> 1)


def _mm_in_gather(a, stk_w_in, stk_rest, names_rest, conv_a_s, conv_f_s, jc, tm):
    m, k = a.shape
    s_n, _, ns = stk_w_in.shape
    nr = len(stk_rest)
    mt = m // tm

    def body(jc_ref, a_ref, w_in_any, *rest):
        del w_in_any
        ca_src, cf_src = rest[nr], rest[nr + 1]
        o_ref, w_full = rest[nr + 2], rest[nr + 3]
        stk = rest[nr + 4:2 * nr + 4]
        ca_dst, cf_dst = rest[2 * nr + 4], rest[2 * nr + 5]
        wbuf, ssem_w, rsem_w, fssem_w, frsem_w, ssem_r, rsem_r, csend, crecv, lsem, wsem = rest[2 * nr + 6:]
        s, i = pl.program_id(0), pl.program_id(1)
        x, y, c, j, chips = _place()
        sends_w, arrive_w, fwds_w, farrive_w = _gather_copies([w_full], ("w_in",), ssem_w, rsem_w, fssem_w, frsem_w)
        sends_r, arrive_r, _, _ = _gather_copies(stk, names_rest, ssem_r, rsem_r)
        conv = ((ca_src, ca_dst), (cf_src, cf_dst))
        locs = [pltpu.make_async_copy(src, dst.at[j], lsem.at[n]) for n, (src, dst) in enumerate(conv)]
        csends = [_remote(src, dst.at[j], csend.at[3 * n + kk], crecv.at[3 * n + kk], (cx, cy, c))
                  for n, (src, dst) in enumerate(conv) for kk, (cx, cy) in enumerate(chips)]

        def fetch(step):
            return pltpu.make_async_copy(w_full.at[j ^ _peer_of_step(step)], wbuf.at[step & 1], wsem.at[step & 1])

        @pl.when((s == 0) & (i == 0))
        def _():
            for cp in sends_w + locs + csends + sends_r:
                cp.start()
            fetch(s).start()

        @pl.when(i == 0)
        def _():
            fetch(s).wait()

        o_ref[...] = _dot(a_ref[...], wbuf[s & 1])

        @pl.when((s == 0) & (i == mt - 1))
        def _():
            for kk in (0, 1):
                arrive_w[kk].wait_recv()
                fwds_w[kk].start()

        @pl.when((s == 1) & (i == mt - 1))
        def _():
            arrive_w[2].wait_recv()
            fwds_w[2].start()

        for kk in range(3):
            @pl.when((s == kk) & (i == mt - 1))
            def _(kk=kk):
                farrive_w[kk].wait_recv()
                fetch(s + 1).start()

        @pl.when((s == s_n - 1) & (i == mt - 1))
        def _():
            for cp in arrive_r:
                cp.wait_recv()
            for n, (_, dst) in enumerate(conv):
                for kk, (cx, cy) in enumerate(chips):
                    got = dst.at[2 * cx + cy]
                    _remote(got, got, csend.at[3 * n + kk], crecv.at[3 * n + kk], (cx, cy, c)).wait_recv()
            for cp in sends_w + fwds_w + sends_r + csends:
                cp.wait_send()
            for cp in locs:
                cp.wait()

    any_spec = pl.BlockSpec(memory_space=pl.ANY)
    sem = pltpu.SemaphoreType.DMA
    grid_spec = pltpu.PrefetchScalarGridSpec(
        num_scalar_prefetch=1, grid=(s_n, mt),
        in_specs=[pl.BlockSpec((tm, k), lambda s, i, jc_ref: (i, 0))] + [any_spec] * (nr + 3),
        out_specs=(pl.BlockSpec((None, tm, ns),
                                lambda s, i, jc_ref: (_slot_of_chip(jc_ref[0] ^ _peer_of_step(s)), i, 0)),)
        + (any_spec,) * (nr + 3),
        scratch_shapes=[pltpu.VMEM((2, k, ns), BF16), sem((3,)), sem((3,)), sem((3,)), sem((3,)),
                        sem((max(3 * nr, 1),)), sem((max(3 * nr, 1),)), sem((6,)), sem((6,)), sem((2,)), sem((2,))])
    outs = pl.pallas_call(
        body, name="mm_in", grid_spec=grid_spec,
        out_shape=(jax.ShapeDtypeStruct((s_n, m, ns), F32), jax.ShapeDtypeStruct(stk_w_in.shape, BF16))
        + tuple(jax.ShapeDtypeStruct(v.shape, v.dtype) for v in stk_rest)
        + tuple(jax.ShapeDtypeStruct((NCHIP,) + v.shape, v.dtype) for v in (conv_a_s, conv_f_s)),
        input_output_aliases={2 + w: 1 + w for w in range(nr + 1)},
        compiler_params=pltpu.CompilerParams(dimension_semantics=("arbitrary", "arbitrary"),
                                             vmem_limit_bytes=VMEM_BIG, has_side_effects=True),
    )(jc, a, stk_w_in, *stk_rest, conv_a_s, conv_f_s)
    return outs[0], outs[1], list(outs[2:2 + nr]), outs[2 + nr], outs[3 + nr]


def _mm_dh1_exchange(dp4, w_in3, p_w_in, x, dx1, gain, tm):
    s, k, ns = w_in3.shape
    m = dp4.shape[1]
    mt = m // tm

    def body(a_ref, b_ref, p_ref, x_ref, dx1_ref, g_ref, o_ref, q_ref, dg_ref, ssem, rsem):
        i, j = pl.program_id(0), pl.program_id(1)
        sends, arrive = _exchange_copies([q_ref], [lambda chip: p_ref.at[chip]], ssem, rsem)

        @pl.when((i == 0) & (j == 0))
        def _():
            for cp in sends:
                cp.start()

        @pl.when(j == 0)
        def _():
            o_ref[...] = _dot_nt(a_ref[...], b_ref[...])

        @pl.when(j > 0)
        def _():
            o_ref[...] += _dot_nt(a_ref[...], b_ref[...])

        @pl.when(j == s - 1)
        def _():
            dh = o_ref[...]
            r, n = _rms_stats(x_ref[...])
            o_ref[...] = dx1_ref[...] + _rms_bwd(dh * g_ref[...], n, r)
            dgv = jnp.sum(dh * n, axis=0, keepdims=True)

            @pl.when(i == 0)
            def _():
                dg_ref[...] = dgv

            @pl.when(i > 0)
            def _():
                dg_ref[...] += dgv

        @pl.when((i == mt - 1) & (j == s - 1))
        def _():
            for cp in arrive:
                cp.wait_recv()
            for cp in sends:
                cp.wait_send()

    any_spec = pl.BlockSpec(memory_space=pl.ANY)
    row_tile = pl.BlockSpec((tm, k), lambda i, j: (i, 0))
    vec = pl.BlockSpec((1, k), lambda i, j: (0, 0))
    return pl.pallas_call(
        body, name="mm_dh1",
        out_shape=(jax.ShapeDtypeStruct((m, k), F32), jax.ShapeDtypeStruct(p_w_in.shape, BF16),
                   jax.ShapeDtypeStruct((1, k), F32)),
        grid=(mt, s),
        in_specs=[pl.BlockSpec((None, tm, ns), lambda i, j: (_slot_of_chip(j), i, 0)),
                  pl.BlockSpec((None, k, ns), lambda i, j: (j, 0, 0)), any_spec, row_tile, row_tile, vec],
        out_specs=(row_tile, any_spec, vec),
        scratch_shapes=[pltpu.SemaphoreType.DMA((3,)), pltpu.SemaphoreType.DMA((3,))],
        compiler_params=pltpu.CompilerParams(dimension_semantics=("arbitrary", "arbitrary"),
                                             vmem_limit_bytes=VMEM_BIG, has_side_effects=True),
    )(dp4, w_in3, p_w_in, x, dx1, gain)


def _mm_nt_sharded(a, b3, tm, name, stacked_slot_fn=None):
    s, k, ns = b3.shape
    m = a.shape[1] if stacked_slot_fn is not None else a.shape[0]

    def body(a_ref, b_ref, o_ref):
        j = pl.program_id(1)
        @pl.when(j == 0)
        def _():
            o_ref[...] = _dot_nt(a_ref[...], b_ref[...])

        @pl.when(j > 0)
        def _():
            o_ref[...] += _dot_nt(a_ref[...], b_ref[...])

    if stacked_slot_fn is None:
        a_spec = pl.BlockSpec((tm, ns), lambda i, j: (i, j))
    else:
        a_spec = pl.BlockSpec((None, tm, ns), lambda i, j: (stacked_slot_fn(j), i, 0))
    return pl.pallas_call(
        body, name=name, out_shape=jax.ShapeDtypeStruct((m, k), F32), grid=(m // tm, s),
        in_specs=[a_spec, pl.BlockSpec((None, k, ns), lambda i, j: (j, 0, 0))],
        out_specs=pl.BlockSpec((tm, k), lambda i, j: (i, 0)),
        compiler_params=_params(("parallel", "arbitrary"), VMEM_BIG),
    )(a, b3)


def _mm_nt(a, b, out_dtype, tm, name):
    m, k = a.shape
    n = b.shape[0]

    def body(a_ref, b_ref, o_ref):
        o_ref[...] = _dot_nt(a_ref[...], b_ref[...]).astype(out_dtype)

    return pl.pallas_call(
        body, name=name, out_shape=jax.ShapeDtypeStruct((m, n), out_dtype), grid=(m // tm,),
        in_specs=[pl.BlockSpec((tm, k), lambda i: (i, 0)), pl.BlockSpec((n, k), lambda i: (0, 0))],
        out_specs=pl.BlockSpec((tm, n), lambda i: (i, 0)),
        compiler_params=_params(("parallel",), VMEM_BIG),
    )(a, b)


def _mm_tn(a, g, tkk, tn, tk, name, stacked_slot_fn=None, stacked_out=False):
    m, k = a.shape
    if stacked_slot_fn is not None:
        n = g.shape[0] * g.shape[2]
        g_spec = pl.BlockSpec((None, tk, tn), lambda kk, j, mm: (stacked_slot_fn(j), mm, 0))
    else:
        n = g.shape[1]
        g_spec = pl.BlockSpec((tk, tn), lambda kk, j, mm: (mm, j))
    steps = m // tk

    def body(a_ref, g_ref, o_ref, acc_ref):
        mm = pl.program_id(2)

        @pl.when(mm == 0)
        def _():
            acc_ref[...] = _dot_tn(a_ref[...], g_ref[...])

        @pl.when(mm > 0)
        def _():
            acc_ref[...] += _dot_tn(a_ref[...], g_ref[...])

        @pl.when(mm == steps - 1)
        def _():
            o_ref[...] = acc_ref[...].astype(BF16)

    if stacked_out:
        out_shape = jax.ShapeDtypeStruct((n // tn, k, tn), BF16)
        out_spec = pl.BlockSpec((None, tkk, tn), lambda kk, j, mm: (j, kk, 0))
    else:
        out_shape = jax.ShapeDtypeStruct((k, n), BF16)
        out_spec = pl.BlockSpec((tkk, tn), lambda kk, j, mm: (kk, j))
    return pl.pallas_call(
        body, name=name, out_shape=out_shape, grid=(k // tkk, n // tn, steps),
        in_specs=[pl.BlockSpec((tk, tkk), lambda kk, j, mm: (mm, kk)), g_spec],
        out_specs=out_spec,
        scratch_shapes=[pltpu.VMEM((tkk, tn), F32)],
        compiler_params=_params(("parallel", "parallel", "arbitrary"), VMEM_BIG),
    )(a, g)


def _norm_fwd(x, gain, tt):
    t = x.shape[0]

    def body(x_ref, g_ref, h_ref):
        _, n = _rms_stats(x_ref[...])
        h_ref[...] = (n * g_ref[...]).astype(BF16)

    return pl.pallas_call(
        body, name="norm_fwd", out_shape=jax.ShapeDtypeStruct((t, D), BF16), grid=(t // tt,),
        in_specs=[pl.BlockSpec((tt, D), lambda i: (i, 0)), pl.BlockSpec((1, D), lambda i: (0, 0))],
        out_specs=pl.BlockSpec((tt, D), lambda i: (i, 0)),
        compiler_params=_params(("parallel",)),
    )(x, gain)


def _lru_gates(xc, wa_ref, wx_ref, ba, bx, lam):
    xcb = xc.astype(BF16)
    ra = jnp.concatenate([_dot(xcb[:, n * HD:(n + 1) * HD], wa_ref[n]) for n in range(NH)], axis=1) + ba
    ix = jnp.concatenate([_dot(xcb[:, n * HD:(n + 1) * HD], wx_ref[n]) for n in range(NH)], axis=1) + bx
    r = _sig(ra)
    ig = _sig(ix)
    z = -lam
    sp = jnp.maximum(z, 0.0) + jnp.log1p(jnp.exp(-jnp.abs(z)))
    log_a = -LRU_C * r * sp
    a = jnp.exp(log_a)
    z2 = 2.0 * log_a
    series = -z2 * (1.0 + z2 * (0.5 + z2 * (1.0 / 6.0 + z2 * (1.0 / 24.0))))
    om = jnp.where(z2 > -0.02, series, 1.0 - jnp.exp(z2))
    mult = jnp.sqrt(om)
    return xcb, r, ig, sp, a, mult


def _mixer_a_fwd(p4, cw, cb, wa, wx, ba, bx, lam, stk, names, tt):
    t = p4.shape[1]
    ng = len(stk)

    def body(p_ref, cw_ref, cb_ref, wa_ref, wx_ref, ba_ref, bx_ref, lam_ref, *rest):
        ya_ref, h_ref, sv_ref = rest[ng:ng + 3]
        halo, hc, ssem, rsem = rest[2 * ng + 3:]
        i = pl.program_id(0)
        finish = _ici_leg_behind(rest[ng + 3:2 * ng + 3], names, ssem, rsem, i == 0, lambda: i == t // tt - 1)

        @pl.when(i == 0)
        def _():
            halo[...] = jnp.zeros((8, D), F32)
            hc[...] = jnp.zeros((1, D), F32)

        xa = p_ref[:, 0:D]
        ga = p_ref[:, D:2 * D]
        xe = jnp.concatenate([halo[...], xa], axis=0)
        xc = (cb_ref[...] + cw_ref[3:4, :] * xe
              + sum(cw_ref[3 - s:4 - s, :] * pltpu.roll(xe, s, 0) for s in (1, 2, 3)))[8:, :]
        halo[...] = xa[tt - 8:, :]
        _, r, ig, _, a, mult = _lru_gates(xc, wa_ref, wx_ref, ba_ref[...], bx_ref[...], lam_ref[...])
        u = mult * ig * xc
        h = _scan_down(a, u, hc[...])
        hc[...] = h[tt - 1:tt, :]
        h_ref[...] = h
        ya_ref[...] = (h * _gelu(ga)).astype(BF16)
        for idx, val in enumerate((xc, r, ig, a, mult)):
            sv_ref[idx] = val
        finish()

    full = lambda shape: pl.BlockSpec(shape, lambda i: (0,) * len(shape))
    any_spec = pl.BlockSpec(memory_space=pl.ANY)
    outs = pl.pallas_call(
        body, name="mixer_a_fwd",
        out_shape=(jax.ShapeDtypeStruct((t, D), BF16), jax.ShapeDtypeStruct((t, D), F32),
                   jax.ShapeDtypeStruct((5, t, D), F32)) + tuple(jax.ShapeDtypeStruct(v.shape, v.dtype) for v in stk),
        grid=(t // tt,),
        in_specs=[pl.BlockSpec((None, tt, 2 * D), lambda i: (SLOT_A, i, 0)),
                  full((4, D)), full((1, D)), full((NH, HD, HD)), full((NH, HD, HD)),
                  full((1, D)), full((1, D)), full((1, D))] + [any_spec] * ng,
        out_specs=(pl.BlockSpec((tt, D), lambda i: (i, 0)), pl.BlockSpec((tt, D), lambda i: (i, 0)),
                   pl.BlockSpec((5, tt, D), lambda i: (0, i, 0))) + (any_spec,) * ng,
        scratch_shapes=[pltpu.VMEM((8, D), F32), pltpu.VMEM((1, D), F32),
                        pltpu.SemaphoreType.DMA((3 * ng,)), pltpu.SemaphoreType.DMA((3 * ng,))],
        input_output_aliases={8 + w: 3 + w for w in range(ng)},
        compiler_params=pltpu.CompilerParams(dimension_semantics=("arbitrary",), vmem_limit_bytes=VMEM_BIG,
                                             has_side_effects=True),
    )(p4, cw, cb, wa, wx, ba, bx, lam, *stk)
    return outs[0], outs[1], outs[2], list(outs[3:])


def _chunk_masks(tt):
    row = lax.broadcasted_iota(jnp.int32, (tt, tt), 0)
    col = lax.broadcasted_iota(jnp.int32, (tt, tt), 1)
    same = jnp.right_shift(row, 5) == jnp.right_shift(col, 5)
    return same & (col <= row)


def _hg_head_fwd(q, fz, lbh, saved=None):
    sgn = _sig(-fz)
    k = (1.0 - lbh) * sgn
    if saved is None:
        sg = _sig(fz)
        f = lbh + (1.0 - lbh) * sg
        g = _chunk_cumsum(jnp.log(f))
    else:
        sg, g = saved
        f = lbh + (1.0 - lbh) * sg
    gu = _chunk_last(g) - g
    eg = jnp.exp(g)
    eng = jnp.exp(-g)
    egu = jnp.exp(gu)
    qt = q * eg
    kt = k * eng
    kd = k * egu
    return sg, sgn, f, k, g, eg, eng, egu, qt, kt, kd


def _lb_of(logits_ref):
    return _sig(logits_ref[0:1, :] - logits_ref[1:2, :])


def _hgrn2_fwd(p4, logits, gnorm, stk, names, tt):
    t = p4.shape[1]
    nc = tt // CH
    ng = len(stk)

    def body(p_ref, lg_ref, gn_ref, *rest):
        yb_ref, o_ref, ss_ref, sv_ref = rest[ng:ng + 4]
        st, ssem, rsem = rest[2 * ng + 4:]
        i = pl.program_id(0)
        finish = _ici_leg_behind(rest[ng + 4:2 * ng + 4], names, ssem, rsem, i == 0, lambda: i == t // tt - 1)

        @pl.when(i == 0)
        def _():
            st[...] = jnp.zeros((NH, HD, HD), F32)

        low = _chunk_masks(tt)
        lb = _lb_of(lg_ref)
        heads = [slice(h * HD, (h + 1) * HD) for h in range(NH)]
        sg, _, _, _, g, _, _, _, qt, kt, kd = _hg_head_fwd(p_ref[0, :, 0:D], p_ref[0, :, D:2 * D], lb)
        sv_ref[0] = sg
        sv_ref[1] = g
        qtb, ktb, kdb, vb = qt.astype(BF16), kt.astype(BF16), kd.astype(BF16), p_ref[1, :, 0:D].astype(BF16)
        decs = [jnp.exp(g[c * CH + CH - 1:c * CH + CH, :]) for c in range(nc)]
        o_in = []
        for hs in heads:
            att = jnp.where(low, _dot_nt(qtb[:, hs], ktb[:, hs]), 0.0)
            o_in.append(_dot(att.astype(BF16), vb[:, hs]))
        s_t = [st[h] for h in range(NH)]
        pieces = [[None] * nc for _ in range(NH)]
        for c in range(nc):
            sl = slice(c * CH, (c + 1) * CH)
            for h, hs in enumerate(heads):
                s_bf = s_t[h].astype(BF16)
                ss_ref[c, h] = s_bf
                pieces[h][c] = o_in[h][sl] + _dot_nt(qtb[sl, hs], s_bf)
                s_t[h] = s_t[h] * decs[c][:, hs] + _dot_tn(vb[sl, hs], kdb[sl, hs])
        for h, hs in enumerate(heads):
            st[h] = s_t[h]
            o = jnp.concatenate(pieces[h], axis=0)
            _, n = _rms_stats(o)
            og = p_ref[1, :, D + h * HD:D + (h + 1) * HD]
            o_ref[:, hs] = o
            yb_ref[:, hs] = (n * gn_ref[:, hs] * (og * _sig(og))).astype(BF16)
        finish()

    any_spec = pl.BlockSpec(memory_space=pl.ANY)
    outs = pl.pallas_call(
        body, name="hgrn2_fwd",
        out_shape=(jax.ShapeDtypeStruct((t, D), BF16), jax.ShapeDtypeStruct((t, D), F32),
                   jax.ShapeDtypeStruct((t // CH, NH, HD, HD), BF16), jax.ShapeDtypeStruct((2, t, D), F32))
        + tuple(jax.ShapeDtypeStruct(v.shape, v.dtype) for v in stk),
        grid=(t // tt,),
        in_specs=[pl.BlockSpec((2, tt, 2 * D), lambda i: (0, i, 0)),
                  pl.BlockSpec((2, D), lambda i: (0, 0)), pl.BlockSpec((1, D), lambda i: (0, 0))] + [any_spec] * ng,
        out_specs=(pl.BlockSpec((tt, D), lambda i: (i, 0)), pl.BlockSpec((tt, D), lambda i: (i, 0)),
                   pl.BlockSpec((nc, NH, HD, HD), lambda i: (i, 0, 0, 0)),
                   pl.BlockSpec((2, tt, D), lambda i: (0, i, 0))) + (any_spec,) * ng,
        scratch_shapes=[pltpu.VMEM((NH, HD, HD), F32), pltpu.SemaphoreType.DMA((3 * ng,)),
                        pltpu.SemaphoreType.DMA((3 * ng,))],
        input_output_aliases={3 + w: 4 + w for w in range(ng)},
        compiler_params=pltpu.CompilerParams(dimension_semantics=("arbitrary",), vmem_limit_bytes=VMEM_BIG,
                                             has_side_effects=True),
    )(p4, logits, gnorm, *stk)
    return outs[0], outs[1], outs[2], outs[3], list(outs[4:])


def _mid_fwd(ya, yb, p4, x, wa, wb, wo, g_pm, g_pf, tt):
    t = x.shape[0]

    def body(ya_ref, yb_ref, gt_ref, x_ref, wa_ref, wb_ref, wo_ref, gpm_ref, gpf_ref,
             za_ref, zb_ref, mix_ref, m2_ref, x1_ref, h2_ref):
        za = _dot(ya_ref[...], wa_ref[...])
        zb = _dot(yb_ref[...], wb_ref[...])
        mix = _sig(gt_ref[:, 0:D]) * za + _sig(gt_ref[:, D:2 * D]) * zb
        mixb = mix.astype(BF16)
        m2 = _dot(mixb, wo_ref[...])
        _, n2 = _rms_stats(m2)
        x1 = x_ref[...] + n2 * gpm_ref[...]
        _, n1 = _rms_stats(x1)
        za_ref[...] = za.astype(BF16)
        zb_ref[...] = zb.astype(BF16)
        mix_ref[...] = mixb
        m2_ref[...] = m2
        x1_ref[...] = x1
        h2_ref[...] = (n1 * gpf_ref[...]).astype(BF16)

    row = lambda dt: jax.ShapeDtypeStruct((t, D), dt)
    tile = pl.BlockSpec((tt, D), lambda i: (i, 0))
    wsp = pl.BlockSpec((D, D), lambda i: (0, 0))
    vec = pl.BlockSpec((1, D), lambda i: (0, 0))
    return pl.pallas_call(
        body, name="mid_fwd",
        out_shape=(row(BF16), row(BF16), row(BF16), row(F32), row(F32), row(BF16)),
        grid=(t // tt,),
        in_specs=[tile, tile, pl.BlockSpec((None, tt, 2 * D), lambda i: (SLOT_G, i, 0)), tile,
                  wsp, wsp, wsp, vec, vec],
        out_specs=(tile,) * 6,
        compiler_params=_params(("parallel",), VMEM_BIG),
    )(ya, yb, p4, x, wa, wb, wo, g_pm, g_pf)


def _up_act_fwd(h2, w_up4, cfw, cfb, tm):
    t = h2.shape[0]
    ns = SH_UP

    def body(a_ref, ah_ref, wg_ref, wv_ref, cwg_ref, cwv_ref, cbg_ref, cbv_ref,
             pg_ref, pv_ref, y_ref, uv_ref, gl_ref, dgl_ref):
        i = pl.program_id(1)
        rows = jnp.concatenate([ah_ref[...], a_ref[...]], axis=0)
        ups = []
        for w_ref, cw_ref, cb_ref, pre_ref in ((wg_ref, cwg_ref, cbg_ref, pg_ref), (wv_ref, cwv_ref, cbv_ref, pv_ref)):
            pre = _dot(rows, w_ref[...])
            pre_ref[...] = pre[16:, :]
            xe = jnp.concatenate([jnp.where(i > 0, pre[8:16, :], 0.0), pre[16:, :]], axis=0)
            up = (cb_ref[...] + cw_ref[2:3, :] * xe + cw_ref[1:2, :] * pltpu.roll(xe, 1, 0)
                  + cw_ref[0:1, :] * pltpu.roll(xe, 2, 0))
            ups.append(up[8:, :])
        gl, dgl = _gelu_and_grad(ups[0])
        y_ref[...] = (gl * ups[1]).astype(BF16)
        uv_ref[...] = ups[1].astype(BF16)
        gl_ref[...] = gl.astype(BF16)
        dgl_ref[...] = dgl.astype(BF16)

    hb = tm // 16
    tile = pl.BlockSpec((tm, ns), lambda p, i: (i, p))
    return pl.pallas_call(
        body, name="up_act_fwd",
        out_shape=(jax.ShapeDtypeStruct((t, DFF), F32),) * 2 + (jax.ShapeDtypeStruct((t, DFF), BF16),) * 4,
        grid=(2, t // tm),
        in_specs=[pl.BlockSpec((tm, D), lambda p, i: (i, 0)),
                  pl.BlockSpec((16, D), lambda p, i: (jnp.maximum(i * hb - 1, 0), 0)),
                  pl.BlockSpec((None, D, ns), lambda p, i: (p, 0, 0)),
                  pl.BlockSpec((None, D, ns), lambda p, i: (p + 2, 0, 0)),
                  pl.BlockSpec((3, ns), lambda p, i: (0, p)), pl.BlockSpec((3, ns), lambda p, i: (0, p + 2)),
                  pl.BlockSpec((1, ns), lambda p, i: (0, p)), pl.BlockSpec((1, ns), lambda p, i: (0, p + 2))],
        out_specs=(tile,) * 6,
        compiler_params=_params(("parallel", "parallel"), VMEM_BIG),
    )(h2, h2, w_up4, w_up4, cfw, cfw, cfb, cfb)


def _down_loss(y, wdn, x1, tgt, g_post, tt):
    t = x1.shape[0]

    def body(y_ref, w_ref, x1_ref, t_ref, g_ref, dx2_ref, dm3_ref, lossv_ref, dg_ref):
        i = pl.program_id(0)
        m3 = _dot(y_ref[...], w_ref[...])
        r, n3 = _rms_stats(m3)
        g = g_ref[...]
        e = x1_ref[...] + n3 * g - t_ref[...]
        dx2 = e * (1.0 / D)
        dx2_ref[...] = dx2
        dm3_ref[...] = _rms_bwd(dx2 * g, n3, r).astype(BF16)
        lv = jnp.sum(e * e, axis=0, keepdims=True)
        dgv = jnp.sum(dx2 * n3, axis=0, keepdims=True)

        @pl.when(i == 0)
        def _():
            lossv_ref[...] = lv
            dg_ref[...] = dgv

        @pl.when(i > 0)
        def _():
            lossv_ref[...] += lv
            dg_ref[...] += dgv

    tile = pl.BlockSpec((tt, D), lambda i: (i, 0))
    vec = pl.BlockSpec((1, D), lambda i: (0, 0))
    return pl.pallas_call(
        body, name="down_loss",
        out_shape=(jax.ShapeDtypeStruct((t, D), F32), jax.ShapeDtypeStruct((t, D), BF16),
                   jax.ShapeDtypeStruct((1, D), F32), jax.ShapeDtypeStruct((1, D), F32)),
        grid=(t // tt,),
        in_specs=[pl.BlockSpec((tt, DFF), lambda i: (i, 0)), pl.BlockSpec((DFF, D), lambda i: (0, 0)),
                  tile, tile, vec],
        out_specs=(tile, tile, vec, vec),
        compiler_params=_params(("arbitrary",), VMEM_BIG),
    )(y, wdn, x1, tgt, g_post)


def _ffn_act_bwd(dy, pre_g, pre_v, uv, gl, dgl, cfw, tt):
    t = dy.shape[0]
    nt = t // tt

    def body(dy_ref, dyn_ref, pg_ref, pv_ref, uv_ref, uvn_ref, gl_ref, gln_ref, dgl_ref, dgln_ref, cw_ref,
             du_ref, dcw_ref, dcb_ref):
        i = pl.program_id(0)
        n = tt + 8
        next_live = jnp.where(i < nt - 1, 1.0, 0.0)
        ext = lambda ref, nref: jnp.concatenate([ref[...].astype(F32), nref[...].astype(F32)[0:8, :]], axis=0)
        dy = jnp.concatenate([dy_ref[...].astype(F32), dyn_ref[...].astype(F32)[0:8, :] * next_live], axis=0)
        ds = (dy * ext(uv_ref, uvn_ref) * ext(dgl_ref, dgln_ref), dy * ext(gl_ref, gln_ref))
        dcw_parts, dcb_parts = [], []
        for hh, c0 in enumerate((0, DFF)):
            cs = slice(c0, c0 + DFF)
            dd = ds[hh]
            d1 = pltpu.roll(dd, n - 1, 0)
            d2 = pltpu.roll(dd, n - 2, 0)
            du_ref[:, cs] = (cw_ref[2:3, cs] * dd + cw_ref[1:2, cs] * d1 + cw_ref[0:1, cs] * d2)[0:tt, :].astype(BF16)
            x = (pg_ref, pv_ref)[hh][...]
            dcw_parts.append(jnp.concatenate(
                [jnp.sum(dk[0:tt, :] * x, axis=0, keepdims=True) for dk in (d2, d1, dd)], axis=0))
            dcb_parts.append(jnp.sum(dd[0:tt, :], axis=0, keepdims=True))
        dcw = jnp.concatenate(dcw_parts, axis=1)
        dcb = jnp.concatenate(dcb_parts, axis=1)

        @pl.when(i == 0)
        def _():
            dcw_ref[...] = dcw
            dcb_ref[...] = dcb

        @pl.when(i > 0)
        def _():
            dcw_ref[...] += dcw
            dcb_ref[...] += dcb

    half = pl.BlockSpec((tt, DFF), lambda i: (i, 0))
    half_next = pl.BlockSpec((16, DFF), lambda i: (jnp.minimum((i + 1) * (tt // 16), t // 16 - 1), 0))
    return pl.pallas_call(
        body, name="ffn_act_bwd",
        out_shape=(jax.ShapeDtypeStruct((t, DUP), BF16), jax.ShapeDtypeStruct((3, DUP), F32),
                   jax.ShapeDtypeStruct((1, DUP), F32)),
        grid=(nt,),
        in_specs=[half, half_next, half, half,
                  half, half_next, half, half_next, half, half_next,
                  pl.BlockSpec((3, DUP), lambda i: (0, 0))],
        out_specs=(pl.BlockSpec((tt, DUP), lambda i: (i, 0)), pl.BlockSpec((3, DUP), lambda i: (0, 0)),
                   pl.BlockSpec((1, DUP), lambda i: (0, 0))),
        compiler_params=_params(("arbitrary",), VMEM_BIG),
    )(dy, dy, pre_g, pre_v, uv, uv, gl, gl, dgl, dgl, cfw)


def _mid_bwd(dh2, dx2, x1, m2, za, zb, p4, wa, wb, wo, g_pm, g_pf, tt):
    t = x1.shape[0]

    def body(dh2_ref, dx2_ref, x1_ref, m2_ref, za_ref, zb_ref, gt_ref, wa_ref, wb_ref, wo_ref, gpm_ref, gpf_ref,
             dx1_ref, dm2_ref, dza_ref, dzb_ref, dya_ref, dyb_ref, dp_ref, dgpm_ref, dgpf_ref):
        i = pl.program_id(0)
        r1, n1 = _rms_stats(x1_ref[...])
        dh2 = dh2_ref[...]
        dx1 = dx2_ref[...] + _rms_bwd(dh2 * gpf_ref[...], n1, r1)
        r2, n2 = _rms_stats(m2_ref[...])
        dm2 = _rms_bwd(dx1 * gpm_ref[...], n2, r2).astype(BF16)
        dmix = _dot_nt(dm2, wo_ref[...])
        sa = _sig(gt_ref[:, 0:D])
        sb = _sig(gt_ref[:, D:2 * D])
        dza = (dmix * sa).astype(BF16)
        dzb = (dmix * sb).astype(BF16)
        dp_ref[:, 0:D] = (dmix * za_ref[...].astype(F32) * sa * (1.0 - sa)).astype(BF16)
        dp_ref[:, D:2 * D] = (dmix * zb_ref[...].astype(F32) * sb * (1.0 - sb)).astype(BF16)
        dx1_ref[...] = dx1
        dm2_ref[...] = dm2
        dza_ref[...] = dza
        dzb_ref[...] = dzb
        dya_ref[...] = _dot_nt(dza, wa_ref[...]).astype(BF16)
        dyb_ref[...] = _dot_nt(dzb, wb_ref[...]).astype(BF16)
        dgpf = jnp.sum(dh2 * n1, axis=0, keepdims=True)
        dgpm = jnp.sum(dx1 * n2, axis=0, keepdims=True)

        @pl.when(i == 0)
        def _():
            dgpf_ref[...] = dgpf
            dgpm_ref[...] = dgpm

        @pl.when(i > 0)
        def _():
            dgpf_ref[...] += dgpf
            dgpm_ref[...] += dgpm

    row = lambda dt: jax.ShapeDtypeStruct((t, D), dt)
    tile = pl.BlockSpec((tt, D), lambda i: (i, 0))
    wsp = pl.BlockSpec((D, D), lambda i: (0, 0))
    vec = pl.BlockSpec((1, D), lambda i: (0, 0))
    gates = pl.BlockSpec((None, tt, 2 * D), lambda i: (SLOT_G, i, 0))
    return pl.pallas_call(
        body, name="mid_bwd",
        out_shape=(row(F32), row(BF16), row(BF16), row(BF16), row(BF16), row(BF16),
                   jax.ShapeDtypeStruct((NCHIP, t, 2 * D), BF16),
                   jax.ShapeDtypeStruct((1, D), F32), jax.ShapeDtypeStruct((1, D), F32)),
        grid=(t // tt,),
        in_specs=[tile, tile, tile, tile, tile, tile, gates, wsp, wsp, wsp, vec, vec],
        out_specs=(tile, tile, tile, tile, tile, tile, gates, vec, vec),
        compiler_params=_params(("arbitrary",), VMEM_BIG),
    )(dh2, dx2, x1, m2, za, zb, p4, wa, wb, wo, g_pm, g_pf)


def _hgrn2_bwd(p4, o_all, ss, saved, dyb, dp4, logits, gnorm, p_early, tt):
    t = p4.shape[1]
    nt = t // tt
    nc = tt // CH
    ne = len(p_early)

    def body(p_ref, o_ref, ss_ref, sv_ref, dyb_ref, dp_in, lg_ref, gn_ref, *rest):
        del dp_in
        pe = rest[:ne]
        dp_ref, dlb_ref, dgn_ref = rest[ne:ne + 3]
        qe = rest[ne + 3:2 * ne + 3]
        dst, ssem, rsem = rest[2 * ne + 3:]
        i = pl.program_id(0)
        sends, arrive = _exchange_copies(qe, [(lambda chip, r=r: r.at[chip]) for r in pe], ssem, rsem)

        @pl.when(i == 0)
        def _():
            dst[...] = jnp.zeros((NH, HD, HD), F32)
            for cp in sends:
                cp.start()

        low = _chunk_masks(tt)
        lb = _lb_of(lg_ref)
        heads = [slice(h * HD, (h + 1) * HD) for h in range(NH)]
        sg, sgn, f, k, g, eg, eng, egu, qt, kt, kd = _hg_head_fwd(p_ref[0, :, 0:D], p_ref[0, :, D:2 * D], lb,
                                                                  (sv_ref[0], sv_ref[1]))
        qtb, ktb, kdb, vb = qt.astype(BF16), kt.astype(BF16), kd.astype(BF16), p_ref[1, :, 0:D].astype(BF16)
        decs = [jnp.exp(g[c * CH + CH - 1:c * CH + CH, :]) for c in range(nc)]
        og = p_ref[1, :, D:2 * D]
        so = _sig(og)
        dyb = dyb_ref[...].astype(F32)
        dob = dyb * (og * so)
        rn = [_rms_stats(o_ref[:, hs]) for hs in heads]
        r_all = jnp.concatenate([jnp.broadcast_to(r, (tt, HD)) for r, _ in rn], axis=1)
        n_all = jnp.concatenate([n for _, n in rn], axis=1)
        gd = dob * gn_ref[...]
        proj = jnp.concatenate(
            [jnp.broadcast_to(jnp.mean(gd[:, hs] * n_all[:, hs], axis=-1, keepdims=True), (tt, HD)) for hs in heads],
            axis=1)
        dob_ = (r_all * (gd - n_all * proj)).astype(BF16)
        dog = dyb * (n_all * gn_ref[...]) * (so * (1.0 + og * (1.0 - so)))
        dgn = jnp.sum(dob * n_all, axis=0, keepdims=True)
        dv_in, dqt_in, dkt_h = [], [], []
        for hs in heads:
            att = jnp.where(low, _dot_nt(qtb[:, hs], ktb[:, hs]), 0.0).astype(BF16)
            d_att = jnp.where(low, _dot_nt(dob_[:, hs], vb[:, hs]), 0.0).astype(BF16)
            dv_in.append(_dot_tn(att, dob_[:, hs]))
            dqt_in.append(_dot(d_att, ktb[:, hs]))
            dkt_h.append(_dot_tn(d_att, qtb[:, hs]))
        ds_t = [dst[h] for h in range(NH)]
        dv_p = [[None] * NH for _ in range(nc)]
        dqt_p = [[None] * NH for _ in range(nc)]
        dkd_p = [[None] * NH for _ in range(nc)]
        dgl_p = [[None] * NH for _ in range(nc)]
        for c in reversed(range(nc)):
            sl = slice(c * CH, (c + 1) * CH)
            for h, hs in enumerate(heads):
                s_prev = ss_ref[c, h]
                ds_bf = ds_t[h].astype(BF16)
                dec = decs[c][:, hs]
                dv_p[c][h] = dv_in[h][sl] + _dot_nt(kdb[sl, hs], ds_bf)
                dqt_p[c][h] = dqt_in[h][sl] + _dot(dob_[sl, hs], s_prev)
                dkd_p[c][h] = _dot(vb[sl, hs], ds_bf)
                ddec = jnp.sum(s_prev.astype(F32) * ds_t[h], axis=0, keepdims=True)
                dgl_p[c][h] = jnp.broadcast_to(ddec * dec, (CH, HD))
                ds_t[h] = ds_t[h] * dec + _dot_tn(dob_[sl, hs], qtb[sl, hs])
        for h in range(NH):
            dst[h] = ds_t[h]
        whole = lambda parts: jnp.concatenate([jnp.concatenate(row, axis=1) for row in parts], axis=0)
        dv, dqt, dkd, dgl = whole(dv_p), whole(dqt_p), whole(dkd_p), whole(dgl_p)
        dkt = jnp.concatenate(dkt_h, axis=1)
        dq = dqt * eg
        dk = dkt * eng + dkd * egu
        dg = dqt * qt - dkt * kt
        dgu = dkd * kd
        dlogf = _chunk_revcumsum(dg - dgu) + _chunk_total(dgu) + dgl
        common = sgn * (dlogf / f - dk)
        dfz = (1.0 - lb) * sg * common
        dlb = jnp.sum(common, axis=0, keepdims=True)
        dp_ref[0, :, 0:D] = dq.astype(BF16)
        dp_ref[0, :, D:2 * D] = dfz.astype(BF16)
        dp_ref[1, :, 0:D] = dv.astype(BF16)
        dp_ref[1, :, D:2 * D] = dog.astype(BF16)

        @pl.when(i == 0)
        def _():
            dlb_ref[0:1, :] = dlb
            dgn_ref[...] = dgn

        @pl.when(i > 0)
        def _():
            dlb_ref[0:1, :] += dlb
            dgn_ref[...] += dgn

        @pl.when(i == nt - 1)
        def _():
            d0 = dlb_ref[0:1, :] * lb * (1.0 - lb)
            dlb_ref[0:1, :] = d0
            dlb_ref[1:2, :] = -d0
            for cp in arrive:
                cp.wait_recv()
            for cp in sends:
                cp.wait_send()

    rev = lambda i: nt - 1 - i
    vec = pl.BlockSpec((1, D), lambda i: (0, 0))
    any_spec = pl.BlockSpec(memory_space=pl.ANY)
    outs = pl.pallas_call(
        body, name="hgrn2_bwd",
        out_shape=(jax.ShapeDtypeStruct(dp4.shape, BF16), jax.ShapeDtypeStruct((2, D), F32),
                   jax.ShapeDtypeStruct((1, D), F32)) + tuple(jax.ShapeDtypeStruct(a.shape, BF16) for a in p_early),
        grid=(nt,),
        in_specs=[pl.BlockSpec((2, tt, 2 * D), lambda i: (0, rev(i), 0)),
                  pl.BlockSpec((tt, D), lambda i: (rev(i), 0)),
                  pl.BlockSpec((nc, NH, HD, HD), lambda i: (rev(i), 0, 0, 0)),
                  pl.BlockSpec((2, tt, D), lambda i: (0, rev(i), 0)),
                  pl.BlockSpec((tt, D), lambda i: (rev(i), 0)),
                  any_spec,
                  pl.BlockSpec((2, D), lambda i: (0, 0)), vec] + [any_spec] * ne,
        out_specs=(pl.BlockSpec((2, tt, 2 * D), lambda i: (0, rev(i), 0)),
                   pl.BlockSpec((2, D), lambda i: (0, 0)), vec) + (any_spec,) * ne,
        scratch_shapes=[pltpu.VMEM((NH, HD, HD), F32), pltpu.SemaphoreType.DMA((3 * ne,)),
                        pltpu.SemaphoreType.DMA((3 * ne,))],
        input_output_aliases={5: 0},
        compiler_params=pltpu.CompilerParams(dimension_semantics=("arbitrary",), vmem_limit_bytes=VMEM_BIG,
                                             has_side_effects=True),
    )(p4, o_all, ss, saved, dyb, dp4, logits, gnorm, *p_early)
    return outs[0], outs[1], outs[2], list(outs[3:])


def _mixer_a_bwd(p4, hseq, saved, dya, dp4, cw, wa, wx, lam, tt):
    t = p4.shape[1]
    nt = t // tt

    def body(p_ref, sv_ref, h_ref, hh_ref, dya_ref, dp_in, cw_ref, wa_ref, wx_ref, lam_ref,
             dp_ref, dcw_ref, dcb_ref, dwa_ref, dwx_ref, dba_ref, dbx_ref, dlam_ref,
             dnext, dhc, afc):
        del dp_in
        i = pl.program_id(0)
        first_tile = i == nt - 1

        @pl.when(i == 0)
        def _():
            dnext[...] = jnp.zeros((8, D), F32)
            dhc[...] = jnp.zeros((1, D), F32)
            afc[...] = jnp.zeros((1, D), F32)

        xa = p_ref[:, 0:D]
        ga = p_ref[:, D:2 * D]
        xc, r, ig, a, mult = (sv_ref[idx] for idx in range(5))
        xcb = xc.astype(BF16)
        lam = lam_ref[...]
        sp = jnp.maximum(-lam, 0.0) + jnp.log1p(jnp.exp(-jnp.abs(lam)))
        h = h_ref[...]
        gl, dgl = _gelu_and_grad(ga)
        dya = dya_ref[...].astype(F32)
        dga = dya * h * dgl
        rows = lax.broadcasted_iota(jnp.int32, (tt, 1), 0)
        a_next = jnp.where(rows == tt - 1, afc[...], pltpu.roll(a, tt - 1, 0))
        dh = _scan_up(a_next, dya * gl, dhc[...])
        dhc[...] = dh[0:1, :]
        afc[...] = a[0:1, :]
        h_prev = jnp.where(rows == 0, jnp.where(first_tile, 0.0, hh_ref[7:8, :]), pltpu.roll(h, 1, 0))
        da = dh * h_prev
        dmult = dh * ig * xc
        di = dh * mult * xc
        dlog_a = da * a - dmult * a * a / mult
        dr = dlog_a * (-LRU_C * sp)
        dsp = jnp.sum(dlog_a * (-LRU_C * r), axis=0, keepdims=True)
        dra = dr * r * (1.0 - r)
        dix = di * ig * (1.0 - ig)
        drab = dra.astype(BF16)
        dixb = dix.astype(BF16)
        dxc_lin = []
        dwa_new = []
        dwx_new = []
        for n in range(NH):
            cs = slice(n * HD, (n + 1) * HD)
            dxc_lin.append(_dot_nt(drab[:, cs], wa_ref[n]) + _dot_nt(dixb[:, cs], wx_ref[n]))
            dwa_new.append(_dot_tn(xcb[:, cs], drab[:, cs]))
            dwx_new.append(_dot_tn(xcb[:, cs], dixb[:, cs]))
        dxc = dh * mult * ig + jnp.concatenate(dxc_lin, axis=1)
        de = jnp.concatenate([dxc, dnext[...]], axis=0)
        ups = [de[0:tt, :]] + [pltpu.roll(de, tt + 8 - s, 0)[0:tt, :] for s in (1, 2, 3)]
        dxa = sum(cw_ref[3 - s:4 - s, :] * ups[s] for s in range(4))
        dnext[...] = dxc[0:8, :]
        dp_ref[:, 0:D] = dxa.astype(BF16)
        dp_ref[:, D:2 * D] = dga.astype(BF16)
        dcw = jnp.concatenate(
            [jnp.sum(ups[3 - k] * xa, axis=0, keepdims=True) for k in range(4)], axis=0)
        dcb = jnp.sum(dxc, axis=0, keepdims=True)
        dba = jnp.sum(dra, axis=0, keepdims=True)
        dbx = jnp.sum(dix, axis=0, keepdims=True)
        dlam = dsp * (-_sig(-lam))

        @pl.when(i == 0)
        def _():
            dcw_ref[...] = dcw
            dcb_ref[...] = dcb
            dba_ref[...] = dba
            dbx_ref[...] = dbx
            dlam_ref[...] = dlam
            for n in range(NH):
                dwa_ref[n] = dwa_new[n]
                dwx_ref[n] = dwx_new[n]

        @pl.when(i > 0)
        def _():
            dcw_ref[...] += dcw
            dcb_ref[...] += dcb
            dba_ref[...] += dba
            dbx_ref[...] += dbx
            dlam_ref[...] += dlam
            for n in range(NH):
                dwa_ref[n] += dwa_new[n]
                dwx_ref[n] += dwx_new[n]

    rev = lambda i: nt - 1 - i
    hb = tt // 8
    full = lambda shape: pl.BlockSpec(shape, lambda i: (0,) * len(shape))
    vecs = jax.ShapeDtypeStruct((1, D), F32)
    blk = jax.ShapeDtypeStruct((NH, HD, HD), F32)
    return pl.pallas_call(
        body, name="mixer_a_bwd",
        out_shape=(jax.ShapeDtypeStruct(dp4.shape, BF16), jax.ShapeDtypeStruct((4, D), F32), vecs, blk, blk,
                   vecs, vecs, vecs),
        grid=(nt,),
        in_specs=[pl.BlockSpec((None, tt, 2 * D), lambda i: (SLOT_A, rev(i), 0)),
                  pl.BlockSpec((5, tt, D), lambda i: (0, rev(i), 0)),
                  pl.BlockSpec((tt, D), lambda i: (rev(i), 0)),
                  pl.BlockSpec((8, D), lambda i: (jnp.maximum(rev(i) * hb - 1, 0), 0)),
                  pl.BlockSpec((tt, D), lambda i: (rev(i), 0)),
                  pl.BlockSpec(memory_space=pl.ANY),
                  full((4, D)), full((NH, HD, HD)), full((NH, HD, HD)), full((1, D))],
        out_specs=(pl.BlockSpec((None, tt, 2 * D), lambda i: (SLOT_A, rev(i), 0)),
                   full((4, D)), full((1, D)), full((NH, HD, HD)), full((NH, HD, HD)),
                   full((1, D)), full((1, D)), full((1, D))),
        scratch_shapes=[pltpu.VMEM((8, D), F32), pltpu.VMEM((1, D), F32), pltpu.VMEM((1, D), F32)],
        input_output_aliases={5: 0},
        compiler_params=_params(("arbitrary",), VMEM_BIG),
    )(p4, saved, hseq, hseq, dya, dp4, cw, wa, wx, lam)


def _local_step(x, tgt, stk_w_in, stk_rest, conv_a_s, conv_f_s, small, jc, cidx):
    t = x.shape[0]
    tt = min(256, t)
    tm = min(1024, t)
    tk = min(2048, t)
    wa_bf = small["lru_wa"].astype(BF16)
    wx_bf = small["lru_wx"].astype(BF16)

    h1 = _norm_fwd(x, small["norm_pre_mix"], tt)
    p4, w_in, _, conv_a_g, conv_f_g = _mm_in_gather(h1, stk_w_in, [], (), conv_a_s, conv_f_s, jc, tm)
    conv_a_w = jnp.transpose(conv_a_g, (1, 0, 2)).reshape(8, D)[0:4]
    conv_f_w = jnp.transpose(conv_f_g, (1, 0, 2)).reshape(8, DUP)[0:3]
    stk = dict(zip(REST, stk_rest))
    ya, hseq, saved_a, got_a = _mixer_a_fwd(p4, conv_a_w, small["conv_a_b"], wa_bf, wx_bf, small["lru_ba"],
                                            small["lru_bx"], small["lru_lambda"],
                                            [stk[n] for n in REST_A], REST_A, tt)
    yb, o_all, ss, saved_b, got_b = _hgrn2_fwd(p4, small["hg_lb_logits"], small["hg_norm_g"],
                                               [stk[n] for n in REST_B], REST_B, tt)
    w = dict(zip(REST_A + REST_B, _gather_forward(got_a + got_b, REST_A + REST_B)))
    w["w_in"] = w_in
    w_br_a = w["w_branch_a"].reshape(D, D)
    w_br_b = w["w_branch_b"].reshape(D, D)
    w_out = w["w_out"].reshape(D, D)
    w_down = w["w_down"].reshape(DFF, D)
    za, zb, mixb, m2, x1, h2 = _mid_fwd(ya, yb, p4, x, w_br_a, w_br_b, w_out, small["norm_post_mix"],
                                        small["norm_pre_ffn"], min(512, t))
    pre_g, pre_v, y, uv, gl, dgl = _up_act_fwd(h2, w["w_up"], conv_f_w, small["conv_f_b"], tt)
    dx2, dm3, lossv, d_norm_post_ffn = _down_loss(y, w_down, x1, tgt, small["norm_post_ffn"], min(512, t))

    d_w_down = _mm_tn(y, dm3, DFF // 2, D, tk, "mm_dw_down")
    dy = _mm_nt(dm3, w_down, BF16, tm, "mm_dy")
    dup_pre, d_conv_f_w, d_conv_f_b = _ffn_act_bwd(dy, pre_g, pre_v, uv, gl, dgl, conv_f_w, tt)
    d_w_up = _mm_tn(h2, dup_pre, D, SH_UP, tk, "mm_dw_up", stacked_out=True)
    dh2 = _mm_nt_sharded(dup_pre, w["w_up"], tm, "mm_dh2")
    dx1, dm2, dza, dzb, dya, dyb, dp4, d_norm_post_mix, d_norm_pre_ffn = _mid_bwd(
        dh2, dx2, x1, m2, za, zb, p4, w_br_a, w_br_b, w_out, small["norm_post_mix"], small["norm_pre_ffn"], tt)
    d_w_out = _mm_tn(mixb, dm2, D, D, tm, "mm_dw_out")
    d_w_br_a = _mm_tn(ya, dza, D, D, tm, "mm_dw_bra")
    d_w_br_b = _mm_tn(yb, dzb, D, D, tm, "mm_dw_brb")
    early = {"w_branch_a": d_w_br_a.reshape(NCHIP, SH_BR, D), "w_branch_b": d_w_br_b.reshape(NCHIP, SH_BR, D),
             "w_out": d_w_out.reshape(NCHIP, SH_BR, D), "w_up": d_w_up, "w_down": d_w_down.reshape(NCHIP, SH_DN, D)}
    rb, _ = _reduce_stage1(early, REST, (), "reduce_d2d_in_early")
    p_rest = [_sum_own_half(early[n], rb[n], cidx, "sum_half_" + n) for n in REST]
    dp4, d_lb, d_hg_norm_g, q_rest = _hgrn2_bwd(p4, o_all, ss, saved_b, dyb, dp4, small["hg_lb_logits"], small["hg_norm_g"],
                                                p_rest, tt)
    dp4, d_conv_a_w, d_conv_a_b, d_lru_wa, d_lru_wx, d_lru_ba, d_lru_bx, d_lru_lambda = _mixer_a_bwd(
        p4, hseq, saved_a, dya, dp4, conv_a_w, wa_bf, wx_bf, small["lru_lambda"], tt)
    d_w_in = _mm_tn(h1, dp4, D, SH_IN, tk, "mm_dw_in", stacked_slot_fn=_slot_of_chip, stacked_out=True)
    rb, _ = _reduce_stage1({"w_in": d_w_in}, ("w_in",), (), "reduce_d2d_in_w_in")
    p_w_in = _sum_own_half(d_w_in, rb["w_in"], cidx, "sum_half_w_in")
    grad_x, q_w_in, d_norm_pre_mix = _mm_dh1_exchange(dp4, w_in, p_w_in, x, dx1, small["norm_pre_mix"], tm)

    smalls = {
        "norm_pre_mix": d_norm_pre_mix, "conv_a_b": d_conv_a_b, "lru_ba": d_lru_ba, "lru_bx": d_lru_bx,
        "lru_lambda": d_lru_lambda, "hg_lb_logits": d_lb, "hg_norm_g": d_hg_norm_g, "norm_post_mix": d_norm_post_mix,
        "norm_pre_ffn": d_norm_pre_ffn, "norm_post_ffn": d_norm_post_ffn, "lossv": lossv,
        "conv_a_w": d_conv_a_w, "lru_wa": d_lru_wa, "lru_wx": d_lru_wx,
        "conv_f_b": d_conv_f_b, "conv_f_w": d_conv_f_w,
    }
    p_big = dict(zip(REST, p_rest), w_in=p_w_in)
    q_big = dict(zip(REST, q_rest), w_in=q_w_in)
    return grad_x, p_big, q_big, smalls


BIG = ("w_in", "w_branch_a", "w_branch_b", "w_out", "w_up", "w_down")
BIG_SHAPE = {"w_in": (D, SH_IN), "w_branch_a": (SH_BR, D), "w_branch_b": (SH_BR, D), "w_out": (SH_BR, D),
             "w_up": (D, SH_UP), "w_down": (SH_DN, D)}
NBIG = len(BIG)
REST = BIG[1:]
REST_A = ("w_branch_a", "w_branch_b", "w_out", "w_down")
REST_B = ("w_up",)
VEC_ROWS = (("norm_pre_mix", 0, 1), ("conv_a_b", 1, 1), ("lru_ba", 2, 1), ("lru_bx", 3, 1), ("lru_lambda", 4, 1),
            ("hg_lb_logits", 5, 2), ("hg_norm_g", 7, 1), ("norm_post_mix", 8, 1), ("norm_pre_ffn", 9, 1),
            ("norm_post_ffn", 10, 1))
ROW_LOSS = 11
ROW_CONV_A = 12
S1_ROWS = 16
S2_ROWS = 8


def _place():
    x, y, c = lax.axis_index("x"), lax.axis_index("y"), lax.axis_index("c")
    chips = [(1 - x, y), (x, 1 - y), (1 - x, 1 - y)]
    return x, y, c, 2 * x + y, chips


def _remote(src, dst, ssem, rsem, dev):
    return pltpu.make_async_remote_copy(src_ref=src, dst_ref=dst, send_sem=ssem, recv_sem=rsem,
                                        device_id=dev, device_id_type=MESH)


def _hbm_call(body, name, ins, out_shapes, n_sems, aliases=None):
    any_spec = pl.BlockSpec(memory_space=pl.ANY)
    return pl.pallas_call(
        body, name=name, out_shape=tuple(out_shapes),
        in_specs=[any_spec] * len(ins), out_specs=tuple([any_spec] * len(out_shapes)),
        scratch_shapes=[pltpu.SemaphoreType.DMA((n,)) for n in n_sems],
        input_output_aliases=aliases or {},
        compiler_params=pltpu.CompilerParams(has_side_effects=True),
    )(*ins)


def _gather_copies(stk, names, ssem, rsem, fssem=None, frsem=None):
    x, y, c, j, chips = _place()
    sends, arrive, fwds, farrive = [], [], [], []
    for w, n in enumerate(names):
        hw = BIG_SHAPE[n][0] // 2
        mine = stk[w].at[j, pl.ds(c * hw, hw), :]
        for k, (cx, cy) in enumerate(chips):
            i = 3 * w + k
            got = stk[w].at[2 * cx + cy, pl.ds(c * hw, hw), :]
            other = stk[w].at[2 * cx + cy, pl.ds((1 - c) * hw, hw), :]
            sends.append(_remote(mine, mine, ssem.at[i], rsem.at[i], (cx, cy, c)))
            arrive.append(_remote(got, got, ssem.at[i], rsem.at[i], (cx, cy, c)))
            if fssem is not None:
                fwds.append(_remote(got, got, fssem.at[i], frsem.at[i], (x, y, 1 - c)))
                farrive.append(_remote(other, other, fssem.at[i], frsem.at[i], (x, y, 1 - c)))
    return sends, arrive, fwds, farrive


def _ici_leg_behind(stk, names, ssem, rsem, first, last_fn):
    sends, arrive, _, _ = _gather_copies(stk, names, ssem, rsem)

    @pl.when(first)
    def _():
        for cp in sends:
            cp.start()

    def finish():
        @pl.when(last_fn())
        def _():
            for cp in arrive:
                cp.wait_recv()
            for cp in sends:
                cp.wait_send()

    return finish


def _gather_forward(stk, names):
    nw = len(names)

    def body(*refs):
        dst = refs[nw:2 * nw]
        ssem, rsem, fssem, frsem = refs[2 * nw:]
        _, _, fwds, farrive = _gather_copies(dst, names, ssem, rsem, fssem, frsem)
        for cp in fwds:
            cp.start()
        for cp in farrive:
            cp.wait_recv()
        for cp in fwds:
            cp.wait_send()

    out_shapes = [jax.ShapeDtypeStruct(a.shape, a.dtype) for a in stk]
    return _hbm_call(body, "gather_forward", stk, out_shapes, (3 * nw,) * 4, aliases={w: w for w in range(nw)})


def _exchange_copies(dst, pieces, ssem, rsem):
    x, y, c, j, chips = _place()
    sends, arrive = [], []
    for w in range(len(dst)):
        for k, (cx, cy) in enumerate(chips):
            i = 3 * w + k
            sends.append(_remote(pieces[w](2 * cx + cy), dst[w].at[j], ssem.at[i], rsem.at[i], (cx, cy, c)))
            got = dst[w].at[2 * cx + cy]
            arrive.append(_remote(got, got, ssem.at[i], rsem.at[i], (cx, cy, c)))
    return sends, arrive


def _reduce_stage1(big_g, names, smalls, name):
    nb = len(names)
    ins = [big_g[n] for n in names] + list(smalls)
    n_in = len(ins)
    halves = [BIG_SHAPE[n][0] // 2 for n in names]
    out_shapes = [jax.ShapeDtypeStruct((NCHIP, halves[w], BIG_SHAPE[n][1]), big_g[n].dtype)
                  for w, n in enumerate(names)]
    out_shapes += [jax.ShapeDtypeStruct(a.shape, F32) for a in smalls]

    def body(*refs):
        src, dst = refs[:n_in], refs[n_in:2 * n_in]
        ssem, rsem = refs[2 * n_in:]
        x, y, c, _, _ = _place()
        cps = []
        for w in range(n_in):
            s_ = src[w].at[:, pl.ds((1 - c) * halves[w], halves[w]), :] if w < nb else src[w]
            cp = _remote(s_, dst[w], ssem.at[w], rsem.at[w], (x, y, 1 - c))
            cp.start()
            cps.append(cp)
        for cp in cps:
            cp.wait()

    outs = _hbm_call(body, name, ins, out_shapes, (n_in, n_in))
    return dict(zip(names, outs[:nb])), outs[nb:]


def _reduce_stage2(ps1, ps2, ps3):
    ins = [ps1, ps2, ps3]
    h1, h2, h3 = S1_ROWS // 2, DUP // 2, D
    out_shapes = [jax.ShapeDtypeStruct((NCHIP, h1, D), F32), jax.ShapeDtypeStruct((NCHIP, S2_ROWS, h2), F32),
                  jax.ShapeDtypeStruct((NCHIP, h3, HD), F32)]

    def body(*refs):
        src, dst = refs[:3], refs[3:6]
        ssem, rsem = refs[6:]
        c = lax.axis_index("c")
        pieces = [lambda chip: src[0].at[pl.ds(c * h1, h1), :],
                  lambda chip: src[1].at[:, pl.ds(c * h2, h2)],
                  lambda chip: src[2].at[pl.ds(c * h3, h3), :]]
        sends, arrive = _exchange_copies(dst, pieces, ssem, rsem)
        for cp in sends:
            cp.start()
        for cp in arrive:
            cp.wait_recv()
        for cp in sends:
            cp.wait_send()

    return _hbm_call(body, "reduce_ici_small", ins, out_shapes, (9, 9))


def _reduce_stage3(f_big, fs1, fs2, fs3):
    ins = [f_big[n] for n in BIG] + [fs1, fs2, fs3]
    n_in = len(ins)
    halves = [BIG_SHAPE[n][0] // 2 for n in BIG]
    h1, h2, h3 = S1_ROWS // 2, DUP // 2, D
    out_shapes = [jax.ShapeDtypeStruct(BIG_SHAPE[n], F32) for n in BIG]
    out_shapes += [jax.ShapeDtypeStruct((S1_ROWS, D), F32), jax.ShapeDtypeStruct((S2_ROWS, DUP), F32),
                   jax.ShapeDtypeStruct((2 * D, HD), F32)]

    def body(*refs):
        dst = refs[n_in:2 * n_in]
        ssem, rsem = refs[2 * n_in:]
        x, y, c, _, _ = _place()

        def place(w, which):
            if w < NBIG:
                return dst[w].at[pl.ds(which * halves[w], halves[w]), :]
            if w == NBIG:
                return dst[w].at[pl.ds(which * h1, h1), :]
            if w == NBIG + 1:
                return dst[w].at[:, pl.ds(which * h2, h2)]
            return dst[w].at[pl.ds(which * h3, h3), :]

        cps = [_remote(place(w, c), place(w, c), ssem.at[w], rsem.at[w], (x, y, 1 - c)) for w in range(n_in)]
        for cp in cps:
            cp.start()
        for w in range(n_in):
            got = place(w, 1 - c)
            _remote(got, got, ssem.at[w], rsem.at[w], (x, y, 1 - c)).wait_recv()
        for cp in cps:
            cp.wait_send()

    outs = _hbm_call(body, "reduce_d2d_out", ins, out_shapes, (n_in, n_in), aliases={w: w for w in range(n_in)})
    return dict(zip(BIG, outs[:NBIG])), outs[NBIG], outs[NBIG + 1], outs[NBIG + 2]


def _row_tile(rows):
    for tr in (128, 176, 64, 16, 8):
        if rows % tr == 0:
            return tr
    return rows


def _sum_own_half(g, rb, cidx, name):
    s, rows, cols = g.shape
    half = rows // 2
    tr = _row_tile(half)
    nb = half // tr

    def body(c_ref, g_ref, r_ref, o_ref):
        del c_ref
        o_ref[...] = (g_ref[...].astype(F32) + r_ref[...].astype(F32)).astype(BF16)

    grid_spec = pltpu.PrefetchScalarGridSpec(
        num_scalar_prefetch=1, grid=(s, nb),
        in_specs=[pl.BlockSpec((None, tr, cols), lambda k, i, c: (k, c[0] * nb + i, 0)),
                  pl.BlockSpec((None, tr, cols), lambda k, i, c: (k, i, 0))],
        out_specs=pl.BlockSpec((None, tr, cols), lambda k, i, c: (k, i, 0)))
    return pl.pallas_call(
        body, name=name, grid_spec=grid_spec, out_shape=jax.ShapeDtypeStruct((s, half, cols), BF16),
        compiler_params=_params(("parallel", "parallel")),
    )(cidx, g, rb)


def _sum_chips(q, p, jc, name, by_cols=False):
    s, rows, cols = q.shape
    tr = _row_tile(rows)
    nb = rows // tr
    stacked = p.ndim == 3

    def body(jc_ref, q_ref, p_ref, o_ref):
        j = jc_ref[0]
        own = p_ref[...].astype(F32)
        acc = None
        for k in range(NCHIP):
            term = jnp.where(j == k, own, q_ref[k].astype(F32))
            acc = term if acc is None else acc + term
        o_ref[...] = acc

    if by_cols:
        half_spec = pl.BlockSpec((tr, cols), lambda i, jc_ref: (i, jc_ref[1]))
        out_shape = jax.ShapeDtypeStruct((rows, 2 * cols), F32)
    else:
        half_spec = pl.BlockSpec((tr, cols), lambda i, jc_ref: (jc_ref[1] * nb + i, 0))
        out_shape = jax.ShapeDtypeStruct((2 * rows, cols), F32)
    p_spec = pl.BlockSpec((None, tr, cols), lambda i, jc_ref: (jc_ref[0], i, 0)) if stacked else half_spec
    grid_spec = pltpu.PrefetchScalarGridSpec(
        num_scalar_prefetch=1, grid=(nb,),
        in_specs=[pl.BlockSpec((s, tr, cols), lambda i, jc_ref: (0, i, 0)), p_spec],
        out_specs=half_spec)
    return pl.pallas_call(
        body, name=name, grid_spec=grid_spec, out_shape=out_shape,
        compiler_params=_params(("parallel",)),
    )(jc, q, p)


def _place_shard(w, jc, name):
    rows, cols = w.shape
    tr = _row_tile(rows)

    def body(jc_ref, w_ref, o_ref):
        del jc_ref
        o_ref[...] = w_ref[...].astype(BF16)

    grid_spec = pltpu.PrefetchScalarGridSpec(
        num_scalar_prefetch=1, grid=(rows // tr,),
        in_specs=[pl.BlockSpec((tr, cols), lambda i, jc_ref: (i, 0))],
        out_specs=pl.BlockSpec((None, tr, cols), lambda i, jc_ref: (jc_ref[0], i, 0)))
    return pl.pallas_call(
        body, name=name, grid_spec=grid_spec, out_shape=jax.ShapeDtypeStruct((NCHIP, rows, cols), BF16),
        compiler_params=_params(("parallel",)),
    )(jc, w)


def _add(a, b, name):
    def body(a_ref, b_ref, o_ref):
        o_ref[...] = a_ref[...] + b_ref[...]

    return pl.pallas_call(body, name=name, out_shape=jax.ShapeDtypeStruct(a.shape, F32))(a, b)


def _pack_small(sm):
    vec_in = [sm[n] for n, _, _ in VEC_ROWS]
    nv = len(vec_in)

    def body(*refs):
        ins, lossv, dcw, dcfb, dcfw, s1, s2 = refs[:nv], refs[nv], refs[nv + 1], refs[nv + 2], refs[nv + 3], \
            refs[nv + 4], refs[nv + 5]
        for ref, (_, r0, nr) in zip(ins, VEC_ROWS):
            s1[r0:r0 + nr, :] = ref[...]
        s1[ROW_LOSS:ROW_LOSS + 1, :] = lossv[...]
        s1[ROW_CONV_A:ROW_CONV_A + 4, :] = dcw[...]
        s2[0:1, :] = dcfb[...]
        s2[1:4, :] = dcfw[...]
        s2[4:8, :] = jnp.zeros((4, DUP), F32)

    return pl.pallas_call(
        body, name="pack_small",
        out_shape=(jax.ShapeDtypeStruct((S1_ROWS, D), F32), jax.ShapeDtypeStruct((S2_ROWS, DUP), F32)),
    )(*vec_in, sm["lossv"], sm["conv_a_w"], sm["conv_f_b"], sm["conv_f_w"])


def _adam_math(w, g, m, v):
    m = ADAM_B1 * m + (1.0 - ADAM_B1) * g
    v = ADAM_B2 * v + (1.0 - ADAM_B2) * (g * g)
    m_hat = m / (1.0 - ADAM_B1 ** ADAM_STEP)
    v_hat = v / (1.0 - ADAM_B2 ** ADAM_STEP)
    delta = -ADAM_LR * (m_hat / (jnp.sqrt(v_hat) + ADAM_EPS) + ADAM_WD * w)
    return delta, m, v


def _adam(w, g, m, v, name):
    rows, cols = w.shape
    tr = _row_tile(rows)

    def body(w_ref, g_ref, m_ref, v_ref, d_ref, mo_ref, vo_ref):
        d_ref[...], mo_ref[...], vo_ref[...] = _adam_math(w_ref[...], g_ref[...], m_ref[...], v_ref[...])

    spec = pl.BlockSpec((tr, cols), lambda i: (i, 0))
    return pl.pallas_call(
        body, name=name, out_shape=(jax.ShapeDtypeStruct(w.shape, F32),) * 3, grid=(rows // tr,),
        in_specs=[spec] * 4, out_specs=(spec,) * 3,
        compiler_params=_params(("parallel",)),
    )(w, g, m, v)


def _adam_small(gs1, gs2, gs3, w, m, v):
    names = [n for n, _, _ in VEC_ROWS] + ["conv_f_b", "lru_wa", "lru_wx"]
    nn = len(names)

    def grad_of(i, g1, g2, g3):
        if i < len(VEC_ROWS):
            _, r0, nr = VEC_ROWS[i]
            return g1[r0:r0 + nr, :]
        if names[i] == "conv_f_b":
            return g2[0:1, :]
        return g3[0] if names[i] == "lru_wa" else g3[1]

    def body(*refs):
        g1, g2, g3 = refs[0], refs[1], refs[2]
        ws, ms, vs = refs[3:3 + nn], refs[3 + nn:3 + 2 * nn], refs[3 + 2 * nn:3 + 3 * nn]
        outs = refs[3 + 3 * nn:]
        for i in range(nn):
            d, mn, vn = _adam_math(ws[i][...], grad_of(i, g1, g2, g3), ms[i][...], vs[i][...])
            outs[i][...] = d
            outs[nn + i][...] = mn
            outs[2 * nn + i][...] = vn

    shapes = [jax.ShapeDtypeStruct(w[n].shape, F32) for n in names]
    outs = pl.pallas_call(body, name="adam_small", out_shape=tuple(shapes * 3))(
        gs1, gs2, gs3, *[w[n] for n in names], *[m[n] for n in names], *[v[n] for n in names])
    return {n: (outs[i], outs[nn + i], outs[2 * nn + i]) for i, n in enumerate(names)}


WEIGHTS = ("norm_pre_mix", "w_in", "conv_a_w", "conv_a_b", "lru_wa", "lru_ba", "lru_wx", "lru_bx", "lru_lambda",
           "hg_lb_logits", "hg_norm_g", "w_branch_a", "w_branch_b", "w_out", "norm_post_mix", "norm_pre_ffn",
           "w_up", "conv_f_w", "conv_f_b", "w_down", "norm_post_ffn")
NW = len(WEIGHTS)


def kernel(x, norm_pre_mix, w_in, conv_a_w, conv_a_b, lru_wa, lru_ba, lru_wx, lru_bx, lru_lambda, hg_lb_logits, hg_norm_g, w_branch_a, w_branch_b, w_out, norm_post_mix, norm_pre_ffn, w_up, conv_f_w, conv_f_b, w_down, norm_post_ffn, loss_target, m_norm_pre_mix, m_w_in, m_conv_a_w, m_conv_a_b, m_lru_wa, m_lru_ba, m_lru_wx, m_lru_bx, m_lru_lambda, m_hg_lb_logits, m_hg_norm_g, m_w_branch_a, m_w_branch_b, m_w_out, m_norm_post_mix, m_norm_pre_ffn, m_w_up, m_conv_f_w, m_conv_f_b, m_w_down, m_norm_post_ffn, v_norm_pre_mix, v_w_in, v_conv_a_w, v_conv_a_b, v_lru_wa, v_lru_ba, v_lru_wx, v_lru_bx, v_lru_lambda, v_hg_lb_logits, v_hg_norm_g, v_w_branch_a, v_w_branch_b, v_w_out, v_norm_post_mix, v_norm_pre_ffn, v_w_up, v_conv_f_w, v_conv_f_b, v_w_down, v_norm_post_ffn):
    rest = (norm_pre_mix, w_in, conv_a_w, conv_a_b, lru_wa, lru_ba, lru_wx, lru_bx, lru_lambda, hg_lb_logits, hg_norm_g, w_branch_a, w_branch_b, w_out, norm_post_mix, norm_pre_ffn, w_up, conv_f_w, conv_f_b, w_down, norm_post_ffn, loss_target, m_norm_pre_mix, m_w_in, m_conv_a_w, m_conv_a_b, m_lru_wa, m_lru_ba, m_lru_wx, m_lru_bx, m_lru_lambda, m_hg_lb_logits, m_hg_norm_g, m_w_branch_a, m_w_branch_b, m_w_out, m_norm_post_mix, m_norm_pre_ffn, m_w_up, m_conv_f_w, m_conv_f_b, m_w_down, m_norm_post_ffn, v_norm_pre_mix, v_w_in, v_conv_a_w, v_conv_a_b, v_lru_wa, v_lru_ba, v_lru_wx, v_lru_bx, v_lru_lambda, v_hg_lb_logits, v_hg_norm_g, v_w_branch_a, v_w_branch_b, v_w_out, v_norm_post_mix, v_norm_pre_ffn, v_w_up, v_conv_f_w, v_conv_f_b, v_w_down, v_norm_post_ffn)
    w_in_args = dict(zip(WEIGHTS, rest[:NW]))
    loss_target = rest[NW]
    m_args = dict(zip(WEIGHTS, rest[NW + 1:2 * NW + 1]))
    v_args = dict(zip(WEIGHTS, rest[2 * NW + 1:3 * NW + 1]))
    shape_of = {n: w_in_args[n].shape for n in WEIGHTS}

    def two_d(n, a):
        if n in BIG:
            return a.reshape(BIG_SHAPE[n])
        if n in ("lru_wa", "lru_wx"):
            return a.reshape(NH, HD, HD)
        return a.reshape(a.shape[-2:])

    w2 = {n: two_d(n, w_in_args[n]) for n in WEIGHTS}
    m2 = {n: two_d(n, m_args[n]) for n in WEIGHTS}
    v2 = {n: two_d(n, v_args[n]) for n in WEIGHTS}

    cidx = lax.axis_index("c").astype(jnp.int32).reshape(1)
    jchip = 2 * lax.axis_index("x") + lax.axis_index("y")

    jc = jnp.stack([jchip, lax.axis_index("c")]).astype(jnp.int32)

    shards = {n: _place_shard(w2[n], jc, "place_" + n) for n in BIG}
    conv_a_s = jnp.pad(w2["conv_a_w"], ((0, 4), (0, 0)))
    conv_f_s = jnp.pad(w2["conv_f_w"], ((0, 5), (0, 0)))
    small = {n: w2[n] for n in WEIGHTS if n not in BIG and n not in ("conv_a_w", "conv_f_w")}

    grad_x, p_big, q_big, sm_g = _local_step(
        x[0], loss_target[0], shards["w_in"], [shards[n] for n in REST], conv_a_s, conv_f_s, small, jc, cidx)

    s1, s2 = _pack_small(sm_g)
    s3 = jnp.concatenate([sm_g["lru_wa"].reshape(D, HD), sm_g["lru_wx"].reshape(D, HD)], axis=0)
    _, (rs1, rs2, rs3) = _reduce_stage1({}, (), (s1, s2, s3), "reduce_d2d_in_small")
    ps1, ps2, ps3 = _add(s1, rs1, "add_s1"), _add(s2, rs2, "add_s2"), _add(s3, rs3, "add_s3")
    qs1, qs2, qs3 = _reduce_stage2(ps1, ps2, ps3)
    f_big = {n: _sum_chips(q_big[n], p_big[n], jc, "sum_chips_" + n) for n in BIG}
    fs1 = _sum_chips(qs1, ps1, jc, "sum_chips_s1")
    fs2 = _sum_chips(qs2, ps2, jc, "sum_chips_s2", by_cols=True)
    fs3 = _sum_chips(qs3, ps3, jc, "sum_chips_s3")
    g_big, gs1, gs2, gs3 = _reduce_stage3(f_big, fs1, fs2, fs3)

    res = {}
    for n in BIG:
        d, mn, vn = _adam(w2[n], g_big[n], m2[n], v2[n], "adam_" + n)
        res[n] = (g_big[n], d, mn, vn)
    small_res = _adam_small(gs1, gs2, gs3.reshape(2, NH, HD, HD), w2, m2, v2)
    for n, r0, nr in VEC_ROWS:
        res[n] = (gs1[r0:r0 + nr],) + small_res[n]
    res["conv_f_b"] = (gs2[0:1],) + small_res["conv_f_b"]
    res["lru_wa"] = (gs3[0:D].reshape(NH, HD, HD),) + small_res["lru_wa"]
    res["lru_wx"] = (gs3[D:2 * D].reshape(NH, HD, HD),) + small_res["lru_wx"]
    g_ca = lax.dynamic_slice_in_dim(gs1[ROW_CONV_A:ROW_CONV_A + 4], jchip * (D // NCHIP), D // NCHIP, axis=1)
    g_cf = lax.dynamic_slice_in_dim(gs2[1:4], jchip * SH_UP, SH_UP, axis=1)
    res["conv_a_w"] = (g_ca,) + _adam(w2["conv_a_w"], g_ca, m2["conv_a_w"], v2["conv_a_w"], "adam_conv_a_w")
    res["conv_f_w"] = (g_cf,) + _adam(w2["conv_f_w"], g_cf, m2["conv_f_w"], v2["conv_f_w"], "adam_conv_f_w")

    loss = (0.5 / D) * jnp.sum(gs1[ROW_LOSS])
    out = [loss, grad_x.reshape(x.shape)]
    for part in range(4):
        out += [res[n][part].reshape(shape_of[n]) for n in WEIGHTS]
    return tuple(out)
```

```python
import functools

import jax
import jax.numpy as jnp
from jax import lax
from jax.experimental import pallas as pl
from jax.experimental.pallas import tpu as pltpu

F32 = jnp.float32
BF16 = jnp.bfloat16

D = 1024
NH = 8
HD = 128
CH = 32
DFF = 2816
DUP = 2 * DFF
NCHIP = 4
SH_IN = 2 * D
SH_UP = DUP // NCHIP
SH_DN = DFF // NCHIP
SH_BR = D // NCHIP
EPS = 1e-6
LRU_C = 8.0
ADAM_LR = 0.001
ADAM_B1 = 0.9
ADAM_B2 = 0.999
ADAM_EPS = 1e-08
ADAM_WD = 0.01
ADAM_STEP = 10
VMEM_BIG = 56 * 1024 * 1024
MESH = pl.DeviceIdType.MESH

SLOT_A, SLOT_B, SLOT_C, SLOT_G = 2, 0, 1, 3


def _slot_of_chip(s):
    return jnp.where(s == 3, 3, (s + 2) % 3)


def _params(sem, vmem=None):
    return pltpu.CompilerParams(dimension_semantics=sem, vmem_limit_bytes=vmem)


_GC = 0.7978845608028654
_GA = 0.044715


def _gelu(x):
    return 0.5 * x * (1.0 + jnp.tanh(_GC * (x + _GA * x * x * x)))


def _gelu_and_grad(x):
    x2 = x * x
    th = jnp.tanh(_GC * x * (1.0 + _GA * x2))
    g = 0.5 * x * (1.0 + th)
    dg = 0.5 * (1.0 + th) + 0.5 * x * (1.0 - th * th) * _GC * (1.0 + 3.0 * _GA * x2)
    return g, dg


def _sig(x):
    return jax.nn.sigmoid(x)


def _dot(a, b):
    return jnp.dot(a, b, preferred_element_type=F32)


def _dot_nt(a, b):
    return lax.dot_general(a, b, (((1,), (1,)), ((), ())), preferred_element_type=F32)


def _dot_tn(a, b):
    return lax.dot_general(a, b, (((0,), (0,)), ((), ())), preferred_element_type=F32)


def _chunk_cumsum(x):
    pos = lax.broadcasted_iota(jnp.int32, (x.shape[0], 1), 0) & (CH - 1)
    d = 1
    while d < CH:
        x = x + jnp.where(pos >= d, pltpu.roll(x, d, 0), 0.0)
        d *= 2
    return x


def _chunk_revcumsum(x):
    n = x.shape[0]
    pos = lax.broadcasted_iota(jnp.int32, (n, 1), 0) & (CH - 1)
    d = 1
    while d < CH:
        x = x + jnp.where(pos < CH - d, pltpu.roll(x, n - d, 0), 0.0)
        d *= 2
    return x


def _chunk_last(x):
    n = x.shape[0]
    return jnp.concatenate(
        [jnp.broadcast_to(x[c * CH + CH - 1:c * CH + CH, :], (CH, x.shape[1])) for c in range(n // CH)], axis=0)


def _chunk_total(x):
    n = x.shape[0]
    return jnp.concatenate(
        [jnp.broadcast_to(jnp.sum(x[c * CH:(c + 1) * CH, :], axis=0, keepdims=True), (CH, x.shape[1]))
         for c in range(n // CH)], axis=0)


def _rms_stats(x):
    r = lax.rsqrt(jnp.mean(x * x, axis=-1, keepdims=True) + EPS)
    return r, x * r


def _rms_bwd(gd, n, r):
    return r * (gd - n * jnp.mean(gd * n, axis=-1, keepdims=True))


def _shift_rows(x, d, fill):
    rows = lax.broadcasted_iota(jnp.int32, (x.shape[0], 1), 0)
    return jnp.where(rows >= d, pltpu.roll(x, d, 0), fill)


def _scan_down(a, u, carry):
    n = a.shape[0]
    pos = lax.broadcasted_iota(jnp.int32, (n, 1), 0) & 7
    for d in (1, 2, 4):
        u = a * jnp.where(pos >= d, pltpu.roll(u, d, 0), 0.0) + u
        a = a * jnp.where(pos >= d, pltpu.roll(a, d, 0), 1.0)
    out = []
    for v in range(n // 8):
        h = a[v * 8:v * 8 + 8, :] * carry + u[v * 8:v * 8 + 8, :]
        carry = h[7:8, :]
        out.append(h)
    return jnp.concatenate(out, axis=0)


def _scan_up(b, g, carry):
    n = b.shape[0]
    pos = lax.broadcasted_iota(jnp.int32, (n, 1), 0) & 7
    for d in (1, 2, 4):
        g = g + b * jnp.where(pos < 8 - d, pltpu.roll(g, n - d, 0), 0.0)
        b = b * jnp.where(pos < 8 - d, pltpu.roll(b, n - d, 0), 1.0)
    out = [None] * (n // 8)
    for v in reversed(range(n // 8)):
        h = g[v * 8:v * 8 + 8, :] + b[v * 8:v * 8 + 8, :] * carry
        carry = h[0:1, :]
        out[v] = h
    return jnp.concatenate(out, axis=0)


def _shift_rows_up(x, d, fill):
    n = x.shape[0]
    rows = lax.broadcasted_iota(jnp.int32, (n, 1), 0)
    return jnp.where(rows < n - d, pltpu.roll(x, n - d, 0), fill)


def _mm_nn_sharded(a, b3, out_dtype, tm, name, slot_fn=None):
    m, k = a.shape
    s, _, ns = b3.shape

    def body(a_ref, b_ref, o_ref):
        o_ref[...] = _dot(a_ref[...], b_ref[...]).astype(out_dtype)

    if slot_fn is None:
        out_shape = jax.ShapeDtypeStruct((m, s * ns), out_dtype)
        out_spec = pl.BlockSpec((tm, ns), lambda j, i: (i, j))
    else:
        out_shape = jax.ShapeDtypeStruct((s, m, ns), out_dtype)
        out_spec = pl.BlockSpec((None, tm, ns), lambda j, i: (slot_fn(j), i, 0))
    return pl.pallas_call(
        body, name=name, out_shape=out_shape, grid=(s, m // tm),
        in_specs=[pl.BlockSpec((tm, k), lambda j, i: (i, 0)),
                  pl.BlockSpec((None, k, ns), lambda j, i: (j, 0, 0))],
        out_specs=out_spec,
        compiler_params=_params(("parallel", "parallel"), VMEM_BIG),
    )(a, b3)


def _peer_of_step(s):
    return ((s & 1) << 1) | (s >> 1)


def _mm_in_gather(a, stk_w_in, stk_rest, names_rest, conv_a_s, conv_f_s, jc, tm):
    m, k = a.shape
    s_n, _, ns = stk_w_in.shape
    nr = len(stk_rest)
    mt = m // tm

    def body(jc_ref, a_ref, w_in_any, *rest):
        del w_in_any
        ca_src, cf_src = rest[nr], rest[nr + 1]
        o_ref, w_full = rest[nr + 2], rest[nr + 3]
        stk = rest[nr + 4:2 * nr + 4]
        ca_dst, cf_dst = rest[2 * nr + 4], rest[2 * nr + 5]
        wbuf, ssem_w, rsem_w, fssem_w, frsem_w, ssem_r, rsem_r, csend, crecv, lsem, wsem = rest[2 * nr + 6:]
        s, i = pl.program_id(0), pl.program_id(1)
        x, y, c, j, chips = _place()
        sends_w, arrive_w, fwds_w, farrive_w = _gather_copies([w_full], ("w_in",), ssem_w, rsem_w, fssem_w, frsem_w)
        sends_r, arrive_r, _, _ = _gather_copies(stk, names_rest, ssem_r, rsem_r)
        conv = ((ca_src, ca_dst), (cf_src, cf_dst))
        locs = [pltpu.make_async_copy(src, dst.at[j], lsem.at[n]) for n, (src, dst) in enumerate(conv)]
        csends = [_remote(src, dst.at[j], csend.at[3 * n + kk], crecv.at[3 * n + kk], (cx, cy, c))
                  for n, (src, dst) in enumerate(conv) for kk, (cx, cy) in enumerate(chips)]

        def fetch(step):
            return pltpu.make_async_copy(w_full.at[j ^ _peer_of_step(step)], wbuf.at[step & 1], wsem.at[step & 1])

        @pl.when((s == 0) & (i == 0))
        def _():
            for cp in sends_w[0:2] + locs + csends + sends_r:
                cp.start()
            fetch(s).start()

        @pl.when(i == 0)
        def _():
            fetch(s).wait()

        o_ref[...] = _dot(a_ref[...], wbuf[s & 1])

        @pl.when((s == 0) & (i == mt - 1))
        def _():
            for kk in (0, 1):
                arrive_w[kk].wait_recv()
                fwds_w[kk].start()
            sends_w[2].start()

        @pl.when((s == 1) & (i == mt - 1))
        def _():
            arrive_w[2].wait_recv()
            fwds_w[2].start()

        for kk in range(3):
            @pl.when((s == kk) & (i == mt - 1))
            def _(kk=kk):
                farrive_w[kk].wait_recv()
                fetch(s + 1).start()

        @pl.when((s == s_n - 1) & (i == mt - 1))
        def _():
            for cp in arrive_r:
                cp.wait_recv()
            for n, (_, dst) in enumerate(conv):
                for kk, (cx, cy) in enumerate(chips):
                    got = dst.at[2 * cx + cy]
                    _remote(got, got, csend.at[3 * n + kk], crecv.at[3 * n + kk], (cx, cy, c)).wait_recv()
            for cp in sends_w + fwds_w + sends_r + csends:
                cp.wait_send()
            for cp in locs:
                cp.wait()

    any_spec = pl.BlockSpec(memory_space=pl.ANY)
    sem = pltpu.SemaphoreType.DMA
    grid_spec = pltpu.PrefetchScalarGridSpec(
        num_scalar_prefetch=1, grid=(s_n, mt),
        in_specs=[pl.BlockSpec((tm, k), lambda s, i, jc_ref: (i, 0))] + [any_spec] * (nr + 3),
        out_specs=(pl.BlockSpec((None, tm, ns),
                                lambda s, i, jc_ref: (_slot_of_chip(jc_ref[0] ^ _peer_of_step(s)), i, 0)),)
        + (any_spec,) * (nr + 3),
        scratch_shapes=[pltpu.VMEM((2, k, ns), BF16), sem((3,)), sem((3,)), sem((3,)), sem((3,)),
                        sem((max(3 * nr, 1),)), sem((max(3 * nr, 1),)), sem((6,)), sem((6,)), sem((2,)), sem((2,))])
    outs = pl.pallas_call(
        body, name="mm_in", grid_spec=grid_spec,
        out_shape=(jax.ShapeDtypeStruct((s_n, m, ns), F32), jax.ShapeDtypeStruct(stk_w_in.shape, BF16))
        + tuple(jax.ShapeDtypeStruct(v.shape, v.dtype) for v in stk_rest)
        + tuple(jax.ShapeDtypeStruct((NCHIP,) + v.shape, v.dtype) for v in (conv_a_s, conv_f_s)),
        input_output_aliases={2 + w: 1 + w for w in range(nr + 1)},
        compiler_params=pltpu.CompilerParams(dimension_semantics=("arbitrary", "arbitrary"),
                                             vmem_limit_bytes=VMEM_BIG, has_side_effects=True),
    )(jc, a, stk_w_in, *stk_rest, conv_a_s, conv_f_s)
    return outs[0], outs[1], list(outs[2:2 + nr]), outs[2 + nr], outs[3 + nr]


def _mm_dh1_exchange(dp4, w_in3, p_w_in, x, dx1, gain, tm):
    s, k, ns = w_in3.shape
    m = dp4.shape[1]
    mt = m // tm

    def body(a_ref, b_ref, p_ref, x_ref, dx1_ref, g_ref, o_ref, q_ref, dg_ref, ssem, rsem):
        i, j = pl.program_id(0), pl.program_id(1)
        sends, arrive = _exchange_copies([q_ref], [lambda chip: p_ref.at[chip]], ssem, rsem)

        @pl.when((i == 0) & (j == 0))
        def _():
            for cp in sends:
                cp.start()

        @pl.when(j == 0)
        def _():
            o_ref[...] = _dot_nt(a_ref[...], b_ref[...])

        @pl.when(j > 0)
        def _():
            o_ref[...] += _dot_nt(a_ref[...], b_ref[...])

        @pl.when(j == s - 1)
        def _():
            dh = o_ref[...]
            r, n = _rms_stats(x_ref[...])
            o_ref[...] = dx1_ref[...] + _rms_bwd(dh * g_ref[...], n, r)
            dgv = jnp.sum(dh * n, axis=0, keepdims=True)

            @pl.when(i == 0)
            def _():
                dg_ref[...] = dgv

            @pl.when(i > 0)
            def _():
                dg_ref[...] += dgv

        @pl.when((i == mt - 1) & (j == s - 1))
        def _():
            for cp in arrive:
                cp.wait_recv()
            for cp in sends:
                cp.wait_send()

    any_spec = pl.BlockSpec(memory_space=pl.ANY)
    row_tile = pl.BlockSpec((tm, k), lambda i, j: (i, 0))
    vec = pl.BlockSpec((1, k), lambda i, j: (0, 0))
    return pl.pallas_call(
        body, name="mm_dh1",
        out_shape=(jax.ShapeDtypeStruct((m, k), F32), jax.ShapeDtypeStruct(p_w_in.shape, BF16),
                   jax.ShapeDtypeStruct((1, k), F32)),
        grid=(mt, s),
        in_specs=[pl.BlockSpec((None, tm, ns), lambda i, j: (_slot_of_chip(j), i, 0)),
                  pl.BlockSpec((None, k, ns), lambda i, j: (j, 0, 0)), any_spec, row_tile, row_tile, vec],
        out_specs=(row_tile, any_spec, vec),
        scratch_shapes=[pltpu.SemaphoreType.DMA((3,)), pltpu.SemaphoreType.DMA((3,))],
        compiler_params=pltpu.CompilerParams(dimension_semantics=("arbitrary", "arbitrary"),
                                             vmem_limit_bytes=VMEM_BIG, has_side_effects=True),
    )(dp4, w_in3, p_w_in, x, dx1, gain)


def _mm_nt_sharded(a, b3, tm, name, stacked_slot_fn=None):
    s, k, ns = b3.shape
    m = a.shape[1] if stacked_slot_fn is not None else a.shape[0]

    def body(a_ref, b_ref, o_ref):
        j = pl.program_id(1)
        @pl.when(j == 0)
        def _():
            o_ref[...] = _dot_nt(a_ref[...], b_ref[...])

        @pl.when(j > 0)
        def _():
            o_ref[...] += _dot_nt(a_ref[...], b_ref[...])

    if stacked_slot_fn is None:
        a_spec = pl.BlockSpec((tm, ns), lambda i, j: (i, j))
    else:
        a_spec = pl.BlockSpec((None, tm, ns), lambda i, j: (stacked_slot_fn(j), i, 0))
    return pl.pallas_call(
        body, name=name, out_shape=jax.ShapeDtypeStruct((m, k), F32), grid=(m // tm, s),
        in_specs=[a_spec, pl.BlockSpec((None, k, ns), lambda i, j: (j, 0, 0))],
        out_specs=pl.BlockSpec((tm, k), lambda i, j: (i, 0)),
        compiler_params=_params(("parallel", "arbitrary"), VMEM_BIG),
    )(a, b3)


def _mm_nt(a, b, out_dtype, tm, name):
    m, k = a.shape
    n = b.shape[0]

    def body(a_ref, b_ref, o_ref):
        o_ref[...] = _dot_nt(a_ref[...], b_ref[...]).astype(out_dtype)

    return pl.pallas_call(
        body, name=name, out_shape=jax.ShapeDtypeStruct((m, n), out_dtype), grid=(m // tm,),
        in_specs=[pl.BlockSpec((tm, k), lambda i: (i, 0)), pl.BlockSpec((n, k), lambda i: (0, 0))],
        out_specs=pl.BlockSpec((tm, n), lambda i: (i, 0)),
        compiler_params=_params(("parallel",), VMEM_BIG),
    )(a, b)


def _mm_tn(a, g, tkk, tn, tk, name, stacked_slot_fn=None, stacked_out=False):
    m, k = a.shape
    if stacked_slot_fn is not None:
        n = g.shape[0] * g.shape[2]
        g_spec = pl.BlockSpec((None, tk, tn), lambda kk, j, mm: (stacked_slot_fn(j), mm, 0))
    else:
        n = g.shape[1]
        g_spec = pl.BlockSpec((tk, tn), lambda kk, j, mm: (mm, j))
    steps = m // tk

    def body(a_ref, g_ref, o_ref, acc_ref):
        mm = pl.program_id(2)

        @pl.when(mm == 0)
        def _():
            acc_ref[...] = _dot_tn(a_ref[...], g_ref[...])

        @pl.when(mm > 0)
        def _():
            acc_ref[...] += _dot_tn(a_ref[...], g_ref[...])

        @pl.when(mm == steps - 1)
        def _():
            o_ref[...] = acc_ref[...].astype(BF16)

    if stacked_out:
        out_shape = jax.ShapeDtypeStruct((n // tn, k, tn), BF16)
        out_spec = pl.BlockSpec((None, tkk, tn), lambda kk, j, mm: (j, kk, 0))
    else:
        out_shape = jax.ShapeDtypeStruct((k, n), BF16)
        out_spec = pl.BlockSpec((tkk, tn), lambda kk, j, mm: (kk, j))
    return pl.pallas_call(
        body, name=name, out_shape=out_shape, grid=(k // tkk, n // tn, steps),
        in_specs=[pl.BlockSpec((tk, tkk), lambda kk, j, mm: (mm, kk)), g_spec],
        out_specs=out_spec,
        scratch_shapes=[pltpu.VMEM((tkk, tn), F32)],
        compiler_params=_params(("parallel", "parallel", "arbitrary"), VMEM_BIG),
    )(a, g)


def _norm_fwd(x, gain, tt):
    t = x.shape[0]

    def body(x_ref, g_ref, h_ref):
        _, n = _rms_stats(x_ref[...])
        h_ref[...] = (n * g_ref[...]).astype(BF16)

    return pl.pallas_call(
        body, name="norm_fwd", out_shape=jax.ShapeDtypeStruct((t, D), BF16), grid=(t // tt,),
        in_specs=[pl.BlockSpec((tt, D), lambda i: (i, 0)), pl.BlockSpec((1, D), lambda i: (0, 0))],
        out_specs=pl.BlockSpec((tt, D), lambda i: (i, 0)),
        compiler_params=_params(("parallel",)),
    )(x, gain)


def _lru_gates(xc, wa_ref, wx_ref, ba, bx, lam):
    xcb = xc.astype(BF16)
    ra = jnp.concatenate([_dot(xcb[:, n * HD:(n + 1) * HD], wa_ref[n]) for n in range(NH)], axis=1) + ba
    ix = jnp.concatenate([_dot(xcb[:, n * HD:(n + 1) * HD], wx_ref[n]) for n in range(NH)], axis=1) + bx
    r = _sig(ra)
    ig = _sig(ix)
    z = -lam
    sp = jnp.maximum(z, 0.0) + jnp.log1p(jnp.exp(-jnp.abs(z)))
    log_a = -LRU_C * r * sp
    a = jnp.exp(log_a)
    z2 = 2.0 * log_a
    series = -z2 * (1.0 + z2 * (0.5 + z2 * (1.0 / 6.0 + z2 * (1.0 / 24.0))))
    om = jnp.where(z2 > -0.02, series, 1.0 - jnp.exp(z2))
    mult = jnp.sqrt(om)
    return xcb, r, ig, sp, a, mult


def _mixer_a_fwd(p4, cw, cb, wa, wx, ba, bx, lam, stk, names, tt):
    t = p4.shape[1]
    ng = len(stk)

    def body(p_ref, cw_ref, cb_ref, wa_ref, wx_ref, ba_ref, bx_ref, lam_ref, *rest):
        ya_ref, h_ref, sv_ref = rest[ng:ng + 3]
        halo, hc, ssem, rsem = rest[2 * ng + 3:]
        i = pl.program_id(0)
        finish = _ici_leg_behind(rest[ng + 3:2 * ng + 3], names, ssem, rsem, i == 0, lambda: i == t // tt - 1)

        @pl.when(i == 0)
        def _():
            halo[...] = jnp.zeros((8, D), F32)
            hc[...] = jnp.zeros((1, D), F32)

        xa = p_ref[:, 0:D]
        ga = p_ref[:, D:2 * D]
        xe = jnp.concatenate([halo[...], xa], axis=0)
        xc = (cb_ref[...] + cw_ref[3:4, :] * xe
              + sum(cw_ref[3 - s:4 - s, :] * pltpu.roll(xe, s, 0) for s in (1, 2, 3)))[8:, :]
        halo[...] = xa[tt - 8:, :]
        _, r, ig, _, a, mult = _lru_gates(xc, wa_ref, wx_ref, ba_ref[...], bx_ref[...], lam_ref[...])
        u = mult * ig * xc
        h = _scan_down(a, u, hc[...])
        hc[...] = h[tt - 1:tt, :]
        h_ref[...] = h
        ya_ref[...] = (h * _gelu(ga)).astype(BF16)
        for idx, val in enumerate((xc, r, ig, a, mult)):
            sv_ref[idx] = val
        finish()

    full = lambda shape: pl.BlockSpec(shape, lambda i: (0,) * len(shape))
    any_spec = pl.BlockSpec(memory_space=pl.ANY)
    outs = pl.pallas_call(
        body, name="mixer_a_fwd",
        out_shape=(jax.ShapeDtypeStruct((t, D), BF16), jax.ShapeDtypeStruct((t, D), F32),
                   jax.ShapeDtypeStruct((5, t, D), F32)) + tuple(jax.ShapeDtypeStruct(v.shape, v.dtype) for v in stk),
        grid=(t // tt,),
        in_specs=[pl.BlockSpec((None, tt, 2 * D), lambda i: (SLOT_A, i, 0)),
                  full((4, D)), full((1, D)), full((NH, HD, HD)), full((NH, HD, HD)),
                  full((1, D)), full((1, D)), full((1, D))] + [any_spec] * ng,
        out_specs=(pl.BlockSpec((tt, D), lambda i: (i, 0)), pl.BlockSpec((tt, D), lambda i: (i, 0)),
                   pl.BlockSpec((5, tt, D), lambda i: (0, i, 0))) + (any_spec,) * ng,
        scratch_shapes=[pltpu.VMEM((8, D), F32), pltpu.VMEM((1, D), F32),
                        pltpu.SemaphoreType.DMA((3 * ng,)), pltpu.SemaphoreType.DMA((3 * ng,))],
        input_output_aliases={8 + w: 3 + w for w in range(ng)},
        compiler_params=pltpu.CompilerParams(dimension_semantics=("arbitrary",), vmem_limit_bytes=VMEM_BIG,
                                             has_side_effects=True),
    )(p4, cw, cb, wa, wx, ba, bx, lam, *stk)
    return outs[0], outs[1], outs[2], list(outs[3:])


def _chunk_masks(tt):
    row = lax.broadcasted_iota(jnp.int32, (tt, tt), 0)
    col = lax.broadcasted_iota(jnp.int32, (tt, tt), 1)
    same = jnp.right_shift(row, 5) == jnp.right_shift(col, 5)
    return same & (col <= row)


def _hg_head_fwd(q, fz, lbh, saved=None):
    sgn = _sig(-fz)
    k = (1.0 - lbh) * sgn
    if saved is None:
        sg = _sig(fz)
        f = lbh + (1.0 - lbh) * sg
        g = _chunk_cumsum(jnp.log(f))
    else:
        sg, g = saved
        f = lbh + (1.0 - lbh) * sg
    gu = _chunk_last(g) - g
    eg = jnp.exp(g)
    eng = jnp.exp(-g)
    egu = jnp.exp(gu)
    qt = q * eg
    kt = k * eng
    kd = k * egu
    return sg, sgn, f, k, g, eg, eng, egu, qt, kt, kd


def _lb_of(logits_ref):
    return _sig(logits_ref[0:1, :] - logits_ref[1:2, :])


def _hgrn2_fwd(p4, logits, gnorm, stk, names, tt):
    t = p4.shape[1]
    nc = tt // CH
    ng = len(stk)

    def body(p_ref, lg_ref, gn_ref, *rest):
        yb_ref, o_ref, ss_ref, sv_ref = rest[ng:ng + 4]
        st, ssem, rsem = rest[2 * ng + 4:]
        i = pl.program_id(0)
        finish = _ici_leg_behind(rest[ng + 4:2 * ng + 4], names, ssem, rsem, i == 0, lambda: i == t // tt - 1)

        @pl.when(i == 0)
        def _():
            st[...] = jnp.zeros((NH, HD, HD), F32)

        low = _chunk_masks(tt)
        lb = _lb_of(lg_ref)
        heads = [slice(h * HD, (h + 1) * HD) for h in range(NH)]
        sg, _, _, _, g, _, _, _, qt, kt, kd = _hg_head_fwd(p_ref[0, :, 0:D], p_ref[0, :, D:2 * D], lb)
        sv_ref[0] = sg
        sv_ref[1] = g
        qtb, ktb, kdb, vb = qt.astype(BF16), kt.astype(BF16), kd.astype(BF16), p_ref[1, :, 0:D].astype(BF16)
        decs = [jnp.exp(g[c * CH + CH - 1:c * CH + CH, :]) for c in range(nc)]
        o_in = []
        for hs in heads:
            att = jnp.where(low, _dot_nt(qtb[:, hs], ktb[:, hs]), 0.0)
            o_in.append(_dot(att.astype(BF16), vb[:, hs]))
        s_t = [st[h] for h in range(NH)]
        pieces = [[None] * nc for _ in range(NH)]
        for c in range(nc):
            sl = slice(c * CH, (c + 1) * CH)
            for h, hs in enumerate(heads):
                s_bf = s_t[h].astype(BF16)
                ss_ref[c, h] = s_bf
                pieces[h][c] = o_in[h][sl] + _dot_nt(qtb[sl, hs], s_bf)
                s_t[h] = s_t[h] * decs[c][:, hs] + _dot_tn(vb[sl, hs], kdb[sl, hs])
        for h, hs in enumerate(heads):
            st[h] = s_t[h]
            o = jnp.concatenate(pieces[h], axis=0)
            _, n = _rms_stats(o)
            og = p_ref[1, :, D + h * HD:D + (h + 1) * HD]
            o_ref[:, hs] = o
            yb_ref[:, hs] = (n * gn_ref[:, hs] * (og * _sig(og))).astype(BF16)
        finish()

    any_spec = pl.BlockSpec(memory_space=pl.ANY)
    outs = pl.pallas_call(
        body, name="hgrn2_fwd",
        out_shape=(jax.ShapeDtypeStruct((t, D), BF16), jax.ShapeDtypeStruct((t, D), F32),
                   jax.ShapeDtypeStruct((t // CH, NH, HD, HD), BF16), jax.ShapeDtypeStruct((2, t, D), F32))
        + tuple(jax.ShapeDtypeStruct(v.shape, v.dtype) for v in stk),
        grid=(t // tt,),
        in_specs=[pl.BlockSpec((2, tt, 2 * D), lambda i: (0, i, 0)),
                  pl.BlockSpec((2, D), lambda i: (0, 0)), pl.BlockSpec((1, D), lambda i: (0, 0))] + [any_spec] * ng,
        out_specs=(pl.BlockSpec((tt, D), lambda i: (i, 0)), pl.BlockSpec((tt, D), lambda i: (i, 0)),
                   pl.BlockSpec((nc, NH, HD, HD), lambda i: (i, 0, 0, 0)),
                   pl.BlockSpec((2, tt, D), lambda i: (0, i, 0))) + (any_spec,) * ng,
        scratch_shapes=[pltpu.VMEM((NH, HD, HD), F32), pltpu.SemaphoreType.DMA((3 * ng,)),
                        pltpu.SemaphoreType.DMA((3 * ng,))],
        input_output_aliases={3 + w: 4 + w for w in range(ng)},
        compiler_params=pltpu.CompilerParams(dimension_semantics=("arbitrary",), vmem_limit_bytes=VMEM_BIG,
                                             has_side_effects=True),
    )(p4, logits, gnorm, *stk)
    return outs[0], outs[1], outs[2], outs[3], list(outs[4:])


def _mid_fwd(ya, yb, p4, x, wa, wb, wo, g_pm, g_pf, tt):
    t = x.shape[0]

    def body(ya_ref, yb_ref, gt_ref, x_ref, wa_ref, wb_ref, wo_ref, gpm_ref, gpf_ref,
             za_ref, zb_ref, mix_ref, m2_ref, x1_ref, h2_ref):
        za = _dot(ya_ref[...], wa_ref[...])
        zb = _dot(yb_ref[...], wb_ref[...])
        mix = _sig(gt_ref[:, 0:D]) * za + _sig(gt_ref[:, D:2 * D]) * zb
        mixb = mix.astype(BF16)
        m2 = _dot(mixb, wo_ref[...])
        _, n2 = _rms_stats(m2)
        x1 = x_ref[...] + n2 * gpm_ref[...]
        _, n1 = _rms_stats(x1)
        za_ref[...] = za.astype(BF16)
        zb_ref[...] = zb.astype(BF16)
        mix_ref[...] = mixb
        m2_ref[...] = m2
        x1_ref[...] = x1
        h2_ref[...] = (n1 * gpf_ref[...]).astype(BF16)

    row = lambda dt: jax.ShapeDtypeStruct((t, D), dt)
    tile = pl.BlockSpec((tt, D), lambda i: (i, 0))
    wsp = pl.BlockSpec((D, D), lambda i: (0, 0))
    vec = pl.BlockSpec((1, D), lambda i: (0, 0))
    return pl.pallas_call(
        body, name="mid_fwd",
        out_shape=(row(BF16), row(BF16), row(BF16), row(F32), row(F32), row(BF16)),
        grid=(t // tt,),
        in_specs=[tile, tile, pl.BlockSpec((None, tt, 2 * D), lambda i: (SLOT_G, i, 0)), tile,
                  wsp, wsp, wsp, vec, vec],
        out_specs=(tile,) * 6,
        compiler_params=_params(("parallel",), VMEM_BIG),
    )(ya, yb, p4, x, wa, wb, wo, g_pm, g_pf)


def _up_act_fwd(h2, w_up4, cfw, cfb, tm):
    t = h2.shape[0]
    ns = SH_UP

    def body(a_ref, ah_ref, wg_ref, wv_ref, cwg_ref, cwv_ref, cbg_ref, cbv_ref,
             pg_ref, pv_ref, y_ref, uv_ref, gl_ref, dgl_ref):
        i = pl.program_id(1)
        rows = jnp.concatenate([ah_ref[...], a_ref[...]], axis=0)
        ups = []
        for w_ref, cw_ref, cb_ref, pre_ref in ((wg_ref, cwg_ref, cbg_ref, pg_ref), (wv_ref, cwv_ref, cbv_ref, pv_ref)):
            pre = _dot(rows, w_ref[...])
            pre_ref[...] = pre[16:, :]
            xe = jnp.concatenate([jnp.where(i > 0, pre[8:16, :], 0.0), pre[16:, :]], axis=0)
            up = (cb_ref[...] + cw_ref[2:3, :] * xe + cw_ref[1:2, :] * pltpu.roll(xe, 1, 0)
                  + cw_ref[0:1, :] * pltpu.roll(xe, 2, 0))
            ups.append(up[8:, :])
        gl, dgl = _gelu_and_grad(ups[0])
        y_ref[...] = (gl * ups[1]).astype(BF16)
        uv_ref[...] = ups[1].astype(BF16)
        gl_ref[...] = gl.astype(BF16)
        dgl_ref[...] = dgl.astype(BF16)

    hb = tm // 16
    tile = pl.BlockSpec((tm, ns), lambda p, i: (i, p))
    return pl.pallas_call(
        body, name="up_act_fwd",
        out_shape=(jax.ShapeDtypeStruct((t, DFF), F32),) * 2 + (jax.ShapeDtypeStruct((t, DFF), BF16),) * 4,
        grid=(2, t // tm),
        in_specs=[pl.BlockSpec((tm, D), lambda p, i: (i, 0)),
                  pl.BlockSpec((16, D), lambda p, i: (jnp.maximum(i * hb - 1, 0), 0)),
                  pl.BlockSpec((None, D, ns), lambda p, i: (p, 0, 0)),
                  pl.BlockSpec((None, D, ns), lambda p, i: (p + 2, 0, 0)),
                  pl.BlockSpec((3, ns), lambda p, i: (0, p)), pl.BlockSpec((3, ns), lambda p, i: (0, p + 2)),
                  pl.BlockSpec((1, ns), lambda p, i: (0, p)), pl.BlockSpec((1, ns), lambda p, i: (0, p + 2))],
        out_specs=(tile,) * 6,
        compiler_params=_params(("parallel", "parallel"), VMEM_BIG),
    )(h2, h2, w_up4, w_up4, cfw, cfw, cfb, cfb)


def _down_loss(y, wdn, x1, tgt, g_post, tt):
    t = x1.shape[0]

    def body(y_ref, w_ref, x1_ref, t_ref, g_ref, dx2_ref, dm3_ref, lossv_ref, dg_ref):
        i = pl.program_id(0)
        m3 = _dot(y_ref[...], w_ref[...])
        r, n3 = _rms_stats(m3)
        g = g_ref[...]
        e = x1_ref[...] + n3 * g - t_ref[...]
        dx2 = e * (1.0 / D)
        dx2_ref[...] = dx2
        dm3_ref[...] = _rms_bwd(dx2 * g, n3, r).astype(BF16)
        lv = jnp.sum(e * e, axis=0, keepdims=True)
        dgv = jnp.sum(dx2 * n3, axis=0, keepdims=True)

        @pl.when(i == 0)
        def _():
            lossv_ref[...] = lv
            dg_ref[...] = dgv

        @pl.when(i > 0)
        def _():
            lossv_ref[...] += lv
            dg_ref[...] += dgv

    tile = pl.BlockSpec((tt, D), lambda i: (i, 0))
    vec = pl.BlockSpec((1, D), lambda i: (0, 0))
    return pl.pallas_call(
        body, name="down_loss",
        out_shape=(jax.ShapeDtypeStruct((t, D), F32), jax.ShapeDtypeStruct((t, D), BF16),
                   jax.ShapeDtypeStruct((1, D), F32), jax.ShapeDtypeStruct((1, D), F32)),
        grid=(t // tt,),
        in_specs=[pl.BlockSpec((tt, DFF), lambda i: (i, 0)), pl.BlockSpec((DFF, D), lambda i: (0, 0)),
                  tile, tile, vec],
        out_specs=(tile, tile, vec, vec),
        compiler_params=_params(("arbitrary",), VMEM_BIG),
    )(y, wdn, x1, tgt, g_post)


def _ffn_act_bwd(dy, pre_g, pre_v, uv, gl, dgl, cfw, tt):
    t = dy.shape[0]
    nt = t // tt

    def body(dy_ref, dyn_ref, pg_ref, pv_ref, uv_ref, uvn_ref, gl_ref, gln_ref, dgl_ref, dgln_ref, cw_ref,
             du_ref, dcw_ref, dcb_ref):
        i = pl.program_id(0)
        n = tt + 8
        next_live = jnp.where(i < nt - 1, 1.0, 0.0)
        ext = lambda ref, nref: jnp.concatenate([ref[...].astype(F32), nref[...].astype(F32)[0:8, :]], axis=0)
        dy = jnp.concatenate([dy_ref[...].astype(F32), dyn_ref[...].astype(F32)[0:8, :] * next_live], axis=0)
        ds = (dy * ext(uv_ref, uvn_ref) * ext(dgl_ref, dgln_ref), dy * ext(gl_ref, gln_ref))
        dcw_parts, dcb_parts = [], []
        for hh, c0 in enumerate((0, DFF)):
            cs = slice(c0, c0 + DFF)
            dd = ds[hh]
            d1 = pltpu.roll(dd, n - 1, 0)
            d2 = pltpu.roll(dd, n - 2, 0)
            du_ref[:, cs] = (cw_ref[2:3, cs] * dd + cw_ref[1:2, cs] * d1 + cw_ref[0:1, cs] * d2)[0:tt, :].astype(BF16)
            x = (pg_ref, pv_ref)[hh][...]
            dcw_parts.append(jnp.concatenate(
                [jnp.sum(dk[0:tt, :] * x, axis=0, keepdims=True) for dk in (d2, d1, dd)], axis=0))
            dcb_parts.append(jnp.sum(dd[0:tt, :], axis=0, keepdims=True))
        dcw = jnp.concatenate(dcw_parts, axis=1)
        dcb = jnp.concatenate(dcb_parts, axis=1)

        @pl.when(i == 0)
        def _():
            dcw_ref[...] = dcw
            dcb_ref[...] = dcb

        @pl.when(i > 0)
        def _():
            dcw_ref[...] += dcw
            dcb_ref[...] += dcb

    half = pl.BlockSpec((tt, DFF), lambda i: (i, 0))
    half_next = pl.BlockSpec((16, DFF), lambda i: (jnp.minimum((i + 1) * (tt // 16), t // 16 - 1), 0))
    return pl.pallas_call(
        body, name="ffn_act_bwd",
        out_shape=(jax.ShapeDtypeStruct((t, DUP), BF16), jax.ShapeDtypeStruct((3, DUP), F32),
                   jax.ShapeDtypeStruct((1, DUP), F32)),
        grid=(nt,),
        in_specs=[half, half_next, half, half,
                  half, half_next, half, half_next, half, half_next,
                  pl.BlockSpec((3, DUP), lambda i: (0, 0))],
        out_specs=(pl.BlockSpec((tt, DUP), lambda i: (i, 0)), pl.BlockSpec((3, DUP), lambda i: (0, 0)),
                   pl.BlockSpec((1, DUP), lambda i: (0, 0))),
        compiler_params=_params(("arbitrary",), VMEM_BIG),
    )(dy, dy, pre_g, pre_v, uv, uv, gl, gl, dgl, dgl, cfw)


def _mid_bwd(dh2, dx2, x1, m2, za, zb, p4, wa, wb, wo, g_pm, g_pf, tt):
    t = x1.shape[0]

    def body(dh2_ref, dx2_ref, x1_ref, m2_ref, za_ref, zb_ref, gt_ref, wa_ref, wb_ref, wo_ref, gpm_ref, gpf_ref,
             dx1_ref, dm2_ref, dza_ref, dzb_ref, dya_ref, dyb_ref, dp_ref, dgpm_ref, dgpf_ref):
        i = pl.program_id(0)
        r1, n1 = _rms_stats(x1_ref[...])
        dh2 = dh2_ref[...]
        dx1 = dx2_ref[...] + _rms_bwd(dh2 * gpf_ref[...], n1, r1)
        r2, n2 = _rms_stats(m2_ref[...])
        dm2 = _rms_bwd(dx1 * gpm_ref[...], n2, r2).astype(BF16)
        dmix = _dot_nt(dm2, wo_ref[...])
        sa = _sig(gt_ref[:, 0:D])
        sb = _sig(gt_ref[:, D:2 * D])
        dza = (dmix * sa).astype(BF16)
        dzb = (dmix * sb).astype(BF16)
        dp_ref[:, 0:D] = (dmix * za_ref[...].astype(F32) * sa * (1.0 - sa)).astype(BF16)
        dp_ref[:, D:2 * D] = (dmix * zb_ref[...].astype(F32) * sb * (1.0 - sb)).astype(BF16)
        dx1_ref[...] = dx1
        dm2_ref[...] = dm2
        dza_ref[...] = dza
        dzb_ref[...] = dzb
        dya_ref[...] = _dot_nt(dza, wa_ref[...]).astype(BF16)
        dyb_ref[...] = _dot_nt(dzb, wb_ref[...]).astype(BF16)
        dgpf = jnp.sum(dh2 * n1, axis=0, keepdims=True)
        dgpm = jnp.sum(dx1 * n2, axis=0, keepdims=True)

        @pl.when(i == 0)
        def _():
            dgpf_ref[...] = dgpf
            dgpm_ref[...] = dgpm

        @pl.when(i > 0)
        def _():
            dgpf_ref[...] += dgpf
            dgpm_ref[...] += dgpm

    row = lambda dt: jax.ShapeDtypeStruct((t, D), dt)
    tile = pl.BlockSpec((tt, D), lambda i: (i, 0))
    wsp = pl.BlockSpec((D, D), lambda i: (0, 0))
    vec = pl.BlockSpec((1, D), lambda i: (0, 0))
    gates = pl.BlockSpec((None, tt, 2 * D), lambda i: (SLOT_G, i, 0))
    return pl.pallas_call(
        body, name="mid_bwd",
        out_shape=(row(F32), row(BF16), row(BF16), row(BF16), row(BF16), row(BF16),
                   jax.ShapeDtypeStruct((NCHIP, t, 2 * D), BF16),
                   jax.ShapeDtypeStruct((1, D), F32), jax.ShapeDtypeStruct((1, D), F32)),
        grid=(t // tt,),
        in_specs=[tile, tile, tile, tile, tile, tile, gates, wsp, wsp, wsp, vec, vec],
        out_specs=(tile, tile, tile, tile, tile, tile, gates, vec, vec),
        compiler_params=_params(("arbitrary",), VMEM_BIG),
    )(dh2, dx2, x1, m2, za, zb, p4, wa, wb, wo, g_pm, g_pf)


def _hgrn2_bwd(p4, o_all, ss, saved, dyb, dp4, logits, gnorm, p_early, tt):
    t = p4.shape[1]
    nt = t // tt
    nc = tt // CH
    ne = len(p_early)

    def body(p_ref, o_ref, ss_ref, sv_ref, dyb_ref, dp_in, lg_ref, gn_ref, *rest):
        del dp_in
        pe = rest[:ne]
        dp_ref, dlb_ref, dgn_ref = rest[ne:ne + 3]
        qe = rest[ne + 3:2 * ne + 3]
        dst, ssem, rsem = rest[2 * ne + 3:]
        i = pl.program_id(0)
        sends, arrive = _exchange_copies(qe, [(lambda chip, r=r: r.at[chip]) for r in pe], ssem, rsem)

        @pl.when(i == 0)
        def _():
            dst[...] = jnp.zeros((NH, HD, HD), F32)
            for cp in sends:
                cp.start()

        low = _chunk_masks(tt)
        lb = _lb_of(lg_ref)
        heads = [slice(h * HD, (h + 1) * HD) for h in range(NH)]
        sg, sgn, f, k, g, eg, eng, egu, qt, kt, kd = _hg_head_fwd(p_ref[0, :, 0:D], p_ref[0, :, D:2 * D], lb,
                                                                  (sv_ref[0], sv_ref[1]))
        qtb, ktb, kdb, vb = qt.astype(BF16), kt.astype(BF16), kd.astype(BF16), p_ref[1, :, 0:D].astype(BF16)
        decs = [jnp.exp(g[c * CH + CH - 1:c * CH + CH, :]) for c in range(nc)]
        og = p_ref[1, :, D:2 * D]
        so = _sig(og)
        dyb = dyb_ref[...].astype(F32)
        dob = dyb * (og * so)
        rn = [_rms_stats(o_ref[:, hs]) for hs in heads]
        r_all = jnp.concatenate([jnp.broadcast_to(r, (tt, HD)) for r, _ in rn], axis=1)
        n_all = jnp.concatenate([n for _, n in rn], axis=1)
        gd = dob * gn_ref[...]
        proj = jnp.concatenate(
            [jnp.broadcast_to(jnp.mean(gd[:, hs] * n_all[:, hs], axis=-1, keepdims=True), (tt, HD)) for hs in heads],
            axis=1)
        dob_ = (r_all * (gd - n_all * proj)).astype(BF16)
        dog = dyb * (n_all * gn_ref[...]) * (so * (1.0 + og * (1.0 - so)))
        dgn = jnp.sum(dob * n_all, axis=0, keepdims=True)
        dv_in, dqt_in, dkt_h = [], [], []
        for hs in heads:
            att = jnp.where(low, _dot_nt(qtb[:, hs], ktb[:, hs]), 0.0).astype(BF16)
            d_att = jnp.where(low, _dot_nt(dob_[:, hs], vb[:, hs]), 0.0).astype(BF16)
            dv_in.append(_dot_tn(att, dob_[:, hs]))
            dqt_in.append(_dot(d_att, ktb[:, hs]))
            dkt_h.append(_dot_tn(d_att, qtb[:, hs]))
        ds_t = [dst[h] for h in range(NH)]
        dv_p = [[None] * NH for _ in range(nc)]
        dqt_p = [[None] * NH for _ in range(nc)]
        dkd_p = [[None] * NH for _ in range(nc)]
        dgl_p = [[None] * NH for _ in range(nc)]
        for c in reversed(range(nc)):
            sl = slice(c * CH, (c + 1) * CH)
            for h, hs in enumerate(heads):
                s_prev = ss_ref[c, h]
                ds_bf = ds_t[h].astype(BF16)
                dec = decs[c][:, hs]
                dv_p[c][h] = dv_in[h][sl] + _dot_nt(kdb[sl, hs], ds_bf)
                dqt_p[c][h] = dqt_in[h][sl] + _dot(dob_[sl, hs], s_prev)
                dkd_p[c][h] = _dot(vb[sl, hs], ds_bf)
                ddec = jnp.sum(s_prev.astype(F32) * ds_t[h], axis=0, keepdims=True)
                dgl_p[c][h] = jnp.broadcast_to(ddec * dec, (CH, HD))
                ds_t[h] = ds_t[h] * dec + _dot_tn(dob_[sl, hs], qtb[sl, hs])
        for h in range(NH):
            dst[h] = ds_t[h]
        whole = lambda parts: jnp.concatenate([jnp.concatenate(row, axis=1) for row in parts], axis=0)
        dv, dqt, dkd, dgl = whole(dv_p), whole(dqt_p), whole(dkd_p), whole(dgl_p)
        dkt = jnp.concatenate(dkt_h, axis=1)
        dq = dqt * eg
        dk = dkt * eng + dkd * egu
        dg = dqt * qt - dkt * kt
        dgu = dkd * kd
        dlogf = _chunk_revcumsum(dg - dgu) + _chunk_total(dgu) + dgl
        common = sgn * (dlogf / f - dk)
        dfz = (1.0 - lb) * sg * common
        dlb = jnp.sum(common, axis=0, keepdims=True)
        dp_ref[0, :, 0:D] = dq.astype(BF16)
        dp_ref[0, :, D:2 * D] = dfz.astype(BF16)
        dp_ref[1, :, 0:D] = dv.astype(BF16)
        dp_ref[1, :, D:2 * D] = dog.astype(BF16)

        @pl.when(i == 0)
        def _():
            dlb_ref[0:1, :] = dlb
            dgn_ref[...] = dgn

        @pl.when(i > 0)
        def _():
            dlb_ref[0:1, :] += dlb
            dgn_ref[...] += dgn

        @pl.when(i == nt - 1)
        def _():
            d0 = dlb_ref[0:1, :] * lb * (1.0 - lb)
            dlb_ref[0:1, :] = d0
            dlb_ref[1:2, :] = -d0
            for cp in arrive:
                cp.wait_recv()
            for cp in sends:
                cp.wait_send()

    rev = lambda i: nt - 1 - i
    vec = pl.BlockSpec((1, D), lambda i: (0, 0))
    any_spec = pl.BlockSpec(memory_space=pl.ANY)
    outs = pl.pallas_call(
        body, name="hgrn2_bwd",
        out_shape=(jax.ShapeDtypeStruct(dp4.shape, BF16), jax.ShapeDtypeStruct((2, D), F32),
                   jax.ShapeDtypeStruct((1, D), F32)) + tuple(jax.ShapeDtypeStruct(a.shape, BF16) for a in p_early),
        grid=(nt,),
        in_specs=[pl.BlockSpec((2, tt, 2 * D), lambda i: (0, rev(i), 0)),
                  pl.BlockSpec((tt, D), lambda i: (rev(i), 0)),
                  pl.BlockSpec((nc, NH, HD, HD), lambda i: (rev(i), 0, 0, 0)),
                  pl.BlockSpec((2, tt, D), lambda i: (0, rev(i), 0)),
                  pl.BlockSpec((tt, D), lambda i: (rev(i), 0)),
                  any_spec,
                  pl.BlockSpec((2, D), lambda i: (0, 0)), vec] + [any_spec] * ne,
        out_specs=(pl.BlockSpec((2, tt, 2 * D), lambda i: (0, rev(i), 0)),
                   pl.BlockSpec((2, D), lambda i: (0, 0)), vec) + (any_spec,) * ne,
        scratch_shapes=[pltpu.VMEM((NH, HD, HD), F32), pltpu.SemaphoreType.DMA((3 * ne,)),
                        pltpu.SemaphoreType.DMA((3 * ne,))],
        input_output_aliases={5: 0},
        compiler_params=pltpu.CompilerParams(dimension_semantics=("arbitrary",), vmem_limit_bytes=VMEM_BIG,
                                             has_side_effects=True),
    )(p4, o_all, ss, saved, dyb, dp4, logits, gnorm, *p_early)
    return outs[0], outs[1], outs[2], list(outs[3:])


def _mixer_a_bwd(p4, hseq, saved, dya, dp4, cw, wa, wx, lam, tt):
    t = p4.shape[1]
    nt = t // tt

    def body(p_ref, sv_ref, h_ref, hh_ref, dya_ref, dp_in, cw_ref, wa_ref, wx_ref, lam_ref,
             dp_ref, dcw_ref, dcb_ref, dwa_ref, dwx_ref, dba_ref, dbx_ref, dlam_ref,
             dnext, dhc, afc):
        del dp_in
        i = pl.program_id(0)
        first_tile = i == nt - 1

        @pl.when(i == 0)
        def _():
            dnext[...] = jnp.zeros((8, D), F32)
            dhc[...] = jnp.zeros((1, D), F32)
            afc[...] = jnp.zeros((1, D), F32)

        xa = p_ref[:, 0:D]
        ga = p_ref[:, D:2 * D]
        xc, r, ig, a, mult = (sv_ref[idx] for idx in range(5))
        xcb = xc.astype(BF16)
        lam = lam_ref[...]
        sp = jnp.maximum(-lam, 0.0) + jnp.log1p(jnp.exp(-jnp.abs(lam)))
        h = h_ref[...]
        gl, dgl = _gelu_and_grad(ga)
        dya = dya_ref[...].astype(F32)
        dga = dya * h * dgl
        rows = lax.broadcasted_iota(jnp.int32, (tt, 1), 0)
        a_next = jnp.where(rows == tt - 1, afc[...], pltpu.roll(a, tt - 1, 0))
        dh = _scan_up(a_next, dya * gl, dhc[...])
        dhc[...] = dh[0:1, :]
        afc[...] = a[0:1, :]
        h_prev = jnp.where(rows == 0, jnp.where(first_tile, 0.0, hh_ref[7:8, :]), pltpu.roll(h, 1, 0))
        da = dh * h_prev
        dmult = dh * ig * xc
        di = dh * mult * xc
        dlog_a = da * a - dmult * a * a / mult
        dr = dlog_a * (-LRU_C * sp)
        dsp = jnp.sum(dlog_a * (-LRU_C * r), axis=0, keepdims=True)
        dra = dr * r * (1.0 - r)
        dix = di * ig * (1.0 - ig)
        drab = dra.astype(BF16)
        dixb = dix.astype(BF16)
        dxc_lin = []
        dwa_new = []
        dwx_new = []
        for n in range(NH):
            cs = slice(n * HD, (n + 1) * HD)
            dxc_lin.append(_dot_nt(drab[:, cs], wa_ref[n]) + _dot_nt(dixb[:, cs], wx_ref[n]))
            dwa_new.append(_dot_tn(xcb[:, cs], drab[:, cs]))
            dwx_new.append(_dot_tn(xcb[:, cs], dixb[:, cs]))
        dxc = dh * mult * ig + jnp.concatenate(dxc_lin, axis=1)
        de = jnp.concatenate([dxc, dnext[...]], axis=0)
        ups = [de[0:tt, :]] + [pltpu.roll(de, tt + 8 - s, 0)[0:tt, :] for s in (1, 2, 3)]
        dxa = sum(cw_ref[3 - s:4 - s, :] * ups[s] for s in range(4))
        dnext[...] = dxc[0:8, :]
        dp_ref[:, 0:D] = dxa.astype(BF16)
        dp_ref[:, D:2 * D] = dga.astype(BF16)
        dcw = jnp.concatenate(
            [jnp.sum(ups[3 - k] * xa, axis=0, keepdims=True) for k in range(4)], axis=0)
        dcb = jnp.sum(dxc, axis=0, keepdims=True)
        dba = jnp.sum(dra, axis=0, keepdims=True)
        dbx = jnp.sum(dix, axis=0, keepdims=True)
        dlam = dsp * (-_sig(-lam))

        @pl.when(i == 0)
        def _():
            dcw_ref[...] = dcw
            dcb_ref[...] = dcb
            dba_ref[...] = dba
            dbx_ref[...] = dbx
            dlam_ref[...] = dlam
            for n in range(NH):
                dwa_ref[n] = dwa_new[n]
                dwx_ref[n] = dwx_new[n]

        @pl.when(i > 0)
        def _():
            dcw_ref[...] += dcw
            dcb_ref[...] += dcb
            dba_ref[...] += dba
            dbx_ref[...] += dbx
            dlam_ref[...] += dlam
            for n in range(NH):
                dwa_ref[n] += dwa_new[n]
                dwx_ref[n] += dwx_new[n]

    rev = lambda i: nt - 1 - i
    hb = tt // 8
    full = lambda shape: pl.BlockSpec(shape, lambda i: (0,) * len(shape))
    vecs = jax.ShapeDtypeStruct((1, D), F32)
    blk = jax.ShapeDtypeStruct((NH, HD, HD), F32)
    return pl.pallas_call(
        body, name="mixer_a_bwd",
        out_shape=(jax.ShapeDtypeStruct(dp4.shape, BF16), jax.ShapeDtypeStruct((4, D), F32), vecs, blk, blk,
                   vecs, vecs, vecs),
        grid=(nt,),
        in_specs=[pl.BlockSpec((None, tt, 2 * D), lambda i: (SLOT_A, rev(i), 0)),
                  pl.BlockSpec((5, tt, D), lambda i: (0, rev(i), 0)),
                  pl.BlockSpec((tt, D), lambda i: (rev(i), 0)),
                  pl.BlockSpec((8, D), lambda i: (jnp.maximum(rev(i) * hb - 1, 0), 0)),
                  pl.BlockSpec((tt, D), lambda i: (rev(i), 0)),
                  pl.BlockSpec(memory_space=pl.ANY),
                  full((4, D)), full((NH, HD, HD)), full((NH, HD, HD)), full((1, D))],
        out_specs=(pl.BlockSpec((None, tt, 2 * D), lambda i: (SLOT_A, rev(i), 0)),
                   full((4, D)), full((1, D)), full((NH, HD, HD)), full((NH, HD, HD)),
                   full((1, D)), full((1, D)), full((1, D))),
        scratch_shapes=[pltpu.VMEM((8, D), F32), pltpu.VMEM((1, D), F32), pltpu.VMEM((1, D), F32)],
        input_output_aliases={5: 0},
        compiler_params=_params(("arbitrary",), VMEM_BIG),
    )(p4, saved, hseq, hseq, dya, dp4, cw, wa, wx, lam)


def _local_step(x, tgt, stk_w_in, stk_rest, conv_a_s, conv_f_s, small, jc, cidx):
    t = x.shape[0]
    tt = min(256, t)
    tm = min(1024, t)
    tk = min(2048, t)
    wa_bf = small["lru_wa"].astype(BF16)
    wx_bf = small["lru_wx"].astype(BF16)

    h1 = _norm_fwd(x, small["norm_pre_mix"], tt)
    p4, w_in, _, conv_a_g, conv_f_g = _mm_in_gather(h1, stk_w_in, [], (), conv_a_s, conv_f_s, jc, tm)
    conv_a_w = jnp.transpose(conv_a_g, (1, 0, 2)).reshape(8, D)[0:4]
    conv_f_w = jnp.transpose(conv_f_g, (1, 0, 2)).reshape(8, DUP)[0:3]
    stk = dict(zip(REST, stk_rest))
    ya, hseq, saved_a, got_a = _mixer_a_fwd(p4, conv_a_w, small["conv_a_b"], wa_bf, wx_bf, small["lru_ba"],
                                            small["lru_bx"], small["lru_lambda"],
                                            [stk[n] for n in REST_A], REST_A, tt)
    yb, o_all, ss, saved_b, got_b = _hgrn2_fwd(p4, small["hg_lb_logits"], small["hg_norm_g"],
                                               [stk[n] for n in REST_B], REST_B, tt)
    w = dict(zip(REST_A + REST_B, _gather_forward(got_a + got_b, REST_A + REST_B)))
    w["w_in"] = w_in
    w_br_a = w["w_branch_a"].reshape(D, D)
    w_br_b = w["w_branch_b"].reshape(D, D)
    w_out = w["w_out"].reshape(D, D)
    w_down = w["w_down"].reshape(DFF, D)
    za, zb, mixb, m2, x1, h2 = _mid_fwd(ya, yb, p4, x, w_br_a, w_br_b, w_out, small["norm_post_mix"],
                                        small["norm_pre_ffn"], min(512, t))
    pre_g, pre_v, y, uv, gl, dgl = _up_act_fwd(h2, w["w_up"], conv_f_w, small["conv_f_b"], tt)
    dx2, dm3, lossv, d_norm_post_ffn = _down_loss(y, w_down, x1, tgt, small["norm_post_ffn"], min(512, t))

    d_w_down = _mm_tn(y, dm3, DFF // 2, D, tk, "mm_dw_down")
    dy = _mm_nt(dm3, w_down, BF16, tm, "mm_dy")
    dup_pre, d_conv_f_w, d_conv_f_b = _ffn_act_bwd(dy, pre_g, pre_v, uv, gl, dgl, conv_f_w, tt)
    d_w_up = _mm_tn(h2, dup_pre, D, SH_UP, tk, "mm_dw_up", stacked_out=True)
    dh2 = _mm_nt_sharded(dup_pre, w["w_up"], tm, "mm_dh2")
    dx1, dm2, dza, dzb, dya, dyb, dp4, d_norm_post_mix, d_norm_pre_ffn = _mid_bwd(
        dh2, dx2, x1, m2, za, zb, p4, w_br_a, w_br_b, w_out, small["norm_post_mix"], small["norm_pre_ffn"], tt)
    d_w_out = _mm_tn(mixb, dm2, D, D, tm, "mm_dw_out")
    d_w_br_a = _mm_tn(ya, dza, D, D, tm, "mm_dw_bra")
    d_w_br_b = _mm_tn(yb, dzb, D, D, tm, "mm_dw_brb")
    early = {"w_branch_a": d_w_br_a.reshape(NCHIP, SH_BR, D), "w_branch_b": d_w_br_b.reshape(NCHIP, SH_BR, D),
             "w_out": d_w_out.reshape(NCHIP, SH_BR, D), "w_up": d_w_up, "w_down": d_w_down.reshape(NCHIP, SH_DN, D)}
    rb, _ = _reduce_stage1(early, REST, (), "reduce_d2d_in_early")
    p_rest = [_sum_own_half(early[n], rb[n], cidx, "sum_half_" + n) for n in REST]
    dp4, d_lb, d_hg_norm_g, q_rest = _hgrn2_bwd(p4, o_all, ss, saved_b, dyb, dp4, small["hg_lb_logits"], small["hg_norm_g"],
                                                p_rest, tt)
    dp4, d_conv_a_w, d_conv_a_b, d_lru_wa, d_lru_wx, d_lru_ba, d_lru_bx, d_lru_lambda = _mixer_a_bwd(
        p4, hseq, saved_a, dya, dp4, conv_a_w, wa_bf, wx_bf, small["lru_lambda"], tt)
    d_w_in = _mm_tn(h1, dp4, D, SH_IN, tk, "mm_dw_in", stacked_slot_fn=_slot_of_chip, stacked_out=True)
    rb, _ = _reduce_stage1({"w_in": d_w_in}, ("w_in",), (), "reduce_d2d_in_w_in")
    p_w_in = _sum_own_half(d_w_in, rb["w_in"], cidx, "sum_half_w_in")
    grad_x, q_w_in, d_norm_pre_mix = _mm_dh1_exchange(dp4, w_in, p_w_in, x, dx1, small["norm_pre_mix"], tm)

    smalls = {
        "norm_pre_mix": d_norm_pre_mix, "conv_a_b": d_conv_a_b, "lru_ba": d_lru_ba, "lru_bx": d_lru_bx,
        "lru_lambda": d_lru_lambda, "hg_lb_logits": d_lb, "hg_norm_g": d_hg_norm_g, "norm_post_mix": d_norm_post_mix,
        "norm_pre_ffn": d_norm_pre_ffn, "norm_post_ffn": d_norm_post_ffn, "lossv": lossv,
        "conv_a_w": d_conv_a_w, "lru_wa": d_lru_wa, "lru_wx": d_lru_wx,
        "conv_f_b": d_conv_f_b, "conv_f_w": d_conv_f_w,
    }
    p_big = dict(zip(REST, p_rest), w_in=p_w_in)
    q_big = dict(zip(REST, q_rest), w_in=q_w_in)
    return grad_x, p_big, q_big, smalls


BIG = ("w_in", "w_branch_a", "w_branch_b", "w_out", "w_up", "w_down")
BIG_SHAPE = {"w_in": (D, SH_IN), "w_branch_a": (SH_BR, D), "w_branch_b": (SH_BR, D), "w_out": (SH_BR, D),
             "w_up": (D, SH_UP), "w_down": (SH_DN, D)}
NBIG = len(BIG)
REST = BIG[1:]
REST_A = ("w_branch_a", "w_branch_b", "w_out", "w_down")
REST_B = ("w_up",)
VEC_ROWS = (("norm_pre_mix", 0, 1), ("conv_a_b", 1, 1), ("lru_ba", 2, 1), ("lru_bx", 3, 1), ("lru_lambda", 4, 1),
            ("hg_lb_logits", 5, 2), ("hg_norm_g", 7, 1), ("norm_post_mix", 8, 1), ("norm_pre_ffn", 9, 1),
            ("norm_post_ffn", 10, 1))
ROW_LOSS = 11
ROW_CONV_A = 12
S1_ROWS = 16
S2_ROWS = 8


def _place():
    x, y, c = lax.axis_index("x"), lax.axis_index("y"), lax.axis_index("c")
    chips = [(1 - x, y), (x, 1 - y), (1 - x, 1 - y)]
    return x, y, c, 2 * x + y, chips


def _remote(src, dst, ssem, rsem, dev):
    return pltpu.make_async_remote_copy(src_ref=src, dst_ref=dst, send_sem=ssem, recv_sem=rsem,
                                        device_id=dev, device_id_type=MESH)


def _hbm_call(body, name, ins, out_shapes, n_sems, aliases=None):
    any_spec = pl.BlockSpec(memory_space=pl.ANY)
    return pl.pallas_call(
        body, name=name, out_shape=tuple(out_shapes),
        in_specs=[any_spec] * len(ins), out_specs=tuple([any_spec] * len(out_shapes)),
        scratch_shapes=[pltpu.SemaphoreType.DMA((n,)) for n in n_sems],
        input_output_aliases=aliases or {},
        compiler_params=pltpu.CompilerParams(has_side_effects=True),
    )(*ins)


def _gather_copies(stk, names, ssem, rsem, fssem=None, frsem=None):
    x, y, c, j, chips = _place()
    sends, arrive, fwds, farrive = [], [], [], []
    for w, n in enumerate(names):
        hw = BIG_SHAPE[n][0] // 2
        mine = stk[w].at[j, pl.ds(c * hw, hw), :]
        for k, (cx, cy) in enumerate(chips):
            i = 3 * w + k
            got = stk[w].at[2 * cx + cy, pl.ds(c * hw, hw), :]
            other = stk[w].at[2 * cx + cy, pl.ds((1 - c) * hw, hw), :]
            sends.append(_remote(mine, mine, ssem.at[i], rsem.at[i], (cx, cy, c)))
            arrive.append(_remote(got, got, ssem.at[i], rsem.at[i], (cx, cy, c)))
            if fssem is not None:
                fwds.append(_remote(got, got, fssem.at[i], frsem.at[i], (x, y, 1 - c)))
                farrive.append(_remote(other, other, fssem.at[i], frsem.at[i], (x, y, 1 - c)))
    return sends, arrive, fwds, farrive


def _ici_leg_behind(stk, names, ssem, rsem, first, last_fn):
    sends, arrive, _, _ = _gather_copies(stk, names, ssem, rsem)

    @pl.when(first)
    def _():
        for cp in sends:
            cp.start()

    def finish():
        @pl.when(last_fn())
        def _():
            for cp in arrive:
                cp.wait_recv()
            for cp in sends:
                cp.wait_send()

    return finish


def _gather_forward(stk, names):
    nw = len(names)

    def body(*refs):
        dst = refs[nw:2 * nw]
        ssem, rsem, fssem, frsem = refs[2 * nw:]
        _, _, fwds, farrive = _gather_copies(dst, names, ssem, rsem, fssem, frsem)
        for cp in fwds:
            cp.start()
        for cp in farrive:
            cp.wait_recv()
        for cp in fwds:
            cp.wait_send()

    out_shapes = [jax.ShapeDtypeStruct(a.shape, a.dtype) for a in stk]
    return _hbm_call(body, "gather_forward", stk, out_shapes, (3 * nw,) * 4, aliases={w: w for w in range(nw)})


def _exchange_copies(dst, pieces, ssem, rsem):
    x, y, c, j, chips = _place()
    sends, arrive = [], []
    for w in range(len(dst)):
        for k, (cx, cy) in enumerate(chips):
            i = 3 * w + k
            sends.append(_remote(pieces[w](2 * cx + cy), dst[w].at[j], ssem.at[i], rsem.at[i], (cx, cy, c)))
            got = dst[w].at[2 * cx + cy]
            arrive.append(_remote(got, got, ssem.at[i], rsem.at[i], (cx, cy, c)))
    return sends, arrive


def _reduce_stage1(big_g, names, smalls, name):
    nb = len(names)
    ins = [big_g[n] for n in names] + list(smalls)
    n_in = len(ins)
    halves = [BIG_SHAPE[n][0] // 2 for n in names]
    out_shapes = [jax.ShapeDtypeStruct((NCHIP, halves[w], BIG_SHAPE[n][1]), big_g[n].dtype)
                  for w, n in enumerate(names)]
    out_shapes += [jax.ShapeDtypeStruct(a.shape, F32) for a in smalls]

    def body(*refs):
        src, dst = refs[:n_in], refs[n_in:2 * n_in]
        ssem, rsem = refs[2 * n_in:]
        x, y, c, _, _ = _place()
        cps = []
        for w in range(n_in):
            s_ = src[w].at[:, pl.ds((1 - c) * halves[w], halves[w]), :] if w < nb else src[w]
            cp = _remote(s_, dst[w], ssem.at[w], rsem.at[w], (x, y, 1 - c))
            cp.start()
            cps.append(cp)
        for cp in cps:
            cp.wait()

    outs = _hbm_call(body, name, ins, out_shapes, (n_in, n_in))
    return dict(zip(names, outs[:nb])), outs[nb:]


def _reduce_stage2(ps1, ps2, ps3):
    ins = [ps1, ps2, ps3]
    h1, h2, h3 = S1_ROWS // 2, DUP // 2, D
    out_shapes = [jax.ShapeDtypeStruct((NCHIP, h1, D), F32), jax.ShapeDtypeStruct((NCHIP, S2_ROWS, h2), F32),
                  jax.ShapeDtypeStruct((NCHIP, h3, HD), F32)]

    def body(*refs):
        src, dst = refs[:3], refs[3:6]
        ssem, rsem = refs[6:]
        c = lax.axis_index("c")
        pieces = [lambda chip: src[0].at[pl.ds(c * h1, h1), :],
                  lambda chip: src[1].at[:, pl.ds(c * h2, h2)],
                  lambda chip: src[2].at[pl.ds(c * h3, h3), :]]
        sends, arrive = _exchange_copies(dst, pieces, ssem, rsem)
        for cp in sends:
            cp.start()
        for cp in arrive:
            cp.wait_recv()
        for cp in sends:
            cp.wait_send()

    return _hbm_call(body, "reduce_ici_small", ins, out_shapes, (9, 9))


def _reduce_stage3(f_big, fs1, fs2, fs3):
    ins = [f_big[n] for n in BIG] + [fs1, fs2, fs3]
    n_in = len(ins)
    halves = [BIG_SHAPE[n][0] // 2 for n in BIG]
    h1, h2, h3 = S1_ROWS // 2, DUP // 2, D
    out_shapes = [jax.ShapeDtypeStruct(BIG_SHAPE[n], F32) for n in BIG]
    out_shapes += [jax.ShapeDtypeStruct((S1_ROWS, D), F32), jax.ShapeDtypeStruct((S2_ROWS, DUP), F32),
                   jax.ShapeDtypeStruct((2 * D, HD), F32)]

    def body(*refs):
        dst = refs[n_in:2 * n_in]
        ssem, rsem = refs[2 * n_in:]
        x, y, c, _, _ = _place()

        def place(w, which):
            if w < NBIG:
                return dst[w].at[pl.ds(which * halves[w], halves[w]), :]
            if w == NBIG:
                return dst[w].at[pl.ds(which * h1, h1), :]
            if w == NBIG + 1:
                return dst[w].at[:, pl.ds(which * h2, h2)]
            return dst[w].at[pl.ds(which * h3, h3), :]

        cps = [_remote(place(w, c), place(w, c), ssem.at[w], rsem.at[w], (x, y, 1 - c)) for w in range(n_in)]
        for cp in cps:
            cp.start()
        for w in range(n_in):
            got = place(w, 1 - c)
            _remote(got, got, ssem.at[w], rsem.at[w], (x, y, 1 - c)).wait_recv()
        for cp in cps:
            cp.wait_send()

    outs = _hbm_call(body, "reduce_d2d_out", ins, out_shapes, (n_in, n_in), aliases={w: w for w in range(n_in)})
    return dict(zip(BIG, outs[:NBIG])), outs[NBIG], outs[NBIG + 1], outs[NBIG + 2]


def _row_tile(rows):
    for tr in (128, 176, 64, 16, 8):
        if rows % tr == 0:
            return tr
    return rows


def _sum_own_half(g, rb, cidx, name):
    s, rows, cols = g.shape
    half = rows // 2
    tr = _row_tile(half)
    nb = half // tr

    def body(c_ref, g_ref, r_ref, o_ref):
        del c_ref
        o_ref[...] = (g_ref[...].astype(F32) + r_ref[...].astype(F32)).astype(BF16)

    grid_spec = pltpu.PrefetchScalarGridSpec(
        num_scalar_prefetch=1, grid=(s, nb),
        in_specs=[pl.BlockSpec((None, tr, cols), lambda k, i, c: (k, c[0] * nb + i, 0)),
                  pl.BlockSpec((None, tr, cols), lambda k, i, c: (k, i, 0))],
        out_specs=pl.BlockSpec((None, tr, cols), lambda k, i, c: (k, i, 0)))
    return pl.pallas_call(
        body, name=name, grid_spec=grid_spec, out_shape=jax.ShapeDtypeStruct((s, half, cols), BF16),
        compiler_params=_params(("parallel", "parallel")),
    )(cidx, g, rb)


def _sum_chips(q, p, jc, name, by_cols=False):
    s, rows, cols = q.shape
    tr = _row_tile(rows)
    nb = rows // tr
    stacked = p.ndim == 3

    def body(jc_ref, q_ref, p_ref, o_ref):
        j = jc_ref[0]
        own = p_ref[...].astype(F32)
        acc = None
        for k in range(NCHIP):
            term = jnp.where(j == k, own, q_ref[k].astype(F32))
            acc = term if acc is None else acc + term
        o_ref[...] = acc

    if by_cols:
        half_spec = pl.BlockSpec((tr, cols), lambda i, jc_ref: (i, jc_ref[1]))
        out_shape = jax.ShapeDtypeStruct((rows, 2 * cols), F32)
    else:
        half_spec = pl.BlockSpec((tr, cols), lambda i, jc_ref: (jc_ref[1] * nb + i, 0))
        out_shape = jax.ShapeDtypeStruct((2 * rows, cols), F32)
    p_spec = pl.BlockSpec((None, tr, cols), lambda i, jc_ref: (jc_ref[0], i, 0)) if stacked else half_spec
    grid_spec = pltpu.PrefetchScalarGridSpec(
        num_scalar_prefetch=1, grid=(nb,),
        in_specs=[pl.BlockSpec((s, tr, cols), lambda i, jc_ref: (0, i, 0)), p_spec],
        out_specs=half_spec)
    return pl.pallas_call(
        body, name=name, grid_spec=grid_spec, out_shape=out_shape,
        compiler_params=_params(("parallel",)),
    )(jc, q, p)


def _place_shard(w, jc, name):
    rows, cols = w.shape
    tr = _row_tile(rows)

    def body(jc_ref, w_ref, o_ref):
        del jc_ref
        o_ref[...] = w_ref[...].astype(BF16)

    grid_spec = pltpu.PrefetchScalarGridSpec(
        num_scalar_prefetch=1, grid=(rows // tr,),
        in_specs=[pl.BlockSpec((tr, cols), lambda i, jc_ref: (i, 0))],
        out_specs=pl.BlockSpec((None, tr, cols), lambda i, jc_ref: (jc_ref[0], i, 0)))
    return pl.pallas_call(
        body, name=name, grid_spec=grid_spec, out_shape=jax.ShapeDtypeStruct((NCHIP, rows, cols), BF16),
        compiler_params=_params(("parallel",)),
    )(jc, w)


def _add(a, b, name):
    def body(a_ref, b_ref, o_ref):
        o_ref[...] = a_ref[...] + b_ref[...]

    return pl.pallas_call(body, name=name, out_shape=jax.ShapeDtypeStruct(a.shape, F32))(a, b)


def _pack_small(sm):
    vec_in = [sm[n] for n, _, _ in VEC_ROWS]
    nv = len(vec_in)

    def body(*refs):
        ins, lossv, dcw, dcfb, dcfw, s1, s2 = refs[:nv], refs[nv], refs[nv + 1], refs[nv + 2], refs[nv + 3], \
            refs[nv + 4], refs[nv + 5]
        for ref, (_, r0, nr) in zip(ins, VEC_ROWS):
            s1[r0:r0 + nr, :] = ref[...]
        s1[ROW_LOSS:ROW_LOSS + 1, :] = lossv[...]
        s1[ROW_CONV_A:ROW_CONV_A + 4, :] = dcw[...]
        s2[0:1, :] = dcfb[...]
        s2[1:4, :] = dcfw[...]
        s2[4:8, :] = jnp.zeros((4, DUP), F32)

    return pl.pallas_call(
        body, name="pack_small",
        out_shape=(jax.ShapeDtypeStruct((S1_ROWS, D), F32), jax.ShapeDtypeStruct((S2_ROWS, DUP), F32)),
    )(*vec_in, sm["lossv"], sm["conv_a_w"], sm["conv_f_b"], sm["conv_f_w"])


def _adam_math(w, g, m, v):
    m = ADAM_B1 * m + (1.0 - ADAM_B1) * g
    v = ADAM_B2 * v + (1.0 - ADAM_B2) * (g * g)
    m_hat = m / (1.0 - ADAM_B1 ** ADAM_STEP)
    v_hat = v / (1.0 - ADAM_B2 ** ADAM_STEP)
    delta = -ADAM_LR * (m_hat / (jnp.sqrt(v_hat) + ADAM_EPS) + ADAM_WD * w)
    return delta, m, v


def _adam(w, g, m, v, name):
    rows, cols = w.shape
    tr = _row_tile(rows)

    def body(w_ref, g_ref, m_ref, v_ref, d_ref, mo_ref, vo_ref):
        d_ref[...], mo_ref[...], vo_ref[...] = _adam_math(w_ref[...], g_ref[...], m_ref[...], v_ref[...])

    spec = pl.BlockSpec((tr, cols), lambda i: (i, 0))
    return pl.pallas_call(
        body, name=name, out_shape=(jax.ShapeDtypeStruct(w.shape, F32),) * 3, grid=(rows // tr,),
        in_specs=[spec] * 4, out_specs=(spec,) * 3,
        compiler_params=_params(("parallel",)),
    )(w, g, m, v)


def _adam_small(gs1, gs2, gs3, w, m, v):
    names = [n for n, _, _ in VEC_ROWS] + ["conv_f_b", "lru_wa", "lru_wx"]
    nn = len(names)

    def grad_of(i, g1, g2, g3):
        if i < len(VEC_ROWS):
            _, r0, nr = VEC_ROWS[i]
            return g1[r0:r0 + nr, :]
        if names[i] == "conv_f_b":
            return g2[0:1, :]
        return g3[0] if names[i] == "lru_wa" else g3[1]

    def body(*refs):
        g1, g2, g3 = refs[0], refs[1], refs[2]
        ws, ms, vs = refs[3:3 + nn], refs[3 + nn:3 + 2 * nn], refs[3 + 2 * nn:3 + 3 * nn]
        outs = refs[3 + 3 * nn:]
        for i in range(nn):
            d, mn, vn = _adam_math(ws[i][...], grad_of(i, g1, g2, g3), ms[i][...], vs[i][...])
            outs[i][...] = d
            outs[nn + i][...] = mn
            outs[2 * nn + i][...] = vn

    shapes = [jax.ShapeDtypeStruct(w[n].shape, F32) for n in names]
    outs = pl.pallas_call(body, name="adam_small", out_shape=tuple(shapes * 3))(
        gs1, gs2, gs3, *[w[n] for n in names], *[m[n] for n in names], *[v[n] for n in names])
    return {n: (outs[i], outs[nn + i], outs[2 * nn + i]) for i, n in enumerate(names)}


WEIGHTS = ("norm_pre_mix", "w_in", "conv_a_w", "conv_a_b", "lru_wa", "lru_ba", "lru_wx", "lru_bx", "lru_lambda",
           "hg_lb_logits", "hg_norm_g", "w_branch_a", "w_branch_b", "w_out", "norm_post_mix", "norm_pre_ffn",
           "w_up", "conv_f_w", "conv_f_b", "w_down", "norm_post_ffn")
NW = len(WEIGHTS)


def kernel(x, norm_pre_mix, w_in, conv_a_w, conv_a_b, lru_wa, lru_ba, lru_wx, lru_bx, lru_lambda, hg_lb_logits, hg_norm_g, w_branch_a, w_branch_b, w_out, norm_post_mix, norm_pre_ffn, w_up, conv_f_w, conv_f_b, w_down, norm_post_ffn, loss_target, m_norm_pre_mix, m_w_in, m_conv_a_w, m_conv_a_b, m_lru_wa, m_lru_ba, m_lru_wx, m_lru_bx, m_lru_lambda, m_hg_lb_logits, m_hg_norm_g, m_w_branch_a, m_w_branch_b, m_w_out, m_norm_post_mix, m_norm_pre_ffn, m_w_up, m_conv_f_w, m_conv_f_b, m_w_down, m_norm_post_ffn, v_norm_pre_mix, v_w_in, v_conv_a_w, v_conv_a_b, v_lru_wa, v_lru_ba, v_lru_wx, v_lru_bx, v_lru_lambda, v_hg_lb_logits, v_hg_norm_g, v_w_branch_a, v_w_branch_b, v_w_out, v_norm_post_mix, v_norm_pre_ffn, v_w_up, v_conv_f_w, v_conv_f_b, v_w_down, v_norm_post_ffn):
    rest = (norm_pre_mix, w_in, conv_a_w, conv_a_b, lru_wa, lru_ba, lru_wx, lru_bx, lru_lambda, hg_lb_logits, hg_norm_g, w_branch_a, w_branch_b, w_out, norm_post_mix, norm_pre_ffn, w_up, conv_f_w, conv_f_b, w_down, norm_post_ffn, loss_target, m_norm_pre_mix, m_w_in, m_conv_a_w, m_conv_a_b, m_lru_wa, m_lru_ba, m_lru_wx, m_lru_bx, m_lru_lambda, m_hg_lb_logits, m_hg_norm_g, m_w_branch_a, m_w_branch_b, m_w_out, m_norm_post_mix, m_norm_pre_ffn, m_w_up, m_conv_f_w, m_conv_f_b, m_w_down, m_norm_post_ffn, v_norm_pre_mix, v_w_in, v_conv_a_w, v_conv_a_b, v_lru_wa, v_lru_ba, v_lru_wx, v_lru_bx, v_lru_lambda, v_hg_lb_logits, v_hg_norm_g, v_w_branch_a, v_w_branch_b, v_w_out, v_norm_post_mix, v_norm_pre_ffn, v_w_up, v_conv_f_w, v_conv_f_b, v_w_down, v_norm_post_ffn)
    w_in_args = dict(zip(WEIGHTS, rest[:NW]))
    loss_target = rest[NW]
    m_args = dict(zip(WEIGHTS, rest[NW + 1:2 * NW + 1]))
    v_args = dict(zip(WEIGHTS, rest[2 * NW + 1:3 * NW + 1]))
    shape_of = {n: w_in_args[n].shape for n in WEIGHTS}

    def two_d(n, a):
        if n in BIG:
            return a.reshape(BIG_SHAPE[n])
        if n in ("lru_wa", "lru_wx"):
            return a.reshape(NH, HD, HD)
        return a.reshape(a.shape[-2:])

    w2 = {n: two_d(n, w_in_args[n]) for n in WEIGHTS}
    m2 = {n: two_d(n, m_args[n]) for n in WEIGHTS}
    v2 = {n: two_d(n, v_args[n]) for n in WEIGHTS}

    cidx = lax.axis_index("c").astype(jnp.int32).reshape(1)
    jchip = 2 * lax.axis_index("x") + lax.axis_index("y")

    jc = jnp.stack([jchip, lax.axis_index("c")]).astype(jnp.int32)

    shards = {n: _place_shard(w2[n], jc, "place_" + n) for n in BIG}
    conv_a_s = jnp.pad(w2["conv_a_w"], ((0, 4), (0, 0)))
    conv_f_s = jnp.pad(w2["conv_f_w"], ((0, 5), (0, 0)))
    small = {n: w2[n] for n in WEIGHTS if n not in BIG and n not in ("conv_a_w", "conv_f_w")}

    grad_x, p_big, q_big, sm_g = _local_step(
        x[0], loss_target[0], shards["w_in"], [shards[n] for n in REST], conv_a_s, conv_f_s, small, jc, cidx)

    s1, s2 = _pack_small(sm_g)
    s3 = jnp.concatenate([sm_g["lru_wa"].reshape(D, HD), sm_g["lru_wx"].reshape(D, HD)], axis=0)
    _, (rs1, rs2, rs3) = _reduce_stage1({}, (), (s1, s2, s3), "reduce_d2d_in_small")
    ps1, ps2, ps3 = _add(s1, rs1, "add_s1"), _add(s2, rs2, "add_s2"), _add(s3, rs3, "add_s3")
    qs1, qs2, qs3 = _reduce_stage2(ps1, ps2, ps3)
    f_big = {n: _sum_chips(q_big[n], p_big[n], jc, "sum_chips_" + n) for n in BIG}
    fs1 = _sum_chips(qs1, ps1, jc, "sum_chips_s1")
    fs2 = _sum_chips(qs2, ps2, jc, "sum_chips_s2", by_cols=True)
    fs3 = _sum_chips(qs3, ps3, jc, "sum_chips_s3")
    g_big, gs1, gs2, gs3 = _reduce_stage3(f_big, fs1, fs2, fs3)

    res = {}
    for n in BIG:
        d, mn, vn = _adam(w2[n], g_big[n], m2[n], v2[n], "adam_" + n)
        res[n] = (g_big[n], d, mn, vn)
    small_res = _adam_small(gs1, gs2, gs3.reshape(2, NH, HD, HD), w2, m2, v2)
    for n, r0, nr in VEC_ROWS:
        res[n] = (gs1[r0:r0 + nr],) + small_res[n]
    res["conv_f_b"] = (gs2[0:1],) + small_res["conv_f_b"]
    res["lru_wa"] = (gs3[0:D].reshape(NH, HD, HD),) + small_res["lru_wa"]
    res["lru_wx"] = (gs3[D:2 * D].reshape(NH, HD, HD),) + small_res["lru_wx"]
    g_ca = lax.dynamic_slice_in_dim(gs1[ROW_CONV_A:ROW_CONV_A + 4], jchip * (D // NCHIP), D // NCHIP, axis=1)
    g_cf = lax.dynamic_slice_in_dim(gs2[1:4], jchip * SH_UP, SH_UP, axis=1)
    res["conv_a_w"] = (g_ca,) + _adam(w2["conv_a_w"], g_ca, m2["conv_a_w"], v2["conv_a_w"], "adam_conv_a_w")
    res["conv_f_w"] = (g_cf,) + _adam(w2["conv_f_w"], g_cf, m2["conv_f_w"], v2["conv_f_w"], "adam_conv_f_w")

    loss = (0.5 / D) * jnp.sum(gs1[ROW_LOSS])
    out = [loss, grad_x.reshape(x.shape)]
    for part in range(4):
        out += [res[n][part].reshape(shape_of[n]) for n in WEIGHTS]
    return tuple(out)
```

```python
import functools

import jax
import jax.numpy as jnp
from jax import lax
from jax.experimental import pallas as pl
from jax.experimental.pallas import tpu as pltpu

F32 = jnp.float32
BF16 = jnp.bfloat16

D = 1024
NH = 8
HD = 128
CH = 32
DFF = 2816
DUP = 2 * DFF
NCHIP = 4
SH_IN = 2 * D
SH_UP = DUP // NCHIP
SH_DN = DFF // NCHIP
SH_BR = D // NCHIP
EPS = 1e-6
LRU_C = 8.0
ADAM_LR = 0.001
ADAM_B1 = 0.9
ADAM_B2 = 0.999
ADAM_EPS = 1e-08
ADAM_WD = 0.01
ADAM_STEP = 10
VMEM_BIG = 56 * 1024 * 1024
MESH = pl.DeviceIdType.MESH

SLOT_A, SLOT_B, SLOT_C, SLOT_G = 2, 0, 1, 3


def _slot_of_chip(s):
    return jnp.where(s == 3, 3, (s + 2) % 3)


def _params(sem, vmem=None):
    return pltpu.CompilerParams(dimension_semantics=sem, vmem_limit_bytes=vmem)


_GC = 0.7978845608028654
_GA = 0.044715


def _gelu(x):
    return 0.5 * x * (1.0 + jnp.tanh(_GC * (x + _GA * x * x * x)))


def _gelu_and_grad(x):
    x2 = x * x
    th = jnp.tanh(_GC * x * (1.0 + _GA * x2))
    g = 0.5 * x * (1.0 + th)
    dg = 0.5 * (1.0 + th) + 0.5 * x * (1.0 - th * th) * _GC * (1.0 + 3.0 * _GA * x2)
    return g, dg


def _sig(x):
    return jax.nn.sigmoid(x)


def _dot(a, b):
    return jnp.dot(a, b, preferred_element_type=F32)


def _dot_nt(a, b):
    return lax.dot_general(a, b, (((1,), (1,)), ((), ())), preferred_element_type=F32)


def _dot_tn(a, b):
    return lax.dot_general(a, b, (((0,), (0,)), ((), ())), preferred_element_type=F32)


def _chunk_cumsum(x):
    pos = lax.broadcasted_iota(jnp.int32, (x.shape[0], 1), 0) & (CH - 1)
    d = 1
    while d < CH:
        x = x + jnp.where(pos >= d, pltpu.roll(x, d, 0), 0.0)
        d *= 2
    return x


def _chunk_revcumsum(x):
    n = x.shape[0]
    pos = lax.broadcasted_iota(jnp.int32, (n, 1), 0) & (CH - 1)
    d = 1
    while d < CH:
        x = x + jnp.where(pos < CH - d, pltpu.roll(x, n - d, 0), 0.0)
        d *= 2
    return x


def _chunk_last(x):
    n = x.shape[0]
    return jnp.concatenate(
        [jnp.broadcast_to(x[c * CH + CH - 1:c * CH + CH, :], (CH, x.shape[1])) for c in range(n // CH)], axis=0)


def _chunk_total(x):
    n = x.shape[0]
    return jnp.concatenate(
        [jnp.broadcast_to(jnp.sum(x[c * CH:(c + 1) * CH, :], axis=0, keepdims=True), (CH, x.shape[1]))
         for c in range(n // CH)], axis=0)


def _rms_stats(x):
    r = lax.rsqrt(jnp.mean(x * x, axis=-1, keepdims=True) + EPS)
    return r, x * r


def _rms_bwd(gd, n, r):
    return r * (gd - n * jnp.mean(gd * n, axis=-1, keepdims=True))


def _shift_rows(x, d, fill):
    rows = lax.broadcasted_iota(jnp.int32, (x.shape[0], 1), 0)
    return jnp.where(rows >= d, pltpu.roll(x, d, 0), fill)


def _scan_down(a, u, carry):
    n = a.shape[0]
    pos = lax.broadcasted_iota(jnp.int32, (n, 1), 0) & 7
    for d in (1, 2, 4):
        u = a * jnp.where(pos >= d, pltpu.roll(u, d, 0), 0.0) + u
        a = a * jnp.where(pos >= d, pltpu.roll(a, d, 0), 1.0)
    out = []
    for v in range(n // 8):
        h = a[v * 8:v * 8 + 8, :] * carry + u[v * 8:v * 8 + 8, :]
        carry = h[7:8, :]
        out.append(h)
    return jnp.concatenate(out, axis=0)


def _scan_up(b, g, carry):
    n = b.shape[0]
    pos = lax.broadcasted_iota(jnp.int32, (n, 1), 0) & 7
    for d in (1, 2, 4):
        g = g + b * jnp.where(pos < 8 - d, pltpu.roll(g, n - d, 0), 0.0)
        b = b * jnp.where(pos < 8 - d, pltpu.roll(b, n - d, 0), 1.0)
    out = [None] * (n // 8)
    for v in reversed(range(n // 8)):
        h = g[v * 8:v * 8 + 8, :] + b[v * 8:v * 8 + 8, :] * carry
        carry = h[0:1, :]
        out[v] = h
    return jnp.concatenate(out, axis=0)


def _shift_rows_up(x, d, fill):
    n = x.shape[0]
    rows = lax.broadcasted_iota(jnp.int32, (n, 1), 0)
    return jnp.where(rows < n - d, pltpu.roll(x, n - d, 0), fill)


def _mm_nn_sharded(a, b3, out_dtype, tm, name, slot_fn=None):
    m, k = a.shape
    s, _, ns = b3.shape

    def body(a_ref, b_ref, o_ref):
        o_ref[...] = _dot(a_ref[...], b_ref[...]).astype(out_dtype)

    if slot_fn is None:
        out_shape = jax.ShapeDtypeStruct((m, s * ns), out_dtype)
        out_spec = pl.BlockSpec((tm, ns), lambda j, i: (i, j))
    else:
        out_shape = jax.ShapeDtypeStruct((s, m, ns), out_dtype)
        out_spec = pl.BlockSpec((None, tm, ns), lambda j, i: (slot_fn(j), i, 0))
    return pl.pallas_call(
        body, name=name, out_shape=out_shape, grid=(s, m // tm),
        in_specs=[pl.BlockSpec((tm, k), lambda j, i: (i, 0)),
                  pl.BlockSpec((None, k, ns), lambda j, i: (j, 0, 0))],
        out_specs=out_spec,
        compiler_params=_params(("parallel", "parallel"), VMEM_BIG),
    )(a, b3)


def _peer_of_step(s):
    return ((s & 1) << 1) | (s >> 1)


def _mm_in_gather(xin, gain, stk_w_in, conv_a_s, conv_f_s, jc, tm):
    m, k = xin.shape
    s_n, _, ns = stk_w_in.shape
    mt = m // tm

    def body(jc_ref, x_ref, g_ref, w_in_any, ca_src, cf_src, o_ref, h_ref, w_full, ca_dst, cf_dst,
             wbuf, hbuf, ssem_w, rsem_w, fssem_w, frsem_w, csend, crecv, lsem, wsem):
        del w_in_any
        s, i = pl.program_id(0), pl.program_id(1)
        x, y, c, j, chips = _place()
        sends_w, arrive_w, fwds_w, farrive_w = _gather_copies([w_full], ("w_in",), ssem_w, rsem_w, fssem_w, frsem_w)
        conv = ((ca_src, ca_dst), (cf_src, cf_dst))
        locs = [pltpu.make_async_copy(src, dst.at[j], lsem.at[n]) for n, (src, dst) in enumerate(conv)]
        csends = [_remote(src, dst.at[j], csend.at[3 * n + kk], crecv.at[3 * n + kk], (cx, cy, c))
                  for n, (src, dst) in enumerate(conv) for kk, (cx, cy) in enumerate(chips)]

        def fetch(step):
            return pltpu.make_async_copy(w_full.at[j ^ _peer_of_step(step)], wbuf.at[step & 1], wsem.at[step & 1])

        @pl.when((s == 0) & (i == 0))
        def _():
            for cp in sends_w[0:2] + locs + csends:
                cp.start()
            fetch(s).start()

        @pl.when(i == 0)
        def _():
            fetch(s).wait()

        @pl.when(s == 0)
        def _():
            _, n = _rms_stats(x_ref[...])
            h = (n * g_ref[...]).astype(BF16)
            hbuf[i] = h
            h_ref[...] = h

        o_ref[...] = _dot(hbuf[i], wbuf[s & 1])

        @pl.when((s == 0) & (i == mt - 1))
        def _():
            for kk in (0, 1):
                arrive_w[kk].wait_recv()
                fwds_w[kk].start()
            sends_w[2].start()

        @pl.when((s == 1) & (i == mt - 1))
        def _():
            arrive_w[2].wait_recv()
            fwds_w[2].start()

        for kk in range(3):
            @pl.when((s == kk) & (i == mt - 1))
            def _(kk=kk):
                farrive_w[kk].wait_recv()
                fetch(s + 1).start()

        @pl.when((s == s_n - 1) & (i == mt - 1))
        def _():
            for n, (_, dst) in enumerate(conv):
                for kk, (cx, cy) in enumerate(chips):
                    got = dst.at[2 * cx + cy]
                    _remote(got, got, csend.at[3 * n + kk], crecv.at[3 * n + kk], (cx, cy, c)).wait_recv()
            for cp in sends_w + fwds_w + csends:
                cp.wait_send()
            for cp in locs:
                cp.wait()

    any_spec = pl.BlockSpec(memory_space=pl.ANY)
    sem = pltpu.SemaphoreType.DMA
    grid_spec = pltpu.PrefetchScalarGridSpec(
        num_scalar_prefetch=1, grid=(s_n, mt),
        in_specs=[pl.BlockSpec((tm, k), lambda s, i, jc_ref: (jnp.where(s == 0, i, mt - 1), 0)),
                  pl.BlockSpec((1, k), lambda s, i, jc_ref: (0, 0)), any_spec, any_spec, any_spec],
        out_specs=(pl.BlockSpec((None, tm, ns),
                                lambda s, i, jc_ref: (_slot_of_chip(jc_ref[0] ^ _peer_of_step(s)), i, 0)),
                   pl.BlockSpec((tm, k), lambda s, i, jc_ref: (jnp.where(s == 0, i, mt - 1), 0)),
                   any_spec, any_spec, any_spec),
        scratch_shapes=[pltpu.VMEM((2, k, ns), BF16), pltpu.VMEM((mt, tm, k), BF16),
                        sem((3,)), sem((3,)), sem((3,)), sem((3,)), sem((6,)), sem((6,)), sem((2,)), sem((2,))])
    return pl.pallas_call(
        body, name="mm_in", grid_spec=grid_spec,
        out_shape=(jax.ShapeDtypeStruct((s_n, m, ns), F32), jax.ShapeDtypeStruct((m, k), BF16),
                   jax.ShapeDtypeStruct(stk_w_in.shape, BF16))
        + tuple(jax.ShapeDtypeStruct((NCHIP,) + v.shape, v.dtype) for v in (conv_a_s, conv_f_s)),
        input_output_aliases={3: 2},
        compiler_params=pltpu.CompilerParams(dimension_semantics=("arbitrary", "arbitrary"),
                                             vmem_limit_bytes=VMEM_BIG, has_side_effects=True),
    )(jc, xin, gain, stk_w_in, conv_a_s, conv_f_s)


def _mm_dh1_exchange(dp4, w_in3, p_w_in, x, dx1, gain, tm):
    s, k, ns = w_in3.shape
    m = dp4.shape[1]
    mt = m // tm

    def body(a_ref, b_ref, p_ref, x_ref, dx1_ref, g_ref, o_ref, q_ref, dg_ref, ssem, rsem):
        i, j = pl.program_id(0), pl.program_id(1)
        sends, arrive = _exchange_copies([q_ref], [lambda chip: p_ref.at[chip]], ssem, rsem)

        @pl.when((i == 0) & (j == 0))
        def _():
            for cp in sends:
                cp.start()

        @pl.when(j == 0)
        def _():
            o_ref[...] = _dot_nt(a_ref[...], b_ref[...])

        @pl.when(j > 0)
        def _():
            o_ref[...] += _dot_nt(a_ref[...], b_ref[...])

        @pl.when(j == s - 1)
        def _():
            dh = o_ref[...]
            r, n = _rms_stats(x_ref[...])
            o_ref[...] = dx1_ref[...] + _rms_bwd(dh * g_ref[...], n, r)
            dgv = jnp.sum(dh * n, axis=0, keepdims=True)

            @pl.when(i == 0)
            def _():
                dg_ref[...] = dgv

            @pl.when(i > 0)
            def _():
                dg_ref[...] += dgv

        @pl.when((i == mt - 1) & (j == s - 1))
        def _():
            for cp in arrive:
                cp.wait_recv()
            for cp in sends:
                cp.wait_send()

    any_spec = pl.BlockSpec(memory_space=pl.ANY)
    row_tile = pl.BlockSpec((tm, k), lambda i, j: (i, 0))
    vec = pl.BlockSpec((1, k), lambda i, j: (0, 0))
    return pl.pallas_call(
        body, name="mm_dh1",
        out_shape=(jax.ShapeDtypeStruct((m, k), F32), jax.ShapeDtypeStruct(p_w_in.shape, BF16),
                   jax.ShapeDtypeStruct((1, k), F32)),
        grid=(mt, s),
        in_specs=[pl.BlockSpec((None, tm, ns), lambda i, j: (_slot_of_chip(j), i, 0)),
                  pl.BlockSpec((None, k, ns), lambda i, j: (j, 0, 0)), any_spec, row_tile, row_tile, vec],
        out_specs=(row_tile, any_spec, vec),
        scratch_shapes=[pltpu.SemaphoreType.DMA((3,)), pltpu.SemaphoreType.DMA((3,))],
        compiler_params=pltpu.CompilerParams(dimension_semantics=("arbitrary", "arbitrary"),
                                             vmem_limit_bytes=VMEM_BIG, has_side_effects=True),
    )(dp4, w_in3, p_w_in, x, dx1, gain)


def _mm_nt_sharded(a, b3, tm, name, stacked_slot_fn=None):
    s, k, ns = b3.shape
    m = a.shape[1] if stacked_slot_fn is not None else a.shape[0]

    def body(a_ref, b_ref, o_ref):
        j = pl.program_id(1)
        @pl.when(j == 0)
        def _():
            o_ref[...] = _dot_nt(a_ref[...], b_ref[...])

        @pl.when(j > 0)
        def _():
            o_ref[...] += _dot_nt(a_ref[...], b_ref[...])

    if stacked_slot_fn is None:
        a_spec = pl.BlockSpec((tm, ns), lambda i, j: (i, j))
    else:
        a_spec = pl.BlockSpec((None, tm, ns), lambda i, j: (stacked_slot_fn(j), i, 0))
    return pl.pallas_call(
        body, name=name, out_shape=jax.ShapeDtypeStruct((m, k), F32), grid=(m // tm, s),
        in_specs=[a_spec, pl.BlockSpec((None, k, ns), lambda i, j: (j, 0, 0))],
        out_specs=pl.BlockSpec((tm, k), lambda i, j: (i, 0)),
        compiler_params=_params(("parallel", "arbitrary"), VMEM_BIG),
    )(a, b3)


def _mm_nt(a, b, out_dtype, tm, name):
    m, k = a.shape
    n = b.shape[0]

    def body(a_ref, b_ref, o_ref):
        o_ref[...] = _dot_nt(a_ref[...], b_ref[...]).astype(out_dtype)

    return pl.pallas_call(
        body, name=name, out_shape=jax.ShapeDtypeStruct((m, n), out_dtype), grid=(m // tm,),
        in_specs=[pl.BlockSpec((tm, k), lambda i: (i, 0)), pl.BlockSpec((n, k), lambda i: (0, 0))],
        out_specs=pl.BlockSpec((tm, n), lambda i: (i, 0)),
        compiler_params=_params(("parallel",), VMEM_BIG),
    )(a, b)


def _mm_tn(a, g, tkk, tn, tk, name, stacked_slot_fn=None, stacked_out=False):
    m, k = a.shape
    if stacked_slot_fn is not None:
        n = g.shape[0] * g.shape[2]
        g_spec = pl.BlockSpec((None, tk, tn), lambda kk, j, mm: (stacked_slot_fn(j), mm, 0))
    else:
        n = g.shape[1]
        g_spec = pl.BlockSpec((tk, tn), lambda kk, j, mm: (mm, j))
    steps = m // tk

    def body(a_ref, g_ref, o_ref, acc_ref):
        mm = pl.program_id(2)

        @pl.when(mm == 0)
        def _():
            acc_ref[...] = _dot_tn(a_ref[...], g_ref[...])

        @pl.when(mm > 0)
        def _():
            acc_ref[...] += _dot_tn(a_ref[...], g_ref[...])

        @pl.when(mm == steps - 1)
        def _():
            o_ref[...] = acc_ref[...].astype(BF16)

    if stacked_out:
        out_shape = jax.ShapeDtypeStruct((n // tn, k, tn), BF16)
        out_spec = pl.BlockSpec((None, tkk, tn), lambda kk, j, mm: (j, kk, 0))
    else:
        out_shape = jax.ShapeDtypeStruct((k, n), BF16)
        out_spec = pl.BlockSpec((tkk, tn), lambda kk, j, mm: (kk, j))
    return pl.pallas_call(
        body, name=name, out_shape=out_shape, grid=(k // tkk, n // tn, steps),
        in_specs=[pl.BlockSpec((tk, tkk), lambda kk, j, mm: (mm, kk)), g_spec],
        out_specs=out_spec,
        scratch_shapes=[pltpu.VMEM((tkk, tn), F32)],
        compiler_params=_params(("parallel", "parallel", "arbitrary"), VMEM_BIG),
    )(a, g)


def _lru_gates(xc, wa_ref, wx_ref, ba, bx, lam):
    xcb = xc.astype(BF16)
    ra = jnp.concatenate([_dot(xcb[:, n * HD:(n + 1) * HD], wa_ref[n]) for n in range(NH)], axis=1) + ba
    ix = jnp.concatenate([_dot(xcb[:, n * HD:(n + 1) * HD], wx_ref[n]) for n in range(NH)], axis=1) + bx
    r = _sig(ra)
    ig = _sig(ix)
    z = -lam
    sp = jnp.maximum(z, 0.0) + jnp.log1p(jnp.exp(-jnp.abs(z)))
    log_a = -LRU_C * r * sp
    a = jnp.exp(log_a)
    z2 = 2.0 * log_a
    series = -z2 * (1.0 + z2 * (0.5 + z2 * (1.0 / 6.0 + z2 * (1.0 / 24.0))))
    om = jnp.where(z2 > -0.02, series, 1.0 - jnp.exp(z2))
    mult = jnp.sqrt(om)
    return xcb, r, ig, sp, a, mult


def _mixer_a_fwd(p4, cw, cb, wa, wx, ba, bx, lam, stk, names, tt):
    t = p4.shape[1]
    ng = len(stk)

    def body(p_ref, cw_ref, cb_ref, wa_ref, wx_ref, ba_ref, bx_ref, lam_ref, *rest):
        ya_ref, h_ref, sv_ref = rest[ng:ng + 3]
        halo, hc, ssem, rsem = rest[2 * ng + 3:]
        i = pl.program_id(0)
        finish = _ici_leg_behind(rest[ng + 3:2 * ng + 3], names, ssem, rsem, i == 0, lambda: i == t // tt - 1)

        @pl.when(i == 0)
        def _():
            halo[...] = jnp.zeros((8, D), F32)
            hc[...] = jnp.zeros((1, D), F32)

        xa = p_ref[:, 0:D]
        ga = p_ref[:, D:2 * D]
        xe = jnp.concatenate([halo[...], xa], axis=0)
        xc = (cb_ref[...] + cw_ref[3:4, :] * xe
              + sum(cw_ref[3 - s:4 - s, :] * pltpu.roll(xe, s, 0) for s in (1, 2, 3)))[8:, :]
        halo[...] = xa[tt - 8:, :]
        _, r, ig, _, a, mult = _lru_gates(xc, wa_ref, wx_ref, ba_ref[...], bx_ref[...], lam_ref[...])
        u = mult * ig * xc
        h = _scan_down(a, u, hc[...])
        hc[...] = h[tt - 1:tt, :]
        h_ref[...] = h
        ya_ref[...] = (h * _gelu(ga)).astype(BF16)
        for idx, val in enumerate((xc, r, ig, a, mult)):
            sv_ref[idx] = val
        finish()

    full = lambda shape: pl.BlockSpec(shape, lambda i: (0,) * len(shape))
    any_spec = pl.BlockSpec(memory_space=pl.ANY)
    outs = pl.pallas_call(
        body, name="mixer_a_fwd",
        out_shape=(jax.ShapeDtypeStruct((t, D), BF16), jax.ShapeDtypeStruct((t, D), F32),
                   jax.ShapeDtypeStruct((5, t, D), F32)) + tuple(jax.ShapeDtypeStruct(v.shape, v.dtype) for v in stk),
        grid=(t // tt,),
        in_specs=[pl.BlockSpec((None, tt, 2 * D), lambda i: (SLOT_A, i, 0)),
                  full((4, D)), full((1, D)), full((NH, HD, HD)), full((NH, HD, HD)),
                  full((1, D)), full((1, D)), full((1, D))] + [any_spec] * ng,
        out_specs=(pl.BlockSpec((tt, D), lambda i: (i, 0)), pl.BlockSpec((tt, D), lambda i: (i, 0)),
                   pl.BlockSpec((5, tt, D), lambda i: (0, i, 0))) + (any_spec,) * ng,
        scratch_shapes=[pltpu.VMEM((8, D), F32), pltpu.VMEM((1, D), F32),
                        pltpu.SemaphoreType.DMA((3 * ng,)), pltpu.SemaphoreType.DMA((3 * ng,))],
        input_output_aliases={8 + w: 3 + w for w in range(ng)},
        compiler_params=pltpu.CompilerParams(dimension_semantics=("arbitrary",), vmem_limit_bytes=VMEM_BIG,
                                             has_side_effects=True),
    )(p4, cw, cb, wa, wx, ba, bx, lam, *stk)
    return outs[0], outs[1], outs[2], list(outs[3:])


def _chunk_masks(tt):
    row = lax.broadcasted_iota(jnp.int32, (tt, tt), 0)
    col = lax.broadcasted_iota(jnp.int32, (tt, tt), 1)
    same = jnp.right_shift(row, 5) == jnp.right_shift(col, 5)
    return same & (col <= row)


def _hg_head_fwd(q, fz, lbh, saved=None):
    sgn = _sig(-fz)
    k = (1.0 - lbh) * sgn
    if saved is None:
        sg = _sig(fz)
        f = lbh + (1.0 - lbh) * sg
        g = _chunk_cumsum(jnp.log(f))
    else:
        sg, g = saved
        f = lbh + (1.0 - lbh) * sg
    gu = _chunk_last(g) - g
    eg = jnp.exp(g)
    eng = jnp.exp(-g)
    egu = jnp.exp(gu)
    qt = q * eg
    kt = k * eng
    kd = k * egu
    return sg, sgn, f, k, g, eg, eng, egu, qt, kt, kd


def _lb_of(logits_ref):
    return _sig(logits_ref[0:1, :] - logits_ref[1:2, :])


def _hgrn2_fwd(p4, logits, gnorm, stk, names, tt):
    t = p4.shape[1]
    nc = tt // CH
    ng = len(stk)

    def body(p_ref, lg_ref, gn_ref, *rest):
        yb_ref, o_ref, ss_ref, sv_ref = rest[ng:ng + 4]
        st, ssem, rsem = rest[2 * ng + 4:]
        i = pl.program_id(0)
        finish = _ici_leg_behind(rest[ng + 4:2 * ng + 4], names, ssem, rsem, i == 0, lambda: i == t // tt - 1)

        @pl.when(i == 0)
        def _():
            st[...] = jnp.zeros((NH, HD, HD), F32)

        low = _chunk_masks(tt)
        lb = _lb_of(lg_ref)
        heads = [slice(h * HD, (h + 1) * HD) for h in range(NH)]
        sg, _, _, _, g, _, _, _, qt, kt, kd = _hg_head_fwd(p_ref[0, :, 0:D], p_ref[0, :, D:2 * D], lb)
        sv_ref[0] = sg
        sv_ref[1] = g
        qtb, ktb, kdb, vb = qt.astype(BF16), kt.astype(BF16), kd.astype(BF16), p_ref[1, :, 0:D].astype(BF16)
        decs = [jnp.exp(g[c * CH + CH - 1:c * CH + CH, :]) for c in range(nc)]
        o_in = []
        for hs in heads:
            att = jnp.where(low, _dot_nt(qtb[:, hs], ktb[:, hs]), 0.0)
            o_in.append(_dot(att.astype(BF16), vb[:, hs]))
        s_t = [st[h] for h in range(NH)]
        pieces = [[None] * nc for _ in range(NH)]
        for c in range(nc):
            sl = slice(c * CH, (c + 1) * CH)
            for h, hs in enumerate(heads):
                s_bf = s_t[h].astype(BF16)
                ss_ref[c, h] = s_bf
                pieces[h][c] = o_in[h][sl] + _dot_nt(qtb[sl, hs], s_bf)
                s_t[h] = s_t[h] * decs[c][:, hs] + _dot_tn(vb[sl, hs], kdb[sl, hs])
        for h, hs in enumerate(heads):
            st[h] = s_t[h]
            o = jnp.concatenate(pieces[h], axis=0)
            _, n = _rms_stats(o)
            og = p_ref[1, :, D + h * HD:D + (h + 1) * HD]
            o_ref[:, hs] = o
            yb_ref[:, hs] = (n * gn_ref[:, hs] * (og * _sig(og))).astype(BF16)
        finish()

    any_spec = pl.BlockSpec(memory_space=pl.ANY)
    outs = pl.pallas_call(
        body, name="hgrn2_fwd",
        out_shape=(jax.ShapeDtypeStruct((t, D), BF16), jax.ShapeDtypeStruct((t, D), F32),
                   jax.ShapeDtypeStruct((t // CH, NH, HD, HD), BF16), jax.ShapeDtypeStruct((2, t, D), F32))
        + tuple(jax.ShapeDtypeStruct(v.shape, v.dtype) for v in stk),
        grid=(t // tt,),
        in_specs=[pl.BlockSpec((2, tt, 2 * D), lambda i: (0, i, 0)),
                  pl.BlockSpec((2, D), lambda i: (0, 0)), pl.BlockSpec((1, D), lambda i: (0, 0))] + [any_spec] * ng,
        out_specs=(pl.BlockSpec((tt, D), lambda i: (i, 0)), pl.BlockSpec((tt, D), lambda i: (i, 0)),
                   pl.BlockSpec((nc, NH, HD, HD), lambda i: (i, 0, 0, 0)),
                   pl.BlockSpec((2, tt, D), lambda i: (0, i, 0))) + (any_spec,) * ng,
        scratch_shapes=[pltpu.VMEM((NH, HD, HD), F32), pltpu.SemaphoreType.DMA((3 * ng,)),
                        pltpu.SemaphoreType.DMA((3 * ng,))],
        input_output_aliases={3 + w: 4 + w for w in range(ng)},
        compiler_params=pltpu.CompilerParams(dimension_semantics=("arbitrary",), vmem_limit_bytes=VMEM_BIG,
                                             has_side_effects=True),
    )(p4, logits, gnorm, *stk)
    return outs[0], outs[1], outs[2], outs[3], list(outs[4:])


def _mid_fwd(ya, yb, p4, x, wa, wb, wo, g_pm, g_pf, tt):
    t = x.shape[0]

    def body(ya_ref, yb_ref, gt_ref, x_ref, wa_ref, wb_ref, wo_ref, gpm_ref, gpf_ref,
             za_ref, zb_ref, mix_ref, m2_ref, x1_ref, h2_ref):
        za = _dot(ya_ref[...], wa_ref[...])
        zb = _dot(yb_ref[...], wb_ref[...])
        mix = _sig(gt_ref[:, 0:D]) * za + _sig(gt_ref[:, D:2 * D]) * zb
        mixb = mix.astype(BF16)
        m2 = _dot(mixb, wo_ref[...])
        _, n2 = _rms_stats(m2)
        x1 = x_ref[...] + n2 * gpm_ref[...]
        _, n1 = _rms_stats(x1)
        za_ref[...] = za.astype(BF16)
        zb_ref[...] = zb.astype(BF16)
        mix_ref[...] = mixb
        m2_ref[...] = m2
        x1_ref[...] = x1
        h2_ref[...] = (n1 * gpf_ref[...]).astype(BF16)

    row = lambda dt: jax.ShapeDtypeStruct((t, D), dt)
    tile = pl.BlockSpec((tt, D), lambda i: (i, 0))
    wsp = pl.BlockSpec((D, D), lambda i: (0, 0))
    vec = pl.BlockSpec((1, D), lambda i: (0, 0))
    return pl.pallas_call(
        body, name="mid_fwd",
        out_shape=(row(BF16), row(BF16), row(BF16), row(F32), row(F32), row(BF16)),
        grid=(t // tt,),
        in_specs=[tile, tile, pl.BlockSpec((None, tt, 2 * D), lambda i: (SLOT_G, i, 0)), tile,
                  wsp, wsp, wsp, vec, vec],
        out_specs=(tile,) * 6,
        compiler_params=_params(("parallel",), VMEM_BIG),
    )(ya, yb, p4, x, wa, wb, wo, g_pm, g_pf)


def _up_act_fwd(h2, w_up4, cfw, cfb, tm):
    t = h2.shape[0]
    ns = SH_UP

    def body(a_ref, ah_ref, wg_ref, wv_ref, cwg_ref, cwv_ref, cbg_ref, cbv_ref,
             pg_ref, pv_ref, y_ref, uv_ref, gl_ref, dgl_ref):
        i = pl.program_id(1)
        rows = jnp.concatenate([ah_ref[...], a_ref[...]], axis=0)
        ups = []
        for w_ref, cw_ref, cb_ref, pre_ref in ((wg_ref, cwg_ref, cbg_ref, pg_ref), (wv_ref, cwv_ref, cbv_ref, pv_ref)):
            pre = _dot(rows, w_ref[...])
            pre_ref[...] = pre[16:, :]
            xe = jnp.concatenate([jnp.where(i > 0, pre[8:16, :], 0.0), pre[16:, :]], axis=0)
            up = (cb_ref[...] + cw_ref[2:3, :] * xe + cw_ref[1:2, :] * pltpu.roll(xe, 1, 0)
                  + cw_ref[0:1, :] * pltpu.roll(xe, 2, 0))
            ups.append(up[8:, :])
        gl, dgl = _gelu_and_grad(ups[0])
        y_ref[...] = (gl * ups[1]).astype(BF16)
        uv_ref[...] = ups[1].astype(BF16)
        gl_ref[...] = gl.astype(BF16)
        dgl_ref[...] = dgl.astype(BF16)

    hb = tm // 16
    tile = pl.BlockSpec((tm, ns), lambda p, i: (i, p))
    return pl.pallas_call(
        body, name="up_act_fwd",
        out_shape=(jax.ShapeDtypeStruct((t, DFF), F32),) * 2 + (jax.ShapeDtypeStruct((t, DFF), BF16),) * 4,
        grid=(2, t // tm),
        in_specs=[pl.BlockSpec((tm, D), lambda p, i: (i, 0)),
                  pl.BlockSpec((16, D), lambda p, i: (jnp.maximum(i * hb - 1, 0), 0)),
                  pl.BlockSpec((None, D, ns), lambda p, i: (p, 0, 0)),
                  pl.BlockSpec((None, D, ns), lambda p, i: (p + 2, 0, 0)),
                  pl.BlockSpec((3, ns), lambda p, i: (0, p)), pl.BlockSpec((3, ns), lambda p, i: (0, p + 2)),
                  pl.BlockSpec((1, ns), lambda p, i: (0, p)), pl.BlockSpec((1, ns), lambda p, i: (0, p + 2))],
        out_specs=(tile,) * 6,
        compiler_params=_params(("parallel", "parallel"), VMEM_BIG),
    )(h2, h2, w_up4, w_up4, cfw, cfw, cfb, cfb)


def _down_loss(y, wdn, x1, tgt, g_post, tt):
    t = x1.shape[0]

    def body(y_ref, w_ref, x1_ref, t_ref, g_ref, dx2_ref, dm3_ref, lossv_ref, dg_ref):
        i = pl.program_id(0)
        m3 = _dot(y_ref[...], w_ref[...])
        r, n3 = _rms_stats(m3)
        g = g_ref[...]
        e = x1_ref[...] + n3 * g - t_ref[...]
        dx2 = e * (1.0 / D)
        dx2_ref[...] = dx2
        dm3_ref[...] = _rms_bwd(dx2 * g, n3, r).astype(BF16)
        lv = jnp.sum(e * e, axis=0, keepdims=True)
        dgv = jnp.sum(dx2 * n3, axis=0, keepdims=True)

        @pl.when(i == 0)
        def _():
            lossv_ref[...] = lv
            dg_ref[...] = dgv

        @pl.when(i > 0)
        def _():
            lossv_ref[...] += lv
            dg_ref[...] += dgv

    tile = pl.BlockSpec((tt, D), lambda i: (i, 0))
    vec = pl.BlockSpec((1, D), lambda i: (0, 0))
    return pl.pallas_call(
        body, name="down_loss",
        out_shape=(jax.ShapeDtypeStruct((t, D), F32), jax.ShapeDtypeStruct((t, D), BF16),
                   jax.ShapeDtypeStruct((1, D), F32), jax.ShapeDtypeStruct((1, D), F32)),
        grid=(t // tt,),
        in_specs=[pl.BlockSpec((tt, DFF), lambda i: (i, 0)), pl.BlockSpec((DFF, D), lambda i: (0, 0)),
                  tile, tile, vec],
        out_specs=(tile, tile, vec, vec),
        compiler_params=_params(("arbitrary",), VMEM_BIG),
    )(y, wdn, x1, tgt, g_post)


def _ffn_act_bwd(dy, pre_g, pre_v, uv, gl, dgl, cfw, tt):
    t = dy.shape[0]
    nt = t // tt

    def body(dy_ref, dyn_ref, pg_ref, pv_ref, uv_ref, uvn_ref, gl_ref, gln_ref, dgl_ref, dgln_ref, cw_ref,
             du_ref, dcw_ref, dcb_ref):
        i = pl.program_id(0)
        n = tt + 8
        next_live = jnp.where(i < nt - 1, 1.0, 0.0)
        ext = lambda ref, nref: jnp.concatenate([ref[...].astype(F32), nref[...].astype(F32)[0:8, :]], axis=0)
        dy = jnp.concatenate([dy_ref[...].astype(F32), dyn_ref[...].astype(F32)[0:8, :] * next_live], axis=0)
        ds = (dy * ext(uv_ref, uvn_ref) * ext(dgl_ref, dgln_ref), dy * ext(gl_ref, gln_ref))
        dcw_parts, dcb_parts = [], []
        for hh, c0 in enumerate((0, DFF)):
            cs = slice(c0, c0 + DFF)
            dd = ds[hh]
            d1 = pltpu.roll(dd, n - 1, 0)
            d2 = pltpu.roll(dd, n - 2, 0)
            du_ref[:, cs] = (cw_ref[2:3, cs] * dd + cw_ref[1:2, cs] * d1 + cw_ref[0:1, cs] * d2)[0:tt, :].astype(BF16)
            x = (pg_ref, pv_ref)[hh][...]
            dcw_parts.append(jnp.concatenate(
                [jnp.sum(dk[0:tt, :] * x, axis=0, keepdims=True) for dk in (d2, d1, dd)], axis=0))
            dcb_parts.append(jnp.sum(dd[0:tt, :], axis=0, keepdims=True))
        dcw = jnp.concatenate(dcw_parts, axis=1)
        dcb = jnp.concatenate(dcb_parts, axis=1)

        @pl.when(i == 0)
        def _():
            dcw_ref[...] = dcw
            dcb_ref[...] = dcb

        @pl.when(i > 0)
        def _():
            dcw_ref[...] += dcw
            dcb_ref[...] += dcb

    half = pl.BlockSpec((tt, DFF), lambda i: (i, 0))
    half_next = pl.BlockSpec((16, DFF), lambda i: (jnp.minimum((i + 1) * (tt // 16), t // 16 - 1), 0))
    return pl.pallas_call(
        body, name="ffn_act_bwd",
        out_shape=(jax.ShapeDtypeStruct((t, DUP), BF16), jax.ShapeDtypeStruct((3, DUP), F32),
                   jax.ShapeDtypeStruct((1, DUP), F32)),
        grid=(nt,),
        in_specs=[half, half_next, half, half,
                  half, half_next, half, half_next, half, half_next,
                  pl.BlockSpec((3, DUP), lambda i: (0, 0))],
        out_specs=(pl.BlockSpec((tt, DUP), lambda i: (i, 0)), pl.BlockSpec((3, DUP), lambda i: (0, 0)),
                   pl.BlockSpec((1, DUP), lambda i: (0, 0))),
        compiler_params=_params(("arbitrary",), VMEM_BIG),
    )(dy, dy, pre_g, pre_v, uv, uv, gl, gl, dgl, dgl, cfw)


def _mid_bwd(dh2, dx2, x1, m2, za, zb, p4, wa, wb, wo, g_pm, g_pf, tt):
    t = x1.shape[0]

    def body(dh2_ref, dx2_ref, x1_ref, m2_ref, za_ref, zb_ref, gt_ref, wa_ref, wb_ref, wo_ref, gpm_ref, gpf_ref,
             dx1_ref, dm2_ref, dza_ref, dzb_ref, dya_ref, dyb_ref, dp_ref, dgpm_ref, dgpf_ref):
        i = pl.program_id(0)
        r1, n1 = _rms_stats(x1_ref[...])
        dh2 = dh2_ref[...]
        dx1 = dx2_ref[...] + _rms_bwd(dh2 * gpf_ref[...], n1, r1)
        r2, n2 = _rms_stats(m2_ref[...])
        dm2 = _rms_bwd(dx1 * gpm_ref[...], n2, r2).astype(BF16)
        dmix = _dot_nt(dm2, wo_ref[...])
        sa = _sig(gt_ref[:, 0:D])
        sb = _sig(gt_ref[:, D:2 * D])
        dza = (dmix * sa).astype(BF16)
        dzb = (dmix * sb).astype(BF16)
        dp_ref[:, 0:D] = (dmix * za_ref[...].astype(F32) * sa * (1.0 - sa)).astype(BF16)
        dp_ref[:, D:2 * D] = (dmix * zb_ref[...].astype(F32) * sb * (1.0 - sb)).astype(BF16)
        dx1_ref[...] = dx1
        dm2_ref[...] = dm2
        dza_ref[...] = dza
        dzb_ref[...] = dzb
        dya_ref[...] = _dot_nt(dza, wa_ref[...]).astype(BF16)
        dyb_ref[...] = _dot_nt(dzb, wb_ref[...]).astype(BF16)
        dgpf = jnp.sum(dh2 * n1, axis=0, keepdims=True)
        dgpm = jnp.sum(dx1 * n2, axis=0, keepdims=True)

        @pl.when(i == 0)
        def _():
            dgpf_ref[...] = dgpf
            dgpm_ref[...] = dgpm

        @pl.when(i > 0)
        def _():
            dgpf_ref[...] += dgpf
            dgpm_ref[...] += dgpm

    row = lambda dt: jax.ShapeDtypeStruct((t, D), dt)
    tile = pl.BlockSpec((tt, D), lambda i: (i, 0))
    wsp = pl.BlockSpec((D, D), lambda i: (0, 0))
    vec = pl.BlockSpec((1, D), lambda i: (0, 0))
    gates = pl.BlockSpec((None, tt, 2 * D), lambda i: (SLOT_G, i, 0))
    return pl.pallas_call(
        body, name="mid_bwd",
        out_shape=(row(F32), row(BF16), row(BF16), row(BF16), row(BF16), row(BF16),
                   jax.ShapeDtypeStruct((NCHIP, t, 2 * D), BF16),
                   jax.ShapeDtypeStruct((1, D), F32), jax.ShapeDtypeStruct((1, D), F32)),
        grid=(t // tt,),
        in_specs=[tile, tile, tile, tile, tile, tile, gates, wsp, wsp, wsp, vec, vec],
        out_specs=(tile, tile, tile, tile, tile, tile, gates, vec, vec),
        compiler_params=_params(("arbitrary",), VMEM_BIG),
    )(dh2, dx2, x1, m2, za, zb, p4, wa, wb, wo, g_pm, g_pf)


def _hgrn2_bwd(p4, o_all, ss, saved, dyb, dp4, logits, gnorm, p_early, tt):
    t = p4.shape[1]
    nt = t // tt
    nc = tt // CH
    ne = len(p_early)

    def body(p_ref, o_ref, ss_ref, sv_ref, dyb_ref, dp_in, lg_ref, gn_ref, *rest):
        del dp_in
        pe = rest[:ne]
        dp_ref, dlb_ref, dgn_ref = rest[ne:ne + 3]
        qe = rest[ne + 3:2 * ne + 3]
        dst, ssem, rsem = rest[2 * ne + 3:]
        i = pl.program_id(0)
        sends, arrive = _exchange_copies(qe, [(lambda chip, r=r: r.at[chip]) for r in pe], ssem, rsem)

        @pl.when(i == 0)
        def _():
            dst[...] = jnp.zeros((NH, HD, HD), F32)
            for cp in sends:
                cp.start()

        low = _chunk_masks(tt)
        lb = _lb_of(lg_ref)
        heads = [slice(h * HD, (h + 1) * HD) for h in range(NH)]
        sg, sgn, f, k, g, eg, eng, egu, qt, kt, kd = _hg_head_fwd(p_ref[0, :, 0:D], p_ref[0, :, D:2 * D], lb,
                                                                  (sv_ref[0], sv_ref[1]))
        qtb, ktb, kdb, vb = qt.astype(BF16), kt.astype(BF16), kd.astype(BF16), p_ref[1, :, 0:D].astype(BF16)
        decs = [jnp.exp(g[c * CH + CH - 1:c * CH + CH, :]) for c in range(nc)]
        og = p_ref[1, :, D:2 * D]
        so = _sig(og)
        dyb = dyb_ref[...].astype(F32)
        dob = dyb * (og * so)
        rn = [_rms_stats(o_ref[:, hs]) for hs in heads]
        r_all = jnp.concatenate([jnp.broadcast_to(r, (tt, HD)) for r, _ in rn], axis=1)
        n_all = jnp.concatenate([n for _, n in rn], axis=1)
        gd = dob * gn_ref[...]
        proj = jnp.concatenate(
            [jnp.broadcast_to(jnp.mean(gd[:, hs] * n_all[:, hs], axis=-1, keepdims=True), (tt, HD)) for hs in heads],
            axis=1)
        dob_ = (r_all * (gd - n_all * proj)).astype(BF16)
        dog = dyb * (n_all * gn_ref[...]) * (so * (1.0 + og * (1.0 - so)))
        dgn = jnp.sum(dob * n_all, axis=0, keepdims=True)
        dv_in, dqt_in, dkt_h = [], [], []
        for hs in heads:
            att = jnp.where(low, _dot_nt(qtb[:, hs], ktb[:, hs]), 0.0).astype(BF16)
            d_att = jnp.where(low, _dot_nt(dob_[:, hs], vb[:, hs]), 0.0).astype(BF16)
            dv_in.append(_dot_tn(att, dob_[:, hs]))
            dqt_in.append(_dot(d_att, ktb[:, hs]))
            dkt_h.append(_dot_tn(d_att, qtb[:, hs]))
        ds_t = [dst[h] for h in range(NH)]
        dv_p = [[None] * NH for _ in range(nc)]
        dqt_p = [[None] * NH for _ in range(nc)]
        dkd_p = [[None] * NH for _ in range(nc)]
        dgl_p = [[None] * NH for _ in range(nc)]
        for c in reversed(range(nc)):
            sl = slice(c * CH, (c + 1) * CH)
            for h, hs in enumerate(heads):
                s_prev = ss_ref[c, h]
                ds_bf = ds_t[h].astype(BF16)
                dec = decs[c][:, hs]
                dv_p[c][h] = dv_in[h][sl] + _dot_nt(kdb[sl, hs], ds_bf)
                dqt_p[c][h] = dqt_in[h][sl] + _dot(dob_[sl, hs], s_prev)
                dkd_p[c][h] = _dot(vb[sl, hs], ds_bf)
                ddec = jnp.sum(s_prev.astype(F32) * ds_t[h], axis=0, keepdims=True)
                dgl_p[c][h] = jnp.broadcast_to(ddec * dec, (CH, HD))
                ds_t[h] = ds_t[h] * dec + _dot_tn(dob_[sl, hs], qtb[sl, hs])
        for h in range(NH):
            dst[h] = ds_t[h]
        whole = lambda parts: jnp.concatenate([jnp.concatenate(row, axis=1) for row in parts], axis=0)
        dv, dqt, dkd, dgl = whole(dv_p), whole(dqt_p), whole(dkd_p), whole(dgl_p)
        dkt = jnp.concatenate(dkt_h, axis=1)
        dq = dqt * eg
        dk = dkt * eng + dkd * egu
        dg = dqt * qt - dkt * kt
        dgu = dkd * kd
        dlogf = _chunk_revcumsum(dg - dgu) + _chunk_total(dgu) + dgl
        common = sgn * (dlogf / f - dk)
        dfz = (1.0 - lb) * sg * common
        dlb = jnp.sum(common, axis=0, keepdims=True)
        dp_ref[0, :, 0:D] = dq.astype(BF16)
        dp_ref[0, :, D:2 * D] = dfz.astype(BF16)
        dp_ref[1, :, 0:D] = dv.astype(BF16)
        dp_ref[1, :, D:2 * D] = dog.astype(BF16)

        @pl.when(i == 0)
        def _():
            dlb_ref[0:1, :] = dlb
            dgn_ref[...] = dgn

        @pl.when(i > 0)
        def _():
            dlb_ref[0:1, :] += dlb
            dgn_ref[...] += dgn

        @pl.when(i == nt - 1)
        def _():
            d0 = dlb_ref[0:1, :] * lb * (1.0 - lb)
            dlb_ref[0:1, :] = d0
            dlb_ref[1:2, :] = -d0
            for cp in arrive:
                cp.wait_recv()
            for cp in sends:
                cp.wait_send()

    rev = lambda i: nt - 1 - i
    vec = pl.BlockSpec((1, D), lambda i: (0, 0))
    any_spec = pl.BlockSpec(memory_space=pl.ANY)
    outs = pl.pallas_call(
        body, name="hgrn2_bwd",
        out_shape=(jax.ShapeDtypeStruct(dp4.shape, BF16), jax.ShapeDtypeStruct((2, D), F32),
                   jax.ShapeDtypeStruct((1, D), F32)) + tuple(jax.ShapeDtypeStruct(a.shape, BF16) for a in p_early),
        grid=(nt,),
        in_specs=[pl.BlockSpec((2, tt, 2 * D), lambda i: (0, rev(i), 0)),
                  pl.BlockSpec((tt, D), lambda i: (rev(i), 0)),
                  pl.BlockSpec((nc, NH, HD, HD), lambda i: (rev(i), 0, 0, 0)),
                  pl.BlockSpec((2, tt, D), lambda i: (0, rev(i), 0)),
                  pl.BlockSpec((tt, D), lambda i: (rev(i), 0)),
                  any_spec,
                  pl.BlockSpec((2, D), lambda i: (0, 0)), vec] + [any_spec] * ne,
        out_specs=(pl.BlockSpec((2, tt, 2 * D), lambda i: (0, rev(i), 0)),
                   pl.BlockSpec((2, D), lambda i: (0, 0)), vec) + (any_spec,) * ne,
        scratch_shapes=[pltpu.VMEM((NH, HD, HD), F32), pltpu.SemaphoreType.DMA((3 * ne,)),
                        pltpu.SemaphoreType.DMA((3 * ne,))],
        input_output_aliases={5: 0},
        compiler_params=pltpu.CompilerParams(dimension_semantics=("arbitrary",), vmem_limit_bytes=VMEM_BIG,
                                             has_side_effects=True),
    )(p4, o_all, ss, saved, dyb, dp4, logits, gnorm, *p_early)
    return outs[0], outs[1], outs[2], list(outs[3:])


def _mixer_a_bwd(p4, hseq, saved, dya, dp4, cw, wa, wx, lam, tt):
    t = p4.shape[1]
    nt = t // tt

    def body(p_ref, sv_ref, h_ref, hh_ref, dya_ref, dp_in, cw_ref, wa_ref, wx_ref, lam_ref,
             dp_ref, dcw_ref, dcb_ref, dwa_ref, dwx_ref, dba_ref, dbx_ref, dlam_ref,
             dnext, dhc, afc):
        del dp_in
        i = pl.program_id(0)
        first_tile = i == nt - 1

        @pl.when(i == 0)
        def _():
            dnext[...] = jnp.zeros((8, D), F32)
            dhc[...] = jnp.zeros((1, D), F32)
            afc[...] = jnp.zeros((1, D), F32)

        xa = p_ref[:, 0:D]
        ga = p_ref[:, D:2 * D]
        xc, r, ig, a, mult = (sv_ref[idx] for idx in range(5))
        xcb = xc.astype(BF16)
        lam = lam_ref[...]
        sp = jnp.maximum(-lam, 0.0) + jnp.log1p(jnp.exp(-jnp.abs(lam)))
        h = h_ref[...]
        gl, dgl = _gelu_and_grad(ga)
        dya = dya_ref[...].astype(F32)
        dga = dya * h * dgl
        rows = lax.broadcasted_iota(jnp.int32, (tt, 1), 0)
        a_next = jnp.where(rows == tt - 1, afc[...], pltpu.roll(a, tt - 1, 0))
        dh = _scan_up(a_next, dya * gl, dhc[...])
        dhc[...] = dh[0:1, :]
        afc[...] = a[0:1, :]
        h_prev = jnp.where(rows == 0, jnp.where(first_tile, 0.0, hh_ref[7:8, :]), pltpu.roll(h, 1, 0))
        da = dh * h_prev
        dmult = dh * ig * xc
        di = dh * mult * xc
        dlog_a = da * a - dmult * a * a / mult
        dr = dlog_a * (-LRU_C * sp)
        dsp = jnp.sum(dlog_a * (-LRU_C * r), axis=0, keepdims=True)
        dra = dr * r * (1.0 - r)
        dix = di * ig * (1.0 - ig)
        drab = dra.astype(BF16)
        dixb = dix.astype(BF16)
        dxc_lin = []
        dwa_new = []
        dwx_new = []
        for n in range(NH):
            cs = slice(n * HD, (n + 1) * HD)
            dxc_lin.append(_dot_nt(drab[:, cs], wa_ref[n]) + _dot_nt(dixb[:, cs], wx_ref[n]))
            dwa_new.append(_dot_tn(xcb[:, cs], drab[:, cs]))
            dwx_new.append(_dot_tn(xcb[:, cs], dixb[:, cs]))
        dxc = dh * mult * ig + jnp.concatenate(dxc_lin, axis=1)
        de = jnp.concatenate([dxc, dnext[...]], axis=0)
        ups = [de[0:tt, :]] + [pltpu.roll(de, tt + 8 - s, 0)[0:tt, :] for s in (1, 2, 3)]
        dxa = sum(cw_ref[3 - s:4 - s, :] * ups[s] for s in range(4))
        dnext[...] = dxc[0:8, :]
        dp_ref[:, 0:D] = dxa.astype(BF16)
        dp_ref[:, D:2 * D] = dga.astype(BF16)
        dcw = jnp.concatenate(
            [jnp.sum(ups[3 - k] * xa, axis=0, keepdims=True) for k in range(4)], axis=0)
        dcb = jnp.sum(dxc, axis=0, keepdims=True)
        dba = jnp.sum(dra, axis=0, keepdims=True)
        dbx = jnp.sum(dix, axis=0, keepdims=True)
        dlam = dsp * (-_sig(-lam))

        @pl.when(i == 0)
        def _():
            dcw_ref[...] = dcw
            dcb_ref[...] = dcb
            dba_ref[...] = dba
            dbx_ref[...] = dbx
            dlam_ref[...] = dlam
            for n in range(NH):
                dwa_ref[n] = dwa_new[n]
                dwx_ref[n] = dwx_new[n]

        @pl.when(i > 0)
        def _():
            dcw_ref[...] += dcw
            dcb_ref[...] += dcb
            dba_ref[...] += dba
            dbx_ref[...] += dbx
            dlam_ref[...] += dlam
            for n in range(NH):
                dwa_ref[n] += dwa_new[n]
                dwx_ref[n] += dwx_new[n]

    rev = lambda i: nt - 1 - i
    hb = tt // 8
    full = lambda shape: pl.BlockSpec(shape, lambda i: (0,) * len(shape))
    vecs = jax.ShapeDtypeStruct((1, D), F32)
    blk = jax.ShapeDtypeStruct((NH, HD, HD), F32)
    return pl.pallas_call(
        body, name="mixer_a_bwd",
        out_shape=(jax.ShapeDtypeStruct(dp4.shape, BF16), jax.ShapeDtypeStruct((4, D), F32), vecs, blk, blk,
                   vecs, vecs, vecs),
        grid=(nt,),
        in_specs=[pl.BlockSpec((None, tt, 2 * D), lambda i: (SLOT_A, rev(i), 0)),
                  pl.BlockSpec((5, tt, D), lambda i: (0, rev(i), 0)),
                  pl.BlockSpec((tt, D), lambda i: (rev(i), 0)),
                  pl.BlockSpec((8, D), lambda i: (jnp.maximum(rev(i) * hb - 1, 0), 0)),
                  pl.BlockSpec((tt, D), lambda i: (rev(i), 0)),
                  pl.BlockSpec(memory_space=pl.ANY),
                  full((4, D)), full((NH, HD, HD)), full((NH, HD, HD)), full((1, D))],
        out_specs=(pl.BlockSpec((None, tt, 2 * D), lambda i: (SLOT_A, rev(i), 0)),
                   full((4, D)), full((1, D)), full((NH, HD, HD)), full((NH, HD, HD)),
                   full((1, D)), full((1, D)), full((1, D))),
        scratch_shapes=[pltpu.VMEM((8, D), F32), pltpu.VMEM((1, D), F32), pltpu.VMEM((1, D), F32)],
        input_output_aliases={5: 0},
        compiler_params=_params(("arbitrary",), VMEM_BIG),
    )(p4, saved, hseq, hseq, dya, dp4, cw, wa, wx, lam)


def _local_step(x, tgt, stk_w_in, stk_rest, conv_a_s, conv_f_s, small, jc, cidx):
    t = x.shape[0]
    tt = min(256, t)
    tm = min(1024, t)
    tk = min(2048, t)
    wa_bf = small["lru_wa"].astype(BF16)
    wx_bf = small["lru_wx"].astype(BF16)

    p4, h1, w_in, conv_a_g, conv_f_g = _mm_in_gather(x, small["norm_pre_mix"], stk_w_in, conv_a_s, conv_f_s, jc, tm)
    conv_a_w = jnp.transpose(conv_a_g, (1, 0, 2)).reshape(8, D)[0:4]
    conv_f_w = jnp.transpose(conv_f_g, (1, 0, 2)).reshape(8, DUP)[0:3]
    stk = dict(zip(REST, stk_rest))
    ya, hseq, saved_a, got_a = _mixer_a_fwd(p4, conv_a_w, small["conv_a_b"], wa_bf, wx_bf, small["lru_ba"],
                                            small["lru_bx"], small["lru_lambda"],
                                            [stk[n] for n in REST_A], REST_A, tt)
    yb, o_all, ss, saved_b, got_b = _hgrn2_fwd(p4, small["hg_lb_logits"], small["hg_norm_g"],
                                               [stk[n] for n in REST_B], REST_B, tt)
    w = dict(zip(REST_A + REST_B, _gather_forward(got_a + got_b, REST_A + REST_B)))
    w["w_in"] = w_in
    w_br_a = w["w_branch_a"].reshape(D, D)
    w_br_b = w["w_branch_b"].reshape(D, D)
    w_out = w["w_out"].reshape(D, D)
    w_down = w["w_down"].reshape(DFF, D)
    za, zb, mixb, m2, x1, h2 = _mid_fwd(ya, yb, p4, x, w_br_a, w_br_b, w_out, small["norm_post_mix"],
                                        small["norm_pre_ffn"], min(512, t))
    pre_g, pre_v, y, uv, gl, dgl = _up_act_fwd(h2, w["w_up"], conv_f_w, small["conv_f_b"], tt)
    dx2, dm3, lossv, d_norm_post_ffn = _down_loss(y, w_down, x1, tgt, small["norm_post_ffn"], min(512, t))

    d_w_down = _mm_tn(y, dm3, DFF // 2, D, tk, "mm_dw_down")
    dy = _mm_nt(dm3, w_down, BF16, tm, "mm_dy")
    dup_pre, d_conv_f_w, d_conv_f_b = _ffn_act_bwd(dy, pre_g, pre_v, uv, gl, dgl, conv_f_w, tt)
    d_w_up = _mm_tn(h2, dup_pre, D, SH_UP, tk, "mm_dw_up", stacked_out=True)
    dh2 = _mm_nt_sharded(dup_pre, w["w_up"], tm, "mm_dh2")
    dx1, dm2, dza, dzb, dya, dyb, dp4, d_norm_post_mix, d_norm_pre_ffn = _mid_bwd(
        dh2, dx2, x1, m2, za, zb, p4, w_br_a, w_br_b, w_out, small["norm_post_mix"], small["norm_pre_ffn"], tt)
    d_w_out = _mm_tn(mixb, dm2, D, D, tm, "mm_dw_out")
    d_w_br_a = _mm_tn(ya, dza, D, D, tm, "mm_dw_bra")
    d_w_br_b = _mm_tn(yb, dzb, D, D, tm, "mm_dw_brb")
    early = {"w_branch_a": d_w_br_a.reshape(NCHIP, SH_BR, D), "w_branch_b": d_w_br_b.reshape(NCHIP, SH_BR, D),
             "w_out": d_w_out.reshape(NCHIP, SH_BR, D), "w_up": d_w_up, "w_down": d_w_down.reshape(NCHIP, SH_DN, D)}
    rb, _ = _reduce_stage1(early, REST, (), "reduce_d2d_in_early")
    p_rest = [_sum_own_half(early[n], rb[n], cidx, "sum_half_" + n) for n in REST]
    dp4, d_lb, d_hg_norm_g, q_rest = _hgrn2_bwd(p4, o_all, ss, saved_b, dyb, dp4, small["hg_lb_logits"], small["hg_norm_g"],
                                                p_rest, tt)
    dp4, d_conv_a_w, d_conv_a_b, d_lru_wa, d_lru_wx, d_lru_ba, d_lru_bx, d_lru_lambda = _mixer_a_bwd(
        p4, hseq, saved_a, dya, dp4, conv_a_w, wa_bf, wx_bf, small["lru_lambda"], tt)
    d_w_in = _mm_tn(h1, dp4, D, SH_IN, tk, "mm_dw_in", stacked_slot_fn=_slot_of_chip, stacked_out=True)
    rb, _ = _reduce_stage1({"w_in": d_w_in}, ("w_in",), (), "reduce_d2d_in_w_in")
    p_w_in = _sum_own_half(d_w_in, rb["w_in"], cidx, "sum_half_w_in")
    grad_x, q_w_in, d_norm_pre_mix = _mm_dh1_exchange(dp4, w_in, p_w_in, x, dx1, small["norm_pre_mix"], tm)

    smalls = {
        "norm_pre_mix": d_norm_pre_mix, "conv_a_b": d_conv_a_b, "lru_ba": d_lru_ba, "lru_bx": d_lru_bx,
        "lru_lambda": d_lru_lambda, "hg_lb_logits": d_lb, "hg_norm_g": d_hg_norm_g, "norm_post_mix": d_norm_post_mix,
        "norm_pre_ffn": d_norm_pre_ffn, "norm_post_ffn": d_norm_post_ffn, "lossv": lossv,
        "conv_a_w": d_conv_a_w, "lru_wa": d_lru_wa, "lru_wx": d_lru_wx,
        "conv_f_b": d_conv_f_b, "conv_f_w": d_conv_f_w,
    }
    p_big = dict(zip(REST, p_rest), w_in=p_w_in)
    q_big = dict(zip(REST, q_rest), w_in=q_w_in)
    return grad_x, p_big, q_big, smalls


BIG = ("w_in", "w_branch_a", "w_branch_b", "w_out", "w_up", "w_down")
BIG_SHAPE = {"w_in": (D, SH_IN), "w_branch_a": (SH_BR, D), "w_branch_b": (SH_BR, D), "w_out": (SH_BR, D),
             "w_up": (D, SH_UP), "w_down": (SH_DN, D)}
NBIG = len(BIG)
REST = BIG[1:]
REST_A = ("w_branch_a", "w_branch_b", "w_out", "w_down")
REST_B = ("w_up",)
VEC_ROWS = (("norm_pre_mix", 0, 1), ("conv_a_b", 1, 1), ("lru_ba", 2, 1), ("lru_bx", 3, 1), ("lru_lambda", 4, 1),
            ("hg_lb_logits", 5, 2), ("hg_norm_g", 7, 1), ("norm_post_mix", 8, 1), ("norm_pre_ffn", 9, 1),
            ("norm_post_ffn", 10, 1))
ROW_LOSS = 11
ROW_CONV_A = 12
S1_ROWS = 16
S2_ROWS = 8


def _place():
    x, y, c = lax.axis_index("x"), lax.axis_index("y"), lax.axis_index("c")
    chips = [(1 - x, y), (x, 1 - y), (1 - x, 1 - y)]
    return x, y, c, 2 * x + y, chips


def _remote(src, dst, ssem, rsem, dev):
    return pltpu.make_async_remote_copy(src_ref=src, dst_ref=dst, send_sem=ssem, recv_sem=rsem,
                                        device_id=dev, device_id_type=MESH)


def _hbm_call(body, name, ins, out_shapes, n_sems, aliases=None):
    any_spec = pl.BlockSpec(memory_space=pl.ANY)
    return pl.pallas_call(
        body, name=name, out_shape=tuple(out_shapes),
        in_specs=[any_spec] * len(ins), out_specs=tuple([any_spec] * len(out_shapes)),
        scratch_shapes=[pltpu.SemaphoreType.DMA((n,)) for n in n_sems],
        input_output_aliases=aliases or {},
        compiler_params=pltpu.CompilerParams(has_side_effects=True),
    )(*ins)


def _gather_copies(stk, names, ssem, rsem, fssem=None, frsem=None):
    x, y, c, j, chips = _place()
    sends, arrive, fwds, farrive = [], [], [], []
    for w, n in enumerate(names):
        hw = BIG_SHAPE[n][0] // 2
        mine = stk[w].at[j, pl.ds(c * hw, hw), :]
        for k, (cx, cy) in enumerate(chips):
            i = 3 * w + k
            got = stk[w].at[2 * cx + cy, pl.ds(c * hw, hw), :]
            other = stk[w].at[2 * cx + cy, pl.ds((1 - c) * hw, hw), :]
            sends.append(_remote(mine, mine, ssem.at[i], rsem.at[i], (cx, cy, c)))
            arrive.append(_remote(got, got, ssem.at[i], rsem.at[i], (cx, cy, c)))
            if fssem is not None:
                fwds.append(_remote(got, got, fssem.at[i], frsem.at[i], (x, y, 1 - c)))
                farrive.append(_remote(other, other, fssem.at[i], frsem.at[i], (x, y, 1 - c)))
    return sends, arrive, fwds, farrive


def _ici_leg_behind(stk, names, ssem, rsem, first, last_fn):
    sends, arrive, _, _ = _gather_copies(stk, names, ssem, rsem)

    @pl.when(first)
    def _():
        for cp in sends:
            cp.start()

    def finish():
        @pl.when(last_fn())
        def _():
            for cp in arrive:
                cp.wait_recv()
            for cp in sends:
                cp.wait_send()

    return finish


def _gather_forward(stk, names):
    nw = len(names)

    def body(*refs):
        dst = refs[nw:2 * nw]
        ssem, rsem, fssem, frsem = refs[2 * nw:]
        _, _, fwds, farrive = _gather_copies(dst, names, ssem, rsem, fssem, frsem)
        for cp in fwds:
            cp.start()
        for cp in farrive:
            cp.wait_recv()
        for cp in fwds:
            cp.wait_send()

    out_shapes = [jax.ShapeDtypeStruct(a.shape, a.dtype) for a in stk]
    return _hbm_call(body, "gather_forward", stk, out_shapes, (3 * nw,) * 4, aliases={w: w for w in range(nw)})


def _exchange_copies(dst, pieces, ssem, rsem):
    x, y, c, j, chips = _place()
    sends, arrive = [], []
    for w in range(len(dst)):
        for k, (cx, cy) in enumerate(chips):
            i = 3 * w + k
            sends.append(_remote(pieces[w](2 * cx + cy), dst[w].at[j], ssem.at[i], rsem.at[i], (cx, cy, c)))
            got = dst[w].at[2 * cx + cy]
            arrive.append(_remote(got, got, ssem.at[i], rsem.at[i], (cx, cy, c)))
    return sends, arrive


def _reduce_stage1(big_g, names, smalls, name):
    nb = len(names)
    ins = [big_g[n] for n in names] + list(smalls)
    n_in = len(ins)
    halves = [BIG_SHAPE[n][0] // 2 for n in names]
    out_shapes = [jax.ShapeDtypeStruct((NCHIP, halves[w], BIG_SHAPE[n][1]), big_g[n].dtype)
                  for w, n in enumerate(names)]
    out_shapes += [jax.ShapeDtypeStruct(a.shape, F32) for a in smalls]

    def body(*refs):
        src, dst = refs[:n_in], refs[n_in:2 * n_in]
        ssem, rsem = refs[2 * n_in:]
        x, y, c, _, _ = _place()
        cps = []
        for w in range(n_in):
            s_ = src[w].at[:, pl.ds((1 - c) * halves[w], halves[w]), :] if w < nb else src[w]
            cp = _remote(s_, dst[w], ssem.at[w], rsem.at[w], (x, y, 1 - c))
            cp.start()
            cps.append(cp)
        for cp in cps:
            cp.wait()

    outs = _hbm_call(body, name, ins, out_shapes, (n_in, n_in))
    return dict(zip(names, outs[:nb])), outs[nb:]


def _reduce_stage2(ps1, ps2, ps3):
    ins = [ps1, ps2, ps3]
    h1, h2, h3 = S1_ROWS // 2, DUP // 2, D
    out_shapes = [jax.ShapeDtypeStruct((NCHIP, h1, D), F32), jax.ShapeDtypeStruct((NCHIP, S2_ROWS, h2), F32),
                  jax.ShapeDtypeStruct((NCHIP, h3, HD), F32)]

    def body(*refs):
        src, dst = refs[:3], refs[3:6]
        ssem, rsem = refs[6:]
        c = lax.axis_index("c")
        pieces = [lambda chip: src[0].at[pl.ds(c * h1, h1), :],
                  lambda chip: src[1].at[:, pl.ds(c * h2, h2)],
                  lambda chip: src[2].at[pl.ds(c * h3, h3), :]]
        sends, arrive = _exchange_copies(dst, pieces, ssem, rsem)
        for cp in sends:
            cp.start()
        for cp in arrive:
            cp.wait_recv()
        for cp in sends:
            cp.wait_send()

    return _hbm_call(body, "reduce_ici_small", ins, out_shapes, (9, 9))


def _reduce_stage3(f_big, fs1, fs2, fs3):
    ins = [f_big[n] for n in BIG] + [fs1, fs2, fs3]
    n_in = len(ins)
    halves = [BIG_SHAPE[n][0] // 2 for n in BIG]
    h1, h2, h3 = S1_ROWS // 2, DUP // 2, D
    out_shapes = [jax.ShapeDtypeStruct(BIG_SHAPE[n], F32) for n in BIG]
    out_shapes += [jax.ShapeDtypeStruct((S1_ROWS, D), F32), jax.ShapeDtypeStruct((S2_ROWS, DUP), F32),
                   jax.ShapeDtypeStruct((2 * D, HD), F32)]

    def body(*refs):
        dst = refs[n_in:2 * n_in]
        ssem, rsem = refs[2 * n_in:]
        x, y, c, _, _ = _place()

        def place(w, which):
            if w < NBIG:
                return dst[w].at[pl.ds(which * halves[w], halves[w]), :]
            if w == NBIG:
                return dst[w].at[pl.ds(which * h1, h1), :]
            if w == NBIG + 1:
                return dst[w].at[:, pl.ds(which * h2, h2)]
            return dst[w].at[pl.ds(which * h3, h3), :]

        cps = [_remote(place(w, c), place(w, c), ssem.at[w], rsem.at[w], (x, y, 1 - c)) for w in range(n_in)]
        for cp in cps:
            cp.start()
        for w in range(n_in):
            got = place(w, 1 - c)
            _remote(got, got, ssem.at[w], rsem.at[w], (x, y, 1 - c)).wait_recv()
        for cp in cps:
            cp.wait_send()

    outs = _hbm_call(body, "reduce_d2d_out", ins, out_shapes, (n_in, n_in), aliases={w: w for w in range(n_in)})
    return dict(zip(BIG, outs[:NBIG])), outs[NBIG], outs[NBIG + 1], outs[NBIG + 2]


def _row_tile(rows):
    for tr in (128, 176, 64, 16, 8):
        if rows % tr == 0:
            return tr
    return rows


def _sum_own_half(g, rb, cidx, name):
    s, rows, cols = g.shape
    half = rows // 2
    tr = _row_tile(half)
    nb = half // tr

    def body(c_ref, g_ref, r_ref, o_ref):
        del c_ref
        o_ref[...] = (g_ref[...].astype(F32) + r_ref[...].astype(F32)).astype(BF16)

    grid_spec = pltpu.PrefetchScalarGridSpec(
        num_scalar_prefetch=1, grid=(s, nb),
        in_specs=[pl.BlockSpec((None, tr, cols), lambda k, i, c: (k, c[0] * nb + i, 0)),
                  pl.BlockSpec((None, tr, cols), lambda k, i, c: (k, i, 0))],
        out_specs=pl.BlockSpec((None, tr, cols), lambda k, i, c: (k, i, 0)))
    return pl.pallas_call(
        body, name=name, grid_spec=grid_spec, out_shape=jax.ShapeDtypeStruct((s, half, cols), BF16),
        compiler_params=_params(("parallel", "parallel")),
    )(cidx, g, rb)


def _sum_chips(q, p, jc, name, by_cols=False):
    s, rows, cols = q.shape
    tr = _row_tile(rows)
    nb = rows // tr
    stacked = p.ndim == 3

    def body(jc_ref, q_ref, p_ref, o_ref):
        j = jc_ref[0]
        own = p_ref[...].astype(F32)
        acc = None
        for k in range(NCHIP):
            term = jnp.where(j == k, own, q_ref[k].astype(F32))
            acc = term if acc is None else acc + term
        o_ref[...] = acc

    if by_cols:
        half_spec = pl.BlockSpec((tr, cols), lambda i, jc_ref: (i, jc_ref[1]))
        out_shape = jax.ShapeDtypeStruct((rows, 2 * cols), F32)
    else:
        half_spec = pl.BlockSpec((tr, cols), lambda i, jc_ref: (jc_ref[1] * nb + i, 0))
        out_shape = jax.ShapeDtypeStruct((2 * rows, cols), F32)
    p_spec = pl.BlockSpec((None, tr, cols), lambda i, jc_ref: (jc_ref[0], i, 0)) if stacked else half_spec
    grid_spec = pltpu.PrefetchScalarGridSpec(
        num_scalar_prefetch=1, grid=(nb,),
        in_specs=[pl.BlockSpec((s, tr, cols), lambda i, jc_ref: (0, i, 0)), p_spec],
        out_specs=half_spec)
    return pl.pallas_call(
        body, name=name, grid_spec=grid_spec, out_shape=out_shape,
        compiler_params=_params(("parallel",)),
    )(jc, q, p)


def _place_shard(w, jc, name):
    rows, cols = w.shape
    tr = _row_tile(rows)

    def body(jc_ref, w_ref, o_ref):
        del jc_ref
        o_ref[...] = w_ref[...].astype(BF16)

    grid_spec = pltpu.PrefetchScalarGridSpec(
        num_scalar_prefetch=1, grid=(rows // tr,),
        in_specs=[pl.BlockSpec((tr, cols), lambda i, jc_ref: (i, 0))],
        out_specs=pl.BlockSpec((None, tr, cols), lambda i, jc_ref: (jc_ref[0], i, 0)))
    return pl.pallas_call(
        body, name=name, grid_spec=grid_spec, out_shape=jax.ShapeDtypeStruct((NCHIP, rows, cols), BF16),
        compiler_params=_params(("parallel",)),
    )(jc, w)


def _add(a, b, name):
    def body(a_ref, b_ref, o_ref):
        o_ref[...] = a_ref[...] + b_ref[...]

    return pl.pallas_call(body, name=name, out_shape=jax.ShapeDtypeStruct(a.shape, F32))(a, b)


def _pack_small(sm):
    vec_in = [sm[n] for n, _, _ in VEC_ROWS]
    nv = len(vec_in)

    def body(*refs):
        ins, lossv, dcw, dcfb, dcfw, s1, s2 = refs[:nv], refs[nv], refs[nv + 1], refs[nv + 2], refs[nv + 3], \
            refs[nv + 4], refs[nv + 5]
        for ref, (_, r0, nr) in zip(ins, VEC_ROWS):
            s1[r0:r0 + nr, :] = ref[...]
        s1[ROW_LOSS:ROW_LOSS + 1, :] = lossv[...]
        s1[ROW_CONV_A:ROW_CONV_A + 4, :] = dcw[...]
        s2[0:1, :] = dcfb[...]
        s2[1:4, :] = dcfw[...]
        s2[4:8, :] = jnp.zeros((4, DUP), F32)

    return pl.pallas_call(
        body, name="pack_small",
        out_shape=(jax.ShapeDtypeStruct((S1_ROWS, D), F32), jax.ShapeDtypeStruct((S2_ROWS, DUP), F32)),
    )(*vec_in, sm["lossv"], sm["conv_a_w"], sm["conv_f_b"], sm["conv_f_w"])


def _adam_math(w, g, m, v):
    m = ADAM_B1 * m + (1.0 - ADAM_B1) * g
    v = ADAM_B2 * v + (1.0 - ADAM_B2) * (g * g)
    m_hat = m / (1.0 - ADAM_B1 ** ADAM_STEP)
    v_hat = v / (1.0 - ADAM_B2 ** ADAM_STEP)
    delta = -ADAM_LR * (m_hat / (jnp.sqrt(v_hat) + ADAM_EPS) + ADAM_WD * w)
    return delta, m, v


def _adam(w, g, m, v, name):
    rows, cols = w.shape
    tr = _row_tile(rows)

    def body(w_ref, g_ref, m_ref, v_ref, d_ref, mo_ref, vo_ref):
        d_ref[...], mo_ref[...], vo_ref[...] = _adam_math(w_ref[...], g_ref[...], m_ref[...], v_ref[...])

    spec = pl.BlockSpec((tr, cols), lambda i: (i, 0))
    return pl.pallas_call(
        body, name=name, out_shape=(jax.ShapeDtypeStruct(w.shape, F32),) * 3, grid=(rows // tr,),
        in_specs=[spec] * 4, out_specs=(spec,) * 3,
        compiler_params=_params(("parallel",)),
    )(w, g, m, v)


def _adam_small(gs1, gs2, gs3, w, m, v):
    names = [n for n, _, _ in VEC_ROWS] + ["conv_f_b", "lru_wa", "lru_wx"]
    nn = len(names)

    def grad_of(i, g1, g2, g3):
        if i < len(VEC_ROWS):
            _, r0, nr = VEC_ROWS[i]
            return g1[r0:r0 + nr, :]
        if names[i] == "conv_f_b":
            return g2[0:1, :]
        return g3[0] if names[i] == "lru_wa" else g3[1]

    def body(*refs):
        g1, g2, g3 = refs[0], refs[1], refs[2]
        ws, ms, vs = refs[3:3 + nn], refs[3 + nn:3 + 2 * nn], refs[3 + 2 * nn:3 + 3 * nn]
        outs = refs[3 + 3 * nn:]
        for i in range(nn):
            d, mn, vn = _adam_math(ws[i][...], grad_of(i, g1, g2, g3), ms[i][...], vs[i][...])
            outs[i][...] = d
            outs[nn + i][...] = mn
            outs[2 * nn + i][...] = vn

    shapes = [jax.ShapeDtypeStruct(w[n].shape, F32) for n in names]
    outs = pl.pallas_call(body, name="adam_small", out_shape=tuple(shapes * 3))(
        gs1, gs2, gs3, *[w[n] for n in names], *[m[n] for n in names], *[v[n] for n in names])
    return {n: (outs[i], outs[nn + i], outs[2 * nn + i]) for i, n in enumerate(names)}


WEIGHTS = ("norm_pre_mix", "w_in", "conv_a_w", "conv_a_b", "lru_wa", "lru_ba", "lru_wx", "lru_bx", "lru_lambda",
           "hg_lb_logits", "hg_norm_g", "w_branch_a", "w_branch_b", "w_out", "norm_post_mix", "norm_pre_ffn",
           "w_up", "conv_f_w", "conv_f_b", "w_down", "norm_post_ffn")
NW = len(WEIGHTS)


def kernel(x, norm_pre_mix, w_in, conv_a_w, conv_a_b, lru_wa, lru_ba, lru_wx, lru_bx, lru_lambda, hg_lb_logits, hg_norm_g, w_branch_a, w_branch_b, w_out, norm_post_mix, norm_pre_ffn, w_up, conv_f_w, conv_f_b, w_down, norm_post_ffn, loss_target, m_norm_pre_mix, m_w_in, m_conv_a_w, m_conv_a_b, m_lru_wa, m_lru_ba, m_lru_wx, m_lru_bx, m_lru_lambda, m_hg_lb_logits, m_hg_norm_g, m_w_branch_a, m_w_branch_b, m_w_out, m_norm_post_mix, m_norm_pre_ffn, m_w_up, m_conv_f_w, m_conv_f_b, m_w_down, m_norm_post_ffn, v_norm_pre_mix, v_w_in, v_conv_a_w, v_conv_a_b, v_lru_wa, v_lru_ba, v_lru_wx, v_lru_bx, v_lru_lambda, v_hg_lb_logits, v_hg_norm_g, v_w_branch_a, v_w_branch_b, v_w_out, v_norm_post_mix, v_norm_pre_ffn, v_w_up, v_conv_f_w, v_conv_f_b, v_w_down, v_norm_post_ffn):
    rest = (norm_pre_mix, w_in, conv_a_w, conv_a_b, lru_wa, lru_ba, lru_wx, lru_bx, lru_lambda, hg_lb_logits, hg_norm_g, w_branch_a, w_branch_b, w_out, norm_post_mix, norm_pre_ffn, w_up, conv_f_w, conv_f_b, w_down, norm_post_ffn, loss_target, m_norm_pre_mix, m_w_in, m_conv_a_w, m_conv_a_b, m_lru_wa, m_lru_ba, m_lru_wx, m_lru_bx, m_lru_lambda, m_hg_lb_logits, m_hg_norm_g, m_w_branch_a, m_w_branch_b, m_w_out, m_norm_post_mix, m_norm_pre_ffn, m_w_up, m_conv_f_w, m_conv_f_b, m_w_down, m_norm_post_ffn, v_norm_pre_mix, v_w_in, v_conv_a_w, v_conv_a_b, v_lru_wa, v_lru_ba, v_lru_wx, v_lru_bx, v_lru_lambda, v_hg_lb_logits, v_hg_norm_g, v_w_branch_a, v_w_branch_b, v_w_out, v_norm_post_mix, v_norm_pre_ffn, v_w_up, v_conv_f_w, v_conv_f_b, v_w_down, v_norm_post_ffn)
    w_in_args = dict(zip(WEIGHTS, rest[:NW]))
    loss_target = rest[NW]
    m_args = dict(zip(WEIGHTS, rest[NW + 1:2 * NW + 1]))
    v_args = dict(zip(WEIGHTS, rest[2 * NW + 1:3 * NW + 1]))
    shape_of = {n: w_in_args[n].shape for n in WEIGHTS}

    def two_d(n, a):
        if n in BIG:
            return a.reshape(BIG_SHAPE[n])
        if n in ("lru_wa", "lru_wx"):
            return a.reshape(NH, HD, HD)
        return a.reshape(a.shape[-2:])

    w2 = {n: two_d(n, w_in_args[n]) for n in WEIGHTS}
    m2 = {n: two_d(n, m_args[n]) for n in WEIGHTS}
    v2 = {n: two_d(n, v_args[n]) for n in WEIGHTS}

    cidx = lax.axis_index("c").astype(jnp.int32).reshape(1)
    jchip = 2 * lax.axis_index("x") + lax.axis_index("y")

    jc = jnp.stack([jchip, lax.axis_index("c")]).astype(jnp.int32)

    shards = {n: _place_shard(w2[n], jc, "place_" + n) for n in BIG}
    conv_a_s = jnp.pad(w2["conv_a_w"], ((0, 4), (0, 0)))
    conv_f_s = jnp.pad(w2["conv_f_w"], ((0, 5), (0, 0)))
    small = {n: w2[n] for n in WEIGHTS if n not in BIG and n not in ("conv_a_w", "conv_f_w")}

    grad_x, p_big, q_big, sm_g = _local_step(
        x[0], loss_target[0], shards["w_in"], [shards[n] for n in REST], conv_a_s, conv_f_s, small, jc, cidx)

    s1, s2 = _pack_small(sm_g)
    s3 = jnp.concatenate([sm_g["lru_wa"].reshape(D, HD), sm_g["lru_wx"].reshape(D, HD)], axis=0)
    _, (rs1, rs2, rs3) = _reduce_stage1({}, (), (s1, s2, s3), "reduce_d2d_in_small")
    ps1, ps2, ps3 = _add(s1, rs1, "add_s1"), _add(s2, rs2, "add_s2"), _add(s3, rs3, "add_s3")
    qs1, qs2, qs3 = _reduce_stage2(ps1, ps2, ps3)
    f_big = {n: _sum_chips(q_big[n], p_big[n], jc, "sum_chips_" + n) for n in BIG}
    fs1 = _sum_chips(qs1, ps1, jc, "sum_chips_s1")
    fs2 = _sum_chips(qs2, ps2, jc, "sum_chips_s2", by_cols=True)
    fs3 = _sum_chips(qs3, ps3, jc, "sum_chips_s3")
    g_big, gs1, gs2, gs3 = _reduce_stage3(f_big, fs1, fs2, fs3)

    res = {}
    for n in BIG:
        d, mn, vn = _adam(w2[n], g_big[n], m2[n], v2[n], "adam_" + n)
        res[n] = (g_big[n], d, mn, vn)
    small_res = _adam_small(gs1, gs2, gs3.reshape(2, NH, HD, HD), w2, m2, v2)
    for n, r0, nr in VEC_ROWS:
        res[n] = (gs1[r0:r0 + nr],) + small_res[n]
    res["conv_f_b"] = (gs2[0:1],) + small_res["conv_f_b"]
    res["lru_wa"] = (gs3[0:D].reshape(NH, HD, HD),) + small_res["lru_wa"]
    res["lru_wx"] = (gs3[D:2 * D].reshape(NH, HD, HD),) + small_res["lru_wx"]
    g_ca = lax.dynamic_slice_in_dim(gs1[ROW_CONV_A:ROW_CONV_A + 4], jchip * (D // NCHIP), D // NCHIP, axis=1)
    g_cf = lax.dynamic_slice_in_dim(gs2[1:4], jchip * SH_UP, SH_UP, axis=1)
    res["conv_a_w"] = (g_ca,) + _adam(w2["conv_a_w"], g_ca, m2["conv_a_w"], v2["conv_a_w"], "adam_conv_a_w")
    res["conv_f_w"] = (g_cf,) + _adam(w2["conv_f_w"], g_cf, m2["conv_f_w"], v2["conv_f_w"], "adam_conv_f_w")

    loss = (0.5 / D) * jnp.sum(gs1[ROW_LOSS])
    out = [loss, grad_x.reshape(x.shape)]
    for part in range(4):
        out += [res[n][part].reshape(shape_of[n]) for n in WEIGHTS]
    return tuple(out)
```

```python
import functools

import jax
import jax.numpy as jnp
from jax import lax
from jax.experimental import pallas as pl
from jax.experimental.pallas import tpu as pltpu

F32 = jnp.float32
BF16 = jnp.bfloat16

D = 1024
NH = 8
HD = 128
CH = 32
DFF = 2816
DUP = 2 * DFF
NCHIP = 4
SH_IN = 2 * D
SH_UP = DUP // NCHIP
SH_DN = DFF // NCHIP
SH_BR = D // NCHIP
EPS = 1e-6
LRU_C = 8.0
ADAM_LR = 0.001
ADAM_B1 = 0.9
ADAM_B2 = 0.999
ADAM_EPS = 1e-08
ADAM_WD = 0.01
ADAM_STEP = 10
VMEM_BIG = 56 * 1024 * 1024
MESH = pl.DeviceIdType.MESH

SLOT_A, SLOT_B, SLOT_C, SLOT_G = 2, 0, 1, 3


def _slot_of_chip(s):
    return jnp.where(s == 3, 3, (s + 2) % 3)


def _params(sem, vmem=None):
    return pltpu.CompilerParams(dimension_semantics=sem, vmem_limit_bytes=vmem)


_GC = 0.7978845608028654
_GA = 0.044715


def _gelu(x):
    return 0.5 * x * (1.0 + jnp.tanh(_GC * (x + _GA * x * x * x)))


def _gelu_and_grad(x):
    x2 = x * x
    th = jnp.tanh(_GC * x * (1.0 + _GA * x2))
    g = 0.5 * x * (1.0 + th)
    dg = 0.5 * (1.0 + th) + 0.5 * x * (1.0 - th * th) * _GC * (1.0 + 3.0 * _GA * x2)
    return g, dg


def _sig(x):
    return jax.nn.sigmoid(x)


def _dot(a, b):
    return jnp.dot(a, b, preferred_element_type=F32)


def _dot_nt(a, b):
    return lax.dot_general(a, b, (((1,), (1,)), ((), ())), preferred_element_type=F32)


def _dot_tn(a, b):
    return lax.dot_general(a, b, (((0,), (0,)), ((), ())), preferred_element_type=F32)


def _chunk_cumsum(x):
    pos = lax.broadcasted_iota(jnp.int32, (x.shape[0], 1), 0) & (CH - 1)
    d = 1
    while d < CH:
        x = x + jnp.where(pos >= d, pltpu.roll(x, d, 0), 0.0)
        d *= 2
    return x


def _chunk_revcumsum(x):
    n = x.shape[0]
    pos = lax.broadcasted_iota(jnp.int32, (n, 1), 0) & (CH - 1)
    d = 1
    while d < CH:
        x = x + jnp.where(pos < CH - d, pltpu.roll(x, n - d, 0), 0.0)
        d *= 2
    return x


def _chunk_last(x):
    n = x.shape[0]
    return jnp.concatenate(
        [jnp.broadcast_to(x[c * CH + CH - 1:c * CH + CH, :], (CH, x.shape[1])) for c in range(n // CH)], axis=0)


def _chunk_total(x):
    n = x.shape[0]
    return jnp.concatenate(
        [jnp.broadcast_to(jnp.sum(x[c * CH:(c + 1) * CH, :], axis=0, keepdims=True), (CH, x.shape[1]))
         for c in range(n // CH)], axis=0)


def _rms_stats(x):
    r = lax.rsqrt(jnp.mean(x * x, axis=-1, keepdims=True) + EPS)
    return r, x * r


def _rms_bwd(gd, n, r):
    return r * (gd - n * jnp.mean(gd * n, axis=-1, keepdims=True))


def _shift_rows(x, d, fill):
    rows = lax.broadcasted_iota(jnp.int32, (x.shape[0], 1), 0)
    return jnp.where(rows >= d, pltpu.roll(x, d, 0), fill)


def _scan_down(a, u, carry):
    n = a.shape[0]
    pos = lax.broadcasted_iota(jnp.int32, (n, 1), 0) & 7
    for d in (1, 2, 4):
        u = a * jnp.where(pos >= d, pltpu.roll(u, d, 0), 0.0) + u
        a = a * jnp.where(pos >= d, pltpu.roll(a, d, 0), 1.0)
    out = []
    for v in range(n // 8):
        h = a[v * 8:v * 8 + 8, :] * carry + u[v * 8:v * 8 + 8, :]
        carry = h[7:8, :]
        out.append(h)
    return jnp.concatenate(out, axis=0)


def _scan_up(b, g, carry):
    n = b.shape[0]
    pos = lax.broadcasted_iota(jnp.int32, (n, 1), 0) & 7
    for d in (1, 2, 4):
        g = g + b * jnp.where(pos < 8 - d, pltpu.roll(g, n - d, 0), 0.0)
        b = b * jnp.where(pos < 8 - d, pltpu.roll(b, n - d, 0), 1.0)
    out = [None] * (n // 8)
    for v in reversed(range(n // 8)):
        h = g[v * 8:v * 8 + 8, :] + b[v * 8:v * 8 + 8, :] * carry
        carry = h[0:1, :]
        out[v] = h
    return jnp.concatenate(out, axis=0)


def _shift_rows_up(x, d, fill):
    n = x.shape[0]
    rows = lax.broadcasted_iota(jnp.int32, (n, 1), 0)
    return jnp.where(rows < n - d, pltpu.roll(x, n - d, 0), fill)


def _mm_nn_sharded(a, b3, out_dtype, tm, name, slot_fn=None):
    m, k = a.shape
    s, _, ns = b3.shape

    def body(a_ref, b_ref, o_ref):
        o_ref[...] = _dot(a_ref[...], b_ref[...]).astype(out_dtype)

    if slot_fn is None:
        out_shape = jax.ShapeDtypeStruct((m, s * ns), out_dtype)
        out_spec = pl.BlockSpec((tm, ns), lambda j, i: (i, j))
    else:
        out_shape = jax.ShapeDtypeStruct((s, m, ns), out_dtype)
        out_spec = pl.BlockSpec((None, tm, ns), lambda j, i: (slot_fn(j), i, 0))
    return pl.pallas_call(
        body, name=name, out_shape=out_shape, grid=(s, m // tm),
        in_specs=[pl.BlockSpec((tm, k), lambda j, i: (i, 0)),
                  pl.BlockSpec((None, k, ns), lambda j, i: (j, 0, 0))],
        out_specs=out_spec,
        compiler_params=_params(("parallel", "parallel"), VMEM_BIG),
    )(a, b3)


def _peer_of_step(s):
    return ((s & 1) << 1) | (s >> 1)


def _mm_in_gather(xin, gain, stk_w_in, conv_a_s, conv_f_s, jc, tm):
    m, k = xin.shape
    s_n, _, ns = stk_w_in.shape
    mt = m // tm

    def body(jc_ref, x_ref, g_ref, w_in_any, ca_src, cf_src, o_ref, h_ref, w_full, ca_dst, cf_dst,
             wbuf, hbuf, ssem_w, rsem_w, fssem_w, frsem_w, csend, crecv, lsem, wsem):
        del w_in_any
        s, i = pl.program_id(0), pl.program_id(1)
        x, y, c, j, chips = _place()
        sends_w, arrive_w, fwds_w, farrive_w = _gather_copies([w_full], ("w_in",), ssem_w, rsem_w, fssem_w, frsem_w)
        conv = ((ca_src, ca_dst), (cf_src, cf_dst))
        locs = [pltpu.make_async_copy(src, dst.at[j], lsem.at[n]) for n, (src, dst) in enumerate(conv)]
        csends = [_remote(src, dst.at[j], csend.at[3 * n + kk], crecv.at[3 * n + kk], (cx, cy, c))
                  for n, (src, dst) in enumerate(conv) for kk, (cx, cy) in enumerate(chips)]

        def fetch(step):
            return pltpu.make_async_copy(w_full.at[j ^ _peer_of_step(step)], wbuf.at[step & 1], wsem.at[step & 1])

        @pl.when((s == 0) & (i == 0))
        def _():
            for cp in sends_w[0:2] + locs + csends:
                cp.start()
            fetch(s).start()

        @pl.when(i == 0)
        def _():
            fetch(s).wait()

        @pl.when(s == 0)
        def _():
            _, n = _rms_stats(x_ref[...])
            h = (n * g_ref[...]).astype(BF16)
            hbuf[i] = h
            h_ref[...] = h

        o_ref[...] = _dot(hbuf[i], wbuf[s & 1])

        @pl.when((s == 0) & (i == mt - 1))
        def _():
            for kk in (0, 1):
                arrive_w[kk].wait_recv()
                fwds_w[kk].start()
            sends_w[2].start()

        @pl.when((s == 1) & (i == mt - 1))
        def _():
            arrive_w[2].wait_recv()
            fwds_w[2].start()

        for kk in range(3):
            @pl.when((s == kk) & (i == mt - 1))
            def _(kk=kk):
                farrive_w[kk].wait_recv()
                fetch(s + 1).start()

        @pl.when((s == s_n - 1) & (i == mt - 1))
        def _():
            for n, (_, dst) in enumerate(conv):
                for kk, (cx, cy) in enumerate(chips):
                    got = dst.at[2 * cx + cy]
                    _remote(got, got, csend.at[3 * n + kk], crecv.at[3 * n + kk], (cx, cy, c)).wait_recv()
            for cp in sends_w + fwds_w + csends:
                cp.wait_send()
            for cp in locs:
                cp.wait()

    any_spec = pl.BlockSpec(memory_space=pl.ANY)
    sem = pltpu.SemaphoreType.DMA
    grid_spec = pltpu.PrefetchScalarGridSpec(
        num_scalar_prefetch=1, grid=(s_n, mt),
        in_specs=[pl.BlockSpec((tm, k), lambda s, i, jc_ref: (jnp.where(s == 0, i, mt - 1), 0)),
                  pl.BlockSpec((1, k), lambda s, i, jc_ref: (0, 0)), any_spec, any_spec, any_spec],
        out_specs=(pl.BlockSpec((None, tm, ns),
                                lambda s, i, jc_ref: (_slot_of_chip(jc_ref[0] ^ _peer_of_step(s)), i, 0)),
                   pl.BlockSpec((tm, k), lambda s, i, jc_ref: (jnp.where(s == 0, i, mt - 1), 0)),
                   any_spec, any_spec, any_spec),
        scratch_shapes=[pltpu.VMEM((2, k, ns), BF16), pltpu.VMEM((mt, tm, k), BF16),
                        sem((3,)), sem((3,)), sem((3,)), sem((3,)), sem((6,)), sem((6,)), sem((2,)), sem((2,))])
    return pl.pallas_call(
        body, name="mm_in", grid_spec=grid_spec,
        out_shape=(jax.ShapeDtypeStruct((s_n, m, ns), F32), jax.ShapeDtypeStruct((m, k), BF16),
                   jax.ShapeDtypeStruct(stk_w_in.shape, BF16))
        + tuple(jax.ShapeDtypeStruct((NCHIP,) + v.shape, v.dtype) for v in (conv_a_s, conv_f_s)),
        input_output_aliases={3: 2},
        compiler_params=pltpu.CompilerParams(dimension_semantics=("arbitrary", "arbitrary"),
                                             vmem_limit_bytes=VMEM_BIG, has_side_effects=True),
    )(jc, xin, gain, stk_w_in, conv_a_s, conv_f_s)


def _mm_dh1_exchange(dp4, w_in3, p_w_in, x, dx1, gain, tm):
    s, k, ns = w_in3.shape
    m = dp4.shape[1]
    mt = m // tm

    def body(a_ref, b_ref, p_ref, x_ref, dx1_ref, g_ref, o_ref, q_ref, dg_ref, ssem, rsem):
        i, j = pl.program_id(0), pl.program_id(1)
        sends, arrive = _exchange_copies([q_ref], [lambda chip: p_ref.at[chip]], ssem, rsem)

        @pl.when((i == 0) & (j == 0))
        def _():
            for cp in sends:
                cp.start()

        @pl.when(j == 0)
        def _():
            o_ref[...] = _dot_nt(a_ref[...], b_ref[...])

        @pl.when(j > 0)
        def _():
            o_ref[...] += _dot_nt(a_ref[...], b_ref[...])

        @pl.when(j == s - 1)
        def _():
            dh = o_ref[...]
            r, n = _rms_stats(x_ref[...])
            o_ref[...] = dx1_ref[...] + _rms_bwd(dh * g_ref[...], n, r)
            dgv = jnp.sum(dh * n, axis=0, keepdims=True)

            @pl.when(i == 0)
            def _():
                dg_ref[...] = dgv

            @pl.when(i > 0)
            def _():
                dg_ref[...] += dgv

        @pl.when((i == mt - 1) & (j == s - 1))
        def _():
            for cp in arrive:
                cp.wait_recv()
            for cp in sends:
                cp.wait_send()

    any_spec = pl.BlockSpec(memory_space=pl.ANY)
    row_tile = pl.BlockSpec((tm, k), lambda i, j: (i, 0))
    vec = pl.BlockSpec((1, k), lambda i, j: (0, 0))
    return pl.pallas_call(
        body, name="mm_dh1",
        out_shape=(jax.ShapeDtypeStruct((m, k), F32), jax.ShapeDtypeStruct(p_w_in.shape, BF16),
                   jax.ShapeDtypeStruct((1, k), F32)),
        grid=(mt, s),
        in_specs=[pl.BlockSpec((None, tm, ns), lambda i, j: (_slot_of_chip(j), i, 0)),
                  pl.BlockSpec((None, k, ns), lambda i, j: (j, 0, 0)), any_spec, row_tile, row_tile, vec],
        out_specs=(row_tile, any_spec, vec),
        scratch_shapes=[pltpu.SemaphoreType.DMA((3,)), pltpu.SemaphoreType.DMA((3,))],
        compiler_params=pltpu.CompilerParams(dimension_semantics=("arbitrary", "arbitrary"),
                                             vmem_limit_bytes=VMEM_BIG, has_side_effects=True),
    )(dp4, w_in3, p_w_in, x, dx1, gain)


def _mm_dh2_norm(dup, w_up4, dx2, x1, gain, tm):
    s, k, ns = w_up4.shape
    m = dup.shape[0]

    def body(a_ref, b_ref, dx2_ref, x1_ref, g_ref, o_ref, dg_ref):
        i, j = pl.program_id(0), pl.program_id(1)

        @pl.when(j == 0)
        def _():
            o_ref[...] = _dot_nt(a_ref[...], b_ref[...])

        @pl.when(j > 0)
        def _():
            o_ref[...] += _dot_nt(a_ref[...], b_ref[...])

        @pl.when(j == s - 1)
        def _():
            dh = o_ref[...]
            r, n = _rms_stats(x1_ref[...])
            o_ref[...] = dx2_ref[...] + _rms_bwd(dh * g_ref[...], n, r)
            dgv = jnp.sum(dh * n, axis=0, keepdims=True)

            @pl.when(i == 0)
            def _():
                dg_ref[...] = dgv

            @pl.when(i > 0)
            def _():
                dg_ref[...] += dgv

    row_tile = pl.BlockSpec((tm, k), lambda i, j: (i, 0))
    vec = pl.BlockSpec((1, k), lambda i, j: (0, 0))
    return pl.pallas_call(
        body, name="mm_dh2",
        out_shape=(jax.ShapeDtypeStruct((m, k), F32), jax.ShapeDtypeStruct((1, k), F32)),
        grid=(m // tm, s),
        in_specs=[pl.BlockSpec((tm, ns), lambda i, j: (i, j)), pl.BlockSpec((None, k, ns), lambda i, j: (j, 0, 0)),
                  row_tile, row_tile, vec],
        out_specs=(row_tile, vec),
        compiler_params=_params(("arbitrary", "arbitrary"), VMEM_BIG),
    )(dup, w_up4, dx2, x1, gain)


def _mm_nt(a, b, out_dtype, tm, name):
    m, k = a.shape
    n = b.shape[0]

    def body(a_ref, b_ref, o_ref):
        o_ref[...] = _dot_nt(a_ref[...], b_ref[...]).astype(out_dtype)

    return pl.pallas_call(
        body, name=name, out_shape=jax.ShapeDtypeStruct((m, n), out_dtype), grid=(m // tm,),
        in_specs=[pl.BlockSpec((tm, k), lambda i: (i, 0)), pl.BlockSpec((n, k), lambda i: (0, 0))],
        out_specs=pl.BlockSpec((tm, n), lambda i: (i, 0)),
        compiler_params=_params(("parallel",), VMEM_BIG),
    )(a, b)


def _mm_tn(a, g, tkk, tn, tk, name, stacked_slot_fn=None, stacked_out=False):
    m, k = a.shape
    if stacked_slot_fn is not None:
        n = g.shape[0] * g.shape[2]
        g_spec = pl.BlockSpec((None, tk, tn), lambda kk, j, mm: (stacked_slot_fn(j), mm, 0))
    else:
        n = g.shape[1]
        g_spec = pl.BlockSpec((tk, tn), lambda kk, j, mm: (mm, j))
    steps = m // tk

    def body(a_ref, g_ref, o_ref, acc_ref):
        mm = pl.program_id(2)

        @pl.when(mm == 0)
        def _():
            acc_ref[...] = _dot_tn(a_ref[...], g_ref[...])

        @pl.when(mm > 0)
        def _():
            acc_ref[...] += _dot_tn(a_ref[...], g_ref[...])

        @pl.when(mm == steps - 1)
        def _():
            o_ref[...] = acc_ref[...].astype(BF16)

    if stacked_out:
        out_shape = jax.ShapeDtypeStruct((n // tn, k, tn), BF16)
        out_spec = pl.BlockSpec((None, tkk, tn), lambda kk, j, mm: (j, kk, 0))
    else:
        out_shape = jax.ShapeDtypeStruct((k, n), BF16)
        out_spec = pl.BlockSpec((tkk, tn), lambda kk, j, mm: (kk, j))
    return pl.pallas_call(
        body, name=name, out_shape=out_shape, grid=(k // tkk, n // tn, steps),
        in_specs=[pl.BlockSpec((tk, tkk), lambda kk, j, mm: (mm, kk)), g_spec],
        out_specs=out_spec,
        scratch_shapes=[pltpu.VMEM((tkk, tn), F32)],
        compiler_params=_params(("parallel", "parallel", "arbitrary"), VMEM_BIG),
    )(a, g)


def _lru_gates(xc, wa_ref, wx_ref, ba, bx, lam):
    xcb = xc.astype(BF16)
    ra = jnp.concatenate([_dot(xcb[:, n * HD:(n + 1) * HD], wa_ref[n]) for n in range(NH)], axis=1) + ba
    ix = jnp.concatenate([_dot(xcb[:, n * HD:(n + 1) * HD], wx_ref[n]) for n in range(NH)], axis=1) + bx
    r = _sig(ra)
    ig = _sig(ix)
    z = -lam
    sp = jnp.maximum(z, 0.0) + jnp.log1p(jnp.exp(-jnp.abs(z)))
    log_a = -LRU_C * r * sp
    a = jnp.exp(log_a)
    z2 = 2.0 * log_a
    series = -z2 * (1.0 + z2 * (0.5 + z2 * (1.0 / 6.0 + z2 * (1.0 / 24.0))))
    om = jnp.where(z2 > -0.02, series, 1.0 - jnp.exp(z2))
    mult = jnp.sqrt(om)
    return xcb, r, ig, sp, a, mult


def _mixer_a_fwd(p4, cw, cb, wa, wx, ba, bx, lam, stk, names, tt):
    t = p4.shape[1]
    ng = len(stk)

    def body(p_ref, cw_ref, cb_ref, wa_ref, wx_ref, ba_ref, bx_ref, lam_ref, *rest):
        ya_ref, h_ref, sv_ref = rest[ng:ng + 3]
        halo, hc, ssem, rsem = rest[2 * ng + 3:]
        i = pl.program_id(0)
        finish = _ici_leg_behind(rest[ng + 3:2 * ng + 3], names, ssem, rsem, i == 0, lambda: i == t // tt - 1)

        @pl.when(i == 0)
        def _():
            halo[...] = jnp.zeros((8, D), F32)
            hc[...] = jnp.zeros((1, D), F32)

        xa = p_ref[:, 0:D]
        ga = p_ref[:, D:2 * D]
        xe = jnp.concatenate([halo[...], xa], axis=0)
        xc = (cb_ref[...] + cw_ref[3:4, :] * xe
              + sum(cw_ref[3 - s:4 - s, :] * pltpu.roll(xe, s, 0) for s in (1, 2, 3)))[8:, :]
        halo[...] = xa[tt - 8:, :]
        _, r, ig, _, a, mult = _lru_gates(xc, wa_ref, wx_ref, ba_ref[...], bx_ref[...], lam_ref[...])
        u = mult * ig * xc
        h = _scan_down(a, u, hc[...])
        hc[...] = h[tt - 1:tt, :]
        h_ref[...] = h
        ya_ref[...] = (h * _gelu(ga)).astype(BF16)
        for idx, val in enumerate((xc, r, ig, a, mult)):
            sv_ref[idx] = val
        finish()

    full = lambda shape: pl.BlockSpec(shape, lambda i: (0,) * len(shape))
    any_spec = pl.BlockSpec(memory_space=pl.ANY)
    outs = pl.pallas_call(
        body, name="mixer_a_fwd",
        out_shape=(jax.ShapeDtypeStruct((t, D), BF16), jax.ShapeDtypeStruct((t, D), F32),
                   jax.ShapeDtypeStruct((5, t, D), F32)) + tuple(jax.ShapeDtypeStruct(v.shape, v.dtype) for v in stk),
        grid=(t // tt,),
        in_specs=[pl.BlockSpec((None, tt, 2 * D), lambda i: (SLOT_A, i, 0)),
                  full((4, D)), full((1, D)), full((NH, HD, HD)), full((NH, HD, HD)),
                  full((1, D)), full((1, D)), full((1, D))] + [any_spec] * ng,
        out_specs=(pl.BlockSpec((tt, D), lambda i: (i, 0)), pl.BlockSpec((tt, D), lambda i: (i, 0)),
                   pl.BlockSpec((5, tt, D), lambda i: (0, i, 0))) + (any_spec,) * ng,
        scratch_shapes=[pltpu.VMEM((8, D), F32), pltpu.VMEM((1, D), F32),
                        pltpu.SemaphoreType.DMA((3 * ng,)), pltpu.SemaphoreType.DMA((3 * ng,))],
        input_output_aliases={8 + w: 3 + w for w in range(ng)},
        compiler_params=pltpu.CompilerParams(dimension_semantics=("arbitrary",), vmem_limit_bytes=VMEM_BIG,
                                             has_side_effects=True),
    )(p4, cw, cb, wa, wx, ba, bx, lam, *stk)
    return outs[0], outs[1], outs[2], list(outs[3:])


def _chunk_masks(tt):
    row = lax.broadcasted_iota(jnp.int32, (tt, tt), 0)
    col = lax.broadcasted_iota(jnp.int32, (tt, tt), 1)
    same = jnp.right_shift(row, 5) == jnp.right_shift(col, 5)
    return same & (col <= row)


def _hg_head_fwd(q, fz, lbh, saved=None):
    sgn = _sig(-fz)
    k = (1.0 - lbh) * sgn
    if saved is None:
        sg = _sig(fz)
        f = lbh + (1.0 - lbh) * sg
        g = _chunk_cumsum(jnp.log(f))
    else:
        sg, g = saved
        f = lbh + (1.0 - lbh) * sg
    gu = _chunk_last(g) - g
    eg = jnp.exp(g)
    eng = jnp.exp(-g)
    egu = jnp.exp(gu)
    qt = q * eg
    kt = k * eng
    kd = k * egu
    return sg, sgn, f, k, g, eg, eng, egu, qt, kt, kd


def _lb_of(logits_ref):
    return _sig(logits_ref[0:1, :] - logits_ref[1:2, :])


def _hgrn2_fwd(p4, logits, gnorm, stk, names, tt):
    t = p4.shape[1]
    nc = tt // CH
    ng = len(stk)

    def body(p_ref, lg_ref, gn_ref, *rest):
        yb_ref, o_ref, ss_ref, sv_ref = rest[ng:ng + 4]
        st, ssem, rsem = rest[2 * ng + 4:]
        i = pl.program_id(0)
        finish = _ici_leg_behind(rest[ng + 4:2 * ng + 4], names, ssem, rsem, i == 0, lambda: i == t // tt - 1)

        @pl.when(i == 0)
        def _():
            st[...] = jnp.zeros((NH, HD, HD), F32)

        low = _chunk_masks(tt)
        lb = _lb_of(lg_ref)
        heads = [slice(h * HD, (h + 1) * HD) for h in range(NH)]
        sg, _, _, _, g, _, _, _, qt, kt, kd = _hg_head_fwd(p_ref[0, :, 0:D], p_ref[0, :, D:2 * D], lb)
        sv_ref[0] = sg
        sv_ref[1] = g
        qtb, ktb, kdb, vb = qt.astype(BF16), kt.astype(BF16), kd.astype(BF16), p_ref[1, :, 0:D].astype(BF16)
        decs = [jnp.exp(g[c * CH + CH - 1:c * CH + CH, :]) for c in range(nc)]
        o_in = []
        for hs in heads:
            att = jnp.where(low, _dot_nt(qtb[:, hs], ktb[:, hs]), 0.0)
            o_in.append(_dot(att.astype(BF16), vb[:, hs]))
        s_t = [st[h] for h in range(NH)]
        pieces = [[None] * nc for _ in range(NH)]
        for c in range(nc):
            sl = slice(c * CH, (c + 1) * CH)
            for h, hs in enumerate(heads):
                s_bf = s_t[h].astype(BF16)
                ss_ref[c, h] = s_bf
                pieces[h][c] = o_in[h][sl] + _dot_nt(qtb[sl, hs], s_bf)
                s_t[h] = s_t[h] * decs[c][:, hs] + _dot_tn(vb[sl, hs], kdb[sl, hs])
        for h, hs in enumerate(heads):
            st[h] = s_t[h]
            o = jnp.concatenate(pieces[h], axis=0)
            _, n = _rms_stats(o)
            og = p_ref[1, :, D + h * HD:D + (h + 1) * HD]
            o_ref[:, hs] = o
            yb_ref[:, hs] = (n * gn_ref[:, hs] * (og * _sig(og))).astype(BF16)
        finish()

    any_spec = pl.BlockSpec(memory_space=pl.ANY)
    outs = pl.pallas_call(
        body, name="hgrn2_fwd",
        out_shape=(jax.ShapeDtypeStruct((t, D), BF16), jax.ShapeDtypeStruct((t, D), F32),
                   jax.ShapeDtypeStruct((t // CH, NH, HD, HD), BF16), jax.ShapeDtypeStruct((2, t, D), F32))
        + tuple(jax.ShapeDtypeStruct(v.shape, v.dtype) for v in stk),
        grid=(t // tt,),
        in_specs=[pl.BlockSpec((2, tt, 2 * D), lambda i: (0, i, 0)),
                  pl.BlockSpec((2, D), lambda i: (0, 0)), pl.BlockSpec((1, D), lambda i: (0, 0))] + [any_spec] * ng,
        out_specs=(pl.BlockSpec((tt, D), lambda i: (i, 0)), pl.BlockSpec((tt, D), lambda i: (i, 0)),
                   pl.BlockSpec((nc, NH, HD, HD), lambda i: (i, 0, 0, 0)),
                   pl.BlockSpec((2, tt, D), lambda i: (0, i, 0))) + (any_spec,) * ng,
        scratch_shapes=[pltpu.VMEM((NH, HD, HD), F32), pltpu.SemaphoreType.DMA((3 * ng,)),
                        pltpu.SemaphoreType.DMA((3 * ng,))],
        input_output_aliases={3 + w: 4 + w for w in range(ng)},
        compiler_params=pltpu.CompilerParams(dimension_semantics=("arbitrary",), vmem_limit_bytes=VMEM_BIG,
                                             has_side_effects=True),
    )(p4, logits, gnorm, *stk)
    return outs[0], outs[1], outs[2], outs[3], list(outs[4:])


def _mid_fwd(ya, yb, p4, x, wa, wb, wo, g_pm, g_pf, tt):
    t = x.shape[0]

    def body(ya_ref, yb_ref, gt_ref, x_ref, wa_ref, wb_ref, wo_ref, gpm_ref, gpf_ref,
             za_ref, zb_ref, mix_ref, m2_ref, x1_ref, h2_ref):
        za = _dot(ya_ref[...], wa_ref[...])
        zb = _dot(yb_ref[...], wb_ref[...])
        mix = _sig(gt_ref[:, 0:D]) * za + _sig(gt_ref[:, D:2 * D]) * zb
        mixb = mix.astype(BF16)
        m2 = _dot(mixb, wo_ref[...])
        _, n2 = _rms_stats(m2)
        x1 = x_ref[...] + n2 * gpm_ref[...]
        _, n1 = _rms_stats(x1)
        za_ref[...] = za.astype(BF16)
        zb_ref[...] = zb.astype(BF16)
        mix_ref[...] = mixb
        m2_ref[...] = m2
        x1_ref[...] = x1
        h2_ref[...] = (n1 * gpf_ref[...]).astype(BF16)

    row = lambda dt: jax.ShapeDtypeStruct((t, D), dt)
    tile = pl.BlockSpec((tt, D), lambda i: (i, 0))
    wsp = pl.BlockSpec((D, D), lambda i: (0, 0))
    vec = pl.BlockSpec((1, D), lambda i: (0, 0))
    return pl.pallas_call(
        body, name="mid_fwd",
        out_shape=(row(BF16), row(BF16), row(BF16), row(F32), row(F32), row(BF16)),
        grid=(t // tt,),
        in_specs=[tile, tile, pl.BlockSpec((None, tt, 2 * D), lambda i: (SLOT_G, i, 0)), tile,
                  wsp, wsp, wsp, vec, vec],
        out_specs=(tile,) * 6,
        compiler_params=_params(("parallel",), VMEM_BIG),
    )(ya, yb, p4, x, wa, wb, wo, g_pm, g_pf)


def _up_act_fwd(h2, w_up4, cfw, cfb, tm):
    t = h2.shape[0]
    ns = SH_UP

    def body(a_ref, ah_ref, wg_ref, wv_ref, cwg_ref, cwv_ref, cbg_ref, cbv_ref,
             pg_ref, pv_ref, y_ref, uv_ref, gl_ref, dgl_ref):
        i = pl.program_id(1)
        rows = jnp.concatenate([ah_ref[...], a_ref[...]], axis=0)
        ups = []
        for w_ref, cw_ref, cb_ref, pre_ref in ((wg_ref, cwg_ref, cbg_ref, pg_ref), (wv_ref, cwv_ref, cbv_ref, pv_ref)):
            pre = _dot(rows, w_ref[...])
            pre_ref[...] = pre[16:, :].astype(BF16)
            xe = jnp.concatenate([jnp.where(i > 0, pre[8:16, :], 0.0), pre[16:, :]], axis=0)
            up = (cb_ref[...] + cw_ref[2:3, :] * xe + cw_ref[1:2, :] * pltpu.roll(xe, 1, 0)
                  + cw_ref[0:1, :] * pltpu.roll(xe, 2, 0))
            ups.append(up[8:, :])
        gl, dgl = _gelu_and_grad(ups[0])
        y_ref[...] = (gl * ups[1]).astype(BF16)
        uv_ref[...] = ups[1].astype(BF16)
        gl_ref[...] = gl.astype(BF16)
        dgl_ref[...] = dgl.astype(BF16)

    hb = tm // 16
    tile = pl.BlockSpec((tm, ns), lambda p, i: (i, p))
    return pl.pallas_call(
        body, name="up_act_fwd",
        out_shape=(jax.ShapeDtypeStruct((t, DFF), BF16),) * 6,
        grid=(2, t // tm),
        in_specs=[pl.BlockSpec((tm, D), lambda p, i: (i, 0)),
                  pl.BlockSpec((16, D), lambda p, i: (jnp.maximum(i * hb - 1, 0), 0)),
                  pl.BlockSpec((None, D, ns), lambda p, i: (p, 0, 0)),
                  pl.BlockSpec((None, D, ns), lambda p, i: (p + 2, 0, 0)),
                  pl.BlockSpec((3, ns), lambda p, i: (0, p)), pl.BlockSpec((3, ns), lambda p, i: (0, p + 2)),
                  pl.BlockSpec((1, ns), lambda p, i: (0, p)), pl.BlockSpec((1, ns), lambda p, i: (0, p + 2))],
        out_specs=(tile,) * 6,
        compiler_params=_params(("parallel", "parallel"), VMEM_BIG),
    )(h2, h2, w_up4, w_up4, cfw, cfw, cfb, cfb)


def _down_loss(y, wdn, x1, tgt, g_post, tt):
    t = x1.shape[0]

    def body(y_ref, w_ref, x1_ref, t_ref, g_ref, dx2_ref, dm3_ref, lossv_ref, dg_ref):
        i = pl.program_id(0)
        m3 = _dot(y_ref[...], w_ref[...])
        r, n3 = _rms_stats(m3)
        g = g_ref[...]
        e = x1_ref[...] + n3 * g - t_ref[...]
        dx2 = e * (1.0 / D)
        dx2_ref[...] = dx2
        dm3_ref[...] = _rms_bwd(dx2 * g, n3, r).astype(BF16)
        lv = jnp.sum(e * e, axis=0, keepdims=True)
        dgv = jnp.sum(dx2 * n3, axis=0, keepdims=True)

        @pl.when(i == 0)
        def _():
            lossv_ref[...] = lv
            dg_ref[...] = dgv

        @pl.when(i > 0)
        def _():
            lossv_ref[...] += lv
            dg_ref[...] += dgv

    tile = pl.BlockSpec((tt, D), lambda i: (i, 0))
    vec = pl.BlockSpec((1, D), lambda i: (0, 0))
    return pl.pallas_call(
        body, name="down_loss",
        out_shape=(jax.ShapeDtypeStruct((t, D), F32), jax.ShapeDtypeStruct((t, D), BF16),
                   jax.ShapeDtypeStruct((1, D), F32), jax.ShapeDtypeStruct((1, D), F32)),
        grid=(t // tt,),
        in_specs=[pl.BlockSpec((tt, DFF), lambda i: (i, 0)), pl.BlockSpec((DFF, D), lambda i: (0, 0)),
                  tile, tile, vec],
        out_specs=(tile, tile, vec, vec),
        compiler_params=_params(("arbitrary",), VMEM_BIG),
    )(y, wdn, x1, tgt, g_post)


def _ffn_act_bwd(dy, pre_g, pre_v, uv, gl, dgl, cfw, tt):
    t = dy.shape[0]
    nt = t // tt

    def body(dy_ref, dyn_ref, pg_ref, pv_ref, uv_ref, uvn_ref, gl_ref, gln_ref, dgl_ref, dgln_ref, cw_ref,
             du_ref, dcw_ref, dcb_ref):
        i = pl.program_id(0)
        n = tt + 8
        next_live = jnp.where(i < nt - 1, 1.0, 0.0)
        ext = lambda ref, nref: jnp.concatenate([ref[...].astype(F32), nref[...].astype(F32)[0:8, :]], axis=0)
        dy = jnp.concatenate([dy_ref[...].astype(F32), dyn_ref[...].astype(F32)[0:8, :] * next_live], axis=0)
        ds = (dy * ext(uv_ref, uvn_ref) * ext(dgl_ref, dgln_ref), dy * ext(gl_ref, gln_ref))
        dcw_parts, dcb_parts = [], []
        for hh, c0 in enumerate((0, DFF)):
            cs = slice(c0, c0 + DFF)
            dd = ds[hh]
            d1 = pltpu.roll(dd, n - 1, 0)
            d2 = pltpu.roll(dd, n - 2, 0)
            du_ref[:, cs] = (cw_ref[2:3, cs] * dd + cw_ref[1:2, cs] * d1 + cw_ref[0:1, cs] * d2)[0:tt, :].astype(BF16)
            x = (pg_ref, pv_ref)[hh][...].astype(F32)
            dcw_parts.append(jnp.concatenate(
                [jnp.sum(dk[0:tt, :] * x, axis=0, keepdims=True) for dk in (d2, d1, dd)], axis=0))
            dcb_parts.append(jnp.sum(dd[0:tt, :], axis=0, keepdims=True))
        dcw = jnp.concatenate(dcw_parts, axis=1)
        dcb = jnp.concatenate(dcb_parts, axis=1)

        @pl.when(i == 0)
        def _():
            dcw_ref[...] = dcw
            dcb_ref[...] = dcb

        @pl.when(i > 0)
        def _():
            dcw_ref[...] += dcw
            dcb_ref[...] += dcb

    half = pl.BlockSpec((tt, DFF), lambda i: (i, 0))
    half_next = pl.BlockSpec((16, DFF), lambda i: (jnp.minimum((i + 1) * (tt // 16), t // 16 - 1), 0))
    return pl.pallas_call(
        body, name="ffn_act_bwd",
        out_shape=(jax.ShapeDtypeStruct((t, DUP), BF16), jax.ShapeDtypeStruct((3, DUP), F32),
                   jax.ShapeDtypeStruct((1, DUP), F32)),
        grid=(nt,),
        in_specs=[half, half_next, half, half,
                  half, half_next, half, half_next, half, half_next,
                  pl.BlockSpec((3, DUP), lambda i: (0, 0))],
        out_specs=(pl.BlockSpec((tt, DUP), lambda i: (i, 0)), pl.BlockSpec((3, DUP), lambda i: (0, 0)),
                   pl.BlockSpec((1, DUP), lambda i: (0, 0))),
        compiler_params=_params(("arbitrary",), VMEM_BIG),
    )(dy, dy, pre_g, pre_v, uv, uv, gl, gl, dgl, dgl, cfw)


def _mid_bwd(dx1, m2, za, zb, p4, wa, wb, wo, g_pm, tt):
    t = dx1.shape[0]

    def body(dx1_ref, m2_ref, za_ref, zb_ref, gt_ref, wa_ref, wb_ref, wo_ref, gpm_ref,
             dm2_ref, dza_ref, dzb_ref, dya_ref, dyb_ref, dp_ref, dgpm_ref):
        i = pl.program_id(0)
        dx1 = dx1_ref[...]
        r2, n2 = _rms_stats(m2_ref[...])
        dm2 = _rms_bwd(dx1 * gpm_ref[...], n2, r2).astype(BF16)
        dmix = _dot_nt(dm2, wo_ref[...])
        sa = _sig(gt_ref[:, 0:D])
        sb = _sig(gt_ref[:, D:2 * D])
        dza = (dmix * sa).astype(BF16)
        dzb = (dmix * sb).astype(BF16)
        dp_ref[:, 0:D] = (dmix * za_ref[...].astype(F32) * sa * (1.0 - sa)).astype(BF16)
        dp_ref[:, D:2 * D] = (dmix * zb_ref[...].astype(F32) * sb * (1.0 - sb)).astype(BF16)
        dm2_ref[...] = dm2
        dza_ref[...] = dza
        dzb_ref[...] = dzb
        dya_ref[...] = _dot_nt(dza, wa_ref[...]).astype(BF16)
        dyb_ref[...] = _dot_nt(dzb, wb_ref[...]).astype(BF16)
        dgpm = jnp.sum(dx1 * n2, axis=0, keepdims=True)

        @pl.when(i == 0)
        def _():
            dgpm_ref[...] = dgpm

        @pl.when(i > 0)
        def _():
            dgpm_ref[...] += dgpm

    row = lambda dt: jax.ShapeDtypeStruct((t, D), dt)
    tile = pl.BlockSpec((tt, D), lambda i: (i, 0))
    wsp = pl.BlockSpec((D, D), lambda i: (0, 0))
    vec = pl.BlockSpec((1, D), lambda i: (0, 0))
    gates = pl.BlockSpec((None, tt, 2 * D), lambda i: (SLOT_G, i, 0))
    return pl.pallas_call(
        body, name="mid_bwd",
        out_shape=(row(BF16), row(BF16), row(BF16), row(BF16), row(BF16),
                   jax.ShapeDtypeStruct((NCHIP, t, 2 * D), BF16), jax.ShapeDtypeStruct((1, D), F32)),
        grid=(t // tt,),
        in_specs=[tile, tile, tile, tile, gates, wsp, wsp, wsp, vec],
        out_specs=(tile, tile, tile, tile, tile, gates, vec),
        compiler_params=_params(("arbitrary",), VMEM_BIG),
    )(dx1, m2, za, zb, p4, wa, wb, wo, g_pm)


def _hgrn2_bwd(p4, o_all, ss, saved, dyb, dp4, logits, gnorm, p_early, tt):
    t = p4.shape[1]
    nt = t // tt
    nc = tt // CH
    ne = len(p_early)

    def body(p_ref, o_ref, ss_ref, sv_ref, dyb_ref, dp_in, lg_ref, gn_ref, *rest):
        del dp_in
        pe = rest[:ne]
        dp_ref, dlb_ref, dgn_ref = rest[ne:ne + 3]
        qe = rest[ne + 3:2 * ne + 3]
        dst, ssem, rsem = rest[2 * ne + 3:]
        i = pl.program_id(0)
        sends, arrive = _exchange_copies(qe, [(lambda chip, r=r: r.at[chip]) for r in pe], ssem, rsem)

        @pl.when(i == 0)
        def _():
            dst[...] = jnp.zeros((NH, HD, HD), F32)
            for cp in sends:
                cp.start()

        low = _chunk_masks(tt)
        lb = _lb_of(lg_ref)
        heads = [slice(h * HD, (h + 1) * HD) for h in range(NH)]
        sg, sgn, f, k, g, eg, eng, egu, qt, kt, kd = _hg_head_fwd(p_ref[0, :, 0:D], p_ref[0, :, D:2 * D], lb,
                                                                  (sv_ref[0], sv_ref[1]))
        qtb, ktb, kdb, vb = qt.astype(BF16), kt.astype(BF16), kd.astype(BF16), p_ref[1, :, 0:D].astype(BF16)
        decs = [jnp.exp(g[c * CH + CH - 1:c * CH + CH, :]) for c in range(nc)]
        og = p_ref[1, :, D:2 * D]
        so = _sig(og)
        dyb = dyb_ref[...].astype(F32)
        dob = dyb * (og * so)
        rn = [_rms_stats(o_ref[:, hs]) for hs in heads]
        r_all = jnp.concatenate([jnp.broadcast_to(r, (tt, HD)) for r, _ in rn], axis=1)
        n_all = jnp.concatenate([n for _, n in rn], axis=1)
        gd = dob * gn_ref[...]
        proj = jnp.concatenate(
            [jnp.broadcast_to(jnp.mean(gd[:, hs] * n_all[:, hs], axis=-1, keepdims=True), (tt, HD)) for hs in heads],
            axis=1)
        dob_ = (r_all * (gd - n_all * proj)).astype(BF16)
        dog = dyb * (n_all * gn_ref[...]) * (so * (1.0 + og * (1.0 - so)))
        dgn = jnp.sum(dob * n_all, axis=0, keepdims=True)
        dv_in, dqt_in, dkt_h = [], [], []
        for hs in heads:
            att = jnp.where(low, _dot_nt(qtb[:, hs], ktb[:, hs]), 0.0).astype(BF16)
            d_att = jnp.where(low, _dot_nt(dob_[:, hs], vb[:, hs]), 0.0).astype(BF16)
            dv_in.append(_dot_tn(att, dob_[:, hs]))
            dqt_in.append(_dot(d_att, ktb[:, hs]))
            dkt_h.append(_dot_tn(d_att, qtb[:, hs]))
        ds_t = [dst[h] for h in range(NH)]
        dv_p = [[None] * NH for _ in range(nc)]
        dqt_p = [[None] * NH for _ in range(nc)]
        dkd_p = [[None] * NH for _ in range(nc)]
        dgl_p = [[None] * NH for _ in range(nc)]
        for c in reversed(range(nc)):
            sl = slice(c * CH, (c + 1) * CH)
            for h, hs in enumerate(heads):
                s_prev = ss_ref[c, h]
                ds_bf = ds_t[h].astype(BF16)
                dec = decs[c][:, hs]
                dv_p[c][h] = dv_in[h][sl] + _dot_nt(kdb[sl, hs], ds_bf)
                dqt_p[c][h] = dqt_in[h][sl] + _dot(dob_[sl, hs], s_prev)
                dkd_p[c][h] = _dot(vb[sl, hs], ds_bf)
                ddec = jnp.sum(s_prev.astype(F32) * ds_t[h], axis=0, keepdims=True)
                dgl_p[c][h] = jnp.broadcast_to(ddec * dec, (CH, HD))
                ds_t[h] = ds_t[h] * dec + _dot_tn(dob_[sl, hs], qtb[sl, hs])
        for h in range(NH):
            dst[h] = ds_t[h]
        whole = lambda parts: jnp.concatenate([jnp.concatenate(row, axis=1) for row in parts], axis=0)
        dv, dqt, dkd, dgl = whole(dv_p), whole(dqt_p), whole(dkd_p), whole(dgl_p)
        dkt = jnp.concatenate(dkt_h, axis=1)
        dq = dqt * eg
        dk = dkt * eng + dkd * egu
        dg = dqt * qt - dkt * kt
        dgu = dkd * kd
        dlogf = _chunk_revcumsum(dg - dgu) + _chunk_total(dgu) + dgl
        common = sgn * (dlogf / f - dk)
        dfz = (1.0 - lb) * sg * common
        dlb = jnp.sum(common, axis=0, keepdims=True)
        dp_ref[0, :, 0:D] = dq.astype(BF16)
        dp_ref[0, :, D:2 * D] = dfz.astype(BF16)
        dp_ref[1, :, 0:D] = dv.astype(BF16)
        dp_ref[1, :, D:2 * D] = dog.astype(BF16)

        @pl.when(i == 0)
        def _():
            dlb_ref[0:1, :] = dlb
            dgn_ref[...] = dgn

        @pl.when(i > 0)
        def _():
            dlb_ref[0:1, :] += dlb
            dgn_ref[...] += dgn

        @pl.when(i == nt - 1)
        def _():
            d0 = dlb_ref[0:1, :] * lb * (1.0 - lb)
            dlb_ref[0:1, :] = d0
            dlb_ref[1:2, :] = -d0
            for cp in arrive:
                cp.wait_recv()
            for cp in sends:
                cp.wait_send()

    rev = lambda i: nt - 1 - i
    vec = pl.BlockSpec((1, D), lambda i: (0, 0))
    any_spec = pl.BlockSpec(memory_space=pl.ANY)
    outs = pl.pallas_call(
        body, name="hgrn2_bwd",
        out_shape=(jax.ShapeDtypeStruct(dp4.shape, BF16), jax.ShapeDtypeStruct((2, D), F32),
                   jax.ShapeDtypeStruct((1, D), F32)) + tuple(jax.ShapeDtypeStruct(a.shape, BF16) for a in p_early),
        grid=(nt,),
        in_specs=[pl.BlockSpec((2, tt, 2 * D), lambda i: (0, rev(i), 0)),
                  pl.BlockSpec((tt, D), lambda i: (rev(i), 0)),
                  pl.BlockSpec((nc, NH, HD, HD), lambda i: (rev(i), 0, 0, 0)),
                  pl.BlockSpec((2, tt, D), lambda i: (0, rev(i), 0)),
                  pl.BlockSpec((tt, D), lambda i: (rev(i), 0)),
                  any_spec,
                  pl.BlockSpec((2, D), lambda i: (0, 0)), vec] + [any_spec] * ne,
        out_specs=(pl.BlockSpec((2, tt, 2 * D), lambda i: (0, rev(i), 0)),
                   pl.BlockSpec((2, D), lambda i: (0, 0)), vec) + (any_spec,) * ne,
        scratch_shapes=[pltpu.VMEM((NH, HD, HD), F32), pltpu.SemaphoreType.DMA((3 * ne,)),
                        pltpu.SemaphoreType.DMA((3 * ne,))],
        input_output_aliases={5: 0},
        compiler_params=pltpu.CompilerParams(dimension_semantics=("arbitrary",), vmem_limit_bytes=VMEM_BIG,
                                             has_side_effects=True),
    )(p4, o_all, ss, saved, dyb, dp4, logits, gnorm, *p_early)
    return outs[0], outs[1], outs[2], list(outs[3:])


def _mixer_a_bwd(p4, hseq, saved, dya, dp4, cw, wa, wx, lam, tt):
    t = p4.shape[1]
    nt = t // tt

    def body(p_ref, sv_ref, h_ref, hh_ref, dya_ref, dp_in, cw_ref, wa_ref, wx_ref, lam_ref,
             dp_ref, dcw_ref, dcb_ref, dwa_ref, dwx_ref, dba_ref, dbx_ref, dlam_ref,
             dnext, dhc, afc):
        del dp_in
        i = pl.program_id(0)
        first_tile = i == nt - 1

        @pl.when(i == 0)
        def _():
            dnext[...] = jnp.zeros((8, D), F32)
            dhc[...] = jnp.zeros((1, D), F32)
            afc[...] = jnp.zeros((1, D), F32)

        xa = p_ref[:, 0:D]
        ga = p_ref[:, D:2 * D]
        xc, r, ig, a, mult = (sv_ref[idx] for idx in range(5))
        xcb = xc.astype(BF16)
        lam = lam_ref[...]
        sp = jnp.maximum(-lam, 0.0) + jnp.log1p(jnp.exp(-jnp.abs(lam)))
        h = h_ref[...]
        gl, dgl = _gelu_and_grad(ga)
        dya = dya_ref[...].astype(F32)
        dga = dya * h * dgl
        rows = lax.broadcasted_iota(jnp.int32, (tt, 1), 0)
        a_next = jnp.where(rows == tt - 1, afc[...], pltpu.roll(a, tt - 1, 0))
        dh = _scan_up(a_next, dya * gl, dhc[...])
        dhc[...] = dh[0:1, :]
        afc[...] = a[0:1, :]
        h_prev = jnp.where(rows == 0, jnp.where(first_tile, 0.0, hh_ref[7:8, :]), pltpu.roll(h, 1, 0))
        da = dh * h_prev
        dmult = dh * ig * xc
        di = dh * mult * xc
        dlog_a = da * a - dmult * a * a / mult
        dr = dlog_a * (-LRU_C * sp)
        dsp = jnp.sum(dlog_a * (-LRU_C * r), axis=0, keepdims=True)
        dra = dr * r * (1.0 - r)
        dix = di * ig * (1.0 - ig)
        drab = dra.astype(BF16)
        dixb = dix.astype(BF16)
        dxc_lin = []
        dwa_new = []
        dwx_new = []
        for n in range(NH):
            cs = slice(n * HD, (n + 1) * HD)
            dxc_lin.append(_dot_nt(drab[:, cs], wa_ref[n]) + _dot_nt(dixb[:, cs], wx_ref[n]))
            dwa_new.append(_dot_tn(xcb[:, cs], drab[:, cs]))
            dwx_new.append(_dot_tn(xcb[:, cs], dixb[:, cs]))
        dxc = dh * mult * ig + jnp.concatenate(dxc_lin, axis=1)
        de = jnp.concatenate([dxc, dnext[...]], axis=0)
        ups = [de[0:tt, :]] + [pltpu.roll(de, tt + 8 - s, 0)[0:tt, :] for s in (1, 2, 3)]
        dxa = sum(cw_ref[3 - s:4 - s, :] * ups[s] for s in range(4))
        dnext[...] = dxc[0:8, :]
        dp_ref[:, 0:D] = dxa.astype(BF16)
        dp_ref[:, D:2 * D] = dga.astype(BF16)
        dcw = jnp.concatenate(
            [jnp.sum(ups[3 - k] * xa, axis=0, keepdims=True) for k in range(4)], axis=0)
        dcb = jnp.sum(dxc, axis=0, keepdims=True)
        dba = jnp.sum(dra, axis=0, keepdims=True)
        dbx = jnp.sum(dix, axis=0, keepdims=True)
        dlam = dsp * (-_sig(-lam))

        @pl.when(i == 0)
        def _():
            dcw_ref[...] = dcw
            dcb_ref[...] = dcb
            dba_ref[...] = dba
            dbx_ref[...] = dbx
            dlam_ref[...] = dlam
            for n in range(NH):
                dwa_ref[n] = dwa_new[n]
                dwx_ref[n] = dwx_new[n]

        @pl.when(i > 0)
        def _():
            dcw_ref[...] += dcw
            dcb_ref[...] += dcb
            dba_ref[...] += dba
            dbx_ref[...] += dbx
            dlam_ref[...] += dlam
            for n in range(NH):
                dwa_ref[n] += dwa_new[n]
                dwx_ref[n] += dwx_new[n]

    rev = lambda i: nt - 1 - i
    hb = tt // 8
    full = lambda shape: pl.BlockSpec(shape, lambda i: (0,) * len(shape))
    vecs = jax.ShapeDtypeStruct((1, D), F32)
    blk = jax.ShapeDtypeStruct((NH, HD, HD), F32)
    return pl.pallas_call(
        body, name="mixer_a_bwd",
        out_shape=(jax.ShapeDtypeStruct(dp4.shape, BF16), jax.ShapeDtypeStruct((4, D), F32), vecs, blk, blk,
                   vecs, vecs, vecs),
        grid=(nt,),
        in_specs=[pl.BlockSpec((None, tt, 2 * D), lambda i: (SLOT_A, rev(i), 0)),
                  pl.BlockSpec((5, tt, D), lambda i: (0, rev(i), 0)),
                  pl.BlockSpec((tt, D), lambda i: (rev(i), 0)),
                  pl.BlockSpec((8, D), lambda i: (jnp.maximum(rev(i) * hb - 1, 0), 0)),
                  pl.BlockSpec((tt, D), lambda i: (rev(i), 0)),
                  pl.BlockSpec(memory_space=pl.ANY),
                  full((4, D)), full((NH, HD, HD)), full((NH, HD, HD)), full((1, D))],
        out_specs=(pl.BlockSpec((None, tt, 2 * D), lambda i: (SLOT_A, rev(i), 0)),
                   full((4, D)), full((1, D)), full((NH, HD, HD)), full((NH, HD, HD)),
                   full((1, D)), full((1, D)), full((1, D))),
        scratch_shapes=[pltpu.VMEM((8, D), F32), pltpu.VMEM((1, D), F32), pltpu.VMEM((1, D), F32)],
        input_output_aliases={5: 0},
        compiler_params=_params(("arbitrary",), VMEM_BIG),
    )(p4, saved, hseq, hseq, dya, dp4, cw, wa, wx, lam)


def _local_step(x, tgt, stk_w_in, stk_rest, conv_a_s, conv_f_s, small, jc, cidx):
    t = x.shape[0]
    tt = min(256, t)
    tm = min(1024, t)
    tk = min(2048, t)
    wa_bf = small["lru_wa"].astype(BF16)
    wx_bf = small["lru_wx"].astype(BF16)

    p4, h1, w_in, conv_a_g, conv_f_g = _mm_in_gather(x, small["norm_pre_mix"], stk_w_in, conv_a_s, conv_f_s, jc, tm)
    conv_a_w = jnp.transpose(conv_a_g, (1, 0, 2)).reshape(8, D)[0:4]
    conv_f_w = jnp.transpose(conv_f_g, (1, 0, 2)).reshape(8, DUP)[0:3]
    stk = dict(zip(REST, stk_rest))
    ya, hseq, saved_a, got_a = _mixer_a_fwd(p4, conv_a_w, small["conv_a_b"], wa_bf, wx_bf, small["lru_ba"],
                                            small["lru_bx"], small["lru_lambda"],
                                            [stk[n] for n in REST_A], REST_A, tt)
    yb, o_all, ss, saved_b, got_b = _hgrn2_fwd(p4, small["hg_lb_logits"], small["hg_norm_g"],
                                               [stk[n] for n in REST_B], REST_B, tt)
    w = dict(zip(REST_A + REST_B, _gather_forward(got_a + got_b, REST_A + REST_B)))
    w["w_in"] = w_in
    w_br_a = w["w_branch_a"].reshape(D, D)
    w_br_b = w["w_branch_b"].reshape(D, D)
    w_out = w["w_out"].reshape(D, D)
    w_down = w["w_down"].reshape(DFF, D)
    za, zb, mixb, m2, x1, h2 = _mid_fwd(ya, yb, p4, x, w_br_a, w_br_b, w_out, small["norm_post_mix"],
                                        small["norm_pre_ffn"], min(512, t))
    pre_g, pre_v, y, uv, gl, dgl = _up_act_fwd(h2, w["w_up"], conv_f_w, small["conv_f_b"], tt)
    dx2, dm3, lossv, d_norm_post_ffn = _down_loss(y, w_down, x1, tgt, small["norm_post_ffn"], min(512, t))

    d_w_down = _mm_tn(y, dm3, DFF // 2, D, tk, "mm_dw_down")
    dy = _mm_nt(dm3, w_down, BF16, tm, "mm_dy")
    dup_pre, d_conv_f_w, d_conv_f_b = _ffn_act_bwd(dy, pre_g, pre_v, uv, gl, dgl, conv_f_w, tt)
    d_w_up = _mm_tn(h2, dup_pre, D, SH_UP, tk, "mm_dw_up", stacked_out=True)
    dx1, d_norm_pre_ffn = _mm_dh2_norm(dup_pre, w["w_up"], dx2, x1, small["norm_pre_ffn"], tm)
    dm2, dza, dzb, dya, dyb, dp4, d_norm_post_mix = _mid_bwd(
        dx1, m2, za, zb, p4, w_br_a, w_br_b, w_out, small["norm_post_mix"], tt)
    d_w_out = _mm_tn(mixb, dm2, D, D, tm, "mm_dw_out")
    d_w_br_a = _mm_tn(ya, dza, D, D, tm, "mm_dw_bra")
    d_w_br_b = _mm_tn(yb, dzb, D, D, tm, "mm_dw_brb")
    early = {"w_branch_a": d_w_br_a.reshape(NCHIP, SH_BR, D), "w_branch_b": d_w_br_b.reshape(NCHIP, SH_BR, D),
             "w_out": d_w_out.reshape(NCHIP, SH_BR, D), "w_up": d_w_up, "w_down": d_w_down.reshape(NCHIP, SH_DN, D)}
    rb, _ = _reduce_stage1(early, REST, (), "reduce_d2d_in_early")
    p_rest = [_sum_own_half(early[n], rb[n], cidx, "sum_half_" + n) for n in REST]
    dp4, d_lb, d_hg_norm_g, q_rest = _hgrn2_bwd(p4, o_all, ss, saved_b, dyb, dp4, small["hg_lb_logits"], small["hg_norm_g"],
                                                p_rest, tt)
    dp4, d_conv_a_w, d_conv_a_b, d_lru_wa, d_lru_wx, d_lru_ba, d_lru_bx, d_lru_lambda = _mixer_a_bwd(
        p4, hseq, saved_a, dya, dp4, conv_a_w, wa_bf, wx_bf, small["lru_lambda"], tt)
    d_w_in = _mm_tn(h1, dp4, D, SH_IN, tk, "mm_dw_in", stacked_slot_fn=_slot_of_chip, stacked_out=True)
    rb, _ = _reduce_stage1({"w_in": d_w_in}, ("w_in",), (), "reduce_d2d_in_w_in")
    p_w_in = _sum_own_half(d_w_in, rb["w_in"], cidx, "sum_half_w_in")
    grad_x, q_w_in, d_norm_pre_mix = _mm_dh1_exchange(dp4, w_in, p_w_in, x, dx1, small["norm_pre_mix"], tm)

    smalls = {
        "norm_pre_mix": d_norm_pre_mix, "conv_a_b": d_conv_a_b, "lru_ba": d_lru_ba, "lru_bx": d_lru_bx,
        "lru_lambda": d_lru_lambda, "hg_lb_logits": d_lb, "hg_norm_g": d_hg_norm_g, "norm_post_mix": d_norm_post_mix,
        "norm_pre_ffn": d_norm_pre_ffn, "norm_post_ffn": d_norm_post_ffn, "lossv": lossv,
        "conv_a_w": d_conv_a_w, "lru_wa": d_lru_wa, "lru_wx": d_lru_wx,
        "conv_f_b": d_conv_f_b, "conv_f_w": d_conv_f_w,
    }
    p_big = dict(zip(REST, p_rest), w_in=p_w_in)
    q_big = dict(zip(REST, q_rest), w_in=q_w_in)
    return grad_x, p_big, q_big, smalls


BIG = ("w_in", "w_branch_a", "w_branch_b", "w_out", "w_up", "w_down")
BIG_SHAPE = {"w_in": (D, SH_IN), "w_branch_a": (SH_BR, D), "w_branch_b": (SH_BR, D), "w_out": (SH_BR, D),
             "w_up": (D, SH_UP), "w_down": (SH_DN, D)}
NBIG = len(BIG)
REST = BIG[1:]
REST_A = ("w_branch_a", "w_branch_b", "w_out", "w_down")
REST_B = ("w_up",)
VEC_ROWS = (("norm_pre_mix", 0, 1), ("conv_a_b", 1, 1), ("lru_ba", 2, 1), ("lru_bx", 3, 1), ("lru_lambda", 4, 1),
            ("hg_lb_logits", 5, 2), ("hg_norm_g", 7, 1), ("norm_post_mix", 8, 1), ("norm_pre_ffn", 9, 1),
            ("norm_post_ffn", 10, 1))
ROW_LOSS = 11
ROW_CONV_A = 12
S1_ROWS = 16
S2_ROWS = 8


def _place():
    x, y, c = lax.axis_index("x"), lax.axis_index("y"), lax.axis_index("c")
    chips = [(1 - x, y), (x, 1 - y), (1 - x, 1 - y)]
    return x, y, c, 2 * x + y, chips


def _remote(src, dst, ssem, rsem, dev):
    return pltpu.make_async_remote_copy(src_ref=src, dst_ref=dst, send_sem=ssem, recv_sem=rsem,
                                        device_id=dev, device_id_type=MESH)


def _hbm_call(body, name, ins, out_shapes, n_sems, aliases=None):
    any_spec = pl.BlockSpec(memory_space=pl.ANY)
    return pl.pallas_call(
        body, name=name, out_shape=tuple(out_shapes),
        in_specs=[any_spec] * len(ins), out_specs=tuple([any_spec] * len(out_shapes)),
        scratch_shapes=[pltpu.SemaphoreType.DMA((n,)) for n in n_sems],
        input_output_aliases=aliases or {},
        compiler_params=pltpu.CompilerParams(has_side_effects=True),
    )(*ins)


def _gather_copies(stk, names, ssem, rsem, fssem=None, frsem=None):
    x, y, c, j, chips = _place()
    sends, arrive, fwds, farrive = [], [], [], []
    for w, n in enumerate(names):
        hw = BIG_SHAPE[n][0] // 2
        mine = stk[w].at[j, pl.ds(c * hw, hw), :]
        for k, (cx, cy) in enumerate(chips):
            i = 3 * w + k
            got = stk[w].at[2 * cx + cy, pl.ds(c * hw, hw), :]
            other = stk[w].at[2 * cx + cy, pl.ds((1 - c) * hw, hw), :]
            sends.append(_remote(mine, mine, ssem.at[i], rsem.at[i], (cx, cy, c)))
            arrive.append(_remote(got, got, ssem.at[i], rsem.at[i], (cx, cy, c)))
            if fssem is not None:
                fwds.append(_remote(got, got, fssem.at[i], frsem.at[i], (x, y, 1 - c)))
                farrive.append(_remote(other, other, fssem.at[i], frsem.at[i], (x, y, 1 - c)))
    return sends, arrive, fwds, farrive


def _ici_leg_behind(stk, names, ssem, rsem, first, last_fn):
    sends, arrive, _, _ = _gather_copies(stk, names, ssem, rsem)

    @pl.when(first)
    def _():
        for cp in sends:
            cp.start()

    def finish():
        @pl.when(last_fn())
        def _():
            for cp in arrive:
                cp.wait_recv()
            for cp in sends:
                cp.wait_send()

    return finish


def _gather_forward(stk, names):
    nw = len(names)

    def body(*refs):
        dst = refs[nw:2 * nw]
        ssem, rsem, fssem, frsem = refs[2 * nw:]
        _, _, fwds, farrive = _gather_copies(dst, names, ssem, rsem, fssem, frsem)
        for cp in fwds:
            cp.start()
        for cp in farrive:
            cp.wait_recv()
        for cp in fwds:
            cp.wait_send()

    out_shapes = [jax.ShapeDtypeStruct(a.shape, a.dtype) for a in stk]
    return _hbm_call(body, "gather_forward", stk, out_shapes, (3 * nw,) * 4, aliases={w: w for w in range(nw)})


def _exchange_copies(dst, pieces, ssem, rsem):
    x, y, c, j, chips = _place()
    sends, arrive = [], []
    for w in range(len(dst)):
        for k, (cx, cy) in enumerate(chips):
            i = 3 * w + k
            sends.append(_remote(pieces[w](2 * cx + cy), dst[w].at[j], ssem.at[i], rsem.at[i], (cx, cy, c)))
            got = dst[w].at[2 * cx + cy]
            arrive.append(_remote(got, got, ssem.at[i], rsem.at[i], (cx, cy, c)))
    return sends, arrive


def _reduce_stage1(big_g, names, smalls, name):
    nb = len(names)
    ins = [big_g[n] for n in names] + list(smalls)
    n_in = len(ins)
    halves = [BIG_SHAPE[n][0] // 2 for n in names]
    out_shapes = [jax.ShapeDtypeStruct((NCHIP, halves[w], BIG_SHAPE[n][1]), big_g[n].dtype)
                  for w, n in enumerate(names)]
    out_shapes += [jax.ShapeDtypeStruct(a.shape, F32) for a in smalls]

    def body(*refs):
        src, dst = refs[:n_in], refs[n_in:2 * n_in]
        ssem, rsem = refs[2 * n_in:]
        x, y, c, _, _ = _place()
        cps = []
        for w in range(n_in):
            s_ = src[w].at[:, pl.ds((1 - c) * halves[w], halves[w]), :] if w < nb else src[w]
            cp = _remote(s_, dst[w], ssem.at[w], rsem.at[w], (x, y, 1 - c))
            cp.start()
            cps.append(cp)
        for cp in cps:
            cp.wait()

    outs = _hbm_call(body, name, ins, out_shapes, (n_in, n_in))
    return dict(zip(names, outs[:nb])), outs[nb:]


def _reduce_stage2(ps1, ps2, ps3):
    ins = [ps1, ps2, ps3]
    h1, h2, h3 = S1_ROWS // 2, DUP // 2, D
    out_shapes = [jax.ShapeDtypeStruct((NCHIP, h1, D), F32), jax.ShapeDtypeStruct((NCHIP, S2_ROWS, h2), F32),
                  jax.ShapeDtypeStruct((NCHIP, h3, HD), F32)]

    def body(*refs):
        src, dst = refs[:3], refs[3:6]
        ssem, rsem = refs[6:]
        c = lax.axis_index("c")
        pieces = [lambda chip: src[0].at[pl.ds(c * h1, h1), :],
                  lambda chip: src[1].at[:, pl.ds(c * h2, h2)],
                  lambda chip: src[2].at[pl.ds(c * h3, h3), :]]
        sends, arrive = _exchange_copies(dst, pieces, ssem, rsem)
        for cp in sends:
            cp.start()
        for cp in arrive:
            cp.wait_recv()
        for cp in sends:
            cp.wait_send()

    return _hbm_call(body, "reduce_ici_small", ins, out_shapes, (9, 9))


def _reduce_stage3(f_big, fs1, fs2, fs3):
    ins = [f_big[n] for n in BIG] + [fs1, fs2, fs3]
    n_in = len(ins)
    halves = [BIG_SHAPE[n][0] // 2 for n in BIG]
    h1, h2, h3 = S1_ROWS // 2, DUP // 2, D
    out_shapes = [jax.ShapeDtypeStruct(BIG_SHAPE[n], F32) for n in BIG]
    out_shapes += [jax.ShapeDtypeStruct((S1_ROWS, D), F32), jax.ShapeDtypeStruct((S2_ROWS, DUP), F32),
                   jax.ShapeDtypeStruct((2 * D, HD), F32)]

    def body(*refs):
        dst = refs[n_in:2 * n_in]
        ssem, rsem = refs[2 * n_in:]
        x, y, c, _, _ = _place()

        def place(w, which):
            if w < NBIG:
                return dst[w].at[pl.ds(which * halves[w], halves[w]), :]
            if w == NBIG:
                return dst[w].at[pl.ds(which * h1, h1), :]
            if w == NBIG + 1:
                return dst[w].at[:, pl.ds(which * h2, h2)]
            return dst[w].at[pl.ds(which * h3, h3), :]

        cps = [_remote(place(w, c), place(w, c), ssem.at[w], rsem.at[w], (x, y, 1 - c)) for w in range(n_in)]
        for cp in cps:
            cp.start()
        for w in range(n_in):
            got = place(w, 1 - c)
            _remote(got, got, ssem.at[w], rsem.at[w], (x, y, 1 - c)).wait_recv()
        for cp in cps:
            cp.wait_send()

    outs = _hbm_call(body, "reduce_d2d_out", ins, out_shapes, (n_in, n_in), aliases={w: w for w in range(n_in)})
    return dict(zip(BIG, outs[:NBIG])), outs[NBIG], outs[NBIG + 1], outs[NBIG + 2]


def _row_tile(rows):
    for tr in (128, 176, 64, 16, 8):
        if rows % tr == 0:
            return tr
    return rows


def _sum_own_half(g, rb, cidx, name):
    s, rows, cols = g.shape
    half = rows // 2
    tr = _row_tile(half)
    nb = half // tr

    def body(c_ref, g_ref, r_ref, o_ref):
        del c_ref
        o_ref[...] = (g_ref[...].astype(F32) + r_ref[...].astype(F32)).astype(BF16)

    grid_spec = pltpu.PrefetchScalarGridSpec(
        num_scalar_prefetch=1, grid=(s, nb),
        in_specs=[pl.BlockSpec((None, tr, cols), lambda k, i, c: (k, c[0] * nb + i, 0)),
                  pl.BlockSpec((None, tr, cols), lambda k, i, c: (k, i, 0))],
        out_specs=pl.BlockSpec((None, tr, cols), lambda k, i, c: (k, i, 0)))
    return pl.pallas_call(
        body, name=name, grid_spec=grid_spec, out_shape=jax.ShapeDtypeStruct((s, half, cols), BF16),
        compiler_params=_params(("parallel", "parallel")),
    )(cidx, g, rb)


def _sum_chips(q, p, jc, name, by_cols=False):
    s, rows, cols = q.shape
    tr = _row_tile(rows)
    nb = rows // tr
    stacked = p.ndim == 3

    def body(jc_ref, q_ref, p_ref, o_ref):
        j = jc_ref[0]
        own = p_ref[...].astype(F32)
        acc = None
        for k in range(NCHIP):
            term = jnp.where(j == k, own, q_ref[k].astype(F32))
            acc = term if acc is None else acc + term
        o_ref[...] = acc

    if by_cols:
        half_spec = pl.BlockSpec((tr, cols), lambda i, jc_ref: (i, jc_ref[1]))
        out_shape = jax.ShapeDtypeStruct((rows, 2 * cols), F32)
    else:
        half_spec = pl.BlockSpec((tr, cols), lambda i, jc_ref: (jc_ref[1] * nb + i, 0))
        out_shape = jax.ShapeDtypeStruct((2 * rows, cols), F32)
    p_spec = pl.BlockSpec((None, tr, cols), lambda i, jc_ref: (jc_ref[0], i, 0)) if stacked else half_spec
    grid_spec = pltpu.PrefetchScalarGridSpec(
        num_scalar_prefetch=1, grid=(nb,),
        in_specs=[pl.BlockSpec((s, tr, cols), lambda i, jc_ref: (0, i, 0)), p_spec],
        out_specs=half_spec)
    return pl.pallas_call(
        body, name=name, grid_spec=grid_spec, out_shape=out_shape,
        compiler_params=_params(("parallel",)),
    )(jc, q, p)


def _place_shard(w, jc, name):
    rows, cols = w.shape
    tr = _row_tile(rows)

    def body(jc_ref, w_ref, o_ref):
        del jc_ref
        o_ref[...] = w_ref[...].astype(BF16)

    grid_spec = pltpu.PrefetchScalarGridSpec(
        num_scalar_prefetch=1, grid=(rows // tr,),
        in_specs=[pl.BlockSpec((tr, cols), lambda i, jc_ref: (i, 0))],
        out_specs=pl.BlockSpec((None, tr, cols), lambda i, jc_ref: (jc_ref[0], i, 0)))
    return pl.pallas_call(
        body, name=name, grid_spec=grid_spec, out_shape=jax.ShapeDtypeStruct((NCHIP, rows, cols), BF16),
        compiler_params=_params(("parallel",)),
    )(jc, w)


def _add(a, b, name):
    def body(a_ref, b_ref, o_ref):
        o_ref[...] = a_ref[...] + b_ref[...]

    return pl.pallas_call(body, name=name, out_shape=jax.ShapeDtypeStruct(a.shape, F32))(a, b)


def _pack_small(sm):
    vec_in = [sm[n] for n, _, _ in VEC_ROWS]
    nv = len(vec_in)

    def body(*refs):
        ins, lossv, dcw, dcfb, dcfw, s1, s2 = refs[:nv], refs[nv], refs[nv + 1], refs[nv + 2], refs[nv + 3], \
            refs[nv + 4], refs[nv + 5]
        for ref, (_, r0, nr) in zip(ins, VEC_ROWS):
            s1[r0:r0 + nr, :] = ref[...]
        s1[ROW_LOSS:ROW_LOSS + 1, :] = lossv[...]
        s1[ROW_CONV_A:ROW_CONV_A + 4, :] = dcw[...]
        s2[0:1, :] = dcfb[...]
        s2[1:4, :] = dcfw[...]
        s2[4:8, :] = jnp.zeros((4, DUP), F32)

    return pl.pallas_call(
        body, name="pack_small",
        out_shape=(jax.ShapeDtypeStruct((S1_ROWS, D), F32), jax.ShapeDtypeStruct((S2_ROWS, DUP), F32)),
    )(*vec_in, sm["lossv"], sm["conv_a_w"], sm["conv_f_b"], sm["conv_f_w"])


def _adam_math(w, g, m, v):
    m = ADAM_B1 * m + (1.0 - ADAM_B1) * g
    v = ADAM_B2 * v + (1.0 - ADAM_B2) * (g * g)
    m_hat = m / (1.0 - ADAM_B1 ** ADAM_STEP)
    v_hat = v / (1.0 - ADAM_B2 ** ADAM_STEP)
    delta = -ADAM_LR * (m_hat / (jnp.sqrt(v_hat) + ADAM_EPS) + ADAM_WD * w)
    return delta, m, v


def _adam(w, g, m, v, name):
    rows, cols = w.shape
    tr = _row_tile(rows)

    def body(w_ref, g_ref, m_ref, v_ref, d_ref, mo_ref, vo_ref):
        d_ref[...], mo_ref[...], vo_ref[...] = _adam_math(w_ref[...], g_ref[...], m_ref[...], v_ref[...])

    spec = pl.BlockSpec((tr, cols), lambda i: (i, 0))
    return pl.pallas_call(
        body, name=name, out_shape=(jax.ShapeDtypeStruct(w.shape, F32),) * 3, grid=(rows // tr,),
        in_specs=[spec] * 4, out_specs=(spec,) * 3,
        compiler_params=_params(("parallel",)),
    )(w, g, m, v)


def _adam_small(gs1, gs2, gs3, w, m, v):
    names = [n for n, _, _ in VEC_ROWS] + ["conv_f_b", "lru_wa", "lru_wx"]
    nn = len(names)

    def grad_of(i, g1, g2, g3):
        if i < len(VEC_ROWS):
            _, r0, nr = VEC_ROWS[i]
            return g1[r0:r0 + nr, :]
        if names[i] == "conv_f_b":
            return g2[0:1, :]
        return g3[0] if names[i] == "lru_wa" else g3[1]

    def body(*refs):
        g1, g2, g3 = refs[0], refs[1], refs[2]
        ws, ms, vs = refs[3:3 + nn], refs[3 + nn:3 + 2 * nn], refs[3 + 2 * nn:3 + 3 * nn]
        outs = refs[3 + 3 * nn:]
        for i in range(nn):
            d, mn, vn = _adam_math(ws[i][...], grad_of(i, g1, g2, g3), ms[i][...], vs[i][...])
            outs[i][...] = d
            outs[nn + i][...] = mn
            outs[2 * nn + i][...] = vn

    shapes = [jax.ShapeDtypeStruct(w[n].shape, F32) for n in names]
    outs = pl.pallas_call(body, name="adam_small", out_shape=tuple(shapes * 3))(
        gs1, gs2, gs3, *[w[n] for n in names], *[m[n] for n in names], *[v[n] for n in names])
    return {n: (outs[i], outs[nn + i], outs[2 * nn + i]) for i, n in enumerate(names)}


WEIGHTS = ("norm_pre_mix", "w_in", "conv_a_w", "conv_a_b", "lru_wa", "lru_ba", "lru_wx", "lru_bx", "lru_lambda",
           "hg_lb_logits", "hg_norm_g", "w_branch_a", "w_branch_b", "w_out", "norm_post_mix", "norm_pre_ffn",
           "w_up", "conv_f_w", "conv_f_b", "w_down", "norm_post_ffn")
NW = len(WEIGHTS)


def kernel(x, norm_pre_mix, w_in, conv_a_w, conv_a_b, lru_wa, lru_ba, lru_wx, lru_bx, lru_lambda, hg_lb_logits, hg_norm_g, w_branch_a, w_branch_b, w_out, norm_post_mix, norm_pre_ffn, w_up, conv_f_w, conv_f_b, w_down, norm_post_ffn, loss_target, m_norm_pre_mix, m_w_in, m_conv_a_w, m_conv_a_b, m_lru_wa, m_lru_ba, m_lru_wx, m_lru_bx, m_lru_lambda, m_hg_lb_logits, m_hg_norm_g, m_w_branch_a, m_w_branch_b, m_w_out, m_norm_post_mix, m_norm_pre_ffn, m_w_up, m_conv_f_w, m_conv_f_b, m_w_down, m_norm_post_ffn, v_norm_pre_mix, v_w_in, v_conv_a_w, v_conv_a_b, v_lru_wa, v_lru_ba, v_lru_wx, v_lru_bx, v_lru_lambda, v_hg_lb_logits, v_hg_norm_g, v_w_branch_a, v_w_branch_b, v_w_out, v_norm_post_mix, v_norm_pre_ffn, v_w_up, v_conv_f_w, v_conv_f_b, v_w_down, v_norm_post_ffn):
    rest = (norm_pre_mix, w_in, conv_a_w, conv_a_b, lru_wa, lru_ba, lru_wx, lru_bx, lru_lambda, hg_lb_logits, hg_norm_g, w_branch_a, w_branch_b, w_out, norm_post_mix, norm_pre_ffn, w_up, conv_f_w, conv_f_b, w_down, norm_post_ffn, loss_target, m_norm_pre_mix, m_w_in, m_conv_a_w, m_conv_a_b, m_lru_wa, m_lru_ba, m_lru_wx, m_lru_bx, m_lru_lambda, m_hg_lb_logits, m_hg_norm_g, m_w_branch_a, m_w_branch_b, m_w_out, m_norm_post_mix, m_norm_pre_ffn, m_w_up, m_conv_f_w, m_conv_f_b, m_w_down, m_norm_post_ffn, v_norm_pre_mix, v_w_in, v_conv_a_w, v_conv_a_b, v_lru_wa, v_lru_ba, v_lru_wx, v_lru_bx, v_lru_lambda, v_hg_lb_logits, v_hg_norm_g, v_w_branch_a, v_w_branch_b, v_w_out, v_norm_post_mix, v_norm_pre_ffn, v_w_up, v_conv_f_w, v_conv_f_b, v_w_down, v_norm_post_ffn)
    w_in_args = dict(zip(WEIGHTS, rest[:NW]))
    loss_target = rest[NW]
    m_args = dict(zip(WEIGHTS, rest[NW + 1:2 * NW + 1]))
    v_args = dict(zip(WEIGHTS, rest[2 * NW + 1:3 * NW + 1]))
    shape_of = {n: w_in_args[n].shape for n in WEIGHTS}

    def two_d(n, a):
        if n in BIG:
            return a.reshape(BIG_SHAPE[n])
        if n in ("lru_wa", "lru_wx"):
            return a.reshape(NH, HD, HD)
        return a.reshape(a.shape[-2:])

    w2 = {n: two_d(n, w_in_args[n]) for n in WEIGHTS}
    m2 = {n: two_d(n, m_args[n]) for n in WEIGHTS}
    v2 = {n: two_d(n, v_args[n]) for n in WEIGHTS}

    cidx = lax.axis_index("c").astype(jnp.int32).reshape(1)
    jchip = 2 * lax.axis_index("x") + lax.axis_index("y")

    jc = jnp.stack([jchip, lax.axis_index("c")]).astype(jnp.int32)

    shards = {n: _place_shard(w2[n], jc, "place_" + n) for n in BIG}
    conv_a_s = jnp.pad(w2["conv_a_w"], ((0, 4), (0, 0)))
    conv_f_s = jnp.pad(w2["conv_f_w"], ((0, 5), (0, 0)))
    small = {n: w2[n] for n in WEIGHTS if n not in BIG and n not in ("conv_a_w", "conv_f_w")}

    grad_x, p_big, q_big, sm_g = _local_step(
        x[0], loss_target[0], shards["w_in"], [shards[n] for n in REST], conv_a_s, conv_f_s, small, jc, cidx)

    s1, s2 = _pack_small(sm_g)
    s3 = jnp.concatenate([sm_g["lru_wa"].reshape(D, HD), sm_g["lru_wx"].reshape(D, HD)], axis=0)
    _, (rs1, rs2, rs3) = _reduce_stage1({}, (), (s1, s2, s3), "reduce_d2d_in_small")
    ps1, ps2, ps3 = _add(s1, rs1, "add_s1"), _add(s2, rs2, "add_s2"), _add(s3, rs3, "add_s3")
    qs1, qs2, qs3 = _reduce_stage2(ps1, ps2, ps3)
    f_big = {n: _sum_chips(q_big[n], p_big[n], jc, "sum_chips_" + n) for n in BIG}
    fs1 = _sum_chips(qs1, ps1, jc, "sum_chips_s1")
    fs2 = _sum_chips(qs2, ps2, jc, "sum_chips_s2", by_cols=True)
    fs3 = _sum_chips(qs3, ps3, jc, "sum_chips_s3")
    g_big, gs1, gs2, gs3 = _reduce_stage3(f_big, fs1, fs2, fs3)

    res = {}
    for n in BIG:
        d, mn, vn = _adam(w2[n], g_big[n], m2[n], v2[n], "adam_" + n)
        res[n] = (g_big[n], d, mn, vn)
    small_res = _adam_small(gs1, gs2, gs3.reshape(2, NH, HD, HD), w2, m2, v2)
    for n, r0, nr in VEC_ROWS:
        res[n] = (gs1[r0:r0 + nr],) + small_res[n]
    res["conv_f_b"] = (gs2[0:1],) + small_res["conv_f_b"]
    res["lru_wa"] = (gs3[0:D].reshape(NH, HD, HD),) + small_res["lru_wa"]
    res["lru_wx"] = (gs3[D:2 * D].reshape(NH, HD, HD),) + small_res["lru_wx"]
    g_ca = lax.dynamic_slice_in_dim(gs1[ROW_CONV_A:ROW_CONV_A + 4], jchip * (D // NCHIP), D // NCHIP, axis=1)
    g_cf = lax.dynamic_slice_in_dim(gs2[1:4], jchip * SH_UP, SH_UP, axis=1)
    res["conv_a_w"] = (g_ca,) + _adam(w2["conv_a_w"], g_ca, m2["conv_a_w"], v2["conv_a_w"], "adam_conv_a_w")
    res["conv_f_w"] = (g_cf,) + _adam(w2["conv_f_w"], g_cf, m2["conv_f_w"], v2["conv_f_w"], "adam_conv_f_w")

    loss = (0.5 / D) * jnp.sum(gs1[ROW_LOSS])
    out = [loss, grad_x.reshape(x.shape)]
    for part in range(4):
        out += [res[n][part].reshape(shape_of[n]) for n in WEIGHTS]
    return tuple(out)
```

```python
import jax
import jax.numpy as jnp
from jax import lax
from jax.experimental import pallas as pl
from jax.experimental.pallas import tpu as pltpu

F32 = jnp.float32
BF16 = jnp.bfloat16

D = 1024
NH = 8
HD = 128
CH = 32
DFF = 2816
DUP = 2 * DFF
NCHIP = 4
SH_IN = 2 * D
SH_UP = DUP // NCHIP
SH_DN = DFF // NCHIP
SH_BR = D // NCHIP
EPS = 1e-6
LRU_C = 8.0
ADAM_LR = 0.001
ADAM_B1 = 0.9
ADAM_B2 = 0.999
ADAM_EPS = 1e-08
ADAM_WD = 0.01
ADAM_STEP = 10
VMEM_BIG = 56 * 1024 * 1024
MESH = pl.DeviceIdType.MESH

SLOT_A, SLOT_B, SLOT_C, SLOT_G = 2, 0, 1, 3


def _slot_of_chip(s):
    return jnp.where(s == 3, 3, (s + 2) % 3)


def _params(sem, vmem=None):
    return pltpu.CompilerParams(dimension_semantics=sem, vmem_limit_bytes=vmem)


_GC = 0.7978845608028654
_GA = 0.044715


def _gelu(x):
    return 0.5 * x * (1.0 + jnp.tanh(_GC * (x + _GA * x * x * x)))


def _gelu_and_grad(x):
    x2 = x * x
    th = jnp.tanh(_GC * x * (1.0 + _GA * x2))
    g = 0.5 * x * (1.0 + th)
    dg = 0.5 * (1.0 + th) + 0.5 * x * (1.0 - th * th) * _GC * (1.0 + 3.0 * _GA * x2)
    return g, dg


def _sig(x):
    return jax.nn.sigmoid(x)


def _dot(a, b):
    return jnp.dot(a, b, preferred_element_type=F32)


def _dot_nt(a, b):
    return lax.dot_general(a, b, (((1,), (1,)), ((), ())), preferred_element_type=F32)


def _dot_tn(a, b):
    return lax.dot_general(a, b, (((0,), (0,)), ((), ())), preferred_element_type=F32)


def _chunk_cumsum(x):
    pos = lax.broadcasted_iota(jnp.int32, (x.shape[0], 1), 0) & (CH - 1)
    d = 1
    while d < CH:
        x = x + jnp.where(pos >= d, pltpu.roll(x, d, 0), 0.0)
        d *= 2
    return x


def _chunk_revcumsum(x):
    n = x.shape[0]
    pos = lax.broadcasted_iota(jnp.int32, (n, 1), 0) & (CH - 1)
    d = 1
    while d < CH:
        x = x + jnp.where(pos < CH - d, pltpu.roll(x, n - d, 0), 0.0)
        d *= 2
    return x


def _chunk_last(x):
    n = x.shape[0]
    return jnp.concatenate(
        [jnp.broadcast_to(x[c * CH + CH - 1:c * CH + CH, :], (CH, x.shape[1])) for c in range(n // CH)], axis=0)


def _chunk_total(x):
    n = x.shape[0]
    return jnp.concatenate(
        [jnp.broadcast_to(jnp.sum(x[c * CH:(c + 1) * CH, :], axis=0, keepdims=True), (CH, x.shape[1]))
         for c in range(n // CH)], axis=0)


def _rms_stats(x):
    r = lax.rsqrt(jnp.mean(x * x, axis=-1, keepdims=True) + EPS)
    return r, x * r


def _rms_bwd(gd, n, r):
    return r * (gd - n * jnp.mean(gd * n, axis=-1, keepdims=True))


def _scan_down(a, u, carry):
    n = a.shape[0]
    pos = lax.broadcasted_iota(jnp.int32, (n, 1), 0) & 7
    for d in (1, 2, 4):
        u = a * jnp.where(pos >= d, pltpu.roll(u, d, 0), 0.0) + u
        a = a * jnp.where(pos >= d, pltpu.roll(a, d, 0), 1.0)
    out = []
    for v in range(n // 8):
        h = a[v * 8:v * 8 + 8, :] * carry + u[v * 8:v * 8 + 8, :]
        carry = h[7:8, :]
        out.append(h)
    return jnp.concatenate(out, axis=0)


def _scan_up(b, g, carry):
    n = b.shape[0]
    pos = lax.broadcasted_iota(jnp.int32, (n, 1), 0) & 7
    for d in (1, 2, 4):
        g = g + b * jnp.where(pos < 8 - d, pltpu.roll(g, n - d, 0), 0.0)
        b = b * jnp.where(pos < 8 - d, pltpu.roll(b, n - d, 0), 1.0)
    out = [None] * (n // 8)
    for v in reversed(range(n // 8)):
        h = g[v * 8:v * 8 + 8, :] + b[v * 8:v * 8 + 8, :] * carry
        carry = h[0:1, :]
        out[v] = h
    return jnp.concatenate(out, axis=0)


def _peer_of_step(s):
    return ((s & 1) << 1) | (s >> 1)


def _mm_in_gather(xin, gain, stk_w_in, conv_a_s, conv_f_s, jc, tm):
    m, k = xin.shape
    s_n, _, ns = stk_w_in.shape
    mt = m // tm

    def body(jc_ref, x_ref, g_ref, w_in_any, ca_src, cf_src, o_ref, h_ref, w_full, ca_dst, cf_dst,
             wbuf, hbuf, ssem_w, rsem_w, fssem_w, frsem_w, csend, crecv, lsem, wsem):
        del w_in_any
        s, i = pl.program_id(0), pl.program_id(1)
        x, y, c, j, chips = _place()
        sends_w, arrive_w, fwds_w, farrive_w = _gather_copies([w_full], ("w_in",), ssem_w, rsem_w, fssem_w, frsem_w)
        conv = ((ca_src, ca_dst), (cf_src, cf_dst))
        locs = [pltpu.make_async_copy(src, dst.at[j], lsem.at[n]) for n, (src, dst) in enumerate(conv)]
        csends = [_remote(src, dst.at[j], csend.at[3 * n + kk], crecv.at[3 * n + kk], (cx, cy, c))
                  for n, (src, dst) in enumerate(conv) for kk, (cx, cy) in enumerate(chips)]

        def fetch(step):
            return pltpu.make_async_copy(w_full.at[j ^ _peer_of_step(step)], wbuf.at[step & 1], wsem.at[step & 1])

        @pl.when((s == 0) & (i == 0))
        def _():
            for cp in sends_w[0:2] + locs + csends:
                cp.start()
            fetch(s).start()

        @pl.when(i == 0)
        def _():
            fetch(s).wait()

        @pl.when(s == 0)
        def _():
            _, n = _rms_stats(x_ref[...])
            h = (n * g_ref[...]).astype(BF16)
            hbuf[i] = h
            h_ref[...] = h

        o_ref[...] = _dot(hbuf[i], wbuf[s & 1])

        @pl.when((s == 0) & (i == mt - 1))
        def _():
            for kk in (0, 1):
                arrive_w[kk].wait_recv()
                fwds_w[kk].start()
            sends_w[2].start()

        @pl.when((s == 1) & (i == mt - 1))
        def _():
            arrive_w[2].wait_recv()
            fwds_w[2].start()

        for kk in range(3):
            @pl.when((s == kk) & (i == mt - 1))
            def _(kk=kk):
                farrive_w[kk].wait_recv()
                fetch(s + 1).start()

        @pl.when((s == s_n - 1) & (i == mt - 1))
        def _():
            for n, (_, dst) in enumerate(conv):
                for kk, (cx, cy) in enumerate(chips):
                    got = dst.at[2 * cx + cy]
                    _remote(got, got, csend.at[3 * n + kk], crecv.at[3 * n + kk], (cx, cy, c)).wait_recv()
            for cp in sends_w + fwds_w + csends:
                cp.wait_send()
            for cp in locs:
                cp.wait()

    any_spec = pl.BlockSpec(memory_space=pl.ANY)
    sem = pltpu.SemaphoreType.DMA
    grid_spec = pltpu.PrefetchScalarGridSpec(
        num_scalar_prefetch=1, grid=(s_n, mt),
        in_specs=[pl.BlockSpec((tm, k), lambda s, i, jc_ref: (jnp.where(s == 0, i, mt - 1), 0)),
                  pl.BlockSpec((1, k), lambda s, i, jc_ref: (0, 0)), any_spec, any_spec, any_spec],
        out_specs=(pl.BlockSpec((None, tm, ns),
                                lambda s, i, jc_ref: (_slot_of_chip(jc_ref[0] ^ _peer_of_step(s)), i, 0)),
                   pl.BlockSpec((tm, k), lambda s, i, jc_ref: (jnp.where(s == 0, i, mt - 1), 0)),
                   any_spec, any_spec, any_spec),
        scratch_shapes=[pltpu.VMEM((2, k, ns), BF16), pltpu.VMEM((mt, tm, k), BF16),
                        sem((3,)), sem((3,)), sem((3,)), sem((3,)), sem((6,)), sem((6,)), sem((2,)), sem((2,))])
    return pl.pallas_call(
        body, name="mm_in", grid_spec=grid_spec,
        out_shape=(jax.ShapeDtypeStruct((s_n, m, ns), F32), jax.ShapeDtypeStruct((m, k), BF16),
                   jax.ShapeDtypeStruct(stk_w_in.shape, BF16))
        + tuple(jax.ShapeDtypeStruct((NCHIP,) + v.shape, v.dtype) for v in (conv_a_s, conv_f_s)),
        input_output_aliases={3: 2},
        compiler_params=pltpu.CompilerParams(dimension_semantics=("arbitrary", "arbitrary"),
                                             vmem_limit_bytes=VMEM_BIG, has_side_effects=True),
    )(jc, xin, gain, stk_w_in, conv_a_s, conv_f_s)


def _mm_dh1_exchange(dp4, w_in3, p_w_in, ps3, x, dx1, gain, tm):
    s, k, ns = w_in3.shape
    m = dp4.shape[1]
    mt = m // tm

    def body(a_ref, b_ref, p_ref, p3_ref, x_ref, dx1_ref, g_ref, o_ref, q_ref, q3_ref, dg_ref, ssem, rsem):
        i, j = pl.program_id(0), pl.program_id(1)
        c = lax.axis_index("c")
        sends, arrive = _exchange_copies(
            [q_ref, q3_ref], [lambda chip: p_ref.at[chip], lambda chip: p3_ref.at[pl.ds(c * D, D), :]], ssem, rsem)

        @pl.when((i == 0) & (j == 0))
        def _():
            for cp in sends:
                cp.start()

        @pl.when(j == 0)
        def _():
            o_ref[...] = _dot_nt(a_ref[...], b_ref[...])

        @pl.when(j > 0)
        def _():
            o_ref[...] += _dot_nt(a_ref[...], b_ref[...])

        @pl.when(j == s - 1)
        def _():
            dh = o_ref[...]
            r, n = _rms_stats(x_ref[...])
            o_ref[...] = dx1_ref[...] + _rms_bwd(dh * g_ref[...], n, r)
            dgv = jnp.sum(dh * n, axis=0, keepdims=True)

            @pl.when(i == 0)
            def _():
                dg_ref[...] = dgv

            @pl.when(i > 0)
            def _():
                dg_ref[...] += dgv

        @pl.when((i == mt - 1) & (j == s - 1))
        def _():
            for cp in arrive:
                cp.wait_recv()
            for cp in sends:
                cp.wait_send()

    any_spec = pl.BlockSpec(memory_space=pl.ANY)
    row_tile = pl.BlockSpec((tm, k), lambda i, j: (i, 0))
    vec = pl.BlockSpec((1, k), lambda i, j: (0, 0))
    return pl.pallas_call(
        body, name="mm_dh1",
        out_shape=(jax.ShapeDtypeStruct((m, k), F32), jax.ShapeDtypeStruct(p_w_in.shape, BF16),
                   jax.ShapeDtypeStruct((NCHIP, D, HD), F32), jax.ShapeDtypeStruct((1, k), F32)),
        grid=(mt, s),
        in_specs=[pl.BlockSpec((None, tm, ns), lambda i, j: (_slot_of_chip(j), i, 0)),
                  pl.BlockSpec((None, k, ns), lambda i, j: (j, 0, 0)), any_spec, any_spec, row_tile, row_tile, vec],
        out_specs=(row_tile, any_spec, any_spec, vec),
        scratch_shapes=[pltpu.SemaphoreType.DMA((6,)), pltpu.SemaphoreType.DMA((6,))],
        compiler_params=pltpu.CompilerParams(dimension_semantics=("arbitrary", "arbitrary"),
                                             vmem_limit_bytes=VMEM_BIG, has_side_effects=True),
    )(dp4, w_in3, p_w_in, ps3, x, dx1, gain)


def _mm_dh2_norm(dup, w_up4, dx2, x1, gain, tm):
    s, k, ns = w_up4.shape
    m = dup.shape[0]

    def body(a_ref, b_ref, dx2_ref, x1_ref, g_ref, o_ref, dg_ref):
        i, j = pl.program_id(0), pl.program_id(1)

        @pl.when(j == 0)
        def _():
            o_ref[...] = _dot_nt(a_ref[...], b_ref[...])

        @pl.when(j > 0)
        def _():
            o_ref[...] += _dot_nt(a_ref[...], b_ref[...])

        @pl.when(j == s - 1)
        def _():
            dh = o_ref[...]
            r, n = _rms_stats(x1_ref[...])
            o_ref[...] = dx2_ref[...] + _rms_bwd(dh * g_ref[...], n, r)
            dgv = jnp.sum(dh * n, axis=0, keepdims=True)

            @pl.when(i == 0)
            def _():
                dg_ref[...] = dgv

            @pl.when(i > 0)
            def _():
                dg_ref[...] += dgv

    row_tile = pl.BlockSpec((tm, k), lambda i, j: (i, 0))
    vec = pl.BlockSpec((1, k), lambda i, j: (0, 0))
    return pl.pallas_call(
        body, name="mm_dh2",
        out_shape=(jax.ShapeDtypeStruct((m, k), F32), jax.ShapeDtypeStruct((1, k), F32)),
        grid=(m // tm, s),
        in_specs=[pl.BlockSpec((tm, ns), lambda i, j: (i, j)), pl.BlockSpec((None, k, ns), lambda i, j: (j, 0, 0)),
                  row_tile, row_tile, vec],
        out_specs=(row_tile, vec),
        compiler_params=_params(("arbitrary", "arbitrary"), VMEM_BIG),
    )(dup, w_up4, dx2, x1, gain)


def _mm_nt(a, b, out_dtype, tm, name):
    m, k = a.shape
    n = b.shape[0]

    def body(a_ref, b_ref, o_ref):
        o_ref[...] = _dot_nt(a_ref[...], b_ref[...]).astype(out_dtype)

    return pl.pallas_call(
        body, name=name, out_shape=jax.ShapeDtypeStruct((m, n), out_dtype), grid=(m // tm,),
        in_specs=[pl.BlockSpec((tm, k), lambda i: (i, 0)), pl.BlockSpec((n, k), lambda i: (0, 0))],
        out_specs=pl.BlockSpec((tm, n), lambda i: (i, 0)),
        compiler_params=_params(("parallel",), VMEM_BIG),
    )(a, b)


def _mm_tn(a, g, tkk, tn, tk, name, stacked_slot_fn=None, stacked_out=False):
    m, k = a.shape
    if stacked_slot_fn is not None:
        n = g.shape[0] * g.shape[2]
        g_spec = pl.BlockSpec((None, tk, tn), lambda kk, j, mm: (stacked_slot_fn(j), mm, 0))
    else:
        n = g.shape[1]
        g_spec = pl.BlockSpec((tk, tn), lambda kk, j, mm: (mm, j))
    steps = m // tk

    def body(a_ref, g_ref, o_ref, acc_ref):
        mm = pl.program_id(2)

        @pl.when(mm == 0)
        def _():
            acc_ref[...] = _dot_tn(a_ref[...], g_ref[...])

        @pl.when(mm > 0)
        def _():
            acc_ref[...] += _dot_tn(a_ref[...], g_ref[...])

        @pl.when(mm == steps - 1)
        def _():
            o_ref[...] = acc_ref[...].astype(BF16)

    if stacked_out:
        out_shape = jax.ShapeDtypeStruct((n // tn, k, tn), BF16)
        out_spec = pl.BlockSpec((None, tkk, tn), lambda kk, j, mm: (j, kk, 0))
    else:
        out_shape = jax.ShapeDtypeStruct((k, n), BF16)
        out_spec = pl.BlockSpec((tkk, tn), lambda kk, j, mm: (kk, j))
    return pl.pallas_call(
        body, name=name, out_shape=out_shape, grid=(k // tkk, n // tn, steps),
        in_specs=[pl.BlockSpec((tk, tkk), lambda kk, j, mm: (mm, kk)), g_spec],
        out_specs=out_spec,
        scratch_shapes=[pltpu.VMEM((tkk, tn), F32)],
        compiler_params=_params(("parallel", "parallel", "arbitrary"), VMEM_BIG),
    )(a, g)


def _lru_gates(xc, wa_ref, wx_ref, ba, bx, lam):
    xcb = xc.astype(BF16)
    ra = jnp.concatenate([_dot(xcb[:, n * HD:(n + 1) * HD], wa_ref[n]) for n in range(NH)], axis=1) + ba
    ix = jnp.concatenate([_dot(xcb[:, n * HD:(n + 1) * HD], wx_ref[n]) for n in range(NH)], axis=1) + bx
    r = _sig(ra)
    ig = _sig(ix)
    z = -lam
    sp = jnp.maximum(z, 0.0) + jnp.log1p(jnp.exp(-jnp.abs(z)))
    log_a = -LRU_C * r * sp
    a = jnp.exp(log_a)
    z2 = 2.0 * log_a
    series = -z2 * (1.0 + z2 * (0.5 + z2 * (1.0 / 6.0 + z2 * (1.0 / 24.0))))
    om = jnp.where(z2 > -0.02, series, 1.0 - jnp.exp(z2))
    mult = jnp.sqrt(om)
    return xcb, r, ig, sp, a, mult


def _mixer_a_fwd(p4, cw, cb, wa, wx, ba, bx, lam, stk, names, tt):
    t = p4.shape[1]
    ng = len(stk)

    def body(p_ref, cw_ref, cb_ref, wa_ref, wx_ref, ba_ref, bx_ref, lam_ref, *rest):
        ya_ref, h_ref, sv_ref = rest[ng:ng + 3]
        halo, hc, ssem, rsem = rest[2 * ng + 3:]
        i = pl.program_id(0)
        finish = _ici_leg_behind(rest[ng + 3:2 * ng + 3], names, ssem, rsem, i == 0, lambda: i == t // tt - 1)

        @pl.when(i == 0)
        def _():
            halo[...] = jnp.zeros((8, D), F32)
            hc[...] = jnp.zeros((1, D), F32)

        xa = p_ref[:, 0:D]
        ga = p_ref[:, D:2 * D]
        xe = jnp.concatenate([halo[...], xa], axis=0)
        xc = (cb_ref[...] + cw_ref[3:4, :] * xe
              + sum(cw_ref[3 - s:4 - s, :] * pltpu.roll(xe, s, 0) for s in (1, 2, 3)))[8:, :]
        halo[...] = xa[tt - 8:, :]
        _, r, ig, _, a, mult = _lru_gates(xc, wa_ref, wx_ref, ba_ref[...], bx_ref[...], lam_ref[...])
        u = mult * ig * xc
        h = _scan_down(a, u, hc[...])
        hc[...] = h[tt - 1:tt, :]
        h_ref[...] = h
        ya_ref[...] = (h * _gelu(ga)).astype(BF16)
        for idx, val in enumerate((xc, r, ig, a, mult)):
            sv_ref[idx] = val
        finish()

    full = lambda shape: pl.BlockSpec(shape, lambda i: (0,) * len(shape))
    any_spec = pl.BlockSpec(memory_space=pl.ANY)
    outs = pl.pallas_call(
        body, name="mixer_a_fwd",
        out_shape=(jax.ShapeDtypeStruct((t, D), BF16), jax.ShapeDtypeStruct((t, D), F32),
                   jax.ShapeDtypeStruct((5, t, D), F32)) + tuple(jax.ShapeDtypeStruct(v.shape, v.dtype) for v in stk),
        grid=(t // tt,),
        in_specs=[pl.BlockSpec((None, tt, 2 * D), lambda i: (SLOT_A, i, 0)),
                  full((4, D)), full((1, D)), full((NH, HD, HD)), full((NH, HD, HD)),
                  full((1, D)), full((1, D)), full((1, D))] + [any_spec] * ng,
        out_specs=(pl.BlockSpec((tt, D), lambda i: (i, 0)), pl.BlockSpec((tt, D), lambda i: (i, 0)),
                   pl.BlockSpec((5, tt, D), lambda i: (0, i, 0))) + (any_spec,) * ng,
        scratch_shapes=[pltpu.VMEM((8, D), F32), pltpu.VMEM((1, D), F32),
                        pltpu.SemaphoreType.DMA((3 * ng,)), pltpu.SemaphoreType.DMA((3 * ng,))],
        input_output_aliases={8 + w: 3 + w for w in range(ng)},
        compiler_params=pltpu.CompilerParams(dimension_semantics=("arbitrary",), vmem_limit_bytes=VMEM_BIG,
                                             has_side_effects=True),
    )(p4, cw, cb, wa, wx, ba, bx, lam, *stk)
    return outs[0], outs[1], outs[2], list(outs[3:])


def _chunk_masks(tt):
    row = lax.broadcasted_iota(jnp.int32, (tt, tt), 0)
    col = lax.broadcasted_iota(jnp.int32, (tt, tt), 1)
    same = jnp.right_shift(row, 5) == jnp.right_shift(col, 5)
    return same & (col <= row)


def _hg_head_fwd(q, fz, lbh, saved=None):
    sgn = _sig(-fz)
    k = (1.0 - lbh) * sgn
    if saved is None:
        sg = _sig(fz)
        f = lbh + (1.0 - lbh) * sg
        g = _chunk_cumsum(jnp.log(f))
    else:
        sg, g = saved
        f = lbh + (1.0 - lbh) * sg
    gu = _chunk_last(g) - g
    eg = jnp.exp(g)
    eng = jnp.exp(-g)
    egu = jnp.exp(gu)
    qt = q * eg
    kt = k * eng
    kd = k * egu
    return sg, sgn, f, k, g, eg, eng, egu, qt, kt, kd


def _lb_of(logits_ref):
    return _sig(logits_ref[0:1, :] - logits_ref[1:2, :])


def _hgrn2_fwd(p4, logits, gnorm, stk, names, tt):
    t = p4.shape[1]
    nc = tt // CH
    ng = len(stk)

    def body(p_ref, lg_ref, gn_ref, *rest):
        yb_ref, o_ref, ss_ref, sv_ref = rest[ng:ng + 4]
        st, ssem, rsem = rest[2 * ng + 4:]
        i = pl.program_id(0)
        finish = _ici_leg_behind(rest[ng + 4:2 * ng + 4], names, ssem, rsem, i == 0, lambda: i == t // tt - 1)

        @pl.when(i == 0)
        def _():
            st[...] = jnp.zeros((NH, HD, HD), F32)

        low = _chunk_masks(tt)
        lb = _lb_of(lg_ref)
        heads = [slice(h * HD, (h + 1) * HD) for h in range(NH)]
        sg, _, _, _, g, _, _, _, qt, kt, kd = _hg_head_fwd(p_ref[0, :, 0:D], p_ref[0, :, D:2 * D], lb)
        sv_ref[0] = sg
        sv_ref[1] = g
        qtb, ktb, kdb, vb = qt.astype(BF16), kt.astype(BF16), kd.astype(BF16), p_ref[1, :, 0:D].astype(BF16)
        decs = [jnp.exp(g[c * CH + CH - 1:c * CH + CH, :]) for c in range(nc)]
        o_in = []
        for hs in heads:
            att = jnp.where(low, _dot_nt(qtb[:, hs], ktb[:, hs]), 0.0)
            o_in.append(_dot(att.astype(BF16), vb[:, hs]))
        s_t = [st[h] for h in range(NH)]
        pieces = [[None] * nc for _ in range(NH)]
        for c in range(nc):
            sl = slice(c * CH, (c + 1) * CH)
            for h, hs in enumerate(heads):
                s_bf = s_t[h].astype(BF16)
                ss_ref[c, h] = s_bf
                pieces[h][c] = o_in[h][sl] + _dot_nt(qtb[sl, hs], s_bf)
                s_t[h] = s_t[h] * decs[c][:, hs] + _dot_tn(vb[sl, hs], kdb[sl, hs])
        for h, hs in enumerate(heads):
            st[h] = s_t[h]
            o = jnp.concatenate(pieces[h], axis=0)
            _, n = _rms_stats(o)
            og = p_ref[1, :, D + h * HD:D + (h + 1) * HD]
            o_ref[:, hs] = o
            yb_ref[:, hs] = (n * gn_ref[:, hs] * (og * _sig(og))).astype(BF16)
        finish()

    any_spec = pl.BlockSpec(memory_space=pl.ANY)
    outs = pl.pallas_call(
        body, name="hgrn2_fwd",
        out_shape=(jax.ShapeDtypeStruct((t, D), BF16), jax.ShapeDtypeStruct((t, D), F32),
                   jax.ShapeDtypeStruct((t // CH, NH, HD, HD), BF16), jax.ShapeDtypeStruct((2, t, D), F32))
        + tuple(jax.ShapeDtypeStruct(v.shape, v.dtype) for v in stk),
        grid=(t // tt,),
        in_specs=[pl.BlockSpec((2, tt, 2 * D), lambda i: (0, i, 0)),
                  pl.BlockSpec((2, D), lambda i: (0, 0)), pl.BlockSpec((1, D), lambda i: (0, 0))] + [any_spec] * ng,
        out_specs=(pl.BlockSpec((tt, D), lambda i: (i, 0)), pl.BlockSpec((tt, D), lambda i: (i, 0)),
                   pl.BlockSpec((nc, NH, HD, HD), lambda i: (i, 0, 0, 0)),
                   pl.BlockSpec((2, tt, D), lambda i: (0, i, 0))) + (any_spec,) * ng,
        scratch_shapes=[pltpu.VMEM((NH, HD, HD), F32), pltpu.SemaphoreType.DMA((3 * ng,)),
                        pltpu.SemaphoreType.DMA((3 * ng,))],
        input_output_aliases={3 + w: 4 + w for w in range(ng)},
        compiler_params=pltpu.CompilerParams(dimension_semantics=("arbitrary",), vmem_limit_bytes=VMEM_BIG,
                                             has_side_effects=True),
    )(p4, logits, gnorm, *stk)
    return outs[0], outs[1], outs[2], outs[3], list(outs[4:])


def _mid_fwd(ya, yb, p4, x, wa, wb, wo, g_pm, g_pf, tt):
    t = x.shape[0]

    def body(ya_ref, yb_ref, gt_ref, x_ref, wa_ref, wb_ref, wo_ref, gpm_ref, gpf_ref,
             za_ref, zb_ref, mix_ref, m2_ref, x1_ref, h2_ref):
        za = _dot(ya_ref[...], wa_ref[...])
        zb = _dot(yb_ref[...], wb_ref[...])
        mix = _sig(gt_ref[:, 0:D]) * za + _sig(gt_ref[:, D:2 * D]) * zb
        mixb = mix.astype(BF16)
        m2 = _dot(mixb, wo_ref[...])
        _, n2 = _rms_stats(m2)
        x1 = x_ref[...] + n2 * gpm_ref[...]
        _, n1 = _rms_stats(x1)
        za_ref[...] = za.astype(BF16)
        zb_ref[...] = zb.astype(BF16)
        mix_ref[...] = mixb
        m2_ref[...] = m2
        x1_ref[...] = x1
        h2_ref[...] = (n1 * gpf_ref[...]).astype(BF16)

    row = lambda dt: jax.ShapeDtypeStruct((t, D), dt)
    tile = pl.BlockSpec((tt, D), lambda i: (i, 0))
    wsp = pl.BlockSpec((D, D), lambda i: (0, 0))
    vec = pl.BlockSpec((1, D), lambda i: (0, 0))
    return pl.pallas_call(
        body, name="mid_fwd",
        out_shape=(row(BF16), row(BF16), row(BF16), row(F32), row(F32), row(BF16)),
        grid=(t // tt,),
        in_specs=[tile, tile, pl.BlockSpec((None, tt, 2 * D), lambda i: (SLOT_G, i, 0)), tile,
                  wsp, wsp, wsp, vec, vec],
        out_specs=(tile,) * 6,
        compiler_params=_params(("parallel",), VMEM_BIG),
    )(ya, yb, p4, x, wa, wb, wo, g_pm, g_pf)


def _up_act_fwd(h2, w_up4, cfw, cfb, tm):
    t = h2.shape[0]
    ns = SH_UP

    def body(a_ref, ah_ref, wg_ref, wv_ref, cwg_ref, cwv_ref, cbg_ref, cbv_ref,
             pg_ref, pv_ref, y_ref, uv_ref, gl_ref, dgl_ref):
        i = pl.program_id(1)
        rows = jnp.concatenate([ah_ref[...], a_ref[...]], axis=0)
        ups = []
        for w_ref, cw_ref, cb_ref, pre_ref in ((wg_ref, cwg_ref, cbg_ref, pg_ref), (wv_ref, cwv_ref, cbv_ref, pv_ref)):
            pre = _dot(rows, w_ref[...])
            pre_ref[...] = pre[16:, :].astype(BF16)
            xe = jnp.concatenate([jnp.where(i > 0, pre[8:16, :], 0.0), pre[16:, :]], axis=0)
            up = (cb_ref[...] + cw_ref[2:3, :] * xe + cw_ref[1:2, :] * pltpu.roll(xe, 1, 0)
                  + cw_ref[0:1, :] * pltpu.roll(xe, 2, 0))
            ups.append(up[8:, :])
        gl, dgl = _gelu_and_grad(ups[0])
        y_ref[...] = (gl * ups[1]).astype(BF16)
        uv_ref[...] = ups[1].astype(BF16)
        gl_ref[...] = gl.astype(BF16)
        dgl_ref[...] = dgl.astype(BF16)

    hb = tm // 16
    tile = pl.BlockSpec((tm, ns), lambda p, i: (i, p))
    return pl.pallas_call(
        body, name="up_act_fwd",
        out_shape=(jax.ShapeDtypeStruct((t, DFF), BF16),) * 6,
        grid=(2, t // tm),
        in_specs=[pl.BlockSpec((tm, D), lambda p, i: (i, 0)),
                  pl.BlockSpec((16, D), lambda p, i: (jnp.maximum(i * hb - 1, 0), 0)),
                  pl.BlockSpec((None, D, ns), lambda p, i: (p, 0, 0)),
                  pl.BlockSpec((None, D, ns), lambda p, i: (p + 2, 0, 0)),
                  pl.BlockSpec((3, ns), lambda p, i: (0, p)), pl.BlockSpec((3, ns), lambda p, i: (0, p + 2)),
                  pl.BlockSpec((1, ns), lambda p, i: (0, p)), pl.BlockSpec((1, ns), lambda p, i: (0, p + 2))],
        out_specs=(tile,) * 6,
        compiler_params=_params(("parallel", "parallel"), VMEM_BIG),
    )(h2, h2, w_up4, w_up4, cfw, cfw, cfb, cfb)


def _down_loss(y, wdn, x1, tgt, g_post, tt):
    t = x1.shape[0]

    def body(y_ref, w_ref, x1_ref, t_ref, g_ref, dx2_ref, dm3_ref, lossv_ref, dg_ref):
        i = pl.program_id(0)
        m3 = _dot(y_ref[...], w_ref[...])
        r, n3 = _rms_stats(m3)
        g = g_ref[...]
        e = x1_ref[...] + n3 * g - t_ref[...]
        dx2 = e * (1.0 / D)
        dx2_ref[...] = dx2
        dm3_ref[...] = _rms_bwd(dx2 * g, n3, r).astype(BF16)
        lv = jnp.sum(e * e, axis=0, keepdims=True)
        dgv = jnp.sum(dx2 * n3, axis=0, keepdims=True)

        @pl.when(i == 0)
        def _():
            lossv_ref[...] = lv
            dg_ref[...] = dgv

        @pl.when(i > 0)
        def _():
            lossv_ref[...] += lv
            dg_ref[...] += dgv

    tile = pl.BlockSpec((tt, D), lambda i: (i, 0))
    vec = pl.BlockSpec((1, D), lambda i: (0, 0))
    return pl.pallas_call(
        body, name="down_loss",
        out_shape=(jax.ShapeDtypeStruct((t, D), F32), jax.ShapeDtypeStruct((t, D), BF16),
                   jax.ShapeDtypeStruct((1, D), F32), jax.ShapeDtypeStruct((1, D), F32)),
        grid=(t // tt,),
        in_specs=[pl.BlockSpec((tt, DFF), lambda i: (i, 0)), pl.BlockSpec((DFF, D), lambda i: (0, 0)),
                  tile, tile, vec],
        out_specs=(tile, tile, vec, vec),
        compiler_params=_params(("arbitrary",), VMEM_BIG),
    )(y, wdn, x1, tgt, g_post)


def _ffn_act_bwd(dy, pre_g, pre_v, uv, gl, dgl, cfw, tt):
    t = dy.shape[0]
    nt = t // tt

    def body(dy_ref, dyn_ref, pg_ref, pv_ref, uv_ref, uvn_ref, gl_ref, gln_ref, dgl_ref, dgln_ref, cw_ref,
             du_ref, dcw_ref, dcb_ref):
        i = pl.program_id(0)
        n = tt + 8
        next_live = jnp.where(i < nt - 1, 1.0, 0.0)
        ext = lambda ref, nref: jnp.concatenate([ref[...].astype(F32), nref[...].astype(F32)[0:8, :]], axis=0)
        dy = jnp.concatenate([dy_ref[...].astype(F32), dyn_ref[...].astype(F32)[0:8, :] * next_live], axis=0)
        ds = (dy * ext(uv_ref, uvn_ref) * ext(dgl_ref, dgln_ref), dy * ext(gl_ref, gln_ref))
        dcw_parts, dcb_parts = [], []
        for hh, c0 in enumerate((0, DFF)):
            cs = slice(c0, c0 + DFF)
            dd = ds[hh]
            d1 = pltpu.roll(dd, n - 1, 0)
            d2 = pltpu.roll(dd, n - 2, 0)
            du_ref[:, cs] = (cw_ref[2:3, cs] * dd + cw_ref[1:2, cs] * d1 + cw_ref[0:1, cs] * d2)[0:tt, :].astype(BF16)
            x = (pg_ref, pv_ref)[hh][...].astype(F32)
            dcw_parts.append(jnp.concatenate(
                [jnp.sum(dk[0:tt, :] * x, axis=0, keepdims=True) for dk in (d2, d1, dd)], axis=0))
            dcb_parts.append(jnp.sum(dd[0:tt, :], axis=0, keepdims=True))
        dcw = jnp.concatenate(dcw_parts, axis=1)
        dcb = jnp.concatenate(dcb_parts, axis=1)

        @pl.when(i == 0)
        def _():
            dcw_ref[...] = dcw
            dcb_ref[...] = dcb

        @pl.when(i > 0)
        def _():
            dcw_ref[...] += dcw
            dcb_ref[...] += dcb

    half = pl.BlockSpec((tt, DFF), lambda i: (i, 0))
    half_next = pl.BlockSpec((16, DFF), lambda i: (jnp.minimum((i + 1) * (tt // 16), t // 16 - 1), 0))
    return pl.pallas_call(
        body, name="ffn_act_bwd",
        out_shape=(jax.ShapeDtypeStruct((t, DUP), BF16), jax.ShapeDtypeStruct((3, DUP), F32),
                   jax.ShapeDtypeStruct((1, DUP), F32)),
        grid=(nt,),
        in_specs=[half, half_next, half, half,
                  half, half_next, half, half_next, half, half_next,
                  pl.BlockSpec((3, DUP), lambda i: (0, 0))],
        out_specs=(pl.BlockSpec((tt, DUP), lambda i: (i, 0)), pl.BlockSpec((3, DUP), lambda i: (0, 0)),
                   pl.BlockSpec((1, DUP), lambda i: (0, 0))),
        compiler_params=_params(("arbitrary",), VMEM_BIG),
    )(dy, dy, pre_g, pre_v, uv, uv, gl, gl, dgl, dgl, cfw)


def _mid_bwd(dx1, m2, za, zb, p4, wa, wb, wo, g_pm, tt):
    t = dx1.shape[0]

    def body(dx1_ref, m2_ref, za_ref, zb_ref, gt_ref, wa_ref, wb_ref, wo_ref, gpm_ref,
             dm2_ref, dza_ref, dzb_ref, dya_ref, dyb_ref, dp_ref, dgpm_ref):
        i = pl.program_id(0)
        dx1 = dx1_ref[...]
        r2, n2 = _rms_stats(m2_ref[...])
        dm2 = _rms_bwd(dx1 * gpm_ref[...], n2, r2).astype(BF16)
        dmix = _dot_nt(dm2, wo_ref[...])
        sa = _sig(gt_ref[:, 0:D])
        sb = _sig(gt_ref[:, D:2 * D])
        dza = (dmix * sa).astype(BF16)
        dzb = (dmix * sb).astype(BF16)
        dp_ref[:, 0:D] = (dmix * za_ref[...].astype(F32) * sa * (1.0 - sa)).astype(BF16)
        dp_ref[:, D:2 * D] = (dmix * zb_ref[...].astype(F32) * sb * (1.0 - sb)).astype(BF16)
        dm2_ref[...] = dm2
        dza_ref[...] = dza
        dzb_ref[...] = dzb
        dya_ref[...] = _dot_nt(dza, wa_ref[...]).astype(BF16)
        dyb_ref[...] = _dot_nt(dzb, wb_ref[...]).astype(BF16)
        dgpm = jnp.sum(dx1 * n2, axis=0, keepdims=True)

        @pl.when(i == 0)
        def _():
            dgpm_ref[...] = dgpm

        @pl.when(i > 0)
        def _():
            dgpm_ref[...] += dgpm

    row = lambda dt: jax.ShapeDtypeStruct((t, D), dt)
    tile = pl.BlockSpec((tt, D), lambda i: (i, 0))
    wsp = pl.BlockSpec((D, D), lambda i: (0, 0))
    vec = pl.BlockSpec((1, D), lambda i: (0, 0))
    gates = pl.BlockSpec((None, tt, 2 * D), lambda i: (SLOT_G, i, 0))
    return pl.pallas_call(
        body, name="mid_bwd",
        out_shape=(row(BF16), row(BF16), row(BF16), row(BF16), row(BF16),
                   jax.ShapeDtypeStruct((NCHIP, t, 2 * D), BF16), jax.ShapeDtypeStruct((1, D), F32)),
        grid=(t // tt,),
        in_specs=[tile, tile, tile, tile, gates, wsp, wsp, wsp, vec],
        out_specs=(tile, tile, tile, tile, tile, gates, vec),
        compiler_params=_params(("arbitrary",), VMEM_BIG),
    )(dx1, m2, za, zb, p4, wa, wb, wo, g_pm)


def _hgrn2_bwd(p4, o_all, ss, saved, dyb, dp4, logits, gnorm, p_early, tt):
    t = p4.shape[1]
    nt = t // tt
    nc = tt // CH
    ne = len(p_early)

    def body(p_ref, o_ref, ss_ref, sv_ref, dyb_ref, dp_in, lg_ref, gn_ref, *rest):
        del dp_in
        pe = rest[:ne]
        dp_ref, dlb_ref, dgn_ref = rest[ne:ne + 3]
        qe = rest[ne + 3:2 * ne + 3]
        dst, ssem, rsem = rest[2 * ne + 3:]
        i = pl.program_id(0)
        sends, arrive = _exchange_copies(qe, [(lambda chip, r=r: r.at[chip]) for r in pe], ssem, rsem)

        @pl.when(i == 0)
        def _():
            dst[...] = jnp.zeros((NH, HD, HD), F32)
            for cp in sends:
                cp.start()

        low = _chunk_masks(tt)
        lb = _lb_of(lg_ref)
        heads = [slice(h * HD, (h + 1) * HD) for h in range(NH)]
        sg, sgn, f, k, g, eg, eng, egu, qt, kt, kd = _hg_head_fwd(p_ref[0, :, 0:D], p_ref[0, :, D:2 * D], lb,
                                                                  (sv_ref[0], sv_ref[1]))
        qtb, ktb, kdb, vb = qt.astype(BF16), kt.astype(BF16), kd.astype(BF16), p_ref[1, :, 0:D].astype(BF16)
        decs = [jnp.exp(g[c * CH + CH - 1:c * CH + CH, :]) for c in range(nc)]
        og = p_ref[1, :, D:2 * D]
        so = _sig(og)
        dyb = dyb_ref[...].astype(F32)
        dob = dyb * (og * so)
        rn = [_rms_stats(o_ref[:, hs]) for hs in heads]
        r_all = jnp.concatenate([jnp.broadcast_to(r, (tt, HD)) for r, _ in rn], axis=1)
        n_all = jnp.concatenate([n for _, n in rn], axis=1)
        gd = dob * gn_ref[...]
        proj = jnp.concatenate(
            [jnp.broadcast_to(jnp.mean(gd[:, hs] * n_all[:, hs], axis=-1, keepdims=True), (tt, HD)) for hs in heads],
            axis=1)
        dob_ = (r_all * (gd - n_all * proj)).astype(BF16)
        dog = dyb * (n_all * gn_ref[...]) * (so * (1.0 + og * (1.0 - so)))
        dgn = jnp.sum(dob * n_all, axis=0, keepdims=True)
        dv_in, dqt_in, dkt_h = [], [], []
        for hs in heads:
            att = jnp.where(low, _dot_nt(qtb[:, hs], ktb[:, hs]), 0.0).astype(BF16)
            d_att = jnp.where(low, _dot_nt(dob_[:, hs], vb[:, hs]), 0.0).astype(BF16)
            dv_in.append(_dot_tn(att, dob_[:, hs]))
            dqt_in.append(_dot(d_att, ktb[:, hs]))
            dkt_h.append(_dot_tn(d_att, qtb[:, hs]))
        ds_t = [dst[h] for h in range(NH)]
        dv_p = [[None] * NH for _ in range(nc)]
        dqt_p = [[None] * NH for _ in range(nc)]
        dkd_p = [[None] * NH for _ in range(nc)]
        dgl_p = [[None] * NH for _ in range(nc)]
        for c in reversed(range(nc)):
            sl = slice(c * CH, (c + 1) * CH)
            for h, hs in enumerate(heads):
                s_prev = ss_ref[c, h]
                ds_bf = ds_t[h].astype(BF16)
                dec = decs[c][:, hs]
                dv_p[c][h] = dv_in[h][sl] + _dot_nt(kdb[sl, hs], ds_bf)
                dqt_p[c][h] = dqt_in[h][sl] + _dot(dob_[sl, hs], s_prev)
                dkd_p[c][h] = _dot(vb[sl, hs], ds_bf)
                ddec = jnp.sum(s_prev.astype(F32) * ds_t[h], axis=0, keepdims=True)
                dgl_p[c][h] = jnp.broadcast_to(ddec * dec, (CH, HD))
                ds_t[h] = ds_t[h] * dec + _dot_tn(dob_[sl, hs], qtb[sl, hs])
        for h in range(NH):
            dst[h] = ds_t[h]
        whole = lambda parts: jnp.concatenate([jnp.concatenate(row, axis=1) for row in parts], axis=0)
        dv, dqt, dkd, dgl = whole(dv_p), whole(dqt_p), whole(dkd_p), whole(dgl_p)
        dkt = jnp.concatenate(dkt_h, axis=1)
        dq = dqt * eg
        dk = dkt * eng + dkd * egu
        dg = dqt * qt - dkt * kt
        dgu = dkd * kd
        dlogf = _chunk_revcumsum(dg - dgu) + _chunk_total(dgu) + dgl
        common = sgn * (dlogf / f - dk)
        dfz = (1.0 - lb) * sg * common
        dlb = jnp.sum(common, axis=0, keepdims=True)
        dp_ref[0, :, 0:D] = dq.astype(BF16)
        dp_ref[0, :, D:2 * D] = dfz.astype(BF16)
        dp_ref[1, :, 0:D] = dv.astype(BF16)
        dp_ref[1, :, D:2 * D] = dog.astype(BF16)

        @pl.when(i == 0)
        def _():
            dlb_ref[0:1, :] = dlb
            dgn_ref[...] = dgn

        @pl.when(i > 0)
        def _():
            dlb_ref[0:1, :] += dlb
            dgn_ref[...] += dgn

        @pl.when(i == nt - 1)
        def _():
            d0 = dlb_ref[0:1, :] * lb * (1.0 - lb)
            dlb_ref[0:1, :] = d0
            dlb_ref[1:2, :] = -d0
            for cp in arrive:
                cp.wait_recv()
            for cp in sends:
                cp.wait_send()

    rev = lambda i: nt - 1 - i
    vec = pl.BlockSpec((1, D), lambda i: (0, 0))
    any_spec = pl.BlockSpec(memory_space=pl.ANY)
    outs = pl.pallas_call(
        body, name="hgrn2_bwd",
        out_shape=(jax.ShapeDtypeStruct(dp4.shape, BF16), jax.ShapeDtypeStruct((2, D), F32),
                   jax.ShapeDtypeStruct((1, D), F32)) + tuple(jax.ShapeDtypeStruct(a.shape, BF16) for a in p_early),
        grid=(nt,),
        in_specs=[pl.BlockSpec((2, tt, 2 * D), lambda i: (0, rev(i), 0)),
                  pl.BlockSpec((tt, D), lambda i: (rev(i), 0)),
                  pl.BlockSpec((nc, NH, HD, HD), lambda i: (rev(i), 0, 0, 0)),
                  pl.BlockSpec((2, tt, D), lambda i: (0, rev(i), 0)),
                  pl.BlockSpec((tt, D), lambda i: (rev(i), 0)),
                  any_spec,
                  pl.BlockSpec((2, D), lambda i: (0, 0)), vec] + [any_spec] * ne,
        out_specs=(pl.BlockSpec((2, tt, 2 * D), lambda i: (0, rev(i), 0)),
                   pl.BlockSpec((2, D), lambda i: (0, 0)), vec) + (any_spec,) * ne,
        scratch_shapes=[pltpu.VMEM((NH, HD, HD), F32), pltpu.SemaphoreType.DMA((3 * ne,)),
                        pltpu.SemaphoreType.DMA((3 * ne,))],
        input_output_aliases={5: 0},
        compiler_params=pltpu.CompilerParams(dimension_semantics=("arbitrary",), vmem_limit_bytes=VMEM_BIG,
                                             has_side_effects=True),
    )(p4, o_all, ss, saved, dyb, dp4, logits, gnorm, *p_early)
    return outs[0], outs[1], outs[2], list(outs[3:])


def _mixer_a_bwd(p4, hseq, saved, dya, dp4, cw, wa, wx, lam, tt):
    t = p4.shape[1]
    nt = t // tt

    def body(p_ref, sv_ref, h_ref, hh_ref, dya_ref, dp_in, cw_ref, wa_ref, wx_ref, lam_ref,
             dp_ref, dcw_ref, dcb_ref, dwa_ref, dwx_ref, dba_ref, dbx_ref, dlam_ref,
             dnext, dhc, afc):
        del dp_in
        i = pl.program_id(0)
        first_tile = i == nt - 1

        @pl.when(i == 0)
        def _():
            dnext[...] = jnp.zeros((8, D), F32)
            dhc[...] = jnp.zeros((1, D), F32)
            afc[...] = jnp.zeros((1, D), F32)

        xa = p_ref[:, 0:D]
        ga = p_ref[:, D:2 * D]
        xc, r, ig, a, mult = (sv_ref[idx] for idx in range(5))
        xcb = xc.astype(BF16)
        lam = lam_ref[...]
        sp = jnp.maximum(-lam, 0.0) + jnp.log1p(jnp.exp(-jnp.abs(lam)))
        h = h_ref[...]
        gl, dgl = _gelu_and_grad(ga)
        dya = dya_ref[...].astype(F32)
        dga = dya * h * dgl
        rows = lax.broadcasted_iota(jnp.int32, (tt, 1), 0)
        a_next = jnp.where(rows == tt - 1, afc[...], pltpu.roll(a, tt - 1, 0))
        dh = _scan_up(a_next, dya * gl, dhc[...])
        dhc[...] = dh[0:1, :]
        afc[...] = a[0:1, :]
        h_prev = jnp.where(rows == 0, jnp.where(first_tile, 0.0, hh_ref[7:8, :]), pltpu.roll(h, 1, 0))
        da = dh * h_prev
        dmult = dh * ig * xc
        di = dh * mult * xc
        dlog_a = da * a - dmult * a * a / mult
        dr = dlog_a * (-LRU_C * sp)
        dsp = jnp.sum(dlog_a * (-LRU_C * r), axis=0, keepdims=True)
        dra = dr * r * (1.0 - r)
        dix = di * ig * (1.0 - ig)
        drab = dra.astype(BF16)
        dixb = dix.astype(BF16)
        dxc_lin = []
        dwa_new = []
        dwx_new = []
        for n in range(NH):
            cs = slice(n * HD, (n + 1) * HD)
            dxc_lin.append(_dot_nt(drab[:, cs], wa_ref[n]) + _dot_nt(dixb[:, cs], wx_ref[n]))
            dwa_new.append(_dot_tn(xcb[:, cs], drab[:, cs]))
            dwx_new.append(_dot_tn(xcb[:, cs], dixb[:, cs]))
        dxc = dh * mult * ig + jnp.concatenate(dxc_lin, axis=1)
        de = jnp.concatenate([dxc, dnext[...]], axis=0)
        ups = [de[0:tt, :]] + [pltpu.roll(de, tt + 8 - s, 0)[0:tt, :] for s in (1, 2, 3)]
        dxa = sum(cw_ref[3 - s:4 - s, :] * ups[s] for s in range(4))
        dnext[...] = dxc[0:8, :]
        dp_ref[:, 0:D] = dxa.astype(BF16)
        dp_ref[:, D:2 * D] = dga.astype(BF16)
        dcw = jnp.concatenate(
            [jnp.sum(ups[3 - k] * xa, axis=0, keepdims=True) for k in range(4)], axis=0)
        dcb = jnp.sum(dxc, axis=0, keepdims=True)
        dba = jnp.sum(dra, axis=0, keepdims=True)
        dbx = jnp.sum(dix, axis=0, keepdims=True)
        dlam = dsp * (-_sig(-lam))

        @pl.when(i == 0)
        def _():
            dcw_ref[...] = dcw
            dcb_ref[...] = dcb
            dba_ref[...] = dba
            dbx_ref[...] = dbx
            dlam_ref[...] = dlam
            for n in range(NH):
                dwa_ref[n] = dwa_new[n]
                dwx_ref[n] = dwx_new[n]

        @pl.when(i > 0)
        def _():
            dcw_ref[...] += dcw
            dcb_ref[...] += dcb
            dba_ref[...] += dba
            dbx_ref[...] += dbx
            dlam_ref[...] += dlam
            for n in range(NH):
                dwa_ref[n] += dwa_new[n]
                dwx_ref[n] += dwx_new[n]

    rev = lambda i: nt - 1 - i
    hb = tt // 8
    full = lambda shape: pl.BlockSpec(shape, lambda i: (0,) * len(shape))
    vecs = jax.ShapeDtypeStruct((1, D), F32)
    blk = jax.ShapeDtypeStruct((NH, HD, HD), F32)
    return pl.pallas_call(
        body, name="mixer_a_bwd",
        out_shape=(jax.ShapeDtypeStruct(dp4.shape, BF16), jax.ShapeDtypeStruct((4, D), F32), vecs, blk, blk,
                   vecs, vecs, vecs),
        grid=(nt,),
        in_specs=[pl.BlockSpec((None, tt, 2 * D), lambda i: (SLOT_A, rev(i), 0)),
                  pl.BlockSpec((5, tt, D), lambda i: (0, rev(i), 0)),
                  pl.BlockSpec((tt, D), lambda i: (rev(i), 0)),
                  pl.BlockSpec((8, D), lambda i: (jnp.maximum(rev(i) * hb - 1, 0), 0)),
                  pl.BlockSpec((tt, D), lambda i: (rev(i), 0)),
                  pl.BlockSpec(memory_space=pl.ANY),
                  full((4, D)), full((NH, HD, HD)), full((NH, HD, HD)), full((1, D))],
        out_specs=(pl.BlockSpec((None, tt, 2 * D), lambda i: (SLOT_A, rev(i), 0)),
                   full((4, D)), full((1, D)), full((NH, HD, HD)), full((NH, HD, HD)),
                   full((1, D)), full((1, D)), full((1, D))),
        scratch_shapes=[pltpu.VMEM((8, D), F32), pltpu.VMEM((1, D), F32), pltpu.VMEM((1, D), F32)],
        input_output_aliases={5: 0},
        compiler_params=_params(("arbitrary",), VMEM_BIG),
    )(p4, saved, hseq, hseq, dya, dp4, cw, wa, wx, lam)


def _local_step(x, tgt, stk_w_in, stk_rest, conv_a_s, conv_f_s, small, jc, cidx):
    t = x.shape[0]
    tt = min(256, t)
    tm = min(1024, t)
    tk = min(2048, t)
    wa_bf = small["lru_wa"].astype(BF16)
    wx_bf = small["lru_wx"].astype(BF16)

    p4, h1, w_in, conv_a_g, conv_f_g = _mm_in_gather(x, small["norm_pre_mix"], stk_w_in, conv_a_s, conv_f_s, jc, tm)
    conv_a_w = jnp.transpose(conv_a_g, (1, 0, 2)).reshape(8, D)[0:4]
    conv_f_w = jnp.transpose(conv_f_g, (1, 0, 2)).reshape(8, DUP)[0:3]
    stk = dict(zip(REST, stk_rest))
    ya, hseq, saved_a, got_a = _mixer_a_fwd(p4, conv_a_w, small["conv_a_b"], wa_bf, wx_bf, small["lru_ba"],
                                            small["lru_bx"], small["lru_lambda"],
                                            [stk[n] for n in REST_A], REST_A, tt)
    yb, o_all, ss, saved_b, got_b = _hgrn2_fwd(p4, small["hg_lb_logits"], small["hg_norm_g"],
                                               [stk[n] for n in REST_B], REST_B, tt)
    w = dict(zip(REST_A + REST_B, _gather_forward(got_a + got_b, REST_A + REST_B)))
    w["w_in"] = w_in
    w_br_a = w["w_branch_a"].reshape(D, D)
    w_br_b = w["w_branch_b"].reshape(D, D)
    w_out = w["w_out"].reshape(D, D)
    w_down = w["w_down"].reshape(DFF, D)
    za, zb, mixb, m2, x1, h2 = _mid_fwd(ya, yb, p4, x, w_br_a, w_br_b, w_out, small["norm_post_mix"],
                                        small["norm_pre_ffn"], min(512, t))
    pre_g, pre_v, y, uv, gl, dgl = _up_act_fwd(h2, w["w_up"], conv_f_w, small["conv_f_b"], tt)
    dx2, dm3, lossv, d_norm_post_ffn = _down_loss(y, w_down, x1, tgt, small["norm_post_ffn"], min(512, t))

    d_w_down = _mm_tn(y, dm3, DFF // 2, D, tk, "mm_dw_down")
    dy = _mm_nt(dm3, w_down, BF16, tm, "mm_dy")
    dup_pre, d_conv_f_w, d_conv_f_b = _ffn_act_bwd(dy, pre_g, pre_v, uv, gl, dgl, conv_f_w, tt)
    d_w_up = _mm_tn(h2, dup_pre, D, SH_UP, tk, "mm_dw_up", stacked_out=True)
    dx1, d_norm_pre_ffn = _mm_dh2_norm(dup_pre, w["w_up"], dx2, x1, small["norm_pre_ffn"], tm)
    dm2, dza, dzb, dya, dyb, dp4, d_norm_post_mix = _mid_bwd(
        dx1, m2, za, zb, p4, w_br_a, w_br_b, w_out, small["norm_post_mix"], tt)
    d_w_out = _mm_tn(mixb, dm2, D, D, tm, "mm_dw_out")
    d_w_br_a = _mm_tn(ya, dza, D, D, tm, "mm_dw_bra")
    d_w_br_b = _mm_tn(yb, dzb, D, D, tm, "mm_dw_brb")
    early = {"w_branch_a": d_w_br_a.reshape(NCHIP, SH_BR, D), "w_branch_b": d_w_br_b.reshape(NCHIP, SH_BR, D),
             "w_out": d_w_out.reshape(NCHIP, SH_BR, D), "w_up": d_w_up, "w_down": d_w_down.reshape(NCHIP, SH_DN, D)}
    rb, _ = _reduce_stage1(early, REST, (), "reduce_d2d_in_early")
    p_rest = [_sum_own_half(early[n], rb[n], cidx, "sum_half_" + n) for n in REST]
    dp4, d_lb, d_hg_norm_g, q_rest = _hgrn2_bwd(p4, o_all, ss, saved_b, dyb, dp4, small["hg_lb_logits"], small["hg_norm_g"],
                                                p_rest, tt)
    dp4, d_conv_a_w, d_conv_a_b, d_lru_wa, d_lru_wx, d_lru_ba, d_lru_bx, d_lru_lambda = _mixer_a_bwd(
        p4, hseq, saved_a, dya, dp4, conv_a_w, wa_bf, wx_bf, small["lru_lambda"], tt)
    d_w_in = _mm_tn(h1, dp4, D, SH_IN, tk, "mm_dw_in", stacked_slot_fn=_slot_of_chip, stacked_out=True)
    s3 = jnp.concatenate([d_lru_wa.reshape(D, HD), d_lru_wx.reshape(D, HD)], axis=0)
    rb, (rs3,) = _reduce_stage1({"w_in": d_w_in}, ("w_in",), (s3,), "reduce_d2d_in_w_in")
    p_w_in = _sum_own_half(d_w_in, rb["w_in"], cidx, "sum_half_w_in")
    ps3 = _add(s3, rs3, "add_s3")
    grad_x, q_w_in, qs3, d_norm_pre_mix = _mm_dh1_exchange(dp4, w_in, p_w_in, ps3, x, dx1, small["norm_pre_mix"], tm)

    smalls = {
        "norm_pre_mix": d_norm_pre_mix, "conv_a_b": d_conv_a_b, "lru_ba": d_lru_ba, "lru_bx": d_lru_bx,
        "lru_lambda": d_lru_lambda, "hg_lb_logits": d_lb, "hg_norm_g": d_hg_norm_g, "norm_post_mix": d_norm_post_mix,
        "norm_pre_ffn": d_norm_pre_ffn, "norm_post_ffn": d_norm_post_ffn, "lossv": lossv,
        "conv_a_w": d_conv_a_w, "s3": (ps3, qs3),
        "conv_f_b": d_conv_f_b, "conv_f_w": d_conv_f_w,
    }
    p_big = dict(zip(REST, p_rest), w_in=p_w_in)
    q_big = dict(zip(REST, q_rest), w_in=q_w_in)
    return grad_x, p_big, q_big, smalls


BIG = ("w_in", "w_branch_a", "w_branch_b", "w_out", "w_up", "w_down")
BIG_SHAPE = {"w_in": (D, SH_IN), "w_branch_a": (SH_BR, D), "w_branch_b": (SH_BR, D), "w_out": (SH_BR, D),
             "w_up": (D, SH_UP), "w_down": (SH_DN, D)}
NBIG = len(BIG)
REST = BIG[1:]
REST_A = ("w_branch_a", "w_branch_b", "w_out", "w_down")
REST_B = ("w_up",)
VEC_ROWS = (("norm_pre_mix", 0, 1), ("conv_a_b", 1, 1), ("lru_ba", 2, 1), ("lru_bx", 3, 1), ("lru_lambda", 4, 1),
            ("hg_lb_logits", 5, 2), ("hg_norm_g", 7, 1), ("norm_post_mix", 8, 1), ("norm_pre_ffn", 9, 1),
            ("norm_post_ffn", 10, 1))
ROW_LOSS = 11
ROW_CONV_A = 12
S1_ROWS = 16
S2_ROWS = 8


def _place():
    x, y, c = lax.axis_index("x"), lax.axis_index("y"), lax.axis_index("c")
    chips = [(1 - x, y), (x, 1 - y), (1 - x, 1 - y)]
    return x, y, c, 2 * x + y, chips


def _remote(src, dst, ssem, rsem, dev):
    return pltpu.make_async_remote_copy(src_ref=src, dst_ref=dst, send_sem=ssem, recv_sem=rsem,
                                        device_id=dev, device_id_type=MESH)


def _hbm_call(body, name, ins, out_shapes, n_sems, aliases=None):
    any_spec = pl.BlockSpec(memory_space=pl.ANY)
    return pl.pallas_call(
        body, name=name, out_shape=tuple(out_shapes),
        in_specs=[any_spec] * len(ins), out_specs=tuple([any_spec] * len(out_shapes)),
        scratch_shapes=[pltpu.SemaphoreType.DMA((n,)) for n in n_sems],
        input_output_aliases=aliases or {},
        compiler_params=pltpu.CompilerParams(has_side_effects=True),
    )(*ins)


def _gather_copies(stk, names, ssem, rsem, fssem=None, frsem=None):
    x, y, c, j, chips = _place()
    sends, arrive, fwds, farrive = [], [], [], []
    for w, n in enumerate(names):
        hw = BIG_SHAPE[n][0] // 2
        mine = stk[w].at[j, pl.ds(c * hw, hw), :]
        for k, (cx, cy) in enumerate(chips):
            i = 3 * w + k
            got = stk[w].at[2 * cx + cy, pl.ds(c * hw, hw), :]
            other = stk[w].at[2 * cx + cy, pl.ds((1 - c) * hw, hw), :]
            sends.append(_remote(mine, mine, ssem.at[i], rsem.at[i], (cx, cy, c)))
            arrive.append(_remote(got, got, ssem.at[i], rsem.at[i], (cx, cy, c)))
            if fssem is not None:
                fwds.append(_remote(got, got, fssem.at[i], frsem.at[i], (x, y, 1 - c)))
                farrive.append(_remote(other, other, fssem.at[i], frsem.at[i], (x, y, 1 - c)))
    return sends, arrive, fwds, farrive


def _ici_leg_behind(stk, names, ssem, rsem, first, last_fn):
    sends, arrive, _, _ = _gather_copies(stk, names, ssem, rsem)

    @pl.when(first)
    def _():
        for cp in sends:
            cp.start()

    def finish():
        @pl.when(last_fn())
        def _():
            for cp in arrive:
                cp.wait_recv()
            for cp in sends:
                cp.wait_send()

    return finish


def _gather_forward(stk, names):
    nw = len(names)

    def body(*refs):
        dst = refs[nw:2 * nw]
        ssem, rsem, fssem, frsem = refs[2 * nw:]
        _, _, fwds, farrive = _gather_copies(dst, names, ssem, rsem, fssem, frsem)
        for cp in fwds:
            cp.start()
        for cp in farrive:
            cp.wait_recv()
        for cp in fwds:
            cp.wait_send()

    out_shapes = [jax.ShapeDtypeStruct(a.shape, a.dtype) for a in stk]
    return _hbm_call(body, "gather_forward", stk, out_shapes, (3 * nw,) * 4, aliases={w: w for w in range(nw)})


def _exchange_copies(dst, pieces, ssem, rsem):
    x, y, c, j, chips = _place()
    sends, arrive = [], []
    for w in range(len(dst)):
        for k, (cx, cy) in enumerate(chips):
            i = 3 * w + k
            sends.append(_remote(pieces[w](2 * cx + cy), dst[w].at[j], ssem.at[i], rsem.at[i], (cx, cy, c)))
            got = dst[w].at[2 * cx + cy]
            arrive.append(_remote(got, got, ssem.at[i], rsem.at[i], (cx, cy, c)))
    return sends, arrive


def _reduce_stage1(big_g, names, smalls, name):
    nb = len(names)
    ins = [big_g[n] for n in names] + list(smalls)
    n_in = len(ins)
    halves = [BIG_SHAPE[n][0] // 2 for n in names]
    out_shapes = [jax.ShapeDtypeStruct((NCHIP, halves[w], BIG_SHAPE[n][1]), big_g[n].dtype)
                  for w, n in enumerate(names)]
    out_shapes += [jax.ShapeDtypeStruct(a.shape, F32) for a in smalls]

    def body(*refs):
        src, dst = refs[:n_in], refs[n_in:2 * n_in]
        ssem, rsem = refs[2 * n_in:]
        x, y, c, _, _ = _place()
        cps = []
        for w in range(n_in):
            s_ = src[w].at[:, pl.ds((1 - c) * halves[w], halves[w]), :] if w < nb else src[w]
            cp = _remote(s_, dst[w], ssem.at[w], rsem.at[w], (x, y, 1 - c))
            cp.start()
            cps.append(cp)
        for cp in cps:
            cp.wait()

    outs = _hbm_call(body, name, ins, out_shapes, (n_in, n_in))
    return dict(zip(names, outs[:nb])), outs[nb:]


def _reduce_stage2(ps1, ps2):
    ins = [ps1, ps2]
    h1, h2 = S1_ROWS // 2, DUP // 2
    out_shapes = [jax.ShapeDtypeStruct((NCHIP, h1, D), F32), jax.ShapeDtypeStruct((NCHIP, S2_ROWS, h2), F32)]

    def body(*refs):
        src, dst = refs[:2], refs[2:4]
        ssem, rsem = refs[4:]
        c = lax.axis_index("c")
        pieces = [lambda chip: src[0].at[pl.ds(c * h1, h1), :],
                  lambda chip: src[1].at[:, pl.ds(c * h2, h2)]]
        sends, arrive = _exchange_copies(dst, pieces, ssem, rsem)
        for cp in sends:
            cp.start()
        for cp in arrive:
            cp.wait_recv()
        for cp in sends:
            cp.wait_send()

    return _hbm_call(body, "reduce_ici_small", ins, out_shapes, (6, 6))


def _reduce_stage3(f_big, fs1, fs2, fs3):
    ins = [f_big[n] for n in BIG] + [fs1, fs2, fs3]
    n_in = len(ins)
    halves = [BIG_SHAPE[n][0] // 2 for n in BIG]
    h1, h2, h3 = S1_ROWS // 2, DUP // 2, D
    out_shapes = [jax.ShapeDtypeStruct(BIG_SHAPE[n], F32) for n in BIG]
    out_shapes += [jax.ShapeDtypeStruct((S1_ROWS, D), F32), jax.ShapeDtypeStruct((S2_ROWS, DUP), F32),
                   jax.ShapeDtypeStruct((2 * D, HD), F32)]

    def body(*refs):
        dst = refs[n_in:2 * n_in]
        ssem, rsem = refs[2 * n_in:]
        x, y, c, _, _ = _place()

        def place(w, which):
            if w < NBIG:
                return dst[w].at[pl.ds(which * halves[w], halves[w]), :]
            if w == NBIG:
                return dst[w].at[pl.ds(which * h1, h1), :]
            if w == NBIG + 1:
                return dst[w].at[:, pl.ds(which * h2, h2)]
            return dst[w].at[pl.ds(which * h3, h3), :]

        cps = [_remote(place(w, c), place(w, c), ssem.at[w], rsem.at[w], (x, y, 1 - c)) for w in range(n_in)]
        for cp in cps:
            cp.start()
        for w in range(n_in):
            got = place(w, 1 - c)
            _remote(got, got, ssem.at[w], rsem.at[w], (x, y, 1 - c)).wait_recv()
        for cp in cps:
            cp.wait_send()

    outs = _hbm_call(body, "reduce_d2d_out", ins, out_shapes, (n_in, n_in), aliases={w: w for w in range(n_in)})
    return dict(zip(BIG, outs[:NBIG])), outs[NBIG], outs[NBIG + 1], outs[NBIG + 2]


def _row_tile(rows):
    for tr in (128, 176, 64, 16, 8):
        if rows % tr == 0:
            return tr
    return rows


def _sum_own_half(g, rb, cidx, name):
    s, rows, cols = g.shape
    half = rows // 2
    tr = _row_tile(half)
    nb = half // tr

    def body(c_ref, g_ref, r_ref, o_ref):
        del c_ref
        o_ref[...] = (g_ref[...].astype(F32) + r_ref[...].astype(F32)).astype(BF16)

    grid_spec = pltpu.PrefetchScalarGridSpec(
        num_scalar_prefetch=1, grid=(s, nb),
        in_specs=[pl.BlockSpec((None, tr, cols), lambda k, i, c: (k, c[0] * nb + i, 0)),
                  pl.BlockSpec((None, tr, cols), lambda k, i, c: (k, i, 0))],
        out_specs=pl.BlockSpec((None, tr, cols), lambda k, i, c: (k, i, 0)))
    return pl.pallas_call(
        body, name=name, grid_spec=grid_spec, out_shape=jax.ShapeDtypeStruct((s, half, cols), BF16),
        compiler_params=_params(("parallel", "parallel")),
    )(cidx, g, rb)


def _sum_chips(q, p, jc, name, by_cols=False):
    s, rows, cols = q.shape
    tr = _row_tile(rows)
    nb = rows // tr
    stacked = p.ndim == 3

    def body(jc_ref, q_ref, p_ref, o_ref):
        j = jc_ref[0]
        own = p_ref[...].astype(F32)
        acc = None
        for k in range(NCHIP):
            term = jnp.where(j == k, own, q_ref[k].astype(F32))
            acc = term if acc is None else acc + term
        o_ref[...] = acc

    if by_cols:
        half_spec = pl.BlockSpec((tr, cols), lambda i, jc_ref: (i, jc_ref[1]))
        out_shape = jax.ShapeDtypeStruct((rows, 2 * cols), F32)
    else:
        half_spec = pl.BlockSpec((tr, cols), lambda i, jc_ref: (jc_ref[1] * nb + i, 0))
        out_shape = jax.ShapeDtypeStruct((2 * rows, cols), F32)
    p_spec = pl.BlockSpec((None, tr, cols), lambda i, jc_ref: (jc_ref[0], i, 0)) if stacked else half_spec
    grid_spec = pltpu.PrefetchScalarGridSpec(
        num_scalar_prefetch=1, grid=(nb,),
        in_specs=[pl.BlockSpec((s, tr, cols), lambda i, jc_ref: (0, i, 0)), p_spec],
        out_specs=half_spec)
    return pl.pallas_call(
        body, name=name, grid_spec=grid_spec, out_shape=out_shape,
        compiler_params=_params(("parallel",)),
    )(jc, q, p)


def _place_shard(w, jc, name):
    rows, cols = w.shape
    tr = _row_tile(rows)

    def body(jc_ref, w_ref, o_ref):
        del jc_ref
        o_ref[...] = w_ref[...].astype(BF16)

    grid_spec = pltpu.PrefetchScalarGridSpec(
        num_scalar_prefetch=1, grid=(rows // tr,),
        in_specs=[pl.BlockSpec((tr, cols), lambda i, jc_ref: (i, 0))],
        out_specs=pl.BlockSpec((None, tr, cols), lambda i, jc_ref: (jc_ref[0], i, 0)))
    return pl.pallas_call(
        body, name=name, grid_spec=grid_spec, out_shape=jax.ShapeDtypeStruct((NCHIP, rows, cols), BF16),
        compiler_params=_params(("parallel",)),
    )(jc, w)


def _add(a, b, name):
    def body(a_ref, b_ref, o_ref):
        o_ref[...] = a_ref[...] + b_ref[...]

    return pl.pallas_call(body, name=name, out_shape=jax.ShapeDtypeStruct(a.shape, F32))(a, b)


def _pack_small(sm):
    vec_in = [sm[n] for n, _, _ in VEC_ROWS]
    nv = len(vec_in)

    def body(*refs):
        ins, lossv, dcw, dcfb, dcfw, s1, s2 = refs[:nv], refs[nv], refs[nv + 1], refs[nv + 2], refs[nv + 3], \
            refs[nv + 4], refs[nv + 5]
        for ref, (_, r0, nr) in zip(ins, VEC_ROWS):
            s1[r0:r0 + nr, :] = ref[...]
        s1[ROW_LOSS:ROW_LOSS + 1, :] = lossv[...]
        s1[ROW_CONV_A:ROW_CONV_A + 4, :] = dcw[...]
        s2[0:1, :] = dcfb[...]
        s2[1:4, :] = dcfw[...]
        s2[4:8, :] = jnp.zeros((4, DUP), F32)

    return pl.pallas_call(
        body, name="pack_small",
        out_shape=(jax.ShapeDtypeStruct((S1_ROWS, D), F32), jax.ShapeDtypeStruct((S2_ROWS, DUP), F32)),
    )(*vec_in, sm["lossv"], sm["conv_a_w"], sm["conv_f_b"], sm["conv_f_w"])


def _adam_math(w, g, m, v):
    m = ADAM_B1 * m + (1.0 - ADAM_B1) * g
    v = ADAM_B2 * v + (1.0 - ADAM_B2) * (g * g)
    m_hat = m / (1.0 - ADAM_B1 ** ADAM_STEP)
    v_hat = v / (1.0 - ADAM_B2 ** ADAM_STEP)
    delta = -ADAM_LR * (m_hat / (jnp.sqrt(v_hat) + ADAM_EPS) + ADAM_WD * w)
    return delta, m, v


def _adam(w, g, m, v, name):
    rows, cols = w.shape
    tr = _row_tile(rows)

    def body(w_ref, g_ref, m_ref, v_ref, d_ref, mo_ref, vo_ref):
        d_ref[...], mo_ref[...], vo_ref[...] = _adam_math(w_ref[...], g_ref[...], m_ref[...], v_ref[...])

    spec = pl.BlockSpec((tr, cols), lambda i: (i, 0))
    return pl.pallas_call(
        body, name=name, out_shape=(jax.ShapeDtypeStruct(w.shape, F32),) * 3, grid=(rows // tr,),
        in_specs=[spec] * 4, out_specs=(spec,) * 3,
        compiler_params=_params(("parallel",)),
    )(w, g, m, v)


def _adam_small(gs1, gs2, gs3, w, m, v):
    names = [n for n, _, _ in VEC_ROWS] + ["conv_f_b", "lru_wa", "lru_wx"]
    nn = len(names)

    def grad_of(i, g1, g2, g3):
        if i < len(VEC_ROWS):
            _, r0, nr = VEC_ROWS[i]
            return g1[r0:r0 + nr, :]
        if names[i] == "conv_f_b":
            return g2[0:1, :]
        return g3[0] if names[i] == "lru_wa" else g3[1]

    def body(*refs):
        g1, g2, g3 = refs[0], refs[1], refs[2]
        ws, ms, vs = refs[3:3 + nn], refs[3 + nn:3 + 2 * nn], refs[3 + 2 * nn:3 + 3 * nn]
        outs = refs[3 + 3 * nn:]
        for i in range(nn):
            d, mn, vn = _adam_math(ws[i][...], grad_of(i, g1, g2, g3), ms[i][...], vs[i][...])
            outs[i][...] = d
            outs[nn + i][...] = mn
            outs[2 * nn + i][...] = vn

    shapes = [jax.ShapeDtypeStruct(w[n].shape, F32) for n in names]
    outs = pl.pallas_call(body, name="adam_small", out_shape=tuple(shapes * 3))(
        gs1, gs2, gs3, *[w[n] for n in names], *[m[n] for n in names], *[v[n] for n in names])
    return {n: (outs[i], outs[nn + i], outs[2 * nn + i]) for i, n in enumerate(names)}


WEIGHTS = ("norm_pre_mix", "w_in", "conv_a_w", "conv_a_b", "lru_wa", "lru_ba", "lru_wx", "lru_bx", "lru_lambda",
           "hg_lb_logits", "hg_norm_g", "w_branch_a", "w_branch_b", "w_out", "norm_post_mix", "norm_pre_ffn",
           "w_up", "conv_f_w", "conv_f_b", "w_down", "norm_post_ffn")
NW = len(WEIGHTS)


def kernel(x, norm_pre_mix, w_in, conv_a_w, conv_a_b, lru_wa, lru_ba, lru_wx, lru_bx, lru_lambda, hg_lb_logits, hg_norm_g, w_branch_a, w_branch_b, w_out, norm_post_mix, norm_pre_ffn, w_up, conv_f_w, conv_f_b, w_down, norm_post_ffn, loss_target, m_norm_pre_mix, m_w_in, m_conv_a_w, m_conv_a_b, m_lru_wa, m_lru_ba, m_lru_wx, m_lru_bx, m_lru_lambda, m_hg_lb_logits, m_hg_norm_g, m_w_branch_a, m_w_branch_b, m_w_out, m_norm_post_mix, m_norm_pre_ffn, m_w_up, m_conv_f_w, m_conv_f_b, m_w_down, m_norm_post_ffn, v_norm_pre_mix, v_w_in, v_conv_a_w, v_conv_a_b, v_lru_wa, v_lru_ba, v_lru_wx, v_lru_bx, v_lru_lambda, v_hg_lb_logits, v_hg_norm_g, v_w_branch_a, v_w_branch_b, v_w_out, v_norm_post_mix, v_norm_pre_ffn, v_w_up, v_conv_f_w, v_conv_f_b, v_w_down, v_norm_post_ffn):
    rest = (norm_pre_mix, w_in, conv_a_w, conv_a_b, lru_wa, lru_ba, lru_wx, lru_bx, lru_lambda, hg_lb_logits, hg_norm_g, w_branch_a, w_branch_b, w_out, norm_post_mix, norm_pre_ffn, w_up, conv_f_w, conv_f_b, w_down, norm_post_ffn, loss_target, m_norm_pre_mix, m_w_in, m_conv_a_w, m_conv_a_b, m_lru_wa, m_lru_ba, m_lru_wx, m_lru_bx, m_lru_lambda, m_hg_lb_logits, m_hg_norm_g, m_w_branch_a, m_w_branch_b, m_w_out, m_norm_post_mix, m_norm_pre_ffn, m_w_up, m_conv_f_w, m_conv_f_b, m_w_down, m_norm_post_ffn, v_norm_pre_mix, v_w_in, v_conv_a_w, v_conv_a_b, v_lru_wa, v_lru_ba, v_lru_wx, v_lru_bx, v_lru_lambda, v_hg_lb_logits, v_hg_norm_g, v_w_branch_a, v_w_branch_b, v_w_out, v_norm_post_mix, v_norm_pre_ffn, v_w_up, v_conv_f_w, v_conv_f_b, v_w_down, v_norm_post_ffn)
    w_in_args = dict(zip(WEIGHTS, rest[:NW]))
    loss_target = rest[NW]
    m_args = dict(zip(WEIGHTS, rest[NW + 1:2 * NW + 1]))
    v_args = dict(zip(WEIGHTS, rest[2 * NW + 1:3 * NW + 1]))
    shape_of = {n: w_in_args[n].shape for n in WEIGHTS}

    def two_d(n, a):
        if n in BIG:
            return a.reshape(BIG_SHAPE[n])
        if n in ("lru_wa", "lru_wx"):
            return a.reshape(NH, HD, HD)
        return a.reshape(a.shape[-2:])

    w2 = {n: two_d(n, w_in_args[n]) for n in WEIGHTS}
    m2 = {n: two_d(n, m_args[n]) for n in WEIGHTS}
    v2 = {n: two_d(n, v_args[n]) for n in WEIGHTS}

    cidx = lax.axis_index("c").astype(jnp.int32).reshape(1)
    jchip = 2 * lax.axis_index("x") + lax.axis_index("y")

    jc = jnp.stack([jchip, lax.axis_index("c")]).astype(jnp.int32)

    shards = {n: _place_shard(w2[n], jc, "place_" + n) for n in BIG}
    conv_a_s = jnp.pad(w2["conv_a_w"], ((0, 4), (0, 0)))
    conv_f_s = jnp.pad(w2["conv_f_w"], ((0, 5), (0, 0)))
    small = {n: w2[n] for n in WEIGHTS if n not in BIG and n not in ("conv_a_w", "conv_f_w")}

    grad_x, p_big, q_big, sm_g = _local_step(
        x[0], loss_target[0], shards["w_in"], [shards[n] for n in REST], conv_a_s, conv_f_s, small, jc, cidx)

    s1, s2 = _pack_small(sm_g)
    ps3, qs3 = sm_g["s3"]
    _, (rs1, rs2) = _reduce_stage1({}, (), (s1, s2), "reduce_d2d_in_small")
    ps1, ps2 = _add(s1, rs1, "add_s1"), _add(s2, rs2, "add_s2")
    qs1, qs2 = _reduce_stage2(ps1, ps2)
    f_big = {n: _sum_chips(q_big[n], p_big[n], jc, "sum_chips_" + n) for n in BIG}
    fs1 = _sum_chips(qs1, ps1, jc, "sum_chips_s1")
    fs2 = _sum_chips(qs2, ps2, jc, "sum_chips_s2", by_cols=True)
    fs3 = _sum_chips(qs3, ps3, jc, "sum_chips_s3")
    g_big, gs1, gs2, gs3 = _reduce_stage3(f_big, fs1, fs2, fs3)

    res = {}
    for n in BIG:
        d, mn, vn = _adam(w2[n], g_big[n], m2[n], v2[n], "adam_" + n)
        res[n] = (g_big[n], d, mn, vn)
    small_res = _adam_small(gs1, gs2, gs3.reshape(2, NH, HD, HD), w2, m2, v2)
    for n, r0, nr in VEC_ROWS:
        res[n] = (gs1[r0:r0 + nr],) + small_res[n]
    res["conv_f_b"] = (gs2[0:1],) + small_res["conv_f_b"]
    res["lru_wa"] = (gs3[0:D].reshape(NH, HD, HD),) + small_res["lru_wa"]
    res["lru_wx"] = (gs3[D:2 * D].reshape(NH, HD, HD),) + small_res["lru_wx"]
    g_ca = lax.dynamic_slice_in_dim(gs1[ROW_CONV_A:ROW_CONV_A + 4], jchip * (D // NCHIP), D // NCHIP, axis=1)
    g_cf = lax.dynamic_slice_in_dim(gs2[1:4], jchip * SH_UP, SH_UP, axis=1)
    res["conv_a_w"] = (g_ca,) + _adam(w2["conv_a_w"], g_ca, m2["conv_a_w"], v2["conv_a_w"], "adam_conv_a_w")
    res["conv_f_w"] = (g_cf,) + _adam(w2["conv_f_w"], g_cf, m2["conv_f_w"], v2["conv_f_w"], "adam_conv_f_w")

    loss = (0.5 / D) * jnp.sum(gs1[ROW_LOSS])
    out = [loss, grad_x.reshape(x.shape)]
    for part in range(4):
        out += [res[n][part].reshape(shape_of[n]) for n in WEIGHTS]
    return tuple(out)
```

```python
import jax
import jax.numpy as jnp
from jax import lax
from jax.experimental import pallas as pl
from jax.experimental.pallas import tpu as pltpu

F32 = jnp.float32
BF16 = jnp.bfloat16

D = 1024
NH = 8
HD = 128
CH = 32
DFF = 2816
DUP = 2 * DFF
NCHIP = 4
SH_IN = 2 * D
SH_UP = DUP // NCHIP
SH_DN = DFF // NCHIP
SH_BR = D // NCHIP
EPS = 1e-6
LRU_C = 8.0
ADAM_LR = 0.001
ADAM_B1 = 0.9
ADAM_B2 = 0.999
ADAM_EPS = 1e-08
ADAM_WD = 0.01
ADAM_STEP = 10
VMEM_BIG = 56 * 1024 * 1024
MESH = pl.DeviceIdType.MESH

SLOT_A, SLOT_B, SLOT_C, SLOT_G = 2, 0, 1, 3


def _slot_of_chip(s):
    return jnp.where(s == 3, 3, (s + 2) % 3)


def _params(sem, vmem=None):
    return pltpu.CompilerParams(dimension_semantics=sem, vmem_limit_bytes=vmem)


_GC = 0.7978845608028654
_GA = 0.044715


def _gelu(x):
    return 0.5 * x * (1.0 + jnp.tanh(_GC * (x + _GA * x * x * x)))


def _gelu_and_grad(x):
    x2 = x * x
    th = jnp.tanh(_GC * x * (1.0 + _GA * x2))
    g = 0.5 * x * (1.0 + th)
    dg = 0.5 * (1.0 + th) + 0.5 * x * (1.0 - th * th) * _GC * (1.0 + 3.0 * _GA * x2)
    return g, dg


def _sig(x):
    return jax.nn.sigmoid(x)


def _dot(a, b):
    return jnp.dot(a, b, preferred_element_type=F32)


def _dot_nt(a, b):
    return lax.dot_general(a, b, (((1,), (1,)), ((), ())), preferred_element_type=F32)


def _dot_tn(a, b):
    return lax.dot_general(a, b, (((0,), (0,)), ((), ())), preferred_element_type=F32)


def _chunk_cumsum(x):
    pos = lax.broadcasted_iota(jnp.int32, (x.shape[0], 1), 0) & (CH - 1)
    d = 1
    while d < CH:
        x = x + jnp.where(pos >= d, pltpu.roll(x, d, 0), 0.0)
        d *= 2
    return x


def _chunk_revcumsum(x):
    n = x.shape[0]
    pos = lax.broadcasted_iota(jnp.int32, (n, 1), 0) & (CH - 1)
    d = 1
    while d < CH:
        x = x + jnp.where(pos < CH - d, pltpu.roll(x, n - d, 0), 0.0)
        d *= 2
    return x


def _chunk_last(x):
    n = x.shape[0]
    return jnp.concatenate(
        [jnp.broadcast_to(x[c * CH + CH - 1:c * CH + CH, :], (CH, x.shape[1])) for c in range(n // CH)], axis=0)


def _chunk_total(x):
    n = x.shape[0]
    return jnp.concatenate(
        [jnp.broadcast_to(jnp.sum(x[c * CH:(c + 1) * CH, :], axis=0, keepdims=True), (CH, x.shape[1]))
         for c in range(n // CH)], axis=0)


def _rms_stats(x):
    r = lax.rsqrt(jnp.mean(x * x, axis=-1, keepdims=True) + EPS)
    return r, x * r


def _rms_bwd(gd, n, r):
    return r * (gd - n * jnp.mean(gd * n, axis=-1, keepdims=True))


def _scan_down(a, u, carry):
    n = a.shape[0]
    pos = lax.broadcasted_iota(jnp.int32, (n, 1), 0) & 7
    for d in (1, 2, 4):
        u = a * jnp.where(pos >= d, pltpu.roll(u, d, 0), 0.0) + u
        a = a * jnp.where(pos >= d, pltpu.roll(a, d, 0), 1.0)
    out = []
    for v in range(n // 8):
        h = a[v * 8:v * 8 + 8, :] * carry + u[v * 8:v * 8 + 8, :]
        carry = h[7:8, :]
        out.append(h)
    return jnp.concatenate(out, axis=0)


def _scan_up(b, g, carry):
    n = b.shape[0]
    pos = lax.broadcasted_iota(jnp.int32, (n, 1), 0) & 7
    for d in (1, 2, 4):
        g = g + b * jnp.where(pos < 8 - d, pltpu.roll(g, n - d, 0), 0.0)
        b = b * jnp.where(pos < 8 - d, pltpu.roll(b, n - d, 0), 1.0)
    out = [None] * (n // 8)
    for v in reversed(range(n // 8)):
        h = g[v * 8:v * 8 + 8, :] + b[v * 8:v * 8 + 8, :] * carry
        carry = h[0:1, :]
        out[v] = h
    return jnp.concatenate(out, axis=0)


def _peer_of_step(s):
    return ((s & 1) << 1) | (s >> 1)


def _mm_in_gather(xin, gain, stk_w_in, conv_a_s, conv_f_s, jc, tm):
    m, k = xin.shape
    s_n, _, ns = stk_w_in.shape
    mt = m // tm

    def body(jc_ref, x_ref, g_ref, w_in_any, ca_src, cf_src, o_ref, h_ref, w_full, ca_dst, cf_dst,
             wbuf, hbuf, ssem_w, rsem_w, fssem_w, frsem_w, csend, crecv, lsem, wsem):
        del w_in_any
        s, i = pl.program_id(0), pl.program_id(1)
        x, y, c, j, chips = _place()
        sends_w, arrive_w, fwds_w, farrive_w = _gather_copies([w_full], ("w_in",), ssem_w, rsem_w, fssem_w, frsem_w)
        conv = ((ca_src, ca_dst), (cf_src, cf_dst))
        locs = [pltpu.make_async_copy(src, dst.at[j], lsem.at[n]) for n, (src, dst) in enumerate(conv)]
        csends = [_remote(src, dst.at[j], csend.at[3 * n + kk], crecv.at[3 * n + kk], (cx, cy, c))
                  for n, (src, dst) in enumerate(conv) for kk, (cx, cy) in enumerate(chips)]

        def fetch(step):
            return pltpu.make_async_copy(w_full.at[j ^ _peer_of_step(step)], wbuf.at[step & 1], wsem.at[step & 1])

        @pl.when((s == 0) & (i == 0))
        def _():
            for cp in sends_w[0:2] + locs + csends:
                cp.start()
            fetch(s).start()

        @pl.when(i == 0)
        def _():
            fetch(s).wait()

        @pl.when(s == 0)
        def _():
            _, n = _rms_stats(x_ref[...])
            h = (n * g_ref[...]).astype(BF16)
            hbuf[i] = h
            h_ref[...] = h

        o_ref[...] = _dot(hbuf[i], wbuf[s & 1])

        @pl.when((s == 0) & (i == mt - 1))
        def _():
            for kk in (0, 1):
                arrive_w[kk].wait_recv()
                fwds_w[kk].start()
            sends_w[2].start()

        @pl.when((s == 1) & (i == mt - 1))
        def _():
            arrive_w[2].wait_recv()
            fwds_w[2].start()

        for kk in range(3):
            @pl.when((s == kk) & (i == mt - 1))
            def _(kk=kk):
                farrive_w[kk].wait_recv()
                fetch(s + 1).start()

        @pl.when((s == s_n - 1) & (i == mt - 1))
        def _():
            for n, (_, dst) in enumerate(conv):
                for kk, (cx, cy) in enumerate(chips):
                    got = dst.at[2 * cx + cy]
                    _remote(got, got, csend.at[3 * n + kk], crecv.at[3 * n + kk], (cx, cy, c)).wait_recv()
            for cp in sends_w + fwds_w + csends:
                cp.wait_send()
            for cp in locs:
                cp.wait()

    any_spec = pl.BlockSpec(memory_space=pl.ANY)
    sem = pltpu.SemaphoreType.DMA
    grid_spec = pltpu.PrefetchScalarGridSpec(
        num_scalar_prefetch=1, grid=(s_n, mt),
        in_specs=[pl.BlockSpec((tm, k), lambda s, i, jc_ref: (jnp.where(s == 0, i, mt - 1), 0)),
                  pl.BlockSpec((1, k), lambda s, i, jc_ref: (0, 0)), any_spec, any_spec, any_spec],
        out_specs=(pl.BlockSpec((None, tm, ns),
                                lambda s, i, jc_ref: (_slot_of_chip(jc_ref[0] ^ _peer_of_step(s)), i, 0)),
                   pl.BlockSpec((tm, k), lambda s, i, jc_ref: (jnp.where(s == 0, i, mt - 1), 0)),
                   any_spec, any_spec, any_spec),
        scratch_shapes=[pltpu.VMEM((2, k, ns), BF16), pltpu.VMEM((mt, tm, k), BF16),
                        sem((3,)), sem((3,)), sem((3,)), sem((3,)), sem((6,)), sem((6,)), sem((2,)), sem((2,))])
    return pl.pallas_call(
        body, name="mm_in", grid_spec=grid_spec,
        out_shape=(jax.ShapeDtypeStruct((s_n, m, ns), F32), jax.ShapeDtypeStruct((m, k), BF16),
                   jax.ShapeDtypeStruct(stk_w_in.shape, BF16))
        + tuple(jax.ShapeDtypeStruct((NCHIP,) + v.shape, v.dtype) for v in (conv_a_s, conv_f_s)),
        input_output_aliases={3: 2},
        compiler_params=pltpu.CompilerParams(dimension_semantics=("arbitrary", "arbitrary"),
                                             vmem_limit_bytes=VMEM_BIG, has_side_effects=True),
    )(jc, xin, gain, stk_w_in, conv_a_s, conv_f_s)


def _mm_dh1_exchange(dp4, w_in3, p_w_in, ps3, x, dx1, gain, tm):
    s, k, ns = w_in3.shape
    m = dp4.shape[1]
    mt = m // tm

    def body(a_ref, b_ref, p_ref, p3_ref, x_ref, dx1_ref, g_ref, o_ref, q_ref, q3_ref, dg_ref, ssem, rsem):
        i, j = pl.program_id(0), pl.program_id(1)
        c = lax.axis_index("c")
        sends, arrive = _exchange_copies(
            [q_ref, q3_ref], [lambda chip: p_ref.at[chip], lambda chip: p3_ref.at[pl.ds(c * D, D), :]], ssem, rsem)

        @pl.when((i == 0) & (j == 0))
        def _():
            for cp in sends:
                cp.start()

        @pl.when(j == 0)
        def _():
            o_ref[...] = _dot_nt(a_ref[...], b_ref[...])

        @pl.when(j > 0)
        def _():
            o_ref[...] += _dot_nt(a_ref[...], b_ref[...])

        @pl.when(j == s - 1)
        def _():
            dh = o_ref[...]
            r, n = _rms_stats(x_ref[...])
            o_ref[...] = dx1_ref[...] + _rms_bwd(dh * g_ref[...], n, r)
            dgv = jnp.sum(dh * n, axis=0, keepdims=True)

            @pl.when(i == 0)
            def _():
                dg_ref[...] = dgv

            @pl.when(i > 0)
            def _():
                dg_ref[...] += dgv

        @pl.when((i == mt - 1) & (j == s - 1))
        def _():
            for cp in arrive:
                cp.wait_recv()
            for cp in sends:
                cp.wait_send()

    any_spec = pl.BlockSpec(memory_space=pl.ANY)
    row_tile = pl.BlockSpec((tm, k), lambda i, j: (i, 0))
    vec = pl.BlockSpec((1, k), lambda i, j: (0, 0))
    return pl.pallas_call(
        body, name="mm_dh1",
        out_shape=(jax.ShapeDtypeStruct((m, k), F32), jax.ShapeDtypeStruct(p_w_in.shape, BF16),
                   jax.ShapeDtypeStruct((NCHIP, D, HD), F32), jax.ShapeDtypeStruct((1, k), F32)),
        grid=(mt, s),
        in_specs=[pl.BlockSpec((None, tm, ns), lambda i, j: (_slot_of_chip(j), i, 0)),
                  pl.BlockSpec((None, k, ns), lambda i, j: (j, 0, 0)), any_spec, any_spec, row_tile, row_tile, vec],
        out_specs=(row_tile, any_spec, any_spec, vec),
        scratch_shapes=[pltpu.SemaphoreType.DMA((6,)), pltpu.SemaphoreType.DMA((6,))],
        compiler_params=pltpu.CompilerParams(dimension_semantics=("arbitrary", "arbitrary"),
                                             vmem_limit_bytes=VMEM_BIG, has_side_effects=True),
    )(dp4, w_in3, p_w_in, ps3, x, dx1, gain)


def _mm_dh2_norm(dup, w_up4, dx2, x1, gain, tm):
    s, k, ns = w_up4.shape
    m = dup.shape[0]

    def body(a_ref, b_ref, dx2_ref, x1_ref, g_ref, o_ref, dg_ref):
        i, j = pl.program_id(0), pl.program_id(1)

        @pl.when(j == 0)
        def _():
            o_ref[...] = _dot_nt(a_ref[...], b_ref[...])

        @pl.when(j > 0)
        def _():
            o_ref[...] += _dot_nt(a_ref[...], b_ref[...])

        @pl.when(j == s - 1)
        def _():
            dh = o_ref[...]
            r, n = _rms_stats(x1_ref[...])
            o_ref[...] = dx2_ref[...] + _rms_bwd(dh * g_ref[...], n, r)
            dgv = jnp.sum(dh * n, axis=0, keepdims=True)

            @pl.when(i == 0)
            def _():
                dg_ref[...] = dgv

            @pl.when(i > 0)
            def _():
                dg_ref[...] += dgv

    row_tile = pl.BlockSpec((tm, k), lambda i, j: (i, 0))
    vec = pl.BlockSpec((1, k), lambda i, j: (0, 0))
    return pl.pallas_call(
        body, name="mm_dh2",
        out_shape=(jax.ShapeDtypeStruct((m, k), F32), jax.ShapeDtypeStruct((1, k), F32)),
        grid=(m // tm, s),
        in_specs=[pl.BlockSpec((tm, ns), lambda i, j: (i, j)), pl.BlockSpec((None, k, ns), lambda i, j: (j, 0, 0)),
                  row_tile, row_tile, vec],
        out_specs=(row_tile, vec),
        compiler_params=_params(("arbitrary", "arbitrary"), VMEM_BIG),
    )(dup, w_up4, dx2, x1, gain)


def _mm_nt(a, b, out_dtype, tm, name):
    m, k = a.shape
    n = b.shape[0]

    def body(a_ref, b_ref, o_ref):
        o_ref[...] = _dot_nt(a_ref[...], b_ref[...]).astype(out_dtype)

    return pl.pallas_call(
        body, name=name, out_shape=jax.ShapeDtypeStruct((m, n), out_dtype), grid=(m // tm,),
        in_specs=[pl.BlockSpec((tm, k), lambda i: (i, 0)), pl.BlockSpec((n, k), lambda i: (0, 0))],
        out_specs=pl.BlockSpec((tm, n), lambda i: (i, 0)),
        compiler_params=_params(("parallel",), VMEM_BIG),
    )(a, b)


def _mm_tn(a, g, tkk, tn, tk, name, stacked_slot_fn=None, stacked_out=False):
    m, k = a.shape
    if stacked_slot_fn is not None:
        n = g.shape[0] * g.shape[2]
        g_spec = pl.BlockSpec((None, tk, tn), lambda kk, j, mm: (stacked_slot_fn(j), mm, 0))
    else:
        n = g.shape[1]
        g_spec = pl.BlockSpec((tk, tn), lambda kk, j, mm: (mm, j))
    steps = m // tk

    def body(a_ref, g_ref, o_ref, acc_ref):
        mm = pl.program_id(2)

        @pl.when(mm == 0)
        def _():
            acc_ref[...] = _dot_tn(a_ref[...], g_ref[...])

        @pl.when(mm > 0)
        def _():
            acc_ref[...] += _dot_tn(a_ref[...], g_ref[...])

        @pl.when(mm == steps - 1)
        def _():
            o_ref[...] = acc_ref[...].astype(BF16)

    if stacked_out:
        out_shape = jax.ShapeDtypeStruct((n // tn, k, tn), BF16)
        out_spec = pl.BlockSpec((None, tkk, tn), lambda kk, j, mm: (j, kk, 0))
    else:
        out_shape = jax.ShapeDtypeStruct((k, n), BF16)
        out_spec = pl.BlockSpec((tkk, tn), lambda kk, j, mm: (kk, j))
    return pl.pallas_call(
        body, name=name, out_shape=out_shape, grid=(k // tkk, n // tn, steps),
        in_specs=[pl.BlockSpec((tk, tkk), lambda kk, j, mm: (mm, kk)), g_spec],
        out_specs=out_spec,
        scratch_shapes=[pltpu.VMEM((tkk, tn), F32)],
        compiler_params=_params(("parallel", "parallel", "arbitrary"), VMEM_BIG),
    )(a, g)


def _lru_gates(xc, wa_ref, wx_ref, ba, bx, lam):
    xcb = xc.astype(BF16)
    ra = jnp.concatenate([_dot(xcb[:, n * HD:(n + 1) * HD], wa_ref[n]) for n in range(NH)], axis=1) + ba
    ix = jnp.concatenate([_dot(xcb[:, n * HD:(n + 1) * HD], wx_ref[n]) for n in range(NH)], axis=1) + bx
    r = _sig(ra)
    ig = _sig(ix)
    z = -lam
    sp = jnp.maximum(z, 0.0) + jnp.log1p(jnp.exp(-jnp.abs(z)))
    log_a = -LRU_C * r * sp
    a = jnp.exp(log_a)
    z2 = 2.0 * log_a
    series = -z2 * (1.0 + z2 * (0.5 + z2 * (1.0 / 6.0 + z2 * (1.0 / 24.0))))
    om = jnp.where(z2 > -0.02, series, 1.0 - jnp.exp(z2))
    mult = jnp.sqrt(om)
    return xcb, r, ig, sp, a, mult


def _mixer_a_fwd(p4, cw, cb, wa, wx, ba, bx, lam, stk, names, tt):
    t = p4.shape[1]
    ng = len(stk)

    def body(p_ref, cw_ref, cb_ref, wa_ref, wx_ref, ba_ref, bx_ref, lam_ref, *rest):
        ya_ref, h_ref, sv_ref = rest[ng:ng + 3]
        halo, hc, ssem, rsem = rest[2 * ng + 3:]
        i = pl.program_id(0)
        finish = _ici_leg_behind(rest[ng + 3:2 * ng + 3], names, ssem, rsem, i == 0, lambda: i == t // tt - 1)

        @pl.when(i == 0)
        def _():
            halo[...] = jnp.zeros((8, D), F32)
            hc[...] = jnp.zeros((1, D), F32)

        xa = p_ref[:, 0:D]
        ga = p_ref[:, D:2 * D]
        xe = jnp.concatenate([halo[...], xa], axis=0)
        xc = (cb_ref[...] + cw_ref[3:4, :] * xe
              + sum(cw_ref[3 - s:4 - s, :] * pltpu.roll(xe, s, 0) for s in (1, 2, 3)))[8:, :]
        halo[...] = xa[tt - 8:, :]
        _, r, ig, _, a, mult = _lru_gates(xc, wa_ref, wx_ref, ba_ref[...], bx_ref[...], lam_ref[...])
        u = mult * ig * xc
        h = _scan_down(a, u, hc[...])
        hc[...] = h[tt - 1:tt, :]
        h_ref[...] = h
        ya_ref[...] = (h * _gelu(ga)).astype(BF16)
        for idx, val in enumerate((xc, r, ig, a, mult)):
            sv_ref[idx] = val
        finish()

    full = lambda shape: pl.BlockSpec(shape, lambda i: (0,) * len(shape))
    any_spec = pl.BlockSpec(memory_space=pl.ANY)
    outs = pl.pallas_call(
        body, name="mixer_a_fwd",
        out_shape=(jax.ShapeDtypeStruct((t, D), BF16), jax.ShapeDtypeStruct((t, D), F32),
                   jax.ShapeDtypeStruct((5, t, D), F32)) + tuple(jax.ShapeDtypeStruct(v.shape, v.dtype) for v in stk),
        grid=(t // tt,),
        in_specs=[pl.BlockSpec((None, tt, 2 * D), lambda i: (SLOT_A, i, 0)),
                  full((4, D)), full((1, D)), full((NH, HD, HD)), full((NH, HD, HD)),
                  full((1, D)), full((1, D)), full((1, D))] + [any_spec] * ng,
        out_specs=(pl.BlockSpec((tt, D), lambda i: (i, 0)), pl.BlockSpec((tt, D), lambda i: (i, 0)),
                   pl.BlockSpec((5, tt, D), lambda i: (0, i, 0))) + (any_spec,) * ng,
        scratch_shapes=[pltpu.VMEM((8, D), F32), pltpu.VMEM((1, D), F32),
                        pltpu.SemaphoreType.DMA((3 * ng,)), pltpu.SemaphoreType.DMA((3 * ng,))],
        input_output_aliases={8 + w: 3 + w for w in range(ng)},
        compiler_params=pltpu.CompilerParams(dimension_semantics=("arbitrary",), vmem_limit_bytes=VMEM_BIG,
                                             has_side_effects=True),
    )(p4, cw, cb, wa, wx, ba, bx, lam, *stk)
    return outs[0], outs[1], outs[2], list(outs[3:])


def _chunk_masks(tt):
    row = lax.broadcasted_iota(jnp.int32, (tt, tt), 0)
    col = lax.broadcasted_iota(jnp.int32, (tt, tt), 1)
    same = jnp.right_shift(row, 5) == jnp.right_shift(col, 5)
    return same & (col <= row)


def _hg_head_fwd(q, fz, lbh, saved=None):
    sgn = _sig(-fz)
    k = (1.0 - lbh) * sgn
    if saved is None:
        sg = _sig(fz)
        f = lbh + (1.0 - lbh) * sg
        g = _chunk_cumsum(jnp.log(f))
    else:
        sg, g = saved
        f = lbh + (1.0 - lbh) * sg
    gu = _chunk_last(g) - g
    eg = jnp.exp(g)
    eng = jnp.exp(-g)
    egu = jnp.exp(gu)
    qt = q * eg
    kt = k * eng
    kd = k * egu
    return sg, sgn, f, k, g, eg, eng, egu, qt, kt, kd


def _lb_of(logits_ref):
    return _sig(logits_ref[0:1, :] - logits_ref[1:2, :])


def _hgrn2_fwd(p4, logits, gnorm, stk, names, tt):
    t = p4.shape[1]
    nc = tt // CH
    ng = len(stk)

    def body(p_ref, lg_ref, gn_ref, *rest):
        yb_ref, o_ref, ss_ref, sv_ref = rest[ng:ng + 4]
        st, ssem, rsem = rest[2 * ng + 4:]
        i = pl.program_id(0)
        finish = _ici_leg_behind(rest[ng + 4:2 * ng + 4], names, ssem, rsem, i == 0, lambda: i == t // tt - 1)

        @pl.when(i == 0)
        def _():
            st[...] = jnp.zeros((NH, HD, HD), F32)

        low = _chunk_masks(tt)
        lb = _lb_of(lg_ref)
        heads = [slice(h * HD, (h + 1) * HD) for h in range(NH)]
        sg, _, _, _, g, _, _, _, qt, kt, kd = _hg_head_fwd(p_ref[0, :, 0:D], p_ref[0, :, D:2 * D], lb)
        sv_ref[0] = sg
        sv_ref[1] = g
        qtb, ktb, kdb, vb = qt.astype(BF16), kt.astype(BF16), kd.astype(BF16), p_ref[1, :, 0:D].astype(BF16)
        decs = [jnp.exp(g[c * CH + CH - 1:c * CH + CH, :]) for c in range(nc)]
        o_in = []
        for hs in heads:
            att = jnp.where(low, _dot_nt(qtb[:, hs], ktb[:, hs]), 0.0)
            o_in.append(_dot(att.astype(BF16), vb[:, hs]))
        s_t = [st[h] for h in range(NH)]
        pieces = [[None] * nc for _ in range(NH)]
        for c in range(nc):
            sl = slice(c * CH, (c + 1) * CH)
            for h, hs in enumerate(heads):
                s_bf = s_t[h].astype(BF16)
                ss_ref[c, h] = s_bf
                pieces[h][c] = o_in[h][sl] + _dot_nt(qtb[sl, hs], s_bf)
                s_t[h] = s_t[h] * decs[c][:, hs] + _dot_tn(vb[sl, hs], kdb[sl, hs])
        for h, hs in enumerate(heads):
            st[h] = s_t[h]
            o = jnp.concatenate(pieces[h], axis=0)
            _, n = _rms_stats(o)
            og = p_ref[1, :, D + h * HD:D + (h + 1) * HD]
            o_ref[:, hs] = o
            yb_ref[:, hs] = (n * gn_ref[:, hs] * (og * _sig(og))).astype(BF16)
        finish()

    any_spec = pl.BlockSpec(memory_space=pl.ANY)
    outs = pl.pallas_call(
        body, name="hgrn2_fwd",
        out_shape=(jax.ShapeDtypeStruct((t, D), BF16), jax.ShapeDtypeStruct((t, D), F32),
                   jax.ShapeDtypeStruct((t // CH, NH, HD, HD), BF16), jax.ShapeDtypeStruct((2, t, D), F32))
        + tuple(jax.ShapeDtypeStruct(v.shape, v.dtype) for v in stk),
        grid=(t // tt,),
        in_specs=[pl.BlockSpec((2, tt, 2 * D), lambda i: (0, i, 0)),
                  pl.BlockSpec((2, D), lambda i: (0, 0)), pl.BlockSpec((1, D), lambda i: (0, 0))] + [any_spec] * ng,
        out_specs=(pl.BlockSpec((tt, D), lambda i: (i, 0)), pl.BlockSpec((tt, D), lambda i: (i, 0)),
                   pl.BlockSpec((nc, NH, HD, HD), lambda i: (i, 0, 0, 0)),
                   pl.BlockSpec((2, tt, D), lambda i: (0, i, 0))) + (any_spec,) * ng,
        scratch_shapes=[pltpu.VMEM((NH, HD, HD), F32), pltpu.SemaphoreType.DMA((3 * ng,)),
                        pltpu.SemaphoreType.DMA((3 * ng,))],
        input_output_aliases={3 + w: 4 + w for w in range(ng)},
        compiler_params=pltpu.CompilerParams(dimension_semantics=("arbitrary",), vmem_limit_bytes=VMEM_BIG,
                                             has_side_effects=True),
    )(p4, logits, gnorm, *stk)
    return outs[0], outs[1], outs[2], outs[3], list(outs[4:])


def _mid_fwd(ya, yb, p4, x, wa, wb, wo, g_pm, g_pf, tt):
    t = x.shape[0]

    def body(ya_ref, yb_ref, gt_ref, x_ref, wa_ref, wb_ref, wo_ref, gpm_ref, gpf_ref,
             za_ref, zb_ref, mix_ref, m2_ref, x1_ref, h2_ref):
        za = _dot(ya_ref[...], wa_ref[...])
        zb = _dot(yb_ref[...], wb_ref[...])
        mix = _sig(gt_ref[:, 0:D]) * za + _sig(gt_ref[:, D:2 * D]) * zb
        mixb = mix.astype(BF16)
        m2 = _dot(mixb, wo_ref[...])
        _, n2 = _rms_stats(m2)
        x1 = x_ref[...] + n2 * gpm_ref[...]
        _, n1 = _rms_stats(x1)
        za_ref[...] = za.astype(BF16)
        zb_ref[...] = zb.astype(BF16)
        mix_ref[...] = mixb
        m2_ref[...] = m2
        x1_ref[...] = x1
        h2_ref[...] = (n1 * gpf_ref[...]).astype(BF16)

    row = lambda dt: jax.ShapeDtypeStruct((t, D), dt)
    tile = pl.BlockSpec((tt, D), lambda i: (i, 0))
    wsp = pl.BlockSpec((D, D), lambda i: (0, 0))
    vec = pl.BlockSpec((1, D), lambda i: (0, 0))
    return pl.pallas_call(
        body, name="mid_fwd",
        out_shape=(row(BF16), row(BF16), row(BF16), row(F32), row(F32), row(BF16)),
        grid=(t // tt,),
        in_specs=[tile, tile, pl.BlockSpec((None, tt, 2 * D), lambda i: (SLOT_G, i, 0)), tile,
                  wsp, wsp, wsp, vec, vec],
        out_specs=(tile,) * 6,
        compiler_params=_params(("parallel",), VMEM_BIG),
    )(ya, yb, p4, x, wa, wb, wo, g_pm, g_pf)


def _up_act_fwd(h2, w_up4, cfw, cfb, tm):
    t = h2.shape[0]
    ns = SH_UP

    def body(a_ref, ah_ref, wg_ref, wv_ref, cwg_ref, cwv_ref, cbg_ref, cbv_ref,
             pg_ref, pv_ref, y_ref, uv_ref, gl_ref, dgl_ref):
        i = pl.program_id(1)
        rows = jnp.concatenate([ah_ref[...], a_ref[...]], axis=0)
        ups = []
        for w_ref, cw_ref, cb_ref, pre_ref in ((wg_ref, cwg_ref, cbg_ref, pg_ref), (wv_ref, cwv_ref, cbv_ref, pv_ref)):
            pre = _dot(rows, w_ref[...])
            pre_ref[...] = pre[16:, :].astype(BF16)
            xe = jnp.concatenate([jnp.where(i > 0, pre[8:16, :], 0.0), pre[16:, :]], axis=0)
            up = (cb_ref[...] + cw_ref[2:3, :] * xe + cw_ref[1:2, :] * pltpu.roll(xe, 1, 0)
                  + cw_ref[0:1, :] * pltpu.roll(xe, 2, 0))
            ups.append(up[8:, :])
        gl, dgl = _gelu_and_grad(ups[0])
        y_ref[...] = (gl * ups[1]).astype(BF16)
        uv_ref[...] = ups[1].astype(BF16)
        gl_ref[...] = gl.astype(BF16)
        dgl_ref[...] = dgl.astype(BF16)

    hb = tm // 16
    tile = pl.BlockSpec((tm, ns), lambda p, i: (i, p))
    return pl.pallas_call(
        body, name="up_act_fwd",
        out_shape=(jax.ShapeDtypeStruct((t, DFF), BF16),) * 6,
        grid=(2, t // tm),
        in_specs=[pl.BlockSpec((tm, D), lambda p, i: (i, 0)),
                  pl.BlockSpec((16, D), lambda p, i: (jnp.maximum(i * hb - 1, 0), 0)),
                  pl.BlockSpec((None, D, ns), lambda p, i: (p, 0, 0)),
                  pl.BlockSpec((None, D, ns), lambda p, i: (p + 2, 0, 0)),
                  pl.BlockSpec((3, ns), lambda p, i: (0, p)), pl.BlockSpec((3, ns), lambda p, i: (0, p + 2)),
                  pl.BlockSpec((1, ns), lambda p, i: (0, p)), pl.BlockSpec((1, ns), lambda p, i: (0, p + 2))],
        out_specs=(tile,) * 6,
        compiler_params=_params(("parallel", "parallel"), VMEM_BIG),
    )(h2, h2, w_up4, w_up4, cfw, cfw, cfb, cfb)


def _down_loss(y, wdn, x1, tgt, g_post, tt):
    t = x1.shape[0]

    def body(y_ref, w_ref, x1_ref, t_ref, g_ref, dx2_ref, dm3_ref, lossv_ref, dg_ref):
        i = pl.program_id(0)
        m3 = _dot(y_ref[...], w_ref[...])
        r, n3 = _rms_stats(m3)
        g = g_ref[...]
        e = x1_ref[...] + n3 * g - t_ref[...]
        dx2 = e * (1.0 / D)
        dx2_ref[...] = dx2
        dm3_ref[...] = _rms_bwd(dx2 * g, n3, r).astype(BF16)
        lv = jnp.sum(e * e, axis=0, keepdims=True)
        dgv = jnp.sum(dx2 * n3, axis=0, keepdims=True)

        @pl.when(i == 0)
        def _():
            lossv_ref[...] = lv
            dg_ref[...] = dgv

        @pl.when(i > 0)
        def _():
            lossv_ref[...] += lv
            dg_ref[...] += dgv

    tile = pl.BlockSpec((tt, D), lambda i: (i, 0))
    vec = pl.BlockSpec((1, D), lambda i: (0, 0))
    return pl.pallas_call(
        body, name="down_loss",
        out_shape=(jax.ShapeDtypeStruct((t, D), F32), jax.ShapeDtypeStruct((t, D), BF16),
                   jax.ShapeDtypeStruct((1, D), F32), jax.ShapeDtypeStruct((1, D), F32)),
        grid=(t // tt,),
        in_specs=[pl.BlockSpec((tt, DFF), lambda i: (i, 0)), pl.BlockSpec((DFF, D), lambda i: (0, 0)),
                  tile, tile, vec],
        out_specs=(tile, tile, vec, vec),
        compiler_params=_params(("arbitrary",), VMEM_BIG),
    )(y, wdn, x1, tgt, g_post)


def _ffn_act_bwd(dy, pre_g, pre_v, uv, gl, dgl, cfw, tt):
    t = dy.shape[0]
    nt = t // tt

    def body(dy_ref, dyn_ref, pg_ref, pv_ref, uv_ref, uvn_ref, gl_ref, gln_ref, dgl_ref, dgln_ref, cw_ref,
             du_ref, dcw_ref, dcb_ref):
        i = pl.program_id(0)
        n = tt + 8
        next_live = jnp.where(i < nt - 1, 1.0, 0.0)
        ext = lambda ref, nref: jnp.concatenate([ref[...].astype(F32), nref[...].astype(F32)[0:8, :]], axis=0)
        dy = jnp.concatenate([dy_ref[...].astype(F32), dyn_ref[...].astype(F32)[0:8, :] * next_live], axis=0)
        ds = (dy * ext(uv_ref, uvn_ref) * ext(dgl_ref, dgln_ref), dy * ext(gl_ref, gln_ref))
        dcw_parts, dcb_parts = [], []
        for hh, c0 in enumerate((0, DFF)):
            cs = slice(c0, c0 + DFF)
            dd = ds[hh]
            d1 = pltpu.roll(dd, n - 1, 0)
            d2 = pltpu.roll(dd, n - 2, 0)
            du_ref[:, cs] = (cw_ref[2:3, cs] * dd + cw_ref[1:2, cs] * d1 + cw_ref[0:1, cs] * d2)[0:tt, :].astype(BF16)
            x = (pg_ref, pv_ref)[hh][...].astype(F32)
            dcw_parts.append(jnp.concatenate(
                [jnp.sum(dk[0:tt, :] * x, axis=0, keepdims=True) for dk in (d2, d1, dd)], axis=0))
            dcb_parts.append(jnp.sum(dd[0:tt, :], axis=0, keepdims=True))
        dcw = jnp.concatenate(dcw_parts, axis=1)
        dcb = jnp.concatenate(dcb_parts, axis=1)

        @pl.when(i == 0)
        def _():
            dcw_ref[...] = dcw
            dcb_ref[...] = dcb

        @pl.when(i > 0)
        def _():
            dcw_ref[...] += dcw
            dcb_ref[...] += dcb

    half = pl.BlockSpec((tt, DFF), lambda i: (i, 0))
    half_next = pl.BlockSpec((16, DFF), lambda i: (jnp.minimum((i + 1) * (tt // 16), t // 16 - 1), 0))
    return pl.pallas_call(
        body, name="ffn_act_bwd",
        out_shape=(jax.ShapeDtypeStruct((t, DUP), BF16), jax.ShapeDtypeStruct((3, DUP), F32),
                   jax.ShapeDtypeStruct((1, DUP), F32)),
        grid=(nt,),
        in_specs=[half, half_next, half, half,
                  half, half_next, half, half_next, half, half_next,
                  pl.BlockSpec((3, DUP), lambda i: (0, 0))],
        out_specs=(pl.BlockSpec((tt, DUP), lambda i: (i, 0)), pl.BlockSpec((3, DUP), lambda i: (0, 0)),
                   pl.BlockSpec((1, DUP), lambda i: (0, 0))),
        compiler_params=_params(("arbitrary",), VMEM_BIG),
    )(dy, dy, pre_g, pre_v, uv, uv, gl, gl, dgl, dgl, cfw)


def _mid_bwd(dx1, m2, za, zb, p4, wa, wb, wo, g_pm, tt):
    t = dx1.shape[0]

    def body(dx1_ref, m2_ref, za_ref, zb_ref, gt_ref, wa_ref, wb_ref, wo_ref, gpm_ref,
             dm2_ref, dza_ref, dzb_ref, dya_ref, dyb_ref, dp_ref, dgpm_ref):
        i = pl.program_id(0)
        dx1 = dx1_ref[...]
        r2, n2 = _rms_stats(m2_ref[...])
        dm2 = _rms_bwd(dx1 * gpm_ref[...], n2, r2).astype(BF16)
        dmix = _dot_nt(dm2, wo_ref[...])
        sa = _sig(gt_ref[:, 0:D])
        sb = _sig(gt_ref[:, D:2 * D])
        dza = (dmix * sa).astype(BF16)
        dzb = (dmix * sb).astype(BF16)
        dp_ref[:, 0:D] = (dmix * za_ref[...].astype(F32) * sa * (1.0 - sa)).astype(BF16)
        dp_ref[:, D:2 * D] = (dmix * zb_ref[...].astype(F32) * sb * (1.0 - sb)).astype(BF16)
        dm2_ref[...] = dm2
        dza_ref[...] = dza
        dzb_ref[...] = dzb
        dya_ref[...] = _dot_nt(dza, wa_ref[...]).astype(BF16)
        dyb_ref[...] = _dot_nt(dzb, wb_ref[...]).astype(BF16)
        dgpm = jnp.sum(dx1 * n2, axis=0, keepdims=True)

        @pl.when(i == 0)
        def _():
            dgpm_ref[...] = dgpm

        @pl.when(i > 0)
        def _():
            dgpm_ref[...] += dgpm

    row = lambda dt: jax.ShapeDtypeStruct((t, D), dt)
    tile = pl.BlockSpec((tt, D), lambda i: (i, 0))
    wsp = pl.BlockSpec((D, D), lambda i: (0, 0))
    vec = pl.BlockSpec((1, D), lambda i: (0, 0))
    gates = pl.BlockSpec((None, tt, 2 * D), lambda i: (SLOT_G, i, 0))
    return pl.pallas_call(
        body, name="mid_bwd",
        out_shape=(row(BF16), row(BF16), row(BF16), row(BF16), row(BF16),
                   jax.ShapeDtypeStruct((NCHIP, t, 2 * D), BF16), jax.ShapeDtypeStruct((1, D), F32)),
        grid=(t // tt,),
        in_specs=[tile, tile, tile, tile, gates, wsp, wsp, wsp, vec],
        out_specs=(tile, tile, tile, tile, tile, gates, vec),
        compiler_params=_params(("arbitrary",), VMEM_BIG),
    )(dx1, m2, za, zb, p4, wa, wb, wo, g_pm)


def _hgrn2_bwd(p4, o_all, ss, saved, dyb, dp4, logits, gnorm, p_early, tt):
    t = p4.shape[1]
    nt = t // tt
    nc = tt // CH
    ne = len(p_early)

    def body(p_ref, o_ref, ss_ref, sv_ref, dyb_ref, dp_in, lg_ref, gn_ref, *rest):
        del dp_in
        pe = rest[:ne]
        dp_ref, dlb_ref, dgn_ref = rest[ne:ne + 3]
        qe = rest[ne + 3:2 * ne + 3]
        dst, ssem, rsem = rest[2 * ne + 3:]
        i = pl.program_id(0)
        sends, arrive = _exchange_copies(qe, [(lambda chip, r=r: r.at[chip]) for r in pe], ssem, rsem)

        @pl.when(i == 0)
        def _():
            dst[...] = jnp.zeros((NH, HD, HD), F32)
            for cp in sends:
                cp.start()

        low = _chunk_masks(tt)
        lb = _lb_of(lg_ref)
        heads = [slice(h * HD, (h + 1) * HD) for h in range(NH)]
        sg, sgn, f, k, g, eg, eng, egu, qt, kt, kd = _hg_head_fwd(p_ref[0, :, 0:D], p_ref[0, :, D:2 * D], lb,
                                                                  (sv_ref[0], sv_ref[1]))
        qtb, ktb, kdb, vb = qt.astype(BF16), kt.astype(BF16), kd.astype(BF16), p_ref[1, :, 0:D].astype(BF16)
        decs = [jnp.exp(g[c * CH + CH - 1:c * CH + CH, :]) for c in range(nc)]
        og = p_ref[1, :, D:2 * D]
        so = _sig(og)
        dyb = dyb_ref[...].astype(F32)
        dob = dyb * (og * so)
        rn = [_rms_stats(o_ref[:, hs]) for hs in heads]
        r_all = jnp.concatenate([jnp.broadcast_to(r, (tt, HD)) for r, _ in rn], axis=1)
        n_all = jnp.concatenate([n for _, n in rn], axis=1)
        gd = dob * gn_ref[...]
        proj = jnp.concatenate(
            [jnp.broadcast_to(jnp.mean(gd[:, hs] * n_all[:, hs], axis=-1, keepdims=True), (tt, HD)) for hs in heads],
            axis=1)
        dob_ = (r_all * (gd - n_all * proj)).astype(BF16)
        dog = dyb * (n_all * gn_ref[...]) * (so * (1.0 + og * (1.0 - so)))
        dgn = jnp.sum(dob * n_all, axis=0, keepdims=True)
        dv_in, dqt_in, dkt_h = [], [], []
        for hs in heads:
            att = jnp.where(low, _dot_nt(qtb[:, hs], ktb[:, hs]), 0.0).astype(BF16)
            d_att = jnp.where(low, _dot_nt(dob_[:, hs], vb[:, hs]), 0.0).astype(BF16)
            dv_in.append(_dot_tn(att, dob_[:, hs]))
            dqt_in.append(_dot(d_att, ktb[:, hs]))
            dkt_h.append(_dot_tn(d_att, qtb[:, hs]))
        ds_t = [dst[h] for h in range(NH)]
        dv_p = [[None] * NH for _ in range(nc)]
        dqt_p = [[None] * NH for _ in range(nc)]
        dkd_p = [[None] * NH for _ in range(nc)]
        dgl_p = [[None] * NH for _ in range(nc)]
        for c in reversed(range(nc)):
            sl = slice(c * CH, (c + 1) * CH)
            for h, hs in enumerate(heads):
                s_prev = ss_ref[c, h]
                ds_bf = ds_t[h].astype(BF16)
                dec = decs[c][:, hs]
                dv_p[c][h] = dv_in[h][sl] + _dot_nt(kdb[sl, hs], ds_bf)
                dqt_p[c][h] = dqt_in[h][sl] + _dot(dob_[sl, hs], s_prev)
                dkd_p[c][h] = _dot(vb[sl, hs], ds_bf)
                ddec = jnp.sum(s_prev.astype(F32) * ds_t[h], axis=0, keepdims=True)
                dgl_p[c][h] = jnp.broadcast_to(ddec * dec, (CH, HD))
                ds_t[h] = ds_t[h] * dec + _dot_tn(dob_[sl, hs], qtb[sl, hs])
        for h in range(NH):
            dst[h] = ds_t[h]
        whole = lambda parts: jnp.concatenate([jnp.concatenate(row, axis=1) for row in parts], axis=0)
        dv, dqt, dkd, dgl = whole(dv_p), whole(dqt_p), whole(dkd_p), whole(dgl_p)
        dkt = jnp.concatenate(dkt_h, axis=1)
        dq = dqt * eg
        dk = dkt * eng + dkd * egu
        dg = dqt * qt - dkt * kt
        dgu = dkd * kd
        dlogf = _chunk_revcumsum(dg - dgu) + _chunk_total(dgu) + dgl
        common = sgn * (dlogf / f - dk)
        dfz = (1.0 - lb) * sg * common
        dlb = jnp.sum(common, axis=0, keepdims=True)
        dp_ref[0, :, 0:D] = dq.astype(BF16)
        dp_ref[0, :, D:2 * D] = dfz.astype(BF16)
        dp_ref[1, :, 0:D] = dv.astype(BF16)
        dp_ref[1, :, D:2 * D] = dog.astype(BF16)

        @pl.when(i == 0)
        def _():
            dlb_ref[0:1, :] = dlb
            dgn_ref[...] = dgn

        @pl.when(i > 0)
        def _():
            dlb_ref[0:1, :] += dlb
            dgn_ref[...] += dgn

        @pl.when(i == nt - 1)
        def _():
            d0 = dlb_ref[0:1, :] * lb * (1.0 - lb)
            dlb_ref[0:1, :] = d0
            dlb_ref[1:2, :] = -d0
            for cp in arrive:
                cp.wait_recv()
            for cp in sends:
                cp.wait_send()

    rev = lambda i: nt - 1 - i
    vec = pl.BlockSpec((1, D), lambda i: (0, 0))
    any_spec = pl.BlockSpec(memory_space=pl.ANY)
    outs = pl.pallas_call(
        body, name="hgrn2_bwd",
        out_shape=(jax.ShapeDtypeStruct(dp4.shape, BF16), jax.ShapeDtypeStruct((2, D), F32),
                   jax.ShapeDtypeStruct((1, D), F32)) + tuple(jax.ShapeDtypeStruct(a.shape, BF16) for a in p_early),
        grid=(nt,),
        in_specs=[pl.BlockSpec((2, tt, 2 * D), lambda i: (0, rev(i), 0)),
                  pl.BlockSpec((tt, D), lambda i: (rev(i), 0)),
                  pl.BlockSpec((nc, NH, HD, HD), lambda i: (rev(i), 0, 0, 0)),
                  pl.BlockSpec((2, tt, D), lambda i: (0, rev(i), 0)),
                  pl.BlockSpec((tt, D), lambda i: (rev(i), 0)),
                  any_spec,
                  pl.BlockSpec((2, D), lambda i: (0, 0)), vec] + [any_spec] * ne,
        out_specs=(pl.BlockSpec((2, tt, 2 * D), lambda i: (0, rev(i), 0)),
                   pl.BlockSpec((2, D), lambda i: (0, 0)), vec) + (any_spec,) * ne,
        scratch_shapes=[pltpu.VMEM((NH, HD, HD), F32), pltpu.SemaphoreType.DMA((3 * ne,)),
                        pltpu.SemaphoreType.DMA((3 * ne,))],
        input_output_aliases={5: 0},
        compiler_params=pltpu.CompilerParams(dimension_semantics=("arbitrary",), vmem_limit_bytes=VMEM_BIG,
                                             has_side_effects=True),
    )(p4, o_all, ss, saved, dyb, dp4, logits, gnorm, *p_early)
    return outs[0], outs[1], outs[2], list(outs[3:])


def _mixer_a_bwd(p4, hseq, saved, dya, dp4, cw, wa, wx, lam, tt):
    t = p4.shape[1]
    nt = t // tt

    def body(p_ref, sv_ref, h_ref, hh_ref, dya_ref, dp_in, cw_ref, wa_ref, wx_ref, lam_ref,
             dp_ref, dcw_ref, dcb_ref, dwa_ref, dwx_ref, dba_ref, dbx_ref, dlam_ref,
             dnext, dhc, afc):
        del dp_in
        i = pl.program_id(0)
        first_tile = i == nt - 1

        @pl.when(i == 0)
        def _():
            dnext[...] = jnp.zeros((8, D), F32)
            dhc[...] = jnp.zeros((1, D), F32)
            afc[...] = jnp.zeros((1, D), F32)

        xa = p_ref[:, 0:D]
        ga = p_ref[:, D:2 * D]
        xc, r, ig, a, mult = (sv_ref[idx] for idx in range(5))
        xcb = xc.astype(BF16)
        lam = lam_ref[...]
        sp = jnp.maximum(-lam, 0.0) + jnp.log1p(jnp.exp(-jnp.abs(lam)))
        h = h_ref[...]
        gl, dgl = _gelu_and_grad(ga)
        dya = dya_ref[...].astype(F32)
        dga = dya * h * dgl
        rows = lax.broadcasted_iota(jnp.int32, (tt, 1), 0)
        a_next = jnp.where(rows == tt - 1, afc[...], pltpu.roll(a, tt - 1, 0))
        dh = _scan_up(a_next, dya * gl, dhc[...])
        dhc[...] = dh[0:1, :]
        afc[...] = a[0:1, :]
        h_prev = jnp.where(rows == 0, jnp.where(first_tile, 0.0, hh_ref[7:8, :]), pltpu.roll(h, 1, 0))
        da = dh * h_prev
        dmult = dh * ig * xc
        di = dh * mult * xc
        dlog_a = da * a - dmult * a * a / mult
        dr = dlog_a * (-LRU_C * sp)
        dsp = jnp.sum(dlog_a * (-LRU_C * r), axis=0, keepdims=True)
        dra = dr * r * (1.0 - r)
        dix = di * ig * (1.0 - ig)
        drab = dra.astype(BF16)
        dixb = dix.astype(BF16)
        dxc_lin = []
        dwa_new = []
        dwx_new = []
        for n in range(NH):
            cs = slice(n * HD, (n + 1) * HD)
            dxc_lin.append(_dot_nt(drab[:, cs], wa_ref[n]) + _dot_nt(dixb[:, cs], wx_ref[n]))
            dwa_new.append(_dot_tn(xcb[:, cs], drab[:, cs]))
            dwx_new.append(_dot_tn(xcb[:, cs], dixb[:, cs]))
        dxc = dh * mult * ig + jnp.concatenate(dxc_lin, axis=1)
        de = jnp.concatenate([dxc, dnext[...]], axis=0)
        ups = [de[0:tt, :]] + [pltpu.roll(de, tt + 8 - s, 0)[0:tt, :] for s in (1, 2, 3)]
        dxa = sum(cw_ref[3 - s:4 - s, :] * ups[s] for s in range(4))
        dnext[...] = dxc[0:8, :]
        dp_ref[:, 0:D] = dxa.astype(BF16)
        dp_ref[:, D:2 * D] = dga.astype(BF16)
        dcw = jnp.concatenate(
            [jnp.sum(ups[3 - k] * xa, axis=0, keepdims=True) for k in range(4)], axis=0)
        dcb = jnp.sum(dxc, axis=0, keepdims=True)
        dba = jnp.sum(dra, axis=0, keepdims=True)
        dbx = jnp.sum(dix, axis=0, keepdims=True)
        dlam = dsp * (-_sig(-lam))

        @pl.when(i == 0)
        def _():
            dcw_ref[...] = dcw
            dcb_ref[...] = dcb
            dba_ref[...] = dba
            dbx_ref[...] = dbx
            dlam_ref[...] = dlam
            for n in range(NH):
                dwa_ref[n] = dwa_new[n]
                dwx_ref[n] = dwx_new[n]

        @pl.when(i > 0)
        def _():
            dcw_ref[...] += dcw
            dcb_ref[...] += dcb
            dba_ref[...] += dba
            dbx_ref[...] += dbx
            dlam_ref[...] += dlam
            for n in range(NH):
                dwa_ref[n] += dwa_new[n]
                dwx_ref[n] += dwx_new[n]

    rev = lambda i: nt - 1 - i
    hb = tt // 8
    full = lambda shape: pl.BlockSpec(shape, lambda i: (0,) * len(shape))
    vecs = jax.ShapeDtypeStruct((1, D), F32)
    blk = jax.ShapeDtypeStruct((NH, HD, HD), F32)
    return pl.pallas_call(
        body, name="mixer_a_bwd",
        out_shape=(jax.ShapeDtypeStruct(dp4.shape, BF16), jax.ShapeDtypeStruct((4, D), F32), vecs, blk, blk,
                   vecs, vecs, vecs),
        grid=(nt,),
        in_specs=[pl.BlockSpec((None, tt, 2 * D), lambda i: (SLOT_A, rev(i), 0)),
                  pl.BlockSpec((5, tt, D), lambda i: (0, rev(i), 0)),
                  pl.BlockSpec((tt, D), lambda i: (rev(i), 0)),
                  pl.BlockSpec((8, D), lambda i: (jnp.maximum(rev(i) * hb - 1, 0), 0)),
                  pl.BlockSpec((tt, D), lambda i: (rev(i), 0)),
                  pl.BlockSpec(memory_space=pl.ANY),
                  full((4, D)), full((NH, HD, HD)), full((NH, HD, HD)), full((1, D))],
        out_specs=(pl.BlockSpec((None, tt, 2 * D), lambda i: (SLOT_A, rev(i), 0)),
                   full((4, D)), full((1, D)), full((NH, HD, HD)), full((NH, HD, HD)),
                   full((1, D)), full((1, D)), full((1, D))),
        scratch_shapes=[pltpu.VMEM((8, D), F32), pltpu.VMEM((1, D), F32), pltpu.VMEM((1, D), F32)],
        input_output_aliases={5: 0},
        compiler_params=_params(("arbitrary",), VMEM_BIG),
    )(p4, saved, hseq, hseq, dya, dp4, cw, wa, wx, lam)


def _local_step(x, tgt, stk_w_in, stk_rest, conv_a_s, conv_f_s, small, jc, cidx):
    t = x.shape[0]
    tt = min(256, t)
    tm = min(1024, t)
    tk = min(2048, t)
    wa_bf = small["lru_wa"].astype(BF16)
    wx_bf = small["lru_wx"].astype(BF16)

    p4, h1, w_in, conv_a_g, conv_f_g = _mm_in_gather(x, small["norm_pre_mix"], stk_w_in, conv_a_s, conv_f_s, jc, tm)
    conv_a_w = jnp.transpose(conv_a_g, (1, 0, 2)).reshape(8, D)[0:4]
    conv_f_w = jnp.transpose(conv_f_g, (1, 0, 2)).reshape(8, DUP)[0:3]
    stk = dict(zip(REST, stk_rest))
    ya, hseq, saved_a, got_a = _mixer_a_fwd(p4, conv_a_w, small["conv_a_b"], wa_bf, wx_bf, small["lru_ba"],
                                            small["lru_bx"], small["lru_lambda"],
                                            [stk[n] for n in REST_A], REST_A, tt)
    yb, o_all, ss, saved_b, got_b = _hgrn2_fwd(p4, small["hg_lb_logits"], small["hg_norm_g"],
                                               [stk[n] for n in REST_B], REST_B, tt)
    w = dict(zip(REST_A + REST_B, _gather_forward(got_a + got_b, REST_A + REST_B)))
    w["w_in"] = w_in
    w_br_a = w["w_branch_a"].reshape(D, D)
    w_br_b = w["w_branch_b"].reshape(D, D)
    w_out = w["w_out"].reshape(D, D)
    w_down = w["w_down"].reshape(DFF, D)
    za, zb, mixb, m2, x1, h2 = _mid_fwd(ya, yb, p4, x, w_br_a, w_br_b, w_out, small["norm_post_mix"],
                                        small["norm_pre_ffn"], min(512, t))
    pre_g, pre_v, y, uv, gl, dgl = _up_act_fwd(h2, w["w_up"], conv_f_w, small["conv_f_b"], tt)
    dx2, dm3, lossv, d_norm_post_ffn = _down_loss(y, w_down, x1, tgt, small["norm_post_ffn"], min(512, t))

    d_w_down = _mm_tn(y, dm3, DFF // 2, D, tk, "mm_dw_down")
    dy = _mm_nt(dm3, w_down, BF16, tm, "mm_dy")
    dup_pre, d_conv_f_w, d_conv_f_b = _ffn_act_bwd(dy, pre_g, pre_v, uv, gl, dgl, conv_f_w, tt)
    d_w_up = _mm_tn(h2, dup_pre, D, SH_UP, tk, "mm_dw_up", stacked_out=True)
    dx1, d_norm_pre_ffn = _mm_dh2_norm(dup_pre, w["w_up"], dx2, x1, small["norm_pre_ffn"], tm)
    dm2, dza, dzb, dya, dyb, dp4, d_norm_post_mix = _mid_bwd(
        dx1, m2, za, zb, p4, w_br_a, w_br_b, w_out, small["norm_post_mix"], tt)
    d_w_out = _mm_tn(mixb, dm2, D, D, tm, "mm_dw_out")
    d_w_br_a = _mm_tn(ya, dza, D, D, tm, "mm_dw_bra")
    d_w_br_b = _mm_tn(yb, dzb, D, D, tm, "mm_dw_brb")
    early = {"w_branch_a": d_w_br_a.reshape(NCHIP, SH_BR, D), "w_branch_b": d_w_br_b.reshape(NCHIP, SH_BR, D),
             "w_out": d_w_out.reshape(NCHIP, SH_BR, D), "w_up": d_w_up, "w_down": d_w_down.reshape(NCHIP, SH_DN, D)}
    rb, _ = _reduce_stage1(early, REST, (), "reduce_d2d_in_early")
    p_rest = [_sum_own_half(early[n], rb[n], cidx, "sum_half_" + n) for n in REST]
    dp4, d_lb, d_hg_norm_g, q_rest = _hgrn2_bwd(p4, o_all, ss, saved_b, dyb, dp4, small["hg_lb_logits"], small["hg_norm_g"],
                                                p_rest, tt)
    dp4, d_conv_a_w, d_conv_a_b, d_lru_wa, d_lru_wx, d_lru_ba, d_lru_bx, d_lru_lambda = _mixer_a_bwd(
        p4, hseq, saved_a, dya, dp4, conv_a_w, wa_bf, wx_bf, small["lru_lambda"], tt)
    d_w_in = _mm_tn(h1, dp4, D, SH_IN, tk, "mm_dw_in", stacked_slot_fn=_slot_of_chip, stacked_out=True)
    s3 = jnp.concatenate([d_lru_wa.reshape(D, HD), d_lru_wx.reshape(D, HD)], axis=0)
    rb, (rs3,) = _reduce_stage1({"w_in": d_w_in}, ("w_in",), (s3,), "reduce_d2d_in_w_in")
    p_w_in = _sum_own_half(d_w_in, rb["w_in"], cidx, "sum_half_w_in")
    ps3 = _add(s3, rs3, "add_s3")
    grad_x, q_w_in, qs3, d_norm_pre_mix = _mm_dh1_exchange(dp4, w_in, p_w_in, ps3, x, dx1, small["norm_pre_mix"], tm)

    smalls = {
        "norm_pre_mix": d_norm_pre_mix, "conv_a_b": d_conv_a_b, "lru_ba": d_lru_ba, "lru_bx": d_lru_bx,
        "lru_lambda": d_lru_lambda, "hg_lb_logits": d_lb, "hg_norm_g": d_hg_norm_g, "norm_post_mix": d_norm_post_mix,
        "norm_pre_ffn": d_norm_pre_ffn, "norm_post_ffn": d_norm_post_ffn, "lossv": lossv,
        "conv_a_w": d_conv_a_w, "s3": (ps3, qs3),
        "conv_f_b": d_conv_f_b, "conv_f_w": d_conv_f_w,
    }
    p_big = dict(zip(REST, p_rest), w_in=p_w_in)
    q_big = dict(zip(REST, q_rest), w_in=q_w_in)
    return grad_x, p_big, q_big, smalls


BIG = ("w_in", "w_branch_a", "w_branch_b", "w_out", "w_up", "w_down")
BIG_SHAPE = {"w_in": (D, SH_IN), "w_branch_a": (SH_BR, D), "w_branch_b": (SH_BR, D), "w_out": (SH_BR, D),
             "w_up": (D, SH_UP), "w_down": (SH_DN, D)}
NBIG = len(BIG)
REST = BIG[1:]
REST_A = ("w_branch_a", "w_branch_b", "w_out", "w_down")
REST_B = ("w_up",)
VEC_ROWS = (("norm_pre_mix", 0, 1), ("conv_a_b", 1, 1), ("lru_ba", 2, 1), ("lru_bx", 3, 1), ("lru_lambda", 4, 1),
            ("hg_lb_logits", 5, 2), ("hg_norm_g", 7, 1), ("norm_post_mix", 8, 1), ("norm_pre_ffn", 9, 1),
            ("norm_post_ffn", 10, 1))
ROW_LOSS = 11
ROW_CONV_A = 12
S1_ROWS = 16
S2_ROWS = 8


def _place():
    x, y, c = lax.axis_index("x"), lax.axis_index("y"), lax.axis_index("c")
    chips = [(1 - x, y), (x, 1 - y), (1 - x, 1 - y)]
    return x, y, c, 2 * x + y, chips


def _remote(src, dst, ssem, rsem, dev):
    return pltpu.make_async_remote_copy(src_ref=src, dst_ref=dst, send_sem=ssem, recv_sem=rsem,
                                        device_id=dev, device_id_type=MESH)


def _hbm_call(body, name, ins, out_shapes, n_sems, aliases=None):
    any_spec = pl.BlockSpec(memory_space=pl.ANY)
    return pl.pallas_call(
        body, name=name, out_shape=tuple(out_shapes),
        in_specs=[any_spec] * len(ins), out_specs=tuple([any_spec] * len(out_shapes)),
        scratch_shapes=[pltpu.SemaphoreType.DMA((n,)) for n in n_sems],
        input_output_aliases=aliases or {},
        compiler_params=pltpu.CompilerParams(has_side_effects=True),
    )(*ins)


def _gather_copies(stk, names, ssem, rsem, fssem=None, frsem=None):
    x, y, c, j, chips = _place()
    sends, arrive, fwds, farrive = [], [], [], []
    for w, n in enumerate(names):
        hw = BIG_SHAPE[n][0] // 2
        mine = stk[w].at[j, pl.ds(c * hw, hw), :]
        for k, (cx, cy) in enumerate(chips):
            i = 3 * w + k
            got = stk[w].at[2 * cx + cy, pl.ds(c * hw, hw), :]
            other = stk[w].at[2 * cx + cy, pl.ds((1 - c) * hw, hw), :]
            sends.append(_remote(mine, mine, ssem.at[i], rsem.at[i], (cx, cy, c)))
            arrive.append(_remote(got, got, ssem.at[i], rsem.at[i], (cx, cy, c)))
            if fssem is not None:
                fwds.append(_remote(got, got, fssem.at[i], frsem.at[i], (x, y, 1 - c)))
                farrive.append(_remote(other, other, fssem.at[i], frsem.at[i], (x, y, 1 - c)))
    return sends, arrive, fwds, farrive


def _ici_leg_behind(stk, names, ssem, rsem, first, last_fn):
    sends, arrive, _, _ = _gather_copies(stk, names, ssem, rsem)

    @pl.when(first)
    def _():
        for cp in sends:
            cp.start()

    def finish():
        @pl.when(last_fn())
        def _():
            for cp in arrive:
                cp.wait_recv()
            for cp in sends:
                cp.wait_send()

    return finish


def _gather_forward(stk, names):
    nw = len(names)

    def body(*refs):
        dst = refs[nw:2 * nw]
        ssem, rsem, fssem, frsem = refs[2 * nw:]
        _, _, fwds, farrive = _gather_copies(dst, names, ssem, rsem, fssem, frsem)
        for cp in fwds:
            cp.start()
        for cp in farrive:
            cp.wait_recv()
        for cp in fwds:
            cp.wait_send()

    out_shapes = [jax.ShapeDtypeStruct(a.shape, a.dtype) for a in stk]
    return _hbm_call(body, "gather_forward", stk, out_shapes, (3 * nw,) * 4, aliases={w: w for w in range(nw)})


def _exchange_copies(dst, pieces, ssem, rsem):
    x, y, c, j, chips = _place()
    sends, arrive = [], []
    for w in range(len(dst)):
        for k, (cx, cy) in enumerate(chips):
            i = 3 * w + k
            sends.append(_remote(pieces[w](2 * cx + cy), dst[w].at[j], ssem.at[i], rsem.at[i], (cx, cy, c)))
            got = dst[w].at[2 * cx + cy]
            arrive.append(_remote(got, got, ssem.at[i], rsem.at[i], (cx, cy, c)))
    return sends, arrive


def _reduce_stage1(big_g, names, smalls, name):
    nb = len(names)
    ins = [big_g[n] for n in names] + list(smalls)
    n_in = len(ins)
    halves = [BIG_SHAPE[n][0] // 2 for n in names]
    out_shapes = [jax.ShapeDtypeStruct((NCHIP, halves[w], BIG_SHAPE[n][1]), big_g[n].dtype)
                  for w, n in enumerate(names)]
    out_shapes += [jax.ShapeDtypeStruct(a.shape, F32) for a in smalls]

    def body(*refs):
        src, dst = refs[:n_in], refs[n_in:2 * n_in]
        ssem, rsem = refs[2 * n_in:]
        x, y, c, _, _ = _place()
        cps = []
        for w in range(n_in):
            s_ = src[w].at[:, pl.ds((1 - c) * halves[w], halves[w]), :] if w < nb else src[w]
            cp = _remote(s_, dst[w], ssem.at[w], rsem.at[w], (x, y, 1 - c))
            cp.start()
            cps.append(cp)
        for cp in cps:
            cp.wait()

    outs = _hbm_call(body, name, ins, out_shapes, (n_in, n_in))
    return dict(zip(names, outs[:nb])), outs[nb:]


def _reduce_stage2(ps1, ps2):
    ins = [ps1, ps2]
    h1, h2 = S1_ROWS // 2, DUP // 2
    out_shapes = [jax.ShapeDtypeStruct((NCHIP, h1, D), F32), jax.ShapeDtypeStruct((NCHIP, S2_ROWS, h2), F32)]

    def body(*refs):
        src, dst = refs[:2], refs[2:4]
        ssem, rsem = refs[4:]
        c = lax.axis_index("c")
        pieces = [lambda chip: src[0].at[pl.ds(c * h1, h1), :],
                  lambda chip: src[1].at[:, pl.ds(c * h2, h2)]]
        sends, arrive = _exchange_copies(dst, pieces, ssem, rsem)
        for cp in sends:
            cp.start()
        for cp in arrive:
            cp.wait_recv()
        for cp in sends:
            cp.wait_send()

    return _hbm_call(body, "reduce_ici_small", ins, out_shapes, (6, 6))


def _reduce_stage3(f_big, fs1, fs2, fs3):
    ins = [f_big[n] for n in BIG] + [fs1, fs2, fs3]
    n_in = len(ins)
    halves = [BIG_SHAPE[n][0] // 2 for n in BIG]
    h1, h2, h3 = S1_ROWS // 2, DUP // 2, D
    out_shapes = [jax.ShapeDtypeStruct(BIG_SHAPE[n], F32) for n in BIG]
    out_shapes += [jax.ShapeDtypeStruct((S1_ROWS, D), F32), jax.ShapeDtypeStruct((S2_ROWS, DUP), F32),
                   jax.ShapeDtypeStruct((2 * D, HD), F32)]

    def body(*refs):
        dst = refs[n_in:2 * n_in]
        ssem, rsem = refs[2 * n_in:]
        x, y, c, _, _ = _place()

        def place(w, which):
            if w < NBIG:
                return dst[w].at[pl.ds(which * halves[w], halves[w]), :]
            if w == NBIG:
                return dst[w].at[pl.ds(which * h1, h1), :]
            if w == NBIG + 1:
                return dst[w].at[:, pl.ds(which * h2, h2)]
            return dst[w].at[pl.ds(which * h3, h3), :]

        cps = [_remote(place(w, c), place(w, c), ssem.at[w], rsem.at[w], (x, y, 1 - c)) for w in range(n_in)]
        for cp in cps:
            cp.start()
        for w in range(n_in):
            got = place(w, 1 - c)
            _remote(got, got, ssem.at[w], rsem.at[w], (x, y, 1 - c)).wait_recv()
        for cp in cps:
            cp.wait_send()

    outs = _hbm_call(body, "reduce_d2d_out", ins, out_shapes, (n_in, n_in), aliases={w: w for w in range(n_in)})
    return dict(zip(BIG, outs[:NBIG])), outs[NBIG], outs[NBIG + 1], outs[NBIG + 2]


def _row_tile(rows):
    for tr in (128, 176, 64, 16, 8):
        if rows % tr == 0:
            return tr
    return rows


def _sum_own_half(g, rb, cidx, name):
    s, rows, cols = g.shape
    half = rows // 2
    tr = half
    nb = half // tr

    def body(c_ref, g_ref, r_ref, o_ref):
        del c_ref
        o_ref[...] = (g_ref[...].astype(F32) + r_ref[...].astype(F32)).astype(BF16)

    grid_spec = pltpu.PrefetchScalarGridSpec(
        num_scalar_prefetch=1, grid=(s, nb),
        in_specs=[pl.BlockSpec((None, tr, cols), lambda k, i, c: (k, c[0] * nb + i, 0)),
                  pl.BlockSpec((None, tr, cols), lambda k, i, c: (k, i, 0))],
        out_specs=pl.BlockSpec((None, tr, cols), lambda k, i, c: (k, i, 0)))
    return pl.pallas_call(
        body, name=name, grid_spec=grid_spec, out_shape=jax.ShapeDtypeStruct((s, half, cols), BF16),
        compiler_params=_params(("parallel", "parallel")),
    )(cidx, g, rb)


def _sum_chips(q, p, jc, name, by_cols=False):
    s, rows, cols = q.shape
    tr = _row_tile(rows)
    nb = rows // tr
    stacked = p.ndim == 3

    def body(jc_ref, q_ref, p_ref, o_ref):
        j = jc_ref[0]
        own = p_ref[...].astype(F32)
        acc = None
        for k in range(NCHIP):
            term = jnp.where(j == k, own, q_ref[k].astype(F32))
            acc = term if acc is None else acc + term
        o_ref[...] = acc

    if by_cols:
        half_spec = pl.BlockSpec((tr, cols), lambda i, jc_ref: (i, jc_ref[1]))
        out_shape = jax.ShapeDtypeStruct((rows, 2 * cols), F32)
    else:
        half_spec = pl.BlockSpec((tr, cols), lambda i, jc_ref: (jc_ref[1] * nb + i, 0))
        out_shape = jax.ShapeDtypeStruct((2 * rows, cols), F32)
    p_spec = pl.BlockSpec((None, tr, cols), lambda i, jc_ref: (jc_ref[0], i, 0)) if stacked else half_spec
    grid_spec = pltpu.PrefetchScalarGridSpec(
        num_scalar_prefetch=1, grid=(nb,),
        in_specs=[pl.BlockSpec((s, tr, cols), lambda i, jc_ref: (0, i, 0)), p_spec],
        out_specs=half_spec)
    return pl.pallas_call(
        body, name=name, grid_spec=grid_spec, out_shape=out_shape,
        compiler_params=_params(("parallel",)),
    )(jc, q, p)


def _place_shard(w, jc, name):
    rows, cols = w.shape
    tr = rows // 2

    def body(jc_ref, w_ref, o_ref):
        del jc_ref
        o_ref[...] = w_ref[...].astype(BF16)

    grid_spec = pltpu.PrefetchScalarGridSpec(
        num_scalar_prefetch=1, grid=(rows // tr,),
        in_specs=[pl.BlockSpec((tr, cols), lambda i, jc_ref: (i, 0))],
        out_specs=pl.BlockSpec((None, tr, cols), lambda i, jc_ref: (jc_ref[0], i, 0)))
    return pl.pallas_call(
        body, name=name, grid_spec=grid_spec, out_shape=jax.ShapeDtypeStruct((NCHIP, rows, cols), BF16),
        compiler_params=_params(("parallel",)),
    )(jc, w)


def _add(a, b, name):
    def body(a_ref, b_ref, o_ref):
        o_ref[...] = a_ref[...] + b_ref[...]

    return pl.pallas_call(body, name=name, out_shape=jax.ShapeDtypeStruct(a.shape, F32))(a, b)


def _pack_small(sm):
    vec_in = [sm[n] for n, _, _ in VEC_ROWS]
    nv = len(vec_in)

    def body(*refs):
        ins, lossv, dcw, dcfb, dcfw, s1, s2 = refs[:nv], refs[nv], refs[nv + 1], refs[nv + 2], refs[nv + 3], \
            refs[nv + 4], refs[nv + 5]
        for ref, (_, r0, nr) in zip(ins, VEC_ROWS):
            s1[r0:r0 + nr, :] = ref[...]
        s1[ROW_LOSS:ROW_LOSS + 1, :] = lossv[...]
        s1[ROW_CONV_A:ROW_CONV_A + 4, :] = dcw[...]
        s2[0:1, :] = dcfb[...]
        s2[1:4, :] = dcfw[...]
        s2[4:8, :] = jnp.zeros((4, DUP), F32)

    return pl.pallas_call(
        body, name="pack_small",
        out_shape=(jax.ShapeDtypeStruct((S1_ROWS, D), F32), jax.ShapeDtypeStruct((S2_ROWS, DUP), F32)),
    )(*vec_in, sm["lossv"], sm["conv_a_w"], sm["conv_f_b"], sm["conv_f_w"])


def _adam_math(w, g, m, v):
    m = ADAM_B1 * m + (1.0 - ADAM_B1) * g
    v = ADAM_B2 * v + (1.0 - ADAM_B2) * (g * g)
    m_hat = m / (1.0 - ADAM_B1 ** ADAM_STEP)
    v_hat = v / (1.0 - ADAM_B2 ** ADAM_STEP)
    delta = -ADAM_LR * (m_hat / (jnp.sqrt(v_hat) + ADAM_EPS) + ADAM_WD * w)
    return delta, m, v


def _adam(w, g, m, v, name):
    rows, cols = w.shape
    tr = _row_tile(rows)

    def body(w_ref, g_ref, m_ref, v_ref, d_ref, mo_ref, vo_ref):
        d_ref[...], mo_ref[...], vo_ref[...] = _adam_math(w_ref[...], g_ref[...], m_ref[...], v_ref[...])

    spec = pl.BlockSpec((tr, cols), lambda i: (i, 0))
    return pl.pallas_call(
        body, name=name, out_shape=(jax.ShapeDtypeStruct(w.shape, F32),) * 3, grid=(rows // tr,),
        in_specs=[spec] * 4, out_specs=(spec,) * 3,
        compiler_params=_params(("parallel",)),
    )(w, g, m, v)


def _adam_small(gs1, gs2, gs3, w, m, v):
    names = [n for n, _, _ in VEC_ROWS] + ["conv_f_b", "lru_wa", "lru_wx"]
    nn = len(names)

    def grad_of(i, g1, g2, g3):
        if i < len(VEC_ROWS):
            _, r0, nr = VEC_ROWS[i]
            return g1[r0:r0 + nr, :]
        if names[i] == "conv_f_b":
            return g2[0:1, :]
        return g3[0] if names[i] == "lru_wa" else g3[1]

    def body(*refs):
        g1, g2, g3 = refs[0], refs[1], refs[2]
        ws, ms, vs = refs[3:3 + nn], refs[3 + nn:3 + 2 * nn], refs[3 + 2 * nn:3 + 3 * nn]
        outs = refs[3 + 3 * nn:]
        for i in range(nn):
            d, mn, vn = _adam_math(ws[i][...], grad_of(i, g1, g2, g3), ms[i][...], vs[i][...])
            outs[i][...] = d
            outs[nn + i][...] = mn
            outs[2 * nn + i][...] = vn

    shapes = [jax.ShapeDtypeStruct(w[n].shape, F32) for n in names]
    outs = pl.pallas_call(body, name="adam_small", out_shape=tuple(shapes * 3))(
        gs1, gs2, gs3, *[w[n] for n in names], *[m[n] for n in names], *[v[n] for n in names])
    return {n: (outs[i], outs[nn + i], outs[2 * nn + i]) for i, n in enumerate(names)}


WEIGHTS = ("norm_pre_mix", "w_in", "conv_a_w", "conv_a_b", "lru_wa", "lru_ba", "lru_wx", "lru_bx", "lru_lambda",
           "hg_lb_logits", "hg_norm_g", "w_branch_a", "w_branch_b", "w_out", "norm_post_mix", "norm_pre_ffn",
           "w_up", "conv_f_w", "conv_f_b", "w_down", "norm_post_ffn")
NW = len(WEIGHTS)


def kernel(x, norm_pre_mix, w_in, conv_a_w, conv_a_b, lru_wa, lru_ba, lru_wx, lru_bx, lru_lambda, hg_lb_logits, hg_norm_g, w_branch_a, w_branch_b, w_out, norm_post_mix, norm_pre_ffn, w_up, conv_f_w, conv_f_b, w_down, norm_post_ffn, loss_target, m_norm_pre_mix, m_w_in, m_conv_a_w, m_conv_a_b, m_lru_wa, m_lru_ba, m_lru_wx, m_lru_bx, m_lru_lambda, m_hg_lb_logits, m_hg_norm_g, m_w_branch_a, m_w_branch_b, m_w_out, m_norm_post_mix, m_norm_pre_ffn, m_w_up, m_conv_f_w, m_conv_f_b, m_w_down, m_norm_post_ffn, v_norm_pre_mix, v_w_in, v_conv_a_w, v_conv_a_b, v_lru_wa, v_lru_ba, v_lru_wx, v_lru_bx, v_lru_lambda, v_hg_lb_logits, v_hg_norm_g, v_w_branch_a, v_w_branch_b, v_w_out, v_norm_post_mix, v_norm_pre_ffn, v_w_up, v_conv_f_w, v_conv_f_b, v_w_down, v_norm_post_ffn):
    rest = (norm_pre_mix, w_in, conv_a_w, conv_a_b, lru_wa, lru_ba, lru_wx, lru_bx, lru_lambda, hg_lb_logits, hg_norm_g, w_branch_a, w_branch_b, w_out, norm_post_mix, norm_pre_ffn, w_up, conv_f_w, conv_f_b, w_down, norm_post_ffn, loss_target, m_norm_pre_mix, m_w_in, m_conv_a_w, m_conv_a_b, m_lru_wa, m_lru_ba, m_lru_wx, m_lru_bx, m_lru_lambda, m_hg_lb_logits, m_hg_norm_g, m_w_branch_a, m_w_branch_b, m_w_out, m_norm_post_mix, m_norm_pre_ffn, m_w_up, m_conv_f_w, m_conv_f_b, m_w_down, m_norm_post_ffn, v_norm_pre_mix, v_w_in, v_conv_a_w, v_conv_a_b, v_lru_wa, v_lru_ba, v_lru_wx, v_lru_bx, v_lru_lambda, v_hg_lb_logits, v_hg_norm_g, v_w_branch_a, v_w_branch_b, v_w_out, v_norm_post_mix, v_norm_pre_ffn, v_w_up, v_conv_f_w, v_conv_f_b, v_w_down, v_norm_post_ffn)
    w_in_args = dict(zip(WEIGHTS, rest[:NW]))
    loss_target = rest[NW]
    m_args = dict(zip(WEIGHTS, rest[NW + 1:2 * NW + 1]))
    v_args = dict(zip(WEIGHTS, rest[2 * NW + 1:3 * NW + 1]))
    shape_of = {n: w_in_args[n].shape for n in WEIGHTS}

    def two_d(n, a):
        if n in BIG:
            return a.reshape(BIG_SHAPE[n])
        if n in ("lru_wa", "lru_wx"):
            return a.reshape(NH, HD, HD)
        return a.reshape(a.shape[-2:])

    w2 = {n: two_d(n, w_in_args[n]) for n in WEIGHTS}
    m2 = {n: two_d(n, m_args[n]) for n in WEIGHTS}
    v2 = {n: two_d(n, v_args[n]) for n in WEIGHTS}

    cidx = lax.axis_index("c").astype(jnp.int32).reshape(1)
    jchip = 2 * lax.axis_index("x") + lax.axis_index("y")

    jc = jnp.stack([jchip, lax.axis_index("c")]).astype(jnp.int32)

    shards = {n: _place_shard(w2[n], jc, "place_" + n) for n in BIG}
    conv_a_s = jnp.pad(w2["conv_a_w"], ((0, 4), (0, 0)))
    conv_f_s = jnp.pad(w2["conv_f_w"], ((0, 5), (0, 0)))
    small = {n: w2[n] for n in WEIGHTS if n not in BIG and n not in ("conv_a_w", "conv_f_w")}

    grad_x, p_big, q_big, sm_g = _local_step(
        x[0], loss_target[0], shards["w_in"], [shards[n] for n in REST], conv_a_s, conv_f_s, small, jc, cidx)

    s1, s2 = _pack_small(sm_g)
    ps3, qs3 = sm_g["s3"]
    _, (rs1, rs2) = _reduce_stage1({}, (), (s1, s2), "reduce_d2d_in_small")
    ps1, ps2 = _add(s1, rs1, "add_s1"), _add(s2, rs2, "add_s2")
    qs1, qs2 = _reduce_stage2(ps1, ps2)
    f_big = {n: _sum_chips(q_big[n], p_big[n], jc, "sum_chips_" + n) for n in BIG}
    fs1 = _sum_chips(qs1, ps1, jc, "sum_chips_s1")
    fs2 = _sum_chips(qs2, ps2, jc, "sum_chips_s2", by_cols=True)
    fs3 = _sum_chips(qs3, ps3, jc, "sum_chips_s3")
    g_big, gs1, gs2, gs3 = _reduce_stage3(f_big, fs1, fs2, fs3)

    res = {}
    for n in BIG:
        d, mn, vn = _adam(w2[n], g_big[n], m2[n], v2[n], "adam_" + n)
        res[n] = (g_big[n], d, mn, vn)
    small_res = _adam_small(gs1, gs2, gs3.reshape(2, NH, HD, HD), w2, m2, v2)
    for n, r0, nr in VEC_ROWS:
        res[n] = (gs1[r0:r0 + nr],) + small_res[n]
    res["conv_f_b"] = (gs2[0:1],) + small_res["conv_f_b"]
    res["lru_wa"] = (gs3[0:D].reshape(NH, HD, HD),) + small_res["lru_wa"]
    res["lru_wx"] = (gs3[D:2 * D].reshape(NH, HD, HD),) + small_res["lru_wx"]
    g_ca = lax.dynamic_slice_in_dim(gs1[ROW_CONV_A:ROW_CONV_A + 4], jchip * (D // NCHIP), D // NCHIP, axis=1)
    g_cf = lax.dynamic_slice_in_dim(gs2[1:4], jchip * SH_UP, SH_UP, axis=1)
    res["conv_a_w"] = (g_ca,) + _adam(w2["conv_a_w"], g_ca, m2["conv_a_w"], v2["conv_a_w"], "adam_conv_a_w")
    res["conv_f_w"] = (g_cf,) + _adam(w2["conv_f_w"], g_cf, m2["conv_f_w"], v2["conv_f_w"], "adam_conv_f_w")

    loss = (0.5 / D) * jnp.sum(gs1[ROW_LOSS])
    out = [loss, grad_x.reshape(x.shape)]
    for part in range(4):
        out += [res[n][part].reshape(shape_of[n]) for n in WEIGHTS]
    return tuple(out)
```

```python
import jax
import jax.numpy as jnp
from jax import lax
from jax.experimental import pallas as pl
from jax.experimental.pallas import tpu as pltpu

F32 = jnp.float32
BF16 = jnp.bfloat16

D = 1024
NH = 8
HD = 128
CH = 32
DFF = 2816
DUP = 2 * DFF
NCHIP = 4
SH_IN = 2 * D
SH_UP = DUP // NCHIP
SH_DN = DFF // NCHIP
SH_BR = D // NCHIP
EPS = 1e-6
LRU_C = 8.0
ADAM_LR = 0.001
ADAM_B1 = 0.9
ADAM_B2 = 0.999
ADAM_EPS = 1e-08
ADAM_WD = 0.01
ADAM_STEP = 10
VMEM_BIG = 56 * 1024 * 1024
MESH = pl.DeviceIdType.MESH

SLOT_A, SLOT_B, SLOT_C, SLOT_G = 2, 0, 1, 3


def _slot_of_chip(s):
    return jnp.where(s == 3, 3, (s + 2) % 3)


def _params(sem, vmem=None):
    return pltpu.CompilerParams(dimension_semantics=sem, vmem_limit_bytes=vmem)


_GC = 0.7978845608028654
_GA = 0.044715


def _gelu(x):
    return 0.5 * x * (1.0 + jnp.tanh(_GC * (x + _GA * x * x * x)))


def _gelu_and_grad(x):
    x2 = x * x
    th = jnp.tanh(_GC * x * (1.0 + _GA * x2))
    g = 0.5 * x * (1.0 + th)
    dg = 0.5 * (1.0 + th) + 0.5 * x * (1.0 - th * th) * _GC * (1.0 + 3.0 * _GA * x2)
    return g, dg


def _sig(x):
    return jax.nn.sigmoid(x)


def _dot(a, b):
    return jnp.dot(a, b, preferred_element_type=F32)


def _dot_nt(a, b):
    return lax.dot_general(a, b, (((1,), (1,)), ((), ())), preferred_element_type=F32)


def _dot_tn(a, b):
    return lax.dot_general(a, b, (((0,), (0,)), ((), ())), preferred_element_type=F32)


def _chunk_cumsum(x):
    pos = lax.broadcasted_iota(jnp.int32, (x.shape[0], 1), 0) & (CH - 1)
    d = 1
    while d < CH:
        x = x + jnp.where(pos >= d, pltpu.roll(x, d, 0), 0.0)
        d *= 2
    return x


def _chunk_revcumsum(x):
    n = x.shape[0]
    pos = lax.broadcasted_iota(jnp.int32, (n, 1), 0) & (CH - 1)
    d = 1
    while d < CH:
        x = x + jnp.where(pos < CH - d, pltpu.roll(x, n - d, 0), 0.0)
        d *= 2
    return x


def _chunk_last(x):
    n = x.shape[0]
    return jnp.concatenate(
        [jnp.broadcast_to(x[c * CH + CH - 1:c * CH + CH, :], (CH, x.shape[1])) for c in range(n // CH)], axis=0)


def _chunk_total(x):
    n = x.shape[0]
    return jnp.concatenate(
        [jnp.broadcast_to(jnp.sum(x[c * CH:(c + 1) * CH, :], axis=0, keepdims=True), (CH, x.shape[1]))
         for c in range(n // CH)], axis=0)


def _rms_stats(x):
    r = lax.rsqrt(jnp.mean(x * x, axis=-1, keepdims=True) + EPS)
    return r, x * r


def _rms_bwd(gd, n, r):
    return r * (gd - n * jnp.mean(gd * n, axis=-1, keepdims=True))


def _scan_down(a, u, carry):
    n = a.shape[0]
    pos = lax.broadcasted_iota(jnp.int32, (n, 1), 0) & 7
    for d in (1, 2, 4):
        u = a * jnp.where(pos >= d, pltpu.roll(u, d, 0), 0.0) + u
        a = a * jnp.where(pos >= d, pltpu.roll(a, d, 0), 1.0)
    out = []
    for v in range(n // 8):
        h = a[v * 8:v * 8 + 8, :] * carry + u[v * 8:v * 8 + 8, :]
        carry = h[7:8, :]
        out.append(h)
    return jnp.concatenate(out, axis=0)


def _scan_up(b, g, carry):
    n = b.shape[0]
    pos = lax.broadcasted_iota(jnp.int32, (n, 1), 0) & 7
    for d in (1, 2, 4):
        g = g + b * jnp.where(pos < 8 - d, pltpu.roll(g, n - d, 0), 0.0)
        b = b * jnp.where(pos < 8 - d, pltpu.roll(b, n - d, 0), 1.0)
    out = [None] * (n // 8)
    for v in reversed(range(n // 8)):
        h = g[v * 8:v * 8 + 8, :] + b[v * 8:v * 8 + 8, :] * carry
        carry = h[0:1, :]
        out[v] = h
    return jnp.concatenate(out, axis=0)


def _peer_of_step(s):
    return ((s & 1) << 1) | (s >> 1)


def _mm_in_gather(xin, gain, stk_w_in, conv_a_s, conv_f_s, jc, tm):
    m, k = xin.shape
    s_n, _, ns = stk_w_in.shape
    mt = m // tm

    def body(jc_ref, x_ref, g_ref, w_in_any, ca_src, cf_src, o_ref, h_ref, w_full, ca_dst, cf_dst,
             wbuf, hbuf, ssem_w, rsem_w, fssem_w, frsem_w, csend, crecv, lsem, wsem):
        del w_in_any
        s, i = pl.program_id(0), pl.program_id(1)
        x, y, c, j, chips = _place()
        sends_w, arrive_w, fwds_w, farrive_w = _gather_copies([w_full], ("w_in",), ssem_w, rsem_w, fssem_w, frsem_w)
        conv = ((ca_src, ca_dst), (cf_src, cf_dst))
        locs = [pltpu.make_async_copy(src, dst.at[j], lsem.at[n]) for n, (src, dst) in enumerate(conv)]
        csends = [_remote(src, dst.at[j], csend.at[3 * n + kk], crecv.at[3 * n + kk], (cx, cy, c))
                  for n, (src, dst) in enumerate(conv) for kk, (cx, cy) in enumerate(chips)]

        def fetch(step):
            return pltpu.make_async_copy(w_full.at[j ^ _peer_of_step(step)], wbuf.at[step & 1], wsem.at[step & 1])

        @pl.when((s == 0) & (i == 0))
        def _():
            for cp in sends_w[0:2] + locs + csends:
                cp.start()
            fetch(s).start()

        @pl.when(i == 0)
        def _():
            fetch(s).wait()

        @pl.when(s == 0)
        def _():
            _, n = _rms_stats(x_ref[...])
            h = (n * g_ref[...]).astype(BF16)
            hbuf[i] = h
            h_ref[...] = h

        o_ref[...] = _dot(hbuf[i], wbuf[s & 1])

        @pl.when((s == 0) & (i == mt - 1))
        def _():
            for kk in (0, 1):
                arrive_w[kk].wait_recv()
                fwds_w[kk].start()
            sends_w[2].start()

        @pl.when((s == 1) & (i == mt - 1))
        def _():
            arrive_w[2].wait_recv()
            fwds_w[2].start()

        for kk in range(3):
            @pl.when((s == kk) & (i == mt - 1))
            def _(kk=kk):
                farrive_w[kk].wait_recv()
                fetch(s + 1).start()

        @pl.when((s == s_n - 1) & (i == mt - 1))
        def _():
            for n, (_, dst) in enumerate(conv):
                for kk, (cx, cy) in enumerate(chips):
                    got = dst.at[2 * cx + cy]
                    _remote(got, got, csend.at[3 * n + kk], crecv.at[3 * n + kk], (cx, cy, c)).wait_recv()
            for cp in sends_w + fwds_w + csends:
                cp.wait_send()
            for cp in locs:
                cp.wait()

    any_spec = pl.BlockSpec(memory_space=pl.ANY)
    sem = pltpu.SemaphoreType.DMA
    grid_spec = pltpu.PrefetchScalarGridSpec(
        num_scalar_prefetch=1, grid=(s_n, mt),
        in_specs=[pl.BlockSpec((tm, k), lambda s, i, jc_ref: (jnp.where(s == 0, i, mt - 1), 0)),
                  pl.BlockSpec((1, k), lambda s, i, jc_ref: (0, 0)), any_spec, any_spec, any_spec],
        out_specs=(pl.BlockSpec((None, tm, ns),
                                lambda s, i, jc_ref: (_slot_of_chip(jc_ref[0] ^ _peer_of_step(s)), i, 0)),
                   pl.BlockSpec((tm, k), lambda s, i, jc_ref: (jnp.where(s == 0, i, mt - 1), 0)),
                   any_spec, any_spec, any_spec),
        scratch_shapes=[pltpu.VMEM((2, k, ns), BF16), pltpu.VMEM((mt, tm, k), BF16),
                        sem((3,)), sem((3,)), sem((3,)), sem((3,)), sem((6,)), sem((6,)), sem((2,)), sem((2,))])
    return pl.pallas_call(
        body, name="mm_in", grid_spec=grid_spec,
        out_shape=(jax.ShapeDtypeStruct((s_n, m, ns), F32), jax.ShapeDtypeStruct((m, k), BF16),
                   jax.ShapeDtypeStruct(stk_w_in.shape, BF16))
        + tuple(jax.ShapeDtypeStruct((NCHIP,) + v.shape, v.dtype) for v in (conv_a_s, conv_f_s)),
        input_output_aliases={3: 2},
        compiler_params=pltpu.CompilerParams(dimension_semantics=("arbitrary", "arbitrary"),
                                             vmem_limit_bytes=VMEM_BIG, has_side_effects=True),
    )(jc, xin, gain, stk_w_in, conv_a_s, conv_f_s)


def _mm_dh1_exchange(dp4, w_in3, p_w_in, ps3, x, dx1, gain, tm):
    s, k, ns = w_in3.shape
    m = dp4.shape[1]
    mt = m // tm

    def body(a_ref, b_ref, p_ref, p3_ref, x_ref, dx1_ref, g_ref, o_ref, q_ref, q3_ref, dg_ref, ssem, rsem):
        i, j = pl.program_id(0), pl.program_id(1)
        c = lax.axis_index("c")
        sends, arrive = _exchange_copies(
            [q_ref, q3_ref], [lambda chip: p_ref.at[chip], lambda chip: p3_ref.at[pl.ds(c * D, D), :]], ssem, rsem)

        @pl.when((i == 0) & (j == 0))
        def _():
            for cp in sends:
                cp.start()

        @pl.when(j == 0)
        def _():
            o_ref[...] = _dot_nt(a_ref[...], b_ref[...])

        @pl.when(j > 0)
        def _():
            o_ref[...] += _dot_nt(a_ref[...], b_ref[...])

        @pl.when(j == s - 1)
        def _():
            dh = o_ref[...]
            r, n = _rms_stats(x_ref[...])
            o_ref[...] = dx1_ref[...] + _rms_bwd(dh * g_ref[...], n, r)
            dgv = jnp.sum(dh * n, axis=0, keepdims=True)

            @pl.when(i == 0)
            def _():
                dg_ref[...] = dgv

            @pl.when(i > 0)
            def _():
                dg_ref[...] += dgv

        @pl.when((i == mt - 1) & (j == s - 1))
        def _():
            for cp in arrive:
                cp.wait_recv()
            for cp in sends:
                cp.wait_send()

    any_spec = pl.BlockSpec(memory_space=pl.ANY)
    row_tile = pl.BlockSpec((tm, k), lambda i, j: (i, 0))
    vec = pl.BlockSpec((1, k), lambda i, j: (0, 0))
    return pl.pallas_call(
        body, name="mm_dh1",
        out_shape=(jax.ShapeDtypeStruct((m, k), F32), jax.ShapeDtypeStruct(p_w_in.shape, BF16),
                   jax.ShapeDtypeStruct((NCHIP, D, HD), F32), jax.ShapeDtypeStruct((1, k), F32)),
        grid=(mt, s),
        in_specs=[pl.BlockSpec((None, tm, ns), lambda i, j: (_slot_of_chip(j), i, 0)),
                  pl.BlockSpec((None, k, ns), lambda i, j: (j, 0, 0)), any_spec, any_spec, row_tile, row_tile, vec],
        out_specs=(row_tile, any_spec, any_spec, vec),
        scratch_shapes=[pltpu.SemaphoreType.DMA((6,)), pltpu.SemaphoreType.DMA((6,))],
        compiler_params=pltpu.CompilerParams(dimension_semantics=("arbitrary", "arbitrary"),
                                             vmem_limit_bytes=VMEM_BIG, has_side_effects=True),
    )(dp4, w_in3, p_w_in, ps3, x, dx1, gain)


def _mm_dh2_norm(dup, w_up4, dx2, x1, gain, tm):
    s, k, ns = w_up4.shape
    m = dup.shape[0]

    def body(a_ref, b_ref, dx2_ref, x1_ref, g_ref, o_ref, dg_ref):
        i, j = pl.program_id(0), pl.program_id(1)

        @pl.when(j == 0)
        def _():
            o_ref[...] = _dot_nt(a_ref[...], b_ref[...])

        @pl.when(j > 0)
        def _():
            o_ref[...] += _dot_nt(a_ref[...], b_ref[...])

        @pl.when(j == s - 1)
        def _():
            dh = o_ref[...]
            r, n = _rms_stats(x1_ref[...])
            o_ref[...] = dx2_ref[...] + _rms_bwd(dh * g_ref[...], n, r)
            dgv = jnp.sum(dh * n, axis=0, keepdims=True)

            @pl.when(i == 0)
            def _():
                dg_ref[...] = dgv

            @pl.when(i > 0)
            def _():
                dg_ref[...] += dgv

    row_tile = pl.BlockSpec((tm, k), lambda i, j: (i, 0))
    vec = pl.BlockSpec((1, k), lambda i, j: (0, 0))
    return pl.pallas_call(
        body, name="mm_dh2",
        out_shape=(jax.ShapeDtypeStruct((m, k), F32), jax.ShapeDtypeStruct((1, k), F32)),
        grid=(m // tm, s),
        in_specs=[pl.BlockSpec((tm, ns), lambda i, j: (i, j)), pl.BlockSpec((None, k, ns), lambda i, j: (j, 0, 0)),
                  row_tile, row_tile, vec],
        out_specs=(row_tile, vec),
        compiler_params=_params(("arbitrary", "arbitrary"), VMEM_BIG),
    )(dup, w_up4, dx2, x1, gain)


def _mm_nt(a, b, out_dtype, tm, name):
    m, k = a.shape
    n = b.shape[0]

    def body(a_ref, b_ref, o_ref):
        o_ref[...] = _dot_nt(a_ref[...], b_ref[...]).astype(out_dtype)

    return pl.pallas_call(
        body, name=name, out_shape=jax.ShapeDtypeStruct((m, n), out_dtype), grid=(m // tm,),
        in_specs=[pl.BlockSpec((tm, k), lambda i: (i, 0)), pl.BlockSpec((n, k), lambda i: (0, 0))],
        out_specs=pl.BlockSpec((tm, n), lambda i: (i, 0)),
        compiler_params=_params(("parallel",), VMEM_BIG),
    )(a, b)


def _mm_tn(a, g, tkk, tn, tk, name, stacked_slot_fn=None, stacked_out=False):
    m, k = a.shape
    if stacked_slot_fn is not None:
        n = g.shape[0] * g.shape[2]
        g_spec = pl.BlockSpec((None, tk, tn), lambda kk, j, mm: (stacked_slot_fn(j), mm, 0))
    else:
        n = g.shape[1]
        g_spec = pl.BlockSpec((tk, tn), lambda kk, j, mm: (mm, j))
    steps = m // tk

    def body(a_ref, g_ref, o_ref, acc_ref):
        mm = pl.program_id(2)

        @pl.when(mm == 0)
        def _():
            acc_ref[...] = _dot_tn(a_ref[...], g_ref[...])

        @pl.when(mm > 0)
        def _():
            acc_ref[...] += _dot_tn(a_ref[...], g_ref[...])

        @pl.when(mm == steps - 1)
        def _():
            o_ref[...] = acc_ref[...].astype(BF16)

    if stacked_out:
        out_shape = jax.ShapeDtypeStruct((n // tn, k, tn), BF16)
        out_spec = pl.BlockSpec((None, tkk, tn), lambda kk, j, mm: (j, kk, 0))
    else:
        out_shape = jax.ShapeDtypeStruct((k, n), BF16)
        out_spec = pl.BlockSpec((tkk, tn), lambda kk, j, mm: (kk, j))
    return pl.pallas_call(
        body, name=name, out_shape=out_shape, grid=(k // tkk, n // tn, steps),
        in_specs=[pl.BlockSpec((tk, tkk), lambda kk, j, mm: (mm, kk)), g_spec],
        out_specs=out_spec,
        scratch_shapes=[pltpu.VMEM((tkk, tn), F32)],
        compiler_params=_params(("parallel", "parallel", "arbitrary"), VMEM_BIG),
    )(a, g)


def _lru_gates(xc, wa_ref, wx_ref, ba, bx, lam):
    xcb = xc.astype(BF16)
    ra = jnp.concatenate([_dot(xcb[:, n * HD:(n + 1) * HD], wa_ref[n]) for n in range(NH)], axis=1) + ba
    ix = jnp.concatenate([_dot(xcb[:, n * HD:(n + 1) * HD], wx_ref[n]) for n in range(NH)], axis=1) + bx
    r = _sig(ra)
    ig = _sig(ix)
    z = -lam
    sp = jnp.maximum(z, 0.0) + jnp.log1p(jnp.exp(-jnp.abs(z)))
    log_a = -LRU_C * r * sp
    a = jnp.exp(log_a)
    z2 = 2.0 * log_a
    series = -z2 * (1.0 + z2 * (0.5 + z2 * (1.0 / 6.0 + z2 * (1.0 / 24.0))))
    om = jnp.where(z2 > -0.02, series, 1.0 - jnp.exp(z2))
    mult = jnp.sqrt(om)
    return xcb, r, ig, sp, a, mult


def _mixer_a_fwd(p4, cw, cb, wa, wx, ba, bx, lam, stk, names, tt):
    t = p4.shape[1]
    ng = len(stk)

    def body(p_ref, cw_ref, cb_ref, wa_ref, wx_ref, ba_ref, bx_ref, lam_ref, *rest):
        ya_ref, h_ref, sv_ref = rest[ng:ng + 3]
        halo, hc, ssem, rsem = rest[2 * ng + 3:]
        i = pl.program_id(0)
        finish = _ici_leg_behind(rest[ng + 3:2 * ng + 3], names, ssem, rsem, i == 0, lambda: i == t // tt - 1)

        @pl.when(i == 0)
        def _():
            halo[...] = jnp.zeros((8, D), F32)
            hc[...] = jnp.zeros((1, D), F32)

        xa = p_ref[:, 0:D]
        ga = p_ref[:, D:2 * D]
        xe = jnp.concatenate([halo[...], xa], axis=0)
        xc = (cb_ref[...] + cw_ref[3:4, :] * xe
              + sum(cw_ref[3 - s:4 - s, :] * pltpu.roll(xe, s, 0) for s in (1, 2, 3)))[8:, :]
        halo[...] = xa[tt - 8:, :]
        _, r, ig, _, a, mult = _lru_gates(xc, wa_ref, wx_ref, ba_ref[...], bx_ref[...], lam_ref[...])
        u = mult * ig * xc
        h = _scan_down(a, u, hc[...])
        hc[...] = h[tt - 1:tt, :]
        h_ref[...] = h
        ya_ref[...] = (h * _gelu(ga)).astype(BF16)
        for idx, val in enumerate((xc, r, ig, a, mult)):
            sv_ref[idx] = val
        finish()

    full = lambda shape: pl.BlockSpec(shape, lambda i: (0,) * len(shape))
    any_spec = pl.BlockSpec(memory_space=pl.ANY)
    outs = pl.pallas_call(
        body, name="mixer_a_fwd",
        out_shape=(jax.ShapeDtypeStruct((t, D), BF16), jax.ShapeDtypeStruct((t, D), F32),
                   jax.ShapeDtypeStruct((5, t, D), F32)) + tuple(jax.ShapeDtypeStruct(v.shape, v.dtype) for v in stk),
        grid=(t // tt,),
        in_specs=[pl.BlockSpec((None, tt, 2 * D), lambda i: (SLOT_A, i, 0)),
                  full((4, D)), full((1, D)), full((NH, HD, HD)), full((NH, HD, HD)),
                  full((1, D)), full((1, D)), full((1, D))] + [any_spec] * ng,
        out_specs=(pl.BlockSpec((tt, D), lambda i: (i, 0)), pl.BlockSpec((tt, D), lambda i: (i, 0)),
                   pl.BlockSpec((5, tt, D), lambda i: (0, i, 0))) + (any_spec,) * ng,
        scratch_shapes=[pltpu.VMEM((8, D), F32), pltpu.VMEM((1, D), F32),
                        pltpu.SemaphoreType.DMA((3 * ng,)), pltpu.SemaphoreType.DMA((3 * ng,))],
        input_output_aliases={8 + w: 3 + w for w in range(ng)},
        compiler_params=pltpu.CompilerParams(dimension_semantics=("arbitrary",), vmem_limit_bytes=VMEM_BIG,
                                             has_side_effects=True),
    )(p4, cw, cb, wa, wx, ba, bx, lam, *stk)
    return outs[0], outs[1], outs[2], list(outs[3:])


def _chunk_masks(tt):
    row = lax.broadcasted_iota(jnp.int32, (tt, tt), 0)
    col = lax.broadcasted_iota(jnp.int32, (tt, tt), 1)
    same = jnp.right_shift(row, 5) == jnp.right_shift(col, 5)
    return same & (col <= row)


def _hg_head_fwd(q, fz, lbh, saved=None):
    sgn = _sig(-fz)
    k = (1.0 - lbh) * sgn
    if saved is None:
        sg = _sig(fz)
        f = lbh + (1.0 - lbh) * sg
        g = _chunk_cumsum(jnp.log(f))
    else:
        sg, g = saved
        f = lbh + (1.0 - lbh) * sg
    gu = _chunk_last(g) - g
    eg = jnp.exp(g)
    eng = jnp.exp(-g)
    egu = jnp.exp(gu)
    qt = q * eg
    kt = k * eng
    kd = k * egu
    return sg, sgn, f, k, g, eg, eng, egu, qt, kt, kd


def _lb_of(logits_ref):
    return _sig(logits_ref[0:1, :] - logits_ref[1:2, :])


def _hgrn2_fwd(p4, logits, gnorm, stk, names, tt):
    t = p4.shape[1]
    nc = tt // CH
    ng = len(stk)

    def body(p_ref, lg_ref, gn_ref, *rest):
        yb_ref, o_ref, ss_ref, sv_ref = rest[ng:ng + 4]
        st, ssem, rsem = rest[2 * ng + 4:]
        i = pl.program_id(0)
        finish = _ici_leg_behind(rest[ng + 4:2 * ng + 4], names, ssem, rsem, i == 0, lambda: i == t // tt - 1)

        @pl.when(i == 0)
        def _():
            st[...] = jnp.zeros((NH, HD, HD), F32)

        low = _chunk_masks(tt)
        lb = _lb_of(lg_ref)
        heads = [slice(h * HD, (h + 1) * HD) for h in range(NH)]
        sg, _, _, _, g, _, _, _, qt, kt, kd = _hg_head_fwd(p_ref[0, :, 0:D], p_ref[0, :, D:2 * D], lb)
        sv_ref[0] = sg
        sv_ref[1] = g
        qtb, ktb, kdb, vb = qt.astype(BF16), kt.astype(BF16), kd.astype(BF16), p_ref[1, :, 0:D].astype(BF16)
        decs = [jnp.exp(g[c * CH + CH - 1:c * CH + CH, :]) for c in range(nc)]
        o_in = []
        for hs in heads:
            att = jnp.where(low, _dot_nt(qtb[:, hs], ktb[:, hs]), 0.0)
            o_in.append(_dot(att.astype(BF16), vb[:, hs]))
        s_t = [st[h] for h in range(NH)]
        pieces = [[None] * nc for _ in range(NH)]
        for c in range(nc):
            sl = slice(c * CH, (c + 1) * CH)
            for h, hs in enumerate(heads):
                s_bf = s_t[h].astype(BF16)
                ss_ref[c, h] = s_bf
                pieces[h][c] = o_in[h][sl] + _dot_nt(qtb[sl, hs], s_bf)
                s_t[h] = s_t[h] * decs[c][:, hs] + _dot_tn(vb[sl, hs], kdb[sl, hs])
        for h, hs in enumerate(heads):
            st[h] = s_t[h]
            o = jnp.concatenate(pieces[h], axis=0)
            _, n = _rms_stats(o)
            og = p_ref[1, :, D + h * HD:D + (h + 1) * HD]
            o_ref[:, hs] = o
            yb_ref[:, hs] = (n * gn_ref[:, hs] * (og * _sig(og))).astype(BF16)
        finish()

    any_spec = pl.BlockSpec(memory_space=pl.ANY)
    outs = pl.pallas_call(
        body, name="hgrn2_fwd",
        out_shape=(jax.ShapeDtypeStruct((t, D), BF16), jax.ShapeDtypeStruct((t, D), F32),
                   jax.ShapeDtypeStruct((t // CH, NH, HD, HD), BF16), jax.ShapeDtypeStruct((2, t, D), F32))
        + tuple(jax.ShapeDtypeStruct(v.shape, v.dtype) for v in stk),
        grid=(t // tt,),
        in_specs=[pl.BlockSpec((2, tt, 2 * D), lambda i: (0, i, 0)),
                  pl.BlockSpec((2, D), lambda i: (0, 0)), pl.BlockSpec((1, D), lambda i: (0, 0))] + [any_spec] * ng,
        out_specs=(pl.BlockSpec((tt, D), lambda i: (i, 0)), pl.BlockSpec((tt, D), lambda i: (i, 0)),
                   pl.BlockSpec((nc, NH, HD, HD), lambda i: (i, 0, 0, 0)),
                   pl.BlockSpec((2, tt, D), lambda i: (0, i, 0))) + (any_spec,) * ng,
        scratch_shapes=[pltpu.VMEM((NH, HD, HD), F32), pltpu.SemaphoreType.DMA((3 * ng,)),
                        pltpu.SemaphoreType.DMA((3 * ng,))],
        input_output_aliases={3 + w: 4 + w for w in range(ng)},
        compiler_params=pltpu.CompilerParams(dimension_semantics=("arbitrary",), vmem_limit_bytes=VMEM_BIG,
                                             has_side_effects=True),
    )(p4, logits, gnorm, *stk)
    return outs[0], outs[1], outs[2], outs[3], list(outs[4:])


def _mid_fwd(ya, yb, p4, x, wa, wb, wo, g_pm, g_pf, tt):
    t = x.shape[0]

    def body(ya_ref, yb_ref, gt_ref, x_ref, wa_ref, wb_ref, wo_ref, gpm_ref, gpf_ref,
             za_ref, zb_ref, mix_ref, m2_ref, x1_ref, h2_ref):
        za = _dot(ya_ref[...], wa_ref[...])
        zb = _dot(yb_ref[...], wb_ref[...])
        mix = _sig(gt_ref[:, 0:D]) * za + _sig(gt_ref[:, D:2 * D]) * zb
        mixb = mix.astype(BF16)
        m2 = _dot(mixb, wo_ref[...])
        _, n2 = _rms_stats(m2)
        x1 = x_ref[...] + n2 * gpm_ref[...]
        _, n1 = _rms_stats(x1)
        za_ref[...] = za.astype(BF16)
        zb_ref[...] = zb.astype(BF16)
        mix_ref[...] = mixb
        m2_ref[...] = m2
        x1_ref[...] = x1
        h2_ref[...] = (n1 * gpf_ref[...]).astype(BF16)

    row = lambda dt: jax.ShapeDtypeStruct((t, D), dt)
    tile = pl.BlockSpec((tt, D), lambda i: (i, 0))
    wsp = pl.BlockSpec((D, D), lambda i: (0, 0))
    vec = pl.BlockSpec((1, D), lambda i: (0, 0))
    return pl.pallas_call(
        body, name="mid_fwd",
        out_shape=(row(BF16), row(BF16), row(BF16), row(F32), row(F32), row(BF16)),
        grid=(t // tt,),
        in_specs=[tile, tile, pl.BlockSpec((None, tt, 2 * D), lambda i: (SLOT_G, i, 0)), tile,
                  wsp, wsp, wsp, vec, vec],
        out_specs=(tile,) * 6,
        compiler_params=_params(("parallel",), VMEM_BIG),
    )(ya, yb, p4, x, wa, wb, wo, g_pm, g_pf)


def _up_act_fwd(h2, w_up4, cfw, cfb, tm):
    t = h2.shape[0]
    ns = SH_UP

    def body(a_ref, ah_ref, wg_ref, wv_ref, cwg_ref, cwv_ref, cbg_ref, cbv_ref,
             pg_ref, pv_ref, y_ref, uv_ref, gl_ref, dgl_ref):
        i = pl.program_id(1)
        rows = jnp.concatenate([ah_ref[...], a_ref[...]], axis=0)
        ups = []
        for w_ref, cw_ref, cb_ref, pre_ref in ((wg_ref, cwg_ref, cbg_ref, pg_ref), (wv_ref, cwv_ref, cbv_ref, pv_ref)):
            pre = _dot(rows, w_ref[...])
            pre_ref[...] = pre[16:, :].astype(BF16)
            xe = jnp.concatenate([jnp.where(i > 0, pre[8:16, :], 0.0), pre[16:, :]], axis=0)
            up = (cb_ref[...] + cw_ref[2:3, :] * xe + cw_ref[1:2, :] * pltpu.roll(xe, 1, 0)
                  + cw_ref[0:1, :] * pltpu.roll(xe, 2, 0))
            ups.append(up[8:, :])
        gl, dgl = _gelu_and_grad(ups[0])
        y_ref[...] = (gl * ups[1]).astype(BF16)
        uv_ref[...] = ups[1].astype(BF16)
        gl_ref[...] = gl.astype(BF16)
        dgl_ref[...] = dgl.astype(BF16)

    hb = tm // 16
    tile = pl.BlockSpec((tm, ns), lambda p, i: (i, p))
    return pl.pallas_call(
        body, name="up_act_fwd",
        out_shape=(jax.ShapeDtypeStruct((t, DFF), BF16),) * 6,
        grid=(2, t // tm),
        in_specs=[pl.BlockSpec((tm, D), lambda p, i: (i, 0)),
                  pl.BlockSpec((16, D), lambda p, i: (jnp.maximum(i * hb - 1, 0), 0)),
                  pl.BlockSpec((None, D, ns), lambda p, i: (p, 0, 0)),
                  pl.BlockSpec((None, D, ns), lambda p, i: (p + 2, 0, 0)),
                  pl.BlockSpec((3, ns), lambda p, i: (0, p)), pl.BlockSpec((3, ns), lambda p, i: (0, p + 2)),
                  pl.BlockSpec((1, ns), lambda p, i: (0, p)), pl.BlockSpec((1, ns), lambda p, i: (0, p + 2))],
        out_specs=(tile,) * 6,
        compiler_params=_params(("parallel", "parallel"), VMEM_BIG),
    )(h2, h2, w_up4, w_up4, cfw, cfw, cfb, cfb)


def _down_loss(y, wdn, x1, tgt, g_post, tt):
    t = x1.shape[0]

    def body(y_ref, w_ref, x1_ref, t_ref, g_ref, dx2_ref, dm3_ref, lossv_ref, dg_ref):
        i = pl.program_id(0)
        m3 = _dot(y_ref[...], w_ref[...])
        r, n3 = _rms_stats(m3)
        g = g_ref[...]
        e = x1_ref[...] + n3 * g - t_ref[...]
        dx2 = e * (1.0 / D)
        dx2_ref[...] = dx2
        dm3_ref[...] = _rms_bwd(dx2 * g, n3, r).astype(BF16)
        lv = jnp.sum(e * e, axis=0, keepdims=True)
        dgv = jnp.sum(dx2 * n3, axis=0, keepdims=True)

        @pl.when(i == 0)
        def _():
            lossv_ref[...] = lv
            dg_ref[...] = dgv

        @pl.when(i > 0)
        def _():
            lossv_ref[...] += lv
            dg_ref[...] += dgv

    tile = pl.BlockSpec((tt, D), lambda i: (i, 0))
    vec = pl.BlockSpec((1, D), lambda i: (0, 0))
    return pl.pallas_call(
        body, name="down_loss",
        out_shape=(jax.ShapeDtypeStruct((t, D), F32), jax.ShapeDtypeStruct((t, D), BF16),
                   jax.ShapeDtypeStruct((1, D), F32), jax.ShapeDtypeStruct((1, D), F32)),
        grid=(t // tt,),
        in_specs=[pl.BlockSpec((tt, DFF), lambda i: (i, 0)), pl.BlockSpec((DFF, D), lambda i: (0, 0)),
                  tile, tile, vec],
        out_specs=(tile, tile, vec, vec),
        compiler_params=_params(("arbitrary",), VMEM_BIG),
    )(y, wdn, x1, tgt, g_post)


def _ffn_act_bwd(dy, pre_g, pre_v, uv, gl, dgl, cfw, tt):
    t = dy.shape[0]
    nt = t // tt

    def body(dy_ref, dyn_ref, pg_ref, pv_ref, uv_ref, uvn_ref, gl_ref, gln_ref, dgl_ref, dgln_ref, cw_ref,
             du_ref, dcw_ref, dcb_ref):
        i = pl.program_id(0)
        n = tt + 8
        next_live = jnp.where(i < nt - 1, 1.0, 0.0)
        ext = lambda ref, nref: jnp.concatenate([ref[...].astype(F32), nref[...].astype(F32)[0:8, :]], axis=0)
        dy = jnp.concatenate([dy_ref[...].astype(F32), dyn_ref[...].astype(F32)[0:8, :] * next_live], axis=0)
        ds = (dy * ext(uv_ref, uvn_ref) * ext(dgl_ref, dgln_ref), dy * ext(gl_ref, gln_ref))
        dcw_parts, dcb_parts = [], []
        for hh, c0 in enumerate((0, DFF)):
            cs = slice(c0, c0 + DFF)
            dd = ds[hh]
            d1 = pltpu.roll(dd, n - 1, 0)
            d2 = pltpu.roll(dd, n - 2, 0)
            du_ref[:, cs] = (cw_ref[2:3, cs] * dd + cw_ref[1:2, cs] * d1 + cw_ref[0:1, cs] * d2)[0:tt, :].astype(BF16)
            x = (pg_ref, pv_ref)[hh][...].astype(F32)
            dcw_parts.append(jnp.concatenate(
                [jnp.sum(dk[0:tt, :] * x, axis=0, keepdims=True) for dk in (d2, d1, dd)], axis=0))
            dcb_parts.append(jnp.sum(dd[0:tt, :], axis=0, keepdims=True))
        dcw = jnp.concatenate(dcw_parts, axis=1)
        dcb = jnp.concatenate(dcb_parts, axis=1)

        @pl.when(i == 0)
        def _():
            dcw_ref[...] = dcw
            dcb_ref[...] = dcb

        @pl.when(i > 0)
        def _():
            dcw_ref[...] += dcw
            dcb_ref[...] += dcb

    half = pl.BlockSpec((tt, DFF), lambda i: (i, 0))
    half_next = pl.BlockSpec((16, DFF), lambda i: (jnp.minimum((i + 1) * (tt // 16), t // 16 - 1), 0))
    return pl.pallas_call(
        body, name="ffn_act_bwd",
        out_shape=(jax.ShapeDtypeStruct((t, DUP), BF16), jax.ShapeDtypeStruct((3, DUP), F32),
                   jax.ShapeDtypeStruct((1, DUP), F32)),
        grid=(nt,),
        in_specs=[half, half_next, half, half,
                  half, half_next, half, half_next, half, half_next,
                  pl.BlockSpec((3, DUP), lambda i: (0, 0))],
        out_specs=(pl.BlockSpec((tt, DUP), lambda i: (i, 0)), pl.BlockSpec((3, DUP), lambda i: (0, 0)),
                   pl.BlockSpec((1, DUP), lambda i: (0, 0))),
        compiler_params=_params(("arbitrary",), VMEM_BIG),
    )(dy, dy, pre_g, pre_v, uv, uv, gl, gl, dgl, dgl, cfw)


def _mid_bwd(dx1, m2, za, zb, p4, wa, wb, wo, g_pm, tt):
    t = dx1.shape[0]

    def body(dx1_ref, m2_ref, za_ref, zb_ref, gt_ref, wa_ref, wb_ref, wo_ref, gpm_ref,
             dm2_ref, dza_ref, dzb_ref, dya_ref, dyb_ref, dp_ref, dgpm_ref):
        i = pl.program_id(0)
        dx1 = dx1_ref[...]
        r2, n2 = _rms_stats(m2_ref[...])
        dm2 = _rms_bwd(dx1 * gpm_ref[...], n2, r2).astype(BF16)
        dmix = _dot_nt(dm2, wo_ref[...])
        sa = _sig(gt_ref[:, 0:D])
        sb = _sig(gt_ref[:, D:2 * D])
        dza = (dmix * sa).astype(BF16)
        dzb = (dmix * sb).astype(BF16)
        dp_ref[:, 0:D] = (dmix * za_ref[...].astype(F32) * sa * (1.0 - sa)).astype(BF16)
        dp_ref[:, D:2 * D] = (dmix * zb_ref[...].astype(F32) * sb * (1.0 - sb)).astype(BF16)
        dm2_ref[...] = dm2
        dza_ref[...] = dza
        dzb_ref[...] = dzb
        dya_ref[...] = _dot_nt(dza, wa_ref[...]).astype(BF16)
        dyb_ref[...] = _dot_nt(dzb, wb_ref[...]).astype(BF16)
        dgpm = jnp.sum(dx1 * n2, axis=0, keepdims=True)

        @pl.when(i == 0)
        def _():
            dgpm_ref[...] = dgpm

        @pl.when(i > 0)
        def _():
            dgpm_ref[...] += dgpm

    row = lambda dt: jax.ShapeDtypeStruct((t, D), dt)
    tile = pl.BlockSpec((tt, D), lambda i: (i, 0))
    wsp = pl.BlockSpec((D, D), lambda i: (0, 0))
    vec = pl.BlockSpec((1, D), lambda i: (0, 0))
    gates = pl.BlockSpec((None, tt, 2 * D), lambda i: (SLOT_G, i, 0))
    return pl.pallas_call(
        body, name="mid_bwd",
        out_shape=(row(BF16), row(BF16), row(BF16), row(BF16), row(BF16),
                   jax.ShapeDtypeStruct((NCHIP, t, 2 * D), BF16), jax.ShapeDtypeStruct((1, D), F32)),
        grid=(t // tt,),
        in_specs=[tile, tile, tile, tile, gates, wsp, wsp, wsp, vec],
        out_specs=(tile, tile, tile, tile, tile, gates, vec),
        compiler_params=_params(("arbitrary",), VMEM_BIG),
    )(dx1, m2, za, zb, p4, wa, wb, wo, g_pm)


def _hgrn2_bwd(p4, o_all, ss, saved, dyb, dp4, logits, gnorm, p_early, tt):
    t = p4.shape[1]
    nt = t // tt
    nc = tt // CH
    ne = len(p_early)

    def body(p_ref, o_ref, ss_ref, sv_ref, dyb_ref, dp_in, lg_ref, gn_ref, *rest):
        del dp_in
        pe = rest[:ne]
        dp_ref, dlb_ref, dgn_ref = rest[ne:ne + 3]
        qe = rest[ne + 3:2 * ne + 3]
        dst, ssem, rsem = rest[2 * ne + 3:]
        i = pl.program_id(0)
        sends, arrive = _exchange_copies(qe, [(lambda chip, r=r: r.at[chip]) for r in pe], ssem, rsem)

        @pl.when(i == 0)
        def _():
            dst[...] = jnp.zeros((NH, HD, HD), F32)
            for cp in sends:
                cp.start()

        low = _chunk_masks(tt)
        lb = _lb_of(lg_ref)
        heads = [slice(h * HD, (h + 1) * HD) for h in range(NH)]
        sg, sgn, f, k, g, eg, eng, egu, qt, kt, kd = _hg_head_fwd(p_ref[0, :, 0:D], p_ref[0, :, D:2 * D], lb,
                                                                  (sv_ref[0], sv_ref[1]))
        qtb, ktb, kdb, vb = qt.astype(BF16), kt.astype(BF16), kd.astype(BF16), p_ref[1, :, 0:D].astype(BF16)
        decs = [jnp.exp(g[c * CH + CH - 1:c * CH + CH, :]) for c in range(nc)]
        og = p_ref[1, :, D:2 * D]
        so = _sig(og)
        dyb = dyb_ref[...].astype(F32)
        dob = dyb * (og * so)
        rn = [_rms_stats(o_ref[:, hs]) for hs in heads]
        r_all = jnp.concatenate([jnp.broadcast_to(r, (tt, HD)) for r, _ in rn], axis=1)
        n_all = jnp.concatenate([n for _, n in rn], axis=1)
        gd = dob * gn_ref[...]
        proj = jnp.concatenate(
            [jnp.broadcast_to(jnp.mean(gd[:, hs] * n_all[:, hs], axis=-1, keepdims=True), (tt, HD)) for hs in heads],
            axis=1)
        dob_ = (r_all * (gd - n_all * proj)).astype(BF16)
        dog = dyb * (n_all * gn_ref[...]) * (so * (1.0 + og * (1.0 - so)))
        dgn = jnp.sum(dob * n_all, axis=0, keepdims=True)
        dv_in, dqt_in, dkt_h = [], [], []
        for hs in heads:
            att = jnp.where(low, _dot_nt(qtb[:, hs], ktb[:, hs]), 0.0).astype(BF16)
            d_att = jnp.where(low, _dot_nt(dob_[:, hs], vb[:, hs]), 0.0).astype(BF16)
            dv_in.append(_dot_tn(att, dob_[:, hs]))
            dqt_in.append(_dot(d_att, ktb[:, hs]))
            dkt_h.append(_dot_tn(d_att, qtb[:, hs]))
        ds_t = [dst[h] for h in range(NH)]
        dv_p = [[None] * NH for _ in range(nc)]
        dqt_p = [[None] * NH for _ in range(nc)]
        dkd_p = [[None] * NH for _ in range(nc)]
        dgl_p = [[None] * NH for _ in range(nc)]
        for c in reversed(range(nc)):
            sl = slice(c * CH, (c + 1) * CH)
            for h, hs in enumerate(heads):
                s_prev = ss_ref[c, h]
                ds_bf = ds_t[h].astype(BF16)
                dec = decs[c][:, hs]
                dv_p[c][h] = dv_in[h][sl] + _dot_nt(kdb[sl, hs], ds_bf)
                dqt_p[c][h] = dqt_in[h][sl] + _dot(dob_[sl, hs], s_prev)
                dkd_p[c][h] = _dot(vb[sl, hs], ds_bf)
                ddec = jnp.sum(s_prev.astype(F32) * ds_t[h], axis=0, keepdims=True)
                dgl_p[c][h] = jnp.broadcast_to(ddec * dec, (CH, HD))
                ds_t[h] = ds_t[h] * dec + _dot_tn(dob_[sl, hs], qtb[sl, hs])
        for h in range(NH):
            dst[h] = ds_t[h]
        whole = lambda parts: jnp.concatenate([jnp.concatenate(row, axis=1) for row in parts], axis=0)
        dv, dqt, dkd, dgl = whole(dv_p), whole(dqt_p), whole(dkd_p), whole(dgl_p)
        dkt = jnp.concatenate(dkt_h, axis=1)
        dq = dqt * eg
        dk = dkt * eng + dkd * egu
        dg = dqt * qt - dkt * kt
        dgu = dkd * kd
        dlogf = _chunk_revcumsum(dg - dgu) + _chunk_total(dgu) + dgl
        common = sgn * (dlogf / f - dk)
        dfz = (1.0 - lb) * sg * common
        dlb = jnp.sum(common, axis=0, keepdims=True)
        dp_ref[0, :, 0:D] = dq.astype(BF16)
        dp_ref[0, :, D:2 * D] = dfz.astype(BF16)
        dp_ref[1, :, 0:D] = dv.astype(BF16)
        dp_ref[1, :, D:2 * D] = dog.astype(BF16)

        @pl.when(i == 0)
        def _():
            dlb_ref[0:1, :] = dlb
            dgn_ref[...] = dgn

        @pl.when(i > 0)
        def _():
            dlb_ref[0:1, :] += dlb
            dgn_ref[...] += dgn

        @pl.when(i == nt - 1)
        def _():
            d0 = dlb_ref[0:1, :] * lb * (1.0 - lb)
            dlb_ref[0:1, :] = d0
            dlb_ref[1:2, :] = -d0
            for cp in arrive:
                cp.wait_recv()
            for cp in sends:
                cp.wait_send()

    rev = lambda i: nt - 1 - i
    vec = pl.BlockSpec((1, D), lambda i: (0, 0))
    any_spec = pl.BlockSpec(memory_space=pl.ANY)
    outs = pl.pallas_call(
        body, name="hgrn2_bwd",
        out_shape=(jax.ShapeDtypeStruct(dp4.shape, BF16), jax.ShapeDtypeStruct((2, D), F32),
                   jax.ShapeDtypeStruct((1, D), F32)) + tuple(jax.ShapeDtypeStruct(a.shape, BF16) for a in p_early),
        grid=(nt,),
        in_specs=[pl.BlockSpec((2, tt, 2 * D), lambda i: (0, rev(i), 0)),
                  pl.BlockSpec((tt, D), lambda i: (rev(i), 0)),
                  pl.BlockSpec((nc, NH, HD, HD), lambda i: (rev(i), 0, 0, 0)),
                  pl.BlockSpec((2, tt, D), lambda i: (0, rev(i), 0)),
                  pl.BlockSpec((tt, D), lambda i: (rev(i), 0)),
                  any_spec,
                  pl.BlockSpec((2, D), lambda i: (0, 0)), vec] + [any_spec] * ne,
        out_specs=(pl.BlockSpec((2, tt, 2 * D), lambda i: (0, rev(i), 0)),
                   pl.BlockSpec((2, D), lambda i: (0, 0)), vec) + (any_spec,) * ne,
        scratch_shapes=[pltpu.VMEM((NH, HD, HD), F32), pltpu.SemaphoreType.DMA((3 * ne,)),
                        pltpu.SemaphoreType.DMA((3 * ne,))],
        input_output_aliases={5: 0},
        compiler_params=pltpu.CompilerParams(dimension_semantics=("arbitrary",), vmem_limit_bytes=VMEM_BIG,
                                             has_side_effects=True),
    )(p4, o_all, ss, saved, dyb, dp4, logits, gnorm, *p_early)
    return outs[0], outs[1], outs[2], list(outs[3:])


def _mixer_a_bwd(p4, hseq, saved, dya, dp4, cw, wa, wx, lam, tt):
    t = p4.shape[1]
    nt = t // tt

    def body(p_ref, sv_ref, h_ref, hh_ref, dya_ref, dp_in, cw_ref, wa_ref, wx_ref, lam_ref,
             dp_ref, dcw_ref, dcb_ref, dwa_ref, dwx_ref, dba_ref, dbx_ref, dlam_ref,
             dnext, dhc, afc):
        del dp_in
        i = pl.program_id(0)
        first_tile = i == nt - 1

        @pl.when(i == 0)
        def _():
            dnext[...] = jnp.zeros((8, D), F32)
            dhc[...] = jnp.zeros((1, D), F32)
            afc[...] = jnp.zeros((1, D), F32)

        xa = p_ref[:, 0:D]
        ga = p_ref[:, D:2 * D]
        xc, r, ig, a, mult = (sv_ref[idx] for idx in range(5))
        xcb = xc.astype(BF16)
        lam = lam_ref[...]
        sp = jnp.maximum(-lam, 0.0) + jnp.log1p(jnp.exp(-jnp.abs(lam)))
        h = h_ref[...]
        gl, dgl = _gelu_and_grad(ga)
        dya = dya_ref[...].astype(F32)
        dga = dya * h * dgl
        rows = lax.broadcasted_iota(jnp.int32, (tt, 1), 0)
        a_next = jnp.where(rows == tt - 1, afc[...], pltpu.roll(a, tt - 1, 0))
        dh = _scan_up(a_next, dya * gl, dhc[...])
        dhc[...] = dh[0:1, :]
        afc[...] = a[0:1, :]
        h_prev = jnp.where(rows == 0, jnp.where(first_tile, 0.0, hh_ref[7:8, :]), pltpu.roll(h, 1, 0))
        da = dh * h_prev
        dmult = dh * ig * xc
        di = dh * mult * xc
        dlog_a = da * a - dmult * a * a / mult
        dr = dlog_a * (-LRU_C * sp)
        dsp = jnp.sum(dlog_a * (-LRU_C * r), axis=0, keepdims=True)
        dra = dr * r * (1.0 - r)
        dix = di * ig * (1.0 - ig)
        drab = dra.astype(BF16)
        dixb = dix.astype(BF16)
        dxc_lin = []
        dwa_new = []
        dwx_new = []
        for n in range(NH):
            cs = slice(n * HD, (n + 1) * HD)
            dxc_lin.append(_dot_nt(drab[:, cs], wa_ref[n]) + _dot_nt(dixb[:, cs], wx_ref[n]))
            dwa_new.append(_dot_tn(xcb[:, cs], drab[:, cs]))
            dwx_new.append(_dot_tn(xcb[:, cs], dixb[:, cs]))
        dxc = dh * mult * ig + jnp.concatenate(dxc_lin, axis=1)
        de = jnp.concatenate([dxc, dnext[...]], axis=0)
        ups = [de[0:tt, :]] + [pltpu.roll(de, tt + 8 - s, 0)[0:tt, :] for s in (1, 2, 3)]
        dxa = sum(cw_ref[3 - s:4 - s, :] * ups[s] for s in range(4))
        dnext[...] = dxc[0:8, :]
        dp_ref[:, 0:D] = dxa.astype(BF16)
        dp_ref[:, D:2 * D] = dga.astype(BF16)
        dcw = jnp.concatenate(
            [jnp.sum(ups[3 - k] * xa, axis=0, keepdims=True) for k in range(4)], axis=0)
        dcb = jnp.sum(dxc, axis=0, keepdims=True)
        dba = jnp.sum(dra, axis=0, keepdims=True)
        dbx = jnp.sum(dix, axis=0, keepdims=True)
        dlam = dsp * (-_sig(-lam))

        @pl.when(i == 0)
        def _():
            dcw_ref[...] = dcw
            dcb_ref[...] = dcb
            dba_ref[...] = dba
            dbx_ref[...] = dbx
            dlam_ref[...] = dlam
            for n in range(NH):
                dwa_ref[n] = dwa_new[n]
                dwx_ref[n] = dwx_new[n]

        @pl.when(i > 0)
        def _():
            dcw_ref[...] += dcw
            dcb_ref[...] += dcb
            dba_ref[...] += dba
            dbx_ref[...] += dbx
            dlam_ref[...] += dlam
            for n in range(NH):
                dwa_ref[n] += dwa_new[n]
                dwx_ref[n] += dwx_new[n]

    rev = lambda i: nt - 1 - i
    hb = tt // 8
    full = lambda shape: pl.BlockSpec(shape, lambda i: (0,) * len(shape))
    vecs = jax.ShapeDtypeStruct((1, D), F32)
    blk = jax.ShapeDtypeStruct((NH, HD, HD), F32)
    return pl.pallas_call(
        body, name="mixer_a_bwd",
        out_shape=(jax.ShapeDtypeStruct(dp4.shape, BF16), jax.ShapeDtypeStruct((4, D), F32), vecs, blk, blk,
                   vecs, vecs, vecs),
        grid=(nt,),
        in_specs=[pl.BlockSpec((None, tt, 2 * D), lambda i: (SLOT_A, rev(i), 0)),
                  pl.BlockSpec((5, tt, D), lambda i: (0, rev(i), 0)),
                  pl.BlockSpec((tt, D), lambda i: (rev(i), 0)),
                  pl.BlockSpec((8, D), lambda i: (jnp.maximum(rev(i) * hb - 1, 0), 0)),
                  pl.BlockSpec((tt, D), lambda i: (rev(i), 0)),
                  pl.BlockSpec(memory_space=pl.ANY),
                  full((4, D)), full((NH, HD, HD)), full((NH, HD, HD)), full((1, D))],
        out_specs=(pl.BlockSpec((None, tt, 2 * D), lambda i: (SLOT_A, rev(i), 0)),
                   full((4, D)), full((1, D)), full((NH, HD, HD)), full((NH, HD, HD)),
                   full((1, D)), full((1, D)), full((1, D))),
        scratch_shapes=[pltpu.VMEM((8, D), F32), pltpu.VMEM((1, D), F32), pltpu.VMEM((1, D), F32)],
        input_output_aliases={5: 0},
        compiler_params=_params(("arbitrary",), VMEM_BIG),
    )(p4, saved, hseq, hseq, dya, dp4, cw, wa, wx, lam)


def _local_step(x, tgt, stk_w_in, stk_rest, conv_a_s, conv_f_s, small, jc, cidx):
    t = x.shape[0]
    tt = min(256, t)
    tm = min(1024, t)
    tk = min(2048, t)
    wa_bf = small["lru_wa"].astype(BF16)
    wx_bf = small["lru_wx"].astype(BF16)

    p4, h1, w_in, conv_a_g, conv_f_g = _mm_in_gather(x, small["norm_pre_mix"], stk_w_in, conv_a_s, conv_f_s, jc, tm)
    conv_a_w = jnp.transpose(conv_a_g, (1, 0, 2)).reshape(8, D)[0:4]
    conv_f_w = jnp.transpose(conv_f_g, (1, 0, 2)).reshape(8, DUP)[0:3]
    stk = dict(zip(REST, stk_rest))
    ya, hseq, saved_a, got_a = _mixer_a_fwd(p4, conv_a_w, small["conv_a_b"], wa_bf, wx_bf, small["lru_ba"],
                                            small["lru_bx"], small["lru_lambda"],
                                            [stk[n] for n in REST_A], REST_A, tt)
    yb, o_all, ss, saved_b, got_b = _hgrn2_fwd(p4, small["hg_lb_logits"], small["hg_norm_g"],
                                               [stk[n] for n in REST_B], REST_B, tt)
    w = dict(zip(REST_A + REST_B, _gather_forward(got_a + got_b, REST_A + REST_B)))
    w["w_in"] = w_in
    w_br_a = w["w_branch_a"].reshape(D, D)
    w_br_b = w["w_branch_b"].reshape(D, D)
    w_out = w["w_out"].reshape(D, D)
    w_down = w["w_down"].reshape(DFF, D)
    za, zb, mixb, m2, x1, h2 = _mid_fwd(ya, yb, p4, x, w_br_a, w_br_b, w_out, small["norm_post_mix"],
                                        small["norm_pre_ffn"], min(512, t))
    pre_g, pre_v, y, uv, gl, dgl = _up_act_fwd(h2, w["w_up"], conv_f_w, small["conv_f_b"], tt)
    dx2, dm3, lossv, d_norm_post_ffn = _down_loss(y, w_down, x1, tgt, small["norm_post_ffn"], min(512, t))

    d_w_down = _mm_tn(y, dm3, DFF // 2, D, tk, "mm_dw_down")
    dy = _mm_nt(dm3, w_down, BF16, tm, "mm_dy")
    dup_pre, d_conv_f_w, d_conv_f_b = _ffn_act_bwd(dy, pre_g, pre_v, uv, gl, dgl, conv_f_w, tt)
    d_w_up = _mm_tn(h2, dup_pre, D, SH_UP, tk, "mm_dw_up", stacked_out=True)
    dx1, d_norm_pre_ffn = _mm_dh2_norm(dup_pre, w["w_up"], dx2, x1, small["norm_pre_ffn"], tm)
    dm2, dza, dzb, dya, dyb, dp4, d_norm_post_mix = _mid_bwd(
        dx1, m2, za, zb, p4, w_br_a, w_br_b, w_out, small["norm_post_mix"], tt)
    d_w_out = _mm_tn(mixb, dm2, D, D, tm, "mm_dw_out")
    d_w_br_a = _mm_tn(ya, dza, D, D, tm, "mm_dw_bra")
    d_w_br_b = _mm_tn(yb, dzb, D, D, tm, "mm_dw_brb")
    early = {"w_branch_a": d_w_br_a.reshape(NCHIP, SH_BR, D), "w_branch_b": d_w_br_b.reshape(NCHIP, SH_BR, D),
             "w_out": d_w_out.reshape(NCHIP, SH_BR, D), "w_up": d_w_up, "w_down": d_w_down.reshape(NCHIP, SH_DN, D)}
    rb, _ = _reduce_stage1(early, REST, (), "reduce_d2d_in_early")
    p_rest = [_sum_own_half(early[n], rb[n], cidx, "sum_half_" + n) for n in REST]
    dp4, d_lb, d_hg_norm_g, q_rest = _hgrn2_bwd(p4, o_all, ss, saved_b, dyb, dp4, small["hg_lb_logits"], small["hg_norm_g"],
                                                p_rest, tt)
    dp4, d_conv_a_w, d_conv_a_b, d_lru_wa, d_lru_wx, d_lru_ba, d_lru_bx, d_lru_lambda = _mixer_a_bwd(
        p4, hseq, saved_a, dya, dp4, conv_a_w, wa_bf, wx_bf, small["lru_lambda"], tt)
    d_w_in = _mm_tn(h1, dp4, D, SH_IN, tk, "mm_dw_in", stacked_slot_fn=_slot_of_chip, stacked_out=True)
    s3 = jnp.concatenate([d_lru_wa.reshape(D, HD), d_lru_wx.reshape(D, HD)], axis=0)
    rb, (rs3,) = _reduce_stage1({"w_in": d_w_in}, ("w_in",), (s3,), "reduce_d2d_in_w_in")
    p_w_in = _sum_own_half(d_w_in, rb["w_in"], cidx, "sum_half_w_in")
    ps3 = _add(s3, rs3, "add_s3")
    grad_x, q_w_in, qs3, d_norm_pre_mix = _mm_dh1_exchange(dp4, w_in, p_w_in, ps3, x, dx1, small["norm_pre_mix"], tm)

    smalls = {
        "norm_pre_mix": d_norm_pre_mix, "conv_a_b": d_conv_a_b, "lru_ba": d_lru_ba, "lru_bx": d_lru_bx,
        "lru_lambda": d_lru_lambda, "hg_lb_logits": d_lb, "hg_norm_g": d_hg_norm_g, "norm_post_mix": d_norm_post_mix,
        "norm_pre_ffn": d_norm_pre_ffn, "norm_post_ffn": d_norm_post_ffn, "lossv": lossv,
        "conv_a_w": d_conv_a_w, "s3": (ps3, qs3),
        "conv_f_b": d_conv_f_b, "conv_f_w": d_conv_f_w,
    }
    p_big = dict(zip(REST, p_rest), w_in=p_w_in)
    q_big = dict(zip(REST, q_rest), w_in=q_w_in)
    return grad_x, p_big, q_big, smalls


BIG = ("w_in", "w_branch_a", "w_branch_b", "w_out", "w_up", "w_down")
BIG_SHAPE = {"w_in": (D, SH_IN), "w_branch_a": (SH_BR, D), "w_branch_b": (SH_BR, D), "w_out": (SH_BR, D),
             "w_up": (D, SH_UP), "w_down": (SH_DN, D)}
NBIG = len(BIG)
REST = BIG[1:]
REST_A = ("w_branch_a", "w_branch_b", "w_out", "w_down")
REST_B = ("w_up",)
VEC_ROWS = (("norm_pre_mix", 0, 1), ("conv_a_b", 1, 1), ("lru_ba", 2, 1), ("lru_bx", 3, 1), ("lru_lambda", 4, 1),
            ("hg_lb_logits", 5, 2), ("hg_norm_g", 7, 1), ("norm_post_mix", 8, 1), ("norm_pre_ffn", 9, 1),
            ("norm_post_ffn", 10, 1))
ROW_LOSS = 11
ROW_CONV_A = 12
S1_ROWS = 16
S2_ROWS = 8


def _place():
    x, y, c = lax.axis_index("x"), lax.axis_index("y"), lax.axis_index("c")
    chips = [(1 - x, y), (x, 1 - y), (1 - x, 1 - y)]
    return x, y, c, 2 * x + y, chips


def _remote(src, dst, ssem, rsem, dev):
    return pltpu.make_async_remote_copy(src_ref=src, dst_ref=dst, send_sem=ssem, recv_sem=rsem,
                                        device_id=dev, device_id_type=MESH)


def _hbm_call(body, name, ins, out_shapes, n_sems, aliases=None):
    any_spec = pl.BlockSpec(memory_space=pl.ANY)
    return pl.pallas_call(
        body, name=name, out_shape=tuple(out_shapes),
        in_specs=[any_spec] * len(ins), out_specs=tuple([any_spec] * len(out_shapes)),
        scratch_shapes=[pltpu.SemaphoreType.DMA((n,)) for n in n_sems],
        input_output_aliases=aliases or {},
        compiler_params=pltpu.CompilerParams(has_side_effects=True),
    )(*ins)


def _gather_copies(stk, names, ssem, rsem, fssem=None, frsem=None):
    x, y, c, j, chips = _place()
    sends, arrive, fwds, farrive = [], [], [], []
    for w, n in enumerate(names):
        hw = BIG_SHAPE[n][0] // 2
        mine = stk[w].at[j, pl.ds(c * hw, hw), :]
        for k, (cx, cy) in enumerate(chips):
            i = 3 * w + k
            got = stk[w].at[2 * cx + cy, pl.ds(c * hw, hw), :]
            other = stk[w].at[2 * cx + cy, pl.ds((1 - c) * hw, hw), :]
            sends.append(_remote(mine, mine, ssem.at[i], rsem.at[i], (cx, cy, c)))
            arrive.append(_remote(got, got, ssem.at[i], rsem.at[i], (cx, cy, c)))
            if fssem is not None:
                fwds.append(_remote(got, got, fssem.at[i], frsem.at[i], (x, y, 1 - c)))
                farrive.append(_remote(other, other, fssem.at[i], frsem.at[i], (x, y, 1 - c)))
    return sends, arrive, fwds, farrive


def _ici_leg_behind(stk, names, ssem, rsem, first, last_fn):
    sends, arrive, _, _ = _gather_copies(stk, names, ssem, rsem)

    @pl.when(first)
    def _():
        for cp in sends:
            cp.start()

    def finish():
        @pl.when(last_fn())
        def _():
            for cp in arrive:
                cp.wait_recv()
            for cp in sends:
                cp.wait_send()

    return finish


def _gather_forward(stk, names):
    nw = len(names)

    def body(*refs):
        dst = refs[nw:2 * nw]
        ssem, rsem, fssem, frsem = refs[2 * nw:]
        _, _, fwds, farrive = _gather_copies(dst, names, ssem, rsem, fssem, frsem)
        for cp in fwds:
            cp.start()
        for cp in farrive:
            cp.wait_recv()
        for cp in fwds:
            cp.wait_send()

    out_shapes = [jax.ShapeDtypeStruct(a.shape, a.dtype) for a in stk]
    return _hbm_call(body, "gather_forward", stk, out_shapes, (3 * nw,) * 4, aliases={w: w for w in range(nw)})


def _exchange_copies(dst, pieces, ssem, rsem):
    x, y, c, j, chips = _place()
    sends, arrive = [], []
    for w in range(len(dst)):
        for k, (cx, cy) in enumerate(chips):
            i = 3 * w + k
            sends.append(_remote(pieces[w](2 * cx + cy), dst[w].at[j], ssem.at[i], rsem.at[i], (cx, cy, c)))
            got = dst[w].at[2 * cx + cy]
            arrive.append(_remote(got, got, ssem.at[i], rsem.at[i], (cx, cy, c)))
    return sends, arrive


def _reduce_stage1(big_g, names, smalls, name):
    nb = len(names)
    ins = [big_g[n] for n in names] + list(smalls)
    n_in = len(ins)
    halves = [BIG_SHAPE[n][0] // 2 for n in names]
    out_shapes = [jax.ShapeDtypeStruct((NCHIP, halves[w], BIG_SHAPE[n][1]), big_g[n].dtype)
                  for w, n in enumerate(names)]
    out_shapes += [jax.ShapeDtypeStruct(a.shape, F32) for a in smalls]

    def body(*refs):
        src, dst = refs[:n_in], refs[n_in:2 * n_in]
        ssem, rsem = refs[2 * n_in:]
        x, y, c, _, _ = _place()
        cps = []
        for w in range(n_in):
            s_ = src[w].at[:, pl.ds((1 - c) * halves[w], halves[w]), :] if w < nb else src[w]
            cp = _remote(s_, dst[w], ssem.at[w], rsem.at[w], (x, y, 1 - c))
            cp.start()
            cps.append(cp)
        for cp in cps:
            cp.wait()

    outs = _hbm_call(body, name, ins, out_shapes, (n_in, n_in))
    return dict(zip(names, outs[:nb])), outs[nb:]


def _reduce_stage2(ps1, ps2):
    ins = [ps1, ps2]
    h1, h2 = S1_ROWS // 2, DUP // 2
    out_shapes = [jax.ShapeDtypeStruct((NCHIP, h1, D), F32), jax.ShapeDtypeStruct((NCHIP, S2_ROWS, h2), F32)]

    def body(*refs):
        src, dst = refs[:2], refs[2:4]
        ssem, rsem = refs[4:]
        c = lax.axis_index("c")
        pieces = [lambda chip: src[0].at[pl.ds(c * h1, h1), :],
                  lambda chip: src[1].at[:, pl.ds(c * h2, h2)]]
        sends, arrive = _exchange_copies(dst, pieces, ssem, rsem)
        for cp in sends:
            cp.start()
        for cp in arrive:
            cp.wait_recv()
        for cp in sends:
            cp.wait_send()

    return _hbm_call(body, "reduce_ici_small", ins, out_shapes, (6, 6))


def _reduce_stage3(f_big, fs1, fs2, fs3):
    ins = [f_big[n] for n in BIG] + [fs1, fs2, fs3]
    n_in = len(ins)
    halves = [BIG_SHAPE[n][0] // 2 for n in BIG]
    h1, h2, h3 = S1_ROWS // 2, DUP // 2, D
    out_shapes = [jax.ShapeDtypeStruct(BIG_SHAPE[n], F32) for n in BIG]
    out_shapes += [jax.ShapeDtypeStruct((S1_ROWS, D), F32), jax.ShapeDtypeStruct((S2_ROWS, DUP), F32),
                   jax.ShapeDtypeStruct((2 * D, HD), F32)]

    def body(*refs):
        dst = refs[n_in:2 * n_in]
        ssem, rsem = refs[2 * n_in:]
        x, y, c, _, _ = _place()

        def place(w, which):
            if w < NBIG:
                return dst[w].at[pl.ds(which * halves[w], halves[w]), :]
            if w == NBIG:
                return dst[w].at[pl.ds(which * h1, h1), :]
            if w == NBIG + 1:
                return dst[w].at[:, pl.ds(which * h2, h2)]
            return dst[w].at[pl.ds(which * h3, h3), :]

        cps = [_remote(place(w, c), place(w, c), ssem.at[w], rsem.at[w], (x, y, 1 - c)) for w in range(n_in)]
        for cp in cps:
            cp.start()
        for w in range(n_in):
            got = place(w, 1 - c)
            _remote(got, got, ssem.at[w], rsem.at[w], (x, y, 1 - c)).wait_recv()
        for cp in cps:
            cp.wait_send()

    outs = _hbm_call(body, "reduce_d2d_out", ins, out_shapes, (n_in, n_in), aliases={w: w for w in range(n_in)})
    return dict(zip(BIG, outs[:NBIG])), outs[NBIG], outs[NBIG + 1], outs[NBIG + 2]


def _row_tile(rows):
    for tr in (256, 352, 128, 176, 64, 16, 8):
        if rows % tr == 0:
            return tr
    return rows


def _sum_own_half(g, rb, cidx, name):
    s, rows, cols = g.shape
    half = rows // 2
    tr = half
    nb = half // tr

    def body(c_ref, g_ref, r_ref, o_ref):
        del c_ref
        o_ref[...] = (g_ref[...].astype(F32) + r_ref[...].astype(F32)).astype(BF16)

    grid_spec = pltpu.PrefetchScalarGridSpec(
        num_scalar_prefetch=1, grid=(s, nb),
        in_specs=[pl.BlockSpec((None, tr, cols), lambda k, i, c: (k, c[0] * nb + i, 0)),
                  pl.BlockSpec((None, tr, cols), lambda k, i, c: (k, i, 0))],
        out_specs=pl.BlockSpec((None, tr, cols), lambda k, i, c: (k, i, 0)))
    return pl.pallas_call(
        body, name=name, grid_spec=grid_spec, out_shape=jax.ShapeDtypeStruct((s, half, cols), BF16),
        compiler_params=_params(("parallel", "parallel")),
    )(cidx, g, rb)


def _sum_chips(q, p, jc, name, by_cols=False):
    s, rows, cols = q.shape
    tr = _row_tile(rows)
    nb = rows // tr
    stacked = p.ndim == 3

    def body(jc_ref, q_ref, p_ref, o_ref):
        j = jc_ref[0]
        own = p_ref[...].astype(F32)
        acc = None
        for k in range(NCHIP):
            term = jnp.where(j == k, own, q_ref[k].astype(F32))
            acc = term if acc is None else acc + term
        o_ref[...] = acc

    if by_cols:
        half_spec = pl.BlockSpec((tr, cols), lambda i, jc_ref: (i, jc_ref[1]))
        out_shape = jax.ShapeDtypeStruct((rows, 2 * cols), F32)
    else:
        half_spec = pl.BlockSpec((tr, cols), lambda i, jc_ref: (jc_ref[1] * nb + i, 0))
        out_shape = jax.ShapeDtypeStruct((2 * rows, cols), F32)
    p_spec = pl.BlockSpec((None, tr, cols), lambda i, jc_ref: (jc_ref[0], i, 0)) if stacked else half_spec
    grid_spec = pltpu.PrefetchScalarGridSpec(
        num_scalar_prefetch=1, grid=(nb,),
        in_specs=[pl.BlockSpec((s, tr, cols), lambda i, jc_ref: (0, i, 0)), p_spec],
        out_specs=half_spec)
    return pl.pallas_call(
        body, name=name, grid_spec=grid_spec, out_shape=out_shape,
        compiler_params=_params(("parallel",), VMEM_BIG),
    )(jc, q, p)


def _place_shard(w, jc, name):
    rows, cols = w.shape
    tr = rows // 2

    def body(jc_ref, w_ref, o_ref):
        del jc_ref
        o_ref[...] = w_ref[...].astype(BF16)

    grid_spec = pltpu.PrefetchScalarGridSpec(
        num_scalar_prefetch=1, grid=(rows // tr,),
        in_specs=[pl.BlockSpec((tr, cols), lambda i, jc_ref: (i, 0))],
        out_specs=pl.BlockSpec((None, tr, cols), lambda i, jc_ref: (jc_ref[0], i, 0)))
    return pl.pallas_call(
        body, name=name, grid_spec=grid_spec, out_shape=jax.ShapeDtypeStruct((NCHIP, rows, cols), BF16),
        compiler_params=_params(("parallel",)),
    )(jc, w)


def _add(a, b, name):
    def body(a_ref, b_ref, o_ref):
        o_ref[...] = a_ref[...] + b_ref[...]

    return pl.pallas_call(body, name=name, out_shape=jax.ShapeDtypeStruct(a.shape, F32))(a, b)


def _pack_small(sm):
    vec_in = [sm[n] for n, _, _ in VEC_ROWS]
    nv = len(vec_in)

    def body(*refs):
        ins, lossv, dcw, dcfb, dcfw, s1, s2 = refs[:nv], refs[nv], refs[nv + 1], refs[nv + 2], refs[nv + 3], \
            refs[nv + 4], refs[nv + 5]
        for ref, (_, r0, nr) in zip(ins, VEC_ROWS):
            s1[r0:r0 + nr, :] = ref[...]
        s1[ROW_LOSS:ROW_LOSS + 1, :] = lossv[...]
        s1[ROW_CONV_A:ROW_CONV_A + 4, :] = dcw[...]
        s2[0:1, :] = dcfb[...]
        s2[1:4, :] = dcfw[...]
        s2[4:8, :] = jnp.zeros((4, DUP), F32)

    return pl.pallas_call(
        body, name="pack_small",
        out_shape=(jax.ShapeDtypeStruct((S1_ROWS, D), F32), jax.ShapeDtypeStruct((S2_ROWS, DUP), F32)),
    )(*vec_in, sm["lossv"], sm["conv_a_w"], sm["conv_f_b"], sm["conv_f_w"])


def _adam_math(w, g, m, v):
    m = ADAM_B1 * m + (1.0 - ADAM_B1) * g
    v = ADAM_B2 * v + (1.0 - ADAM_B2) * (g * g)
    m_hat = m / (1.0 - ADAM_B1 ** ADAM_STEP)
    v_hat = v / (1.0 - ADAM_B2 ** ADAM_STEP)
    delta = -ADAM_LR * (m_hat / (jnp.sqrt(v_hat) + ADAM_EPS) + ADAM_WD * w)
    return delta, m, v


def _adam(w, g, m, v, name):
    rows, cols = w.shape
    tr = _row_tile(rows)

    def body(w_ref, g_ref, m_ref, v_ref, d_ref, mo_ref, vo_ref):
        d_ref[...], mo_ref[...], vo_ref[...] = _adam_math(w_ref[...], g_ref[...], m_ref[...], v_ref[...])

    spec = pl.BlockSpec((tr, cols), lambda i: (i, 0))
    return pl.pallas_call(
        body, name=name, out_shape=(jax.ShapeDtypeStruct(w.shape, F32),) * 3, grid=(rows // tr,),
        in_specs=[spec] * 4, out_specs=(spec,) * 3,
        compiler_params=_params(("parallel",), VMEM_BIG),
    )(w, g, m, v)


def _adam_small(gs1, gs2, gs3, w, m, v):
    names = [n for n, _, _ in VEC_ROWS] + ["conv_f_b", "lru_wa", "lru_wx"]
    nn = len(names)

    def grad_of(i, g1, g2, g3):
        if i < len(VEC_ROWS):
            _, r0, nr = VEC_ROWS[i]
            return g1[r0:r0 + nr, :]
        if names[i] == "conv_f_b":
            return g2[0:1, :]
        return g3[0] if names[i] == "lru_wa" else g3[1]

    def body(*refs):
        g1, g2, g3 = refs[0], refs[1], refs[2]
        ws, ms, vs = refs[3:3 + nn], refs[3 + nn:3 + 2 * nn], refs[3 + 2 * nn:3 + 3 * nn]
        outs = refs[3 + 3 * nn:]
        for i in range(nn):
            d, mn, vn = _adam_math(ws[i][...], grad_of(i, g1, g2, g3), ms[i][...], vs[i][...])
            outs[i][...] = d
            outs[nn + i][...] = mn
            outs[2 * nn + i][...] = vn

    shapes = [jax.ShapeDtypeStruct(w[n].shape, F32) for n in names]
    outs = pl.pallas_call(body, name="adam_small", out_shape=tuple(shapes * 3))(
        gs1, gs2, gs3, *[w[n] for n in names], *[m[n] for n in names], *[v[n] for n in names])
    return {n: (outs[i], outs[nn + i], outs[2 * nn + i]) for i, n in enumerate(names)}


WEIGHTS = ("norm_pre_mix", "w_in", "conv_a_w", "conv_a_b", "lru_wa", "lru_ba", "lru_wx", "lru_bx", "lru_lambda",
           "hg_lb_logits", "hg_norm_g", "w_branch_a", "w_branch_b", "w_out", "norm_post_mix", "norm_pre_ffn",
           "w_up", "conv_f_w", "conv_f_b", "w_down", "norm_post_ffn")
NW = len(WEIGHTS)


def kernel(x, norm_pre_mix, w_in, conv_a_w, conv_a_b, lru_wa, lru_ba, lru_wx, lru_bx, lru_lambda, hg_lb_logits, hg_norm_g, w_branch_a, w_branch_b, w_out, norm_post_mix, norm_pre_ffn, w_up, conv_f_w, conv_f_b, w_down, norm_post_ffn, loss_target, m_norm_pre_mix, m_w_in, m_conv_a_w, m_conv_a_b, m_lru_wa, m_lru_ba, m_lru_wx, m_lru_bx, m_lru_lambda, m_hg_lb_logits, m_hg_norm_g, m_w_branch_a, m_w_branch_b, m_w_out, m_norm_post_mix, m_norm_pre_ffn, m_w_up, m_conv_f_w, m_conv_f_b, m_w_down, m_norm_post_ffn, v_norm_pre_mix, v_w_in, v_conv_a_w, v_conv_a_b, v_lru_wa, v_lru_ba, v_lru_wx, v_lru_bx, v_lru_lambda, v_hg_lb_logits, v_hg_norm_g, v_w_branch_a, v_w_branch_b, v_w_out, v_norm_post_mix, v_norm_pre_ffn, v_w_up, v_conv_f_w, v_conv_f_b, v_w_down, v_norm_post_ffn):
    rest = (norm_pre_mix, w_in, conv_a_w, conv_a_b, lru_wa, lru_ba, lru_wx, lru_bx, lru_lambda, hg_lb_logits, hg_norm_g, w_branch_a, w_branch_b, w_out, norm_post_mix, norm_pre_ffn, w_up, conv_f_w, conv_f_b, w_down, norm_post_ffn, loss_target, m_norm_pre_mix, m_w_in, m_conv_a_w, m_conv_a_b, m_lru_wa, m_lru_ba, m_lru_wx, m_lru_bx, m_lru_lambda, m_hg_lb_logits, m_hg_norm_g, m_w_branch_a, m_w_branch_b, m_w_out, m_norm_post_mix, m_norm_pre_ffn, m_w_up, m_conv_f_w, m_conv_f_b, m_w_down, m_norm_post_ffn, v_norm_pre_mix, v_w_in, v_conv_a_w, v_conv_a_b, v_lru_wa, v_lru_ba, v_lru_wx, v_lru_bx, v_lru_lambda, v_hg_lb_logits, v_hg_norm_g, v_w_branch_a, v_w_branch_b, v_w_out, v_norm_post_mix, v_norm_pre_ffn, v_w_up, v_conv_f_w, v_conv_f_b, v_w_down, v_norm_post_ffn)
    w_in_args = dict(zip(WEIGHTS, rest[:NW]))
    loss_target = rest[NW]
    m_args = dict(zip(WEIGHTS, rest[NW + 1:2 * NW + 1]))
    v_args = dict(zip(WEIGHTS, rest[2 * NW + 1:3 * NW + 1]))
    shape_of = {n: w_in_args[n].shape for n in WEIGHTS}

    def two_d(n, a):
        if n in BIG:
            return a.reshape(BIG_SHAPE[n])
        if n in ("lru_wa", "lru_wx"):
            return a.reshape(NH, HD, HD)
        return a.reshape(a.shape[-2:])

    w2 = {n: two_d(n, w_in_args[n]) for n in WEIGHTS}
    m2 = {n: two_d(n, m_args[n]) for n in WEIGHTS}
    v2 = {n: two_d(n, v_args[n]) for n in WEIGHTS}

    cidx = lax.axis_index("c").astype(jnp.int32).reshape(1)
    jchip = 2 * lax.axis_index("x") + lax.axis_index("y")

    jc = jnp.stack([jchip, lax.axis_index("c")]).astype(jnp.int32)

    shards = {n: _place_shard(w2[n], jc, "place_" + n) for n in BIG}
    conv_a_s = jnp.pad(w2["conv_a_w"], ((0, 4), (0, 0)))
    conv_f_s = jnp.pad(w2["conv_f_w"], ((0, 5), (0, 0)))
    small = {n: w2[n] for n in WEIGHTS if n not in BIG and n not in ("conv_a_w", "conv_f_w")}

    grad_x, p_big, q_big, sm_g = _local_step(
        x[0], loss_target[0], shards["w_in"], [shards[n] for n in REST], conv_a_s, conv_f_s, small, jc, cidx)

    s1, s2 = _pack_small(sm_g)
    ps3, qs3 = sm_g["s3"]
    _, (rs1, rs2) = _reduce_stage1({}, (), (s1, s2), "reduce_d2d_in_small")
    ps1, ps2 = _add(s1, rs1, "add_s1"), _add(s2, rs2, "add_s2")
    qs1, qs2 = _reduce_stage2(ps1, ps2)
    f_big = {n: _sum_chips(q_big[n], p_big[n], jc, "sum_chips_" + n) for n in BIG}
    fs1 = _sum_chips(qs1, ps1, jc, "sum_chips_s1")
    fs2 = _sum_chips(qs2, ps2, jc, "sum_chips_s2", by_cols=True)
    fs3 = _sum_chips(qs3, ps3, jc, "sum_chips_s3")
    g_big, gs1, gs2, gs3 = _reduce_stage3(f_big, fs1, fs2, fs3)

    res = {}
    for n in BIG:
        d, mn, vn = _adam(w2[n], g_big[n], m2[n], v2[n], "adam_" + n)
        res[n] = (g_big[n], d, mn, vn)
    small_res = _adam_small(gs1, gs2, gs3.reshape(2, NH, HD, HD), w2, m2, v2)
    for n, r0, nr in VEC_ROWS:
        res[n] = (gs1[r0:r0 + nr],) + small_res[n]
    res["conv_f_b"] = (gs2[0:1],) + small_res["conv_f_b"]
    res["lru_wa"] = (gs3[0:D].reshape(NH, HD, HD),) + small_res["lru_wa"]
    res["lru_wx"] = (gs3[D:2 * D].reshape(NH, HD, HD),) + small_res["lru_wx"]
    g_ca = lax.dynamic_slice_in_dim(gs1[ROW_CONV_A:ROW_CONV_A + 4], jchip * (D // NCHIP), D // NCHIP, axis=1)
    g_cf = lax.dynamic_slice_in_dim(gs2[1:4], jchip * SH_UP, SH_UP, axis=1)
    res["conv_a_w"] = (g_ca,) + _adam(w2["conv_a_w"], g_ca, m2["conv_a_w"], v2["conv_a_w"], "adam_conv_a_w")
    res["conv_f_w"] = (g_cf,) + _adam(w2["conv_f_w"], g_cf, m2["conv_f_w"], v2["conv_f_w"], "adam_conv_f_w")

    loss = (0.5 / D) * jnp.sum(gs1[ROW_LOSS])
    out = [loss, grad_x.reshape(x.shape)]
    for part in range(4):
        out += [res[n][part].reshape(shape_of[n]) for n in WEIGHTS]
    return tuple(out)
```
